```python
import jax, jax.numpy as jnp
from jax import lax
import numpy as np

D_MODEL = 1024
BATCH = 8
SEQ = 2048
DEPTH = 2

N_HEADS = 16
HEAD_DIM = D_MODEL // N_HEADS
N_MIXERS = 2
GRID_W = 64
NA_ROWS = 8
NA_COLS = 16
DIL_GROUPS = ((128, 1), (512, 4), (2048, 16))
N_GROUPS = len(DIL_GROUPS)
BAND_BLOCK = 128
D_FF = -(-8 * D_MODEL // (3 * 256)) * 256
RMS_EPS = 1e-6
NEG_INF = -1e30
N_A_LAYERS = (DEPTH + 1) // 2
N_B_LAYERS = DEPTH // 2

kernel_name = "hybrid_natten_dilated_encoder"


def rms_norm(x, g):
    xf = x.astype(jnp.float32)
    y = xf * lax.rsqrt(jnp.mean(xf * xf, axis=-1, keepdims=True) + RMS_EPS)
    return (y * g.astype(jnp.float32)).astype(x.dtype)


def alibi_slopes(n):
    return 2.0 ** (-8.0 * jnp.arange(1, n + 1, dtype=jnp.float32) / n)


def swiglu(x, w_gate, w_up, w_down):
    return (jax.nn.silu(x @ w_gate) * (x @ w_up)) @ w_down


def neighbourhood_attention(x, w_qkv, w_o, rpb):
    b, s, _ = x.shape
    rows = s // GRID_W
    kh = min(NA_ROWS, rows)
    qkv = (x @ w_qkv).reshape(b, rows, GRID_W, 3, N_HEADS, HEAD_DIM)
    q, k, v = (jnp.transpose(qkv[:, :, :, i], (0, 3, 1, 2, 4)) for i in range(3))
    q = q * HEAD_DIM ** -0.5
    col = jnp.arange(GRID_W)
    col_start = jnp.clip(col - NA_COLS // 2, 0, GRID_W - NA_COLS)
    col_mask = (col[None, :] >= col_start[:, None]) & (col[None, :] < col_start[:, None] + NA_COLS)
    col_idx = jnp.clip(col[None, :] - col[:, None] + NA_COLS - 1, 0, 2 * NA_COLS - 2)
    rpb_cols = rpb.astype(jnp.float32)[:, :, col_idx]

    def row_block(i):
        rs = jnp.clip(i - kh // 2, 0, rows - kh)
        qi = lax.dynamic_index_in_dim(q, i, axis=2, keepdims=False)
        kr = lax.dynamic_slice_in_dim(k, rs, kh, axis=2)
        vr = lax.dynamic_slice_in_dim(v, rs, kh, axis=2)
        bias = lax.dynamic_slice_in_dim(rpb_cols, rs - i + NA_ROWS - 1, kh, axis=1)
        sc = jnp.einsum('bhqd,bhrkd->bhqrk', qi, kr).astype(jnp.float32)
        sc = sc + jnp.transpose(bias, (0, 2, 1, 3))[None]
        sc = jnp.where(col_mask[:, None, :], sc, NEG_INF)
        p = jax.nn.softmax(sc.reshape(b, N_HEADS, GRID_W, kh * GRID_W), axis=-1).reshape(sc.shape)
        return jnp.einsum('bhqrk,bhrkd->bhqd', p.astype(vr.dtype), vr)

    o = lax.map(row_block, jnp.arange(rows))
    o = jnp.transpose(o, (1, 0, 3, 2, 4)).reshape(b, s, D_MODEL)
    return o @ w_o


def banded_attention(q, k, v, radius, slope_dist):
    n, h, l, dh = q.shape
    nb = -(-l // BAND_BLOCK)
    lp = nb * BAND_BLOCK
    qb = jnp.pad(q, ((0, 0), (0, 0), (0, lp - l), (0, 0))).reshape(n, h, nb, BAND_BLOCK, dh)

    def key_blocks(t):
        tp = jnp.pad(t, ((0, 0), (0, 0), (radius, lp - l + BAND_BLOCK - radius), (0, 0)))
        tp = tp.reshape(n, h, nb + 1, BAND_BLOCK, dh)
        return jnp.concatenate([tp[:, :, :-1], tp[:, :, 1:]], axis=3)

    kb, vb = key_blocks(k), key_blocks(v)
    qi = jnp.arange(lp).reshape(nb, BAND_BLOCK)
    kj = jnp.arange(nb)[:, None] * BAND_BLOCK - radius + jnp.arange(2 * BAND_BLOCK)[None, :]
    dist = jnp.abs(qi[:, :, None] - kj[:, None, :])
    valid = (dist <= radius) & (kj[:, None, :] >= 0) & (kj[:, None, :] < l)
    sc = jnp.einsum('nhbqd,nhbkd->nhbqk', qb, kb).astype(jnp.float32)
    sc = sc - slope_dist[None, :, None, None, None] * dist.astype(jnp.float32)
    sc = jnp.where(valid, sc, NEG_INF)
    lse = jax.nn.logsumexp(sc, axis=-1)
    p = jnp.exp(sc - lse[..., None])
    o = jnp.einsum('nhbqk,nhbkd->nhbqd', p.astype(vb.dtype), vb)
    return o.reshape(n, h, lp, dh)[:, :, :l], lse.reshape(n, h, lp)[:, :, :l]


def dilated_attention(x, w_qkv, w_o):
    b, s, _ = x.shape
    qkv = (x @ w_qkv).reshape(b, s, N_GROUPS, 3, N_HEADS, HEAD_DIM)
    slopes = alibi_slopes(N_HEADS)
    outs, lses = [], []
    for g, (window, dil) in enumerate(DIL_GROUPS):
        radius = window // (2 * dil)
        l = s // dil

        def to_sub(t):
            t = jnp.transpose(t.reshape(b, l, dil, N_HEADS, HEAD_DIM), (0, 2, 3, 1, 4))
            return t.reshape(b * dil, N_HEADS, l, HEAD_DIM)

        q, k, v = (to_sub(qkv[:, :, g, i]) for i in range(3))
        o, lse = banded_attention(q * HEAD_DIM ** -0.5, k, v, radius, slopes * dil)
        o = jnp.transpose(o.reshape(b, dil, N_HEADS, l, HEAD_DIM), (0, 3, 1, 2, 4)).reshape(b, s, N_HEADS, HEAD_DIM)
        lse = jnp.transpose(lse.reshape(b, dil, N_HEADS, l), (0, 3, 1, 2)).reshape(b, s, N_HEADS)
        outs.append(o)
        lses.append(lse)
    alpha = jax.nn.softmax(jnp.stack(lses, axis=0), axis=0)
    o = jnp.einsum('gbsh,gbshd->bshd', alpha, jnp.stack(outs, axis=0).astype(jnp.float32)).astype(x.dtype)
    return o.reshape(b, s, D_MODEL) @ w_o


def _fwd_setup_inputs(seed: int = 0) -> dict:
    key = jax.random.key(seed)
    ks = jax.random.split(key, 14)
    d = D_MODEL

    def w(k, shape, fan_in):
        return jax.random.normal(k, shape, jnp.float32) * fan_in ** -0.5

    def gain(k):
        return 1.0 + 0.02 * jax.random.normal(k, (DEPTH, d), jnp.float32)

    return {
        "x": jax.random.normal(ks[0], (BATCH, SEQ, d), jnp.float32),
        "norm_mix_pre": gain(ks[1]),
        "norm_mix_post": gain(ks[2]),
        "norm_ffn_pre": gain(ks[3]),
        "norm_ffn_post": gain(ks[4]),
        "na_w_qkv": w(ks[5], (N_A_LAYERS, d, 3 * d), d),
        "na_w_o": w(ks[6], (N_A_LAYERS, d, d), d),
        "na_rpb": 0.5 * jax.random.normal(ks[7], (N_A_LAYERS, N_HEADS, 2 * NA_ROWS - 1, 2 * NA_COLS - 1), jnp.float32),
        "dil_w_qkv": w(ks[8], (N_B_LAYERS, d, N_GROUPS * 3 * d), d),
        "dil_w_o": w(ks[9], (N_B_LAYERS, d, d), d),
        "ffn_w_gate": w(ks[10], (DEPTH, d, D_FF), d),
        "ffn_w_up": w(ks[11], (DEPTH, d, D_FF), d),
        "ffn_w_down": w(ks[12], (DEPTH, D_FF, d), D_FF),
    }


def _fwd_reference(x, norm_mix_pre, norm_mix_post, norm_ffn_pre, norm_ffn_post, na_w_qkv, na_w_o, na_rpb,
              dil_w_qkv, dil_w_o, ffn_w_gate, ffn_w_up, ffn_w_down):
    for layer in range(DEPTH):
        j = layer // N_MIXERS
        h = rms_norm(x, norm_mix_pre[layer])
        if layer % N_MIXERS == 0:
            h = neighbourhood_attention(h, na_w_qkv[j], na_w_o[j], na_rpb[j])
        else:
            h = dilated_attention(h, dil_w_qkv[j], dil_w_o[j])
        x = x + rms_norm(h, norm_mix_post[layer])
        h = rms_norm(x, norm_ffn_pre[layer])
        x = x + rms_norm(swiglu(h, ffn_w_gate[layer], ffn_w_up[layer], ffn_w_down[layer]), norm_ffn_post[layer])
    return x


import jax as _jax
import jax.numpy as _jnp

TWIN_FORMAT = 'train_step'
FWD_PARAMS = ['x', 'norm_mix_pre', 'norm_mix_post', 'norm_ffn_pre', 'norm_ffn_post', 'na_w_qkv', 'na_w_o', 'na_rpb', 'dil_w_qkv', 'dil_w_o', 'ffn_w_gate', 'ffn_w_up', 'ffn_w_down']
TWIN_WEIGHTS = ['norm_mix_pre', 'norm_mix_post', 'norm_ffn_pre', 'norm_ffn_post', 'na_w_qkv', 'na_w_o', 'na_rpb', 'dil_w_qkv', 'dil_w_o', 'ffn_w_gate', 'ffn_w_up', 'ffn_w_down']
TWIN_DIFF_INPUT = 'x'
TWIN_INPUTS = ['x', 'norm_mix_pre', 'norm_mix_post', 'norm_ffn_pre', 'norm_ffn_post', 'na_w_qkv', 'na_w_o', 'na_rpb', 'dil_w_qkv', 'dil_w_o', 'ffn_w_gate', 'ffn_w_up', 'ffn_w_down', 'loss_target', 'm_norm_mix_pre', 'm_norm_mix_post', 'm_norm_ffn_pre', 'm_norm_ffn_post', 'm_na_w_qkv', 'm_na_w_o', 'm_na_rpb', 'm_dil_w_qkv', 'm_dil_w_o', 'm_ffn_w_gate', 'm_ffn_w_up', 'm_ffn_w_down', 'v_norm_mix_pre', 'v_norm_mix_post', 'v_norm_ffn_pre', 'v_norm_ffn_post', 'v_na_w_qkv', 'v_na_w_o', 'v_na_rpb', 'v_dil_w_qkv', 'v_dil_w_o', 'v_ffn_w_gate', 'v_ffn_w_up', 'v_ffn_w_down']
TWIN_OUTPUTS = ['loss', 'grad_x', 'grad_norm_mix_pre', 'grad_norm_mix_post', 'grad_norm_ffn_pre', 'grad_norm_ffn_post', 'grad_na_w_qkv', 'grad_na_w_o', 'grad_na_rpb', 'grad_dil_w_qkv', 'grad_dil_w_o', 'grad_ffn_w_gate', 'grad_ffn_w_up', 'grad_ffn_w_down', 'delta_norm_mix_pre', 'delta_norm_mix_post', 'delta_norm_ffn_pre', 'delta_norm_ffn_post', 'delta_na_w_qkv', 'delta_na_w_o', 'delta_na_rpb', 'delta_dil_w_qkv', 'delta_dil_w_o', 'delta_ffn_w_gate', 'delta_ffn_w_up', 'delta_ffn_w_down', 'new_m_norm_mix_pre', 'new_m_norm_mix_post', 'new_m_norm_ffn_pre', 'new_m_norm_ffn_post', 'new_m_na_w_qkv', 'new_m_na_w_o', 'new_m_na_rpb', 'new_m_dil_w_qkv', 'new_m_dil_w_o', 'new_m_ffn_w_gate', 'new_m_ffn_w_up', 'new_m_ffn_w_down', 'new_v_norm_mix_pre', 'new_v_norm_mix_post', 'new_v_norm_ffn_pre', 'new_v_norm_ffn_post', 'new_v_na_w_qkv', 'new_v_na_w_o', 'new_v_na_rpb', 'new_v_dil_w_qkv', 'new_v_dil_w_o', 'new_v_ffn_w_gate', 'new_v_ffn_w_up', 'new_v_ffn_w_down']
TWIN_LEAF_KINDS = {'loss': 'loss', 'grad_x': 'grad_x', 'grad_norm_mix_pre': 'grad_w', 'grad_norm_mix_post': 'grad_w', 'grad_norm_ffn_pre': 'grad_w', 'grad_norm_ffn_post': 'grad_w', 'grad_na_w_qkv': 'grad_w', 'grad_na_w_o': 'grad_w', 'grad_na_rpb': 'grad_w', 'grad_dil_w_qkv': 'grad_w', 'grad_dil_w_o': 'grad_w', 'grad_ffn_w_gate': 'grad_w', 'grad_ffn_w_up': 'grad_w', 'grad_ffn_w_down': 'grad_w', 'delta_norm_mix_pre': 'delta_w', 'delta_norm_mix_post': 'delta_w', 'delta_norm_ffn_pre': 'delta_w', 'delta_norm_ffn_post': 'delta_w', 'delta_na_w_qkv': 'delta_w', 'delta_na_w_o': 'delta_w', 'delta_na_rpb': 'delta_w', 'delta_dil_w_qkv': 'delta_w', 'delta_dil_w_o': 'delta_w', 'delta_ffn_w_gate': 'delta_w', 'delta_ffn_w_up': 'delta_w', 'delta_ffn_w_down': 'delta_w', 'new_m_norm_mix_pre': 'new_m', 'new_m_norm_mix_post': 'new_m', 'new_m_norm_ffn_pre': 'new_m', 'new_m_norm_ffn_post': 'new_m', 'new_m_na_w_qkv': 'new_m', 'new_m_na_w_o': 'new_m', 'new_m_na_rpb': 'new_m', 'new_m_dil_w_qkv': 'new_m', 'new_m_dil_w_o': 'new_m', 'new_m_ffn_w_gate': 'new_m', 'new_m_ffn_w_up': 'new_m', 'new_m_ffn_w_down': 'new_m', 'new_v_norm_mix_pre': 'new_v', 'new_v_norm_mix_post': 'new_v', 'new_v_norm_ffn_pre': 'new_v', 'new_v_norm_ffn_post': 'new_v', 'new_v_na_w_qkv': 'new_v', 'new_v_na_w_o': 'new_v', 'new_v_na_rpb': 'new_v', 'new_v_dil_w_qkv': 'new_v', 'new_v_dil_w_o': 'new_v', 'new_v_ffn_w_gate': 'new_v', 'new_v_ffn_w_up': 'new_v', 'new_v_ffn_w_down': 'new_v'}


def _forward(args):
    return _fwd_reference(*[args[k] for k in FWD_PARAMS])


def _output_shape():
    out = _jax.eval_shape(lambda: _forward(_fwd_setup_inputs(0)))
    return out.shape, out.dtype

N_MICROBATCH = 1
ADAM_LR = 0.001
ADAM_B1 = 0.9
ADAM_B2 = 0.999
ADAM_EPS = 1e-08
ADAM_WD = 0.01
ADAM_STEP = 10
PER_EXAMPLE_BATCH_AXIS = {'x': 0, 'loss_target': 0}
SHARED_INPUTS = []
_WEIGHT_DTYPES = {'norm_mix_pre': _jnp.float32, 'norm_mix_post': _jnp.float32, 'norm_ffn_pre': _jnp.float32, 'norm_ffn_post': _jnp.float32, 'na_w_qkv': _jnp.float32, 'na_w_o': _jnp.float32, 'na_rpb': _jnp.float32, 'dil_w_qkv': _jnp.float32, 'dil_w_o': _jnp.float32, 'ffn_w_gate': _jnp.float32, 'ffn_w_up': _jnp.float32, 'ffn_w_down': _jnp.float32}
MOMENT_SCALE = {'norm_mix_pre': 1.161728e+00, 'norm_mix_post': 1.569533e+01, 'norm_ffn_pre': 6.437807e-01, 'norm_ffn_post': 1.601538e+01, 'na_w_qkv': 8.066882e-01, 'na_w_o': 1.102646e+00, 'na_rpb': 2.125676e-01, 'dil_w_qkv': 2.935735e-01, 'dil_w_o': 8.141375e-01, 'ffn_w_gate': 2.557400e-01, 'ffn_w_up': 3.153192e-01, 'ffn_w_down': 5.294231e-01}


def _to_microbatches(a, axis):
    t = _jnp.moveaxis(a, axis, 0)
    t = t.reshape((N_MICROBATCH, t.shape[0] // N_MICROBATCH) + t.shape[1:])
    return _jnp.moveaxis(t, 1, axis + 1)


def setup_inputs(seed: int = 0) -> dict:
    inp = _fwd_setup_inputs(seed)
    key = _jax.random.fold_in(_jax.random.key(seed), 7919)
    shape, _ = _output_shape()
    out = dict(inp)
    out["loss_target"] = _jax.random.normal(_jax.random.fold_in(key, 0), shape, _jnp.float32)
    for i, name in enumerate(TWIN_WEIGHTS):
        w = inp[name].astype(_jnp.float32)
        if MOMENT_SCALE is None:
            s = _jnp.sqrt(_jnp.mean(_jnp.square(w)) + 1e-30)
        else:
            s = MOMENT_SCALE[name]
        km, kv = _jax.random.split(_jax.random.fold_in(key, i + 1))
        out[name] = w
        out["m_" + name] = s * _jax.random.normal(km, w.shape, _jnp.float32)
        out["v_" + name] = (s * s) * _jax.random.uniform(kv, w.shape, _jnp.float32, 0.5, 1.5)
    if N_MICROBATCH > 1:
        for name, axis in PER_EXAMPLE_BATCH_AXIS.items():
            out[name] = _to_microbatches(out[name], axis)
    return {'x': out['x'], 'norm_mix_pre': out['norm_mix_pre'], 'norm_mix_post': out['norm_mix_post'], 'norm_ffn_pre': out['norm_ffn_pre'], 'norm_ffn_post': out['norm_ffn_post'], 'na_w_qkv': out['na_w_qkv'], 'na_w_o': out['na_w_o'], 'na_rpb': out['na_rpb'], 'dil_w_qkv': out['dil_w_qkv'], 'dil_w_o': out['dil_w_o'], 'ffn_w_gate': out['ffn_w_gate'], 'ffn_w_up': out['ffn_w_up'], 'ffn_w_down': out['ffn_w_down'], 'loss_target': out['loss_target'], 'm_norm_mix_pre': out['m_norm_mix_pre'], 'm_norm_mix_post': out['m_norm_mix_post'], 'm_norm_ffn_pre': out['m_norm_ffn_pre'], 'm_norm_ffn_post': out['m_norm_ffn_post'], 'm_na_w_qkv': out['m_na_w_qkv'], 'm_na_w_o': out['m_na_w_o'], 'm_na_rpb': out['m_na_rpb'], 'm_dil_w_qkv': out['m_dil_w_qkv'], 'm_dil_w_o': out['m_dil_w_o'], 'm_ffn_w_gate': out['m_ffn_w_gate'], 'm_ffn_w_up': out['m_ffn_w_up'], 'm_ffn_w_down': out['m_ffn_w_down'], 'v_norm_mix_pre': out['v_norm_mix_pre'], 'v_norm_mix_post': out['v_norm_mix_post'], 'v_norm_ffn_pre': out['v_norm_ffn_pre'], 'v_norm_ffn_post': out['v_norm_ffn_post'], 'v_na_w_qkv': out['v_na_w_qkv'], 'v_na_w_o': out['v_na_w_o'], 'v_na_rpb': out['v_na_rpb'], 'v_dil_w_qkv': out['v_dil_w_qkv'], 'v_dil_w_o': out['v_dil_w_o'], 'v_ffn_w_gate': out['v_ffn_w_gate'], 'v_ffn_w_up': out['v_ffn_w_up'], 'v_ffn_w_down': out['v_ffn_w_down']}


def _loss(weights, diff, rest, loss_target):
    with _jax.named_scope("forward"):
        args = {**rest, TWIN_DIFF_INPUT: diff, **{k: w.astype(_WEIGHT_DTYPES[k]) for k, w in weights.items()}}
        y = _forward(args)
    with _jax.named_scope("loss_head"):
        err = _jnp.square(y.astype(_jnp.float32) - loss_target)
        return 0.5 * _jnp.sum(_jnp.mean(err, axis=-1)) if err.ndim else 0.5 * err


def _adamw(w, g, m, v):
    m = ADAM_B1 * m + (1.0 - ADAM_B1) * g
    v = ADAM_B2 * v + (1.0 - ADAM_B2) * _jnp.square(g)
    m_hat = m / (1.0 - ADAM_B1 ** ADAM_STEP)
    v_hat = v / (1.0 - ADAM_B2 ** ADAM_STEP)
    delta = -ADAM_LR * (m_hat / (_jnp.sqrt(v_hat) + ADAM_EPS) + ADAM_WD * w)
    return delta, m, v


def reference(x, norm_mix_pre, norm_mix_post, norm_ffn_pre, norm_ffn_post, na_w_qkv, na_w_o, na_rpb, dil_w_qkv, dil_w_o, ffn_w_gate, ffn_w_up, ffn_w_down, loss_target, m_norm_mix_pre, m_norm_mix_post, m_norm_ffn_pre, m_norm_ffn_post, m_na_w_qkv, m_na_w_o, m_na_rpb, m_dil_w_qkv, m_dil_w_o, m_ffn_w_gate, m_ffn_w_up, m_ffn_w_down, v_norm_mix_pre, v_norm_mix_post, v_norm_ffn_pre, v_norm_ffn_post, v_na_w_qkv, v_na_w_o, v_na_rpb, v_dil_w_qkv, v_dil_w_o, v_ffn_w_gate, v_ffn_w_up, v_ffn_w_down):
    given = dict(x=x, norm_mix_pre=norm_mix_pre, norm_mix_post=norm_mix_post, norm_ffn_pre=norm_ffn_pre, norm_ffn_post=norm_ffn_post, na_w_qkv=na_w_qkv, na_w_o=na_w_o, na_rpb=na_rpb, dil_w_qkv=dil_w_qkv, dil_w_o=dil_w_o, ffn_w_gate=ffn_w_gate, ffn_w_up=ffn_w_up, ffn_w_down=ffn_w_down, loss_target=loss_target, m_norm_mix_pre=m_norm_mix_pre, m_norm_mix_post=m_norm_mix_post, m_norm_ffn_pre=m_norm_ffn_pre, m_norm_ffn_post=m_norm_ffn_post, m_na_w_qkv=m_na_w_qkv, m_na_w_o=m_na_w_o, m_na_rpb=m_na_rpb, m_dil_w_qkv=m_dil_w_qkv, m_dil_w_o=m_dil_w_o, m_ffn_w_gate=m_ffn_w_gate, m_ffn_w_up=m_ffn_w_up, m_ffn_w_down=m_ffn_w_down, v_norm_mix_pre=v_norm_mix_pre, v_norm_mix_post=v_norm_mix_post, v_norm_ffn_pre=v_norm_ffn_pre, v_norm_ffn_post=v_norm_ffn_post, v_na_w_qkv=v_na_w_qkv, v_na_w_o=v_na_w_o, v_na_rpb=v_na_rpb, v_dil_w_qkv=v_dil_w_qkv, v_dil_w_o=v_dil_w_o, v_ffn_w_gate=v_ffn_w_gate, v_ffn_w_up=v_ffn_w_up, v_ffn_w_down=v_ffn_w_down)
    weights = {n: given[n] for n in TWIN_WEIGHTS}
    shared = {n: given[n] for n in SHARED_INPUTS}
    per_example = {n: given[n] for n in ['x']}
    grad_fn = _jax.value_and_grad(_loss, argnums=(0, 1))

    def one_microbatch(ex, loss_target):
        ex = dict(ex)
        diff = ex.pop(TWIN_DIFF_INPUT)
        return grad_fn(weights, diff, {**shared, **ex}, loss_target)

    if N_MICROBATCH == 1:
        loss, (grad_w, grad_x) = one_microbatch(per_example, given["loss_target"])
    else:
        def body(carry, xs):
            loss_sum, grad_sum = carry
            l_k, (gw_k, gx_k) = one_microbatch(xs[0], xs[1])
            with _jax.named_scope("update"):
                return (loss_sum + l_k, _jax.tree.map(_jnp.add, grad_sum, gw_k)), gx_k

        init = (_jnp.zeros((), _jnp.float32), _jax.tree.map(_jnp.zeros_like, weights))
        (loss, grad_w), grad_x = _jax.lax.scan(body, init, (per_example, given["loss_target"]))
    with _jax.named_scope("update"):
        delta_w, new_m, new_v = {}, {}, {}
        for n in TWIN_WEIGHTS:
            delta_w[n], new_m[n], new_v[n] = _adamw(weights[n], grad_w[n], given["m_" + n], given["v_" + n])
    return (loss, grad_x, *[grad_w[n] for n in TWIN_WEIGHTS], *[delta_w[n] for n in TWIN_WEIGHTS],
            *[new_m[n] for n in TWIN_WEIGHTS], *[new_v[n] for n in TWIN_WEIGHTS])
```

```python
import functools

import numpy as np
import jax
import jax.numpy as jnp
from jax import lax
from jax.experimental import pallas as pl
from jax.experimental.pallas import tpu as pltpu

F32 = jnp.float32
BF16 = jnp.bfloat16

SEQ = 2048
DM = 1024
NH = 16
HD = 64
DFF = 2816
NCHIP = 4
FSH = DFF // NCHIP
GRID_W = 64
NA_QROWS = 4
NA_QB = NA_QROWS * GRID_W
NA_WROWS = 12
NA_WIN = NA_WROWS * GRID_W
DIL = (1, 4, 16)
DIL_QB = 128
DIL_WIN = 384
DIL_RADIUS = 64
RMS_EPS = 1e-6
NEG = -1e30
QSCALE = HD ** -0.5
CH = 256
MESH = pl.DeviceIdType.MESH

ADAM_LR, ADAM_B1, ADAM_B2, ADAM_EPS, ADAM_WD, ADAM_STEP = 0.001, 0.9, 0.999, 1e-08, 0.01, 10

VMEM_LIMIT = 56 * 1024 * 1024

_NN = (((1,), (0,)), ((), ()))
_NT = (((1,), (1,)), ((), ()))
_TN = (((0,), (0,)), ((), ()))


def _params(sem):
    return pltpu.CompilerParams(dimension_semantics=sem, vmem_limit_bytes=VMEM_LIMIT)


def _matmul(name, pairs, grid, out_shape, out_spec, acc_shape):
    nk = grid[-1]
    npair = len(pairs)

    def body(*refs):
        ins, o_ref = refs[:2 * npair], refs[2 * npair]
        part = None
        for p in range(npair):
            d = lax.dot_general(ins[2 * p][...].astype(BF16), ins[2 * p + 1][...].astype(BF16), pairs[p][4],
                                preferred_element_type=F32)
            part = d if part is None else part + d
        if nk == 1:
            o_ref[...] = part.astype(o_ref.dtype)
        else:
            acc_ref = refs[2 * npair + 1]
            kk = pl.program_id(len(grid) - 1)

            @pl.when(kk == 0)
            def _():
                acc_ref[...] = part

            @pl.when(kk > 0)
            def _():
                acc_ref[...] += part

            @pl.when(kk == nk - 1)
            def _():
                o_ref[...] = acc_ref[...].astype(o_ref.dtype)

    ops, specs = [], []
    for a, a_spec, b, b_spec, _ in pairs:
        ops += [a, b]
        specs += [a_spec, b_spec]
    return pl.pallas_call(
        body, grid=grid, in_specs=specs, out_specs=out_spec, out_shape=out_shape,
        scratch_shapes=[] if nk == 1 else [pltpu.VMEM(acc_shape, F32)],
        compiler_params=_params(("parallel",) * (len(grid) - 1) + ("arbitrary",)), name=name,
    )(*ops)


def _qkv_fwd(name, h_all, w4):
    g_n = h_all.shape[0]
    per = w4.shape[2] // CH
    return _matmul(
        name, [(h_all, pl.BlockSpec((None, SEQ, DM), lambda g, q, k: (g, 0, 0)),
                w4, pl.BlockSpec((None, DM, CH), lambda g, q, k: ((g * 12 + q) // per, 0, (g * 12 + q) % per)), _NN)],
        (g_n, 12, 1), jax.ShapeDtypeStruct((g_n, SEQ, 3 * DM), BF16),
        pl.BlockSpec((None, SEQ, CH), lambda g, q, k: (g, 0, q)), None)


def _qkv_bwd_dh(name, dqkv, w4):
    g_n = dqkv.shape[0]
    per = w4.shape[2] // CH
    tm = 1024
    return _matmul(
        name, [(dqkv, pl.BlockSpec((None, None, tm, CH), lambda g, i, q: (g, q // 4, i, q % 4)),
                w4, pl.BlockSpec((None, DM, CH), lambda g, i, q: ((g * 12 + q) // per, 0, (g * 12 + q) % per)), _NT)],
        (g_n, SEQ // tm, 12), jax.ShapeDtypeStruct((g_n, SEQ, DM), F32),
        pl.BlockSpec((None, tm, DM), lambda g, i, q: (g, i, 0)), (tm, DM))


def _qkv_bwd_dw(name, h_all, dqkv, shard_cols):
    g_n = dqkv.shape[0]
    per = shard_cols // CH
    tk = 512
    return _matmul(
        name, [(h_all, pl.BlockSpec((None, tk, DM), lambda qq, k: (qq // 12, k, 0)),
                dqkv, pl.BlockSpec((None, None, tk, CH), lambda qq, k: (qq // 12, (qq % 12) // 4, k, qq % 4)), _TN)],
        (g_n * 12, SEQ // tk), jax.ShapeDtypeStruct((NCHIP, DM, shard_cols), BF16),
        pl.BlockSpec((None, DM, CH), lambda qq, k: (qq // per, 0, qq % per)), (DM, CH))


def _proj_fwd(name, o, wo):
    tm = 512
    return _matmul(
        name, [(o, pl.BlockSpec((tm, DM), lambda i, k: (i, 0)), wo, pl.BlockSpec((DM, DM), lambda i, k: (0, 0)), _NN)],
        (SEQ // tm, 1), jax.ShapeDtypeStruct((SEQ, DM), F32), pl.BlockSpec((tm, DM), lambda i, k: (i, 0)), None)


def _proj_bwd_do(name, du, wo):
    tm = 512
    return _matmul(
        name, [(du, pl.BlockSpec((tm, DM), lambda i, k: (i, 0)), wo, pl.BlockSpec((DM, DM), lambda i, k: (0, 0)), _NT)],
        (SEQ // tm, 1), jax.ShapeDtypeStruct((SEQ, DM), BF16), pl.BlockSpec((tm, DM), lambda i, k: (i, 0)), None)


def _proj_bwd_dw(name, o, du):
    tk, tn = 512, 512
    return _matmul(
        name, [(o, pl.BlockSpec((tk, DM), lambda j, k: (k, 0)), du, pl.BlockSpec((tk, tn), lambda j, k: (k, j)), _TN)],
        (DM // tn, SEQ // tk), jax.ShapeDtypeStruct((DM, DM), BF16), pl.BlockSpec((DM, tn), lambda j, k: (0, j)), (DM, tn))


def _ffn_in(name, h, w4):
    tm = 1024
    return _matmul(
        name, [(h, pl.BlockSpec((tm, DM), lambda i, s, k: (i, 0)), w4, pl.BlockSpec((None, DM, FSH), lambda i, s, k: (s, 0, 0)), _NN)],
        (SEQ // tm, NCHIP, 1), jax.ShapeDtypeStruct((NCHIP, SEQ, FSH), BF16),
        pl.BlockSpec((None, tm, FSH), lambda i, s, k: (s, i, 0)), None)


def _ffn_out(name, act, wd4):
    tm = 1024
    return _matmul(
        name, [(act, pl.BlockSpec((None, tm, FSH), lambda i, s: (s, i, 0)), wd4, pl.BlockSpec((None, FSH, DM), lambda i, s: (s, 0, 0)), _NN)],
        (SEQ // tm, NCHIP), jax.ShapeDtypeStruct((SEQ, DM), F32), pl.BlockSpec((tm, DM), lambda i, s: (i, 0)), (tm, DM))


def _ffn_bwd_dact(name, du, wd4):
    tm = 1024
    return _matmul(
        name, [(du, pl.BlockSpec((tm, DM), lambda i, s, k: (i, 0)), wd4, pl.BlockSpec((None, FSH, DM), lambda i, s, k: (s, 0, 0)), _NT)],
        (SEQ // tm, NCHIP, 1), jax.ShapeDtypeStruct((NCHIP, SEQ, FSH), BF16),
        pl.BlockSpec((None, tm, FSH), lambda i, s, k: (s, i, 0)), None)


def _ffn_bwd_dwd(name, act, du):
    tk = 512
    return _matmul(
        name, [(act, pl.BlockSpec((None, tk, FSH), lambda s, k: (s, k, 0)), du, pl.BlockSpec((tk, DM), lambda s, k: (k, 0)), _TN)],
        (NCHIP, SEQ // tk), jax.ShapeDtypeStruct((NCHIP, FSH, DM), BF16), pl.BlockSpec((None, FSH, DM), lambda s, k: (s, 0, 0)), (FSH, DM))


def _ffn_bwd_dh(name, dgate, wg4, dup, wu4):
    tm = 1024
    a_spec = pl.BlockSpec((None, tm, FSH), lambda i, s: (s, i, 0))
    b_spec = pl.BlockSpec((None, DM, FSH), lambda i, s: (s, 0, 0))
    return _matmul(
        name, [(dgate, a_spec, wg4, b_spec, _NT), (dup, a_spec, wu4, b_spec, _NT)],
        (SEQ // tm, NCHIP), jax.ShapeDtypeStruct((SEQ, DM), F32), pl.BlockSpec((tm, DM), lambda i, s: (i, 0)), (tm, DM))


def _ffn_bwd_dwin(name, h, dg):
    tk = 512
    return _matmul(
        name, [(h, pl.BlockSpec((tk, DM), lambda s, k: (k, 0)), dg, pl.BlockSpec((None, tk, FSH), lambda s, k: (s, k, 0)), _TN)],
        (NCHIP, SEQ // tk), jax.ShapeDtypeStruct((NCHIP, DM, FSH), BF16), pl.BlockSpec((None, DM, FSH), lambda s, k: (s, 0, 0)), (DM, FSH))


ROWS = 256


def _row_spec():
    return pl.BlockSpec((ROWS, DM), lambda i: (i, 0))


def _vec_spec():
    return pl.BlockSpec((1, DM), lambda i: (0, 0))


def _rms_fwd(name, x, g):
    def body(x_ref, g_ref, o_ref):
        x = x_ref[...]
        r = lax.rsqrt(jnp.mean(x * x, axis=-1, keepdims=True) + RMS_EPS)
        o_ref[...] = (x * r * g_ref[...]).astype(o_ref.dtype)

    return pl.pallas_call(body, grid=(SEQ // ROWS,), in_specs=[_row_spec(), _vec_spec()], out_specs=_row_spec(),
                          out_shape=jax.ShapeDtypeStruct((SEQ, DM), BF16), compiler_params=_params(("parallel",)), name=name)(x, g)


def _resid_norm(name, x, u, g):
    def body(x_ref, u_ref, g_ref, o_ref):
        u = u_ref[...]
        r = lax.rsqrt(jnp.mean(u * u, axis=-1, keepdims=True) + RMS_EPS)
        o_ref[...] = x_ref[...] + u * r * g_ref[...]

    return pl.pallas_call(body, grid=(SEQ // ROWS,), in_specs=[_row_spec(), _row_spec(), _vec_spec()], out_specs=_row_spec(),
                          out_shape=jax.ShapeDtypeStruct((SEQ, DM), F32), compiler_params=_params(("parallel",)), name=name)(x, u, g)


def _norm_bwd(name, dys, u, g, res=None):
    ndy = len(dys)

    def body(*refs):
        dy = refs[0][...]
        for r_ in refs[1:ndy]:
            dy = dy + r_[...]
        u_ref, g_ref = refs[ndy], refs[ndy + 1]
        res_ref = refs[ndy + 2] if res is not None else None
        du_ref, dg_ref = refs[-2], refs[-1]
        u = u_ref[...]
        r = lax.rsqrt(jnp.mean(u * u, axis=-1, keepdims=True) + RMS_EPS)
        yh = u * r
        t = dy * g_ref[...]
        du = r * (t - yh * jnp.mean(t * yh, axis=-1, keepdims=True))
        if res_ref is not None:
            du = du + res_ref[...]
        du_ref[...] = du

        @pl.when(pl.program_id(0) == 0)
        def _():
            dg_ref[...] = jnp.zeros_like(dg_ref)

        dg_ref[...] += jnp.sum(dy * yh, axis=0, keepdims=True)

    ops = list(dys) + [u, g] + ([res] if res is not None else [])
    specs = [_row_spec()] * ndy + [_row_spec(), _vec_spec()] + ([_row_spec()] if res is not None else [])
    return pl.pallas_call(
        body, grid=(SEQ // ROWS,), in_specs=specs, out_specs=[_row_spec(), _vec_spec()],
        out_shape=[jax.ShapeDtypeStruct((SEQ, DM), F32), jax.ShapeDtypeStruct((1, DM), F32)],
        compiler_params=_params(("arbitrary",)), name=name)(*ops)


def _loss_grad(name, y, t):
    def body(y_ref, t_ref, dy_ref, l_ref):
        e = y_ref[...] - t_ref[...]
        dy_ref[...] = e * (1.0 / DM)

        @pl.when(pl.program_id(0) == 0)
        def _():
            l_ref[...] = jnp.zeros_like(l_ref)

        l_ref[...] += jnp.sum(e * e) * (0.5 / DM)

    return pl.pallas_call(
        body, grid=(SEQ // ROWS,), in_specs=[_row_spec(), _row_spec()],
        out_specs=[_row_spec(), pl.BlockSpec((1, 128), lambda i: (0, 0))],
        out_shape=[jax.ShapeDtypeStruct((SEQ, DM), F32), jax.ShapeDtypeStruct((1, 128), F32)],
        compiler_params=_params(("arbitrary",)), name=name)(y, t)


def _ffn_spec():
    return pl.BlockSpec((None, 512, FSH), lambda s, i: (s, i, 0))


def _swiglu_fwd(name, gate, up):
    def body(g_ref, u_ref, o_ref):
        g = g_ref[...].astype(F32)
        o_ref[...] = (g * jax.nn.sigmoid(g) * u_ref[...].astype(F32)).astype(o_ref.dtype)

    return pl.pallas_call(body, grid=(NCHIP, SEQ // 512), in_specs=[_ffn_spec(), _ffn_spec()], out_specs=_ffn_spec(),
                          out_shape=jax.ShapeDtypeStruct((NCHIP, SEQ, FSH), BF16),
                          compiler_params=_params(("parallel", "parallel")), name=name)(gate, up)


def _swiglu_bwd(name, dact, gate, up):
    def body(d_ref, g_ref, u_ref, dg_ref, du_ref):
        d = d_ref[...].astype(F32)
        g = g_ref[...].astype(F32)
        u = u_ref[...].astype(F32)
        sg = jax.nn.sigmoid(g)
        dg_ref[...] = (d * u * sg * (1.0 + g * (1.0 - sg))).astype(dg_ref.dtype)
        du_ref[...] = (d * g * sg).astype(du_ref.dtype)

    sh = jax.ShapeDtypeStruct((NCHIP, SEQ, FSH), BF16)
    return pl.pallas_call(body, grid=(NCHIP, SEQ // 512), in_specs=[_ffn_spec()] * 3, out_specs=[_ffn_spec()] * 2,
                          out_shape=[sh, sh], compiler_params=_params(("parallel", "parallel")), name=name)(dact, gate, up)


NA_BLOCKS = SEQ // NA_QB
NA_ROWS_TOTAL = SEQ // GRID_W
NA_CLASSES = ((0, 0), (8, 4), (NA_ROWS_TOTAL - NA_QROWS, NA_ROWS_TOTAL - NA_WROWS))


def _na_pairs(i0, ws):
    out = []
    for qi in range(NA_QROWS):
        i = i0 + qi
        rs = min(max(i - 4, 0), NA_ROWS_TOTAL - 8)
        for kr in range(NA_WROWS):
            r = ws + kr
            if rs <= r < rs + 8:
                out.append((qi, kr, r - i + 7))
    return out


def _na_bias_tiles(rpb):
    col = np.arange(GRID_W)
    col_start = np.clip(col - 8, 0, GRID_W - 16)
    col_mask = (col[None, :] >= col_start[:, None]) & (col[None, :] < col_start[:, None] + 16)
    col_idx = np.clip(col[None, :] - col[:, None] + 15, 0, 30)
    rc = jnp.where(col_mask[None, None], rpb[:, :, col_idx], NEG)
    neg = jnp.full((NH, GRID_W, GRID_W), NEG, F32)
    tiles = []
    for i0, ws in NA_CLASSES:
        pairs = {(qi, kr): dr for qi, kr, dr in _na_pairs(i0, ws)}
        rows = [jnp.concatenate([rc[:, pairs[(qi, kr)]] if (qi, kr) in pairs else neg for kr in range(NA_WROWS)], axis=2)
                for qi in range(NA_QROWS)]
        tiles.append(jnp.concatenate(rows, axis=1))
    return jnp.stack(tiles)


def _na_cls(b):
    return jnp.where(b == 0, 0, jnp.where(b == NA_BLOCKS - 1, 2, 1))


def _na_start(b):
    return pl.multiple_of(jnp.clip(b * NA_QROWS - 4, 0, NA_ROWS_TOTAL - NA_WROWS) * GRID_W, GRID_W)


def _na_in_specs():
    return [pl.BlockSpec((NA_QB, 128), lambda hp, b: (b, hp)),
            pl.BlockSpec((SEQ, 128), lambda hp, b: (0, 8 + hp)),
            pl.BlockSpec((SEQ, 128), lambda hp, b: (0, 16 + hp)),
            pl.BlockSpec((None, 2, NA_QB, NA_WIN), lambda hp, b: (_na_cls(b), hp, 0, 0))]


def _na_fwd(qkv, bias):
    def body(q_ref, k_ref, v_ref, b_ref, o_ref):
        start = _na_start(pl.program_id(1))
        q = q_ref[...]
        kw = k_ref[pl.ds(start, NA_WIN), :]
        vw = v_ref[pl.ds(start, NA_WIN), :]
        outs = []
        for hh in range(2):
            sl = slice(hh * HD, (hh + 1) * HD)
            s = lax.dot_general(q[:, sl] * QSCALE, kw[:, sl], _NT, preferred_element_type=F32) + b_ref[hh]
            p = jnp.exp(s - jnp.max(s, axis=-1, keepdims=True))
            l = jnp.sum(p, axis=-1, keepdims=True)
            outs.append(jnp.dot(p.astype(BF16), vw[:, sl], preferred_element_type=F32) / l)
        o_ref[...] = jnp.concatenate(outs, axis=1).astype(o_ref.dtype)

    return pl.pallas_call(
        body, grid=(NH // 2, NA_BLOCKS), in_specs=_na_in_specs(), out_specs=pl.BlockSpec((NA_QB, 128), lambda hp, b: (b, hp)),
        out_shape=jax.ShapeDtypeStruct((SEQ, DM), BF16), compiler_params=_params(("parallel", "arbitrary")), name="na_fwd")(qkv, qkv, qkv, bias)


def _na_bwd(qkv, bias, do):
    def body(q_ref, k_ref, v_ref, b_ref, do_ref, dqkv_ref, z_ref, dk_acc, dv_acc):
        blk = pl.program_id(1)

        @pl.when(blk == 0)
        def _():
            dk_acc[...] = jnp.zeros_like(dk_acc)
            dv_acc[...] = jnp.zeros_like(dv_acc)
            z_ref[...] = jnp.zeros_like(z_ref)

        start = _na_start(blk)
        q = q_ref[...]
        do = do_ref[...]
        kw = k_ref[pl.ds(start, NA_WIN), :]
        vw = v_ref[pl.ds(start, NA_WIN), :]
        dqs, dks, dvs = [], [], []
        for hh in range(2):
            sl = slice(hh * HD, (hh + 1) * HD)
            qh = q[:, sl] * QSCALE
            s = lax.dot_general(qh, kw[:, sl], _NT, preferred_element_type=F32) + b_ref[hh]
            p = jnp.exp(s - jnp.max(s, axis=-1, keepdims=True))
            p = p / jnp.sum(p, axis=-1, keepdims=True)
            dp = lax.dot_general(do[:, sl], vw[:, sl], _NT, preferred_element_type=F32)
            ds = p * (dp - jnp.sum(p * dp, axis=-1, keepdims=True))
            dsb = ds.astype(BF16)
            dqs.append(jnp.dot(dsb, kw[:, sl], preferred_element_type=F32) * QSCALE)
            dks.append(lax.dot_general(dsb, qh, _TN, preferred_element_type=F32))
            dvs.append(lax.dot_general(p.astype(BF16), do[:, sl], _TN, preferred_element_type=F32))
            for cls, (i0, ws) in enumerate(NA_CLASSES):
                @pl.when(_na_cls(blk) == cls)
                def _(ds=ds, hh=hh, i0=i0, ws=ws):
                    for qi, kr, dr in _na_pairs(i0, ws):
                        z_ref[hh, dr * GRID_W:(dr + 1) * GRID_W, :] += ds[qi * GRID_W:(qi + 1) * GRID_W, kr * GRID_W:(kr + 1) * GRID_W]
        dqkv_ref[0, pl.ds(pl.multiple_of(blk * NA_QB, NA_QB), NA_QB), :] = jnp.concatenate(dqs, axis=1).astype(dqkv_ref.dtype)
        dk_acc[pl.ds(start, NA_WIN), :] += jnp.concatenate(dks, axis=1)
        dv_acc[pl.ds(start, NA_WIN), :] += jnp.concatenate(dvs, axis=1)

        @pl.when(blk == NA_BLOCKS - 1)
        def _():
            dqkv_ref[1] = dk_acc[...].astype(dqkv_ref.dtype)
            dqkv_ref[2] = dv_acc[...].astype(dqkv_ref.dtype)

    return pl.pallas_call(
        body, grid=(NH // 2, NA_BLOCKS),
        in_specs=_na_in_specs() + [pl.BlockSpec((NA_QB, 128), lambda hp, b: (b, hp))],
        out_specs=[pl.BlockSpec((3, SEQ, 128), lambda hp, b: (0, 0, hp)), pl.BlockSpec((2, 15 * GRID_W, GRID_W), lambda hp, b: (hp, 0, 0))],
        out_shape=[jax.ShapeDtypeStruct((3, SEQ, DM), BF16), jax.ShapeDtypeStruct((NH, 15 * GRID_W, GRID_W), F32)],
        scratch_shapes=[pltpu.VMEM((SEQ, 128), F32), pltpu.VMEM((SEQ, 128), F32)],
        compiler_params=_params(("parallel", "arbitrary")), name="na_bwd")(qkv, qkv, qkv, bias, do)


def _diag_onehot():
    qc, kc = np.meshgrid(np.arange(GRID_W), np.arange(GRID_W), indexing="ij")
    e = np.zeros((GRID_W * GRID_W, 128), np.float32)
    j = (kc - qc + 15).reshape(-1)
    ok = (j >= 0) & (j <= 30)
    e[np.arange(GRID_W * GRID_W)[ok], j[ok]] = 1.0
    return jnp.asarray(e)


def _rpb_grad(z):
    z2 = z.reshape(NH * 15, GRID_W * GRID_W)

    def body(z_ref, e_ref, o_ref):
        o_ref[...] = jnp.dot(z_ref[...], e_ref[...], preferred_element_type=F32, precision=lax.Precision.HIGHEST)

    out = pl.pallas_call(body, out_shape=jax.ShapeDtypeStruct((NH * 15, 128), F32), name="rpb_grad",
                         compiler_params=pltpu.CompilerParams(vmem_limit_bytes=VMEM_LIMIT))(z2, _diag_onehot())
    return out[:, :31].reshape(NH, 15, 31)


DIL_BLOCKS = SEQ // DIL_QB


def _perm(a, d):
    return a if d == 1 else a.reshape(SEQ // d, d, a.shape[-1]).transpose(1, 0, 2).reshape(SEQ, a.shape[-1])


def _unperm(a, d):
    return a if d == 1 else a.reshape(d, SEQ // d, a.shape[-1]).transpose(1, 0, 2).reshape(SEQ, a.shape[-1])


def _to_groups(a):
    return jnp.stack([_perm(a, d) for d in DIL])


def _dil_start(b):
    return pl.multiple_of(jnp.clip(b - 1, 0, DIL_BLOCKS - 3) * DIL_QB, DIL_QB)


def _dil_mask(g, b, start):
    shift = 11 - 2 * g
    ii = b * DIL_QB + lax.broadcasted_iota(jnp.int32, (DIL_QB, DIL_WIN), 0)
    jj = start + lax.broadcasted_iota(jnp.int32, (DIL_QB, DIL_WIN), 1)
    dist = jnp.abs(ii - jj)
    valid = (dist <= DIL_RADIUS) & (jnp.right_shift(ii, shift) == jnp.right_shift(jj, shift))
    return valid, dist.astype(F32)


def _dil_in_specs():
    return [pl.BlockSpec(memory_space=pltpu.SMEM),
            pl.BlockSpec((None, DIL_QB, 128), lambda g, hp, b: (g, b, hp)),
            pl.BlockSpec((None, SEQ, 128), lambda g, hp, b: (g, 0, 8 + hp)),
            pl.BlockSpec((None, SEQ, 128), lambda g, hp, b: (g, 0, 16 + hp))]


def _dil_fwd(qkv, slopes):
    def body(sl_ref, q_ref, k_ref, v_ref, o_ref, lse_ref):
        g, hp, b = pl.program_id(0), pl.program_id(1), pl.program_id(2)
        start = _dil_start(b)
        valid, dist = _dil_mask(g, b, start)
        dil = jnp.left_shift(1, 2 * g).astype(F32)
        q = q_ref[...]
        kw = k_ref[pl.ds(start, DIL_WIN), :]
        vw = v_ref[pl.ds(start, DIL_WIN), :]
        outs, lses = [], []
        for hh in range(2):
            sl = slice(hh * HD, (hh + 1) * HD)
            s = lax.dot_general(q[:, sl] * QSCALE, kw[:, sl], _NT, preferred_element_type=F32)
            s = jnp.where(valid, s - (sl_ref[hp * 2 + hh] * dil) * dist, NEG)
            m = jnp.max(s, axis=-1, keepdims=True)
            p = jnp.exp(s - m)
            l = jnp.sum(p, axis=-1, keepdims=True)
            outs.append(jnp.dot(p.astype(BF16), vw[:, sl], preferred_element_type=F32) / l)
            lses.append(jnp.broadcast_to(m + jnp.log(l), (DIL_QB, HD)))
        o_ref[...] = jnp.concatenate(outs, axis=1)
        lse_ref[...] = jnp.concatenate(lses, axis=1)

    ospec = pl.BlockSpec((None, DIL_QB, 128), lambda g, hp, b: (g, b, hp))
    sh = jax.ShapeDtypeStruct((3, SEQ, DM), F32)
    return pl.pallas_call(
        body, grid=(3, NH // 2, DIL_BLOCKS), in_specs=_dil_in_specs(), out_specs=[ospec, ospec], out_shape=[sh, sh],
        compiler_params=_params(("parallel", "parallel", "arbitrary")), name="dil_fwd")(slopes, qkv, qkv, qkv)


def _dil_merge(o_all, lse_all):
    def body(o_ref, l_ref, out_ref, lse_ref):
        l = l_ref[...]
        m = jnp.max(l, axis=0)
        w = jnp.exp(l - m[None])
        sw = jnp.sum(w, axis=0)
        out_ref[...] = (jnp.sum(w * o_ref[...], axis=0) / sw).astype(out_ref.dtype)
        lse_ref[...] = m + jnp.log(sw)

    gspec = pl.BlockSpec((3, ROWS, DM), lambda i: (0, i, 0))
    return pl.pallas_call(
        body, grid=(SEQ // ROWS,), in_specs=[gspec, gspec], out_specs=[_row_spec(), _row_spec()],
        out_shape=[jax.ShapeDtypeStruct((SEQ, DM), BF16), jax.ShapeDtypeStruct((SEQ, DM), F32)],
        compiler_params=_params(("parallel",)), name="dil_merge")(o_all, lse_all)


def _head_rowdot(do, o):
    def body(a_ref, b_ref, o_ref):
        prod = a_ref[...].astype(F32) * b_ref[...].astype(F32)
        o_ref[...] = jnp.concatenate(
            [jnp.broadcast_to(jnp.sum(prod[:, h * HD:(h + 1) * HD], axis=-1, keepdims=True), (ROWS, HD)) for h in range(NH)], axis=1)

    return pl.pallas_call(body, grid=(SEQ // ROWS,), in_specs=[_row_spec(), _row_spec()], out_specs=_row_spec(),
                          out_shape=jax.ShapeDtypeStruct((SEQ, DM), F32), compiler_params=_params(("parallel",)), name="head_rowdot")(do, o)


def _dil_bwd(qkv, do, dd, lse, slopes):
    def body(sl_ref, q_ref, k_ref, v_ref, do_ref, dd_ref, lse_ref, dqkv_ref, dk_acc, dv_acc):
        g, hp, b = pl.program_id(0), pl.program_id(1), pl.program_id(2)

        @pl.when(b == 0)
        def _():
            dk_acc[...] = jnp.zeros_like(dk_acc)
            dv_acc[...] = jnp.zeros_like(dv_acc)

        start = _dil_start(b)
        valid, dist = _dil_mask(g, b, start)
        dil = jnp.left_shift(1, 2 * g).astype(F32)
        q = q_ref[...]
        do = do_ref[...]
        kw = k_ref[pl.ds(start, DIL_WIN), :]
        vw = v_ref[pl.ds(start, DIL_WIN), :]
        lse = lse_ref[...]
        dd = dd_ref[...]
        dqs, dks, dvs = [], [], []
        for hh in range(2):
            sl = slice(hh * HD, (hh + 1) * HD)
            qh = q[:, sl] * QSCALE
            s = lax.dot_general(qh, kw[:, sl], _NT, preferred_element_type=F32)
            s = jnp.where(valid, s - (sl_ref[hp * 2 + hh] * dil) * dist, NEG)
            p = jnp.exp(s - lse[:, hh * HD:hh * HD + 1])
            dp = lax.dot_general(do[:, sl], vw[:, sl], _NT, preferred_element_type=F32)
            dsb = (p * (dp - dd[:, hh * HD:hh * HD + 1])).astype(BF16)
            dqs.append(jnp.dot(dsb, kw[:, sl], preferred_element_type=F32) * QSCALE)
            dks.append(lax.dot_general(dsb, qh, _TN, preferred_element_type=F32))
            dvs.append(lax.dot_general(p.astype(BF16), do[:, sl], _TN, preferred_element_type=F32))
        dqkv_ref[0, pl.ds(pl.multiple_of(b * DIL_QB, DIL_QB), DIL_QB), :] = jnp.concatenate(dqs, axis=1).astype(dqkv_ref.dtype)
        dk_acc[pl.ds(start, DIL_WIN), :] += jnp.concatenate(dks, axis=1)
        dv_acc[pl.ds(start, DIL_WIN), :] += jnp.concatenate(dvs, axis=1)

        @pl.when(b == DIL_BLOCKS - 1)
        def _():
            dqkv_ref[1] = dk_acc[...].astype(dqkv_ref.dtype)
            dqkv_ref[2] = dv_acc[...].astype(dqkv_ref.dtype)

    rspec = pl.BlockSpec((None, DIL_QB, 128), lambda g, hp, b: (g, b, hp))
    return pl.pallas_call(
        body, grid=(3, NH // 2, DIL_BLOCKS), in_specs=_dil_in_specs() + [rspec, rspec, rspec],
        out_specs=pl.BlockSpec((None, 3, SEQ, 128), lambda g, hp, b: (g, 0, 0, hp)),
        out_shape=jax.ShapeDtypeStruct((3, 3, SEQ, DM), BF16),
        scratch_shapes=[pltpu.VMEM((SEQ, 128), F32), pltpu.VMEM((SEQ, 128), F32)],
        compiler_params=_params(("parallel", "parallel", "arbitrary")), name="dil_bwd")(slopes, qkv, qkv, qkv, do, dd, lse)


def _ffn_block(tag, x, g_pre, g_post, wg4, wu4, wd4):
    h = _rms_fwd(f"{tag}_ffn_pre", x, g_pre)
    gate = _ffn_in(f"{tag}_gate", h, wg4)
    up = _ffn_in(f"{tag}_up", h, wu4)
    act = _swiglu_fwd(f"{tag}_swiglu", gate, up)
    u = _ffn_out(f"{tag}_down", act, wd4)
    return _resid_norm(f"{tag}_ffn_post", x, u, g_post), (x, h, gate, up, act, u)


def _ffn_block_bwd(tag, dx, saved, g_pre, g_post, wg4, wu4, wd4):
    x, h, gate, up, act, u = saved
    du, dg_post = _norm_bwd(f"{tag}_ffn_post_bwd", [dx], u, g_post)
    d_wd = _ffn_bwd_dwd(f"{tag}_dwd", act, du)
    dact = _ffn_bwd_dact(f"{tag}_dact", du, wd4)
    dgate, dup = _swiglu_bwd(f"{tag}_swiglu_bwd", dact, gate, up)
    d_wg = _ffn_bwd_dwin(f"{tag}_dwg", h, dgate)
    d_wu = _ffn_bwd_dwin(f"{tag}_dwu", h, dup)
    dh = _ffn_bwd_dh(f"{tag}_ffn_dh", dgate, wg4, dup, wu4)
    dx_in, dg_pre = _norm_bwd(f"{tag}_ffn_pre_bwd", [dh], x, g_pre, res=dx)
    return dx_in, dg_pre, dg_post, d_wg, d_wu, d_wd


def _alibi_slopes():
    return 2.0 ** (-8.0 * jnp.arange(1, NH + 1, dtype=F32) / NH)


def _local_step(x, target, norms, rpb, w):
    g_mix_pre, g_mix_post, g_ffn_pre, g_ffn_post = norms
    row = lambda a, i: a[i:i + 1]
    na_wo = w["na_w_o"].reshape(DM, DM)
    dil_wo = w["dil_w_o"].reshape(DM, DM)

    bias = _na_bias_tiles(rpb)
    h0 = _rms_fwd("l0_mix_pre", x, row(g_mix_pre, 0))
    qkv0 = _qkv_fwd("l0_qkv", h0[None], w["na_w_qkv"])
    o0 = _na_fwd(qkv0[0], bias)
    u0 = _proj_fwd("l0_proj", o0, na_wo)
    x1 = _resid_norm("l0_mix_post", x, u0, row(g_mix_post, 0))
    x2, ffn0 = _ffn_block("l0", x1, row(g_ffn_pre, 0), row(g_ffn_post, 0), w["ffn_w_gate"][:, 0], w["ffn_w_up"][:, 0], w["ffn_w_down"][:, 0])

    slopes = _alibi_slopes()
    h2 = _rms_fwd("l1_mix_pre", x2, row(g_mix_pre, 1))
    h2g = _to_groups(h2)
    qkv1 = _qkv_fwd("l1_qkv", h2g, w["dil_w_qkv"])
    og, lg = _dil_fwd(qkv1, slopes)
    o1, lse = _dil_merge(jnp.stack([_unperm(og[i], d) for i, d in enumerate(DIL)]), jnp.stack([_unperm(lg[i], d) for i, d in enumerate(DIL)]))
    u1 = _proj_fwd("l1_proj", o1, dil_wo)
    x3 = _resid_norm("l1_mix_post", x2, u1, row(g_mix_post, 1))
    x4, ffn1 = _ffn_block("l1", x3, row(g_ffn_pre, 1), row(g_ffn_post, 1), w["ffn_w_gate"][:, 1], w["ffn_w_up"][:, 1], w["ffn_w_down"][:, 1])

    dx4, loss_row = _loss_grad("loss", x4, target)

    dx3, dg_fpre1, dg_fpost1, d_wg1, d_wu1, d_wd1 = _ffn_block_bwd(
        "l1", dx4, ffn1, row(g_ffn_pre, 1), row(g_ffn_post, 1), w["ffn_w_gate"][:, 1], w["ffn_w_up"][:, 1], w["ffn_w_down"][:, 1])
    du1, dg_mpost1 = _norm_bwd("l1_mix_post_bwd", [dx3], u1, row(g_mix_post, 1))
    d_dil_wo = _proj_bwd_dw("l1_dwo", o1, du1)
    do1 = _proj_bwd_do("l1_do", du1, dil_wo)
    dd = _head_rowdot(do1, o1)
    dqkv1 = _dil_bwd(qkv1, _to_groups(do1), _to_groups(dd), _to_groups(lse), slopes)
    d_dil_wqkv = _qkv_bwd_dw("l1_dwqkv", h2g, dqkv1, w["dil_w_qkv"].shape[2])
    dh2g = _qkv_bwd_dh("l1_dh", dqkv1, w["dil_w_qkv"])
    dx2, dg_mpre1 = _norm_bwd("l1_mix_pre_bwd", [_unperm(dh2g[i], d) for i, d in enumerate(DIL)], x2, row(g_mix_pre, 1), res=dx3)

    dx1, dg_fpre0, dg_fpost0, d_wg0, d_wu0, d_wd0 = _ffn_block_bwd(
        "l0", dx2, ffn0, row(g_ffn_pre, 0), row(g_ffn_post, 0), w["ffn_w_gate"][:, 0], w["ffn_w_up"][:, 0], w["ffn_w_down"][:, 0])
    du0, dg_mpost0 = _norm_bwd("l0_mix_post_bwd", [dx1], u0, row(g_mix_post, 0))
    d_na_wo = _proj_bwd_dw("l0_dwo", o0, du0)
    do0 = _proj_bwd_do("l0_do", du0, na_wo)
    dqkv0, z = _na_bwd(qkv0[0], bias, do0)
    d_rpb = _rpb_grad(z)
    d_na_wqkv = _qkv_bwd_dw("l0_dwqkv", h0[None], dqkv0[None], w["na_w_qkv"].shape[2])
    dh0 = _qkv_bwd_dh("l0_dh", dqkv0[None], w["na_w_qkv"])
    dx0, dg_mpre0 = _norm_bwd("l0_mix_pre_bwd", [dh0[0]], x, row(g_mix_pre, 0), res=dx1)

    dnorms = (jnp.concatenate([dg_mpre0, dg_mpre1]), jnp.concatenate([dg_mpost0, dg_mpost1]),
              jnp.concatenate([dg_fpre0, dg_fpre1]), jnp.concatenate([dg_fpost0, dg_fpost1]))
    dw = {
        "na_w_qkv": d_na_wqkv, "na_w_o": d_na_wo.reshape(NCHIP, DM // NCHIP, DM),
        "dil_w_qkv": d_dil_wqkv, "dil_w_o": d_dil_wo.reshape(NCHIP, DM // NCHIP, DM),
        "ffn_w_gate": jnp.stack([d_wg0, d_wg1], axis=1), "ffn_w_up": jnp.stack([d_wu0, d_wu1], axis=1),
        "ffn_w_down": jnp.stack([d_wd0, d_wd1], axis=1),
    }
    return loss_row, dx0, dnorms, d_rpb, dw


WEIGHT_NAMES = ("na_w_qkv", "na_w_o", "ffn_w_gate", "ffn_w_up", "ffn_w_down", "dil_w_qkv", "dil_w_o")
HBM_SPEC = pl.BlockSpec(memory_space=pltpu.HBM)


def _place():
    x, y, c = lax.axis_index("x"), lax.axis_index("y"), lax.axis_index("c")
    chips = ((1 - x, y), (x, 1 - y), (1 - x, 1 - y))
    return x, y, c, chips


def _chip_id(chip):
    return 2 * chip[0] + chip[1]


def _comm_call(name, body, ins, out_shapes, n_sems):
    return pl.pallas_call(
        body, in_specs=[HBM_SPEC] * len(ins), out_specs=[HBM_SPEC] * len(out_shapes), out_shape=out_shapes,
        scratch_shapes=[pltpu.SemaphoreType.DMA((k,)) for k in n_sems],
        compiler_params=pltpu.CompilerParams(has_side_effects=True), name=name)(*ins)


def _gather_weights(shards):
    n = len(shards)

    def body(*refs):
        src, out = refs[:n], refs[n:2 * n]
        send_sems, recv_sems, local_sems = refs[2 * n:]
        x, y, c, chips = _place()
        me = _chip_id((x, y))
        sibling = (x, y, 1 - c)

        def copy(t, k, chip, half, to, from_src=False):
            blk = out[t].at[_chip_id(chip), half]
            return pltpu.make_async_remote_copy(
                src_ref=src[t].at[half] if from_src else blk, dst_ref=blk,
                send_sem=send_sems.at[6 * t + k], recv_sem=recv_sems.at[6 * t + k], device_id=to, device_id_type=MESH)

        own = [pltpu.make_async_copy(src[t], out[t].at[me], local_sems.at[t]) for t in range(n)]
        for cp in own:
            cp.start()
        first = [copy(t, j, (x, y), c, (*chip, c), from_src=True) for t in range(n) for j, chip in enumerate(chips)]
        for cp in first:
            cp.start()
        passed = []
        for t in range(n):
            for j, chip in enumerate(chips):
                copy(t, j, chip, c, (x, y, c)).wait_recv()
                fwd = copy(t, 3 + j, chip, c, sibling)
                fwd.start()
                passed.append(fwd)
        for t in range(n):
            for j, chip in enumerate(chips):
                copy(t, 3 + j, chip, 1 - c, (x, y, c)).wait_recv()
        for cp in first + passed:
            cp.wait_send()
        for cp in own:
            cp.wait()

    return _comm_call("gather_weights", body, shards, [jax.ShapeDtypeStruct((NCHIP,) + s.shape, s.dtype) for s in shards], (6 * n, 6 * n, n))


def _pair_exchange(grads):
    n = len(grads)

    def body(*refs):
        g, mine, theirs = refs[:n], refs[n:2 * n], refs[2 * n:3 * n]
        send_sems, recv_sems, local_sems = refs[3 * n:]
        x, y, c, _ = _place()
        keep = [pltpu.make_async_copy(g[t].at[:, c], mine[t], local_sems.at[t]) for t in range(n)]
        swap = [pltpu.make_async_remote_copy(src_ref=g[t].at[:, 1 - c], dst_ref=theirs[t], send_sem=send_sems.at[t],
                                             recv_sem=recv_sems.at[t], device_id=(x, y, 1 - c), device_id_type=MESH) for t in range(n)]
        for cp in swap + keep:
            cp.start()
        for cp in swap + keep:
            cp.wait()

    half = [jax.ShapeDtypeStruct((NCHIP,) + g.shape[2:], g.dtype) for g in grads]
    outs = _comm_call("grad_pair_exchange", body, grads, half + half, (n, n, n))
    return outs[:n], outs[n:]


def _chip_exchange(parts):
    n = len(parts)

    def body(*refs):
        p, slots = refs[:n], refs[n:2 * n]
        send_sems, recv_sems, local_sems = refs[2 * n:]
        x, y, c, chips = _place()
        me = _chip_id((x, y))
        keep = [pltpu.make_async_copy(p[t].at[me], slots[t].at[me], local_sems.at[t]) for t in range(n)]
        sends = [pltpu.make_async_remote_copy(src_ref=p[t].at[_chip_id(chip)], dst_ref=slots[t].at[me], send_sem=send_sems.at[3 * t + j],
                                              recv_sem=recv_sems.at[3 * t + j], device_id=(*chip, c), device_id_type=MESH)
                 for t in range(n) for j, chip in enumerate(chips)]
        for cp in sends + keep:
            cp.start()
        for t in range(n):
            for j, chip in enumerate(chips):
                pltpu.make_async_remote_copy(src_ref=p[t].at[me], dst_ref=slots[t].at[_chip_id(chip)], send_sem=send_sems.at[3 * t + j],
                                             recv_sem=recv_sems.at[3 * t + j], device_id=(*chip, c), device_id_type=MESH).wait_recv()
        for cp in sends:
            cp.wait_send()
        for cp in keep:
            cp.wait()

    return _comm_call("grad_chip_exchange", body, parts, [jax.ShapeDtypeStruct(p.shape, p.dtype) for p in parts], (3 * n, 3 * n, n))


def _pair_share(halves):
    n = len(halves)

    def body(*refs):
        h, full = refs[:n], refs[n:2 * n]
        send_sems, recv_sems, local_sems = refs[2 * n:]
        x, y, c, _ = _place()
        keep = [pltpu.make_async_copy(h[t], full[t].at[c], local_sems.at[t]) for t in range(n)]
        sends = [pltpu.make_async_remote_copy(src_ref=h[t], dst_ref=full[t].at[c], send_sem=send_sems.at[t], recv_sem=recv_sems.at[t],
                                              device_id=(x, y, 1 - c), device_id_type=MESH) for t in range(n)]
        for cp in sends + keep:
            cp.start()
        for t in range(n):
            pltpu.make_async_remote_copy(src_ref=h[t], dst_ref=full[t].at[1 - c], send_sem=send_sems.at[t], recv_sem=recv_sems.at[t],
                                         device_id=(x, y, 1 - c), device_id_type=MESH).wait_recv()
        for cp in sends:
            cp.wait_send()
        for cp in keep:
            cp.wait()

    return _comm_call("grad_pair_share", body, halves, [jax.ShapeDtypeStruct((2,) + h.shape, h.dtype) for h in halves], (n, n, n))


SMALL_ROWS = 128


def _allreduce_small(v):
    def body(v_ref, o_ref, buf, send_sems, recv_sems):
        x, y, c, _ = _place()
        me = 4 * x + 2 * y + c
        flip = lambda a, f: 1 - a if f else a
        buf[me] = v_ref[...]
        peers = [(flip(x, d >> 2 & 1), flip(y, d >> 1 & 1), flip(c, d & 1)) for d in range(1, 8)]
        sends = [pltpu.make_async_remote_copy(src_ref=v_ref, dst_ref=buf.at[me], send_sem=send_sems.at[i], recv_sem=recv_sems.at[i],
                                              device_id=peer, device_id_type=MESH) for i, peer in enumerate(peers)]
        for cp in sends:
            cp.start()
        for i, (px, py, pc) in enumerate(peers):
            pltpu.make_async_remote_copy(src_ref=v_ref, dst_ref=buf.at[4 * px + 2 * py + pc], send_sem=send_sems.at[i], recv_sem=recv_sems.at[i],
                                         device_id=(px, py, pc), device_id_type=MESH).wait_recv()
        for cp in sends:
            cp.wait_send()
        acc = buf[0]
        for k in range(1, 8):
            acc = acc + buf[k]
        o_ref[...] = acc

    vm = pl.BlockSpec(memory_space=pltpu.VMEM)
    return pl.pallas_call(
        body, in_specs=[vm], out_specs=vm, out_shape=jax.ShapeDtypeStruct((SMALL_ROWS, 128), F32),
        scratch_shapes=[pltpu.VMEM((8, SMALL_ROWS, 128), F32), pltpu.SemaphoreType.DMA((7,)), pltpu.SemaphoreType.DMA((7,))],
        compiler_params=pltpu.CompilerParams(has_side_effects=True), name="allreduce_small")(v)


def _row_block(rows, cols, budget=1 << 20):
    best = 8
    for bm in range(8, rows + 1, 8):
        if rows % bm == 0 and bm * cols * 4 <= budget:
            best = bm
    return best


def _pair_sum(name, a, b):
    _, m, c = a.shape
    bm = _row_block(m, c)

    def body(a_ref, b_ref, o_ref):
        o_ref[...] = (a_ref[...].astype(F32) + b_ref[...].astype(F32)).astype(o_ref.dtype)

    spec = pl.BlockSpec((None, bm, c), lambda k, i: (k, i, 0))
    return pl.pallas_call(body, grid=(NCHIP, m // bm), in_specs=[spec, spec], out_specs=spec, out_shape=jax.ShapeDtypeStruct(a.shape, BF16),
                          compiler_params=_params(("parallel", "parallel")), name=name)(a, b)


def _chip_sum(name, slots):
    _, m, c = slots.shape
    bm = _row_block(m, c)

    def body(s_ref, o_ref):
        s = s_ref[...].astype(F32)
        o_ref[...] = ((s[0] + s[1]) + s[2]) + s[3]

    return pl.pallas_call(body, grid=(m // bm,), in_specs=[pl.BlockSpec((NCHIP, bm, c), lambda i: (0, i, 0))],
                          out_specs=pl.BlockSpec((bm, c), lambda i: (i, 0)), out_shape=jax.ShapeDtypeStruct((m, c), F32),
                          compiler_params=_params(("parallel",)), name=name)(slots)


def _adamw(name, w, g, m, v):
    rows, cols = w.shape
    bm = _row_block(rows, cols, budget=768 * 1024)
    c1 = 1.0 - ADAM_B1 ** ADAM_STEP
    c2 = 1.0 - ADAM_B2 ** ADAM_STEP

    def body(w_ref, g_ref, m_ref, v_ref, go_ref, d_ref, mo_ref, vo_ref):
        g = g_ref[...]
        mn = ADAM_B1 * m_ref[...] + (1.0 - ADAM_B1) * g
        vn = ADAM_B2 * v_ref[...] + (1.0 - ADAM_B2) * (g * g)
        go_ref[...] = g
        mo_ref[...] = mn
        vo_ref[...] = vn
        d_ref[...] = -ADAM_LR * ((mn / c1) / (jnp.sqrt(vn / c2) + ADAM_EPS) + ADAM_WD * w_ref[...])

    spec = pl.BlockSpec((bm, cols), lambda i: (i, 0))
    sh = jax.ShapeDtypeStruct((rows, cols), F32)
    return pl.pallas_call(body, grid=(rows // bm,), in_specs=[spec] * 4, out_specs=[spec] * 4, out_shape=[sh] * 4,
                          compiler_params=_params(("parallel",)), name=name)(w, g, m, v)


def _pack_small(norms, rpb):
    flat = jnp.concatenate([a.reshape(-1) for a in norms] + [rpb.reshape(-1)])
    return jnp.pad(flat, (0, SMALL_ROWS * 128 - flat.shape[0])).reshape(SMALL_ROWS, 128)


def _unpack_small(p):
    flat = p.reshape(-1)
    norms = [flat[i * 2 * DM:(i + 1) * 2 * DM].reshape(2, DM) for i in range(4)]
    rpb = flat[8 * DM:8 * DM + NH * 15 * 31].reshape(1, NH, 15, 31)
    return norms, rpb


def kernel(x, norm_mix_pre, norm_mix_post, norm_ffn_pre, norm_ffn_post, na_w_qkv, na_w_o, na_rpb, dil_w_qkv, dil_w_o, ffn_w_gate, ffn_w_up, ffn_w_down, loss_target, m_norm_mix_pre, m_norm_mix_post, m_norm_ffn_pre, m_norm_ffn_post, m_na_w_qkv, m_na_w_o, m_na_rpb, m_dil_w_qkv, m_dil_w_o, m_ffn_w_gate, m_ffn_w_up, m_ffn_w_down, v_norm_mix_pre, v_norm_mix_post, v_norm_ffn_pre, v_norm_ffn_post, v_na_w_qkv, v_na_w_o, v_na_rpb, v_dil_w_qkv, v_dil_w_o, v_ffn_w_gate, v_ffn_w_up, v_ffn_w_down):
    weights = {"na_w_qkv": na_w_qkv[0], "na_w_o": na_w_o[0], "dil_w_qkv": dil_w_qkv[0], "dil_w_o": dil_w_o[0],
               "ffn_w_gate": ffn_w_gate, "ffn_w_up": ffn_w_up, "ffn_w_down": ffn_w_down}
    m_in = {"na_w_qkv": m_na_w_qkv[0], "na_w_o": m_na_w_o[0], "dil_w_qkv": m_dil_w_qkv[0], "dil_w_o": m_dil_w_o[0],
            "ffn_w_gate": m_ffn_w_gate, "ffn_w_up": m_ffn_w_up, "ffn_w_down": m_ffn_w_down}
    v_in = {"na_w_qkv": v_na_w_qkv[0], "na_w_o": v_na_w_o[0], "dil_w_qkv": v_dil_w_qkv[0], "dil_w_o": v_dil_w_o[0],
            "ffn_w_gate": v_ffn_w_gate, "ffn_w_up": v_ffn_w_up, "ffn_w_down": v_ffn_w_down}
    halves = lambda a: a.reshape(2, -1, a.shape[-1])
    flat2 = lambda a: a.reshape(-1, a.shape[-1])

    gathered = _gather_weights([halves(weights[n]).astype(BF16) for n in WEIGHT_NAMES])
    w = {}
    for n, gw in zip(WEIGHT_NAMES, gathered):
        w[n] = gw if weights[n].ndim == 3 else gw.reshape((NCHIP,) + weights[n].shape)

    norms = (norm_mix_pre, norm_mix_post, norm_ffn_pre, norm_ffn_post)
    loss_row, dx, dnorms, d_rpb, dw = _local_step(x[0], loss_target[0], norms, na_rpb[0], w)
    loss = lax.psum(loss_row[0, 0], ("x", "y", "c"))

    grads = [dw[n].reshape((NCHIP, 2, -1, dw[n].shape[-1])) for n in WEIGHT_NAMES]
    mine, theirs = _pair_exchange(grads)
    parts = [_pair_sum(f"pair_sum_{n}", a, b) for n, a, b in zip(WEIGHT_NAMES, mine, theirs)]
    slots = _chip_exchange(parts)
    sums = [_chip_sum(f"chip_sum_{n}", s) for n, s in zip(WEIGHT_NAMES, slots)]
    full = _pair_share(sums)
    small = _allreduce_small(_pack_small(dnorms, d_rpb))

    out_g, out_d, out_m, out_v = {}, {}, {}, {}
    for n, gf in zip(WEIGHT_NAMES, full):
        shp = weights[n].shape
        res = _adamw(f"adamw_{n}", flat2(weights[n]), flat2(gf), flat2(m_in[n]), flat2(v_in[n]))
        lead = (1,) if weights[n].ndim == 2 else ()
        out_g[n], out_d[n], out_m[n], out_v[n] = (r.reshape(lead + shp) for r in res)
    sm_names = ("norm_mix_pre", "norm_mix_post", "norm_ffn_pre", "norm_ffn_post", "na_rpb")
    sm = _adamw("adamw_small", _pack_small(norms, na_rpb),
                small, _pack_small((m_norm_mix_pre, m_norm_mix_post, m_norm_ffn_pre, m_norm_ffn_post), m_na_rpb),
                _pack_small((v_norm_mix_pre, v_norm_mix_post, v_norm_ffn_pre, v_norm_ffn_post), v_na_rpb))
    for res, dst in zip(sm, (out_g, out_d, out_m, out_v)):
        ns, rp = _unpack_small(res)
        for n, a in zip(sm_names, ns + [rp]):
            dst[n] = a

    order = ("norm_mix_pre", "norm_mix_post", "norm_ffn_pre", "norm_ffn_post", "na_w_qkv", "na_w_o", "na_rpb", "dil_w_qkv", "dil_w_o",
             "ffn_w_gate", "ffn_w_up", "ffn_w_down")
    return (loss, dx[None], *[out_g[n] for n in order], *[out_d[n] for n in order], *[out_m[n] for n in order], *[out_v[n] for n in order])
```

```python
import functools

import numpy as np
import jax
import jax.numpy as jnp
from jax import lax
from jax.experimental import pallas as pl
from jax.experimental.pallas import tpu as pltpu

F32 = jnp.float32
BF16 = jnp.bfloat16

SEQ = 2048
DM = 1024
NH = 16
HD = 64
DFF = 2816
NCHIP = 4
FSH = DFF // NCHIP
GRID_W = 64
NA_QROWS = 4
NA_QB = NA_QROWS * GRID_W
NA_WROWS = 12
NA_WIN = NA_WROWS * GRID_W
DIL = (1, 4, 16)
DIL_QB = 128
DIL_WIN = 384
DIL_RADIUS = 64
RMS_EPS = 1e-6
NEG = -1e30
QSCALE = HD ** -0.5
CH = 256
MESH = pl.DeviceIdType.MESH

ADAM_LR, ADAM_B1, ADAM_B2, ADAM_EPS, ADAM_WD, ADAM_STEP = 0.001, 0.9, 0.999, 1e-08, 0.01, 10

VMEM_LIMIT = 56 * 1024 * 1024

_NN = (((1,), (0,)), ((), ()))
_NT = (((1,), (1,)), ((), ()))
_TN = (((0,), (0,)), ((), ()))


def _params(sem):
    return pltpu.CompilerParams(dimension_semantics=sem, vmem_limit_bytes=VMEM_LIMIT)


def _matmul(name, pairs, grid, out_shape, out_spec, acc_shape):
    nk = grid[-1]
    npair = len(pairs)

    def body(*refs):
        ins, o_ref = refs[:2 * npair], refs[2 * npair]
        part = None
        for p in range(npair):
            d = lax.dot_general(ins[2 * p][...].astype(BF16), ins[2 * p + 1][...].astype(BF16), pairs[p][4],
                                preferred_element_type=F32)
            part = d if part is None else part + d
        if nk == 1:
            o_ref[...] = part.astype(o_ref.dtype)
        else:
            acc_ref = refs[2 * npair + 1]
            kk = pl.program_id(len(grid) - 1)

            @pl.when(kk == 0)
            def _():
                acc_ref[...] = part

            @pl.when(kk > 0)
            def _():
                acc_ref[...] += part

            @pl.when(kk == nk - 1)
            def _():
                o_ref[...] = acc_ref[...].astype(o_ref.dtype)

    ops, specs = [], []
    for a, a_spec, b, b_spec, _ in pairs:
        ops += [a, b]
        specs += [a_spec, b_spec]
    return pl.pallas_call(
        body, grid=grid, in_specs=specs, out_specs=out_spec, out_shape=out_shape,
        scratch_shapes=[] if nk == 1 else [pltpu.VMEM(acc_shape, F32)],
        compiler_params=_params(("parallel",) * (len(grid) - 1) + ("arbitrary",)), name=name,
    )(*ops)


def _qkv_fwd(name, h_all, w4):
    g_n = h_all.shape[0]
    per = w4.shape[2] // CH
    return _matmul(
        name, [(h_all, pl.BlockSpec((None, SEQ, DM), lambda g, q, k: (g, 0, 0)),
                w4, pl.BlockSpec((None, DM, CH), lambda g, q, k: ((g * 12 + q) // per, 0, (g * 12 + q) % per)), _NN)],
        (g_n, 12, 1), jax.ShapeDtypeStruct((g_n, SEQ, 3 * DM), BF16),
        pl.BlockSpec((None, SEQ, CH), lambda g, q, k: (g, 0, q)), None)


def _qkv_bwd_dh(name, dqkv, w4):
    g_n = dqkv.shape[0]
    per = w4.shape[2] // CH
    tm = 1024
    return _matmul(
        name, [(dqkv, pl.BlockSpec((None, None, tm, CH), lambda g, i, q: (g, q // 4, i, q % 4)),
                w4, pl.BlockSpec((None, DM, CH), lambda g, i, q: ((g * 12 + q) // per, 0, (g * 12 + q) % per)), _NT)],
        (g_n, SEQ // tm, 12), jax.ShapeDtypeStruct((g_n, SEQ, DM), F32),
        pl.BlockSpec((None, tm, DM), lambda g, i, q: (g, i, 0)), (tm, DM))


def _qkv_bwd_dw(name, h_all, dqkv, shard_cols):
    g_n = dqkv.shape[0]
    per = shard_cols // CH
    tk = 512
    return _matmul(
        name, [(h_all, pl.BlockSpec((None, tk, DM), lambda qq, k: (qq // 12, k, 0)),
                dqkv, pl.BlockSpec((None, None, tk, CH), lambda qq, k: (qq // 12, (qq % 12) // 4, k, qq % 4)), _TN)],
        (g_n * 12, SEQ // tk), jax.ShapeDtypeStruct((NCHIP, DM, shard_cols), BF16),
        pl.BlockSpec((None, DM, CH), lambda qq, k: (qq // per, 0, qq % per)), (DM, CH))


def _proj_fwd(name, o, wo):
    tm = 512
    return _matmul(
        name, [(o, pl.BlockSpec((tm, DM), lambda i, k: (i, 0)), wo, pl.BlockSpec((DM, DM), lambda i, k: (0, 0)), _NN)],
        (SEQ // tm, 1), jax.ShapeDtypeStruct((SEQ, DM), F32), pl.BlockSpec((tm, DM), lambda i, k: (i, 0)), None)


def _proj_bwd_do(name, du, wo):
    tm = 512
    return _matmul(
        name, [(du, pl.BlockSpec((tm, DM), lambda i, k: (i, 0)), wo, pl.BlockSpec((DM, DM), lambda i, k: (0, 0)), _NT)],
        (SEQ // tm, 1), jax.ShapeDtypeStruct((SEQ, DM), BF16), pl.BlockSpec((tm, DM), lambda i, k: (i, 0)), None)


def _proj_bwd_dw(name, o, du):
    tk, tn = 512, 512
    return _matmul(
        name, [(o, pl.BlockSpec((tk, DM), lambda j, k: (k, 0)), du, pl.BlockSpec((tk, tn), lambda j, k: (k, j)), _TN)],
        (DM // tn, SEQ // tk), jax.ShapeDtypeStruct((DM, DM), BF16), pl.BlockSpec((DM, tn), lambda j, k: (0, j)), (DM, tn))


def _ffn_in(name, h, w4):
    tm = 1024
    return _matmul(
        name, [(h, pl.BlockSpec((tm, DM), lambda i, s, k: (i, 0)), w4, pl.BlockSpec((None, DM, FSH), lambda i, s, k: (s, 0, 0)), _NN)],
        (SEQ // tm, NCHIP, 1), jax.ShapeDtypeStruct((NCHIP, SEQ, FSH), BF16),
        pl.BlockSpec((None, tm, FSH), lambda i, s, k: (s, i, 0)), None)


def _ffn_out(name, act, wd4):
    tm = 1024
    return _matmul(
        name, [(act, pl.BlockSpec((None, tm, FSH), lambda i, s: (s, i, 0)), wd4, pl.BlockSpec((None, FSH, DM), lambda i, s: (s, 0, 0)), _NN)],
        (SEQ // tm, NCHIP), jax.ShapeDtypeStruct((SEQ, DM), F32), pl.BlockSpec((tm, DM), lambda i, s: (i, 0)), (tm, DM))


def _ffn_bwd_dact(name, du, wd4):
    tm = 1024
    return _matmul(
        name, [(du, pl.BlockSpec((tm, DM), lambda i, s, k: (i, 0)), wd4, pl.BlockSpec((None, FSH, DM), lambda i, s, k: (s, 0, 0)), _NT)],
        (SEQ // tm, NCHIP, 1), jax.ShapeDtypeStruct((NCHIP, SEQ, FSH), BF16),
        pl.BlockSpec((None, tm, FSH), lambda i, s, k: (s, i, 0)), None)


def _ffn_bwd_dwd(name, act, du):
    tk = 512
    return _matmul(
        name, [(act, pl.BlockSpec((None, tk, FSH), lambda s, k: (s, k, 0)), du, pl.BlockSpec((tk, DM), lambda s, k: (k, 0)), _TN)],
        (NCHIP, SEQ // tk), jax.ShapeDtypeStruct((NCHIP, FSH, DM), BF16), pl.BlockSpec((None, FSH, DM), lambda s, k: (s, 0, 0)), (FSH, DM))


def _ffn_bwd_dh(name, dgate, wg4, dup, wu4):
    tm = 1024
    a_spec = pl.BlockSpec((None, tm, FSH), lambda i, s: (s, i, 0))
    b_spec = pl.BlockSpec((None, DM, FSH), lambda i, s: (s, 0, 0))
    return _matmul(
        name, [(dgate, a_spec, wg4, b_spec, _NT), (dup, a_spec, wu4, b_spec, _NT)],
        (SEQ // tm, NCHIP), jax.ShapeDtypeStruct((SEQ, DM), F32), pl.BlockSpec((tm, DM), lambda i, s: (i, 0)), (tm, DM))


def _ffn_bwd_dwin(name, h, dg):
    tk = 512
    return _matmul(
        name, [(h, pl.BlockSpec((tk, DM), lambda s, k: (k, 0)), dg, pl.BlockSpec((None, tk, FSH), lambda s, k: (s, k, 0)), _TN)],
        (NCHIP, SEQ // tk), jax.ShapeDtypeStruct((NCHIP, DM, FSH), BF16), pl.BlockSpec((None, DM, FSH), lambda s, k: (s, 0, 0)), (DM, FSH))


ROWS = 256


def _row_spec():
    return pl.BlockSpec((ROWS, DM), lambda i: (i, 0))


def _vec_spec():
    return pl.BlockSpec((1, DM), lambda i: (0, 0))


def _rms_fwd(name, x, g):
    def body(x_ref, g_ref, o_ref):
        x = x_ref[...]
        r = lax.rsqrt(jnp.mean(x * x, axis=-1, keepdims=True) + RMS_EPS)
        o_ref[...] = (x * r * g_ref[...]).astype(o_ref.dtype)

    return pl.pallas_call(body, grid=(SEQ // ROWS,), in_specs=[_row_spec(), _vec_spec()], out_specs=_row_spec(),
                          out_shape=jax.ShapeDtypeStruct((SEQ, DM), BF16), compiler_params=_params(("parallel",)), name=name)(x, g)


def _resid_norm(name, x, u, g):
    def body(x_ref, u_ref, g_ref, o_ref):
        u = u_ref[...]
        r = lax.rsqrt(jnp.mean(u * u, axis=-1, keepdims=True) + RMS_EPS)
        o_ref[...] = x_ref[...] + u * r * g_ref[...]

    return pl.pallas_call(body, grid=(SEQ // ROWS,), in_specs=[_row_spec(), _row_spec(), _vec_spec()], out_specs=_row_spec(),
                          out_shape=jax.ShapeDtypeStruct((SEQ, DM), F32), compiler_params=_params(("parallel",)), name=name)(x, u, g)


def _norm_bwd(name, dys, u, g, res=None):
    ndy = len(dys)

    def body(*refs):
        dy = refs[0][...]
        for r_ in refs[1:ndy]:
            dy = dy + r_[...]
        u_ref, g_ref = refs[ndy], refs[ndy + 1]
        res_ref = refs[ndy + 2] if res is not None else None
        du_ref, dg_ref = refs[-2], refs[-1]
        u = u_ref[...]
        r = lax.rsqrt(jnp.mean(u * u, axis=-1, keepdims=True) + RMS_EPS)
        yh = u * r
        t = dy * g_ref[...]
        du = r * (t - yh * jnp.mean(t * yh, axis=-1, keepdims=True))
        if res_ref is not None:
            du = du + res_ref[...]
        du_ref[...] = du

        @pl.when(pl.program_id(0) == 0)
        def _():
            dg_ref[...] = jnp.zeros_like(dg_ref)

        dg_ref[...] += jnp.sum(dy * yh, axis=0, keepdims=True)

    ops = list(dys) + [u, g] + ([res] if res is not None else [])
    specs = [_row_spec()] * ndy + [_row_spec(), _vec_spec()] + ([_row_spec()] if res is not None else [])
    return pl.pallas_call(
        body, grid=(SEQ // ROWS,), in_specs=specs, out_specs=[_row_spec(), _vec_spec()],
        out_shape=[jax.ShapeDtypeStruct((SEQ, DM), F32), jax.ShapeDtypeStruct((1, DM), F32)],
        compiler_params=_params(("arbitrary",)), name=name)(*ops)


def _loss_grad(name, y, t):
    def body(y_ref, t_ref, dy_ref, l_ref):
        e = y_ref[...] - t_ref[...]
        dy_ref[...] = e * (1.0 / DM)

        @pl.when(pl.program_id(0) == 0)
        def _():
            l_ref[...] = jnp.zeros_like(l_ref)

        l_ref[...] += jnp.sum(e * e) * (0.5 / DM)

    return pl.pallas_call(
        body, grid=(SEQ // ROWS,), in_specs=[_row_spec(), _row_spec()],
        out_specs=[_row_spec(), pl.BlockSpec((1, 128), lambda i: (0, 0))],
        out_shape=[jax.ShapeDtypeStruct((SEQ, DM), F32), jax.ShapeDtypeStruct((1, 128), F32)],
        compiler_params=_params(("arbitrary",)), name=name)(y, t)


def _ffn_spec():
    return pl.BlockSpec((None, 512, FSH), lambda s, i: (s, i, 0))


def _swiglu_fwd(name, gate, up):
    def body(g_ref, u_ref, o_ref):
        g = g_ref[...].astype(F32)
        o_ref[...] = (g * jax.nn.sigmoid(g) * u_ref[...].astype(F32)).astype(o_ref.dtype)

    return pl.pallas_call(body, grid=(NCHIP, SEQ // 512), in_specs=[_ffn_spec(), _ffn_spec()], out_specs=_ffn_spec(),
                          out_shape=jax.ShapeDtypeStruct((NCHIP, SEQ, FSH), BF16),
                          compiler_params=_params(("parallel", "parallel")), name=name)(gate, up)


def _swiglu_bwd(name, dact, gate, up):
    def body(d_ref, g_ref, u_ref, dg_ref, du_ref):
        d = d_ref[...].astype(F32)
        g = g_ref[...].astype(F32)
        u = u_ref[...].astype(F32)
        sg = jax.nn.sigmoid(g)
        dg_ref[...] = (d * u * sg * (1.0 + g * (1.0 - sg))).astype(dg_ref.dtype)
        du_ref[...] = (d * g * sg).astype(du_ref.dtype)

    sh = jax.ShapeDtypeStruct((NCHIP, SEQ, FSH), BF16)
    return pl.pallas_call(body, grid=(NCHIP, SEQ // 512), in_specs=[_ffn_spec()] * 3, out_specs=[_ffn_spec()] * 2,
                          out_shape=[sh, sh], compiler_params=_params(("parallel", "parallel")), name=name)(dact, gate, up)


NA_BLOCKS = SEQ // NA_QB
NA_ROWS_TOTAL = SEQ // GRID_W
NA_CLASSES = ((0, 0), (8, 4), (NA_ROWS_TOTAL - NA_QROWS, NA_ROWS_TOTAL - NA_WROWS))


def _na_pairs(i0, ws):
    out = []
    for qi in range(NA_QROWS):
        i = i0 + qi
        rs = min(max(i - 4, 0), NA_ROWS_TOTAL - 8)
        for kr in range(NA_WROWS):
            r = ws + kr
            if rs <= r < rs + 8:
                out.append((qi, kr, r - i + 7))
    return out


def _na_bias_tiles(rpb):
    col = np.arange(GRID_W)
    col_start = np.clip(col - 8, 0, GRID_W - 16)
    col_mask = (col[None, :] >= col_start[:, None]) & (col[None, :] < col_start[:, None] + 16)
    col_idx = np.clip(col[None, :] - col[:, None] + 15, 0, 30)
    rc = jnp.where(col_mask[None, None], rpb[:, :, col_idx], NEG)
    neg = jnp.full((NH, GRID_W, GRID_W), NEG, F32)
    tiles = []
    for i0, ws in NA_CLASSES:
        pairs = {(qi, kr): dr for qi, kr, dr in _na_pairs(i0, ws)}
        rows = [jnp.concatenate([rc[:, pairs[(qi, kr)]] if (qi, kr) in pairs else neg for kr in range(NA_WROWS)], axis=2)
                for qi in range(NA_QROWS)]
        tiles.append(jnp.concatenate(rows, axis=1))
    return jnp.stack(tiles)


def _na_cls(b):
    return jnp.where(b == 0, 0, jnp.where(b == NA_BLOCKS - 1, 2, 1))


def _na_start(b):
    return pl.multiple_of(jnp.clip(b * NA_QROWS - 4, 0, NA_ROWS_TOTAL - NA_WROWS) * GRID_W, GRID_W)


def _na_in_specs():
    return [pl.BlockSpec((NA_QB, 128), lambda hp, b: (b, hp)),
            pl.BlockSpec((SEQ, 128), lambda hp, b: (0, 8 + hp)),
            pl.BlockSpec((SEQ, 128), lambda hp, b: (0, 16 + hp)),
            pl.BlockSpec((None, 2, NA_QB, NA_WIN), lambda hp, b: (_na_cls(b), hp, 0, 0))]


def _na_fwd(qkv, bias):
    def body(q_ref, k_ref, v_ref, b_ref, o_ref):
        start = _na_start(pl.program_id(1))
        q = q_ref[...]
        kw = k_ref[pl.ds(start, NA_WIN), :]
        vw = v_ref[pl.ds(start, NA_WIN), :]
        outs = []
        for hh in range(2):
            sl = slice(hh * HD, (hh + 1) * HD)
            s = lax.dot_general(q[:, sl] * QSCALE, kw[:, sl], _NT, preferred_element_type=F32) + b_ref[hh]
            p = jnp.exp(s - jnp.max(s, axis=-1, keepdims=True))
            l = jnp.sum(p, axis=-1, keepdims=True)
            outs.append(jnp.dot(p.astype(BF16), vw[:, sl], preferred_element_type=F32) / l)
        o_ref[...] = jnp.concatenate(outs, axis=1).astype(o_ref.dtype)

    return pl.pallas_call(
        body, grid=(NH // 2, NA_BLOCKS), in_specs=_na_in_specs(), out_specs=pl.BlockSpec((NA_QB, 128), lambda hp, b: (b, hp)),
        out_shape=jax.ShapeDtypeStruct((SEQ, DM), BF16), compiler_params=_params(("parallel", "arbitrary")), name="na_fwd")(qkv, qkv, qkv, bias)


def _na_bwd(qkv, bias, do):
    def body(q_ref, k_ref, v_ref, b_ref, do_ref, dqkv_ref, z_ref, dk_acc, dv_acc):
        blk = pl.program_id(1)

        @pl.when(blk == 0)
        def _():
            dk_acc[...] = jnp.zeros_like(dk_acc)
            dv_acc[...] = jnp.zeros_like(dv_acc)
            z_ref[...] = jnp.zeros_like(z_ref)

        start = _na_start(blk)
        q = q_ref[...]
        do = do_ref[...]
        kw = k_ref[pl.ds(start, NA_WIN), :]
        vw = v_ref[pl.ds(start, NA_WIN), :]
        dqs, dks, dvs = [], [], []
        for hh in range(2):
            sl = slice(hh * HD, (hh + 1) * HD)
            qh = q[:, sl] * QSCALE
            s = lax.dot_general(qh, kw[:, sl], _NT, preferred_element_type=F32) + b_ref[hh]
            p = jnp.exp(s - jnp.max(s, axis=-1, keepdims=True))
            p = p / jnp.sum(p, axis=-1, keepdims=True)
            dp = lax.dot_general(do[:, sl], vw[:, sl], _NT, preferred_element_type=F32)
            ds = p * (dp - jnp.sum(p * dp, axis=-1, keepdims=True))
            dsb = ds.astype(BF16)
            dqs.append(jnp.dot(dsb, kw[:, sl], preferred_element_type=F32) * QSCALE)
            dks.append(lax.dot_general(dsb, qh, _TN, preferred_element_type=F32))
            dvs.append(lax.dot_general(p.astype(BF16), do[:, sl], _TN, preferred_element_type=F32))
            for cls, (i0, ws) in enumerate(NA_CLASSES):
                @pl.when(_na_cls(blk) == cls)
                def _(ds=ds, hh=hh, i0=i0, ws=ws):
                    for qi, kr, dr in _na_pairs(i0, ws):
                        z_ref[hh, dr * GRID_W:(dr + 1) * GRID_W, :] += ds[qi * GRID_W:(qi + 1) * GRID_W, kr * GRID_W:(kr + 1) * GRID_W]
        dqkv_ref[0, pl.ds(pl.multiple_of(blk * NA_QB, NA_QB), NA_QB), :] = jnp.concatenate(dqs, axis=1).astype(dqkv_ref.dtype)
        dk_acc[pl.ds(start, NA_WIN), :] += jnp.concatenate(dks, axis=1)
        dv_acc[pl.ds(start, NA_WIN), :] += jnp.concatenate(dvs, axis=1)

        @pl.when(blk == NA_BLOCKS - 1)
        def _():
            dqkv_ref[1] = dk_acc[...].astype(dqkv_ref.dtype)
            dqkv_ref[2] = dv_acc[...].astype(dqkv_ref.dtype)

    return pl.pallas_call(
        body, grid=(NH // 2, NA_BLOCKS),
        in_specs=_na_in_specs() + [pl.BlockSpec((NA_QB, 128), lambda hp, b: (b, hp))],
        out_specs=[pl.BlockSpec((3, SEQ, 128), lambda hp, b: (0, 0, hp)), pl.BlockSpec((2, 15 * GRID_W, GRID_W), lambda hp, b: (hp, 0, 0))],
        out_shape=[jax.ShapeDtypeStruct((3, SEQ, DM), BF16), jax.ShapeDtypeStruct((NH, 15 * GRID_W, GRID_W), F32)],
        scratch_shapes=[pltpu.VMEM((SEQ, 128), F32), pltpu.VMEM((SEQ, 128), F32)],
        compiler_params=_params(("parallel", "arbitrary")), name="na_bwd")(qkv, qkv, qkv, bias, do)


def _diag_onehot():
    qc, kc = np.meshgrid(np.arange(GRID_W), np.arange(GRID_W), indexing="ij")
    e = np.zeros((GRID_W * GRID_W, 128), np.float32)
    j = (kc - qc + 15).reshape(-1)
    ok = (j >= 0) & (j <= 30)
    e[np.arange(GRID_W * GRID_W)[ok], j[ok]] = 1.0
    return jnp.asarray(e)


def _rpb_grad(z):
    z2 = z.reshape(NH * 15, GRID_W * GRID_W)

    def body(z_ref, e_ref, o_ref):
        o_ref[...] = jnp.dot(z_ref[...], e_ref[...], preferred_element_type=F32, precision=lax.Precision.HIGHEST)

    out = pl.pallas_call(body, out_shape=jax.ShapeDtypeStruct((NH * 15, 128), F32), name="rpb_grad",
                         compiler_params=pltpu.CompilerParams(vmem_limit_bytes=VMEM_LIMIT))(z2, _diag_onehot())
    return out[:, :31].reshape(NH, 15, 31)


DIL_BLOCKS = SEQ // DIL_QB


def _perm(a, d):
    return a if d == 1 else a.reshape(SEQ // d, d, a.shape[-1]).transpose(1, 0, 2).reshape(SEQ, a.shape[-1])


def _unperm(a, d):
    return a if d == 1 else a.reshape(d, SEQ // d, a.shape[-1]).transpose(1, 0, 2).reshape(SEQ, a.shape[-1])


def _to_groups(a):
    return jnp.stack([_perm(a, d) for d in DIL])


def _dil_start(b):
    return pl.multiple_of(jnp.clip(b - 1, 0, DIL_BLOCKS - 3) * DIL_QB, DIL_QB)


def _dil_mask(g, b, start):
    shift = 11 - 2 * g
    ii = b * DIL_QB + lax.broadcasted_iota(jnp.int32, (DIL_QB, DIL_WIN), 0)
    jj = start + lax.broadcasted_iota(jnp.int32, (DIL_QB, DIL_WIN), 1)
    dist = jnp.abs(ii - jj)
    valid = (dist <= DIL_RADIUS) & (jnp.right_shift(ii, shift) == jnp.right_shift(jj, shift))
    return valid, dist.astype(F32)


def _dil_in_specs():
    return [pl.BlockSpec(memory_space=pltpu.SMEM),
            pl.BlockSpec((None, DIL_QB, 128), lambda g, hp, b: (g, b, hp)),
            pl.BlockSpec((None, SEQ, 128), lambda g, hp, b: (g, 0, 8 + hp)),
            pl.BlockSpec((None, SEQ, 128), lambda g, hp, b: (g, 0, 16 + hp))]


def _dil_fwd(qkv, slopes):
    def body(sl_ref, q_ref, k_ref, v_ref, o_ref, lse_ref):
        g, hp, b = pl.program_id(0), pl.program_id(1), pl.program_id(2)
        start = _dil_start(b)
        valid, dist = _dil_mask(g, b, start)
        dil = jnp.left_shift(1, 2 * g).astype(F32)
        q = q_ref[...]
        kw = k_ref[pl.ds(start, DIL_WIN), :]
        vw = v_ref[pl.ds(start, DIL_WIN), :]
        outs, lses = [], []
        for hh in range(2):
            sl = slice(hh * HD, (hh + 1) * HD)
            s = lax.dot_general(q[:, sl] * QSCALE, kw[:, sl], _NT, preferred_element_type=F32)
            s = jnp.where(valid, s - (sl_ref[hp * 2 + hh] * dil) * dist, NEG)
            m = jnp.max(s, axis=-1, keepdims=True)
            p = jnp.exp(s - m)
            l = jnp.sum(p, axis=-1, keepdims=True)
            outs.append(jnp.dot(p.astype(BF16), vw[:, sl], preferred_element_type=F32) / l)
            lses.append(jnp.broadcast_to(m + jnp.log(l), (DIL_QB, HD)))
        o_ref[...] = jnp.concatenate(outs, axis=1)
        lse_ref[...] = jnp.concatenate(lses, axis=1)

    ospec = pl.BlockSpec((None, DIL_QB, 128), lambda g, hp, b: (g, b, hp))
    sh = jax.ShapeDtypeStruct((3, SEQ, DM), F32)
    return pl.pallas_call(
        body, grid=(3, NH // 2, DIL_BLOCKS), in_specs=_dil_in_specs(), out_specs=[ospec, ospec], out_shape=[sh, sh],
        compiler_params=_params(("parallel", "parallel", "arbitrary")), name="dil_fwd")(slopes, qkv, qkv, qkv)


def _dil_merge(o_all, lse_all):
    def body(o_ref, l_ref, out_ref, lse_ref):
        l = l_ref[...]
        m = jnp.max(l, axis=0)
        w = jnp.exp(l - m[None])
        sw = jnp.sum(w, axis=0)
        out_ref[...] = (jnp.sum(w * o_ref[...], axis=0) / sw).astype(out_ref.dtype)
        lse_ref[...] = m + jnp.log(sw)

    gspec = pl.BlockSpec((3, ROWS, DM), lambda i: (0, i, 0))
    return pl.pallas_call(
        body, grid=(SEQ // ROWS,), in_specs=[gspec, gspec], out_specs=[_row_spec(), _row_spec()],
        out_shape=[jax.ShapeDtypeStruct((SEQ, DM), BF16), jax.ShapeDtypeStruct((SEQ, DM), F32)],
        compiler_params=_params(("parallel",)), name="dil_merge")(o_all, lse_all)


def _head_rowdot(do, o):
    def body(a_ref, b_ref, o_ref):
        prod = a_ref[...].astype(F32) * b_ref[...].astype(F32)
        o_ref[...] = jnp.concatenate(
            [jnp.broadcast_to(jnp.sum(prod[:, h * HD:(h + 1) * HD], axis=-1, keepdims=True), (ROWS, HD)) for h in range(NH)], axis=1)

    return pl.pallas_call(body, grid=(SEQ // ROWS,), in_specs=[_row_spec(), _row_spec()], out_specs=_row_spec(),
                          out_shape=jax.ShapeDtypeStruct((SEQ, DM), F32), compiler_params=_params(("parallel",)), name="head_rowdot")(do, o)


def _dil_bwd(qkv, do, dd, lse, slopes):
    def body(sl_ref, q_ref, k_ref, v_ref, do_ref, dd_ref, lse_ref, dqkv_ref, dk_acc, dv_acc):
        g, hp, b = pl.program_id(0), pl.program_id(1), pl.program_id(2)

        @pl.when(b == 0)
        def _():
            dk_acc[...] = jnp.zeros_like(dk_acc)
            dv_acc[...] = jnp.zeros_like(dv_acc)

        start = _dil_start(b)
        valid, dist = _dil_mask(g, b, start)
        dil = jnp.left_shift(1, 2 * g).astype(F32)
        q = q_ref[...]
        do = do_ref[...]
        kw = k_ref[pl.ds(start, DIL_WIN), :]
        vw = v_ref[pl.ds(start, DIL_WIN), :]
        lse = lse_ref[...]
        dd = dd_ref[...]
        dqs, dks, dvs = [], [], []
        for hh in range(2):
            sl = slice(hh * HD, (hh + 1) * HD)
            qh = q[:, sl] * QSCALE
            s = lax.dot_general(qh, kw[:, sl], _NT, preferred_element_type=F32)
            s = jnp.where(valid, s - (sl_ref[hp * 2 + hh] * dil) * dist, NEG)
            p = jnp.exp(s - lse[:, hh * HD:hh * HD + 1])
            dp = lax.dot_general(do[:, sl], vw[:, sl], _NT, preferred_element_type=F32)
            dsb = (p * (dp - dd[:, hh * HD:hh * HD + 1])).astype(BF16)
            dqs.append(jnp.dot(dsb, kw[:, sl], preferred_element_type=F32) * QSCALE)
            dks.append(lax.dot_general(dsb, qh, _TN, preferred_element_type=F32))
            dvs.append(lax.dot_general(p.astype(BF16), do[:, sl], _TN, preferred_element_type=F32))
        dqkv_ref[0, pl.ds(pl.multiple_of(b * DIL_QB, DIL_QB), DIL_QB), :] = jnp.concatenate(dqs, axis=1).astype(dqkv_ref.dtype)
        dk_acc[pl.ds(start, DIL_WIN), :] += jnp.concatenate(dks, axis=1)
        dv_acc[pl.ds(start, DIL_WIN), :] += jnp.concatenate(dvs, axis=1)

        @pl.when(b == DIL_BLOCKS - 1)
        def _():
            dqkv_ref[1] = dk_acc[...].astype(dqkv_ref.dtype)
            dqkv_ref[2] = dv_acc[...].astype(dqkv_ref.dtype)

    rspec = pl.BlockSpec((None, DIL_QB, 128), lambda g, hp, b: (g, b, hp))
    return pl.pallas_call(
        body, grid=(3, NH // 2, DIL_BLOCKS), in_specs=_dil_in_specs() + [rspec, rspec, rspec],
        out_specs=pl.BlockSpec((None, 3, SEQ, 128), lambda g, hp, b: (g, 0, 0, hp)),
        out_shape=jax.ShapeDtypeStruct((3, 3, SEQ, DM), BF16),
        scratch_shapes=[pltpu.VMEM((SEQ, 128), F32), pltpu.VMEM((SEQ, 128), F32)],
        compiler_params=_params(("parallel", "parallel", "arbitrary")), name="dil_bwd")(slopes, qkv, qkv, qkv, do, dd, lse)


def _ffn_block(tag, x, g_pre, g_post, wg4, wu4, wd4):
    h = _rms_fwd(f"{tag}_ffn_pre", x, g_pre)
    gate = _ffn_in(f"{tag}_gate", h, wg4)
    up = _ffn_in(f"{tag}_up", h, wu4)
    act = _swiglu_fwd(f"{tag}_swiglu", gate, up)
    u = _ffn_out(f"{tag}_down", act, wd4)
    return _resid_norm(f"{tag}_ffn_post", x, u, g_post), (x, h, gate, up, act, u)


def _ffn_block_bwd(tag, dx, saved, g_pre, g_post, wg4, wu4, wd4):
    x, h, gate, up, act, u = saved
    du, dg_post = _norm_bwd(f"{tag}_ffn_post_bwd", [dx], u, g_post)
    d_wd = _ffn_bwd_dwd(f"{tag}_dwd", act, du)
    dact = _ffn_bwd_dact(f"{tag}_dact", du, wd4)
    dgate, dup = _swiglu_bwd(f"{tag}_swiglu_bwd", dact, gate, up)
    d_wg = _ffn_bwd_dwin(f"{tag}_dwg", h, dgate)
    d_wu = _ffn_bwd_dwin(f"{tag}_dwu", h, dup)
    dh = _ffn_bwd_dh(f"{tag}_ffn_dh", dgate, wg4, dup, wu4)
    dx_in, dg_pre = _norm_bwd(f"{tag}_ffn_pre_bwd", [dh], x, g_pre, res=dx)
    return dx_in, dg_pre, dg_post, d_wg, d_wu, d_wd


def _alibi_slopes():
    return 2.0 ** (-8.0 * jnp.arange(1, NH + 1, dtype=F32) / NH)


def _local_step(x, target, norms, rpb, w):
    g_mix_pre, g_mix_post, g_ffn_pre, g_ffn_post = norms
    row = lambda a, i: a[i:i + 1]
    na_wo = w["na_w_o"].reshape(DM, DM)
    dil_wo = w["dil_w_o"].reshape(DM, DM)

    bias = _na_bias_tiles(rpb)
    h0 = _rms_fwd("l0_mix_pre", x, row(g_mix_pre, 0))
    qkv0 = _qkv_fwd("l0_qkv", h0[None], w["na_w_qkv"])
    o0 = _na_fwd(qkv0[0], bias)
    u0 = _proj_fwd("l0_proj", o0, na_wo)
    x1 = _resid_norm("l0_mix_post", x, u0, row(g_mix_post, 0))
    x2, ffn0 = _ffn_block("l0", x1, row(g_ffn_pre, 0), row(g_ffn_post, 0), w["ffn_w_gate"][:, 0], w["ffn_w_up"][:, 0], w["ffn_w_down"][:, 0])

    slopes = _alibi_slopes()
    h2 = _rms_fwd("l1_mix_pre", x2, row(g_mix_pre, 1))
    h2g = _to_groups(h2)
    qkv1 = _qkv_fwd("l1_qkv", h2g, w["dil_w_qkv"])
    og, lg = _dil_fwd(qkv1, slopes)
    o1, lse = _dil_merge(jnp.stack([_unperm(og[i], d) for i, d in enumerate(DIL)]), jnp.stack([_unperm(lg[i], d) for i, d in enumerate(DIL)]))
    u1 = _proj_fwd("l1_proj", o1, dil_wo)
    x3 = _resid_norm("l1_mix_post", x2, u1, row(g_mix_post, 1))
    x4, ffn1 = _ffn_block("l1", x3, row(g_ffn_pre, 1), row(g_ffn_post, 1), w["ffn_w_gate"][:, 1], w["ffn_w_up"][:, 1], w["ffn_w_down"][:, 1])

    dx4, loss_row = _loss_grad("loss", x4, target)

    dx3, dg_fpre1, dg_fpost1, d_wg1, d_wu1, d_wd1 = _ffn_block_bwd(
        "l1", dx4, ffn1, row(g_ffn_pre, 1), row(g_ffn_post, 1), w["ffn_w_gate"][:, 1], w["ffn_w_up"][:, 1], w["ffn_w_down"][:, 1])
    du1, dg_mpost1 = _norm_bwd("l1_mix_post_bwd", [dx3], u1, row(g_mix_post, 1))
    d_dil_wo = _proj_bwd_dw("l1_dwo", o1, du1)
    do1 = _proj_bwd_do("l1_do", du1, dil_wo)
    dd = _head_rowdot(do1, o1)
    dqkv1 = _dil_bwd(qkv1, _to_groups(do1), _to_groups(dd), _to_groups(lse), slopes)
    d_dil_wqkv = _qkv_bwd_dw("l1_dwqkv", h2g, dqkv1, w["dil_w_qkv"].shape[2])
    dh2g = _qkv_bwd_dh("l1_dh", dqkv1, w["dil_w_qkv"])
    dx2, dg_mpre1 = _norm_bwd("l1_mix_pre_bwd", [_unperm(dh2g[i], d) for i, d in enumerate(DIL)], x2, row(g_mix_pre, 1), res=dx3)

    dx1, dg_fpre0, dg_fpost0, d_wg0, d_wu0, d_wd0 = _ffn_block_bwd(
        "l0", dx2, ffn0, row(g_ffn_pre, 0), row(g_ffn_post, 0), w["ffn_w_gate"][:, 0], w["ffn_w_up"][:, 0], w["ffn_w_down"][:, 0])
    du0, dg_mpost0 = _norm_bwd("l0_mix_post_bwd", [dx1], u0, row(g_mix_post, 0))
    d_na_wo = _proj_bwd_dw("l0_dwo", o0, du0)
    do0 = _proj_bwd_do("l0_do", du0, na_wo)
    dqkv0, z = _na_bwd(qkv0[0], bias, do0)
    d_rpb = _rpb_grad(z)
    d_na_wqkv = _qkv_bwd_dw("l0_dwqkv", h0[None], dqkv0[None], w["na_w_qkv"].shape[2])
    dh0 = _qkv_bwd_dh("l0_dh", dqkv0[None], w["na_w_qkv"])
    dx0, dg_mpre0 = _norm_bwd("l0_mix_pre_bwd", [dh0[0]], x, row(g_mix_pre, 0), res=dx1)

    dnorms = (jnp.concatenate([dg_mpre0, dg_mpre1]), jnp.concatenate([dg_mpost0, dg_mpost1]),
              jnp.concatenate([dg_fpre0, dg_fpre1]), jnp.concatenate([dg_fpost0, dg_fpost1]))
    dw = {
        "na_w_qkv": d_na_wqkv, "na_w_o": d_na_wo.reshape(NCHIP, DM // NCHIP, DM),
        "dil_w_qkv": d_dil_wqkv, "dil_w_o": d_dil_wo.reshape(NCHIP, DM // NCHIP, DM),
        "ffn_w_gate": jnp.stack([d_wg0, d_wg1], axis=1), "ffn_w_up": jnp.stack([d_wu0, d_wu1], axis=1),
        "ffn_w_down": jnp.stack([d_wd0, d_wd1], axis=1),
    }
    return loss_row, dx0, dnorms, d_rpb, dw


WEIGHT_NAMES = ("na_w_qkv", "na_w_o", "ffn_w_gate", "ffn_w_up", "ffn_w_down", "dil_w_qkv", "dil_w_o")
HBM_SPEC = pl.BlockSpec(memory_space=pltpu.HBM)


def _place():
    x, y, c = lax.axis_index("x"), lax.axis_index("y"), lax.axis_index("c")
    chips = ((1 - x, y), (x, 1 - y), (1 - x, 1 - y))
    return x, y, c, chips


def _chip_id(chip):
    return 2 * chip[0] + chip[1]


def _comm_call(name, body, ins, out_shapes, n_sems, aliases=None):
    return pl.pallas_call(
        body, in_specs=[HBM_SPEC] * len(ins), out_specs=[HBM_SPEC] * len(out_shapes), out_shape=out_shapes,
        scratch_shapes=[pltpu.SemaphoreType.DMA((k,)) for k in n_sems], input_output_aliases=aliases or {},
        compiler_params=pltpu.CompilerParams(has_side_effects=True), name=name)(*ins)


def _gather_weights(shards):
    n = len(shards)

    def body(*refs):
        src, out = refs[:n], refs[n:2 * n]
        send_sems, recv_sems = refs[2 * n:]
        x, y, c, chips = _place()
        sibling = (x, y, 1 - c)

        def copy(t, k, chip, half, to, from_src=False):
            blk = out[t].at[_chip_id(chip), half]
            return pltpu.make_async_remote_copy(
                src_ref=src[t].at[half] if from_src else blk, dst_ref=blk,
                send_sem=send_sems.at[6 * t + k], recv_sem=recv_sems.at[6 * t + k], device_id=to, device_id_type=MESH)

        first = [copy(t, j, (x, y), c, (*chip, c), from_src=True) for t in range(n) for j, chip in enumerate(chips)]
        for cp in first:
            cp.start()
        passed = []
        for t in range(n):
            for j, chip in enumerate(chips):
                copy(t, j, chip, c, (x, y, c)).wait_recv()
                fwd = copy(t, 3 + j, chip, c, sibling)
                fwd.start()
                passed.append(fwd)
        for t in range(n):
            for j, chip in enumerate(chips):
                copy(t, 3 + j, chip, 1 - c, (x, y, c)).wait_recv()
        for cp in first + passed:
            cp.wait_send()

    return _comm_call("gather_weights", body, shards, [jax.ShapeDtypeStruct((NCHIP,) + s.shape, s.dtype) for s in shards], (6 * n, 6 * n))


def _pair_exchange(grads):
    n = len(grads)

    def body(*refs):
        g, theirs = refs[:n], refs[n:2 * n]
        send_sems, recv_sems = refs[2 * n:]
        x, y, c, _ = _place()
        swap = [pltpu.make_async_remote_copy(src_ref=g[t].at[:, 1 - c], dst_ref=theirs[t], send_sem=send_sems.at[t],
                                             recv_sem=recv_sems.at[t], device_id=(x, y, 1 - c), device_id_type=MESH) for t in range(n)]
        for cp in swap:
            cp.start()
        for cp in swap:
            cp.wait()

    return _comm_call("grad_pair_exchange", body, grads, [jax.ShapeDtypeStruct((NCHIP,) + g.shape[2:], g.dtype) for g in grads], (n, n))


def _chip_exchange(parts):
    n = len(parts)

    def body(*refs):
        p, slots = refs[:n], refs[n:2 * n]
        send_sems, recv_sems = refs[2 * n:]
        x, y, c, chips = _place()

        def copy(t, j):
            return pltpu.make_async_remote_copy(src_ref=p[t].at[_chip_id(chips[j])], dst_ref=slots[t].at[j], send_sem=send_sems.at[3 * t + j],
                                                recv_sem=recv_sems.at[3 * t + j], device_id=(*chips[j], c), device_id_type=MESH)

        sends = [copy(t, j) for t in range(n) for j in range(3)]
        for cp in sends:
            cp.start()
        for cp in sends:
            cp.wait()

    return _comm_call("grad_chip_exchange", body, parts, [jax.ShapeDtypeStruct((3,) + p.shape[1:], p.dtype) for p in parts], (3 * n, 3 * n))


def _pair_share(full):
    n = len(full)

    def body(*refs):
        buf = refs[n:2 * n]
        send_sems, recv_sems = refs[2 * n:]
        x, y, c, _ = _place()
        sends = [pltpu.make_async_remote_copy(src_ref=buf[t].at[c], dst_ref=buf[t].at[c], send_sem=send_sems.at[t], recv_sem=recv_sems.at[t],
                                              device_id=(x, y, 1 - c), device_id_type=MESH) for t in range(n)]
        for cp in sends:
            cp.start()
        for t in range(n):
            pltpu.make_async_remote_copy(src_ref=buf[t].at[c], dst_ref=buf[t].at[1 - c], send_sem=send_sems.at[t], recv_sem=recv_sems.at[t],
                                         device_id=(x, y, 1 - c), device_id_type=MESH).wait_recv()
        for cp in sends:
            cp.wait_send()

    return _comm_call("grad_pair_share", body, full, [jax.ShapeDtypeStruct(f.shape, f.dtype) for f in full], (n, n),
                      aliases={t: t for t in range(n)})


SMALL_ROWS = 128


def _allreduce_small(v):
    def body(v_ref, o_ref, buf, send_sems, recv_sems):
        x, y, c, _ = _place()
        me = 4 * x + 2 * y + c
        flip = lambda a, f: 1 - a if f else a
        buf[me] = v_ref[...]
        peers = [(flip(x, d >> 2 & 1), flip(y, d >> 1 & 1), flip(c, d & 1)) for d in range(1, 8)]
        sends = [pltpu.make_async_remote_copy(src_ref=v_ref, dst_ref=buf.at[me], send_sem=send_sems.at[i], recv_sem=recv_sems.at[i],
                                              device_id=peer, device_id_type=MESH) for i, peer in enumerate(peers)]
        for cp in sends:
            cp.start()
        for i, (px, py, pc) in enumerate(peers):
            pltpu.make_async_remote_copy(src_ref=v_ref, dst_ref=buf.at[4 * px + 2 * py + pc], send_sem=send_sems.at[i], recv_sem=recv_sems.at[i],
                                         device_id=(px, py, pc), device_id_type=MESH).wait_recv()
        for cp in sends:
            cp.wait_send()
        acc = buf[0]
        for k in range(1, 8):
            acc = acc + buf[k]
        o_ref[...] = acc

    vm = pl.BlockSpec(memory_space=pltpu.VMEM)
    return pl.pallas_call(
        body, in_specs=[vm], out_specs=vm, out_shape=jax.ShapeDtypeStruct((SMALL_ROWS, 128), F32),
        scratch_shapes=[pltpu.VMEM((8, SMALL_ROWS, 128), F32), pltpu.SemaphoreType.DMA((7,)), pltpu.SemaphoreType.DMA((7,))],
        compiler_params=pltpu.CompilerParams(has_side_effects=True), name="allreduce_small")(v)


def _row_block(rows, cols, budget=1 << 20):
    best = 8
    for bm in range(8, rows + 1, 8):
        if rows % bm == 0 and bm * cols * 4 <= budget:
            best = bm
    return best


def _pair_sum(name, place, g, theirs):
    _, m, c = theirs.shape
    bm = _row_block(m, c)

    def body(place_ref, a_ref, b_ref, o_ref):
        o_ref[...] = (a_ref[...].astype(F32) + b_ref[...].astype(F32)).astype(o_ref.dtype)

    spec = pl.BlockSpec((None, bm, c), lambda k, i, pr: (k, i, 0))
    return pl.pallas_call(
        body, out_shape=jax.ShapeDtypeStruct(theirs.shape, BF16),
        grid_spec=pltpu.PrefetchScalarGridSpec(
            num_scalar_prefetch=1, grid=(NCHIP, m // bm),
            in_specs=[pl.BlockSpec((None, None, bm, c), lambda k, i, pr: (k, pr[0], i, 0)), spec], out_specs=spec),
        compiler_params=_params(("parallel", "parallel")), name=name)(place, g, theirs)


def _chip_sum(name, place, parts, slots):
    _, m, c = parts.shape
    bm = _row_block(m, c)

    def body(place_ref, p_ref, s_ref, o_ref):
        s = s_ref[...].astype(F32)
        o_ref[...] = ((p_ref[...].astype(F32) + s[0]) + s[1]) + s[2]

    return pl.pallas_call(
        body, out_shape=jax.ShapeDtypeStruct((2, m, c), F32),
        grid_spec=pltpu.PrefetchScalarGridSpec(
            num_scalar_prefetch=1, grid=(m // bm,),
            in_specs=[pl.BlockSpec((None, bm, c), lambda i, pr: (pr[1], i, 0)), pl.BlockSpec((3, bm, c), lambda i, pr: (0, i, 0))],
            out_specs=pl.BlockSpec((None, bm, c), lambda i, pr: (pr[0], i, 0))),
        compiler_params=_params(("parallel",)), name=name)(place, parts, slots)


def _adamw(name, w, g, m, v):
    rows, cols = w.shape
    bm = _row_block(rows, cols, budget=768 * 1024)
    c1 = 1.0 - ADAM_B1 ** ADAM_STEP
    c2 = 1.0 - ADAM_B2 ** ADAM_STEP

    def body(w_ref, g_ref, m_ref, v_ref, go_ref, d_ref, mo_ref, vo_ref):
        g = g_ref[...]
        mn = ADAM_B1 * m_ref[...] + (1.0 - ADAM_B1) * g
        vn = ADAM_B2 * v_ref[...] + (1.0 - ADAM_B2) * (g * g)
        go_ref[...] = g
        mo_ref[...] = mn
        vo_ref[...] = vn
        d_ref[...] = -ADAM_LR * ((mn / c1) / (jnp.sqrt(vn / c2) + ADAM_EPS) + ADAM_WD * w_ref[...])

    spec = pl.BlockSpec((bm, cols), lambda i: (i, 0))
    sh = jax.ShapeDtypeStruct((rows, cols), F32)
    return pl.pallas_call(body, grid=(rows // bm,), in_specs=[spec] * 4, out_specs=[spec] * 4, out_shape=[sh] * 4,
                          compiler_params=_params(("parallel",)), name=name)(w, g, m, v)


def _pack_small(norms, rpb):
    flat = jnp.concatenate([a.reshape(-1) for a in norms] + [rpb.reshape(-1)])
    return jnp.pad(flat, (0, SMALL_ROWS * 128 - flat.shape[0])).reshape(SMALL_ROWS, 128)


def _unpack_small(p):
    flat = p.reshape(-1)
    norms = [flat[i * 2 * DM:(i + 1) * 2 * DM].reshape(2, DM) for i in range(4)]
    rpb = flat[8 * DM:8 * DM + NH * 15 * 31].reshape(1, NH, 15, 31)
    return norms, rpb


def kernel(x, norm_mix_pre, norm_mix_post, norm_ffn_pre, norm_ffn_post, na_w_qkv, na_w_o, na_rpb, dil_w_qkv, dil_w_o, ffn_w_gate, ffn_w_up, ffn_w_down, loss_target, m_norm_mix_pre, m_norm_mix_post, m_norm_ffn_pre, m_norm_ffn_post, m_na_w_qkv, m_na_w_o, m_na_rpb, m_dil_w_qkv, m_dil_w_o, m_ffn_w_gate, m_ffn_w_up, m_ffn_w_down, v_norm_mix_pre, v_norm_mix_post, v_norm_ffn_pre, v_norm_ffn_post, v_na_w_qkv, v_na_w_o, v_na_rpb, v_dil_w_qkv, v_dil_w_o, v_ffn_w_gate, v_ffn_w_up, v_ffn_w_down):
    weights = {"na_w_qkv": na_w_qkv[0], "na_w_o": na_w_o[0], "dil_w_qkv": dil_w_qkv[0], "dil_w_o": dil_w_o[0],
               "ffn_w_gate": ffn_w_gate, "ffn_w_up": ffn_w_up, "ffn_w_down": ffn_w_down}
    m_in = {"na_w_qkv": m_na_w_qkv[0], "na_w_o": m_na_w_o[0], "dil_w_qkv": m_dil_w_qkv[0], "dil_w_o": m_dil_w_o[0],
            "ffn_w_gate": m_ffn_w_gate, "ffn_w_up": m_ffn_w_up, "ffn_w_down": m_ffn_w_down}
    v_in = {"na_w_qkv": v_na_w_qkv[0], "na_w_o": v_na_w_o[0], "dil_w_qkv": v_dil_w_qkv[0], "dil_w_o": v_dil_w_o[0],
            "ffn_w_gate": v_ffn_w_gate, "ffn_w_up": v_ffn_w_up, "ffn_w_down": v_ffn_w_down}
    halves = lambda a: a.reshape(2, -1, a.shape[-1])
    flat2 = lambda a: a.reshape(-1, a.shape[-1])

    chip = 2 * lax.axis_index("x") + lax.axis_index("y")
    place = jnp.stack([lax.axis_index("c"), chip]).astype(jnp.int32)
    own = [halves(weights[n]).astype(BF16) for n in WEIGHT_NAMES]
    gathered = _gather_weights(own)
    w = {}
    for n, gw, ow in zip(WEIGHT_NAMES, gathered, own):
        gw = lax.dynamic_update_slice(gw, ow[None], (chip, 0, 0, 0))
        w[n] = gw if weights[n].ndim == 3 else gw.reshape((NCHIP,) + weights[n].shape)

    norms = (norm_mix_pre, norm_mix_post, norm_ffn_pre, norm_ffn_post)
    loss_row, dx, dnorms, d_rpb, dw = _local_step(x[0], loss_target[0], norms, na_rpb[0], w)
    loss = lax.psum(loss_row[0, 0], ("x", "y", "c"))

    grads = [dw[n].reshape((NCHIP, 2, -1, dw[n].shape[-1])) for n in WEIGHT_NAMES]
    theirs = _pair_exchange(grads)
    parts = [_pair_sum(f"pair_sum_{n}", place, a, b) for n, a, b in zip(WEIGHT_NAMES, grads, theirs)]
    slots = _chip_exchange(parts)
    sums = [_chip_sum(f"chip_sum_{n}", place, p, s) for n, p, s in zip(WEIGHT_NAMES, parts, slots)]
    full = _pair_share(sums)
    small = _allreduce_small(_pack_small(dnorms, d_rpb))

    out_g, out_d, out_m, out_v = {}, {}, {}, {}
    for n, gf in zip(WEIGHT_NAMES, full):
        shp = weights[n].shape
        res = _adamw(f"adamw_{n}", flat2(weights[n]), flat2(gf), flat2(m_in[n]), flat2(v_in[n]))
        lead = (1,) if weights[n].ndim == 2 else ()
        out_g[n], out_d[n], out_m[n], out_v[n] = (r.reshape(lead + shp) for r in res)
    sm_names = ("norm_mix_pre", "norm_mix_post", "norm_ffn_pre", "norm_ffn_post", "na_rpb")
    sm = _adamw("adamw_small", _pack_small(norms, na_rpb),
                small, _pack_small((m_norm_mix_pre, m_norm_mix_post, m_norm_ffn_pre, m_norm_ffn_post), m_na_rpb),
                _pack_small((v_norm_mix_pre, v_norm_mix_post, v_norm_ffn_pre, v_norm_ffn_post), v_na_rpb))
    for res, dst in zip(sm, (out_g, out_d, out_m, out_v)):
        ns, rp = _unpack_small(res)
        for n, a in zip(sm_names, ns + [rp]):
            dst[n] = a

    order = ("norm_mix_pre", "norm_mix_post", "norm_ffn_pre", "norm_ffn_post", "na_w_qkv", "na_w_o", "na_rpb", "dil_w_qkv", "dil_w_o",
             "ffn_w_gate", "ffn_w_up", "ffn_w_down")
    return (loss, dx[None], *[out_g[n] for n in order], *[out_d[n] for n in order], *[out_m[n] for n in order], *[out_v[n] for n in order])
```

```python
import functools

import numpy as np
import jax
import jax.numpy as jnp
from jax import lax
from jax.experimental import pallas as pl
from jax.experimental.pallas import tpu as pltpu

F32 = jnp.float32
BF16 = jnp.bfloat16

SEQ = 2048
DM = 1024
NH = 16
HD = 64
DFF = 2816
NCHIP = 4
FSH = DFF // NCHIP
GRID_W = 64
NA_QROWS = 4
NA_QB = NA_QROWS * GRID_W
NA_WROWS = 12
NA_WIN = NA_WROWS * GRID_W
DIL = (1, 4, 16)
DIL_QB = 128
DIL_WIN = 384
DIL_RADIUS = 64
RMS_EPS = 1e-6
NEG = -1e30
QSCALE = HD ** -0.5
CH = 256
MESH = pl.DeviceIdType.MESH

ADAM_LR, ADAM_B1, ADAM_B2, ADAM_EPS, ADAM_WD, ADAM_STEP = 0.001, 0.9, 0.999, 1e-08, 0.01, 10

VMEM_LIMIT = 56 * 1024 * 1024

_NN = (((1,), (0,)), ((), ()))
_NT = (((1,), (1,)), ((), ()))
_TN = (((0,), (0,)), ((), ()))


def _params(sem):
    return pltpu.CompilerParams(dimension_semantics=sem, vmem_limit_bytes=VMEM_LIMIT)


def _matmul(name, pairs, grid, out_shape, out_spec, acc_shape, into=None):
    nk = grid[-1]
    npair = len(pairs)
    n_in = 2 * npair + (into is not None)

    def body(*refs):
        ins, o_ref = refs[:2 * npair], refs[n_in]
        part = None
        for p in range(npair):
            d = lax.dot_general(ins[2 * p][...].astype(BF16), ins[2 * p + 1][...].astype(BF16), pairs[p][4],
                                preferred_element_type=F32)
            part = d if part is None else part + d
        if nk == 1:
            o_ref[...] = part.astype(o_ref.dtype)
        else:
            acc_ref = refs[n_in + 1]
            kk = pl.program_id(len(grid) - 1)

            @pl.when(kk == 0)
            def _():
                acc_ref[...] = part

            @pl.when(kk > 0)
            def _():
                acc_ref[...] += part

            @pl.when(kk == nk - 1)
            def _():
                o_ref[...] = acc_ref[...].astype(o_ref.dtype)

    ops, specs = [], []
    for a, a_spec, b, b_spec, _ in pairs:
        ops += [a, b]
        specs += [a_spec, b_spec]
    if into is not None:
        ops.append(into)
        specs.append(pl.BlockSpec(memory_space=pl.ANY))
    return pl.pallas_call(
        body, grid=grid, in_specs=specs, out_specs=out_spec, out_shape=out_shape,
        input_output_aliases={} if into is None else {n_in - 1: 0},
        scratch_shapes=[] if nk == 1 else [pltpu.VMEM(acc_shape, F32)],
        compiler_params=_params(("parallel",) * (len(grid) - 1) + ("arbitrary",)), name=name,
    )(*ops)


def _qkv_fwd(name, h_all, w4):
    g_n = h_all.shape[0]
    per = w4.shape[2] // CH
    return _matmul(
        name, [(h_all, pl.BlockSpec((None, SEQ, DM), lambda g, q, k: (g, 0, 0)),
                w4, pl.BlockSpec((None, DM, CH), lambda g, q, k: ((g * 12 + q) // per, 0, (g * 12 + q) % per)), _NN)],
        (g_n, 12, 1), jax.ShapeDtypeStruct((g_n, SEQ, 3 * DM), BF16),
        pl.BlockSpec((None, SEQ, CH), lambda g, q, k: (g, 0, q)), None)


def _qkv_bwd_dh(name, dqkv, w4):
    g_n = dqkv.shape[0]
    per = w4.shape[2] // CH
    tm = 1024
    return _matmul(
        name, [(dqkv, pl.BlockSpec((None, None, tm, CH), lambda g, i, q: (g, q // 4, i, q % 4)),
                w4, pl.BlockSpec((None, DM, CH), lambda g, i, q: ((g * 12 + q) // per, 0, (g * 12 + q) % per)), _NT)],
        (g_n, SEQ // tm, 12), jax.ShapeDtypeStruct((g_n, SEQ, DM), F32),
        pl.BlockSpec((None, tm, DM), lambda g, i, q: (g, i, 0)), (tm, DM))


def _qkv_bwd_dw(name, h_all, dqkv, shard_cols):
    g_n = dqkv.shape[0]
    per = shard_cols // CH
    tk = 512
    return _matmul(
        name, [(h_all, pl.BlockSpec((None, tk, DM), lambda qq, k: (qq // 12, k, 0)),
                dqkv, pl.BlockSpec((None, None, tk, CH), lambda qq, k: (qq // 12, (qq % 12) // 4, k, qq % 4)), _TN)],
        (g_n * 12, SEQ // tk), jax.ShapeDtypeStruct((NCHIP, DM, shard_cols), BF16),
        pl.BlockSpec((None, DM, CH), lambda qq, k: (qq // per, 0, qq % per)), (DM, CH))


def _proj_fwd(name, o, wo):
    tm = 512
    return _matmul(
        name, [(o, pl.BlockSpec((tm, DM), lambda i, k: (i, 0)), wo, pl.BlockSpec((DM, DM), lambda i, k: (0, 0)), _NN)],
        (SEQ // tm, 1), jax.ShapeDtypeStruct((SEQ, DM), F32), pl.BlockSpec((tm, DM), lambda i, k: (i, 0)), None)


def _proj_bwd_do(name, du, wo, dtype=BF16):
    tm = 512
    return _matmul(
        name, [(du, pl.BlockSpec((tm, DM), lambda i, k: (i, 0)), wo, pl.BlockSpec((DM, DM), lambda i, k: (0, 0)), _NT)],
        (SEQ // tm, 1), jax.ShapeDtypeStruct((SEQ, DM), dtype), pl.BlockSpec((tm, DM), lambda i, k: (i, 0)), None)


def _proj_bwd_dw(name, o, du):
    tk, tn = 512, 512
    return _matmul(
        name, [(o, pl.BlockSpec((tk, DM), lambda j, k: (k, 0)), du, pl.BlockSpec((tk, tn), lambda j, k: (k, j)), _TN)],
        (DM // tn, SEQ // tk), jax.ShapeDtypeStruct((DM, DM), BF16), pl.BlockSpec((DM, tn), lambda j, k: (0, j)), (DM, tn))


def _ffn_in(name, h, w4, layer):
    tm = 1024
    return _matmul(
        name, [(h, pl.BlockSpec((tm, DM), lambda i, s, k: (i, 0)), w4, pl.BlockSpec((None, None, DM, FSH), lambda i, s, k: (s, layer, 0, 0)), _NN)],
        (SEQ // tm, NCHIP, 1), jax.ShapeDtypeStruct((NCHIP, SEQ, FSH), BF16),
        pl.BlockSpec((None, tm, FSH), lambda i, s, k: (s, i, 0)), None)


def _ffn_out(name, act, wd4, layer):
    tm = 1024
    return _matmul(
        name, [(act, pl.BlockSpec((None, tm, FSH), lambda i, s: (s, i, 0)), wd4, pl.BlockSpec((None, None, FSH, DM), lambda i, s: (s, layer, 0, 0)), _NN)],
        (SEQ // tm, NCHIP), jax.ShapeDtypeStruct((SEQ, DM), F32), pl.BlockSpec((tm, DM), lambda i, s: (i, 0)), (tm, DM))


def _ffn_bwd_dact(name, du, wd4, layer):
    tm = 1024
    return _matmul(
        name, [(du, pl.BlockSpec((tm, DM), lambda i, s, k: (i, 0)), wd4, pl.BlockSpec((None, None, FSH, DM), lambda i, s, k: (s, layer, 0, 0)), _NT)],
        (SEQ // tm, NCHIP, 1), jax.ShapeDtypeStruct((NCHIP, SEQ, FSH), BF16),
        pl.BlockSpec((None, tm, FSH), lambda i, s, k: (s, i, 0)), None)


def _ffn_bwd_dwd(name, act, du, layer, into):
    tk = 512
    return _matmul(
        name, [(act, pl.BlockSpec((None, tk, FSH), lambda s, k: (s, k, 0)), du, pl.BlockSpec((tk, DM), lambda s, k: (k, 0)), _TN)],
        (NCHIP, SEQ // tk), jax.ShapeDtypeStruct((NCHIP, 2, FSH, DM), BF16), pl.BlockSpec((None, None, FSH, DM), lambda s, k: (s, layer, 0, 0)),
        (FSH, DM), into=into)


def _ffn_bwd_dh(name, dgate, wg4, dup, wu4, layer):
    tm = 1024
    a_spec = pl.BlockSpec((None, tm, FSH), lambda i, s: (s, i, 0))
    b_spec = pl.BlockSpec((None, None, DM, FSH), lambda i, s: (s, layer, 0, 0))
    return _matmul(
        name, [(dgate, a_spec, wg4, b_spec, _NT), (dup, a_spec, wu4, b_spec, _NT)],
        (SEQ // tm, NCHIP), jax.ShapeDtypeStruct((SEQ, DM), F32), pl.BlockSpec((tm, DM), lambda i, s: (i, 0)), (tm, DM))


def _ffn_bwd_dwin(name, h, dg, layer, into):
    tk = 512
    return _matmul(
        name, [(h, pl.BlockSpec((tk, DM), lambda s, k: (k, 0)), dg, pl.BlockSpec((None, tk, FSH), lambda s, k: (s, k, 0)), _TN)],
        (NCHIP, SEQ // tk), jax.ShapeDtypeStruct((NCHIP, 2, DM, FSH), BF16), pl.BlockSpec((None, None, DM, FSH), lambda s, k: (s, layer, 0, 0)),
        (DM, FSH), into=into)


ROWS = 256


def _row_spec():
    return pl.BlockSpec((ROWS, DM), lambda i: (i, 0))


def _vec_spec():
    return pl.BlockSpec((1, DM), lambda i: (0, 0))


def _rms_fwd(name, x, g, dtype=BF16):
    def body(x_ref, g_ref, o_ref):
        x = x_ref[...]
        r = lax.rsqrt(jnp.mean(x * x, axis=-1, keepdims=True) + RMS_EPS)
        o_ref[...] = (x * r * g_ref[...]).astype(o_ref.dtype)

    return pl.pallas_call(body, grid=(SEQ // ROWS,), in_specs=[_row_spec(), _vec_spec()], out_specs=_row_spec(),
                          out_shape=jax.ShapeDtypeStruct((SEQ, DM), dtype), compiler_params=_params(("parallel",)), name=name)(x, g)


def _resid_norm(name, x, u, g):
    def body(x_ref, u_ref, g_ref, o_ref):
        u = u_ref[...]
        r = lax.rsqrt(jnp.mean(u * u, axis=-1, keepdims=True) + RMS_EPS)
        o_ref[...] = x_ref[...] + u * r * g_ref[...]

    return pl.pallas_call(body, grid=(SEQ // ROWS,), in_specs=[_row_spec(), _row_spec(), _vec_spec()], out_specs=_row_spec(),
                          out_shape=jax.ShapeDtypeStruct((SEQ, DM), F32), compiler_params=_params(("parallel",)), name=name)(x, u, g)


def _norm_bwd(name, dys, u, g, res=None):
    ndy = len(dys)

    def body(*refs):
        dy = refs[0][...]
        for r_ in refs[1:ndy]:
            dy = dy + r_[...]
        u_ref, g_ref = refs[ndy], refs[ndy + 1]
        res_ref = refs[ndy + 2] if res is not None else None
        du_ref, dg_ref = refs[-2], refs[-1]
        u = u_ref[...]
        r = lax.rsqrt(jnp.mean(u * u, axis=-1, keepdims=True) + RMS_EPS)
        yh = u * r
        t = dy * g_ref[...]
        du = r * (t - yh * jnp.mean(t * yh, axis=-1, keepdims=True))
        if res_ref is not None:
            du = du + res_ref[...]
        du_ref[...] = du

        @pl.when(pl.program_id(0) == 0)
        def _():
            dg_ref[...] = jnp.zeros_like(dg_ref)

        dg_ref[...] += jnp.sum(dy * yh, axis=0, keepdims=True)

    ops = list(dys) + [u, g] + ([res] if res is not None else [])
    specs = [_row_spec()] * ndy + [_row_spec(), _vec_spec()] + ([_row_spec()] if res is not None else [])
    return pl.pallas_call(
        body, grid=(SEQ // ROWS,), in_specs=specs, out_specs=[_row_spec(), _vec_spec()],
        out_shape=[jax.ShapeDtypeStruct((SEQ, DM), F32), jax.ShapeDtypeStruct((1, DM), F32)],
        compiler_params=_params(("arbitrary",)), name=name)(*ops)


def _loss_grad(name, y, t):
    def body(y_ref, t_ref, dy_ref, l_ref):
        e = y_ref[...] - t_ref[...]
        dy_ref[...] = e * (1.0 / DM)

        @pl.when(pl.program_id(0) == 0)
        def _():
            l_ref[...] = jnp.zeros_like(l_ref)

        l_ref[...] += jnp.sum(e * e) * (0.5 / DM)

    return pl.pallas_call(
        body, grid=(SEQ // ROWS,), in_specs=[_row_spec(), _row_spec()],
        out_specs=[_row_spec(), pl.BlockSpec((1, 128), lambda i: (0, 0))],
        out_shape=[jax.ShapeDtypeStruct((SEQ, DM), F32), jax.ShapeDtypeStruct((1, 128), F32)],
        compiler_params=_params(("arbitrary",)), name=name)(y, t)


def _ffn_spec():
    return pl.BlockSpec((None, 512, FSH), lambda s, i: (s, i, 0))


def _swiglu_fwd(name, gate, up):
    def body(g_ref, u_ref, o_ref):
        g = g_ref[...].astype(F32)
        o_ref[...] = (g * jax.nn.sigmoid(g) * u_ref[...].astype(F32)).astype(o_ref.dtype)

    return pl.pallas_call(body, grid=(NCHIP, SEQ // 512), in_specs=[_ffn_spec(), _ffn_spec()], out_specs=_ffn_spec(),
                          out_shape=jax.ShapeDtypeStruct((NCHIP, SEQ, FSH), BF16),
                          compiler_params=_params(("parallel", "parallel")), name=name)(gate, up)


def _swiglu_bwd(name, dact, gate, up):
    def body(d_ref, g_ref, u_ref, dg_ref, du_ref):
        d = d_ref[...].astype(F32)
        g = g_ref[...].astype(F32)
        u = u_ref[...].astype(F32)
        sg = jax.nn.sigmoid(g)
        dg_ref[...] = (d * u * sg * (1.0 + g * (1.0 - sg))).astype(dg_ref.dtype)
        du_ref[...] = (d * g * sg).astype(du_ref.dtype)

    sh = jax.ShapeDtypeStruct((NCHIP, SEQ, FSH), BF16)
    return pl.pallas_call(body, grid=(NCHIP, SEQ // 512), in_specs=[_ffn_spec()] * 3, out_specs=[_ffn_spec()] * 2,
                          out_shape=[sh, sh], compiler_params=_params(("parallel", "parallel")), name=name)(dact, gate, up)


NA_BLOCKS = SEQ // NA_QB
NA_ROWS_TOTAL = SEQ // GRID_W
NA_CLASSES = ((0, 0), (8, 4), (NA_ROWS_TOTAL - NA_QROWS, NA_ROWS_TOTAL - NA_WROWS))


def _na_pairs(i0, ws):
    out = []
    for qi in range(NA_QROWS):
        i = i0 + qi
        rs = min(max(i - 4, 0), NA_ROWS_TOTAL - 8)
        for kr in range(NA_WROWS):
            r = ws + kr
            if rs <= r < rs + 8:
                out.append((qi, kr, r - i + 7))
    return out


def _diag_onehot():
    qc, kc = np.meshgrid(np.arange(GRID_W), np.arange(GRID_W), indexing="ij")
    e = np.zeros((GRID_W * GRID_W, 128), np.float32)
    j = (kc - qc + 15).reshape(-1)
    ok = (j >= 0) & (j <= 30)
    e[np.arange(GRID_W * GRID_W)[ok], j[ok]] = 1.0
    return jnp.asarray(e)


def _rpb_expand(rpb):
    r2 = jnp.pad(rpb.reshape(NH * 15, 31), ((0, 0), (0, 128 - 31)))

    def body(r_ref, e_ref, o_ref):
        o_ref[...] = lax.dot_general(r_ref[...], e_ref[...], _NT, preferred_element_type=F32, precision=lax.Precision.HIGHEST)

    out = pl.pallas_call(body, out_shape=jax.ShapeDtypeStruct((NH * 15, GRID_W * GRID_W), F32), name="rpb_expand",
                         compiler_params=pltpu.CompilerParams(vmem_limit_bytes=VMEM_LIMIT))(r2, _diag_onehot())
    return out.reshape(NH, 15, GRID_W, GRID_W)


def _na_bias_tiles(rpb):
    col = np.arange(GRID_W)
    col_start = np.clip(col - 8, 0, GRID_W - 16)
    col_mask = (col[None, :] >= col_start[:, None]) & (col[None, :] < col_start[:, None] + 16)
    rc = jnp.where(col_mask[None, None], _rpb_expand(rpb), NEG)
    neg = jnp.full((NH, GRID_W, GRID_W), NEG, F32)
    tiles = []
    for i0, ws in NA_CLASSES:
        pairs = {(qi, kr): dr for qi, kr, dr in _na_pairs(i0, ws)}
        rows = [jnp.concatenate([rc[:, pairs[(qi, kr)]] if (qi, kr) in pairs else neg for kr in range(NA_WROWS)], axis=2)
                for qi in range(NA_QROWS)]
        tiles.append(jnp.concatenate(rows, axis=1))
    return jnp.stack(tiles)


def _na_cls(b):
    return jnp.where(b == 0, 0, jnp.where(b == NA_BLOCKS - 1, 2, 1))


def _na_start(b):
    return pl.multiple_of(jnp.clip(b * NA_QROWS - 4, 0, NA_ROWS_TOTAL - NA_WROWS) * GRID_W, GRID_W)


def _na_in_specs():
    return [pl.BlockSpec((NA_QB, 128), lambda hp, b: (b, hp)),
            pl.BlockSpec((SEQ, 128), lambda hp, b: (0, 8 + hp)),
            pl.BlockSpec((SEQ, 128), lambda hp, b: (0, 16 + hp)),
            pl.BlockSpec((None, 2, NA_QB, NA_WIN), lambda hp, b: (_na_cls(b), hp, 0, 0))]


def _na_fwd(qkv, bias):
    def body(q_ref, k_ref, v_ref, b_ref, o_ref):
        start = _na_start(pl.program_id(1))
        q = q_ref[...]
        kw = k_ref[pl.ds(start, NA_WIN), :]
        vw = v_ref[pl.ds(start, NA_WIN), :]
        outs = []
        for hh in range(2):
            sl = slice(hh * HD, (hh + 1) * HD)
            s = lax.dot_general(q[:, sl] * QSCALE, kw[:, sl], _NT, preferred_element_type=F32) + b_ref[hh]
            p = jnp.exp(s - jnp.max(s, axis=-1, keepdims=True))
            l = jnp.sum(p, axis=-1, keepdims=True)
            outs.append(jnp.dot(p.astype(BF16), vw[:, sl], preferred_element_type=F32) / l)
        o_ref[...] = jnp.concatenate(outs, axis=1).astype(o_ref.dtype)

    return pl.pallas_call(
        body, grid=(NH // 2, NA_BLOCKS), in_specs=_na_in_specs(), out_specs=pl.BlockSpec((NA_QB, 128), lambda hp, b: (b, hp)),
        out_shape=jax.ShapeDtypeStruct((SEQ, DM), BF16), compiler_params=_params(("parallel", "arbitrary")), name="na_fwd")(qkv, qkv, qkv, bias)


def _na_bwd(qkv, bias, do):
    def body(q_ref, k_ref, v_ref, b_ref, do_ref, dqkv_ref, z_ref, dk_acc, dv_acc):
        blk = pl.program_id(1)

        @pl.when(blk == 0)
        def _():
            dk_acc[...] = jnp.zeros_like(dk_acc)
            dv_acc[...] = jnp.zeros_like(dv_acc)
            z_ref[...] = jnp.zeros_like(z_ref)

        start = _na_start(blk)
        q = q_ref[...]
        do = do_ref[...]
        kw = k_ref[pl.ds(start, NA_WIN), :]
        vw = v_ref[pl.ds(start, NA_WIN), :]
        dqs, dks, dvs = [], [], []
        for hh in range(2):
            sl = slice(hh * HD, (hh + 1) * HD)
            qh = q[:, sl] * QSCALE
            s = lax.dot_general(qh, kw[:, sl], _NT, preferred_element_type=F32) + b_ref[hh]
            p = jnp.exp(s - jnp.max(s, axis=-1, keepdims=True))
            p = p / jnp.sum(p, axis=-1, keepdims=True)
            dp = lax.dot_general(do[:, sl], vw[:, sl], _NT, preferred_element_type=F32)
            ds = p * (dp - jnp.sum(p * dp, axis=-1, keepdims=True))
            dsb = ds.astype(BF16)
            dqs.append(jnp.dot(dsb, kw[:, sl], preferred_element_type=F32) * QSCALE)
            dks.append(lax.dot_general(dsb, qh, _TN, preferred_element_type=F32))
            dvs.append(lax.dot_general(p.astype(BF16), do[:, sl], _TN, preferred_element_type=F32))
            for cls, (i0, ws) in enumerate(NA_CLASSES):
                @pl.when(_na_cls(blk) == cls)
                def _(ds=ds, hh=hh, i0=i0, ws=ws):
                    for qi, kr, dr in _na_pairs(i0, ws):
                        z_ref[hh, dr * GRID_W:(dr + 1) * GRID_W, :] += ds[qi * GRID_W:(qi + 1) * GRID_W, kr * GRID_W:(kr + 1) * GRID_W]
        dqkv_ref[0, pl.ds(pl.multiple_of(blk * NA_QB, NA_QB), NA_QB), :] = jnp.concatenate(dqs, axis=1).astype(dqkv_ref.dtype)
        dk_acc[pl.ds(start, NA_WIN), :] += jnp.concatenate(dks, axis=1)
        dv_acc[pl.ds(start, NA_WIN), :] += jnp.concatenate(dvs, axis=1)

        @pl.when(blk == NA_BLOCKS - 1)
        def _():
            dqkv_ref[1] = dk_acc[...].astype(dqkv_ref.dtype)
            dqkv_ref[2] = dv_acc[...].astype(dqkv_ref.dtype)

    return pl.pallas_call(
        body, grid=(NH // 2, NA_BLOCKS),
        in_specs=_na_in_specs() + [pl.BlockSpec((NA_QB, 128), lambda hp, b: (b, hp))],
        out_specs=[pl.BlockSpec((3, SEQ, 128), lambda hp, b: (0, 0, hp)), pl.BlockSpec((2, 15 * GRID_W, GRID_W), lambda hp, b: (hp, 0, 0))],
        out_shape=[jax.ShapeDtypeStruct((3, SEQ, DM), BF16), jax.ShapeDtypeStruct((NH, 15 * GRID_W, GRID_W), F32)],
        scratch_shapes=[pltpu.VMEM((SEQ, 128), F32), pltpu.VMEM((SEQ, 128), F32)],
        compiler_params=_params(("parallel", "arbitrary")), name="na_bwd")(qkv, qkv, qkv, bias, do)


def _rpb_grad(z):
    z2 = z.reshape(NH * 15, GRID_W * GRID_W)

    def body(z_ref, e_ref, o_ref):
        o_ref[...] = jnp.dot(z_ref[...], e_ref[...], preferred_element_type=F32, precision=lax.Precision.HIGHEST)

    out = pl.pallas_call(body, out_shape=jax.ShapeDtypeStruct((NH * 15, 128), F32), name="rpb_grad",
                         compiler_params=pltpu.CompilerParams(vmem_limit_bytes=VMEM_LIMIT))(z2, _diag_onehot())
    return out[:, :31].reshape(NH, 15, 31)


DIL_BLOCKS = SEQ // DIL_QB


COLS = 128


def _col_spec():
    return pl.BlockSpec((SEQ, COLS), lambda j: (0, j))


def _grp_spec():
    return pl.BlockSpec((3, SEQ, COLS), lambda j: (0, 0, j))


def _store_group_order(dst_ref, src_ref):
    for g, d in enumerate(DIL):
        n = SEQ // d
        for r in range(d):
            dst_ref[g, r * n:(r + 1) * n, :] = src_ref[pl.ds(r, n, stride=d), :].astype(dst_ref.dtype)


def _store_token_order(dst_ref, src_ref, g):
    d = DIL[g]
    n = SEQ // d
    for r in range(d):
        dst_ref[pl.ds(r, n, stride=d), :] = src_ref[g, r * n:(r + 1) * n, :]


def _to_groups(name, a):
    def body(a_ref, o_ref):
        _store_group_order(o_ref, a_ref)

    return pl.pallas_call(body, grid=(DM // COLS,), in_specs=[_col_spec()], out_specs=_grp_spec(),
                          out_shape=jax.ShapeDtypeStruct((3, SEQ, DM), BF16), compiler_params=_params(("parallel",)), name=name)(a)


def _from_groups_sum(name, a):
    def body(a_ref, o_ref, t1, t2):
        _store_token_order(t1, a_ref, 1)
        _store_token_order(t2, a_ref, 2)
        o_ref[...] = (a_ref[0] + t1[...]) + t2[...]

    return pl.pallas_call(body, grid=(DM // COLS,), in_specs=[_grp_spec()], out_specs=_col_spec(),
                          out_shape=jax.ShapeDtypeStruct((SEQ, DM), F32), scratch_shapes=[pltpu.VMEM((SEQ, COLS), F32)] * 2,
                          compiler_params=_params(("parallel",)), name=name)(a)


def _dil_start(b):
    return pl.multiple_of(jnp.clip(b - 1, 0, DIL_BLOCKS - 3) * DIL_QB, DIL_QB)


def _dil_mask(g, b, start):
    shift = 11 - 2 * g
    ii = b * DIL_QB + lax.broadcasted_iota(jnp.int32, (DIL_QB, DIL_WIN), 0)
    jj = start + lax.broadcasted_iota(jnp.int32, (DIL_QB, DIL_WIN), 1)
    dist = jnp.abs(ii - jj)
    valid = (dist <= DIL_RADIUS) & (jnp.right_shift(ii, shift) == jnp.right_shift(jj, shift))
    return valid, dist.astype(F32)


def _dil_in_specs():
    return [pl.BlockSpec(memory_space=pltpu.SMEM),
            pl.BlockSpec((None, DIL_QB, 128), lambda g, hp, b: (g, b, hp)),
            pl.BlockSpec((None, SEQ, 128), lambda g, hp, b: (g, 0, 8 + hp)),
            pl.BlockSpec((None, SEQ, 128), lambda g, hp, b: (g, 0, 16 + hp))]


def _dil_fwd(qkv, slopes):
    def body(sl_ref, q_ref, k_ref, v_ref, o_ref, lse_ref):
        g, hp, b = pl.program_id(0), pl.program_id(1), pl.program_id(2)
        start = _dil_start(b)
        valid, dist = _dil_mask(g, b, start)
        dil = jnp.left_shift(1, 2 * g).astype(F32)
        q = q_ref[...]
        kw = k_ref[pl.ds(start, DIL_WIN), :]
        vw = v_ref[pl.ds(start, DIL_WIN), :]
        outs, lses = [], []
        for hh in range(2):
            sl = slice(hh * HD, (hh + 1) * HD)
            s = lax.dot_general(q[:, sl] * QSCALE, kw[:, sl], _NT, preferred_element_type=F32)
            s = jnp.where(valid, s - (sl_ref[hp * 2 + hh] * dil) * dist, NEG)
            m = jnp.max(s, axis=-1, keepdims=True)
            p = jnp.exp(s - m)
            l = jnp.sum(p, axis=-1, keepdims=True)
            outs.append(jnp.dot(p.astype(BF16), vw[:, sl], preferred_element_type=F32) / l)
            lses.append(jnp.broadcast_to(m + jnp.log(l), (DIL_QB, HD)))
        o_ref[...] = jnp.concatenate(outs, axis=1)
        lse_ref[...] = jnp.concatenate(lses, axis=1)

    ospec = pl.BlockSpec((None, DIL_QB, 128), lambda g, hp, b: (g, b, hp))
    sh = jax.ShapeDtypeStruct((3, SEQ, DM), F32)
    return pl.pallas_call(
        body, grid=(3, NH // 2, DIL_BLOCKS), in_specs=_dil_in_specs(), out_specs=[ospec, ospec], out_shape=[sh, sh],
        compiler_params=_params(("parallel", "parallel", "arbitrary")), name="dil_fwd")(slopes, qkv, qkv, qkv)


def _dil_merge(o_all, lse_all):
    def body(o_ref, l_ref, out_ref, lse_ref, o1, o2, l1, l2):
        for g, (ot, lt) in ((1, (o1, l1)), (2, (o2, l2))):
            _store_token_order(ot, o_ref, g)
            _store_token_order(lt, l_ref, g)
        la, lb, lc = l_ref[0], l1[...], l2[...]
        m = jnp.maximum(jnp.maximum(la, lb), lc)
        wa, wb, wc = jnp.exp(la - m), jnp.exp(lb - m), jnp.exp(lc - m)
        sw = (wa + wb) + wc
        out_ref[...] = (((wa * o_ref[0] + wb * o1[...]) + wc * o2[...]) / sw).astype(out_ref.dtype)
        lse_ref[...] = m + jnp.log(sw)

    return pl.pallas_call(
        body, grid=(DM // COLS,), in_specs=[_grp_spec(), _grp_spec()], out_specs=[_col_spec(), _col_spec()],
        out_shape=[jax.ShapeDtypeStruct((SEQ, DM), BF16), jax.ShapeDtypeStruct((SEQ, DM), F32)],
        scratch_shapes=[pltpu.VMEM((SEQ, COLS), F32)] * 4, compiler_params=_params(("parallel",)), name="dil_merge")(o_all, lse_all)


def _dil_bwd_prep(do, o, lse):
    def body(do_ref, o_ref, lse_ref, dog_ref, ddg_ref, lseg_ref, dd):
        prod = do_ref[...] * o_ref[...].astype(F32)
        dd[...] = jnp.concatenate(
            [jnp.broadcast_to(jnp.sum(prod[:, h * HD:(h + 1) * HD], axis=-1, keepdims=True), (SEQ, HD)) for h in range(COLS // HD)], axis=1)
        _store_group_order(dog_ref, do_ref)
        _store_group_order(ddg_ref, dd)
        _store_group_order(lseg_ref, lse_ref)

    return pl.pallas_call(
        body, grid=(DM // COLS,), in_specs=[_col_spec()] * 3, out_specs=[_grp_spec()] * 3,
        out_shape=[jax.ShapeDtypeStruct((3, SEQ, DM), BF16), jax.ShapeDtypeStruct((3, SEQ, DM), F32), jax.ShapeDtypeStruct((3, SEQ, DM), F32)],
        scratch_shapes=[pltpu.VMEM((SEQ, COLS), F32)], compiler_params=_params(("parallel",)), name="dil_bwd_prep")(do, o, lse)


def _dil_bwd(qkv, do, dd, lse, slopes):
    def body(sl_ref, q_ref, k_ref, v_ref, do_ref, dd_ref, lse_ref, dqkv_ref, dk_acc, dv_acc):
        g, hp, b = pl.program_id(0), pl.program_id(1), pl.program_id(2)

        @pl.when(b == 0)
        def _():
            dk_acc[...] = jnp.zeros_like(dk_acc)
            dv_acc[...] = jnp.zeros_like(dv_acc)

        start = _dil_start(b)
        valid, dist = _dil_mask(g, b, start)
        dil = jnp.left_shift(1, 2 * g).astype(F32)
        q = q_ref[...]
        do = do_ref[...]
        kw = k_ref[pl.ds(start, DIL_WIN), :]
        vw = v_ref[pl.ds(start, DIL_WIN), :]
        lse = lse_ref[...]
        dd = dd_ref[...]
        dqs, dks, dvs = [], [], []
        for hh in range(2):
            sl = slice(hh * HD, (hh + 1) * HD)
            qh = q[:, sl] * QSCALE
            s = lax.dot_general(qh, kw[:, sl], _NT, preferred_element_type=F32)
            s = jnp.where(valid, s - (sl_ref[hp * 2 + hh] * dil) * dist, NEG)
            p = jnp.exp(s - lse[:, hh * HD:hh * HD + 1])
            dp = lax.dot_general(do[:, sl], vw[:, sl], _NT, preferred_element_type=F32)
            dsb = (p * (dp - dd[:, hh * HD:hh * HD + 1])).astype(BF16)
            dqs.append(jnp.dot(dsb, kw[:, sl], preferred_element_type=F32) * QSCALE)
            dks.append(lax.dot_general(dsb, qh, _TN, preferred_element_type=F32))
            dvs.append(lax.dot_general(p.astype(BF16), do[:, sl], _TN, preferred_element_type=F32))
        dqkv_ref[0, pl.ds(pl.multiple_of(b * DIL_QB, DIL_QB), DIL_QB), :] = jnp.concatenate(dqs, axis=1).astype(dqkv_ref.dtype)
        dk_acc[pl.ds(start, DIL_WIN), :] += jnp.concatenate(dks, axis=1)
        dv_acc[pl.ds(start, DIL_WIN), :] += jnp.concatenate(dvs, axis=1)

        @pl.when(b == DIL_BLOCKS - 1)
        def _():
            dqkv_ref[1] = dk_acc[...].astype(dqkv_ref.dtype)
            dqkv_ref[2] = dv_acc[...].astype(dqkv_ref.dtype)

    rspec = pl.BlockSpec((None, DIL_QB, 128), lambda g, hp, b: (g, b, hp))
    return pl.pallas_call(
        body, grid=(3, NH // 2, DIL_BLOCKS), in_specs=_dil_in_specs() + [rspec, rspec, rspec],
        out_specs=pl.BlockSpec((None, 3, SEQ, 128), lambda g, hp, b: (g, 0, 0, hp)),
        out_shape=jax.ShapeDtypeStruct((3, 3, SEQ, DM), BF16),
        scratch_shapes=[pltpu.VMEM((SEQ, 128), F32), pltpu.VMEM((SEQ, 128), F32)],
        compiler_params=_params(("parallel", "parallel", "arbitrary")), name="dil_bwd")(slopes, qkv, qkv, qkv, do, dd, lse)


def _ffn_block(layer, x, g_pre, g_post, w):
    tag = f"l{layer}"
    h = _rms_fwd(f"{tag}_ffn_pre", x, g_pre)
    gate = _ffn_in(f"{tag}_gate", h, w["ffn_w_gate"], layer)
    up = _ffn_in(f"{tag}_up", h, w["ffn_w_up"], layer)
    act = _swiglu_fwd(f"{tag}_swiglu", gate, up)
    u = _ffn_out(f"{tag}_down", act, w["ffn_w_down"], layer)
    return _resid_norm(f"{tag}_ffn_post", x, u, g_post), (x, h, gate, up, act, u)


def _ffn_block_bwd(layer, dx, saved, g_pre, g_post, w, into):
    tag = f"l{layer}"
    x, h, gate, up, act, u = saved
    du, dg_post = _norm_bwd(f"{tag}_ffn_post_bwd", [dx], u, g_post)
    d_wd = _ffn_bwd_dwd(f"{tag}_dwd", act, du, layer, into[2])
    dact = _ffn_bwd_dact(f"{tag}_dact", du, w["ffn_w_down"], layer)
    dgate, dup = _swiglu_bwd(f"{tag}_swiglu_bwd", dact, gate, up)
    d_wg = _ffn_bwd_dwin(f"{tag}_dwg", h, dgate, layer, into[0])
    d_wu = _ffn_bwd_dwin(f"{tag}_dwu", h, dup, layer, into[1])
    dh = _ffn_bwd_dh(f"{tag}_ffn_dh", dgate, w["ffn_w_gate"], dup, w["ffn_w_up"], layer)
    dx_in, dg_pre = _norm_bwd(f"{tag}_ffn_pre_bwd", [dh], x, g_pre, res=dx)
    return dx_in, dg_pre, dg_post, (d_wg, d_wu, d_wd)


def _alibi_slopes():
    return 2.0 ** (-8.0 * jnp.arange(1, NH + 1, dtype=F32) / NH)


def _local_step(x, target, norms, rpb, w):
    g_mix_pre, g_mix_post, g_ffn_pre, g_ffn_post = norms
    row = lambda a, i: a[i:i + 1]
    na_wo = w["na_w_o"].reshape(DM, DM)
    dil_wo = w["dil_w_o"].reshape(DM, DM)

    bias = _na_bias_tiles(rpb)
    h0 = _rms_fwd("l0_mix_pre", x, row(g_mix_pre, 0))
    qkv0 = _qkv_fwd("l0_qkv", h0[None], w["na_w_qkv"])
    o0 = _na_fwd(qkv0[0], bias)
    u0 = _proj_fwd("l0_proj", o0, na_wo)
    x1 = _resid_norm("l0_mix_post", x, u0, row(g_mix_post, 0))
    x2, ffn0 = _ffn_block(0, x1, row(g_ffn_pre, 0), row(g_ffn_post, 0), w)

    slopes = _alibi_slopes()
    h2g = _to_groups("l1_h_groups", _rms_fwd("l1_mix_pre", x2, row(g_mix_pre, 1), F32))
    qkv1 = _qkv_fwd("l1_qkv", h2g, w["dil_w_qkv"])
    og, lg = _dil_fwd(qkv1, slopes)
    o1, lse = _dil_merge(og, lg)
    u1 = _proj_fwd("l1_proj", o1, dil_wo)
    x3 = _resid_norm("l1_mix_post", x2, u1, row(g_mix_post, 1))
    x4, ffn1 = _ffn_block(1, x3, row(g_ffn_pre, 1), row(g_ffn_post, 1), w)

    dx4, loss_row = _loss_grad("loss", x4, target)

    dx3, dg_fpre1, dg_fpost1, d_ffn = _ffn_block_bwd(1, dx4, ffn1, row(g_ffn_pre, 1), row(g_ffn_post, 1), w, (None, None, None))
    du1, dg_mpost1 = _norm_bwd("l1_mix_post_bwd", [dx3], u1, row(g_mix_post, 1))
    d_dil_wo = _proj_bwd_dw("l1_dwo", o1, du1)
    do1 = _proj_bwd_do("l1_do", du1, dil_wo, F32)
    dog, ddg, lseg = _dil_bwd_prep(do1, o1, lse)
    dqkv1 = _dil_bwd(qkv1, dog, ddg, lseg, slopes)
    d_dil_wqkv = _qkv_bwd_dw("l1_dwqkv", h2g, dqkv1, w["dil_w_qkv"].shape[2])
    dh2 = _from_groups_sum("l1_dh_tokens", _qkv_bwd_dh("l1_dh", dqkv1, w["dil_w_qkv"]))
    dx2, dg_mpre1 = _norm_bwd("l1_mix_pre_bwd", [dh2], x2, row(g_mix_pre, 1), res=dx3)

    dx1, dg_fpre0, dg_fpost0, d_ffn = _ffn_block_bwd(0, dx2, ffn0, row(g_ffn_pre, 0), row(g_ffn_post, 0), w, d_ffn)
    du0, dg_mpost0 = _norm_bwd("l0_mix_post_bwd", [dx1], u0, row(g_mix_post, 0))
    d_na_wo = _proj_bwd_dw("l0_dwo", o0, du0)
    do0 = _proj_bwd_do("l0_do", du0, na_wo)
    dqkv0, z = _na_bwd(qkv0[0], bias, do0)
    d_rpb = _rpb_grad(z)
    d_na_wqkv = _qkv_bwd_dw("l0_dwqkv", h0[None], dqkv0[None], w["na_w_qkv"].shape[2])
    dh0 = _qkv_bwd_dh("l0_dh", dqkv0[None], w["na_w_qkv"])
    dx0, dg_mpre0 = _norm_bwd("l0_mix_pre_bwd", [dh0[0]], x, row(g_mix_pre, 0), res=dx1)

    dnorms = (jnp.concatenate([dg_mpre0, dg_mpre1]), jnp.concatenate([dg_mpost0, dg_mpost1]),
              jnp.concatenate([dg_fpre0, dg_fpre1]), jnp.concatenate([dg_fpost0, dg_fpost1]))
    dw = {
        "na_w_qkv": d_na_wqkv, "na_w_o": d_na_wo.reshape(NCHIP, DM // NCHIP, DM),
        "dil_w_qkv": d_dil_wqkv, "dil_w_o": d_dil_wo.reshape(NCHIP, DM // NCHIP, DM),
        "ffn_w_gate": d_ffn[0], "ffn_w_up": d_ffn[1], "ffn_w_down": d_ffn[2],
    }
    return loss_row, dx0, dnorms, d_rpb, dw


WEIGHT_NAMES = ("na_w_qkv", "na_w_o", "ffn_w_gate", "ffn_w_up", "ffn_w_down", "dil_w_qkv", "dil_w_o")
HBM_SPEC = pl.BlockSpec(memory_space=pltpu.HBM)


def _place():
    x, y, c = lax.axis_index("x"), lax.axis_index("y"), lax.axis_index("c")
    chips = ((1 - x, y), (x, 1 - y), (1 - x, 1 - y))
    return x, y, c, chips


def _chip_id(chip):
    return 2 * chip[0] + chip[1]


def _comm_call(name, body, ins, out_shapes, n_sems, aliases=None):
    return pl.pallas_call(
        body, in_specs=[HBM_SPEC] * len(ins), out_specs=[HBM_SPEC] * len(out_shapes), out_shape=out_shapes,
        scratch_shapes=[pltpu.SemaphoreType.DMA((k,)) for k in n_sems], input_output_aliases=aliases or {},
        compiler_params=pltpu.CompilerParams(has_side_effects=True), name=name)(*ins)


def _gather_weights(shards):
    n = len(shards)

    def body(*refs):
        src, out = refs[:n], refs[n:2 * n]
        send_sems, recv_sems = refs[2 * n:]
        x, y, c, chips = _place()
        sibling = (x, y, 1 - c)

        def copy(t, k, chip, half, to, from_src=False):
            blk = out[t].at[_chip_id(chip), half]
            return pltpu.make_async_remote_copy(
                src_ref=src[t].at[half] if from_src else blk, dst_ref=blk,
                send_sem=send_sems.at[6 * t + k], recv_sem=recv_sems.at[6 * t + k], device_id=to, device_id_type=MESH)

        first = [copy(t, j, (x, y), c, (*chip, c), from_src=True) for t in range(n) for j, chip in enumerate(chips)]
        for cp in first:
            cp.start()
        passed = []
        for t in range(n):
            for j, chip in enumerate(chips):
                copy(t, j, chip, c, (x, y, c)).wait_recv()
                fwd = copy(t, 3 + j, chip, c, sibling)
                fwd.start()
                passed.append(fwd)
        for t in range(n):
            for j, chip in enumerate(chips):
                copy(t, 3 + j, chip, 1 - c, (x, y, c)).wait_recv()
        for cp in first + passed:
            cp.wait_send()

    return _comm_call("gather_weights", body, shards, [jax.ShapeDtypeStruct((NCHIP,) + s.shape, s.dtype) for s in shards], (6 * n, 6 * n))


def _pair_exchange(grads):
    n = len(grads)

    def body(*refs):
        g, theirs = refs[:n], refs[n:2 * n]
        send_sems, recv_sems = refs[2 * n:]
        x, y, c, _ = _place()
        swap = [pltpu.make_async_remote_copy(src_ref=g[t].at[:, 1 - c], dst_ref=theirs[t], send_sem=send_sems.at[t],
                                             recv_sem=recv_sems.at[t], device_id=(x, y, 1 - c), device_id_type=MESH) for t in range(n)]
        for cp in swap:
            cp.start()
        for cp in swap:
            cp.wait()

    return _comm_call("grad_pair_exchange", body, grads, [jax.ShapeDtypeStruct((NCHIP,) + g.shape[2:], g.dtype) for g in grads], (n, n))


def _chip_exchange(parts):
    n = len(parts)

    def body(*refs):
        p, slots = refs[:n], refs[n:2 * n]
        send_sems, recv_sems = refs[2 * n:]
        x, y, c, chips = _place()

        def copy(t, j):
            return pltpu.make_async_remote_copy(src_ref=p[t].at[_chip_id(chips[j])], dst_ref=slots[t].at[j], send_sem=send_sems.at[3 * t + j],
                                                recv_sem=recv_sems.at[3 * t + j], device_id=(*chips[j], c), device_id_type=MESH)

        sends = [copy(t, j) for t in range(n) for j in range(3)]
        for cp in sends:
            cp.start()
        for cp in sends:
            cp.wait()

    return _comm_call("grad_chip_exchange", body, parts, [jax.ShapeDtypeStruct((3,) + p.shape[1:], p.dtype) for p in parts], (3 * n, 3 * n))


def _pair_share(full):
    n = len(full)

    def body(*refs):
        buf = refs[n:2 * n]
        send_sems, recv_sems = refs[2 * n:]
        x, y, c, _ = _place()
        sends = [pltpu.make_async_remote_copy(src_ref=buf[t].at[c], dst_ref=buf[t].at[c], send_sem=send_sems.at[t], recv_sem=recv_sems.at[t],
                                              device_id=(x, y, 1 - c), device_id_type=MESH) for t in range(n)]
        for cp in sends:
            cp.start()
        for t in range(n):
            pltpu.make_async_remote_copy(src_ref=buf[t].at[c], dst_ref=buf[t].at[1 - c], send_sem=send_sems.at[t], recv_sem=recv_sems.at[t],
                                         device_id=(x, y, 1 - c), device_id_type=MESH).wait_recv()
        for cp in sends:
            cp.wait_send()

    return _comm_call("grad_pair_share", body, full, [jax.ShapeDtypeStruct(f.shape, f.dtype) for f in full], (n, n),
                      aliases={t: t for t in range(n)})


SMALL_ROWS = 128


def _allreduce_small(v):
    def body(v_ref, o_ref, buf, send_sems, recv_sems):
        x, y, c, _ = _place()
        me = 4 * x + 2 * y + c
        flip = lambda a, f: 1 - a if f else a
        buf[me] = v_ref[...]
        peers = [(flip(x, d >> 2 & 1), flip(y, d >> 1 & 1), flip(c, d & 1)) for d in range(1, 8)]
        sends = [pltpu.make_async_remote_copy(src_ref=v_ref, dst_ref=buf.at[me], send_sem=send_sems.at[i], recv_sem=recv_sems.at[i],
                                              device_id=peer, device_id_type=MESH) for i, peer in enumerate(peers)]
        for cp in sends:
            cp.start()
        for i, (px, py, pc) in enumerate(peers):
            pltpu.make_async_remote_copy(src_ref=v_ref, dst_ref=buf.at[4 * px + 2 * py + pc], send_sem=send_sems.at[i], recv_sem=recv_sems.at[i],
                                         device_id=(px, py, pc), device_id_type=MESH).wait_recv()
        for cp in sends:
            cp.wait_send()
        acc = buf[0]
        for k in range(1, 8):
            acc = acc + buf[k]
        o_ref[...] = acc

    vm = pl.BlockSpec(memory_space=pltpu.VMEM)
    return pl.pallas_call(
        body, in_specs=[vm], out_specs=vm, out_shape=jax.ShapeDtypeStruct((SMALL_ROWS, 128), F32),
        scratch_shapes=[pltpu.VMEM((8, SMALL_ROWS, 128), F32), pltpu.SemaphoreType.DMA((7,)), pltpu.SemaphoreType.DMA((7,))],
        compiler_params=pltpu.CompilerParams(has_side_effects=True), name="allreduce_small")(v)


def _row_block(rows, cols, budget=1 << 20):
    best = 8
    for bm in range(8, rows + 1, 8):
        if rows % bm == 0 and bm * cols * 4 <= budget:
            best = bm
    return best


def _pair_sum(name, place, g, theirs):
    _, m, c = theirs.shape
    bm = _row_block(m, c)

    def body(place_ref, a_ref, b_ref, o_ref):
        o_ref[...] = (a_ref[...].astype(F32) + b_ref[...].astype(F32)).astype(o_ref.dtype)

    spec = pl.BlockSpec((None, bm, c), lambda k, i, pr: (k, i, 0))
    return pl.pallas_call(
        body, out_shape=jax.ShapeDtypeStruct(theirs.shape, BF16),
        grid_spec=pltpu.PrefetchScalarGridSpec(
            num_scalar_prefetch=1, grid=(NCHIP, m // bm),
            in_specs=[pl.BlockSpec((None, None, bm, c), lambda k, i, pr: (k, pr[0], i, 0)), spec], out_specs=spec),
        compiler_params=_params(("parallel", "parallel")), name=name)(place, g, theirs)


def _chip_sum(name, place, parts, slots):
    _, m, c = parts.shape
    bm = _row_block(m, c)

    def body(place_ref, p_ref, s_ref, o_ref):
        s = s_ref[...].astype(F32)
        o_ref[...] = ((p_ref[...].astype(F32) + s[0]) + s[1]) + s[2]

    return pl.pallas_call(
        body, out_shape=jax.ShapeDtypeStruct((2, m, c), F32),
        grid_spec=pltpu.PrefetchScalarGridSpec(
            num_scalar_prefetch=1, grid=(m // bm,),
            in_specs=[pl.BlockSpec((None, bm, c), lambda i, pr: (pr[1], i, 0)), pl.BlockSpec((3, bm, c), lambda i, pr: (0, i, 0))],
            out_specs=pl.BlockSpec((None, bm, c), lambda i, pr: (pr[0], i, 0))),
        compiler_params=_params(("parallel",)), name=name)(place, parts, slots)


def _adamw(name, w, g, m, v):
    lead, rows, cols = w.shape
    bm = _row_block(rows, cols, budget=768 * 1024)
    c1 = 1.0 - ADAM_B1 ** ADAM_STEP
    c2 = 1.0 - ADAM_B2 ** ADAM_STEP

    def body(w_ref, g_ref, m_ref, v_ref, go_ref, d_ref, mo_ref, vo_ref):
        g = g_ref[...]
        mn = ADAM_B1 * m_ref[...] + (1.0 - ADAM_B1) * g
        vn = ADAM_B2 * v_ref[...] + (1.0 - ADAM_B2) * (g * g)
        go_ref[...] = g
        mo_ref[...] = mn
        vo_ref[...] = vn
        d_ref[...] = -ADAM_LR * ((mn / c1) / (jnp.sqrt(vn / c2) + ADAM_EPS) + ADAM_WD * w_ref[...])

    spec = pl.BlockSpec((None, bm, cols), lambda l, i: (l, i, 0))
    sh = jax.ShapeDtypeStruct((lead, rows, cols), F32)
    return pl.pallas_call(body, grid=(lead, rows // bm), in_specs=[spec] * 4, out_specs=[spec] * 4, out_shape=[sh] * 4,
                          compiler_params=_params(("parallel", "parallel")), name=name)(w, g, m, v)


def _pack_small(norms, rpb):
    flat = jnp.concatenate([a.reshape(-1) for a in norms] + [rpb.reshape(-1)])
    return jnp.pad(flat, (0, SMALL_ROWS * 128 - flat.shape[0])).reshape(SMALL_ROWS, 128)


def _unpack_small(p):
    flat = p.reshape(-1)
    norms = [flat[i * 2 * DM:(i + 1) * 2 * DM].reshape(2, DM) for i in range(4)]
    rpb = flat[8 * DM:8 * DM + NH * 15 * 31].reshape(1, NH, 15, 31)
    return norms, rpb


def kernel(x, norm_mix_pre, norm_mix_post, norm_ffn_pre, norm_ffn_post, na_w_qkv, na_w_o, na_rpb, dil_w_qkv, dil_w_o, ffn_w_gate, ffn_w_up, ffn_w_down, loss_target, m_norm_mix_pre, m_norm_mix_post, m_norm_ffn_pre, m_norm_ffn_post, m_na_w_qkv, m_na_w_o, m_na_rpb, m_dil_w_qkv, m_dil_w_o, m_ffn_w_gate, m_ffn_w_up, m_ffn_w_down, v_norm_mix_pre, v_norm_mix_post, v_norm_ffn_pre, v_norm_ffn_post, v_na_w_qkv, v_na_w_o, v_na_rpb, v_dil_w_qkv, v_dil_w_o, v_ffn_w_gate, v_ffn_w_up, v_ffn_w_down):
    weights = {"na_w_qkv": na_w_qkv[0], "na_w_o": na_w_o[0], "dil_w_qkv": dil_w_qkv[0], "dil_w_o": dil_w_o[0],
               "ffn_w_gate": ffn_w_gate, "ffn_w_up": ffn_w_up, "ffn_w_down": ffn_w_down}
    m_in = {"na_w_qkv": m_na_w_qkv[0], "na_w_o": m_na_w_o[0], "dil_w_qkv": m_dil_w_qkv[0], "dil_w_o": m_dil_w_o[0],
            "ffn_w_gate": m_ffn_w_gate, "ffn_w_up": m_ffn_w_up, "ffn_w_down": m_ffn_w_down}
    v_in = {"na_w_qkv": v_na_w_qkv[0], "na_w_o": v_na_w_o[0], "dil_w_qkv": v_dil_w_qkv[0], "dil_w_o": v_dil_w_o[0],
            "ffn_w_gate": v_ffn_w_gate, "ffn_w_up": v_ffn_w_up, "ffn_w_down": v_ffn_w_down}
    halves = lambda a: a.reshape(2, -1, a.shape[-1])
    flat2 = lambda a: a.reshape(-1, a.shape[-1])

    chip = 2 * lax.axis_index("x") + lax.axis_index("y")
    place = jnp.stack([lax.axis_index("c"), chip]).astype(jnp.int32)
    own = [halves(weights[n]).astype(BF16) for n in WEIGHT_NAMES]
    gathered = _gather_weights(own)
    w = {}
    for n, gw, ow in zip(WEIGHT_NAMES, gathered, own):
        gw = lax.dynamic_update_slice(gw, ow[None], (chip, 0, 0, 0))
        w[n] = gw if weights[n].ndim == 3 else gw.reshape((NCHIP,) + weights[n].shape)

    norms = (norm_mix_pre, norm_mix_post, norm_ffn_pre, norm_ffn_post)
    loss_row, dx, dnorms, d_rpb, dw = _local_step(x[0], loss_target[0], norms, na_rpb[0], w)
    loss = lax.psum(loss_row[0, 0], ("x", "y", "c"))

    grads = [dw[n].reshape((NCHIP, 2, -1, dw[n].shape[-1])) for n in WEIGHT_NAMES]
    theirs = _pair_exchange(grads)
    parts = [_pair_sum(f"pair_sum_{n}", place, a, b) for n, a, b in zip(WEIGHT_NAMES, grads, theirs)]
    slots = _chip_exchange(parts)
    sums = [_chip_sum(f"chip_sum_{n}", place, p, s) for n, p, s in zip(WEIGHT_NAMES, parts, slots)]
    full = _pair_share(sums)
    small = _allreduce_small(_pack_small(dnorms, d_rpb))

    out_g, out_d, out_m, out_v = {}, {}, {}, {}
    for n, gf in zip(WEIGHT_NAMES, full):
        shp = weights[n].shape
        as3 = lambda a: a.reshape((-1,) + shp[-2:])
        res = _adamw(f"adamw_{n}", as3(weights[n]), as3(gf), as3(m_in[n]), as3(v_in[n]))
        lead = (1,) if weights[n].ndim == 2 else ()
        out_g[n], out_d[n], out_m[n], out_v[n] = (r.reshape(lead + shp) for r in res)
    sm_names = ("norm_mix_pre", "norm_mix_post", "norm_ffn_pre", "norm_ffn_post", "na_rpb")
    sm = _adamw("adamw_small", _pack_small(norms, na_rpb)[None],
                small[None], _pack_small((m_norm_mix_pre, m_norm_mix_post, m_norm_ffn_pre, m_norm_ffn_post), m_na_rpb)[None],
                _pack_small((v_norm_mix_pre, v_norm_mix_post, v_norm_ffn_pre, v_norm_ffn_post), v_na_rpb)[None])
    for res, dst in zip(sm, (out_g, out_d, out_m, out_v)):
        ns, rp = _unpack_small(res)
        for n, a in zip(sm_names, ns + [rp]):
            dst[n] = a

    order = ("norm_mix_pre", "norm_mix_post", "norm_ffn_pre", "norm_ffn_post", "na_w_qkv", "na_w_o", "na_rpb", "dil_w_qkv", "dil_w_o",
             "ffn_w_gate", "ffn_w_up", "ffn_w_down")
    return (loss, dx[None], *[out_g[n] for n in order], *[out_d[n] for n in order], *[out_m[n] for n in order], *[out_v[n] for n in order])
```

```python
import functools

import numpy as np
import jax
import jax.numpy as jnp
from jax import lax
from jax.experimental import pallas as pl
from jax.experimental.pallas import tpu as pltpu

F32 = jnp.float32
BF16 = jnp.bfloat16

SEQ = 2048
DM = 1024
NH = 16
HD = 64
DFF = 2816
NCHIP = 4
FSH = DFF // NCHIP
GRID_W = 64
NA_QROWS = 4
NA_QB = NA_QROWS * GRID_W
NA_WROWS = 12
NA_WIN = NA_WROWS * GRID_W
DIL = (1, 4, 16)
DIL_QB = 256
DIL_WIN = DIL_QB + 128
DIL_RADIUS = 64
RMS_EPS = 1e-6
NEG = -1e30
QSCALE = HD ** -0.5
CH = 256
MESH = pl.DeviceIdType.MESH

ADAM_LR, ADAM_B1, ADAM_B2, ADAM_EPS, ADAM_WD, ADAM_STEP = 0.001, 0.9, 0.999, 1e-08, 0.01, 10

VMEM_LIMIT = 56 * 1024 * 1024

_NN = (((1,), (0,)), ((), ()))
_NT = (((1,), (1,)), ((), ()))
_TN = (((0,), (0,)), ((), ()))


def _params(sem):
    return pltpu.CompilerParams(dimension_semantics=sem, vmem_limit_bytes=VMEM_LIMIT)


def _matmul(name, pairs, grid, out_shape, out_spec, acc_shape, into=None):
    nk = grid[-1]
    npair = len(pairs)
    n_in = 2 * npair + (into is not None)

    def body(*refs):
        ins, o_ref = refs[:2 * npair], refs[n_in]
        part = None
        for p in range(npair):
            d = lax.dot_general(ins[2 * p][...].astype(BF16), ins[2 * p + 1][...].astype(BF16), pairs[p][4],
                                preferred_element_type=F32)
            part = d if part is None else part + d
        if nk == 1:
            o_ref[...] = part.astype(o_ref.dtype)
        else:
            acc_ref = refs[n_in + 1]
            kk = pl.program_id(len(grid) - 1)

            @pl.when(kk == 0)
            def _():
                acc_ref[...] = part

            @pl.when(kk > 0)
            def _():
                acc_ref[...] += part

            @pl.when(kk == nk - 1)
            def _():
                o_ref[...] = acc_ref[...].astype(o_ref.dtype)

    ops, specs = [], []
    for a, a_spec, b, b_spec, _ in pairs:
        ops += [a, b]
        specs += [a_spec, b_spec]
    if into is not None:
        ops.append(into)
        specs.append(pl.BlockSpec(memory_space=pl.ANY))
    return pl.pallas_call(
        body, grid=grid, in_specs=specs, out_specs=out_spec, out_shape=out_shape,
        input_output_aliases={} if into is None else {n_in - 1: 0},
        scratch_shapes=[] if nk == 1 else [pltpu.VMEM(acc_shape, F32)],
        compiler_params=_params(("parallel",) * (len(grid) - 1) + ("arbitrary",)), name=name,
    )(*ops)


def _qkv_fwd(name, h_all, w4):
    g_n = h_all.shape[0]
    per = w4.shape[2] // CH
    return _matmul(
        name, [(h_all, pl.BlockSpec((None, SEQ, DM), lambda g, q, k: (g, 0, 0)),
                w4, pl.BlockSpec((None, DM, CH), lambda g, q, k: ((g * 12 + q) // per, 0, (g * 12 + q) % per)), _NN)],
        (g_n, 12, 1), jax.ShapeDtypeStruct((g_n, SEQ, 3 * DM), BF16),
        pl.BlockSpec((None, SEQ, CH), lambda g, q, k: (g, 0, q)), None)


def _qkv_bwd_dh(name, dqkv, w4):
    g_n = dqkv.shape[0]
    per = w4.shape[2] // CH
    tm = 1024
    return _matmul(
        name, [(dqkv, pl.BlockSpec((None, None, tm, CH), lambda g, i, q: (g, q // 4, i, q % 4)),
                w4, pl.BlockSpec((None, DM, CH), lambda g, i, q: ((g * 12 + q) // per, 0, (g * 12 + q) % per)), _NT)],
        (g_n, SEQ // tm, 12), jax.ShapeDtypeStruct((g_n, SEQ, DM), F32),
        pl.BlockSpec((None, tm, DM), lambda g, i, q: (g, i, 0)), (tm, DM))


def _qkv_bwd_dw(name, h_all, dqkv, shard_cols):
    g_n = dqkv.shape[0]
    per = shard_cols // CH
    tk = 512
    return _matmul(
        name, [(h_all, pl.BlockSpec((None, tk, DM), lambda qq, k: (qq // 12, k, 0)),
                dqkv, pl.BlockSpec((None, None, tk, CH), lambda qq, k: (qq // 12, (qq % 12) // 4, k, qq % 4)), _TN)],
        (g_n * 12, SEQ // tk), jax.ShapeDtypeStruct((NCHIP, DM, shard_cols), BF16),
        pl.BlockSpec((None, DM, CH), lambda qq, k: (qq // per, 0, qq % per)), (DM, CH))


def _proj_fwd(name, o, wo):
    tm = 512
    return _matmul(
        name, [(o, pl.BlockSpec((tm, DM), lambda i, k: (i, 0)), wo, pl.BlockSpec((DM, DM), lambda i, k: (0, 0)), _NN)],
        (SEQ // tm, 1), jax.ShapeDtypeStruct((SEQ, DM), F32), pl.BlockSpec((tm, DM), lambda i, k: (i, 0)), None)


def _proj_bwd_do(name, du, wo, dtype=BF16):
    tm = 512
    return _matmul(
        name, [(du, pl.BlockSpec((tm, DM), lambda i, k: (i, 0)), wo, pl.BlockSpec((DM, DM), lambda i, k: (0, 0)), _NT)],
        (SEQ // tm, 1), jax.ShapeDtypeStruct((SEQ, DM), dtype), pl.BlockSpec((tm, DM), lambda i, k: (i, 0)), None)


def _proj_bwd_dw(name, o, du):
    tk, tn = 512, 512
    return _matmul(
        name, [(o, pl.BlockSpec((tk, DM), lambda j, k: (k, 0)), du, pl.BlockSpec((tk, tn), lambda j, k: (k, j)), _TN)],
        (DM // tn, SEQ // tk), jax.ShapeDtypeStruct((DM, DM), BF16), pl.BlockSpec((DM, tn), lambda j, k: (0, j)), (DM, tn))


def _ffn_in(name, h, w4, layer):
    tm = 1024
    return _matmul(
        name, [(h, pl.BlockSpec((tm, DM), lambda i, s, k: (i, 0)), w4, pl.BlockSpec((None, None, DM, FSH), lambda i, s, k: (s, layer, 0, 0)), _NN)],
        (SEQ // tm, NCHIP, 1), jax.ShapeDtypeStruct((NCHIP, SEQ, FSH), BF16),
        pl.BlockSpec((None, tm, FSH), lambda i, s, k: (s, i, 0)), None)


def _ffn_out(name, act, wd4, layer):
    tm = 1024
    return _matmul(
        name, [(act, pl.BlockSpec((None, tm, FSH), lambda i, s: (s, i, 0)), wd4, pl.BlockSpec((None, None, FSH, DM), lambda i, s: (s, layer, 0, 0)), _NN)],
        (SEQ // tm, NCHIP), jax.ShapeDtypeStruct((SEQ, DM), F32), pl.BlockSpec((tm, DM), lambda i, s: (i, 0)), (tm, DM))


def _ffn_bwd_dact(name, du, wd4, layer):
    tm = 1024
    return _matmul(
        name, [(du, pl.BlockSpec((tm, DM), lambda i, s, k: (i, 0)), wd4, pl.BlockSpec((None, None, FSH, DM), lambda i, s, k: (s, layer, 0, 0)), _NT)],
        (SEQ // tm, NCHIP, 1), jax.ShapeDtypeStruct((NCHIP, SEQ, FSH), BF16),
        pl.BlockSpec((None, tm, FSH), lambda i, s, k: (s, i, 0)), None)


def _ffn_bwd_dwd(name, act, du, layer, into):
    tk = 512
    return _matmul(
        name, [(act, pl.BlockSpec((None, tk, FSH), lambda s, k: (s, k, 0)), du, pl.BlockSpec((tk, DM), lambda s, k: (k, 0)), _TN)],
        (NCHIP, SEQ // tk), jax.ShapeDtypeStruct((NCHIP, 2, FSH, DM), BF16), pl.BlockSpec((None, None, FSH, DM), lambda s, k: (s, layer, 0, 0)),
        (FSH, DM), into=into)


def _ffn_bwd_dh(name, dgate, wg4, dup, wu4, layer):
    tm = 1024
    a_spec = pl.BlockSpec((None, tm, FSH), lambda i, s: (s, i, 0))
    b_spec = pl.BlockSpec((None, None, DM, FSH), lambda i, s: (s, layer, 0, 0))
    return _matmul(
        name, [(dgate, a_spec, wg4, b_spec, _NT), (dup, a_spec, wu4, b_spec, _NT)],
        (SEQ // tm, NCHIP), jax.ShapeDtypeStruct((SEQ, DM), F32), pl.BlockSpec((tm, DM), lambda i, s: (i, 0)), (tm, DM))


def _ffn_bwd_dwin(name, h, dg, layer, into):
    tk = 512
    return _matmul(
        name, [(h, pl.BlockSpec((tk, DM), lambda s, k: (k, 0)), dg, pl.BlockSpec((None, tk, FSH), lambda s, k: (s, k, 0)), _TN)],
        (NCHIP, SEQ // tk), jax.ShapeDtypeStruct((NCHIP, 2, DM, FSH), BF16), pl.BlockSpec((None, None, DM, FSH), lambda s, k: (s, layer, 0, 0)),
        (DM, FSH), into=into)


ROWS = 256


def _row_spec():
    return pl.BlockSpec((ROWS, DM), lambda i: (i, 0))


def _vec_spec():
    return pl.BlockSpec((1, DM), lambda i: (0, 0))


def _rms_fwd(name, x, g, dtype=BF16):
    def body(x_ref, g_ref, o_ref):
        x = x_ref[...]
        r = lax.rsqrt(jnp.mean(x * x, axis=-1, keepdims=True) + RMS_EPS)
        o_ref[...] = (x * r * g_ref[...]).astype(o_ref.dtype)

    return pl.pallas_call(body, grid=(SEQ // ROWS,), in_specs=[_row_spec(), _vec_spec()], out_specs=_row_spec(),
                          out_shape=jax.ShapeDtypeStruct((SEQ, DM), dtype), compiler_params=_params(("parallel",)), name=name)(x, g)


def _resid_norm(name, x, u, g):
    def body(x_ref, u_ref, g_ref, o_ref):
        u = u_ref[...]
        r = lax.rsqrt(jnp.mean(u * u, axis=-1, keepdims=True) + RMS_EPS)
        o_ref[...] = x_ref[...] + u * r * g_ref[...]

    return pl.pallas_call(body, grid=(SEQ // ROWS,), in_specs=[_row_spec(), _row_spec(), _vec_spec()], out_specs=_row_spec(),
                          out_shape=jax.ShapeDtypeStruct((SEQ, DM), F32), compiler_params=_params(("parallel",)), name=name)(x, u, g)


def _norm_bwd(name, dys, u, g, res=None):
    ndy = len(dys)

    def body(*refs):
        dy = refs[0][...]
        for r_ in refs[1:ndy]:
            dy = dy + r_[...]
        u_ref, g_ref = refs[ndy], refs[ndy + 1]
        res_ref = refs[ndy + 2] if res is not None else None
        du_ref, dg_ref = refs[-2], refs[-1]
        u = u_ref[...]
        r = lax.rsqrt(jnp.mean(u * u, axis=-1, keepdims=True) + RMS_EPS)
        yh = u * r
        t = dy * g_ref[...]
        du = r * (t - yh * jnp.mean(t * yh, axis=-1, keepdims=True))
        if res_ref is not None:
            du = du + res_ref[...]
        du_ref[...] = du

        @pl.when(pl.program_id(0) == 0)
        def _():
            dg_ref[...] = jnp.zeros_like(dg_ref)

        dg_ref[...] += jnp.sum(dy * yh, axis=0, keepdims=True)

    ops = list(dys) + [u, g] + ([res] if res is not None else [])
    specs = [_row_spec()] * ndy + [_row_spec(), _vec_spec()] + ([_row_spec()] if res is not None else [])
    return pl.pallas_call(
        body, grid=(SEQ // ROWS,), in_specs=specs, out_specs=[_row_spec(), _vec_spec()],
        out_shape=[jax.ShapeDtypeStruct((SEQ, DM), F32), jax.ShapeDtypeStruct((1, DM), F32)],
        compiler_params=_params(("arbitrary",)), name=name)(*ops)


def _loss_grad(name, y, t):
    def body(y_ref, t_ref, dy_ref, l_ref):
        e = y_ref[...] - t_ref[...]
        dy_ref[...] = e * (1.0 / DM)

        @pl.when(pl.program_id(0) == 0)
        def _():
            l_ref[...] = jnp.zeros_like(l_ref)

        l_ref[...] += jnp.sum(e * e) * (0.5 / DM)

    return pl.pallas_call(
        body, grid=(SEQ // ROWS,), in_specs=[_row_spec(), _row_spec()],
        out_specs=[_row_spec(), pl.BlockSpec((1, 128), lambda i: (0, 0))],
        out_shape=[jax.ShapeDtypeStruct((SEQ, DM), F32), jax.ShapeDtypeStruct((1, 128), F32)],
        compiler_params=_params(("arbitrary",)), name=name)(y, t)


def _ffn_spec():
    return pl.BlockSpec((None, 512, FSH), lambda s, i: (s, i, 0))


def _swiglu_fwd(name, gate, up):
    def body(g_ref, u_ref, o_ref):
        g = g_ref[...].astype(F32)
        o_ref[...] = (g * jax.nn.sigmoid(g) * u_ref[...].astype(F32)).astype(o_ref.dtype)

    return pl.pallas_call(body, grid=(NCHIP, SEQ // 512), in_specs=[_ffn_spec(), _ffn_spec()], out_specs=_ffn_spec(),
                          out_shape=jax.ShapeDtypeStruct((NCHIP, SEQ, FSH), BF16),
                          compiler_params=_params(("parallel", "parallel")), name=name)(gate, up)


def _swiglu_bwd(name, dact, gate, up):
    def body(d_ref, g_ref, u_ref, dg_ref, du_ref):
        d = d_ref[...].astype(F32)
        g = g_ref[...].astype(F32)
        u = u_ref[...].astype(F32)
        sg = jax.nn.sigmoid(g)
        dg_ref[...] = (d * u * sg * (1.0 + g * (1.0 - sg))).astype(dg_ref.dtype)
        du_ref[...] = (d * g * sg).astype(du_ref.dtype)

    sh = jax.ShapeDtypeStruct((NCHIP, SEQ, FSH), BF16)
    return pl.pallas_call(body, grid=(NCHIP, SEQ // 512), in_specs=[_ffn_spec()] * 3, out_specs=[_ffn_spec()] * 2,
                          out_shape=[sh, sh], compiler_params=_params(("parallel", "parallel")), name=name)(dact, gate, up)


NA_BLOCKS = SEQ // NA_QB
NA_ROWS_TOTAL = SEQ // GRID_W
NA_CLASSES = ((0, 0), (8, 4), (NA_ROWS_TOTAL - NA_QROWS, NA_ROWS_TOTAL - NA_WROWS))


def _na_pairs(i0, ws):
    out = []
    for qi in range(NA_QROWS):
        i = i0 + qi
        rs = min(max(i - 4, 0), NA_ROWS_TOTAL - 8)
        for kr in range(NA_WROWS):
            r = ws + kr
            if rs <= r < rs + 8:
                out.append((qi, kr, r - i + 7))
    return out


def _diag_onehot():
    qc, kc = np.meshgrid(np.arange(GRID_W), np.arange(GRID_W), indexing="ij")
    e = np.zeros((GRID_W * GRID_W, 128), np.float32)
    j = (kc - qc + 15).reshape(-1)
    ok = (j >= 0) & (j <= 30)
    e[np.arange(GRID_W * GRID_W)[ok], j[ok]] = 1.0
    return jnp.asarray(e)


def _rpb_expand(rpb):
    r2 = jnp.pad(rpb.reshape(NH * 15, 31), ((0, 0), (0, 128 - 31)))

    def body(r_ref, e_ref, o_ref):
        o_ref[...] = lax.dot_general(r_ref[...], e_ref[...], _NT, preferred_element_type=F32, precision=lax.Precision.HIGHEST)

    out = pl.pallas_call(body, out_shape=jax.ShapeDtypeStruct((NH * 15, GRID_W * GRID_W), F32), name="rpb_expand",
                         compiler_params=pltpu.CompilerParams(vmem_limit_bytes=VMEM_LIMIT))(r2, _diag_onehot())
    return out.reshape(NH, 15, GRID_W, GRID_W)


def _na_bias_tiles(rpb):
    col = np.arange(GRID_W)
    col_start = np.clip(col - 8, 0, GRID_W - 16)
    col_mask = (col[None, :] >= col_start[:, None]) & (col[None, :] < col_start[:, None] + 16)
    rc = jnp.where(col_mask[None, None], _rpb_expand(rpb), NEG)
    neg = jnp.full((NH, GRID_W, GRID_W), NEG, F32)
    tiles = []
    for i0, ws in NA_CLASSES:
        pairs = {(qi, kr): dr for qi, kr, dr in _na_pairs(i0, ws)}
        rows = [jnp.concatenate([rc[:, pairs[(qi, kr)]] if (qi, kr) in pairs else neg for kr in range(NA_WROWS)], axis=2)
                for qi in range(NA_QROWS)]
        tiles.append(jnp.concatenate(rows, axis=1))
    return jnp.stack(tiles)


def _na_cls(b):
    return jnp.where(b == 0, 0, jnp.where(b == NA_BLOCKS - 1, 2, 1))


def _na_start(b):
    return pl.multiple_of(jnp.clip(b * NA_QROWS - 4, 0, NA_ROWS_TOTAL - NA_WROWS) * GRID_W, GRID_W)


HPS = 4
LW = HPS * HD
NLW = DM // LW


NA_BWD_HPS = 2


def _na_in_specs(hps=HPS):
    lw = hps * HD
    nlw = DM // lw
    return [pl.BlockSpec((NA_QB, lw), lambda hp, b: (b, hp)),
            pl.BlockSpec((SEQ, lw), lambda hp, b: (0, nlw + hp)),
            pl.BlockSpec((SEQ, lw), lambda hp, b: (0, 2 * nlw + hp)),
            pl.BlockSpec((None, hps, NA_QB, NA_WIN), lambda hp, b: (_na_cls(b), hp, 0, 0))]


def _na_fwd(qkv, bias):
    def body(q_ref, k_ref, v_ref, b_ref, o_ref):
        start = _na_start(pl.program_id(1))
        q = q_ref[...]
        kw = k_ref[pl.ds(start, NA_WIN), :]
        vw = v_ref[pl.ds(start, NA_WIN), :]
        outs = []
        for hh in range(HPS):
            sl = slice(hh * HD, (hh + 1) * HD)
            s = lax.dot_general(q[:, sl] * QSCALE, kw[:, sl], _NT, preferred_element_type=F32) + b_ref[hh]
            p = jnp.exp(s - jnp.max(s, axis=-1, keepdims=True))
            l = jnp.sum(p, axis=-1, keepdims=True)
            outs.append(jnp.dot(p.astype(BF16), vw[:, sl], preferred_element_type=F32) / l)
        o_ref[...] = jnp.concatenate(outs, axis=1).astype(o_ref.dtype)

    return pl.pallas_call(
        body, grid=(NLW, NA_BLOCKS), in_specs=_na_in_specs(), out_specs=pl.BlockSpec((NA_QB, LW), lambda hp, b: (b, hp)),
        out_shape=jax.ShapeDtypeStruct((SEQ, DM), BF16), compiler_params=_params(("parallel", "arbitrary")), name="na_fwd")(qkv, qkv, qkv, bias)


def _na_bwd(qkv, bias, do):
    lw = NA_BWD_HPS * HD

    def body(q_ref, k_ref, v_ref, b_ref, do_ref, dqkv_ref, z_ref, dk_acc, dv_acc):
        blk = pl.program_id(1)

        @pl.when(blk == 0)
        def _():
            dk_acc[...] = jnp.zeros_like(dk_acc)
            dv_acc[...] = jnp.zeros_like(dv_acc)
            z_ref[...] = jnp.zeros_like(z_ref)

        start = _na_start(blk)
        q = q_ref[...]
        do = do_ref[...]
        kw = k_ref[pl.ds(start, NA_WIN), :]
        vw = v_ref[pl.ds(start, NA_WIN), :]
        dqs, dks, dvs = [], [], []
        for hh in range(NA_BWD_HPS):
            sl = slice(hh * HD, (hh + 1) * HD)
            qh = q[:, sl] * QSCALE
            s = lax.dot_general(qh, kw[:, sl], _NT, preferred_element_type=F32) + b_ref[hh]
            p = jnp.exp(s - jnp.max(s, axis=-1, keepdims=True))
            p = p / jnp.sum(p, axis=-1, keepdims=True)
            dp = lax.dot_general(do[:, sl], vw[:, sl], _NT, preferred_element_type=F32)
            ds = p * (dp - jnp.sum(p * dp, axis=-1, keepdims=True))
            dsb = ds.astype(BF16)
            dqs.append(jnp.dot(dsb, kw[:, sl], preferred_element_type=F32) * QSCALE)
            dks.append(lax.dot_general(dsb, qh, _TN, preferred_element_type=F32))
            dvs.append(lax.dot_general(p.astype(BF16), do[:, sl], _TN, preferred_element_type=F32))
            for cls, (i0, ws) in enumerate(NA_CLASSES):
                @pl.when(_na_cls(blk) == cls)
                def _(ds=ds, hh=hh, i0=i0, ws=ws):
                    for qi, kr, dr in _na_pairs(i0, ws):
                        z_ref[hh, dr * GRID_W:(dr + 1) * GRID_W, :] += ds[qi * GRID_W:(qi + 1) * GRID_W, kr * GRID_W:(kr + 1) * GRID_W]
        dqkv_ref[0, pl.ds(pl.multiple_of(blk * NA_QB, NA_QB), NA_QB), :] = jnp.concatenate(dqs, axis=1).astype(dqkv_ref.dtype)
        dk_acc[pl.ds(start, NA_WIN), :] += jnp.concatenate(dks, axis=1)
        dv_acc[pl.ds(start, NA_WIN), :] += jnp.concatenate(dvs, axis=1)

        @pl.when(blk == NA_BLOCKS - 1)
        def _():
            dqkv_ref[1] = dk_acc[...].astype(dqkv_ref.dtype)
            dqkv_ref[2] = dv_acc[...].astype(dqkv_ref.dtype)

    return pl.pallas_call(
        body, grid=(NH // NA_BWD_HPS, NA_BLOCKS),
        in_specs=_na_in_specs(NA_BWD_HPS) + [pl.BlockSpec((NA_QB, lw), lambda hp, b: (b, hp))],
        out_specs=[pl.BlockSpec((3, SEQ, lw), lambda hp, b: (0, 0, hp)), pl.BlockSpec((NA_BWD_HPS, 15 * GRID_W, GRID_W), lambda hp, b: (hp, 0, 0))],
        out_shape=[jax.ShapeDtypeStruct((3, SEQ, DM), BF16), jax.ShapeDtypeStruct((NH, 15 * GRID_W, GRID_W), F32)],
        scratch_shapes=[pltpu.VMEM((SEQ, lw), F32), pltpu.VMEM((SEQ, lw), F32)],
        compiler_params=_params(("parallel", "arbitrary")), name="na_bwd")(qkv, qkv, qkv, bias, do)


def _rpb_grad(z):
    z2 = z.reshape(NH * 15, GRID_W * GRID_W)

    def body(z_ref, e_ref, o_ref):
        o_ref[...] = jnp.dot(z_ref[...], e_ref[...], preferred_element_type=F32, precision=lax.Precision.HIGHEST)

    out = pl.pallas_call(body, out_shape=jax.ShapeDtypeStruct((NH * 15, 128), F32), name="rpb_grad",
                         compiler_params=pltpu.CompilerParams(vmem_limit_bytes=VMEM_LIMIT))(z2, _diag_onehot())
    return out[:, :31].reshape(NH, 15, 31)


DIL_BLOCKS = SEQ // DIL_QB


COLS = 128


def _col_spec():
    return pl.BlockSpec((SEQ, COLS), lambda j: (0, j))


def _grp_spec():
    return pl.BlockSpec((3, SEQ, COLS), lambda j: (0, 0, j))


def _store_group_order(dst_ref, src_ref):
    for g, d in enumerate(DIL):
        n = SEQ // d
        for r in range(d):
            dst_ref[g, r * n:(r + 1) * n, :] = src_ref[pl.ds(r, n, stride=d), :].astype(dst_ref.dtype)


def _store_token_order(dst_ref, src_ref, g):
    d = DIL[g]
    n = SEQ // d
    for r in range(d):
        dst_ref[pl.ds(r, n, stride=d), :] = src_ref[g, r * n:(r + 1) * n, :]


def _to_groups(name, a):
    def body(a_ref, o_ref):
        _store_group_order(o_ref, a_ref)

    return pl.pallas_call(body, grid=(DM // COLS,), in_specs=[_col_spec()], out_specs=_grp_spec(),
                          out_shape=jax.ShapeDtypeStruct((3, SEQ, DM), BF16), compiler_params=_params(("parallel",)), name=name)(a)


def _from_groups_sum(name, a):
    def body(a_ref, o_ref, t1, t2):
        _store_token_order(t1, a_ref, 1)
        _store_token_order(t2, a_ref, 2)
        o_ref[...] = (a_ref[0] + t1[...]) + t2[...]

    return pl.pallas_call(body, grid=(DM // COLS,), in_specs=[_grp_spec()], out_specs=_col_spec(),
                          out_shape=jax.ShapeDtypeStruct((SEQ, DM), F32), scratch_shapes=[pltpu.VMEM((SEQ, COLS), F32)] * 2,
                          compiler_params=_params(("parallel",)), name=name)(a)


def _dil_start(b):
    return pl.multiple_of(jnp.clip(b * DIL_QB - DIL_RADIUS, 0, SEQ - DIL_WIN), DIL_RADIUS)


def _dil_mask(g, b, start):
    shift = 11 - 2 * g
    ii = b * DIL_QB + lax.broadcasted_iota(jnp.int32, (DIL_QB, DIL_WIN), 0)
    jj = start + lax.broadcasted_iota(jnp.int32, (DIL_QB, DIL_WIN), 1)
    dist = jnp.abs(ii - jj)
    valid = (dist <= DIL_RADIUS) & (jnp.right_shift(ii, shift) == jnp.right_shift(jj, shift))
    return valid, dist.astype(F32)


def _dil_in_specs():
    return [pl.BlockSpec(memory_space=pltpu.SMEM),
            pl.BlockSpec((None, DIL_QB, LW), lambda g, hp, b: (g, b, hp)),
            pl.BlockSpec((None, SEQ, LW), lambda g, hp, b: (g, 0, NLW + hp)),
            pl.BlockSpec((None, SEQ, LW), lambda g, hp, b: (g, 0, 2 * NLW + hp))]


def _dil_fwd(qkv, slopes):
    def body(sl_ref, q_ref, k_ref, v_ref, o_ref, lse_ref):
        g, hp, b = pl.program_id(0), pl.program_id(1), pl.program_id(2)
        start = _dil_start(b)
        valid, dist = _dil_mask(g, b, start)
        dil = jnp.left_shift(1, 2 * g).astype(F32)
        q = q_ref[...]
        kw = k_ref[pl.ds(start, DIL_WIN), :]
        vw = v_ref[pl.ds(start, DIL_WIN), :]
        outs, lses = [], []
        for hh in range(HPS):
            sl = slice(hh * HD, (hh + 1) * HD)
            s = lax.dot_general(q[:, sl] * QSCALE, kw[:, sl], _NT, preferred_element_type=F32)
            s = jnp.where(valid, s - (sl_ref[hp * HPS + hh] * dil) * dist, NEG)
            m = jnp.max(s, axis=-1, keepdims=True)
            p = jnp.exp(s - m)
            l = jnp.sum(p, axis=-1, keepdims=True)
            outs.append(jnp.dot(p.astype(BF16), vw[:, sl], preferred_element_type=F32) / l)
            lses.append(jnp.broadcast_to(m + jnp.log(l), (DIL_QB, HD)))
        o_ref[...] = jnp.concatenate(outs, axis=1)
        lse_ref[...] = jnp.concatenate(lses, axis=1)

    ospec = pl.BlockSpec((None, DIL_QB, LW), lambda g, hp, b: (g, b, hp))
    sh = jax.ShapeDtypeStruct((3, SEQ, DM), F32)
    return pl.pallas_call(
        body, grid=(3, NLW, DIL_BLOCKS), in_specs=_dil_in_specs(), out_specs=[ospec, ospec], out_shape=[sh, sh],
        compiler_params=_params(("parallel", "parallel", "arbitrary")), name="dil_fwd")(slopes, qkv, qkv, qkv)


def _dil_merge(o_all, lse_all):
    def body(o_ref, l_ref, out_ref, lse_ref, o1, o2, l1, l2):
        for g, (ot, lt) in ((1, (o1, l1)), (2, (o2, l2))):
            _store_token_order(ot, o_ref, g)
            _store_token_order(lt, l_ref, g)
        la, lb, lc = l_ref[0], l1[...], l2[...]
        m = jnp.maximum(jnp.maximum(la, lb), lc)
        wa, wb, wc = jnp.exp(la - m), jnp.exp(lb - m), jnp.exp(lc - m)
        sw = (wa + wb) + wc
        out_ref[...] = (((wa * o_ref[0] + wb * o1[...]) + wc * o2[...]) / sw).astype(out_ref.dtype)
        lse_ref[...] = m + jnp.log(sw)

    return pl.pallas_call(
        body, grid=(DM // COLS,), in_specs=[_grp_spec(), _grp_spec()], out_specs=[_col_spec(), _col_spec()],
        out_shape=[jax.ShapeDtypeStruct((SEQ, DM), BF16), jax.ShapeDtypeStruct((SEQ, DM), F32)],
        scratch_shapes=[pltpu.VMEM((SEQ, COLS), F32)] * 4, compiler_params=_params(("parallel",)), name="dil_merge")(o_all, lse_all)


def _dil_bwd_prep(do, o, lse):
    def body(do_ref, o_ref, lse_ref, dog_ref, ddg_ref, lseg_ref, dd):
        prod = do_ref[...] * o_ref[...].astype(F32)
        dd[...] = jnp.concatenate(
            [jnp.broadcast_to(jnp.sum(prod[:, h * HD:(h + 1) * HD], axis=-1, keepdims=True), (SEQ, HD)) for h in range(COLS // HD)], axis=1)
        _store_group_order(dog_ref, do_ref)
        _store_group_order(ddg_ref, dd)
        _store_group_order(lseg_ref, lse_ref)

    return pl.pallas_call(
        body, grid=(DM // COLS,), in_specs=[_col_spec()] * 3, out_specs=[_grp_spec()] * 3,
        out_shape=[jax.ShapeDtypeStruct((3, SEQ, DM), BF16), jax.ShapeDtypeStruct((3, SEQ, DM), F32), jax.ShapeDtypeStruct((3, SEQ, DM), F32)],
        scratch_shapes=[pltpu.VMEM((SEQ, COLS), F32)], compiler_params=_params(("parallel",)), name="dil_bwd_prep")(do, o, lse)


def _dil_bwd(qkv, do, dd, lse, slopes):
    def body(sl_ref, q_ref, k_ref, v_ref, do_ref, dd_ref, lse_ref, dqkv_ref, dk_acc, dv_acc):
        g, hp, b = pl.program_id(0), pl.program_id(1), pl.program_id(2)

        @pl.when(b == 0)
        def _():
            dk_acc[...] = jnp.zeros_like(dk_acc)
            dv_acc[...] = jnp.zeros_like(dv_acc)

        start = _dil_start(b)
        valid, dist = _dil_mask(g, b, start)
        dil = jnp.left_shift(1, 2 * g).astype(F32)
        q = q_ref[...]
        do = do_ref[...]
        kw = k_ref[pl.ds(start, DIL_WIN), :]
        vw = v_ref[pl.ds(start, DIL_WIN), :]
        lse = lse_ref[...]
        dd = dd_ref[...]
        dqs, dks, dvs = [], [], []
        for hh in range(HPS):
            sl = slice(hh * HD, (hh + 1) * HD)
            qh = q[:, sl] * QSCALE
            s = lax.dot_general(qh, kw[:, sl], _NT, preferred_element_type=F32)
            s = jnp.where(valid, s - (sl_ref[hp * HPS + hh] * dil) * dist, NEG)
            p = jnp.exp(s - lse[:, hh * HD:hh * HD + 1])
            dp = lax.dot_general(do[:, sl], vw[:, sl], _NT, preferred_element_type=F32)
            dsb = (p * (dp - dd[:, hh * HD:hh * HD + 1])).astype(BF16)
            dqs.append(jnp.dot(dsb, kw[:, sl], preferred_element_type=F32) * QSCALE)
            dks.append(lax.dot_general(dsb, qh, _TN, preferred_element_type=F32))
            dvs.append(lax.dot_general(p.astype(BF16), do[:, sl], _TN, preferred_element_type=F32))
        dqkv_ref[0, pl.ds(pl.multiple_of(b * DIL_QB, DIL_QB), DIL_QB), :] = jnp.concatenate(dqs, axis=1).astype(dqkv_ref.dtype)
        dk_acc[pl.ds(start, DIL_WIN), :] += jnp.concatenate(dks, axis=1)
        dv_acc[pl.ds(start, DIL_WIN), :] += jnp.concatenate(dvs, axis=1)

        @pl.when(b == DIL_BLOCKS - 1)
        def _():
            dqkv_ref[1] = dk_acc[...].astype(dqkv_ref.dtype)
            dqkv_ref[2] = dv_acc[...].astype(dqkv_ref.dtype)

    rspec = pl.BlockSpec((None, DIL_QB, LW), lambda g, hp, b: (g, b, hp))
    return pl.pallas_call(
        body, grid=(3, NLW, DIL_BLOCKS), in_specs=_dil_in_specs() + [rspec, rspec, rspec],
        out_specs=pl.BlockSpec((None, 3, SEQ, LW), lambda g, hp, b: (g, 0, 0, hp)),
        out_shape=jax.ShapeDtypeStruct((3, 3, SEQ, DM), BF16),
        scratch_shapes=[pltpu.VMEM((SEQ, LW), F32), pltpu.VMEM((SEQ, LW), F32)],
        compiler_params=_params(("parallel", "parallel", "arbitrary")), name="dil_bwd")(slopes, qkv, qkv, qkv, do, dd, lse)


def _ffn_block(layer, x, g_pre, g_post, w):
    tag = f"l{layer}"
    h = _rms_fwd(f"{tag}_ffn_pre", x, g_pre)
    gate = _ffn_in(f"{tag}_gate", h, w["ffn_w_gate"], layer)
    up = _ffn_in(f"{tag}_up", h, w["ffn_w_up"], layer)
    act = _swiglu_fwd(f"{tag}_swiglu", gate, up)
    u = _ffn_out(f"{tag}_down", act, w["ffn_w_down"], layer)
    return _resid_norm(f"{tag}_ffn_post", x, u, g_post), (x, h, gate, up, act, u)


def _ffn_block_bwd(layer, dx, saved, g_pre, g_post, w, into):
    tag = f"l{layer}"
    x, h, gate, up, act, u = saved
    du, dg_post = _norm_bwd(f"{tag}_ffn_post_bwd", [dx], u, g_post)
    d_wd = _ffn_bwd_dwd(f"{tag}_dwd", act, du, layer, into[2])
    dact = _ffn_bwd_dact(f"{tag}_dact", du, w["ffn_w_down"], layer)
    dgate, dup = _swiglu_bwd(f"{tag}_swiglu_bwd", dact, gate, up)
    d_wg = _ffn_bwd_dwin(f"{tag}_dwg", h, dgate, layer, into[0])
    d_wu = _ffn_bwd_dwin(f"{tag}_dwu", h, dup, layer, into[1])
    dh = _ffn_bwd_dh(f"{tag}_ffn_dh", dgate, w["ffn_w_gate"], dup, w["ffn_w_up"], layer)
    dx_in, dg_pre = _norm_bwd(f"{tag}_ffn_pre_bwd", [dh], x, g_pre, res=dx)
    return dx_in, dg_pre, dg_post, (d_wg, d_wu, d_wd)


def _alibi_slopes():
    return 2.0 ** (-8.0 * jnp.arange(1, NH + 1, dtype=F32) / NH)


def _local_step(x, target, norms, rpb, w):
    g_mix_pre, g_mix_post, g_ffn_pre, g_ffn_post = norms
    row = lambda a, i: a[i:i + 1]
    na_wo = w["na_w_o"].reshape(DM, DM)
    dil_wo = w["dil_w_o"].reshape(DM, DM)

    bias = _na_bias_tiles(rpb)
    h0 = _rms_fwd("l0_mix_pre", x, row(g_mix_pre, 0))
    qkv0 = _qkv_fwd("l0_qkv", h0[None], w["na_w_qkv"])
    o0 = _na_fwd(qkv0[0], bias)
    u0 = _proj_fwd("l0_proj", o0, na_wo)
    x1 = _resid_norm("l0_mix_post", x, u0, row(g_mix_post, 0))
    x2, ffn0 = _ffn_block(0, x1, row(g_ffn_pre, 0), row(g_ffn_post, 0), w)

    slopes = _alibi_slopes()
    h2g = _to_groups("l1_h_groups", _rms_fwd("l1_mix_pre", x2, row(g_mix_pre, 1), F32))
    qkv1 = _qkv_fwd("l1_qkv", h2g, w["dil_w_qkv"])
    og, lg = _dil_fwd(qkv1, slopes)
    o1, lse = _dil_merge(og, lg)
    u1 = _proj_fwd("l1_proj", o1, dil_wo)
    x3 = _resid_norm("l1_mix_post", x2, u1, row(g_mix_post, 1))
    x4, ffn1 = _ffn_block(1, x3, row(g_ffn_pre, 1), row(g_ffn_post, 1), w)

    dx4, loss_row = _loss_grad("loss", x4, target)

    dx3, dg_fpre1, dg_fpost1, d_ffn = _ffn_block_bwd(1, dx4, ffn1, row(g_ffn_pre, 1), row(g_ffn_post, 1), w, (None, None, None))
    du1, dg_mpost1 = _norm_bwd("l1_mix_post_bwd", [dx3], u1, row(g_mix_post, 1))
    d_dil_wo = _proj_bwd_dw("l1_dwo", o1, du1)
    do1 = _proj_bwd_do("l1_do", du1, dil_wo, F32)
    dog, ddg, lseg = _dil_bwd_prep(do1, o1, lse)
    dqkv1 = _dil_bwd(qkv1, dog, ddg, lseg, slopes)
    d_dil_wqkv = _qkv_bwd_dw("l1_dwqkv", h2g, dqkv1, w["dil_w_qkv"].shape[2])
    dh2 = _from_groups_sum("l1_dh_tokens", _qkv_bwd_dh("l1_dh", dqkv1, w["dil_w_qkv"]))
    dx2, dg_mpre1 = _norm_bwd("l1_mix_pre_bwd", [dh2], x2, row(g_mix_pre, 1), res=dx3)

    dx1, dg_fpre0, dg_fpost0, d_ffn = _ffn_block_bwd(0, dx2, ffn0, row(g_ffn_pre, 0), row(g_ffn_post, 0), w, d_ffn)
    du0, dg_mpost0 = _norm_bwd("l0_mix_post_bwd", [dx1], u0, row(g_mix_post, 0))
    d_na_wo = _proj_bwd_dw("l0_dwo", o0, du0)
    do0 = _proj_bwd_do("l0_do", du0, na_wo)
    dqkv0, z = _na_bwd(qkv0[0], bias, do0)
    d_rpb = _rpb_grad(z)
    d_na_wqkv = _qkv_bwd_dw("l0_dwqkv", h0[None], dqkv0[None], w["na_w_qkv"].shape[2])
    dh0 = _qkv_bwd_dh("l0_dh", dqkv0[None], w["na_w_qkv"])
    dx0, dg_mpre0 = _norm_bwd("l0_mix_pre_bwd", [dh0[0]], x, row(g_mix_pre, 0), res=dx1)

    dnorms = (jnp.concatenate([dg_mpre0, dg_mpre1]), jnp.concatenate([dg_mpost0, dg_mpost1]),
              jnp.concatenate([dg_fpre0, dg_fpre1]), jnp.concatenate([dg_fpost0, dg_fpost1]))
    dw = {
        "na_w_qkv": d_na_wqkv, "na_w_o": d_na_wo.reshape(NCHIP, DM // NCHIP, DM),
        "dil_w_qkv": d_dil_wqkv, "dil_w_o": d_dil_wo.reshape(NCHIP, DM // NCHIP, DM),
        "ffn_w_gate": d_ffn[0], "ffn_w_up": d_ffn[1], "ffn_w_down": d_ffn[2],
    }
    return loss_row, dx0, dnorms, d_rpb, dw


WEIGHT_NAMES = ("na_w_qkv", "na_w_o", "ffn_w_gate", "ffn_w_up", "ffn_w_down", "dil_w_qkv", "dil_w_o")
HBM_SPEC = pl.BlockSpec(memory_space=pltpu.HBM)


def _place():
    x, y, c = lax.axis_index("x"), lax.axis_index("y"), lax.axis_index("c")
    chips = ((1 - x, y), (x, 1 - y), (1 - x, 1 - y))
    return x, y, c, chips


def _chip_id(chip):
    return 2 * chip[0] + chip[1]


def _comm_call(name, body, ins, out_shapes, n_sems, aliases=None):
    return pl.pallas_call(
        body, in_specs=[HBM_SPEC] * len(ins), out_specs=[HBM_SPEC] * len(out_shapes), out_shape=out_shapes,
        scratch_shapes=[pltpu.SemaphoreType.DMA((k,)) for k in n_sems], input_output_aliases=aliases or {},
        compiler_params=pltpu.CompilerParams(has_side_effects=True), name=name)(*ins)


def _gather_weights(shards):
    n = len(shards)

    def body(*refs):
        src, out = refs[:n], refs[n:2 * n]
        send_sems, recv_sems = refs[2 * n:]
        x, y, c, chips = _place()
        sibling = (x, y, 1 - c)

        def copy(t, k, chip, half, to, from_src=False):
            blk = out[t].at[_chip_id(chip), half]
            return pltpu.make_async_remote_copy(
                src_ref=src[t].at[half] if from_src else blk, dst_ref=blk,
                send_sem=send_sems.at[6 * t + k], recv_sem=recv_sems.at[6 * t + k], device_id=to, device_id_type=MESH)

        first = [copy(t, j, (x, y), c, (*chip, c), from_src=True) for t in range(n) for j, chip in enumerate(chips)]
        for cp in first:
            cp.start()
        passed = []
        for t in range(n):
            for j, chip in enumerate(chips):
                copy(t, j, chip, c, (x, y, c)).wait_recv()
                fwd = copy(t, 3 + j, chip, c, sibling)
                fwd.start()
                passed.append(fwd)
        for t in range(n):
            for j, chip in enumerate(chips):
                copy(t, 3 + j, chip, 1 - c, (x, y, c)).wait_recv()
        for cp in first + passed:
            cp.wait_send()

    return _comm_call("gather_weights", body, shards, [jax.ShapeDtypeStruct((NCHIP,) + s.shape, s.dtype) for s in shards], (6 * n, 6 * n))


def _pair_exchange(grads):
    n = len(grads)

    def body(*refs):
        g, theirs = refs[:n], refs[n:2 * n]
        send_sems, recv_sems = refs[2 * n:]
        x, y, c, _ = _place()
        swap = [pltpu.make_async_remote_copy(src_ref=g[t].at[:, 1 - c], dst_ref=theirs[t], send_sem=send_sems.at[t],
                                             recv_sem=recv_sems.at[t], device_id=(x, y, 1 - c), device_id_type=MESH) for t in range(n)]
        for cp in swap:
            cp.start()
        for cp in swap:
            cp.wait()

    return _comm_call("grad_pair_exchange", body, grads, [jax.ShapeDtypeStruct((NCHIP,) + g.shape[2:], g.dtype) for g in grads], (n, n))


def _chip_exchange(parts):
    n = len(parts)

    def body(*refs):
        p, slots = refs[:n], refs[n:2 * n]
        send_sems, recv_sems = refs[2 * n:]
        x, y, c, chips = _place()

        def copy(t, j):
            return pltpu.make_async_remote_copy(src_ref=p[t].at[_chip_id(chips[j])], dst_ref=slots[t].at[j], send_sem=send_sems.at[3 * t + j],
                                                recv_sem=recv_sems.at[3 * t + j], device_id=(*chips[j], c), device_id_type=MESH)

        sends = [copy(t, j) for t in range(n) for j in range(3)]
        for cp in sends:
            cp.start()
        for cp in sends:
            cp.wait()

    return _comm_call("grad_chip_exchange", body, parts, [jax.ShapeDtypeStruct((3,) + p.shape[1:], p.dtype) for p in parts], (3 * n, 3 * n))


def _pair_share(full):
    n = len(full)

    def body(*refs):
        buf = refs[n:2 * n]
        send_sems, recv_sems = refs[2 * n:]
        x, y, c, _ = _place()
        sends = [pltpu.make_async_remote_copy(src_ref=buf[t].at[c], dst_ref=buf[t].at[c], send_sem=send_sems.at[t], recv_sem=recv_sems.at[t],
                                              device_id=(x, y, 1 - c), device_id_type=MESH) for t in range(n)]
        for cp in sends:
            cp.start()
        for t in range(n):
            pltpu.make_async_remote_copy(src_ref=buf[t].at[c], dst_ref=buf[t].at[1 - c], send_sem=send_sems.at[t], recv_sem=recv_sems.at[t],
                                         device_id=(x, y, 1 - c), device_id_type=MESH).wait_recv()
        for cp in sends:
            cp.wait_send()

    return _comm_call("grad_pair_share", body, full, [jax.ShapeDtypeStruct(f.shape, f.dtype) for f in full], (n, n),
                      aliases={t: t for t in range(n)})


SMALL_ROWS = 128


def _allreduce_small(v):
    def body(v_ref, o_ref, buf, send_sems, recv_sems):
        x, y, c, _ = _place()
        me = 4 * x + 2 * y + c
        flip = lambda a, f: 1 - a if f else a
        buf[me] = v_ref[...]
        peers = [(flip(x, d >> 2 & 1), flip(y, d >> 1 & 1), flip(c, d & 1)) for d in range(1, 8)]
        sends = [pltpu.make_async_remote_copy(src_ref=v_ref, dst_ref=buf.at[me], send_sem=send_sems.at[i], recv_sem=recv_sems.at[i],
                                              device_id=peer, device_id_type=MESH) for i, peer in enumerate(peers)]
        for cp in sends:
            cp.start()
        for i, (px, py, pc) in enumerate(peers):
            pltpu.make_async_remote_copy(src_ref=v_ref, dst_ref=buf.at[4 * px + 2 * py + pc], send_sem=send_sems.at[i], recv_sem=recv_sems.at[i],
                                         device_id=(px, py, pc), device_id_type=MESH).wait_recv()
        for cp in sends:
            cp.wait_send()
        acc = buf[0]
        for k in range(1, 8):
            acc = acc + buf[k]
        o_ref[...] = acc

    vm = pl.BlockSpec(memory_space=pltpu.VMEM)
    return pl.pallas_call(
        body, in_specs=[vm], out_specs=vm, out_shape=jax.ShapeDtypeStruct((SMALL_ROWS, 128), F32),
        scratch_shapes=[pltpu.VMEM((8, SMALL_ROWS, 128), F32), pltpu.SemaphoreType.DMA((7,)), pltpu.SemaphoreType.DMA((7,))],
        compiler_params=pltpu.CompilerParams(has_side_effects=True), name="allreduce_small")(v)


def _row_block(rows, cols, budget=1 << 20):
    best = 8
    for bm in range(8, rows + 1, 8):
        if rows % bm == 0 and bm * cols * 4 <= budget:
            best = bm
    return best


def _pair_sum(name, place, g, theirs):
    _, m, c = theirs.shape
    bm = _row_block(m, c)

    def body(place_ref, a_ref, b_ref, o_ref):
        o_ref[...] = (a_ref[...].astype(F32) + b_ref[...].astype(F32)).astype(o_ref.dtype)

    spec = pl.BlockSpec((None, bm, c), lambda k, i, pr: (k, i, 0))
    return pl.pallas_call(
        body, out_shape=jax.ShapeDtypeStruct(theirs.shape, BF16),
        grid_spec=pltpu.PrefetchScalarGridSpec(
            num_scalar_prefetch=1, grid=(NCHIP, m // bm),
            in_specs=[pl.BlockSpec((None, None, bm, c), lambda k, i, pr: (k, pr[0], i, 0)), spec], out_specs=spec),
        compiler_params=_params(("parallel", "parallel")), name=name)(place, g, theirs)


def _chip_sum(name, place, parts, slots):
    _, m, c = parts.shape
    bm = _row_block(m, c)

    def body(place_ref, p_ref, s_ref, o_ref):
        s = s_ref[...].astype(F32)
        o_ref[...] = ((p_ref[...].astype(F32) + s[0]) + s[1]) + s[2]

    return pl.pallas_call(
        body, out_shape=jax.ShapeDtypeStruct((2, m, c), F32),
        grid_spec=pltpu.PrefetchScalarGridSpec(
            num_scalar_prefetch=1, grid=(m // bm,),
            in_specs=[pl.BlockSpec((None, bm, c), lambda i, pr: (pr[1], i, 0)), pl.BlockSpec((3, bm, c), lambda i, pr: (0, i, 0))],
            out_specs=pl.BlockSpec((None, bm, c), lambda i, pr: (pr[0], i, 0))),
        compiler_params=_params(("parallel",)), name=name)(place, parts, slots)


def _adamw(name, w, g, m, v):
    lead, rows, cols = w.shape
    bm = _row_block(rows, cols, budget=768 * 1024)
    c1 = 1.0 - ADAM_B1 ** ADAM_STEP
    c2 = 1.0 - ADAM_B2 ** ADAM_STEP

    def body(w_ref, g_ref, m_ref, v_ref, go_ref, d_ref, mo_ref, vo_ref):
        g = g_ref[...]
        mn = ADAM_B1 * m_ref[...] + (1.0 - ADAM_B1) * g
        vn = ADAM_B2 * v_ref[...] + (1.0 - ADAM_B2) * (g * g)
        go_ref[...] = g
        mo_ref[...] = mn
        vo_ref[...] = vn
        d_ref[...] = -ADAM_LR * ((mn / c1) / (jnp.sqrt(vn / c2) + ADAM_EPS) + ADAM_WD * w_ref[...])

    spec = pl.BlockSpec((None, bm, cols), lambda l, i: (l, i, 0))
    sh = jax.ShapeDtypeStruct((lead, rows, cols), F32)
    return pl.pallas_call(body, grid=(lead, rows // bm), in_specs=[spec] * 4, out_specs=[spec] * 4, out_shape=[sh] * 4,
                          compiler_params=_params(("parallel", "parallel")), name=name)(w, g, m, v)


def _pack_small(norms, rpb):
    flat = jnp.concatenate([a.reshape(-1) for a in norms] + [rpb.reshape(-1)])
    return jnp.pad(flat, (0, SMALL_ROWS * 128 - flat.shape[0])).reshape(SMALL_ROWS, 128)


def _unpack_small(p):
    flat = p.reshape(-1)
    norms = [flat[i * 2 * DM:(i + 1) * 2 * DM].reshape(2, DM) for i in range(4)]
    rpb = flat[8 * DM:8 * DM + NH * 15 * 31].reshape(1, NH, 15, 31)
    return norms, rpb


def kernel(x, norm_mix_pre, norm_mix_post, norm_ffn_pre, norm_ffn_post, na_w_qkv, na_w_o, na_rpb, dil_w_qkv, dil_w_o, ffn_w_gate, ffn_w_up, ffn_w_down, loss_target, m_norm_mix_pre, m_norm_mix_post, m_norm_ffn_pre, m_norm_ffn_post, m_na_w_qkv, m_na_w_o, m_na_rpb, m_dil_w_qkv, m_dil_w_o, m_ffn_w_gate, m_ffn_w_up, m_ffn_w_down, v_norm_mix_pre, v_norm_mix_post, v_norm_ffn_pre, v_norm_ffn_post, v_na_w_qkv, v_na_w_o, v_na_rpb, v_dil_w_qkv, v_dil_w_o, v_ffn_w_gate, v_ffn_w_up, v_ffn_w_down):
    weights = {"na_w_qkv": na_w_qkv[0], "na_w_o": na_w_o[0], "dil_w_qkv": dil_w_qkv[0], "dil_w_o": dil_w_o[0],
               "ffn_w_gate": ffn_w_gate, "ffn_w_up": ffn_w_up, "ffn_w_down": ffn_w_down}
    m_in = {"na_w_qkv": m_na_w_qkv[0], "na_w_o": m_na_w_o[0], "dil_w_qkv": m_dil_w_qkv[0], "dil_w_o": m_dil_w_o[0],
            "ffn_w_gate": m_ffn_w_gate, "ffn_w_up": m_ffn_w_up, "ffn_w_down": m_ffn_w_down}
    v_in = {"na_w_qkv": v_na_w_qkv[0], "na_w_o": v_na_w_o[0], "dil_w_qkv": v_dil_w_qkv[0], "dil_w_o": v_dil_w_o[0],
            "ffn_w_gate": v_ffn_w_gate, "ffn_w_up": v_ffn_w_up, "ffn_w_down": v_ffn_w_down}
    halves = lambda a: a.reshape(2, -1, a.shape[-1])
    flat2 = lambda a: a.reshape(-1, a.shape[-1])

    chip = 2 * lax.axis_index("x") + lax.axis_index("y")
    place = jnp.stack([lax.axis_index("c"), chip]).astype(jnp.int32)
    own = [halves(weights[n]).astype(BF16) for n in WEIGHT_NAMES]
    gathered = _gather_weights(own)
    w = {}
    for n, gw, ow in zip(WEIGHT_NAMES, gathered, own):
        gw = lax.dynamic_update_slice(gw, ow[None], (chip, 0, 0, 0))
        w[n] = gw if weights[n].ndim == 3 else gw.reshape((NCHIP,) + weights[n].shape)

    norms = (norm_mix_pre, norm_mix_post, norm_ffn_pre, norm_ffn_post)
    loss_row, dx, dnorms, d_rpb, dw = _local_step(x[0], loss_target[0], norms, na_rpb[0], w)
    loss = lax.psum(loss_row[0, 0], ("x", "y", "c"))

    grads = [dw[n].reshape((NCHIP, 2, -1, dw[n].shape[-1])) for n in WEIGHT_NAMES]
    theirs = _pair_exchange(grads)
    parts = [_pair_sum(f"pair_sum_{n}", place, a, b) for n, a, b in zip(WEIGHT_NAMES, grads, theirs)]
    slots = _chip_exchange(parts)
    sums = [_chip_sum(f"chip_sum_{n}", place, p, s) for n, p, s in zip(WEIGHT_NAMES, parts, slots)]
    full = _pair_share(sums)
    small = _allreduce_small(_pack_small(dnorms, d_rpb))

    out_g, out_d, out_m, out_v = {}, {}, {}, {}
    for n, gf in zip(WEIGHT_NAMES, full):
        shp = weights[n].shape
        as3 = lambda a: a.reshape((-1,) + shp[-2:])
        res = _adamw(f"adamw_{n}", as3(weights[n]), as3(gf), as3(m_in[n]), as3(v_in[n]))
        lead = (1,) if weights[n].ndim == 2 else ()
        out_g[n], out_d[n], out_m[n], out_v[n] = (r.reshape(lead + shp) for r in res)
    sm_names = ("norm_mix_pre", "norm_mix_post", "norm_ffn_pre", "norm_ffn_post", "na_rpb")
    sm = _adamw("adamw_small", _pack_small(norms, na_rpb)[None],
                small[None], _pack_small((m_norm_mix_pre, m_norm_mix_post, m_norm_ffn_pre, m_norm_ffn_post), m_na_rpb)[None],
                _pack_small((v_norm_mix_pre, v_norm_mix_post, v_norm_ffn_pre, v_norm_ffn_post), v_na_rpb)[None])
    for res, dst in zip(sm, (out_g, out_d, out_m, out_v)):
        ns, rp = _unpack_small(res)
        for n, a in zip(sm_names, ns + [rp]):
            dst[n] = a

    order = ("norm_mix_pre", "norm_mix_post", "norm_ffn_pre", "norm_ffn_post", "na_w_qkv", "na_w_o", "na_rpb", "dil_w_qkv", "dil_w_o",
             "ffn_w_gate", "ffn_w_up", "ffn_w_down")
    return (loss, dx[None], *[out_g[n] for n in order], *[out_d[n] for n in order], *[out_m[n] for n in order], *[out_v[n] for n in order])
```

```python
import functools

import numpy as np
import jax
import jax.numpy as jnp
from jax import lax
from jax.experimental import pallas as pl
from jax.experimental.pallas import tpu as pltpu

F32 = jnp.float32
BF16 = jnp.bfloat16

SEQ = 2048
DM = 1024
NH = 16
HD = 64
DFF = 2816
NCHIP = 4
FSH = DFF // NCHIP
GRID_W = 64
NA_QROWS = 4
NA_QB = NA_QROWS * GRID_W
NA_WROWS = 12
NA_WIN = NA_WROWS * GRID_W
DIL = (1, 4, 16)
DIL_QB = 256
DIL_WIN = DIL_QB + 128
DIL_RADIUS = 64
RMS_EPS = 1e-6
NEG = -1e30
QSCALE = HD ** -0.5
CH = 256
MESH = pl.DeviceIdType.MESH

ADAM_LR, ADAM_B1, ADAM_B2, ADAM_EPS, ADAM_WD, ADAM_STEP = 0.001, 0.9, 0.999, 1e-08, 0.01, 10

VMEM_LIMIT = 56 * 1024 * 1024

_NN = (((1,), (0,)), ((), ()))
_NT = (((1,), (1,)), ((), ()))
_TN = (((0,), (0,)), ((), ()))


def _params(sem):
    return pltpu.CompilerParams(dimension_semantics=sem, vmem_limit_bytes=VMEM_LIMIT)


def _matmul(name, pairs, grid, out_shape, out_spec, acc_shape):
    nk = grid[-1]
    npair = len(pairs)
    n_in = 2 * npair

    def body(*refs):
        ins, o_ref = refs[:2 * npair], refs[n_in]
        part = None
        for p in range(npair):
            d = lax.dot_general(ins[2 * p][...].astype(BF16), ins[2 * p + 1][...].astype(BF16), pairs[p][4],
                                preferred_element_type=F32)
            part = d if part is None else part + d
        if nk == 1:
            o_ref[...] = part.astype(o_ref.dtype)
        else:
            acc_ref = refs[n_in + 1]
            kk = pl.program_id(len(grid) - 1)

            @pl.when(kk == 0)
            def _():
                acc_ref[...] = part

            @pl.when(kk > 0)
            def _():
                acc_ref[...] += part

            @pl.when(kk == nk - 1)
            def _():
                o_ref[...] = acc_ref[...].astype(o_ref.dtype)

    ops, specs = [], []
    for a, a_spec, b, b_spec, _ in pairs:
        ops += [a, b]
        specs += [a_spec, b_spec]
    return pl.pallas_call(
        body, grid=grid, in_specs=specs, out_specs=out_spec, out_shape=out_shape,
        scratch_shapes=[] if nk == 1 else [pltpu.VMEM(acc_shape, F32)],
        compiler_params=_params(("parallel",) * (len(grid) - 1) + ("arbitrary",)), name=name,
    )(*ops)


def _qkv_fwd(name, h_all, w4):
    g_n = h_all.shape[0]
    per = w4.shape[2] // CH
    return _matmul(
        name, [(h_all, pl.BlockSpec((None, SEQ, DM), lambda g, q, k: (g, 0, 0)),
                w4, pl.BlockSpec((None, DM, CH), lambda g, q, k: ((g * 12 + q) // per, 0, (g * 12 + q) % per)), _NN)],
        (g_n, 12, 1), jax.ShapeDtypeStruct((g_n, SEQ, 3 * DM), BF16),
        pl.BlockSpec((None, SEQ, CH), lambda g, q, k: (g, 0, q)), None)


def _qkv_bwd_dh(name, dqkv, w4):
    g_n = dqkv.shape[0]
    per = w4.shape[2] // CH
    tm = 1024
    return _matmul(
        name, [(dqkv, pl.BlockSpec((None, None, tm, CH), lambda g, i, q: (g, q // 4, i, q % 4)),
                w4, pl.BlockSpec((None, DM, CH), lambda g, i, q: ((g * 12 + q) // per, 0, (g * 12 + q) % per)), _NT)],
        (g_n, SEQ // tm, 12), jax.ShapeDtypeStruct((g_n, SEQ, DM), F32),
        pl.BlockSpec((None, tm, DM), lambda g, i, q: (g, i, 0)), (tm, DM))


def _qkv_bwd_dw(name, h_all, dqkv, shard_cols):
    g_n = dqkv.shape[0]
    per = shard_cols // CH
    tk = 512
    return _matmul(
        name, [(h_all, pl.BlockSpec((None, tk, DM), lambda qq, k: (qq // 12, k, 0)),
                dqkv, pl.BlockSpec((None, None, tk, CH), lambda qq, k: (qq // 12, (qq % 12) // 4, k, qq % 4)), _TN)],
        (g_n * 12, SEQ // tk), jax.ShapeDtypeStruct((NCHIP, DM, shard_cols), BF16),
        pl.BlockSpec((None, DM, CH), lambda qq, k: (qq // per, 0, qq % per)), (DM, CH))


def _proj_fwd(name, o, wo):
    tm = 512
    return _matmul(
        name, [(o, pl.BlockSpec((tm, DM), lambda i, k: (i, 0)), wo, pl.BlockSpec((DM, DM), lambda i, k: (0, 0)), _NN)],
        (SEQ // tm, 1), jax.ShapeDtypeStruct((SEQ, DM), F32), pl.BlockSpec((tm, DM), lambda i, k: (i, 0)), None)


def _proj_bwd_do(name, du, wo, dtype=BF16):
    tm = 512
    return _matmul(
        name, [(du, pl.BlockSpec((tm, DM), lambda i, k: (i, 0)), wo, pl.BlockSpec((DM, DM), lambda i, k: (0, 0)), _NT)],
        (SEQ // tm, 1), jax.ShapeDtypeStruct((SEQ, DM), dtype), pl.BlockSpec((tm, DM), lambda i, k: (i, 0)), None)


def _proj_bwd_dw(name, o, du):
    tk, tn = 512, 512
    return _matmul(
        name, [(o, pl.BlockSpec((tk, DM), lambda j, k: (k, 0)), du, pl.BlockSpec((tk, tn), lambda j, k: (k, j)), _TN)],
        (DM // tn, SEQ // tk), jax.ShapeDtypeStruct((DM, DM), BF16), pl.BlockSpec((DM, tn), lambda j, k: (0, j)), (DM, tn))


def _ffn_wspec(index_map):
    return pl.BlockSpec((None, FSH, DM), index_map)


def _ffn_in(name, h, wt4):
    tm = 1024
    return _matmul(
        name, [(h, pl.BlockSpec((tm, DM), lambda i, s, k: (i, 0)), wt4, _ffn_wspec(lambda i, s, k: (s, 0, 0)), _NT)],
        (SEQ // tm, NCHIP, 1), jax.ShapeDtypeStruct((NCHIP, SEQ, FSH), BF16),
        pl.BlockSpec((None, tm, FSH), lambda i, s, k: (s, i, 0)), None)


def _ffn_out(name, act, wd4):
    tm = 1024
    return _matmul(
        name, [(act, pl.BlockSpec((None, tm, FSH), lambda i, s: (s, i, 0)), wd4, _ffn_wspec(lambda i, s: (s, 0, 0)), _NN)],
        (SEQ // tm, NCHIP), jax.ShapeDtypeStruct((SEQ, DM), F32), pl.BlockSpec((tm, DM), lambda i, s: (i, 0)), (tm, DM))


def _ffn_bwd_dact(name, du, wd4):
    tm = 1024
    return _matmul(
        name, [(du, pl.BlockSpec((tm, DM), lambda i, s, k: (i, 0)), wd4, _ffn_wspec(lambda i, s, k: (s, 0, 0)), _NT)],
        (SEQ // tm, NCHIP, 1), jax.ShapeDtypeStruct((NCHIP, SEQ, FSH), BF16),
        pl.BlockSpec((None, tm, FSH), lambda i, s, k: (s, i, 0)), None)


def _ffn_bwd_dw(name, a4, b):
    tk = 512
    return _matmul(
        name, [(a4, pl.BlockSpec((None, tk, FSH), lambda s, k: (s, k, 0)), b, pl.BlockSpec((tk, DM), lambda s, k: (k, 0)), _TN)],
        (NCHIP, SEQ // tk), jax.ShapeDtypeStruct((NCHIP, FSH, DM), BF16), _ffn_wspec(lambda s, k: (s, 0, 0)), (FSH, DM))


def _ffn_bwd_dh(name, dgate, wgt4, dup, wut4):
    tm = 1024
    a_spec = pl.BlockSpec((None, tm, FSH), lambda i, s: (s, i, 0))
    b_spec = _ffn_wspec(lambda i, s: (s, 0, 0))
    return _matmul(
        name, [(dgate, a_spec, wgt4, b_spec, _NN), (dup, a_spec, wut4, b_spec, _NN)],
        (SEQ // tm, NCHIP), jax.ShapeDtypeStruct((SEQ, DM), F32), pl.BlockSpec((tm, DM), lambda i, s: (i, 0)), (tm, DM))


ROWS = 256


def _row_spec():
    return pl.BlockSpec((ROWS, DM), lambda i: (i, 0))


def _vec_spec():
    return pl.BlockSpec((1, DM), lambda i: (0, 0))


def _rms_fwd(name, x, g, dtype=BF16):
    def body(x_ref, g_ref, o_ref):
        x = x_ref[...]
        r = lax.rsqrt(jnp.mean(x * x, axis=-1, keepdims=True) + RMS_EPS)
        o_ref[...] = (x * r * g_ref[...]).astype(o_ref.dtype)

    return pl.pallas_call(body, grid=(SEQ // ROWS,), in_specs=[_row_spec(), _vec_spec()], out_specs=_row_spec(),
                          out_shape=jax.ShapeDtypeStruct((SEQ, DM), dtype), compiler_params=_params(("parallel",)), name=name)(x, g)


def _resid_norm(name, x, u, g):
    def body(x_ref, u_ref, g_ref, o_ref):
        u = u_ref[...]
        r = lax.rsqrt(jnp.mean(u * u, axis=-1, keepdims=True) + RMS_EPS)
        o_ref[...] = x_ref[...] + u * r * g_ref[...]

    return pl.pallas_call(body, grid=(SEQ // ROWS,), in_specs=[_row_spec(), _row_spec(), _vec_spec()], out_specs=_row_spec(),
                          out_shape=jax.ShapeDtypeStruct((SEQ, DM), F32), compiler_params=_params(("parallel",)), name=name)(x, u, g)


def _norm_bwd(name, dys, u, g, res=None):
    ndy = len(dys)

    def body(*refs):
        dy = refs[0][...]
        for r_ in refs[1:ndy]:
            dy = dy + r_[...]
        u_ref, g_ref = refs[ndy], refs[ndy + 1]
        res_ref = refs[ndy + 2] if res is not None else None
        du_ref, dg_ref = refs[-2], refs[-1]
        u = u_ref[...]
        r = lax.rsqrt(jnp.mean(u * u, axis=-1, keepdims=True) + RMS_EPS)
        yh = u * r
        t = dy * g_ref[...]
        du = r * (t - yh * jnp.mean(t * yh, axis=-1, keepdims=True))
        if res_ref is not None:
            du = du + res_ref[...]
        du_ref[...] = du

        @pl.when(pl.program_id(0) == 0)
        def _():
            dg_ref[...] = jnp.zeros_like(dg_ref)

        dg_ref[...] += jnp.sum(dy * yh, axis=0, keepdims=True)

    ops = list(dys) + [u, g] + ([res] if res is not None else [])
    specs = [_row_spec()] * ndy + [_row_spec(), _vec_spec()] + ([_row_spec()] if res is not None else [])
    return pl.pallas_call(
        body, grid=(SEQ // ROWS,), in_specs=specs, out_specs=[_row_spec(), _vec_spec()],
        out_shape=[jax.ShapeDtypeStruct((SEQ, DM), F32), jax.ShapeDtypeStruct((1, DM), F32)],
        compiler_params=_params(("arbitrary",)), name=name)(*ops)


def _loss_grad(name, y, t):
    def body(y_ref, t_ref, dy_ref, l_ref):
        e = y_ref[...] - t_ref[...]
        dy_ref[...] = e * (1.0 / DM)

        @pl.when(pl.program_id(0) == 0)
        def _():
            l_ref[...] = jnp.zeros_like(l_ref)

        l_ref[...] += jnp.sum(e * e) * (0.5 / DM)

    return pl.pallas_call(
        body, grid=(SEQ // ROWS,), in_specs=[_row_spec(), _row_spec()],
        out_specs=[_row_spec(), pl.BlockSpec((1, 128), lambda i: (0, 0))],
        out_shape=[jax.ShapeDtypeStruct((SEQ, DM), F32), jax.ShapeDtypeStruct((1, 128), F32)],
        compiler_params=_params(("arbitrary",)), name=name)(y, t)


def _ffn_spec():
    return pl.BlockSpec((None, 512, FSH), lambda s, i: (s, i, 0))


def _swiglu_fwd(name, gate, up):
    def body(g_ref, u_ref, o_ref):
        g = g_ref[...].astype(F32)
        o_ref[...] = (g * jax.nn.sigmoid(g) * u_ref[...].astype(F32)).astype(o_ref.dtype)

    return pl.pallas_call(body, grid=(NCHIP, SEQ // 512), in_specs=[_ffn_spec(), _ffn_spec()], out_specs=_ffn_spec(),
                          out_shape=jax.ShapeDtypeStruct((NCHIP, SEQ, FSH), BF16),
                          compiler_params=_params(("parallel", "parallel")), name=name)(gate, up)


def _swiglu_bwd(name, dact, gate, up):
    def body(d_ref, g_ref, u_ref, dg_ref, du_ref):
        d = d_ref[...].astype(F32)
        g = g_ref[...].astype(F32)
        u = u_ref[...].astype(F32)
        sg = jax.nn.sigmoid(g)
        dg_ref[...] = (d * u * sg * (1.0 + g * (1.0 - sg))).astype(dg_ref.dtype)
        du_ref[...] = (d * g * sg).astype(du_ref.dtype)

    sh = jax.ShapeDtypeStruct((NCHIP, SEQ, FSH), BF16)
    return pl.pallas_call(body, grid=(NCHIP, SEQ // 512), in_specs=[_ffn_spec()] * 3, out_specs=[_ffn_spec()] * 2,
                          out_shape=[sh, sh], compiler_params=_params(("parallel", "parallel")), name=name)(dact, gate, up)


HBM_SPEC = pl.BlockSpec(memory_space=pltpu.HBM)


class _Carried:
    def __init__(self, ins, out_shapes, n_sems, issue, drain):
        self.ins, self.out_shapes, self.n_sems, self.issue, self.drain = list(ins), list(out_shapes), tuple(n_sems), issue, drain


def _carrier_call(name, body, grid, in_specs, out_specs, out_shape, scratch_shapes, operands, carry):
    n_in, n_out, n_scr = len(in_specs), len(out_specs), len(scratch_shapes)
    if carry is None:
        res = pl.pallas_call(body, grid=grid, in_specs=in_specs, out_specs=out_specs, out_shape=out_shape, scratch_shapes=scratch_shapes,
                             compiler_params=_params(("arbitrary",) * len(grid)), name=name)(*operands)
        return list(res), []
    ci, co = len(carry.ins), len(carry.out_shapes)

    def wrapped(*refs):
        ins, cins = refs[:n_in], refs[n_in:n_in + ci]
        outs, couts = refs[n_in + ci:n_in + ci + n_out], refs[n_in + ci + n_out:n_in + ci + n_out + co]
        scr, sems = refs[n_in + ci + n_out + co:n_in + ci + n_out + co + n_scr], refs[n_in + ci + n_out + co + n_scr:]
        first = functools.reduce(jnp.logical_and, [pl.program_id(a) == 0 for a in range(len(grid))])
        last = functools.reduce(jnp.logical_and, [pl.program_id(a) == grid[a] - 1 for a in range(len(grid))])

        @pl.when(first)
        def _():
            carry.issue(cins, couts, sems)

        body(*ins, *outs, *scr)

        @pl.when(last)
        def _():
            carry.drain(cins, couts, sems)

    res = pl.pallas_call(
        wrapped, grid=grid, in_specs=list(in_specs) + [HBM_SPEC] * ci, out_specs=list(out_specs) + [HBM_SPEC] * co,
        out_shape=list(out_shape) + carry.out_shapes,
        scratch_shapes=list(scratch_shapes) + [pltpu.SemaphoreType.DMA((k,)) for k in carry.n_sems],
        compiler_params=pltpu.CompilerParams(dimension_semantics=("arbitrary",) * len(grid), vmem_limit_bytes=VMEM_LIMIT, has_side_effects=True),
        name=name)(*operands, *carry.ins)
    return list(res[:n_out]), list(res[n_out:])


def _run_carried(name, carry):
    def body(*refs):
        ci, co = len(carry.ins), len(carry.out_shapes)
        carry.issue(refs[:ci], refs[ci:ci + co], refs[ci + co:])
        carry.drain(refs[:ci], refs[ci:ci + co], refs[ci + co:])

    return pl.pallas_call(
        body, in_specs=[HBM_SPEC] * len(carry.ins), out_specs=[HBM_SPEC] * len(carry.out_shapes), out_shape=carry.out_shapes,
        scratch_shapes=[pltpu.SemaphoreType.DMA((k,)) for k in carry.n_sems],
        compiler_params=pltpu.CompilerParams(has_side_effects=True), name=name)(*carry.ins)


NA_BLOCKS = SEQ // NA_QB
NA_ROWS_TOTAL = SEQ // GRID_W
NA_CLASSES = ((0, 0), (8, 4), (NA_ROWS_TOTAL - NA_QROWS, NA_ROWS_TOTAL - NA_WROWS))


def _na_pairs(i0, ws):
    out = []
    for qi in range(NA_QROWS):
        i = i0 + qi
        rs = min(max(i - 4, 0), NA_ROWS_TOTAL - 8)
        for kr in range(NA_WROWS):
            r = ws + kr
            if rs <= r < rs + 8:
                out.append((qi, kr, r - i + 7))
    return out


def _diag_onehot():
    qc, kc = np.meshgrid(np.arange(GRID_W), np.arange(GRID_W), indexing="ij")
    e = np.zeros((GRID_W * GRID_W, 128), np.float32)
    j = (kc - qc + 15).reshape(-1)
    ok = (j >= 0) & (j <= 30)
    e[np.arange(GRID_W * GRID_W)[ok], j[ok]] = 1.0
    return jnp.asarray(e)


def _rpb_expand(rpb):
    r2 = jnp.pad(rpb.reshape(NH * 15, 31), ((0, 0), (0, 128 - 31)))

    def body(r_ref, e_ref, o_ref):
        o_ref[...] = lax.dot_general(r_ref[...], e_ref[...], _NT, preferred_element_type=F32, precision=lax.Precision.HIGHEST)

    out = pl.pallas_call(body, out_shape=jax.ShapeDtypeStruct((NH * 15, GRID_W * GRID_W), F32), name="rpb_expand",
                         compiler_params=pltpu.CompilerParams(vmem_limit_bytes=VMEM_LIMIT))(r2, _diag_onehot())
    return out.reshape(NH, 15, GRID_W, GRID_W)


def _na_bias_tiles(rpb):
    col = np.arange(GRID_W)
    col_start = np.clip(col - 8, 0, GRID_W - 16)
    col_mask = (col[None, :] >= col_start[:, None]) & (col[None, :] < col_start[:, None] + 16)
    rc = jnp.where(col_mask[None, None], _rpb_expand(rpb), NEG)
    neg = jnp.full((NH, GRID_W, GRID_W), NEG, F32)
    tiles = []
    for i0, ws in NA_CLASSES:
        pairs = {(qi, kr): dr for qi, kr, dr in _na_pairs(i0, ws)}
        rows = [jnp.concatenate([rc[:, pairs[(qi, kr)]] if (qi, kr) in pairs else neg for kr in range(NA_WROWS)], axis=2)
                for qi in range(NA_QROWS)]
        tiles.append(jnp.concatenate(rows, axis=1))
    return jnp.stack(tiles)


def _na_cls(b):
    return jnp.where(b == 0, 0, jnp.where(b == NA_BLOCKS - 1, 2, 1))


def _na_start(b):
    return pl.multiple_of(jnp.clip(b * NA_QROWS - 4, 0, NA_ROWS_TOTAL - NA_WROWS) * GRID_W, GRID_W)


HPS = 4
LW = HPS * HD
NLW = DM // LW


NA_BWD_HPS = 2


def _na_in_specs(hps=HPS):
    lw = hps * HD
    nlw = DM // lw
    return [pl.BlockSpec((NA_QB, lw), lambda hp, b: (b, hp)),
            pl.BlockSpec((SEQ, lw), lambda hp, b: (0, nlw + hp)),
            pl.BlockSpec((SEQ, lw), lambda hp, b: (0, 2 * nlw + hp)),
            pl.BlockSpec((None, hps, NA_QB, NA_WIN), lambda hp, b: (_na_cls(b), hp, 0, 0))]


def _na_fwd(qkv, bias, carry):
    def body(q_ref, k_ref, v_ref, b_ref, o_ref):
        start = _na_start(pl.program_id(1))
        q = q_ref[...]
        kw = k_ref[pl.ds(start, NA_WIN), :]
        vw = v_ref[pl.ds(start, NA_WIN), :]
        outs = []
        for hh in range(HPS):
            sl = slice(hh * HD, (hh + 1) * HD)
            s = lax.dot_general(q[:, sl] * QSCALE, kw[:, sl], _NT, preferred_element_type=F32) + b_ref[hh]
            p = jnp.exp(s - jnp.max(s, axis=-1, keepdims=True))
            l = jnp.sum(p, axis=-1, keepdims=True)
            outs.append(jnp.dot(p.astype(BF16), vw[:, sl], preferred_element_type=F32) / l)
        o_ref[...] = jnp.concatenate(outs, axis=1).astype(o_ref.dtype)

    (o,), sent = _carrier_call(
        "na_fwd", body, (NLW, NA_BLOCKS), _na_in_specs(), [pl.BlockSpec((NA_QB, LW), lambda hp, b: (b, hp))],
        [jax.ShapeDtypeStruct((SEQ, DM), BF16)], [], (qkv, qkv, qkv, bias), carry)
    return o, sent


def _na_bwd(qkv, bias, do, carry):
    lw = NA_BWD_HPS * HD

    def body(q_ref, k_ref, v_ref, b_ref, do_ref, dqkv_ref, z_ref, dk_acc, dv_acc):
        blk = pl.program_id(1)

        @pl.when(blk == 0)
        def _():
            dk_acc[...] = jnp.zeros_like(dk_acc)
            dv_acc[...] = jnp.zeros_like(dv_acc)
            z_ref[...] = jnp.zeros_like(z_ref)

        start = _na_start(blk)
        q = q_ref[...]
        do = do_ref[...]
        kw = k_ref[pl.ds(start, NA_WIN), :]
        vw = v_ref[pl.ds(start, NA_WIN), :]
        dqs, dks, dvs = [], [], []
        for hh in range(NA_BWD_HPS):
            sl = slice(hh * HD, (hh + 1) * HD)
            qh = q[:, sl] * QSCALE
            s = lax.dot_general(qh, kw[:, sl], _NT, preferred_element_type=F32) + b_ref[hh]
            p = jnp.exp(s - jnp.max(s, axis=-1, keepdims=True))
            p = p / jnp.sum(p, axis=-1, keepdims=True)
            dp = lax.dot_general(do[:, sl], vw[:, sl], _NT, preferred_element_type=F32)
            ds = p * (dp - jnp.sum(p * dp, axis=-1, keepdims=True))
            dsb = ds.astype(BF16)
            dqs.append(jnp.dot(dsb, kw[:, sl], preferred_element_type=F32) * QSCALE)
            dks.append(lax.dot_general(dsb, qh, _TN, preferred_element_type=F32))
            dvs.append(lax.dot_general(p.astype(BF16), do[:, sl], _TN, preferred_element_type=F32))
            for cls, (i0, ws) in enumerate(NA_CLASSES):
                @pl.when(_na_cls(blk) == cls)
                def _(ds=ds, hh=hh, i0=i0, ws=ws):
                    for qi, kr, dr in _na_pairs(i0, ws):
                        z_ref[hh, dr * GRID_W:(dr + 1) * GRID_W, :] += ds[qi * GRID_W:(qi + 1) * GRID_W, kr * GRID_W:(kr + 1) * GRID_W]
        dqkv_ref[0, pl.ds(pl.multiple_of(blk * NA_QB, NA_QB), NA_QB), :] = jnp.concatenate(dqs, axis=1).astype(dqkv_ref.dtype)
        dk_acc[pl.ds(start, NA_WIN), :] += jnp.concatenate(dks, axis=1)
        dv_acc[pl.ds(start, NA_WIN), :] += jnp.concatenate(dvs, axis=1)

        @pl.when(blk == NA_BLOCKS - 1)
        def _():
            dqkv_ref[1] = dk_acc[...].astype(dqkv_ref.dtype)
            dqkv_ref[2] = dv_acc[...].astype(dqkv_ref.dtype)

    (dqkv, z), sent = _carrier_call(
        "na_bwd", body, (NH // NA_BWD_HPS, NA_BLOCKS),
        _na_in_specs(NA_BWD_HPS) + [pl.BlockSpec((NA_QB, lw), lambda hp, b: (b, hp))],
        [pl.BlockSpec((3, SEQ, lw), lambda hp, b: (0, 0, hp)), pl.BlockSpec((NA_BWD_HPS, 15 * GRID_W, GRID_W), lambda hp, b: (hp, 0, 0))],
        [jax.ShapeDtypeStruct((3, SEQ, DM), BF16), jax.ShapeDtypeStruct((NH, 15 * GRID_W, GRID_W), F32)],
        [pltpu.VMEM((SEQ, lw), F32), pltpu.VMEM((SEQ, lw), F32)], (qkv, qkv, qkv, bias, do), carry)
    return dqkv, z, sent


def _rpb_grad(z):
    z2 = z.reshape(NH * 15, GRID_W * GRID_W)

    def body(z_ref, e_ref, o_ref):
        o_ref[...] = jnp.dot(z_ref[...], e_ref[...], preferred_element_type=F32, precision=lax.Precision.HIGHEST)

    out = pl.pallas_call(body, out_shape=jax.ShapeDtypeStruct((NH * 15, 128), F32), name="rpb_grad",
                         compiler_params=pltpu.CompilerParams(vmem_limit_bytes=VMEM_LIMIT))(z2, _diag_onehot())
    return out[:, :31].reshape(NH, 15, 31)


DIL_BLOCKS = SEQ // DIL_QB


COLS = 128


def _col_spec():
    return pl.BlockSpec((SEQ, COLS), lambda j: (0, j))


def _grp_spec():
    return pl.BlockSpec((3, SEQ, COLS), lambda j: (0, 0, j))


def _store_group_order(dst_ref, src_ref):
    for g, d in enumerate(DIL):
        n = SEQ // d
        for r in range(d):
            dst_ref[g, r * n:(r + 1) * n, :] = src_ref[pl.ds(r, n, stride=d), :].astype(dst_ref.dtype)


def _store_token_order(dst_ref, src_ref, g):
    d = DIL[g]
    n = SEQ // d
    for r in range(d):
        dst_ref[pl.ds(r, n, stride=d), :] = src_ref[g, r * n:(r + 1) * n, :]


def _to_groups(name, a):
    def body(a_ref, o_ref):
        _store_group_order(o_ref, a_ref)

    return pl.pallas_call(body, grid=(DM // COLS,), in_specs=[_col_spec()], out_specs=_grp_spec(),
                          out_shape=jax.ShapeDtypeStruct((3, SEQ, DM), BF16), compiler_params=_params(("parallel",)), name=name)(a)


def _from_groups_sum(name, a):
    def body(a_ref, o_ref, t1, t2):
        _store_token_order(t1, a_ref, 1)
        _store_token_order(t2, a_ref, 2)
        o_ref[...] = (a_ref[0] + t1[...]) + t2[...]

    return pl.pallas_call(body, grid=(DM // COLS,), in_specs=[_grp_spec()], out_specs=_col_spec(),
                          out_shape=jax.ShapeDtypeStruct((SEQ, DM), F32), scratch_shapes=[pltpu.VMEM((SEQ, COLS), F32)] * 2,
                          compiler_params=_params(("parallel",)), name=name)(a)


def _dil_start(b):
    return pl.multiple_of(jnp.clip(b * DIL_QB - DIL_RADIUS, 0, SEQ - DIL_WIN), DIL_RADIUS)


def _dil_mask(g, b, start):
    shift = 11 - 2 * g
    ii = b * DIL_QB + lax.broadcasted_iota(jnp.int32, (DIL_QB, DIL_WIN), 0)
    jj = start + lax.broadcasted_iota(jnp.int32, (DIL_QB, DIL_WIN), 1)
    dist = jnp.abs(ii - jj)
    valid = (dist <= DIL_RADIUS) & (jnp.right_shift(ii, shift) == jnp.right_shift(jj, shift))
    return valid, dist.astype(F32)


def _dil_in_specs():
    return [pl.BlockSpec(memory_space=pltpu.SMEM),
            pl.BlockSpec((None, DIL_QB, LW), lambda g, hp, b: (g, b, hp)),
            pl.BlockSpec((None, SEQ, LW), lambda g, hp, b: (g, 0, NLW + hp)),
            pl.BlockSpec((None, SEQ, LW), lambda g, hp, b: (g, 0, 2 * NLW + hp))]


def _dil_fwd(qkv, slopes, carry):
    def body(sl_ref, q_ref, k_ref, v_ref, o_ref, lse_ref):
        g, hp, b = pl.program_id(0), pl.program_id(1), pl.program_id(2)
        start = _dil_start(b)
        valid, dist = _dil_mask(g, b, start)
        dil = jnp.left_shift(1, 2 * g).astype(F32)
        q = q_ref[...]
        kw = k_ref[pl.ds(start, DIL_WIN), :]
        vw = v_ref[pl.ds(start, DIL_WIN), :]
        outs, lses = [], []
        for hh in range(HPS):
            sl = slice(hh * HD, (hh + 1) * HD)
            s = lax.dot_general(q[:, sl] * QSCALE, kw[:, sl], _NT, preferred_element_type=F32)
            s = jnp.where(valid, s - (sl_ref[hp * HPS + hh] * dil) * dist, NEG)
            m = jnp.max(s, axis=-1, keepdims=True)
            p = jnp.exp(s - m)
            l = jnp.sum(p, axis=-1, keepdims=True)
            outs.append(jnp.dot(p.astype(BF16), vw[:, sl], preferred_element_type=F32) / l)
            lses.append(jnp.broadcast_to(m + jnp.log(l), (DIL_QB, HD)))
        o_ref[...] = jnp.concatenate(outs, axis=1)
        lse_ref[...] = jnp.concatenate(lses, axis=1)

    ospec = pl.BlockSpec((None, DIL_QB, LW), lambda g, hp, b: (g, b, hp))
    sh = jax.ShapeDtypeStruct((3, SEQ, DM), F32)
    (o, lse), sent = _carrier_call("dil_fwd", body, (3, NLW, DIL_BLOCKS), _dil_in_specs(), [ospec, ospec], [sh, sh], [],
                                   (slopes, qkv, qkv, qkv), carry)
    return o, lse, sent


def _dil_merge(o_all, lse_all):
    def body(o_ref, l_ref, out_ref, lse_ref, o1, o2, l1, l2):
        for g, (ot, lt) in ((1, (o1, l1)), (2, (o2, l2))):
            _store_token_order(ot, o_ref, g)
            _store_token_order(lt, l_ref, g)
        la, lb, lc = l_ref[0], l1[...], l2[...]
        m = jnp.maximum(jnp.maximum(la, lb), lc)
        wa, wb, wc = jnp.exp(la - m), jnp.exp(lb - m), jnp.exp(lc - m)
        sw = (wa + wb) + wc
        out_ref[...] = (((wa * o_ref[0] + wb * o1[...]) + wc * o2[...]) / sw).astype(out_ref.dtype)
        lse_ref[...] = m + jnp.log(sw)

    return pl.pallas_call(
        body, grid=(DM // COLS,), in_specs=[_grp_spec(), _grp_spec()], out_specs=[_col_spec(), _col_spec()],
        out_shape=[jax.ShapeDtypeStruct((SEQ, DM), BF16), jax.ShapeDtypeStruct((SEQ, DM), F32)],
        scratch_shapes=[pltpu.VMEM((SEQ, COLS), F32)] * 4, compiler_params=_params(("parallel",)), name="dil_merge")(o_all, lse_all)


def _dil_bwd_prep(do, o, lse):
    def body(do_ref, o_ref, lse_ref, dog_ref, ddg_ref, lseg_ref, dd):
        prod = do_ref[...] * o_ref[...].astype(F32)
        dd[...] = jnp.concatenate(
            [jnp.broadcast_to(jnp.sum(prod[:, h * HD:(h + 1) * HD], axis=-1, keepdims=True), (SEQ, HD)) for h in range(COLS // HD)], axis=1)
        _store_group_order(dog_ref, do_ref)
        _store_group_order(ddg_ref, dd)
        _store_group_order(lseg_ref, lse_ref)

    return pl.pallas_call(
        body, grid=(DM // COLS,), in_specs=[_col_spec()] * 3, out_specs=[_grp_spec()] * 3,
        out_shape=[jax.ShapeDtypeStruct((3, SEQ, DM), BF16), jax.ShapeDtypeStruct((3, SEQ, DM), F32), jax.ShapeDtypeStruct((3, SEQ, DM), F32)],
        scratch_shapes=[pltpu.VMEM((SEQ, COLS), F32)], compiler_params=_params(("parallel",)), name="dil_bwd_prep")(do, o, lse)


def _dil_bwd(qkv, do, dd, lse, slopes, carry):
    def body(sl_ref, q_ref, k_ref, v_ref, do_ref, dd_ref, lse_ref, dqkv_ref, dk_acc, dv_acc):
        g, hp, b = pl.program_id(0), pl.program_id(1), pl.program_id(2)

        @pl.when(b == 0)
        def _():
            dk_acc[...] = jnp.zeros_like(dk_acc)
            dv_acc[...] = jnp.zeros_like(dv_acc)

        start = _dil_start(b)
        valid, dist = _dil_mask(g, b, start)
        dil = jnp.left_shift(1, 2 * g).astype(F32)
        q = q_ref[...]
        do = do_ref[...]
        kw = k_ref[pl.ds(start, DIL_WIN), :]
        vw = v_ref[pl.ds(start, DIL_WIN), :]
        lse = lse_ref[...]
        dd = dd_ref[...]
        dqs, dks, dvs = [], [], []
        for hh in range(HPS):
            sl = slice(hh * HD, (hh + 1) * HD)
            qh = q[:, sl] * QSCALE
            s = lax.dot_general(qh, kw[:, sl], _NT, preferred_element_type=F32)
            s = jnp.where(valid, s - (sl_ref[hp * HPS + hh] * dil) * dist, NEG)
            p = jnp.exp(s - lse[:, hh * HD:hh * HD + 1])
            dp = lax.dot_general(do[:, sl], vw[:, sl], _NT, preferred_element_type=F32)
            dsb = (p * (dp - dd[:, hh * HD:hh * HD + 1])).astype(BF16)
            dqs.append(jnp.dot(dsb, kw[:, sl], preferred_element_type=F32) * QSCALE)
            dks.append(lax.dot_general(dsb, qh, _TN, preferred_element_type=F32))
            dvs.append(lax.dot_general(p.astype(BF16), do[:, sl], _TN, preferred_element_type=F32))
        dqkv_ref[0, pl.ds(pl.multiple_of(b * DIL_QB, DIL_QB), DIL_QB), :] = jnp.concatenate(dqs, axis=1).astype(dqkv_ref.dtype)
        dk_acc[pl.ds(start, DIL_WIN), :] += jnp.concatenate(dks, axis=1)
        dv_acc[pl.ds(start, DIL_WIN), :] += jnp.concatenate(dvs, axis=1)

        @pl.when(b == DIL_BLOCKS - 1)
        def _():
            dqkv_ref[1] = dk_acc[...].astype(dqkv_ref.dtype)
            dqkv_ref[2] = dv_acc[...].astype(dqkv_ref.dtype)

    rspec = pl.BlockSpec((None, DIL_QB, LW), lambda g, hp, b: (g, b, hp))
    (dqkv,), sent = _carrier_call(
        "dil_bwd", body, (3, NLW, DIL_BLOCKS), _dil_in_specs() + [rspec, rspec, rspec],
        [pl.BlockSpec((None, 3, SEQ, LW), lambda g, hp, b: (g, 0, 0, hp))], [jax.ShapeDtypeStruct((3, 3, SEQ, DM), BF16)],
        [pltpu.VMEM((SEQ, LW), F32), pltpu.VMEM((SEQ, LW), F32)], (slopes, qkv, qkv, qkv, do, dd, lse), carry)
    return dqkv, sent


def _ffn_block(layer, x, g_pre, g_post, ex):
    tag = f"l{layer}"
    h = _rms_fwd(f"{tag}_ffn_pre", x, g_pre)
    gate = _ffn_in(f"{tag}_gate", h, ex.weight(("ffn_w_gate", layer)))
    up = _ffn_in(f"{tag}_up", h, ex.weight(("ffn_w_up", layer)))
    act = _swiglu_fwd(f"{tag}_swiglu", gate, up)
    u = _ffn_out(f"{tag}_down", act, ex.weight(("ffn_w_down", layer)))
    return _resid_norm(f"{tag}_ffn_post", x, u, g_post), (x, h, gate, up, act, u)


def _ffn_block_bwd(layer, dx, saved, g_pre, g_post, ex):
    tag = f"l{layer}"
    x, h, gate, up, act, u = saved
    du, dg_post = _norm_bwd(f"{tag}_ffn_post_bwd", [dx], u, g_post)
    d_wd = _ffn_bwd_dw(f"{tag}_dwd", act, du)
    dact = _ffn_bwd_dact(f"{tag}_dact", du, ex.weight(("ffn_w_down", layer)))
    dgate, dup = _swiglu_bwd(f"{tag}_swiglu_bwd", dact, gate, up)
    d_wg = _ffn_bwd_dw(f"{tag}_dwg", dgate, h)
    d_wu = _ffn_bwd_dw(f"{tag}_dwu", dup, h)
    ex.grads(f"{tag}_ffn", {("ffn_w_gate", layer): d_wg, ("ffn_w_up", layer): d_wu, ("ffn_w_down", layer): d_wd})
    dh = _ffn_bwd_dh(f"{tag}_ffn_dh", dgate, ex.weight(("ffn_w_gate", layer)), dup, ex.weight(("ffn_w_up", layer)))
    dx_in, dg_pre = _norm_bwd(f"{tag}_ffn_pre_bwd", [dh], x, g_pre, res=dx)
    return dx_in, dg_pre, dg_post


def _alibi_slopes():
    return 2.0 ** (-8.0 * jnp.arange(1, NH + 1, dtype=F32) / NH)


def _local_step(x, target, norms, rpb, ex):
    g_mix_pre, g_mix_post, g_ffn_pre, g_ffn_post = norms
    row = lambda a, i: a[i:i + 1]

    bias = _na_bias_tiles(rpb)
    h0 = _rms_fwd("l0_mix_pre", x, row(g_mix_pre, 0))
    qkv0 = _qkv_fwd("l0_qkv", h0[None], ex.weight(("na_w_qkv", 0)))
    o0, sent = _na_fwd(qkv0[0], bias, ex.carry("na_fwd"))
    ex.carried("na_fwd", sent)
    na_wo = ex.weight(("na_w_o", 0)).reshape(DM, DM)
    u0 = _proj_fwd("l0_proj", o0, na_wo)
    x1 = _resid_norm("l0_mix_post", x, u0, row(g_mix_post, 0))
    x2, ffn0 = _ffn_block(0, x1, row(g_ffn_pre, 0), row(g_ffn_post, 0), ex)

    slopes = _alibi_slopes()
    h2g = _to_groups("l1_h_groups", _rms_fwd("l1_mix_pre", x2, row(g_mix_pre, 1), F32))
    dil_wqkv = ex.weight(("dil_w_qkv", 0))
    qkv1 = _qkv_fwd("l1_qkv", h2g, dil_wqkv)
    og, lg, sent = _dil_fwd(qkv1, slopes, ex.carry("dil_fwd"))
    ex.carried("dil_fwd", sent)
    o1, lse = _dil_merge(og, lg)
    dil_wo = ex.weight(("dil_w_o", 0)).reshape(DM, DM)
    u1 = _proj_fwd("l1_proj", o1, dil_wo)
    x3 = _resid_norm("l1_mix_post", x2, u1, row(g_mix_post, 1))
    x4, ffn1 = _ffn_block(1, x3, row(g_ffn_pre, 1), row(g_ffn_post, 1), ex)

    dx4, loss_row = _loss_grad("loss", x4, target)

    dx3, dg_fpre1, dg_fpost1 = _ffn_block_bwd(1, dx4, ffn1, row(g_ffn_pre, 1), row(g_ffn_post, 1), ex)
    du1, dg_mpost1 = _norm_bwd("l1_mix_post_bwd", [dx3], u1, row(g_mix_post, 1))
    d_dil_wo = _proj_bwd_dw("l1_dwo", o1, du1)
    do1 = _proj_bwd_do("l1_do", du1, dil_wo, F32)
    dog, ddg, lseg = _dil_bwd_prep(do1, o1, lse)
    dqkv1, sent = _dil_bwd(qkv1, dog, ddg, lseg, slopes, ex.carry("dil_bwd"))
    ex.carried("dil_bwd", sent)
    d_dil_wqkv = _qkv_bwd_dw("l1_dwqkv", h2g, dqkv1, dil_wqkv.shape[2])
    ex.grads("l1_mix", {("dil_w_qkv", 0): d_dil_wqkv, ("dil_w_o", 0): d_dil_wo.reshape(NCHIP, DM // NCHIP, DM)})
    dh2 = _from_groups_sum("l1_dh_tokens", _qkv_bwd_dh("l1_dh", dqkv1, dil_wqkv))
    dx2, dg_mpre1 = _norm_bwd("l1_mix_pre_bwd", [dh2], x2, row(g_mix_pre, 1), res=dx3)

    dx1, dg_fpre0, dg_fpost0 = _ffn_block_bwd(0, dx2, ffn0, row(g_ffn_pre, 0), row(g_ffn_post, 0), ex)
    du0, dg_mpost0 = _norm_bwd("l0_mix_post_bwd", [dx1], u0, row(g_mix_post, 0))
    d_na_wo = _proj_bwd_dw("l0_dwo", o0, du0)
    do0 = _proj_bwd_do("l0_do", du0, na_wo)
    dqkv0, z, sent = _na_bwd(qkv0[0], bias, do0, ex.carry("na_bwd"))
    ex.carried("na_bwd", sent)
    d_rpb = _rpb_grad(z)
    na_wqkv = ex.weight(("na_w_qkv", 0))
    d_na_wqkv = _qkv_bwd_dw("l0_dwqkv", h0[None], dqkv0[None], na_wqkv.shape[2])
    ex.grads("l0_mix", {("na_w_qkv", 0): d_na_wqkv, ("na_w_o", 0): d_na_wo.reshape(NCHIP, DM // NCHIP, DM)})
    dh0 = _qkv_bwd_dh("l0_dh", dqkv0[None], na_wqkv)
    dx0, dg_mpre0 = _norm_bwd("l0_mix_pre_bwd", [dh0[0]], x, row(g_mix_pre, 0), res=dx1)

    dnorms = (jnp.concatenate([dg_mpre0, dg_mpre1]), jnp.concatenate([dg_mpost0, dg_mpost1]),
              jnp.concatenate([dg_fpre0, dg_fpre1]), jnp.concatenate([dg_fpost0, dg_fpost1]))
    return loss_row, dx0, dnorms, d_rpb


def _place():
    x, y, c = lax.axis_index("x"), lax.axis_index("y"), lax.axis_index("c")
    chips = ((1 - x, y), (x, 1 - y), (1 - x, 1 - y))
    return x, y, c, chips


def _chip_id(chip):
    return 2 * chip[0] + chip[1]


def _comm_call(name, body, ins, out_shapes, n_sems, aliases=None):
    return pl.pallas_call(
        body, in_specs=[HBM_SPEC] * len(ins), out_specs=[HBM_SPEC] * len(out_shapes), out_shape=out_shapes,
        scratch_shapes=[pltpu.SemaphoreType.DMA((k,)) for k in n_sems], input_output_aliases=aliases or {},
        compiler_params=pltpu.CompilerParams(has_side_effects=True), name=name)(*ins)


def _gather_copies(shards):
    n = len(shards)

    def copies(src, out, sems):
        send_sems, recv_sems = sems
        x, y, c, chips = _place()

        def copy(t, k, chip, half, to, from_src=False):
            blk = out[t].at[_chip_id(chip), half]
            return pltpu.make_async_remote_copy(
                src_ref=src[t].at[half] if from_src else blk, dst_ref=blk,
                send_sem=send_sems.at[6 * t + k], recv_sem=recv_sems.at[6 * t + k], device_id=to, device_id_type=MESH)

        return copy, x, y, c, chips

    def issue(src, out, sems):
        copy, x, y, c, chips = copies(src, out, sems)
        for t in range(n):
            for j, chip in enumerate(chips):
                copy(t, j, (x, y), c, (*chip, c), from_src=True).start()

    def drain(src, out, sems):
        copy, x, y, c, chips = copies(src, out, sems)
        passed = []
        for t in range(n):
            for j, chip in enumerate(chips):
                copy(t, j, chip, c, (x, y, c)).wait_recv()
                fwd = copy(t, 3 + j, chip, c, (x, y, 1 - c))
                fwd.start()
                passed.append(fwd)
        for t in range(n):
            for j, chip in enumerate(chips):
                copy(t, 3 + j, chip, 1 - c, (x, y, c)).wait_recv()
        for t in range(n):
            for j, chip in enumerate(chips):
                copy(t, j, (x, y), c, (*chip, c), from_src=True).wait_send()
        for cp in passed:
            cp.wait_send()

    return _Carried(shards, [jax.ShapeDtypeStruct((NCHIP,) + s.shape, s.dtype) for s in shards], (6 * n, 6 * n), issue, drain)


def _pair_exchange(name, grads):
    n = len(grads)

    def body(*refs):
        g, theirs = refs[:n], refs[n:2 * n]
        send_sems, recv_sems = refs[2 * n:]
        x, y, c, _ = _place()
        swap = [pltpu.make_async_remote_copy(src_ref=g[t].at[:, 1 - c], dst_ref=theirs[t], send_sem=send_sems.at[t],
                                             recv_sem=recv_sems.at[t], device_id=(x, y, 1 - c), device_id_type=MESH) for t in range(n)]
        for cp in swap:
            cp.start()
        for cp in swap:
            cp.wait()

    return _comm_call(name, body, grads, [jax.ShapeDtypeStruct((NCHIP,) + g.shape[2:], g.dtype) for g in grads], (n, n))


def _chip_exchange_copies(parts):
    n = len(parts)

    def copies(p, slots, sems):
        send_sems, recv_sems = sems
        x, y, c, chips = _place()
        return [pltpu.make_async_remote_copy(src_ref=p[t].at[_chip_id(chips[j])], dst_ref=slots[t].at[j], send_sem=send_sems.at[3 * t + j],
                                             recv_sem=recv_sems.at[3 * t + j], device_id=(*chips[j], c), device_id_type=MESH)
                for t in range(n) for j in range(3)]

    def issue(p, slots, sems):
        for cp in copies(p, slots, sems):
            cp.start()

    def drain(p, slots, sems):
        for cp in copies(p, slots, sems):
            cp.wait()

    return _Carried(parts, [jax.ShapeDtypeStruct((3,) + p.shape[1:], p.dtype) for p in parts], (3 * n, 3 * n), issue, drain)


def _pair_share(full):
    n = len(full)

    def body(*refs):
        buf = refs[n:2 * n]
        send_sems, recv_sems = refs[2 * n:]
        x, y, c, _ = _place()
        sends = [pltpu.make_async_remote_copy(src_ref=buf[t].at[c], dst_ref=buf[t].at[c], send_sem=send_sems.at[t], recv_sem=recv_sems.at[t],
                                              device_id=(x, y, 1 - c), device_id_type=MESH) for t in range(n)]
        for cp in sends:
            cp.start()
        for t in range(n):
            pltpu.make_async_remote_copy(src_ref=buf[t].at[c], dst_ref=buf[t].at[1 - c], send_sem=send_sems.at[t], recv_sem=recv_sems.at[t],
                                         device_id=(x, y, 1 - c), device_id_type=MESH).wait_recv()
        for cp in sends:
            cp.wait_send()

    return _comm_call("grad_pair_share", body, full, [jax.ShapeDtypeStruct(f.shape, f.dtype) for f in full], (n, n),
                      aliases={t: t for t in range(n)})


SMALL_ROWS = 128


def _allreduce_small(v):
    def body(v_ref, o_ref, buf, send_sems, recv_sems):
        x, y, c, _ = _place()
        me = 4 * x + 2 * y + c
        flip = lambda a, f: 1 - a if f else a
        buf[me] = v_ref[...]
        peers = [(flip(x, d >> 2 & 1), flip(y, d >> 1 & 1), flip(c, d & 1)) for d in range(1, 8)]
        sends = [pltpu.make_async_remote_copy(src_ref=v_ref, dst_ref=buf.at[me], send_sem=send_sems.at[i], recv_sem=recv_sems.at[i],
                                              device_id=peer, device_id_type=MESH) for i, peer in enumerate(peers)]
        for cp in sends:
            cp.start()
        for i, (px, py, pc) in enumerate(peers):
            pltpu.make_async_remote_copy(src_ref=v_ref, dst_ref=buf.at[4 * px + 2 * py + pc], send_sem=send_sems.at[i], recv_sem=recv_sems.at[i],
                                         device_id=(px, py, pc), device_id_type=MESH).wait_recv()
        for cp in sends:
            cp.wait_send()
        acc = buf[0]
        for k in range(1, 8):
            acc = acc + buf[k]
        o_ref[...] = acc

    vm = pl.BlockSpec(memory_space=pltpu.VMEM)
    return pl.pallas_call(
        body, in_specs=[vm], out_specs=vm, out_shape=jax.ShapeDtypeStruct((SMALL_ROWS, 128), F32),
        scratch_shapes=[pltpu.VMEM((8, SMALL_ROWS, 128), F32), pltpu.SemaphoreType.DMA((7,)), pltpu.SemaphoreType.DMA((7,))],
        compiler_params=pltpu.CompilerParams(has_side_effects=True), name="allreduce_small")(v)


def _row_block(rows, cols, budget=1 << 20):
    best = 8
    for bm in range(8, rows + 1, 8):
        if rows % bm == 0 and bm * cols * 4 <= budget:
            best = bm
    return best


def _pair_sum(name, place, g, theirs):
    _, m, c = theirs.shape
    bm = _row_block(m, c)

    def body(place_ref, a_ref, b_ref, o_ref):
        o_ref[...] = (a_ref[...].astype(F32) + b_ref[...].astype(F32)).astype(o_ref.dtype)

    spec = pl.BlockSpec((None, bm, c), lambda k, i, pr: (k, i, 0))
    return pl.pallas_call(
        body, out_shape=jax.ShapeDtypeStruct(theirs.shape, BF16),
        grid_spec=pltpu.PrefetchScalarGridSpec(
            num_scalar_prefetch=1, grid=(NCHIP, m // bm),
            in_specs=[pl.BlockSpec((None, None, bm, c), lambda k, i, pr: (k, pr[0], i, 0)), spec], out_specs=spec),
        compiler_params=_params(("parallel", "parallel")), name=name)(place, g, theirs)


def _chip_sum(name, place, parts, slots):
    _, m, c = parts.shape
    bm = _row_block(m, c)

    def body(place_ref, p_ref, s_ref, o_ref):
        s = s_ref[...].astype(F32)
        o_ref[...] = ((p_ref[...].astype(F32) + s[0]) + s[1]) + s[2]

    return pl.pallas_call(
        body, out_shape=jax.ShapeDtypeStruct((2, m, c), F32),
        grid_spec=pltpu.PrefetchScalarGridSpec(
            num_scalar_prefetch=1, grid=(m // bm,),
            in_specs=[pl.BlockSpec((None, bm, c), lambda i, pr: (pr[1], i, 0)), pl.BlockSpec((3, bm, c), lambda i, pr: (0, i, 0))],
            out_specs=pl.BlockSpec((None, bm, c), lambda i, pr: (pr[0], i, 0))),
        compiler_params=_params(("parallel",)), name=name)(place, parts, slots)


def _adamw(name, w, g, m, v, layer=0, into=None):
    lead, rows, cols = w.shape
    bm = _row_block(rows, cols, budget=768 * 1024)
    c1 = 1.0 - ADAM_B1 ** ADAM_STEP
    c2 = 1.0 - ADAM_B2 ** ADAM_STEP

    def body(w_ref, g_ref, m_ref, v_ref, *rest):
        go_ref, d_ref, mo_ref, vo_ref = rest[-4:]
        g = g_ref[...]
        mn = ADAM_B1 * m_ref[...] + (1.0 - ADAM_B1) * g
        vn = ADAM_B2 * v_ref[...] + (1.0 - ADAM_B2) * (g * g)
        go_ref[...] = g
        mo_ref[...] = mn
        vo_ref[...] = vn
        d_ref[...] = -ADAM_LR * ((mn / c1) / (jnp.sqrt(vn / c2) + ADAM_EPS) + ADAM_WD * w_ref[...])

    spec = pl.BlockSpec((None, bm, cols), lambda i: (layer, i, 0))
    sh = jax.ShapeDtypeStruct((lead, rows, cols), F32)
    prev = [] if into is None else list(into)
    return pl.pallas_call(
        body, grid=(rows // bm,), in_specs=[spec, pl.BlockSpec((bm, cols), lambda i: (i, 0)), spec, spec] + [pl.BlockSpec(memory_space=pl.ANY)] * len(prev),
        out_specs=[spec] * 4, out_shape=[sh] * 4, input_output_aliases={4 + k: k for k in range(len(prev))},
        compiler_params=_params(("parallel",)), name=name)(w, g, m, v, *prev)


def _pack_small(norms, rpb):
    flat = jnp.concatenate([a.reshape(-1) for a in norms] + [rpb.reshape(-1)])
    return jnp.pad(flat, (0, SMALL_ROWS * 128 - flat.shape[0])).reshape(SMALL_ROWS, 128)


def _unpack_small(p):
    flat = p.reshape(-1)
    norms = [flat[i * 2 * DM:(i + 1) * 2 * DM].reshape(2, DM) for i in range(4)]
    rpb = flat[8 * DM:8 * DM + NH * 15 * 31].reshape(1, NH, 15, 31)
    return norms, rpb


FFN_NAMES = ("ffn_w_gate", "ffn_w_up", "ffn_w_down")
L0_FFN = tuple((n, 0) for n in FFN_NAMES)
L1_FFN = tuple((n, 1) for n in FFN_NAMES)
NA_KEYS = (("na_w_qkv", 0), ("na_w_o", 0))
DIL_KEYS = (("dil_w_qkv", 0), ("dil_w_o", 0))


class _Exchange:
    GATHERS = {"na_fwd": L0_FFN, "dil_fwd": L1_FFN}
    EXCHANGES = {"dil_bwd": L1_FFN, "na_bwd": DIL_KEYS + L0_FFN}

    def __init__(self, shards):
        self.chip = 2 * lax.axis_index("x") + lax.axis_index("y")
        self.place = jnp.stack([lax.axis_index("c"), self.chip]).astype(jnp.int32)
        self.own = {k: s.reshape(2, s.shape[0] // 2, s.shape[1]).astype(BF16) for k, s in shards.items()}
        self.gathered, self.ready, self.parts, self.full = {}, {}, {}, {}
        first = NA_KEYS + DIL_KEYS
        self._take(first, _run_carried("gather_mixers", _gather_copies([self.own[k] for k in first])))

    def _take(self, keys, landed):
        for k, gw in zip(keys, landed):
            self.gathered[k] = lax.dynamic_update_slice(gw, self.own[k][None], (self.chip, 0, 0, 0))

    def _sum(self, keys, slots):
        for k, s in zip(keys, slots):
            self.full[k] = _chip_sum(f"chip_sum_{k[0]}_{k[1]}", self.place, self.parts[k], s)

    def weight(self, key):
        g = self.gathered[key]
        return g.reshape(NCHIP, 2 * g.shape[2], g.shape[3])

    def carry(self, tag):
        if tag in self.GATHERS:
            return _gather_copies([self.own[k] for k in self.GATHERS[tag]])
        return _chip_exchange_copies([self.parts[k] for k in self.EXCHANGES[tag]])

    def carried(self, tag, landed):
        if tag in self.GATHERS:
            self._take(self.GATHERS[tag], landed)
        else:
            self._sum(self.EXCHANGES[tag], landed)

    def grads(self, tag, dw):
        self.ready.update(dw)
        if tag == "l1_mix":
            return
        keys = tuple(self.ready)
        mine = [self.ready[k].reshape(NCHIP, 2, -1, self.ready[k].shape[-1]) for k in keys]
        theirs = _pair_exchange(f"grad_pair_exchange_{tag}", mine)
        for k, a, b in zip(keys, mine, theirs):
            self.parts[k] = _pair_sum(f"pair_sum_{k[0]}_{k[1]}", self.place, a, b)
        self.ready = {}
        if tag == "l0_mix":
            self._sum(keys, _run_carried("grad_chip_exchange_last", _chip_exchange_copies([self.parts[k] for k in keys])))

    def finish(self):
        keys = tuple(self.full)
        shared = _pair_share([self.full[k] for k in keys])
        return {k: s.reshape(2 * s.shape[1], s.shape[2]) for k, s in zip(keys, shared)}


def kernel(x, norm_mix_pre, norm_mix_post, norm_ffn_pre, norm_ffn_post, na_w_qkv, na_w_o, na_rpb, dil_w_qkv, dil_w_o, ffn_w_gate, ffn_w_up, ffn_w_down, loss_target, m_norm_mix_pre, m_norm_mix_post, m_norm_ffn_pre, m_norm_ffn_post, m_na_w_qkv, m_na_w_o, m_na_rpb, m_dil_w_qkv, m_dil_w_o, m_ffn_w_gate, m_ffn_w_up, m_ffn_w_down, v_norm_mix_pre, v_norm_mix_post, v_norm_ffn_pre, v_norm_ffn_post, v_na_w_qkv, v_na_w_o, v_na_rpb, v_dil_w_qkv, v_dil_w_o, v_ffn_w_gate, v_ffn_w_up, v_ffn_w_down):
    tr = lambda a: jnp.swapaxes(a, 1, 2)
    weights = {"na_w_qkv": na_w_qkv, "na_w_o": na_w_o, "dil_w_qkv": dil_w_qkv, "dil_w_o": dil_w_o,
               "ffn_w_gate": tr(ffn_w_gate), "ffn_w_up": tr(ffn_w_up), "ffn_w_down": ffn_w_down}
    m_in = {"na_w_qkv": m_na_w_qkv, "na_w_o": m_na_w_o, "dil_w_qkv": m_dil_w_qkv, "dil_w_o": m_dil_w_o,
            "ffn_w_gate": tr(m_ffn_w_gate), "ffn_w_up": tr(m_ffn_w_up), "ffn_w_down": m_ffn_w_down}
    v_in = {"na_w_qkv": v_na_w_qkv, "na_w_o": v_na_w_o, "dil_w_qkv": v_dil_w_qkv, "dil_w_o": v_dil_w_o,
            "ffn_w_gate": tr(v_ffn_w_gate), "ffn_w_up": tr(v_ffn_w_up), "ffn_w_down": v_ffn_w_down}

    ex = _Exchange({(n, l): weights[n][l] for n in weights for l in range(weights[n].shape[0])})
    norms = (norm_mix_pre, norm_mix_post, norm_ffn_pre, norm_ffn_post)
    loss_row, dx, dnorms, d_rpb = _local_step(x[0], loss_target[0], norms, na_rpb[0], ex)
    loss = lax.psum(loss_row[0, 0], ("x", "y", "c"))
    full = ex.finish()
    small = _allreduce_small(_pack_small(dnorms, d_rpb))

    out_g, out_d, out_m, out_v = {}, {}, {}, {}
    for n in weights:
        res = None
        for l in range(weights[n].shape[0]):
            res = _adamw(f"adamw_{n}_{l}", weights[n], full[(n, l)], m_in[n], v_in[n], l, res)
        if n in ("ffn_w_gate", "ffn_w_up"):
            res = [tr(r) for r in res]
        out_g[n], out_d[n], out_m[n], out_v[n] = res
    sm_names = ("norm_mix_pre", "norm_mix_post", "norm_ffn_pre", "norm_ffn_post", "na_rpb")
    sm = _adamw("adamw_small", _pack_small(norms, na_rpb)[None], small,
                _pack_small((m_norm_mix_pre, m_norm_mix_post, m_norm_ffn_pre, m_norm_ffn_post), m_na_rpb)[None],
                _pack_small((v_norm_mix_pre, v_norm_mix_post, v_norm_ffn_pre, v_norm_ffn_post), v_na_rpb)[None])
    for res, dst in zip(sm, (out_g, out_d, out_m, out_v)):
        ns, rp = _unpack_small(res)
        for n, a in zip(sm_names, ns + [rp]):
            dst[n] = a

    order = ("norm_mix_pre", "norm_mix_post", "norm_ffn_pre", "norm_ffn_post", "na_w_qkv", "na_w_o", "na_rpb", "dil_w_qkv", "dil_w_o",
             "ffn_w_gate", "ffn_w_up", "ffn_w_down")
    return (loss, dx[None], *[out_g[n] for n in order], *[out_d[n] for n in order], *[out_m[n] for n in order], *[out_v[n] for n in order])
```

```python
import functools

import numpy as np
import jax
import jax.numpy as jnp
from jax import lax
from jax.experimental import pallas as pl
from jax.experimental.pallas import tpu as pltpu

F32 = jnp.float32
BF16 = jnp.bfloat16

SEQ = 2048
DM = 1024
NH = 16
HD = 64
DFF = 2816
NCHIP = 4
FSH = DFF // NCHIP
GRID_W = 64
NA_QROWS = 4
NA_QB = NA_QROWS * GRID_W
NA_WROWS = 12
NA_WIN = NA_WROWS * GRID_W
DIL = (1, 4, 16)
DIL_QB = 256
DIL_WIN = DIL_QB + 128
DIL_RADIUS = 64
RMS_EPS = 1e-6
NEG = -1e30
QSCALE = HD ** -0.5
CH = 256
MESH = pl.DeviceIdType.MESH

ADAM_LR, ADAM_B1, ADAM_B2, ADAM_EPS, ADAM_WD, ADAM_STEP = 0.001, 0.9, 0.999, 1e-08, 0.01, 10

VMEM_LIMIT = 56 * 1024 * 1024

_NN = (((1,), (0,)), ((), ()))
_NT = (((1,), (1,)), ((), ()))
_TN = (((0,), (0,)), ((), ()))


def _params(sem):
    return pltpu.CompilerParams(dimension_semantics=sem, vmem_limit_bytes=VMEM_LIMIT)


def _matmul(name, pairs, grid, out_shape, out_spec, acc_shape):
    nk = grid[-1]
    npair = len(pairs)
    n_in = 2 * npair

    def body(*refs):
        ins, o_ref = refs[:2 * npair], refs[n_in]
        part = None
        for p in range(npair):
            d = lax.dot_general(ins[2 * p][...].astype(BF16), ins[2 * p + 1][...].astype(BF16), pairs[p][4],
                                preferred_element_type=F32)
            part = d if part is None else part + d
        if nk == 1:
            o_ref[...] = part.astype(o_ref.dtype)
        else:
            acc_ref = refs[n_in + 1]
            kk = pl.program_id(len(grid) - 1)

            @pl.when(kk == 0)
            def _():
                acc_ref[...] = part

            @pl.when(kk > 0)
            def _():
                acc_ref[...] += part

            @pl.when(kk == nk - 1)
            def _():
                o_ref[...] = acc_ref[...].astype(o_ref.dtype)

    ops, specs = [], []
    for a, a_spec, b, b_spec, _ in pairs:
        ops += [a, b]
        specs += [a_spec, b_spec]
    return pl.pallas_call(
        body, grid=grid, in_specs=specs, out_specs=out_spec, out_shape=out_shape,
        scratch_shapes=[] if nk == 1 else [pltpu.VMEM(acc_shape, F32)],
        compiler_params=_params(("parallel",) * (len(grid) - 1) + ("arbitrary",)), name=name,
    )(*ops)


def _qkv_fwd(name, h_all, w4):
    g_n = h_all.shape[0]
    per = w4.shape[2] // CH
    return _matmul(
        name, [(h_all, pl.BlockSpec((None, SEQ, DM), lambda g, q, k: (g, 0, 0)),
                w4, pl.BlockSpec((None, DM, CH), lambda g, q, k: ((g * 12 + q) // per, 0, (g * 12 + q) % per)), _NN)],
        (g_n, 12, 1), jax.ShapeDtypeStruct((g_n, SEQ, 3 * DM), BF16),
        pl.BlockSpec((None, SEQ, CH), lambda g, q, k: (g, 0, q)), None)


def _qkv_bwd_dh(name, dqkv, w4):
    g_n = dqkv.shape[0]
    per = w4.shape[2] // CH
    tm = 1024

    def pair(cb):
        chunk = lambda g, t: g * 12 + t * 4 + cb
        return (dqkv, pl.BlockSpec((None, None, tm, CH), lambda g, i, t: (g, t, i, cb)),
                w4, pl.BlockSpec((None, DM, CH), lambda g, i, t: (chunk(g, t) // per, 0, chunk(g, t) % per)), _NT)

    return _matmul(name, [pair(cb) for cb in range(4)], (g_n, SEQ // tm, 3), jax.ShapeDtypeStruct((g_n, SEQ, DM), F32),
                   pl.BlockSpec((None, tm, DM), lambda g, i, t: (g, i, 0)), (tm, DM))


def _qkv_bwd_dw(name, ht_all, dqkv, shard_cols):
    g_n = dqkv.shape[0]
    per = shard_cols // CH
    return _matmul(
        name, [(ht_all, pl.BlockSpec((None, DM, SEQ), lambda qq, k: (qq // 12, 0, 0)),
                dqkv, pl.BlockSpec((None, None, SEQ, CH), lambda qq, k: (qq // 12, (qq % 12) // 4, 0, qq % 4)), _NN)],
        (g_n * 12, 1), jax.ShapeDtypeStruct((NCHIP, DM, shard_cols), BF16),
        pl.BlockSpec((None, DM, CH), lambda qq, k: (qq // per, 0, qq % per)), None)


def _proj_fwd(name, o, wo):
    tm = 512
    return _matmul(
        name, [(o, pl.BlockSpec((tm, DM), lambda i, k: (i, 0)), wo, pl.BlockSpec((DM, DM), lambda i, k: (0, 0)), _NN)],
        (SEQ // tm, 1), jax.ShapeDtypeStruct((SEQ, DM), F32), pl.BlockSpec((tm, DM), lambda i, k: (i, 0)), None)


def _proj_bwd_do(name, du, wo, dtype=BF16):
    tm = 512
    return _matmul(
        name, [(du, pl.BlockSpec((tm, DM), lambda i, k: (i, 0)), wo, pl.BlockSpec((DM, DM), lambda i, k: (0, 0)), _NT)],
        (SEQ // tm, 1), jax.ShapeDtypeStruct((SEQ, DM), dtype), pl.BlockSpec((tm, DM), lambda i, k: (i, 0)), None)


def _proj_bwd_dw(name, o, du):
    tk, tn = 512, 512
    return _matmul(
        name, [(o, pl.BlockSpec((tk, DM), lambda j, k: (k, 0)), du, pl.BlockSpec((tk, tn), lambda j, k: (k, j)), _TN)],
        (DM // tn, SEQ // tk), jax.ShapeDtypeStruct((DM, DM), BF16), pl.BlockSpec((DM, tn), lambda j, k: (0, j)), (DM, tn))


def _ffn_wspec(index_map):
    return pl.BlockSpec((None, FSH, DM), index_map)


def _ffn_bwd_dact(name, du, wd4):
    tm = 1024
    return _matmul(
        name, [(du, pl.BlockSpec((tm, DM), lambda i, s, k: (i, 0)), wd4, _ffn_wspec(lambda i, s, k: (s, 0, 0)), _NT)],
        (SEQ // tm, NCHIP, 1), jax.ShapeDtypeStruct((NCHIP, SEQ, FSH), BF16),
        pl.BlockSpec((None, tm, FSH), lambda i, s, k: (s, i, 0)), None)


def _ffn_bwd_dw(name, a4, b):
    tk = 512
    return _matmul(
        name, [(a4, pl.BlockSpec((None, tk, FSH), lambda s, k: (s, k, 0)), b, pl.BlockSpec((tk, DM), lambda s, k: (k, 0)), _TN)],
        (NCHIP, SEQ // tk), jax.ShapeDtypeStruct((NCHIP, FSH, DM), BF16), _ffn_wspec(lambda s, k: (s, 0, 0)), (FSH, DM))


def _ffn_bwd_dh(name, dgate, wgt4, dup, wut4):
    tm = 1024
    a_spec = pl.BlockSpec((None, tm, FSH), lambda i, s: (s, i, 0))
    b_spec = _ffn_wspec(lambda i, s: (s, 0, 0))
    return _matmul(
        name, [(dgate, a_spec, wgt4, b_spec, _NN), (dup, a_spec, wut4, b_spec, _NN)],
        (SEQ // tm, NCHIP), jax.ShapeDtypeStruct((SEQ, DM), F32), pl.BlockSpec((tm, DM), lambda i, s: (i, 0)), (tm, DM))


ROWS = 256


def _row_spec():
    return pl.BlockSpec((ROWS, DM), lambda i: (i, 0))


def _vec_spec():
    return pl.BlockSpec((1, DM), lambda i: (0, 0))


def _rms_fwd(name, x, g, dtype=BF16):
    def body(x_ref, g_ref, o_ref):
        x = x_ref[...]
        r = lax.rsqrt(jnp.mean(x * x, axis=-1, keepdims=True) + RMS_EPS)
        o_ref[...] = (x * r * g_ref[...]).astype(o_ref.dtype)

    return pl.pallas_call(body, grid=(SEQ // ROWS,), in_specs=[_row_spec(), _vec_spec()], out_specs=_row_spec(),
                          out_shape=jax.ShapeDtypeStruct((SEQ, DM), dtype), compiler_params=_params(("parallel",)), name=name)(x, g)


def _rms_fwd_both(name, x, g):
    def body(x_ref, g_ref, o_ref, t_ref):
        x = x_ref[...]
        r = lax.rsqrt(jnp.mean(x * x, axis=-1, keepdims=True) + RMS_EPS)
        h = x * r * g_ref[...]
        o_ref[...] = h.astype(o_ref.dtype)
        t_ref[...] = h.T.astype(t_ref.dtype)

    return pl.pallas_call(
        body, grid=(SEQ // ROWS,), in_specs=[_row_spec(), _vec_spec()], out_specs=[_row_spec(), pl.BlockSpec((DM, ROWS), lambda i: (0, i))],
        out_shape=[jax.ShapeDtypeStruct((SEQ, DM), BF16), jax.ShapeDtypeStruct((DM, SEQ), BF16)],
        compiler_params=_params(("parallel",)), name=name)(x, g)


def _resid_norm(name, x, u, g):
    def body(x_ref, u_ref, g_ref, o_ref):
        u = u_ref[...]
        r = lax.rsqrt(jnp.mean(u * u, axis=-1, keepdims=True) + RMS_EPS)
        o_ref[...] = x_ref[...] + u * r * g_ref[...]

    return pl.pallas_call(body, grid=(SEQ // ROWS,), in_specs=[_row_spec(), _row_spec(), _vec_spec()], out_specs=_row_spec(),
                          out_shape=jax.ShapeDtypeStruct((SEQ, DM), F32), compiler_params=_params(("parallel",)), name=name)(x, u, g)


def _norm_bwd(name, dys, u, g, res=None):
    ndy = len(dys)

    def body(*refs):
        dy = refs[0][...]
        for r_ in refs[1:ndy]:
            dy = dy + r_[...]
        u_ref, g_ref = refs[ndy], refs[ndy + 1]
        res_ref = refs[ndy + 2] if res is not None else None
        du_ref, dg_ref = refs[-2], refs[-1]
        u = u_ref[...]
        r = lax.rsqrt(jnp.mean(u * u, axis=-1, keepdims=True) + RMS_EPS)
        yh = u * r
        t = dy * g_ref[...]
        du = r * (t - yh * jnp.mean(t * yh, axis=-1, keepdims=True))
        if res_ref is not None:
            du = du + res_ref[...]
        du_ref[...] = du

        @pl.when(pl.program_id(0) == 0)
        def _():
            dg_ref[...] = jnp.zeros_like(dg_ref)

        dg_ref[...] += jnp.sum(dy * yh, axis=0, keepdims=True)

    ops = list(dys) + [u, g] + ([res] if res is not None else [])
    specs = [_row_spec()] * ndy + [_row_spec(), _vec_spec()] + ([_row_spec()] if res is not None else [])
    return pl.pallas_call(
        body, grid=(SEQ // ROWS,), in_specs=specs, out_specs=[_row_spec(), _vec_spec()],
        out_shape=[jax.ShapeDtypeStruct((SEQ, DM), F32), jax.ShapeDtypeStruct((1, DM), F32)],
        compiler_params=_params(("arbitrary",)), name=name)(*ops)


def _loss_grad(name, y, t):
    def body(y_ref, t_ref, dy_ref, l_ref):
        e = y_ref[...] - t_ref[...]
        dy_ref[...] = e * (1.0 / DM)

        @pl.when(pl.program_id(0) == 0)
        def _():
            l_ref[...] = jnp.zeros_like(l_ref)

        l_ref[...] += jnp.sum(e * e) * (0.5 / DM)

    return pl.pallas_call(
        body, grid=(SEQ // ROWS,), in_specs=[_row_spec(), _row_spec()],
        out_specs=[_row_spec(), pl.BlockSpec((1, 128), lambda i: (0, 0))],
        out_shape=[jax.ShapeDtypeStruct((SEQ, DM), F32), jax.ShapeDtypeStruct((1, 128), F32)],
        compiler_params=_params(("arbitrary",)), name=name)(y, t)


def _ffn_spec():
    return pl.BlockSpec((None, 512, FSH), lambda s, i: (s, i, 0))


def _swiglu_bwd(name, dact, gate, up):
    def body(d_ref, g_ref, u_ref, dg_ref, du_ref, a_ref):
        d = d_ref[...].astype(F32)
        g = g_ref[...].astype(F32)
        u = u_ref[...].astype(F32)
        sg = jax.nn.sigmoid(g)
        dg_ref[...] = (d * u * sg * (1.0 + g * (1.0 - sg))).astype(dg_ref.dtype)
        du_ref[...] = (d * g * sg).astype(du_ref.dtype)
        a_ref[...] = (g * sg * u).astype(a_ref.dtype)

    sh = jax.ShapeDtypeStruct((NCHIP, SEQ, FSH), BF16)
    return pl.pallas_call(body, grid=(NCHIP, SEQ // 512), in_specs=[_ffn_spec()] * 3, out_specs=[_ffn_spec()] * 3,
                          out_shape=[sh, sh, sh], compiler_params=_params(("parallel", "parallel")), name=name)(dact, gate, up)


HBM_SPEC = pl.BlockSpec(memory_space=pltpu.HBM)


class _Carried:
    def __init__(self, ins, out_shapes, n_sems, issue, drain):
        self.ins, self.out_shapes, self.n_sems, self.issue, self.drain = list(ins), list(out_shapes), tuple(n_sems), issue, drain


def _carrier_call(name, body, grid, in_specs, out_specs, out_shape, scratch_shapes, operands, carry):
    n_in, n_out, n_scr = len(in_specs), len(out_specs), len(scratch_shapes)
    if carry is None:
        res = pl.pallas_call(body, grid=grid, in_specs=in_specs, out_specs=out_specs, out_shape=out_shape, scratch_shapes=scratch_shapes,
                             compiler_params=_params(("arbitrary",) * len(grid)), name=name)(*operands)
        return list(res), []
    ci, co = len(carry.ins), len(carry.out_shapes)

    def wrapped(*refs):
        ins, cins = refs[:n_in], refs[n_in:n_in + ci]
        outs, couts = refs[n_in + ci:n_in + ci + n_out], refs[n_in + ci + n_out:n_in + ci + n_out + co]
        scr, sems = refs[n_in + ci + n_out + co:n_in + ci + n_out + co + n_scr], refs[n_in + ci + n_out + co + n_scr:]
        first = functools.reduce(jnp.logical_and, [pl.program_id(a) == 0 for a in range(len(grid))])
        last = functools.reduce(jnp.logical_and, [pl.program_id(a) == grid[a] - 1 for a in range(len(grid))])

        @pl.when(first)
        def _():
            carry.issue(cins, couts, sems)

        body(*ins, *outs, *scr)

        @pl.when(last)
        def _():
            carry.drain(cins, couts, sems)

    res = pl.pallas_call(
        wrapped, grid=grid, in_specs=list(in_specs) + [HBM_SPEC] * ci, out_specs=list(out_specs) + [HBM_SPEC] * co,
        out_shape=list(out_shape) + carry.out_shapes,
        scratch_shapes=list(scratch_shapes) + [pltpu.SemaphoreType.DMA((k,)) for k in carry.n_sems],
        compiler_params=pltpu.CompilerParams(dimension_semantics=("arbitrary",) * len(grid), vmem_limit_bytes=VMEM_LIMIT, has_side_effects=True),
        name=name)(*operands, *carry.ins)
    return list(res[:n_out]), list(res[n_out:])


def _run_carried(name, carry):
    def body(*refs):
        ci, co = len(carry.ins), len(carry.out_shapes)
        carry.issue(refs[:ci], refs[ci:ci + co], refs[ci + co:])
        carry.drain(refs[:ci], refs[ci:ci + co], refs[ci + co:])

    return pl.pallas_call(
        body, in_specs=[HBM_SPEC] * len(carry.ins), out_specs=[HBM_SPEC] * len(carry.out_shapes), out_shape=carry.out_shapes,
        scratch_shapes=[pltpu.SemaphoreType.DMA((k,)) for k in carry.n_sems],
        compiler_params=pltpu.CompilerParams(has_side_effects=True), name=name)(*carry.ins)


NA_BLOCKS = SEQ // NA_QB
NA_ROWS_TOTAL = SEQ // GRID_W
NA_CLASSES = ((0, 0), (8, 4), (NA_ROWS_TOTAL - NA_QROWS, NA_ROWS_TOTAL - NA_WROWS))


def _na_pairs(i0, ws):
    out = []
    for qi in range(NA_QROWS):
        i = i0 + qi
        rs = min(max(i - 4, 0), NA_ROWS_TOTAL - 8)
        for kr in range(NA_WROWS):
            r = ws + kr
            if rs <= r < rs + 8:
                out.append((qi, kr, r - i + 7))
    return out


def _diag_onehot():
    qc, kc = np.meshgrid(np.arange(GRID_W), np.arange(GRID_W), indexing="ij")
    e = np.zeros((GRID_W * GRID_W, 128), np.float32)
    j = (kc - qc + 15).reshape(-1)
    ok = (j >= 0) & (j <= 30)
    e[np.arange(GRID_W * GRID_W)[ok], j[ok]] = 1.0
    return jnp.asarray(e)


def _rpb_expand(rpb):
    r2 = jnp.pad(rpb.reshape(NH * 15, 31), ((0, 0), (0, 128 - 31)))

    def body(r_ref, e_ref, o_ref):
        o_ref[...] = lax.dot_general(r_ref[...], e_ref[...], _NT, preferred_element_type=F32, precision=lax.Precision.HIGHEST)

    out = pl.pallas_call(body, out_shape=jax.ShapeDtypeStruct((NH * 15, GRID_W * GRID_W), F32), name="rpb_expand",
                         compiler_params=pltpu.CompilerParams(vmem_limit_bytes=VMEM_LIMIT))(r2, _diag_onehot())
    return out.reshape(NH, 15, GRID_W, GRID_W)


def _na_bias_tiles(rpb):
    col = np.arange(GRID_W)
    col_start = np.clip(col - 8, 0, GRID_W - 16)
    col_mask = (col[None, :] >= col_start[:, None]) & (col[None, :] < col_start[:, None] + 16)
    rc = jnp.where(col_mask[None, None], _rpb_expand(rpb), NEG)
    neg = jnp.full((NH, GRID_W, GRID_W), NEG, F32)
    tiles = []
    for i0, ws in NA_CLASSES:
        pairs = {(qi, kr): dr for qi, kr, dr in _na_pairs(i0, ws)}
        rows = [jnp.concatenate([rc[:, pairs[(qi, kr)]] if (qi, kr) in pairs else neg for kr in range(NA_WROWS)], axis=2)
                for qi in range(NA_QROWS)]
        tiles.append(jnp.concatenate(rows, axis=1))
    return jnp.stack(tiles)


def _na_cls(b):
    return jnp.where(b == 0, 0, jnp.where(b == NA_BLOCKS - 1, 2, 1))


def _na_start(b):
    return pl.multiple_of(jnp.clip(b * NA_QROWS - 4, 0, NA_ROWS_TOTAL - NA_WROWS) * GRID_W, GRID_W)


HPS = 4
LW = HPS * HD
NLW = DM // LW


NA_BWD_HPS = 2


def _na_in_specs(hps=HPS):
    lw = hps * HD
    nlw = DM // lw
    return [pl.BlockSpec((NA_QB, lw), lambda hp, b: (b, hp)),
            pl.BlockSpec((SEQ, lw), lambda hp, b: (0, nlw + hp)),
            pl.BlockSpec((SEQ, lw), lambda hp, b: (0, 2 * nlw + hp)),
            pl.BlockSpec((None, hps, NA_QB, NA_WIN), lambda hp, b: (_na_cls(b), hp, 0, 0))]


def _na_fwd(qkv, bias, carry):
    def body(q_ref, k_ref, v_ref, b_ref, o_ref):
        start = _na_start(pl.program_id(1))
        q = q_ref[...]
        kw = k_ref[pl.ds(start, NA_WIN), :]
        vw = v_ref[pl.ds(start, NA_WIN), :]
        outs = []
        for hh in range(HPS):
            sl = slice(hh * HD, (hh + 1) * HD)
            s = lax.dot_general(q[:, sl] * QSCALE, kw[:, sl], _NT, preferred_element_type=F32) + b_ref[hh]
            p = jnp.exp(s - jnp.max(s, axis=-1, keepdims=True))
            l = jnp.sum(p, axis=-1, keepdims=True)
            outs.append(jnp.dot(p.astype(BF16), vw[:, sl], preferred_element_type=F32) / l)
        o_ref[...] = jnp.concatenate(outs, axis=1).astype(o_ref.dtype)

    (o,), sent = _carrier_call(
        "na_fwd", body, (NLW, NA_BLOCKS), _na_in_specs(), [pl.BlockSpec((NA_QB, LW), lambda hp, b: (b, hp))],
        [jax.ShapeDtypeStruct((SEQ, DM), BF16)], [], (qkv, qkv, qkv, bias), carry)
    return o, sent


def _na_bwd(qkv, bias, do, carry):
    lw = NA_BWD_HPS * HD

    def body(q_ref, k_ref, v_ref, b_ref, do_ref, dqkv_ref, z_ref, dk_acc, dv_acc):
        blk = pl.program_id(1)

        @pl.when(blk == 0)
        def _():
            dk_acc[...] = jnp.zeros_like(dk_acc)
            dv_acc[...] = jnp.zeros_like(dv_acc)
            z_ref[...] = jnp.zeros_like(z_ref)

        start = _na_start(blk)
        q = q_ref[...]
        do = do_ref[...]
        kw = k_ref[pl.ds(start, NA_WIN), :]
        vw = v_ref[pl.ds(start, NA_WIN), :]
        dqs, dks, dvs = [], [], []
        for hh in range(NA_BWD_HPS):
            sl = slice(hh * HD, (hh + 1) * HD)
            qh = q[:, sl] * QSCALE
            s = lax.dot_general(qh, kw[:, sl], _NT, preferred_element_type=F32) + b_ref[hh]
            p = jnp.exp(s - jnp.max(s, axis=-1, keepdims=True))
            p = p / jnp.sum(p, axis=-1, keepdims=True)
            dp = lax.dot_general(do[:, sl], vw[:, sl], _NT, preferred_element_type=F32)
            ds = p * (dp - jnp.sum(p * dp, axis=-1, keepdims=True))
            dsb = ds.astype(BF16)
            dqs.append(jnp.dot(dsb, kw[:, sl], preferred_element_type=F32) * QSCALE)
            dks.append(lax.dot_general(dsb, qh, _TN, preferred_element_type=F32))
            dvs.append(lax.dot_general(p.astype(BF16), do[:, sl], _TN, preferred_element_type=F32))
            for cls, (i0, ws) in enumerate(NA_CLASSES):
                @pl.when(_na_cls(blk) == cls)
                def _(ds=ds, hh=hh, i0=i0, ws=ws):
                    for qi, kr, dr in _na_pairs(i0, ws):
                        z_ref[hh, dr * GRID_W:(dr + 1) * GRID_W, :] += ds[qi * GRID_W:(qi + 1) * GRID_W, kr * GRID_W:(kr + 1) * GRID_W]
        dqkv_ref[0, pl.ds(pl.multiple_of(blk * NA_QB, NA_QB), NA_QB), :] = jnp.concatenate(dqs, axis=1).astype(dqkv_ref.dtype)
        dk_acc[pl.ds(start, NA_WIN), :] += jnp.concatenate(dks, axis=1)
        dv_acc[pl.ds(start, NA_WIN), :] += jnp.concatenate(dvs, axis=1)

        @pl.when(blk == NA_BLOCKS - 1)
        def _():
            dqkv_ref[1] = dk_acc[...].astype(dqkv_ref.dtype)
            dqkv_ref[2] = dv_acc[...].astype(dqkv_ref.dtype)

    (dqkv, z), sent = _carrier_call(
        "na_bwd", body, (NH // NA_BWD_HPS, NA_BLOCKS),
        _na_in_specs(NA_BWD_HPS) + [pl.BlockSpec((NA_QB, lw), lambda hp, b: (b, hp))],
        [pl.BlockSpec((3, SEQ, lw), lambda hp, b: (0, 0, hp)), pl.BlockSpec((NA_BWD_HPS, 15 * GRID_W, GRID_W), lambda hp, b: (hp, 0, 0))],
        [jax.ShapeDtypeStruct((3, SEQ, DM), BF16), jax.ShapeDtypeStruct((NH, 15 * GRID_W, GRID_W), F32)],
        [pltpu.VMEM((SEQ, lw), F32), pltpu.VMEM((SEQ, lw), F32)], (qkv, qkv, qkv, bias, do), carry)
    return dqkv, z, sent


def _rpb_grad(z):
    z2 = z.reshape(NH * 15, GRID_W * GRID_W)

    def body(z_ref, e_ref, o_ref):
        o_ref[...] = jnp.dot(z_ref[...], e_ref[...], preferred_element_type=F32, precision=lax.Precision.HIGHEST)

    out = pl.pallas_call(body, out_shape=jax.ShapeDtypeStruct((NH * 15, 128), F32), name="rpb_grad",
                         compiler_params=pltpu.CompilerParams(vmem_limit_bytes=VMEM_LIMIT))(z2, _diag_onehot())
    return out[:, :31].reshape(NH, 15, 31)


DIL_BLOCKS = SEQ // DIL_QB


COLS = 128


def _col_spec():
    return pl.BlockSpec((SEQ, COLS), lambda j: (0, j))


def _grp_spec():
    return pl.BlockSpec((3, SEQ, COLS), lambda j: (0, 0, j))


def _store_group_order(dst_ref, src_ref):
    for g, d in enumerate(DIL):
        n = SEQ // d
        for r in range(d):
            dst_ref[g, r * n:(r + 1) * n, :] = src_ref[pl.ds(r, n, stride=d), :].astype(dst_ref.dtype)


def _store_token_order(dst_ref, src_ref, g):
    d = DIL[g]
    n = SEQ // d
    for r in range(d):
        dst_ref[pl.ds(r, n, stride=d), :] = src_ref[g, r * n:(r + 1) * n, :]


def _to_groups(name, a):
    def body(a_ref, o_ref, t_ref):
        _store_group_order(o_ref, a_ref)
        for g in range(3):
            t_ref[g] = o_ref[g].astype(F32).T.astype(t_ref.dtype)

    return pl.pallas_call(
        body, grid=(DM // COLS,), in_specs=[_col_spec()], out_specs=[_grp_spec(), pl.BlockSpec((3, COLS, SEQ), lambda j: (0, j, 0))],
        out_shape=[jax.ShapeDtypeStruct((3, SEQ, DM), BF16), jax.ShapeDtypeStruct((3, DM, SEQ), BF16)],
        compiler_params=_params(("parallel",)), name=name)(a)


def _from_groups_sum(name, a):
    def body(a_ref, o_ref, t1, t2):
        _store_token_order(t1, a_ref, 1)
        _store_token_order(t2, a_ref, 2)
        o_ref[...] = (a_ref[0] + t1[...]) + t2[...]

    return pl.pallas_call(body, grid=(DM // COLS,), in_specs=[_grp_spec()], out_specs=_col_spec(),
                          out_shape=jax.ShapeDtypeStruct((SEQ, DM), F32), scratch_shapes=[pltpu.VMEM((SEQ, COLS), F32)] * 2,
                          compiler_params=_params(("parallel",)), name=name)(a)


def _dil_start(b):
    return pl.multiple_of(jnp.clip(b * DIL_QB - DIL_RADIUS, 0, SEQ - DIL_WIN), DIL_RADIUS)


def _dil_mask(g, b, start):
    shift = 11 - 2 * g
    ii = b * DIL_QB + lax.broadcasted_iota(jnp.int32, (DIL_QB, DIL_WIN), 0)
    jj = start + lax.broadcasted_iota(jnp.int32, (DIL_QB, DIL_WIN), 1)
    dist = jnp.abs(ii - jj)
    valid = (dist <= DIL_RADIUS) & (jnp.right_shift(ii, shift) == jnp.right_shift(jj, shift))
    return valid, dist.astype(F32)


def _dil_in_specs():
    return [pl.BlockSpec(memory_space=pltpu.SMEM),
            pl.BlockSpec((None, DIL_QB, LW), lambda g, hp, b: (g, b, hp)),
            pl.BlockSpec((None, SEQ, LW), lambda g, hp, b: (g, 0, NLW + hp)),
            pl.BlockSpec((None, SEQ, LW), lambda g, hp, b: (g, 0, 2 * NLW + hp))]


def _dil_fwd(qkv, slopes, carry):
    def body(sl_ref, q_ref, k_ref, v_ref, o_ref, lse_ref):
        g, hp, b = pl.program_id(0), pl.program_id(1), pl.program_id(2)
        start = _dil_start(b)
        valid, dist = _dil_mask(g, b, start)
        dil = jnp.left_shift(1, 2 * g).astype(F32)
        q = q_ref[...]
        kw = k_ref[pl.ds(start, DIL_WIN), :]
        vw = v_ref[pl.ds(start, DIL_WIN), :]
        outs, lses = [], []
        for hh in range(HPS):
            sl = slice(hh * HD, (hh + 1) * HD)
            s = lax.dot_general(q[:, sl] * QSCALE, kw[:, sl], _NT, preferred_element_type=F32)
            s = jnp.where(valid, s - (sl_ref[hp * HPS + hh] * dil) * dist, NEG)
            m = jnp.max(s, axis=-1, keepdims=True)
            p = jnp.exp(s - m)
            l = jnp.sum(p, axis=-1, keepdims=True)
            outs.append(jnp.dot(p.astype(BF16), vw[:, sl], preferred_element_type=F32) / l)
            lses.append(jnp.broadcast_to(m + jnp.log(l), (DIL_QB, HD)))
        o_ref[...] = jnp.concatenate(outs, axis=1)
        lse_ref[...] = jnp.concatenate(lses, axis=1)

    ospec = pl.BlockSpec((None, DIL_QB, LW), lambda g, hp, b: (g, b, hp))
    sh = jax.ShapeDtypeStruct((3, SEQ, DM), F32)
    (o, lse), sent = _carrier_call("dil_fwd", body, (3, NLW, DIL_BLOCKS), _dil_in_specs(), [ospec, ospec], [sh, sh], [],
                                   (slopes, qkv, qkv, qkv), carry)
    return o, lse, sent


def _dil_merge(o_all, lse_all):
    def body(o_ref, l_ref, out_ref, lse_ref, o1, o2, l1, l2):
        for g, (ot, lt) in ((1, (o1, l1)), (2, (o2, l2))):
            _store_token_order(ot, o_ref, g)
            _store_token_order(lt, l_ref, g)
        la, lb, lc = l_ref[0], l1[...], l2[...]
        m = jnp.maximum(jnp.maximum(la, lb), lc)
        wa, wb, wc = jnp.exp(la - m), jnp.exp(lb - m), jnp.exp(lc - m)
        sw = (wa + wb) + wc
        out_ref[...] = (((wa * o_ref[0] + wb * o1[...]) + wc * o2[...]) / sw).astype(out_ref.dtype)
        lse_ref[...] = m + jnp.log(sw)

    return pl.pallas_call(
        body, grid=(DM // COLS,), in_specs=[_grp_spec(), _grp_spec()], out_specs=[_col_spec(), _col_spec()],
        out_shape=[jax.ShapeDtypeStruct((SEQ, DM), BF16), jax.ShapeDtypeStruct((SEQ, DM), F32)],
        scratch_shapes=[pltpu.VMEM((SEQ, COLS), F32)] * 4, compiler_params=_params(("parallel",)), name="dil_merge")(o_all, lse_all)


def _dil_bwd_prep(do, o, lse):
    def body(do_ref, o_ref, lse_ref, dog_ref, ddg_ref, lseg_ref, dd):
        prod = do_ref[...] * o_ref[...].astype(F32)
        dd[...] = jnp.concatenate(
            [jnp.broadcast_to(jnp.sum(prod[:, h * HD:(h + 1) * HD], axis=-1, keepdims=True), (SEQ, HD)) for h in range(COLS // HD)], axis=1)
        _store_group_order(dog_ref, do_ref)
        _store_group_order(ddg_ref, dd)
        _store_group_order(lseg_ref, lse_ref)

    return pl.pallas_call(
        body, grid=(DM // COLS,), in_specs=[_col_spec()] * 3, out_specs=[_grp_spec()] * 3,
        out_shape=[jax.ShapeDtypeStruct((3, SEQ, DM), BF16), jax.ShapeDtypeStruct((3, SEQ, DM), F32), jax.ShapeDtypeStruct((3, SEQ, DM), F32)],
        scratch_shapes=[pltpu.VMEM((SEQ, COLS), F32)], compiler_params=_params(("parallel",)), name="dil_bwd_prep")(do, o, lse)


def _dil_bwd(qkv, do, dd, lse, slopes, carry):
    def body(sl_ref, q_ref, k_ref, v_ref, do_ref, dd_ref, lse_ref, dqkv_ref, dk_acc, dv_acc):
        g, hp, b = pl.program_id(0), pl.program_id(1), pl.program_id(2)

        @pl.when(b == 0)
        def _():
            dk_acc[...] = jnp.zeros_like(dk_acc)
            dv_acc[...] = jnp.zeros_like(dv_acc)

        start = _dil_start(b)
        valid, dist = _dil_mask(g, b, start)
        dil = jnp.left_shift(1, 2 * g).astype(F32)
        q = q_ref[...]
        do = do_ref[...]
        kw = k_ref[pl.ds(start, DIL_WIN), :]
        vw = v_ref[pl.ds(start, DIL_WIN), :]
        lse = lse_ref[...]
        dd = dd_ref[...]
        dqs, dks, dvs = [], [], []
        for hh in range(HPS):
            sl = slice(hh * HD, (hh + 1) * HD)
            qh = q[:, sl] * QSCALE
            s = lax.dot_general(qh, kw[:, sl], _NT, preferred_element_type=F32)
            s = jnp.where(valid, s - (sl_ref[hp * HPS + hh] * dil) * dist, NEG)
            p = jnp.exp(s - lse[:, hh * HD:hh * HD + 1])
            dp = lax.dot_general(do[:, sl], vw[:, sl], _NT, preferred_element_type=F32)
            dsb = (p * (dp - dd[:, hh * HD:hh * HD + 1])).astype(BF16)
            dqs.append(jnp.dot(dsb, kw[:, sl], preferred_element_type=F32) * QSCALE)
            dks.append(lax.dot_general(dsb, qh, _TN, preferred_element_type=F32))
            dvs.append(lax.dot_general(p.astype(BF16), do[:, sl], _TN, preferred_element_type=F32))
        dqkv_ref[0, pl.ds(pl.multiple_of(b * DIL_QB, DIL_QB), DIL_QB), :] = jnp.concatenate(dqs, axis=1).astype(dqkv_ref.dtype)
        dk_acc[pl.ds(start, DIL_WIN), :] += jnp.concatenate(dks, axis=1)
        dv_acc[pl.ds(start, DIL_WIN), :] += jnp.concatenate(dvs, axis=1)

        @pl.when(b == DIL_BLOCKS - 1)
        def _():
            dqkv_ref[1] = dk_acc[...].astype(dqkv_ref.dtype)
            dqkv_ref[2] = dv_acc[...].astype(dqkv_ref.dtype)

    rspec = pl.BlockSpec((None, DIL_QB, LW), lambda g, hp, b: (g, b, hp))
    (dqkv,), sent = _carrier_call(
        "dil_bwd", body, (3, NLW, DIL_BLOCKS), _dil_in_specs() + [rspec, rspec, rspec],
        [pl.BlockSpec((None, 3, SEQ, LW), lambda g, hp, b: (g, 0, 0, hp))], [jax.ShapeDtypeStruct((3, 3, SEQ, DM), BF16)],
        [pltpu.VMEM((SEQ, LW), F32), pltpu.VMEM((SEQ, LW), F32)], (slopes, qkv, qkv, qkv, do, dd, lse), carry)
    return dqkv, sent


def _ffn_fwd(name, x, g_pre, g_post, wgt4, wut4, wd4, carry):
    tm = 512

    def body(x_ref, gpre_ref, gpost_ref, wg_ref, wu_ref, wd_ref, xn_ref, h_ref, gate_ref, up_ref, u_ref, acc):
        s = pl.program_id(1)

        @pl.when(s == 0)
        def _():
            x = x_ref[...]
            r = lax.rsqrt(jnp.mean(x * x, axis=-1, keepdims=True) + RMS_EPS)
            h_ref[...] = (x * r * gpre_ref[...]).astype(h_ref.dtype)

        h = h_ref[...]
        gate = lax.dot_general(h, wg_ref[...], _NT, preferred_element_type=F32).astype(BF16)
        up = lax.dot_general(h, wu_ref[...], _NT, preferred_element_type=F32).astype(BF16)
        gate_ref[...] = gate
        up_ref[...] = up
        gf = gate.astype(F32)
        act = (gf * jax.nn.sigmoid(gf) * up.astype(F32)).astype(BF16)
        part = jnp.dot(act, wd_ref[...], preferred_element_type=F32)

        @pl.when(s == 0)
        def _():
            acc[...] = part

        @pl.when(s > 0)
        def _():
            acc[...] += part

        @pl.when(s == NCHIP - 1)
        def _():
            u = acc[...]
            u_ref[...] = u
            r = lax.rsqrt(jnp.mean(u * u, axis=-1, keepdims=True) + RMS_EPS)
            xn_ref[...] = x_ref[...] + u * r * gpost_ref[...]

    rows = pl.BlockSpec((tm, DM), lambda i, s: (i, 0))
    vec = pl.BlockSpec((1, DM), lambda i, s: (0, 0))
    wspec = _ffn_wspec(lambda i, s: (s, 0, 0))
    mid = pl.BlockSpec((None, tm, FSH), lambda i, s: (s, i, 0))
    outs, sent = _carrier_call(
        name, body, (SEQ // tm, NCHIP), [rows, vec, vec, wspec, wspec, wspec], [rows, rows, mid, mid, rows],
        [jax.ShapeDtypeStruct((SEQ, DM), F32), jax.ShapeDtypeStruct((SEQ, DM), BF16), jax.ShapeDtypeStruct((NCHIP, SEQ, FSH), BF16),
         jax.ShapeDtypeStruct((NCHIP, SEQ, FSH), BF16), jax.ShapeDtypeStruct((SEQ, DM), F32)],
        [pltpu.VMEM((tm, DM), F32)], (x, g_pre, g_post, wgt4, wut4, wd4), carry)
    return outs, sent


def _ffn_block(layer, x, g_pre, g_post, ex):
    tag = f"l{layer}_ffn_fwd"
    (x_new, h, gate, up, u), sent = _ffn_fwd(tag, x, g_pre, g_post, ex.weight(("ffn_w_gate", layer)), ex.weight(("ffn_w_up", layer)),
                                             ex.weight(("ffn_w_down", layer)), ex.carry(tag))
    ex.carried(tag, sent)
    return x_new, (x, h, gate, up, u)


def _ffn_block_bwd(layer, dx, saved, g_pre, g_post, ex):
    tag = f"l{layer}"
    x, h, gate, up, u = saved
    du, dg_post = _norm_bwd(f"{tag}_ffn_post_bwd", [dx], u, g_post)
    dact = _ffn_bwd_dact(f"{tag}_dact", du, ex.weight(("ffn_w_down", layer)))
    dgate, dup, act = _swiglu_bwd(f"{tag}_swiglu_bwd", dact, gate, up)
    d_wd = _ffn_bwd_dw(f"{tag}_dwd", act, du)
    d_wg = _ffn_bwd_dw(f"{tag}_dwg", dgate, h)
    d_wu = _ffn_bwd_dw(f"{tag}_dwu", dup, h)
    ex.grads(f"{tag}_ffn", {("ffn_w_gate", layer): d_wg, ("ffn_w_up", layer): d_wu, ("ffn_w_down", layer): d_wd})
    dh = _ffn_bwd_dh(f"{tag}_ffn_dh", dgate, ex.weight(("ffn_w_gate", layer)), dup, ex.weight(("ffn_w_up", layer)))
    dx_in, dg_pre = _norm_bwd(f"{tag}_ffn_pre_bwd", [dh], x, g_pre, res=dx)
    return dx_in, dg_pre, dg_post


def _alibi_slopes():
    return 2.0 ** (-8.0 * jnp.arange(1, NH + 1, dtype=F32) / NH)


def _local_step(x, target, norms, rpb, ex):
    g_mix_pre, g_mix_post, g_ffn_pre, g_ffn_post = norms
    row = lambda a, i: a[i:i + 1]

    bias = _na_bias_tiles(rpb)
    h0, h0t = _rms_fwd_both("l0_mix_pre", x, row(g_mix_pre, 0))
    qkv0 = _qkv_fwd("l0_qkv", h0[None], ex.weight(("na_w_qkv", 0)))
    o0, sent = _na_fwd(qkv0[0], bias, ex.carry("na_fwd"))
    ex.carried("na_fwd", sent)
    na_wo = ex.weight(("na_w_o", 0)).reshape(DM, DM)
    u0 = _proj_fwd("l0_proj", o0, na_wo)
    x1 = _resid_norm("l0_mix_post", x, u0, row(g_mix_post, 0))
    x2, ffn0 = _ffn_block(0, x1, row(g_ffn_pre, 0), row(g_ffn_post, 0), ex)

    slopes = _alibi_slopes()
    h2g, h2gt = _to_groups("l1_h_groups", _rms_fwd("l1_mix_pre", x2, row(g_mix_pre, 1), F32))
    dil_wqkv = ex.weight(("dil_w_qkv", 0))
    qkv1 = _qkv_fwd("l1_qkv", h2g, dil_wqkv)
    og, lg, sent = _dil_fwd(qkv1, slopes, ex.carry("dil_fwd"))
    ex.carried("dil_fwd", sent)
    o1, lse = _dil_merge(og, lg)
    dil_wo = ex.weight(("dil_w_o", 0)).reshape(DM, DM)
    u1 = _proj_fwd("l1_proj", o1, dil_wo)
    x3 = _resid_norm("l1_mix_post", x2, u1, row(g_mix_post, 1))
    x4, ffn1 = _ffn_block(1, x3, row(g_ffn_pre, 1), row(g_ffn_post, 1), ex)

    dx4, loss_row = _loss_grad("loss", x4, target)

    dx3, dg_fpre1, dg_fpost1 = _ffn_block_bwd(1, dx4, ffn1, row(g_ffn_pre, 1), row(g_ffn_post, 1), ex)
    du1, dg_mpost1 = _norm_bwd("l1_mix_post_bwd", [dx3], u1, row(g_mix_post, 1))
    d_dil_wo = _proj_bwd_dw("l1_dwo", o1, du1)
    do1 = _proj_bwd_do("l1_do", du1, dil_wo, F32)
    dog, ddg, lseg = _dil_bwd_prep(do1, o1, lse)
    dqkv1, sent = _dil_bwd(qkv1, dog, ddg, lseg, slopes, ex.carry("dil_bwd"))
    ex.carried("dil_bwd", sent)
    d_dil_wqkv = _qkv_bwd_dw("l1_dwqkv", h2gt, dqkv1, dil_wqkv.shape[2])
    ex.grads("l1_mix", {("dil_w_qkv", 0): d_dil_wqkv, ("dil_w_o", 0): d_dil_wo.reshape(NCHIP, DM // NCHIP, DM)})
    dh2 = _from_groups_sum("l1_dh_tokens", _qkv_bwd_dh("l1_dh", dqkv1, dil_wqkv))
    dx2, dg_mpre1 = _norm_bwd("l1_mix_pre_bwd", [dh2], x2, row(g_mix_pre, 1), res=dx3)

    dx1, dg_fpre0, dg_fpost0 = _ffn_block_bwd(0, dx2, ffn0, row(g_ffn_pre, 0), row(g_ffn_post, 0), ex)
    du0, dg_mpost0 = _norm_bwd("l0_mix_post_bwd", [dx1], u0, row(g_mix_post, 0))
    d_na_wo = _proj_bwd_dw("l0_dwo", o0, du0)
    do0 = _proj_bwd_do("l0_do", du0, na_wo)
    dqkv0, z, sent = _na_bwd(qkv0[0], bias, do0, ex.carry("na_bwd"))
    ex.carried("na_bwd", sent)
    d_rpb = _rpb_grad(z)
    na_wqkv = ex.weight(("na_w_qkv", 0))
    d_na_wqkv = _qkv_bwd_dw("l0_dwqkv", h0t[None], dqkv0[None], na_wqkv.shape[2])
    ex.grads("l0_mix", {("na_w_qkv", 0): d_na_wqkv, ("na_w_o", 0): d_na_wo.reshape(NCHIP, DM // NCHIP, DM)})
    dh0 = _qkv_bwd_dh("l0_dh", dqkv0[None], na_wqkv)
    dx0, dg_mpre0 = _norm_bwd("l0_mix_pre_bwd", [dh0[0]], x, row(g_mix_pre, 0), res=dx1)

    dnorms = (jnp.concatenate([dg_mpre0, dg_mpre1]), jnp.concatenate([dg_mpost0, dg_mpost1]),
              jnp.concatenate([dg_fpre0, dg_fpre1]), jnp.concatenate([dg_fpost0, dg_fpost1]))
    return loss_row, dx0, dnorms, d_rpb


def _place():
    x, y, c = lax.axis_index("x"), lax.axis_index("y"), lax.axis_index("c")
    chips = ((1 - x, y), (x, 1 - y), (1 - x, 1 - y))
    return x, y, c, chips


def _chip_id(chip):
    return 2 * chip[0] + chip[1]


def _comm_call(name, body, ins, out_shapes, n_sems, aliases=None):
    return pl.pallas_call(
        body, in_specs=[HBM_SPEC] * len(ins), out_specs=[HBM_SPEC] * len(out_shapes), out_shape=out_shapes,
        scratch_shapes=[pltpu.SemaphoreType.DMA((k,)) for k in n_sems], input_output_aliases=aliases or {},
        compiler_params=pltpu.CompilerParams(has_side_effects=True), name=name)(*ins)


def _gather_copies(shards):
    n = len(shards)

    def copies(src, out, sems):
        send_sems, recv_sems = sems
        x, y, c, chips = _place()

        def copy(t, k, chip, half, to, from_src=False):
            blk = out[t].at[_chip_id(chip), half]
            return pltpu.make_async_remote_copy(
                src_ref=src[t].at[half] if from_src else blk, dst_ref=blk,
                send_sem=send_sems.at[6 * t + k], recv_sem=recv_sems.at[6 * t + k], device_id=to, device_id_type=MESH)

        return copy, x, y, c, chips

    def issue(src, out, sems):
        copy, x, y, c, chips = copies(src, out, sems)
        for t in range(n):
            for j, chip in enumerate(chips):
                copy(t, j, (x, y), c, (*chip, c), from_src=True).start()

    def drain(src, out, sems):
        copy, x, y, c, chips = copies(src, out, sems)
        passed = []
        for t in range(n):
            for j, chip in enumerate(chips):
                copy(t, j, chip, c, (x, y, c)).wait_recv()
                fwd = copy(t, 3 + j, chip, c, (x, y, 1 - c))
                fwd.start()
                passed.append(fwd)
        for t in range(n):
            for j, chip in enumerate(chips):
                copy(t, 3 + j, chip, 1 - c, (x, y, c)).wait_recv()
        for t in range(n):
            for j, chip in enumerate(chips):
                copy(t, j, (x, y), c, (*chip, c), from_src=True).wait_send()
        for cp in passed:
            cp.wait_send()

    return _Carried(shards, [jax.ShapeDtypeStruct((NCHIP,) + s.shape, s.dtype) for s in shards], (6 * n, 6 * n), issue, drain)


def _pair_exchange(name, grads):
    n = len(grads)

    def body(*refs):
        g, theirs = refs[:n], refs[n:2 * n]
        send_sems, recv_sems = refs[2 * n:]
        x, y, c, _ = _place()
        swap = [pltpu.make_async_remote_copy(src_ref=g[t].at[:, 1 - c], dst_ref=theirs[t], send_sem=send_sems.at[t],
                                             recv_sem=recv_sems.at[t], device_id=(x, y, 1 - c), device_id_type=MESH) for t in range(n)]
        for cp in swap:
            cp.start()
        for cp in swap:
            cp.wait()

    return _comm_call(name, body, grads, [jax.ShapeDtypeStruct((NCHIP,) + g.shape[2:], g.dtype) for g in grads], (n, n))


def _chip_exchange_copies(parts):
    n = len(parts)

    def copies(p, slots, sems):
        send_sems, recv_sems = sems
        x, y, c, chips = _place()
        return [pltpu.make_async_remote_copy(src_ref=p[t].at[_chip_id(chips[j])], dst_ref=slots[t].at[j], send_sem=send_sems.at[3 * t + j],
                                             recv_sem=recv_sems.at[3 * t + j], device_id=(*chips[j], c), device_id_type=MESH)
                for t in range(n) for j in range(3)]

    def issue(p, slots, sems):
        for cp in copies(p, slots, sems):
            cp.start()

    def drain(p, slots, sems):
        for cp in copies(p, slots, sems):
            cp.wait()

    return _Carried(parts, [jax.ShapeDtypeStruct((3,) + p.shape[1:], p.dtype) for p in parts], (3 * n, 3 * n), issue, drain)


def _pair_share(full):
    n = len(full)

    def body(*refs):
        buf = refs[n:2 * n]
        send_sems, recv_sems = refs[2 * n:]
        x, y, c, _ = _place()
        sends = [pltpu.make_async_remote_copy(src_ref=buf[t].at[c], dst_ref=buf[t].at[c], send_sem=send_sems.at[t], recv_sem=recv_sems.at[t],
                                              device_id=(x, y, 1 - c), device_id_type=MESH) for t in range(n)]
        for cp in sends:
            cp.start()
        for t in range(n):
            pltpu.make_async_remote_copy(src_ref=buf[t].at[c], dst_ref=buf[t].at[1 - c], send_sem=send_sems.at[t], recv_sem=recv_sems.at[t],
                                         device_id=(x, y, 1 - c), device_id_type=MESH).wait_recv()
        for cp in sends:
            cp.wait_send()

    return _comm_call("grad_pair_share", body, full, [jax.ShapeDtypeStruct(f.shape, f.dtype) for f in full], (n, n),
                      aliases={t: t for t in range(n)})


SMALL_ROWS = 128


def _allreduce_small(v):
    def body(v_ref, o_ref, buf, send_sems, recv_sems):
        x, y, c, _ = _place()
        me = 4 * x + 2 * y + c
        flip = lambda a, f: 1 - a if f else a
        buf[me] = v_ref[...]
        peers = [(flip(x, d >> 2 & 1), flip(y, d >> 1 & 1), flip(c, d & 1)) for d in range(1, 8)]
        sends = [pltpu.make_async_remote_copy(src_ref=v_ref, dst_ref=buf.at[me], send_sem=send_sems.at[i], recv_sem=recv_sems.at[i],
                                              device_id=peer, device_id_type=MESH) for i, peer in enumerate(peers)]
        for cp in sends:
            cp.start()
        for i, (px, py, pc) in enumerate(peers):
            pltpu.make_async_remote_copy(src_ref=v_ref, dst_ref=buf.at[4 * px + 2 * py + pc], send_sem=send_sems.at[i], recv_sem=recv_sems.at[i],
                                         device_id=(px, py, pc), device_id_type=MESH).wait_recv()
        for cp in sends:
            cp.wait_send()
        acc = buf[0]
        for k in range(1, 8):
            acc = acc + buf[k]
        o_ref[...] = acc

    vm = pl.BlockSpec(memory_space=pltpu.VMEM)
    return pl.pallas_call(
        body, in_specs=[vm], out_specs=vm, out_shape=jax.ShapeDtypeStruct((SMALL_ROWS, 128), F32),
        scratch_shapes=[pltpu.VMEM((8, SMALL_ROWS, 128), F32), pltpu.SemaphoreType.DMA((7,)), pltpu.SemaphoreType.DMA((7,))],
        compiler_params=pltpu.CompilerParams(has_side_effects=True), name="allreduce_small")(v)


def _row_block(rows, cols, budget=1 << 20):
    best = 8
    for bm in range(8, rows + 1, 8):
        if rows % bm == 0 and bm * cols * 4 <= budget:
            best = bm
    return best


def _pair_sum(name, place, g, theirs):
    _, m, c = theirs.shape
    bm = _row_block(m, c)

    def body(place_ref, a_ref, b_ref, o_ref):
        o_ref[...] = (a_ref[...].astype(F32) + b_ref[...].astype(F32)).astype(o_ref.dtype)

    spec = pl.BlockSpec((None, bm, c), lambda k, i, pr: (k, i, 0))
    return pl.pallas_call(
        body, out_shape=jax.ShapeDtypeStruct(theirs.shape, BF16),
        grid_spec=pltpu.PrefetchScalarGridSpec(
            num_scalar_prefetch=1, grid=(NCHIP, m // bm),
            in_specs=[pl.BlockSpec((None, None, bm, c), lambda k, i, pr: (k, pr[0], i, 0)), spec], out_specs=spec),
        compiler_params=_params(("parallel", "parallel")), name=name)(place, g, theirs)


def _chip_sum(name, place, parts, slots):
    _, m, c = parts.shape
    bm = _row_block(m, c)

    def body(place_ref, p_ref, s_ref, o_ref):
        s = s_ref[...].astype(F32)
        o_ref[...] = ((p_ref[...].astype(F32) + s[0]) + s[1]) + s[2]

    return pl.pallas_call(
        body, out_shape=jax.ShapeDtypeStruct((2, m, c), F32),
        grid_spec=pltpu.PrefetchScalarGridSpec(
            num_scalar_prefetch=1, grid=(m // bm,),
            in_specs=[pl.BlockSpec((None, bm, c), lambda i, pr: (pr[1], i, 0)), pl.BlockSpec((3, bm, c), lambda i, pr: (0, i, 0))],
            out_specs=pl.BlockSpec((None, bm, c), lambda i, pr: (pr[0], i, 0))),
        compiler_params=_params(("parallel",)), name=name)(place, parts, slots)


def _adamw(name, w, g, m, v, layer=0, into=None):
    lead, rows, cols = w.shape
    bm = _row_block(rows, cols, budget=768 * 1024)
    c1 = 1.0 - ADAM_B1 ** ADAM_STEP
    c2 = 1.0 - ADAM_B2 ** ADAM_STEP

    def body(w_ref, g_ref, m_ref, v_ref, *rest):
        go_ref, d_ref, mo_ref, vo_ref = rest[-4:]
        g = g_ref[...]
        mn = ADAM_B1 * m_ref[...] + (1.0 - ADAM_B1) * g
        vn = ADAM_B2 * v_ref[...] + (1.0 - ADAM_B2) * (g * g)
        go_ref[...] = g
        mo_ref[...] = mn
        vo_ref[...] = vn
        d_ref[...] = -ADAM_LR * ((mn / c1) / (jnp.sqrt(vn / c2) + ADAM_EPS) + ADAM_WD * w_ref[...])

    spec = pl.BlockSpec((None, bm, cols), lambda i: (layer, i, 0))
    sh = jax.ShapeDtypeStruct((lead, rows, cols), F32)
    prev = [] if into is None else list(into)
    return pl.pallas_call(
        body, grid=(rows // bm,), in_specs=[spec, pl.BlockSpec((bm, cols), lambda i: (i, 0)), spec, spec] + [pl.BlockSpec(memory_space=pl.ANY)] * len(prev),
        out_specs=[spec] * 4, out_shape=[sh] * 4, input_output_aliases={4 + k: k for k in range(len(prev))},
        compiler_params=_params(("parallel",)), name=name)(w, g, m, v, *prev)


def _pack_small(norms, rpb):
    flat = jnp.concatenate([a.reshape(-1) for a in norms] + [rpb.reshape(-1)])
    return jnp.pad(flat, (0, SMALL_ROWS * 128 - flat.shape[0])).reshape(SMALL_ROWS, 128)


def _unpack_small(p):
    flat = p.reshape(-1)
    norms = [flat[i * 2 * DM:(i + 1) * 2 * DM].reshape(2, DM) for i in range(4)]
    rpb = flat[8 * DM:8 * DM + NH * 15 * 31].reshape(1, NH, 15, 31)
    return norms, rpb


FFN_NAMES = ("ffn_w_gate", "ffn_w_up", "ffn_w_down")
L0_FFN = tuple((n, 0) for n in FFN_NAMES)
L1_FFN = tuple((n, 1) for n in FFN_NAMES)
NA_KEYS = (("na_w_qkv", 0), ("na_w_o", 0))
DIL_KEYS = (("dil_w_qkv", 0), ("dil_w_o", 0))


class _Exchange:
    GATHERS = {"na_fwd": L0_FFN, "l0_ffn_fwd": DIL_KEYS, "dil_fwd": L1_FFN}
    EXCHANGES = {"dil_bwd": L1_FFN, "na_bwd": DIL_KEYS + L0_FFN}

    def __init__(self, shards):
        self.chip = 2 * lax.axis_index("x") + lax.axis_index("y")
        self.place = jnp.stack([lax.axis_index("c"), self.chip]).astype(jnp.int32)
        self.own = {k: s.reshape(2, s.shape[0] // 2, s.shape[1]).astype(BF16) for k, s in shards.items()}
        self.gathered, self.ready, self.parts, self.full = {}, {}, {}, {}
        self._take(NA_KEYS, _run_carried("gather_first", _gather_copies([self.own[k] for k in NA_KEYS])))

    def _take(self, keys, landed):
        for k, gw in zip(keys, landed):
            self.gathered[k] = lax.dynamic_update_slice(gw, self.own[k][None], (self.chip, 0, 0, 0))

    def _sum(self, keys, slots):
        for k, s in zip(keys, slots):
            self.full[k] = _chip_sum(f"chip_sum_{k[0]}_{k[1]}", self.place, self.parts[k], s)

    def weight(self, key):
        g = self.gathered[key]
        return g.reshape(NCHIP, 2 * g.shape[2], g.shape[3])

    def carry(self, tag):
        if tag in self.GATHERS:
            return _gather_copies([self.own[k] for k in self.GATHERS[tag]])
        if tag in self.EXCHANGES:
            return _chip_exchange_copies([self.parts[k] for k in self.EXCHANGES[tag]])
        return None

    def carried(self, tag, landed):
        if tag in self.GATHERS:
            self._take(self.GATHERS[tag], landed)
        elif tag in self.EXCHANGES:
            self._sum(self.EXCHANGES[tag], landed)

    def grads(self, tag, dw):
        self.ready.update(dw)
        if tag == "l1_mix":
            return
        keys = tuple(self.ready)
        mine = [self.ready[k].reshape(NCHIP, 2, -1, self.ready[k].shape[-1]) for k in keys]
        theirs = _pair_exchange(f"grad_pair_exchange_{tag}", mine)
        for k, a, b in zip(keys, mine, theirs):
            self.parts[k] = _pair_sum(f"pair_sum_{k[0]}_{k[1]}", self.place, a, b)
        self.ready = {}
        if tag == "l0_mix":
            self._sum(keys, _run_carried("grad_chip_exchange_last", _chip_exchange_copies([self.parts[k] for k in keys])))

    def finish(self):
        keys = tuple(self.full)
        shared = _pair_share([self.full[k] for k in keys])
        return {k: s.reshape(2 * s.shape[1], s.shape[2]) for k, s in zip(keys, shared)}


def kernel(x, norm_mix_pre, norm_mix_post, norm_ffn_pre, norm_ffn_post, na_w_qkv, na_w_o, na_rpb, dil_w_qkv, dil_w_o, ffn_w_gate, ffn_w_up, ffn_w_down, loss_target, m_norm_mix_pre, m_norm_mix_post, m_norm_ffn_pre, m_norm_ffn_post, m_na_w_qkv, m_na_w_o, m_na_rpb, m_dil_w_qkv, m_dil_w_o, m_ffn_w_gate, m_ffn_w_up, m_ffn_w_down, v_norm_mix_pre, v_norm_mix_post, v_norm_ffn_pre, v_norm_ffn_post, v_na_w_qkv, v_na_w_o, v_na_rpb, v_dil_w_qkv, v_dil_w_o, v_ffn_w_gate, v_ffn_w_up, v_ffn_w_down):
    tr = lambda a: jnp.swapaxes(a, 1, 2)
    weights = {"na_w_qkv": na_w_qkv, "na_w_o": na_w_o, "dil_w_qkv": dil_w_qkv, "dil_w_o": dil_w_o,
               "ffn_w_gate": tr(ffn_w_gate), "ffn_w_up": tr(ffn_w_up), "ffn_w_down": ffn_w_down}
    m_in = {"na_w_qkv": m_na_w_qkv, "na_w_o": m_na_w_o, "dil_w_qkv": m_dil_w_qkv, "dil_w_o": m_dil_w_o,
            "ffn_w_gate": tr(m_ffn_w_gate), "ffn_w_up": tr(m_ffn_w_up), "ffn_w_down": m_ffn_w_down}
    v_in = {"na_w_qkv": v_na_w_qkv, "na_w_o": v_na_w_o, "dil_w_qkv": v_dil_w_qkv, "dil_w_o": v_dil_w_o,
            "ffn_w_gate": tr(v_ffn_w_gate), "ffn_w_up": tr(v_ffn_w_up), "ffn_w_down": v_ffn_w_down}

    ex = _Exchange({(n, l): weights[n][l] for n in weights for l in range(weights[n].shape[0])})
    norms = (norm_mix_pre, norm_mix_post, norm_ffn_pre, norm_ffn_post)
    loss_row, dx, dnorms, d_rpb = _local_step(x[0], loss_target[0], norms, na_rpb[0], ex)
    loss = lax.psum(loss_row[0, 0], ("x", "y", "c"))
    full = ex.finish()
    small = _allreduce_small(_pack_small(dnorms, d_rpb))

    out_g, out_d, out_m, out_v = {}, {}, {}, {}
    for n in weights:
        res = None
        for l in range(weights[n].shape[0]):
            res = _adamw(f"adamw_{n}_{l}", weights[n], full[(n, l)], m_in[n], v_in[n], l, res)
        if n in ("ffn_w_gate", "ffn_w_up"):
            res = [tr(r) for r in res]
        out_g[n], out_d[n], out_m[n], out_v[n] = res
    sm_names = ("norm_mix_pre", "norm_mix_post", "norm_ffn_pre", "norm_ffn_post", "na_rpb")
    sm = _adamw("adamw_small", _pack_small(norms, na_rpb)[None], small,
                _pack_small((m_norm_mix_pre, m_norm_mix_post, m_norm_ffn_pre, m_norm_ffn_post), m_na_rpb)[None],
                _pack_small((v_norm_mix_pre, v_norm_mix_post, v_norm_ffn_pre, v_norm_ffn_post), v_na_rpb)[None])
    for res, dst in zip(sm, (out_g, out_d, out_m, out_v)):
        ns, rp = _unpack_small(res)
        for n, a in zip(sm_names, ns + [rp]):
            dst[n] = a

    order = ("norm_mix_pre", "norm_mix_post", "norm_ffn_pre", "norm_ffn_post", "na_w_qkv", "na_w_o", "na_rpb", "dil_w_qkv", "dil_w_o",
             "ffn_w_gate", "ffn_w_up", "ffn_w_down")
    return (loss, dx[None], *[out_g[n] for n in order], *[out_d[n] for n in order], *[out_m[n] for n in order], *[out_v[n] for n in order])
```

```python
import functools

import numpy as np
import jax
import jax.numpy as jnp
from jax import lax
from jax.experimental import pallas as pl
from jax.experimental.pallas import tpu as pltpu

F32 = jnp.float32
BF16 = jnp.bfloat16

SEQ = 2048
DM = 1024
NH = 16
HD = 64
DFF = 2816
NCHIP = 4
FSH = DFF // NCHIP
GRID_W = 64
NA_QROWS = 4
NA_QB = NA_QROWS * GRID_W
NA_WROWS = 12
NA_WIN = NA_WROWS * GRID_W
DIL = (1, 4, 16)
DIL_QB = 256
DIL_WIN = DIL_QB + 128
DIL_RADIUS = 64
RMS_EPS = 1e-6
NEG = -1e30
QSCALE = HD ** -0.5
CH = 256
MESH = pl.DeviceIdType.MESH

ADAM_LR, ADAM_B1, ADAM_B2, ADAM_EPS, ADAM_WD, ADAM_STEP = 0.001, 0.9, 0.999, 1e-08, 0.01, 10

VMEM_LIMIT = 56 * 1024 * 1024

_NN = (((1,), (0,)), ((), ()))
_NT = (((1,), (1,)), ((), ()))
_TN = (((0,), (0,)), ((), ()))


def _params(sem):
    return pltpu.CompilerParams(dimension_semantics=sem, vmem_limit_bytes=VMEM_LIMIT)


def _matmul(name, pairs, grid, out_shape, out_spec, acc_shape):
    nk = grid[-1]
    npair = len(pairs)
    n_in = 2 * npair

    def body(*refs):
        ins, o_ref = refs[:2 * npair], refs[n_in]
        part = None
        for p in range(npair):
            d = lax.dot_general(ins[2 * p][...].astype(BF16), ins[2 * p + 1][...].astype(BF16), pairs[p][4],
                                preferred_element_type=F32)
            part = d if part is None else part + d
        if nk == 1:
            o_ref[...] = part.astype(o_ref.dtype)
        else:
            acc_ref = refs[n_in + 1]
            kk = pl.program_id(len(grid) - 1)

            @pl.when(kk == 0)
            def _():
                acc_ref[...] = part

            @pl.when(kk > 0)
            def _():
                acc_ref[...] += part

            @pl.when(kk == nk - 1)
            def _():
                o_ref[...] = acc_ref[...].astype(o_ref.dtype)

    ops, specs = [], []
    for a, a_spec, b, b_spec, _ in pairs:
        ops += [a, b]
        specs += [a_spec, b_spec]
    return pl.pallas_call(
        body, grid=grid, in_specs=specs, out_specs=out_spec, out_shape=out_shape,
        scratch_shapes=[] if nk == 1 else [pltpu.VMEM(acc_shape, F32)],
        compiler_params=_params(("parallel",) * (len(grid) - 1) + ("arbitrary",)), name=name,
    )(*ops)


def _qkv_fwd(name, h_all, w4):
    g_n = h_all.shape[0]
    per = w4.shape[2] // CH
    return _matmul(
        name, [(h_all, pl.BlockSpec((None, SEQ, DM), lambda g, q, k: (g, 0, 0)),
                w4, pl.BlockSpec((None, DM, CH), lambda g, q, k: ((g * 12 + q) // per, 0, (g * 12 + q) % per)), _NN)],
        (g_n, 12, 1), jax.ShapeDtypeStruct((g_n, SEQ, 3 * DM), BF16),
        pl.BlockSpec((None, SEQ, CH), lambda g, q, k: (g, 0, q)), None)


def _qkv_bwd_dh(name, dqkv, w4):
    g_n = dqkv.shape[0]
    per = w4.shape[2] // CH
    tm = 1024

    def pair(cb):
        chunk = lambda g, t: g * 12 + t * 4 + cb
        return (dqkv, pl.BlockSpec((None, None, tm, CH), lambda g, i, t: (g, t, i, cb)),
                w4, pl.BlockSpec((None, DM, CH), lambda g, i, t: (chunk(g, t) // per, 0, chunk(g, t) % per)), _NT)

    return _matmul(name, [pair(cb) for cb in range(4)], (g_n, SEQ // tm, 3), jax.ShapeDtypeStruct((g_n, SEQ, DM), F32),
                   pl.BlockSpec((None, tm, DM), lambda g, i, t: (g, i, 0)), (tm, DM))


def _qkv_bwd_dw(name, ht_all, dqkv, shard_cols):
    g_n = dqkv.shape[0]
    per = shard_cols // CH
    return _matmul(
        name, [(ht_all, pl.BlockSpec((None, DM, SEQ), lambda qq, k: (qq // 12, 0, 0)),
                dqkv, pl.BlockSpec((None, None, SEQ, CH), lambda qq, k: (qq // 12, (qq % 12) // 4, 0, qq % 4)), _NN)],
        (g_n * 12, 1), jax.ShapeDtypeStruct((NCHIP, DM, shard_cols), BF16),
        pl.BlockSpec((None, DM, CH), lambda qq, k: (qq // per, 0, qq % per)), None)


def _proj_fwd(name, o, wo):
    tm = 512
    return _matmul(
        name, [(o, pl.BlockSpec((tm, DM), lambda i, k: (i, 0)), wo, pl.BlockSpec((DM, DM), lambda i, k: (0, 0)), _NN)],
        (SEQ // tm, 1), jax.ShapeDtypeStruct((SEQ, DM), F32), pl.BlockSpec((tm, DM), lambda i, k: (i, 0)), None)


def _proj_bwd_do(name, du, wo, dtype=BF16):
    tm = 512
    return _matmul(
        name, [(du, pl.BlockSpec((tm, DM), lambda i, k: (i, 0)), wo, pl.BlockSpec((DM, DM), lambda i, k: (0, 0)), _NT)],
        (SEQ // tm, 1), jax.ShapeDtypeStruct((SEQ, DM), dtype), pl.BlockSpec((tm, DM), lambda i, k: (i, 0)), None)


def _proj_bwd_dw(name, o, du):
    tk, tn = 512, 512
    return _matmul(
        name, [(o, pl.BlockSpec((tk, DM), lambda j, k: (k, 0)), du, pl.BlockSpec((tk, tn), lambda j, k: (k, j)), _TN)],
        (DM // tn, SEQ // tk), jax.ShapeDtypeStruct((DM, DM), BF16), pl.BlockSpec((DM, tn), lambda j, k: (0, j)), (DM, tn))


def _ffn_wspec(index_map):
    return pl.BlockSpec((None, FSH, DM), index_map)


def _ffn_bwd_dw(name, a4, b):
    tk = 512
    return _matmul(
        name, [(a4, pl.BlockSpec((None, tk, FSH), lambda s, k: (s, k, 0)), b, pl.BlockSpec((tk, DM), lambda s, k: (k, 0)), _TN)],
        (NCHIP, SEQ // tk), jax.ShapeDtypeStruct((NCHIP, FSH, DM), BF16), _ffn_wspec(lambda s, k: (s, 0, 0)), (FSH, DM))


ROWS = 256


def _row_spec():
    return pl.BlockSpec((ROWS, DM), lambda i: (i, 0))


def _vec_spec():
    return pl.BlockSpec((1, DM), lambda i: (0, 0))


def _rms_fwd(name, x, g, dtype=BF16):
    def body(x_ref, g_ref, o_ref):
        x = x_ref[...]
        r = lax.rsqrt(jnp.mean(x * x, axis=-1, keepdims=True) + RMS_EPS)
        o_ref[...] = (x * r * g_ref[...]).astype(o_ref.dtype)

    return pl.pallas_call(body, grid=(SEQ // ROWS,), in_specs=[_row_spec(), _vec_spec()], out_specs=_row_spec(),
                          out_shape=jax.ShapeDtypeStruct((SEQ, DM), dtype), compiler_params=_params(("parallel",)), name=name)(x, g)


def _rms_fwd_both(name, x, g):
    def body(x_ref, g_ref, o_ref, t_ref):
        x = x_ref[...]
        r = lax.rsqrt(jnp.mean(x * x, axis=-1, keepdims=True) + RMS_EPS)
        h = x * r * g_ref[...]
        o_ref[...] = h.astype(o_ref.dtype)
        t_ref[...] = h.T.astype(t_ref.dtype)

    return pl.pallas_call(
        body, grid=(SEQ // ROWS,), in_specs=[_row_spec(), _vec_spec()], out_specs=[_row_spec(), pl.BlockSpec((DM, ROWS), lambda i: (0, i))],
        out_shape=[jax.ShapeDtypeStruct((SEQ, DM), BF16), jax.ShapeDtypeStruct((DM, SEQ), BF16)],
        compiler_params=_params(("parallel",)), name=name)(x, g)


def _resid_norm(name, x, u, g):
    def body(x_ref, u_ref, g_ref, o_ref):
        u = u_ref[...]
        r = lax.rsqrt(jnp.mean(u * u, axis=-1, keepdims=True) + RMS_EPS)
        o_ref[...] = x_ref[...] + u * r * g_ref[...]

    return pl.pallas_call(body, grid=(SEQ // ROWS,), in_specs=[_row_spec(), _row_spec(), _vec_spec()], out_specs=_row_spec(),
                          out_shape=jax.ShapeDtypeStruct((SEQ, DM), F32), compiler_params=_params(("parallel",)), name=name)(x, u, g)


def _norm_bwd(name, dys, u, g, res=None):
    ndy = len(dys)

    def body(*refs):
        dy = refs[0][...]
        for r_ in refs[1:ndy]:
            dy = dy + r_[...]
        u_ref, g_ref = refs[ndy], refs[ndy + 1]
        res_ref = refs[ndy + 2] if res is not None else None
        du_ref, dg_ref = refs[-2], refs[-1]
        u = u_ref[...]
        r = lax.rsqrt(jnp.mean(u * u, axis=-1, keepdims=True) + RMS_EPS)
        yh = u * r
        t = dy * g_ref[...]
        du = r * (t - yh * jnp.mean(t * yh, axis=-1, keepdims=True))
        if res_ref is not None:
            du = du + res_ref[...]
        du_ref[...] = du

        @pl.when(pl.program_id(0) == 0)
        def _():
            dg_ref[...] = jnp.zeros_like(dg_ref)

        dg_ref[...] += jnp.sum(dy * yh, axis=0, keepdims=True)

    ops = list(dys) + [u, g] + ([res] if res is not None else [])
    specs = [_row_spec()] * ndy + [_row_spec(), _vec_spec()] + ([_row_spec()] if res is not None else [])
    return pl.pallas_call(
        body, grid=(SEQ // ROWS,), in_specs=specs, out_specs=[_row_spec(), _vec_spec()],
        out_shape=[jax.ShapeDtypeStruct((SEQ, DM), F32), jax.ShapeDtypeStruct((1, DM), F32)],
        compiler_params=_params(("arbitrary",)), name=name)(*ops)


def _loss_grad(name, y, t):
    def body(y_ref, t_ref, dy_ref, l_ref):
        e = y_ref[...] - t_ref[...]
        dy_ref[...] = e * (1.0 / DM)

        @pl.when(pl.program_id(0) == 0)
        def _():
            l_ref[...] = jnp.zeros_like(l_ref)

        l_ref[...] += jnp.sum(e * e) * (0.5 / DM)

    return pl.pallas_call(
        body, grid=(SEQ // ROWS,), in_specs=[_row_spec(), _row_spec()],
        out_specs=[_row_spec(), pl.BlockSpec((1, 128), lambda i: (0, 0))],
        out_shape=[jax.ShapeDtypeStruct((SEQ, DM), F32), jax.ShapeDtypeStruct((1, 128), F32)],
        compiler_params=_params(("arbitrary",)), name=name)(y, t)


HBM_SPEC = pl.BlockSpec(memory_space=pltpu.HBM)


class _Carried:
    def __init__(self, ins, out_shapes, n_sems, issue, drain):
        self.ins, self.out_shapes, self.n_sems, self.issue, self.drain = list(ins), list(out_shapes), tuple(n_sems), issue, drain


def _carrier_call(name, body, grid, in_specs, out_specs, out_shape, scratch_shapes, operands, carry):
    n_in, n_out, n_scr = len(in_specs), len(out_specs), len(scratch_shapes)
    if carry is None:
        res = pl.pallas_call(body, grid=grid, in_specs=in_specs, out_specs=out_specs, out_shape=out_shape, scratch_shapes=scratch_shapes,
                             compiler_params=_params(("arbitrary",) * len(grid)), name=name)(*operands)
        return list(res), []
    ci, co = len(carry.ins), len(carry.out_shapes)

    def wrapped(*refs):
        ins, cins = refs[:n_in], refs[n_in:n_in + ci]
        outs, couts = refs[n_in + ci:n_in + ci + n_out], refs[n_in + ci + n_out:n_in + ci + n_out + co]
        scr, sems = refs[n_in + ci + n_out + co:n_in + ci + n_out + co + n_scr], refs[n_in + ci + n_out + co + n_scr:]
        first = functools.reduce(jnp.logical_and, [pl.program_id(a) == 0 for a in range(len(grid))])
        last = functools.reduce(jnp.logical_and, [pl.program_id(a) == grid[a] - 1 for a in range(len(grid))])

        @pl.when(first)
        def _():
            carry.issue(cins, couts, sems)

        body(*ins, *outs, *scr)

        @pl.when(last)
        def _():
            carry.drain(cins, couts, sems)

    res = pl.pallas_call(
        wrapped, grid=grid, in_specs=list(in_specs) + [HBM_SPEC] * ci, out_specs=list(out_specs) + [HBM_SPEC] * co,
        out_shape=list(out_shape) + carry.out_shapes,
        scratch_shapes=list(scratch_shapes) + [pltpu.SemaphoreType.DMA((k,)) for k in carry.n_sems],
        compiler_params=pltpu.CompilerParams(dimension_semantics=("arbitrary",) * len(grid), vmem_limit_bytes=VMEM_LIMIT, has_side_effects=True),
        name=name)(*operands, *carry.ins)
    return list(res[:n_out]), list(res[n_out:])


def _run_carried(name, carry):
    def body(*refs):
        ci, co = len(carry.ins), len(carry.out_shapes)
        carry.issue(refs[:ci], refs[ci:ci + co], refs[ci + co:])
        carry.drain(refs[:ci], refs[ci:ci + co], refs[ci + co:])

    return pl.pallas_call(
        body, in_specs=[HBM_SPEC] * len(carry.ins), out_specs=[HBM_SPEC] * len(carry.out_shapes), out_shape=carry.out_shapes,
        scratch_shapes=[pltpu.SemaphoreType.DMA((k,)) for k in carry.n_sems],
        compiler_params=pltpu.CompilerParams(has_side_effects=True), name=name)(*carry.ins)


NA_BLOCKS = SEQ // NA_QB
NA_ROWS_TOTAL = SEQ // GRID_W
NA_CLASSES = ((0, 0), (8, 4), (NA_ROWS_TOTAL - NA_QROWS, NA_ROWS_TOTAL - NA_WROWS))


def _na_pairs(i0, ws):
    out = []
    for qi in range(NA_QROWS):
        i = i0 + qi
        rs = min(max(i - 4, 0), NA_ROWS_TOTAL - 8)
        for kr in range(NA_WROWS):
            r = ws + kr
            if rs <= r < rs + 8:
                out.append((qi, kr, r - i + 7))
    return out


def _diag_onehot():
    qc, kc = np.meshgrid(np.arange(GRID_W), np.arange(GRID_W), indexing="ij")
    e = np.zeros((GRID_W * GRID_W, 128), np.float32)
    j = (kc - qc + 15).reshape(-1)
    ok = (j >= 0) & (j <= 30)
    e[np.arange(GRID_W * GRID_W)[ok], j[ok]] = 1.0
    return jnp.asarray(e)


def _rpb_expand(rpb):
    r2 = jnp.pad(rpb.reshape(NH * 15, 31), ((0, 0), (0, 128 - 31)))

    def body(r_ref, e_ref, o_ref):
        o_ref[...] = lax.dot_general(r_ref[...], e_ref[...], _NT, preferred_element_type=F32, precision=lax.Precision.HIGHEST)

    out = pl.pallas_call(body, out_shape=jax.ShapeDtypeStruct((NH * 15, GRID_W * GRID_W), F32), name="rpb_expand",
                         compiler_params=pltpu.CompilerParams(vmem_limit_bytes=VMEM_LIMIT))(r2, _diag_onehot())
    return out.reshape(NH, 15, GRID_W, GRID_W)


def _na_bias_tiles(rpb):
    col = np.arange(GRID_W)
    col_start = np.clip(col - 8, 0, GRID_W - 16)
    col_mask = (col[None, :] >= col_start[:, None]) & (col[None, :] < col_start[:, None] + 16)
    rc = jnp.where(col_mask[None, None], _rpb_expand(rpb), NEG)
    neg = jnp.full((NH, GRID_W, GRID_W), NEG, F32)
    tiles = []
    for i0, ws in NA_CLASSES:
        pairs = {(qi, kr): dr for qi, kr, dr in _na_pairs(i0, ws)}
        rows = [jnp.concatenate([rc[:, pairs[(qi, kr)]] if (qi, kr) in pairs else neg for kr in range(NA_WROWS)], axis=2)
                for qi in range(NA_QROWS)]
        tiles.append(jnp.concatenate(rows, axis=1))
    return jnp.stack(tiles)


def _na_cls(b):
    return jnp.where(b == 0, 0, jnp.where(b == NA_BLOCKS - 1, 2, 1))


def _na_start(b):
    return pl.multiple_of(jnp.clip(b * NA_QROWS - 4, 0, NA_ROWS_TOTAL - NA_WROWS) * GRID_W, GRID_W)


HPS = 4
LW = HPS * HD
NLW = DM // LW


NA_BWD_HPS = 2


def _na_in_specs(hps=HPS):
    lw = hps * HD
    nlw = DM // lw
    return [pl.BlockSpec((NA_QB, lw), lambda hp, b: (b, hp)),
            pl.BlockSpec((SEQ, lw), lambda hp, b: (0, nlw + hp)),
            pl.BlockSpec((SEQ, lw), lambda hp, b: (0, 2 * nlw + hp)),
            pl.BlockSpec((None, hps, NA_QB, NA_WIN), lambda hp, b: (_na_cls(b), hp, 0, 0))]


def _na_fwd(qkv, bias, carry):
    def body(q_ref, k_ref, v_ref, b_ref, o_ref):
        start = _na_start(pl.program_id(1))
        q = q_ref[...]
        kw = k_ref[pl.ds(start, NA_WIN), :]
        vw = v_ref[pl.ds(start, NA_WIN), :]
        outs = []
        for hh in range(HPS):
            sl = slice(hh * HD, (hh + 1) * HD)
            s = lax.dot_general(q[:, sl] * QSCALE, kw[:, sl], _NT, preferred_element_type=F32) + b_ref[hh]
            p = jnp.exp(s - jnp.max(s, axis=-1, keepdims=True))
            l = jnp.sum(p, axis=-1, keepdims=True)
            outs.append(jnp.dot(p.astype(BF16), vw[:, sl], preferred_element_type=F32) / l)
        o_ref[...] = jnp.concatenate(outs, axis=1).astype(o_ref.dtype)

    (o,), sent = _carrier_call(
        "na_fwd", body, (NLW, NA_BLOCKS), _na_in_specs(), [pl.BlockSpec((NA_QB, LW), lambda hp, b: (b, hp))],
        [jax.ShapeDtypeStruct((SEQ, DM), BF16)], [], (qkv, qkv, qkv, bias), carry)
    return o, sent


def _na_bwd(qkv, bias, do, carry):
    lw = NA_BWD_HPS * HD

    def body(q_ref, k_ref, v_ref, b_ref, do_ref, dqkv_ref, z_ref, dk_acc, dv_acc):
        blk = pl.program_id(1)

        @pl.when(blk == 0)
        def _():
            dk_acc[...] = jnp.zeros_like(dk_acc)
            dv_acc[...] = jnp.zeros_like(dv_acc)
            z_ref[...] = jnp.zeros_like(z_ref)

        start = _na_start(blk)
        q = q_ref[...]
        do = do_ref[...]
        kw = k_ref[pl.ds(start, NA_WIN), :]
        vw = v_ref[pl.ds(start, NA_WIN), :]
        dqs, dks, dvs = [], [], []
        for hh in range(NA_BWD_HPS):
            sl = slice(hh * HD, (hh + 1) * HD)
            qh = q[:, sl] * QSCALE
            s = lax.dot_general(qh, kw[:, sl], _NT, preferred_element_type=F32) + b_ref[hh]
            p = jnp.exp(s - jnp.max(s, axis=-1, keepdims=True))
            p = p / jnp.sum(p, axis=-1, keepdims=True)
            dp = lax.dot_general(do[:, sl], vw[:, sl], _NT, preferred_element_type=F32)
            ds = p * (dp - jnp.sum(p * dp, axis=-1, keepdims=True))
            dsb = ds.astype(BF16)
            dqs.append(jnp.dot(dsb, kw[:, sl], preferred_element_type=F32) * QSCALE)
            dks.append(lax.dot_general(dsb, qh, _TN, preferred_element_type=F32))
            dvs.append(lax.dot_general(p.astype(BF16), do[:, sl], _TN, preferred_element_type=F32))
            for cls, (i0, ws) in enumerate(NA_CLASSES):
                @pl.when(_na_cls(blk) == cls)
                def _(ds=ds, hh=hh, i0=i0, ws=ws):
                    for qi, kr, dr in _na_pairs(i0, ws):
                        z_ref[hh, dr * GRID_W:(dr + 1) * GRID_W, :] += ds[qi * GRID_W:(qi + 1) * GRID_W, kr * GRID_W:(kr + 1) * GRID_W]
        dqkv_ref[0, pl.ds(pl.multiple_of(blk * NA_QB, NA_QB), NA_QB), :] = jnp.concatenate(dqs, axis=1).astype(dqkv_ref.dtype)
        dk_acc[pl.ds(start, NA_WIN), :] += jnp.concatenate(dks, axis=1)
        dv_acc[pl.ds(start, NA_WIN), :] += jnp.concatenate(dvs, axis=1)

        @pl.when(blk == NA_BLOCKS - 1)
        def _():
            dqkv_ref[1] = dk_acc[...].astype(dqkv_ref.dtype)
            dqkv_ref[2] = dv_acc[...].astype(dqkv_ref.dtype)

    (dqkv, z), sent = _carrier_call(
        "na_bwd", body, (NH // NA_BWD_HPS, NA_BLOCKS),
        _na_in_specs(NA_BWD_HPS) + [pl.BlockSpec((NA_QB, lw), lambda hp, b: (b, hp))],
        [pl.BlockSpec((3, SEQ, lw), lambda hp, b: (0, 0, hp)), pl.BlockSpec((NA_BWD_HPS, 15 * GRID_W, GRID_W), lambda hp, b: (hp, 0, 0))],
        [jax.ShapeDtypeStruct((3, SEQ, DM), BF16), jax.ShapeDtypeStruct((NH, 15 * GRID_W, GRID_W), F32)],
        [pltpu.VMEM((SEQ, lw), F32), pltpu.VMEM((SEQ, lw), F32)], (qkv, qkv, qkv, bias, do), carry)
    return dqkv, z, sent


def _rpb_grad(z):
    z2 = z.reshape(NH * 15, GRID_W * GRID_W)

    def body(z_ref, e_ref, o_ref):
        o_ref[...] = jnp.dot(z_ref[...], e_ref[...], preferred_element_type=F32, precision=lax.Precision.HIGHEST)

    out = pl.pallas_call(body, out_shape=jax.ShapeDtypeStruct((NH * 15, 128), F32), name="rpb_grad",
                         compiler_params=pltpu.CompilerParams(vmem_limit_bytes=VMEM_LIMIT))(z2, _diag_onehot())
    return out[:, :31].reshape(NH, 15, 31)


DIL_BLOCKS = SEQ // DIL_QB
DIL_HPS = 8
DIL_LW = DIL_HPS * HD
DIL_NLW = DM // DIL_LW


COLS = 128


def _col_spec():
    return pl.BlockSpec((SEQ, COLS), lambda j: (0, j))


def _grp_spec():
    return pl.BlockSpec((3, SEQ, COLS), lambda j: (0, 0, j))


def _store_group_order(dst_ref, src_ref):
    for g, d in enumerate(DIL):
        n = SEQ // d
        for r in range(d):
            dst_ref[g, r * n:(r + 1) * n, :] = src_ref[pl.ds(r, n, stride=d), :].astype(dst_ref.dtype)


def _store_token_order(dst_ref, src_ref, g):
    d = DIL[g]
    n = SEQ // d
    for r in range(d):
        dst_ref[pl.ds(r, n, stride=d), :] = src_ref[g, r * n:(r + 1) * n, :]


def _to_groups(name, a):
    def body(a_ref, o_ref, t_ref):
        _store_group_order(o_ref, a_ref)
        for g in range(3):
            t_ref[g] = o_ref[g].astype(F32).T.astype(t_ref.dtype)

    return pl.pallas_call(
        body, grid=(DM // COLS,), in_specs=[_col_spec()], out_specs=[_grp_spec(), pl.BlockSpec((3, COLS, SEQ), lambda j: (0, j, 0))],
        out_shape=[jax.ShapeDtypeStruct((3, SEQ, DM), BF16), jax.ShapeDtypeStruct((3, DM, SEQ), BF16)],
        compiler_params=_params(("parallel",)), name=name)(a)


def _from_groups_sum(name, a):
    def body(a_ref, o_ref, t1, t2):
        _store_token_order(t1, a_ref, 1)
        _store_token_order(t2, a_ref, 2)
        o_ref[...] = (a_ref[0] + t1[...]) + t2[...]

    return pl.pallas_call(body, grid=(DM // COLS,), in_specs=[_grp_spec()], out_specs=_col_spec(),
                          out_shape=jax.ShapeDtypeStruct((SEQ, DM), F32), scratch_shapes=[pltpu.VMEM((SEQ, COLS), F32)] * 2,
                          compiler_params=_params(("parallel",)), name=name)(a)


def _dil_start(b):
    return pl.multiple_of(jnp.clip(b * DIL_QB - DIL_RADIUS, 0, SEQ - DIL_WIN), DIL_RADIUS)


def _dil_mask(g, b, start):
    shift = 11 - 2 * g
    ii = b * DIL_QB + lax.broadcasted_iota(jnp.int32, (DIL_QB, DIL_WIN), 0)
    jj = start + lax.broadcasted_iota(jnp.int32, (DIL_QB, DIL_WIN), 1)
    dist = jnp.abs(ii - jj)
    valid = (dist <= DIL_RADIUS) & (jnp.right_shift(ii, shift) == jnp.right_shift(jj, shift))
    return valid, dist.astype(F32)


def _dil_in_specs():
    return [pl.BlockSpec(memory_space=pltpu.SMEM),
            pl.BlockSpec((None, DIL_QB, DIL_LW), lambda g, hp, b: (g, b, hp)),
            pl.BlockSpec((None, SEQ, DIL_LW), lambda g, hp, b: (g, 0, DIL_NLW + hp)),
            pl.BlockSpec((None, SEQ, DIL_LW), lambda g, hp, b: (g, 0, 2 * DIL_NLW + hp))]


def _dil_fwd(qkv, slopes, carry):
    def body(sl_ref, q_ref, k_ref, v_ref, o_ref, lse_ref):
        g, hp, b = pl.program_id(0), pl.program_id(1), pl.program_id(2)
        start = _dil_start(b)
        valid, dist = _dil_mask(g, b, start)
        dil = jnp.left_shift(1, 2 * g).astype(F32)
        q = q_ref[...]
        kw = k_ref[pl.ds(start, DIL_WIN), :]
        vw = v_ref[pl.ds(start, DIL_WIN), :]
        outs, lses = [], []
        for hh in range(DIL_HPS):
            sl = slice(hh * HD, (hh + 1) * HD)
            s = lax.dot_general(q[:, sl] * QSCALE, kw[:, sl], _NT, preferred_element_type=F32)
            s = jnp.where(valid, s - (sl_ref[hp * DIL_HPS + hh] * dil) * dist, NEG)
            m = jnp.max(s, axis=-1, keepdims=True)
            p = jnp.exp(s - m)
            l = jnp.sum(p, axis=-1, keepdims=True)
            outs.append(jnp.dot(p.astype(BF16), vw[:, sl], preferred_element_type=F32) / l)
            lses.append(jnp.broadcast_to(m + jnp.log(l), (DIL_QB, HD)))
        o_ref[...] = jnp.concatenate(outs, axis=1)
        lse_ref[...] = jnp.concatenate(lses, axis=1)

    ospec = pl.BlockSpec((None, DIL_QB, DIL_LW), lambda g, hp, b: (g, b, hp))
    sh = jax.ShapeDtypeStruct((3, SEQ, DM), F32)
    (o, lse), sent = _carrier_call("dil_fwd", body, (3, DIL_NLW, DIL_BLOCKS), _dil_in_specs(), [ospec, ospec], [sh, sh], [],
                                   (slopes, qkv, qkv, qkv), carry)
    return o, lse, sent


def _dil_merge(o_all, lse_all):
    def body(o_ref, l_ref, out_ref, lse_ref, o1, o2, l1, l2):
        for g, (ot, lt) in ((1, (o1, l1)), (2, (o2, l2))):
            _store_token_order(ot, o_ref, g)
            _store_token_order(lt, l_ref, g)
        la, lb, lc = l_ref[0], l1[...], l2[...]
        m = jnp.maximum(jnp.maximum(la, lb), lc)
        wa, wb, wc = jnp.exp(la - m), jnp.exp(lb - m), jnp.exp(lc - m)
        sw = (wa + wb) + wc
        out_ref[...] = (((wa * o_ref[0] + wb * o1[...]) + wc * o2[...]) / sw).astype(out_ref.dtype)
        lse_ref[...] = m + jnp.log(sw)

    return pl.pallas_call(
        body, grid=(DM // COLS,), in_specs=[_grp_spec(), _grp_spec()], out_specs=[_col_spec(), _col_spec()],
        out_shape=[jax.ShapeDtypeStruct((SEQ, DM), BF16), jax.ShapeDtypeStruct((SEQ, DM), F32)],
        scratch_shapes=[pltpu.VMEM((SEQ, COLS), F32)] * 4, compiler_params=_params(("parallel",)), name="dil_merge")(o_all, lse_all)


def _dil_bwd_prep(do, o, lse):
    def body(do_ref, o_ref, lse_ref, dog_ref, ddg_ref, lseg_ref, dd):
        prod = do_ref[...] * o_ref[...].astype(F32)
        dd[...] = jnp.concatenate(
            [jnp.broadcast_to(jnp.sum(prod[:, h * HD:(h + 1) * HD], axis=-1, keepdims=True), (SEQ, HD)) for h in range(COLS // HD)], axis=1)
        _store_group_order(dog_ref, do_ref)
        _store_group_order(ddg_ref, dd)
        _store_group_order(lseg_ref, lse_ref)

    return pl.pallas_call(
        body, grid=(DM // COLS,), in_specs=[_col_spec()] * 3, out_specs=[_grp_spec()] * 3,
        out_shape=[jax.ShapeDtypeStruct((3, SEQ, DM), BF16), jax.ShapeDtypeStruct((3, SEQ, DM), F32), jax.ShapeDtypeStruct((3, SEQ, DM), F32)],
        scratch_shapes=[pltpu.VMEM((SEQ, COLS), F32)], compiler_params=_params(("parallel",)), name="dil_bwd_prep")(do, o, lse)


def _dil_bwd(qkv, do, dd, lse, slopes, carry):
    def body(sl_ref, q_ref, k_ref, v_ref, do_ref, dd_ref, lse_ref, dqkv_ref, dk_acc, dv_acc):
        g, hp, b = pl.program_id(0), pl.program_id(1), pl.program_id(2)

        @pl.when(b == 0)
        def _():
            dk_acc[...] = jnp.zeros_like(dk_acc)
            dv_acc[...] = jnp.zeros_like(dv_acc)

        start = _dil_start(b)
        valid, dist = _dil_mask(g, b, start)
        dil = jnp.left_shift(1, 2 * g).astype(F32)
        q = q_ref[...]
        do = do_ref[...]
        kw = k_ref[pl.ds(start, DIL_WIN), :]
        vw = v_ref[pl.ds(start, DIL_WIN), :]
        lse = lse_ref[...]
        dd = dd_ref[...]
        dqs, dks, dvs = [], [], []
        for hh in range(DIL_HPS):
            sl = slice(hh * HD, (hh + 1) * HD)
            qh = q[:, sl] * QSCALE
            s = lax.dot_general(qh, kw[:, sl], _NT, preferred_element_type=F32)
            s = jnp.where(valid, s - (sl_ref[hp * DIL_HPS + hh] * dil) * dist, NEG)
            p = jnp.exp(s - lse[:, hh * HD:hh * HD + 1])
            dp = lax.dot_general(do[:, sl], vw[:, sl], _NT, preferred_element_type=F32)
            dsb = (p * (dp - dd[:, hh * HD:hh * HD + 1])).astype(BF16)
            dqs.append(jnp.dot(dsb, kw[:, sl], preferred_element_type=F32) * QSCALE)
            dks.append(lax.dot_general(dsb, qh, _TN, preferred_element_type=F32))
            dvs.append(lax.dot_general(p.astype(BF16), do[:, sl], _TN, preferred_element_type=F32))
        dqkv_ref[0, pl.ds(pl.multiple_of(b * DIL_QB, DIL_QB), DIL_QB), :] = jnp.concatenate(dqs, axis=1).astype(dqkv_ref.dtype)
        dk_acc[pl.ds(start, DIL_WIN), :] += jnp.concatenate(dks, axis=1)
        dv_acc[pl.ds(start, DIL_WIN), :] += jnp.concatenate(dvs, axis=1)

        @pl.when(b == DIL_BLOCKS - 1)
        def _():
            dqkv_ref[1] = dk_acc[...].astype(dqkv_ref.dtype)
            dqkv_ref[2] = dv_acc[...].astype(dqkv_ref.dtype)

    rspec = pl.BlockSpec((None, DIL_QB, DIL_LW), lambda g, hp, b: (g, b, hp))
    (dqkv,), sent = _carrier_call(
        "dil_bwd", body, (3, DIL_NLW, DIL_BLOCKS), _dil_in_specs() + [rspec, rspec, rspec],
        [pl.BlockSpec((None, 3, SEQ, DIL_LW), lambda g, hp, b: (g, 0, 0, hp))], [jax.ShapeDtypeStruct((3, 3, SEQ, DM), BF16)],
        [pltpu.VMEM((SEQ, DIL_LW), F32), pltpu.VMEM((SEQ, DIL_LW), F32)], (slopes, qkv, qkv, qkv, do, dd, lse), carry)
    return dqkv, sent


def _ffn_fwd(name, x, g_pre, g_post, wgt4, wut4, wd4, carry):
    tm = 512

    def body(x_ref, gpre_ref, gpost_ref, wg_ref, wu_ref, wd_ref, xn_ref, h_ref, gate_ref, up_ref, u_ref, acc):
        s = pl.program_id(1)

        @pl.when(s == 0)
        def _():
            x = x_ref[...]
            r = lax.rsqrt(jnp.mean(x * x, axis=-1, keepdims=True) + RMS_EPS)
            h_ref[...] = (x * r * gpre_ref[...]).astype(h_ref.dtype)

        h = h_ref[...]
        gate = lax.dot_general(h, wg_ref[...], _NT, preferred_element_type=F32).astype(BF16)
        up = lax.dot_general(h, wu_ref[...], _NT, preferred_element_type=F32).astype(BF16)
        gate_ref[...] = gate
        up_ref[...] = up
        gf = gate.astype(F32)
        act = (gf * jax.nn.sigmoid(gf) * up.astype(F32)).astype(BF16)
        part = jnp.dot(act, wd_ref[...], preferred_element_type=F32)

        @pl.when(s == 0)
        def _():
            acc[...] = part

        @pl.when(s > 0)
        def _():
            acc[...] += part

        @pl.when(s == NCHIP - 1)
        def _():
            u = acc[...]
            u_ref[...] = u
            r = lax.rsqrt(jnp.mean(u * u, axis=-1, keepdims=True) + RMS_EPS)
            xn_ref[...] = x_ref[...] + u * r * gpost_ref[...]

    rows = pl.BlockSpec((tm, DM), lambda i, s: (i, 0))
    vec = pl.BlockSpec((1, DM), lambda i, s: (0, 0))
    wspec = _ffn_wspec(lambda i, s: (s, 0, 0))
    mid = pl.BlockSpec((None, tm, FSH), lambda i, s: (s, i, 0))
    outs, sent = _carrier_call(
        name, body, (SEQ // tm, NCHIP), [rows, vec, vec, wspec, wspec, wspec], [rows, rows, mid, mid, rows],
        [jax.ShapeDtypeStruct((SEQ, DM), F32), jax.ShapeDtypeStruct((SEQ, DM), BF16), jax.ShapeDtypeStruct((NCHIP, SEQ, FSH), BF16),
         jax.ShapeDtypeStruct((NCHIP, SEQ, FSH), BF16), jax.ShapeDtypeStruct((SEQ, DM), F32)],
        [pltpu.VMEM((tm, DM), F32)], (x, g_pre, g_post, wgt4, wut4, wd4), carry)
    return outs, sent


def _ffn_block(layer, x, g_pre, g_post, ex):
    tag = f"l{layer}_ffn_fwd"
    (x_new, h, gate, up, u), sent = _ffn_fwd(tag, x, g_pre, g_post, ex.weight(("ffn_w_gate", layer)), ex.weight(("ffn_w_up", layer)),
                                             ex.weight(("ffn_w_down", layer)), ex.carry(tag))
    ex.carried(tag, sent)
    return x_new, (x, h, gate, up, u)


def _ffn_bwd(name, dx, x, gate, up, u, g_pre, g_post, wgt4, wut4, wd4, carry):
    tm = 512

    def body(dx_ref, x_ref, gate_ref, up_ref, u_ref, gpre_ref, gpost_ref, wg_ref, wu_ref, wd_ref,
             dxin_ref, du_ref, dgate_ref, dup_ref, act_ref, dgpre_ref, dgpost_ref, dh_acc):
        i, s = pl.program_id(0), pl.program_id(1)

        @pl.when((i == 0) & (s == 0))
        def _():
            dgpre_ref[...] = jnp.zeros_like(dgpre_ref)
            dgpost_ref[...] = jnp.zeros_like(dgpost_ref)

        @pl.when(s == 0)
        def _():
            dy = dx_ref[...]
            uu = u_ref[...]
            r = lax.rsqrt(jnp.mean(uu * uu, axis=-1, keepdims=True) + RMS_EPS)
            yh = uu * r
            t = dy * gpost_ref[...]
            du_ref[...] = (r * (t - yh * jnp.mean(t * yh, axis=-1, keepdims=True))).astype(du_ref.dtype)
            dgpost_ref[...] += jnp.sum(dy * yh, axis=0, keepdims=True)

        dact = lax.dot_general(du_ref[...], wd_ref[...], _NT, preferred_element_type=F32)
        g = gate_ref[...].astype(F32)
        upv = up_ref[...].astype(F32)
        sg = jax.nn.sigmoid(g)
        dgate = (dact * upv * sg * (1.0 + g * (1.0 - sg))).astype(BF16)
        dup = (dact * g * sg).astype(BF16)
        dgate_ref[...] = dgate
        dup_ref[...] = dup
        act_ref[...] = (g * sg * upv).astype(act_ref.dtype)
        part = jnp.dot(dgate, wg_ref[...], preferred_element_type=F32) + jnp.dot(dup, wu_ref[...], preferred_element_type=F32)

        @pl.when(s == 0)
        def _():
            dh_acc[...] = part

        @pl.when(s > 0)
        def _():
            dh_acc[...] += part

        @pl.when(s == NCHIP - 1)
        def _():
            dh = dh_acc[...]
            xx = x_ref[...]
            r = lax.rsqrt(jnp.mean(xx * xx, axis=-1, keepdims=True) + RMS_EPS)
            yh = xx * r
            t = dh * gpre_ref[...]
            dxin_ref[...] = dx_ref[...] + r * (t - yh * jnp.mean(t * yh, axis=-1, keepdims=True))
            dgpre_ref[...] += jnp.sum(dh * yh, axis=0, keepdims=True)

    rows = pl.BlockSpec((tm, DM), lambda i, s: (i, 0))
    vec = pl.BlockSpec((1, DM), lambda i, s: (0, 0))
    wspec = _ffn_wspec(lambda i, s: (s, 0, 0))
    mid = pl.BlockSpec((None, tm, FSH), lambda i, s: (s, i, 0))
    mid_shape = jax.ShapeDtypeStruct((NCHIP, SEQ, FSH), BF16)
    return _carrier_call(
        name, body, (SEQ // tm, NCHIP), [rows, rows, mid, mid, rows, vec, vec, wspec, wspec, wspec], [rows, rows, mid, mid, mid, vec, vec],
        [jax.ShapeDtypeStruct((SEQ, DM), F32), jax.ShapeDtypeStruct((SEQ, DM), BF16), mid_shape, mid_shape, mid_shape,
         jax.ShapeDtypeStruct((1, DM), F32), jax.ShapeDtypeStruct((1, DM), F32)],
        [pltpu.VMEM((tm, DM), F32)], (dx, x, gate, up, u, g_pre, g_post, wgt4, wut4, wd4), carry)


def _ffn_block_bwd(layer, dx, saved, g_pre, g_post, ex):
    tag = f"l{layer}"
    x, h, gate, up, u = saved
    (dx_in, du, dgate, dup, act, dg_pre, dg_post), sent = _ffn_bwd(
        f"{tag}_ffn_bwd", dx, x, gate, up, u, g_pre, g_post, ex.weight(("ffn_w_gate", layer)), ex.weight(("ffn_w_up", layer)),
        ex.weight(("ffn_w_down", layer)), ex.carry(f"{tag}_ffn_bwd"))
    ex.carried(f"{tag}_ffn_bwd", sent)
    d_wd = _ffn_bwd_dw(f"{tag}_dwd", act, du)
    d_wg = _ffn_bwd_dw(f"{tag}_dwg", dgate, h)
    d_wu = _ffn_bwd_dw(f"{tag}_dwu", dup, h)
    ex.grads(f"{tag}_ffn", {("ffn_w_gate", layer): d_wg, ("ffn_w_up", layer): d_wu, ("ffn_w_down", layer): d_wd})
    return dx_in, dg_pre, dg_post


def _alibi_slopes():
    return 2.0 ** (-8.0 * jnp.arange(1, NH + 1, dtype=F32) / NH)


def _local_step(x, target, norms, rpb, ex):
    g_mix_pre, g_mix_post, g_ffn_pre, g_ffn_post = norms
    row = lambda a, i: a[i:i + 1]

    bias = _na_bias_tiles(rpb)
    h0, h0t = _rms_fwd_both("l0_mix_pre", x, row(g_mix_pre, 0))
    qkv0 = _qkv_fwd("l0_qkv", h0[None], ex.weight(("na_w_qkv", 0)))
    o0, sent = _na_fwd(qkv0[0], bias, ex.carry("na_fwd"))
    ex.carried("na_fwd", sent)
    na_wo = ex.weight(("na_w_o", 0)).reshape(DM, DM)
    u0 = _proj_fwd("l0_proj", o0, na_wo)
    x1 = _resid_norm("l0_mix_post", x, u0, row(g_mix_post, 0))
    x2, ffn0 = _ffn_block(0, x1, row(g_ffn_pre, 0), row(g_ffn_post, 0), ex)

    slopes = _alibi_slopes()
    h2g, h2gt = _to_groups("l1_h_groups", _rms_fwd("l1_mix_pre", x2, row(g_mix_pre, 1), F32))
    dil_wqkv = ex.weight(("dil_w_qkv", 0))
    qkv1 = _qkv_fwd("l1_qkv", h2g, dil_wqkv)
    og, lg, sent = _dil_fwd(qkv1, slopes, ex.carry("dil_fwd"))
    ex.carried("dil_fwd", sent)
    o1, lse = _dil_merge(og, lg)
    dil_wo = ex.weight(("dil_w_o", 0)).reshape(DM, DM)
    u1 = _proj_fwd("l1_proj", o1, dil_wo)
    x3 = _resid_norm("l1_mix_post", x2, u1, row(g_mix_post, 1))
    x4, ffn1 = _ffn_block(1, x3, row(g_ffn_pre, 1), row(g_ffn_post, 1), ex)

    dx4, loss_row = _loss_grad("loss", x4, target)

    dx3, dg_fpre1, dg_fpost1 = _ffn_block_bwd(1, dx4, ffn1, row(g_ffn_pre, 1), row(g_ffn_post, 1), ex)
    du1, dg_mpost1 = _norm_bwd("l1_mix_post_bwd", [dx3], u1, row(g_mix_post, 1))
    d_dil_wo = _proj_bwd_dw("l1_dwo", o1, du1)
    do1 = _proj_bwd_do("l1_do", du1, dil_wo, F32)
    dog, ddg, lseg = _dil_bwd_prep(do1, o1, lse)
    dqkv1, sent = _dil_bwd(qkv1, dog, ddg, lseg, slopes, ex.carry("dil_bwd"))
    ex.carried("dil_bwd", sent)
    d_dil_wqkv = _qkv_bwd_dw("l1_dwqkv", h2gt, dqkv1, dil_wqkv.shape[2])
    ex.grads("l1_mix", {("dil_w_qkv", 0): d_dil_wqkv, ("dil_w_o", 0): d_dil_wo.reshape(NCHIP, DM // NCHIP, DM)})
    dh2 = _from_groups_sum("l1_dh_tokens", _qkv_bwd_dh("l1_dh", dqkv1, dil_wqkv))
    dx2, dg_mpre1 = _norm_bwd("l1_mix_pre_bwd", [dh2], x2, row(g_mix_pre, 1), res=dx3)

    dx1, dg_fpre0, dg_fpost0 = _ffn_block_bwd(0, dx2, ffn0, row(g_ffn_pre, 0), row(g_ffn_post, 0), ex)
    du0, dg_mpost0 = _norm_bwd("l0_mix_post_bwd", [dx1], u0, row(g_mix_post, 0))
    d_na_wo = _proj_bwd_dw("l0_dwo", o0, du0)
    do0 = _proj_bwd_do("l0_do", du0, na_wo)
    dqkv0, z, sent = _na_bwd(qkv0[0], bias, do0, ex.carry("na_bwd"))
    ex.carried("na_bwd", sent)
    d_rpb = _rpb_grad(z)
    na_wqkv = ex.weight(("na_w_qkv", 0))
    d_na_wqkv = _qkv_bwd_dw("l0_dwqkv", h0t[None], dqkv0[None], na_wqkv.shape[2])
    ex.grads("l0_mix", {("na_w_qkv", 0): d_na_wqkv, ("na_w_o", 0): d_na_wo.reshape(NCHIP, DM // NCHIP, DM)})
    dh0 = _qkv_bwd_dh("l0_dh", dqkv0[None], na_wqkv)
    dx0, dg_mpre0 = _norm_bwd("l0_mix_pre_bwd", [dh0[0]], x, row(g_mix_pre, 0), res=dx1)

    dnorms = (jnp.concatenate([dg_mpre0, dg_mpre1]), jnp.concatenate([dg_mpost0, dg_mpost1]),
              jnp.concatenate([dg_fpre0, dg_fpre1]), jnp.concatenate([dg_fpost0, dg_fpost1]))
    return loss_row, dx0, dnorms, d_rpb


def _place():
    x, y, c = lax.axis_index("x"), lax.axis_index("y"), lax.axis_index("c")
    chips = ((1 - x, y), (x, 1 - y), (1 - x, 1 - y))
    return x, y, c, chips


def _chip_id(chip):
    return 2 * chip[0] + chip[1]


def _comm_call(name, body, ins, out_shapes, n_sems, aliases=None):
    return pl.pallas_call(
        body, in_specs=[HBM_SPEC] * len(ins), out_specs=[HBM_SPEC] * len(out_shapes), out_shape=out_shapes,
        scratch_shapes=[pltpu.SemaphoreType.DMA((k,)) for k in n_sems], input_output_aliases=aliases or {},
        compiler_params=pltpu.CompilerParams(has_side_effects=True), name=name)(*ins)


def _gather_copies(shards):
    n = len(shards)

    def copies(src, out, sems):
        send_sems, recv_sems = sems
        x, y, c, chips = _place()

        def copy(t, k, chip, half, to, from_src=False):
            blk = out[t].at[_chip_id(chip), half]
            return pltpu.make_async_remote_copy(
                src_ref=src[t].at[half] if from_src else blk, dst_ref=blk,
                send_sem=send_sems.at[6 * t + k], recv_sem=recv_sems.at[6 * t + k], device_id=to, device_id_type=MESH)

        return copy, x, y, c, chips

    def issue(src, out, sems):
        copy, x, y, c, chips = copies(src, out, sems)
        for t in range(n):
            for j, chip in enumerate(chips):
                copy(t, j, (x, y), c, (*chip, c), from_src=True).start()

    def drain(src, out, sems):
        copy, x, y, c, chips = copies(src, out, sems)
        passed = []
        for t in range(n):
            for j, chip in enumerate(chips):
                copy(t, j, chip, c, (x, y, c)).wait_recv()
                fwd = copy(t, 3 + j, chip, c, (x, y, 1 - c))
                fwd.start()
                passed.append(fwd)
        for t in range(n):
            for j, chip in enumerate(chips):
                copy(t, 3 + j, chip, 1 - c, (x, y, c)).wait_recv()
        for t in range(n):
            for j, chip in enumerate(chips):
                copy(t, j, (x, y), c, (*chip, c), from_src=True).wait_send()
        for cp in passed:
            cp.wait_send()

    return _Carried(shards, [jax.ShapeDtypeStruct((NCHIP,) + s.shape, s.dtype) for s in shards], (6 * n, 6 * n), issue, drain)


def _pair_exchange(name, grads):
    n = len(grads)

    def body(*refs):
        g, theirs = refs[:n], refs[n:2 * n]
        send_sems, recv_sems = refs[2 * n:]
        x, y, c, _ = _place()
        swap = [pltpu.make_async_remote_copy(src_ref=g[t].at[:, 1 - c], dst_ref=theirs[t], send_sem=send_sems.at[t],
                                             recv_sem=recv_sems.at[t], device_id=(x, y, 1 - c), device_id_type=MESH) for t in range(n)]
        for cp in swap:
            cp.start()
        for cp in swap:
            cp.wait()

    return _comm_call(name, body, grads, [jax.ShapeDtypeStruct((NCHIP,) + g.shape[2:], g.dtype) for g in grads], (n, n))


def _chip_exchange_copies(parts):
    n = len(parts)

    def copies(p, slots, sems):
        send_sems, recv_sems = sems
        x, y, c, chips = _place()
        return [pltpu.make_async_remote_copy(src_ref=p[t].at[_chip_id(chips[j])], dst_ref=slots[t].at[j], send_sem=send_sems.at[3 * t + j],
                                             recv_sem=recv_sems.at[3 * t + j], device_id=(*chips[j], c), device_id_type=MESH)
                for t in range(n) for j in range(3)]

    def issue(p, slots, sems):
        for cp in copies(p, slots, sems):
            cp.start()

    def drain(p, slots, sems):
        for cp in copies(p, slots, sems):
            cp.wait()

    return _Carried(parts, [jax.ShapeDtypeStruct((3,) + p.shape[1:], p.dtype) for p in parts], (3 * n, 3 * n), issue, drain)


def _pair_share(full):
    n = len(full)

    def body(*refs):
        buf = refs[n:2 * n]
        send_sems, recv_sems = refs[2 * n:]
        x, y, c, _ = _place()
        sends = [pltpu.make_async_remote_copy(src_ref=buf[t].at[c], dst_ref=buf[t].at[c], send_sem=send_sems.at[t], recv_sem=recv_sems.at[t],
                                              device_id=(x, y, 1 - c), device_id_type=MESH) for t in range(n)]
        for cp in sends:
            cp.start()
        for t in range(n):
            pltpu.make_async_remote_copy(src_ref=buf[t].at[c], dst_ref=buf[t].at[1 - c], send_sem=send_sems.at[t], recv_sem=recv_sems.at[t],
                                         device_id=(x, y, 1 - c), device_id_type=MESH).wait_recv()
        for cp in sends:
            cp.wait_send()

    return _comm_call("grad_pair_share", body, full, [jax.ShapeDtypeStruct(f.shape, f.dtype) for f in full], (n, n),
                      aliases={t: t for t in range(n)})


SMALL_ROWS = 128


def _allreduce_small(v):
    def body(v_ref, o_ref, buf, send_sems, recv_sems):
        x, y, c, _ = _place()
        me = 4 * x + 2 * y + c
        flip = lambda a, f: 1 - a if f else a
        buf[me] = v_ref[...]
        peers = [(flip(x, d >> 2 & 1), flip(y, d >> 1 & 1), flip(c, d & 1)) for d in range(1, 8)]
        sends = [pltpu.make_async_remote_copy(src_ref=v_ref, dst_ref=buf.at[me], send_sem=send_sems.at[i], recv_sem=recv_sems.at[i],
                                              device_id=peer, device_id_type=MESH) for i, peer in enumerate(peers)]
        for cp in sends:
            cp.start()
        for i, (px, py, pc) in enumerate(peers):
            pltpu.make_async_remote_copy(src_ref=v_ref, dst_ref=buf.at[4 * px + 2 * py + pc], send_sem=send_sems.at[i], recv_sem=recv_sems.at[i],
                                         device_id=(px, py, pc), device_id_type=MESH).wait_recv()
        for cp in sends:
            cp.wait_send()
        acc = buf[0]
        for k in range(1, 8):
            acc = acc + buf[k]
        o_ref[...] = acc

    vm = pl.BlockSpec(memory_space=pltpu.VMEM)
    return pl.pallas_call(
        body, in_specs=[vm], out_specs=vm, out_shape=jax.ShapeDtypeStruct((SMALL_ROWS, 128), F32),
        scratch_shapes=[pltpu.VMEM((8, SMALL_ROWS, 128), F32), pltpu.SemaphoreType.DMA((7,)), pltpu.SemaphoreType.DMA((7,))],
        compiler_params=pltpu.CompilerParams(has_side_effects=True), name="allreduce_small")(v)


def _row_block(rows, cols, budget=1 << 20):
    best = 8
    for bm in range(8, rows + 1, 8):
        if rows % bm == 0 and bm * cols * 4 <= budget:
            best = bm
    return best


def _pair_sum(name, place, g, theirs):
    _, m, c = theirs.shape
    bm = _row_block(m, c)

    def body(place_ref, a_ref, b_ref, o_ref):
        o_ref[...] = (a_ref[...].astype(F32) + b_ref[...].astype(F32)).astype(o_ref.dtype)

    spec = pl.BlockSpec((None, bm, c), lambda k, i, pr: (k, i, 0))
    return pl.pallas_call(
        body, out_shape=jax.ShapeDtypeStruct(theirs.shape, BF16),
        grid_spec=pltpu.PrefetchScalarGridSpec(
            num_scalar_prefetch=1, grid=(NCHIP, m // bm),
            in_specs=[pl.BlockSpec((None, None, bm, c), lambda k, i, pr: (k, pr[0], i, 0)), spec], out_specs=spec),
        compiler_params=_params(("parallel", "parallel")), name=name)(place, g, theirs)


def _chip_sum(name, place, parts, slots):
    _, m, c = parts.shape
    bm = _row_block(m, c)

    def body(place_ref, p_ref, s_ref, o_ref):
        s = s_ref[...].astype(F32)
        o_ref[...] = ((p_ref[...].astype(F32) + s[0]) + s[1]) + s[2]

    return pl.pallas_call(
        body, out_shape=jax.ShapeDtypeStruct((2, m, c), F32),
        grid_spec=pltpu.PrefetchScalarGridSpec(
            num_scalar_prefetch=1, grid=(m // bm,),
            in_specs=[pl.BlockSpec((None, bm, c), lambda i, pr: (pr[1], i, 0)), pl.BlockSpec((3, bm, c), lambda i, pr: (0, i, 0))],
            out_specs=pl.BlockSpec((None, bm, c), lambda i, pr: (pr[0], i, 0))),
        compiler_params=_params(("parallel",)), name=name)(place, parts, slots)


def _adamw(name, w, g, m, v, layer=0, into=None):
    lead, rows, cols = w.shape
    bm = _row_block(rows, cols, budget=768 * 1024)
    c1 = 1.0 - ADAM_B1 ** ADAM_STEP
    c2 = 1.0 - ADAM_B2 ** ADAM_STEP

    def body(w_ref, g_ref, m_ref, v_ref, *rest):
        go_ref, d_ref, mo_ref, vo_ref = rest[-4:]
        g = g_ref[...]
        mn = ADAM_B1 * m_ref[...] + (1.0 - ADAM_B1) * g
        vn = ADAM_B2 * v_ref[...] + (1.0 - ADAM_B2) * (g * g)
        go_ref[...] = g
        mo_ref[...] = mn
        vo_ref[...] = vn
        d_ref[...] = -ADAM_LR * ((mn / c1) / (jnp.sqrt(vn / c2) + ADAM_EPS) + ADAM_WD * w_ref[...])

    spec = pl.BlockSpec((None, bm, cols), lambda i: (layer, i, 0))
    sh = jax.ShapeDtypeStruct((lead, rows, cols), F32)
    prev = [] if into is None else list(into)
    return pl.pallas_call(
        body, grid=(rows // bm,), in_specs=[spec, pl.BlockSpec((bm, cols), lambda i: (i, 0)), spec, spec] + [pl.BlockSpec(memory_space=pl.ANY)] * len(prev),
        out_specs=[spec] * 4, out_shape=[sh] * 4, input_output_aliases={4 + k: k for k in range(len(prev))},
        compiler_params=_params(("parallel",)), name=name)(w, g, m, v, *prev)


def _pack_small(norms, rpb):
    flat = jnp.concatenate([a.reshape(-1) for a in norms] + [rpb.reshape(-1)])
    return jnp.pad(flat, (0, SMALL_ROWS * 128 - flat.shape[0])).reshape(SMALL_ROWS, 128)


def _unpack_small(p):
    flat = p.reshape(-1)
    norms = [flat[i * 2 * DM:(i + 1) * 2 * DM].reshape(2, DM) for i in range(4)]
    rpb = flat[8 * DM:8 * DM + NH * 15 * 31].reshape(1, NH, 15, 31)
    return norms, rpb


FFN_NAMES = ("ffn_w_gate", "ffn_w_up", "ffn_w_down")
L0_FFN = tuple((n, 0) for n in FFN_NAMES)
L1_FFN = tuple((n, 1) for n in FFN_NAMES)
NA_KEYS = (("na_w_qkv", 0), ("na_w_o", 0))
DIL_KEYS = (("dil_w_qkv", 0), ("dil_w_o", 0))


class _Exchange:
    GATHERS = {"na_fwd": L0_FFN, "l0_ffn_fwd": DIL_KEYS, "dil_fwd": L1_FFN}
    EXCHANGES = {"dil_bwd": L1_FFN, "l0_ffn_bwd": DIL_KEYS, "na_bwd": L0_FFN}

    def __init__(self, shards):
        self.chip = 2 * lax.axis_index("x") + lax.axis_index("y")
        self.place = jnp.stack([lax.axis_index("c"), self.chip]).astype(jnp.int32)
        self.own = {k: s.reshape(2, s.shape[0] // 2, s.shape[1]).astype(BF16) for k, s in shards.items()}
        self.gathered, self.parts, self.full = {}, {}, {}
        self._take(NA_KEYS, _run_carried("gather_first", _gather_copies([self.own[k] for k in NA_KEYS])))

    def _take(self, keys, landed):
        for k, gw in zip(keys, landed):
            self.gathered[k] = lax.dynamic_update_slice(gw, self.own[k][None], (self.chip, 0, 0, 0))

    def _sum(self, keys, slots):
        for k, s in zip(keys, slots):
            self.full[k] = _chip_sum(f"chip_sum_{k[0]}_{k[1]}", self.place, self.parts[k], s)

    def weight(self, key):
        g = self.gathered[key]
        return g.reshape(NCHIP, 2 * g.shape[2], g.shape[3])

    def carry(self, tag):
        if tag in self.GATHERS:
            return _gather_copies([self.own[k] for k in self.GATHERS[tag]])
        if tag in self.EXCHANGES:
            return _chip_exchange_copies([self.parts[k] for k in self.EXCHANGES[tag]])
        return None

    def carried(self, tag, landed):
        if tag in self.GATHERS:
            self._take(self.GATHERS[tag], landed)
        elif tag in self.EXCHANGES:
            self._sum(self.EXCHANGES[tag], landed)

    def grads(self, tag, dw):
        keys = tuple(dw)
        mine = [dw[k].reshape(NCHIP, 2, -1, dw[k].shape[-1]) for k in keys]
        theirs = _pair_exchange(f"grad_pair_exchange_{tag}", mine)
        for k, a, b in zip(keys, mine, theirs):
            self.parts[k] = _pair_sum(f"pair_sum_{k[0]}_{k[1]}", self.place, a, b)
        if tag == "l0_mix":
            self._sum(keys, _run_carried("grad_chip_exchange_last", _chip_exchange_copies([self.parts[k] for k in keys])))

    def finish(self):
        keys = tuple(self.full)
        shared = _pair_share([self.full[k] for k in keys])
        return {k: s.reshape(2 * s.shape[1], s.shape[2]) for k, s in zip(keys, shared)}


def kernel(x, norm_mix_pre, norm_mix_post, norm_ffn_pre, norm_ffn_post, na_w_qkv, na_w_o, na_rpb, dil_w_qkv, dil_w_o, ffn_w_gate, ffn_w_up, ffn_w_down, loss_target, m_norm_mix_pre, m_norm_mix_post, m_norm_ffn_pre, m_norm_ffn_post, m_na_w_qkv, m_na_w_o, m_na_rpb, m_dil_w_qkv, m_dil_w_o, m_ffn_w_gate, m_ffn_w_up, m_ffn_w_down, v_norm_mix_pre, v_norm_mix_post, v_norm_ffn_pre, v_norm_ffn_post, v_na_w_qkv, v_na_w_o, v_na_rpb, v_dil_w_qkv, v_dil_w_o, v_ffn_w_gate, v_ffn_w_up, v_ffn_w_down):
    tr = lambda a: jnp.swapaxes(a, 1, 2)
    weights = {"na_w_qkv": na_w_qkv, "na_w_o": na_w_o, "dil_w_qkv": dil_w_qkv, "dil_w_o": dil_w_o,
               "ffn_w_gate": tr(ffn_w_gate), "ffn_w_up": tr(ffn_w_up), "ffn_w_down": ffn_w_down}
    m_in = {"na_w_qkv": m_na_w_qkv, "na_w_o": m_na_w_o, "dil_w_qkv": m_dil_w_qkv, "dil_w_o": m_dil_w_o,
            "ffn_w_gate": tr(m_ffn_w_gate), "ffn_w_up": tr(m_ffn_w_up), "ffn_w_down": m_ffn_w_down}
    v_in = {"na_w_qkv": v_na_w_qkv, "na_w_o": v_na_w_o, "dil_w_qkv": v_dil_w_qkv, "dil_w_o": v_dil_w_o,
            "ffn_w_gate": tr(v_ffn_w_gate), "ffn_w_up": tr(v_ffn_w_up), "ffn_w_down": v_ffn_w_down}

    ex = _Exchange({(n, l): weights[n][l] for n in weights for l in range(weights[n].shape[0])})
    norms = (norm_mix_pre, norm_mix_post, norm_ffn_pre, norm_ffn_post)
    loss_row, dx, dnorms, d_rpb = _local_step(x[0], loss_target[0], norms, na_rpb[0], ex)
    loss = lax.psum(loss_row[0, 0], ("x", "y", "c"))
    full = ex.finish()
    small = _allreduce_small(_pack_small(dnorms, d_rpb))

    out_g, out_d, out_m, out_v = {}, {}, {}, {}
    for n in weights:
        res = None
        for l in range(weights[n].shape[0]):
            res = _adamw(f"adamw_{n}_{l}", weights[n], full[(n, l)], m_in[n], v_in[n], l, res)
        if n in ("ffn_w_gate", "ffn_w_up"):
            res = [tr(r) for r in res]
        out_g[n], out_d[n], out_m[n], out_v[n] = res
    sm_names = ("norm_mix_pre", "norm_mix_post", "norm_ffn_pre", "norm_ffn_post", "na_rpb")
    sm = _adamw("adamw_small", _pack_small(norms, na_rpb)[None], small,
                _pack_small((m_norm_mix_pre, m_norm_mix_post, m_norm_ffn_pre, m_norm_ffn_post), m_na_rpb)[None],
                _pack_small((v_norm_mix_pre, v_norm_mix_post, v_norm_ffn_pre, v_norm_ffn_post), v_na_rpb)[None])
    for res, dst in zip(sm, (out_g, out_d, out_m, out_v)):
        ns, rp = _unpack_small(res)
        for n, a in zip(sm_names, ns + [rp]):
            dst[n] = a

    order = ("norm_mix_pre", "norm_mix_post", "norm_ffn_pre", "norm_ffn_post", "na_w_qkv", "na_w_o", "na_rpb", "dil_w_qkv", "dil_w_o",
             "ffn_w_gate", "ffn_w_up", "ffn_w_down")
    return (loss, dx[None], *[out_g[n] for n in order], *[out_d[n] for n in order], *[out_m[n] for n in order], *[out_v[n] for n in order])
```

```python
import functools

import numpy as np
import jax
import jax.numpy as jnp
from jax import lax
from jax.experimental import pallas as pl
from jax.experimental.pallas import tpu as pltpu

F32 = jnp.float32
BF16 = jnp.bfloat16

SEQ = 2048
DM = 1024
NH = 16
HD = 64
DFF = 2816
NCHIP = 4
FSH = DFF // NCHIP
GRID_W = 64
NA_QROWS = 4
NA_QB = NA_QROWS * GRID_W
NA_WROWS = 12
NA_WIN = NA_WROWS * GRID_W
DIL = (1, 4, 16)
DIL_QB = 256
DIL_WIN = DIL_QB + 128
DIL_RADIUS = 64
RMS_EPS = 1e-6
NEG = -1e30
QSCALE = HD ** -0.5
CH = 256
MESH = pl.DeviceIdType.MESH

ADAM_LR, ADAM_B1, ADAM_B2, ADAM_EPS, ADAM_WD, ADAM_STEP = 0.001, 0.9, 0.999, 1e-08, 0.01, 10

VMEM_LIMIT = 56 * 1024 * 1024

_NN = (((1,), (0,)), ((), ()))
_NT = (((1,), (1,)), ((), ()))
_TN = (((0,), (0,)), ((), ()))


def _params(sem):
    return pltpu.CompilerParams(dimension_semantics=sem, vmem_limit_bytes=VMEM_LIMIT)


def _matmul(name, pairs, grid, out_shape, out_spec, acc_shape, carrying=False, carry=None):
    nk = grid[-1]
    npair = len(pairs)
    n_in = 2 * npair

    def body(*refs):
        ins, o_ref = refs[:2 * npair], refs[n_in]
        part = None
        for p in range(npair):
            d = lax.dot_general(ins[2 * p][...].astype(BF16), ins[2 * p + 1][...].astype(BF16), pairs[p][4],
                                preferred_element_type=F32)
            part = d if part is None else part + d
        if nk == 1:
            o_ref[...] = part.astype(o_ref.dtype)
        else:
            acc_ref = refs[n_in + 1]
            kk = pl.program_id(len(grid) - 1)

            @pl.when(kk == 0)
            def _():
                acc_ref[...] = part

            @pl.when(kk > 0)
            def _():
                acc_ref[...] += part

            @pl.when(kk == nk - 1)
            def _():
                o_ref[...] = acc_ref[...].astype(o_ref.dtype)

    ops, specs = [], []
    for a, a_spec, b, b_spec, _ in pairs:
        ops += [a, b]
        specs += [a_spec, b_spec]
    (out,), sent = _carrier_call(name, body, grid, specs, [out_spec], [out_shape], [] if nk == 1 else [pltpu.VMEM(acc_shape, F32)], ops, carry)
    return (out, sent) if carrying else out


def _qkv_fwd(name, h_all, w4):
    g_n = h_all.shape[0]
    per = w4.shape[2] // CH
    return _matmul(
        name, [(h_all, pl.BlockSpec((None, SEQ, DM), lambda g, q, k: (g, 0, 0)),
                w4, pl.BlockSpec((None, DM, CH), lambda g, q, k: ((g * 12 + q) // per, 0, (g * 12 + q) % per)), _NN)],
        (g_n, 12, 1), jax.ShapeDtypeStruct((g_n, SEQ, 3 * DM), BF16),
        pl.BlockSpec((None, SEQ, CH), lambda g, q, k: (g, 0, q)), None)


def _qkv_bwd_dh(name, dqkv, w4, carry):
    g_n = dqkv.shape[0]
    per = w4.shape[2] // CH
    tm = 1024

    def pair(cb):
        chunk = lambda g, t: g * 12 + t * 4 + cb
        return (dqkv, pl.BlockSpec((None, None, tm, CH), lambda g, i, t: (g, t, i, cb)),
                w4, pl.BlockSpec((None, DM, CH), lambda g, i, t: (chunk(g, t) // per, 0, chunk(g, t) % per)), _NT)

    return _matmul(name, [pair(cb) for cb in range(4)], (g_n, SEQ // tm, 3), jax.ShapeDtypeStruct((g_n, SEQ, DM), F32),
                   pl.BlockSpec((None, tm, DM), lambda g, i, t: (g, i, 0)), (tm, DM), carrying=True, carry=carry)


def _qkv_bwd_dw(name, ht_all, dqkv, shard_cols):
    g_n = dqkv.shape[0]
    per = shard_cols // CH
    return _matmul(
        name, [(ht_all, pl.BlockSpec((None, DM, SEQ), lambda qq, k: (qq // 12, 0, 0)),
                dqkv, pl.BlockSpec((None, None, SEQ, CH), lambda qq, k: (qq // 12, (qq % 12) // 4, 0, qq % 4)), _NN)],
        (g_n * 12, 1), jax.ShapeDtypeStruct((NCHIP, DM, shard_cols), BF16),
        pl.BlockSpec((None, DM, CH), lambda qq, k: (qq // per, 0, qq % per)), None)


def _proj_fwd(name, o, wo, x, g):
    tm = 512

    def body(o_ref, w_ref, x_ref, g_ref, xn_ref, u_ref):
        u = jnp.dot(o_ref[...], w_ref[...], preferred_element_type=F32)
        u_ref[...] = u
        r = lax.rsqrt(jnp.mean(u * u, axis=-1, keepdims=True) + RMS_EPS)
        xn_ref[...] = x_ref[...] + u * r * g_ref[...]

    rows = pl.BlockSpec((tm, DM), lambda i: (i, 0))
    sh = jax.ShapeDtypeStruct((SEQ, DM), F32)
    return pl.pallas_call(
        body, grid=(SEQ // tm,), in_specs=[rows, pl.BlockSpec((DM, DM), lambda i: (0, 0)), rows, pl.BlockSpec((1, DM), lambda i: (0, 0))],
        out_specs=[rows, rows], out_shape=[sh, sh], compiler_params=_params(("parallel",)), name=name)(o, wo, x, g)


def _proj_bwd(name, dy, u, g, wo, dtype):
    tm = 512

    def body(dy_ref, u_ref, g_ref, w_ref, do_ref, du_ref, dg_ref):
        dy = dy_ref[...]
        u = u_ref[...]
        r = lax.rsqrt(jnp.mean(u * u, axis=-1, keepdims=True) + RMS_EPS)
        yh = u * r
        t = dy * g_ref[...]
        du = (r * (t - yh * jnp.mean(t * yh, axis=-1, keepdims=True))).astype(BF16)
        du_ref[...] = du
        do_ref[...] = lax.dot_general(du, w_ref[...], _NT, preferred_element_type=F32).astype(do_ref.dtype)

        @pl.when(pl.program_id(0) == 0)
        def _():
            dg_ref[...] = jnp.zeros_like(dg_ref)

        dg_ref[...] += jnp.sum(dy * yh, axis=0, keepdims=True)

    rows = pl.BlockSpec((tm, DM), lambda i: (i, 0))
    vec = pl.BlockSpec((1, DM), lambda i: (0, 0))
    return pl.pallas_call(
        body, grid=(SEQ // tm,), in_specs=[rows, rows, vec, pl.BlockSpec((DM, DM), lambda i: (0, 0))], out_specs=[rows, rows, vec],
        out_shape=[jax.ShapeDtypeStruct((SEQ, DM), dtype), jax.ShapeDtypeStruct((SEQ, DM), BF16), jax.ShapeDtypeStruct((1, DM), F32)],
        compiler_params=_params(("arbitrary",)), name=name)(dy, u, g, wo)


def _proj_bwd_dw(name, o, du):
    tk, tn = 512, 512
    return _matmul(
        name, [(o, pl.BlockSpec((tk, DM), lambda j, k: (k, 0)), du, pl.BlockSpec((tk, tn), lambda j, k: (k, j)), _TN)],
        (DM // tn, SEQ // tk), jax.ShapeDtypeStruct((DM, DM), BF16), pl.BlockSpec((DM, tn), lambda j, k: (0, j)), (DM, tn))


def _ffn_wspec(index_map):
    return pl.BlockSpec((None, FSH, DM), index_map)


def _ffn_bwd_dw(name, a4, b):
    tk = 512
    return _matmul(
        name, [(a4, pl.BlockSpec((None, tk, FSH), lambda s, k: (s, k, 0)), b, pl.BlockSpec((tk, DM), lambda s, k: (k, 0)), _TN)],
        (NCHIP, SEQ // tk), jax.ShapeDtypeStruct((NCHIP, FSH, DM), BF16), _ffn_wspec(lambda s, k: (s, 0, 0)), (FSH, DM))


ROWS = 256


def _row_spec():
    return pl.BlockSpec((ROWS, DM), lambda i: (i, 0))


def _vec_spec():
    return pl.BlockSpec((1, DM), lambda i: (0, 0))


def _rms_fwd(name, x, g, dtype=BF16):
    def body(x_ref, g_ref, o_ref):
        x = x_ref[...]
        r = lax.rsqrt(jnp.mean(x * x, axis=-1, keepdims=True) + RMS_EPS)
        o_ref[...] = (x * r * g_ref[...]).astype(o_ref.dtype)

    return pl.pallas_call(body, grid=(SEQ // ROWS,), in_specs=[_row_spec(), _vec_spec()], out_specs=_row_spec(),
                          out_shape=jax.ShapeDtypeStruct((SEQ, DM), dtype), compiler_params=_params(("parallel",)), name=name)(x, g)


def _rms_fwd_both(name, x, g):
    def body(x_ref, g_ref, o_ref, t_ref):
        x = x_ref[...]
        r = lax.rsqrt(jnp.mean(x * x, axis=-1, keepdims=True) + RMS_EPS)
        h = x * r * g_ref[...]
        o_ref[...] = h.astype(o_ref.dtype)
        t_ref[...] = h.T.astype(t_ref.dtype)

    return pl.pallas_call(
        body, grid=(SEQ // ROWS,), in_specs=[_row_spec(), _vec_spec()], out_specs=[_row_spec(), pl.BlockSpec((DM, ROWS), lambda i: (0, i))],
        out_shape=[jax.ShapeDtypeStruct((SEQ, DM), BF16), jax.ShapeDtypeStruct((DM, SEQ), BF16)],
        compiler_params=_params(("parallel",)), name=name)(x, g)


def _norm_bwd(name, dys, u, g, res=None):
    ndy = len(dys)

    def body(*refs):
        dy = refs[0][...]
        for r_ in refs[1:ndy]:
            dy = dy + r_[...]
        u_ref, g_ref = refs[ndy], refs[ndy + 1]
        res_ref = refs[ndy + 2] if res is not None else None
        du_ref, dg_ref = refs[-2], refs[-1]
        u = u_ref[...]
        r = lax.rsqrt(jnp.mean(u * u, axis=-1, keepdims=True) + RMS_EPS)
        yh = u * r
        t = dy * g_ref[...]
        du = r * (t - yh * jnp.mean(t * yh, axis=-1, keepdims=True))
        if res_ref is not None:
            du = du + res_ref[...]
        du_ref[...] = du

        @pl.when(pl.program_id(0) == 0)
        def _():
            dg_ref[...] = jnp.zeros_like(dg_ref)

        dg_ref[...] += jnp.sum(dy * yh, axis=0, keepdims=True)

    ops = list(dys) + [u, g] + ([res] if res is not None else [])
    specs = [_row_spec()] * ndy + [_row_spec(), _vec_spec()] + ([_row_spec()] if res is not None else [])
    return pl.pallas_call(
        body, grid=(SEQ // ROWS,), in_specs=specs, out_specs=[_row_spec(), _vec_spec()],
        out_shape=[jax.ShapeDtypeStruct((SEQ, DM), F32), jax.ShapeDtypeStruct((1, DM), F32)],
        compiler_params=_params(("arbitrary",)), name=name)(*ops)


def _loss_grad(name, y, t):
    def body(y_ref, t_ref, dy_ref, l_ref):
        e = y_ref[...] - t_ref[...]
        dy_ref[...] = e * (1.0 / DM)

        @pl.when(pl.program_id(0) == 0)
        def _():
            l_ref[...] = jnp.zeros_like(l_ref)

        l_ref[...] += jnp.sum(e * e) * (0.5 / DM)

    return pl.pallas_call(
        body, grid=(SEQ // ROWS,), in_specs=[_row_spec(), _row_spec()],
        out_specs=[_row_spec(), pl.BlockSpec((1, 128), lambda i: (0, 0))],
        out_shape=[jax.ShapeDtypeStruct((SEQ, DM), F32), jax.ShapeDtypeStruct((1, 128), F32)],
        compiler_params=_params(("arbitrary",)), name=name)(y, t)


HBM_SPEC = pl.BlockSpec(memory_space=pltpu.HBM)


class _Carried:
    def __init__(self, ins, out_shapes, n_sems, issue, drain):
        self.ins, self.out_shapes, self.n_sems, self.issue, self.drain = list(ins), list(out_shapes), tuple(n_sems), issue, drain


def _carrier_call(name, body, grid, in_specs, out_specs, out_shape, scratch_shapes, operands, carry):
    n_in, n_out, n_scr = len(in_specs), len(out_specs), len(scratch_shapes)
    if carry is None:
        res = pl.pallas_call(body, grid=grid, in_specs=in_specs, out_specs=out_specs, out_shape=out_shape, scratch_shapes=scratch_shapes,
                             compiler_params=_params(("arbitrary",) * len(grid)), name=name)(*operands)
        return list(res), []
    ci, co = len(carry.ins), len(carry.out_shapes)

    def wrapped(*refs):
        ins, cins = refs[:n_in], refs[n_in:n_in + ci]
        outs, couts = refs[n_in + ci:n_in + ci + n_out], refs[n_in + ci + n_out:n_in + ci + n_out + co]
        scr, sems = refs[n_in + ci + n_out + co:n_in + ci + n_out + co + n_scr], refs[n_in + ci + n_out + co + n_scr:]
        first = functools.reduce(jnp.logical_and, [pl.program_id(a) == 0 for a in range(len(grid))])
        last = functools.reduce(jnp.logical_and, [pl.program_id(a) == grid[a] - 1 for a in range(len(grid))])

        @pl.when(first)
        def _():
            carry.issue(cins, couts, sems)

        body(*ins, *outs, *scr)

        @pl.when(last)
        def _():
            carry.drain(cins, couts, sems)

    res = pl.pallas_call(
        wrapped, grid=grid, in_specs=list(in_specs) + [HBM_SPEC] * ci, out_specs=list(out_specs) + [HBM_SPEC] * co,
        out_shape=list(out_shape) + carry.out_shapes,
        scratch_shapes=list(scratch_shapes) + [pltpu.SemaphoreType.DMA((k,)) for k in carry.n_sems],
        compiler_params=pltpu.CompilerParams(dimension_semantics=("arbitrary",) * len(grid), vmem_limit_bytes=VMEM_LIMIT, has_side_effects=True),
        name=name)(*operands, *carry.ins)
    return list(res[:n_out]), list(res[n_out:])


def _run_carried(name, carry):
    def body(*refs):
        ci, co = len(carry.ins), len(carry.out_shapes)
        carry.issue(refs[:ci], refs[ci:ci + co], refs[ci + co:])
        carry.drain(refs[:ci], refs[ci:ci + co], refs[ci + co:])

    return pl.pallas_call(
        body, in_specs=[HBM_SPEC] * len(carry.ins), out_specs=[HBM_SPEC] * len(carry.out_shapes), out_shape=carry.out_shapes,
        scratch_shapes=[pltpu.SemaphoreType.DMA((k,)) for k in carry.n_sems],
        compiler_params=pltpu.CompilerParams(has_side_effects=True), name=name)(*carry.ins)


NA_BLOCKS = SEQ // NA_QB
NA_ROWS_TOTAL = SEQ // GRID_W
NA_CLASSES = ((0, 0), (8, 4), (NA_ROWS_TOTAL - NA_QROWS, NA_ROWS_TOTAL - NA_WROWS))


def _na_pairs(i0, ws):
    out = []
    for qi in range(NA_QROWS):
        i = i0 + qi
        rs = min(max(i - 4, 0), NA_ROWS_TOTAL - 8)
        for kr in range(NA_WROWS):
            r = ws + kr
            if rs <= r < rs + 8:
                out.append((qi, kr, r - i + 7))
    return out


def _diag_onehot():
    qc, kc = np.meshgrid(np.arange(GRID_W), np.arange(GRID_W), indexing="ij")
    e = np.zeros((GRID_W * GRID_W, 128), np.float32)
    j = (kc - qc + 15).reshape(-1)
    ok = (j >= 0) & (j <= 30)
    e[np.arange(GRID_W * GRID_W)[ok], j[ok]] = 1.0
    return jnp.asarray(e)


def _rpb_expand(rpb):
    r2 = jnp.pad(rpb.reshape(NH * 15, 31), ((0, 0), (0, 128 - 31)))

    def body(r_ref, e_ref, o_ref):
        o_ref[...] = lax.dot_general(r_ref[...], e_ref[...], _NT, preferred_element_type=F32, precision=lax.Precision.HIGHEST)

    out = pl.pallas_call(body, out_shape=jax.ShapeDtypeStruct((NH * 15, GRID_W * GRID_W), F32), name="rpb_expand",
                         compiler_params=pltpu.CompilerParams(vmem_limit_bytes=VMEM_LIMIT))(r2, _diag_onehot())
    return out.reshape(NH, 15, GRID_W, GRID_W)


def _na_bias_tiles(rpb):
    col = np.arange(GRID_W)
    col_start = np.clip(col - 8, 0, GRID_W - 16)
    col_mask = (col[None, :] >= col_start[:, None]) & (col[None, :] < col_start[:, None] + 16)
    rc = jnp.where(col_mask[None, None], _rpb_expand(rpb), NEG)
    neg = jnp.full((NH, GRID_W, GRID_W), NEG, F32)
    tiles = []
    for i0, ws in NA_CLASSES:
        pairs = {(qi, kr): dr for qi, kr, dr in _na_pairs(i0, ws)}
        rows = [jnp.concatenate([rc[:, pairs[(qi, kr)]] if (qi, kr) in pairs else neg for kr in range(NA_WROWS)], axis=2)
                for qi in range(NA_QROWS)]
        tiles.append(jnp.concatenate(rows, axis=1))
    return jnp.stack(tiles)


def _na_cls(b):
    return jnp.where(b == 0, 0, jnp.where(b == NA_BLOCKS - 1, 2, 1))


def _na_start(b):
    return pl.multiple_of(jnp.clip(b * NA_QROWS - 4, 0, NA_ROWS_TOTAL - NA_WROWS) * GRID_W, GRID_W)


HPS = 4
LW = HPS * HD
NLW = DM // LW


NA_BWD_HPS = 4


def _na_in_specs(hps=HPS):
    lw = hps * HD
    nlw = DM // lw
    return [pl.BlockSpec((NA_QB, lw), lambda hp, b: (b, hp)),
            pl.BlockSpec((SEQ, lw), lambda hp, b: (0, nlw + hp)),
            pl.BlockSpec((SEQ, lw), lambda hp, b: (0, 2 * nlw + hp)),
            pl.BlockSpec((None, hps, NA_QB, NA_WIN), lambda hp, b: (_na_cls(b), hp, 0, 0))]


def _na_fwd(qkv, bias, carry):
    def body(q_ref, k_ref, v_ref, b_ref, o_ref):
        start = _na_start(pl.program_id(1))
        q = q_ref[...]
        kw = k_ref[pl.ds(start, NA_WIN), :]
        vw = v_ref[pl.ds(start, NA_WIN), :]
        outs = []
        for hh in range(HPS):
            sl = slice(hh * HD, (hh + 1) * HD)
            s = lax.dot_general(q[:, sl] * QSCALE, kw[:, sl], _NT, preferred_element_type=F32) + b_ref[hh]
            p = jnp.exp(s - jnp.max(s, axis=-1, keepdims=True))
            l = jnp.sum(p, axis=-1, keepdims=True)
            outs.append(jnp.dot(p.astype(BF16), vw[:, sl], preferred_element_type=F32) / l)
        o_ref[...] = jnp.concatenate(outs, axis=1).astype(o_ref.dtype)

    (o,), sent = _carrier_call(
        "na_fwd", body, (NLW, NA_BLOCKS), _na_in_specs(), [pl.BlockSpec((NA_QB, LW), lambda hp, b: (b, hp))],
        [jax.ShapeDtypeStruct((SEQ, DM), BF16)], [], (qkv, qkv, qkv, bias), carry)
    return o, sent


def _na_bwd(qkv, bias, do, carry):
    lw = NA_BWD_HPS * HD

    def body(q_ref, k_ref, v_ref, b_ref, do_ref, dqkv_ref, z_ref, dk_acc, dv_acc):
        blk = pl.program_id(1)

        @pl.when(blk == 0)
        def _():
            dk_acc[...] = jnp.zeros_like(dk_acc)
            dv_acc[...] = jnp.zeros_like(dv_acc)
            z_ref[...] = jnp.zeros_like(z_ref)

        start = _na_start(blk)
        q = q_ref[...]
        do = do_ref[...]
        kw = k_ref[pl.ds(start, NA_WIN), :]
        vw = v_ref[pl.ds(start, NA_WIN), :]
        dqs, dks, dvs, dss = [], [], [], []
        for hh in range(NA_BWD_HPS):
            sl = slice(hh * HD, (hh + 1) * HD)
            qh = q[:, sl] * QSCALE
            s = lax.dot_general(qh, kw[:, sl], _NT, preferred_element_type=F32) + b_ref[hh]
            p = jnp.exp(s - jnp.max(s, axis=-1, keepdims=True))
            p = p / jnp.sum(p, axis=-1, keepdims=True)
            dp = lax.dot_general(do[:, sl], vw[:, sl], _NT, preferred_element_type=F32)
            ds = p * (dp - jnp.sum(p * dp, axis=-1, keepdims=True))
            dsb = ds.astype(BF16)
            dqs.append(jnp.dot(dsb, kw[:, sl], preferred_element_type=F32) * QSCALE)
            dks.append(lax.dot_general(dsb, qh, _TN, preferred_element_type=F32))
            dvs.append(lax.dot_general(p.astype(BF16), do[:, sl], _TN, preferred_element_type=F32))
            dss.append(ds)
        for cls, (i0, ws) in enumerate(NA_CLASSES):
            @pl.when(_na_cls(blk) == cls)
            def _(i0=i0, ws=ws):
                for hh, ds in enumerate(dss):
                    for qi, kr, dr in _na_pairs(i0, ws):
                        z_ref[hh, dr * GRID_W:(dr + 1) * GRID_W, :] += ds[qi * GRID_W:(qi + 1) * GRID_W, kr * GRID_W:(kr + 1) * GRID_W]
        dqkv_ref[0, pl.ds(pl.multiple_of(blk * NA_QB, NA_QB), NA_QB), :] = jnp.concatenate(dqs, axis=1).astype(dqkv_ref.dtype)
        dk_acc[pl.ds(start, NA_WIN), :] += jnp.concatenate(dks, axis=1)
        dv_acc[pl.ds(start, NA_WIN), :] += jnp.concatenate(dvs, axis=1)

        @pl.when(blk == NA_BLOCKS - 1)
        def _():
            dqkv_ref[1] = dk_acc[...].astype(dqkv_ref.dtype)
            dqkv_ref[2] = dv_acc[...].astype(dqkv_ref.dtype)

    (dqkv, z), sent = _carrier_call(
        "na_bwd", body, (NH // NA_BWD_HPS, NA_BLOCKS),
        _na_in_specs(NA_BWD_HPS) + [pl.BlockSpec((NA_QB, lw), lambda hp, b: (b, hp))],
        [pl.BlockSpec((3, SEQ, lw), lambda hp, b: (0, 0, hp)), pl.BlockSpec((NA_BWD_HPS, 15 * GRID_W, GRID_W), lambda hp, b: (hp, 0, 0))],
        [jax.ShapeDtypeStruct((3, SEQ, DM), BF16), jax.ShapeDtypeStruct((NH, 15 * GRID_W, GRID_W), F32)],
        [pltpu.VMEM((SEQ, lw), F32), pltpu.VMEM((SEQ, lw), F32)], (qkv, qkv, qkv, bias, do), carry)
    return dqkv, z, sent


def _rpb_grad(z):
    z2 = z.reshape(NH * 15, GRID_W * GRID_W)

    def body(z_ref, e_ref, o_ref):
        o_ref[...] = jnp.dot(z_ref[...], e_ref[...], preferred_element_type=F32, precision=lax.Precision.HIGHEST)

    out = pl.pallas_call(body, out_shape=jax.ShapeDtypeStruct((NH * 15, 128), F32), name="rpb_grad",
                         compiler_params=pltpu.CompilerParams(vmem_limit_bytes=VMEM_LIMIT))(z2, _diag_onehot())
    return out[:, :31].reshape(NH, 15, 31)


DIL_BLOCKS = SEQ // DIL_QB
DIL_HPS = 8
DIL_LW = DIL_HPS * HD
DIL_NLW = DM // DIL_LW


COLS = 128


def _col_spec():
    return pl.BlockSpec((SEQ, COLS), lambda j: (0, j))


def _grp_spec():
    return pl.BlockSpec((3, SEQ, COLS), lambda j: (0, 0, j))


def _store_group_order(dst_ref, src_ref):
    for g, d in enumerate(DIL):
        n = SEQ // d
        for r in range(d):
            dst_ref[g, r * n:(r + 1) * n, :] = src_ref[pl.ds(r, n, stride=d), :].astype(dst_ref.dtype)


def _store_token_order(dst_ref, src_ref, g):
    d = DIL[g]
    n = SEQ // d
    for r in range(d):
        dst_ref[pl.ds(r, n, stride=d), :] = src_ref[g, r * n:(r + 1) * n, :]


def _to_groups(name, a):
    def body(a_ref, o_ref, t_ref):
        _store_group_order(o_ref, a_ref)
        for g in range(3):
            t_ref[g] = o_ref[g].astype(F32).T.astype(t_ref.dtype)

    return pl.pallas_call(
        body, grid=(DM // COLS,), in_specs=[_col_spec()], out_specs=[_grp_spec(), pl.BlockSpec((3, COLS, SEQ), lambda j: (0, j, 0))],
        out_shape=[jax.ShapeDtypeStruct((3, SEQ, DM), BF16), jax.ShapeDtypeStruct((3, DM, SEQ), BF16)],
        compiler_params=_params(("parallel",)), name=name)(a)


def _from_groups_sum(name, a):
    def body(a_ref, o_ref, t1, t2):
        _store_token_order(t1, a_ref, 1)
        _store_token_order(t2, a_ref, 2)
        o_ref[...] = (a_ref[0] + t1[...]) + t2[...]

    return pl.pallas_call(body, grid=(DM // COLS,), in_specs=[_grp_spec()], out_specs=_col_spec(),
                          out_shape=jax.ShapeDtypeStruct((SEQ, DM), F32), scratch_shapes=[pltpu.VMEM((SEQ, COLS), F32)] * 2,
                          compiler_params=_params(("parallel",)), name=name)(a)


def _dil_start(b):
    return pl.multiple_of(jnp.clip(b * DIL_QB - DIL_RADIUS, 0, SEQ - DIL_WIN), DIL_RADIUS)


def _dil_mask(g, b, start):
    shift = 11 - 2 * g
    ii = b * DIL_QB + lax.broadcasted_iota(jnp.int32, (DIL_QB, DIL_WIN), 0)
    jj = start + lax.broadcasted_iota(jnp.int32, (DIL_QB, DIL_WIN), 1)
    dist = jnp.abs(ii - jj)
    valid = (dist <= DIL_RADIUS) & (jnp.right_shift(ii, shift) == jnp.right_shift(jj, shift))
    return valid, dist.astype(F32)


def _dil_in_specs():
    return [pl.BlockSpec(memory_space=pltpu.SMEM),
            pl.BlockSpec((None, DIL_QB, DIL_LW), lambda g, hp, b: (g, b, hp)),
            pl.BlockSpec((None, SEQ, DIL_LW), lambda g, hp, b: (g, 0, DIL_NLW + hp)),
            pl.BlockSpec((None, SEQ, DIL_LW), lambda g, hp, b: (g, 0, 2 * DIL_NLW + hp))]


def _dil_fwd(qkv, slopes, carry):
    def body(sl_ref, q_ref, k_ref, v_ref, o_ref, lse_ref):
        g, hp, b = pl.program_id(0), pl.program_id(1), pl.program_id(2)
        start = _dil_start(b)
        valid, dist = _dil_mask(g, b, start)
        dil = jnp.left_shift(1, 2 * g).astype(F32)
        q = q_ref[...]
        kw = k_ref[pl.ds(start, DIL_WIN), :]
        vw = v_ref[pl.ds(start, DIL_WIN), :]
        outs, lses = [], []
        for hh in range(DIL_HPS):
            sl = slice(hh * HD, (hh + 1) * HD)
            s = lax.dot_general(q[:, sl] * QSCALE, kw[:, sl], _NT, preferred_element_type=F32)
            s = jnp.where(valid, s - (sl_ref[hp * DIL_HPS + hh] * dil) * dist, NEG)
            m = jnp.max(s, axis=-1, keepdims=True)
            p = jnp.exp(s - m)
            l = jnp.sum(p, axis=-1, keepdims=True)
            outs.append(jnp.dot(p.astype(BF16), vw[:, sl], preferred_element_type=F32) / l)
            lses.append(jnp.broadcast_to(m + jnp.log(l), (DIL_QB, HD)))
        o_ref[...] = jnp.concatenate(outs, axis=1)
        lse_ref[...] = jnp.concatenate(lses, axis=1)

    ospec = pl.BlockSpec((None, DIL_QB, DIL_LW), lambda g, hp, b: (g, b, hp))
    sh = jax.ShapeDtypeStruct((3, SEQ, DM), F32)
    (o, lse), sent = _carrier_call("dil_fwd", body, (3, DIL_NLW, DIL_BLOCKS), _dil_in_specs(), [ospec, ospec], [sh, sh], [],
                                   (slopes, qkv, qkv, qkv), carry)
    return o, lse, sent


def _dil_merge(o_all, lse_all):
    def body(o_ref, l_ref, out_ref, lse_ref, o1, o2, l1, l2):
        for g, (ot, lt) in ((1, (o1, l1)), (2, (o2, l2))):
            _store_token_order(ot, o_ref, g)
            _store_token_order(lt, l_ref, g)
        la, lb, lc = l_ref[0], l1[...], l2[...]
        m = jnp.maximum(jnp.maximum(la, lb), lc)
        wa, wb, wc = jnp.exp(la - m), jnp.exp(lb - m), jnp.exp(lc - m)
        sw = (wa + wb) + wc
        out_ref[...] = (((wa * o_ref[0] + wb * o1[...]) + wc * o2[...]) / sw).astype(out_ref.dtype)
        lse_ref[...] = m + jnp.log(sw)

    return pl.pallas_call(
        body, grid=(DM // COLS,), in_specs=[_grp_spec(), _grp_spec()], out_specs=[_col_spec(), _col_spec()],
        out_shape=[jax.ShapeDtypeStruct((SEQ, DM), BF16), jax.ShapeDtypeStruct((SEQ, DM), F32)],
        scratch_shapes=[pltpu.VMEM((SEQ, COLS), F32)] * 4, compiler_params=_params(("parallel",)), name="dil_merge")(o_all, lse_all)


def _dil_bwd_prep(do, o, lse):
    heads = COLS // HD

    def body(do_ref, o_ref, lse_ref, dog_ref, ddr_ref, lser_ref, dd, grp):
        prod = do_ref[...] * o_ref[...].astype(F32)
        dd[...] = jnp.concatenate(
            [jnp.broadcast_to(jnp.sum(prod[:, h * HD:(h + 1) * HD], axis=-1, keepdims=True), (SEQ, HD)) for h in range(heads)], axis=1)
        _store_group_order(dog_ref, do_ref)
        for src, dst in ((dd, ddr_ref), (lse_ref, lser_ref)):
            _store_group_order(grp, src)
            for g in range(3):
                t = grp[g].T
                for h in range(heads):
                    dst[g, h] = t[h * HD:h * HD + 8, :]

    rows = jax.ShapeDtypeStruct((3, NH, 8, SEQ), F32)
    rspec = pl.BlockSpec((3, heads, 8, SEQ), lambda j: (0, j, 0, 0))
    return pl.pallas_call(
        body, grid=(DM // COLS,), in_specs=[_col_spec()] * 3, out_specs=[_grp_spec(), rspec, rspec],
        out_shape=[jax.ShapeDtypeStruct((3, SEQ, DM), BF16), rows, rows],
        scratch_shapes=[pltpu.VMEM((SEQ, COLS), F32), pltpu.VMEM((3, SEQ, COLS), F32)],
        compiler_params=_params(("parallel",)), name="dil_bwd_prep")(do, o, lse)


def _dil_bwd(qkv, do, dd, lse, slopes, carry):
    def body(sl_ref, q_ref, k_ref, v_ref, do_ref, dd_ref, lse_ref, dqkv_ref, dk_acc, dv_acc):
        g, hp, b = pl.program_id(0), pl.program_id(1), pl.program_id(2)

        @pl.when(b == 0)
        def _():
            dk_acc[...] = jnp.zeros_like(dk_acc)
            dv_acc[...] = jnp.zeros_like(dv_acc)

        start = _dil_start(b)
        shift = 11 - 2 * g
        jj = start + lax.broadcasted_iota(jnp.int32, (DIL_WIN, DIL_QB), 0)
        ii = b * DIL_QB + lax.broadcasted_iota(jnp.int32, (DIL_WIN, DIL_QB), 1)
        dist = jnp.abs(ii - jj)
        valid = (dist <= DIL_RADIUS) & (jnp.right_shift(ii, shift) == jnp.right_shift(jj, shift))
        dist = dist.astype(F32)
        dil = jnp.left_shift(1, 2 * g).astype(F32)
        q = q_ref[...]
        do = do_ref[...]
        kw = k_ref[pl.ds(start, DIL_WIN), :]
        vw = v_ref[pl.ds(start, DIL_WIN), :]
        dqs, dks, dvs = [], [], []
        for hh in range(DIL_HPS):
            sl = slice(hh * HD, (hh + 1) * HD)
            qh = q[:, sl] * QSCALE
            st = lax.dot_general(kw[:, sl], qh, _NT, preferred_element_type=F32)
            st = jnp.where(valid, st - (sl_ref[hp * DIL_HPS + hh] * dil) * dist, NEG)
            pt = jnp.exp(st - lse_ref[hh, 0:1, :])
            dpt = lax.dot_general(vw[:, sl], do[:, sl], _NT, preferred_element_type=F32)
            dst = (pt * (dpt - dd_ref[hh, 0:1, :])).astype(BF16)
            dqs.append(lax.dot_general(kw[:, sl], dst, _TN, preferred_element_type=F32).T * QSCALE)
            dks.append(jnp.dot(dst, qh, preferred_element_type=F32))
            dvs.append(jnp.dot(pt.astype(BF16), do[:, sl], preferred_element_type=F32))
        dqkv_ref[0, pl.ds(pl.multiple_of(b * DIL_QB, DIL_QB), DIL_QB), :] = jnp.concatenate(dqs, axis=1).astype(dqkv_ref.dtype)
        dk_acc[pl.ds(start, DIL_WIN), :] += jnp.concatenate(dks, axis=1)
        dv_acc[pl.ds(start, DIL_WIN), :] += jnp.concatenate(dvs, axis=1)

        @pl.when(b == DIL_BLOCKS - 1)
        def _():
            dqkv_ref[1] = dk_acc[...].astype(dqkv_ref.dtype)
            dqkv_ref[2] = dv_acc[...].astype(dqkv_ref.dtype)

    qspec = pl.BlockSpec((None, DIL_QB, DIL_LW), lambda g, hp, b: (g, b, hp))
    rspec = pl.BlockSpec((None, DIL_HPS, 8, DIL_QB), lambda g, hp, b: (g, hp, 0, b))
    (dqkv,), sent = _carrier_call(
        "dil_bwd", body, (3, DIL_NLW, DIL_BLOCKS), _dil_in_specs() + [qspec, rspec, rspec],
        [pl.BlockSpec((None, 3, SEQ, DIL_LW), lambda g, hp, b: (g, 0, 0, hp))], [jax.ShapeDtypeStruct((3, 3, SEQ, DM), BF16)],
        [pltpu.VMEM((SEQ, DIL_LW), F32), pltpu.VMEM((SEQ, DIL_LW), F32)], (slopes, qkv, qkv, qkv, do, dd, lse), carry)
    return dqkv, sent


def _ffn_fwd(name, x, g_pre, g_post, wgt4, wut4, wd4, carry):
    tm = 512

    def body(x_ref, gpre_ref, gpost_ref, wg_ref, wu_ref, wd_ref, xn_ref, h_ref, gate_ref, up_ref, u_ref, acc):
        s = pl.program_id(1)

        @pl.when(s == 0)
        def _():
            x = x_ref[...]
            r = lax.rsqrt(jnp.mean(x * x, axis=-1, keepdims=True) + RMS_EPS)
            h_ref[...] = (x * r * gpre_ref[...]).astype(h_ref.dtype)

        h = h_ref[...]
        gate = lax.dot_general(h, wg_ref[...], _NT, preferred_element_type=F32).astype(BF16)
        up = lax.dot_general(h, wu_ref[...], _NT, preferred_element_type=F32).astype(BF16)
        gate_ref[...] = gate
        up_ref[...] = up
        gf = gate.astype(F32)
        act = (gf * jax.nn.sigmoid(gf) * up.astype(F32)).astype(BF16)
        part = jnp.dot(act, wd_ref[...], preferred_element_type=F32)

        @pl.when(s == 0)
        def _():
            acc[...] = part

        @pl.when(s > 0)
        def _():
            acc[...] += part

        @pl.when(s == NCHIP - 1)
        def _():
            u = acc[...]
            u_ref[...] = u
            r = lax.rsqrt(jnp.mean(u * u, axis=-1, keepdims=True) + RMS_EPS)
            xn_ref[...] = x_ref[...] + u * r * gpost_ref[...]

    rows = pl.BlockSpec((tm, DM), lambda i, s: (i, 0))
    vec = pl.BlockSpec((1, DM), lambda i, s: (0, 0))
    wspec = _ffn_wspec(lambda i, s: (s, 0, 0))
    mid = pl.BlockSpec((None, tm, FSH), lambda i, s: (s, i, 0))
    outs, sent = _carrier_call(
        name, body, (SEQ // tm, NCHIP), [rows, vec, vec, wspec, wspec, wspec], [rows, rows, mid, mid, rows],
        [jax.ShapeDtypeStruct((SEQ, DM), F32), jax.ShapeDtypeStruct((SEQ, DM), BF16), jax.ShapeDtypeStruct((NCHIP, SEQ, FSH), BF16),
         jax.ShapeDtypeStruct((NCHIP, SEQ, FSH), BF16), jax.ShapeDtypeStruct((SEQ, DM), F32)],
        [pltpu.VMEM((tm, DM), F32)], (x, g_pre, g_post, wgt4, wut4, wd4), carry)
    return outs, sent


def _ffn_block(layer, x, g_pre, g_post, ex):
    tag = f"l{layer}_ffn_fwd"
    (x_new, h, gate, up, u), sent = _ffn_fwd(tag, x, g_pre, g_post, ex.weight(("ffn_w_gate", layer)), ex.weight(("ffn_w_up", layer)),
                                             ex.weight(("ffn_w_down", layer)), ex.carry(tag))
    ex.carried(tag, sent)
    return x_new, (x, h, gate, up, u)


def _ffn_bwd(name, dx, x, gate, up, u, g_pre, g_post, wgt4, wut4, wd4, carry):
    tm = 512

    def body(dx_ref, x_ref, gate_ref, up_ref, u_ref, gpre_ref, gpost_ref, wg_ref, wu_ref, wd_ref,
             dxin_ref, du_ref, dgate_ref, dup_ref, act_ref, dgpre_ref, dgpost_ref, dh_acc):
        i, s = pl.program_id(0), pl.program_id(1)

        @pl.when((i == 0) & (s == 0))
        def _():
            dgpre_ref[...] = jnp.zeros_like(dgpre_ref)
            dgpost_ref[...] = jnp.zeros_like(dgpost_ref)

        @pl.when(s == 0)
        def _():
            dy = dx_ref[...]
            uu = u_ref[...]
            r = lax.rsqrt(jnp.mean(uu * uu, axis=-1, keepdims=True) + RMS_EPS)
            yh = uu * r
            t = dy * gpost_ref[...]
            du_ref[...] = (r * (t - yh * jnp.mean(t * yh, axis=-1, keepdims=True))).astype(du_ref.dtype)
            dgpost_ref[...] += jnp.sum(dy * yh, axis=0, keepdims=True)

        dact = lax.dot_general(du_ref[...], wd_ref[...], _NT, preferred_element_type=F32)
        g = gate_ref[...].astype(F32)
        upv = up_ref[...].astype(F32)
        sg = jax.nn.sigmoid(g)
        dgate = (dact * upv * sg * (1.0 + g * (1.0 - sg))).astype(BF16)
        dup = (dact * g * sg).astype(BF16)
        dgate_ref[...] = dgate
        dup_ref[...] = dup
        act_ref[...] = (g * sg * upv).astype(act_ref.dtype)
        part = jnp.dot(dgate, wg_ref[...], preferred_element_type=F32) + jnp.dot(dup, wu_ref[...], preferred_element_type=F32)

        @pl.when(s == 0)
        def _():
            dh_acc[...] = part

        @pl.when(s > 0)
        def _():
            dh_acc[...] += part

        @pl.when(s == NCHIP - 1)
        def _():
            dh = dh_acc[...]
            xx = x_ref[...]
            r = lax.rsqrt(jnp.mean(xx * xx, axis=-1, keepdims=True) + RMS_EPS)
            yh = xx * r
            t = dh * gpre_ref[...]
            dxin_ref[...] = dx_ref[...] + r * (t - yh * jnp.mean(t * yh, axis=-1, keepdims=True))
            dgpre_ref[...] += jnp.sum(dh * yh, axis=0, keepdims=True)

    rows = pl.BlockSpec((tm, DM), lambda i, s: (i, 0))
    vec = pl.BlockSpec((1, DM), lambda i, s: (0, 0))
    wspec = _ffn_wspec(lambda i, s: (s, 0, 0))
    mid = pl.BlockSpec((None, tm, FSH), lambda i, s: (s, i, 0))
    mid_shape = jax.ShapeDtypeStruct((NCHIP, SEQ, FSH), BF16)
    return _carrier_call(
        name, body, (SEQ // tm, NCHIP), [rows, rows, mid, mid, rows, vec, vec, wspec, wspec, wspec], [rows, rows, mid, mid, mid, vec, vec],
        [jax.ShapeDtypeStruct((SEQ, DM), F32), jax.ShapeDtypeStruct((SEQ, DM), BF16), mid_shape, mid_shape, mid_shape,
         jax.ShapeDtypeStruct((1, DM), F32), jax.ShapeDtypeStruct((1, DM), F32)],
        [pltpu.VMEM((tm, DM), F32)], (dx, x, gate, up, u, g_pre, g_post, wgt4, wut4, wd4), carry)


def _ffn_block_bwd(layer, dx, saved, g_pre, g_post, ex):
    tag = f"l{layer}"
    x, h, gate, up, u = saved
    (dx_in, du, dgate, dup, act, dg_pre, dg_post), sent = _ffn_bwd(
        f"{tag}_ffn_bwd", dx, x, gate, up, u, g_pre, g_post, ex.weight(("ffn_w_gate", layer)), ex.weight(("ffn_w_up", layer)),
        ex.weight(("ffn_w_down", layer)), ex.carry(f"{tag}_ffn_bwd"))
    ex.carried(f"{tag}_ffn_bwd", sent)
    d_wd = _ffn_bwd_dw(f"{tag}_dwd", act, du)
    d_wg = _ffn_bwd_dw(f"{tag}_dwg", dgate, h)
    d_wu = _ffn_bwd_dw(f"{tag}_dwu", dup, h)
    ex.grads(f"{tag}_ffn", {("ffn_w_gate", layer): d_wg, ("ffn_w_up", layer): d_wu, ("ffn_w_down", layer): d_wd})
    return dx_in, dg_pre, dg_post


def _alibi_slopes():
    return 2.0 ** (-8.0 * jnp.arange(1, NH + 1, dtype=F32) / NH)


def _local_step(x, target, norms, rpb, ex):
    g_mix_pre, g_mix_post, g_ffn_pre, g_ffn_post = norms
    row = lambda a, i: a[i:i + 1]

    bias = _na_bias_tiles(rpb)
    h0, h0t = _rms_fwd_both("l0_mix_pre", x, row(g_mix_pre, 0))
    qkv0 = _qkv_fwd("l0_qkv", h0[None], ex.weight(("na_w_qkv", 0)))
    o0, sent = _na_fwd(qkv0[0], bias, ex.carry("na_fwd"))
    ex.carried("na_fwd", sent)
    na_wo = ex.weight(("na_w_o", 0)).reshape(DM, DM)
    x1, u0 = _proj_fwd("l0_proj", o0, na_wo, x, row(g_mix_post, 0))
    x2, ffn0 = _ffn_block(0, x1, row(g_ffn_pre, 0), row(g_ffn_post, 0), ex)

    slopes = _alibi_slopes()
    h2g, h2gt = _to_groups("l1_h_groups", _rms_fwd("l1_mix_pre", x2, row(g_mix_pre, 1), F32))
    dil_wqkv = ex.weight(("dil_w_qkv", 0))
    qkv1 = _qkv_fwd("l1_qkv", h2g, dil_wqkv)
    og, lg, sent = _dil_fwd(qkv1, slopes, ex.carry("dil_fwd"))
    ex.carried("dil_fwd", sent)
    o1, lse = _dil_merge(og, lg)
    dil_wo = ex.weight(("dil_w_o", 0)).reshape(DM, DM)
    x3, u1 = _proj_fwd("l1_proj", o1, dil_wo, x2, row(g_mix_post, 1))
    x4, ffn1 = _ffn_block(1, x3, row(g_ffn_pre, 1), row(g_ffn_post, 1), ex)

    dx4, loss_row = _loss_grad("loss", x4, target)

    dx3, dg_fpre1, dg_fpost1 = _ffn_block_bwd(1, dx4, ffn1, row(g_ffn_pre, 1), row(g_ffn_post, 1), ex)
    do1, du1, dg_mpost1 = _proj_bwd("l1_proj_bwd", dx3, u1, row(g_mix_post, 1), dil_wo, F32)
    d_dil_wo = _proj_bwd_dw("l1_dwo", o1, du1)
    dog, ddg, lseg = _dil_bwd_prep(do1, o1, lse)
    dqkv1, sent = _dil_bwd(qkv1, dog, ddg, lseg, slopes, ex.carry("dil_bwd"))
    ex.carried("dil_bwd", sent)
    d_dil_wqkv = _qkv_bwd_dw("l1_dwqkv", h2gt, dqkv1, dil_wqkv.shape[2])
    ex.grads("l1_mix", {("dil_w_qkv", 0): d_dil_wqkv, ("dil_w_o", 0): d_dil_wo.reshape(NCHIP, DM // NCHIP, DM)})
    dh2g, sent = _qkv_bwd_dh("l1_dh", dqkv1, dil_wqkv, ex.carry("l1_dh"))
    ex.carried("l1_dh", sent)
    dh2 = _from_groups_sum("l1_dh_tokens", dh2g)
    dx2, dg_mpre1 = _norm_bwd("l1_mix_pre_bwd", [dh2], x2, row(g_mix_pre, 1), res=dx3)

    dx1, dg_fpre0, dg_fpost0 = _ffn_block_bwd(0, dx2, ffn0, row(g_ffn_pre, 0), row(g_ffn_post, 0), ex)
    do0, du0, dg_mpost0 = _proj_bwd("l0_proj_bwd", dx1, u0, row(g_mix_post, 0), na_wo, BF16)
    d_na_wo = _proj_bwd_dw("l0_dwo", o0, du0)
    dqkv0, z, sent = _na_bwd(qkv0[0], bias, do0, ex.carry("na_bwd"))
    ex.carried("na_bwd", sent)
    d_rpb = _rpb_grad(z)
    na_wqkv = ex.weight(("na_w_qkv", 0))
    d_na_wqkv = _qkv_bwd_dw("l0_dwqkv", h0t[None], dqkv0[None], na_wqkv.shape[2])
    ex.grads("l0_mix", {("na_w_qkv", 0): d_na_wqkv, ("na_w_o", 0): d_na_wo.reshape(NCHIP, DM // NCHIP, DM)})
    dh0, sent = _qkv_bwd_dh("l0_dh", dqkv0[None], na_wqkv, ex.carry("l0_dh"))
    ex.carried("l0_dh", sent)
    dx0, dg_mpre0 = _norm_bwd("l0_mix_pre_bwd", [dh0[0]], x, row(g_mix_pre, 0), res=dx1)

    dnorms = (jnp.concatenate([dg_mpre0, dg_mpre1]), jnp.concatenate([dg_mpost0, dg_mpost1]),
              jnp.concatenate([dg_fpre0, dg_fpre1]), jnp.concatenate([dg_fpost0, dg_fpost1]))
    return loss_row, dx0, dnorms, d_rpb


def _place():
    x, y, c = lax.axis_index("x"), lax.axis_index("y"), lax.axis_index("c")
    chips = ((1 - x, y), (x, 1 - y), (1 - x, 1 - y))
    return x, y, c, chips


def _chip_id(chip):
    return 2 * chip[0] + chip[1]


def _comm_call(name, body, ins, out_shapes, n_sems, aliases=None):
    return pl.pallas_call(
        body, in_specs=[HBM_SPEC] * len(ins), out_specs=[HBM_SPEC] * len(out_shapes), out_shape=out_shapes,
        scratch_shapes=[pltpu.SemaphoreType.DMA((k,)) for k in n_sems], input_output_aliases=aliases or {},
        compiler_params=pltpu.CompilerParams(has_side_effects=True), name=name)(*ins)


def _gather_copies(shards):
    n = len(shards)

    def copies(src, out, sems):
        send_sems, recv_sems = sems
        x, y, c, chips = _place()

        def copy(t, k, chip, half, to, from_src=False):
            blk = out[t].at[_chip_id(chip), half]
            return pltpu.make_async_remote_copy(
                src_ref=src[t].at[half] if from_src else blk, dst_ref=blk,
                send_sem=send_sems.at[6 * t + k], recv_sem=recv_sems.at[6 * t + k], device_id=to, device_id_type=MESH)

        return copy, x, y, c, chips

    def issue(src, out, sems):
        copy, x, y, c, chips = copies(src, out, sems)
        for t in range(n):
            for j, chip in enumerate(chips):
                copy(t, j, (x, y), c, (*chip, c), from_src=True).start()

    def drain(src, out, sems):
        copy, x, y, c, chips = copies(src, out, sems)
        passed = []
        for t in range(n):
            for j, chip in enumerate(chips):
                copy(t, j, chip, c, (x, y, c)).wait_recv()
                fwd = copy(t, 3 + j, chip, c, (x, y, 1 - c))
                fwd.start()
                passed.append(fwd)
        for t in range(n):
            for j, chip in enumerate(chips):
                copy(t, 3 + j, chip, 1 - c, (x, y, c)).wait_recv()
        for t in range(n):
            for j, chip in enumerate(chips):
                copy(t, j, (x, y), c, (*chip, c), from_src=True).wait_send()
        for cp in passed:
            cp.wait_send()

    return _Carried(shards, [jax.ShapeDtypeStruct((NCHIP,) + s.shape, s.dtype) for s in shards], (6 * n, 6 * n), issue, drain)


def _pair_exchange(name, grads):
    n = len(grads)

    def body(*refs):
        g, theirs = refs[:n], refs[n:2 * n]
        send_sems, recv_sems = refs[2 * n:]
        x, y, c, _ = _place()
        swap = [pltpu.make_async_remote_copy(src_ref=g[t].at[:, 1 - c], dst_ref=theirs[t], send_sem=send_sems.at[t],
                                             recv_sem=recv_sems.at[t], device_id=(x, y, 1 - c), device_id_type=MESH) for t in range(n)]
        for cp in swap:
            cp.start()
        for cp in swap:
            cp.wait()

    return _comm_call(name, body, grads, [jax.ShapeDtypeStruct((NCHIP,) + g.shape[2:], g.dtype) for g in grads], (n, n))


def _chip_exchange_copies(parts):
    n = len(parts)

    def copies(p, slots, sems):
        send_sems, recv_sems = sems
        x, y, c, chips = _place()
        return [pltpu.make_async_remote_copy(src_ref=p[t].at[_chip_id(chips[j])], dst_ref=slots[t].at[j], send_sem=send_sems.at[3 * t + j],
                                             recv_sem=recv_sems.at[3 * t + j], device_id=(*chips[j], c), device_id_type=MESH)
                for t in range(n) for j in range(3)]

    def issue(p, slots, sems):
        for cp in copies(p, slots, sems):
            cp.start()

    def drain(p, slots, sems):
        for cp in copies(p, slots, sems):
            cp.wait()

    return _Carried(parts, [jax.ShapeDtypeStruct((3,) + p.shape[1:], p.dtype) for p in parts], (3 * n, 3 * n), issue, drain)


def _pair_share(full):
    n = len(full)

    def body(*refs):
        buf = refs[n:2 * n]
        send_sems, recv_sems = refs[2 * n:]
        x, y, c, _ = _place()
        sends = [pltpu.make_async_remote_copy(src_ref=buf[t].at[c], dst_ref=buf[t].at[c], send_sem=send_sems.at[t], recv_sem=recv_sems.at[t],
                                              device_id=(x, y, 1 - c), device_id_type=MESH) for t in range(n)]
        for cp in sends:
            cp.start()
        for t in range(n):
            pltpu.make_async_remote_copy(src_ref=buf[t].at[c], dst_ref=buf[t].at[1 - c], send_sem=send_sems.at[t], recv_sem=recv_sems.at[t],
                                         device_id=(x, y, 1 - c), device_id_type=MESH).wait_recv()
        for cp in sends:
            cp.wait_send()

    return _comm_call("grad_pair_share", body, full, [jax.ShapeDtypeStruct(f.shape, f.dtype) for f in full], (n, n),
                      aliases={t: t for t in range(n)})


SMALL_ROWS = 128


def _allreduce_small(v):
    def body(v_ref, o_ref, buf, send_sems, recv_sems):
        x, y, c, _ = _place()
        me = 4 * x + 2 * y + c
        flip = lambda a, f: 1 - a if f else a
        buf[me] = v_ref[...]
        peers = [(flip(x, d >> 2 & 1), flip(y, d >> 1 & 1), flip(c, d & 1)) for d in range(1, 8)]
        sends = [pltpu.make_async_remote_copy(src_ref=v_ref, dst_ref=buf.at[me], send_sem=send_sems.at[i], recv_sem=recv_sems.at[i],
                                              device_id=peer, device_id_type=MESH) for i, peer in enumerate(peers)]
        for cp in sends:
            cp.start()
        for i, (px, py, pc) in enumerate(peers):
            pltpu.make_async_remote_copy(src_ref=v_ref, dst_ref=buf.at[4 * px + 2 * py + pc], send_sem=send_sems.at[i], recv_sem=recv_sems.at[i],
                                         device_id=(px, py, pc), device_id_type=MESH).wait_recv()
        for cp in sends:
            cp.wait_send()
        acc = buf[0]
        for k in range(1, 8):
            acc = acc + buf[k]
        o_ref[...] = acc

    vm = pl.BlockSpec(memory_space=pltpu.VMEM)
    return pl.pallas_call(
        body, in_specs=[vm], out_specs=vm, out_shape=jax.ShapeDtypeStruct((SMALL_ROWS, 128), F32),
        scratch_shapes=[pltpu.VMEM((8, SMALL_ROWS, 128), F32), pltpu.SemaphoreType.DMA((7,)), pltpu.SemaphoreType.DMA((7,))],
        compiler_params=pltpu.CompilerParams(has_side_effects=True), name="allreduce_small")(v)


def _row_block(rows, cols, budget=1 << 20):
    best = 8
    for bm in range(8, rows + 1, 8):
        if rows % bm == 0 and bm * cols * 4 <= budget:
            best = bm
    return best


def _pair_sum(name, place, g, theirs):
    _, m, c = theirs.shape
    bm = _row_block(m, c)

    def body(place_ref, a_ref, b_ref, o_ref):
        o_ref[...] = (a_ref[...].astype(F32) + b_ref[...].astype(F32)).astype(o_ref.dtype)

    spec = pl.BlockSpec((None, bm, c), lambda k, i, pr: (k, i, 0))
    return pl.pallas_call(
        body, out_shape=jax.ShapeDtypeStruct(theirs.shape, BF16),
        grid_spec=pltpu.PrefetchScalarGridSpec(
            num_scalar_prefetch=1, grid=(NCHIP, m // bm),
            in_specs=[pl.BlockSpec((None, None, bm, c), lambda k, i, pr: (k, pr[0], i, 0)), spec], out_specs=spec),
        compiler_params=_params(("parallel", "parallel")), name=name)(place, g, theirs)


def _chip_sum(name, place, parts, slots):
    _, m, c = parts.shape
    bm = _row_block(m, c)

    def body(place_ref, p_ref, s_ref, o_ref):
        s = s_ref[...].astype(F32)
        o_ref[...] = ((p_ref[...].astype(F32) + s[0]) + s[1]) + s[2]

    return pl.pallas_call(
        body, out_shape=jax.ShapeDtypeStruct((2, m, c), F32),
        grid_spec=pltpu.PrefetchScalarGridSpec(
            num_scalar_prefetch=1, grid=(m // bm,),
            in_specs=[pl.BlockSpec((None, bm, c), lambda i, pr: (pr[1], i, 0)), pl.BlockSpec((3, bm, c), lambda i, pr: (0, i, 0))],
            out_specs=pl.BlockSpec((None, bm, c), lambda i, pr: (pr[0], i, 0))),
        compiler_params=_params(("parallel",)), name=name)(place, parts, slots)


def _adamw(name, w, g, m, v, layer=0, into=None):
    lead, rows, cols = w.shape
    bm = _row_block(rows, cols, budget=768 * 1024)
    c1 = 1.0 - ADAM_B1 ** ADAM_STEP
    c2 = 1.0 - ADAM_B2 ** ADAM_STEP

    def body(w_ref, g_ref, m_ref, v_ref, *rest):
        go_ref, d_ref, mo_ref, vo_ref = rest[-4:]
        g = g_ref[...]
        mn = ADAM_B1 * m_ref[...] + (1.0 - ADAM_B1) * g
        vn = ADAM_B2 * v_ref[...] + (1.0 - ADAM_B2) * (g * g)
        go_ref[...] = g
        mo_ref[...] = mn
        vo_ref[...] = vn
        d_ref[...] = -ADAM_LR * ((mn / c1) / (jnp.sqrt(vn / c2) + ADAM_EPS) + ADAM_WD * w_ref[...])

    spec = pl.BlockSpec((None, bm, cols), lambda i: (layer, i, 0))
    sh = jax.ShapeDtypeStruct((lead, rows, cols), F32)
    prev = [] if into is None else list(into)
    return pl.pallas_call(
        body, grid=(rows // bm,), in_specs=[spec, pl.BlockSpec((bm, cols), lambda i: (i, 0)), spec, spec] + [pl.BlockSpec(memory_space=pl.ANY)] * len(prev),
        out_specs=[spec] * 4, out_shape=[sh] * 4, input_output_aliases={4 + k: k for k in range(len(prev))},
        compiler_params=_params(("parallel",)), name=name)(w, g, m, v, *prev)


def _pack_small(norms, rpb):
    flat = jnp.concatenate([a.reshape(-1) for a in norms] + [rpb.reshape(-1)])
    return jnp.pad(flat, (0, SMALL_ROWS * 128 - flat.shape[0])).reshape(SMALL_ROWS, 128)


def _unpack_small(p):
    flat = p.reshape(-1)
    norms = [flat[i * 2 * DM:(i + 1) * 2 * DM].reshape(2, DM) for i in range(4)]
    rpb = flat[8 * DM:8 * DM + NH * 15 * 31].reshape(1, NH, 15, 31)
    return norms, rpb


FFN_NAMES = ("ffn_w_gate", "ffn_w_up", "ffn_w_down")
L0_FFN = tuple((n, 0) for n in FFN_NAMES)
L1_FFN = tuple((n, 1) for n in FFN_NAMES)
NA_KEYS = (("na_w_qkv", 0), ("na_w_o", 0))
DIL_KEYS = (("dil_w_qkv", 0), ("dil_w_o", 0))


class _Exchange:
    GATHERS = {"na_fwd": L0_FFN, "l0_ffn_fwd": DIL_KEYS, "dil_fwd": L1_FFN}
    EXCHANGES = {"dil_bwd": L1_FFN, "l1_dh": DIL_KEYS[1:], "l0_ffn_bwd": DIL_KEYS[:1], "na_bwd": L0_FFN, "l0_dh": NA_KEYS}

    def __init__(self, shards):
        self.chip = 2 * lax.axis_index("x") + lax.axis_index("y")
        self.place = jnp.stack([lax.axis_index("c"), self.chip]).astype(jnp.int32)
        self.own = {k: s.reshape(2, s.shape[0] // 2, s.shape[1]).astype(BF16) for k, s in shards.items()}
        self.gathered, self.parts, self.full = {}, {}, {}
        self._take(NA_KEYS, _run_carried("gather_first", _gather_copies([self.own[k] for k in NA_KEYS])))

    def _take(self, keys, landed):
        for k, gw in zip(keys, landed):
            self.gathered[k] = lax.dynamic_update_slice(gw, self.own[k][None], (self.chip, 0, 0, 0))

    def _sum(self, keys, slots):
        for k, s in zip(keys, slots):
            self.full[k] = _chip_sum(f"chip_sum_{k[0]}_{k[1]}", self.place, self.parts[k], s)

    def weight(self, key):
        g = self.gathered[key]
        return g.reshape(NCHIP, 2 * g.shape[2], g.shape[3])

    def carry(self, tag):
        if tag in self.GATHERS:
            return _gather_copies([self.own[k] for k in self.GATHERS[tag]])
        if tag in self.EXCHANGES:
            return _chip_exchange_copies([self.parts[k] for k in self.EXCHANGES[tag]])
        return None

    def carried(self, tag, landed):
        if tag in self.GATHERS:
            self._take(self.GATHERS[tag], landed)
        elif tag in self.EXCHANGES:
            self._sum(self.EXCHANGES[tag], landed)

    def grads(self, tag, dw):
        keys = tuple(dw)
        mine = [dw[k].reshape(NCHIP, 2, -1, dw[k].shape[-1]) for k in keys]
        theirs = _pair_exchange(f"grad_pair_exchange_{tag}", mine)
        for k, a, b in zip(keys, mine, theirs):
            self.parts[k] = _pair_sum(f"pair_sum_{k[0]}_{k[1]}", self.place, a, b)

    def finish(self):
        keys = tuple(self.full)
        shared = _pair_share([self.full[k] for k in keys])
        return {k: s.reshape(2 * s.shape[1], s.shape[2]) for k, s in zip(keys, shared)}


def kernel(x, norm_mix_pre, norm_mix_post, norm_ffn_pre, norm_ffn_post, na_w_qkv, na_w_o, na_rpb, dil_w_qkv, dil_w_o, ffn_w_gate, ffn_w_up, ffn_w_down, loss_target, m_norm_mix_pre, m_norm_mix_post, m_norm_ffn_pre, m_norm_ffn_post, m_na_w_qkv, m_na_w_o, m_na_rpb, m_dil_w_qkv, m_dil_w_o, m_ffn_w_gate, m_ffn_w_up, m_ffn_w_down, v_norm_mix_pre, v_norm_mix_post, v_norm_ffn_pre, v_norm_ffn_post, v_na_w_qkv, v_na_w_o, v_na_rpb, v_dil_w_qkv, v_dil_w_o, v_ffn_w_gate, v_ffn_w_up, v_ffn_w_down):
    tr = lambda a: jnp.swapaxes(a, 1, 2)
    weights = {"na_w_qkv": na_w_qkv, "na_w_o": na_w_o, "dil_w_qkv": dil_w_qkv, "dil_w_o": dil_w_o,
               "ffn_w_gate": tr(ffn_w_gate), "ffn_w_up": tr(ffn_w_up), "ffn_w_down": ffn_w_down}
    m_in = {"na_w_qkv": m_na_w_qkv, "na_w_o": m_na_w_o, "dil_w_qkv": m_dil_w_qkv, "dil_w_o": m_dil_w_o,
            "ffn_w_gate": tr(m_ffn_w_gate), "ffn_w_up": tr(m_ffn_w_up), "ffn_w_down": m_ffn_w_down}
    v_in = {"na_w_qkv": v_na_w_qkv, "na_w_o": v_na_w_o, "dil_w_qkv": v_dil_w_qkv, "dil_w_o": v_dil_w_o,
            "ffn_w_gate": tr(v_ffn_w_gate), "ffn_w_up": tr(v_ffn_w_up), "ffn_w_down": v_ffn_w_down}

    ex = _Exchange({(n, l): weights[n][l] for n in weights for l in range(weights[n].shape[0])})
    norms = (norm_mix_pre, norm_mix_post, norm_ffn_pre, norm_ffn_post)
    loss_row, dx, dnorms, d_rpb = _local_step(x[0], loss_target[0], norms, na_rpb[0], ex)
    loss = lax.psum(loss_row[0, 0], ("x", "y", "c"))
    full = ex.finish()
    small = _allreduce_small(_pack_small(dnorms, d_rpb))

    out_g, out_d, out_m, out_v = {}, {}, {}, {}
    for n in weights:
        res = None
        for l in range(weights[n].shape[0]):
            res = _adamw(f"adamw_{n}_{l}", weights[n], full[(n, l)], m_in[n], v_in[n], l, res)
        if n in ("ffn_w_gate", "ffn_w_up"):
            res = [tr(r) for r in res]
        out_g[n], out_d[n], out_m[n], out_v[n] = res
    sm_names = ("norm_mix_pre", "norm_mix_post", "norm_ffn_pre", "norm_ffn_post", "na_rpb")
    sm = _adamw("adamw_small", _pack_small(norms, na_rpb)[None], small,
                _pack_small((m_norm_mix_pre, m_norm_mix_post, m_norm_ffn_pre, m_norm_ffn_post), m_na_rpb)[None],
                _pack_small((v_norm_mix_pre, v_norm_mix_post, v_norm_ffn_pre, v_norm_ffn_post), v_na_rpb)[None])
    for res, dst in zip(sm, (out_g, out_d, out_m, out_v)):
        ns, rp = _unpack_small(res)
        for n, a in zip(sm_names, ns + [rp]):
            dst[n] = a

    order = ("norm_mix_pre", "norm_mix_post", "norm_ffn_pre", "norm_ffn_post", "na_w_qkv", "na_w_o", "na_rpb", "dil_w_qkv", "dil_w_o",
             "ffn_w_gate", "ffn_w_up", "ffn_w_down")
    return (loss, dx[None], *[out_g[n] for n in order], *[out_d[n] for n in order], *[out_m[n] for n in order], *[out_v[n] for n in order])
```

```python
import functools

import numpy as np
import jax
import jax.numpy as jnp
from jax import lax
from jax.experimental import pallas as pl
from jax.experimental.pallas import tpu as pltpu

F32 = jnp.float32
BF16 = jnp.bfloat16

SEQ = 2048
DM = 1024
NH = 16
HD = 64
DFF = 2816
NCHIP = 4
FSH = DFF // NCHIP
GRID_W = 64
NA_QROWS = 4
NA_QB = NA_QROWS * GRID_W
NA_WROWS = 12
NA_WIN = NA_WROWS * GRID_W
DIL = (1, 4, 16)
DIL_QB = 256
DIL_WIN = DIL_QB + 128
DIL_RADIUS = 64
RMS_EPS = 1e-6
NEG = -1e30
QSCALE = HD ** -0.5
CH = 256
MESH = pl.DeviceIdType.MESH

ADAM_LR, ADAM_B1, ADAM_B2, ADAM_EPS, ADAM_WD, ADAM_STEP = 0.001, 0.9, 0.999, 1e-08, 0.01, 10

VMEM_LIMIT = 56 * 1024 * 1024

_NN = (((1,), (0,)), ((), ()))
_NT = (((1,), (1,)), ((), ()))
_TN = (((0,), (0,)), ((), ()))


def _params(sem):
    return pltpu.CompilerParams(dimension_semantics=sem, vmem_limit_bytes=VMEM_LIMIT)


def _matmul(name, pairs, grid, out_shape, out_spec, acc_shape, carrying=False, carry=None):
    nk = grid[-1]
    npair = len(pairs)
    n_in = 2 * npair

    def body(*refs):
        ins, o_ref = refs[:2 * npair], refs[n_in]
        part = None
        for p in range(npair):
            d = lax.dot_general(ins[2 * p][...].astype(BF16), ins[2 * p + 1][...].astype(BF16), pairs[p][4],
                                preferred_element_type=F32)
            part = d if part is None else part + d
        if nk == 1:
            o_ref[...] = part.astype(o_ref.dtype)
        else:
            acc_ref = refs[n_in + 1]
            kk = pl.program_id(len(grid) - 1)

            @pl.when(kk == 0)
            def _():
                acc_ref[...] = part

            @pl.when(kk > 0)
            def _():
                acc_ref[...] += part

            @pl.when(kk == nk - 1)
            def _():
                o_ref[...] = acc_ref[...].astype(o_ref.dtype)

    ops, specs = [], []
    for a, a_spec, b, b_spec, _ in pairs:
        ops += [a, b]
        specs += [a_spec, b_spec]
    (out,), sent = _carrier_call(name, body, grid, specs, [out_spec], [out_shape], [] if nk == 1 else [pltpu.VMEM(acc_shape, F32)], ops, carry)
    return (out, sent) if carrying else out


def _qkv_fwd(name, h_all, w4, carry):
    g_n = h_all.shape[0]
    per = w4.shape[2] // CH
    return _matmul(
        name, [(h_all, pl.BlockSpec((None, SEQ, DM), lambda g, q, k: (g, 0, 0)),
                w4, pl.BlockSpec((None, DM, CH), lambda g, q, k: ((g * 12 + q) // per, 0, (g * 12 + q) % per)), _NN)],
        (g_n, 12, 1), jax.ShapeDtypeStruct((g_n, SEQ, 3 * DM), BF16),
        pl.BlockSpec((None, SEQ, CH), lambda g, q, k: (g, 0, q)), None, carrying=True, carry=carry)


def _qkv_bwd_dh(name, dqkv, w4, carry):
    g_n = dqkv.shape[0]
    per = w4.shape[2] // CH
    tm = SEQ

    def pair(cb):
        chunk = lambda g, t: g * 12 + t * 4 + cb
        return (dqkv, pl.BlockSpec((None, None, tm, CH), lambda g, i, t: (g, t, i, cb)),
                w4, pl.BlockSpec((None, DM, CH), lambda g, i, t: (chunk(g, t) // per, 0, chunk(g, t) % per)), _NT)

    return _matmul(name, [pair(cb) for cb in range(4)], (g_n, SEQ // tm, 3), jax.ShapeDtypeStruct((g_n, SEQ, DM), F32),
                   pl.BlockSpec((None, tm, DM), lambda g, i, t: (g, i, 0)), (tm, DM), carrying=True, carry=carry)


def _qkv_bwd_dw(name, ht_all, dqkv, shard_cols):
    g_n = dqkv.shape[0]
    per = shard_cols // CH
    return _matmul(
        name, [(ht_all, pl.BlockSpec((None, DM, SEQ), lambda qq, k: (qq // 12, 0, 0)),
                dqkv, pl.BlockSpec((None, None, SEQ, CH), lambda qq, k: (qq // 12, (qq % 12) // 4, 0, qq % 4)), _NN)],
        (g_n * 12, 1), jax.ShapeDtypeStruct((NCHIP, DM, shard_cols), BF16),
        pl.BlockSpec((None, DM, CH), lambda qq, k: (qq // per, 0, qq % per)), None)


def _proj_fwd(name, o, wo, x, g):
    tm = 512

    def body(o_ref, w_ref, x_ref, g_ref, xn_ref, u_ref):
        u = jnp.dot(o_ref[...], w_ref[...], preferred_element_type=F32)
        u_ref[...] = u
        r = lax.rsqrt(jnp.mean(u * u, axis=-1, keepdims=True) + RMS_EPS)
        xn_ref[...] = x_ref[...] + u * r * g_ref[...]

    rows = pl.BlockSpec((tm, DM), lambda i: (i, 0))
    sh = jax.ShapeDtypeStruct((SEQ, DM), F32)
    return pl.pallas_call(
        body, grid=(SEQ // tm,), in_specs=[rows, pl.BlockSpec((DM, DM), lambda i: (0, 0)), rows, pl.BlockSpec((1, DM), lambda i: (0, 0))],
        out_specs=[rows, rows], out_shape=[sh, sh], compiler_params=_params(("parallel",)), name=name)(o, wo, x, g)


def _proj_bwd(name, dy, u, g, wo, dtype):
    tm = 512

    def body(dy_ref, u_ref, g_ref, w_ref, do_ref, du_ref, dg_ref):
        dy = dy_ref[...]
        u = u_ref[...]
        r = lax.rsqrt(jnp.mean(u * u, axis=-1, keepdims=True) + RMS_EPS)
        yh = u * r
        t = dy * g_ref[...]
        du = (r * (t - yh * jnp.mean(t * yh, axis=-1, keepdims=True))).astype(BF16)
        du_ref[...] = du
        do_ref[...] = lax.dot_general(du, w_ref[...], _NT, preferred_element_type=F32).astype(do_ref.dtype)

        @pl.when(pl.program_id(0) == 0)
        def _():
            dg_ref[...] = jnp.zeros_like(dg_ref)

        dg_ref[...] += jnp.sum(dy * yh, axis=0, keepdims=True)

    rows = pl.BlockSpec((tm, DM), lambda i: (i, 0))
    vec = pl.BlockSpec((1, DM), lambda i: (0, 0))
    return pl.pallas_call(
        body, grid=(SEQ // tm,), in_specs=[rows, rows, vec, pl.BlockSpec((DM, DM), lambda i: (0, 0))], out_specs=[rows, rows, vec],
        out_shape=[jax.ShapeDtypeStruct((SEQ, DM), dtype), jax.ShapeDtypeStruct((SEQ, DM), BF16), jax.ShapeDtypeStruct((1, DM), F32)],
        compiler_params=_params(("arbitrary",)), name=name)(dy, u, g, wo)


def _proj_bwd_dw(name, o, du):
    tn = 512
    return _matmul(
        name, [(o, pl.BlockSpec((SEQ, DM), lambda j, k: (0, 0)), du, pl.BlockSpec((SEQ, tn), lambda j, k: (0, j)), _TN)],
        (DM // tn, 1), jax.ShapeDtypeStruct((DM, DM), BF16), pl.BlockSpec((DM, tn), lambda j, k: (0, j)), None)


def _ffn_wspec(index_map):
    return pl.BlockSpec((None, FSH, DM), index_map)


def _ffn_bwd_dw(name, a4, b):
    return _matmul(
        name, [(a4, pl.BlockSpec((None, SEQ, FSH), lambda s, k: (s, 0, 0)), b, pl.BlockSpec((SEQ, DM), lambda s, k: (0, 0)), _TN)],
        (NCHIP, 1), jax.ShapeDtypeStruct((NCHIP, FSH, DM), BF16), _ffn_wspec(lambda s, k: (s, 0, 0)), None)


ROWS = 256


def _row_spec():
    return pl.BlockSpec((ROWS, DM), lambda i: (i, 0))


def _vec_spec():
    return pl.BlockSpec((1, DM), lambda i: (0, 0))


def _rms_fwd(name, x, g, dtype=BF16):
    def body(x_ref, g_ref, o_ref):
        x = x_ref[...]
        r = lax.rsqrt(jnp.mean(x * x, axis=-1, keepdims=True) + RMS_EPS)
        o_ref[...] = (x * r * g_ref[...]).astype(o_ref.dtype)

    return pl.pallas_call(body, grid=(SEQ // ROWS,), in_specs=[_row_spec(), _vec_spec()], out_specs=_row_spec(),
                          out_shape=jax.ShapeDtypeStruct((SEQ, DM), dtype), compiler_params=_params(("parallel",)), name=name)(x, g)


def _rms_fwd_both(name, x, g):
    def body(x_ref, g_ref, o_ref, t_ref):
        x = x_ref[...]
        r = lax.rsqrt(jnp.mean(x * x, axis=-1, keepdims=True) + RMS_EPS)
        h = x * r * g_ref[...]
        o_ref[...] = h.astype(o_ref.dtype)
        t_ref[...] = h.T.astype(t_ref.dtype)

    return pl.pallas_call(
        body, grid=(SEQ // ROWS,), in_specs=[_row_spec(), _vec_spec()], out_specs=[_row_spec(), pl.BlockSpec((DM, ROWS), lambda i: (0, i))],
        out_shape=[jax.ShapeDtypeStruct((SEQ, DM), BF16), jax.ShapeDtypeStruct((DM, SEQ), BF16)],
        compiler_params=_params(("parallel",)), name=name)(x, g)


def _norm_bwd(name, dys, u, g, res=None):
    ndy = len(dys)

    def body(*refs):
        dy = refs[0][...]
        for r_ in refs[1:ndy]:
            dy = dy + r_[...]
        u_ref, g_ref = refs[ndy], refs[ndy + 1]
        res_ref = refs[ndy + 2] if res is not None else None
        du_ref, dg_ref = refs[-2], refs[-1]
        u = u_ref[...]
        r = lax.rsqrt(jnp.mean(u * u, axis=-1, keepdims=True) + RMS_EPS)
        yh = u * r
        t = dy * g_ref[...]
        du = r * (t - yh * jnp.mean(t * yh, axis=-1, keepdims=True))
        if res_ref is not None:
            du = du + res_ref[...]
        du_ref[...] = du

        @pl.when(pl.program_id(0) == 0)
        def _():
            dg_ref[...] = jnp.zeros_like(dg_ref)

        dg_ref[...] += jnp.sum(dy * yh, axis=0, keepdims=True)

    ops = list(dys) + [u, g] + ([res] if res is not None else [])
    specs = [_row_spec()] * ndy + [_row_spec(), _vec_spec()] + ([_row_spec()] if res is not None else [])
    return pl.pallas_call(
        body, grid=(SEQ // ROWS,), in_specs=specs, out_specs=[_row_spec(), _vec_spec()],
        out_shape=[jax.ShapeDtypeStruct((SEQ, DM), F32), jax.ShapeDtypeStruct((1, DM), F32)],
        compiler_params=_params(("arbitrary",)), name=name)(*ops)


def _loss_grad(name, y, t):
    def body(y_ref, t_ref, dy_ref, l_ref):
        e = y_ref[...] - t_ref[...]
        dy_ref[...] = e * (1.0 / DM)

        @pl.when(pl.program_id(0) == 0)
        def _():
            l_ref[...] = jnp.zeros_like(l_ref)

        l_ref[...] += jnp.sum(e * e) * (0.5 / DM)

    return pl.pallas_call(
        body, grid=(SEQ // ROWS,), in_specs=[_row_spec(), _row_spec()],
        out_specs=[_row_spec(), pl.BlockSpec((1, 128), lambda i: (0, 0))],
        out_shape=[jax.ShapeDtypeStruct((SEQ, DM), F32), jax.ShapeDtypeStruct((1, 128), F32)],
        compiler_params=_params(("arbitrary",)), name=name)(y, t)


HBM_SPEC = pl.BlockSpec(memory_space=pltpu.HBM)


class _Carried:
    def __init__(self, ins, out_shapes, n_sems, issue, drain):
        self.ins, self.out_shapes, self.n_sems, self.issue, self.drain = list(ins), list(out_shapes), tuple(n_sems), issue, drain


def _carrier_call(name, body, grid, in_specs, out_specs, out_shape, scratch_shapes, operands, carry):
    n_in, n_out, n_scr = len(in_specs), len(out_specs), len(scratch_shapes)
    if carry is None:
        res = pl.pallas_call(body, grid=grid, in_specs=in_specs, out_specs=out_specs, out_shape=out_shape, scratch_shapes=scratch_shapes,
                             compiler_params=_params(("arbitrary",) * len(grid)), name=name)(*operands)
        return list(res), []
    ci, co = len(carry.ins), len(carry.out_shapes)

    def wrapped(*refs):
        ins, cins = refs[:n_in], refs[n_in:n_in + ci]
        outs, couts = refs[n_in + ci:n_in + ci + n_out], refs[n_in + ci + n_out:n_in + ci + n_out + co]
        scr, sems = refs[n_in + ci + n_out + co:n_in + ci + n_out + co + n_scr], refs[n_in + ci + n_out + co + n_scr:]
        first = functools.reduce(jnp.logical_and, [pl.program_id(a) == 0 for a in range(len(grid))])
        last = functools.reduce(jnp.logical_and, [pl.program_id(a) == grid[a] - 1 for a in range(len(grid))])

        @pl.when(first)
        def _():
            carry.issue(cins, couts, sems)

        body(*ins, *outs, *scr)

        @pl.when(last)
        def _():
            carry.drain(cins, couts, sems)

    res = pl.pallas_call(
        wrapped, grid=grid, in_specs=list(in_specs) + [HBM_SPEC] * ci, out_specs=list(out_specs) + [HBM_SPEC] * co,
        out_shape=list(out_shape) + carry.out_shapes,
        scratch_shapes=list(scratch_shapes) + [pltpu.SemaphoreType.DMA((k,)) for k in carry.n_sems],
        compiler_params=pltpu.CompilerParams(dimension_semantics=("arbitrary",) * len(grid), vmem_limit_bytes=VMEM_LIMIT, has_side_effects=True),
        name=name)(*operands, *carry.ins)
    return list(res[:n_out]), list(res[n_out:])


def _run_carried(name, carry):
    def body(*refs):
        ci, co = len(carry.ins), len(carry.out_shapes)
        carry.issue(refs[:ci], refs[ci:ci + co], refs[ci + co:])
        carry.drain(refs[:ci], refs[ci:ci + co], refs[ci + co:])

    return pl.pallas_call(
        body, in_specs=[HBM_SPEC] * len(carry.ins), out_specs=[HBM_SPEC] * len(carry.out_shapes), out_shape=carry.out_shapes,
        scratch_shapes=[pltpu.SemaphoreType.DMA((k,)) for k in carry.n_sems],
        compiler_params=pltpu.CompilerParams(has_side_effects=True), name=name)(*carry.ins)


NA_BLOCKS = SEQ // NA_QB
NA_ROWS_TOTAL = SEQ // GRID_W
NA_CLASSES = ((0, 0), (8, 4), (NA_ROWS_TOTAL - NA_QROWS, NA_ROWS_TOTAL - NA_WROWS))


def _na_pairs(i0, ws):
    out = []
    for qi in range(NA_QROWS):
        i = i0 + qi
        rs = min(max(i - 4, 0), NA_ROWS_TOTAL - 8)
        for kr in range(NA_WROWS):
            r = ws + kr
            if rs <= r < rs + 8:
                out.append((qi, kr, r - i + 7))
    return out


def _diag_onehot():
    qc, kc = np.meshgrid(np.arange(GRID_W), np.arange(GRID_W), indexing="ij")
    e = np.zeros((GRID_W * GRID_W, 128), np.float32)
    j = (kc - qc + 15).reshape(-1)
    ok = (j >= 0) & (j <= 30)
    e[np.arange(GRID_W * GRID_W)[ok], j[ok]] = 1.0
    return jnp.asarray(e)


def _rpb_expand(rpb):
    r2 = jnp.pad(rpb.reshape(NH * 15, 31), ((0, 0), (0, 128 - 31)))

    def body(r_ref, e_ref, o_ref):
        o_ref[...] = lax.dot_general(r_ref[...], e_ref[...], _NT, preferred_element_type=F32, precision=lax.Precision.HIGHEST)

    out = pl.pallas_call(body, out_shape=jax.ShapeDtypeStruct((NH * 15, GRID_W * GRID_W), F32), name="rpb_expand",
                         compiler_params=pltpu.CompilerParams(vmem_limit_bytes=VMEM_LIMIT))(r2, _diag_onehot())
    return out.reshape(NH, 15, GRID_W, GRID_W)


def _na_bias_tiles(rpb):
    col = np.arange(GRID_W)
    col_start = np.clip(col - 8, 0, GRID_W - 16)
    col_mask = (col[None, :] >= col_start[:, None]) & (col[None, :] < col_start[:, None] + 16)
    rc = jnp.where(col_mask[None, None], _rpb_expand(rpb), NEG)
    neg = jnp.full((NH, GRID_W, GRID_W), NEG, F32)
    tiles = []
    for i0, ws in NA_CLASSES:
        pairs = {(qi, kr): dr for qi, kr, dr in _na_pairs(i0, ws)}
        rows = [jnp.concatenate([rc[:, pairs[(qi, kr)]] if (qi, kr) in pairs else neg for kr in range(NA_WROWS)], axis=2)
                for qi in range(NA_QROWS)]
        tiles.append(jnp.concatenate(rows, axis=1))
    return jnp.stack(tiles)


def _na_cls(b):
    return jnp.where(b == 0, 0, jnp.where(b == NA_BLOCKS - 1, 2, 1))


def _na_start(b):
    return pl.multiple_of(jnp.clip(b * NA_QROWS - 4, 0, NA_ROWS_TOTAL - NA_WROWS) * GRID_W, GRID_W)


HPS = 4
LW = HPS * HD
NLW = DM // LW


NA_BWD_HPS = 4


def _na_in_specs(hps=HPS):
    lw = hps * HD
    nlw = DM // lw
    return [pl.BlockSpec((NA_QB, lw), lambda hp, b: (b, hp)),
            pl.BlockSpec((SEQ, lw), lambda hp, b: (0, nlw + hp)),
            pl.BlockSpec((SEQ, lw), lambda hp, b: (0, 2 * nlw + hp)),
            pl.BlockSpec((None, hps, NA_QB, NA_WIN), lambda hp, b: (_na_cls(b), hp, 0, 0))]


def _na_fwd(qkv, bias, carry):
    def body(q_ref, k_ref, v_ref, b_ref, o_ref):
        start = _na_start(pl.program_id(1))
        q = q_ref[...]
        kw = k_ref[pl.ds(start, NA_WIN), :]
        vw = v_ref[pl.ds(start, NA_WIN), :]
        outs = []
        for hh in range(HPS):
            sl = slice(hh * HD, (hh + 1) * HD)
            s = lax.dot_general(q[:, sl] * QSCALE, kw[:, sl], _NT, preferred_element_type=F32) + b_ref[hh]
            p = jnp.exp(s - jnp.max(s, axis=-1, keepdims=True))
            l = jnp.sum(p, axis=-1, keepdims=True)
            outs.append(jnp.dot(p.astype(BF16), vw[:, sl], preferred_element_type=F32) / l)
        o_ref[...] = jnp.concatenate(outs, axis=1).astype(o_ref.dtype)

    (o,), sent = _carrier_call(
        "na_fwd", body, (NLW, NA_BLOCKS), _na_in_specs(), [pl.BlockSpec((NA_QB, LW), lambda hp, b: (b, hp))],
        [jax.ShapeDtypeStruct((SEQ, DM), BF16)], [], (qkv, qkv, qkv, bias), carry)
    return o, sent


def _na_bwd(qkv, bias, do, carry):
    lw = NA_BWD_HPS * HD

    def body(q_ref, k_ref, v_ref, b_ref, do_ref, dqkv_ref, z_ref, dk_acc, dv_acc):
        blk = pl.program_id(1)

        @pl.when(blk == 0)
        def _():
            dk_acc[...] = jnp.zeros_like(dk_acc)
            dv_acc[...] = jnp.zeros_like(dv_acc)
            z_ref[...] = jnp.zeros_like(z_ref)

        start = _na_start(blk)
        q = q_ref[...]
        do = do_ref[...]
        kw = k_ref[pl.ds(start, NA_WIN), :]
        vw = v_ref[pl.ds(start, NA_WIN), :]
        dqs, dks, dvs, dss = [], [], [], []
        for hh in range(NA_BWD_HPS):
            sl = slice(hh * HD, (hh + 1) * HD)
            qh = q[:, sl] * QSCALE
            s = lax.dot_general(qh, kw[:, sl], _NT, preferred_element_type=F32) + b_ref[hh]
            p = jnp.exp(s - jnp.max(s, axis=-1, keepdims=True))
            p = p / jnp.sum(p, axis=-1, keepdims=True)
            dp = lax.dot_general(do[:, sl], vw[:, sl], _NT, preferred_element_type=F32)
            ds = p * (dp - jnp.sum(p * dp, axis=-1, keepdims=True))
            dsb = ds.astype(BF16)
            dqs.append(jnp.dot(dsb, kw[:, sl], preferred_element_type=F32) * QSCALE)
            dks.append(lax.dot_general(dsb, qh, _TN, preferred_element_type=F32))
            dvs.append(lax.dot_general(p.astype(BF16), do[:, sl], _TN, preferred_element_type=F32))
            dss.append(ds)
        for cls, (i0, ws) in enumerate(NA_CLASSES):
            @pl.when(_na_cls(blk) == cls)
            def _(i0=i0, ws=ws):
                for hh, ds in enumerate(dss):
                    for qi, kr, dr in _na_pairs(i0, ws):
                        z_ref[hh, dr * GRID_W:(dr + 1) * GRID_W, :] += ds[qi * GRID_W:(qi + 1) * GRID_W, kr * GRID_W:(kr + 1) * GRID_W]
        dqkv_ref[0, pl.ds(pl.multiple_of(blk * NA_QB, NA_QB), NA_QB), :] = jnp.concatenate(dqs, axis=1).astype(dqkv_ref.dtype)
        dk_acc[pl.ds(start, NA_WIN), :] += jnp.concatenate(dks, axis=1)
        dv_acc[pl.ds(start, NA_WIN), :] += jnp.concatenate(dvs, axis=1)

        @pl.when(blk == NA_BLOCKS - 1)
        def _():
            dqkv_ref[1] = dk_acc[...].astype(dqkv_ref.dtype)
            dqkv_ref[2] = dv_acc[...].astype(dqkv_ref.dtype)

    (dqkv, z), sent = _carrier_call(
        "na_bwd", body, (NH // NA_BWD_HPS, NA_BLOCKS),
        _na_in_specs(NA_BWD_HPS) + [pl.BlockSpec((NA_QB, lw), lambda hp, b: (b, hp))],
        [pl.BlockSpec((3, SEQ, lw), lambda hp, b: (0, 0, hp)), pl.BlockSpec((NA_BWD_HPS, 15 * GRID_W, GRID_W), lambda hp, b: (hp, 0, 0))],
        [jax.ShapeDtypeStruct((3, SEQ, DM), BF16), jax.ShapeDtypeStruct((NH, 15 * GRID_W, GRID_W), F32)],
        [pltpu.VMEM((SEQ, lw), F32), pltpu.VMEM((SEQ, lw), F32)], (qkv, qkv, qkv, bias, do), carry)
    return dqkv, z, sent


def _rpb_grad(z):
    z2 = z.reshape(NH * 15, GRID_W * GRID_W)

    def body(z_ref, e_ref, o_ref):
        o_ref[...] = jnp.dot(z_ref[...], e_ref[...], preferred_element_type=F32, precision=lax.Precision.HIGHEST)

    out = pl.pallas_call(body, out_shape=jax.ShapeDtypeStruct((NH * 15, 128), F32), name="rpb_grad",
                         compiler_params=pltpu.CompilerParams(vmem_limit_bytes=VMEM_LIMIT))(z2, _diag_onehot())
    return out[:, :31].reshape(NH, 15, 31)


DIL_BLOCKS = SEQ // DIL_QB
DIL_HPS = 8
DIL_LW = DIL_HPS * HD
DIL_NLW = DM // DIL_LW


COLS = 128


def _col_spec():
    return pl.BlockSpec((SEQ, COLS), lambda j: (0, j))


def _grp_spec():
    return pl.BlockSpec((3, SEQ, COLS), lambda j: (0, 0, j))


def _store_group_order(dst_ref, src_ref):
    for g, d in enumerate(DIL):
        n = SEQ // d
        for r in range(d):
            dst_ref[g, r * n:(r + 1) * n, :] = src_ref[pl.ds(r, n, stride=d), :].astype(dst_ref.dtype)


def _store_token_order(dst_ref, src_ref, g):
    d = DIL[g]
    n = SEQ // d
    for r in range(d):
        dst_ref[pl.ds(r, n, stride=d), :] = src_ref[g, r * n:(r + 1) * n, :]


def _to_groups(name, a):
    def body(a_ref, o_ref, t_ref):
        _store_group_order(o_ref, a_ref)
        for g in range(3):
            t_ref[g] = o_ref[g].astype(F32).T.astype(t_ref.dtype)

    return pl.pallas_call(
        body, grid=(DM // COLS,), in_specs=[_col_spec()], out_specs=[_grp_spec(), pl.BlockSpec((3, COLS, SEQ), lambda j: (0, j, 0))],
        out_shape=[jax.ShapeDtypeStruct((3, SEQ, DM), BF16), jax.ShapeDtypeStruct((3, DM, SEQ), BF16)],
        compiler_params=_params(("parallel",)), name=name)(a)


def _from_groups_sum(name, a):
    def body(a_ref, o_ref, t1, t2):
        _store_token_order(t1, a_ref, 1)
        _store_token_order(t2, a_ref, 2)
        o_ref[...] = (a_ref[0] + t1[...]) + t2[...]

    return pl.pallas_call(body, grid=(DM // COLS,), in_specs=[_grp_spec()], out_specs=_col_spec(),
                          out_shape=jax.ShapeDtypeStruct((SEQ, DM), F32), scratch_shapes=[pltpu.VMEM((SEQ, COLS), F32)] * 2,
                          compiler_params=_params(("parallel",)), name=name)(a)


def _dil_start(b):
    return pl.multiple_of(jnp.clip(b * DIL_QB - DIL_RADIUS, 0, SEQ - DIL_WIN), DIL_RADIUS)


def _dil_mask(g, b, start):
    shift = 11 - 2 * g
    ii = b * DIL_QB + lax.broadcasted_iota(jnp.int32, (DIL_QB, DIL_WIN), 0)
    jj = start + lax.broadcasted_iota(jnp.int32, (DIL_QB, DIL_WIN), 1)
    dist = jnp.abs(ii - jj)
    valid = (dist <= DIL_RADIUS) & (jnp.right_shift(ii, shift) == jnp.right_shift(jj, shift))
    return valid, dist.astype(F32)


def _dil_in_specs():
    return [pl.BlockSpec(memory_space=pltpu.SMEM),
            pl.BlockSpec((None, DIL_QB, DIL_LW), lambda g, hp, b: (g, b, hp)),
            pl.BlockSpec((None, SEQ, DIL_LW), lambda g, hp, b: (g, 0, DIL_NLW + hp)),
            pl.BlockSpec((None, SEQ, DIL_LW), lambda g, hp, b: (g, 0, 2 * DIL_NLW + hp))]


def _dil_fwd(qkv, slopes, carry):
    def body(sl_ref, q_ref, k_ref, v_ref, o_ref, lse_ref):
        g, hp, b = pl.program_id(0), pl.program_id(1), pl.program_id(2)
        start = _dil_start(b)
        valid, dist = _dil_mask(g, b, start)
        dil = jnp.left_shift(1, 2 * g).astype(F32)
        q = q_ref[...]
        kw = k_ref[pl.ds(start, DIL_WIN), :]
        vw = v_ref[pl.ds(start, DIL_WIN), :]
        outs, lses = [], []
        for hh in range(DIL_HPS):
            sl = slice(hh * HD, (hh + 1) * HD)
            s = lax.dot_general(q[:, sl] * QSCALE, kw[:, sl], _NT, preferred_element_type=F32)
            s = jnp.where(valid, s - (sl_ref[hp * DIL_HPS + hh] * dil) * dist, NEG)
            m = jnp.max(s, axis=-1, keepdims=True)
            p = jnp.exp(s - m)
            l = jnp.sum(p, axis=-1, keepdims=True)
            outs.append(jnp.dot(p.astype(BF16), vw[:, sl], preferred_element_type=F32) / l)
            lses.append(jnp.broadcast_to(m + jnp.log(l), (DIL_QB, HD)))
        o_ref[...] = jnp.concatenate(outs, axis=1)
        lse_ref[...] = jnp.concatenate(lses, axis=1)

    ospec = pl.BlockSpec((None, DIL_QB, DIL_LW), lambda g, hp, b: (g, b, hp))
    sh = jax.ShapeDtypeStruct((3, SEQ, DM), F32)
    (o, lse), sent = _carrier_call("dil_fwd", body, (3, DIL_NLW, DIL_BLOCKS), _dil_in_specs(), [ospec, ospec], [sh, sh], [],
                                   (slopes, qkv, qkv, qkv), carry)
    return o, lse, sent


def _dil_merge(o_all, lse_all):
    def body(o_ref, l_ref, out_ref, lse_ref, o1, o2, l1, l2):
        for g, (ot, lt) in ((1, (o1, l1)), (2, (o2, l2))):
            _store_token_order(ot, o_ref, g)
            _store_token_order(lt, l_ref, g)
        la, lb, lc = l_ref[0], l1[...], l2[...]
        m = jnp.maximum(jnp.maximum(la, lb), lc)
        wa, wb, wc = jnp.exp(la - m), jnp.exp(lb - m), jnp.exp(lc - m)
        sw = (wa + wb) + wc
        out_ref[...] = (((wa * o_ref[0] + wb * o1[...]) + wc * o2[...]) / sw).astype(out_ref.dtype)
        lse_ref[...] = m + jnp.log(sw)

    return pl.pallas_call(
        body, grid=(DM // COLS,), in_specs=[_grp_spec(), _grp_spec()], out_specs=[_col_spec(), _col_spec()],
        out_shape=[jax.ShapeDtypeStruct((SEQ, DM), BF16), jax.ShapeDtypeStruct((SEQ, DM), F32)],
        scratch_shapes=[pltpu.VMEM((SEQ, COLS), F32)] * 4, compiler_params=_params(("parallel",)), name="dil_merge")(o_all, lse_all)


def _dil_bwd_prep(do, o, lse):
    heads = COLS // HD

    def body(do_ref, o_ref, lse_ref, dog_ref, ddr_ref, lser_ref, dd, grp):
        prod = do_ref[...] * o_ref[...].astype(F32)
        dd[...] = jnp.concatenate(
            [jnp.broadcast_to(jnp.sum(prod[:, h * HD:(h + 1) * HD], axis=-1, keepdims=True), (SEQ, HD)) for h in range(heads)], axis=1)
        _store_group_order(dog_ref, do_ref)
        for src, dst in ((dd, ddr_ref), (lse_ref, lser_ref)):
            _store_group_order(grp, src)
            for g in range(3):
                t = grp[g].T
                for h in range(heads):
                    dst[g, h] = t[h * HD:h * HD + 8, :]

    rows = jax.ShapeDtypeStruct((3, NH, 8, SEQ), F32)
    rspec = pl.BlockSpec((3, heads, 8, SEQ), lambda j: (0, j, 0, 0))
    return pl.pallas_call(
        body, grid=(DM // COLS,), in_specs=[_col_spec()] * 3, out_specs=[_grp_spec(), rspec, rspec],
        out_shape=[jax.ShapeDtypeStruct((3, SEQ, DM), BF16), rows, rows],
        scratch_shapes=[pltpu.VMEM((SEQ, COLS), F32), pltpu.VMEM((3, SEQ, COLS), F32)],
        compiler_params=_params(("parallel",)), name="dil_bwd_prep")(do, o, lse)


def _dil_bwd(qkv, do, dd, lse, slopes, carry):
    def body(sl_ref, q_ref, k_ref, v_ref, do_ref, dd_ref, lse_ref, dqkv_ref, dk_acc, dv_acc):
        g, hp, b = pl.program_id(0), pl.program_id(1), pl.program_id(2)

        @pl.when(b == 0)
        def _():
            dk_acc[...] = jnp.zeros_like(dk_acc)
            dv_acc[...] = jnp.zeros_like(dv_acc)

        start = _dil_start(b)
        shift = 11 - 2 * g
        jj = start + lax.broadcasted_iota(jnp.int32, (DIL_WIN, DIL_QB), 0)
        ii = b * DIL_QB + lax.broadcasted_iota(jnp.int32, (DIL_WIN, DIL_QB), 1)
        dist = jnp.abs(ii - jj)
        valid = (dist <= DIL_RADIUS) & (jnp.right_shift(ii, shift) == jnp.right_shift(jj, shift))
        dist = dist.astype(F32)
        dil = jnp.left_shift(1, 2 * g).astype(F32)
        q = q_ref[...]
        do = do_ref[...]
        kw = k_ref[pl.ds(start, DIL_WIN), :]
        vw = v_ref[pl.ds(start, DIL_WIN), :]
        dqs, dks, dvs = [], [], []
        for hh in range(DIL_HPS):
            sl = slice(hh * HD, (hh + 1) * HD)
            qh = q[:, sl] * QSCALE
            st = lax.dot_general(kw[:, sl], qh, _NT, preferred_element_type=F32)
            st = jnp.where(valid, st - (sl_ref[hp * DIL_HPS + hh] * dil) * dist, NEG)
            pt = jnp.exp(st - lse_ref[hh, 0:1, :])
            dpt = lax.dot_general(vw[:, sl], do[:, sl], _NT, preferred_element_type=F32)
            dst = (pt * (dpt - dd_ref[hh, 0:1, :])).astype(BF16)
            dqs.append(lax.dot_general(kw[:, sl], dst, _TN, preferred_element_type=F32).T * QSCALE)
            dks.append(jnp.dot(dst, qh, preferred_element_type=F32))
            dvs.append(jnp.dot(pt.astype(BF16), do[:, sl], preferred_element_type=F32))
        dqkv_ref[0, pl.ds(pl.multiple_of(b * DIL_QB, DIL_QB), DIL_QB), :] = jnp.concatenate(dqs, axis=1).astype(dqkv_ref.dtype)
        dk_acc[pl.ds(start, DIL_WIN), :] += jnp.concatenate(dks, axis=1)
        dv_acc[pl.ds(start, DIL_WIN), :] += jnp.concatenate(dvs, axis=1)

        @pl.when(b == DIL_BLOCKS - 1)
        def _():
            dqkv_ref[1] = dk_acc[...].astype(dqkv_ref.dtype)
            dqkv_ref[2] = dv_acc[...].astype(dqkv_ref.dtype)

    qspec = pl.BlockSpec((None, DIL_QB, DIL_LW), lambda g, hp, b: (g, b, hp))
    rspec = pl.BlockSpec((None, DIL_HPS, 8, DIL_QB), lambda g, hp, b: (g, hp, 0, b))
    (dqkv,), sent = _carrier_call(
        "dil_bwd", body, (3, DIL_NLW, DIL_BLOCKS), _dil_in_specs() + [qspec, rspec, rspec],
        [pl.BlockSpec((None, 3, SEQ, DIL_LW), lambda g, hp, b: (g, 0, 0, hp))], [jax.ShapeDtypeStruct((3, 3, SEQ, DM), BF16)],
        [pltpu.VMEM((SEQ, DIL_LW), F32), pltpu.VMEM((SEQ, DIL_LW), F32)], (slopes, qkv, qkv, qkv, do, dd, lse), carry)
    return dqkv, sent


def _ffn_fwd(name, x, g_pre, g_post, wgt4, wut4, wd4, carry):
    tm = 512

    def body(x_ref, gpre_ref, gpost_ref, wg_ref, wu_ref, wd_ref, xn_ref, h_ref, gate_ref, up_ref, u_ref, acc):
        s = pl.program_id(1)

        @pl.when(s == 0)
        def _():
            x = x_ref[...]
            r = lax.rsqrt(jnp.mean(x * x, axis=-1, keepdims=True) + RMS_EPS)
            h_ref[...] = (x * r * gpre_ref[...]).astype(h_ref.dtype)

        h = h_ref[...]
        gate = lax.dot_general(h, wg_ref[...], _NT, preferred_element_type=F32).astype(BF16)
        up = lax.dot_general(h, wu_ref[...], _NT, preferred_element_type=F32).astype(BF16)
        gate_ref[...] = gate
        up_ref[...] = up
        gf = gate.astype(F32)
        act = (gf * jax.nn.sigmoid(gf) * up.astype(F32)).astype(BF16)
        part = jnp.dot(act, wd_ref[...], preferred_element_type=F32)

        @pl.when(s == 0)
        def _():
            acc[...] = part

        @pl.when(s > 0)
        def _():
            acc[...] += part

        @pl.when(s == NCHIP - 1)
        def _():
            u = acc[...]
            u_ref[...] = u
            r = lax.rsqrt(jnp.mean(u * u, axis=-1, keepdims=True) + RMS_EPS)
            xn_ref[...] = x_ref[...] + u * r * gpost_ref[...]

    rows = pl.BlockSpec((tm, DM), lambda i, s: (i, 0))
    vec = pl.BlockSpec((1, DM), lambda i, s: (0, 0))
    wspec = _ffn_wspec(lambda i, s: (s, 0, 0))
    mid = pl.BlockSpec((None, tm, FSH), lambda i, s: (s, i, 0))
    outs, sent = _carrier_call(
        name, body, (SEQ // tm, NCHIP), [rows, vec, vec, wspec, wspec, wspec], [rows, rows, mid, mid, rows],
        [jax.ShapeDtypeStruct((SEQ, DM), F32), jax.ShapeDtypeStruct((SEQ, DM), BF16), jax.ShapeDtypeStruct((NCHIP, SEQ, FSH), BF16),
         jax.ShapeDtypeStruct((NCHIP, SEQ, FSH), BF16), jax.ShapeDtypeStruct((SEQ, DM), F32)],
        [pltpu.VMEM((tm, DM), F32)], (x, g_pre, g_post, wgt4, wut4, wd4), carry)
    return outs, sent


def _ffn_block(layer, x, g_pre, g_post, ex):
    tag = f"l{layer}_ffn_fwd"
    (x_new, h, gate, up, u), sent = _ffn_fwd(tag, x, g_pre, g_post, ex.weight(("ffn_w_gate", layer)), ex.weight(("ffn_w_up", layer)),
                                             ex.weight(("ffn_w_down", layer)), ex.carry(tag))
    ex.carried(tag, sent)
    return x_new, (x, h, gate, up, u)


def _ffn_bwd(name, dx, x, gate, up, u, g_pre, g_post, wgt4, wut4, wd4, carry):
    tm = 512

    def body(dx_ref, x_ref, gate_ref, up_ref, u_ref, gpre_ref, gpost_ref, wg_ref, wu_ref, wd_ref,
             dxin_ref, du_ref, dgate_ref, dup_ref, act_ref, dgpre_ref, dgpost_ref, dh_acc):
        i, s = pl.program_id(0), pl.program_id(1)

        @pl.when((i == 0) & (s == 0))
        def _():
            dgpre_ref[...] = jnp.zeros_like(dgpre_ref)
            dgpost_ref[...] = jnp.zeros_like(dgpost_ref)

        @pl.when(s == 0)
        def _():
            dy = dx_ref[...]
            uu = u_ref[...]
            r = lax.rsqrt(jnp.mean(uu * uu, axis=-1, keepdims=True) + RMS_EPS)
            yh = uu * r
            t = dy * gpost_ref[...]
            du_ref[...] = (r * (t - yh * jnp.mean(t * yh, axis=-1, keepdims=True))).astype(du_ref.dtype)
            dgpost_ref[...] += jnp.sum(dy * yh, axis=0, keepdims=True)

        dact = lax.dot_general(du_ref[...], wd_ref[...], _NT, preferred_element_type=F32)
        g = gate_ref[...].astype(F32)
        upv = up_ref[...].astype(F32)
        sg = jax.nn.sigmoid(g)
        dgate = (dact * upv * sg * (1.0 + g * (1.0 - sg))).astype(BF16)
        dup = (dact * g * sg).astype(BF16)
        dgate_ref[...] = dgate
        dup_ref[...] = dup
        act_ref[...] = (g * sg * upv).astype(act_ref.dtype)
        part = jnp.dot(dgate, wg_ref[...], preferred_element_type=F32) + jnp.dot(dup, wu_ref[...], preferred_element_type=F32)

        @pl.when(s == 0)
        def _():
            dh_acc[...] = part

        @pl.when(s > 0)
        def _():
            dh_acc[...] += part

        @pl.when(s == NCHIP - 1)
        def _():
            dh = dh_acc[...]
            xx = x_ref[...]
            r = lax.rsqrt(jnp.mean(xx * xx, axis=-1, keepdims=True) + RMS_EPS)
            yh = xx * r
            t = dh * gpre_ref[...]
            dxin_ref[...] = dx_ref[...] + r * (t - yh * jnp.mean(t * yh, axis=-1, keepdims=True))
            dgpre_ref[...] += jnp.sum(dh * yh, axis=0, keepdims=True)

    rows = pl.BlockSpec((tm, DM), lambda i, s: (i, 0))
    vec = pl.BlockSpec((1, DM), lambda i, s: (0, 0))
    wspec = _ffn_wspec(lambda i, s: (s, 0, 0))
    mid = pl.BlockSpec((None, tm, FSH), lambda i, s: (s, i, 0))
    mid_shape = jax.ShapeDtypeStruct((NCHIP, SEQ, FSH), BF16)
    return _carrier_call(
        name, body, (SEQ // tm, NCHIP), [rows, rows, mid, mid, rows, vec, vec, wspec, wspec, wspec], [rows, rows, mid, mid, mid, vec, vec],
        [jax.ShapeDtypeStruct((SEQ, DM), F32), jax.ShapeDtypeStruct((SEQ, DM), BF16), mid_shape, mid_shape, mid_shape,
         jax.ShapeDtypeStruct((1, DM), F32), jax.ShapeDtypeStruct((1, DM), F32)],
        [pltpu.VMEM((tm, DM), F32)], (dx, x, gate, up, u, g_pre, g_post, wgt4, wut4, wd4), carry)


def _ffn_block_bwd(layer, dx, saved, g_pre, g_post, ex):
    tag = f"l{layer}"
    x, h, gate, up, u = saved
    (dx_in, du, dgate, dup, act, dg_pre, dg_post), sent = _ffn_bwd(
        f"{tag}_ffn_bwd", dx, x, gate, up, u, g_pre, g_post, ex.weight(("ffn_w_gate", layer)), ex.weight(("ffn_w_up", layer)),
        ex.weight(("ffn_w_down", layer)), ex.carry(f"{tag}_ffn_bwd"))
    ex.carried(f"{tag}_ffn_bwd", sent)
    d_wd = _ffn_bwd_dw(f"{tag}_dwd", act, du)
    d_wg = _ffn_bwd_dw(f"{tag}_dwg", dgate, h)
    d_wu = _ffn_bwd_dw(f"{tag}_dwu", dup, h)
    ex.grads(f"{tag}_ffn", {("ffn_w_gate", layer): d_wg, ("ffn_w_up", layer): d_wu, ("ffn_w_down", layer): d_wd})
    return dx_in, dg_pre, dg_post


def _alibi_slopes():
    return 2.0 ** (-8.0 * jnp.arange(1, NH + 1, dtype=F32) / NH)


def _local_step(x, target, norms, rpb, ex):
    g_mix_pre, g_mix_post, g_ffn_pre, g_ffn_post = norms
    row = lambda a, i: a[i:i + 1]

    bias = _na_bias_tiles(rpb)
    h0, h0t = _rms_fwd_both("l0_mix_pre", x, row(g_mix_pre, 0))
    qkv0, sent = _qkv_fwd("l0_qkv", h0[None], ex.weight(("na_w_qkv", 0)), ex.carry("l0_qkv"))
    ex.carried("l0_qkv", sent)
    o0, sent = _na_fwd(qkv0[0], bias, ex.carry("na_fwd"))
    ex.carried("na_fwd", sent)
    na_wo = ex.weight(("na_w_o", 0)).reshape(DM, DM)
    x1, u0 = _proj_fwd("l0_proj", o0, na_wo, x, row(g_mix_post, 0))
    x2, ffn0 = _ffn_block(0, x1, row(g_ffn_pre, 0), row(g_ffn_post, 0), ex)

    slopes = _alibi_slopes()
    h2g, h2gt = _to_groups("l1_h_groups", _rms_fwd("l1_mix_pre", x2, row(g_mix_pre, 1), F32))
    dil_wqkv = ex.weight(("dil_w_qkv", 0))
    qkv1, sent = _qkv_fwd("l1_qkv", h2g, dil_wqkv, ex.carry("l1_qkv"))
    ex.carried("l1_qkv", sent)
    og, lg, sent = _dil_fwd(qkv1, slopes, ex.carry("dil_fwd"))
    ex.carried("dil_fwd", sent)
    o1, lse = _dil_merge(og, lg)
    dil_wo = ex.weight(("dil_w_o", 0)).reshape(DM, DM)
    x3, u1 = _proj_fwd("l1_proj", o1, dil_wo, x2, row(g_mix_post, 1))
    x4, ffn1 = _ffn_block(1, x3, row(g_ffn_pre, 1), row(g_ffn_post, 1), ex)

    dx4, loss_row = _loss_grad("loss", x4, target)

    dx3, dg_fpre1, dg_fpost1 = _ffn_block_bwd(1, dx4, ffn1, row(g_ffn_pre, 1), row(g_ffn_post, 1), ex)
    do1, du1, dg_mpost1 = _proj_bwd("l1_proj_bwd", dx3, u1, row(g_mix_post, 1), dil_wo, F32)
    d_dil_wo = _proj_bwd_dw("l1_dwo", o1, du1)
    dog, ddg, lseg = _dil_bwd_prep(do1, o1, lse)
    dqkv1, sent = _dil_bwd(qkv1, dog, ddg, lseg, slopes, ex.carry("dil_bwd"))
    ex.carried("dil_bwd", sent)
    d_dil_wqkv = _qkv_bwd_dw("l1_dwqkv", h2gt, dqkv1, dil_wqkv.shape[2])
    ex.grads("l1_mix", {("dil_w_qkv", 0): d_dil_wqkv, ("dil_w_o", 0): d_dil_wo.reshape(NCHIP, DM // NCHIP, DM)})
    dh2g, sent = _qkv_bwd_dh("l1_dh", dqkv1, dil_wqkv, ex.carry("l1_dh"))
    ex.carried("l1_dh", sent)
    dh2 = _from_groups_sum("l1_dh_tokens", dh2g)
    dx2, dg_mpre1 = _norm_bwd("l1_mix_pre_bwd", [dh2], x2, row(g_mix_pre, 1), res=dx3)

    dx1, dg_fpre0, dg_fpost0 = _ffn_block_bwd(0, dx2, ffn0, row(g_ffn_pre, 0), row(g_ffn_post, 0), ex)
    do0, du0, dg_mpost0 = _proj_bwd("l0_proj_bwd", dx1, u0, row(g_mix_post, 0), na_wo, BF16)
    d_na_wo = _proj_bwd_dw("l0_dwo", o0, du0)
    dqkv0, z, sent = _na_bwd(qkv0[0], bias, do0, ex.carry("na_bwd"))
    ex.carried("na_bwd", sent)
    d_rpb = _rpb_grad(z)
    na_wqkv = ex.weight(("na_w_qkv", 0))
    d_na_wqkv = _qkv_bwd_dw("l0_dwqkv", h0t[None], dqkv0[None], na_wqkv.shape[2])
    ex.grads("l0_mix", {("na_w_qkv", 0): d_na_wqkv, ("na_w_o", 0): d_na_wo.reshape(NCHIP, DM // NCHIP, DM)})
    dh0, sent = _qkv_bwd_dh("l0_dh", dqkv0[None], na_wqkv, ex.carry("l0_dh"))
    ex.carried("l0_dh", sent)
    dx0, dg_mpre0 = _norm_bwd("l0_mix_pre_bwd", [dh0[0]], x, row(g_mix_pre, 0), res=dx1)

    dnorms = (jnp.concatenate([dg_mpre0, dg_mpre1]), jnp.concatenate([dg_mpost0, dg_mpost1]),
              jnp.concatenate([dg_fpre0, dg_fpre1]), jnp.concatenate([dg_fpost0, dg_fpost1]))
    return loss_row, dx0, dnorms, d_rpb


def _place():
    x, y, c = lax.axis_index("x"), lax.axis_index("y"), lax.axis_index("c")
    chips = ((1 - x, y), (x, 1 - y), (1 - x, 1 - y))
    return x, y, c, chips


def _chip_id(chip):
    return 2 * chip[0] + chip[1]


def _comm_call(name, body, ins, out_shapes, n_sems, aliases=None):
    return pl.pallas_call(
        body, in_specs=[HBM_SPEC] * len(ins), out_specs=[HBM_SPEC] * len(out_shapes), out_shape=out_shapes,
        scratch_shapes=[pltpu.SemaphoreType.DMA((k,)) for k in n_sems], input_output_aliases=aliases or {},
        compiler_params=pltpu.CompilerParams(has_side_effects=True), name=name)(*ins)


def _gather_copies(shards):
    n = len(shards)

    def copies(src, out, sems):
        send_sems, recv_sems = sems
        x, y, c, chips = _place()

        def copy(t, k, chip, half, to, from_src=False):
            blk = out[t].at[_chip_id(chip), half]
            return pltpu.make_async_remote_copy(
                src_ref=src[t].at[half] if from_src else blk, dst_ref=blk,
                send_sem=send_sems.at[6 * t + k], recv_sem=recv_sems.at[6 * t + k], device_id=to, device_id_type=MESH)

        return copy, x, y, c, chips

    def issue(src, out, sems):
        copy, x, y, c, chips = copies(src, out, sems)
        for t in range(n):
            for j, chip in enumerate(chips):
                copy(t, j, (x, y), c, (*chip, c), from_src=True).start()

    def drain(src, out, sems):
        copy, x, y, c, chips = copies(src, out, sems)
        passed = []
        for t in range(n):
            for j, chip in enumerate(chips):
                copy(t, j, chip, c, (x, y, c)).wait_recv()
                fwd = copy(t, 3 + j, chip, c, (x, y, 1 - c))
                fwd.start()
                passed.append(fwd)
        for t in range(n):
            for j, chip in enumerate(chips):
                copy(t, 3 + j, chip, 1 - c, (x, y, c)).wait_recv()
        for t in range(n):
            for j, chip in enumerate(chips):
                copy(t, j, (x, y), c, (*chip, c), from_src=True).wait_send()
        for cp in passed:
            cp.wait_send()

    return _Carried(shards, [jax.ShapeDtypeStruct((NCHIP,) + s.shape, s.dtype) for s in shards], (6 * n, 6 * n), issue, drain)


def _pair_exchange(name, grads):
    n = len(grads)

    def body(*refs):
        g, theirs = refs[:n], refs[n:2 * n]
        send_sems, recv_sems = refs[2 * n:]
        x, y, c, _ = _place()
        swap = [pltpu.make_async_remote_copy(src_ref=g[t].at[:, 1 - c], dst_ref=theirs[t], send_sem=send_sems.at[t],
                                             recv_sem=recv_sems.at[t], device_id=(x, y, 1 - c), device_id_type=MESH) for t in range(n)]
        for cp in swap:
            cp.start()
        for cp in swap:
            cp.wait()

    return _comm_call(name, body, grads, [jax.ShapeDtypeStruct((NCHIP,) + g.shape[2:], g.dtype) for g in grads], (n, n))


def _chip_exchange_copies(parts):
    n = len(parts)

    def copies(p, slots, sems):
        send_sems, recv_sems = sems
        x, y, c, chips = _place()
        return [pltpu.make_async_remote_copy(src_ref=p[t].at[_chip_id(chips[j])], dst_ref=slots[t].at[j], send_sem=send_sems.at[3 * t + j],
                                             recv_sem=recv_sems.at[3 * t + j], device_id=(*chips[j], c), device_id_type=MESH)
                for t in range(n) for j in range(3)]

    def issue(p, slots, sems):
        for cp in copies(p, slots, sems):
            cp.start()

    def drain(p, slots, sems):
        for cp in copies(p, slots, sems):
            cp.wait()

    return _Carried(parts, [jax.ShapeDtypeStruct((3,) + p.shape[1:], p.dtype) for p in parts], (3 * n, 3 * n), issue, drain)


def _pair_share(full):
    n = len(full)

    def body(*refs):
        buf = refs[n:2 * n]
        send_sems, recv_sems = refs[2 * n:]
        x, y, c, _ = _place()
        sends = [pltpu.make_async_remote_copy(src_ref=buf[t].at[c], dst_ref=buf[t].at[c], send_sem=send_sems.at[t], recv_sem=recv_sems.at[t],
                                              device_id=(x, y, 1 - c), device_id_type=MESH) for t in range(n)]
        for cp in sends:
            cp.start()
        for t in range(n):
            pltpu.make_async_remote_copy(src_ref=buf[t].at[c], dst_ref=buf[t].at[1 - c], send_sem=send_sems.at[t], recv_sem=recv_sems.at[t],
                                         device_id=(x, y, 1 - c), device_id_type=MESH).wait_recv()
        for cp in sends:
            cp.wait_send()

    return _comm_call("grad_pair_share", body, full, [jax.ShapeDtypeStruct(f.shape, f.dtype) for f in full], (n, n),
                      aliases={t: t for t in range(n)})


SMALL_ROWS = 128


def _allreduce_small(v):
    def body(v_ref, o_ref, buf, send_sems, recv_sems):
        x, y, c, _ = _place()
        me = 4 * x + 2 * y + c
        flip = lambda a, f: 1 - a if f else a
        buf[me] = v_ref[...]
        peers = [(flip(x, d >> 2 & 1), flip(y, d >> 1 & 1), flip(c, d & 1)) for d in range(1, 8)]
        sends = [pltpu.make_async_remote_copy(src_ref=v_ref, dst_ref=buf.at[me], send_sem=send_sems.at[i], recv_sem=recv_sems.at[i],
                                              device_id=peer, device_id_type=MESH) for i, peer in enumerate(peers)]
        for cp in sends:
            cp.start()
        for i, (px, py, pc) in enumerate(peers):
            pltpu.make_async_remote_copy(src_ref=v_ref, dst_ref=buf.at[4 * px + 2 * py + pc], send_sem=send_sems.at[i], recv_sem=recv_sems.at[i],
                                         device_id=(px, py, pc), device_id_type=MESH).wait_recv()
        for cp in sends:
            cp.wait_send()
        acc = buf[0]
        for k in range(1, 8):
            acc = acc + buf[k]
        o_ref[...] = acc

    vm = pl.BlockSpec(memory_space=pltpu.VMEM)
    return pl.pallas_call(
        body, in_specs=[vm], out_specs=vm, out_shape=jax.ShapeDtypeStruct((SMALL_ROWS, 128), F32),
        scratch_shapes=[pltpu.VMEM((8, SMALL_ROWS, 128), F32), pltpu.SemaphoreType.DMA((7,)), pltpu.SemaphoreType.DMA((7,))],
        compiler_params=pltpu.CompilerParams(has_side_effects=True), name="allreduce_small")(v)


def _row_block(rows, cols, budget=1 << 20):
    best = 8
    for bm in range(8, rows + 1, 8):
        if rows % bm == 0 and bm * cols * 4 <= budget:
            best = bm
    return best


def _pair_sum(name, place, g, theirs):
    _, m, c = theirs.shape
    bm = _row_block(m, c)

    def body(place_ref, a_ref, b_ref, o_ref):
        o_ref[...] = (a_ref[...].astype(F32) + b_ref[...].astype(F32)).astype(o_ref.dtype)

    spec = pl.BlockSpec((None, bm, c), lambda k, i, pr: (k, i, 0))
    return pl.pallas_call(
        body, out_shape=jax.ShapeDtypeStruct(theirs.shape, BF16),
        grid_spec=pltpu.PrefetchScalarGridSpec(
            num_scalar_prefetch=1, grid=(NCHIP, m // bm),
            in_specs=[pl.BlockSpec((None, None, bm, c), lambda k, i, pr: (k, pr[0], i, 0)), spec], out_specs=spec),
        compiler_params=_params(("parallel", "parallel")), name=name)(place, g, theirs)


def _chip_sum(name, place, parts, slots):
    _, m, c = parts.shape
    bm = _row_block(m, c)

    def body(place_ref, p_ref, s_ref, o_ref):
        s = s_ref[...].astype(F32)
        o_ref[...] = ((p_ref[...].astype(F32) + s[0]) + s[1]) + s[2]

    return pl.pallas_call(
        body, out_shape=jax.ShapeDtypeStruct((2, m, c), F32),
        grid_spec=pltpu.PrefetchScalarGridSpec(
            num_scalar_prefetch=1, grid=(m // bm,),
            in_specs=[pl.BlockSpec((None, bm, c), lambda i, pr: (pr[1], i, 0)), pl.BlockSpec((3, bm, c), lambda i, pr: (0, i, 0))],
            out_specs=pl.BlockSpec((None, bm, c), lambda i, pr: (pr[0], i, 0))),
        compiler_params=_params(("parallel",)), name=name)(place, parts, slots)


def _adamw(name, w, g, m, v, layer=0, into=None):
    lead, rows, cols = w.shape
    bm = _row_block(rows, cols, budget=768 * 1024)
    c1 = 1.0 - ADAM_B1 ** ADAM_STEP
    c2 = 1.0 - ADAM_B2 ** ADAM_STEP

    def body(w_ref, g_ref, m_ref, v_ref, *rest):
        go_ref, d_ref, mo_ref, vo_ref = rest[-4:]
        g = g_ref[...]
        mn = ADAM_B1 * m_ref[...] + (1.0 - ADAM_B1) * g
        vn = ADAM_B2 * v_ref[...] + (1.0 - ADAM_B2) * (g * g)
        go_ref[...] = g
        mo_ref[...] = mn
        vo_ref[...] = vn
        d_ref[...] = -ADAM_LR * ((mn / c1) / (jnp.sqrt(vn / c2) + ADAM_EPS) + ADAM_WD * w_ref[...])

    spec = pl.BlockSpec((None, bm, cols), lambda i: (layer, i, 0))
    sh = jax.ShapeDtypeStruct((lead, rows, cols), F32)
    prev = [] if into is None else list(into)
    return pl.pallas_call(
        body, grid=(rows // bm,), in_specs=[spec, pl.BlockSpec((bm, cols), lambda i: (i, 0)), spec, spec] + [pl.BlockSpec(memory_space=pl.ANY)] * len(prev),
        out_specs=[spec] * 4, out_shape=[sh] * 4, input_output_aliases={4 + k: k for k in range(len(prev))},
        compiler_params=_params(("parallel",)), name=name)(w, g, m, v, *prev)


def _pack_small(norms, rpb):
    flat = jnp.concatenate([a.reshape(-1) for a in norms] + [rpb.reshape(-1)])
    return jnp.pad(flat, (0, SMALL_ROWS * 128 - flat.shape[0])).reshape(SMALL_ROWS, 128)


def _unpack_small(p):
    flat = p.reshape(-1)
    norms = [flat[i * 2 * DM:(i + 1) * 2 * DM].reshape(2, DM) for i in range(4)]
    rpb = flat[8 * DM:8 * DM + NH * 15 * 31].reshape(1, NH, 15, 31)
    return norms, rpb


FFN_NAMES = ("ffn_w_gate", "ffn_w_up", "ffn_w_down")
L0_FFN = tuple((n, 0) for n in FFN_NAMES)
L1_FFN = tuple((n, 1) for n in FFN_NAMES)
NA_KEYS = (("na_w_qkv", 0), ("na_w_o", 0))
DIL_KEYS = (("dil_w_qkv", 0), ("dil_w_o", 0))


class _Exchange:
    GATHERS = {"l0_qkv": L0_FFN[:1], "na_fwd": L0_FFN[1:], "l0_ffn_fwd": DIL_KEYS[:1], "dil_fwd": L1_FFN + DIL_KEYS[1:]}
    EXCHANGES = {"dil_bwd": L1_FFN, "l1_dh": DIL_KEYS[1:], "l0_ffn_bwd": DIL_KEYS[:1], "na_bwd": L0_FFN, "l0_dh": NA_KEYS}

    def __init__(self, shards):
        self.chip = 2 * lax.axis_index("x") + lax.axis_index("y")
        self.place = jnp.stack([lax.axis_index("c"), self.chip]).astype(jnp.int32)
        self.own = {k: s.reshape(2, s.shape[0] // 2, s.shape[1]).astype(BF16) for k, s in shards.items()}
        self.gathered, self.parts, self.full = {}, {}, {}
        self._take(NA_KEYS, _run_carried("gather_first", _gather_copies([self.own[k] for k in NA_KEYS])))

    def _take(self, keys, landed):
        for k, gw in zip(keys, landed):
            self.gathered[k] = lax.dynamic_update_slice(gw, self.own[k][None], (self.chip, 0, 0, 0))

    def _sum(self, keys, slots):
        for k, s in zip(keys, slots):
            self.full[k] = _chip_sum(f"chip_sum_{k[0]}_{k[1]}", self.place, self.parts[k], s)

    def weight(self, key):
        g = self.gathered[key]
        return g.reshape(NCHIP, 2 * g.shape[2], g.shape[3])

    def carry(self, tag):
        if tag in self.GATHERS:
            return _gather_copies([self.own[k] for k in self.GATHERS[tag]])
        if tag in self.EXCHANGES:
            return _chip_exchange_copies([self.parts[k] for k in self.EXCHANGES[tag]])
        return None

    def carried(self, tag, landed):
        if tag in self.GATHERS:
            self._take(self.GATHERS[tag], landed)
        elif tag in self.EXCHANGES:
            self._sum(self.EXCHANGES[tag], landed)

    def grads(self, tag, dw):
        keys = tuple(dw)
        mine = [dw[k].reshape(NCHIP, 2, -1, dw[k].shape[-1]) for k in keys]
        theirs = _pair_exchange(f"grad_pair_exchange_{tag}", mine)
        for k, a, b in zip(keys, mine, theirs):
            self.parts[k] = _pair_sum(f"pair_sum_{k[0]}_{k[1]}", self.place, a, b)

    def finish(self):
        keys = tuple(self.full)
        shared = _pair_share([self.full[k] for k in keys])
        return {k: s.reshape(2 * s.shape[1], s.shape[2]) for k, s in zip(keys, shared)}


def kernel(x, norm_mix_pre, norm_mix_post, norm_ffn_pre, norm_ffn_post, na_w_qkv, na_w_o, na_rpb, dil_w_qkv, dil_w_o, ffn_w_gate, ffn_w_up, ffn_w_down, loss_target, m_norm_mix_pre, m_norm_mix_post, m_norm_ffn_pre, m_norm_ffn_post, m_na_w_qkv, m_na_w_o, m_na_rpb, m_dil_w_qkv, m_dil_w_o, m_ffn_w_gate, m_ffn_w_up, m_ffn_w_down, v_norm_mix_pre, v_norm_mix_post, v_norm_ffn_pre, v_norm_ffn_post, v_na_w_qkv, v_na_w_o, v_na_rpb, v_dil_w_qkv, v_dil_w_o, v_ffn_w_gate, v_ffn_w_up, v_ffn_w_down):
    tr = lambda a: jnp.swapaxes(a, 1, 2)
    weights = {"na_w_qkv": na_w_qkv, "na_w_o": na_w_o, "dil_w_qkv": dil_w_qkv, "dil_w_o": dil_w_o,
               "ffn_w_gate": tr(ffn_w_gate), "ffn_w_up": tr(ffn_w_up), "ffn_w_down": ffn_w_down}
    m_in = {"na_w_qkv": m_na_w_qkv, "na_w_o": m_na_w_o, "dil_w_qkv": m_dil_w_qkv, "dil_w_o": m_dil_w_o,
            "ffn_w_gate": tr(m_ffn_w_gate), "ffn_w_up": tr(m_ffn_w_up), "ffn_w_down": m_ffn_w_down}
    v_in = {"na_w_qkv": v_na_w_qkv, "na_w_o": v_na_w_o, "dil_w_qkv": v_dil_w_qkv, "dil_w_o": v_dil_w_o,
            "ffn_w_gate": tr(v_ffn_w_gate), "ffn_w_up": tr(v_ffn_w_up), "ffn_w_down": v_ffn_w_down}

    ex = _Exchange({(n, l): weights[n][l] for n in weights for l in range(weights[n].shape[0])})
    norms = (norm_mix_pre, norm_mix_post, norm_ffn_pre, norm_ffn_post)
    loss_row, dx, dnorms, d_rpb = _local_step(x[0], loss_target[0], norms, na_rpb[0], ex)
    loss = lax.psum(loss_row[0, 0], ("x", "y", "c"))
    full = ex.finish()
    small = _allreduce_small(_pack_small(dnorms, d_rpb))

    out_g, out_d, out_m, out_v = {}, {}, {}, {}
    for n in weights:
        res = None
        for l in range(weights[n].shape[0]):
            res = _adamw(f"adamw_{n}_{l}", weights[n], full[(n, l)], m_in[n], v_in[n], l, res)
        if n in ("ffn_w_gate", "ffn_w_up"):
            res = [tr(r) for r in res]
        out_g[n], out_d[n], out_m[n], out_v[n] = res
    sm_names = ("norm_mix_pre", "norm_mix_post", "norm_ffn_pre", "norm_ffn_post", "na_rpb")
    sm = _adamw("adamw_small", _pack_small(norms, na_rpb)[None], small,
                _pack_small((m_norm_mix_pre, m_norm_mix_post, m_norm_ffn_pre, m_norm_ffn_post), m_na_rpb)[None],
                _pack_small((v_norm_mix_pre, v_norm_mix_post, v_norm_ffn_pre, v_norm_ffn_post), v_na_rpb)[None])
    for res, dst in zip(sm, (out_g, out_d, out_m, out_v)):
        ns, rp = _unpack_small(res)
        for n, a in zip(sm_names, ns + [rp]):
            dst[n] = a

    order = ("norm_mix_pre", "norm_mix_post", "norm_ffn_pre", "norm_ffn_post", "na_w_qkv", "na_w_o", "na_rpb", "dil_w_qkv", "dil_w_o",
             "ffn_w_gate", "ffn_w_up", "ffn_w_down")
    return (loss, dx[None], *[out_g[n] for n in order], *[out_d[n] for n in order], *[out_m[n] for n in order], *[out_v[n] for n in order])
```

```python
import functools

import numpy as np
import jax
import jax.numpy as jnp
from jax import lax
from jax.experimental import pallas as pl
from jax.experimental.pallas import tpu as pltpu

F32 = jnp.float32
BF16 = jnp.bfloat16

SEQ = 2048
DM = 1024
NH = 16
HD = 64
DFF = 2816
NCHIP = 4
FSH = DFF // NCHIP
GRID_W = 64
NA_QROWS = 4
NA_QB = NA_QROWS * GRID_W
NA_WROWS = 12
NA_WIN = NA_WROWS * GRID_W
DIL = (1, 4, 16)
DIL_QB = 256
DIL_WIN = DIL_QB + 128
DIL_RADIUS = 64
RMS_EPS = 1e-6
NEG = -1e30
QSCALE = HD ** -0.5
CH = 256
MESH = pl.DeviceIdType.MESH

ADAM_LR, ADAM_B1, ADAM_B2, ADAM_EPS, ADAM_WD, ADAM_STEP = 0.001, 0.9, 0.999, 1e-08, 0.01, 10

VMEM_LIMIT = 56 * 1024 * 1024

_NN = (((1,), (0,)), ((), ()))
_NT = (((1,), (1,)), ((), ()))
_TN = (((0,), (0,)), ((), ()))


def _params(sem):
    return pltpu.CompilerParams(dimension_semantics=sem, vmem_limit_bytes=VMEM_LIMIT)


def _matmul(name, pairs, grid, out_shape, out_spec, acc_shape, carrying=False, carry=None):
    nk = grid[-1]
    npair = len(pairs)
    n_in = 2 * npair

    def body(*refs):
        ins, o_ref = refs[:2 * npair], refs[n_in]
        part = None
        for p in range(npair):
            d = lax.dot_general(ins[2 * p][...].astype(BF16), ins[2 * p + 1][...].astype(BF16), pairs[p][4],
                                preferred_element_type=F32)
            part = d if part is None else part + d
        if nk == 1:
            o_ref[...] = part.astype(o_ref.dtype)
        else:
            acc_ref = refs[n_in + 1]
            kk = pl.program_id(len(grid) - 1)

            @pl.when(kk == 0)
            def _():
                acc_ref[...] = part

            @pl.when(kk > 0)
            def _():
                acc_ref[...] += part

            @pl.when(kk == nk - 1)
            def _():
                o_ref[...] = acc_ref[...].astype(o_ref.dtype)

    ops, specs = [], []
    for a, a_spec, b, b_spec, _ in pairs:
        ops += [a, b]
        specs += [a_spec, b_spec]
    (out,), sent = _carrier_call(name, body, grid, specs, [out_spec], [out_shape], [] if nk == 1 else [pltpu.VMEM(acc_shape, F32)], ops, carry)
    return (out, sent) if carrying else out


def _qkv_fwd(name, h_all, w4, carry):
    g_n = h_all.shape[0]
    per = w4.shape[2] // CH
    return _matmul(
        name, [(h_all, pl.BlockSpec((None, SEQ, DM), lambda g, q, k: (g, 0, 0)),
                w4, pl.BlockSpec((None, DM, CH), lambda g, q, k: ((g * 12 + q) // per, 0, (g * 12 + q) % per)), _NN)],
        (g_n, 12, 1), jax.ShapeDtypeStruct((g_n, SEQ, 3 * DM), BF16),
        pl.BlockSpec((None, SEQ, CH), lambda g, q, k: (g, 0, q)), None, carrying=True, carry=carry)


def _qkv_bwd_dh(name, dqkv, w4, carry):
    g_n = dqkv.shape[0]
    per = w4.shape[2] // CH
    tm = SEQ

    def pair(cb):
        chunk = lambda g, t: g * 12 + t * 4 + cb
        return (dqkv, pl.BlockSpec((None, None, tm, CH), lambda g, i, t: (g, t, i, cb)),
                w4, pl.BlockSpec((None, DM, CH), lambda g, i, t: (chunk(g, t) // per, 0, chunk(g, t) % per)), _NT)

    return _matmul(name, [pair(cb) for cb in range(4)], (g_n, SEQ // tm, 3), jax.ShapeDtypeStruct((g_n, SEQ, DM), F32),
                   pl.BlockSpec((None, tm, DM), lambda g, i, t: (g, i, 0)), (tm, DM), carrying=True, carry=carry)


def _qkv_bwd_dw(name, ht_all, dqkv, shard_cols):
    g_n = dqkv.shape[0]
    per = shard_cols // CH
    return _matmul(
        name, [(ht_all, pl.BlockSpec((None, DM, SEQ), lambda qq, k: (qq // 12, 0, 0)),
                dqkv, pl.BlockSpec((None, None, SEQ, CH), lambda qq, k: (qq // 12, (qq % 12) // 4, 0, qq % 4)), _NN)],
        (g_n * 12, 1), jax.ShapeDtypeStruct((NCHIP, DM, shard_cols), BF16),
        pl.BlockSpec((None, DM, CH), lambda qq, k: (qq // per, 0, qq % per)), None)


def _proj_fwd(name, o, wo, x, g):
    tm = 512

    def body(o_ref, w_ref, x_ref, g_ref, xn_ref, u_ref):
        u = jnp.dot(o_ref[...], w_ref[...], preferred_element_type=F32)
        u_ref[...] = u
        r = lax.rsqrt(jnp.mean(u * u, axis=-1, keepdims=True) + RMS_EPS)
        xn_ref[...] = x_ref[...] + u * r * g_ref[...]

    rows = pl.BlockSpec((tm, DM), lambda i: (i, 0))
    sh = jax.ShapeDtypeStruct((SEQ, DM), F32)
    return pl.pallas_call(
        body, grid=(SEQ // tm,), in_specs=[rows, pl.BlockSpec((DM, DM), lambda i: (0, 0)), rows, pl.BlockSpec((1, DM), lambda i: (0, 0))],
        out_specs=[rows, rows], out_shape=[sh, sh], compiler_params=_params(("parallel",)), name=name)(o, wo, x, g)


def _proj_bwd(name, dy, u, g, wo, dtype):
    tm = 512

    def body(dy_ref, u_ref, g_ref, w_ref, do_ref, du_ref, dg_ref):
        dy = dy_ref[...]
        u = u_ref[...]
        r = lax.rsqrt(jnp.mean(u * u, axis=-1, keepdims=True) + RMS_EPS)
        yh = u * r
        t = dy * g_ref[...]
        du = (r * (t - yh * jnp.mean(t * yh, axis=-1, keepdims=True))).astype(BF16)
        du_ref[...] = du
        do_ref[...] = lax.dot_general(du, w_ref[...], _NT, preferred_element_type=F32).astype(do_ref.dtype)

        @pl.when(pl.program_id(0) == 0)
        def _():
            dg_ref[...] = jnp.zeros_like(dg_ref)

        dg_ref[...] += jnp.sum(dy * yh, axis=0, keepdims=True)

    rows = pl.BlockSpec((tm, DM), lambda i: (i, 0))
    vec = pl.BlockSpec((1, DM), lambda i: (0, 0))
    return pl.pallas_call(
        body, grid=(SEQ // tm,), in_specs=[rows, rows, vec, pl.BlockSpec((DM, DM), lambda i: (0, 0))], out_specs=[rows, rows, vec],
        out_shape=[jax.ShapeDtypeStruct((SEQ, DM), dtype), jax.ShapeDtypeStruct((SEQ, DM), BF16), jax.ShapeDtypeStruct((1, DM), F32)],
        compiler_params=_params(("arbitrary",)), name=name)(dy, u, g, wo)


def _proj_bwd_dw(name, o, du):
    tn = 512
    return _matmul(
        name, [(o, pl.BlockSpec((SEQ, DM), lambda j, k: (0, 0)), du, pl.BlockSpec((SEQ, tn), lambda j, k: (0, j)), _TN)],
        (DM // tn, 1), jax.ShapeDtypeStruct((DM, DM), BF16), pl.BlockSpec((DM, tn), lambda j, k: (0, j)), None)


def _ffn_wspec(index_map):
    return pl.BlockSpec((None, FSH, DM), index_map)


def _ffn_bwd_dw(name, a4, b):
    return _matmul(
        name, [(a4, pl.BlockSpec((None, SEQ, FSH), lambda s, k: (s, 0, 0)), b, pl.BlockSpec((SEQ, DM), lambda s, k: (0, 0)), _TN)],
        (NCHIP, 1), jax.ShapeDtypeStruct((NCHIP, FSH, DM), BF16), _ffn_wspec(lambda s, k: (s, 0, 0)), None)


ROWS = 256


def _row_spec():
    return pl.BlockSpec((ROWS, DM), lambda i: (i, 0))


def _vec_spec():
    return pl.BlockSpec((1, DM), lambda i: (0, 0))


def _rms_fwd(name, x, g, dtype=BF16):
    def body(x_ref, g_ref, o_ref):
        x = x_ref[...]
        r = lax.rsqrt(jnp.mean(x * x, axis=-1, keepdims=True) + RMS_EPS)
        o_ref[...] = (x * r * g_ref[...]).astype(o_ref.dtype)

    return pl.pallas_call(body, grid=(SEQ // ROWS,), in_specs=[_row_spec(), _vec_spec()], out_specs=_row_spec(),
                          out_shape=jax.ShapeDtypeStruct((SEQ, DM), dtype), compiler_params=_params(("parallel",)), name=name)(x, g)


def _rms_fwd_both(name, x, g):
    def body(x_ref, g_ref, o_ref, t_ref):
        x = x_ref[...]
        r = lax.rsqrt(jnp.mean(x * x, axis=-1, keepdims=True) + RMS_EPS)
        h = x * r * g_ref[...]
        o_ref[...] = h.astype(o_ref.dtype)
        t_ref[...] = h.T.astype(t_ref.dtype)

    return pl.pallas_call(
        body, grid=(SEQ // ROWS,), in_specs=[_row_spec(), _vec_spec()], out_specs=[_row_spec(), pl.BlockSpec((DM, ROWS), lambda i: (0, i))],
        out_shape=[jax.ShapeDtypeStruct((SEQ, DM), BF16), jax.ShapeDtypeStruct((DM, SEQ), BF16)],
        compiler_params=_params(("parallel",)), name=name)(x, g)


def _norm_bwd(name, dys, u, g, res=None):
    ndy = len(dys)

    def body(*refs):
        dy = refs[0][...]
        for r_ in refs[1:ndy]:
            dy = dy + r_[...]
        u_ref, g_ref = refs[ndy], refs[ndy + 1]
        res_ref = refs[ndy + 2] if res is not None else None
        du_ref, dg_ref = refs[-2], refs[-1]
        u = u_ref[...]
        r = lax.rsqrt(jnp.mean(u * u, axis=-1, keepdims=True) + RMS_EPS)
        yh = u * r
        t = dy * g_ref[...]
        du = r * (t - yh * jnp.mean(t * yh, axis=-1, keepdims=True))
        if res_ref is not None:
            du = du + res_ref[...]
        du_ref[...] = du

        @pl.when(pl.program_id(0) == 0)
        def _():
            dg_ref[...] = jnp.zeros_like(dg_ref)

        dg_ref[...] += jnp.sum(dy * yh, axis=0, keepdims=True)

    ops = list(dys) + [u, g] + ([res] if res is not None else [])
    specs = [_row_spec()] * ndy + [_row_spec(), _vec_spec()] + ([_row_spec()] if res is not None else [])
    return pl.pallas_call(
        body, grid=(SEQ // ROWS,), in_specs=specs, out_specs=[_row_spec(), _vec_spec()],
        out_shape=[jax.ShapeDtypeStruct((SEQ, DM), F32), jax.ShapeDtypeStruct((1, DM), F32)],
        compiler_params=_params(("arbitrary",)), name=name)(*ops)


def _loss_grad(name, y, t):
    def body(y_ref, t_ref, dy_ref, l_ref):
        e = y_ref[...] - t_ref[...]
        dy_ref[...] = e * (1.0 / DM)

        @pl.when(pl.program_id(0) == 0)
        def _():
            l_ref[...] = jnp.zeros_like(l_ref)

        l_ref[...] += jnp.sum(e * e) * (0.5 / DM)

    return pl.pallas_call(
        body, grid=(SEQ // ROWS,), in_specs=[_row_spec(), _row_spec()],
        out_specs=[_row_spec(), pl.BlockSpec((1, 128), lambda i: (0, 0))],
        out_shape=[jax.ShapeDtypeStruct((SEQ, DM), F32), jax.ShapeDtypeStruct((1, 128), F32)],
        compiler_params=_params(("arbitrary",)), name=name)(y, t)


HBM_SPEC = pl.BlockSpec(memory_space=pltpu.HBM)


class _Carried:
    def __init__(self, ins, out_shapes, n_sems, issue, drain):
        self.ins, self.out_shapes, self.n_sems, self.issue, self.drain = list(ins), list(out_shapes), tuple(n_sems), issue, drain


def _carrier_call(name, body, grid, in_specs, out_specs, out_shape, scratch_shapes, operands, carry):
    n_in, n_out, n_scr = len(in_specs), len(out_specs), len(scratch_shapes)
    if carry is None:
        res = pl.pallas_call(body, grid=grid, in_specs=in_specs, out_specs=out_specs, out_shape=out_shape, scratch_shapes=scratch_shapes,
                             compiler_params=_params(("arbitrary",) * len(grid)), name=name)(*operands)
        return list(res), []
    ci, co = len(carry.ins), len(carry.out_shapes)

    def wrapped(*refs):
        ins, cins = refs[:n_in], refs[n_in:n_in + ci]
        outs, couts = refs[n_in + ci:n_in + ci + n_out], refs[n_in + ci + n_out:n_in + ci + n_out + co]
        scr, sems = refs[n_in + ci + n_out + co:n_in + ci + n_out + co + n_scr], refs[n_in + ci + n_out + co + n_scr:]
        first = functools.reduce(jnp.logical_and, [pl.program_id(a) == 0 for a in range(len(grid))])
        last = functools.reduce(jnp.logical_and, [pl.program_id(a) == grid[a] - 1 for a in range(len(grid))])

        @pl.when(first)
        def _():
            carry.issue(cins, couts, sems)

        body(*ins, *outs, *scr)

        @pl.when(last)
        def _():
            carry.drain(cins, couts, sems)

    res = pl.pallas_call(
        wrapped, grid=grid, in_specs=list(in_specs) + [HBM_SPEC] * ci, out_specs=list(out_specs) + [HBM_SPEC] * co,
        out_shape=list(out_shape) + carry.out_shapes,
        scratch_shapes=list(scratch_shapes) + [pltpu.SemaphoreType.DMA((k,)) for k in carry.n_sems],
        compiler_params=pltpu.CompilerParams(dimension_semantics=("arbitrary",) * len(grid), vmem_limit_bytes=VMEM_LIMIT, has_side_effects=True),
        name=name)(*operands, *carry.ins)
    return list(res[:n_out]), list(res[n_out:])


def _run_carried(name, carry):
    def body(*refs):
        ci, co = len(carry.ins), len(carry.out_shapes)
        carry.issue(refs[:ci], refs[ci:ci + co], refs[ci + co:])
        carry.drain(refs[:ci], refs[ci:ci + co], refs[ci + co:])

    return pl.pallas_call(
        body, in_specs=[HBM_SPEC] * len(carry.ins), out_specs=[HBM_SPEC] * len(carry.out_shapes), out_shape=carry.out_shapes,
        scratch_shapes=[pltpu.SemaphoreType.DMA((k,)) for k in carry.n_sems],
        compiler_params=pltpu.CompilerParams(has_side_effects=True), name=name)(*carry.ins)


NA_BLOCKS = SEQ // NA_QB
NA_ROWS_TOTAL = SEQ // GRID_W
NA_CLASSES = ((0, 0), (8, 4), (NA_ROWS_TOTAL - NA_QROWS, NA_ROWS_TOTAL - NA_WROWS))


def _na_pairs(i0, ws):
    out = []
    for qi in range(NA_QROWS):
        i = i0 + qi
        rs = min(max(i - 4, 0), NA_ROWS_TOTAL - 8)
        for kr in range(NA_WROWS):
            r = ws + kr
            if rs <= r < rs + 8:
                out.append((qi, kr, r - i + 7))
    return out


def _diag_onehot():
    qc, kc = np.meshgrid(np.arange(GRID_W), np.arange(GRID_W), indexing="ij")
    e = np.zeros((GRID_W * GRID_W, 128), np.float32)
    j = (kc - qc + 15).reshape(-1)
    ok = (j >= 0) & (j <= 30)
    e[np.arange(GRID_W * GRID_W)[ok], j[ok]] = 1.0
    return jnp.asarray(e)


def _rpb_expand(rpb):
    r2 = jnp.pad(rpb.reshape(NH * 15, 31), ((0, 0), (0, 128 - 31)))

    def body(r_ref, e_ref, o_ref):
        o_ref[...] = lax.dot_general(r_ref[...], e_ref[...], _NT, preferred_element_type=F32, precision=lax.Precision.HIGHEST)

    out = pl.pallas_call(body, out_shape=jax.ShapeDtypeStruct((NH * 15, GRID_W * GRID_W), F32), name="rpb_expand",
                         compiler_params=pltpu.CompilerParams(vmem_limit_bytes=VMEM_LIMIT))(r2, _diag_onehot())
    return out.reshape(NH, 15, GRID_W, GRID_W)


def _na_bias_tiles(rpb):
    def body(b_ref, o_ref):
        qc = lax.broadcasted_iota(jnp.int32, (GRID_W, GRID_W), 0)
        kc = lax.broadcasted_iota(jnp.int32, (GRID_W, GRID_W), 1)
        first = jnp.clip(qc - 8, 0, GRID_W - 16)
        in_window = (kc >= first) & (kc < first + 16)
        neg = jnp.full((GRID_W, GRID_W), NEG, F32)
        for cls, (i0, ws) in enumerate(NA_CLASSES):
            @pl.when(pl.program_id(0) == cls)
            def _(i0=i0, ws=ws):
                pairs = {(qi, kr): dr for qi, kr, dr in _na_pairs(i0, ws)}
                masked = {dr: jnp.where(in_window, b_ref[dr], NEG) for dr in sorted(set(pairs.values()))}
                for qi in range(NA_QROWS):
                    for k2 in range(NA_WROWS // 2):
                        blocks = [masked[pairs[(qi, kr)]] if (qi, kr) in pairs else neg for kr in (2 * k2, 2 * k2 + 1)]
                        o_ref[qi * GRID_W:(qi + 1) * GRID_W, k2 * 128:(k2 + 1) * 128] = jnp.concatenate(blocks, axis=1)

    return pl.pallas_call(
        body, grid=(3, NH), in_specs=[pl.BlockSpec((None, 15, GRID_W, GRID_W), lambda c, h: (h, 0, 0, 0))],
        out_specs=pl.BlockSpec((None, None, NA_QB, NA_WIN), lambda c, h: (c, h, 0, 0)),
        out_shape=jax.ShapeDtypeStruct((3, NH, NA_QB, NA_WIN), F32), compiler_params=_params(("parallel", "parallel")),
        name="na_bias_tiles")(_rpb_expand(rpb))


def _na_cls(b):
    return jnp.where(b == 0, 0, jnp.where(b == NA_BLOCKS - 1, 2, 1))


def _na_start(b):
    return pl.multiple_of(jnp.clip(b * NA_QROWS - 4, 0, NA_ROWS_TOTAL - NA_WROWS) * GRID_W, GRID_W)


HPS = 4
LW = HPS * HD
NLW = DM // LW


NA_BWD_HPS = 4


def _na_in_specs(hps=HPS):
    lw = hps * HD
    nlw = DM // lw
    return [pl.BlockSpec((NA_QB, lw), lambda hp, b: (b, hp)),
            pl.BlockSpec((SEQ, lw), lambda hp, b: (0, nlw + hp)),
            pl.BlockSpec((SEQ, lw), lambda hp, b: (0, 2 * nlw + hp)),
            pl.BlockSpec((None, hps, NA_QB, NA_WIN), lambda hp, b: (_na_cls(b), hp, 0, 0))]


def _na_fwd(qkv, bias, carry):
    def body(q_ref, k_ref, v_ref, b_ref, o_ref):
        start = _na_start(pl.program_id(1))
        q = q_ref[...]
        kw = k_ref[pl.ds(start, NA_WIN), :]
        vw = v_ref[pl.ds(start, NA_WIN), :]
        outs = []
        for hh in range(HPS):
            sl = slice(hh * HD, (hh + 1) * HD)
            s = lax.dot_general(q[:, sl] * QSCALE, kw[:, sl], _NT, preferred_element_type=F32) + b_ref[hh]
            p = jnp.exp(s - jnp.max(s, axis=-1, keepdims=True))
            l = jnp.sum(p, axis=-1, keepdims=True)
            outs.append(jnp.dot(p.astype(BF16), vw[:, sl], preferred_element_type=F32) / l)
        o_ref[...] = jnp.concatenate(outs, axis=1).astype(o_ref.dtype)

    (o,), sent = _carrier_call(
        "na_fwd", body, (NLW, NA_BLOCKS), _na_in_specs(), [pl.BlockSpec((NA_QB, LW), lambda hp, b: (b, hp))],
        [jax.ShapeDtypeStruct((SEQ, DM), BF16)], [], (qkv, qkv, qkv, bias), carry)
    return o, sent


def _na_bwd(qkv, bias, do, carry):
    lw = NA_BWD_HPS * HD

    def body(q_ref, k_ref, v_ref, b_ref, do_ref, dqkv_ref, z_ref, dk_acc, dv_acc):
        blk = pl.program_id(1)

        @pl.when(blk == 0)
        def _():
            dk_acc[...] = jnp.zeros_like(dk_acc)
            dv_acc[...] = jnp.zeros_like(dv_acc)
            z_ref[...] = jnp.zeros_like(z_ref)

        start = _na_start(blk)
        q = q_ref[...]
        do = do_ref[...]
        kw = k_ref[pl.ds(start, NA_WIN), :]
        vw = v_ref[pl.ds(start, NA_WIN), :]
        dqs, dks, dvs, dss = [], [], [], []
        for hh in range(NA_BWD_HPS):
            sl = slice(hh * HD, (hh + 1) * HD)
            qh = q[:, sl] * QSCALE
            s = lax.dot_general(qh, kw[:, sl], _NT, preferred_element_type=F32) + b_ref[hh]
            p = jnp.exp(s - jnp.max(s, axis=-1, keepdims=True))
            p = p / jnp.sum(p, axis=-1, keepdims=True)
            dp = lax.dot_general(do[:, sl], vw[:, sl], _NT, preferred_element_type=F32)
            ds = p * (dp - jnp.sum(p * dp, axis=-1, keepdims=True))
            dsb = ds.astype(BF16)
            dqs.append(jnp.dot(dsb, kw[:, sl], preferred_element_type=F32) * QSCALE)
            dks.append(lax.dot_general(dsb, qh, _TN, preferred_element_type=F32))
            dvs.append(lax.dot_general(p.astype(BF16), do[:, sl], _TN, preferred_element_type=F32))
            dss.append(ds)
        for cls, (i0, ws) in enumerate(NA_CLASSES):
            @pl.when(_na_cls(blk) == cls)
            def _(i0=i0, ws=ws):
                for hh, ds in enumerate(dss):
                    for qi, kr, dr in _na_pairs(i0, ws):
                        z_ref[hh, dr * GRID_W:(dr + 1) * GRID_W, :] += ds[qi * GRID_W:(qi + 1) * GRID_W, kr * GRID_W:(kr + 1) * GRID_W]
        dqkv_ref[0, pl.ds(pl.multiple_of(blk * NA_QB, NA_QB), NA_QB), :] = jnp.concatenate(dqs, axis=1).astype(dqkv_ref.dtype)
        dk_acc[pl.ds(start, NA_WIN), :] += jnp.concatenate(dks, axis=1)
        dv_acc[pl.ds(start, NA_WIN), :] += jnp.concatenate(dvs, axis=1)

        @pl.when(blk == NA_BLOCKS - 1)
        def _():
            dqkv_ref[1] = dk_acc[...].astype(dqkv_ref.dtype)
            dqkv_ref[2] = dv_acc[...].astype(dqkv_ref.dtype)

    (dqkv, z), sent = _carrier_call(
        "na_bwd", body, (NH // NA_BWD_HPS, NA_BLOCKS),
        _na_in_specs(NA_BWD_HPS) + [pl.BlockSpec((NA_QB, lw), lambda hp, b: (b, hp))],
        [pl.BlockSpec((3, SEQ, lw), lambda hp, b: (0, 0, hp)), pl.BlockSpec((NA_BWD_HPS, 15 * GRID_W, GRID_W), lambda hp, b: (hp, 0, 0))],
        [jax.ShapeDtypeStruct((3, SEQ, DM), BF16), jax.ShapeDtypeStruct((NH, 15 * GRID_W, GRID_W), F32)],
        [pltpu.VMEM((SEQ, lw), F32), pltpu.VMEM((SEQ, lw), F32)], (qkv, qkv, qkv, bias, do), carry)
    return dqkv, z, sent


def _rpb_grad(z):
    z2 = z.reshape(NH * 15, GRID_W * GRID_W)

    def body(z_ref, e_ref, o_ref):
        o_ref[...] = jnp.dot(z_ref[...], e_ref[...], preferred_element_type=F32, precision=lax.Precision.HIGHEST)

    out = pl.pallas_call(body, out_shape=jax.ShapeDtypeStruct((NH * 15, 128), F32), name="rpb_grad",
                         compiler_params=pltpu.CompilerParams(vmem_limit_bytes=VMEM_LIMIT))(z2, _diag_onehot())
    return out[:, :31].reshape(NH, 15, 31)


DIL_BLOCKS = SEQ // DIL_QB
DIL_HPS = 8
DIL_LW = DIL_HPS * HD
DIL_NLW = DM // DIL_LW


COLS = 128


def _col_spec():
    return pl.BlockSpec((SEQ, COLS), lambda j: (0, j))


def _grp_spec():
    return pl.BlockSpec((3, SEQ, COLS), lambda j: (0, 0, j))


def _store_group_order(dst_ref, src_ref):
    for g, d in enumerate(DIL):
        n = SEQ // d
        for r in range(d):
            dst_ref[g, r * n:(r + 1) * n, :] = src_ref[pl.ds(r, n, stride=d), :].astype(dst_ref.dtype)


def _store_token_order(dst_ref, src_ref, g):
    d = DIL[g]
    n = SEQ // d
    for r in range(d):
        dst_ref[pl.ds(r, n, stride=d), :] = src_ref[g, r * n:(r + 1) * n, :]


def _to_groups(name, a):
    def body(a_ref, o_ref, t_ref):
        _store_group_order(o_ref, a_ref)
        for g in range(3):
            t_ref[g] = o_ref[g].astype(F32).T.astype(t_ref.dtype)

    return pl.pallas_call(
        body, grid=(DM // COLS,), in_specs=[_col_spec()], out_specs=[_grp_spec(), pl.BlockSpec((3, COLS, SEQ), lambda j: (0, j, 0))],
        out_shape=[jax.ShapeDtypeStruct((3, SEQ, DM), BF16), jax.ShapeDtypeStruct((3, DM, SEQ), BF16)],
        compiler_params=_params(("parallel",)), name=name)(a)


def _from_groups_sum(name, a):
    def body(a_ref, o_ref, t1, t2):
        _store_token_order(t1, a_ref, 1)
        _store_token_order(t2, a_ref, 2)
        o_ref[...] = (a_ref[0] + t1[...]) + t2[...]

    return pl.pallas_call(body, grid=(DM // COLS,), in_specs=[_grp_spec()], out_specs=_col_spec(),
                          out_shape=jax.ShapeDtypeStruct((SEQ, DM), F32), scratch_shapes=[pltpu.VMEM((SEQ, COLS), F32)] * 2,
                          compiler_params=_params(("parallel",)), name=name)(a)


def _dil_start(b):
    return pl.multiple_of(jnp.clip(b * DIL_QB - DIL_RADIUS, 0, SEQ - DIL_WIN), DIL_RADIUS)


def _dil_mask(g, b, start):
    shift = 11 - 2 * g
    ii = b * DIL_QB + lax.broadcasted_iota(jnp.int32, (DIL_QB, DIL_WIN), 0)
    jj = start + lax.broadcasted_iota(jnp.int32, (DIL_QB, DIL_WIN), 1)
    dist = jnp.abs(ii - jj)
    valid = (dist <= DIL_RADIUS) & (jnp.right_shift(ii, shift) == jnp.right_shift(jj, shift))
    return valid, dist.astype(F32)


def _dil_in_specs():
    return [pl.BlockSpec(memory_space=pltpu.SMEM),
            pl.BlockSpec((None, DIL_QB, DIL_LW), lambda g, hp, b: (g, b, hp)),
            pl.BlockSpec((None, SEQ, DIL_LW), lambda g, hp, b: (g, 0, DIL_NLW + hp)),
            pl.BlockSpec((None, SEQ, DIL_LW), lambda g, hp, b: (g, 0, 2 * DIL_NLW + hp))]


def _dil_fwd(qkv, slopes, carry):
    def body(sl_ref, q_ref, k_ref, v_ref, o_ref, lse_ref):
        g, hp, b = pl.program_id(0), pl.program_id(1), pl.program_id(2)
        start = _dil_start(b)
        valid, dist = _dil_mask(g, b, start)
        dil = jnp.left_shift(1, 2 * g).astype(F32)
        q = q_ref[...]
        kw = k_ref[pl.ds(start, DIL_WIN), :]
        vw = v_ref[pl.ds(start, DIL_WIN), :]
        outs, lses = [], []
        for hh in range(DIL_HPS):
            sl = slice(hh * HD, (hh + 1) * HD)
            s = lax.dot_general(q[:, sl] * QSCALE, kw[:, sl], _NT, preferred_element_type=F32)
            s = jnp.where(valid, s - (sl_ref[hp * DIL_HPS + hh] * dil) * dist, NEG)
            m = jnp.max(s, axis=-1, keepdims=True)
            p = jnp.exp(s - m)
            l = jnp.sum(p, axis=-1, keepdims=True)
            outs.append(jnp.dot(p.astype(BF16), vw[:, sl], preferred_element_type=F32) / l)
            lses.append(jnp.broadcast_to(m + jnp.log(l), (DIL_QB, HD)))
        o_ref[...] = jnp.concatenate(outs, axis=1)
        lse_ref[...] = jnp.concatenate(lses, axis=1)

    ospec = pl.BlockSpec((None, DIL_QB, DIL_LW), lambda g, hp, b: (g, b, hp))
    sh = jax.ShapeDtypeStruct((3, SEQ, DM), F32)
    (o, lse), sent = _carrier_call("dil_fwd", body, (3, DIL_NLW, DIL_BLOCKS), _dil_in_specs(), [ospec, ospec], [sh, sh], [],
                                   (slopes, qkv, qkv, qkv), carry)
    return o, lse, sent


def _dil_merge(o_all, lse_all):
    def body(o_ref, l_ref, out_ref, lse_ref, o1, o2, l1, l2):
        for g, (ot, lt) in ((1, (o1, l1)), (2, (o2, l2))):
            _store_token_order(ot, o_ref, g)
            _store_token_order(lt, l_ref, g)
        la, lb, lc = l_ref[0], l1[...], l2[...]
        m = jnp.maximum(jnp.maximum(la, lb), lc)
        wa, wb, wc = jnp.exp(la - m), jnp.exp(lb - m), jnp.exp(lc - m)
        sw = (wa + wb) + wc
        out_ref[...] = (((wa * o_ref[0] + wb * o1[...]) + wc * o2[...]) / sw).astype(out_ref.dtype)
        lse_ref[...] = m + jnp.log(sw)

    return pl.pallas_call(
        body, grid=(DM // COLS,), in_specs=[_grp_spec(), _grp_spec()], out_specs=[_col_spec(), _col_spec()],
        out_shape=[jax.ShapeDtypeStruct((SEQ, DM), BF16), jax.ShapeDtypeStruct((SEQ, DM), F32)],
        scratch_shapes=[pltpu.VMEM((SEQ, COLS), F32)] * 4, compiler_params=_params(("parallel",)), name="dil_merge")(o_all, lse_all)


def _dil_bwd_prep(do, o, lse):
    heads = COLS // HD

    def body(do_ref, o_ref, lse_ref, dog_ref, ddr_ref, lser_ref, dd, grp):
        prod = do_ref[...] * o_ref[...].astype(F32)
        dd[...] = jnp.concatenate(
            [jnp.broadcast_to(jnp.sum(prod[:, h * HD:(h + 1) * HD], axis=-1, keepdims=True), (SEQ, HD)) for h in range(heads)], axis=1)
        _store_group_order(dog_ref, do_ref)
        for src, dst in ((dd, ddr_ref), (lse_ref, lser_ref)):
            _store_group_order(grp, src)
            for g in range(3):
                t = grp[g].T
                for h in range(heads):
                    dst[g, h] = t[h * HD:h * HD + 8, :]

    rows = jax.ShapeDtypeStruct((3, NH, 8, SEQ), F32)
    rspec = pl.BlockSpec((3, heads, 8, SEQ), lambda j: (0, j, 0, 0))
    return pl.pallas_call(
        body, grid=(DM // COLS,), in_specs=[_col_spec()] * 3, out_specs=[_grp_spec(), rspec, rspec],
        out_shape=[jax.ShapeDtypeStruct((3, SEQ, DM), BF16), rows, rows],
        scratch_shapes=[pltpu.VMEM((SEQ, COLS), F32), pltpu.VMEM((3, SEQ, COLS), F32)],
        compiler_params=_params(("parallel",)), name="dil_bwd_prep")(do, o, lse)


def _dil_bwd(qkv, do, dd, lse, slopes, carry):
    def body(sl_ref, q_ref, k_ref, v_ref, do_ref, dd_ref, lse_ref, dqkv_ref, dk_acc, dv_acc):
        g, hp, b = pl.program_id(0), pl.program_id(1), pl.program_id(2)

        @pl.when(b == 0)
        def _():
            dk_acc[...] = jnp.zeros_like(dk_acc)
            dv_acc[...] = jnp.zeros_like(dv_acc)

        start = _dil_start(b)
        shift = 11 - 2 * g
        jj = start + lax.broadcasted_iota(jnp.int32, (DIL_WIN, DIL_QB), 0)
        ii = b * DIL_QB + lax.broadcasted_iota(jnp.int32, (DIL_WIN, DIL_QB), 1)
        dist = jnp.abs(ii - jj)
        valid = (dist <= DIL_RADIUS) & (jnp.right_shift(ii, shift) == jnp.right_shift(jj, shift))
        dist = dist.astype(F32)
        dil = jnp.left_shift(1, 2 * g).astype(F32)
        q = q_ref[...]
        do = do_ref[...]
        kw = k_ref[pl.ds(start, DIL_WIN), :]
        vw = v_ref[pl.ds(start, DIL_WIN), :]
        dqs, dks, dvs = [], [], []
        for hh in range(DIL_HPS):
            sl = slice(hh * HD, (hh + 1) * HD)
            qh = q[:, sl] * QSCALE
            st = lax.dot_general(kw[:, sl], qh, _NT, preferred_element_type=F32)
            st = jnp.where(valid, st - (sl_ref[hp * DIL_HPS + hh] * dil) * dist, NEG)
            pt = jnp.exp(st - lse_ref[hh, 0:1, :])
            dpt = lax.dot_general(vw[:, sl], do[:, sl], _NT, preferred_element_type=F32)
            dst = (pt * (dpt - dd_ref[hh, 0:1, :])).astype(BF16)
            dqs.append(lax.dot_general(kw[:, sl], dst, _TN, preferred_element_type=F32).T * QSCALE)
            dks.append(jnp.dot(dst, qh, preferred_element_type=F32))
            dvs.append(jnp.dot(pt.astype(BF16), do[:, sl], preferred_element_type=F32))
        dqkv_ref[0, pl.ds(pl.multiple_of(b * DIL_QB, DIL_QB), DIL_QB), :] = jnp.concatenate(dqs, axis=1).astype(dqkv_ref.dtype)
        dk_acc[pl.ds(start, DIL_WIN), :] += jnp.concatenate(dks, axis=1)
        dv_acc[pl.ds(start, DIL_WIN), :] += jnp.concatenate(dvs, axis=1)

        @pl.when(b == DIL_BLOCKS - 1)
        def _():
            dqkv_ref[1] = dk_acc[...].astype(dqkv_ref.dtype)
            dqkv_ref[2] = dv_acc[...].astype(dqkv_ref.dtype)

    qspec = pl.BlockSpec((None, DIL_QB, DIL_LW), lambda g, hp, b: (g, b, hp))
    rspec = pl.BlockSpec((None, DIL_HPS, 8, DIL_QB), lambda g, hp, b: (g, hp, 0, b))
    (dqkv,), sent = _carrier_call(
        "dil_bwd", body, (3, DIL_NLW, DIL_BLOCKS), _dil_in_specs() + [qspec, rspec, rspec],
        [pl.BlockSpec((None, 3, SEQ, DIL_LW), lambda g, hp, b: (g, 0, 0, hp))], [jax.ShapeDtypeStruct((3, 3, SEQ, DM), BF16)],
        [pltpu.VMEM((SEQ, DIL_LW), F32), pltpu.VMEM((SEQ, DIL_LW), F32)], (slopes, qkv, qkv, qkv, do, dd, lse), carry)
    return dqkv, sent


def _ffn_fwd(name, x, g_pre, g_post, wgt4, wut4, wd4, carry):
    tm = 512

    def body(x_ref, gpre_ref, gpost_ref, wg_ref, wu_ref, wd_ref, xn_ref, h_ref, gate_ref, up_ref, u_ref, acc):
        s = pl.program_id(1)

        @pl.when(s == 0)
        def _():
            x = x_ref[...]
            r = lax.rsqrt(jnp.mean(x * x, axis=-1, keepdims=True) + RMS_EPS)
            h_ref[...] = (x * r * gpre_ref[...]).astype(h_ref.dtype)

        h = h_ref[...]
        gate = lax.dot_general(h, wg_ref[...], _NT, preferred_element_type=F32).astype(BF16)
        up = lax.dot_general(h, wu_ref[...], _NT, preferred_element_type=F32).astype(BF16)
        gate_ref[...] = gate
        up_ref[...] = up
        gf = gate.astype(F32)
        act = (gf * jax.nn.sigmoid(gf) * up.astype(F32)).astype(BF16)
        part = jnp.dot(act, wd_ref[...], preferred_element_type=F32)

        @pl.when(s == 0)
        def _():
            acc[...] = part

        @pl.when(s > 0)
        def _():
            acc[...] += part

        @pl.when(s == NCHIP - 1)
        def _():
            u = acc[...]
            u_ref[...] = u
            r = lax.rsqrt(jnp.mean(u * u, axis=-1, keepdims=True) + RMS_EPS)
            xn_ref[...] = x_ref[...] + u * r * gpost_ref[...]

    rows = pl.BlockSpec((tm, DM), lambda i, s: (i, 0))
    vec = pl.BlockSpec((1, DM), lambda i, s: (0, 0))
    wspec = _ffn_wspec(lambda i, s: (s, 0, 0))
    mid = pl.BlockSpec((None, tm, FSH), lambda i, s: (s, i, 0))
    outs, sent = _carrier_call(
        name, body, (SEQ // tm, NCHIP), [rows, vec, vec, wspec, wspec, wspec], [rows, rows, mid, mid, rows],
        [jax.ShapeDtypeStruct((SEQ, DM), F32), jax.ShapeDtypeStruct((SEQ, DM), BF16), jax.ShapeDtypeStruct((NCHIP, SEQ, FSH), BF16),
         jax.ShapeDtypeStruct((NCHIP, SEQ, FSH), BF16), jax.ShapeDtypeStruct((SEQ, DM), F32)],
        [pltpu.VMEM((tm, DM), F32)], (x, g_pre, g_post, wgt4, wut4, wd4), carry)
    return outs, sent


def _ffn_block(layer, x, g_pre, g_post, ex):
    tag = f"l{layer}_ffn_fwd"
    (x_new, h, gate, up, u), sent = _ffn_fwd(tag, x, g_pre, g_post, ex.weight(("ffn_w_gate", layer)), ex.weight(("ffn_w_up", layer)),
                                             ex.weight(("ffn_w_down", layer)), ex.carry(tag))
    ex.carried(tag, sent)
    return x_new, (x, h, gate, up, u)


def _ffn_bwd(name, dx, x, gate, up, u, g_pre, g_post, wgt4, wut4, wd4, carry):
    tm = 512

    def body(dx_ref, x_ref, gate_ref, up_ref, u_ref, gpre_ref, gpost_ref, wg_ref, wu_ref, wd_ref,
             dxin_ref, du_ref, dgate_ref, dup_ref, act_ref, dgpre_ref, dgpost_ref, dh_acc):
        i, s = pl.program_id(0), pl.program_id(1)

        @pl.when((i == 0) & (s == 0))
        def _():
            dgpre_ref[...] = jnp.zeros_like(dgpre_ref)
            dgpost_ref[...] = jnp.zeros_like(dgpost_ref)

        @pl.when(s == 0)
        def _():
            dy = dx_ref[...]
            uu = u_ref[...]
            r = lax.rsqrt(jnp.mean(uu * uu, axis=-1, keepdims=True) + RMS_EPS)
            yh = uu * r
            t = dy * gpost_ref[...]
            du_ref[...] = (r * (t - yh * jnp.mean(t * yh, axis=-1, keepdims=True))).astype(du_ref.dtype)
            dgpost_ref[...] += jnp.sum(dy * yh, axis=0, keepdims=True)

        dact = lax.dot_general(du_ref[...], wd_ref[...], _NT, preferred_element_type=F32)
        g = gate_ref[...].astype(F32)
        upv = up_ref[...].astype(F32)
        sg = jax.nn.sigmoid(g)
        dgate = (dact * upv * sg * (1.0 + g * (1.0 - sg))).astype(BF16)
        dup = (dact * g * sg).astype(BF16)
        dgate_ref[...] = dgate
        dup_ref[...] = dup
        act_ref[...] = (g * sg * upv).astype(act_ref.dtype)
        part = jnp.dot(dgate, wg_ref[...], preferred_element_type=F32) + jnp.dot(dup, wu_ref[...], preferred_element_type=F32)

        @pl.when(s == 0)
        def _():
            dh_acc[...] = part

        @pl.when(s > 0)
        def _():
            dh_acc[...] += part

        @pl.when(s == NCHIP - 1)
        def _():
            dh = dh_acc[...]
            xx = x_ref[...]
            r = lax.rsqrt(jnp.mean(xx * xx, axis=-1, keepdims=True) + RMS_EPS)
            yh = xx * r
            t = dh * gpre_ref[...]
            dxin_ref[...] = dx_ref[...] + r * (t - yh * jnp.mean(t * yh, axis=-1, keepdims=True))
            dgpre_ref[...] += jnp.sum(dh * yh, axis=0, keepdims=True)

    rows = pl.BlockSpec((tm, DM), lambda i, s: (i, 0))
    vec = pl.BlockSpec((1, DM), lambda i, s: (0, 0))
    wspec = _ffn_wspec(lambda i, s: (s, 0, 0))
    mid = pl.BlockSpec((None, tm, FSH), lambda i, s: (s, i, 0))
    mid_shape = jax.ShapeDtypeStruct((NCHIP, SEQ, FSH), BF16)
    return _carrier_call(
        name, body, (SEQ // tm, NCHIP), [rows, rows, mid, mid, rows, vec, vec, wspec, wspec, wspec], [rows, rows, mid, mid, mid, vec, vec],
        [jax.ShapeDtypeStruct((SEQ, DM), F32), jax.ShapeDtypeStruct((SEQ, DM), BF16), mid_shape, mid_shape, mid_shape,
         jax.ShapeDtypeStruct((1, DM), F32), jax.ShapeDtypeStruct((1, DM), F32)],
        [pltpu.VMEM((tm, DM), F32)], (dx, x, gate, up, u, g_pre, g_post, wgt4, wut4, wd4), carry)


def _ffn_block_bwd(layer, dx, saved, g_pre, g_post, ex):
    tag = f"l{layer}"
    x, h, gate, up, u = saved
    (dx_in, du, dgate, dup, act, dg_pre, dg_post), sent = _ffn_bwd(
        f"{tag}_ffn_bwd", dx, x, gate, up, u, g_pre, g_post, ex.weight(("ffn_w_gate", layer)), ex.weight(("ffn_w_up", layer)),
        ex.weight(("ffn_w_down", layer)), ex.carry(f"{tag}_ffn_bwd"))
    ex.carried(f"{tag}_ffn_bwd", sent)
    d_wd = _ffn_bwd_dw(f"{tag}_dwd", act, du)
    d_wg = _ffn_bwd_dw(f"{tag}_dwg", dgate, h)
    d_wu = _ffn_bwd_dw(f"{tag}_dwu", dup, h)
    ex.grads(f"{tag}_ffn", {("ffn_w_gate", layer): d_wg, ("ffn_w_up", layer): d_wu, ("ffn_w_down", layer): d_wd})
    return dx_in, dg_pre, dg_post


def _alibi_slopes():
    return 2.0 ** (-8.0 * jnp.arange(1, NH + 1, dtype=F32) / NH)


def _local_step(x, target, norms, rpb, ex):
    g_mix_pre, g_mix_post, g_ffn_pre, g_ffn_post = norms
    row = lambda a, i: a[i:i + 1]

    bias = _na_bias_tiles(rpb)
    h0, h0t = _rms_fwd_both("l0_mix_pre", x, row(g_mix_pre, 0))
    qkv0, sent = _qkv_fwd("l0_qkv", h0[None], ex.weight(("na_w_qkv", 0)), ex.carry("l0_qkv"))
    ex.carried("l0_qkv", sent)
    o0, sent = _na_fwd(qkv0[0], bias, ex.carry("na_fwd"))
    ex.carried("na_fwd", sent)
    na_wo = ex.weight(("na_w_o", 0)).reshape(DM, DM)
    x1, u0 = _proj_fwd("l0_proj", o0, na_wo, x, row(g_mix_post, 0))
    x2, ffn0 = _ffn_block(0, x1, row(g_ffn_pre, 0), row(g_ffn_post, 0), ex)

    slopes = _alibi_slopes()
    h2g, h2gt = _to_groups("l1_h_groups", _rms_fwd("l1_mix_pre", x2, row(g_mix_pre, 1), F32))
    dil_wqkv = ex.weight(("dil_w_qkv", 0))
    qkv1, sent = _qkv_fwd("l1_qkv", h2g, dil_wqkv, ex.carry("l1_qkv"))
    ex.carried("l1_qkv", sent)
    og, lg, sent = _dil_fwd(qkv1, slopes, ex.carry("dil_fwd"))
    ex.carried("dil_fwd", sent)
    o1, lse = _dil_merge(og, lg)
    dil_wo = ex.weight(("dil_w_o", 0)).reshape(DM, DM)
    x3, u1 = _proj_fwd("l1_proj", o1, dil_wo, x2, row(g_mix_post, 1))
    x4, ffn1 = _ffn_block(1, x3, row(g_ffn_pre, 1), row(g_ffn_post, 1), ex)

    dx4, loss_row = _loss_grad("loss", x4, target)

    dx3, dg_fpre1, dg_fpost1 = _ffn_block_bwd(1, dx4, ffn1, row(g_ffn_pre, 1), row(g_ffn_post, 1), ex)
    do1, du1, dg_mpost1 = _proj_bwd("l1_proj_bwd", dx3, u1, row(g_mix_post, 1), dil_wo, F32)
    d_dil_wo = _proj_bwd_dw("l1_dwo", o1, du1)
    dog, ddg, lseg = _dil_bwd_prep(do1, o1, lse)
    dqkv1, sent = _dil_bwd(qkv1, dog, ddg, lseg, slopes, ex.carry("dil_bwd"))
    ex.carried("dil_bwd", sent)
    d_dil_wqkv = _qkv_bwd_dw("l1_dwqkv", h2gt, dqkv1, dil_wqkv.shape[2])
    ex.grads("l1_mix", {("dil_w_qkv", 0): d_dil_wqkv, ("dil_w_o", 0): d_dil_wo.reshape(NCHIP, DM // NCHIP, DM)})
    dh2g, sent = _qkv_bwd_dh("l1_dh", dqkv1, dil_wqkv, ex.carry("l1_dh"))
    ex.carried("l1_dh", sent)
    dh2 = _from_groups_sum("l1_dh_tokens", dh2g)
    dx2, dg_mpre1 = _norm_bwd("l1_mix_pre_bwd", [dh2], x2, row(g_mix_pre, 1), res=dx3)

    dx1, dg_fpre0, dg_fpost0 = _ffn_block_bwd(0, dx2, ffn0, row(g_ffn_pre, 0), row(g_ffn_post, 0), ex)
    do0, du0, dg_mpost0 = _proj_bwd("l0_proj_bwd", dx1, u0, row(g_mix_post, 0), na_wo, BF16)
    d_na_wo = _proj_bwd_dw("l0_dwo", o0, du0)
    dqkv0, z, sent = _na_bwd(qkv0[0], bias, do0, ex.carry("na_bwd"))
    ex.carried("na_bwd", sent)
    d_rpb = _rpb_grad(z)
    na_wqkv = ex.weight(("na_w_qkv", 0))
    d_na_wqkv = _qkv_bwd_dw("l0_dwqkv", h0t[None], dqkv0[None], na_wqkv.shape[2])
    ex.grads("l0_mix", {("na_w_qkv", 0): d_na_wqkv, ("na_w_o", 0): d_na_wo.reshape(NCHIP, DM // NCHIP, DM)})
    dh0, sent = _qkv_bwd_dh("l0_dh", dqkv0[None], na_wqkv, ex.carry("l0_dh"))
    ex.carried("l0_dh", sent)
    dx0, dg_mpre0 = _norm_bwd("l0_mix_pre_bwd", [dh0[0]], x, row(g_mix_pre, 0), res=dx1)

    dnorms = (jnp.concatenate([dg_mpre0, dg_mpre1]), jnp.concatenate([dg_mpost0, dg_mpost1]),
              jnp.concatenate([dg_fpre0, dg_fpre1]), jnp.concatenate([dg_fpost0, dg_fpost1]))
    return loss_row, dx0, dnorms, d_rpb


def _place():
    x, y, c = lax.axis_index("x"), lax.axis_index("y"), lax.axis_index("c")
    chips = ((1 - x, y), (x, 1 - y), (1 - x, 1 - y))
    return x, y, c, chips


def _chip_id(chip):
    return 2 * chip[0] + chip[1]


def _comm_call(name, body, ins, out_shapes, n_sems, aliases=None):
    return pl.pallas_call(
        body, in_specs=[HBM_SPEC] * len(ins), out_specs=[HBM_SPEC] * len(out_shapes), out_shape=out_shapes,
        scratch_shapes=[pltpu.SemaphoreType.DMA((k,)) for k in n_sems], input_output_aliases=aliases or {},
        compiler_params=pltpu.CompilerParams(has_side_effects=True), name=name)(*ins)


def _gather_copies(shards):
    n = len(shards)

    def copies(src, out, sems):
        send_sems, recv_sems = sems
        x, y, c, chips = _place()

        def copy(t, k, chip, half, to, from_src=False):
            blk = out[t].at[_chip_id(chip), half]
            return pltpu.make_async_remote_copy(
                src_ref=src[t].at[half] if from_src else blk, dst_ref=blk,
                send_sem=send_sems.at[6 * t + k], recv_sem=recv_sems.at[6 * t + k], device_id=to, device_id_type=MESH)

        return copy, x, y, c, chips

    def issue(src, out, sems):
        copy, x, y, c, chips = copies(src, out, sems)
        for t in range(n):
            for j, chip in enumerate(chips):
                copy(t, j, (x, y), c, (*chip, c), from_src=True).start()

    def drain(src, out, sems):
        copy, x, y, c, chips = copies(src, out, sems)
        passed = []
        for t in range(n):
            for j, chip in enumerate(chips):
                copy(t, j, chip, c, (x, y, c)).wait_recv()
                fwd = copy(t, 3 + j, chip, c, (x, y, 1 - c))
                fwd.start()
                passed.append(fwd)
        for t in range(n):
            for j, chip in enumerate(chips):
                copy(t, 3 + j, chip, 1 - c, (x, y, c)).wait_recv()
        for t in range(n):
            for j, chip in enumerate(chips):
                copy(t, j, (x, y), c, (*chip, c), from_src=True).wait_send()
        for cp in passed:
            cp.wait_send()

    return _Carried(shards, [jax.ShapeDtypeStruct((NCHIP,) + s.shape, s.dtype) for s in shards], (6 * n, 6 * n), issue, drain)


def _pair_exchange(name, grads):
    n = len(grads)

    def body(*refs):
        g, theirs = refs[:n], refs[n:2 * n]
        send_sems, recv_sems = refs[2 * n:]
        x, y, c, _ = _place()
        swap = [pltpu.make_async_remote_copy(src_ref=g[t].at[:, 1 - c], dst_ref=theirs[t], send_sem=send_sems.at[t],
                                             recv_sem=recv_sems.at[t], device_id=(x, y, 1 - c), device_id_type=MESH) for t in range(n)]
        for cp in swap:
            cp.start()
        for cp in swap:
            cp.wait()

    return _comm_call(name, body, grads, [jax.ShapeDtypeStruct((NCHIP,) + g.shape[2:], g.dtype) for g in grads], (n, n))


def _chip_exchange_copies(parts):
    n = len(parts)

    def copies(p, slots, sems):
        send_sems, recv_sems = sems
        x, y, c, chips = _place()
        return [pltpu.make_async_remote_copy(src_ref=p[t].at[_chip_id(chips[j])], dst_ref=slots[t].at[j], send_sem=send_sems.at[3 * t + j],
                                             recv_sem=recv_sems.at[3 * t + j], device_id=(*chips[j], c), device_id_type=MESH)
                for t in range(n) for j in range(3)]

    def issue(p, slots, sems):
        for cp in copies(p, slots, sems):
            cp.start()

    def drain(p, slots, sems):
        for cp in copies(p, slots, sems):
            cp.wait()

    return _Carried(parts, [jax.ShapeDtypeStruct((3,) + p.shape[1:], p.dtype) for p in parts], (3 * n, 3 * n), issue, drain)


def _pair_share(full):
    n = len(full)

    def body(*refs):
        buf = refs[n:2 * n]
        send_sems, recv_sems = refs[2 * n:]
        x, y, c, _ = _place()
        sends = [pltpu.make_async_remote_copy(src_ref=buf[t].at[c], dst_ref=buf[t].at[c], send_sem=send_sems.at[t], recv_sem=recv_sems.at[t],
                                              device_id=(x, y, 1 - c), device_id_type=MESH) for t in range(n)]
        for cp in sends:
            cp.start()
        for t in range(n):
            pltpu.make_async_remote_copy(src_ref=buf[t].at[c], dst_ref=buf[t].at[1 - c], send_sem=send_sems.at[t], recv_sem=recv_sems.at[t],
                                         device_id=(x, y, 1 - c), device_id_type=MESH).wait_recv()
        for cp in sends:
            cp.wait_send()

    return _comm_call("grad_pair_share", body, full, [jax.ShapeDtypeStruct(f.shape, f.dtype) for f in full], (n, n),
                      aliases={t: t for t in range(n)})


SMALL_ROWS = 128


def _allreduce_small(v):
    def body(v_ref, o_ref, buf, send_sems, recv_sems):
        x, y, c, _ = _place()
        me = 4 * x + 2 * y + c
        flip = lambda a, f: 1 - a if f else a
        buf[me] = v_ref[...]
        peers = [(flip(x, d >> 2 & 1), flip(y, d >> 1 & 1), flip(c, d & 1)) for d in range(1, 8)]
        sends = [pltpu.make_async_remote_copy(src_ref=v_ref, dst_ref=buf.at[me], send_sem=send_sems.at[i], recv_sem=recv_sems.at[i],
                                              device_id=peer, device_id_type=MESH) for i, peer in enumerate(peers)]
        for cp in sends:
            cp.start()
        for i, (px, py, pc) in enumerate(peers):
            pltpu.make_async_remote_copy(src_ref=v_ref, dst_ref=buf.at[4 * px + 2 * py + pc], send_sem=send_sems.at[i], recv_sem=recv_sems.at[i],
                                         device_id=(px, py, pc), device_id_type=MESH).wait_recv()
        for cp in sends:
            cp.wait_send()
        acc = buf[0]
        for k in range(1, 8):
            acc = acc + buf[k]
        o_ref[...] = acc

    vm = pl.BlockSpec(memory_space=pltpu.VMEM)
    return pl.pallas_call(
        body, in_specs=[vm], out_specs=vm, out_shape=jax.ShapeDtypeStruct((SMALL_ROWS, 128), F32),
        scratch_shapes=[pltpu.VMEM((8, SMALL_ROWS, 128), F32), pltpu.SemaphoreType.DMA((7,)), pltpu.SemaphoreType.DMA((7,))],
        compiler_params=pltpu.CompilerParams(has_side_effects=True), name="allreduce_small")(v)


def _row_block(rows, cols, budget=1 << 20):
    best = 8
    for bm in range(8, rows + 1, 8):
        if rows % bm == 0 and bm * cols * 4 <= budget:
            best = bm
    return best


def _pair_sum(name, place, g, theirs):
    _, m, c = theirs.shape
    bm = _row_block(m, c)

    def body(place_ref, a_ref, b_ref, o_ref):
        o_ref[...] = (a_ref[...].astype(F32) + b_ref[...].astype(F32)).astype(o_ref.dtype)

    spec = pl.BlockSpec((None, bm, c), lambda k, i, pr: (k, i, 0))
    return pl.pallas_call(
        body, out_shape=jax.ShapeDtypeStruct(theirs.shape, BF16),
        grid_spec=pltpu.PrefetchScalarGridSpec(
            num_scalar_prefetch=1, grid=(NCHIP, m // bm),
            in_specs=[pl.BlockSpec((None, None, bm, c), lambda k, i, pr: (k, pr[0], i, 0)), spec], out_specs=spec),
        compiler_params=_params(("parallel", "parallel")), name=name)(place, g, theirs)


def _chip_sum(name, place, parts, slots):
    _, m, c = parts.shape
    bm = _row_block(m, c)

    def body(place_ref, p_ref, s_ref, o_ref):
        s = s_ref[...].astype(F32)
        o_ref[...] = ((p_ref[...].astype(F32) + s[0]) + s[1]) + s[2]

    return pl.pallas_call(
        body, out_shape=jax.ShapeDtypeStruct((2, m, c), F32),
        grid_spec=pltpu.PrefetchScalarGridSpec(
            num_scalar_prefetch=1, grid=(m // bm,),
            in_specs=[pl.BlockSpec((None, bm, c), lambda i, pr: (pr[1], i, 0)), pl.BlockSpec((3, bm, c), lambda i, pr: (0, i, 0))],
            out_specs=pl.BlockSpec((None, bm, c), lambda i, pr: (pr[0], i, 0))),
        compiler_params=_params(("parallel",)), name=name)(place, parts, slots)


def _adamw(name, w, g, m, v, layer=0, into=None):
    lead, rows, cols = w.shape
    bm = _row_block(rows, cols, budget=768 * 1024)
    c1 = 1.0 - ADAM_B1 ** ADAM_STEP
    c2 = 1.0 - ADAM_B2 ** ADAM_STEP

    def body(w_ref, g_ref, m_ref, v_ref, *rest):
        go_ref, d_ref, mo_ref, vo_ref = rest[-4:]
        g = g_ref[...]
        mn = ADAM_B1 * m_ref[...] + (1.0 - ADAM_B1) * g
        vn = ADAM_B2 * v_ref[...] + (1.0 - ADAM_B2) * (g * g)
        go_ref[...] = g
        mo_ref[...] = mn
        vo_ref[...] = vn
        d_ref[...] = -ADAM_LR * ((mn / c1) / (jnp.sqrt(vn / c2) + ADAM_EPS) + ADAM_WD * w_ref[...])

    spec = pl.BlockSpec((None, bm, cols), lambda i: (layer, i, 0))
    sh = jax.ShapeDtypeStruct((lead, rows, cols), F32)
    prev = [] if into is None else list(into)
    return pl.pallas_call(
        body, grid=(rows // bm,), in_specs=[spec, pl.BlockSpec((bm, cols), lambda i: (i, 0)), spec, spec] + [pl.BlockSpec(memory_space=pl.ANY)] * len(prev),
        out_specs=[spec] * 4, out_shape=[sh] * 4, input_output_aliases={4 + k: k for k in range(len(prev))},
        compiler_params=_params(("parallel",)), name=name)(w, g, m, v, *prev)


def _pack_small(norms, rpb):
    flat = jnp.concatenate([a.reshape(-1) for a in norms] + [rpb.reshape(-1)])
    return jnp.pad(flat, (0, SMALL_ROWS * 128 - flat.shape[0])).reshape(SMALL_ROWS, 128)


def _unpack_small(p):
    flat = p.reshape(-1)
    norms = [flat[i * 2 * DM:(i + 1) * 2 * DM].reshape(2, DM) for i in range(4)]
    rpb = flat[8 * DM:8 * DM + NH * 15 * 31].reshape(1, NH, 15, 31)
    return norms, rpb


FFN_NAMES = ("ffn_w_gate", "ffn_w_up", "ffn_w_down")
L0_FFN = tuple((n, 0) for n in FFN_NAMES)
L1_FFN = tuple((n, 1) for n in FFN_NAMES)
NA_KEYS = (("na_w_qkv", 0), ("na_w_o", 0))
DIL_KEYS = (("dil_w_qkv", 0), ("dil_w_o", 0))


class _Exchange:
    GATHERS = {"l0_qkv": L0_FFN[:1], "na_fwd": L0_FFN[1:], "l0_ffn_fwd": DIL_KEYS[:1], "dil_fwd": L1_FFN + DIL_KEYS[1:]}
    EXCHANGES = {"dil_bwd": L1_FFN, "l1_dh": DIL_KEYS[1:], "l0_ffn_bwd": DIL_KEYS[:1], "na_bwd": L0_FFN, "l0_dh": NA_KEYS}

    def __init__(self, shards):
        self.chip = 2 * lax.axis_index("x") + lax.axis_index("y")
        self.place = jnp.stack([lax.axis_index("c"), self.chip]).astype(jnp.int32)
        self.own = {k: s.reshape(2, s.shape[0] // 2, s.shape[1]).astype(BF16) for k, s in shards.items()}
        self.gathered, self.parts, self.full = {}, {}, {}
        self._take(NA_KEYS, _run_carried("gather_first", _gather_copies([self.own[k] for k in NA_KEYS])))

    def _take(self, keys, landed):
        for k, gw in zip(keys, landed):
            self.gathered[k] = lax.dynamic_update_slice(gw, self.own[k][None], (self.chip, 0, 0, 0))

    def _sum(self, keys, slots):
        for k, s in zip(keys, slots):
            self.full[k] = _chip_sum(f"chip_sum_{k[0]}_{k[1]}", self.place, self.parts[k], s)

    def weight(self, key):
        g = self.gathered[key]
        return g.reshape(NCHIP, 2 * g.shape[2], g.shape[3])

    def carry(self, tag):
        if tag in self.GATHERS:
            return _gather_copies([self.own[k] for k in self.GATHERS[tag]])
        if tag in self.EXCHANGES:
            return _chip_exchange_copies([self.parts[k] for k in self.EXCHANGES[tag]])
        return None

    def carried(self, tag, landed):
        if tag in self.GATHERS:
            self._take(self.GATHERS[tag], landed)
        elif tag in self.EXCHANGES:
            self._sum(self.EXCHANGES[tag], landed)

    def grads(self, tag, dw):
        keys = tuple(dw)
        mine = [dw[k].reshape(NCHIP, 2, -1, dw[k].shape[-1]) for k in keys]
        theirs = _pair_exchange(f"grad_pair_exchange_{tag}", mine)
        for k, a, b in zip(keys, mine, theirs):
            self.parts[k] = _pair_sum(f"pair_sum_{k[0]}_{k[1]}", self.place, a, b)

    def finish(self):
        keys = tuple(self.full)
        shared = _pair_share([self.full[k] for k in keys])
        return {k: s.reshape(2 * s.shape[1], s.shape[2]) for k, s in zip(keys, shared)}


def kernel(x, norm_mix_pre, norm_mix_post, norm_ffn_pre, norm_ffn_post, na_w_qkv, na_w_o, na_rpb, dil_w_qkv, dil_w_o, ffn_w_gate, ffn_w_up, ffn_w_down, loss_target, m_norm_mix_pre, m_norm_mix_post, m_norm_ffn_pre, m_norm_ffn_post, m_na_w_qkv, m_na_w_o, m_na_rpb, m_dil_w_qkv, m_dil_w_o, m_ffn_w_gate, m_ffn_w_up, m_ffn_w_down, v_norm_mix_pre, v_norm_mix_post, v_norm_ffn_pre, v_norm_ffn_post, v_na_w_qkv, v_na_w_o, v_na_rpb, v_dil_w_qkv, v_dil_w_o, v_ffn_w_gate, v_ffn_w_up, v_ffn_w_down):
    tr = lambda a: jnp.swapaxes(a, 1, 2)
    weights = {"na_w_qkv": na_w_qkv, "na_w_o": na_w_o, "dil_w_qkv": dil_w_qkv, "dil_w_o": dil_w_o,
               "ffn_w_gate": tr(ffn_w_gate), "ffn_w_up": tr(ffn_w_up), "ffn_w_down": ffn_w_down}
    m_in = {"na_w_qkv": m_na_w_qkv, "na_w_o": m_na_w_o, "dil_w_qkv": m_dil_w_qkv, "dil_w_o": m_dil_w_o,
            "ffn_w_gate": tr(m_ffn_w_gate), "ffn_w_up": tr(m_ffn_w_up), "ffn_w_down": m_ffn_w_down}
    v_in = {"na_w_qkv": v_na_w_qkv, "na_w_o": v_na_w_o, "dil_w_qkv": v_dil_w_qkv, "dil_w_o": v_dil_w_o,
            "ffn_w_gate": tr(v_ffn_w_gate), "ffn_w_up": tr(v_ffn_w_up), "ffn_w_down": v_ffn_w_down}

    ex = _Exchange({(n, l): weights[n][l] for n in weights for l in range(weights[n].shape[0])})
    norms = (norm_mix_pre, norm_mix_post, norm_ffn_pre, norm_ffn_post)
    loss_row, dx, dnorms, d_rpb = _local_step(x[0], loss_target[0], norms, na_rpb[0], ex)
    loss = lax.psum(loss_row[0, 0], ("x", "y", "c"))
    full = ex.finish()
    small = _allreduce_small(_pack_small(dnorms, d_rpb))

    out_g, out_d, out_m, out_v = {}, {}, {}, {}
    for n in weights:
        res = None
        for l in range(weights[n].shape[0]):
            res = _adamw(f"adamw_{n}_{l}", weights[n], full[(n, l)], m_in[n], v_in[n], l, res)
        if n in ("ffn_w_gate", "ffn_w_up"):
            res = [tr(r) for r in res]
        out_g[n], out_d[n], out_m[n], out_v[n] = res
    sm_names = ("norm_mix_pre", "norm_mix_post", "norm_ffn_pre", "norm_ffn_post", "na_rpb")
    sm = _adamw("adamw_small", _pack_small(norms, na_rpb)[None], small,
                _pack_small((m_norm_mix_pre, m_norm_mix_post, m_norm_ffn_pre, m_norm_ffn_post), m_na_rpb)[None],
                _pack_small((v_norm_mix_pre, v_norm_mix_post, v_norm_ffn_pre, v_norm_ffn_post), v_na_rpb)[None])
    for res, dst in zip(sm, (out_g, out_d, out_m, out_v)):
        ns, rp = _unpack_small(res)
        for n, a in zip(sm_names, ns + [rp]):
            dst[n] = a

    order = ("norm_mix_pre", "norm_mix_post", "norm_ffn_pre", "norm_ffn_post", "na_w_qkv", "na_w_o", "na_rpb", "dil_w_qkv", "dil_w_o",
             "ffn_w_gate", "ffn_w_up", "ffn_w_down")
    return (loss, dx[None], *[out_g[n] for n in order], *[out_d[n] for n in order], *[out_m[n] for n in order], *[out_v[n] for n in order])
```

```python
import functools

import numpy as np
import jax
import jax.numpy as jnp
from jax import lax
from jax.experimental import pallas as pl
from jax.experimental.pallas import tpu as pltpu

F32 = jnp.float32
BF16 = jnp.bfloat16

SEQ = 2048
DM = 1024
NH = 16
HD = 64
DFF = 2816
NCHIP = 4
FSH = DFF // NCHIP
GRID_W = 64
NA_QROWS = 4
NA_QB = NA_QROWS * GRID_W
NA_WROWS = 12
NA_WIN = NA_WROWS * GRID_W
DIL = (1, 4, 16)
DIL_QB = 256
DIL_WIN = DIL_QB + 128
DIL_RADIUS = 64
RMS_EPS = 1e-6
NEG = -1e30
QSCALE = HD ** -0.5
CH = 256
MESH = pl.DeviceIdType.MESH

ADAM_LR, ADAM_B1, ADAM_B2, ADAM_EPS, ADAM_WD, ADAM_STEP = 0.001, 0.9, 0.999, 1e-08, 0.01, 10

VMEM_LIMIT = 56 * 1024 * 1024

_NN = (((1,), (0,)), ((), ()))
_NT = (((1,), (1,)), ((), ()))
_TN = (((0,), (0,)), ((), ()))


def _params(sem):
    return pltpu.CompilerParams(dimension_semantics=sem, vmem_limit_bytes=VMEM_LIMIT)


def _matmul(name, pairs, grid, out_shape, out_spec, acc_shape, carrying=False, carry=None):
    nk = grid[-1]
    npair = len(pairs)
    n_in = 2 * npair

    def body(*refs):
        ins, o_ref = refs[:2 * npair], refs[n_in]
        part = None
        for p in range(npair):
            d = lax.dot_general(ins[2 * p][...].astype(BF16), ins[2 * p + 1][...].astype(BF16), pairs[p][4],
                                preferred_element_type=F32)
            part = d if part is None else part + d
        if nk == 1:
            o_ref[...] = part.astype(o_ref.dtype)
        else:
            acc_ref = refs[n_in + 1]
            kk = pl.program_id(len(grid) - 1)

            @pl.when(kk == 0)
            def _():
                acc_ref[...] = part

            @pl.when(kk > 0)
            def _():
                acc_ref[...] += part

            @pl.when(kk == nk - 1)
            def _():
                o_ref[...] = acc_ref[...].astype(o_ref.dtype)

    ops, specs = [], []
    for a, a_spec, b, b_spec, _ in pairs:
        ops += [a, b]
        specs += [a_spec, b_spec]
    (out,), sent = _carrier_call(name, body, grid, specs, [out_spec], [out_shape], [] if nk == 1 else [pltpu.VMEM(acc_shape, F32)], ops, carry)
    return (out, sent) if carrying else out


def _qkv_fwd(name, h_all, w4, carry):
    g_n = h_all.shape[0]
    per = w4.shape[2] // CH
    return _matmul(
        name, [(h_all, pl.BlockSpec((None, SEQ, DM), lambda g, q, k: (g, 0, 0)),
                w4, pl.BlockSpec((None, DM, CH), lambda g, q, k: ((g * 12 + q) // per, 0, (g * 12 + q) % per)), _NN)],
        (g_n, 12, 1), jax.ShapeDtypeStruct((g_n, SEQ, 3 * DM), BF16),
        pl.BlockSpec((None, SEQ, CH), lambda g, q, k: (g, 0, q)), None, carrying=True, carry=carry)


def _qkv_bwd_dh(name, dqkv, w4, carry):
    g_n = dqkv.shape[0]
    per = w4.shape[2] // CH
    tm = SEQ

    def pair(cb):
        chunk = lambda g, t: g * 12 + t * 4 + cb
        return (dqkv, pl.BlockSpec((None, None, tm, CH), lambda g, i, t: (g, t, i, cb)),
                w4, pl.BlockSpec((None, DM, CH), lambda g, i, t: (chunk(g, t) // per, 0, chunk(g, t) % per)), _NT)

    return _matmul(name, [pair(cb) for cb in range(4)], (g_n, SEQ // tm, 3), jax.ShapeDtypeStruct((g_n, SEQ, DM), F32),
                   pl.BlockSpec((None, tm, DM), lambda g, i, t: (g, i, 0)), (tm, DM), carrying=True, carry=carry)


def _qkv_bwd_dw(name, ht_all, dqkv, shard_cols):
    g_n = dqkv.shape[0]
    per = shard_cols // CH
    return _matmul(
        name, [(ht_all, pl.BlockSpec((None, DM, SEQ), lambda qq, k: (qq // 12, 0, 0)),
                dqkv, pl.BlockSpec((None, None, SEQ, CH), lambda qq, k: (qq // 12, (qq % 12) // 4, 0, qq % 4)), _NN)],
        (g_n * 12, 1), jax.ShapeDtypeStruct((NCHIP, DM, shard_cols), BF16),
        pl.BlockSpec((None, DM, CH), lambda qq, k: (qq // per, 0, qq % per)), None)


def _proj_fwd(name, o, wo, x, g):
    tm = 512

    def body(o_ref, w_ref, x_ref, g_ref, xn_ref, u_ref):
        u = jnp.dot(o_ref[...], w_ref[...], preferred_element_type=F32)
        u_ref[...] = u
        r = lax.rsqrt(jnp.mean(u * u, axis=-1, keepdims=True) + RMS_EPS)
        xn_ref[...] = x_ref[...] + u * r * g_ref[...]

    rows = pl.BlockSpec((tm, DM), lambda i: (i, 0))
    sh = jax.ShapeDtypeStruct((SEQ, DM), F32)
    return pl.pallas_call(
        body, grid=(SEQ // tm,), in_specs=[rows, pl.BlockSpec((DM, DM), lambda i: (0, 0)), rows, pl.BlockSpec((1, DM), lambda i: (0, 0))],
        out_specs=[rows, rows], out_shape=[sh, sh], compiler_params=_params(("parallel",)), name=name)(o, wo, x, g)


def _proj_bwd(name, dy, u, g, wo, dtype, carry):
    tm = 512

    def body(dy_ref, u_ref, g_ref, w_ref, do_ref, du_ref, dg_ref):
        dy = dy_ref[...]
        u = u_ref[...]
        r = lax.rsqrt(jnp.mean(u * u, axis=-1, keepdims=True) + RMS_EPS)
        yh = u * r
        t = dy * g_ref[...]
        du = (r * (t - yh * jnp.mean(t * yh, axis=-1, keepdims=True))).astype(BF16)
        du_ref[...] = du
        do_ref[...] = lax.dot_general(du, w_ref[...], _NT, preferred_element_type=F32).astype(do_ref.dtype)

        @pl.when(pl.program_id(0) == 0)
        def _():
            dg_ref[...] = jnp.zeros_like(dg_ref)

        dg_ref[...] += jnp.sum(dy * yh, axis=0, keepdims=True)

    rows = pl.BlockSpec((tm, DM), lambda i: (i, 0))
    vec = pl.BlockSpec((1, DM), lambda i: (0, 0))
    return _carrier_call(
        name, body, (SEQ // tm,), [rows, rows, vec, pl.BlockSpec((DM, DM), lambda i: (0, 0))], [rows, rows, vec],
        [jax.ShapeDtypeStruct((SEQ, DM), dtype), jax.ShapeDtypeStruct((SEQ, DM), BF16), jax.ShapeDtypeStruct((1, DM), F32)],
        [], (dy, u, g, wo), carry)


def _proj_bwd_dw(name, o, du):
    tn = 512
    return _matmul(
        name, [(o, pl.BlockSpec((SEQ, DM), lambda j, k: (0, 0)), du, pl.BlockSpec((SEQ, tn), lambda j, k: (0, j)), _TN)],
        (DM // tn, 1), jax.ShapeDtypeStruct((DM, DM), BF16), pl.BlockSpec((DM, tn), lambda j, k: (0, j)), None)


def _ffn_wspec(index_map):
    return pl.BlockSpec((None, FSH, DM), index_map)


def _ffn_bwd_dw(name, a4, b):
    return _matmul(
        name, [(a4, pl.BlockSpec((None, SEQ, FSH), lambda s, k: (s, 0, 0)), b, pl.BlockSpec((SEQ, DM), lambda s, k: (0, 0)), _TN)],
        (NCHIP, 1), jax.ShapeDtypeStruct((NCHIP, FSH, DM), BF16), _ffn_wspec(lambda s, k: (s, 0, 0)), None)


ROWS = 256


def _row_spec():
    return pl.BlockSpec((ROWS, DM), lambda i: (i, 0))


def _vec_spec():
    return pl.BlockSpec((1, DM), lambda i: (0, 0))


def _rms_fwd(name, x, g, dtype=BF16):
    def body(x_ref, g_ref, o_ref):
        x = x_ref[...]
        r = lax.rsqrt(jnp.mean(x * x, axis=-1, keepdims=True) + RMS_EPS)
        o_ref[...] = (x * r * g_ref[...]).astype(o_ref.dtype)

    return pl.pallas_call(body, grid=(SEQ // ROWS,), in_specs=[_row_spec(), _vec_spec()], out_specs=_row_spec(),
                          out_shape=jax.ShapeDtypeStruct((SEQ, DM), dtype), compiler_params=_params(("parallel",)), name=name)(x, g)


def _rms_fwd_both(name, x, g):
    def body(x_ref, g_ref, o_ref, t_ref):
        x = x_ref[...]
        r = lax.rsqrt(jnp.mean(x * x, axis=-1, keepdims=True) + RMS_EPS)
        h = x * r * g_ref[...]
        o_ref[...] = h.astype(o_ref.dtype)
        t_ref[...] = h.T.astype(t_ref.dtype)

    return pl.pallas_call(
        body, grid=(SEQ // ROWS,), in_specs=[_row_spec(), _vec_spec()], out_specs=[_row_spec(), pl.BlockSpec((DM, ROWS), lambda i: (0, i))],
        out_shape=[jax.ShapeDtypeStruct((SEQ, DM), BF16), jax.ShapeDtypeStruct((DM, SEQ), BF16)],
        compiler_params=_params(("parallel",)), name=name)(x, g)


def _norm_bwd(name, dys, u, g, res=None):
    ndy = len(dys)

    def body(*refs):
        dy = refs[0][...]
        for r_ in refs[1:ndy]:
            dy = dy + r_[...]
        u_ref, g_ref = refs[ndy], refs[ndy + 1]
        res_ref = refs[ndy + 2] if res is not None else None
        du_ref, dg_ref = refs[-2], refs[-1]
        u = u_ref[...]
        r = lax.rsqrt(jnp.mean(u * u, axis=-1, keepdims=True) + RMS_EPS)
        yh = u * r
        t = dy * g_ref[...]
        du = r * (t - yh * jnp.mean(t * yh, axis=-1, keepdims=True))
        if res_ref is not None:
            du = du + res_ref[...]
        du_ref[...] = du

        @pl.when(pl.program_id(0) == 0)
        def _():
            dg_ref[...] = jnp.zeros_like(dg_ref)

        dg_ref[...] += jnp.sum(dy * yh, axis=0, keepdims=True)

    ops = list(dys) + [u, g] + ([res] if res is not None else [])
    specs = [_row_spec()] * ndy + [_row_spec(), _vec_spec()] + ([_row_spec()] if res is not None else [])
    return pl.pallas_call(
        body, grid=(SEQ // ROWS,), in_specs=specs, out_specs=[_row_spec(), _vec_spec()],
        out_shape=[jax.ShapeDtypeStruct((SEQ, DM), F32), jax.ShapeDtypeStruct((1, DM), F32)],
        compiler_params=_params(("arbitrary",)), name=name)(*ops)


def _loss_grad(name, y, t):
    def body(y_ref, t_ref, dy_ref, l_ref):
        e = y_ref[...] - t_ref[...]
        dy_ref[...] = e * (1.0 / DM)

        @pl.when(pl.program_id(0) == 0)
        def _():
            l_ref[...] = jnp.zeros_like(l_ref)

        l_ref[...] += jnp.sum(e * e) * (0.5 / DM)

    return pl.pallas_call(
        body, grid=(SEQ // ROWS,), in_specs=[_row_spec(), _row_spec()],
        out_specs=[_row_spec(), pl.BlockSpec((1, 128), lambda i: (0, 0))],
        out_shape=[jax.ShapeDtypeStruct((SEQ, DM), F32), jax.ShapeDtypeStruct((1, 128), F32)],
        compiler_params=_params(("arbitrary",)), name=name)(y, t)


HBM_SPEC = pl.BlockSpec(memory_space=pltpu.HBM)


class _Carried:
    def __init__(self, ins, out_shapes, n_sems, issue, drain):
        self.ins, self.out_shapes, self.n_sems, self.issue, self.drain = list(ins), list(out_shapes), tuple(n_sems), issue, drain


def _carrier_call(name, body, grid, in_specs, out_specs, out_shape, scratch_shapes, operands, carry):
    n_in, n_out, n_scr = len(in_specs), len(out_specs), len(scratch_shapes)
    if carry is None:
        res = pl.pallas_call(body, grid=grid, in_specs=in_specs, out_specs=out_specs, out_shape=out_shape, scratch_shapes=scratch_shapes,
                             compiler_params=_params(("arbitrary",) * len(grid)), name=name)(*operands)
        return list(res), []
    ci, co = len(carry.ins), len(carry.out_shapes)

    def wrapped(*refs):
        ins, cins = refs[:n_in], refs[n_in:n_in + ci]
        outs, couts = refs[n_in + ci:n_in + ci + n_out], refs[n_in + ci + n_out:n_in + ci + n_out + co]
        scr, sems = refs[n_in + ci + n_out + co:n_in + ci + n_out + co + n_scr], refs[n_in + ci + n_out + co + n_scr:]
        first = functools.reduce(jnp.logical_and, [pl.program_id(a) == 0 for a in range(len(grid))])
        last = functools.reduce(jnp.logical_and, [pl.program_id(a) == grid[a] - 1 for a in range(len(grid))])

        @pl.when(first)
        def _():
            carry.issue(cins, couts, sems)

        body(*ins, *outs, *scr)

        @pl.when(last)
        def _():
            carry.drain(cins, couts, sems)

    res = pl.pallas_call(
        wrapped, grid=grid, in_specs=list(in_specs) + [HBM_SPEC] * ci, out_specs=list(out_specs) + [HBM_SPEC] * co,
        out_shape=list(out_shape) + carry.out_shapes,
        scratch_shapes=list(scratch_shapes) + [pltpu.SemaphoreType.DMA((k,)) for k in carry.n_sems],
        compiler_params=pltpu.CompilerParams(dimension_semantics=("arbitrary",) * len(grid), vmem_limit_bytes=VMEM_LIMIT, has_side_effects=True),
        name=name)(*operands, *carry.ins)
    return list(res[:n_out]), list(res[n_out:])


def _run_carried(name, carry):
    def body(*refs):
        ci, co = len(carry.ins), len(carry.out_shapes)
        carry.issue(refs[:ci], refs[ci:ci + co], refs[ci + co:])
        carry.drain(refs[:ci], refs[ci:ci + co], refs[ci + co:])

    return pl.pallas_call(
        body, in_specs=[HBM_SPEC] * len(carry.ins), out_specs=[HBM_SPEC] * len(carry.out_shapes), out_shape=carry.out_shapes,
        scratch_shapes=[pltpu.SemaphoreType.DMA((k,)) for k in carry.n_sems],
        compiler_params=pltpu.CompilerParams(has_side_effects=True), name=name)(*carry.ins)


NA_BLOCKS = SEQ // NA_QB
NA_ROWS_TOTAL = SEQ // GRID_W
NA_CLASSES = ((0, 0), (8, 4), (NA_ROWS_TOTAL - NA_QROWS, NA_ROWS_TOTAL - NA_WROWS))


def _na_pairs(i0, ws):
    out = []
    for qi in range(NA_QROWS):
        i = i0 + qi
        rs = min(max(i - 4, 0), NA_ROWS_TOTAL - 8)
        for kr in range(NA_WROWS):
            r = ws + kr
            if rs <= r < rs + 8:
                out.append((qi, kr, r - i + 7))
    return out


def _diag_onehot():
    qc, kc = np.meshgrid(np.arange(GRID_W), np.arange(GRID_W), indexing="ij")
    e = np.zeros((GRID_W * GRID_W, 128), np.float32)
    j = (kc - qc + 15).reshape(-1)
    ok = (j >= 0) & (j <= 30)
    e[np.arange(GRID_W * GRID_W)[ok], j[ok]] = 1.0
    return jnp.asarray(e)


def _rpb_expand(rpb):
    r2 = jnp.pad(rpb.reshape(NH * 15, 31), ((0, 0), (0, 128 - 31)))

    def body(r_ref, e_ref, o_ref):
        o_ref[...] = lax.dot_general(r_ref[...], e_ref[...], _NT, preferred_element_type=F32, precision=lax.Precision.HIGHEST)

    out = pl.pallas_call(body, out_shape=jax.ShapeDtypeStruct((NH * 15, GRID_W * GRID_W), F32), name="rpb_expand",
                         compiler_params=pltpu.CompilerParams(vmem_limit_bytes=VMEM_LIMIT))(r2, _diag_onehot())
    return out.reshape(NH, 15, GRID_W, GRID_W)


def _na_bias_tiles(rpb, carry):
    def body(b_ref, o_ref):
        qc = lax.broadcasted_iota(jnp.int32, (GRID_W, GRID_W), 0)
        kc = lax.broadcasted_iota(jnp.int32, (GRID_W, GRID_W), 1)
        first = jnp.clip(qc - 8, 0, GRID_W - 16)
        in_window = (kc >= first) & (kc < first + 16)
        neg = jnp.full((GRID_W, GRID_W), NEG, F32)
        for cls, (i0, ws) in enumerate(NA_CLASSES):
            @pl.when(pl.program_id(0) == cls)
            def _(i0=i0, ws=ws):
                pairs = {(qi, kr): dr for qi, kr, dr in _na_pairs(i0, ws)}
                masked = {dr: jnp.where(in_window, b_ref[dr], NEG) for dr in sorted(set(pairs.values()))}
                for qi in range(NA_QROWS):
                    for k2 in range(NA_WROWS // 2):
                        blocks = [masked[pairs[(qi, kr)]] if (qi, kr) in pairs else neg for kr in (2 * k2, 2 * k2 + 1)]
                        o_ref[qi * GRID_W:(qi + 1) * GRID_W, k2 * 128:(k2 + 1) * 128] = jnp.concatenate(blocks, axis=1)

    (tiles,), sent = _carrier_call(
        "na_bias_tiles", body, (3, NH), [pl.BlockSpec((None, 15, GRID_W, GRID_W), lambda c, h: (h, 0, 0, 0))],
        [pl.BlockSpec((None, None, NA_QB, NA_WIN), lambda c, h: (c, h, 0, 0))], [jax.ShapeDtypeStruct((3, NH, NA_QB, NA_WIN), F32)],
        [], (_rpb_expand(rpb),), carry)
    return tiles, sent


def _na_cls(b):
    return jnp.where(b == 0, 0, jnp.where(b == NA_BLOCKS - 1, 2, 1))


def _na_start(b):
    return pl.multiple_of(jnp.clip(b * NA_QROWS - 4, 0, NA_ROWS_TOTAL - NA_WROWS) * GRID_W, GRID_W)


HPS = 4
LW = HPS * HD
NLW = DM // LW


NA_BWD_HPS = 4


def _na_in_specs(hps=HPS):
    lw = hps * HD
    nlw = DM // lw
    return [pl.BlockSpec((NA_QB, lw), lambda hp, b: (b, hp)),
            pl.BlockSpec((SEQ, lw), lambda hp, b: (0, nlw + hp)),
            pl.BlockSpec((SEQ, lw), lambda hp, b: (0, 2 * nlw + hp)),
            pl.BlockSpec((None, hps, NA_QB, NA_WIN), lambda hp, b: (_na_cls(b), hp, 0, 0))]


def _na_fwd(qkv, bias, carry):
    def body(q_ref, k_ref, v_ref, b_ref, o_ref):
        start = _na_start(pl.program_id(1))
        q = q_ref[...]
        kw = k_ref[pl.ds(start, NA_WIN), :]
        vw = v_ref[pl.ds(start, NA_WIN), :]
        outs = []
        for hh in range(HPS):
            sl = slice(hh * HD, (hh + 1) * HD)
            s = lax.dot_general(q[:, sl] * QSCALE, kw[:, sl], _NT, preferred_element_type=F32) + b_ref[hh]
            p = jnp.exp(s - jnp.max(s, axis=-1, keepdims=True))
            l = jnp.sum(p, axis=-1, keepdims=True)
            outs.append(jnp.dot(p.astype(BF16), vw[:, sl], preferred_element_type=F32) / l)
        o_ref[...] = jnp.concatenate(outs, axis=1).astype(o_ref.dtype)

    (o,), sent = _carrier_call(
        "na_fwd", body, (NLW, NA_BLOCKS), _na_in_specs(), [pl.BlockSpec((NA_QB, LW), lambda hp, b: (b, hp))],
        [jax.ShapeDtypeStruct((SEQ, DM), BF16)], [], (qkv, qkv, qkv, bias), carry)
    return o, sent


def _na_bwd(qkv, bias, do, carry):
    lw = NA_BWD_HPS * HD

    def body(q_ref, k_ref, v_ref, b_ref, do_ref, dqkv_ref, z_ref, dk_acc, dv_acc):
        blk = pl.program_id(1)

        @pl.when(blk == 0)
        def _():
            dk_acc[...] = jnp.zeros_like(dk_acc)
            dv_acc[...] = jnp.zeros_like(dv_acc)
            z_ref[...] = jnp.zeros_like(z_ref)

        start = _na_start(blk)
        q = q_ref[...]
        do = do_ref[...]
        kw = k_ref[pl.ds(start, NA_WIN), :]
        vw = v_ref[pl.ds(start, NA_WIN), :]
        dqs, dks, dvs, dss = [], [], [], []
        for hh in range(NA_BWD_HPS):
            sl = slice(hh * HD, (hh + 1) * HD)
            qh = q[:, sl] * QSCALE
            s = lax.dot_general(qh, kw[:, sl], _NT, preferred_element_type=F32) + b_ref[hh]
            p = jnp.exp(s - jnp.max(s, axis=-1, keepdims=True))
            p = p / jnp.sum(p, axis=-1, keepdims=True)
            dp = lax.dot_general(do[:, sl], vw[:, sl], _NT, preferred_element_type=F32)
            ds = p * (dp - jnp.sum(p * dp, axis=-1, keepdims=True))
            dsb = ds.astype(BF16)
            dqs.append(jnp.dot(dsb, kw[:, sl], preferred_element_type=F32) * QSCALE)
            dks.append(lax.dot_general(dsb, qh, _TN, preferred_element_type=F32))
            dvs.append(lax.dot_general(p.astype(BF16), do[:, sl], _TN, preferred_element_type=F32))
            dss.append(ds)
        for cls, (i0, ws) in enumerate(NA_CLASSES):
            @pl.when(_na_cls(blk) == cls)
            def _(i0=i0, ws=ws):
                for hh, ds in enumerate(dss):
                    for qi, kr, dr in _na_pairs(i0, ws):
                        z_ref[hh, dr * GRID_W:(dr + 1) * GRID_W, :] += ds[qi * GRID_W:(qi + 1) * GRID_W, kr * GRID_W:(kr + 1) * GRID_W]
        dqkv_ref[0, pl.ds(pl.multiple_of(blk * NA_QB, NA_QB), NA_QB), :] = jnp.concatenate(dqs, axis=1).astype(dqkv_ref.dtype)
        dk_acc[pl.ds(start, NA_WIN), :] += jnp.concatenate(dks, axis=1)
        dv_acc[pl.ds(start, NA_WIN), :] += jnp.concatenate(dvs, axis=1)

        @pl.when(blk == NA_BLOCKS - 1)
        def _():
            dqkv_ref[1] = dk_acc[...].astype(dqkv_ref.dtype)
            dqkv_ref[2] = dv_acc[...].astype(dqkv_ref.dtype)

    (dqkv, z), sent = _carrier_call(
        "na_bwd", body, (NH // NA_BWD_HPS, NA_BLOCKS),
        _na_in_specs(NA_BWD_HPS) + [pl.BlockSpec((NA_QB, lw), lambda hp, b: (b, hp))],
        [pl.BlockSpec((3, SEQ, lw), lambda hp, b: (0, 0, hp)), pl.BlockSpec((NA_BWD_HPS, 15 * GRID_W, GRID_W), lambda hp, b: (hp, 0, 0))],
        [jax.ShapeDtypeStruct((3, SEQ, DM), BF16), jax.ShapeDtypeStruct((NH, 15 * GRID_W, GRID_W), F32)],
        [pltpu.VMEM((SEQ, lw), F32), pltpu.VMEM((SEQ, lw), F32)], (qkv, qkv, qkv, bias, do), carry)
    return dqkv, z, sent


def _rpb_grad(z):
    z2 = z.reshape(NH * 15, GRID_W * GRID_W)

    def body(z_ref, e_ref, o_ref):
        o_ref[...] = jnp.dot(z_ref[...], e_ref[...], preferred_element_type=F32, precision=lax.Precision.HIGHEST)

    out = pl.pallas_call(body, out_shape=jax.ShapeDtypeStruct((NH * 15, 128), F32), name="rpb_grad",
                         compiler_params=pltpu.CompilerParams(vmem_limit_bytes=VMEM_LIMIT))(z2, _diag_onehot())
    return out[:, :31].reshape(NH, 15, 31)


DIL_BLOCKS = SEQ // DIL_QB
DIL_HPS = 8
DIL_LW = DIL_HPS * HD
DIL_NLW = DM // DIL_LW


COLS = 128


def _col_spec():
    return pl.BlockSpec((SEQ, COLS), lambda j: (0, j))


def _grp_spec():
    return pl.BlockSpec((3, SEQ, COLS), lambda j: (0, 0, j))


def _store_group_order(dst_ref, src_ref):
    for g, d in enumerate(DIL):
        n = SEQ // d
        for r in range(d):
            dst_ref[g, r * n:(r + 1) * n, :] = src_ref[pl.ds(r, n, stride=d), :].astype(dst_ref.dtype)


def _store_token_order(dst_ref, src_ref, g):
    d = DIL[g]
    n = SEQ // d
    for r in range(d):
        dst_ref[pl.ds(r, n, stride=d), :] = src_ref[g, r * n:(r + 1) * n, :]


def _to_groups(name, a):
    def body(a_ref, o_ref, t_ref):
        _store_group_order(o_ref, a_ref)
        for g in range(3):
            t_ref[g] = o_ref[g].astype(F32).T.astype(t_ref.dtype)

    return pl.pallas_call(
        body, grid=(DM // COLS,), in_specs=[_col_spec()], out_specs=[_grp_spec(), pl.BlockSpec((3, COLS, SEQ), lambda j: (0, j, 0))],
        out_shape=[jax.ShapeDtypeStruct((3, SEQ, DM), BF16), jax.ShapeDtypeStruct((3, DM, SEQ), BF16)],
        compiler_params=_params(("parallel",)), name=name)(a)


def _from_groups_sum(name, a):
    def body(a_ref, o_ref, t1, t2):
        _store_token_order(t1, a_ref, 1)
        _store_token_order(t2, a_ref, 2)
        o_ref[...] = (a_ref[0] + t1[...]) + t2[...]

    return pl.pallas_call(body, grid=(DM // COLS,), in_specs=[_grp_spec()], out_specs=_col_spec(),
                          out_shape=jax.ShapeDtypeStruct((SEQ, DM), F32), scratch_shapes=[pltpu.VMEM((SEQ, COLS), F32)] * 2,
                          compiler_params=_params(("parallel",)), name=name)(a)


def _dil_start(b):
    return pl.multiple_of(jnp.clip(b * DIL_QB - DIL_RADIUS, 0, SEQ - DIL_WIN), DIL_RADIUS)


def _dil_mask(g, b, start):
    shift = 11 - 2 * g
    ii = b * DIL_QB + lax.broadcasted_iota(jnp.int32, (DIL_QB, DIL_WIN), 0)
    jj = start + lax.broadcasted_iota(jnp.int32, (DIL_QB, DIL_WIN), 1)
    dist = jnp.abs(ii - jj)
    valid = (dist <= DIL_RADIUS) & (jnp.right_shift(ii, shift) == jnp.right_shift(jj, shift))
    return valid, dist.astype(F32)


def _dil_in_specs():
    return [pl.BlockSpec(memory_space=pltpu.SMEM),
            pl.BlockSpec((None, DIL_QB, DIL_LW), lambda g, hp, b: (g, b, hp)),
            pl.BlockSpec((None, SEQ, DIL_LW), lambda g, hp, b: (g, 0, DIL_NLW + hp)),
            pl.BlockSpec((None, SEQ, DIL_LW), lambda g, hp, b: (g, 0, 2 * DIL_NLW + hp))]


def _dil_fwd(qkv, slopes, carry):
    def body(sl_ref, q_ref, k_ref, v_ref, o_ref, lse_ref):
        g, hp, b = pl.program_id(0), pl.program_id(1), pl.program_id(2)
        start = _dil_start(b)
        valid, dist = _dil_mask(g, b, start)
        dil = jnp.left_shift(1, 2 * g).astype(F32)
        q = q_ref[...]
        kw = k_ref[pl.ds(start, DIL_WIN), :]
        vw = v_ref[pl.ds(start, DIL_WIN), :]
        outs, lses = [], []
        for hh in range(DIL_HPS):
            sl = slice(hh * HD, (hh + 1) * HD)
            s = lax.dot_general(q[:, sl] * QSCALE, kw[:, sl], _NT, preferred_element_type=F32)
            s = jnp.where(valid, s - (sl_ref[hp * DIL_HPS + hh] * dil) * dist, NEG)
            m = jnp.max(s, axis=-1, keepdims=True)
            p = jnp.exp(s - m)
            l = jnp.sum(p, axis=-1, keepdims=True)
            outs.append(jnp.dot(p.astype(BF16), vw[:, sl], preferred_element_type=F32) / l)
            lses.append(jnp.broadcast_to(m + jnp.log(l), (DIL_QB, HD)))
        o_ref[...] = jnp.concatenate(outs, axis=1)
        lse_ref[...] = jnp.concatenate(lses, axis=1)

    ospec = pl.BlockSpec((None, DIL_QB, DIL_LW), lambda g, hp, b: (g, b, hp))
    sh = jax.ShapeDtypeStruct((3, SEQ, DM), F32)
    (o, lse), sent = _carrier_call("dil_fwd", body, (3, DIL_NLW, DIL_BLOCKS), _dil_in_specs(), [ospec, ospec], [sh, sh], [],
                                   (slopes, qkv, qkv, qkv), carry)
    return o, lse, sent


def _dil_merge(o_all, lse_all):
    def body(o_ref, l_ref, out_ref, lse_ref, o1, o2, l1, l2):
        for g, (ot, lt) in ((1, (o1, l1)), (2, (o2, l2))):
            _store_token_order(ot, o_ref, g)
            _store_token_order(lt, l_ref, g)
        la, lb, lc = l_ref[0], l1[...], l2[...]
        m = jnp.maximum(jnp.maximum(la, lb), lc)
        wa, wb, wc = jnp.exp(la - m), jnp.exp(lb - m), jnp.exp(lc - m)
        sw = (wa + wb) + wc
        out_ref[...] = (((wa * o_ref[0] + wb * o1[...]) + wc * o2[...]) / sw).astype(out_ref.dtype)
        lse_ref[...] = m + jnp.log(sw)

    return pl.pallas_call(
        body, grid=(DM // COLS,), in_specs=[_grp_spec(), _grp_spec()], out_specs=[_col_spec(), _col_spec()],
        out_shape=[jax.ShapeDtypeStruct((SEQ, DM), BF16), jax.ShapeDtypeStruct((SEQ, DM), F32)],
        scratch_shapes=[pltpu.VMEM((SEQ, COLS), F32)] * 4, compiler_params=_params(("parallel",)), name="dil_merge")(o_all, lse_all)


def _dil_bwd_prep(do, o, lse):
    heads = COLS // HD

    def body(do_ref, o_ref, lse_ref, dog_ref, ddr_ref, lser_ref, dd, grp):
        prod = do_ref[...] * o_ref[...].astype(F32)
        dd[...] = jnp.concatenate(
            [jnp.broadcast_to(jnp.sum(prod[:, h * HD:(h + 1) * HD], axis=-1, keepdims=True), (SEQ, HD)) for h in range(heads)], axis=1)
        _store_group_order(dog_ref, do_ref)
        for src, dst in ((dd, ddr_ref), (lse_ref, lser_ref)):
            _store_group_order(grp, src)
            for g in range(3):
                t = grp[g].T
                for h in range(heads):
                    dst[g, h] = t[h * HD:h * HD + 8, :]

    rows = jax.ShapeDtypeStruct((3, NH, 8, SEQ), F32)
    rspec = pl.BlockSpec((3, heads, 8, SEQ), lambda j: (0, j, 0, 0))
    return pl.pallas_call(
        body, grid=(DM // COLS,), in_specs=[_col_spec()] * 3, out_specs=[_grp_spec(), rspec, rspec],
        out_shape=[jax.ShapeDtypeStruct((3, SEQ, DM), BF16), rows, rows],
        scratch_shapes=[pltpu.VMEM((SEQ, COLS), F32), pltpu.VMEM((3, SEQ, COLS), F32)],
        compiler_params=_params(("parallel",)), name="dil_bwd_prep")(do, o, lse)


def _dil_bwd(qkv, do, dd, lse, slopes, carry):
    def body(sl_ref, q_ref, k_ref, v_ref, do_ref, dd_ref, lse_ref, dqkv_ref, dk_acc, dv_acc):
        g, hp, b = pl.program_id(0), pl.program_id(1), pl.program_id(2)

        @pl.when(b == 0)
        def _():
            dk_acc[...] = jnp.zeros_like(dk_acc)
            dv_acc[...] = jnp.zeros_like(dv_acc)

        start = _dil_start(b)
        shift = 11 - 2 * g
        jj = start + lax.broadcasted_iota(jnp.int32, (DIL_WIN, DIL_QB), 0)
        ii = b * DIL_QB + lax.broadcasted_iota(jnp.int32, (DIL_WIN, DIL_QB), 1)
        dist = jnp.abs(ii - jj)
        valid = (dist <= DIL_RADIUS) & (jnp.right_shift(ii, shift) == jnp.right_shift(jj, shift))
        dist = dist.astype(F32)
        dil = jnp.left_shift(1, 2 * g).astype(F32)
        q = q_ref[...]
        do = do_ref[...]
        kw = k_ref[pl.ds(start, DIL_WIN), :]
        vw = v_ref[pl.ds(start, DIL_WIN), :]
        dqs, dks, dvs = [], [], []
        for hh in range(DIL_HPS):
            sl = slice(hh * HD, (hh + 1) * HD)
            qh = q[:, sl] * QSCALE
            st = lax.dot_general(kw[:, sl], qh, _NT, preferred_element_type=F32)
            st = jnp.where(valid, st - (sl_ref[hp * DIL_HPS + hh] * dil) * dist, NEG)
            pt = jnp.exp(st - lse_ref[hh, 0:1, :])
            dpt = lax.dot_general(vw[:, sl], do[:, sl], _NT, preferred_element_type=F32)
            dst = (pt * (dpt - dd_ref[hh, 0:1, :])).astype(BF16)
            dqs.append(lax.dot_general(kw[:, sl], dst, _TN, preferred_element_type=F32).T * QSCALE)
            dks.append(jnp.dot(dst, qh, preferred_element_type=F32))
            dvs.append(jnp.dot(pt.astype(BF16), do[:, sl], preferred_element_type=F32))
        dqkv_ref[0, pl.ds(pl.multiple_of(b * DIL_QB, DIL_QB), DIL_QB), :] = jnp.concatenate(dqs, axis=1).astype(dqkv_ref.dtype)
        dk_acc[pl.ds(start, DIL_WIN), :] += jnp.concatenate(dks, axis=1)
        dv_acc[pl.ds(start, DIL_WIN), :] += jnp.concatenate(dvs, axis=1)

        @pl.when(b == DIL_BLOCKS - 1)
        def _():
            dqkv_ref[1] = dk_acc[...].astype(dqkv_ref.dtype)
            dqkv_ref[2] = dv_acc[...].astype(dqkv_ref.dtype)

    qspec = pl.BlockSpec((None, DIL_QB, DIL_LW), lambda g, hp, b: (g, b, hp))
    rspec = pl.BlockSpec((None, DIL_HPS, 8, DIL_QB), lambda g, hp, b: (g, hp, 0, b))
    (dqkv,), sent = _carrier_call(
        "dil_bwd", body, (3, DIL_NLW, DIL_BLOCKS), _dil_in_specs() + [qspec, rspec, rspec],
        [pl.BlockSpec((None, 3, SEQ, DIL_LW), lambda g, hp, b: (g, 0, 0, hp))], [jax.ShapeDtypeStruct((3, 3, SEQ, DM), BF16)],
        [pltpu.VMEM((SEQ, DIL_LW), F32), pltpu.VMEM((SEQ, DIL_LW), F32)], (slopes, qkv, qkv, qkv, do, dd, lse), carry)
    return dqkv, sent


def _ffn_fwd(name, x, g_pre, g_post, wgt4, wut4, wd4, carry):
    tm = 512

    def body(x_ref, gpre_ref, gpost_ref, wg_ref, wu_ref, wd_ref, xn_ref, h_ref, gate_ref, up_ref, u_ref, acc):
        s = pl.program_id(1)

        @pl.when(s == 0)
        def _():
            x = x_ref[...]
            r = lax.rsqrt(jnp.mean(x * x, axis=-1, keepdims=True) + RMS_EPS)
            h_ref[...] = (x * r * gpre_ref[...]).astype(h_ref.dtype)

        h = h_ref[...]
        gate = lax.dot_general(h, wg_ref[...], _NT, preferred_element_type=F32).astype(BF16)
        up = lax.dot_general(h, wu_ref[...], _NT, preferred_element_type=F32).astype(BF16)
        gate_ref[...] = gate
        up_ref[...] = up
        gf = gate.astype(F32)
        act = (gf * jax.nn.sigmoid(gf) * up.astype(F32)).astype(BF16)
        part = jnp.dot(act, wd_ref[...], preferred_element_type=F32)

        @pl.when(s == 0)
        def _():
            acc[...] = part

        @pl.when(s > 0)
        def _():
            acc[...] += part

        @pl.when(s == NCHIP - 1)
        def _():
            u = acc[...]
            u_ref[...] = u
            r = lax.rsqrt(jnp.mean(u * u, axis=-1, keepdims=True) + RMS_EPS)
            xn_ref[...] = x_ref[...] + u * r * gpost_ref[...]

    rows = pl.BlockSpec((tm, DM), lambda i, s: (i, 0))
    vec = pl.BlockSpec((1, DM), lambda i, s: (0, 0))
    wspec = _ffn_wspec(lambda i, s: (s, 0, 0))
    mid = pl.BlockSpec((None, tm, FSH), lambda i, s: (s, i, 0))
    outs, sent = _carrier_call(
        name, body, (SEQ // tm, NCHIP), [rows, vec, vec, wspec, wspec, wspec], [rows, rows, mid, mid, rows],
        [jax.ShapeDtypeStruct((SEQ, DM), F32), jax.ShapeDtypeStruct((SEQ, DM), BF16), jax.ShapeDtypeStruct((NCHIP, SEQ, FSH), BF16),
         jax.ShapeDtypeStruct((NCHIP, SEQ, FSH), BF16), jax.ShapeDtypeStruct((SEQ, DM), F32)],
        [pltpu.VMEM((tm, DM), F32)], (x, g_pre, g_post, wgt4, wut4, wd4), carry)
    return outs, sent


def _ffn_block(layer, x, g_pre, g_post, ex):
    tag = f"l{layer}_ffn_fwd"
    (x_new, h, gate, up, u), sent = _ffn_fwd(tag, x, g_pre, g_post, ex.weight(("ffn_w_gate", layer)), ex.weight(("ffn_w_up", layer)),
                                             ex.weight(("ffn_w_down", layer)), ex.carry(tag))
    ex.carried(tag, sent)
    return x_new, (x, h, gate, up, u)


def _ffn_bwd(name, dx, x, gate, up, u, g_pre, g_post, wgt4, wut4, wd4, carry):
    tm = 512

    def body(dx_ref, x_ref, gate_ref, up_ref, u_ref, gpre_ref, gpost_ref, wg_ref, wu_ref, wd_ref,
             dxin_ref, du_ref, dgate_ref, dup_ref, act_ref, dgpre_ref, dgpost_ref, dh_acc):
        i, s = pl.program_id(0), pl.program_id(1)

        @pl.when((i == 0) & (s == 0))
        def _():
            dgpre_ref[...] = jnp.zeros_like(dgpre_ref)
            dgpost_ref[...] = jnp.zeros_like(dgpost_ref)

        @pl.when(s == 0)
        def _():
            dy = dx_ref[...]
            uu = u_ref[...]
            r = lax.rsqrt(jnp.mean(uu * uu, axis=-1, keepdims=True) + RMS_EPS)
            yh = uu * r
            t = dy * gpost_ref[...]
            du_ref[...] = (r * (t - yh * jnp.mean(t * yh, axis=-1, keepdims=True))).astype(du_ref.dtype)
            dgpost_ref[...] += jnp.sum(dy * yh, axis=0, keepdims=True)

        dact = lax.dot_general(du_ref[...], wd_ref[...], _NT, preferred_element_type=F32)
        g = gate_ref[...].astype(F32)
        upv = up_ref[...].astype(F32)
        sg = jax.nn.sigmoid(g)
        dgate = (dact * upv * sg * (1.0 + g * (1.0 - sg))).astype(BF16)
        dup = (dact * g * sg).astype(BF16)
        dgate_ref[...] = dgate
        dup_ref[...] = dup
        act_ref[...] = (g * sg * upv).astype(act_ref.dtype)
        part = jnp.dot(dgate, wg_ref[...], preferred_element_type=F32) + jnp.dot(dup, wu_ref[...], preferred_element_type=F32)

        @pl.when(s == 0)
        def _():
            dh_acc[...] = part

        @pl.when(s > 0)
        def _():
            dh_acc[...] += part

        @pl.when(s == NCHIP - 1)
        def _():
            dh = dh_acc[...]
            xx = x_ref[...]
            r = lax.rsqrt(jnp.mean(xx * xx, axis=-1, keepdims=True) + RMS_EPS)
            yh = xx * r
            t = dh * gpre_ref[...]
            dxin_ref[...] = dx_ref[...] + r * (t - yh * jnp.mean(t * yh, axis=-1, keepdims=True))
            dgpre_ref[...] += jnp.sum(dh * yh, axis=0, keepdims=True)

    rows = pl.BlockSpec((tm, DM), lambda i, s: (i, 0))
    vec = pl.BlockSpec((1, DM), lambda i, s: (0, 0))
    wspec = _ffn_wspec(lambda i, s: (s, 0, 0))
    mid = pl.BlockSpec((None, tm, FSH), lambda i, s: (s, i, 0))
    mid_shape = jax.ShapeDtypeStruct((NCHIP, SEQ, FSH), BF16)
    return _carrier_call(
        name, body, (SEQ // tm, NCHIP), [rows, rows, mid, mid, rows, vec, vec, wspec, wspec, wspec], [rows, rows, mid, mid, mid, vec, vec],
        [jax.ShapeDtypeStruct((SEQ, DM), F32), jax.ShapeDtypeStruct((SEQ, DM), BF16), mid_shape, mid_shape, mid_shape,
         jax.ShapeDtypeStruct((1, DM), F32), jax.ShapeDtypeStruct((1, DM), F32)],
        [pltpu.VMEM((tm, DM), F32)], (dx, x, gate, up, u, g_pre, g_post, wgt4, wut4, wd4), carry)


def _ffn_block_bwd(layer, dx, saved, g_pre, g_post, ex):
    tag = f"l{layer}"
    x, h, gate, up, u = saved
    (dx_in, du, dgate, dup, act, dg_pre, dg_post), sent = _ffn_bwd(
        f"{tag}_ffn_bwd", dx, x, gate, up, u, g_pre, g_post, ex.weight(("ffn_w_gate", layer)), ex.weight(("ffn_w_up", layer)),
        ex.weight(("ffn_w_down", layer)), ex.carry(f"{tag}_ffn_bwd"))
    ex.carried(f"{tag}_ffn_bwd", sent)
    d_wd = _ffn_bwd_dw(f"{tag}_dwd", act, du)
    d_wg = _ffn_bwd_dw(f"{tag}_dwg", dgate, h)
    d_wu = _ffn_bwd_dw(f"{tag}_dwu", dup, h)
    ex.grads(f"{tag}_ffn", {("ffn_w_gate", layer): d_wg, ("ffn_w_up", layer): d_wu, ("ffn_w_down", layer): d_wd})
    return dx_in, dg_pre, dg_post


def _alibi_slopes():
    return 2.0 ** (-8.0 * jnp.arange(1, NH + 1, dtype=F32) / NH)


def _local_step(x, target, norms, rpb, ex):
    g_mix_pre, g_mix_post, g_ffn_pre, g_ffn_post = norms
    row = lambda a, i: a[i:i + 1]

    bias, sent = _na_bias_tiles(rpb, ex.carry("na_bias_tiles"))
    ex.carried("na_bias_tiles", sent)
    h0, h0t = _rms_fwd_both("l0_mix_pre", x, row(g_mix_pre, 0))
    qkv0, sent = _qkv_fwd("l0_qkv", h0[None], ex.weight(("na_w_qkv", 0)), ex.carry("l0_qkv"))
    ex.carried("l0_qkv", sent)
    o0, sent = _na_fwd(qkv0[0], bias, ex.carry("na_fwd"))
    ex.carried("na_fwd", sent)
    na_wo = ex.weight(("na_w_o", 0)).reshape(DM, DM)
    x1, u0 = _proj_fwd("l0_proj", o0, na_wo, x, row(g_mix_post, 0))
    x2, ffn0 = _ffn_block(0, x1, row(g_ffn_pre, 0), row(g_ffn_post, 0), ex)

    slopes = _alibi_slopes()
    h2g, h2gt = _to_groups("l1_h_groups", _rms_fwd("l1_mix_pre", x2, row(g_mix_pre, 1), F32))
    dil_wqkv = ex.weight(("dil_w_qkv", 0))
    qkv1, sent = _qkv_fwd("l1_qkv", h2g, dil_wqkv, ex.carry("l1_qkv"))
    ex.carried("l1_qkv", sent)
    og, lg, sent = _dil_fwd(qkv1, slopes, ex.carry("dil_fwd"))
    ex.carried("dil_fwd", sent)
    o1, lse = _dil_merge(og, lg)
    dil_wo = ex.weight(("dil_w_o", 0)).reshape(DM, DM)
    x3, u1 = _proj_fwd("l1_proj", o1, dil_wo, x2, row(g_mix_post, 1))
    x4, ffn1 = _ffn_block(1, x3, row(g_ffn_pre, 1), row(g_ffn_post, 1), ex)

    dx4, loss_row = _loss_grad("loss", x4, target)

    dx3, dg_fpre1, dg_fpost1 = _ffn_block_bwd(1, dx4, ffn1, row(g_ffn_pre, 1), row(g_ffn_post, 1), ex)
    (do1, du1, dg_mpost1), sent = _proj_bwd("l1_proj_bwd", dx3, u1, row(g_mix_post, 1), dil_wo, F32, ex.carry("l1_proj_bwd"))
    ex.carried("l1_proj_bwd", sent)
    d_dil_wo = _proj_bwd_dw("l1_dwo", o1, du1)
    dog, ddg, lseg = _dil_bwd_prep(do1, o1, lse)
    dqkv1, sent = _dil_bwd(qkv1, dog, ddg, lseg, slopes, ex.carry("dil_bwd"))
    ex.carried("dil_bwd", sent)
    d_dil_wqkv = _qkv_bwd_dw("l1_dwqkv", h2gt, dqkv1, dil_wqkv.shape[2])
    ex.grads("l1_mix", {("dil_w_qkv", 0): d_dil_wqkv, ("dil_w_o", 0): d_dil_wo.reshape(NCHIP, DM // NCHIP, DM)})
    dh2g, sent = _qkv_bwd_dh("l1_dh", dqkv1, dil_wqkv, ex.carry("l1_dh"))
    ex.carried("l1_dh", sent)
    dh2 = _from_groups_sum("l1_dh_tokens", dh2g)
    dx2, dg_mpre1 = _norm_bwd("l1_mix_pre_bwd", [dh2], x2, row(g_mix_pre, 1), res=dx3)

    dx1, dg_fpre0, dg_fpost0 = _ffn_block_bwd(0, dx2, ffn0, row(g_ffn_pre, 0), row(g_ffn_post, 0), ex)
    (do0, du0, dg_mpost0), sent = _proj_bwd("l0_proj_bwd", dx1, u0, row(g_mix_post, 0), na_wo, BF16, ex.carry("l0_proj_bwd"))
    ex.carried("l0_proj_bwd", sent)
    d_na_wo = _proj_bwd_dw("l0_dwo", o0, du0)
    dqkv0, z, sent = _na_bwd(qkv0[0], bias, do0, ex.carry("na_bwd"))
    ex.carried("na_bwd", sent)
    d_rpb = _rpb_grad(z)
    na_wqkv = ex.weight(("na_w_qkv", 0))
    d_na_wqkv = _qkv_bwd_dw("l0_dwqkv", h0t[None], dqkv0[None], na_wqkv.shape[2])
    ex.grads("l0_mix", {("na_w_qkv", 0): d_na_wqkv, ("na_w_o", 0): d_na_wo.reshape(NCHIP, DM // NCHIP, DM)})
    dh0, sent = _qkv_bwd_dh("l0_dh", dqkv0[None], na_wqkv, ex.carry("l0_dh"))
    ex.carried("l0_dh", sent)
    dx0, dg_mpre0 = _norm_bwd("l0_mix_pre_bwd", [dh0[0]], x, row(g_mix_pre, 0), res=dx1)

    dnorms = (jnp.concatenate([dg_mpre0, dg_mpre1]), jnp.concatenate([dg_mpost0, dg_mpost1]),
              jnp.concatenate([dg_fpre0, dg_fpre1]), jnp.concatenate([dg_fpost0, dg_fpost1]))
    return loss_row, dx0, dnorms, d_rpb


def _place():
    x, y, c = lax.axis_index("x"), lax.axis_index("y"), lax.axis_index("c")
    chips = ((1 - x, y), (x, 1 - y), (1 - x, 1 - y))
    return x, y, c, chips


def _chip_id(chip):
    return 2 * chip[0] + chip[1]


def _comm_call(name, body, ins, out_shapes, n_sems, aliases=None):
    return pl.pallas_call(
        body, in_specs=[HBM_SPEC] * len(ins), out_specs=[HBM_SPEC] * len(out_shapes), out_shape=out_shapes,
        scratch_shapes=[pltpu.SemaphoreType.DMA((k,)) for k in n_sems], input_output_aliases=aliases or {},
        compiler_params=pltpu.CompilerParams(has_side_effects=True), name=name)(*ins)


def _gather_copies(shards):
    n = len(shards)

    def copies(src, out, sems):
        send_sems, recv_sems = sems
        x, y, c, chips = _place()

        def copy(t, k, chip, half, to, from_src=False):
            blk = out[t].at[_chip_id(chip), half]
            return pltpu.make_async_remote_copy(
                src_ref=src[t].at[half] if from_src else blk, dst_ref=blk,
                send_sem=send_sems.at[6 * t + k], recv_sem=recv_sems.at[6 * t + k], device_id=to, device_id_type=MESH)

        return copy, x, y, c, chips

    def issue(src, out, sems):
        copy, x, y, c, chips = copies(src, out, sems)
        for t in range(n):
            for j, chip in enumerate(chips):
                copy(t, j, (x, y), c, (*chip, c), from_src=True).start()

    def drain(src, out, sems):
        copy, x, y, c, chips = copies(src, out, sems)
        passed = []
        for t in range(n):
            for j, chip in enumerate(chips):
                copy(t, j, chip, c, (x, y, c)).wait_recv()
                fwd = copy(t, 3 + j, chip, c, (x, y, 1 - c))
                fwd.start()
                passed.append(fwd)
        for t in range(n):
            for j, chip in enumerate(chips):
                copy(t, 3 + j, chip, 1 - c, (x, y, c)).wait_recv()
        for t in range(n):
            for j, chip in enumerate(chips):
                copy(t, j, (x, y), c, (*chip, c), from_src=True).wait_send()
        for cp in passed:
            cp.wait_send()

    return _Carried(shards, [jax.ShapeDtypeStruct((NCHIP,) + s.shape, s.dtype) for s in shards], (6 * n, 6 * n), issue, drain)


def _pair_exchange_copies(grads):
    n = len(grads)

    def copies(g, theirs, sems):
        send_sems, recv_sems = sems
        x, y, c, _ = _place()
        return [pltpu.make_async_remote_copy(src_ref=g[t].at[:, 1 - c], dst_ref=theirs[t], send_sem=send_sems.at[t],
                                             recv_sem=recv_sems.at[t], device_id=(x, y, 1 - c), device_id_type=MESH) for t in range(n)]

    def issue(g, theirs, sems):
        for cp in copies(g, theirs, sems):
            cp.start()

    def drain(g, theirs, sems):
        for cp in copies(g, theirs, sems):
            cp.wait()

    return _Carried(grads, [jax.ShapeDtypeStruct((NCHIP,) + g.shape[2:], g.dtype) for g in grads], (n, n), issue, drain)


def _chip_exchange_copies(parts):
    n = len(parts)

    def copies(p, slots, sems):
        send_sems, recv_sems = sems
        x, y, c, chips = _place()
        return [pltpu.make_async_remote_copy(src_ref=p[t].at[_chip_id(chips[j])], dst_ref=slots[t].at[j], send_sem=send_sems.at[3 * t + j],
                                             recv_sem=recv_sems.at[3 * t + j], device_id=(*chips[j], c), device_id_type=MESH)
                for t in range(n) for j in range(3)]

    def issue(p, slots, sems):
        for cp in copies(p, slots, sems):
            cp.start()

    def drain(p, slots, sems):
        for cp in copies(p, slots, sems):
            cp.wait()

    return _Carried(parts, [jax.ShapeDtypeStruct((3,) + p.shape[1:], p.dtype) for p in parts], (3 * n, 3 * n), issue, drain)


def _pair_share(full):
    n = len(full)

    def body(*refs):
        buf = refs[n:2 * n]
        send_sems, recv_sems = refs[2 * n:]
        x, y, c, _ = _place()
        sends = [pltpu.make_async_remote_copy(src_ref=buf[t].at[c], dst_ref=buf[t].at[c], send_sem=send_sems.at[t], recv_sem=recv_sems.at[t],
                                              device_id=(x, y, 1 - c), device_id_type=MESH) for t in range(n)]
        for cp in sends:
            cp.start()
        for t in range(n):
            pltpu.make_async_remote_copy(src_ref=buf[t].at[c], dst_ref=buf[t].at[1 - c], send_sem=send_sems.at[t], recv_sem=recv_sems.at[t],
                                         device_id=(x, y, 1 - c), device_id_type=MESH).wait_recv()
        for cp in sends:
            cp.wait_send()

    return _comm_call("grad_pair_share", body, full, [jax.ShapeDtypeStruct(f.shape, f.dtype) for f in full], (n, n),
                      aliases={t: t for t in range(n)})


SMALL_ROWS = 128


def _allreduce_small(v):
    def body(v_ref, o_ref, buf, send_sems, recv_sems):
        x, y, c, _ = _place()
        me = 4 * x + 2 * y + c
        flip = lambda a, f: 1 - a if f else a
        buf[me] = v_ref[...]
        peers = [(flip(x, d >> 2 & 1), flip(y, d >> 1 & 1), flip(c, d & 1)) for d in range(1, 8)]
        sends = [pltpu.make_async_remote_copy(src_ref=v_ref, dst_ref=buf.at[me], send_sem=send_sems.at[i], recv_sem=recv_sems.at[i],
                                              device_id=peer, device_id_type=MESH) for i, peer in enumerate(peers)]
        for cp in sends:
            cp.start()
        for i, (px, py, pc) in enumerate(peers):
            pltpu.make_async_remote_copy(src_ref=v_ref, dst_ref=buf.at[4 * px + 2 * py + pc], send_sem=send_sems.at[i], recv_sem=recv_sems.at[i],
                                         device_id=(px, py, pc), device_id_type=MESH).wait_recv()
        for cp in sends:
            cp.wait_send()
        acc = buf[0]
        for k in range(1, 8):
            acc = acc + buf[k]
        o_ref[...] = acc

    vm = pl.BlockSpec(memory_space=pltpu.VMEM)
    return pl.pallas_call(
        body, in_specs=[vm], out_specs=vm, out_shape=jax.ShapeDtypeStruct((SMALL_ROWS, 128), F32),
        scratch_shapes=[pltpu.VMEM((8, SMALL_ROWS, 128), F32), pltpu.SemaphoreType.DMA((7,)), pltpu.SemaphoreType.DMA((7,))],
        compiler_params=pltpu.CompilerParams(has_side_effects=True), name="allreduce_small")(v)


def _row_block(rows, cols, budget=3 << 19):
    best = 8
    for bm in range(8, rows + 1, 8):
        if rows % bm == 0 and bm * cols * 4 <= budget:
            best = bm
    return best


def _pair_sum(name, place, g, theirs):
    _, m, c = theirs.shape
    bm = _row_block(m, c)

    def body(place_ref, a_ref, b_ref, o_ref):
        o_ref[...] = (a_ref[...].astype(F32) + b_ref[...].astype(F32)).astype(o_ref.dtype)

    spec = pl.BlockSpec((None, bm, c), lambda k, i, pr: (k, i, 0))
    return pl.pallas_call(
        body, out_shape=jax.ShapeDtypeStruct(theirs.shape, BF16),
        grid_spec=pltpu.PrefetchScalarGridSpec(
            num_scalar_prefetch=1, grid=(NCHIP, m // bm),
            in_specs=[pl.BlockSpec((None, None, bm, c), lambda k, i, pr: (k, pr[0], i, 0)), spec], out_specs=spec),
        compiler_params=_params(("parallel", "parallel")), name=name)(place, g, theirs)


def _chip_sum(name, place, parts, slots):
    _, m, c = parts.shape
    bm = _row_block(m, c)

    def body(place_ref, p_ref, s_ref, o_ref):
        s = s_ref[...].astype(F32)
        o_ref[...] = ((p_ref[...].astype(F32) + s[0]) + s[1]) + s[2]

    return pl.pallas_call(
        body, out_shape=jax.ShapeDtypeStruct((2, m, c), F32),
        grid_spec=pltpu.PrefetchScalarGridSpec(
            num_scalar_prefetch=1, grid=(m // bm,),
            in_specs=[pl.BlockSpec((None, bm, c), lambda i, pr: (pr[1], i, 0)), pl.BlockSpec((3, bm, c), lambda i, pr: (0, i, 0))],
            out_specs=pl.BlockSpec((None, bm, c), lambda i, pr: (pr[0], i, 0))),
        compiler_params=_params(("parallel",)), name=name)(place, parts, slots)


def _adamw(name, w, g, m, v, layer=0, into=None):
    lead, rows, cols = w.shape
    bm = _row_block(rows, cols, budget=768 * 1024)
    c1 = 1.0 - ADAM_B1 ** ADAM_STEP
    c2 = 1.0 - ADAM_B2 ** ADAM_STEP

    def body(w_ref, g_ref, m_ref, v_ref, *rest):
        go_ref, d_ref, mo_ref, vo_ref = rest[-4:]
        g = g_ref[...]
        mn = ADAM_B1 * m_ref[...] + (1.0 - ADAM_B1) * g
        vn = ADAM_B2 * v_ref[...] + (1.0 - ADAM_B2) * (g * g)
        go_ref[...] = g
        mo_ref[...] = mn
        vo_ref[...] = vn
        d_ref[...] = -ADAM_LR * ((mn / c1) / (jnp.sqrt(vn / c2) + ADAM_EPS) + ADAM_WD * w_ref[...])

    spec = pl.BlockSpec((None, bm, cols), lambda i: (layer, i, 0))
    sh = jax.ShapeDtypeStruct((lead, rows, cols), F32)
    prev = [] if into is None else list(into)
    return pl.pallas_call(
        body, grid=(rows // bm,), in_specs=[spec, pl.BlockSpec((bm, cols), lambda i: (i, 0)), spec, spec] + [pl.BlockSpec(memory_space=pl.ANY)] * len(prev),
        out_specs=[spec] * 4, out_shape=[sh] * 4, input_output_aliases={4 + k: k for k in range(len(prev))},
        compiler_params=_params(("parallel",)), name=name)(w, g, m, v, *prev)


def _pack_small(norms, rpb):
    flat = jnp.concatenate([a.reshape(-1) for a in norms] + [rpb.reshape(-1)])
    return jnp.pad(flat, (0, SMALL_ROWS * 128 - flat.shape[0])).reshape(SMALL_ROWS, 128)


def _unpack_small(p):
    flat = p.reshape(-1)
    norms = [flat[i * 2 * DM:(i + 1) * 2 * DM].reshape(2, DM) for i in range(4)]
    rpb = flat[8 * DM:8 * DM + NH * 15 * 31].reshape(1, NH, 15, 31)
    return norms, rpb


FFN_NAMES = ("ffn_w_gate", "ffn_w_up", "ffn_w_down")
L0_FFN = tuple((n, 0) for n in FFN_NAMES)
L1_FFN = tuple((n, 1) for n in FFN_NAMES)
NA_KEYS = (("na_w_qkv", 0), ("na_w_o", 0))
DIL_KEYS = (("dil_w_qkv", 0), ("dil_w_o", 0))


class _Exchange:
    GATHERS = {"na_bias_tiles": NA_KEYS, "l0_qkv": L0_FFN[:1], "na_fwd": L0_FFN[1:], "l0_ffn_fwd": DIL_KEYS[:1], "dil_fwd": L1_FFN + DIL_KEYS[1:]}
    PAIRS = {"l1_proj_bwd": L1_FFN, "l1_dh": DIL_KEYS, "l0_proj_bwd": L0_FFN}
    EXCHANGES = {"dil_bwd": L1_FFN, "l0_ffn_bwd": DIL_KEYS, "na_bwd": L0_FFN, "l0_dh": NA_KEYS}

    def __init__(self, shards):
        self.chip = 2 * lax.axis_index("x") + lax.axis_index("y")
        self.place = jnp.stack([lax.axis_index("c"), self.chip]).astype(jnp.int32)
        self.own = {k: s.reshape(2, s.shape[0] // 2, s.shape[1]).astype(BF16) for k, s in shards.items()}
        self.gathered, self.mine, self.parts, self.full = {}, {}, {}, {}

    def _take(self, keys, landed):
        for k, gw in zip(keys, landed):
            self.gathered[k] = lax.dynamic_update_slice(gw, self.own[k][None], (self.chip, 0, 0, 0))

    def _sum(self, keys, slots):
        for k, s in zip(keys, slots):
            self.full[k] = _chip_sum(f"chip_sum_{k[0]}_{k[1]}", self.place, self.parts[k], s)

    def weight(self, key):
        g = self.gathered[key]
        return g.reshape(NCHIP, 2 * g.shape[2], g.shape[3])

    def _pair_sums(self, keys, theirs):
        for k, t in zip(keys, theirs):
            self.parts[k] = _pair_sum(f"pair_sum_{k[0]}_{k[1]}", self.place, self.mine[k], t)

    def carry(self, tag):
        if tag in self.GATHERS:
            return _gather_copies([self.own[k] for k in self.GATHERS[tag]])
        if tag in self.PAIRS:
            return _pair_exchange_copies([self.mine[k] for k in self.PAIRS[tag]])
        if tag in self.EXCHANGES:
            return _chip_exchange_copies([self.parts[k] for k in self.EXCHANGES[tag]])
        return None

    def carried(self, tag, landed):
        if tag in self.GATHERS:
            self._take(self.GATHERS[tag], landed)
        elif tag in self.PAIRS:
            self._pair_sums(self.PAIRS[tag], landed)
        elif tag in self.EXCHANGES:
            self._sum(self.EXCHANGES[tag], landed)

    def grads(self, tag, dw):
        for k, g in dw.items():
            self.mine[k] = g.reshape(NCHIP, 2, -1, g.shape[-1])
        if tag == "l0_mix":
            keys = tuple(dw)
            self._pair_sums(keys, _run_carried("grad_pair_exchange_last", _pair_exchange_copies([self.mine[k] for k in keys])))

    def finish(self):
        keys = tuple(self.full)
        shared = _pair_share([self.full[k] for k in keys])
        return {k: s.reshape(2 * s.shape[1], s.shape[2]) for k, s in zip(keys, shared)}


def kernel(x, norm_mix_pre, norm_mix_post, norm_ffn_pre, norm_ffn_post, na_w_qkv, na_w_o, na_rpb, dil_w_qkv, dil_w_o, ffn_w_gate, ffn_w_up, ffn_w_down, loss_target, m_norm_mix_pre, m_norm_mix_post, m_norm_ffn_pre, m_norm_ffn_post, m_na_w_qkv, m_na_w_o, m_na_rpb, m_dil_w_qkv, m_dil_w_o, m_ffn_w_gate, m_ffn_w_up, m_ffn_w_down, v_norm_mix_pre, v_norm_mix_post, v_norm_ffn_pre, v_norm_ffn_post, v_na_w_qkv, v_na_w_o, v_na_rpb, v_dil_w_qkv, v_dil_w_o, v_ffn_w_gate, v_ffn_w_up, v_ffn_w_down):
    tr = lambda a: jnp.swapaxes(a, 1, 2)
    weights = {"na_w_qkv": na_w_qkv, "na_w_o": na_w_o, "dil_w_qkv": dil_w_qkv, "dil_w_o": dil_w_o,
               "ffn_w_gate": tr(ffn_w_gate), "ffn_w_up": tr(ffn_w_up), "ffn_w_down": ffn_w_down}
    m_in = {"na_w_qkv": m_na_w_qkv, "na_w_o": m_na_w_o, "dil_w_qkv": m_dil_w_qkv, "dil_w_o": m_dil_w_o,
            "ffn_w_gate": tr(m_ffn_w_gate), "ffn_w_up": tr(m_ffn_w_up), "ffn_w_down": m_ffn_w_down}
    v_in = {"na_w_qkv": v_na_w_qkv, "na_w_o": v_na_w_o, "dil_w_qkv": v_dil_w_qkv, "dil_w_o": v_dil_w_o,
            "ffn_w_gate": tr(v_ffn_w_gate), "ffn_w_up": tr(v_ffn_w_up), "ffn_w_down": v_ffn_w_down}

    ex = _Exchange({(n, l): weights[n][l] for n in weights for l in range(weights[n].shape[0])})
    norms = (norm_mix_pre, norm_mix_post, norm_ffn_pre, norm_ffn_post)
    loss_row, dx, dnorms, d_rpb = _local_step(x[0], loss_target[0], norms, na_rpb[0], ex)
    loss = lax.psum(loss_row[0, 0], ("x", "y", "c"))
    full = ex.finish()
    small = _allreduce_small(_pack_small(dnorms, d_rpb))

    out_g, out_d, out_m, out_v = {}, {}, {}, {}
    for n in weights:
        res = None
        for l in range(weights[n].shape[0]):
            res = _adamw(f"adamw_{n}_{l}", weights[n], full[(n, l)], m_in[n], v_in[n], l, res)
        if n in ("ffn_w_gate", "ffn_w_up"):
            res = [tr(r) for r in res]
        out_g[n], out_d[n], out_m[n], out_v[n] = res
    sm_names = ("norm_mix_pre", "norm_mix_post", "norm_ffn_pre", "norm_ffn_post", "na_rpb")
    sm = _adamw("adamw_small", _pack_small(norms, na_rpb)[None], small,
                _pack_small((m_norm_mix_pre, m_norm_mix_post, m_norm_ffn_pre, m_norm_ffn_post), m_na_rpb)[None],
                _pack_small((v_norm_mix_pre, v_norm_mix_post, v_norm_ffn_pre, v_norm_ffn_post), v_na_rpb)[None])
    for res, dst in zip(sm, (out_g, out_d, out_m, out_v)):
        ns, rp = _unpack_small(res)
        for n, a in zip(sm_names, ns + [rp]):
            dst[n] = a

    order = ("norm_mix_pre", "norm_mix_post", "norm_ffn_pre", "norm_ffn_post", "na_w_qkv", "na_w_o", "na_rpb", "dil_w_qkv", "dil_w_o",
             "ffn_w_gate", "ffn_w_up", "ffn_w_down")
    return (loss, dx[None], *[out_g[n] for n in order], *[out_d[n] for n in order], *[out_m[n] for n in order], *[out_v[n] for n in order])
```

```python
import functools

import numpy as np
import jax
import jax.numpy as jnp
from jax import lax
from jax.experimental import pallas as pl
from jax.experimental.pallas import tpu as pltpu

F32 = jnp.float32
BF16 = jnp.bfloat16

SEQ = 2048
DM = 1024
NH = 16
HD = 64
DFF = 2816
NCHIP = 4
FSH = DFF // NCHIP
GRID_W = 64
NA_QROWS = 4
NA_QB = NA_QROWS * GRID_W
NA_WROWS = 12
NA_WIN = NA_WROWS * GRID_W
DIL = (1, 4, 16)
DIL_QB = 256
DIL_WIN = DIL_QB + 128
DIL_RADIUS = 64
RMS_EPS = 1e-6
NEG = -1e30
QSCALE = HD ** -0.5
CH = 256
MESH = pl.DeviceIdType.MESH

ADAM_LR, ADAM_B1, ADAM_B2, ADAM_EPS, ADAM_WD, ADAM_STEP = 0.001, 0.9, 0.999, 1e-08, 0.01, 10

VMEM_LIMIT = 56 * 1024 * 1024

_NN = (((1,), (0,)), ((), ()))
_NT = (((1,), (1,)), ((), ()))
_TN = (((0,), (0,)), ((), ()))


def _params(sem):
    return pltpu.CompilerParams(dimension_semantics=sem, vmem_limit_bytes=VMEM_LIMIT)


def _matmul(name, pairs, grid, out_shape, out_spec, acc_shape, carrying=False, carry=None):
    nk = grid[-1]
    npair = len(pairs)
    n_in = 2 * npair

    def body(*refs):
        ins, o_ref = refs[:2 * npair], refs[n_in]
        part = None
        for p in range(npair):
            d = lax.dot_general(ins[2 * p][...].astype(BF16), ins[2 * p + 1][...].astype(BF16), pairs[p][4],
                                preferred_element_type=F32)
            part = d if part is None else part + d
        if nk == 1:
            o_ref[...] = part.astype(o_ref.dtype)
        else:
            acc_ref = refs[n_in + 1]
            kk = pl.program_id(len(grid) - 1)

            @pl.when(kk == 0)
            def _():
                acc_ref[...] = part

            @pl.when(kk > 0)
            def _():
                acc_ref[...] += part

            @pl.when(kk == nk - 1)
            def _():
                o_ref[...] = acc_ref[...].astype(o_ref.dtype)

    ops, specs = [], []
    for a, a_spec, b, b_spec, _ in pairs:
        ops += [a, b]
        specs += [a_spec, b_spec]
    (out,), sent = _carrier_call(name, body, grid, specs, [out_spec], [out_shape], [] if nk == 1 else [pltpu.VMEM(acc_shape, F32)], ops, carry)
    return (out, sent) if carrying else out


def _qkv_fwd(name, h_all, w4, carry):
    g_n = h_all.shape[0]
    per = w4.shape[2] // CH
    return _matmul(
        name, [(h_all, pl.BlockSpec((None, SEQ, DM), lambda g, q, k: (g, 0, 0)),
                w4, pl.BlockSpec((None, DM, CH), lambda g, q, k: ((g * 12 + q) // per, 0, (g * 12 + q) % per)), _NN)],
        (g_n, 12, 1), jax.ShapeDtypeStruct((g_n, SEQ, 3 * DM), BF16),
        pl.BlockSpec((None, SEQ, CH), lambda g, q, k: (g, 0, q)), None, carrying=True, carry=carry)


def _qkv_bwd_dh(name, dqkv, w4, carry):
    g_n = dqkv.shape[0]
    per = w4.shape[2] // CH
    tm = SEQ

    def pair(cb):
        chunk = lambda g, t: g * 12 + t * 4 + cb
        return (dqkv, pl.BlockSpec((None, None, tm, CH), lambda g, i, t: (g, t, i, cb)),
                w4, pl.BlockSpec((None, DM, CH), lambda g, i, t: (chunk(g, t) // per, 0, chunk(g, t) % per)), _NT)

    return _matmul(name, [pair(cb) for cb in range(4)], (g_n, SEQ // tm, 3), jax.ShapeDtypeStruct((g_n, SEQ, DM), F32),
                   pl.BlockSpec((None, tm, DM), lambda g, i, t: (g, i, 0)), (tm, DM), carrying=True, carry=carry)


def _qkv_bwd_dw(name, ht_all, dqkv, shard_cols):
    g_n = dqkv.shape[0]
    per = shard_cols // CH
    return _matmul(
        name, [(ht_all, pl.BlockSpec((None, DM, SEQ), lambda qq, k: (qq // 12, 0, 0)),
                dqkv, pl.BlockSpec((None, None, SEQ, CH), lambda qq, k: (qq // 12, (qq % 12) // 4, 0, qq % 4)), _NN)],
        (g_n * 12, 1), jax.ShapeDtypeStruct((NCHIP, DM, shard_cols), BF16),
        pl.BlockSpec((None, DM, CH), lambda qq, k: (qq // per, 0, qq % per)), None)


def _proj_fwd(name, o, wo, x, g):
    tm = 512

    def body(o_ref, w_ref, x_ref, g_ref, xn_ref, u_ref):
        u = jnp.dot(o_ref[...], w_ref[...], preferred_element_type=F32)
        u_ref[...] = u
        r = lax.rsqrt(jnp.mean(u * u, axis=-1, keepdims=True) + RMS_EPS)
        xn_ref[...] = x_ref[...] + u * r * g_ref[...]

    rows = pl.BlockSpec((tm, DM), lambda i: (i, 0))
    sh = jax.ShapeDtypeStruct((SEQ, DM), F32)
    return pl.pallas_call(
        body, grid=(SEQ // tm,), in_specs=[rows, pl.BlockSpec((DM, DM), lambda i: (0, 0)), rows, pl.BlockSpec((1, DM), lambda i: (0, 0))],
        out_specs=[rows, rows], out_shape=[sh, sh], compiler_params=_params(("parallel",)), name=name)(o, wo, x, g)


def _proj_bwd(name, dy, u, g, wo, dtype, carry):
    tm = 512

    def body(dy_ref, u_ref, g_ref, w_ref, do_ref, du_ref, dg_ref):
        dy = dy_ref[...]
        u = u_ref[...]
        r = lax.rsqrt(jnp.mean(u * u, axis=-1, keepdims=True) + RMS_EPS)
        yh = u * r
        t = dy * g_ref[...]
        du = (r * (t - yh * jnp.mean(t * yh, axis=-1, keepdims=True))).astype(BF16)
        du_ref[...] = du
        do_ref[...] = lax.dot_general(du, w_ref[...], _NT, preferred_element_type=F32).astype(do_ref.dtype)

        @pl.when(pl.program_id(0) == 0)
        def _():
            dg_ref[...] = jnp.zeros_like(dg_ref)

        dg_ref[...] += jnp.sum(dy * yh, axis=0, keepdims=True)

    rows = pl.BlockSpec((tm, DM), lambda i: (i, 0))
    vec = pl.BlockSpec((1, DM), lambda i: (0, 0))
    return _carrier_call(
        name, body, (SEQ // tm,), [rows, rows, vec, pl.BlockSpec((DM, DM), lambda i: (0, 0))], [rows, rows, vec],
        [jax.ShapeDtypeStruct((SEQ, DM), dtype), jax.ShapeDtypeStruct((SEQ, DM), BF16), jax.ShapeDtypeStruct((1, DM), F32)],
        [], (dy, u, g, wo), carry)


def _proj_bwd_dw(name, o, du):
    tn = 512
    return _matmul(
        name, [(o, pl.BlockSpec((SEQ, DM), lambda j, k: (0, 0)), du, pl.BlockSpec((SEQ, tn), lambda j, k: (0, j)), _TN)],
        (DM // tn, 1), jax.ShapeDtypeStruct((DM, DM), BF16), pl.BlockSpec((DM, tn), lambda j, k: (0, j)), None)


def _ffn_wspec(index_map):
    return pl.BlockSpec((None, FSH, DM), index_map)


def _ffn_bwd_dw(name, a4, b):
    return _matmul(
        name, [(a4, pl.BlockSpec((None, SEQ, FSH), lambda s, k: (s, 0, 0)), b, pl.BlockSpec((SEQ, DM), lambda s, k: (0, 0)), _TN)],
        (NCHIP, 1), jax.ShapeDtypeStruct((NCHIP, FSH, DM), BF16), _ffn_wspec(lambda s, k: (s, 0, 0)), None)


ROWS = 256


def _row_spec():
    return pl.BlockSpec((ROWS, DM), lambda i: (i, 0))


def _vec_spec():
    return pl.BlockSpec((1, DM), lambda i: (0, 0))


def _rms_fwd(name, x, g, dtype=BF16):
    def body(x_ref, g_ref, o_ref):
        x = x_ref[...]
        r = lax.rsqrt(jnp.mean(x * x, axis=-1, keepdims=True) + RMS_EPS)
        o_ref[...] = (x * r * g_ref[...]).astype(o_ref.dtype)

    return pl.pallas_call(body, grid=(SEQ // ROWS,), in_specs=[_row_spec(), _vec_spec()], out_specs=_row_spec(),
                          out_shape=jax.ShapeDtypeStruct((SEQ, DM), dtype), compiler_params=_params(("parallel",)), name=name)(x, g)


def _rms_fwd_both(name, x, g):
    def body(x_ref, g_ref, o_ref, t_ref):
        x = x_ref[...]
        r = lax.rsqrt(jnp.mean(x * x, axis=-1, keepdims=True) + RMS_EPS)
        h = x * r * g_ref[...]
        o_ref[...] = h.astype(o_ref.dtype)
        t_ref[...] = h.T.astype(t_ref.dtype)

    return pl.pallas_call(
        body, grid=(SEQ // ROWS,), in_specs=[_row_spec(), _vec_spec()], out_specs=[_row_spec(), pl.BlockSpec((DM, ROWS), lambda i: (0, i))],
        out_shape=[jax.ShapeDtypeStruct((SEQ, DM), BF16), jax.ShapeDtypeStruct((DM, SEQ), BF16)],
        compiler_params=_params(("parallel",)), name=name)(x, g)


def _norm_bwd(name, dys, u, g, res=None):
    ndy = len(dys)

    def body(*refs):
        dy = refs[0][...]
        for r_ in refs[1:ndy]:
            dy = dy + r_[...]
        u_ref, g_ref = refs[ndy], refs[ndy + 1]
        res_ref = refs[ndy + 2] if res is not None else None
        du_ref, dg_ref = refs[-2], refs[-1]
        u = u_ref[...]
        r = lax.rsqrt(jnp.mean(u * u, axis=-1, keepdims=True) + RMS_EPS)
        yh = u * r
        t = dy * g_ref[...]
        du = r * (t - yh * jnp.mean(t * yh, axis=-1, keepdims=True))
        if res_ref is not None:
            du = du + res_ref[...]
        du_ref[...] = du

        @pl.when(pl.program_id(0) == 0)
        def _():
            dg_ref[...] = jnp.zeros_like(dg_ref)

        dg_ref[...] += jnp.sum(dy * yh, axis=0, keepdims=True)

    ops = list(dys) + [u, g] + ([res] if res is not None else [])
    specs = [_row_spec()] * ndy + [_row_spec(), _vec_spec()] + ([_row_spec()] if res is not None else [])
    return pl.pallas_call(
        body, grid=(SEQ // ROWS,), in_specs=specs, out_specs=[_row_spec(), _vec_spec()],
        out_shape=[jax.ShapeDtypeStruct((SEQ, DM), F32), jax.ShapeDtypeStruct((1, DM), F32)],
        compiler_params=_params(("arbitrary",)), name=name)(*ops)


def _loss_grad(name, y, t):
    def body(y_ref, t_ref, dy_ref, l_ref):
        e = y_ref[...] - t_ref[...]
        dy_ref[...] = e * (1.0 / DM)

        @pl.when(pl.program_id(0) == 0)
        def _():
            l_ref[...] = jnp.zeros_like(l_ref)

        l_ref[...] += jnp.sum(e * e) * (0.5 / DM)

    return pl.pallas_call(
        body, grid=(SEQ // ROWS,), in_specs=[_row_spec(), _row_spec()],
        out_specs=[_row_spec(), pl.BlockSpec((1, 128), lambda i: (0, 0))],
        out_shape=[jax.ShapeDtypeStruct((SEQ, DM), F32), jax.ShapeDtypeStruct((1, 128), F32)],
        compiler_params=_params(("arbitrary",)), name=name)(y, t)


HBM_SPEC = pl.BlockSpec(memory_space=pltpu.HBM)


class _Carried:
    def __init__(self, ins, out_shapes, n_sems, issue, drain):
        self.ins, self.out_shapes, self.n_sems, self.issue, self.drain = list(ins), list(out_shapes), tuple(n_sems), issue, drain


def _carrier_call(name, body, grid, in_specs, out_specs, out_shape, scratch_shapes, operands, carry):
    n_in, n_out, n_scr = len(in_specs), len(out_specs), len(scratch_shapes)
    if carry is None:
        res = pl.pallas_call(body, grid=grid, in_specs=in_specs, out_specs=out_specs, out_shape=out_shape, scratch_shapes=scratch_shapes,
                             compiler_params=_params(("arbitrary",) * len(grid)), name=name)(*operands)
        return list(res), []
    ci, co = len(carry.ins), len(carry.out_shapes)

    def wrapped(*refs):
        ins, cins = refs[:n_in], refs[n_in:n_in + ci]
        outs, couts = refs[n_in + ci:n_in + ci + n_out], refs[n_in + ci + n_out:n_in + ci + n_out + co]
        scr, sems = refs[n_in + ci + n_out + co:n_in + ci + n_out + co + n_scr], refs[n_in + ci + n_out + co + n_scr:]
        first = functools.reduce(jnp.logical_and, [pl.program_id(a) == 0 for a in range(len(grid))])
        last = functools.reduce(jnp.logical_and, [pl.program_id(a) == grid[a] - 1 for a in range(len(grid))])

        @pl.when(first)
        def _():
            carry.issue(cins, couts, sems)

        body(*ins, *outs, *scr)

        @pl.when(last)
        def _():
            carry.drain(cins, couts, sems)

    res = pl.pallas_call(
        wrapped, grid=grid, in_specs=list(in_specs) + [HBM_SPEC] * ci, out_specs=list(out_specs) + [HBM_SPEC] * co,
        out_shape=list(out_shape) + carry.out_shapes,
        scratch_shapes=list(scratch_shapes) + [pltpu.SemaphoreType.DMA((k,)) for k in carry.n_sems],
        compiler_params=pltpu.CompilerParams(dimension_semantics=("arbitrary",) * len(grid), vmem_limit_bytes=VMEM_LIMIT, has_side_effects=True),
        name=name)(*operands, *carry.ins)
    return list(res[:n_out]), list(res[n_out:])


def _run_carried(name, carry):
    def body(*refs):
        ci, co = len(carry.ins), len(carry.out_shapes)
        carry.issue(refs[:ci], refs[ci:ci + co], refs[ci + co:])
        carry.drain(refs[:ci], refs[ci:ci + co], refs[ci + co:])

    return pl.pallas_call(
        body, in_specs=[HBM_SPEC] * len(carry.ins), out_specs=[HBM_SPEC] * len(carry.out_shapes), out_shape=carry.out_shapes,
        scratch_shapes=[pltpu.SemaphoreType.DMA((k,)) for k in carry.n_sems],
        compiler_params=pltpu.CompilerParams(has_side_effects=True), name=name)(*carry.ins)


NA_BLOCKS = SEQ // NA_QB
NA_ROWS_TOTAL = SEQ // GRID_W
NA_CLASSES = ((0, 0), (8, 4), (NA_ROWS_TOTAL - NA_QROWS, NA_ROWS_TOTAL - NA_WROWS))


def _na_pairs(i0, ws):
    out = []
    for qi in range(NA_QROWS):
        i = i0 + qi
        rs = min(max(i - 4, 0), NA_ROWS_TOTAL - 8)
        for kr in range(NA_WROWS):
            r = ws + kr
            if rs <= r < rs + 8:
                out.append((qi, kr, r - i + 7))
    return out


def _diag_onehot():
    qc, kc = np.meshgrid(np.arange(GRID_W), np.arange(GRID_W), indexing="ij")
    e = np.zeros((GRID_W * GRID_W, 128), np.float32)
    j = (kc - qc + 15).reshape(-1)
    ok = (j >= 0) & (j <= 30)
    e[np.arange(GRID_W * GRID_W)[ok], j[ok]] = 1.0
    return jnp.asarray(e)


def _rpb_expand(rpb):
    r2 = jnp.pad(rpb.reshape(NH * 15, 31), ((0, 0), (0, 128 - 31)))

    def body(r_ref, e_ref, o_ref):
        o_ref[...] = lax.dot_general(r_ref[...], e_ref[...], _NT, preferred_element_type=F32, precision=lax.Precision.HIGHEST)

    out = pl.pallas_call(body, out_shape=jax.ShapeDtypeStruct((NH * 15, GRID_W * GRID_W), F32), name="rpb_expand",
                         compiler_params=pltpu.CompilerParams(vmem_limit_bytes=VMEM_LIMIT))(r2, _diag_onehot())
    return out.reshape(NH, 15, GRID_W, GRID_W)


def _na_bias_tiles(rpb, carry):
    def body(b_ref, o_ref):
        qc = lax.broadcasted_iota(jnp.int32, (GRID_W, GRID_W), 0)
        kc = lax.broadcasted_iota(jnp.int32, (GRID_W, GRID_W), 1)
        first = jnp.clip(qc - 8, 0, GRID_W - 16)
        in_window = (kc >= first) & (kc < first + 16)
        neg = jnp.full((GRID_W, GRID_W), NEG, F32)
        for cls, (i0, ws) in enumerate(NA_CLASSES):
            @pl.when(pl.program_id(0) == cls)
            def _(i0=i0, ws=ws):
                pairs = {(qi, kr): dr for qi, kr, dr in _na_pairs(i0, ws)}
                masked = {dr: jnp.where(in_window, b_ref[dr], NEG) for dr in sorted(set(pairs.values()))}
                for qi in range(NA_QROWS):
                    for k2 in range(NA_WROWS // 2):
                        blocks = [masked[pairs[(qi, kr)]] if (qi, kr) in pairs else neg for kr in (2 * k2, 2 * k2 + 1)]
                        o_ref[qi * GRID_W:(qi + 1) * GRID_W, k2 * 128:(k2 + 1) * 128] = jnp.concatenate(blocks, axis=1)

    (tiles,), sent = _carrier_call(
        "na_bias_tiles", body, (3, NH), [pl.BlockSpec((None, 15, GRID_W, GRID_W), lambda c, h: (h, 0, 0, 0))],
        [pl.BlockSpec((None, None, NA_QB, NA_WIN), lambda c, h: (c, h, 0, 0))], [jax.ShapeDtypeStruct((3, NH, NA_QB, NA_WIN), F32)],
        [], (_rpb_expand(rpb),), carry)
    return tiles, sent


def _na_cls(b):
    return jnp.where(b == 0, 0, jnp.where(b == NA_BLOCKS - 1, 2, 1))


def _na_start(b):
    return pl.multiple_of(jnp.clip(b * NA_QROWS - 4, 0, NA_ROWS_TOTAL - NA_WROWS) * GRID_W, GRID_W)


HPS = 4
LW = HPS * HD
NLW = DM // LW


NA_BWD_HPS = 4


def _na_in_specs(hps=HPS):
    lw = hps * HD
    nlw = DM // lw
    return [pl.BlockSpec((NA_QB, lw), lambda hp, b: (b, hp)),
            pl.BlockSpec((SEQ, lw), lambda hp, b: (0, nlw + hp)),
            pl.BlockSpec((SEQ, lw), lambda hp, b: (0, 2 * nlw + hp)),
            pl.BlockSpec((None, hps, NA_QB, NA_WIN), lambda hp, b: (_na_cls(b), hp, 0, 0))]


def _na_fwd(qkv, bias, carry):
    def body(q_ref, k_ref, v_ref, b_ref, o_ref):
        start = _na_start(pl.program_id(1))
        q = q_ref[...]
        kw = k_ref[pl.ds(start, NA_WIN), :]
        vw = v_ref[pl.ds(start, NA_WIN), :]
        outs = []
        for hh in range(HPS):
            sl = slice(hh * HD, (hh + 1) * HD)
            s = lax.dot_general(q[:, sl] * QSCALE, kw[:, sl], _NT, preferred_element_type=F32) + b_ref[hh]
            p = jnp.exp(s - jnp.max(s, axis=-1, keepdims=True))
            l = jnp.sum(p, axis=-1, keepdims=True)
            outs.append(jnp.dot(p.astype(BF16), vw[:, sl], preferred_element_type=F32) / l)
        o_ref[...] = jnp.concatenate(outs, axis=1).astype(o_ref.dtype)

    (o,), sent = _carrier_call(
        "na_fwd", body, (NLW, NA_BLOCKS), _na_in_specs(), [pl.BlockSpec((NA_QB, LW), lambda hp, b: (b, hp))],
        [jax.ShapeDtypeStruct((SEQ, DM), BF16)], [], (qkv, qkv, qkv, bias), carry)
    return o, sent


def _na_bwd(qkv, bias, do, carry):
    lw = NA_BWD_HPS * HD

    def body(q_ref, k_ref, v_ref, b_ref, do_ref, dqkv_ref, z_ref, dk_acc, dv_acc):
        blk = pl.program_id(1)

        @pl.when(blk == 0)
        def _():
            dk_acc[...] = jnp.zeros_like(dk_acc)
            dv_acc[...] = jnp.zeros_like(dv_acc)
            z_ref[...] = jnp.zeros_like(z_ref)

        start = _na_start(blk)
        q = q_ref[...]
        do = do_ref[...]
        kw = k_ref[pl.ds(start, NA_WIN), :]
        vw = v_ref[pl.ds(start, NA_WIN), :]
        dqs, dks, dvs, dss = [], [], [], []
        for hh in range(NA_BWD_HPS):
            sl = slice(hh * HD, (hh + 1) * HD)
            qh = q[:, sl] * QSCALE
            s = lax.dot_general(qh, kw[:, sl], _NT, preferred_element_type=F32) + b_ref[hh]
            p = jnp.exp(s - jnp.max(s, axis=-1, keepdims=True))
            p = p / jnp.sum(p, axis=-1, keepdims=True)
            dp = lax.dot_general(do[:, sl], vw[:, sl], _NT, preferred_element_type=F32)
            ds = p * (dp - jnp.sum(p * dp, axis=-1, keepdims=True))
            dsb = ds.astype(BF16)
            dqs.append(jnp.dot(dsb, kw[:, sl], preferred_element_type=F32) * QSCALE)
            dks.append(lax.dot_general(dsb, qh, _TN, preferred_element_type=F32))
            dvs.append(lax.dot_general(p.astype(BF16), do[:, sl], _TN, preferred_element_type=F32))
            dss.append(ds)
        for cls, (i0, ws) in enumerate(NA_CLASSES):
            @pl.when(_na_cls(blk) == cls)
            def _(i0=i0, ws=ws):
                for hh, ds in enumerate(dss):
                    for qi, kr, dr in _na_pairs(i0, ws):
                        z_ref[hh, dr * GRID_W:(dr + 1) * GRID_W, :] += ds[qi * GRID_W:(qi + 1) * GRID_W, kr * GRID_W:(kr + 1) * GRID_W]
        dqkv_ref[0, pl.ds(pl.multiple_of(blk * NA_QB, NA_QB), NA_QB), :] = jnp.concatenate(dqs, axis=1).astype(dqkv_ref.dtype)
        dk_acc[pl.ds(start, NA_WIN), :] += jnp.concatenate(dks, axis=1)
        dv_acc[pl.ds(start, NA_WIN), :] += jnp.concatenate(dvs, axis=1)

        @pl.when(blk == NA_BLOCKS - 1)
        def _():
            dqkv_ref[1] = dk_acc[...].astype(dqkv_ref.dtype)
            dqkv_ref[2] = dv_acc[...].astype(dqkv_ref.dtype)

    (dqkv, z), sent = _carrier_call(
        "na_bwd", body, (NH // NA_BWD_HPS, NA_BLOCKS),
        _na_in_specs(NA_BWD_HPS) + [pl.BlockSpec((NA_QB, lw), lambda hp, b: (b, hp))],
        [pl.BlockSpec((3, SEQ, lw), lambda hp, b: (0, 0, hp)), pl.BlockSpec((NA_BWD_HPS, 15 * GRID_W, GRID_W), lambda hp, b: (hp, 0, 0))],
        [jax.ShapeDtypeStruct((3, SEQ, DM), BF16), jax.ShapeDtypeStruct((NH, 15 * GRID_W, GRID_W), F32)],
        [pltpu.VMEM((SEQ, lw), F32), pltpu.VMEM((SEQ, lw), F32)], (qkv, qkv, qkv, bias, do), carry)
    return dqkv, z, sent


def _rpb_grad(z):
    z2 = z.reshape(NH * 15, GRID_W * GRID_W)

    def body(z_ref, e_ref, o_ref):
        o_ref[...] = jnp.dot(z_ref[...], e_ref[...], preferred_element_type=F32, precision=lax.Precision.HIGHEST)

    out = pl.pallas_call(body, out_shape=jax.ShapeDtypeStruct((NH * 15, 128), F32), name="rpb_grad",
                         compiler_params=pltpu.CompilerParams(vmem_limit_bytes=VMEM_LIMIT))(z2, _diag_onehot())
    return out[:, :31].reshape(NH, 15, 31)


DIL_BLOCKS = SEQ // DIL_QB
DIL_HPS = 8
DIL_LW = DIL_HPS * HD
DIL_NLW = DM // DIL_LW


COLS = 128


def _col_spec():
    return pl.BlockSpec((SEQ, COLS), lambda j: (0, j))


def _grp_spec():
    return pl.BlockSpec((3, SEQ, COLS), lambda j: (0, 0, j))


def _store_group_order(dst_ref, src_ref):
    for g, d in enumerate(DIL):
        n = SEQ // d
        for r in range(d):
            dst_ref[g, r * n:(r + 1) * n, :] = src_ref[pl.ds(r, n, stride=d), :].astype(dst_ref.dtype)


def _store_token_order(dst_ref, src_ref, g):
    d = DIL[g]
    n = SEQ // d
    for r in range(d):
        dst_ref[pl.ds(r, n, stride=d), :] = src_ref[g, r * n:(r + 1) * n, :]


def _to_groups(name, a):
    def body(a_ref, o_ref, t_ref):
        _store_group_order(o_ref, a_ref)
        for g in range(3):
            t_ref[g] = o_ref[g].astype(F32).T.astype(t_ref.dtype)

    return pl.pallas_call(
        body, grid=(DM // COLS,), in_specs=[_col_spec()], out_specs=[_grp_spec(), pl.BlockSpec((3, COLS, SEQ), lambda j: (0, j, 0))],
        out_shape=[jax.ShapeDtypeStruct((3, SEQ, DM), BF16), jax.ShapeDtypeStruct((3, DM, SEQ), BF16)],
        compiler_params=_params(("parallel",)), name=name)(a)


def _from_groups_sum(name, a):
    def body(a_ref, o_ref, t1, t2):
        _store_token_order(t1, a_ref, 1)
        _store_token_order(t2, a_ref, 2)
        o_ref[...] = (a_ref[0] + t1[...]) + t2[...]

    return pl.pallas_call(body, grid=(DM // COLS,), in_specs=[_grp_spec()], out_specs=_col_spec(),
                          out_shape=jax.ShapeDtypeStruct((SEQ, DM), F32), scratch_shapes=[pltpu.VMEM((SEQ, COLS), F32)] * 2,
                          compiler_params=_params(("parallel",)), name=name)(a)


def _dil_start(b):
    return pl.multiple_of(jnp.clip(b * DIL_QB - DIL_RADIUS, 0, SEQ - DIL_WIN), DIL_RADIUS)


def _dil_mask(g, b, start):
    shift = 11 - 2 * g
    ii = b * DIL_QB + lax.broadcasted_iota(jnp.int32, (DIL_QB, DIL_WIN), 0)
    jj = start + lax.broadcasted_iota(jnp.int32, (DIL_QB, DIL_WIN), 1)
    dist = jnp.abs(ii - jj)
    valid = (dist <= DIL_RADIUS) & (jnp.right_shift(ii, shift) == jnp.right_shift(jj, shift))
    return valid, dist.astype(F32)


def _dil_in_specs():
    return [pl.BlockSpec(memory_space=pltpu.SMEM),
            pl.BlockSpec((None, DIL_QB, DIL_LW), lambda g, hp, b: (g, b, hp)),
            pl.BlockSpec((None, SEQ, DIL_LW), lambda g, hp, b: (g, 0, DIL_NLW + hp)),
            pl.BlockSpec((None, SEQ, DIL_LW), lambda g, hp, b: (g, 0, 2 * DIL_NLW + hp))]


def _dil_fwd(qkv, slopes, carry):
    def body(sl_ref, q_ref, k_ref, v_ref, o_ref, lse_ref):
        g, hp, b = pl.program_id(0), pl.program_id(1), pl.program_id(2)
        start = _dil_start(b)
        valid, dist = _dil_mask(g, b, start)
        dil = jnp.left_shift(1, 2 * g).astype(F32)
        q = q_ref[...]
        kw = k_ref[pl.ds(start, DIL_WIN), :]
        vw = v_ref[pl.ds(start, DIL_WIN), :]
        outs, lses = [], []
        for hh in range(DIL_HPS):
            sl = slice(hh * HD, (hh + 1) * HD)
            s = lax.dot_general(q[:, sl] * QSCALE, kw[:, sl], _NT, preferred_element_type=F32)
            s = jnp.where(valid, s - (sl_ref[hp * DIL_HPS + hh] * dil) * dist, NEG)
            m = jnp.max(s, axis=-1, keepdims=True)
            p = jnp.exp(s - m)
            l = jnp.sum(p, axis=-1, keepdims=True)
            outs.append(jnp.dot(p.astype(BF16), vw[:, sl], preferred_element_type=F32) / l)
            lses.append(jnp.broadcast_to(m + jnp.log(l), (DIL_QB, HD)))
        o_ref[...] = jnp.concatenate(outs, axis=1)
        lse_ref[...] = jnp.concatenate(lses, axis=1)

    ospec = pl.BlockSpec((None, DIL_QB, DIL_LW), lambda g, hp, b: (g, b, hp))
    sh = jax.ShapeDtypeStruct((3, SEQ, DM), F32)
    (o, lse), sent = _carrier_call("dil_fwd", body, (3, DIL_NLW, DIL_BLOCKS), _dil_in_specs(), [ospec, ospec], [sh, sh], [],
                                   (slopes, qkv, qkv, qkv), carry)
    return o, lse, sent


def _dil_merge(o_all, lse_all):
    def body(o_ref, l_ref, out_ref, lse_ref, o1, o2, l1, l2):
        for g, (ot, lt) in ((1, (o1, l1)), (2, (o2, l2))):
            _store_token_order(ot, o_ref, g)
            _store_token_order(lt, l_ref, g)
        la, lb, lc = l_ref[0], l1[...], l2[...]
        m = jnp.maximum(jnp.maximum(la, lb), lc)
        wa, wb, wc = jnp.exp(la - m), jnp.exp(lb - m), jnp.exp(lc - m)
        sw = (wa + wb) + wc
        out_ref[...] = (((wa * o_ref[0] + wb * o1[...]) + wc * o2[...]) / sw).astype(out_ref.dtype)
        lse_ref[...] = m + jnp.log(sw)

    return pl.pallas_call(
        body, grid=(DM // COLS,), in_specs=[_grp_spec(), _grp_spec()], out_specs=[_col_spec(), _col_spec()],
        out_shape=[jax.ShapeDtypeStruct((SEQ, DM), BF16), jax.ShapeDtypeStruct((SEQ, DM), F32)],
        scratch_shapes=[pltpu.VMEM((SEQ, COLS), F32)] * 4, compiler_params=_params(("parallel",)), name="dil_merge")(o_all, lse_all)


def _dil_bwd_prep(do, o, lse):
    heads = COLS // HD

    def body(do_ref, o_ref, lse_ref, dog_ref, ddr_ref, lser_ref, dd, grp):
        prod = do_ref[...] * o_ref[...].astype(F32)
        dd[...] = jnp.concatenate(
            [jnp.broadcast_to(jnp.sum(prod[:, h * HD:(h + 1) * HD], axis=-1, keepdims=True), (SEQ, HD)) for h in range(heads)], axis=1)
        _store_group_order(dog_ref, do_ref)
        for src, dst in ((dd, ddr_ref), (lse_ref, lser_ref)):
            _store_group_order(grp, src)
            for g in range(3):
                t = grp[g].T
                for h in range(heads):
                    dst[g, h] = t[h * HD:h * HD + 8, :]

    rows = jax.ShapeDtypeStruct((3, NH, 8, SEQ), F32)
    rspec = pl.BlockSpec((3, heads, 8, SEQ), lambda j: (0, j, 0, 0))
    return pl.pallas_call(
        body, grid=(DM // COLS,), in_specs=[_col_spec()] * 3, out_specs=[_grp_spec(), rspec, rspec],
        out_shape=[jax.ShapeDtypeStruct((3, SEQ, DM), BF16), rows, rows],
        scratch_shapes=[pltpu.VMEM((SEQ, COLS), F32), pltpu.VMEM((3, SEQ, COLS), F32)],
        compiler_params=_params(("parallel",)), name="dil_bwd_prep")(do, o, lse)


def _dil_bwd(qkv, do, dd, lse, slopes, carry):
    def body(sl_ref, q_ref, k_ref, v_ref, do_ref, dd_ref, lse_ref, dqkv_ref, dk_acc, dv_acc):
        g, hp, b = pl.program_id(0), pl.program_id(1), pl.program_id(2)

        @pl.when(b == 0)
        def _():
            dk_acc[...] = jnp.zeros_like(dk_acc)
            dv_acc[...] = jnp.zeros_like(dv_acc)

        start = _dil_start(b)
        shift = 11 - 2 * g
        jj = start + lax.broadcasted_iota(jnp.int32, (DIL_WIN, DIL_QB), 0)
        ii = b * DIL_QB + lax.broadcasted_iota(jnp.int32, (DIL_WIN, DIL_QB), 1)
        dist = jnp.abs(ii - jj)
        valid = (dist <= DIL_RADIUS) & (jnp.right_shift(ii, shift) == jnp.right_shift(jj, shift))
        dist = dist.astype(F32)
        dil = jnp.left_shift(1, 2 * g).astype(F32)
        q = q_ref[...]
        do = do_ref[...]
        kw = k_ref[pl.ds(start, DIL_WIN), :]
        vw = v_ref[pl.ds(start, DIL_WIN), :]
        dqs, dks, dvs = [], [], []
        for hh in range(DIL_HPS):
            sl = slice(hh * HD, (hh + 1) * HD)
            qh = q[:, sl] * QSCALE
            st = lax.dot_general(kw[:, sl], qh, _NT, preferred_element_type=F32)
            st = jnp.where(valid, st - (sl_ref[hp * DIL_HPS + hh] * dil) * dist, NEG)
            pt = jnp.exp(st - lse_ref[hh, 0:1, :])
            dpt = lax.dot_general(vw[:, sl], do[:, sl], _NT, preferred_element_type=F32)
            dst = (pt * (dpt - dd_ref[hh, 0:1, :])).astype(BF16)
            dqs.append(lax.dot_general(kw[:, sl], dst, _TN, preferred_element_type=F32).T * QSCALE)
            dks.append(jnp.dot(dst, qh, preferred_element_type=F32))
            dvs.append(jnp.dot(pt.astype(BF16), do[:, sl], preferred_element_type=F32))
        dqkv_ref[0, pl.ds(pl.multiple_of(b * DIL_QB, DIL_QB), DIL_QB), :] = jnp.concatenate(dqs, axis=1).astype(dqkv_ref.dtype)
        dk_acc[pl.ds(start, DIL_WIN), :] += jnp.concatenate(dks, axis=1)
        dv_acc[pl.ds(start, DIL_WIN), :] += jnp.concatenate(dvs, axis=1)

        @pl.when(b == DIL_BLOCKS - 1)
        def _():
            dqkv_ref[1] = dk_acc[...].astype(dqkv_ref.dtype)
            dqkv_ref[2] = dv_acc[...].astype(dqkv_ref.dtype)

    qspec = pl.BlockSpec((None, DIL_QB, DIL_LW), lambda g, hp, b: (g, b, hp))
    rspec = pl.BlockSpec((None, DIL_HPS, 8, DIL_QB), lambda g, hp, b: (g, hp, 0, b))
    (dqkv,), sent = _carrier_call(
        "dil_bwd", body, (3, DIL_NLW, DIL_BLOCKS), _dil_in_specs() + [qspec, rspec, rspec],
        [pl.BlockSpec((None, 3, SEQ, DIL_LW), lambda g, hp, b: (g, 0, 0, hp))], [jax.ShapeDtypeStruct((3, 3, SEQ, DM), BF16)],
        [pltpu.VMEM((SEQ, DIL_LW), F32), pltpu.VMEM((SEQ, DIL_LW), F32)], (slopes, qkv, qkv, qkv, do, dd, lse), carry)
    return dqkv, sent


def _ffn_fwd(name, x, g_pre, g_post, wgt4, wut4, wd4, carry):
    tm = 512

    def body(x_ref, gpre_ref, gpost_ref, wg_ref, wu_ref, wd_ref, xn_ref, h_ref, gate_ref, up_ref, u_ref, acc):
        s = pl.program_id(1)

        @pl.when(s == 0)
        def _():
            x = x_ref[...]
            r = lax.rsqrt(jnp.mean(x * x, axis=-1, keepdims=True) + RMS_EPS)
            h_ref[...] = (x * r * gpre_ref[...]).astype(h_ref.dtype)

        h = h_ref[...]
        gate = lax.dot_general(h, wg_ref[...], _NT, preferred_element_type=F32).astype(BF16)
        up = lax.dot_general(h, wu_ref[...], _NT, preferred_element_type=F32).astype(BF16)
        gate_ref[...] = gate
        up_ref[...] = up
        gf = gate.astype(F32)
        act = (gf * jax.nn.sigmoid(gf) * up.astype(F32)).astype(BF16)
        part = jnp.dot(act, wd_ref[...], preferred_element_type=F32)

        @pl.when(s == 0)
        def _():
            acc[...] = part

        @pl.when(s > 0)
        def _():
            acc[...] += part

        @pl.when(s == NCHIP - 1)
        def _():
            u = acc[...]
            u_ref[...] = u
            r = lax.rsqrt(jnp.mean(u * u, axis=-1, keepdims=True) + RMS_EPS)
            xn_ref[...] = x_ref[...] + u * r * gpost_ref[...]

    rows = pl.BlockSpec((tm, DM), lambda i, s: (i, 0))
    vec = pl.BlockSpec((1, DM), lambda i, s: (0, 0))
    wspec = _ffn_wspec(lambda i, s: (s, 0, 0))
    mid = pl.BlockSpec((None, tm, FSH), lambda i, s: (s, i, 0))
    outs, sent = _carrier_call(
        name, body, (SEQ // tm, NCHIP), [rows, vec, vec, wspec, wspec, wspec], [rows, rows, mid, mid, rows],
        [jax.ShapeDtypeStruct((SEQ, DM), F32), jax.ShapeDtypeStruct((SEQ, DM), BF16), jax.ShapeDtypeStruct((NCHIP, SEQ, FSH), BF16),
         jax.ShapeDtypeStruct((NCHIP, SEQ, FSH), BF16), jax.ShapeDtypeStruct((SEQ, DM), F32)],
        [pltpu.VMEM((tm, DM), F32)], (x, g_pre, g_post, wgt4, wut4, wd4), carry)
    return outs, sent


def _ffn_block(layer, x, g_pre, g_post, ex):
    tag = f"l{layer}_ffn_fwd"
    (x_new, h, gate, up, u), sent = _ffn_fwd(tag, x, g_pre, g_post, ex.weight(("ffn_w_gate", layer)), ex.weight(("ffn_w_up", layer)),
                                             ex.weight(("ffn_w_down", layer)), ex.carry(tag))
    ex.carried(tag, sent)
    return x_new, (x, h, gate, up, u)


def _ffn_bwd(name, dx, x, gate, up, u, g_pre, g_post, wgt4, wut4, wd4, carry):
    tm = 512

    def body(dx_ref, x_ref, gate_ref, up_ref, u_ref, gpre_ref, gpost_ref, wg_ref, wu_ref, wd_ref,
             dxin_ref, du_ref, dgate_ref, dup_ref, act_ref, dgpre_ref, dgpost_ref, dh_acc):
        i, s = pl.program_id(0), pl.program_id(1)

        @pl.when((i == 0) & (s == 0))
        def _():
            dgpre_ref[...] = jnp.zeros_like(dgpre_ref)
            dgpost_ref[...] = jnp.zeros_like(dgpost_ref)

        @pl.when(s == 0)
        def _():
            dy = dx_ref[...]
            uu = u_ref[...]
            r = lax.rsqrt(jnp.mean(uu * uu, axis=-1, keepdims=True) + RMS_EPS)
            yh = uu * r
            t = dy * gpost_ref[...]
            du_ref[...] = (r * (t - yh * jnp.mean(t * yh, axis=-1, keepdims=True))).astype(du_ref.dtype)
            dgpost_ref[...] += jnp.sum(dy * yh, axis=0, keepdims=True)

        dact = lax.dot_general(du_ref[...], wd_ref[...], _NT, preferred_element_type=F32)
        g = gate_ref[...].astype(F32)
        upv = up_ref[...].astype(F32)
        sg = jax.nn.sigmoid(g)
        dgate = (dact * upv * sg * (1.0 + g * (1.0 - sg))).astype(BF16)
        dup = (dact * g * sg).astype(BF16)
        dgate_ref[...] = dgate
        dup_ref[...] = dup
        act_ref[...] = (g * sg * upv).astype(act_ref.dtype)
        part = jnp.dot(dgate, wg_ref[...], preferred_element_type=F32) + jnp.dot(dup, wu_ref[...], preferred_element_type=F32)

        @pl.when(s == 0)
        def _():
            dh_acc[...] = part

        @pl.when(s > 0)
        def _():
            dh_acc[...] += part

        @pl.when(s == NCHIP - 1)
        def _():
            dh = dh_acc[...]
            xx = x_ref[...]
            r = lax.rsqrt(jnp.mean(xx * xx, axis=-1, keepdims=True) + RMS_EPS)
            yh = xx * r
            t = dh * gpre_ref[...]
            dxin_ref[...] = dx_ref[...] + r * (t - yh * jnp.mean(t * yh, axis=-1, keepdims=True))
            dgpre_ref[...] += jnp.sum(dh * yh, axis=0, keepdims=True)

    rows = pl.BlockSpec((tm, DM), lambda i, s: (i, 0))
    vec = pl.BlockSpec((1, DM), lambda i, s: (0, 0))
    wspec = _ffn_wspec(lambda i, s: (s, 0, 0))
    mid = pl.BlockSpec((None, tm, FSH), lambda i, s: (s, i, 0))
    mid_shape = jax.ShapeDtypeStruct((NCHIP, SEQ, FSH), BF16)
    return _carrier_call(
        name, body, (SEQ // tm, NCHIP), [rows, rows, mid, mid, rows, vec, vec, wspec, wspec, wspec], [rows, rows, mid, mid, mid, vec, vec],
        [jax.ShapeDtypeStruct((SEQ, DM), F32), jax.ShapeDtypeStruct((SEQ, DM), BF16), mid_shape, mid_shape, mid_shape,
         jax.ShapeDtypeStruct((1, DM), F32), jax.ShapeDtypeStruct((1, DM), F32)],
        [pltpu.VMEM((tm, DM), F32)], (dx, x, gate, up, u, g_pre, g_post, wgt4, wut4, wd4), carry)


def _ffn_block_bwd(layer, dx, saved, g_pre, g_post, ex):
    tag = f"l{layer}"
    x, h, gate, up, u = saved
    (dx_in, du, dgate, dup, act, dg_pre, dg_post), sent = _ffn_bwd(
        f"{tag}_ffn_bwd", dx, x, gate, up, u, g_pre, g_post, ex.weight(("ffn_w_gate", layer)), ex.weight(("ffn_w_up", layer)),
        ex.weight(("ffn_w_down", layer)), ex.carry(f"{tag}_ffn_bwd"))
    ex.carried(f"{tag}_ffn_bwd", sent)
    d_wd = _ffn_bwd_dw(f"{tag}_dwd", act, du)
    d_wg = _ffn_bwd_dw(f"{tag}_dwg", dgate, h)
    d_wu = _ffn_bwd_dw(f"{tag}_dwu", dup, h)
    ex.grads(f"{tag}_ffn", {("ffn_w_gate", layer): d_wg, ("ffn_w_up", layer): d_wu, ("ffn_w_down", layer): d_wd})
    return dx_in, dg_pre, dg_post


def _alibi_slopes():
    return 2.0 ** (-8.0 * jnp.arange(1, NH + 1, dtype=F32) / NH)


def _local_step(x, target, norms, rpb, ex):
    g_mix_pre, g_mix_post, g_ffn_pre, g_ffn_post = norms
    row = lambda a, i: a[i:i + 1]

    bias, sent = _na_bias_tiles(rpb, ex.carry("na_bias_tiles"))
    ex.carried("na_bias_tiles", sent)
    h0, h0t = _rms_fwd_both("l0_mix_pre", x, row(g_mix_pre, 0))
    qkv0, sent = _qkv_fwd("l0_qkv", h0[None], ex.weight(("na_w_qkv", 0)), ex.carry("l0_qkv"))
    ex.carried("l0_qkv", sent)
    o0, sent = _na_fwd(qkv0[0], bias, ex.carry("na_fwd"))
    ex.carried("na_fwd", sent)
    na_wo = ex.weight(("na_w_o", 0)).reshape(DM, DM)
    x1, u0 = _proj_fwd("l0_proj", o0, na_wo, x, row(g_mix_post, 0))
    x2, ffn0 = _ffn_block(0, x1, row(g_ffn_pre, 0), row(g_ffn_post, 0), ex)

    slopes = _alibi_slopes()
    h2g, h2gt = _to_groups("l1_h_groups", _rms_fwd("l1_mix_pre", x2, row(g_mix_pre, 1), F32))
    dil_wqkv = ex.weight(("dil_w_qkv", 0))
    qkv1, sent = _qkv_fwd("l1_qkv", h2g, dil_wqkv, ex.carry("l1_qkv"))
    ex.carried("l1_qkv", sent)
    og, lg, sent = _dil_fwd(qkv1, slopes, ex.carry("dil_fwd"))
    ex.carried("dil_fwd", sent)
    o1, lse = _dil_merge(og, lg)
    dil_wo = ex.weight(("dil_w_o", 0)).reshape(DM, DM)
    x3, u1 = _proj_fwd("l1_proj", o1, dil_wo, x2, row(g_mix_post, 1))
    x4, ffn1 = _ffn_block(1, x3, row(g_ffn_pre, 1), row(g_ffn_post, 1), ex)

    dx4, loss_row = _loss_grad("loss", x4, target)

    dx3, dg_fpre1, dg_fpost1 = _ffn_block_bwd(1, dx4, ffn1, row(g_ffn_pre, 1), row(g_ffn_post, 1), ex)
    (do1, du1, dg_mpost1), sent = _proj_bwd("l1_proj_bwd", dx3, u1, row(g_mix_post, 1), dil_wo, F32, ex.carry("l1_proj_bwd"))
    ex.carried("l1_proj_bwd", sent)
    d_dil_wo = _proj_bwd_dw("l1_dwo", o1, du1)
    dog, ddg, lseg = _dil_bwd_prep(do1, o1, lse)
    dqkv1, sent = _dil_bwd(qkv1, dog, ddg, lseg, slopes, ex.carry("dil_bwd"))
    ex.carried("dil_bwd", sent)
    d_dil_wqkv = _qkv_bwd_dw("l1_dwqkv", h2gt, dqkv1, dil_wqkv.shape[2])
    ex.grads("l1_mix", {("dil_w_qkv", 0): d_dil_wqkv, ("dil_w_o", 0): d_dil_wo.reshape(NCHIP, DM // NCHIP, DM)})
    dh2g, sent = _qkv_bwd_dh("l1_dh", dqkv1, dil_wqkv, ex.carry("l1_dh"))
    ex.carried("l1_dh", sent)
    dh2 = _from_groups_sum("l1_dh_tokens", dh2g)
    dx2, dg_mpre1 = _norm_bwd("l1_mix_pre_bwd", [dh2], x2, row(g_mix_pre, 1), res=dx3)

    dx1, dg_fpre0, dg_fpost0 = _ffn_block_bwd(0, dx2, ffn0, row(g_ffn_pre, 0), row(g_ffn_post, 0), ex)
    (do0, du0, dg_mpost0), sent = _proj_bwd("l0_proj_bwd", dx1, u0, row(g_mix_post, 0), na_wo, BF16, ex.carry("l0_proj_bwd"))
    ex.carried("l0_proj_bwd", sent)
    d_na_wo = _proj_bwd_dw("l0_dwo", o0, du0)
    dqkv0, z, sent = _na_bwd(qkv0[0], bias, do0, ex.carry("na_bwd"))
    ex.carried("na_bwd", sent)
    d_rpb = _rpb_grad(z)
    na_wqkv = ex.weight(("na_w_qkv", 0))
    d_na_wqkv = _qkv_bwd_dw("l0_dwqkv", h0t[None], dqkv0[None], na_wqkv.shape[2])
    ex.grads("l0_mix", {("na_w_qkv", 0): d_na_wqkv, ("na_w_o", 0): d_na_wo.reshape(NCHIP, DM // NCHIP, DM)})
    dh0, sent = _qkv_bwd_dh("l0_dh", dqkv0[None], na_wqkv, ex.carry("l0_dh"))
    ex.carried("l0_dh", sent)
    dx0, dg_mpre0 = _norm_bwd("l0_mix_pre_bwd", [dh0[0]], x, row(g_mix_pre, 0), res=dx1)

    dnorms = (jnp.concatenate([dg_mpre0, dg_mpre1]), jnp.concatenate([dg_mpost0, dg_mpost1]),
              jnp.concatenate([dg_fpre0, dg_fpre1]), jnp.concatenate([dg_fpost0, dg_fpost1]))
    return loss_row, dx0, dnorms, d_rpb


def _place():
    x, y, c = lax.axis_index("x"), lax.axis_index("y"), lax.axis_index("c")
    chips = ((1 - x, y), (x, 1 - y), (1 - x, 1 - y))
    return x, y, c, chips


def _chip_id(chip):
    return 2 * chip[0] + chip[1]


def _comm_call(name, body, ins, out_shapes, n_sems, aliases=None):
    return pl.pallas_call(
        body, in_specs=[HBM_SPEC] * len(ins), out_specs=[HBM_SPEC] * len(out_shapes), out_shape=out_shapes,
        scratch_shapes=[pltpu.SemaphoreType.DMA((k,)) for k in n_sems], input_output_aliases=aliases or {},
        compiler_params=pltpu.CompilerParams(has_side_effects=True), name=name)(*ins)


def _gather_copies(shards):
    n = len(shards)

    def copies(src, out, sems):
        send_sems, recv_sems = sems
        x, y, c, chips = _place()

        def copy(t, k, chip, half, to, from_src=False):
            blk = out[t].at[_chip_id(chip), half]
            return pltpu.make_async_remote_copy(
                src_ref=src[t].at[half] if from_src else blk, dst_ref=blk,
                send_sem=send_sems.at[6 * t + k], recv_sem=recv_sems.at[6 * t + k], device_id=to, device_id_type=MESH)

        return copy, x, y, c, chips

    def issue(src, out, sems):
        copy, x, y, c, chips = copies(src, out, sems)
        for t in range(n):
            for j, chip in enumerate(chips):
                copy(t, j, (x, y), c, (*chip, c), from_src=True).start()

    def drain(src, out, sems):
        copy, x, y, c, chips = copies(src, out, sems)
        passed = []
        for t in range(n):
            for j, chip in enumerate(chips):
                copy(t, j, chip, c, (x, y, c)).wait_recv()
                fwd = copy(t, 3 + j, chip, c, (x, y, 1 - c))
                fwd.start()
                passed.append(fwd)
        for t in range(n):
            for j, chip in enumerate(chips):
                copy(t, 3 + j, chip, 1 - c, (x, y, c)).wait_recv()
        for t in range(n):
            for j, chip in enumerate(chips):
                copy(t, j, (x, y), c, (*chip, c), from_src=True).wait_send()
        for cp in passed:
            cp.wait_send()

    return _Carried(shards, [jax.ShapeDtypeStruct((NCHIP,) + s.shape, s.dtype) for s in shards], (6 * n, 6 * n), issue, drain)


def _pair_exchange_copies(grads):
    n = len(grads)

    def copies(g, theirs, sems):
        send_sems, recv_sems = sems
        x, y, c, _ = _place()
        return [pltpu.make_async_remote_copy(src_ref=g[t].at[:, 1 - c], dst_ref=theirs[t], send_sem=send_sems.at[t],
                                             recv_sem=recv_sems.at[t], device_id=(x, y, 1 - c), device_id_type=MESH) for t in range(n)]

    def issue(g, theirs, sems):
        for cp in copies(g, theirs, sems):
            cp.start()

    def drain(g, theirs, sems):
        for cp in copies(g, theirs, sems):
            cp.wait()

    return _Carried(grads, [jax.ShapeDtypeStruct((NCHIP,) + g.shape[2:], g.dtype) for g in grads], (n, n), issue, drain)


def _chip_exchange_copies(items):
    flat = [(t, i, j) for t, (_, peers) in enumerate(items) for i, j in enumerate(peers)]

    def copies(p, slots, sems):
        send_sems, recv_sems = sems
        x, y, c, chips = _place()
        return [pltpu.make_async_remote_copy(src_ref=p[t].at[_chip_id(chips[j])], dst_ref=slots[t].at[i], send_sem=send_sems.at[k],
                                             recv_sem=recv_sems.at[k], device_id=(*chips[j], c), device_id_type=MESH)
                for k, (t, i, j) in enumerate(flat)]

    def issue(p, slots, sems):
        for cp in copies(p, slots, sems):
            cp.start()

    def drain(p, slots, sems):
        for cp in copies(p, slots, sems):
            cp.wait()

    return _Carried([p for p, _ in items], [jax.ShapeDtypeStruct((len(peers),) + p.shape[1:], p.dtype) for p, peers in items],
                    (len(flat), len(flat)), issue, drain)


def _pair_share(full):
    n = len(full)

    def body(*refs):
        buf = refs[n:2 * n]
        send_sems, recv_sems = refs[2 * n:]
        x, y, c, _ = _place()
        sends = [pltpu.make_async_remote_copy(src_ref=buf[t].at[c], dst_ref=buf[t].at[c], send_sem=send_sems.at[t], recv_sem=recv_sems.at[t],
                                              device_id=(x, y, 1 - c), device_id_type=MESH) for t in range(n)]
        for cp in sends:
            cp.start()
        for t in range(n):
            pltpu.make_async_remote_copy(src_ref=buf[t].at[c], dst_ref=buf[t].at[1 - c], send_sem=send_sems.at[t], recv_sem=recv_sems.at[t],
                                         device_id=(x, y, 1 - c), device_id_type=MESH).wait_recv()
        for cp in sends:
            cp.wait_send()

    return _comm_call("grad_pair_share", body, full, [jax.ShapeDtypeStruct(f.shape, f.dtype) for f in full], (n, n),
                      aliases={t: t for t in range(n)})


SMALL_ROWS = 128


def _allreduce_small(v):
    def body(v_ref, o_ref, buf, send_sems, recv_sems):
        x, y, c, _ = _place()
        me = 4 * x + 2 * y + c
        flip = lambda a, f: 1 - a if f else a
        buf[me] = v_ref[...]
        peers = [(flip(x, d >> 2 & 1), flip(y, d >> 1 & 1), flip(c, d & 1)) for d in range(1, 8)]
        sends = [pltpu.make_async_remote_copy(src_ref=v_ref, dst_ref=buf.at[me], send_sem=send_sems.at[i], recv_sem=recv_sems.at[i],
                                              device_id=peer, device_id_type=MESH) for i, peer in enumerate(peers)]
        for cp in sends:
            cp.start()
        for i, (px, py, pc) in enumerate(peers):
            pltpu.make_async_remote_copy(src_ref=v_ref, dst_ref=buf.at[4 * px + 2 * py + pc], send_sem=send_sems.at[i], recv_sem=recv_sems.at[i],
                                         device_id=(px, py, pc), device_id_type=MESH).wait_recv()
        for cp in sends:
            cp.wait_send()
        acc = buf[0]
        for k in range(1, 8):
            acc = acc + buf[k]
        o_ref[...] = acc

    vm = pl.BlockSpec(memory_space=pltpu.VMEM)
    return pl.pallas_call(
        body, in_specs=[vm], out_specs=vm, out_shape=jax.ShapeDtypeStruct((SMALL_ROWS, 128), F32),
        scratch_shapes=[pltpu.VMEM((8, SMALL_ROWS, 128), F32), pltpu.SemaphoreType.DMA((7,)), pltpu.SemaphoreType.DMA((7,))],
        compiler_params=pltpu.CompilerParams(has_side_effects=True), name="allreduce_small")(v)


def _row_block(rows, cols, budget=3 << 19):
    best = 8
    for bm in range(8, rows + 1, 8):
        if rows % bm == 0 and bm * cols * 4 <= budget:
            best = bm
    return best


def _pair_sum(name, place, g, theirs):
    _, m, c = theirs.shape
    bm = _row_block(m, c)

    def body(place_ref, a_ref, b_ref, o_ref):
        o_ref[...] = (a_ref[...].astype(F32) + b_ref[...].astype(F32)).astype(o_ref.dtype)

    spec = pl.BlockSpec((None, bm, c), lambda k, i, pr: (k, i, 0))
    return pl.pallas_call(
        body, out_shape=jax.ShapeDtypeStruct(theirs.shape, BF16),
        grid_spec=pltpu.PrefetchScalarGridSpec(
            num_scalar_prefetch=1, grid=(NCHIP, m // bm),
            in_specs=[pl.BlockSpec((None, None, bm, c), lambda k, i, pr: (k, pr[0], i, 0)), spec], out_specs=spec),
        compiler_params=_params(("parallel", "parallel")), name=name)(place, g, theirs)


def _chip_sum(name, place, parts, slots):
    _, m, c = parts.shape
    bm = _row_block(m, c)

    def body(place_ref, p_ref, *refs):
        acc = p_ref[...].astype(F32)
        for s_ref in refs[:-1]:
            for i in range(s_ref.shape[0]):
                acc = acc + s_ref[i].astype(F32)
        refs[-1][...] = acc

    return pl.pallas_call(
        body, out_shape=jax.ShapeDtypeStruct((2, m, c), F32),
        grid_spec=pltpu.PrefetchScalarGridSpec(
            num_scalar_prefetch=1, grid=(m // bm,),
            in_specs=[pl.BlockSpec((None, bm, c), lambda i, pr: (pr[1], i, 0))]
            + [pl.BlockSpec((s.shape[0], bm, c), lambda i, pr: (0, i, 0)) for s in slots],
            out_specs=pl.BlockSpec((None, bm, c), lambda i, pr: (pr[0], i, 0))),
        compiler_params=_params(("parallel",)), name=name)(place, parts, *slots)


def _adamw(name, w, g, m, v, layer=0, into=None):
    lead, rows, cols = w.shape
    bm = _row_block(rows, cols, budget=384 * 1024)
    c1 = 1.0 - ADAM_B1 ** ADAM_STEP
    c2 = 1.0 - ADAM_B2 ** ADAM_STEP

    def body(w_ref, g_ref, m_ref, v_ref, *rest):
        go_ref, d_ref, mo_ref, vo_ref = rest[-4:]
        g = g_ref[...]
        mn = ADAM_B1 * m_ref[...] + (1.0 - ADAM_B1) * g
        vn = ADAM_B2 * v_ref[...] + (1.0 - ADAM_B2) * (g * g)
        go_ref[...] = g
        mo_ref[...] = mn
        vo_ref[...] = vn
        d_ref[...] = -ADAM_LR * ((mn / c1) / (jnp.sqrt(vn / c2) + ADAM_EPS) + ADAM_WD * w_ref[...])

    spec = pl.BlockSpec((None, bm, cols), lambda i: (layer, i, 0))
    sh = jax.ShapeDtypeStruct((lead, rows, cols), F32)
    prev = [] if into is None else list(into)
    return pl.pallas_call(
        body, grid=(rows // bm,), in_specs=[spec, pl.BlockSpec((bm, cols), lambda i: (i, 0)), spec, spec] + [pl.BlockSpec(memory_space=pl.ANY)] * len(prev),
        out_specs=[spec] * 4, out_shape=[sh] * 4, input_output_aliases={4 + k: k for k in range(len(prev))},
        compiler_params=_params(("parallel",)), name=name)(w, g, m, v, *prev)


def _pack_small(norms, rpb):
    flat = jnp.concatenate([a.reshape(-1) for a in norms] + [rpb.reshape(-1)])
    return jnp.pad(flat, (0, SMALL_ROWS * 128 - flat.shape[0])).reshape(SMALL_ROWS, 128)


def _unpack_small(p):
    flat = p.reshape(-1)
    norms = [flat[i * 2 * DM:(i + 1) * 2 * DM].reshape(2, DM) for i in range(4)]
    rpb = flat[8 * DM:8 * DM + NH * 15 * 31].reshape(1, NH, 15, 31)
    return norms, rpb


FFN_NAMES = ("ffn_w_gate", "ffn_w_up", "ffn_w_down")
L0_FFN = tuple((n, 0) for n in FFN_NAMES)
L1_FFN = tuple((n, 1) for n in FFN_NAMES)
NA_KEYS = (("na_w_qkv", 0), ("na_w_o", 0))
DIL_KEYS = (("dil_w_qkv", 0), ("dil_w_o", 0))
ALL_PEERS, NEIGHBOURS, DIAGONAL = (0, 1, 2), (0, 1), (2,)


class _Exchange:
    GATHERS = {"na_bias_tiles": NA_KEYS, "l0_qkv": L0_FFN[:1], "na_fwd": L0_FFN[1:], "l0_ffn_fwd": DIL_KEYS[:1], "dil_fwd": L1_FFN + DIL_KEYS[1:]}
    PAIRS = {"l1_proj_bwd": L1_FFN, "l1_dh": DIL_KEYS, "l0_proj_bwd": L0_FFN}
    EXCHANGES = {"dil_bwd": [(k, ALL_PEERS) for k in L1_FFN],
                 "l0_ffn_bwd": [(DIL_KEYS[0], NEIGHBOURS)],
                 "na_bwd": [(k, ALL_PEERS) for k in L0_FFN] + [(DIL_KEYS[0], DIAGONAL), (DIL_KEYS[1], ALL_PEERS)],
                 "l0_dh": [(k, ALL_PEERS) for k in NA_KEYS]}

    def __init__(self, shards):
        self.chip = 2 * lax.axis_index("x") + lax.axis_index("y")
        self.place = jnp.stack([lax.axis_index("c"), self.chip]).astype(jnp.int32)
        self.own = {k: s.reshape(2, s.shape[0] // 2, s.shape[1]).astype(BF16) for k, s in shards.items()}
        self.gathered, self.mine, self.parts, self.slots, self.full = {}, {}, {}, {}, {}

    def _take(self, keys, landed):
        for k, gw in zip(keys, landed):
            self.gathered[k] = lax.dynamic_update_slice(gw, self.own[k][None], (self.chip, 0, 0, 0))

    def _sum(self, items, landed):
        for (k, peers), s in zip(items, landed):
            got = self.slots.setdefault(k, {})
            got[peers] = s
            if sum(len(p) for p in got) == len(ALL_PEERS):
                self.full[k] = _chip_sum(f"chip_sum_{k[0]}_{k[1]}", self.place, self.parts[k], [got[p] for p in sorted(got)])

    def weight(self, key):
        g = self.gathered[key]
        return g.reshape(NCHIP, 2 * g.shape[2], g.shape[3])

    def _pair_sums(self, keys, theirs):
        for k, t in zip(keys, theirs):
            self.parts[k] = _pair_sum(f"pair_sum_{k[0]}_{k[1]}", self.place, self.mine[k], t)

    def carry(self, tag):
        if tag in self.GATHERS:
            return _gather_copies([self.own[k] for k in self.GATHERS[tag]])
        if tag in self.PAIRS:
            return _pair_exchange_copies([self.mine[k] for k in self.PAIRS[tag]])
        if tag in self.EXCHANGES:
            return _chip_exchange_copies([(self.parts[k], peers) for k, peers in self.EXCHANGES[tag]])
        return None

    def carried(self, tag, landed):
        if tag in self.GATHERS:
            self._take(self.GATHERS[tag], landed)
        elif tag in self.PAIRS:
            self._pair_sums(self.PAIRS[tag], landed)
        elif tag in self.EXCHANGES:
            self._sum(self.EXCHANGES[tag], landed)

    def grads(self, tag, dw):
        for k, g in dw.items():
            self.mine[k] = g.reshape(NCHIP, 2, -1, g.shape[-1])
        if tag == "l0_mix":
            keys = tuple(dw)
            self._pair_sums(keys, _run_carried("grad_pair_exchange_last", _pair_exchange_copies([self.mine[k] for k in keys])))

    def finish(self):
        keys = tuple(self.full)
        shared = _pair_share([self.full[k] for k in keys])
        return {k: s.reshape(2 * s.shape[1], s.shape[2]) for k, s in zip(keys, shared)}


def kernel(x, norm_mix_pre, norm_mix_post, norm_ffn_pre, norm_ffn_post, na_w_qkv, na_w_o, na_rpb, dil_w_qkv, dil_w_o, ffn_w_gate, ffn_w_up, ffn_w_down, loss_target, m_norm_mix_pre, m_norm_mix_post, m_norm_ffn_pre, m_norm_ffn_post, m_na_w_qkv, m_na_w_o, m_na_rpb, m_dil_w_qkv, m_dil_w_o, m_ffn_w_gate, m_ffn_w_up, m_ffn_w_down, v_norm_mix_pre, v_norm_mix_post, v_norm_ffn_pre, v_norm_ffn_post, v_na_w_qkv, v_na_w_o, v_na_rpb, v_dil_w_qkv, v_dil_w_o, v_ffn_w_gate, v_ffn_w_up, v_ffn_w_down):
    tr = lambda a: jnp.swapaxes(a, 1, 2)
    weights = {"na_w_qkv": na_w_qkv, "na_w_o": na_w_o, "dil_w_qkv": dil_w_qkv, "dil_w_o": dil_w_o,
               "ffn_w_gate": tr(ffn_w_gate), "ffn_w_up": tr(ffn_w_up), "ffn_w_down": ffn_w_down}
    m_in = {"na_w_qkv": m_na_w_qkv, "na_w_o": m_na_w_o, "dil_w_qkv": m_dil_w_qkv, "dil_w_o": m_dil_w_o,
            "ffn_w_gate": tr(m_ffn_w_gate), "ffn_w_up": tr(m_ffn_w_up), "ffn_w_down": m_ffn_w_down}
    v_in = {"na_w_qkv": v_na_w_qkv, "na_w_o": v_na_w_o, "dil_w_qkv": v_dil_w_qkv, "dil_w_o": v_dil_w_o,
            "ffn_w_gate": tr(v_ffn_w_gate), "ffn_w_up": tr(v_ffn_w_up), "ffn_w_down": v_ffn_w_down}

    ex = _Exchange({(n, l): weights[n][l] for n in weights for l in range(weights[n].shape[0])})
    norms = (norm_mix_pre, norm_mix_post, norm_ffn_pre, norm_ffn_post)
    loss_row, dx, dnorms, d_rpb = _local_step(x[0], loss_target[0], norms, na_rpb[0], ex)
    loss = lax.psum(loss_row[0, 0], ("x", "y", "c"))
    full = ex.finish()
    small = _allreduce_small(_pack_small(dnorms, d_rpb))

    out_g, out_d, out_m, out_v = {}, {}, {}, {}
    for n in weights:
        res = None
        for l in range(weights[n].shape[0]):
            res = _adamw(f"adamw_{n}_{l}", weights[n], full[(n, l)], m_in[n], v_in[n], l, res)
        if n in ("ffn_w_gate", "ffn_w_up"):
            res = [tr(r) for r in res]
        out_g[n], out_d[n], out_m[n], out_v[n] = res
    sm_names = ("norm_mix_pre", "norm_mix_post", "norm_ffn_pre", "norm_ffn_post", "na_rpb")
    sm = _adamw("adamw_small", _pack_small(norms, na_rpb)[None], small,
                _pack_small((m_norm_mix_pre, m_norm_mix_post, m_norm_ffn_pre, m_norm_ffn_post), m_na_rpb)[None],
                _pack_small((v_norm_mix_pre, v_norm_mix_post, v_norm_ffn_pre, v_norm_ffn_post), v_na_rpb)[None])
    for res, dst in zip(sm, (out_g, out_d, out_m, out_v)):
        ns, rp = _unpack_small(res)
        for n, a in zip(sm_names, ns + [rp]):
            dst[n] = a

    order = ("norm_mix_pre", "norm_mix_post", "norm_ffn_pre", "norm_ffn_post", "na_w_qkv", "na_w_o", "na_rpb", "dil_w_qkv", "dil_w_o",
             "ffn_w_gate", "ffn_w_up", "ffn_w_down")
    return (loss, dx[None], *[out_g[n] for n in order], *[out_d[n] for n in order], *[out_m[n] for n in order], *[out_v[n] for n in order])
```

```python
import functools

import numpy as np
import jax
import jax.numpy as jnp
from jax import lax
from jax.experimental import pallas as pl
from jax.experimental.pallas import tpu as pltpu

F32 = jnp.float32
BF16 = jnp.bfloat16

SEQ = 2048
DM = 1024
NH = 16
HD = 64
DFF = 2816
NCHIP = 4
FSH = DFF // NCHIP
GRID_W = 64
NA_QROWS = 4
NA_QB = NA_QROWS * GRID_W
NA_WROWS = 12
NA_WIN = NA_WROWS * GRID_W
DIL = (1, 4, 16)
DIL_QB = 256
DIL_WIN = DIL_QB + 128
DIL_RADIUS = 64
RMS_EPS = 1e-6
NEG = -1e30
QSCALE = HD ** -0.5
CH = 256
MESH = pl.DeviceIdType.MESH

ADAM_LR, ADAM_B1, ADAM_B2, ADAM_EPS, ADAM_WD, ADAM_STEP = 0.001, 0.9, 0.999, 1e-08, 0.01, 10

VMEM_LIMIT = 56 * 1024 * 1024

_NN = (((1,), (0,)), ((), ()))
_NT = (((1,), (1,)), ((), ()))
_TN = (((0,), (0,)), ((), ()))


def _params(sem):
    return pltpu.CompilerParams(dimension_semantics=sem, vmem_limit_bytes=VMEM_LIMIT)


def _matmul(name, pairs, grid, out_shape, out_spec, acc_shape, carrying=False, carry=None):
    nk = grid[-1]
    npair = len(pairs)
    n_in = 2 * npair

    def body(*refs):
        ins, o_ref = refs[:2 * npair], refs[n_in]
        part = None
        for p in range(npair):
            d = lax.dot_general(ins[2 * p][...].astype(BF16), ins[2 * p + 1][...].astype(BF16), pairs[p][4],
                                preferred_element_type=F32)
            part = d if part is None else part + d
        if nk == 1:
            o_ref[...] = part.astype(o_ref.dtype)
        else:
            acc_ref = refs[n_in + 1]
            kk = pl.program_id(len(grid) - 1)

            @pl.when(kk == 0)
            def _():
                acc_ref[...] = part

            @pl.when(kk > 0)
            def _():
                acc_ref[...] += part

            @pl.when(kk == nk - 1)
            def _():
                o_ref[...] = acc_ref[...].astype(o_ref.dtype)

    ops, specs = [], []
    for a, a_spec, b, b_spec, _ in pairs:
        ops += [a, b]
        specs += [a_spec, b_spec]
    (out,), sent = _carrier_call(name, body, grid, specs, [out_spec], [out_shape], [] if nk == 1 else [pltpu.VMEM(acc_shape, F32)], ops, carry)
    return (out, sent) if carrying else out


def _qkv_fwd(name, h_all, w4, carry):
    g_n = h_all.shape[0]
    per = w4.shape[2] // CH
    return _matmul(
        name, [(h_all, pl.BlockSpec((None, SEQ, DM), lambda g, q, k: (g, 0, 0)),
                w4, pl.BlockSpec((None, DM, CH), lambda g, q, k: ((g * 12 + q) // per, 0, (g * 12 + q) % per)), _NN)],
        (g_n, 12, 1), jax.ShapeDtypeStruct((g_n, SEQ, 3 * DM), BF16),
        pl.BlockSpec((None, SEQ, CH), lambda g, q, k: (g, 0, q)), None, carrying=True, carry=carry)


def _qkv_bwd_dh(name, dqkv, w4, carry):
    g_n = dqkv.shape[0]
    per = w4.shape[2] // CH
    tm = SEQ

    def pair(cb):
        chunk = lambda g, t: g * 12 + t * 4 + cb
        return (dqkv, pl.BlockSpec((None, None, tm, CH), lambda g, i, t: (g, t, i, cb)),
                w4, pl.BlockSpec((None, DM, CH), lambda g, i, t: (chunk(g, t) // per, 0, chunk(g, t) % per)), _NT)

    return _matmul(name, [pair(cb) for cb in range(4)], (g_n, SEQ // tm, 3), jax.ShapeDtypeStruct((g_n, SEQ, DM), F32),
                   pl.BlockSpec((None, tm, DM), lambda g, i, t: (g, i, 0)), (tm, DM), carrying=True, carry=carry)


def _qkv_bwd_dw(name, ht_all, dqkv, shard_cols):
    g_n = dqkv.shape[0]
    per = shard_cols // CH
    return _matmul(
        name, [(ht_all, pl.BlockSpec((None, DM, SEQ), lambda qq, k: (qq // 12, 0, 0)),
                dqkv, pl.BlockSpec((None, None, SEQ, CH), lambda qq, k: (qq // 12, (qq % 12) // 4, 0, qq % 4)), _NN)],
        (g_n * 12, 1), jax.ShapeDtypeStruct((NCHIP, DM, shard_cols), BF16),
        pl.BlockSpec((None, DM, CH), lambda qq, k: (qq // per, 0, qq % per)), None)


def _proj_fwd(name, o, wo, x, g):
    tm = 512

    def body(o_ref, w_ref, x_ref, g_ref, xn_ref, u_ref):
        u = jnp.dot(o_ref[...], w_ref[...], preferred_element_type=F32)
        u_ref[...] = u
        r = lax.rsqrt(jnp.mean(u * u, axis=-1, keepdims=True) + RMS_EPS)
        xn_ref[...] = x_ref[...] + u * r * g_ref[...]

    rows = pl.BlockSpec((tm, DM), lambda i: (i, 0))
    sh = jax.ShapeDtypeStruct((SEQ, DM), F32)
    return pl.pallas_call(
        body, grid=(SEQ // tm,), in_specs=[rows, pl.BlockSpec((DM, DM), lambda i: (0, 0)), rows, pl.BlockSpec((1, DM), lambda i: (0, 0))],
        out_specs=[rows, rows], out_shape=[sh, sh], compiler_params=_params(("parallel",)), name=name)(o, wo, x, g)


def _proj_bwd(name, dy, u, g, wo, dtype, carry):
    tm = 512

    def body(dy_ref, u_ref, g_ref, w_ref, do_ref, du_ref, dg_ref):
        dy = dy_ref[...]
        u = u_ref[...]
        r = lax.rsqrt(jnp.mean(u * u, axis=-1, keepdims=True) + RMS_EPS)
        yh = u * r
        t = dy * g_ref[...]
        du = (r * (t - yh * jnp.mean(t * yh, axis=-1, keepdims=True))).astype(BF16)
        du_ref[...] = du
        do_ref[...] = lax.dot_general(du, w_ref[...], _NT, preferred_element_type=F32).astype(do_ref.dtype)

        @pl.when(pl.program_id(0) == 0)
        def _():
            dg_ref[...] = jnp.zeros_like(dg_ref)

        dg_ref[...] += jnp.sum(dy * yh, axis=0, keepdims=True)

    rows = pl.BlockSpec((tm, DM), lambda i: (i, 0))
    vec = pl.BlockSpec((1, DM), lambda i: (0, 0))
    return _carrier_call(
        name, body, (SEQ // tm,), [rows, rows, vec, pl.BlockSpec((DM, DM), lambda i: (0, 0))], [rows, rows, vec],
        [jax.ShapeDtypeStruct((SEQ, DM), dtype), jax.ShapeDtypeStruct((SEQ, DM), BF16), jax.ShapeDtypeStruct((1, DM), F32)],
        [], (dy, u, g, wo), carry)


def _proj_bwd_dw(name, o, du):
    tn = 512
    return _matmul(
        name, [(o, pl.BlockSpec((SEQ, DM), lambda j, k: (0, 0)), du, pl.BlockSpec((SEQ, tn), lambda j, k: (0, j)), _TN)],
        (DM // tn, 1), jax.ShapeDtypeStruct((DM, DM), BF16), pl.BlockSpec((DM, tn), lambda j, k: (0, j)), None)


def _ffn_wspec(index_map):
    return pl.BlockSpec((None, FSH, DM), index_map)


def _ffn_bwd_dw(name, a4, b):
    return _matmul(
        name, [(a4, pl.BlockSpec((None, SEQ, FSH), lambda s, k: (s, 0, 0)), b, pl.BlockSpec((SEQ, DM), lambda s, k: (0, 0)), _TN)],
        (NCHIP, 1), jax.ShapeDtypeStruct((NCHIP, FSH, DM), BF16), _ffn_wspec(lambda s, k: (s, 0, 0)), None)


ROWS = 256


def _row_spec():
    return pl.BlockSpec((ROWS, DM), lambda i: (i, 0))


def _vec_spec():
    return pl.BlockSpec((1, DM), lambda i: (0, 0))


def _rms_fwd(name, x, g, dtype=BF16):
    def body(x_ref, g_ref, o_ref):
        x = x_ref[...]
        r = lax.rsqrt(jnp.mean(x * x, axis=-1, keepdims=True) + RMS_EPS)
        o_ref[...] = (x * r * g_ref[...]).astype(o_ref.dtype)

    return pl.pallas_call(body, grid=(SEQ // ROWS,), in_specs=[_row_spec(), _vec_spec()], out_specs=_row_spec(),
                          out_shape=jax.ShapeDtypeStruct((SEQ, DM), dtype), compiler_params=_params(("parallel",)), name=name)(x, g)


def _rms_fwd_both(name, x, g):
    def body(x_ref, g_ref, o_ref, t_ref):
        x = x_ref[...]
        r = lax.rsqrt(jnp.mean(x * x, axis=-1, keepdims=True) + RMS_EPS)
        h = x * r * g_ref[...]
        o_ref[...] = h.astype(o_ref.dtype)
        t_ref[...] = h.T.astype(t_ref.dtype)

    return pl.pallas_call(
        body, grid=(SEQ // ROWS,), in_specs=[_row_spec(), _vec_spec()], out_specs=[_row_spec(), pl.BlockSpec((DM, ROWS), lambda i: (0, i))],
        out_shape=[jax.ShapeDtypeStruct((SEQ, DM), BF16), jax.ShapeDtypeStruct((DM, SEQ), BF16)],
        compiler_params=_params(("parallel",)), name=name)(x, g)


def _norm_bwd(name, dy, u, g, res, carry):
    def body(dy_ref, u_ref, g_ref, res_ref, du_ref, dg_ref):
        dy = dy_ref[...]
        u = u_ref[...]
        r = lax.rsqrt(jnp.mean(u * u, axis=-1, keepdims=True) + RMS_EPS)
        yh = u * r
        t = dy * g_ref[...]
        du_ref[...] = r * (t - yh * jnp.mean(t * yh, axis=-1, keepdims=True)) + res_ref[...]

        @pl.when(pl.program_id(0) == 0)
        def _():
            dg_ref[...] = jnp.zeros_like(dg_ref)

        dg_ref[...] += jnp.sum(dy * yh, axis=0, keepdims=True)

    return _carrier_call(
        name, body, (SEQ // ROWS,), [_row_spec(), _row_spec(), _vec_spec(), _row_spec()], [_row_spec(), _vec_spec()],
        [jax.ShapeDtypeStruct((SEQ, DM), F32), jax.ShapeDtypeStruct((1, DM), F32)], [], (dy, u, g, res), carry)


def _loss_grad(name, y, t):
    def body(y_ref, t_ref, dy_ref, l_ref):
        e = y_ref[...] - t_ref[...]
        dy_ref[...] = e * (1.0 / DM)

        @pl.when(pl.program_id(0) == 0)
        def _():
            l_ref[...] = jnp.zeros_like(l_ref)

        l_ref[...] += jnp.sum(e * e) * (0.5 / DM)

    return pl.pallas_call(
        body, grid=(SEQ // ROWS,), in_specs=[_row_spec(), _row_spec()],
        out_specs=[_row_spec(), pl.BlockSpec((1, 128), lambda i: (0, 0))],
        out_shape=[jax.ShapeDtypeStruct((SEQ, DM), F32), jax.ShapeDtypeStruct((1, 128), F32)],
        compiler_params=_params(("arbitrary",)), name=name)(y, t)


HBM_SPEC = pl.BlockSpec(memory_space=pltpu.HBM)


class _Carried:
    def __init__(self, ins, out_shapes, n_sems, issue, drain):
        self.ins, self.out_shapes, self.n_sems, self.issue, self.drain = list(ins), list(out_shapes), tuple(n_sems), issue, drain


def _carrier_call(name, body, grid, in_specs, out_specs, out_shape, scratch_shapes, operands, carry):
    n_in, n_out, n_scr = len(in_specs), len(out_specs), len(scratch_shapes)
    if carry is None:
        res = pl.pallas_call(body, grid=grid, in_specs=in_specs, out_specs=out_specs, out_shape=out_shape, scratch_shapes=scratch_shapes,
                             compiler_params=_params(("arbitrary",) * len(grid)), name=name)(*operands)
        return list(res), []
    ci, co = len(carry.ins), len(carry.out_shapes)

    def wrapped(*refs):
        ins, cins = refs[:n_in], refs[n_in:n_in + ci]
        outs, couts = refs[n_in + ci:n_in + ci + n_out], refs[n_in + ci + n_out:n_in + ci + n_out + co]
        scr, sems = refs[n_in + ci + n_out + co:n_in + ci + n_out + co + n_scr], refs[n_in + ci + n_out + co + n_scr:]
        first = functools.reduce(jnp.logical_and, [pl.program_id(a) == 0 for a in range(len(grid))])
        last = functools.reduce(jnp.logical_and, [pl.program_id(a) == grid[a] - 1 for a in range(len(grid))])

        @pl.when(first)
        def _():
            carry.issue(cins, couts, sems)

        body(*ins, *outs, *scr)

        @pl.when(last)
        def _():
            carry.drain(cins, couts, sems)

    res = pl.pallas_call(
        wrapped, grid=grid, in_specs=list(in_specs) + [HBM_SPEC] * ci, out_specs=list(out_specs) + [HBM_SPEC] * co,
        out_shape=list(out_shape) + carry.out_shapes,
        scratch_shapes=list(scratch_shapes) + [pltpu.SemaphoreType.DMA((k,)) for k in carry.n_sems],
        compiler_params=pltpu.CompilerParams(dimension_semantics=("arbitrary",) * len(grid), vmem_limit_bytes=VMEM_LIMIT, has_side_effects=True),
        name=name)(*operands, *carry.ins)
    return list(res[:n_out]), list(res[n_out:])


def _run_carried(name, carry):
    def body(*refs):
        ci, co = len(carry.ins), len(carry.out_shapes)
        carry.issue(refs[:ci], refs[ci:ci + co], refs[ci + co:])
        carry.drain(refs[:ci], refs[ci:ci + co], refs[ci + co:])

    return pl.pallas_call(
        body, in_specs=[HBM_SPEC] * len(carry.ins), out_specs=[HBM_SPEC] * len(carry.out_shapes), out_shape=carry.out_shapes,
        scratch_shapes=[pltpu.SemaphoreType.DMA((k,)) for k in carry.n_sems],
        compiler_params=pltpu.CompilerParams(has_side_effects=True), name=name)(*carry.ins)


NA_BLOCKS = SEQ // NA_QB
NA_ROWS_TOTAL = SEQ // GRID_W
NA_CLASSES = ((0, 0), (8, 4), (NA_ROWS_TOTAL - NA_QROWS, NA_ROWS_TOTAL - NA_WROWS))


def _na_pairs(i0, ws):
    out = []
    for qi in range(NA_QROWS):
        i = i0 + qi
        rs = min(max(i - 4, 0), NA_ROWS_TOTAL - 8)
        for kr in range(NA_WROWS):
            r = ws + kr
            if rs <= r < rs + 8:
                out.append((qi, kr, r - i + 7))
    return out


def _diag_onehot():
    qc, kc = np.meshgrid(np.arange(GRID_W), np.arange(GRID_W), indexing="ij")
    e = np.zeros((GRID_W * GRID_W, 128), np.float32)
    j = (kc - qc + 15).reshape(-1)
    ok = (j >= 0) & (j <= 30)
    e[np.arange(GRID_W * GRID_W)[ok], j[ok]] = 1.0
    return jnp.asarray(e)


def _rpb_expand(rpb):
    r2 = jnp.pad(rpb.reshape(NH * 15, 31), ((0, 0), (0, 128 - 31)))

    def body(r_ref, e_ref, o_ref):
        o_ref[...] = lax.dot_general(r_ref[...], e_ref[...], _NT, preferred_element_type=F32, precision=lax.Precision.HIGHEST)

    out = pl.pallas_call(body, out_shape=jax.ShapeDtypeStruct((NH * 15, GRID_W * GRID_W), F32), name="rpb_expand",
                         compiler_params=pltpu.CompilerParams(vmem_limit_bytes=VMEM_LIMIT))(r2, _diag_onehot())
    return out.reshape(NH, 15, GRID_W, GRID_W)


def _na_bias_tiles(rpb, carry):
    def body(b_ref, o_ref):
        qc = lax.broadcasted_iota(jnp.int32, (GRID_W, GRID_W), 0)
        kc = lax.broadcasted_iota(jnp.int32, (GRID_W, GRID_W), 1)
        first = jnp.clip(qc - 8, 0, GRID_W - 16)
        in_window = (kc >= first) & (kc < first + 16)
        neg = jnp.full((GRID_W, GRID_W), NEG, F32)
        for cls, (i0, ws) in enumerate(NA_CLASSES):
            @pl.when(pl.program_id(0) == cls)
            def _(i0=i0, ws=ws):
                pairs = {(qi, kr): dr for qi, kr, dr in _na_pairs(i0, ws)}
                masked = {dr: jnp.where(in_window, b_ref[dr], NEG) for dr in sorted(set(pairs.values()))}
                for qi in range(NA_QROWS):
                    for k2 in range(NA_WROWS // 2):
                        blocks = [masked[pairs[(qi, kr)]] if (qi, kr) in pairs else neg for kr in (2 * k2, 2 * k2 + 1)]
                        o_ref[qi * GRID_W:(qi + 1) * GRID_W, k2 * 128:(k2 + 1) * 128] = jnp.concatenate(blocks, axis=1)

    (tiles,), sent = _carrier_call(
        "na_bias_tiles", body, (3, NH), [pl.BlockSpec((None, 15, GRID_W, GRID_W), lambda c, h: (h, 0, 0, 0))],
        [pl.BlockSpec((None, None, NA_QB, NA_WIN), lambda c, h: (c, h, 0, 0))], [jax.ShapeDtypeStruct((3, NH, NA_QB, NA_WIN), F32)],
        [], (_rpb_expand(rpb),), carry)
    return tiles, sent


def _na_cls(b):
    return jnp.where(b == 0, 0, jnp.where(b == NA_BLOCKS - 1, 2, 1))


def _na_start(b):
    return pl.multiple_of(jnp.clip(b * NA_QROWS - 4, 0, NA_ROWS_TOTAL - NA_WROWS) * GRID_W, GRID_W)


HPS = 4
LW = HPS * HD
NLW = DM // LW


NA_BWD_HPS = 4


def _na_in_specs(hps=HPS):
    lw = hps * HD
    nlw = DM // lw
    return [pl.BlockSpec((NA_QB, lw), lambda hp, b: (b, hp)),
            pl.BlockSpec((SEQ, lw), lambda hp, b: (0, nlw + hp)),
            pl.BlockSpec((SEQ, lw), lambda hp, b: (0, 2 * nlw + hp)),
            pl.BlockSpec((None, hps, NA_QB, NA_WIN), lambda hp, b: (_na_cls(b), hp, 0, 0))]


def _na_fwd(qkv, bias, carry):
    def body(q_ref, k_ref, v_ref, b_ref, o_ref):
        start = _na_start(pl.program_id(1))
        q = q_ref[...]
        kw = k_ref[pl.ds(start, NA_WIN), :]
        vw = v_ref[pl.ds(start, NA_WIN), :]
        outs = []
        for hh in range(HPS):
            sl = slice(hh * HD, (hh + 1) * HD)
            s = lax.dot_general(q[:, sl] * QSCALE, kw[:, sl], _NT, preferred_element_type=F32) + b_ref[hh]
            p = jnp.exp(s - jnp.max(s, axis=-1, keepdims=True))
            l = jnp.sum(p, axis=-1, keepdims=True)
            outs.append(jnp.dot(p.astype(BF16), vw[:, sl], preferred_element_type=F32) / l)
        o_ref[...] = jnp.concatenate(outs, axis=1).astype(o_ref.dtype)

    (o,), sent = _carrier_call(
        "na_fwd", body, (NLW, NA_BLOCKS), _na_in_specs(), [pl.BlockSpec((NA_QB, LW), lambda hp, b: (b, hp))],
        [jax.ShapeDtypeStruct((SEQ, DM), BF16)], [], (qkv, qkv, qkv, bias), carry)
    return o, sent


def _na_bwd(qkv, bias, do, carry):
    lw = NA_BWD_HPS * HD

    def body(q_ref, k_ref, v_ref, b_ref, do_ref, dqkv_ref, z_ref, dk_acc, dv_acc):
        blk = pl.program_id(1)

        @pl.when(blk == 0)
        def _():
            dk_acc[...] = jnp.zeros_like(dk_acc)
            dv_acc[...] = jnp.zeros_like(dv_acc)
            z_ref[...] = jnp.zeros_like(z_ref)

        start = _na_start(blk)
        q = q_ref[...]
        do = do_ref[...]
        kw = k_ref[pl.ds(start, NA_WIN), :]
        vw = v_ref[pl.ds(start, NA_WIN), :]
        dqs, dks, dvs, dss = [], [], [], []
        for hh in range(NA_BWD_HPS):
            sl = slice(hh * HD, (hh + 1) * HD)
            qh = q[:, sl] * QSCALE
            s = lax.dot_general(qh, kw[:, sl], _NT, preferred_element_type=F32) + b_ref[hh]
            p = jnp.exp(s - jnp.max(s, axis=-1, keepdims=True))
            p = p / jnp.sum(p, axis=-1, keepdims=True)
            dp = lax.dot_general(do[:, sl], vw[:, sl], _NT, preferred_element_type=F32)
            ds = p * (dp - jnp.sum(p * dp, axis=-1, keepdims=True))
            dsb = ds.astype(BF16)
            dqs.append(jnp.dot(dsb, kw[:, sl], preferred_element_type=F32) * QSCALE)
            dks.append(lax.dot_general(dsb, qh, _TN, preferred_element_type=F32))
            dvs.append(lax.dot_general(p.astype(BF16), do[:, sl], _TN, preferred_element_type=F32))
            dss.append(ds)
        for cls, (i0, ws) in enumerate(NA_CLASSES):
            @pl.when(_na_cls(blk) == cls)
            def _(i0=i0, ws=ws):
                for hh, ds in enumerate(dss):
                    for qi, kr, dr in _na_pairs(i0, ws):
                        z_ref[hh, dr * GRID_W:(dr + 1) * GRID_W, :] += ds[qi * GRID_W:(qi + 1) * GRID_W, kr * GRID_W:(kr + 1) * GRID_W]
        dqkv_ref[0, pl.ds(pl.multiple_of(blk * NA_QB, NA_QB), NA_QB), :] = jnp.concatenate(dqs, axis=1).astype(dqkv_ref.dtype)
        dk_acc[pl.ds(start, NA_WIN), :] += jnp.concatenate(dks, axis=1)
        dv_acc[pl.ds(start, NA_WIN), :] += jnp.concatenate(dvs, axis=1)

        @pl.when(blk == NA_BLOCKS - 1)
        def _():
            dqkv_ref[1] = dk_acc[...].astype(dqkv_ref.dtype)
            dqkv_ref[2] = dv_acc[...].astype(dqkv_ref.dtype)

    (dqkv, z), sent = _carrier_call(
        "na_bwd", body, (NH // NA_BWD_HPS, NA_BLOCKS),
        _na_in_specs(NA_BWD_HPS) + [pl.BlockSpec((NA_QB, lw), lambda hp, b: (b, hp))],
        [pl.BlockSpec((3, SEQ, lw), lambda hp, b: (0, 0, hp)), pl.BlockSpec((NA_BWD_HPS, 15 * GRID_W, GRID_W), lambda hp, b: (hp, 0, 0))],
        [jax.ShapeDtypeStruct((3, SEQ, DM), BF16), jax.ShapeDtypeStruct((NH, 15 * GRID_W, GRID_W), F32)],
        [pltpu.VMEM((SEQ, lw), F32), pltpu.VMEM((SEQ, lw), F32)], (qkv, qkv, qkv, bias, do), carry)
    return dqkv, z, sent


def _rpb_grad(z):
    z2 = z.reshape(NH * 15, GRID_W * GRID_W)

    def body(z_ref, e_ref, o_ref):
        o_ref[...] = jnp.dot(z_ref[...], e_ref[...], preferred_element_type=F32, precision=lax.Precision.HIGHEST)

    out = pl.pallas_call(body, out_shape=jax.ShapeDtypeStruct((NH * 15, 128), F32), name="rpb_grad",
                         compiler_params=pltpu.CompilerParams(vmem_limit_bytes=VMEM_LIMIT))(z2, _diag_onehot())
    return out[:, :31].reshape(NH, 15, 31)


DIL_BLOCKS = SEQ // DIL_QB
DIL_HPS = 8
DIL_LW = DIL_HPS * HD
DIL_NLW = DM // DIL_LW


COLS = 128


def _col_spec():
    return pl.BlockSpec((SEQ, COLS), lambda j: (0, j))


def _grp_spec():
    return pl.BlockSpec((3, SEQ, COLS), lambda j: (0, 0, j))


def _store_group_order(dst_ref, src_ref):
    for g, d in enumerate(DIL):
        n = SEQ // d
        for r in range(d):
            dst_ref[g, r * n:(r + 1) * n, :] = src_ref[pl.ds(r, n, stride=d), :].astype(dst_ref.dtype)


def _store_token_order(dst_ref, src_ref, g):
    d = DIL[g]
    n = SEQ // d
    for r in range(d):
        dst_ref[pl.ds(r, n, stride=d), :] = src_ref[g, r * n:(r + 1) * n, :]


def _to_groups(name, a):
    def body(a_ref, o_ref, t_ref):
        _store_group_order(o_ref, a_ref)
        for g in range(3):
            t_ref[g] = o_ref[g].astype(F32).T.astype(t_ref.dtype)

    return pl.pallas_call(
        body, grid=(DM // COLS,), in_specs=[_col_spec()], out_specs=[_grp_spec(), pl.BlockSpec((3, COLS, SEQ), lambda j: (0, j, 0))],
        out_shape=[jax.ShapeDtypeStruct((3, SEQ, DM), BF16), jax.ShapeDtypeStruct((3, DM, SEQ), BF16)],
        compiler_params=_params(("parallel",)), name=name)(a)


def _from_groups_sum(name, a):
    def body(a_ref, o_ref, t1, t2):
        _store_token_order(t1, a_ref, 1)
        _store_token_order(t2, a_ref, 2)
        o_ref[...] = (a_ref[0] + t1[...]) + t2[...]

    return pl.pallas_call(body, grid=(DM // COLS,), in_specs=[_grp_spec()], out_specs=_col_spec(),
                          out_shape=jax.ShapeDtypeStruct((SEQ, DM), F32), scratch_shapes=[pltpu.VMEM((SEQ, COLS), F32)] * 2,
                          compiler_params=_params(("parallel",)), name=name)(a)


def _dil_start(b):
    return pl.multiple_of(jnp.clip(b * DIL_QB - DIL_RADIUS, 0, SEQ - DIL_WIN), DIL_RADIUS)


def _dil_mask(g, b, start):
    shift = 11 - 2 * g
    ii = b * DIL_QB + lax.broadcasted_iota(jnp.int32, (DIL_QB, DIL_WIN), 0)
    jj = start + lax.broadcasted_iota(jnp.int32, (DIL_QB, DIL_WIN), 1)
    dist = jnp.abs(ii - jj)
    valid = (dist <= DIL_RADIUS) & (jnp.right_shift(ii, shift) == jnp.right_shift(jj, shift))
    return valid, dist.astype(F32)


def _dil_in_specs():
    return [pl.BlockSpec(memory_space=pltpu.SMEM),
            pl.BlockSpec((None, DIL_QB, DIL_LW), lambda g, hp, b: (g, b, hp)),
            pl.BlockSpec((None, SEQ, DIL_LW), lambda g, hp, b: (g, 0, DIL_NLW + hp)),
            pl.BlockSpec((None, SEQ, DIL_LW), lambda g, hp, b: (g, 0, 2 * DIL_NLW + hp))]


def _dil_fwd(qkv, slopes, carry):
    def body(sl_ref, q_ref, k_ref, v_ref, o_ref, lse_ref):
        g, hp, b = pl.program_id(0), pl.program_id(1), pl.program_id(2)
        start = _dil_start(b)
        valid, dist = _dil_mask(g, b, start)
        dil = jnp.left_shift(1, 2 * g).astype(F32)
        q = q_ref[...]
        kw = k_ref[pl.ds(start, DIL_WIN), :]
        vw = v_ref[pl.ds(start, DIL_WIN), :]
        outs, lses = [], []
        for hh in range(DIL_HPS):
            sl = slice(hh * HD, (hh + 1) * HD)
            s = lax.dot_general(q[:, sl] * QSCALE, kw[:, sl], _NT, preferred_element_type=F32)
            s = jnp.where(valid, s - (sl_ref[hp * DIL_HPS + hh] * dil) * dist, NEG)
            m = jnp.max(s, axis=-1, keepdims=True)
            p = jnp.exp(s - m)
            l = jnp.sum(p, axis=-1, keepdims=True)
            outs.append(jnp.dot(p.astype(BF16), vw[:, sl], preferred_element_type=F32) / l)
            lses.append(jnp.broadcast_to(m + jnp.log(l), (DIL_QB, HD)))
        o_ref[...] = jnp.concatenate(outs, axis=1)
        lse_ref[...] = jnp.concatenate(lses, axis=1)

    ospec = pl.BlockSpec((None, DIL_QB, DIL_LW), lambda g, hp, b: (g, b, hp))
    sh = jax.ShapeDtypeStruct((3, SEQ, DM), F32)
    (o, lse), sent = _carrier_call("dil_fwd", body, (3, DIL_NLW, DIL_BLOCKS), _dil_in_specs(), [ospec, ospec], [sh, sh], [],
                                   (slopes, qkv, qkv, qkv), carry)
    return o, lse, sent


def _dil_merge(o_all, lse_all):
    def body(o_ref, l_ref, out_ref, lse_ref, o1, o2, l1, l2):
        for g, (ot, lt) in ((1, (o1, l1)), (2, (o2, l2))):
            _store_token_order(ot, o_ref, g)
            _store_token_order(lt, l_ref, g)
        la, lb, lc = l_ref[0], l1[...], l2[...]
        m = jnp.maximum(jnp.maximum(la, lb), lc)
        wa, wb, wc = jnp.exp(la - m), jnp.exp(lb - m), jnp.exp(lc - m)
        sw = (wa + wb) + wc
        out_ref[...] = (((wa * o_ref[0] + wb * o1[...]) + wc * o2[...]) / sw).astype(out_ref.dtype)
        lse_ref[...] = m + jnp.log(sw)

    return pl.pallas_call(
        body, grid=(DM // COLS,), in_specs=[_grp_spec(), _grp_spec()], out_specs=[_col_spec(), _col_spec()],
        out_shape=[jax.ShapeDtypeStruct((SEQ, DM), BF16), jax.ShapeDtypeStruct((SEQ, DM), F32)],
        scratch_shapes=[pltpu.VMEM((SEQ, COLS), F32)] * 4, compiler_params=_params(("parallel",)), name="dil_merge")(o_all, lse_all)


def _dil_bwd_prep(do, o, lse):
    heads = COLS // HD

    def body(do_ref, o_ref, lse_ref, dog_ref, ddr_ref, lser_ref, dd, grp):
        prod = do_ref[...] * o_ref[...].astype(F32)
        dd[...] = jnp.concatenate(
            [jnp.broadcast_to(jnp.sum(prod[:, h * HD:(h + 1) * HD], axis=-1, keepdims=True), (SEQ, HD)) for h in range(heads)], axis=1)
        _store_group_order(dog_ref, do_ref)
        for src, dst in ((dd, ddr_ref), (lse_ref, lser_ref)):
            _store_group_order(grp, src)
            for g in range(3):
                t = grp[g].T
                for h in range(heads):
                    dst[g, h] = t[h * HD:h * HD + 8, :]

    rows = jax.ShapeDtypeStruct((3, NH, 8, SEQ), F32)
    rspec = pl.BlockSpec((3, heads, 8, SEQ), lambda j: (0, j, 0, 0))
    return pl.pallas_call(
        body, grid=(DM // COLS,), in_specs=[_col_spec()] * 3, out_specs=[_grp_spec(), rspec, rspec],
        out_shape=[jax.ShapeDtypeStruct((3, SEQ, DM), BF16), rows, rows],
        scratch_shapes=[pltpu.VMEM((SEQ, COLS), F32), pltpu.VMEM((3, SEQ, COLS), F32)],
        compiler_params=_params(("parallel",)), name="dil_bwd_prep")(do, o, lse)


def _dil_bwd(qkv, do, dd, lse, slopes, carry):
    def body(sl_ref, q_ref, k_ref, v_ref, do_ref, dd_ref, lse_ref, dqkv_ref, dk_acc, dv_acc):
        g, hp, b = pl.program_id(0), pl.program_id(1), pl.program_id(2)

        @pl.when(b == 0)
        def _():
            dk_acc[...] = jnp.zeros_like(dk_acc)
            dv_acc[...] = jnp.zeros_like(dv_acc)

        start = _dil_start(b)
        shift = 11 - 2 * g
        jj = start + lax.broadcasted_iota(jnp.int32, (DIL_WIN, DIL_QB), 0)
        ii = b * DIL_QB + lax.broadcasted_iota(jnp.int32, (DIL_WIN, DIL_QB), 1)
        dist = jnp.abs(ii - jj)
        valid = (dist <= DIL_RADIUS) & (jnp.right_shift(ii, shift) == jnp.right_shift(jj, shift))
        dist = dist.astype(F32)
        dil = jnp.left_shift(1, 2 * g).astype(F32)
        q = q_ref[...]
        do = do_ref[...]
        kw = k_ref[pl.ds(start, DIL_WIN), :]
        vw = v_ref[pl.ds(start, DIL_WIN), :]
        dqs, dks, dvs = [], [], []
        for hh in range(DIL_HPS):
            sl = slice(hh * HD, (hh + 1) * HD)
            qh = q[:, sl] * QSCALE
            st = lax.dot_general(kw[:, sl], qh, _NT, preferred_element_type=F32)
            st = jnp.where(valid, st - (sl_ref[hp * DIL_HPS + hh] * dil) * dist, NEG)
            pt = jnp.exp(st - lse_ref[hh, 0:1, :])
            dpt = lax.dot_general(vw[:, sl], do[:, sl], _NT, preferred_element_type=F32)
            dst = (pt * (dpt - dd_ref[hh, 0:1, :])).astype(BF16)
            dqs.append(lax.dot_general(kw[:, sl], dst, _TN, preferred_element_type=F32).T * QSCALE)
            dks.append(jnp.dot(dst, qh, preferred_element_type=F32))
            dvs.append(jnp.dot(pt.astype(BF16), do[:, sl], preferred_element_type=F32))
        dqkv_ref[0, pl.ds(pl.multiple_of(b * DIL_QB, DIL_QB), DIL_QB), :] = jnp.concatenate(dqs, axis=1).astype(dqkv_ref.dtype)
        dk_acc[pl.ds(start, DIL_WIN), :] += jnp.concatenate(dks, axis=1)
        dv_acc[pl.ds(start, DIL_WIN), :] += jnp.concatenate(dvs, axis=1)

        @pl.when(b == DIL_BLOCKS - 1)
        def _():
            dqkv_ref[1] = dk_acc[...].astype(dqkv_ref.dtype)
            dqkv_ref[2] = dv_acc[...].astype(dqkv_ref.dtype)

    qspec = pl.BlockSpec((None, DIL_QB, DIL_LW), lambda g, hp, b: (g, b, hp))
    rspec = pl.BlockSpec((None, DIL_HPS, 8, DIL_QB), lambda g, hp, b: (g, hp, 0, b))
    (dqkv,), sent = _carrier_call(
        "dil_bwd", body, (3, DIL_NLW, DIL_BLOCKS), _dil_in_specs() + [qspec, rspec, rspec],
        [pl.BlockSpec((None, 3, SEQ, DIL_LW), lambda g, hp, b: (g, 0, 0, hp))], [jax.ShapeDtypeStruct((3, 3, SEQ, DM), BF16)],
        [pltpu.VMEM((SEQ, DIL_LW), F32), pltpu.VMEM((SEQ, DIL_LW), F32)], (slopes, qkv, qkv, qkv, do, dd, lse), carry)
    return dqkv, sent


def _ffn_fwd(name, x, g_pre, g_post, wgt4, wut4, wd4, carry):
    tm = 512

    def body(x_ref, gpre_ref, gpost_ref, wg_ref, wu_ref, wd_ref, xn_ref, h_ref, gate_ref, up_ref, u_ref, acc):
        s = pl.program_id(1)

        @pl.when(s == 0)
        def _():
            x = x_ref[...]
            r = lax.rsqrt(jnp.mean(x * x, axis=-1, keepdims=True) + RMS_EPS)
            h_ref[...] = (x * r * gpre_ref[...]).astype(h_ref.dtype)

        h = h_ref[...]
        gate = lax.dot_general(h, wg_ref[...], _NT, preferred_element_type=F32).astype(BF16)
        up = lax.dot_general(h, wu_ref[...], _NT, preferred_element_type=F32).astype(BF16)
        gate_ref[...] = gate
        up_ref[...] = up
        gf = gate.astype(F32)
        act = (gf * jax.nn.sigmoid(gf) * up.astype(F32)).astype(BF16)
        part = jnp.dot(act, wd_ref[...], preferred_element_type=F32)

        @pl.when(s == 0)
        def _():
            acc[...] = part

        @pl.when(s > 0)
        def _():
            acc[...] += part

        @pl.when(s == NCHIP - 1)
        def _():
            u = acc[...]
            u_ref[...] = u
            r = lax.rsqrt(jnp.mean(u * u, axis=-1, keepdims=True) + RMS_EPS)
            xn_ref[...] = x_ref[...] + u * r * gpost_ref[...]

    rows = pl.BlockSpec((tm, DM), lambda i, s: (i, 0))
    vec = pl.BlockSpec((1, DM), lambda i, s: (0, 0))
    wspec = _ffn_wspec(lambda i, s: (s, 0, 0))
    mid = pl.BlockSpec((None, tm, FSH), lambda i, s: (s, i, 0))
    outs, sent = _carrier_call(
        name, body, (SEQ // tm, NCHIP), [rows, vec, vec, wspec, wspec, wspec], [rows, rows, mid, mid, rows],
        [jax.ShapeDtypeStruct((SEQ, DM), F32), jax.ShapeDtypeStruct((SEQ, DM), BF16), jax.ShapeDtypeStruct((NCHIP, SEQ, FSH), BF16),
         jax.ShapeDtypeStruct((NCHIP, SEQ, FSH), BF16), jax.ShapeDtypeStruct((SEQ, DM), F32)],
        [pltpu.VMEM((tm, DM), F32)], (x, g_pre, g_post, wgt4, wut4, wd4), carry)
    return outs, sent


def _ffn_block(layer, x, g_pre, g_post, ex):
    tag = f"l{layer}_ffn_fwd"
    (x_new, h, gate, up, u), sent = _ffn_fwd(tag, x, g_pre, g_post, ex.weight(("ffn_w_gate", layer)), ex.weight(("ffn_w_up", layer)),
                                             ex.weight(("ffn_w_down", layer)), ex.carry(tag))
    ex.carried(tag, sent)
    return x_new, (x, h, gate, up, u)


def _ffn_bwd(name, dx, x, gate, up, u, g_pre, g_post, wgt4, wut4, wd4, carry):
    tm = 512

    def body(dx_ref, x_ref, gate_ref, up_ref, u_ref, gpre_ref, gpost_ref, wg_ref, wu_ref, wd_ref,
             dxin_ref, du_ref, dgate_ref, dup_ref, act_ref, dgpre_ref, dgpost_ref, dh_acc):
        i, s = pl.program_id(0), pl.program_id(1)

        @pl.when((i == 0) & (s == 0))
        def _():
            dgpre_ref[...] = jnp.zeros_like(dgpre_ref)
            dgpost_ref[...] = jnp.zeros_like(dgpost_ref)

        @pl.when(s == 0)
        def _():
            dy = dx_ref[...]
            uu = u_ref[...]
            r = lax.rsqrt(jnp.mean(uu * uu, axis=-1, keepdims=True) + RMS_EPS)
            yh = uu * r
            t = dy * gpost_ref[...]
            du_ref[...] = (r * (t - yh * jnp.mean(t * yh, axis=-1, keepdims=True))).astype(du_ref.dtype)
            dgpost_ref[...] += jnp.sum(dy * yh, axis=0, keepdims=True)

        dact = lax.dot_general(du_ref[...], wd_ref[...], _NT, preferred_element_type=F32)
        g = gate_ref[...].astype(F32)
        upv = up_ref[...].astype(F32)
        sg = jax.nn.sigmoid(g)
        dgate = (dact * upv * sg * (1.0 + g * (1.0 - sg))).astype(BF16)
        dup = (dact * g * sg).astype(BF16)
        dgate_ref[...] = dgate
        dup_ref[...] = dup
        act_ref[...] = (g * sg * upv).astype(act_ref.dtype)
        part = jnp.dot(dgate, wg_ref[...], preferred_element_type=F32) + jnp.dot(dup, wu_ref[...], preferred_element_type=F32)

        @pl.when(s == 0)
        def _():
            dh_acc[...] = part

        @pl.when(s > 0)
        def _():
            dh_acc[...] += part

        @pl.when(s == NCHIP - 1)
        def _():
            dh = dh_acc[...]
            xx = x_ref[...]
            r = lax.rsqrt(jnp.mean(xx * xx, axis=-1, keepdims=True) + RMS_EPS)
            yh = xx * r
            t = dh * gpre_ref[...]
            dxin_ref[...] = dx_ref[...] + r * (t - yh * jnp.mean(t * yh, axis=-1, keepdims=True))
            dgpre_ref[...] += jnp.sum(dh * yh, axis=0, keepdims=True)

    rows = pl.BlockSpec((tm, DM), lambda i, s: (i, 0))
    vec = pl.BlockSpec((1, DM), lambda i, s: (0, 0))
    wspec = _ffn_wspec(lambda i, s: (s, 0, 0))
    mid = pl.BlockSpec((None, tm, FSH), lambda i, s: (s, i, 0))
    mid_shape = jax.ShapeDtypeStruct((NCHIP, SEQ, FSH), BF16)
    return _carrier_call(
        name, body, (SEQ // tm, NCHIP), [rows, rows, mid, mid, rows, vec, vec, wspec, wspec, wspec], [rows, rows, mid, mid, mid, vec, vec],
        [jax.ShapeDtypeStruct((SEQ, DM), F32), jax.ShapeDtypeStruct((SEQ, DM), BF16), mid_shape, mid_shape, mid_shape,
         jax.ShapeDtypeStruct((1, DM), F32), jax.ShapeDtypeStruct((1, DM), F32)],
        [pltpu.VMEM((tm, DM), F32)], (dx, x, gate, up, u, g_pre, g_post, wgt4, wut4, wd4), carry)


def _ffn_block_bwd(layer, dx, saved, g_pre, g_post, ex):
    tag = f"l{layer}"
    x, h, gate, up, u = saved
    (dx_in, du, dgate, dup, act, dg_pre, dg_post), sent = _ffn_bwd(
        f"{tag}_ffn_bwd", dx, x, gate, up, u, g_pre, g_post, ex.weight(("ffn_w_gate", layer)), ex.weight(("ffn_w_up", layer)),
        ex.weight(("ffn_w_down", layer)), ex.carry(f"{tag}_ffn_bwd"))
    ex.carried(f"{tag}_ffn_bwd", sent)
    d_wd = _ffn_bwd_dw(f"{tag}_dwd", act, du)
    d_wg = _ffn_bwd_dw(f"{tag}_dwg", dgate, h)
    d_wu = _ffn_bwd_dw(f"{tag}_dwu", dup, h)
    ex.grads(f"{tag}_ffn", {("ffn_w_gate", layer): d_wg, ("ffn_w_up", layer): d_wu, ("ffn_w_down", layer): d_wd})
    return dx_in, dg_pre, dg_post


def _alibi_slopes():
    return 2.0 ** (-8.0 * jnp.arange(1, NH + 1, dtype=F32) / NH)


def _local_step(x, target, norms, rpb, ex):
    g_mix_pre, g_mix_post, g_ffn_pre, g_ffn_post = norms
    row = lambda a, i: a[i:i + 1]

    bias, sent = _na_bias_tiles(rpb, ex.carry("na_bias_tiles"))
    ex.carried("na_bias_tiles", sent)
    h0, h0t = _rms_fwd_both("l0_mix_pre", x, row(g_mix_pre, 0))
    qkv0, sent = _qkv_fwd("l0_qkv", h0[None], ex.weight(("na_w_qkv", 0)), ex.carry("l0_qkv"))
    ex.carried("l0_qkv", sent)
    o0, sent = _na_fwd(qkv0[0], bias, ex.carry("na_fwd"))
    ex.carried("na_fwd", sent)
    na_wo = ex.weight(("na_w_o", 0)).reshape(DM, DM)
    x1, u0 = _proj_fwd("l0_proj", o0, na_wo, x, row(g_mix_post, 0))
    x2, ffn0 = _ffn_block(0, x1, row(g_ffn_pre, 0), row(g_ffn_post, 0), ex)

    slopes = _alibi_slopes()
    h2g, h2gt = _to_groups("l1_h_groups", _rms_fwd("l1_mix_pre", x2, row(g_mix_pre, 1), F32))
    dil_wqkv = ex.weight(("dil_w_qkv", 0))
    qkv1, sent = _qkv_fwd("l1_qkv", h2g, dil_wqkv, ex.carry("l1_qkv"))
    ex.carried("l1_qkv", sent)
    og, lg, sent = _dil_fwd(qkv1, slopes, ex.carry("dil_fwd"))
    ex.carried("dil_fwd", sent)
    o1, lse = _dil_merge(og, lg)
    dil_wo = ex.weight(("dil_w_o", 0)).reshape(DM, DM)
    x3, u1 = _proj_fwd("l1_proj", o1, dil_wo, x2, row(g_mix_post, 1))
    x4, ffn1 = _ffn_block(1, x3, row(g_ffn_pre, 1), row(g_ffn_post, 1), ex)

    dx4, loss_row = _loss_grad("loss", x4, target)

    dx3, dg_fpre1, dg_fpost1 = _ffn_block_bwd(1, dx4, ffn1, row(g_ffn_pre, 1), row(g_ffn_post, 1), ex)
    (do1, du1, dg_mpost1), sent = _proj_bwd("l1_proj_bwd", dx3, u1, row(g_mix_post, 1), dil_wo, F32, ex.carry("l1_proj_bwd"))
    ex.carried("l1_proj_bwd", sent)
    d_dil_wo = _proj_bwd_dw("l1_dwo", o1, du1)
    dog, ddg, lseg = _dil_bwd_prep(do1, o1, lse)
    dqkv1, sent = _dil_bwd(qkv1, dog, ddg, lseg, slopes, ex.carry("dil_bwd"))
    ex.carried("dil_bwd", sent)
    d_dil_wqkv = _qkv_bwd_dw("l1_dwqkv", h2gt, dqkv1, dil_wqkv.shape[2])
    ex.grads("l1_mix", {("dil_w_qkv", 0): d_dil_wqkv, ("dil_w_o", 0): d_dil_wo.reshape(NCHIP, DM // NCHIP, DM)})
    dh2g, sent = _qkv_bwd_dh("l1_dh", dqkv1, dil_wqkv, ex.carry("l1_dh"))
    ex.carried("l1_dh", sent)
    dh2 = _from_groups_sum("l1_dh_tokens", dh2g)
    (dx2, dg_mpre1), sent = _norm_bwd("l1_mix_pre_bwd", dh2, x2, row(g_mix_pre, 1), dx3, ex.carry("l1_mix_pre_bwd"))
    ex.carried("l1_mix_pre_bwd", sent)

    dx1, dg_fpre0, dg_fpost0 = _ffn_block_bwd(0, dx2, ffn0, row(g_ffn_pre, 0), row(g_ffn_post, 0), ex)
    (do0, du0, dg_mpost0), sent = _proj_bwd("l0_proj_bwd", dx1, u0, row(g_mix_post, 0), na_wo, BF16, ex.carry("l0_proj_bwd"))
    ex.carried("l0_proj_bwd", sent)
    d_na_wo = _proj_bwd_dw("l0_dwo", o0, du0)
    dqkv0, z, sent = _na_bwd(qkv0[0], bias, do0, ex.carry("na_bwd"))
    ex.carried("na_bwd", sent)
    d_rpb = _rpb_grad(z)
    na_wqkv = ex.weight(("na_w_qkv", 0))
    d_na_wqkv = _qkv_bwd_dw("l0_dwqkv", h0t[None], dqkv0[None], na_wqkv.shape[2])
    ex.grads("l0_mix", {("na_w_qkv", 0): d_na_wqkv, ("na_w_o", 0): d_na_wo.reshape(NCHIP, DM // NCHIP, DM)})
    dh0, sent = _qkv_bwd_dh("l0_dh", dqkv0[None], na_wqkv, ex.carry("l0_dh"))
    ex.carried("l0_dh", sent)
    (dx0, dg_mpre0), sent = _norm_bwd("l0_mix_pre_bwd", dh0[0], x, row(g_mix_pre, 0), dx1, ex.carry("l0_mix_pre_bwd"))
    ex.carried("l0_mix_pre_bwd", sent)

    dnorms = (jnp.concatenate([dg_mpre0, dg_mpre1]), jnp.concatenate([dg_mpost0, dg_mpost1]),
              jnp.concatenate([dg_fpre0, dg_fpre1]), jnp.concatenate([dg_fpost0, dg_fpost1]))
    return loss_row, dx0, dnorms, d_rpb


def _place():
    x, y, c = lax.axis_index("x"), lax.axis_index("y"), lax.axis_index("c")
    chips = ((1 - x, y), (x, 1 - y), (1 - x, 1 - y))
    return x, y, c, chips


def _chip_id(chip):
    return 2 * chip[0] + chip[1]


def _comm_call(name, body, ins, out_shapes, n_sems, aliases=None):
    return pl.pallas_call(
        body, in_specs=[HBM_SPEC] * len(ins), out_specs=[HBM_SPEC] * len(out_shapes), out_shape=out_shapes,
        scratch_shapes=[pltpu.SemaphoreType.DMA((k,)) for k in n_sems], input_output_aliases=aliases or {},
        compiler_params=pltpu.CompilerParams(has_side_effects=True), name=name)(*ins)


def _gather_copies(shards):
    n = len(shards)

    def copies(src, out, sems):
        send_sems, recv_sems = sems
        x, y, c, chips = _place()

        def copy(t, k, chip, half, to, from_src=False):
            blk = out[t].at[_chip_id(chip), half]
            return pltpu.make_async_remote_copy(
                src_ref=src[t].at[half] if from_src else blk, dst_ref=blk,
                send_sem=send_sems.at[6 * t + k], recv_sem=recv_sems.at[6 * t + k], device_id=to, device_id_type=MESH)

        return copy, x, y, c, chips

    def issue(src, out, sems):
        copy, x, y, c, chips = copies(src, out, sems)
        for t in range(n):
            for j, chip in enumerate(chips):
                copy(t, j, (x, y), c, (*chip, c), from_src=True).start()

    def drain(src, out, sems):
        copy, x, y, c, chips = copies(src, out, sems)
        passed = []
        for t in range(n):
            for j, chip in enumerate(chips):
                copy(t, j, chip, c, (x, y, c)).wait_recv()
                fwd = copy(t, 3 + j, chip, c, (x, y, 1 - c))
                fwd.start()
                passed.append(fwd)
        for t in range(n):
            for j, chip in enumerate(chips):
                copy(t, 3 + j, chip, 1 - c, (x, y, c)).wait_recv()
        for t in range(n):
            for j, chip in enumerate(chips):
                copy(t, j, (x, y), c, (*chip, c), from_src=True).wait_send()
        for cp in passed:
            cp.wait_send()

    return _Carried(shards, [jax.ShapeDtypeStruct((NCHIP,) + s.shape, s.dtype) for s in shards], (6 * n, 6 * n), issue, drain)


def _pair_exchange_copies(grads):
    n = len(grads)

    def copies(g, theirs, sems):
        send_sems, recv_sems = sems
        x, y, c, _ = _place()
        return [pltpu.make_async_remote_copy(src_ref=g[t].at[:, 1 - c], dst_ref=theirs[t], send_sem=send_sems.at[t],
                                             recv_sem=recv_sems.at[t], device_id=(x, y, 1 - c), device_id_type=MESH) for t in range(n)]

    def issue(g, theirs, sems):
        for cp in copies(g, theirs, sems):
            cp.start()

    def drain(g, theirs, sems):
        for cp in copies(g, theirs, sems):
            cp.wait()

    return _Carried(grads, [jax.ShapeDtypeStruct((NCHIP,) + g.shape[2:], g.dtype) for g in grads], (n, n), issue, drain)


def _chip_exchange_copies(items):
    flat = [(t, i, j) for t, (_, peers) in enumerate(items) for i, j in enumerate(peers)]

    def copies(p, slots, sems):
        send_sems, recv_sems = sems
        x, y, c, chips = _place()
        return [pltpu.make_async_remote_copy(src_ref=p[t].at[_chip_id(chips[j])], dst_ref=slots[t].at[i], send_sem=send_sems.at[k],
                                             recv_sem=recv_sems.at[k], device_id=(*chips[j], c), device_id_type=MESH)
                for k, (t, i, j) in enumerate(flat)]

    def issue(p, slots, sems):
        for cp in copies(p, slots, sems):
            cp.start()

    def drain(p, slots, sems):
        for cp in copies(p, slots, sems):
            cp.wait()

    return _Carried([p for p, _ in items], [jax.ShapeDtypeStruct((len(peers),) + p.shape[1:], p.dtype) for p, peers in items],
                    (len(flat), len(flat)), issue, drain)


def _pair_share(full):
    n = len(full)

    def body(*refs):
        buf = refs[n:2 * n]
        send_sems, recv_sems = refs[2 * n:]
        x, y, c, _ = _place()
        sends = [pltpu.make_async_remote_copy(src_ref=buf[t].at[c], dst_ref=buf[t].at[c], send_sem=send_sems.at[t], recv_sem=recv_sems.at[t],
                                              device_id=(x, y, 1 - c), device_id_type=MESH) for t in range(n)]
        for cp in sends:
            cp.start()
        for t in range(n):
            pltpu.make_async_remote_copy(src_ref=buf[t].at[c], dst_ref=buf[t].at[1 - c], send_sem=send_sems.at[t], recv_sem=recv_sems.at[t],
                                         device_id=(x, y, 1 - c), device_id_type=MESH).wait_recv()
        for cp in sends:
            cp.wait_send()

    return _comm_call("grad_pair_share", body, full, [jax.ShapeDtypeStruct(f.shape, f.dtype) for f in full], (n, n),
                      aliases={t: t for t in range(n)})


SMALL_ROWS = 128


def _allreduce_small(v):
    def body(v_ref, o_ref, buf, send_sems, recv_sems):
        x, y, c, _ = _place()
        me = 4 * x + 2 * y + c
        flip = lambda a, f: 1 - a if f else a
        buf[me] = v_ref[...]
        peers = [(flip(x, d >> 2 & 1), flip(y, d >> 1 & 1), flip(c, d & 1)) for d in range(1, 8)]
        sends = [pltpu.make_async_remote_copy(src_ref=v_ref, dst_ref=buf.at[me], send_sem=send_sems.at[i], recv_sem=recv_sems.at[i],
                                              device_id=peer, device_id_type=MESH) for i, peer in enumerate(peers)]
        for cp in sends:
            cp.start()
        for i, (px, py, pc) in enumerate(peers):
            pltpu.make_async_remote_copy(src_ref=v_ref, dst_ref=buf.at[4 * px + 2 * py + pc], send_sem=send_sems.at[i], recv_sem=recv_sems.at[i],
                                         device_id=(px, py, pc), device_id_type=MESH).wait_recv()
        for cp in sends:
            cp.wait_send()
        acc = buf[0]
        for k in range(1, 8):
            acc = acc + buf[k]
        o_ref[...] = acc

    vm = pl.BlockSpec(memory_space=pltpu.VMEM)
    return pl.pallas_call(
        body, in_specs=[vm], out_specs=vm, out_shape=jax.ShapeDtypeStruct((SMALL_ROWS, 128), F32),
        scratch_shapes=[pltpu.VMEM((8, SMALL_ROWS, 128), F32), pltpu.SemaphoreType.DMA((7,)), pltpu.SemaphoreType.DMA((7,))],
        compiler_params=pltpu.CompilerParams(has_side_effects=True), name="allreduce_small")(v)


def _row_block(rows, cols, budget=3 << 19):
    best = 8
    for bm in range(8, rows + 1, 8):
        if rows % bm == 0 and bm * cols * 4 <= budget:
            best = bm
    return best


def _pair_sum(name, place, g, theirs):
    _, m, c = theirs.shape
    bm = _row_block(m, c)

    def body(place_ref, a_ref, b_ref, o_ref):
        o_ref[...] = (a_ref[...].astype(F32) + b_ref[...].astype(F32)).astype(o_ref.dtype)

    spec = pl.BlockSpec((None, bm, c), lambda k, i, pr: (k, i, 0))
    return pl.pallas_call(
        body, out_shape=jax.ShapeDtypeStruct(theirs.shape, BF16),
        grid_spec=pltpu.PrefetchScalarGridSpec(
            num_scalar_prefetch=1, grid=(NCHIP, m // bm),
            in_specs=[pl.BlockSpec((None, None, bm, c), lambda k, i, pr: (k, pr[0], i, 0)), spec], out_specs=spec),
        compiler_params=_params(("parallel", "parallel")), name=name)(place, g, theirs)


def _chip_sum(name, place, parts, slots):
    _, m, c = parts.shape
    bm = _row_block(m, c)

    def body(place_ref, p_ref, *refs):
        acc = p_ref[...].astype(F32)
        for s_ref in refs[:-1]:
            for i in range(s_ref.shape[0]):
                acc = acc + s_ref[i].astype(F32)
        refs[-1][...] = acc

    return pl.pallas_call(
        body, out_shape=jax.ShapeDtypeStruct((2, m, c), F32),
        grid_spec=pltpu.PrefetchScalarGridSpec(
            num_scalar_prefetch=1, grid=(m // bm,),
            in_specs=[pl.BlockSpec((None, bm, c), lambda i, pr: (pr[1], i, 0))]
            + [pl.BlockSpec((s.shape[0], bm, c), lambda i, pr: (0, i, 0)) for s in slots],
            out_specs=pl.BlockSpec((None, bm, c), lambda i, pr: (pr[0], i, 0))),
        compiler_params=_params(("parallel",)), name=name)(place, parts, *slots)


def _adamw(name, w, g, m, v, layer=0, into=None):
    lead, rows, cols = w.shape
    bm = _row_block(rows, cols, budget=768 * 1024)
    c1 = 1.0 - ADAM_B1 ** ADAM_STEP
    c2 = 1.0 - ADAM_B2 ** ADAM_STEP

    def body(w_ref, g_ref, m_ref, v_ref, *rest):
        go_ref, d_ref, mo_ref, vo_ref = rest[-4:]
        g = g_ref[...]
        mn = ADAM_B1 * m_ref[...] + (1.0 - ADAM_B1) * g
        vn = ADAM_B2 * v_ref[...] + (1.0 - ADAM_B2) * (g * g)
        go_ref[...] = g
        mo_ref[...] = mn
        vo_ref[...] = vn
        d_ref[...] = -ADAM_LR * ((mn / c1) / (jnp.sqrt(vn / c2) + ADAM_EPS) + ADAM_WD * w_ref[...])

    spec = pl.BlockSpec((None, bm, cols), lambda i: (layer, i, 0))
    sh = jax.ShapeDtypeStruct((lead, rows, cols), F32)
    prev = [] if into is None else list(into)
    return pl.pallas_call(
        body, grid=(rows // bm,), in_specs=[spec, pl.BlockSpec((bm, cols), lambda i: (i, 0)), spec, spec] + [pl.BlockSpec(memory_space=pl.ANY)] * len(prev),
        out_specs=[spec] * 4, out_shape=[sh] * 4, input_output_aliases={4 + k: k for k in range(len(prev))},
        compiler_params=_params(("parallel",)), name=name)(w, g, m, v, *prev)


def _pack_small(norms, rpb):
    flat = jnp.concatenate([a.reshape(-1) for a in norms] + [rpb.reshape(-1)])
    return jnp.pad(flat, (0, SMALL_ROWS * 128 - flat.shape[0])).reshape(SMALL_ROWS, 128)


def _unpack_small(p):
    flat = p.reshape(-1)
    norms = [flat[i * 2 * DM:(i + 1) * 2 * DM].reshape(2, DM) for i in range(4)]
    rpb = flat[8 * DM:8 * DM + NH * 15 * 31].reshape(1, NH, 15, 31)
    return norms, rpb


FFN_NAMES = ("ffn_w_gate", "ffn_w_up", "ffn_w_down")
L0_FFN = tuple((n, 0) for n in FFN_NAMES)
L1_FFN = tuple((n, 1) for n in FFN_NAMES)
NA_KEYS = (("na_w_qkv", 0), ("na_w_o", 0))
DIL_KEYS = (("dil_w_qkv", 0), ("dil_w_o", 0))
ALL_PEERS, NEIGHBOURS, DIAGONAL = (0, 1, 2), (0, 1), (2,)


class _Exchange:
    GATHERS = {"na_bias_tiles": NA_KEYS, "l0_qkv": L0_FFN[:1], "na_fwd": L0_FFN[1:], "l0_ffn_fwd": DIL_KEYS[:1], "dil_fwd": L1_FFN + DIL_KEYS[1:]}
    PAIRS = {"l1_proj_bwd": L1_FFN, "l1_dh": DIL_KEYS, "l0_proj_bwd": L0_FFN}
    EXCHANGES = {"dil_bwd": [(k, ALL_PEERS) for k in L1_FFN],
                 "l0_ffn_bwd": [(DIL_KEYS[0], NEIGHBOURS)],
                 "na_bwd": [(k, ALL_PEERS) for k in L0_FFN] + [(DIL_KEYS[0], DIAGONAL), (DIL_KEYS[1], ALL_PEERS)],
                 "l0_dh": [(k, NEIGHBOURS) for k in NA_KEYS],
                 "l0_mix_pre_bwd": [(k, DIAGONAL) for k in NA_KEYS]}

    def __init__(self, shards):
        self.chip = 2 * lax.axis_index("x") + lax.axis_index("y")
        self.place = jnp.stack([lax.axis_index("c"), self.chip]).astype(jnp.int32)
        self.own = {k: s.reshape(2, s.shape[0] // 2, s.shape[1]).astype(BF16) for k, s in shards.items()}
        self.gathered, self.mine, self.parts, self.slots, self.full = {}, {}, {}, {}, {}

    def _take(self, keys, landed):
        for k, gw in zip(keys, landed):
            self.gathered[k] = lax.dynamic_update_slice(gw, self.own[k][None], (self.chip, 0, 0, 0))

    def _sum(self, items, landed):
        for (k, peers), s in zip(items, landed):
            got = self.slots.setdefault(k, {})
            got[peers] = s
            if sum(len(p) for p in got) == len(ALL_PEERS):
                self.full[k] = _chip_sum(f"chip_sum_{k[0]}_{k[1]}", self.place, self.parts[k], [got[p] for p in sorted(got)])

    def weight(self, key):
        g = self.gathered[key]
        return g.reshape(NCHIP, 2 * g.shape[2], g.shape[3])

    def _pair_sums(self, keys, theirs):
        for k, t in zip(keys, theirs):
            self.parts[k] = _pair_sum(f"pair_sum_{k[0]}_{k[1]}", self.place, self.mine[k], t)

    def carry(self, tag):
        if tag in self.GATHERS:
            return _gather_copies([self.own[k] for k in self.GATHERS[tag]])
        if tag in self.PAIRS:
            return _pair_exchange_copies([self.mine[k] for k in self.PAIRS[tag]])
        if tag in self.EXCHANGES:
            return _chip_exchange_copies([(self.parts[k], peers) for k, peers in self.EXCHANGES[tag]])
        return None

    def carried(self, tag, landed):
        if tag in self.GATHERS:
            self._take(self.GATHERS[tag], landed)
        elif tag in self.PAIRS:
            self._pair_sums(self.PAIRS[tag], landed)
        elif tag in self.EXCHANGES:
            self._sum(self.EXCHANGES[tag], landed)

    def grads(self, tag, dw):
        for k, g in dw.items():
            self.mine[k] = g.reshape(NCHIP, 2, -1, g.shape[-1])
        if tag == "l0_mix":
            keys = tuple(dw)
            self._pair_sums(keys, _run_carried("grad_pair_exchange_last", _pair_exchange_copies([self.mine[k] for k in keys])))

    def finish(self):
        keys = tuple(self.full)
        shared = _pair_share([self.full[k] for k in keys])
        return {k: s.reshape(2 * s.shape[1], s.shape[2]) for k, s in zip(keys, shared)}


def kernel(x, norm_mix_pre, norm_mix_post, norm_ffn_pre, norm_ffn_post, na_w_qkv, na_w_o, na_rpb, dil_w_qkv, dil_w_o, ffn_w_gate, ffn_w_up, ffn_w_down, loss_target, m_norm_mix_pre, m_norm_mix_post, m_norm_ffn_pre, m_norm_ffn_post, m_na_w_qkv, m_na_w_o, m_na_rpb, m_dil_w_qkv, m_dil_w_o, m_ffn_w_gate, m_ffn_w_up, m_ffn_w_down, v_norm_mix_pre, v_norm_mix_post, v_norm_ffn_pre, v_norm_ffn_post, v_na_w_qkv, v_na_w_o, v_na_rpb, v_dil_w_qkv, v_dil_w_o, v_ffn_w_gate, v_ffn_w_up, v_ffn_w_down):
    tr = lambda a: jnp.swapaxes(a, 1, 2)
    weights = {"na_w_qkv": na_w_qkv, "na_w_o": na_w_o, "dil_w_qkv": dil_w_qkv, "dil_w_o": dil_w_o,
               "ffn_w_gate": tr(ffn_w_gate), "ffn_w_up": tr(ffn_w_up), "ffn_w_down": ffn_w_down}
    m_in = {"na_w_qkv": m_na_w_qkv, "na_w_o": m_na_w_o, "dil_w_qkv": m_dil_w_qkv, "dil_w_o": m_dil_w_o,
            "ffn_w_gate": tr(m_ffn_w_gate), "ffn_w_up": tr(m_ffn_w_up), "ffn_w_down": m_ffn_w_down}
    v_in = {"na_w_qkv": v_na_w_qkv, "na_w_o": v_na_w_o, "dil_w_qkv": v_dil_w_qkv, "dil_w_o": v_dil_w_o,
            "ffn_w_gate": tr(v_ffn_w_gate), "ffn_w_up": tr(v_ffn_w_up), "ffn_w_down": v_ffn_w_down}

    ex = _Exchange({(n, l): weights[n][l] for n in weights for l in range(weights[n].shape[0])})
    norms = (norm_mix_pre, norm_mix_post, norm_ffn_pre, norm_ffn_post)
    loss_row, dx, dnorms, d_rpb = _local_step(x[0], loss_target[0], norms, na_rpb[0], ex)
    loss = lax.psum(loss_row[0, 0], ("x", "y", "c"))
    full = ex.finish()
    small = _allreduce_small(_pack_small(dnorms, d_rpb))

    out_g, out_d, out_m, out_v = {}, {}, {}, {}
    for n in weights:
        res = None
        for l in range(weights[n].shape[0]):
            res = _adamw(f"adamw_{n}_{l}", weights[n], full[(n, l)], m_in[n], v_in[n], l, res)
        if n in ("ffn_w_gate", "ffn_w_up"):
            res = [tr(r) for r in res]
        out_g[n], out_d[n], out_m[n], out_v[n] = res
    sm_names = ("norm_mix_pre", "norm_mix_post", "norm_ffn_pre", "norm_ffn_post", "na_rpb")
    sm = _adamw("adamw_small", _pack_small(norms, na_rpb)[None], small,
                _pack_small((m_norm_mix_pre, m_norm_mix_post, m_norm_ffn_pre, m_norm_ffn_post), m_na_rpb)[None],
                _pack_small((v_norm_mix_pre, v_norm_mix_post, v_norm_ffn_pre, v_norm_ffn_post), v_na_rpb)[None])
    for res, dst in zip(sm, (out_g, out_d, out_m, out_v)):
        ns, rp = _unpack_small(res)
        for n, a in zip(sm_names, ns + [rp]):
            dst[n] = a

    order = ("norm_mix_pre", "norm_mix_post", "norm_ffn_pre", "norm_ffn_post", "na_w_qkv", "na_w_o", "na_rpb", "dil_w_qkv", "dil_w_o",
             "ffn_w_gate", "ffn_w_up", "ffn_w_down")
    return (loss, dx[None], *[out_g[n] for n in order], *[out_d[n] for n in order], *[out_m[n] for n in order], *[out_v[n] for n in order])
```

```python
import functools

import numpy as np
import jax
import jax.numpy as jnp
from jax import lax
from jax.experimental import pallas as pl
from jax.experimental.pallas import tpu as pltpu

F32 = jnp.float32
BF16 = jnp.bfloat16

SEQ = 2048
DM = 1024
NH = 16
HD = 64
DFF = 2816
NCHIP = 4
FSH = DFF // NCHIP
GRID_W = 64
NA_QROWS = 4
NA_QB = NA_QROWS * GRID_W
NA_WROWS = 12
NA_WIN = NA_WROWS * GRID_W
DIL = (1, 4, 16)
DIL_QB = 256
DIL_WIN = DIL_QB + 128
DIL_RADIUS = 64
RMS_EPS = 1e-6
NEG = -1e30
QSCALE = HD ** -0.5
CH = 256
MESH = pl.DeviceIdType.MESH

ADAM_LR, ADAM_B1, ADAM_B2, ADAM_EPS, ADAM_WD, ADAM_STEP = 0.001, 0.9, 0.999, 1e-08, 0.01, 10

VMEM_LIMIT = 56 * 1024 * 1024

_NN = (((1,), (0,)), ((), ()))
_NT = (((1,), (1,)), ((), ()))
_TN = (((0,), (0,)), ((), ()))


def _params(sem):
    return pltpu.CompilerParams(dimension_semantics=sem, vmem_limit_bytes=VMEM_LIMIT)


def _matmul(name, pairs, grid, out_shape, out_spec, acc_shape, carrying=False, carry=None):
    nk = grid[-1]
    npair = len(pairs)
    n_in = 2 * npair

    def body(*refs):
        ins, o_ref = refs[:2 * npair], refs[n_in]
        part = None
        for p in range(npair):
            d = lax.dot_general(ins[2 * p][...].astype(BF16), ins[2 * p + 1][...].astype(BF16), pairs[p][4],
                                preferred_element_type=F32)
            part = d if part is None else part + d
        if nk == 1:
            o_ref[...] = part.astype(o_ref.dtype)
        else:
            acc_ref = refs[n_in + 1]
            kk = pl.program_id(len(grid) - 1)

            @pl.when(kk == 0)
            def _():
                acc_ref[...] = part

            @pl.when(kk > 0)
            def _():
                acc_ref[...] += part

            @pl.when(kk == nk - 1)
            def _():
                o_ref[...] = acc_ref[...].astype(o_ref.dtype)

    ops, specs = [], []
    for a, a_spec, b, b_spec, _ in pairs:
        ops += [a, b]
        specs += [a_spec, b_spec]
    (out,), sent = _carrier_call(name, body, grid, specs, [out_spec], [out_shape], [] if nk == 1 else [pltpu.VMEM(acc_shape, F32)], ops, carry)
    return (out, sent) if carrying else out


def _qkv_fwd(name, h_all, w4, carry):
    g_n = h_all.shape[0]
    per = w4.shape[2] // CH
    return _matmul(
        name, [(h_all, pl.BlockSpec((None, SEQ, DM), lambda g, q, k: (g, 0, 0)),
                w4, pl.BlockSpec((None, DM, CH), lambda g, q, k: ((g * 12 + q) // per, 0, (g * 12 + q) % per)), _NN)],
        (g_n, 12, 1), jax.ShapeDtypeStruct((g_n, SEQ, 3 * DM), BF16),
        pl.BlockSpec((None, SEQ, CH), lambda g, q, k: (g, 0, q)), None, carrying=True, carry=carry)


def _qkv_bwd_dh(name, dqkv, w4, carry):
    g_n = dqkv.shape[0]
    per = w4.shape[2] // CH
    tm = SEQ

    def pair(cb):
        chunk = lambda g, t: g * 12 + t * 4 + cb
        return (dqkv, pl.BlockSpec((None, None, tm, CH), lambda g, i, t: (g, t, i, cb)),
                w4, pl.BlockSpec((None, DM, CH), lambda g, i, t: (chunk(g, t) // per, 0, chunk(g, t) % per)), _NT)

    return _matmul(name, [pair(cb) for cb in range(4)], (g_n, SEQ // tm, 3), jax.ShapeDtypeStruct((g_n, SEQ, DM), F32),
                   pl.BlockSpec((None, tm, DM), lambda g, i, t: (g, i, 0)), (tm, DM), carrying=True, carry=carry)


def _qkv_bwd_dw(name, ht_all, dqkv, shard_cols):
    g_n = dqkv.shape[0]
    per = shard_cols // CH
    return _matmul(
        name, [(ht_all, pl.BlockSpec((None, DM, SEQ), lambda qq, k: (qq // 12, 0, 0)),
                dqkv, pl.BlockSpec((None, None, SEQ, CH), lambda qq, k: (qq // 12, (qq % 12) // 4, 0, qq % 4)), _NN)],
        (g_n * 12, 1), jax.ShapeDtypeStruct((NCHIP, DM, shard_cols), BF16),
        pl.BlockSpec((None, DM, CH), lambda qq, k: (qq // per, 0, qq % per)), None)


def _proj_fwd(name, o, wo, x, g):
    tm = 512

    def body(o_ref, w_ref, x_ref, g_ref, xn_ref, u_ref):
        u = jnp.dot(o_ref[...], w_ref[...], preferred_element_type=F32)
        u_ref[...] = u
        r = lax.rsqrt(jnp.mean(u * u, axis=-1, keepdims=True) + RMS_EPS)
        xn_ref[...] = x_ref[...] + u * r * g_ref[...]

    rows = pl.BlockSpec((tm, DM), lambda i: (i, 0))
    sh = jax.ShapeDtypeStruct((SEQ, DM), F32)
    return pl.pallas_call(
        body, grid=(SEQ // tm,), in_specs=[rows, pl.BlockSpec((DM, DM), lambda i: (0, 0)), rows, pl.BlockSpec((1, DM), lambda i: (0, 0))],
        out_specs=[rows, rows], out_shape=[sh, sh], compiler_params=_params(("parallel",)), name=name)(o, wo, x, g)


def _proj_bwd(name, dy, u, g, wo, dtype, carry):
    tm = 512

    def body(dy_ref, u_ref, g_ref, w_ref, do_ref, du_ref, dg_ref):
        dy = dy_ref[...]
        u = u_ref[...]
        r = lax.rsqrt(jnp.mean(u * u, axis=-1, keepdims=True) + RMS_EPS)
        yh = u * r
        t = dy * g_ref[...]
        du = (r * (t - yh * jnp.mean(t * yh, axis=-1, keepdims=True))).astype(BF16)
        du_ref[...] = du
        do_ref[...] = lax.dot_general(du, w_ref[...], _NT, preferred_element_type=F32).astype(do_ref.dtype)

        @pl.when(pl.program_id(0) == 0)
        def _():
            dg_ref[...] = jnp.zeros_like(dg_ref)

        dg_ref[...] += jnp.sum(dy * yh, axis=0, keepdims=True)

    rows = pl.BlockSpec((tm, DM), lambda i: (i, 0))
    vec = pl.BlockSpec((1, DM), lambda i: (0, 0))
    return _carrier_call(
        name, body, (SEQ // tm,), [rows, rows, vec, pl.BlockSpec((DM, DM), lambda i: (0, 0))], [rows, rows, vec],
        [jax.ShapeDtypeStruct((SEQ, DM), dtype), jax.ShapeDtypeStruct((SEQ, DM), BF16), jax.ShapeDtypeStruct((1, DM), F32)],
        [], (dy, u, g, wo), carry)


def _proj_bwd_dw(name, o, du):
    tn = 512
    return _matmul(
        name, [(o, pl.BlockSpec((SEQ, DM), lambda j, k: (0, 0)), du, pl.BlockSpec((SEQ, tn), lambda j, k: (0, j)), _TN)],
        (DM // tn, 1), jax.ShapeDtypeStruct((DM, DM), BF16), pl.BlockSpec((DM, tn), lambda j, k: (0, j)), None)


def _ffn_wspec(index_map):
    return pl.BlockSpec((None, FSH, DM), index_map)


def _ffn_bwd_dw(name, a4, b):
    return _matmul(
        name, [(a4, pl.BlockSpec((None, SEQ, FSH), lambda s, k: (s, 0, 0)), b, pl.BlockSpec((SEQ, DM), lambda s, k: (0, 0)), _TN)],
        (NCHIP, 1), jax.ShapeDtypeStruct((NCHIP, FSH, DM), BF16), _ffn_wspec(lambda s, k: (s, 0, 0)), None)


ROWS = 256


def _row_spec():
    return pl.BlockSpec((ROWS, DM), lambda i: (i, 0))


def _vec_spec():
    return pl.BlockSpec((1, DM), lambda i: (0, 0))


def _rms_fwd(name, x, g, dtype=BF16):
    def body(x_ref, g_ref, o_ref):
        x = x_ref[...]
        r = lax.rsqrt(jnp.mean(x * x, axis=-1, keepdims=True) + RMS_EPS)
        o_ref[...] = (x * r * g_ref[...]).astype(o_ref.dtype)

    return pl.pallas_call(body, grid=(SEQ // ROWS,), in_specs=[_row_spec(), _vec_spec()], out_specs=_row_spec(),
                          out_shape=jax.ShapeDtypeStruct((SEQ, DM), dtype), compiler_params=_params(("parallel",)), name=name)(x, g)


def _rms_fwd_both(name, x, g):
    def body(x_ref, g_ref, o_ref, t_ref):
        x = x_ref[...]
        r = lax.rsqrt(jnp.mean(x * x, axis=-1, keepdims=True) + RMS_EPS)
        h = x * r * g_ref[...]
        o_ref[...] = h.astype(o_ref.dtype)
        t_ref[...] = h.T.astype(t_ref.dtype)

    return pl.pallas_call(
        body, grid=(SEQ // ROWS,), in_specs=[_row_spec(), _vec_spec()], out_specs=[_row_spec(), pl.BlockSpec((DM, ROWS), lambda i: (0, i))],
        out_shape=[jax.ShapeDtypeStruct((SEQ, DM), BF16), jax.ShapeDtypeStruct((DM, SEQ), BF16)],
        compiler_params=_params(("parallel",)), name=name)(x, g)


def _norm_bwd(name, dy, u, g, res, carry):
    def body(dy_ref, u_ref, g_ref, res_ref, du_ref, dg_ref):
        dy = dy_ref[...]
        u = u_ref[...]
        r = lax.rsqrt(jnp.mean(u * u, axis=-1, keepdims=True) + RMS_EPS)
        yh = u * r
        t = dy * g_ref[...]
        du_ref[...] = r * (t - yh * jnp.mean(t * yh, axis=-1, keepdims=True)) + res_ref[...]

        @pl.when(pl.program_id(0) == 0)
        def _():
            dg_ref[...] = jnp.zeros_like(dg_ref)

        dg_ref[...] += jnp.sum(dy * yh, axis=0, keepdims=True)

    return _carrier_call(
        name, body, (SEQ // ROWS,), [_row_spec(), _row_spec(), _vec_spec(), _row_spec()], [_row_spec(), _vec_spec()],
        [jax.ShapeDtypeStruct((SEQ, DM), F32), jax.ShapeDtypeStruct((1, DM), F32)], [], (dy, u, g, res), carry)


def _loss_grad(name, y, t):
    def body(y_ref, t_ref, dy_ref, l_ref):
        e = y_ref[...] - t_ref[...]
        dy_ref[...] = e * (1.0 / DM)

        @pl.when(pl.program_id(0) == 0)
        def _():
            l_ref[...] = jnp.zeros_like(l_ref)

        l_ref[...] += jnp.sum(e * e) * (0.5 / DM)

    return pl.pallas_call(
        body, grid=(SEQ // ROWS,), in_specs=[_row_spec(), _row_spec()],
        out_specs=[_row_spec(), pl.BlockSpec((1, 128), lambda i: (0, 0))],
        out_shape=[jax.ShapeDtypeStruct((SEQ, DM), F32), jax.ShapeDtypeStruct((1, 128), F32)],
        compiler_params=_params(("arbitrary",)), name=name)(y, t)


HBM_SPEC = pl.BlockSpec(memory_space=pltpu.HBM)


class _Carried:
    def __init__(self, ins, out_shapes, n_sems, issue, drain):
        self.ins, self.out_shapes, self.n_sems, self.issue, self.drain = list(ins), list(out_shapes), tuple(n_sems), issue, drain


def _carrier_call(name, body, grid, in_specs, out_specs, out_shape, scratch_shapes, operands, carry):
    n_in, n_out, n_scr = len(in_specs), len(out_specs), len(scratch_shapes)
    if carry is None:
        res = pl.pallas_call(body, grid=grid, in_specs=in_specs, out_specs=out_specs, out_shape=out_shape, scratch_shapes=scratch_shapes,
                             compiler_params=_params(("arbitrary",) * len(grid)), name=name)(*operands)
        return list(res), []
    ci, co = len(carry.ins), len(carry.out_shapes)

    def wrapped(*refs):
        ins, cins = refs[:n_in], refs[n_in:n_in + ci]
        outs, couts = refs[n_in + ci:n_in + ci + n_out], refs[n_in + ci + n_out:n_in + ci + n_out + co]
        scr, sems = refs[n_in + ci + n_out + co:n_in + ci + n_out + co + n_scr], refs[n_in + ci + n_out + co + n_scr:]
        first = functools.reduce(jnp.logical_and, [pl.program_id(a) == 0 for a in range(len(grid))])
        last = functools.reduce(jnp.logical_and, [pl.program_id(a) == grid[a] - 1 for a in range(len(grid))])

        @pl.when(first)
        def _():
            carry.issue(cins, couts, sems)

        body(*ins, *outs, *scr)

        @pl.when(last)
        def _():
            carry.drain(cins, couts, sems)

    res = pl.pallas_call(
        wrapped, grid=grid, in_specs=list(in_specs) + [HBM_SPEC] * ci, out_specs=list(out_specs) + [HBM_SPEC] * co,
        out_shape=list(out_shape) + carry.out_shapes,
        scratch_shapes=list(scratch_shapes) + [pltpu.SemaphoreType.DMA((k,)) for k in carry.n_sems],
        compiler_params=pltpu.CompilerParams(dimension_semantics=("arbitrary",) * len(grid), vmem_limit_bytes=VMEM_LIMIT, has_side_effects=True),
        name=name)(*operands, *carry.ins)
    return list(res[:n_out]), list(res[n_out:])


def _run_carried(name, carry):
    def body(*refs):
        ci, co = len(carry.ins), len(carry.out_shapes)
        carry.issue(refs[:ci], refs[ci:ci + co], refs[ci + co:])
        carry.drain(refs[:ci], refs[ci:ci + co], refs[ci + co:])

    return pl.pallas_call(
        body, in_specs=[HBM_SPEC] * len(carry.ins), out_specs=[HBM_SPEC] * len(carry.out_shapes), out_shape=carry.out_shapes,
        scratch_shapes=[pltpu.SemaphoreType.DMA((k,)) for k in carry.n_sems],
        compiler_params=pltpu.CompilerParams(has_side_effects=True), name=name)(*carry.ins)


NA_BLOCKS = SEQ // NA_QB
NA_ROWS_TOTAL = SEQ // GRID_W
NA_CLASSES = ((0, 0), (8, 4), (NA_ROWS_TOTAL - NA_QROWS, NA_ROWS_TOTAL - NA_WROWS))


def _na_pairs(i0, ws):
    out = []
    for qi in range(NA_QROWS):
        i = i0 + qi
        rs = min(max(i - 4, 0), NA_ROWS_TOTAL - 8)
        for kr in range(NA_WROWS):
            r = ws + kr
            if rs <= r < rs + 8:
                out.append((qi, kr, r - i + 7))
    return out


def _diag_onehot():
    qc, kc = np.meshgrid(np.arange(GRID_W), np.arange(GRID_W), indexing="ij")
    e = np.zeros((GRID_W * GRID_W, 128), np.float32)
    j = (kc - qc + 15).reshape(-1)
    ok = (j >= 0) & (j <= 30)
    e[np.arange(GRID_W * GRID_W)[ok], j[ok]] = 1.0
    return jnp.asarray(e)


def _rpb_expand(rpb):
    r2 = jnp.pad(rpb.reshape(NH * 15, 31), ((0, 0), (0, 128 - 31)))

    def body(r_ref, e_ref, o_ref):
        o_ref[...] = lax.dot_general(r_ref[...], e_ref[...], _NT, preferred_element_type=F32, precision=lax.Precision.HIGHEST)

    out = pl.pallas_call(body, out_shape=jax.ShapeDtypeStruct((NH * 15, GRID_W * GRID_W), F32), name="rpb_expand",
                         compiler_params=pltpu.CompilerParams(vmem_limit_bytes=VMEM_LIMIT))(r2, _diag_onehot())
    return out.reshape(NH, 15, GRID_W, GRID_W)


def _na_bias_tiles(rpb, carry):
    def body(b_ref, o_ref):
        qc = lax.broadcasted_iota(jnp.int32, (GRID_W, GRID_W), 0)
        kc = lax.broadcasted_iota(jnp.int32, (GRID_W, GRID_W), 1)
        first = jnp.clip(qc - 8, 0, GRID_W - 16)
        in_window = (kc >= first) & (kc < first + 16)
        neg = jnp.full((GRID_W, GRID_W), NEG, F32)
        for cls, (i0, ws) in enumerate(NA_CLASSES):
            @pl.when(pl.program_id(0) == cls)
            def _(i0=i0, ws=ws):
                pairs = {(qi, kr): dr for qi, kr, dr in _na_pairs(i0, ws)}
                masked = {dr: jnp.where(in_window, b_ref[dr], NEG) for dr in sorted(set(pairs.values()))}
                for qi in range(NA_QROWS):
                    for k2 in range(NA_WROWS // 2):
                        blocks = [masked[pairs[(qi, kr)]] if (qi, kr) in pairs else neg for kr in (2 * k2, 2 * k2 + 1)]
                        o_ref[qi * GRID_W:(qi + 1) * GRID_W, k2 * 128:(k2 + 1) * 128] = jnp.concatenate(blocks, axis=1)

    (tiles,), sent = _carrier_call(
        "na_bias_tiles", body, (3, NH), [pl.BlockSpec((None, 15, GRID_W, GRID_W), lambda c, h: (h, 0, 0, 0))],
        [pl.BlockSpec((None, None, NA_QB, NA_WIN), lambda c, h: (c, h, 0, 0))], [jax.ShapeDtypeStruct((3, NH, NA_QB, NA_WIN), F32)],
        [], (_rpb_expand(rpb),), carry)
    return tiles, sent


def _na_cls(b):
    return jnp.where(b == 0, 0, jnp.where(b == NA_BLOCKS - 1, 2, 1))


def _na_start(b):
    return pl.multiple_of(jnp.clip(b * NA_QROWS - 4, 0, NA_ROWS_TOTAL - NA_WROWS) * GRID_W, GRID_W)


HPS = 4
LW = HPS * HD
NLW = DM // LW


NA_BWD_HPS = 4


def _na_in_specs(hps=HPS):
    lw = hps * HD
    nlw = DM // lw
    return [pl.BlockSpec((NA_QB, lw), lambda hp, b: (b, hp)),
            pl.BlockSpec((SEQ, lw), lambda hp, b: (0, nlw + hp)),
            pl.BlockSpec((SEQ, lw), lambda hp, b: (0, 2 * nlw + hp)),
            pl.BlockSpec((None, hps, NA_QB, NA_WIN), lambda hp, b: (_na_cls(b), hp, 0, 0))]


def _na_fwd(qkv, bias, carry):
    def body(q_ref, k_ref, v_ref, b_ref, o_ref):
        start = _na_start(pl.program_id(1))
        q = q_ref[...]
        kw = k_ref[pl.ds(start, NA_WIN), :]
        vw = v_ref[pl.ds(start, NA_WIN), :]
        outs = []
        for hh in range(HPS):
            sl = slice(hh * HD, (hh + 1) * HD)
            s = lax.dot_general(q[:, sl] * QSCALE, kw[:, sl], _NT, preferred_element_type=F32) + b_ref[hh]
            p = jnp.exp(s - jnp.max(s, axis=-1, keepdims=True))
            l = jnp.sum(p, axis=-1, keepdims=True)
            outs.append(jnp.dot(p.astype(BF16), vw[:, sl], preferred_element_type=F32) / l)
        o_ref[...] = jnp.concatenate(outs, axis=1).astype(o_ref.dtype)

    (o,), sent = _carrier_call(
        "na_fwd", body, (NLW, NA_BLOCKS), _na_in_specs(), [pl.BlockSpec((NA_QB, LW), lambda hp, b: (b, hp))],
        [jax.ShapeDtypeStruct((SEQ, DM), BF16)], [], (qkv, qkv, qkv, bias), carry)
    return o, sent


def _na_bwd(qkv, bias, do, carry):
    lw = NA_BWD_HPS * HD

    def body(q_ref, k_ref, v_ref, b_ref, do_ref, dqkv_ref, z_ref, dk_acc, dv_acc):
        blk = pl.program_id(1)

        @pl.when(blk == 0)
        def _():
            dk_acc[...] = jnp.zeros_like(dk_acc)
            dv_acc[...] = jnp.zeros_like(dv_acc)
            z_ref[...] = jnp.zeros_like(z_ref)

        start = _na_start(blk)
        q = q_ref[...]
        do = do_ref[...]
        kw = k_ref[pl.ds(start, NA_WIN), :]
        vw = v_ref[pl.ds(start, NA_WIN), :]
        dqs, dks, dvs, dss = [], [], [], []
        for hh in range(NA_BWD_HPS):
            sl = slice(hh * HD, (hh + 1) * HD)
            qh = q[:, sl] * QSCALE
            s = lax.dot_general(qh, kw[:, sl], _NT, preferred_element_type=F32) + b_ref[hh]
            p = jnp.exp(s - jnp.max(s, axis=-1, keepdims=True))
            p = p / jnp.sum(p, axis=-1, keepdims=True)
            dp = lax.dot_general(do[:, sl], vw[:, sl], _NT, preferred_element_type=F32)
            ds = p * (dp - jnp.sum(p * dp, axis=-1, keepdims=True))
            dsb = ds.astype(BF16)
            dqs.append(jnp.dot(dsb, kw[:, sl], preferred_element_type=F32) * QSCALE)
            dks.append(lax.dot_general(qh, dsb, _TN, preferred_element_type=F32).T)
            dvs.append(lax.dot_general(do[:, sl], p.astype(BF16), _TN, preferred_element_type=F32).T)
            dss.append(ds)
        for cls, (i0, ws) in enumerate(NA_CLASSES):
            @pl.when(_na_cls(blk) == cls)
            def _(i0=i0, ws=ws):
                for hh, ds in enumerate(dss):
                    for qi, kr, dr in _na_pairs(i0, ws):
                        z_ref[hh, dr * GRID_W:(dr + 1) * GRID_W, :] += ds[qi * GRID_W:(qi + 1) * GRID_W, kr * GRID_W:(kr + 1) * GRID_W]
        dqkv_ref[0, pl.ds(pl.multiple_of(blk * NA_QB, NA_QB), NA_QB), :] = jnp.concatenate(dqs, axis=1).astype(dqkv_ref.dtype)
        dk_acc[pl.ds(start, NA_WIN), :] += jnp.concatenate(dks, axis=1)
        dv_acc[pl.ds(start, NA_WIN), :] += jnp.concatenate(dvs, axis=1)

        @pl.when(blk == NA_BLOCKS - 1)
        def _():
            dqkv_ref[1] = dk_acc[...].astype(dqkv_ref.dtype)
            dqkv_ref[2] = dv_acc[...].astype(dqkv_ref.dtype)

    (dqkv, z), sent = _carrier_call(
        "na_bwd", body, (NH // NA_BWD_HPS, NA_BLOCKS),
        _na_in_specs(NA_BWD_HPS) + [pl.BlockSpec((NA_QB, lw), lambda hp, b: (b, hp))],
        [pl.BlockSpec((3, SEQ, lw), lambda hp, b: (0, 0, hp)), pl.BlockSpec((NA_BWD_HPS, 15 * GRID_W, GRID_W), lambda hp, b: (hp, 0, 0))],
        [jax.ShapeDtypeStruct((3, SEQ, DM), BF16), jax.ShapeDtypeStruct((NH, 15 * GRID_W, GRID_W), F32)],
        [pltpu.VMEM((SEQ, lw), F32), pltpu.VMEM((SEQ, lw), F32)], (qkv, qkv, qkv, bias, do), carry)
    return dqkv, z, sent


def _rpb_grad(z):
    z2 = z.reshape(NH * 15, GRID_W * GRID_W)

    def body(z_ref, e_ref, o_ref):
        o_ref[...] = jnp.dot(z_ref[...], e_ref[...], preferred_element_type=F32, precision=lax.Precision.HIGHEST)

    out = pl.pallas_call(body, out_shape=jax.ShapeDtypeStruct((NH * 15, 128), F32), name="rpb_grad",
                         compiler_params=pltpu.CompilerParams(vmem_limit_bytes=VMEM_LIMIT))(z2, _diag_onehot())
    return out[:, :31].reshape(NH, 15, 31)


DIL_BLOCKS = SEQ // DIL_QB
DIL_HPS = 8
DIL_LW = DIL_HPS * HD
DIL_NLW = DM // DIL_LW


COLS = 128


def _col_spec():
    return pl.BlockSpec((SEQ, COLS), lambda j: (0, j))


def _grp_spec():
    return pl.BlockSpec((3, SEQ, COLS), lambda j: (0, 0, j))


def _store_group_order(dst_ref, src_ref):
    for g, d in enumerate(DIL):
        n = SEQ // d
        for r in range(d):
            dst_ref[g, r * n:(r + 1) * n, :] = src_ref[pl.ds(r, n, stride=d), :].astype(dst_ref.dtype)


def _store_token_order(dst_ref, src_ref, g):
    d = DIL[g]
    n = SEQ // d
    for r in range(d):
        dst_ref[pl.ds(r, n, stride=d), :] = src_ref[g, r * n:(r + 1) * n, :]


def _to_groups(name, a):
    def body(a_ref, o_ref, t_ref):
        _store_group_order(o_ref, a_ref)
        for g in range(3):
            t_ref[g] = o_ref[g].astype(F32).T.astype(t_ref.dtype)

    return pl.pallas_call(
        body, grid=(DM // COLS,), in_specs=[_col_spec()], out_specs=[_grp_spec(), pl.BlockSpec((3, COLS, SEQ), lambda j: (0, j, 0))],
        out_shape=[jax.ShapeDtypeStruct((3, SEQ, DM), BF16), jax.ShapeDtypeStruct((3, DM, SEQ), BF16)],
        compiler_params=_params(("parallel",)), name=name)(a)


def _from_groups_sum(name, a):
    def body(a_ref, o_ref, t1, t2):
        _store_token_order(t1, a_ref, 1)
        _store_token_order(t2, a_ref, 2)
        o_ref[...] = (a_ref[0] + t1[...]) + t2[...]

    return pl.pallas_call(body, grid=(DM // COLS,), in_specs=[_grp_spec()], out_specs=_col_spec(),
                          out_shape=jax.ShapeDtypeStruct((SEQ, DM), F32), scratch_shapes=[pltpu.VMEM((SEQ, COLS), F32)] * 2,
                          compiler_params=_params(("parallel",)), name=name)(a)


def _dil_start(b):
    return pl.multiple_of(jnp.clip(b * DIL_QB - DIL_RADIUS, 0, SEQ - DIL_WIN), DIL_RADIUS)


def _dil_mask(g, b, start):
    shift = 11 - 2 * g
    ii = b * DIL_QB + lax.broadcasted_iota(jnp.int32, (DIL_QB, DIL_WIN), 0)
    jj = start + lax.broadcasted_iota(jnp.int32, (DIL_QB, DIL_WIN), 1)
    dist = jnp.abs(ii - jj)
    valid = (dist <= DIL_RADIUS) & (jnp.right_shift(ii, shift) == jnp.right_shift(jj, shift))
    return valid, dist.astype(F32)


def _dil_in_specs():
    return [pl.BlockSpec(memory_space=pltpu.SMEM),
            pl.BlockSpec((None, DIL_QB, DIL_LW), lambda g, hp, b: (g, b, hp)),
            pl.BlockSpec((None, SEQ, DIL_LW), lambda g, hp, b: (g, 0, DIL_NLW + hp)),
            pl.BlockSpec((None, SEQ, DIL_LW), lambda g, hp, b: (g, 0, 2 * DIL_NLW + hp))]


def _dil_fwd(qkv, slopes, carry):
    def body(sl_ref, q_ref, k_ref, v_ref, o_ref, lse_ref):
        g, hp, b = pl.program_id(0), pl.program_id(1), pl.program_id(2)
        start = _dil_start(b)
        valid, dist = _dil_mask(g, b, start)
        dil = jnp.left_shift(1, 2 * g).astype(F32)
        q = q_ref[...]
        kw = k_ref[pl.ds(start, DIL_WIN), :]
        vw = v_ref[pl.ds(start, DIL_WIN), :]
        outs, lses = [], []
        for hh in range(DIL_HPS):
            sl = slice(hh * HD, (hh + 1) * HD)
            s = lax.dot_general(q[:, sl] * QSCALE, kw[:, sl], _NT, preferred_element_type=F32)
            s = jnp.where(valid, s - (sl_ref[hp * DIL_HPS + hh] * dil) * dist, NEG)
            m = jnp.max(s, axis=-1, keepdims=True)
            p = jnp.exp(s - m)
            l = jnp.sum(p, axis=-1, keepdims=True)
            outs.append(jnp.dot(p.astype(BF16), vw[:, sl], preferred_element_type=F32) / l)
            lses.append(jnp.broadcast_to(m + jnp.log(l), (DIL_QB, HD)))
        o_ref[...] = jnp.concatenate(outs, axis=1)
        lse_ref[...] = jnp.concatenate(lses, axis=1)

    ospec = pl.BlockSpec((None, DIL_QB, DIL_LW), lambda g, hp, b: (g, b, hp))
    sh = jax.ShapeDtypeStruct((3, SEQ, DM), F32)
    (o, lse), sent = _carrier_call("dil_fwd", body, (3, DIL_NLW, DIL_BLOCKS), _dil_in_specs(), [ospec, ospec], [sh, sh], [],
                                   (slopes, qkv, qkv, qkv), carry)
    return o, lse, sent


def _dil_merge(o_all, lse_all):
    def body(o_ref, l_ref, out_ref, lse_ref, o1, o2, l1, l2):
        for g, (ot, lt) in ((1, (o1, l1)), (2, (o2, l2))):
            _store_token_order(ot, o_ref, g)
            _store_token_order(lt, l_ref, g)
        la, lb, lc = l_ref[0], l1[...], l2[...]
        m = jnp.maximum(jnp.maximum(la, lb), lc)
        wa, wb, wc = jnp.exp(la - m), jnp.exp(lb - m), jnp.exp(lc - m)
        sw = (wa + wb) + wc
        out_ref[...] = (((wa * o_ref[0] + wb * o1[...]) + wc * o2[...]) / sw).astype(out_ref.dtype)
        lse_ref[...] = m + jnp.log(sw)

    return pl.pallas_call(
        body, grid=(DM // COLS,), in_specs=[_grp_spec(), _grp_spec()], out_specs=[_col_spec(), _col_spec()],
        out_shape=[jax.ShapeDtypeStruct((SEQ, DM), BF16), jax.ShapeDtypeStruct((SEQ, DM), F32)],
        scratch_shapes=[pltpu.VMEM((SEQ, COLS), F32)] * 4, compiler_params=_params(("parallel",)), name="dil_merge")(o_all, lse_all)


def _dil_bwd_prep(do, o, lse):
    heads = COLS // HD

    def body(do_ref, o_ref, lse_ref, dog_ref, ddr_ref, lser_ref, dd, grp):
        prod = do_ref[...] * o_ref[...].astype(F32)
        dd[...] = jnp.concatenate(
            [jnp.broadcast_to(jnp.sum(prod[:, h * HD:(h + 1) * HD], axis=-1, keepdims=True), (SEQ, HD)) for h in range(heads)], axis=1)
        _store_group_order(dog_ref, do_ref)
        for src, dst in ((dd, ddr_ref), (lse_ref, lser_ref)):
            _store_group_order(grp, src)
            for g in range(3):
                t = grp[g].T
                for h in range(heads):
                    dst[g, h] = t[h * HD:h * HD + 8, :]

    rows = jax.ShapeDtypeStruct((3, NH, 8, SEQ), F32)
    rspec = pl.BlockSpec((3, heads, 8, SEQ), lambda j: (0, j, 0, 0))
    return pl.pallas_call(
        body, grid=(DM // COLS,), in_specs=[_col_spec()] * 3, out_specs=[_grp_spec(), rspec, rspec],
        out_shape=[jax.ShapeDtypeStruct((3, SEQ, DM), BF16), rows, rows],
        scratch_shapes=[pltpu.VMEM((SEQ, COLS), F32), pltpu.VMEM((3, SEQ, COLS), F32)],
        compiler_params=_params(("parallel",)), name="dil_bwd_prep")(do, o, lse)


def _dil_bwd(qkv, do, dd, lse, slopes, carry):
    def body(sl_ref, q_ref, k_ref, v_ref, do_ref, dd_ref, lse_ref, dqkv_ref, dk_acc, dv_acc):
        g, hp, b = pl.program_id(0), pl.program_id(1), pl.program_id(2)

        @pl.when(b == 0)
        def _():
            dk_acc[...] = jnp.zeros_like(dk_acc)
            dv_acc[...] = jnp.zeros_like(dv_acc)

        start = _dil_start(b)
        shift = 11 - 2 * g
        jj = start + lax.broadcasted_iota(jnp.int32, (DIL_WIN, DIL_QB), 0)
        ii = b * DIL_QB + lax.broadcasted_iota(jnp.int32, (DIL_WIN, DIL_QB), 1)
        dist = jnp.abs(ii - jj)
        valid = (dist <= DIL_RADIUS) & (jnp.right_shift(ii, shift) == jnp.right_shift(jj, shift))
        dist = dist.astype(F32)
        dil = jnp.left_shift(1, 2 * g).astype(F32)
        q = q_ref[...]
        do = do_ref[...]
        kw = k_ref[pl.ds(start, DIL_WIN), :]
        vw = v_ref[pl.ds(start, DIL_WIN), :]
        dqs, dks, dvs = [], [], []
        for hh in range(DIL_HPS):
            sl = slice(hh * HD, (hh + 1) * HD)
            qh = q[:, sl] * QSCALE
            st = lax.dot_general(kw[:, sl], qh, _NT, preferred_element_type=F32)
            st = jnp.where(valid, st - (sl_ref[hp * DIL_HPS + hh] * dil) * dist, NEG)
            pt = jnp.exp(st - lse_ref[hh, 0:1, :])
            dpt = lax.dot_general(vw[:, sl], do[:, sl], _NT, preferred_element_type=F32)
            dst = (pt * (dpt - dd_ref[hh, 0:1, :])).astype(BF16)
            dqs.append(lax.dot_general(kw[:, sl], dst, _TN, preferred_element_type=F32).T * QSCALE)
            dks.append(jnp.dot(dst, qh, preferred_element_type=F32))
            dvs.append(jnp.dot(pt.astype(BF16), do[:, sl], preferred_element_type=F32))
        dqkv_ref[0, pl.ds(pl.multiple_of(b * DIL_QB, DIL_QB), DIL_QB), :] = jnp.concatenate(dqs, axis=1).astype(dqkv_ref.dtype)
        dk_acc[pl.ds(start, DIL_WIN), :] += jnp.concatenate(dks, axis=1)
        dv_acc[pl.ds(start, DIL_WIN), :] += jnp.concatenate(dvs, axis=1)

        @pl.when(b == DIL_BLOCKS - 1)
        def _():
            dqkv_ref[1] = dk_acc[...].astype(dqkv_ref.dtype)
            dqkv_ref[2] = dv_acc[...].astype(dqkv_ref.dtype)

    qspec = pl.BlockSpec((None, DIL_QB, DIL_LW), lambda g, hp, b: (g, b, hp))
    rspec = pl.BlockSpec((None, DIL_HPS, 8, DIL_QB), lambda g, hp, b: (g, hp, 0, b))
    (dqkv,), sent = _carrier_call(
        "dil_bwd", body, (3, DIL_NLW, DIL_BLOCKS), _dil_in_specs() + [qspec, rspec, rspec],
        [pl.BlockSpec((None, 3, SEQ, DIL_LW), lambda g, hp, b: (g, 0, 0, hp))], [jax.ShapeDtypeStruct((3, 3, SEQ, DM), BF16)],
        [pltpu.VMEM((SEQ, DIL_LW), F32), pltpu.VMEM((SEQ, DIL_LW), F32)], (slopes, qkv, qkv, qkv, do, dd, lse), carry)
    return dqkv, sent


def _ffn_fwd(name, x, g_pre, g_post, wgt4, wut4, wd4, carry):
    tm = 512

    def body(x_ref, gpre_ref, gpost_ref, wg_ref, wu_ref, wd_ref, xn_ref, h_ref, gate_ref, up_ref, u_ref, acc):
        s = pl.program_id(1)

        @pl.when(s == 0)
        def _():
            x = x_ref[...]
            r = lax.rsqrt(jnp.mean(x * x, axis=-1, keepdims=True) + RMS_EPS)
            h_ref[...] = (x * r * gpre_ref[...]).astype(h_ref.dtype)

        h = h_ref[...]
        gate = lax.dot_general(h, wg_ref[...], _NT, preferred_element_type=F32).astype(BF16)
        up = lax.dot_general(h, wu_ref[...], _NT, preferred_element_type=F32).astype(BF16)
        gate_ref[...] = gate
        up_ref[...] = up
        gf = gate.astype(F32)
        act = (gf * jax.nn.sigmoid(gf) * up.astype(F32)).astype(BF16)
        part = jnp.dot(act, wd_ref[...], preferred_element_type=F32)

        @pl.when(s == 0)
        def _():
            acc[...] = part

        @pl.when(s > 0)
        def _():
            acc[...] += part

        @pl.when(s == NCHIP - 1)
        def _():
            u = acc[...]
            u_ref[...] = u
            r = lax.rsqrt(jnp.mean(u * u, axis=-1, keepdims=True) + RMS_EPS)
            xn_ref[...] = x_ref[...] + u * r * gpost_ref[...]

    rows = pl.BlockSpec((tm, DM), lambda i, s: (i, 0))
    vec = pl.BlockSpec((1, DM), lambda i, s: (0, 0))
    wspec = _ffn_wspec(lambda i, s: (s, 0, 0))
    mid = pl.BlockSpec((None, tm, FSH), lambda i, s: (s, i, 0))
    outs, sent = _carrier_call(
        name, body, (SEQ // tm, NCHIP), [rows, vec, vec, wspec, wspec, wspec], [rows, rows, mid, mid, rows],
        [jax.ShapeDtypeStruct((SEQ, DM), F32), jax.ShapeDtypeStruct((SEQ, DM), BF16), jax.ShapeDtypeStruct((NCHIP, SEQ, FSH), BF16),
         jax.ShapeDtypeStruct((NCHIP, SEQ, FSH), BF16), jax.ShapeDtypeStruct((SEQ, DM), F32)],
        [pltpu.VMEM((tm, DM), F32)], (x, g_pre, g_post, wgt4, wut4, wd4), carry)
    return outs, sent


def _ffn_block(layer, x, g_pre, g_post, ex):
    tag = f"l{layer}_ffn_fwd"
    (x_new, h, gate, up, u), sent = _ffn_fwd(tag, x, g_pre, g_post, ex.weight(("ffn_w_gate", layer)), ex.weight(("ffn_w_up", layer)),
                                             ex.weight(("ffn_w_down", layer)), ex.carry(tag))
    ex.carried(tag, sent)
    return x_new, (x, h, gate, up, u)


def _ffn_bwd(name, dx, x, gate, up, u, g_pre, g_post, wgt4, wut4, wd4, carry):
    tm = 512

    def body(dx_ref, x_ref, gate_ref, up_ref, u_ref, gpre_ref, gpost_ref, wg_ref, wu_ref, wd_ref,
             dxin_ref, du_ref, dgate_ref, dup_ref, act_ref, dgpre_ref, dgpost_ref, dh_acc):
        i, s = pl.program_id(0), pl.program_id(1)

        @pl.when((i == 0) & (s == 0))
        def _():
            dgpre_ref[...] = jnp.zeros_like(dgpre_ref)
            dgpost_ref[...] = jnp.zeros_like(dgpost_ref)

        @pl.when(s == 0)
        def _():
            dy = dx_ref[...]
            uu = u_ref[...]
            r = lax.rsqrt(jnp.mean(uu * uu, axis=-1, keepdims=True) + RMS_EPS)
            yh = uu * r
            t = dy * gpost_ref[...]
            du_ref[...] = (r * (t - yh * jnp.mean(t * yh, axis=-1, keepdims=True))).astype(du_ref.dtype)
            dgpost_ref[...] += jnp.sum(dy * yh, axis=0, keepdims=True)

        dact = lax.dot_general(du_ref[...], wd_ref[...], _NT, preferred_element_type=F32)
        g = gate_ref[...].astype(F32)
        upv = up_ref[...].astype(F32)
        sg = jax.nn.sigmoid(g)
        dgate = (dact * upv * sg * (1.0 + g * (1.0 - sg))).astype(BF16)
        dup = (dact * g * sg).astype(BF16)
        dgate_ref[...] = dgate
        dup_ref[...] = dup
        act_ref[...] = (g * sg * upv).astype(act_ref.dtype)
        part = jnp.dot(dgate, wg_ref[...], preferred_element_type=F32) + jnp.dot(dup, wu_ref[...], preferred_element_type=F32)

        @pl.when(s == 0)
        def _():
            dh_acc[...] = part

        @pl.when(s > 0)
        def _():
            dh_acc[...] += part

        @pl.when(s == NCHIP - 1)
        def _():
            dh = dh_acc[...]
            xx = x_ref[...]
            r = lax.rsqrt(jnp.mean(xx * xx, axis=-1, keepdims=True) + RMS_EPS)
            yh = xx * r
            t = dh * gpre_ref[...]
            dxin_ref[...] = dx_ref[...] + r * (t - yh * jnp.mean(t * yh, axis=-1, keepdims=True))
            dgpre_ref[...] += jnp.sum(dh * yh, axis=0, keepdims=True)

    rows = pl.BlockSpec((tm, DM), lambda i, s: (i, 0))
    vec = pl.BlockSpec((1, DM), lambda i, s: (0, 0))
    wspec = _ffn_wspec(lambda i, s: (s, 0, 0))
    mid = pl.BlockSpec((None, tm, FSH), lambda i, s: (s, i, 0))
    mid_shape = jax.ShapeDtypeStruct((NCHIP, SEQ, FSH), BF16)
    return _carrier_call(
        name, body, (SEQ // tm, NCHIP), [rows, rows, mid, mid, rows, vec, vec, wspec, wspec, wspec], [rows, rows, mid, mid, mid, vec, vec],
        [jax.ShapeDtypeStruct((SEQ, DM), F32), jax.ShapeDtypeStruct((SEQ, DM), BF16), mid_shape, mid_shape, mid_shape,
         jax.ShapeDtypeStruct((1, DM), F32), jax.ShapeDtypeStruct((1, DM), F32)],
        [pltpu.VMEM((tm, DM), F32)], (dx, x, gate, up, u, g_pre, g_post, wgt4, wut4, wd4), carry)


def _ffn_block_bwd(layer, dx, saved, g_pre, g_post, ex):
    tag = f"l{layer}"
    x, h, gate, up, u = saved
    (dx_in, du, dgate, dup, act, dg_pre, dg_post), sent = _ffn_bwd(
        f"{tag}_ffn_bwd", dx, x, gate, up, u, g_pre, g_post, ex.weight(("ffn_w_gate", layer)), ex.weight(("ffn_w_up", layer)),
        ex.weight(("ffn_w_down", layer)), ex.carry(f"{tag}_ffn_bwd"))
    ex.carried(f"{tag}_ffn_bwd", sent)
    d_wd = _ffn_bwd_dw(f"{tag}_dwd", act, du)
    d_wg = _ffn_bwd_dw(f"{tag}_dwg", dgate, h)
    d_wu = _ffn_bwd_dw(f"{tag}_dwu", dup, h)
    ex.grads(f"{tag}_ffn", {("ffn_w_gate", layer): d_wg, ("ffn_w_up", layer): d_wu, ("ffn_w_down", layer): d_wd})
    return dx_in, dg_pre, dg_post


def _alibi_slopes():
    return 2.0 ** (-8.0 * jnp.arange(1, NH + 1, dtype=F32) / NH)


def _local_step(x, target, norms, rpb, ex):
    g_mix_pre, g_mix_post, g_ffn_pre, g_ffn_post = norms
    row = lambda a, i: a[i:i + 1]

    bias, sent = _na_bias_tiles(rpb, ex.carry("na_bias_tiles"))
    ex.carried("na_bias_tiles", sent)
    h0, h0t = _rms_fwd_both("l0_mix_pre", x, row(g_mix_pre, 0))
    qkv0, sent = _qkv_fwd("l0_qkv", h0[None], ex.weight(("na_w_qkv", 0)), ex.carry("l0_qkv"))
    ex.carried("l0_qkv", sent)
    o0, sent = _na_fwd(qkv0[0], bias, ex.carry("na_fwd"))
    ex.carried("na_fwd", sent)
    na_wo = ex.weight(("na_w_o", 0)).reshape(DM, DM)
    x1, u0 = _proj_fwd("l0_proj", o0, na_wo, x, row(g_mix_post, 0))
    x2, ffn0 = _ffn_block(0, x1, row(g_ffn_pre, 0), row(g_ffn_post, 0), ex)

    slopes = _alibi_slopes()
    h2g, h2gt = _to_groups("l1_h_groups", _rms_fwd("l1_mix_pre", x2, row(g_mix_pre, 1), F32))
    dil_wqkv = ex.weight(("dil_w_qkv", 0))
    qkv1, sent = _qkv_fwd("l1_qkv", h2g, dil_wqkv, ex.carry("l1_qkv"))
    ex.carried("l1_qkv", sent)
    og, lg, sent = _dil_fwd(qkv1, slopes, ex.carry("dil_fwd"))
    ex.carried("dil_fwd", sent)
    o1, lse = _dil_merge(og, lg)
    dil_wo = ex.weight(("dil_w_o", 0)).reshape(DM, DM)
    x3, u1 = _proj_fwd("l1_proj", o1, dil_wo, x2, row(g_mix_post, 1))
    x4, ffn1 = _ffn_block(1, x3, row(g_ffn_pre, 1), row(g_ffn_post, 1), ex)

    dx4, loss_row = _loss_grad("loss", x4, target)

    dx3, dg_fpre1, dg_fpost1 = _ffn_block_bwd(1, dx4, ffn1, row(g_ffn_pre, 1), row(g_ffn_post, 1), ex)
    (do1, du1, dg_mpost1), sent = _proj_bwd("l1_proj_bwd", dx3, u1, row(g_mix_post, 1), dil_wo, F32, ex.carry("l1_proj_bwd"))
    ex.carried("l1_proj_bwd", sent)
    d_dil_wo = _proj_bwd_dw("l1_dwo", o1, du1)
    dog, ddg, lseg = _dil_bwd_prep(do1, o1, lse)
    dqkv1, sent = _dil_bwd(qkv1, dog, ddg, lseg, slopes, ex.carry("dil_bwd"))
    ex.carried("dil_bwd", sent)
    d_dil_wqkv = _qkv_bwd_dw("l1_dwqkv", h2gt, dqkv1, dil_wqkv.shape[2])
    ex.grads("l1_mix", {("dil_w_qkv", 0): d_dil_wqkv, ("dil_w_o", 0): d_dil_wo.reshape(NCHIP, DM // NCHIP, DM)})
    dh2g, sent = _qkv_bwd_dh("l1_dh", dqkv1, dil_wqkv, ex.carry("l1_dh"))
    ex.carried("l1_dh", sent)
    dh2 = _from_groups_sum("l1_dh_tokens", dh2g)
    (dx2, dg_mpre1), sent = _norm_bwd("l1_mix_pre_bwd", dh2, x2, row(g_mix_pre, 1), dx3, ex.carry("l1_mix_pre_bwd"))
    ex.carried("l1_mix_pre_bwd", sent)

    dx1, dg_fpre0, dg_fpost0 = _ffn_block_bwd(0, dx2, ffn0, row(g_ffn_pre, 0), row(g_ffn_post, 0), ex)
    (do0, du0, dg_mpost0), sent = _proj_bwd("l0_proj_bwd", dx1, u0, row(g_mix_post, 0), na_wo, BF16, ex.carry("l0_proj_bwd"))
    ex.carried("l0_proj_bwd", sent)
    d_na_wo = _proj_bwd_dw("l0_dwo", o0, du0)
    dqkv0, z, sent = _na_bwd(qkv0[0], bias, do0, ex.carry("na_bwd"))
    ex.carried("na_bwd", sent)
    d_rpb = _rpb_grad(z)
    na_wqkv = ex.weight(("na_w_qkv", 0))
    d_na_wqkv = _qkv_bwd_dw("l0_dwqkv", h0t[None], dqkv0[None], na_wqkv.shape[2])
    ex.grads("l0_mix", {("na_w_qkv", 0): d_na_wqkv, ("na_w_o", 0): d_na_wo.reshape(NCHIP, DM // NCHIP, DM)})
    dh0, sent = _qkv_bwd_dh("l0_dh", dqkv0[None], na_wqkv, ex.carry("l0_dh"))
    ex.carried("l0_dh", sent)
    (dx0, dg_mpre0), sent = _norm_bwd("l0_mix_pre_bwd", dh0[0], x, row(g_mix_pre, 0), dx1, ex.carry("l0_mix_pre_bwd"))
    ex.carried("l0_mix_pre_bwd", sent)

    dnorms = (jnp.concatenate([dg_mpre0, dg_mpre1]), jnp.concatenate([dg_mpost0, dg_mpost1]),
              jnp.concatenate([dg_fpre0, dg_fpre1]), jnp.concatenate([dg_fpost0, dg_fpost1]))
    return loss_row, dx0, dnorms, d_rpb


def _place():
    x, y, c = lax.axis_index("x"), lax.axis_index("y"), lax.axis_index("c")
    chips = ((1 - x, y), (x, 1 - y), (1 - x, 1 - y))
    return x, y, c, chips


def _chip_id(chip):
    return 2 * chip[0] + chip[1]


def _comm_call(name, body, ins, out_shapes, n_sems, aliases=None):
    return pl.pallas_call(
        body, in_specs=[HBM_SPEC] * len(ins), out_specs=[HBM_SPEC] * len(out_shapes), out_shape=out_shapes,
        scratch_shapes=[pltpu.SemaphoreType.DMA((k,)) for k in n_sems], input_output_aliases=aliases or {},
        compiler_params=pltpu.CompilerParams(has_side_effects=True), name=name)(*ins)


def _gather_copies(shards):
    n = len(shards)

    def copies(src, out, sems):
        send_sems, recv_sems = sems
        x, y, c, chips = _place()

        def copy(t, k, chip, half, to, from_src=False):
            blk = out[t].at[_chip_id(chip), half]
            return pltpu.make_async_remote_copy(
                src_ref=src[t].at[half] if from_src else blk, dst_ref=blk,
                send_sem=send_sems.at[6 * t + k], recv_sem=recv_sems.at[6 * t + k], device_id=to, device_id_type=MESH)

        return copy, x, y, c, chips

    def issue(src, out, sems):
        copy, x, y, c, chips = copies(src, out, sems)
        for t in range(n):
            for j, chip in enumerate(chips):
                copy(t, j, (x, y), c, (*chip, c), from_src=True).start()

    def drain(src, out, sems):
        copy, x, y, c, chips = copies(src, out, sems)
        passed = []
        for t in range(n):
            for j, chip in enumerate(chips):
                copy(t, j, chip, c, (x, y, c)).wait_recv()
                fwd = copy(t, 3 + j, chip, c, (x, y, 1 - c))
                fwd.start()
                passed.append(fwd)
        for t in range(n):
            for j, chip in enumerate(chips):
                copy(t, 3 + j, chip, 1 - c, (x, y, c)).wait_recv()
        for t in range(n):
            for j, chip in enumerate(chips):
                copy(t, j, (x, y), c, (*chip, c), from_src=True).wait_send()
        for cp in passed:
            cp.wait_send()

    return _Carried(shards, [jax.ShapeDtypeStruct((NCHIP,) + s.shape, s.dtype) for s in shards], (6 * n, 6 * n), issue, drain)


def _pair_exchange_copies(grads):
    n = len(grads)

    def copies(g, theirs, sems):
        send_sems, recv_sems = sems
        x, y, c, _ = _place()
        return [pltpu.make_async_remote_copy(src_ref=g[t].at[:, 1 - c], dst_ref=theirs[t], send_sem=send_sems.at[t],
                                             recv_sem=recv_sems.at[t], device_id=(x, y, 1 - c), device_id_type=MESH) for t in range(n)]

    def issue(g, theirs, sems):
        for cp in copies(g, theirs, sems):
            cp.start()

    def drain(g, theirs, sems):
        for cp in copies(g, theirs, sems):
            cp.wait()

    return _Carried(grads, [jax.ShapeDtypeStruct((NCHIP,) + g.shape[2:], g.dtype) for g in grads], (n, n), issue, drain)


def _chip_exchange_copies(items):
    flat = [(t, i, j) for t, (_, peers) in enumerate(items) for i, j in enumerate(peers)]

    def copies(p, slots, sems):
        send_sems, recv_sems = sems
        x, y, c, chips = _place()
        return [pltpu.make_async_remote_copy(src_ref=p[t].at[_chip_id(chips[j])], dst_ref=slots[t].at[i], send_sem=send_sems.at[k],
                                             recv_sem=recv_sems.at[k], device_id=(*chips[j], c), device_id_type=MESH)
                for k, (t, i, j) in enumerate(flat)]

    def issue(p, slots, sems):
        for cp in copies(p, slots, sems):
            cp.start()

    def drain(p, slots, sems):
        for cp in copies(p, slots, sems):
            cp.wait()

    return _Carried([p for p, _ in items], [jax.ShapeDtypeStruct((len(peers),) + p.shape[1:], p.dtype) for p, peers in items],
                    (len(flat), len(flat)), issue, drain)


def _pair_share(full):
    n = len(full)

    def body(*refs):
        buf = refs[n:2 * n]
        send_sems, recv_sems = refs[2 * n:]
        x, y, c, _ = _place()
        sends = [pltpu.make_async_remote_copy(src_ref=buf[t].at[c], dst_ref=buf[t].at[c], send_sem=send_sems.at[t], recv_sem=recv_sems.at[t],
                                              device_id=(x, y, 1 - c), device_id_type=MESH) for t in range(n)]
        for cp in sends:
            cp.start()
        for t in range(n):
            pltpu.make_async_remote_copy(src_ref=buf[t].at[c], dst_ref=buf[t].at[1 - c], send_sem=send_sems.at[t], recv_sem=recv_sems.at[t],
                                         device_id=(x, y, 1 - c), device_id_type=MESH).wait_recv()
        for cp in sends:
            cp.wait_send()

    return _comm_call("grad_pair_share", body, full, [jax.ShapeDtypeStruct(f.shape, f.dtype) for f in full], (n, n),
                      aliases={t: t for t in range(n)})


SMALL_ROWS = 128


def _allreduce_small(v):
    def body(v_ref, o_ref, buf, send_sems, recv_sems):
        x, y, c, _ = _place()
        me = 4 * x + 2 * y + c
        flip = lambda a, f: 1 - a if f else a
        buf[me] = v_ref[...]
        peers = [(flip(x, d >> 2 & 1), flip(y, d >> 1 & 1), flip(c, d & 1)) for d in range(1, 8)]
        sends = [pltpu.make_async_remote_copy(src_ref=v_ref, dst_ref=buf.at[me], send_sem=send_sems.at[i], recv_sem=recv_sems.at[i],
                                              device_id=peer, device_id_type=MESH) for i, peer in enumerate(peers)]
        for cp in sends:
            cp.start()
        for i, (px, py, pc) in enumerate(peers):
            pltpu.make_async_remote_copy(src_ref=v_ref, dst_ref=buf.at[4 * px + 2 * py + pc], send_sem=send_sems.at[i], recv_sem=recv_sems.at[i],
                                         device_id=(px, py, pc), device_id_type=MESH).wait_recv()
        for cp in sends:
            cp.wait_send()
        acc = buf[0]
        for k in range(1, 8):
            acc = acc + buf[k]
        o_ref[...] = acc

    vm = pl.BlockSpec(memory_space=pltpu.VMEM)
    return pl.pallas_call(
        body, in_specs=[vm], out_specs=vm, out_shape=jax.ShapeDtypeStruct((SMALL_ROWS, 128), F32),
        scratch_shapes=[pltpu.VMEM((8, SMALL_ROWS, 128), F32), pltpu.SemaphoreType.DMA((7,)), pltpu.SemaphoreType.DMA((7,))],
        compiler_params=pltpu.CompilerParams(has_side_effects=True), name="allreduce_small")(v)


def _row_block(rows, cols, budget=3 << 19):
    best = 8
    for bm in range(8, rows + 1, 8):
        if rows % bm == 0 and bm * cols * 4 <= budget:
            best = bm
    return best


def _pair_sum(name, place, gs, theirs):
    n = len(gs)
    _, m, c = theirs[0].shape
    bm = _row_block(m, c)

    def body(place_ref, *refs):
        for a_ref, b_ref, o_ref in zip(refs[:n], refs[n:2 * n], refs[2 * n:]):
            o_ref[...] = (a_ref[...].astype(F32) + b_ref[...].astype(F32)).astype(o_ref.dtype)

    spec = pl.BlockSpec((None, bm, c), lambda k, i, pr: (k, i, 0))
    return pl.pallas_call(
        body, out_shape=[jax.ShapeDtypeStruct(theirs[0].shape, BF16)] * n,
        grid_spec=pltpu.PrefetchScalarGridSpec(
            num_scalar_prefetch=1, grid=(NCHIP, m // bm),
            in_specs=[pl.BlockSpec((None, None, bm, c), lambda k, i, pr: (k, pr[0], i, 0))] * n + [spec] * n, out_specs=[spec] * n),
        compiler_params=_params(("parallel", "parallel")), name=name)(place, *gs, *theirs)


def _chip_sum(name, place, parts, slots):
    _, m, c = parts.shape
    bm = _row_block(m, c)

    def body(place_ref, p_ref, *refs):
        acc = p_ref[...].astype(F32)
        for s_ref in refs[:-1]:
            for i in range(s_ref.shape[0]):
                acc = acc + s_ref[i].astype(F32)
        refs[-1][...] = acc

    return pl.pallas_call(
        body, out_shape=jax.ShapeDtypeStruct((2, m, c), F32),
        grid_spec=pltpu.PrefetchScalarGridSpec(
            num_scalar_prefetch=1, grid=(m // bm,),
            in_specs=[pl.BlockSpec((None, bm, c), lambda i, pr: (pr[1], i, 0))]
            + [pl.BlockSpec((s.shape[0], bm, c), lambda i, pr: (0, i, 0)) for s in slots],
            out_specs=pl.BlockSpec((None, bm, c), lambda i, pr: (pr[0], i, 0))),
        compiler_params=_params(("parallel",)), name=name)(place, parts, *slots)


def _adamw(name, w, g, m, v, layer=0, into=None):
    lead, rows, cols = w.shape
    bm = _row_block(rows, cols, budget=768 * 1024)
    c1 = 1.0 - ADAM_B1 ** ADAM_STEP
    c2 = 1.0 - ADAM_B2 ** ADAM_STEP

    def body(w_ref, g_ref, m_ref, v_ref, *rest):
        go_ref, d_ref, mo_ref, vo_ref = rest[-4:]
        g = g_ref[...]
        mn = ADAM_B1 * m_ref[...] + (1.0 - ADAM_B1) * g
        vn = ADAM_B2 * v_ref[...] + (1.0 - ADAM_B2) * (g * g)
        go_ref[...] = g
        mo_ref[...] = mn
        vo_ref[...] = vn
        d_ref[...] = -ADAM_LR * ((mn / c1) / (jnp.sqrt(vn / c2) + ADAM_EPS) + ADAM_WD * w_ref[...])

    spec = pl.BlockSpec((None, bm, cols), lambda i: (layer, i, 0))
    sh = jax.ShapeDtypeStruct((lead, rows, cols), F32)
    prev = [] if into is None else list(into)
    return pl.pallas_call(
        body, grid=(rows // bm,), in_specs=[spec, pl.BlockSpec((bm, cols), lambda i: (i, 0)), spec, spec] + [pl.BlockSpec(memory_space=pl.ANY)] * len(prev),
        out_specs=[spec] * 4, out_shape=[sh] * 4, input_output_aliases={4 + k: k for k in range(len(prev))},
        compiler_params=_params(("parallel",)), name=name)(w, g, m, v, *prev)


def _pack_small(norms, rpb):
    flat = jnp.concatenate([a.reshape(-1) for a in norms] + [rpb.reshape(-1)])
    return jnp.pad(flat, (0, SMALL_ROWS * 128 - flat.shape[0])).reshape(SMALL_ROWS, 128)


def _unpack_small(p):
    flat = p.reshape(-1)
    norms = [flat[i * 2 * DM:(i + 1) * 2 * DM].reshape(2, DM) for i in range(4)]
    rpb = flat[8 * DM:8 * DM + NH * 15 * 31].reshape(1, NH, 15, 31)
    return norms, rpb


FFN_NAMES = ("ffn_w_gate", "ffn_w_up", "ffn_w_down")
L0_FFN = tuple((n, 0) for n in FFN_NAMES)
L1_FFN = tuple((n, 1) for n in FFN_NAMES)
NA_KEYS = (("na_w_qkv", 0), ("na_w_o", 0))
DIL_KEYS = (("dil_w_qkv", 0), ("dil_w_o", 0))
ALL_PEERS, NEIGHBOURS, DIAGONAL = (0, 1, 2), (0, 1), (2,)


class _Exchange:
    GATHERS = {"na_bias_tiles": NA_KEYS, "l0_qkv": L0_FFN[:1], "na_fwd": L0_FFN[1:], "l0_ffn_fwd": DIL_KEYS[:1], "dil_fwd": L1_FFN + DIL_KEYS[1:]}
    PAIRS = {"l1_proj_bwd": L1_FFN, "l1_dh": DIL_KEYS, "l0_proj_bwd": L0_FFN}
    EXCHANGES = {"dil_bwd": [(k, ALL_PEERS) for k in L1_FFN],
                 "l0_ffn_bwd": [(DIL_KEYS[0], NEIGHBOURS)],
                 "na_bwd": [(k, ALL_PEERS) for k in L0_FFN] + [(DIL_KEYS[0], DIAGONAL), (DIL_KEYS[1], ALL_PEERS)],
                 "l0_dh": [(k, NEIGHBOURS) for k in NA_KEYS],
                 "l0_mix_pre_bwd": [(k, DIAGONAL) for k in NA_KEYS]}

    def __init__(self, shards):
        self.chip = 2 * lax.axis_index("x") + lax.axis_index("y")
        self.place = jnp.stack([lax.axis_index("c"), self.chip]).astype(jnp.int32)
        self.own = {k: s.reshape(2, s.shape[0] // 2, s.shape[1]).astype(BF16) for k, s in shards.items()}
        self.gathered, self.mine, self.parts, self.slots, self.full = {}, {}, {}, {}, {}

    def _take(self, keys, landed):
        for k, gw in zip(keys, landed):
            self.gathered[k] = lax.dynamic_update_slice(gw, self.own[k][None], (self.chip, 0, 0, 0))

    def _sum(self, items, landed):
        for (k, peers), s in zip(items, landed):
            got = self.slots.setdefault(k, {})
            got[peers] = s
            if sum(len(p) for p in got) == len(ALL_PEERS):
                self.full[k] = _chip_sum(f"chip_sum_{k[0]}_{k[1]}", self.place, self.parts[k], [got[p] for p in sorted(got)])

    def weight(self, key):
        g = self.gathered[key]
        return g.reshape(NCHIP, 2 * g.shape[2], g.shape[3])

    def _pair_sums(self, keys, theirs):
        runs = []
        for k, t in zip(keys, theirs):
            if runs and runs[-1][0][1].shape == t.shape:
                runs[-1].append((k, t))
            else:
                runs.append([(k, t)])
        for run in runs:
            ks = [k for k, _ in run]
            sums = _pair_sum(f"pair_sum_{ks[0][0]}_{ks[0][1]}", self.place, [self.mine[k] for k in ks], [t for _, t in run])
            self.parts.update(zip(ks, sums))

    def carry(self, tag):
        if tag in self.GATHERS:
            return _gather_copies([self.own[k] for k in self.GATHERS[tag]])
        if tag in self.PAIRS:
            return _pair_exchange_copies([self.mine[k] for k in self.PAIRS[tag]])
        if tag in self.EXCHANGES:
            return _chip_exchange_copies([(self.parts[k], peers) for k, peers in self.EXCHANGES[tag]])
        return None

    def carried(self, tag, landed):
        if tag in self.GATHERS:
            self._take(self.GATHERS[tag], landed)
        elif tag in self.PAIRS:
            self._pair_sums(self.PAIRS[tag], landed)
        elif tag in self.EXCHANGES:
            self._sum(self.EXCHANGES[tag], landed)

    def grads(self, tag, dw):
        for k, g in dw.items():
            self.mine[k] = g.reshape(NCHIP, 2, -1, g.shape[-1])
        if tag == "l0_mix":
            keys = tuple(dw)
            self._pair_sums(keys, _run_carried("grad_pair_exchange_last", _pair_exchange_copies([self.mine[k] for k in keys])))

    def finish(self):
        keys = tuple(self.full)
        shared = _pair_share([self.full[k] for k in keys])
        return {k: s.reshape(2 * s.shape[1], s.shape[2]) for k, s in zip(keys, shared)}


def kernel(x, norm_mix_pre, norm_mix_post, norm_ffn_pre, norm_ffn_post, na_w_qkv, na_w_o, na_rpb, dil_w_qkv, dil_w_o, ffn_w_gate, ffn_w_up, ffn_w_down, loss_target, m_norm_mix_pre, m_norm_mix_post, m_norm_ffn_pre, m_norm_ffn_post, m_na_w_qkv, m_na_w_o, m_na_rpb, m_dil_w_qkv, m_dil_w_o, m_ffn_w_gate, m_ffn_w_up, m_ffn_w_down, v_norm_mix_pre, v_norm_mix_post, v_norm_ffn_pre, v_norm_ffn_post, v_na_w_qkv, v_na_w_o, v_na_rpb, v_dil_w_qkv, v_dil_w_o, v_ffn_w_gate, v_ffn_w_up, v_ffn_w_down):
    tr = lambda a: jnp.swapaxes(a, 1, 2)
    weights = {"na_w_qkv": na_w_qkv, "na_w_o": na_w_o, "dil_w_qkv": dil_w_qkv, "dil_w_o": dil_w_o,
               "ffn_w_gate": tr(ffn_w_gate), "ffn_w_up": tr(ffn_w_up), "ffn_w_down": ffn_w_down}
    m_in = {"na_w_qkv": m_na_w_qkv, "na_w_o": m_na_w_o, "dil_w_qkv": m_dil_w_qkv, "dil_w_o": m_dil_w_o,
            "ffn_w_gate": tr(m_ffn_w_gate), "ffn_w_up": tr(m_ffn_w_up), "ffn_w_down": m_ffn_w_down}
    v_in = {"na_w_qkv": v_na_w_qkv, "na_w_o": v_na_w_o, "dil_w_qkv": v_dil_w_qkv, "dil_w_o": v_dil_w_o,
            "ffn_w_gate": tr(v_ffn_w_gate), "ffn_w_up": tr(v_ffn_w_up), "ffn_w_down": v_ffn_w_down}

    ex = _Exchange({(n, l): weights[n][l] for n in weights for l in range(weights[n].shape[0])})
    norms = (norm_mix_pre, norm_mix_post, norm_ffn_pre, norm_ffn_post)
    loss_row, dx, dnorms, d_rpb = _local_step(x[0], loss_target[0], norms, na_rpb[0], ex)
    loss = lax.psum(loss_row[0, 0], ("x", "y", "c"))
    full = ex.finish()
    small = _allreduce_small(_pack_small(dnorms, d_rpb))

    out_g, out_d, out_m, out_v = {}, {}, {}, {}
    for n in weights:
        res = None
        for l in range(weights[n].shape[0]):
            res = _adamw(f"adamw_{n}_{l}", weights[n], full[(n, l)], m_in[n], v_in[n], l, res)
        if n in ("ffn_w_gate", "ffn_w_up"):
            res = [tr(r) for r in res]
        out_g[n], out_d[n], out_m[n], out_v[n] = res
    sm_names = ("norm_mix_pre", "norm_mix_post", "norm_ffn_pre", "norm_ffn_post", "na_rpb")
    sm = _adamw("adamw_small", _pack_small(norms, na_rpb)[None], small,
                _pack_small((m_norm_mix_pre, m_norm_mix_post, m_norm_ffn_pre, m_norm_ffn_post), m_na_rpb)[None],
                _pack_small((v_norm_mix_pre, v_norm_mix_post, v_norm_ffn_pre, v_norm_ffn_post), v_na_rpb)[None])
    for res, dst in zip(sm, (out_g, out_d, out_m, out_v)):
        ns, rp = _unpack_small(res)
        for n, a in zip(sm_names, ns + [rp]):
            dst[n] = a

    order = ("norm_mix_pre", "norm_mix_post", "norm_ffn_pre", "norm_ffn_post", "na_w_qkv", "na_w_o", "na_rpb", "dil_w_qkv", "dil_w_o",
             "ffn_w_gate", "ffn_w_up", "ffn_w_down")
    return (loss, dx[None], *[out_g[n] for n in order], *[out_d[n] for n in order], *[out_m[n] for n in order], *[out_v[n] for n in order])
```

```python
import functools

import numpy as np
import jax
import jax.numpy as jnp
from jax import lax
from jax.experimental import pallas as pl
from jax.experimental.pallas import tpu as pltpu

F32 = jnp.float32
BF16 = jnp.bfloat16

SEQ = 2048
DM = 1024
NH = 16
HD = 64
DFF = 2816
NCHIP = 4
FSH = DFF // NCHIP
GRID_W = 64
NA_QROWS = 4
NA_QB = NA_QROWS * GRID_W
NA_WROWS = 12
NA_WIN = NA_WROWS * GRID_W
DIL = (1, 4, 16)
DIL_QB = 256
DIL_WIN = DIL_QB + 128
DIL_RADIUS = 64
RMS_EPS = 1e-6
NEG = -1e30
QSCALE = HD ** -0.5
CH = 256
MESH = pl.DeviceIdType.MESH

ADAM_LR, ADAM_B1, ADAM_B2, ADAM_EPS, ADAM_WD, ADAM_STEP = 0.001, 0.9, 0.999, 1e-08, 0.01, 10

VMEM_LIMIT = 56 * 1024 * 1024

_NN = (((1,), (0,)), ((), ()))
_NT = (((1,), (1,)), ((), ()))
_TN = (((0,), (0,)), ((), ()))


def _params(sem):
    return pltpu.CompilerParams(dimension_semantics=sem, vmem_limit_bytes=VMEM_LIMIT)


def _matmul(name, pairs, grid, out_shape, out_spec, acc_shape, carrying=False, carry=None):
    nk = grid[-1]
    npair = len(pairs)
    n_in = 2 * npair

    def body(*refs):
        ins, o_ref = refs[:2 * npair], refs[n_in]
        part = None
        for p in range(npair):
            d = lax.dot_general(ins[2 * p][...].astype(BF16), ins[2 * p + 1][...].astype(BF16), pairs[p][4],
                                preferred_element_type=F32)
            part = d if part is None else part + d
        if nk == 1:
            o_ref[...] = part.astype(o_ref.dtype)
        else:
            acc_ref = refs[n_in + 1]
            kk = pl.program_id(len(grid) - 1)

            @pl.when(kk == 0)
            def _():
                acc_ref[...] = part

            @pl.when(kk > 0)
            def _():
                acc_ref[...] += part

            @pl.when(kk == nk - 1)
            def _():
                o_ref[...] = acc_ref[...].astype(o_ref.dtype)

    ops, specs = [], []
    for a, a_spec, b, b_spec, _ in pairs:
        ops += [a, b]
        specs += [a_spec, b_spec]
    (out,), sent = _carrier_call(name, body, grid, specs, [out_spec], [out_shape], [] if nk == 1 else [pltpu.VMEM(acc_shape, F32)], ops, carry)
    return (out, sent) if carrying else out


def _qkv_fwd(name, h_all, w4, carry):
    g_n = h_all.shape[0]
    per = w4.shape[2] // CH
    return _matmul(
        name, [(h_all, pl.BlockSpec((None, SEQ, DM), lambda g, q, k: (g, 0, 0)),
                w4, pl.BlockSpec((None, DM, CH), lambda g, q, k: ((g * 12 + q) // per, 0, (g * 12 + q) % per)), _NN)],
        (g_n, 12, 1), jax.ShapeDtypeStruct((g_n, SEQ, 3 * DM), BF16),
        pl.BlockSpec((None, SEQ, CH), lambda g, q, k: (g, 0, q)), None, carrying=True, carry=carry)


def _qkv_bwd_dh(name, dqkv, w4, carry):
    g_n = dqkv.shape[0]
    per = w4.shape[2] // CH
    tm = SEQ

    def pair(cb):
        chunk = lambda g, t: g * 12 + t * 4 + cb
        return (dqkv, pl.BlockSpec((None, None, tm, CH), lambda g, i, t: (g, t, i, cb)),
                w4, pl.BlockSpec((None, DM, CH), lambda g, i, t: (chunk(g, t) // per, 0, chunk(g, t) % per)), _NT)

    return _matmul(name, [pair(cb) for cb in range(4)], (g_n, SEQ // tm, 3), jax.ShapeDtypeStruct((g_n, SEQ, DM), F32),
                   pl.BlockSpec((None, tm, DM), lambda g, i, t: (g, i, 0)), (tm, DM), carrying=True, carry=carry)


def _qkv_bwd_dw(name, ht_all, dqkv, shard_cols):
    g_n = dqkv.shape[0]
    per = shard_cols // CH
    return _matmul(
        name, [(ht_all, pl.BlockSpec((None, DM, SEQ), lambda qq, k: (qq // 12, 0, 0)),
                dqkv, pl.BlockSpec((None, None, SEQ, CH), lambda qq, k: (qq // 12, (qq % 12) // 4, 0, qq % 4)), _NN)],
        (g_n * 12, 1), jax.ShapeDtypeStruct((NCHIP, DM, shard_cols), BF16),
        pl.BlockSpec((None, DM, CH), lambda qq, k: (qq // per, 0, qq % per)), None)


def _proj_fwd(name, o, wo, x, g):
    tm = 512

    def body(o_ref, w_ref, x_ref, g_ref, xn_ref, u_ref):
        u = jnp.dot(o_ref[...], w_ref[...], preferred_element_type=F32)
        u_ref[...] = u
        r = lax.rsqrt(jnp.mean(u * u, axis=-1, keepdims=True) + RMS_EPS)
        xn_ref[...] = x_ref[...] + u * r * g_ref[...]

    rows = pl.BlockSpec((tm, DM), lambda i: (i, 0))
    sh = jax.ShapeDtypeStruct((SEQ, DM), F32)
    return pl.pallas_call(
        body, grid=(SEQ // tm,), in_specs=[rows, pl.BlockSpec((DM, DM), lambda i: (0, 0)), rows, pl.BlockSpec((1, DM), lambda i: (0, 0))],
        out_specs=[rows, rows], out_shape=[sh, sh], compiler_params=_params(("parallel",)), name=name)(o, wo, x, g)


def _proj_bwd(name, dy, u, g, wo, dtype, carry):
    tm = 512

    def body(dy_ref, u_ref, g_ref, w_ref, do_ref, du_ref, dg_ref):
        dy = dy_ref[...]
        u = u_ref[...]
        r = lax.rsqrt(jnp.mean(u * u, axis=-1, keepdims=True) + RMS_EPS)
        yh = u * r
        t = dy * g_ref[...]
        du = (r * (t - yh * jnp.mean(t * yh, axis=-1, keepdims=True))).astype(BF16)
        du_ref[...] = du
        do_ref[...] = lax.dot_general(du, w_ref[...], _NT, preferred_element_type=F32).astype(do_ref.dtype)

        @pl.when(pl.program_id(0) == 0)
        def _():
            dg_ref[...] = jnp.zeros_like(dg_ref)

        dg_ref[...] += jnp.sum(dy * yh, axis=0, keepdims=True)

    rows = pl.BlockSpec((tm, DM), lambda i: (i, 0))
    vec = pl.BlockSpec((1, DM), lambda i: (0, 0))
    return _carrier_call(
        name, body, (SEQ // tm,), [rows, rows, vec, pl.BlockSpec((DM, DM), lambda i: (0, 0))], [rows, rows, vec],
        [jax.ShapeDtypeStruct((SEQ, DM), dtype), jax.ShapeDtypeStruct((SEQ, DM), BF16), jax.ShapeDtypeStruct((1, DM), F32)],
        [], (dy, u, g, wo), carry)


def _proj_bwd_dw(name, o, du):
    tn = 512
    return _matmul(
        name, [(o, pl.BlockSpec((SEQ, DM), lambda j, k: (0, 0)), du, pl.BlockSpec((SEQ, tn), lambda j, k: (0, j)), _TN)],
        (DM // tn, 1), jax.ShapeDtypeStruct((DM, DM), BF16), pl.BlockSpec((DM, tn), lambda j, k: (0, j)), None)


def _ffn_wspec(index_map):
    return pl.BlockSpec((None, FSH, DM), index_map)


def _ffn_bwd_dw(name, a4, b):
    return _matmul(
        name, [(a4, pl.BlockSpec((None, SEQ, FSH), lambda s, k: (s, 0, 0)), b, pl.BlockSpec((SEQ, DM), lambda s, k: (0, 0)), _TN)],
        (NCHIP, 1), jax.ShapeDtypeStruct((NCHIP, FSH, DM), BF16), _ffn_wspec(lambda s, k: (s, 0, 0)), None)


ROWS = 256


def _row_spec():
    return pl.BlockSpec((ROWS, DM), lambda i: (i, 0))


def _vec_spec():
    return pl.BlockSpec((1, DM), lambda i: (0, 0))


def _rms_fwd(name, x, g, dtype=BF16):
    def body(x_ref, g_ref, o_ref):
        x = x_ref[...]
        r = lax.rsqrt(jnp.mean(x * x, axis=-1, keepdims=True) + RMS_EPS)
        o_ref[...] = (x * r * g_ref[...]).astype(o_ref.dtype)

    return pl.pallas_call(body, grid=(SEQ // ROWS,), in_specs=[_row_spec(), _vec_spec()], out_specs=_row_spec(),
                          out_shape=jax.ShapeDtypeStruct((SEQ, DM), dtype), compiler_params=_params(("parallel",)), name=name)(x, g)


def _rms_fwd_both(name, x, g):
    def body(x_ref, g_ref, o_ref, t_ref):
        x = x_ref[...]
        r = lax.rsqrt(jnp.mean(x * x, axis=-1, keepdims=True) + RMS_EPS)
        h = x * r * g_ref[...]
        o_ref[...] = h.astype(o_ref.dtype)
        t_ref[...] = h.T.astype(t_ref.dtype)

    return pl.pallas_call(
        body, grid=(SEQ // ROWS,), in_specs=[_row_spec(), _vec_spec()], out_specs=[_row_spec(), pl.BlockSpec((DM, ROWS), lambda i: (0, i))],
        out_shape=[jax.ShapeDtypeStruct((SEQ, DM), BF16), jax.ShapeDtypeStruct((DM, SEQ), BF16)],
        compiler_params=_params(("parallel",)), name=name)(x, g)


def _norm_bwd(name, dy, u, g, res, carry):
    def body(dy_ref, u_ref, g_ref, res_ref, du_ref, dg_ref):
        dy = dy_ref[...]
        u = u_ref[...]
        r = lax.rsqrt(jnp.mean(u * u, axis=-1, keepdims=True) + RMS_EPS)
        yh = u * r
        t = dy * g_ref[...]
        du_ref[...] = r * (t - yh * jnp.mean(t * yh, axis=-1, keepdims=True)) + res_ref[...]

        @pl.when(pl.program_id(0) == 0)
        def _():
            dg_ref[...] = jnp.zeros_like(dg_ref)

        dg_ref[...] += jnp.sum(dy * yh, axis=0, keepdims=True)

    return _carrier_call(
        name, body, (SEQ // ROWS,), [_row_spec(), _row_spec(), _vec_spec(), _row_spec()], [_row_spec(), _vec_spec()],
        [jax.ShapeDtypeStruct((SEQ, DM), F32), jax.ShapeDtypeStruct((1, DM), F32)], [], (dy, u, g, res), carry)


def _loss_grad(name, y, t):
    def body(y_ref, t_ref, dy_ref, l_ref):
        e = y_ref[...] - t_ref[...]
        dy_ref[...] = e * (1.0 / DM)

        @pl.when(pl.program_id(0) == 0)
        def _():
            l_ref[...] = jnp.zeros_like(l_ref)

        l_ref[...] += jnp.sum(e * e) * (0.5 / DM)

    return pl.pallas_call(
        body, grid=(SEQ // ROWS,), in_specs=[_row_spec(), _row_spec()],
        out_specs=[_row_spec(), pl.BlockSpec((1, 128), lambda i: (0, 0))],
        out_shape=[jax.ShapeDtypeStruct((SEQ, DM), F32), jax.ShapeDtypeStruct((1, 128), F32)],
        compiler_params=_params(("arbitrary",)), name=name)(y, t)


HBM_SPEC = pl.BlockSpec(memory_space=pltpu.HBM)


class _Carried:
    def __init__(self, ins, out_shapes, n_sems, issue, drain):
        self.ins, self.out_shapes, self.n_sems, self.issue, self.drain = list(ins), list(out_shapes), tuple(n_sems), issue, drain


def _carrier_call(name, body, grid, in_specs, out_specs, out_shape, scratch_shapes, operands, carry):
    n_in, n_out, n_scr = len(in_specs), len(out_specs), len(scratch_shapes)
    if carry is None:
        res = pl.pallas_call(body, grid=grid, in_specs=in_specs, out_specs=out_specs, out_shape=out_shape, scratch_shapes=scratch_shapes,
                             compiler_params=_params(("arbitrary",) * len(grid)), name=name)(*operands)
        return list(res), []
    ci, co = len(carry.ins), len(carry.out_shapes)

    def wrapped(*refs):
        ins, cins = refs[:n_in], refs[n_in:n_in + ci]
        outs, couts = refs[n_in + ci:n_in + ci + n_out], refs[n_in + ci + n_out:n_in + ci + n_out + co]
        scr, sems = refs[n_in + ci + n_out + co:n_in + ci + n_out + co + n_scr], refs[n_in + ci + n_out + co + n_scr:]
        first = functools.reduce(jnp.logical_and, [pl.program_id(a) == 0 for a in range(len(grid))])
        last = functools.reduce(jnp.logical_and, [pl.program_id(a) == grid[a] - 1 for a in range(len(grid))])

        @pl.when(first)
        def _():
            carry.issue(cins, couts, sems)

        body(*ins, *outs, *scr)

        @pl.when(last)
        def _():
            carry.drain(cins, couts, sems)

    res = pl.pallas_call(
        wrapped, grid=grid, in_specs=list(in_specs) + [HBM_SPEC] * ci, out_specs=list(out_specs) + [HBM_SPEC] * co,
        out_shape=list(out_shape) + carry.out_shapes,
        scratch_shapes=list(scratch_shapes) + [pltpu.SemaphoreType.DMA((k,)) for k in carry.n_sems],
        compiler_params=pltpu.CompilerParams(dimension_semantics=("arbitrary",) * len(grid), vmem_limit_bytes=VMEM_LIMIT, has_side_effects=True),
        name=name)(*operands, *carry.ins)
    return list(res[:n_out]), list(res[n_out:])


def _run_carried(name, carry):
    def body(*refs):
        ci, co = len(carry.ins), len(carry.out_shapes)
        carry.issue(refs[:ci], refs[ci:ci + co], refs[ci + co:])
        carry.drain(refs[:ci], refs[ci:ci + co], refs[ci + co:])

    return pl.pallas_call(
        body, in_specs=[HBM_SPEC] * len(carry.ins), out_specs=[HBM_SPEC] * len(carry.out_shapes), out_shape=carry.out_shapes,
        scratch_shapes=[pltpu.SemaphoreType.DMA((k,)) for k in carry.n_sems],
        compiler_params=pltpu.CompilerParams(has_side_effects=True), name=name)(*carry.ins)


NA_BLOCKS = SEQ // NA_QB
NA_ROWS_TOTAL = SEQ // GRID_W
NA_CLASSES = ((0, 0), (8, 4), (NA_ROWS_TOTAL - NA_QROWS, NA_ROWS_TOTAL - NA_WROWS))


def _na_pairs(i0, ws):
    out = []
    for qi in range(NA_QROWS):
        i = i0 + qi
        rs = min(max(i - 4, 0), NA_ROWS_TOTAL - 8)
        for kr in range(NA_WROWS):
            r = ws + kr
            if rs <= r < rs + 8:
                out.append((qi, kr, r - i + 7))
    return out


def _diag_onehot():
    qc, kc = np.meshgrid(np.arange(GRID_W), np.arange(GRID_W), indexing="ij")
    e = np.zeros((GRID_W * GRID_W, 128), np.float32)
    j = (kc - qc + 15).reshape(-1)
    ok = (j >= 0) & (j <= 30)
    e[np.arange(GRID_W * GRID_W)[ok], j[ok]] = 1.0
    return jnp.asarray(e)


def _rpb_expand(rpb):
    r2 = jnp.pad(rpb.reshape(NH * 15, 31), ((0, 0), (0, 128 - 31)))

    def body(r_ref, e_ref, o_ref):
        o_ref[...] = lax.dot_general(r_ref[...], e_ref[...], _NT, preferred_element_type=F32, precision=lax.Precision.HIGHEST)

    out = pl.pallas_call(body, out_shape=jax.ShapeDtypeStruct((NH * 15, GRID_W * GRID_W), F32), name="rpb_expand",
                         compiler_params=pltpu.CompilerParams(vmem_limit_bytes=VMEM_LIMIT))(r2, _diag_onehot())
    return out.reshape(NH, 15, GRID_W, GRID_W)


def _na_bias_tiles(rpb, carry):
    def body(b_ref, o_ref):
        qc = lax.broadcasted_iota(jnp.int32, (GRID_W, GRID_W), 0)
        kc = lax.broadcasted_iota(jnp.int32, (GRID_W, GRID_W), 1)
        first = jnp.clip(qc - 8, 0, GRID_W - 16)
        in_window = (kc >= first) & (kc < first + 16)
        neg = jnp.full((GRID_W, GRID_W), NEG, F32)
        for cls, (i0, ws) in enumerate(NA_CLASSES):
            @pl.when(pl.program_id(0) == cls)
            def _(i0=i0, ws=ws):
                pairs = {(qi, kr): dr for qi, kr, dr in _na_pairs(i0, ws)}
                masked = {dr: jnp.where(in_window, b_ref[dr], NEG) for dr in sorted(set(pairs.values()))}
                for qi in range(NA_QROWS):
                    for k2 in range(NA_WROWS // 2):
                        blocks = [masked[pairs[(qi, kr)]] if (qi, kr) in pairs else neg for kr in (2 * k2, 2 * k2 + 1)]
                        o_ref[qi * GRID_W:(qi + 1) * GRID_W, k2 * 128:(k2 + 1) * 128] = jnp.concatenate(blocks, axis=1)

    (tiles,), sent = _carrier_call(
        "na_bias_tiles", body, (3, NH), [pl.BlockSpec((None, 15, GRID_W, GRID_W), lambda c, h: (h, 0, 0, 0))],
        [pl.BlockSpec((None, None, NA_QB, NA_WIN), lambda c, h: (c, h, 0, 0))], [jax.ShapeDtypeStruct((3, NH, NA_QB, NA_WIN), F32)],
        [], (_rpb_expand(rpb),), carry)
    return tiles, sent


def _na_cls(b):
    return jnp.where(b == 0, 0, jnp.where(b == NA_BLOCKS - 1, 2, 1))


def _na_start(b):
    return pl.multiple_of(jnp.clip(b * NA_QROWS - 4, 0, NA_ROWS_TOTAL - NA_WROWS) * GRID_W, GRID_W)


HPS = 4
LW = HPS * HD
NLW = DM // LW


NA_BWD_HPS = 4


def _na_in_specs(hps=HPS):
    lw = hps * HD
    nlw = DM // lw
    return [pl.BlockSpec((NA_QB, lw), lambda hp, b: (b, hp)),
            pl.BlockSpec((SEQ, lw), lambda hp, b: (0, nlw + hp)),
            pl.BlockSpec((SEQ, lw), lambda hp, b: (0, 2 * nlw + hp)),
            pl.BlockSpec((None, hps, NA_QB, NA_WIN), lambda hp, b: (_na_cls(b), hp, 0, 0))]


def _na_fwd(qkv, bias, carry):
    def body(q_ref, k_ref, v_ref, b_ref, o_ref):
        start = _na_start(pl.program_id(1))
        q = q_ref[...]
        kw = k_ref[pl.ds(start, NA_WIN), :]
        vw = v_ref[pl.ds(start, NA_WIN), :]
        outs = []
        for hh in range(HPS):
            sl = slice(hh * HD, (hh + 1) * HD)
            s = lax.dot_general(q[:, sl] * QSCALE, kw[:, sl], _NT, preferred_element_type=F32) + b_ref[hh]
            p = jnp.exp(s - jnp.max(s, axis=-1, keepdims=True))
            l = jnp.sum(p, axis=-1, keepdims=True)
            outs.append(jnp.dot(p.astype(BF16), vw[:, sl], preferred_element_type=F32) / l)
        o_ref[...] = jnp.concatenate(outs, axis=1).astype(o_ref.dtype)

    (o,), sent = _carrier_call(
        "na_fwd", body, (NLW, NA_BLOCKS), _na_in_specs(), [pl.BlockSpec((NA_QB, LW), lambda hp, b: (b, hp))],
        [jax.ShapeDtypeStruct((SEQ, DM), BF16)], [], (qkv, qkv, qkv, bias), carry)
    return o, sent


def _na_bwd(qkv, bias, do, carry):
    lw = NA_BWD_HPS * HD

    def body(q_ref, k_ref, v_ref, b_ref, do_ref, dqkv_ref, z_ref, dk_acc, dv_acc):
        blk = pl.program_id(1)

        @pl.when(blk == 0)
        def _():
            dk_acc[...] = jnp.zeros_like(dk_acc)
            dv_acc[...] = jnp.zeros_like(dv_acc)
            z_ref[...] = jnp.zeros_like(z_ref)

        start = _na_start(blk)
        q = q_ref[...]
        do = do_ref[...]
        kw = k_ref[pl.ds(start, NA_WIN), :]
        vw = v_ref[pl.ds(start, NA_WIN), :]
        dqs, dks, dvs, dss = [], [], [], []
        for hh in range(NA_BWD_HPS):
            sl = slice(hh * HD, (hh + 1) * HD)
            qh = q[:, sl] * QSCALE
            s = lax.dot_general(qh, kw[:, sl], _NT, preferred_element_type=F32) + b_ref[hh]
            p = jnp.exp(s - jnp.max(s, axis=-1, keepdims=True))
            p = p / jnp.sum(p, axis=-1, keepdims=True)
            dp = lax.dot_general(do[:, sl], vw[:, sl], _NT, preferred_element_type=F32)
            ds = p * (dp - jnp.sum(p * dp, axis=-1, keepdims=True))
            dsb = ds.astype(BF16)
            dqs.append(jnp.dot(dsb, kw[:, sl], preferred_element_type=F32) * QSCALE)
            dks.append(lax.dot_general(qh, dsb, _TN, preferred_element_type=F32).T)
            dvs.append(lax.dot_general(do[:, sl], p.astype(BF16), _TN, preferred_element_type=F32).T)
            dss.append(ds)
        for cls, (i0, ws) in enumerate(NA_CLASSES):
            @pl.when(_na_cls(blk) == cls)
            def _(i0=i0, ws=ws):
                for hh, ds in enumerate(dss):
                    for qi, kr, dr in _na_pairs(i0, ws):
                        z_ref[hh, dr * GRID_W:(dr + 1) * GRID_W, :] += ds[qi * GRID_W:(qi + 1) * GRID_W, kr * GRID_W:(kr + 1) * GRID_W]
        dqkv_ref[0, pl.ds(pl.multiple_of(blk * NA_QB, NA_QB), NA_QB), :] = jnp.concatenate(dqs, axis=1).astype(dqkv_ref.dtype)
        dk_acc[pl.ds(start, NA_WIN), :] += jnp.concatenate(dks, axis=1)
        dv_acc[pl.ds(start, NA_WIN), :] += jnp.concatenate(dvs, axis=1)

        @pl.when(blk == NA_BLOCKS - 1)
        def _():
            dqkv_ref[1] = dk_acc[...].astype(dqkv_ref.dtype)
            dqkv_ref[2] = dv_acc[...].astype(dqkv_ref.dtype)

    (dqkv, z), sent = _carrier_call(
        "na_bwd", body, (NH // NA_BWD_HPS, NA_BLOCKS),
        _na_in_specs(NA_BWD_HPS) + [pl.BlockSpec((NA_QB, lw), lambda hp, b: (b, hp))],
        [pl.BlockSpec((3, SEQ, lw), lambda hp, b: (0, 0, hp)), pl.BlockSpec((NA_BWD_HPS, 15 * GRID_W, GRID_W), lambda hp, b: (hp, 0, 0))],
        [jax.ShapeDtypeStruct((3, SEQ, DM), BF16), jax.ShapeDtypeStruct((NH, 15 * GRID_W, GRID_W), F32)],
        [pltpu.VMEM((SEQ, lw), F32), pltpu.VMEM((SEQ, lw), F32)], (qkv, qkv, qkv, bias, do), carry)
    return dqkv, z, sent


def _rpb_grad(z):
    z2 = z.reshape(NH * 15, GRID_W * GRID_W)

    def body(z_ref, e_ref, o_ref):
        o_ref[...] = jnp.dot(z_ref[...], e_ref[...], preferred_element_type=F32, precision=lax.Precision.HIGHEST)

    out = pl.pallas_call(body, out_shape=jax.ShapeDtypeStruct((NH * 15, 128), F32), name="rpb_grad",
                         compiler_params=pltpu.CompilerParams(vmem_limit_bytes=VMEM_LIMIT))(z2, _diag_onehot())
    return out[:, :31].reshape(NH, 15, 31)


DIL_BLOCKS = SEQ // DIL_QB
DIL_HPS = 8
DIL_LW = DIL_HPS * HD
DIL_NLW = DM // DIL_LW


COLS = 128


def _col_spec():
    return pl.BlockSpec((SEQ, COLS), lambda j: (0, j))


def _grp_spec():
    return pl.BlockSpec((3, SEQ, COLS), lambda j: (0, 0, j))


def _store_group_order(dst_ref, src_ref):
    for g, d in enumerate(DIL):
        n = SEQ // d
        for r in range(d):
            dst_ref[g, r * n:(r + 1) * n, :] = src_ref[pl.ds(r, n, stride=d), :].astype(dst_ref.dtype)


def _store_token_order(dst_ref, src_ref, g):
    d = DIL[g]
    n = SEQ // d
    for r in range(d):
        dst_ref[pl.ds(r, n, stride=d), :] = src_ref[g, r * n:(r + 1) * n, :]


def _to_groups(name, a):
    def body(a_ref, o_ref, t_ref):
        _store_group_order(o_ref, a_ref)
        for g in range(3):
            t_ref[g] = o_ref[g].astype(F32).T.astype(t_ref.dtype)

    return pl.pallas_call(
        body, grid=(DM // COLS,), in_specs=[_col_spec()], out_specs=[_grp_spec(), pl.BlockSpec((3, COLS, SEQ), lambda j: (0, j, 0))],
        out_shape=[jax.ShapeDtypeStruct((3, SEQ, DM), BF16), jax.ShapeDtypeStruct((3, DM, SEQ), BF16)],
        compiler_params=_params(("parallel",)), name=name)(a)


def _from_groups_sum(name, a):
    def body(a_ref, o_ref, t1, t2):
        _store_token_order(t1, a_ref, 1)
        _store_token_order(t2, a_ref, 2)
        o_ref[...] = (a_ref[0] + t1[...]) + t2[...]

    return pl.pallas_call(body, grid=(DM // COLS,), in_specs=[_grp_spec()], out_specs=_col_spec(),
                          out_shape=jax.ShapeDtypeStruct((SEQ, DM), F32), scratch_shapes=[pltpu.VMEM((SEQ, COLS), F32)] * 2,
                          compiler_params=_params(("parallel",)), name=name)(a)


def _dil_start(b):
    return pl.multiple_of(jnp.clip(b * DIL_QB - DIL_RADIUS, 0, SEQ - DIL_WIN), DIL_RADIUS)


def _dil_mask(g, b, start):
    shift = 11 - 2 * g
    ii = b * DIL_QB + lax.broadcasted_iota(jnp.int32, (DIL_QB, DIL_WIN), 0)
    jj = start + lax.broadcasted_iota(jnp.int32, (DIL_QB, DIL_WIN), 1)
    dist = jnp.abs(ii - jj)
    valid = (dist <= DIL_RADIUS) & (jnp.right_shift(ii, shift) == jnp.right_shift(jj, shift))
    return valid, dist.astype(F32)


def _dil_in_specs():
    return [pl.BlockSpec(memory_space=pltpu.SMEM),
            pl.BlockSpec((None, DIL_QB, DIL_LW), lambda g, hp, b: (g, b, hp)),
            pl.BlockSpec((None, SEQ, DIL_LW), lambda g, hp, b: (g, 0, DIL_NLW + hp)),
            pl.BlockSpec((None, SEQ, DIL_LW), lambda g, hp, b: (g, 0, 2 * DIL_NLW + hp))]


def _dil_fwd(qkv, slopes, carry):
    def body(sl_ref, q_ref, k_ref, v_ref, o_ref, lse_ref):
        g, hp, b = pl.program_id(0), pl.program_id(1), pl.program_id(2)
        start = _dil_start(b)
        valid, dist = _dil_mask(g, b, start)
        dil = jnp.left_shift(1, 2 * g).astype(F32)
        q = q_ref[...]
        kw = k_ref[pl.ds(start, DIL_WIN), :]
        vw = v_ref[pl.ds(start, DIL_WIN), :]
        outs, lses = [], []
        for hh in range(DIL_HPS):
            sl = slice(hh * HD, (hh + 1) * HD)
            s = lax.dot_general(q[:, sl] * QSCALE, kw[:, sl], _NT, preferred_element_type=F32)
            s = jnp.where(valid, s - (sl_ref[hp * DIL_HPS + hh] * dil) * dist, NEG)
            m = jnp.max(s, axis=-1, keepdims=True)
            p = jnp.exp(s - m)
            l = jnp.sum(p, axis=-1, keepdims=True)
            outs.append(jnp.dot(p.astype(BF16), vw[:, sl], preferred_element_type=F32) / l)
            lses.append(jnp.broadcast_to(m + jnp.log(l), (DIL_QB, HD)))
        o_ref[...] = jnp.concatenate(outs, axis=1)
        lse_ref[...] = jnp.concatenate(lses, axis=1)

    ospec = pl.BlockSpec((None, DIL_QB, DIL_LW), lambda g, hp, b: (g, b, hp))
    sh = jax.ShapeDtypeStruct((3, SEQ, DM), F32)
    (o, lse), sent = _carrier_call("dil_fwd", body, (3, DIL_NLW, DIL_BLOCKS), _dil_in_specs(), [ospec, ospec], [sh, sh], [],
                                   (slopes, qkv, qkv, qkv), carry)
    return o, lse, sent


def _dil_merge(o_all, lse_all):
    def body(o_ref, l_ref, out_ref, lse_ref, o1, o2, l1, l2):
        for g, (ot, lt) in ((1, (o1, l1)), (2, (o2, l2))):
            _store_token_order(ot, o_ref, g)
            _store_token_order(lt, l_ref, g)
        la, lb, lc = l_ref[0], l1[...], l2[...]
        m = jnp.maximum(jnp.maximum(la, lb), lc)
        wa, wb, wc = jnp.exp(la - m), jnp.exp(lb - m), jnp.exp(lc - m)
        sw = (wa + wb) + wc
        out_ref[...] = (((wa * o_ref[0] + wb * o1[...]) + wc * o2[...]) / sw).astype(out_ref.dtype)
        lse_ref[...] = m + jnp.log(sw)

    return pl.pallas_call(
        body, grid=(DM // COLS,), in_specs=[_grp_spec(), _grp_spec()], out_specs=[_col_spec(), _col_spec()],
        out_shape=[jax.ShapeDtypeStruct((SEQ, DM), BF16), jax.ShapeDtypeStruct((SEQ, DM), F32)],
        scratch_shapes=[pltpu.VMEM((SEQ, COLS), F32)] * 4, compiler_params=_params(("parallel",)), name="dil_merge")(o_all, lse_all)


def _dil_bwd_prep(do, o, lse):
    heads = COLS // HD

    def body(do_ref, o_ref, lse_ref, dog_ref, ddr_ref, lser_ref, dd, grp):
        prod = do_ref[...] * o_ref[...].astype(F32)
        dd[...] = jnp.concatenate(
            [jnp.broadcast_to(jnp.sum(prod[:, h * HD:(h + 1) * HD], axis=-1, keepdims=True), (SEQ, HD)) for h in range(heads)], axis=1)
        _store_group_order(dog_ref, do_ref)
        for src, dst in ((dd, ddr_ref), (lse_ref, lser_ref)):
            _store_group_order(grp, src)
            for g in range(3):
                t = grp[g].T
                for h in range(heads):
                    dst[g, h] = t[h * HD:h * HD + 8, :]

    rows = jax.ShapeDtypeStruct((3, NH, 8, SEQ), F32)
    rspec = pl.BlockSpec((3, heads, 8, SEQ), lambda j: (0, j, 0, 0))
    return pl.pallas_call(
        body, grid=(DM // COLS,), in_specs=[_col_spec()] * 3, out_specs=[_grp_spec(), rspec, rspec],
        out_shape=[jax.ShapeDtypeStruct((3, SEQ, DM), BF16), rows, rows],
        scratch_shapes=[pltpu.VMEM((SEQ, COLS), F32), pltpu.VMEM((3, SEQ, COLS), F32)],
        compiler_params=_params(("parallel",)), name="dil_bwd_prep")(do, o, lse)


def _dil_bwd(qkv, do, dd, lse, slopes, carry):
    def body(sl_ref, q_ref, k_ref, v_ref, do_ref, dd_ref, lse_ref, dqkv_ref, dk_acc, dv_acc):
        g, hp, b = pl.program_id(0), pl.program_id(1), pl.program_id(2)

        @pl.when(b == 0)
        def _():
            dk_acc[...] = jnp.zeros_like(dk_acc)
            dv_acc[...] = jnp.zeros_like(dv_acc)

        start = _dil_start(b)
        shift = 11 - 2 * g
        jj = start + lax.broadcasted_iota(jnp.int32, (DIL_WIN, DIL_QB), 0)
        ii = b * DIL_QB + lax.broadcasted_iota(jnp.int32, (DIL_WIN, DIL_QB), 1)
        dist = jnp.abs(ii - jj)
        valid = (dist <= DIL_RADIUS) & (jnp.right_shift(ii, shift) == jnp.right_shift(jj, shift))
        dist = dist.astype(F32)
        dil = jnp.left_shift(1, 2 * g).astype(F32)
        q = q_ref[...]
        do = do_ref[...]
        kw = k_ref[pl.ds(start, DIL_WIN), :]
        vw = v_ref[pl.ds(start, DIL_WIN), :]
        dqs, dks, dvs = [], [], []
        for hh in range(DIL_HPS):
            sl = slice(hh * HD, (hh + 1) * HD)
            qh = q[:, sl] * QSCALE
            st = lax.dot_general(kw[:, sl], qh, _NT, preferred_element_type=F32)
            st = jnp.where(valid, st - (sl_ref[hp * DIL_HPS + hh] * dil) * dist, NEG)
            pt = jnp.exp(st - lse_ref[hh, 0:1, :])
            dpt = lax.dot_general(vw[:, sl], do[:, sl], _NT, preferred_element_type=F32)
            dst = (pt * (dpt - dd_ref[hh, 0:1, :])).astype(BF16)
            dqs.append(lax.dot_general(kw[:, sl], dst, _TN, preferred_element_type=F32).T * QSCALE)
            dks.append(jnp.dot(dst, qh, preferred_element_type=F32))
            dvs.append(jnp.dot(pt.astype(BF16), do[:, sl], preferred_element_type=F32))
        dqkv_ref[0, pl.ds(pl.multiple_of(b * DIL_QB, DIL_QB), DIL_QB), :] = jnp.concatenate(dqs, axis=1).astype(dqkv_ref.dtype)
        dk_acc[pl.ds(start, DIL_WIN), :] += jnp.concatenate(dks, axis=1)
        dv_acc[pl.ds(start, DIL_WIN), :] += jnp.concatenate(dvs, axis=1)

        @pl.when(b == DIL_BLOCKS - 1)
        def _():
            dqkv_ref[1] = dk_acc[...].astype(dqkv_ref.dtype)
            dqkv_ref[2] = dv_acc[...].astype(dqkv_ref.dtype)

    qspec = pl.BlockSpec((None, DIL_QB, DIL_LW), lambda g, hp, b: (g, b, hp))
    rspec = pl.BlockSpec((None, DIL_HPS, 8, DIL_QB), lambda g, hp, b: (g, hp, 0, b))
    (dqkv,), sent = _carrier_call(
        "dil_bwd", body, (3, DIL_NLW, DIL_BLOCKS), _dil_in_specs() + [qspec, rspec, rspec],
        [pl.BlockSpec((None, 3, SEQ, DIL_LW), lambda g, hp, b: (g, 0, 0, hp))], [jax.ShapeDtypeStruct((3, 3, SEQ, DM), BF16)],
        [pltpu.VMEM((SEQ, DIL_LW), F32), pltpu.VMEM((SEQ, DIL_LW), F32)], (slopes, qkv, qkv, qkv, do, dd, lse), carry)
    return dqkv, sent


def _ffn_fwd(name, x, g_pre, g_post, wgt4, wut4, wd4, carry):
    tm = 512

    def body(x_ref, gpre_ref, gpost_ref, wg_ref, wu_ref, wd_ref, xn_ref, h_ref, gate_ref, up_ref, u_ref, acc):
        s = pl.program_id(1)

        @pl.when(s == 0)
        def _():
            x = x_ref[...]
            r = lax.rsqrt(jnp.mean(x * x, axis=-1, keepdims=True) + RMS_EPS)
            h_ref[...] = (x * r * gpre_ref[...]).astype(h_ref.dtype)

        h = h_ref[...]
        gate = lax.dot_general(h, wg_ref[...], _NT, preferred_element_type=F32).astype(BF16)
        up = lax.dot_general(h, wu_ref[...], _NT, preferred_element_type=F32).astype(BF16)
        gate_ref[...] = gate
        up_ref[...] = up
        gf = gate.astype(F32)
        act = (gf * jax.nn.sigmoid(gf) * up.astype(F32)).astype(BF16)
        part = jnp.dot(act, wd_ref[...], preferred_element_type=F32)

        @pl.when(s == 0)
        def _():
            acc[...] = part

        @pl.when(s > 0)
        def _():
            acc[...] += part

        @pl.when(s == NCHIP - 1)
        def _():
            u = acc[...]
            u_ref[...] = u
            r = lax.rsqrt(jnp.mean(u * u, axis=-1, keepdims=True) + RMS_EPS)
            xn_ref[...] = x_ref[...] + u * r * gpost_ref[...]

    rows = pl.BlockSpec((tm, DM), lambda i, s: (i, 0))
    vec = pl.BlockSpec((1, DM), lambda i, s: (0, 0))
    wspec = _ffn_wspec(lambda i, s: (s, 0, 0))
    mid = pl.BlockSpec((None, tm, FSH), lambda i, s: (s, i, 0))
    outs, sent = _carrier_call(
        name, body, (SEQ // tm, NCHIP), [rows, vec, vec, wspec, wspec, wspec], [rows, rows, mid, mid, rows],
        [jax.ShapeDtypeStruct((SEQ, DM), F32), jax.ShapeDtypeStruct((SEQ, DM), BF16), jax.ShapeDtypeStruct((NCHIP, SEQ, FSH), BF16),
         jax.ShapeDtypeStruct((NCHIP, SEQ, FSH), BF16), jax.ShapeDtypeStruct((SEQ, DM), F32)],
        [pltpu.VMEM((tm, DM), F32)], (x, g_pre, g_post, wgt4, wut4, wd4), carry)
    return outs, sent


def _ffn_block(layer, x, g_pre, g_post, ex):
    tag = f"l{layer}_ffn_fwd"
    (x_new, h, gate, up, u), sent = _ffn_fwd(tag, x, g_pre, g_post, ex.weight(("ffn_w_gate", layer)), ex.weight(("ffn_w_up", layer)),
                                             ex.weight(("ffn_w_down", layer)), ex.carry(tag))
    ex.carried(tag, sent)
    return x_new, (x, h, gate, up, u)


def _ffn_bwd(name, dx, x, gate, up, u, g_pre, g_post, wgt4, wut4, wd4, carry):
    tm = 512

    def body(dx_ref, x_ref, gate_ref, up_ref, u_ref, gpre_ref, gpost_ref, wg_ref, wu_ref, wd_ref,
             dxin_ref, du_ref, dgate_ref, dup_ref, act_ref, dgpre_ref, dgpost_ref, dh_acc):
        i, s = pl.program_id(0), pl.program_id(1)

        @pl.when((i == 0) & (s == 0))
        def _():
            dgpre_ref[...] = jnp.zeros_like(dgpre_ref)
            dgpost_ref[...] = jnp.zeros_like(dgpost_ref)

        @pl.when(s == 0)
        def _():
            dy = dx_ref[...]
            uu = u_ref[...]
            r = lax.rsqrt(jnp.mean(uu * uu, axis=-1, keepdims=True) + RMS_EPS)
            yh = uu * r
            t = dy * gpost_ref[...]
            du_ref[...] = (r * (t - yh * jnp.mean(t * yh, axis=-1, keepdims=True))).astype(du_ref.dtype)
            dgpost_ref[...] += jnp.sum(dy * yh, axis=0, keepdims=True)

        dact = lax.dot_general(du_ref[...], wd_ref[...], _NT, preferred_element_type=F32)
        g = gate_ref[...].astype(F32)
        upv = up_ref[...].astype(F32)
        sg = jax.nn.sigmoid(g)
        dgate = (dact * upv * sg * (1.0 + g * (1.0 - sg))).astype(BF16)
        dup = (dact * g * sg).astype(BF16)
        dgate_ref[...] = dgate
        dup_ref[...] = dup
        act_ref[...] = (g * sg * upv).astype(act_ref.dtype)
        part = jnp.dot(dgate, wg_ref[...], preferred_element_type=F32) + jnp.dot(dup, wu_ref[...], preferred_element_type=F32)

        @pl.when(s == 0)
        def _():
            dh_acc[...] = part

        @pl.when(s > 0)
        def _():
            dh_acc[...] += part

        @pl.when(s == NCHIP - 1)
        def _():
            dh = dh_acc[...]
            xx = x_ref[...]
            r = lax.rsqrt(jnp.mean(xx * xx, axis=-1, keepdims=True) + RMS_EPS)
            yh = xx * r
            t = dh * gpre_ref[...]
            dxin_ref[...] = dx_ref[...] + r * (t - yh * jnp.mean(t * yh, axis=-1, keepdims=True))
            dgpre_ref[...] += jnp.sum(dh * yh, axis=0, keepdims=True)

    rows = pl.BlockSpec((tm, DM), lambda i, s: (i, 0))
    vec = pl.BlockSpec((1, DM), lambda i, s: (0, 0))
    wspec = _ffn_wspec(lambda i, s: (s, 0, 0))
    mid = pl.BlockSpec((None, tm, FSH), lambda i, s: (s, i, 0))
    mid_shape = jax.ShapeDtypeStruct((NCHIP, SEQ, FSH), BF16)
    return _carrier_call(
        name, body, (SEQ // tm, NCHIP), [rows, rows, mid, mid, rows, vec, vec, wspec, wspec, wspec], [rows, rows, mid, mid, mid, vec, vec],
        [jax.ShapeDtypeStruct((SEQ, DM), F32), jax.ShapeDtypeStruct((SEQ, DM), BF16), mid_shape, mid_shape, mid_shape,
         jax.ShapeDtypeStruct((1, DM), F32), jax.ShapeDtypeStruct((1, DM), F32)],
        [pltpu.VMEM((tm, DM), F32)], (dx, x, gate, up, u, g_pre, g_post, wgt4, wut4, wd4), carry)


def _ffn_block_bwd(layer, dx, saved, g_pre, g_post, ex):
    tag = f"l{layer}"
    x, h, gate, up, u = saved
    (dx_in, du, dgate, dup, act, dg_pre, dg_post), sent = _ffn_bwd(
        f"{tag}_ffn_bwd", dx, x, gate, up, u, g_pre, g_post, ex.weight(("ffn_w_gate", layer)), ex.weight(("ffn_w_up", layer)),
        ex.weight(("ffn_w_down", layer)), ex.carry(f"{tag}_ffn_bwd"))
    ex.carried(f"{tag}_ffn_bwd", sent)
    d_wd = _ffn_bwd_dw(f"{tag}_dwd", act, du)
    d_wg = _ffn_bwd_dw(f"{tag}_dwg", dgate, h)
    d_wu = _ffn_bwd_dw(f"{tag}_dwu", dup, h)
    ex.grads(f"{tag}_ffn", {("ffn_w_gate", layer): d_wg, ("ffn_w_up", layer): d_wu, ("ffn_w_down", layer): d_wd})
    return dx_in, dg_pre, dg_post


def _alibi_slopes():
    return 2.0 ** (-8.0 * jnp.arange(1, NH + 1, dtype=F32) / NH)


def _local_step(x, target, norms, rpb, ex):
    g_mix_pre, g_mix_post, g_ffn_pre, g_ffn_post = norms
    row = lambda a, i: a[i:i + 1]

    bias, sent = _na_bias_tiles(rpb, ex.carry("na_bias_tiles"))
    ex.carried("na_bias_tiles", sent)
    h0, h0t = _rms_fwd_both("l0_mix_pre", x, row(g_mix_pre, 0))
    qkv0, sent = _qkv_fwd("l0_qkv", h0[None], ex.weight(("na_w_qkv", 0)), ex.carry("l0_qkv"))
    ex.carried("l0_qkv", sent)
    o0, sent = _na_fwd(qkv0[0], bias, ex.carry("na_fwd"))
    ex.carried("na_fwd", sent)
    na_wo = ex.weight(("na_w_o", 0)).reshape(DM, DM)
    x1, u0 = _proj_fwd("l0_proj", o0, na_wo, x, row(g_mix_post, 0))
    x2, ffn0 = _ffn_block(0, x1, row(g_ffn_pre, 0), row(g_ffn_post, 0), ex)

    slopes = _alibi_slopes()
    h2g, h2gt = _to_groups("l1_h_groups", _rms_fwd("l1_mix_pre", x2, row(g_mix_pre, 1), F32))
    dil_wqkv = ex.weight(("dil_w_qkv", 0))
    qkv1, sent = _qkv_fwd("l1_qkv", h2g, dil_wqkv, ex.carry("l1_qkv"))
    ex.carried("l1_qkv", sent)
    og, lg, sent = _dil_fwd(qkv1, slopes, ex.carry("dil_fwd"))
    ex.carried("dil_fwd", sent)
    o1, lse = _dil_merge(og, lg)
    dil_wo = ex.weight(("dil_w_o", 0)).reshape(DM, DM)
    x3, u1 = _proj_fwd("l1_proj", o1, dil_wo, x2, row(g_mix_post, 1))
    x4, ffn1 = _ffn_block(1, x3, row(g_ffn_pre, 1), row(g_ffn_post, 1), ex)

    dx4, loss_row = _loss_grad("loss", x4, target)

    dx3, dg_fpre1, dg_fpost1 = _ffn_block_bwd(1, dx4, ffn1, row(g_ffn_pre, 1), row(g_ffn_post, 1), ex)
    (do1, du1, dg_mpost1), sent = _proj_bwd("l1_proj_bwd", dx3, u1, row(g_mix_post, 1), dil_wo, F32, ex.carry("l1_proj_bwd"))
    ex.carried("l1_proj_bwd", sent)
    d_dil_wo = _proj_bwd_dw("l1_dwo", o1, du1)
    dog, ddg, lseg = _dil_bwd_prep(do1, o1, lse)
    dqkv1, sent = _dil_bwd(qkv1, dog, ddg, lseg, slopes, ex.carry("dil_bwd"))
    ex.carried("dil_bwd", sent)
    d_dil_wqkv = _qkv_bwd_dw("l1_dwqkv", h2gt, dqkv1, dil_wqkv.shape[2])
    ex.grads("l1_mix", {("dil_w_qkv", 0): d_dil_wqkv, ("dil_w_o", 0): d_dil_wo.reshape(NCHIP, DM // NCHIP, DM)})
    dh2g, sent = _qkv_bwd_dh("l1_dh", dqkv1, dil_wqkv, ex.carry("l1_dh"))
    ex.carried("l1_dh", sent)
    dh2 = _from_groups_sum("l1_dh_tokens", dh2g)
    (dx2, dg_mpre1), sent = _norm_bwd("l1_mix_pre_bwd", dh2, x2, row(g_mix_pre, 1), dx3, ex.carry("l1_mix_pre_bwd"))
    ex.carried("l1_mix_pre_bwd", sent)

    dx1, dg_fpre0, dg_fpost0 = _ffn_block_bwd(0, dx2, ffn0, row(g_ffn_pre, 0), row(g_ffn_post, 0), ex)
    (do0, du0, dg_mpost0), sent = _proj_bwd("l0_proj_bwd", dx1, u0, row(g_mix_post, 0), na_wo, BF16, ex.carry("l0_proj_bwd"))
    ex.carried("l0_proj_bwd", sent)
    d_na_wo = _proj_bwd_dw("l0_dwo", o0, du0)
    dqkv0, z, sent = _na_bwd(qkv0[0], bias, do0, ex.carry("na_bwd"))
    ex.carried("na_bwd", sent)
    d_rpb = _rpb_grad(z)
    na_wqkv = ex.weight(("na_w_qkv", 0))
    d_na_wqkv = _qkv_bwd_dw("l0_dwqkv", h0t[None], dqkv0[None], na_wqkv.shape[2])
    ex.grads("l0_mix", {("na_w_qkv", 0): d_na_wqkv, ("na_w_o", 0): d_na_wo.reshape(NCHIP, DM // NCHIP, DM)})
    dh0, sent = _qkv_bwd_dh("l0_dh", dqkv0[None], na_wqkv, ex.carry("l0_dh"))
    ex.carried("l0_dh", sent)
    (dx0, dg_mpre0), sent = _norm_bwd("l0_mix_pre_bwd", dh0[0], x, row(g_mix_pre, 0), dx1, ex.carry("l0_mix_pre_bwd"))
    ex.carried("l0_mix_pre_bwd", sent)

    dnorms = (jnp.concatenate([dg_mpre0, dg_mpre1]), jnp.concatenate([dg_mpost0, dg_mpost1]),
              jnp.concatenate([dg_fpre0, dg_fpre1]), jnp.concatenate([dg_fpost0, dg_fpost1]))
    return loss_row, dx0, dnorms, d_rpb


def _place():
    x, y, c = lax.axis_index("x"), lax.axis_index("y"), lax.axis_index("c")
    chips = ((1 - x, y), (x, 1 - y), (1 - x, 1 - y))
    return x, y, c, chips


def _chip_id(chip):
    return 2 * chip[0] + chip[1]


def _comm_call(name, body, ins, out_shapes, n_sems, aliases=None):
    return pl.pallas_call(
        body, in_specs=[HBM_SPEC] * len(ins), out_specs=[HBM_SPEC] * len(out_shapes), out_shape=out_shapes,
        scratch_shapes=[pltpu.SemaphoreType.DMA((k,)) for k in n_sems], input_output_aliases=aliases or {},
        compiler_params=pltpu.CompilerParams(has_side_effects=True), name=name)(*ins)


def _gather_copies(shards):
    n = len(shards)

    def copies(src, out, sems):
        send_sems, recv_sems = sems
        x, y, c, chips = _place()

        def copy(t, k, chip, half, to, from_src=False):
            blk = out[t].at[_chip_id(chip), half]
            return pltpu.make_async_remote_copy(
                src_ref=src[t].at[half] if from_src else blk, dst_ref=blk,
                send_sem=send_sems.at[6 * t + k], recv_sem=recv_sems.at[6 * t + k], device_id=to, device_id_type=MESH)

        return copy, x, y, c, chips

    def issue(src, out, sems):
        copy, x, y, c, chips = copies(src, out, sems)
        for t in range(n):
            for j, chip in enumerate(chips):
                copy(t, j, (x, y), c, (*chip, c), from_src=True).start()

    def drain(src, out, sems):
        copy, x, y, c, chips = copies(src, out, sems)
        passed = []
        for t in range(n):
            for j, chip in enumerate(chips):
                copy(t, j, chip, c, (x, y, c)).wait_recv()
                fwd = copy(t, 3 + j, chip, c, (x, y, 1 - c))
                fwd.start()
                passed.append(fwd)
        for t in range(n):
            for j, chip in enumerate(chips):
                copy(t, 3 + j, chip, 1 - c, (x, y, c)).wait_recv()
        for t in range(n):
            for j, chip in enumerate(chips):
                copy(t, j, (x, y), c, (*chip, c), from_src=True).wait_send()
        for cp in passed:
            cp.wait_send()

    return _Carried(shards, [jax.ShapeDtypeStruct((NCHIP,) + s.shape, s.dtype) for s in shards], (6 * n, 6 * n), issue, drain)


def _pair_exchange_copies(grads):
    n = len(grads)

    def copies(g, theirs, sems):
        send_sems, recv_sems = sems
        x, y, c, _ = _place()
        return [pltpu.make_async_remote_copy(src_ref=g[t].at[:, 1 - c], dst_ref=theirs[t], send_sem=send_sems.at[t],
                                             recv_sem=recv_sems.at[t], device_id=(x, y, 1 - c), device_id_type=MESH) for t in range(n)]

    def issue(g, theirs, sems):
        for cp in copies(g, theirs, sems):
            cp.start()

    def drain(g, theirs, sems):
        for cp in copies(g, theirs, sems):
            cp.wait()

    return _Carried(grads, [jax.ShapeDtypeStruct((NCHIP,) + g.shape[2:], g.dtype) for g in grads], (n, n), issue, drain)


def _chip_exchange_copies(items):
    flat = [(t, i, j) for t, (_, peers) in enumerate(items) for i, j in enumerate(peers)]

    def copies(p, slots, sems):
        send_sems, recv_sems = sems
        x, y, c, chips = _place()
        return [pltpu.make_async_remote_copy(src_ref=p[t].at[_chip_id(chips[j])], dst_ref=slots[t].at[i], send_sem=send_sems.at[k],
                                             recv_sem=recv_sems.at[k], device_id=(*chips[j], c), device_id_type=MESH)
                for k, (t, i, j) in enumerate(flat)]

    def issue(p, slots, sems):
        for cp in copies(p, slots, sems):
            cp.start()

    def drain(p, slots, sems):
        for cp in copies(p, slots, sems):
            cp.wait()

    return _Carried([p for p, _ in items], [jax.ShapeDtypeStruct((len(peers),) + p.shape[1:], p.dtype) for p, peers in items],
                    (len(flat), len(flat)), issue, drain)


def _pair_share(full):
    n = len(full)

    def body(*refs):
        buf = refs[n:2 * n]
        send_sems, recv_sems = refs[2 * n:]
        x, y, c, _ = _place()
        sends = [pltpu.make_async_remote_copy(src_ref=buf[t].at[c], dst_ref=buf[t].at[c], send_sem=send_sems.at[t], recv_sem=recv_sems.at[t],
                                              device_id=(x, y, 1 - c), device_id_type=MESH) for t in range(n)]
        for cp in sends:
            cp.start()
        for t in range(n):
            pltpu.make_async_remote_copy(src_ref=buf[t].at[c], dst_ref=buf[t].at[1 - c], send_sem=send_sems.at[t], recv_sem=recv_sems.at[t],
                                         device_id=(x, y, 1 - c), device_id_type=MESH).wait_recv()
        for cp in sends:
            cp.wait_send()

    return _comm_call("grad_pair_share", body, full, [jax.ShapeDtypeStruct(f.shape, f.dtype) for f in full], (n, n),
                      aliases={t: t for t in range(n)})


SMALL_ROWS = 128


def _allreduce_small(v):
    def body(v_ref, o_ref, buf, send_sems, recv_sems):
        x, y, c, _ = _place()
        me = 4 * x + 2 * y + c
        flip = lambda a, f: 1 - a if f else a
        buf[me] = v_ref[...]
        peers = [(flip(x, d >> 2 & 1), flip(y, d >> 1 & 1), flip(c, d & 1)) for d in range(1, 8)]
        sends = [pltpu.make_async_remote_copy(src_ref=v_ref, dst_ref=buf.at[me], send_sem=send_sems.at[i], recv_sem=recv_sems.at[i],
                                              device_id=peer, device_id_type=MESH) for i, peer in enumerate(peers)]
        for cp in sends:
            cp.start()
        for i, (px, py, pc) in enumerate(peers):
            pltpu.make_async_remote_copy(src_ref=v_ref, dst_ref=buf.at[4 * px + 2 * py + pc], send_sem=send_sems.at[i], recv_sem=recv_sems.at[i],
                                         device_id=(px, py, pc), device_id_type=MESH).wait_recv()
        for cp in sends:
            cp.wait_send()
        acc = buf[0]
        for k in range(1, 8):
            acc = acc + buf[k]
        o_ref[...] = acc

    vm = pl.BlockSpec(memory_space=pltpu.VMEM)
    return pl.pallas_call(
        body, in_specs=[vm], out_specs=vm, out_shape=jax.ShapeDtypeStruct((SMALL_ROWS, 128), F32),
        scratch_shapes=[pltpu.VMEM((8, SMALL_ROWS, 128), F32), pltpu.SemaphoreType.DMA((7,)), pltpu.SemaphoreType.DMA((7,))],
        compiler_params=pltpu.CompilerParams(has_side_effects=True), name="allreduce_small")(v)


def _row_block(rows, cols, budget=3 << 19):
    best = 8
    for bm in range(8, rows + 1, 8):
        if rows % bm == 0 and bm * cols * 4 <= budget:
            best = bm
    return best


def _pair_sum(name, place, gs, theirs):
    n = len(gs)
    _, m, c = theirs[0].shape
    bm = _row_block(m, c)

    def body(place_ref, *refs):
        for a_ref, b_ref, o_ref in zip(refs[:n], refs[n:2 * n], refs[2 * n:]):
            o_ref[...] = (a_ref[...].astype(F32) + b_ref[...].astype(F32)).astype(o_ref.dtype)

    spec = pl.BlockSpec((None, bm, c), lambda k, i, pr: (k, i, 0))
    return pl.pallas_call(
        body, out_shape=[jax.ShapeDtypeStruct(theirs[0].shape, BF16)] * n,
        grid_spec=pltpu.PrefetchScalarGridSpec(
            num_scalar_prefetch=1, grid=(NCHIP, m // bm),
            in_specs=[pl.BlockSpec((None, None, bm, c), lambda k, i, pr: (k, pr[0], i, 0))] * n + [spec] * n, out_specs=[spec] * n),
        compiler_params=_params(("parallel", "parallel")), name=name)(place, *gs, *theirs)


def _chip_sum(name, place, parts, slots):
    n, ns = len(parts), len(slots[0])
    _, m, c = parts[0].shape
    bm = _row_block(m, c)

    def body(place_ref, *refs):
        for t in range(n):
            acc = refs[t][...].astype(F32)
            for s_ref in refs[n + t * ns:n + (t + 1) * ns]:
                for i in range(s_ref.shape[0]):
                    acc = acc + s_ref[i].astype(F32)
            refs[n + n * ns + t][...] = acc

    half = pl.BlockSpec((None, bm, c), lambda i, pr: (pr[0], i, 0))
    return pl.pallas_call(
        body, out_shape=[jax.ShapeDtypeStruct((2, m, c), F32)] * n,
        grid_spec=pltpu.PrefetchScalarGridSpec(
            num_scalar_prefetch=1, grid=(m // bm,),
            in_specs=[pl.BlockSpec((None, bm, c), lambda i, pr: (pr[1], i, 0))] * n
            + [pl.BlockSpec((s.shape[0], bm, c), lambda i, pr: (0, i, 0)) for group in slots for s in group],
            out_specs=[half] * n),
        compiler_params=_params(("parallel",)), name=name)(place, *parts, *[s for group in slots for s in group])


def _adamw(name, w, g, m, v, layer=0, into=None):
    lead, rows, cols = w.shape
    bm = _row_block(rows, cols, budget=768 * 1024)
    c1 = 1.0 - ADAM_B1 ** ADAM_STEP
    c2 = 1.0 - ADAM_B2 ** ADAM_STEP

    def body(w_ref, g_ref, m_ref, v_ref, *rest):
        go_ref, d_ref, mo_ref, vo_ref = rest[-4:]
        g = g_ref[...]
        mn = ADAM_B1 * m_ref[...] + (1.0 - ADAM_B1) * g
        vn = ADAM_B2 * v_ref[...] + (1.0 - ADAM_B2) * (g * g)
        go_ref[...] = g
        mo_ref[...] = mn
        vo_ref[...] = vn
        d_ref[...] = -ADAM_LR * ((mn / c1) / (jnp.sqrt(vn / c2) + ADAM_EPS) + ADAM_WD * w_ref[...])

    spec = pl.BlockSpec((None, bm, cols), lambda i: (layer, i, 0))
    sh = jax.ShapeDtypeStruct((lead, rows, cols), F32)
    prev = [] if into is None else list(into)
    return pl.pallas_call(
        body, grid=(rows // bm,), in_specs=[spec, pl.BlockSpec((bm, cols), lambda i: (i, 0)), spec, spec] + [pl.BlockSpec(memory_space=pl.ANY)] * len(prev),
        out_specs=[spec] * 4, out_shape=[sh] * 4, input_output_aliases={4 + k: k for k in range(len(prev))},
        compiler_params=_params(("parallel",)), name=name)(w, g, m, v, *prev)


def _pack_small(norms, rpb, last=None):
    flat = jnp.concatenate([a.reshape(-1) for a in norms] + [rpb.reshape(-1)])
    flat = jnp.pad(flat, (0, SMALL_ROWS * 128 - flat.shape[0]))
    if last is not None:
        flat = jnp.concatenate([flat[:-1], last.reshape(1)])
    return flat.reshape(SMALL_ROWS, 128)


def _unpack_small(p):
    flat = p.reshape(-1)
    norms = [flat[i * 2 * DM:(i + 1) * 2 * DM].reshape(2, DM) for i in range(4)]
    rpb = flat[8 * DM:8 * DM + NH * 15 * 31].reshape(1, NH, 15, 31)
    return norms, rpb


FFN_NAMES = ("ffn_w_gate", "ffn_w_up", "ffn_w_down")
L0_FFN = tuple((n, 0) for n in FFN_NAMES)
L1_FFN = tuple((n, 1) for n in FFN_NAMES)
NA_KEYS = (("na_w_qkv", 0), ("na_w_o", 0))
DIL_KEYS = (("dil_w_qkv", 0), ("dil_w_o", 0))
ALL_PEERS, NEIGHBOURS, DIAGONAL = (0, 1, 2), (0, 1), (2,)


class _Exchange:
    GATHERS = {"na_bias_tiles": NA_KEYS, "l0_qkv": L0_FFN[:1], "na_fwd": L0_FFN[1:], "l0_ffn_fwd": DIL_KEYS[:1], "dil_fwd": L1_FFN + DIL_KEYS[1:]}
    PAIRS = {"l1_proj_bwd": L1_FFN, "l1_dh": DIL_KEYS, "l0_proj_bwd": L0_FFN}
    EXCHANGES = {"dil_bwd": [(k, ALL_PEERS) for k in L1_FFN],
                 "l0_ffn_bwd": [(DIL_KEYS[0], NEIGHBOURS), (DIL_KEYS[1], ALL_PEERS)],
                 "na_bwd": [(k, ALL_PEERS) for k in L0_FFN] + [(DIL_KEYS[0], DIAGONAL)],
                 "l0_dh": [(k, NEIGHBOURS) for k in NA_KEYS],
                 "l0_mix_pre_bwd": [(k, DIAGONAL) for k in NA_KEYS]}

    def __init__(self, shards):
        self.chip = 2 * lax.axis_index("x") + lax.axis_index("y")
        self.place = jnp.stack([lax.axis_index("c"), self.chip]).astype(jnp.int32)
        self.own = {k: s.reshape(2, s.shape[0] // 2, s.shape[1]).astype(BF16) for k, s in shards.items()}
        self.gathered, self.mine, self.parts, self.slots, self.full = {}, {}, {}, {}, {}

    def _take(self, keys, landed):
        for k, gw in zip(keys, landed):
            self.gathered[k] = lax.dynamic_update_slice(gw, self.own[k][None], (self.chip, 0, 0, 0))

    def _sum(self, items, landed):
        runs = []
        for (k, peers), s in zip(items, landed):
            got = self.slots.setdefault(k, {})
            got[peers] = s
            if sum(len(p) for p in got) == len(ALL_PEERS):
                like = (self.parts[k].shape, tuple(sorted(got)))
                if runs and runs[-1][0] == like:
                    runs[-1][1].append(k)
                else:
                    runs.append((like, [k]))
        for (_, split), ks in runs:
            sums = _chip_sum(f"chip_sum_{ks[0][0]}_{ks[0][1]}", self.place, [self.parts[k] for k in ks],
                             [[self.slots[k][p] for p in split] for k in ks])
            self.full.update(zip(ks, sums))

    def weight(self, key):
        g = self.gathered[key]
        return g.reshape(NCHIP, 2 * g.shape[2], g.shape[3])

    def _pair_sums(self, keys, theirs):
        runs = []
        for k, t in zip(keys, theirs):
            if runs and runs[-1][0][1].shape == t.shape:
                runs[-1].append((k, t))
            else:
                runs.append([(k, t)])
        for run in runs:
            ks = [k for k, _ in run]
            sums = _pair_sum(f"pair_sum_{ks[0][0]}_{ks[0][1]}", self.place, [self.mine[k] for k in ks], [t for _, t in run])
            self.parts.update(zip(ks, sums))

    def carry(self, tag):
        if tag in self.GATHERS:
            return _gather_copies([self.own[k] for k in self.GATHERS[tag]])
        if tag in self.PAIRS:
            return _pair_exchange_copies([self.mine[k] for k in self.PAIRS[tag]])
        if tag in self.EXCHANGES:
            return _chip_exchange_copies([(self.parts[k], peers) for k, peers in self.EXCHANGES[tag]])
        return None

    def carried(self, tag, landed):
        if tag in self.GATHERS:
            self._take(self.GATHERS[tag], landed)
        elif tag in self.PAIRS:
            self._pair_sums(self.PAIRS[tag], landed)
        elif tag in self.EXCHANGES:
            self._sum(self.EXCHANGES[tag], landed)

    def grads(self, tag, dw):
        for k, g in dw.items():
            self.mine[k] = g.reshape(NCHIP, 2, -1, g.shape[-1])
        if tag == "l0_mix":
            keys = tuple(dw)
            self._pair_sums(keys, _run_carried("grad_pair_exchange_last", _pair_exchange_copies([self.mine[k] for k in keys])))

    def finish(self):
        keys = tuple(self.full)
        shared = _pair_share([self.full[k] for k in keys])
        return {k: s.reshape(2 * s.shape[1], s.shape[2]) for k, s in zip(keys, shared)}


def kernel(x, norm_mix_pre, norm_mix_post, norm_ffn_pre, norm_ffn_post, na_w_qkv, na_w_o, na_rpb, dil_w_qkv, dil_w_o, ffn_w_gate, ffn_w_up, ffn_w_down, loss_target, m_norm_mix_pre, m_norm_mix_post, m_norm_ffn_pre, m_norm_ffn_post, m_na_w_qkv, m_na_w_o, m_na_rpb, m_dil_w_qkv, m_dil_w_o, m_ffn_w_gate, m_ffn_w_up, m_ffn_w_down, v_norm_mix_pre, v_norm_mix_post, v_norm_ffn_pre, v_norm_ffn_post, v_na_w_qkv, v_na_w_o, v_na_rpb, v_dil_w_qkv, v_dil_w_o, v_ffn_w_gate, v_ffn_w_up, v_ffn_w_down):
    tr = lambda a: jnp.swapaxes(a, 1, 2)
    weights = {"na_w_qkv": na_w_qkv, "na_w_o": na_w_o, "dil_w_qkv": dil_w_qkv, "dil_w_o": dil_w_o,
               "ffn_w_gate": tr(ffn_w_gate), "ffn_w_up": tr(ffn_w_up), "ffn_w_down": ffn_w_down}
    m_in = {"na_w_qkv": m_na_w_qkv, "na_w_o": m_na_w_o, "dil_w_qkv": m_dil_w_qkv, "dil_w_o": m_dil_w_o,
            "ffn_w_gate": tr(m_ffn_w_gate), "ffn_w_up": tr(m_ffn_w_up), "ffn_w_down": m_ffn_w_down}
    v_in = {"na_w_qkv": v_na_w_qkv, "na_w_o": v_na_w_o, "dil_w_qkv": v_dil_w_qkv, "dil_w_o": v_dil_w_o,
            "ffn_w_gate": tr(v_ffn_w_gate), "ffn_w_up": tr(v_ffn_w_up), "ffn_w_down": v_ffn_w_down}

    ex = _Exchange({(n, l): weights[n][l] for n in weights for l in range(weights[n].shape[0])})
    norms = (norm_mix_pre, norm_mix_post, norm_ffn_pre, norm_ffn_post)
    loss_row, dx, dnorms, d_rpb = _local_step(x[0], loss_target[0], norms, na_rpb[0], ex)
    full = ex.finish()
    small = _allreduce_small(_pack_small(dnorms, d_rpb, last=loss_row[0, 0]))
    loss = small[SMALL_ROWS - 1, 127]

    out_g, out_d, out_m, out_v = {}, {}, {}, {}
    for n in weights:
        res = None
        for l in range(weights[n].shape[0]):
            res = _adamw(f"adamw_{n}_{l}", weights[n], full[(n, l)], m_in[n], v_in[n], l, res)
        if n in ("ffn_w_gate", "ffn_w_up"):
            res = [tr(r) for r in res]
        out_g[n], out_d[n], out_m[n], out_v[n] = res
    sm_names = ("norm_mix_pre", "norm_mix_post", "norm_ffn_pre", "norm_ffn_post", "na_rpb")
    sm = _adamw("adamw_small", _pack_small(norms, na_rpb)[None], small,
                _pack_small((m_norm_mix_pre, m_norm_mix_post, m_norm_ffn_pre, m_norm_ffn_post), m_na_rpb)[None],
                _pack_small((v_norm_mix_pre, v_norm_mix_post, v_norm_ffn_pre, v_norm_ffn_post), v_na_rpb)[None])
    for res, dst in zip(sm, (out_g, out_d, out_m, out_v)):
        ns, rp = _unpack_small(res)
        for n, a in zip(sm_names, ns + [rp]):
            dst[n] = a

    order = ("norm_mix_pre", "norm_mix_post", "norm_ffn_pre", "norm_ffn_post", "na_w_qkv", "na_w_o", "na_rpb", "dil_w_qkv", "dil_w_o",
             "ffn_w_gate", "ffn_w_up", "ffn_w_down")
    return (loss, dx[None], *[out_g[n] for n in order], *[out_d[n] for n in order], *[out_m[n] for n in order], *[out_v[n] for n in order])
```

```python
import functools

import numpy as np
import jax
import jax.numpy as jnp
from jax import lax
from jax.experimental import pallas as pl
from jax.experimental.pallas import tpu as pltpu

F32 = jnp.float32
BF16 = jnp.bfloat16

SEQ = 2048
DM = 1024
NH = 16
HD = 64
DFF = 2816
NCHIP = 4
FSH = DFF // NCHIP
GRID_W = 64
NA_QROWS = 4
NA_QB = NA_QROWS * GRID_W
NA_WROWS = 12
NA_WIN = NA_WROWS * GRID_W
DIL = (1, 4, 16)
DIL_QB = 256
DIL_WIN = DIL_QB + 128
DIL_RADIUS = 64
RMS_EPS = 1e-6
NEG = -1e30
QSCALE = HD ** -0.5
CH = 256
MESH = pl.DeviceIdType.MESH

ADAM_LR, ADAM_B1, ADAM_B2, ADAM_EPS, ADAM_WD, ADAM_STEP = 0.001, 0.9, 0.999, 1e-08, 0.01, 10

VMEM_LIMIT = 56 * 1024 * 1024

_NN = (((1,), (0,)), ((), ()))
_NT = (((1,), (1,)), ((), ()))
_TN = (((0,), (0,)), ((), ()))


def _params(sem):
    return pltpu.CompilerParams(dimension_semantics=sem, vmem_limit_bytes=VMEM_LIMIT)


def _matmul(name, pairs, grid, out_shape, out_spec, acc_shape, carrying=False, carry=None):
    nk = grid[-1]
    npair = len(pairs)
    n_in = 2 * npair

    def body(*refs):
        ins, o_ref = refs[:2 * npair], refs[n_in]
        part = None
        for p in range(npair):
            d = lax.dot_general(ins[2 * p][...].astype(BF16), ins[2 * p + 1][...].astype(BF16), pairs[p][4],
                                preferred_element_type=F32)
            part = d if part is None else part + d
        if nk == 1:
            o_ref[...] = part.astype(o_ref.dtype)
        else:
            acc_ref = refs[n_in + 1]
            kk = pl.program_id(len(grid) - 1)

            @pl.when(kk == 0)
            def _():
                acc_ref[...] = part

            @pl.when(kk > 0)
            def _():
                acc_ref[...] += part

            @pl.when(kk == nk - 1)
            def _():
                o_ref[...] = acc_ref[...].astype(o_ref.dtype)

    ops, specs = [], []
    for a, a_spec, b, b_spec, _ in pairs:
        ops += [a, b]
        specs += [a_spec, b_spec]
    (out,), sent = _carrier_call(name, body, grid, specs, [out_spec], [out_shape], [] if nk == 1 else [pltpu.VMEM(acc_shape, F32)], ops, carry)
    return (out, sent) if carrying else out


def _qkv_fwd(name, h_all, w4, carry):
    g_n = h_all.shape[0]
    per = w4.shape[2] // CH
    return _matmul(
        name, [(h_all, pl.BlockSpec((None, SEQ, DM), lambda g, q, k: (g, 0, 0)),
                w4, pl.BlockSpec((None, DM, CH), lambda g, q, k: ((g * 12 + q) // per, 0, (g * 12 + q) % per)), _NN)],
        (g_n, 12, 1), jax.ShapeDtypeStruct((g_n, SEQ, 3 * DM), BF16),
        pl.BlockSpec((None, SEQ, CH), lambda g, q, k: (g, 0, q)), None, carrying=True, carry=carry)


def _qkv_bwd_dh(name, dqkv, w4, carry):
    g_n = dqkv.shape[0]
    per = w4.shape[2] // CH
    tm = SEQ

    def pair(cb):
        chunk = lambda g, t: g * 12 + t * 4 + cb
        return (dqkv, pl.BlockSpec((None, None, tm, CH), lambda g, i, t: (g, t, i, cb)),
                w4, pl.BlockSpec((None, DM, CH), lambda g, i, t: (chunk(g, t) // per, 0, chunk(g, t) % per)), _NT)

    return _matmul(name, [pair(cb) for cb in range(4)], (g_n, SEQ // tm, 3), jax.ShapeDtypeStruct((g_n, SEQ, DM), F32),
                   pl.BlockSpec((None, tm, DM), lambda g, i, t: (g, i, 0)), (tm, DM), carrying=True, carry=carry)


def _qkv_bwd_dw(name, ht_all, dqkv, shard_cols, carry):
    g_n = dqkv.shape[0]
    per = shard_cols // CH
    return _matmul(
        name, [(ht_all, pl.BlockSpec((None, DM, SEQ), lambda qq, k: (qq // 12, 0, 0)),
                dqkv, pl.BlockSpec((None, None, SEQ, CH), lambda qq, k: (qq // 12, (qq % 12) // 4, 0, qq % 4)), _NN)],
        (g_n * 12, 1), jax.ShapeDtypeStruct((NCHIP, DM, shard_cols), BF16),
        pl.BlockSpec((None, DM, CH), lambda qq, k: (qq // per, 0, qq % per)), None, carrying=True, carry=carry)


def _proj_fwd(name, o, wo, x, g):
    tm = 512

    def body(o_ref, w_ref, x_ref, g_ref, xn_ref, u_ref):
        u = jnp.dot(o_ref[...], w_ref[...], preferred_element_type=F32)
        u_ref[...] = u
        r = lax.rsqrt(jnp.mean(u * u, axis=-1, keepdims=True) + RMS_EPS)
        xn_ref[...] = x_ref[...] + u * r * g_ref[...]

    rows = pl.BlockSpec((tm, DM), lambda i: (i, 0))
    sh = jax.ShapeDtypeStruct((SEQ, DM), F32)
    return pl.pallas_call(
        body, grid=(SEQ // tm,), in_specs=[rows, pl.BlockSpec((DM, DM), lambda i: (0, 0)), rows, pl.BlockSpec((1, DM), lambda i: (0, 0))],
        out_specs=[rows, rows], out_shape=[sh, sh], compiler_params=_params(("parallel",)), name=name)(o, wo, x, g)


def _proj_bwd(name, dy, u, g, wo, dtype, carry):
    tm = 512

    def body(dy_ref, u_ref, g_ref, w_ref, do_ref, du_ref, dg_ref):
        dy = dy_ref[...]
        u = u_ref[...]
        r = lax.rsqrt(jnp.mean(u * u, axis=-1, keepdims=True) + RMS_EPS)
        yh = u * r
        t = dy * g_ref[...]
        du = (r * (t - yh * jnp.mean(t * yh, axis=-1, keepdims=True))).astype(BF16)
        du_ref[...] = du
        do_ref[...] = lax.dot_general(du, w_ref[...], _NT, preferred_element_type=F32).astype(do_ref.dtype)

        @pl.when(pl.program_id(0) == 0)
        def _():
            dg_ref[...] = jnp.zeros_like(dg_ref)

        dg_ref[...] += jnp.sum(dy * yh, axis=0, keepdims=True)

    rows = pl.BlockSpec((tm, DM), lambda i: (i, 0))
    vec = pl.BlockSpec((1, DM), lambda i: (0, 0))
    return _carrier_call(
        name, body, (SEQ // tm,), [rows, rows, vec, pl.BlockSpec((DM, DM), lambda i: (0, 0))], [rows, rows, vec],
        [jax.ShapeDtypeStruct((SEQ, DM), dtype), jax.ShapeDtypeStruct((SEQ, DM), BF16), jax.ShapeDtypeStruct((1, DM), F32)],
        [], (dy, u, g, wo), carry)


def _proj_bwd_dw(name, o, du):
    tn = 512
    return _matmul(
        name, [(o, pl.BlockSpec((SEQ, DM), lambda j, k: (0, 0)), du, pl.BlockSpec((SEQ, tn), lambda j, k: (0, j)), _TN)],
        (DM // tn, 1), jax.ShapeDtypeStruct((DM, DM), BF16), pl.BlockSpec((DM, tn), lambda j, k: (0, j)), None)


def _ffn_wspec(index_map):
    return pl.BlockSpec((None, FSH, DM), index_map)


def _ffn_bwd_dw(name, a4, b):
    return _matmul(
        name, [(a4, pl.BlockSpec((None, SEQ, FSH), lambda s, k: (s, 0, 0)), b, pl.BlockSpec((SEQ, DM), lambda s, k: (0, 0)), _TN)],
        (NCHIP, 1), jax.ShapeDtypeStruct((NCHIP, FSH, DM), BF16), _ffn_wspec(lambda s, k: (s, 0, 0)), None)


ROWS = 256


def _row_spec():
    return pl.BlockSpec((ROWS, DM), lambda i: (i, 0))


def _vec_spec():
    return pl.BlockSpec((1, DM), lambda i: (0, 0))


def _rms_fwd(name, x, g, dtype=BF16):
    def body(x_ref, g_ref, o_ref):
        x = x_ref[...]
        r = lax.rsqrt(jnp.mean(x * x, axis=-1, keepdims=True) + RMS_EPS)
        o_ref[...] = (x * r * g_ref[...]).astype(o_ref.dtype)

    return pl.pallas_call(body, grid=(SEQ // ROWS,), in_specs=[_row_spec(), _vec_spec()], out_specs=_row_spec(),
                          out_shape=jax.ShapeDtypeStruct((SEQ, DM), dtype), compiler_params=_params(("parallel",)), name=name)(x, g)


def _rms_fwd_both(name, x, g):
    def body(x_ref, g_ref, o_ref, t_ref):
        x = x_ref[...]
        r = lax.rsqrt(jnp.mean(x * x, axis=-1, keepdims=True) + RMS_EPS)
        h = x * r * g_ref[...]
        o_ref[...] = h.astype(o_ref.dtype)
        t_ref[...] = h.T.astype(t_ref.dtype)

    return pl.pallas_call(
        body, grid=(SEQ // ROWS,), in_specs=[_row_spec(), _vec_spec()], out_specs=[_row_spec(), pl.BlockSpec((DM, ROWS), lambda i: (0, i))],
        out_shape=[jax.ShapeDtypeStruct((SEQ, DM), BF16), jax.ShapeDtypeStruct((DM, SEQ), BF16)],
        compiler_params=_params(("parallel",)), name=name)(x, g)


def _norm_bwd(name, dy, u, g, res, carry):
    def body(dy_ref, u_ref, g_ref, res_ref, du_ref, dg_ref):
        dy = dy_ref[...]
        u = u_ref[...]
        r = lax.rsqrt(jnp.mean(u * u, axis=-1, keepdims=True) + RMS_EPS)
        yh = u * r
        t = dy * g_ref[...]
        du_ref[...] = r * (t - yh * jnp.mean(t * yh, axis=-1, keepdims=True)) + res_ref[...]

        @pl.when(pl.program_id(0) == 0)
        def _():
            dg_ref[...] = jnp.zeros_like(dg_ref)

        dg_ref[...] += jnp.sum(dy * yh, axis=0, keepdims=True)

    return _carrier_call(
        name, body, (SEQ // ROWS,), [_row_spec(), _row_spec(), _vec_spec(), _row_spec()], [_row_spec(), _vec_spec()],
        [jax.ShapeDtypeStruct((SEQ, DM), F32), jax.ShapeDtypeStruct((1, DM), F32)], [], (dy, u, g, res), carry)


def _loss_grad(name, y, t):
    def body(y_ref, t_ref, dy_ref, l_ref):
        e = y_ref[...] - t_ref[...]
        dy_ref[...] = e * (1.0 / DM)

        @pl.when(pl.program_id(0) == 0)
        def _():
            l_ref[...] = jnp.zeros_like(l_ref)

        l_ref[...] += jnp.sum(e * e) * (0.5 / DM)

    return pl.pallas_call(
        body, grid=(SEQ // ROWS,), in_specs=[_row_spec(), _row_spec()],
        out_specs=[_row_spec(), pl.BlockSpec((1, 128), lambda i: (0, 0))],
        out_shape=[jax.ShapeDtypeStruct((SEQ, DM), F32), jax.ShapeDtypeStruct((1, 128), F32)],
        compiler_params=_params(("arbitrary",)), name=name)(y, t)


HBM_SPEC = pl.BlockSpec(memory_space=pltpu.HBM)


class _Carried:
    def __init__(self, ins, out_shapes, n_sems, issue, drain):
        self.ins, self.out_shapes, self.n_sems, self.issue, self.drain = list(ins), list(out_shapes), tuple(n_sems), issue, drain


def _carrier_call(name, body, grid, in_specs, out_specs, out_shape, scratch_shapes, operands, carry):
    n_in, n_out, n_scr = len(in_specs), len(out_specs), len(scratch_shapes)
    if carry is None:
        res = pl.pallas_call(body, grid=grid, in_specs=in_specs, out_specs=out_specs, out_shape=out_shape, scratch_shapes=scratch_shapes,
                             compiler_params=_params(("arbitrary",) * len(grid)), name=name)(*operands)
        return list(res), []
    ci, co = len(carry.ins), len(carry.out_shapes)

    def wrapped(*refs):
        ins, cins = refs[:n_in], refs[n_in:n_in + ci]
        outs, couts = refs[n_in + ci:n_in + ci + n_out], refs[n_in + ci + n_out:n_in + ci + n_out + co]
        scr, sems = refs[n_in + ci + n_out + co:n_in + ci + n_out + co + n_scr], refs[n_in + ci + n_out + co + n_scr:]
        first = functools.reduce(jnp.logical_and, [pl.program_id(a) == 0 for a in range(len(grid))])
        last = functools.reduce(jnp.logical_and, [pl.program_id(a) == grid[a] - 1 for a in range(len(grid))])

        @pl.when(first)
        def _():
            carry.issue(cins, couts, sems)

        body(*ins, *outs, *scr)

        @pl.when(last)
        def _():
            carry.drain(cins, couts, sems)

    res = pl.pallas_call(
        wrapped, grid=grid, in_specs=list(in_specs) + [HBM_SPEC] * ci, out_specs=list(out_specs) + [HBM_SPEC] * co,
        out_shape=list(out_shape) + carry.out_shapes,
        scratch_shapes=list(scratch_shapes) + [pltpu.SemaphoreType.DMA((k,)) for k in carry.n_sems],
        compiler_params=pltpu.CompilerParams(dimension_semantics=("arbitrary",) * len(grid), vmem_limit_bytes=VMEM_LIMIT, has_side_effects=True),
        name=name)(*operands, *carry.ins)
    return list(res[:n_out]), list(res[n_out:])


def _run_carried(name, carry):
    def body(*refs):
        ci, co = len(carry.ins), len(carry.out_shapes)
        carry.issue(refs[:ci], refs[ci:ci + co], refs[ci + co:])
        carry.drain(refs[:ci], refs[ci:ci + co], refs[ci + co:])

    return pl.pallas_call(
        body, in_specs=[HBM_SPEC] * len(carry.ins), out_specs=[HBM_SPEC] * len(carry.out_shapes), out_shape=carry.out_shapes,
        scratch_shapes=[pltpu.SemaphoreType.DMA((k,)) for k in carry.n_sems],
        compiler_params=pltpu.CompilerParams(has_side_effects=True), name=name)(*carry.ins)


NA_BLOCKS = SEQ // NA_QB
NA_ROWS_TOTAL = SEQ // GRID_W
NA_CLASSES = ((0, 0), (8, 4), (NA_ROWS_TOTAL - NA_QROWS, NA_ROWS_TOTAL - NA_WROWS))


def _na_pairs(i0, ws):
    out = []
    for qi in range(NA_QROWS):
        i = i0 + qi
        rs = min(max(i - 4, 0), NA_ROWS_TOTAL - 8)
        for kr in range(NA_WROWS):
            r = ws + kr
            if rs <= r < rs + 8:
                out.append((qi, kr, r - i + 7))
    return out


def _diag_onehot():
    qc, kc = np.meshgrid(np.arange(GRID_W), np.arange(GRID_W), indexing="ij")
    e = np.zeros((GRID_W * GRID_W, 128), np.float32)
    j = (kc - qc + 15).reshape(-1)
    ok = (j >= 0) & (j <= 30)
    e[np.arange(GRID_W * GRID_W)[ok], j[ok]] = 1.0
    return jnp.asarray(e)


def _rpb_expand(rpb):
    r2 = jnp.pad(rpb.reshape(NH * 15, 31), ((0, 0), (0, 128 - 31)))

    def body(r_ref, e_ref, o_ref):
        o_ref[...] = lax.dot_general(r_ref[...], e_ref[...], _NT, preferred_element_type=F32, precision=lax.Precision.HIGHEST)

    out = pl.pallas_call(body, out_shape=jax.ShapeDtypeStruct((NH * 15, GRID_W * GRID_W), F32), name="rpb_expand",
                         compiler_params=pltpu.CompilerParams(vmem_limit_bytes=VMEM_LIMIT))(r2, _diag_onehot())
    return out.reshape(NH, 15, GRID_W, GRID_W)


def _na_bias_tiles(rpb, carry):
    def body(b_ref, o_ref):
        qc = lax.broadcasted_iota(jnp.int32, (GRID_W, GRID_W), 0)
        kc = lax.broadcasted_iota(jnp.int32, (GRID_W, GRID_W), 1)
        first = jnp.clip(qc - 8, 0, GRID_W - 16)
        in_window = (kc >= first) & (kc < first + 16)
        neg = jnp.full((GRID_W, GRID_W), NEG, F32)
        for cls, (i0, ws) in enumerate(NA_CLASSES):
            @pl.when(pl.program_id(0) == cls)
            def _(i0=i0, ws=ws):
                pairs = {(qi, kr): dr for qi, kr, dr in _na_pairs(i0, ws)}
                masked = {dr: jnp.where(in_window, b_ref[dr], NEG) for dr in sorted(set(pairs.values()))}
                for qi in range(NA_QROWS):
                    for k2 in range(NA_WROWS // 2):
                        blocks = [masked[pairs[(qi, kr)]] if (qi, kr) in pairs else neg for kr in (2 * k2, 2 * k2 + 1)]
                        o_ref[qi * GRID_W:(qi + 1) * GRID_W, k2 * 128:(k2 + 1) * 128] = jnp.concatenate(blocks, axis=1)

    (tiles,), sent = _carrier_call(
        "na_bias_tiles", body, (3, NH), [pl.BlockSpec((None, 15, GRID_W, GRID_W), lambda c, h: (h, 0, 0, 0))],
        [pl.BlockSpec((None, None, NA_QB, NA_WIN), lambda c, h: (c, h, 0, 0))], [jax.ShapeDtypeStruct((3, NH, NA_QB, NA_WIN), F32)],
        [], (_rpb_expand(rpb),), carry)
    return tiles, sent


def _na_cls(b):
    return jnp.where(b == 0, 0, jnp.where(b == NA_BLOCKS - 1, 2, 1))


def _na_start(b):
    return pl.multiple_of(jnp.clip(b * NA_QROWS - 4, 0, NA_ROWS_TOTAL - NA_WROWS) * GRID_W, GRID_W)


HPS = 4
LW = HPS * HD
NLW = DM // LW


NA_BWD_HPS = 4


def _na_in_specs(hps=HPS):
    lw = hps * HD
    nlw = DM // lw
    return [pl.BlockSpec((NA_QB, lw), lambda hp, b: (b, hp)),
            pl.BlockSpec((SEQ, lw), lambda hp, b: (0, nlw + hp)),
            pl.BlockSpec((SEQ, lw), lambda hp, b: (0, 2 * nlw + hp)),
            pl.BlockSpec((None, hps, NA_QB, NA_WIN), lambda hp, b: (_na_cls(b), hp, 0, 0))]


def _na_fwd(qkv, bias, carry):
    def body(q_ref, k_ref, v_ref, b_ref, o_ref):
        start = _na_start(pl.program_id(1))
        q = q_ref[...]
        kw = k_ref[pl.ds(start, NA_WIN), :]
        vw = v_ref[pl.ds(start, NA_WIN), :]
        outs = []
        for hh in range(HPS):
            sl = slice(hh * HD, (hh + 1) * HD)
            s = lax.dot_general(q[:, sl] * QSCALE, kw[:, sl], _NT, preferred_element_type=F32) + b_ref[hh]
            p = jnp.exp(s - jnp.max(s, axis=-1, keepdims=True))
            l = jnp.sum(p, axis=-1, keepdims=True)
            outs.append(jnp.dot(p.astype(BF16), vw[:, sl], preferred_element_type=F32) / l)
        o_ref[...] = jnp.concatenate(outs, axis=1).astype(o_ref.dtype)

    (o,), sent = _carrier_call(
        "na_fwd", body, (NLW, NA_BLOCKS), _na_in_specs(), [pl.BlockSpec((NA_QB, LW), lambda hp, b: (b, hp))],
        [jax.ShapeDtypeStruct((SEQ, DM), BF16)], [], (qkv, qkv, qkv, bias), carry)
    return o, sent


def _na_bwd(qkv, bias, do, carry):
    lw = NA_BWD_HPS * HD

    def body(q_ref, k_ref, v_ref, b_ref, do_ref, dqkv_ref, z_ref, dk_acc, dv_acc):
        blk = pl.program_id(1)

        @pl.when(blk == 0)
        def _():
            dk_acc[...] = jnp.zeros_like(dk_acc)
            dv_acc[...] = jnp.zeros_like(dv_acc)
            z_ref[...] = jnp.zeros_like(z_ref)

        start = _na_start(blk)
        q = q_ref[...]
        do = do_ref[...]
        kw = k_ref[pl.ds(start, NA_WIN), :]
        vw = v_ref[pl.ds(start, NA_WIN), :]
        dqs, dks, dvs, dss = [], [], [], []
        for hh in range(NA_BWD_HPS):
            sl = slice(hh * HD, (hh + 1) * HD)
            qh = q[:, sl] * QSCALE
            s = lax.dot_general(qh, kw[:, sl], _NT, preferred_element_type=F32) + b_ref[hh]
            p = jnp.exp(s - jnp.max(s, axis=-1, keepdims=True))
            p = p / jnp.sum(p, axis=-1, keepdims=True)
            dp = lax.dot_general(do[:, sl], vw[:, sl], _NT, preferred_element_type=F32)
            ds = p * (dp - jnp.sum(p * dp, axis=-1, keepdims=True))
            dsb = ds.astype(BF16)
            dqs.append(jnp.dot(dsb, kw[:, sl], preferred_element_type=F32) * QSCALE)
            dks.append(lax.dot_general(qh, dsb, _TN, preferred_element_type=F32).T)
            dvs.append(lax.dot_general(do[:, sl], p.astype(BF16), _TN, preferred_element_type=F32).T)
            dss.append(ds)
        for cls, (i0, ws) in enumerate(NA_CLASSES):
            @pl.when(_na_cls(blk) == cls)
            def _(i0=i0, ws=ws):
                for hh, ds in enumerate(dss):
                    for qi, kr, dr in _na_pairs(i0, ws):
                        z_ref[hh, dr * GRID_W:(dr + 1) * GRID_W, :] += ds[qi * GRID_W:(qi + 1) * GRID_W, kr * GRID_W:(kr + 1) * GRID_W]
        dqkv_ref[0, pl.ds(pl.multiple_of(blk * NA_QB, NA_QB), NA_QB), :] = jnp.concatenate(dqs, axis=1).astype(dqkv_ref.dtype)
        dk_acc[pl.ds(start, NA_WIN), :] += jnp.concatenate(dks, axis=1)
        dv_acc[pl.ds(start, NA_WIN), :] += jnp.concatenate(dvs, axis=1)

        @pl.when(blk == NA_BLOCKS - 1)
        def _():
            dqkv_ref[1] = dk_acc[...].astype(dqkv_ref.dtype)
            dqkv_ref[2] = dv_acc[...].astype(dqkv_ref.dtype)

    (dqkv, z), sent = _carrier_call(
        "na_bwd", body, (NH // NA_BWD_HPS, NA_BLOCKS),
        _na_in_specs(NA_BWD_HPS) + [pl.BlockSpec((NA_QB, lw), lambda hp, b: (b, hp))],
        [pl.BlockSpec((3, SEQ, lw), lambda hp, b: (0, 0, hp)), pl.BlockSpec((NA_BWD_HPS, 15 * GRID_W, GRID_W), lambda hp, b: (hp, 0, 0))],
        [jax.ShapeDtypeStruct((3, SEQ, DM), BF16), jax.ShapeDtypeStruct((NH, 15 * GRID_W, GRID_W), F32)],
        [pltpu.VMEM((SEQ, lw), F32), pltpu.VMEM((SEQ, lw), F32)], (qkv, qkv, qkv, bias, do), carry)
    return dqkv, z, sent


def _rpb_grad(z):
    z2 = z.reshape(NH * 15, GRID_W * GRID_W)

    def body(z_ref, e_ref, o_ref):
        o_ref[...] = jnp.dot(z_ref[...], e_ref[...], preferred_element_type=F32, precision=lax.Precision.HIGHEST)

    out = pl.pallas_call(body, out_shape=jax.ShapeDtypeStruct((NH * 15, 128), F32), name="rpb_grad",
                         compiler_params=pltpu.CompilerParams(vmem_limit_bytes=VMEM_LIMIT))(z2, _diag_onehot())
    return out[:, :31].reshape(NH, 15, 31)


DIL_BLOCKS = SEQ // DIL_QB
DIL_HPS = 8
DIL_LW = DIL_HPS * HD
DIL_NLW = DM // DIL_LW


COLS = 128


def _col_spec():
    return pl.BlockSpec((SEQ, COLS), lambda j: (0, j))


def _grp_spec():
    return pl.BlockSpec((3, SEQ, COLS), lambda j: (0, 0, j))


def _store_group_order(dst_ref, src_ref):
    for g, d in enumerate(DIL):
        n = SEQ // d
        for r in range(d):
            dst_ref[g, r * n:(r + 1) * n, :] = src_ref[pl.ds(r, n, stride=d), :].astype(dst_ref.dtype)


def _store_token_order(dst_ref, src_ref, g):
    d = DIL[g]
    n = SEQ // d
    for r in range(d):
        dst_ref[pl.ds(r, n, stride=d), :] = src_ref[g, r * n:(r + 1) * n, :]


def _to_groups(name, a):
    def body(a_ref, o_ref, t_ref):
        _store_group_order(o_ref, a_ref)
        for g in range(3):
            t_ref[g] = o_ref[g].astype(F32).T.astype(t_ref.dtype)

    return pl.pallas_call(
        body, grid=(DM // COLS,), in_specs=[_col_spec()], out_specs=[_grp_spec(), pl.BlockSpec((3, COLS, SEQ), lambda j: (0, j, 0))],
        out_shape=[jax.ShapeDtypeStruct((3, SEQ, DM), BF16), jax.ShapeDtypeStruct((3, DM, SEQ), BF16)],
        compiler_params=_params(("parallel",)), name=name)(a)


def _from_groups_sum(name, a):
    def body(a_ref, o_ref, t1, t2):
        _store_token_order(t1, a_ref, 1)
        _store_token_order(t2, a_ref, 2)
        o_ref[...] = (a_ref[0] + t1[...]) + t2[...]

    return pl.pallas_call(body, grid=(DM // COLS,), in_specs=[_grp_spec()], out_specs=_col_spec(),
                          out_shape=jax.ShapeDtypeStruct((SEQ, DM), F32), scratch_shapes=[pltpu.VMEM((SEQ, COLS), F32)] * 2,
                          compiler_params=_params(("parallel",)), name=name)(a)


def _dil_start(b):
    return pl.multiple_of(jnp.clip(b * DIL_QB - DIL_RADIUS, 0, SEQ - DIL_WIN), DIL_RADIUS)


def _dil_neg_dist(g, ii, jj):
    shift = 11 - 2 * g
    dist = jnp.abs(ii - jj)
    valid = (dist <= DIL_RADIUS) & (jnp.right_shift(ii, shift) == jnp.right_shift(jj, shift))
    return jnp.where(valid, -dist.astype(F32), NEG)


def _dil_in_specs():
    return [pl.BlockSpec(memory_space=pltpu.SMEM),
            pl.BlockSpec((None, DIL_QB, DIL_LW), lambda g, hp, b: (g, b, hp)),
            pl.BlockSpec((None, SEQ, DIL_LW), lambda g, hp, b: (g, 0, DIL_NLW + hp)),
            pl.BlockSpec((None, SEQ, DIL_LW), lambda g, hp, b: (g, 0, 2 * DIL_NLW + hp))]


def _dil_fwd(qkv, slopes, carry):
    def body(sl_ref, q_ref, k_ref, v_ref, o_ref, lse_ref):
        g, hp, b = pl.program_id(0), pl.program_id(1), pl.program_id(2)
        start = _dil_start(b)
        neg_dist = _dil_neg_dist(g, b * DIL_QB + lax.broadcasted_iota(jnp.int32, (DIL_QB, DIL_WIN), 0),
                                 start + lax.broadcasted_iota(jnp.int32, (DIL_QB, DIL_WIN), 1))
        dil = jnp.left_shift(1, 2 * g).astype(F32)
        q = q_ref[...]
        kw = k_ref[pl.ds(start, DIL_WIN), :]
        vw = v_ref[pl.ds(start, DIL_WIN), :]
        outs, lses = [], []
        for hh in range(DIL_HPS):
            sl = slice(hh * HD, (hh + 1) * HD)
            s = lax.dot_general(q[:, sl] * QSCALE, kw[:, sl], _NT, preferred_element_type=F32)
            s = s + (sl_ref[hp * DIL_HPS + hh] * dil) * neg_dist
            m = jnp.max(s, axis=-1, keepdims=True)
            p = jnp.exp(s - m)
            l = jnp.sum(p, axis=-1, keepdims=True)
            outs.append(jnp.dot(p.astype(BF16), vw[:, sl], preferred_element_type=F32) / l)
            lses.append(jnp.broadcast_to(m + jnp.log(l), (DIL_QB, HD)))
        o_ref[...] = jnp.concatenate(outs, axis=1)
        lse_ref[...] = jnp.concatenate(lses, axis=1)

    ospec = pl.BlockSpec((None, DIL_QB, DIL_LW), lambda g, hp, b: (g, b, hp))
    sh = jax.ShapeDtypeStruct((3, SEQ, DM), F32)
    (o, lse), sent = _carrier_call("dil_fwd", body, (3, DIL_NLW, DIL_BLOCKS), _dil_in_specs(), [ospec, ospec], [sh, sh], [],
                                   (slopes, qkv, qkv, qkv), carry)
    return o, lse, sent


def _dil_merge(o_all, lse_all):
    def body(o_ref, l_ref, out_ref, lse_ref, o1, o2, l1, l2):
        for g, (ot, lt) in ((1, (o1, l1)), (2, (o2, l2))):
            _store_token_order(ot, o_ref, g)
            _store_token_order(lt, l_ref, g)
        la, lb, lc = l_ref[0], l1[...], l2[...]
        m = jnp.maximum(jnp.maximum(la, lb), lc)
        wa, wb, wc = jnp.exp(la - m), jnp.exp(lb - m), jnp.exp(lc - m)
        sw = (wa + wb) + wc
        out_ref[...] = (((wa * o_ref[0] + wb * o1[...]) + wc * o2[...]) / sw).astype(out_ref.dtype)
        lse_ref[...] = m + jnp.log(sw)

    return pl.pallas_call(
        body, grid=(DM // COLS,), in_specs=[_grp_spec(), _grp_spec()], out_specs=[_col_spec(), _col_spec()],
        out_shape=[jax.ShapeDtypeStruct((SEQ, DM), BF16), jax.ShapeDtypeStruct((SEQ, DM), F32)],
        scratch_shapes=[pltpu.VMEM((SEQ, COLS), F32)] * 4, compiler_params=_params(("parallel",)), name="dil_merge")(o_all, lse_all)


def _dil_bwd_prep(do, o, lse):
    heads = COLS // HD

    def body(do_ref, o_ref, lse_ref, dog_ref, ddr_ref, lser_ref, dd, grp):
        prod = do_ref[...] * o_ref[...].astype(F32)
        dd[...] = jnp.concatenate(
            [jnp.broadcast_to(jnp.sum(prod[:, h * HD:(h + 1) * HD], axis=-1, keepdims=True), (SEQ, HD)) for h in range(heads)], axis=1)
        _store_group_order(dog_ref, do_ref)
        for src, dst in ((dd, ddr_ref), (lse_ref, lser_ref)):
            _store_group_order(grp, src)
            for g in range(3):
                t = grp[g].T
                for h in range(heads):
                    dst[g, h] = t[h * HD:h * HD + 8, :]

    rows = jax.ShapeDtypeStruct((3, NH, 8, SEQ), F32)
    rspec = pl.BlockSpec((3, heads, 8, SEQ), lambda j: (0, j, 0, 0))
    return pl.pallas_call(
        body, grid=(DM // COLS,), in_specs=[_col_spec()] * 3, out_specs=[_grp_spec(), rspec, rspec],
        out_shape=[jax.ShapeDtypeStruct((3, SEQ, DM), BF16), rows, rows],
        scratch_shapes=[pltpu.VMEM((SEQ, COLS), F32), pltpu.VMEM((3, SEQ, COLS), F32)],
        compiler_params=_params(("parallel",)), name="dil_bwd_prep")(do, o, lse)


def _dil_bwd(qkv, do, dd, lse, slopes, carry):
    def body(sl_ref, q_ref, k_ref, v_ref, do_ref, dd_ref, lse_ref, dqkv_ref, dk_acc, dv_acc):
        g, hp, b = pl.program_id(0), pl.program_id(1), pl.program_id(2)

        @pl.when(b == 0)
        def _():
            dk_acc[...] = jnp.zeros_like(dk_acc)
            dv_acc[...] = jnp.zeros_like(dv_acc)

        start = _dil_start(b)
        neg_dist = _dil_neg_dist(g, b * DIL_QB + lax.broadcasted_iota(jnp.int32, (DIL_WIN, DIL_QB), 1),
                                 start + lax.broadcasted_iota(jnp.int32, (DIL_WIN, DIL_QB), 0))
        dil = jnp.left_shift(1, 2 * g).astype(F32)
        q = q_ref[...]
        do = do_ref[...]
        kw = k_ref[pl.ds(start, DIL_WIN), :]
        vw = v_ref[pl.ds(start, DIL_WIN), :]
        dqs, dks, dvs = [], [], []
        for hh in range(DIL_HPS):
            sl = slice(hh * HD, (hh + 1) * HD)
            qh = q[:, sl] * QSCALE
            st = lax.dot_general(kw[:, sl], qh, _NT, preferred_element_type=F32)
            st = st + (sl_ref[hp * DIL_HPS + hh] * dil) * neg_dist
            pt = jnp.exp(st - lse_ref[hh, 0:1, :])
            dpt = lax.dot_general(vw[:, sl], do[:, sl], _NT, preferred_element_type=F32)
            dst = (pt * (dpt - dd_ref[hh, 0:1, :])).astype(BF16)
            dqs.append(lax.dot_general(kw[:, sl], dst, _TN, preferred_element_type=F32).T * QSCALE)
            dks.append(jnp.dot(dst, qh, preferred_element_type=F32))
            dvs.append(jnp.dot(pt.astype(BF16), do[:, sl], preferred_element_type=F32))
        dqkv_ref[0, pl.ds(pl.multiple_of(b * DIL_QB, DIL_QB), DIL_QB), :] = jnp.concatenate(dqs, axis=1).astype(dqkv_ref.dtype)
        dk_acc[pl.ds(start, DIL_WIN), :] += jnp.concatenate(dks, axis=1)
        dv_acc[pl.ds(start, DIL_WIN), :] += jnp.concatenate(dvs, axis=1)

        @pl.when(b == DIL_BLOCKS - 1)
        def _():
            dqkv_ref[1] = dk_acc[...].astype(dqkv_ref.dtype)
            dqkv_ref[2] = dv_acc[...].astype(dqkv_ref.dtype)

    qspec = pl.BlockSpec((None, DIL_QB, DIL_LW), lambda g, hp, b: (g, b, hp))
    rspec = pl.BlockSpec((None, DIL_HPS, 8, DIL_QB), lambda g, hp, b: (g, hp, 0, b))
    (dqkv,), sent = _carrier_call(
        "dil_bwd", body, (3, DIL_NLW, DIL_BLOCKS), _dil_in_specs() + [qspec, rspec, rspec],
        [pl.BlockSpec((None, 3, SEQ, DIL_LW), lambda g, hp, b: (g, 0, 0, hp))], [jax.ShapeDtypeStruct((3, 3, SEQ, DM), BF16)],
        [pltpu.VMEM((SEQ, DIL_LW), F32), pltpu.VMEM((SEQ, DIL_LW), F32)], (slopes, qkv, qkv, qkv, do, dd, lse), carry)
    return dqkv, sent


def _ffn_fwd(name, x, g_pre, g_post, wgt4, wut4, wd4, carry):
    tm = 512

    def body(x_ref, gpre_ref, gpost_ref, wg_ref, wu_ref, wd_ref, xn_ref, h_ref, gate_ref, up_ref, u_ref, acc):
        s = pl.program_id(1)

        @pl.when(s == 0)
        def _():
            x = x_ref[...]
            r = lax.rsqrt(jnp.mean(x * x, axis=-1, keepdims=True) + RMS_EPS)
            h_ref[...] = (x * r * gpre_ref[...]).astype(h_ref.dtype)

        h = h_ref[...]
        gate = lax.dot_general(h, wg_ref[...], _NT, preferred_element_type=F32).astype(BF16)
        up = lax.dot_general(h, wu_ref[...], _NT, preferred_element_type=F32).astype(BF16)
        gate_ref[...] = gate
        up_ref[...] = up
        gf = gate.astype(F32)
        act = (gf * jax.nn.sigmoid(gf) * up.astype(F32)).astype(BF16)
        part = jnp.dot(act, wd_ref[...], preferred_element_type=F32)

        @pl.when(s == 0)
        def _():
            acc[...] = part

        @pl.when(s > 0)
        def _():
            acc[...] += part

        @pl.when(s == NCHIP - 1)
        def _():
            u = acc[...]
            u_ref[...] = u
            r = lax.rsqrt(jnp.mean(u * u, axis=-1, keepdims=True) + RMS_EPS)
            xn_ref[...] = x_ref[...] + u * r * gpost_ref[...]

    rows = pl.BlockSpec((tm, DM), lambda i, s: (i, 0))
    vec = pl.BlockSpec((1, DM), lambda i, s: (0, 0))
    wspec = _ffn_wspec(lambda i, s: (s, 0, 0))
    mid = pl.BlockSpec((None, tm, FSH), lambda i, s: (s, i, 0))
    outs, sent = _carrier_call(
        name, body, (SEQ // tm, NCHIP), [rows, vec, vec, wspec, wspec, wspec], [rows, rows, mid, mid, rows],
        [jax.ShapeDtypeStruct((SEQ, DM), F32), jax.ShapeDtypeStruct((SEQ, DM), BF16), jax.ShapeDtypeStruct((NCHIP, SEQ, FSH), BF16),
         jax.ShapeDtypeStruct((NCHIP, SEQ, FSH), BF16), jax.ShapeDtypeStruct((SEQ, DM), F32)],
        [pltpu.VMEM((tm, DM), F32)], (x, g_pre, g_post, wgt4, wut4, wd4), carry)
    return outs, sent


def _ffn_block(layer, x, g_pre, g_post, ex):
    tag = f"l{layer}_ffn_fwd"
    (x_new, h, gate, up, u), sent = _ffn_fwd(tag, x, g_pre, g_post, ex.weight(("ffn_w_gate", layer)), ex.weight(("ffn_w_up", layer)),
                                             ex.weight(("ffn_w_down", layer)), ex.carry(tag))
    ex.carried(tag, sent)
    return x_new, (x, h, gate, up, u)


def _ffn_bwd(name, dx, x, gate, up, u, g_pre, g_post, wgt4, wut4, wd4, carry):
    tm = 512

    def body(dx_ref, x_ref, gate_ref, up_ref, u_ref, gpre_ref, gpost_ref, wg_ref, wu_ref, wd_ref,
             dxin_ref, du_ref, dgate_ref, dup_ref, act_ref, dgpre_ref, dgpost_ref, dh_acc):
        i, s = pl.program_id(0), pl.program_id(1)

        @pl.when((i == 0) & (s == 0))
        def _():
            dgpre_ref[...] = jnp.zeros_like(dgpre_ref)
            dgpost_ref[...] = jnp.zeros_like(dgpost_ref)

        @pl.when(s == 0)
        def _():
            dy = dx_ref[...]
            uu = u_ref[...]
            r = lax.rsqrt(jnp.mean(uu * uu, axis=-1, keepdims=True) + RMS_EPS)
            yh = uu * r
            t = dy * gpost_ref[...]
            du_ref[...] = (r * (t - yh * jnp.mean(t * yh, axis=-1, keepdims=True))).astype(du_ref.dtype)
            dgpost_ref[...] += jnp.sum(dy * yh, axis=0, keepdims=True)

        dact = lax.dot_general(du_ref[...], wd_ref[...], _NT, preferred_element_type=F32)
        g = gate_ref[...].astype(F32)
        upv = up_ref[...].astype(F32)
        sg = jax.nn.sigmoid(g)
        dgate = (dact * upv * sg * (1.0 + g * (1.0 - sg))).astype(BF16)
        dup = (dact * g * sg).astype(BF16)
        dgate_ref[...] = dgate
        dup_ref[...] = dup
        act_ref[...] = (g * sg * upv).astype(act_ref.dtype)
        part = jnp.dot(dgate, wg_ref[...], preferred_element_type=F32) + jnp.dot(dup, wu_ref[...], preferred_element_type=F32)

        @pl.when(s == 0)
        def _():
            dh_acc[...] = part

        @pl.when(s > 0)
        def _():
            dh_acc[...] += part

        @pl.when(s == NCHIP - 1)
        def _():
            dh = dh_acc[...]
            xx = x_ref[...]
            r = lax.rsqrt(jnp.mean(xx * xx, axis=-1, keepdims=True) + RMS_EPS)
            yh = xx * r
            t = dh * gpre_ref[...]
            dxin_ref[...] = dx_ref[...] + r * (t - yh * jnp.mean(t * yh, axis=-1, keepdims=True))
            dgpre_ref[...] += jnp.sum(dh * yh, axis=0, keepdims=True)

    rows = pl.BlockSpec((tm, DM), lambda i, s: (i, 0))
    vec = pl.BlockSpec((1, DM), lambda i, s: (0, 0))
    wspec = _ffn_wspec(lambda i, s: (s, 0, 0))
    mid = pl.BlockSpec((None, tm, FSH), lambda i, s: (s, i, 0))
    mid_shape = jax.ShapeDtypeStruct((NCHIP, SEQ, FSH), BF16)
    return _carrier_call(
        name, body, (SEQ // tm, NCHIP), [rows, rows, mid, mid, rows, vec, vec, wspec, wspec, wspec], [rows, rows, mid, mid, mid, vec, vec],
        [jax.ShapeDtypeStruct((SEQ, DM), F32), jax.ShapeDtypeStruct((SEQ, DM), BF16), mid_shape, mid_shape, mid_shape,
         jax.ShapeDtypeStruct((1, DM), F32), jax.ShapeDtypeStruct((1, DM), F32)],
        [pltpu.VMEM((tm, DM), F32)], (dx, x, gate, up, u, g_pre, g_post, wgt4, wut4, wd4), carry)


def _ffn_block_bwd(layer, dx, saved, g_pre, g_post, ex):
    tag = f"l{layer}"
    x, h, gate, up, u = saved
    (dx_in, du, dgate, dup, act, dg_pre, dg_post), sent = _ffn_bwd(
        f"{tag}_ffn_bwd", dx, x, gate, up, u, g_pre, g_post, ex.weight(("ffn_w_gate", layer)), ex.weight(("ffn_w_up", layer)),
        ex.weight(("ffn_w_down", layer)), ex.carry(f"{tag}_ffn_bwd"))
    ex.carried(f"{tag}_ffn_bwd", sent)
    d_wd = _ffn_bwd_dw(f"{tag}_dwd", act, du)
    d_wg = _ffn_bwd_dw(f"{tag}_dwg", dgate, h)
    d_wu = _ffn_bwd_dw(f"{tag}_dwu", dup, h)
    ex.grads(f"{tag}_ffn", {("ffn_w_gate", layer): d_wg, ("ffn_w_up", layer): d_wu, ("ffn_w_down", layer): d_wd})
    return dx_in, dg_pre, dg_post


def _alibi_slopes():
    return 2.0 ** (-8.0 * jnp.arange(1, NH + 1, dtype=F32) / NH)


def _local_step(x, target, norms, rpb, ex):
    g_mix_pre, g_mix_post, g_ffn_pre, g_ffn_post = norms
    row = lambda a, i: a[i:i + 1]

    bias, sent = _na_bias_tiles(rpb, ex.carry("na_bias_tiles"))
    ex.carried("na_bias_tiles", sent)
    h0, h0t = _rms_fwd_both("l0_mix_pre", x, row(g_mix_pre, 0))
    qkv0, sent = _qkv_fwd("l0_qkv", h0[None], ex.weight(("na_w_qkv", 0)), ex.carry("l0_qkv"))
    ex.carried("l0_qkv", sent)
    o0, sent = _na_fwd(qkv0[0], bias, ex.carry("na_fwd"))
    ex.carried("na_fwd", sent)
    na_wo = ex.weight(("na_w_o", 0)).reshape(DM, DM)
    x1, u0 = _proj_fwd("l0_proj", o0, na_wo, x, row(g_mix_post, 0))
    x2, ffn0 = _ffn_block(0, x1, row(g_ffn_pre, 0), row(g_ffn_post, 0), ex)

    slopes = _alibi_slopes()
    h2g, h2gt = _to_groups("l1_h_groups", _rms_fwd("l1_mix_pre", x2, row(g_mix_pre, 1), F32))
    dil_wqkv = ex.weight(("dil_w_qkv", 0))
    qkv1, sent = _qkv_fwd("l1_qkv", h2g, dil_wqkv, ex.carry("l1_qkv"))
    ex.carried("l1_qkv", sent)
    og, lg, sent = _dil_fwd(qkv1, slopes, ex.carry("dil_fwd"))
    ex.carried("dil_fwd", sent)
    o1, lse = _dil_merge(og, lg)
    dil_wo = ex.weight(("dil_w_o", 0)).reshape(DM, DM)
    x3, u1 = _proj_fwd("l1_proj", o1, dil_wo, x2, row(g_mix_post, 1))
    x4, ffn1 = _ffn_block(1, x3, row(g_ffn_pre, 1), row(g_ffn_post, 1), ex)

    dx4, loss_row = _loss_grad("loss", x4, target)

    dx3, dg_fpre1, dg_fpost1 = _ffn_block_bwd(1, dx4, ffn1, row(g_ffn_pre, 1), row(g_ffn_post, 1), ex)
    (do1, du1, dg_mpost1), sent = _proj_bwd("l1_proj_bwd", dx3, u1, row(g_mix_post, 1), dil_wo, F32, ex.carry("l1_proj_bwd"))
    ex.carried("l1_proj_bwd", sent)
    d_dil_wo = _proj_bwd_dw("l1_dwo", o1, du1)
    dog, ddg, lseg = _dil_bwd_prep(do1, o1, lse)
    dqkv1, sent = _dil_bwd(qkv1, dog, ddg, lseg, slopes, ex.carry("dil_bwd"))
    ex.carried("dil_bwd", sent)
    d_dil_wqkv, sent = _qkv_bwd_dw("l1_dwqkv", h2gt, dqkv1, dil_wqkv.shape[2], ex.carry("l1_dwqkv"))
    ex.carried("l1_dwqkv", sent)
    ex.grads("l1_mix", {("dil_w_qkv", 0): d_dil_wqkv, ("dil_w_o", 0): d_dil_wo.reshape(NCHIP, DM // NCHIP, DM)})
    dh2g, sent = _qkv_bwd_dh("l1_dh", dqkv1, dil_wqkv, ex.carry("l1_dh"))
    ex.carried("l1_dh", sent)
    dh2 = _from_groups_sum("l1_dh_tokens", dh2g)
    (dx2, dg_mpre1), sent = _norm_bwd("l1_mix_pre_bwd", dh2, x2, row(g_mix_pre, 1), dx3, ex.carry("l1_mix_pre_bwd"))
    ex.carried("l1_mix_pre_bwd", sent)

    dx1, dg_fpre0, dg_fpost0 = _ffn_block_bwd(0, dx2, ffn0, row(g_ffn_pre, 0), row(g_ffn_post, 0), ex)
    (do0, du0, dg_mpost0), sent = _proj_bwd("l0_proj_bwd", dx1, u0, row(g_mix_post, 0), na_wo, BF16, ex.carry("l0_proj_bwd"))
    ex.carried("l0_proj_bwd", sent)
    d_na_wo = _proj_bwd_dw("l0_dwo", o0, du0)
    dqkv0, z, sent = _na_bwd(qkv0[0], bias, do0, ex.carry("na_bwd"))
    ex.carried("na_bwd", sent)
    d_rpb = _rpb_grad(z)
    na_wqkv = ex.weight(("na_w_qkv", 0))
    d_na_wqkv, sent = _qkv_bwd_dw("l0_dwqkv", h0t[None], dqkv0[None], na_wqkv.shape[2], ex.carry("l0_dwqkv"))
    ex.carried("l0_dwqkv", sent)
    ex.grads("l0_mix", {("na_w_qkv", 0): d_na_wqkv, ("na_w_o", 0): d_na_wo.reshape(NCHIP, DM // NCHIP, DM)})
    dh0, sent = _qkv_bwd_dh("l0_dh", dqkv0[None], na_wqkv, ex.carry("l0_dh"))
    ex.carried("l0_dh", sent)
    (dx0, dg_mpre0), sent = _norm_bwd("l0_mix_pre_bwd", dh0[0], x, row(g_mix_pre, 0), dx1, ex.carry("l0_mix_pre_bwd"))
    ex.carried("l0_mix_pre_bwd", sent)

    dnorms = (jnp.concatenate([dg_mpre0, dg_mpre1]), jnp.concatenate([dg_mpost0, dg_mpost1]),
              jnp.concatenate([dg_fpre0, dg_fpre1]), jnp.concatenate([dg_fpost0, dg_fpost1]))
    return loss_row, dx0, dnorms, d_rpb


def _place():
    x, y, c = lax.axis_index("x"), lax.axis_index("y"), lax.axis_index("c")
    chips = ((1 - x, y), (x, 1 - y), (1 - x, 1 - y))
    return x, y, c, chips


def _chip_id(chip):
    return 2 * chip[0] + chip[1]


def _gather_copies(shards):
    n = len(shards)

    def copies(src, out, sems):
        send_sems, recv_sems = sems
        x, y, c, chips = _place()

        def copy(t, k, chip, half, to, from_src=False):
            blk = out[t].at[_chip_id(chip), half]
            return pltpu.make_async_remote_copy(
                src_ref=src[t].at[half] if from_src else blk, dst_ref=blk,
                send_sem=send_sems.at[6 * t + k], recv_sem=recv_sems.at[6 * t + k], device_id=to, device_id_type=MESH)

        return copy, x, y, c, chips

    def issue(src, out, sems):
        copy, x, y, c, chips = copies(src, out, sems)
        for t in range(n):
            for j, chip in enumerate(chips):
                copy(t, j, (x, y), c, (*chip, c), from_src=True).start()

    def drain(src, out, sems):
        copy, x, y, c, chips = copies(src, out, sems)
        passed = []
        for t in range(n):
            for j, chip in enumerate(chips):
                copy(t, j, chip, c, (x, y, c)).wait_recv()
                fwd = copy(t, 3 + j, chip, c, (x, y, 1 - c))
                fwd.start()
                passed.append(fwd)
        for t in range(n):
            for j, chip in enumerate(chips):
                copy(t, 3 + j, chip, 1 - c, (x, y, c)).wait_recv()
        for t in range(n):
            for j, chip in enumerate(chips):
                copy(t, j, (x, y), c, (*chip, c), from_src=True).wait_send()
        for cp in passed:
            cp.wait_send()

    return _Carried(shards, [jax.ShapeDtypeStruct((NCHIP,) + s.shape, s.dtype) for s in shards], (6 * n, 6 * n), issue, drain)


def _pair_exchange_copies(grads):
    n = len(grads)

    def copies(g, theirs, sems):
        send_sems, recv_sems = sems
        x, y, c, _ = _place()
        return [pltpu.make_async_remote_copy(src_ref=g[t].at[:, 1 - c], dst_ref=theirs[t], send_sem=send_sems.at[t],
                                             recv_sem=recv_sems.at[t], device_id=(x, y, 1 - c), device_id_type=MESH) for t in range(n)]

    def issue(g, theirs, sems):
        for cp in copies(g, theirs, sems):
            cp.start()

    def drain(g, theirs, sems):
        for cp in copies(g, theirs, sems):
            cp.wait()

    return _Carried(grads, [jax.ShapeDtypeStruct((NCHIP,) + g.shape[2:], g.dtype) for g in grads], (n, n), issue, drain)


def _chip_exchange_copies(items):
    flat = [(t, i, j) for t, (_, peers) in enumerate(items) for i, j in enumerate(peers)]

    def copies(p, slots, sems):
        send_sems, recv_sems = sems
        x, y, c, chips = _place()
        return [pltpu.make_async_remote_copy(src_ref=p[t].at[_chip_id(chips[j])], dst_ref=slots[t].at[i], send_sem=send_sems.at[k],
                                             recv_sem=recv_sems.at[k], device_id=(*chips[j], c), device_id_type=MESH)
                for k, (t, i, j) in enumerate(flat)]

    def issue(p, slots, sems):
        for cp in copies(p, slots, sems):
            cp.start()

    def drain(p, slots, sems):
        for cp in copies(p, slots, sems):
            cp.wait()

    return _Carried([p for p, _ in items], [jax.ShapeDtypeStruct((len(peers),) + p.shape[1:], p.dtype) for p, peers in items],
                    (len(flat), len(flat)), issue, drain)


def _pair_share_copies(halves):
    n = len(halves)

    def copies(h, other, sems):
        send_sems, recv_sems = sems
        x, y, c, _ = _place()
        return [pltpu.make_async_remote_copy(src_ref=h[t], dst_ref=other[t], send_sem=send_sems.at[t], recv_sem=recv_sems.at[t],
                                             device_id=(x, y, 1 - c), device_id_type=MESH) for t in range(n)]

    def issue(h, other, sems):
        for cp in copies(h, other, sems):
            cp.start()

    def drain(h, other, sems):
        for cp in copies(h, other, sems):
            cp.wait()

    return _Carried(halves, [jax.ShapeDtypeStruct(h.shape, h.dtype) for h in halves], (n, n), issue, drain)


SMALL_ROWS = 128


def _allreduce_small(v):
    def body(v_ref, o_ref, buf, send_sems, recv_sems):
        x, y, c, _ = _place()
        me = 4 * x + 2 * y + c
        flip = lambda a, f: 1 - a if f else a
        buf[me] = v_ref[...]
        peers = [(flip(x, d >> 2 & 1), flip(y, d >> 1 & 1), flip(c, d & 1)) for d in range(1, 8)]
        sends = [pltpu.make_async_remote_copy(src_ref=v_ref, dst_ref=buf.at[me], send_sem=send_sems.at[i], recv_sem=recv_sems.at[i],
                                              device_id=peer, device_id_type=MESH) for i, peer in enumerate(peers)]
        for cp in sends:
            cp.start()
        for i, (px, py, pc) in enumerate(peers):
            pltpu.make_async_remote_copy(src_ref=v_ref, dst_ref=buf.at[4 * px + 2 * py + pc], send_sem=send_sems.at[i], recv_sem=recv_sems.at[i],
                                         device_id=(px, py, pc), device_id_type=MESH).wait_recv()
        for cp in sends:
            cp.wait_send()
        acc = buf[0]
        for k in range(1, 8):
            acc = acc + buf[k]
        o_ref[...] = acc

    vm = pl.BlockSpec(memory_space=pltpu.VMEM)
    return pl.pallas_call(
        body, in_specs=[vm], out_specs=vm, out_shape=jax.ShapeDtypeStruct((SMALL_ROWS, 128), F32),
        scratch_shapes=[pltpu.VMEM((8, SMALL_ROWS, 128), F32), pltpu.SemaphoreType.DMA((7,)), pltpu.SemaphoreType.DMA((7,))],
        compiler_params=pltpu.CompilerParams(has_side_effects=True), name="allreduce_small")(v)


def _row_block(rows, cols, budget=3 << 19):
    best = 8
    for bm in range(8, rows + 1, 8):
        if rows % bm == 0 and bm * cols * 4 <= budget:
            best = bm
    return best


def _pair_sum(name, place, gs, theirs):
    n = len(gs)
    _, m, c = theirs[0].shape
    bm = _row_block(m, c)

    def body(place_ref, *refs):
        for a_ref, b_ref, o_ref in zip(refs[:n], refs[n:2 * n], refs[2 * n:]):
            o_ref[...] = (a_ref[...].astype(F32) + b_ref[...].astype(F32)).astype(o_ref.dtype)

    spec = pl.BlockSpec((None, bm, c), lambda k, i, pr: (k, i, 0))
    return pl.pallas_call(
        body, out_shape=[jax.ShapeDtypeStruct(theirs[0].shape, BF16)] * n,
        grid_spec=pltpu.PrefetchScalarGridSpec(
            num_scalar_prefetch=1, grid=(NCHIP, m // bm),
            in_specs=[pl.BlockSpec((None, None, bm, c), lambda k, i, pr: (k, pr[0], i, 0))] * n + [spec] * n, out_specs=[spec] * n),
        compiler_params=_params(("parallel", "parallel")), name=name)(place, *gs, *theirs)


def _chip_sum(name, place, parts, slots):
    n, ns = len(parts), len(slots[0])
    _, m, c = parts[0].shape
    bm = _row_block(m, c)

    def body(place_ref, *refs):
        for t in range(n):
            acc = refs[t][...].astype(F32)
            for s_ref in refs[n + t * ns:n + (t + 1) * ns]:
                for i in range(s_ref.shape[0]):
                    acc = acc + s_ref[i].astype(F32)
            refs[n + n * ns + t][...] = acc

    half = pl.BlockSpec((bm, c), lambda i, pr: (i, 0))
    return pl.pallas_call(
        body, out_shape=[jax.ShapeDtypeStruct((m, c), F32)] * n,
        grid_spec=pltpu.PrefetchScalarGridSpec(
            num_scalar_prefetch=1, grid=(m // bm,),
            in_specs=[pl.BlockSpec((None, bm, c), lambda i, pr: (pr[1], i, 0))] * n
            + [pl.BlockSpec((s.shape[0], bm, c), lambda i, pr: (0, i, 0)) for group in slots for s in group],
            out_specs=[half] * n),
        compiler_params=_params(("parallel",)), name=name)(place, *parts, *[s for group in slots for s in group])


def _adamw(name, place, w, g_mine, g_other, m, v, layer=0, into=None):
    lead, rows, cols = w.shape
    bm = _row_block(rows // 2, cols, budget=768 * 1024)
    per_half = rows // 2 // bm
    c1 = 1.0 - ADAM_B1 ** ADAM_STEP
    c2 = 1.0 - ADAM_B2 ** ADAM_STEP

    def body(place_ref, w_ref, ga_ref, gb_ref, m_ref, v_ref, *rest):
        go_ref, d_ref, mo_ref, vo_ref = rest[-4:]
        g = jnp.where(pl.program_id(0) // per_half == place_ref[0], ga_ref[...], gb_ref[...])
        mn = ADAM_B1 * m_ref[...] + (1.0 - ADAM_B1) * g
        vn = ADAM_B2 * v_ref[...] + (1.0 - ADAM_B2) * (g * g)
        go_ref[...] = g
        mo_ref[...] = mn
        vo_ref[...] = vn
        d_ref[...] = -ADAM_LR * ((mn / c1) / (jnp.sqrt(vn / c2) + ADAM_EPS) + ADAM_WD * w_ref[...])

    spec = pl.BlockSpec((None, bm, cols), lambda i, pr: (layer, i, 0))
    gspec = pl.BlockSpec((bm, cols), lambda i, pr: (i % per_half, 0))
    sh = jax.ShapeDtypeStruct((lead, rows, cols), F32)
    prev = [] if into is None else list(into)
    return pl.pallas_call(
        body, out_shape=[sh] * 4, input_output_aliases={6 + k: k for k in range(len(prev))},
        grid_spec=pltpu.PrefetchScalarGridSpec(
            num_scalar_prefetch=1, grid=(rows // bm,),
            in_specs=[spec, gspec, gspec, spec, spec] + [pl.BlockSpec(memory_space=pl.ANY)] * len(prev), out_specs=[spec] * 4),
        compiler_params=_params(("parallel",)), name=name)(place, w, g_mine, g_other, m, v, *prev)


def _pack_small(norms, rpb, last=None):
    flat = jnp.concatenate([a.reshape(-1) for a in norms] + [rpb.reshape(-1)])
    flat = jnp.pad(flat, (0, SMALL_ROWS * 128 - flat.shape[0]))
    if last is not None:
        flat = jnp.concatenate([flat[:-1], last.reshape(1)])
    return flat.reshape(SMALL_ROWS, 128)


def _unpack_small(p):
    flat = p.reshape(-1)
    norms = [flat[i * 2 * DM:(i + 1) * 2 * DM].reshape(2, DM) for i in range(4)]
    rpb = flat[8 * DM:8 * DM + NH * 15 * 31].reshape(1, NH, 15, 31)
    return norms, rpb


FFN_NAMES = ("ffn_w_gate", "ffn_w_up", "ffn_w_down")
L0_FFN = tuple((n, 0) for n in FFN_NAMES)
L1_FFN = tuple((n, 1) for n in FFN_NAMES)
NA_KEYS = (("na_w_qkv", 0), ("na_w_o", 0))
DIL_KEYS = (("dil_w_qkv", 0), ("dil_w_o", 0))
ALL_PEERS, NEIGHBOURS, DIAGONAL = (0, 1, 2), (0, 1), (2,)


class _Exchange:
    GATHERS = {"na_bias_tiles": NA_KEYS, "l0_qkv": L0_FFN[:1], "na_fwd": L0_FFN[1:], "l0_ffn_fwd": DIL_KEYS[:1], "dil_fwd": L1_FFN + DIL_KEYS[1:]}
    PAIRS = {"l1_proj_bwd": L1_FFN, "l1_dh": DIL_KEYS, "l0_proj_bwd": L0_FFN}
    EXCHANGES = {"dil_bwd": [(k, ALL_PEERS) for k in L1_FFN],
                 "l0_ffn_bwd": [(DIL_KEYS[0], NEIGHBOURS), (DIL_KEYS[1], ALL_PEERS)],
                 "na_bwd": [(k, ALL_PEERS) for k in L0_FFN] + [(DIL_KEYS[0], DIAGONAL)],
                 "l0_dh": [(k, NEIGHBOURS) for k in NA_KEYS],
                 "l0_mix_pre_bwd": [(k, DIAGONAL) for k in NA_KEYS]}
    SHARES = {"l1_dwqkv": L1_FFN, "l0_dwqkv": L0_FFN + DIL_KEYS}

    def __init__(self, shards):
        self.chip = 2 * lax.axis_index("x") + lax.axis_index("y")
        self.place = jnp.stack([lax.axis_index("c"), self.chip]).astype(jnp.int32)
        self.own = {k: s.reshape(2, s.shape[0] // 2, s.shape[1]).astype(BF16) for k, s in shards.items()}
        self.gathered, self.mine, self.parts, self.slots, self.full, self.other = {}, {}, {}, {}, {}, {}

    def _take(self, keys, landed):
        for k, gw in zip(keys, landed):
            self.gathered[k] = lax.dynamic_update_slice(gw, self.own[k][None], (self.chip, 0, 0, 0))

    def _sum(self, items, landed):
        runs = []
        for (k, peers), s in zip(items, landed):
            got = self.slots.setdefault(k, {})
            got[peers] = s
            if sum(len(p) for p in got) == len(ALL_PEERS):
                like = (self.parts[k].shape, tuple(sorted(got)))
                if runs and runs[-1][0] == like:
                    runs[-1][1].append(k)
                else:
                    runs.append((like, [k]))
        for (_, split), ks in runs:
            sums = _chip_sum(f"chip_sum_{ks[0][0]}_{ks[0][1]}", self.place, [self.parts[k] for k in ks],
                             [[self.slots[k][p] for p in split] for k in ks])
            self.full.update(zip(ks, sums))

    def weight(self, key):
        g = self.gathered[key]
        return g.reshape(NCHIP, 2 * g.shape[2], g.shape[3])

    def _pair_sums(self, keys, theirs):
        runs = []
        for k, t in zip(keys, theirs):
            if runs and runs[-1][0][1].shape == t.shape:
                runs[-1].append((k, t))
            else:
                runs.append([(k, t)])
        for run in runs:
            ks = [k for k, _ in run]
            sums = _pair_sum(f"pair_sum_{ks[0][0]}_{ks[0][1]}", self.place, [self.mine[k] for k in ks], [t for _, t in run])
            self.parts.update(zip(ks, sums))

    def carry(self, tag):
        if tag in self.GATHERS:
            return _gather_copies([self.own[k] for k in self.GATHERS[tag]])
        if tag in self.PAIRS:
            return _pair_exchange_copies([self.mine[k] for k in self.PAIRS[tag]])
        if tag in self.EXCHANGES:
            return _chip_exchange_copies([(self.parts[k], peers) for k, peers in self.EXCHANGES[tag]])
        if tag in self.SHARES:
            return _pair_share_copies([self.full[k] for k in self.SHARES[tag]])
        return None

    def carried(self, tag, landed):
        if tag in self.GATHERS:
            self._take(self.GATHERS[tag], landed)
        elif tag in self.PAIRS:
            self._pair_sums(self.PAIRS[tag], landed)
        elif tag in self.EXCHANGES:
            self._sum(self.EXCHANGES[tag], landed)
        elif tag in self.SHARES:
            self.other.update(zip(self.SHARES[tag], landed))

    def grads(self, tag, dw):
        for k, g in dw.items():
            self.mine[k] = g.reshape(NCHIP, 2, -1, g.shape[-1])
        if tag == "l0_mix":
            keys = tuple(dw)
            self._pair_sums(keys, _run_carried("grad_pair_exchange_last", _pair_exchange_copies([self.mine[k] for k in keys])))

    def finish(self):
        rest = tuple(k for k in self.full if k not in self.other)
        self.other.update(zip(rest, _run_carried("grad_pair_share_last", _pair_share_copies([self.full[k] for k in rest]))))
        return {k: (self.full[k], self.other[k]) for k in self.full}


def kernel(x, norm_mix_pre, norm_mix_post, norm_ffn_pre, norm_ffn_post, na_w_qkv, na_w_o, na_rpb, dil_w_qkv, dil_w_o, ffn_w_gate, ffn_w_up, ffn_w_down, loss_target, m_norm_mix_pre, m_norm_mix_post, m_norm_ffn_pre, m_norm_ffn_post, m_na_w_qkv, m_na_w_o, m_na_rpb, m_dil_w_qkv, m_dil_w_o, m_ffn_w_gate, m_ffn_w_up, m_ffn_w_down, v_norm_mix_pre, v_norm_mix_post, v_norm_ffn_pre, v_norm_ffn_post, v_na_w_qkv, v_na_w_o, v_na_rpb, v_dil_w_qkv, v_dil_w_o, v_ffn_w_gate, v_ffn_w_up, v_ffn_w_down):
    tr = lambda a: jnp.swapaxes(a, 1, 2)
    weights = {"na_w_qkv": na_w_qkv, "na_w_o": na_w_o, "dil_w_qkv": dil_w_qkv, "dil_w_o": dil_w_o,
               "ffn_w_gate": tr(ffn_w_gate), "ffn_w_up": tr(ffn_w_up), "ffn_w_down": ffn_w_down}
    m_in = {"na_w_qkv": m_na_w_qkv, "na_w_o": m_na_w_o, "dil_w_qkv": m_dil_w_qkv, "dil_w_o": m_dil_w_o,
            "ffn_w_gate": tr(m_ffn_w_gate), "ffn_w_up": tr(m_ffn_w_up), "ffn_w_down": m_ffn_w_down}
    v_in = {"na_w_qkv": v_na_w_qkv, "na_w_o": v_na_w_o, "dil_w_qkv": v_dil_w_qkv, "dil_w_o": v_dil_w_o,
            "ffn_w_gate": tr(v_ffn_w_gate), "ffn_w_up": tr(v_ffn_w_up), "ffn_w_down": v_ffn_w_down}

    ex = _Exchange({(n, l): weights[n][l] for n in weights for l in range(weights[n].shape[0])})
    norms = (norm_mix_pre, norm_mix_post, norm_ffn_pre, norm_ffn_post)
    loss_row, dx, dnorms, d_rpb = _local_step(x[0], loss_target[0], norms, na_rpb[0], ex)
    full = ex.finish()
    small = _allreduce_small(_pack_small(dnorms, d_rpb, last=loss_row[0, 0]))
    loss = small[SMALL_ROWS - 1, 127]

    out_g, out_d, out_m, out_v = {}, {}, {}, {}
    for n in weights:
        res = None
        for l in range(weights[n].shape[0]):
            res = _adamw(f"adamw_{n}_{l}", ex.place, weights[n], *full[(n, l)], m_in[n], v_in[n], l, res)
        if n in ("ffn_w_gate", "ffn_w_up"):
            res = [tr(r) for r in res]
        out_g[n], out_d[n], out_m[n], out_v[n] = res
    sm_names = ("norm_mix_pre", "norm_mix_post", "norm_ffn_pre", "norm_ffn_post", "na_rpb")
    sm = _adamw("adamw_small", jnp.zeros((2,), jnp.int32), _pack_small(norms, na_rpb)[None], small[:SMALL_ROWS // 2], small[SMALL_ROWS // 2:],
                _pack_small((m_norm_mix_pre, m_norm_mix_post, m_norm_ffn_pre, m_norm_ffn_post), m_na_rpb)[None],
                _pack_small((v_norm_mix_pre, v_norm_mix_post, v_norm_ffn_pre, v_norm_ffn_post), v_na_rpb)[None])
    for res, dst in zip(sm, (out_g, out_d, out_m, out_v)):
        ns, rp = _unpack_small(res)
        for n, a in zip(sm_names, ns + [rp]):
            dst[n] = a

    order = ("norm_mix_pre", "norm_mix_post", "norm_ffn_pre", "norm_ffn_post", "na_w_qkv", "na_w_o", "na_rpb", "dil_w_qkv", "dil_w_o",
             "ffn_w_gate", "ffn_w_up", "ffn_w_down")
    return (loss, dx[None], *[out_g[n] for n in order], *[out_d[n] for n in order], *[out_m[n] for n in order], *[out_v[n] for n in order])
```

```python
import functools

import numpy as np
import jax
import jax.numpy as jnp
from jax import lax
from jax.experimental import pallas as pl
from jax.experimental.pallas import tpu as pltpu

F32 = jnp.float32
BF16 = jnp.bfloat16

SEQ = 2048
DM = 1024
NH = 16
HD = 64
DFF = 2816
NCHIP = 4
FSH = DFF // NCHIP
GRID_W = 64
NA_QROWS = 4
NA_QB = NA_QROWS * GRID_W
NA_WROWS = 12
NA_WIN = NA_WROWS * GRID_W
DIL = (1, 4, 16)
DIL_QB = 256
DIL_WIN = DIL_QB + 128
DIL_RADIUS = 64
RMS_EPS = 1e-6
NEG = -1e30
QSCALE = HD ** -0.5
CH = 256
MESH = pl.DeviceIdType.MESH

ADAM_LR, ADAM_B1, ADAM_B2, ADAM_EPS, ADAM_WD, ADAM_STEP = 0.001, 0.9, 0.999, 1e-08, 0.01, 10

VMEM_LIMIT = 56 * 1024 * 1024

_NN = (((1,), (0,)), ((), ()))
_NT = (((1,), (1,)), ((), ()))
_TN = (((0,), (0,)), ((), ()))


def _params(sem):
    return pltpu.CompilerParams(dimension_semantics=sem, vmem_limit_bytes=VMEM_LIMIT)


def _matmul(name, pairs, grid, out_shape, out_spec, acc_shape, carrying=False, carry=None):
    nk = grid[-1]
    npair = len(pairs)
    n_in = 2 * npair

    def body(*refs):
        ins, o_ref = refs[:2 * npair], refs[n_in]
        part = None
        for p in range(npair):
            d = lax.dot_general(ins[2 * p][...].astype(BF16), ins[2 * p + 1][...].astype(BF16), pairs[p][4],
                                preferred_element_type=F32)
            part = d if part is None else part + d
        if nk == 1:
            o_ref[...] = part.astype(o_ref.dtype)
        else:
            acc_ref = refs[n_in + 1]
            kk = pl.program_id(len(grid) - 1)

            @pl.when(kk == 0)
            def _():
                acc_ref[...] = part

            @pl.when(kk > 0)
            def _():
                acc_ref[...] += part

            @pl.when(kk == nk - 1)
            def _():
                o_ref[...] = acc_ref[...].astype(o_ref.dtype)

    ops, specs = [], []
    for a, a_spec, b, b_spec, _ in pairs:
        ops += [a, b]
        specs += [a_spec, b_spec]
    (out,), sent = _carrier_call(name, body, grid, specs, [out_spec], [out_shape], [] if nk == 1 else [pltpu.VMEM(acc_shape, F32)], ops, carry)
    return (out, sent) if carrying else out


def _qkv_fwd(name, h_all, w4, carry):
    g_n = h_all.shape[0]
    per = w4.shape[2] // CH
    return _matmul(
        name, [(h_all, pl.BlockSpec((None, SEQ, DM), lambda g, q, k: (g, 0, 0)),
                w4, pl.BlockSpec((None, DM, CH), lambda g, q, k: ((g * 12 + q) // per, 0, (g * 12 + q) % per)), _NN)],
        (g_n, 12, 1), jax.ShapeDtypeStruct((g_n, SEQ, 3 * DM), BF16),
        pl.BlockSpec((None, SEQ, CH), lambda g, q, k: (g, 0, q)), None, carrying=True, carry=carry)


def _qkv_bwd_dh(name, dqkv, w4, carry):
    g_n = dqkv.shape[0]
    per = w4.shape[2] // CH
    tm = SEQ

    def pair(cb):
        chunk = lambda g, t: g * 12 + t * 4 + cb
        return (dqkv, pl.BlockSpec((None, None, tm, CH), lambda g, i, t: (g, t, i, cb)),
                w4, pl.BlockSpec((None, DM, CH), lambda g, i, t: (chunk(g, t) // per, 0, chunk(g, t) % per)), _NT)

    return _matmul(name, [pair(cb) for cb in range(4)], (g_n, SEQ // tm, 3), jax.ShapeDtypeStruct((g_n, SEQ, DM), F32),
                   pl.BlockSpec((None, tm, DM), lambda g, i, t: (g, i, 0)), (tm, DM), carrying=True, carry=carry)


def _qkv_bwd_dw(name, ht_all, dqkv, shard_cols, carry):
    g_n = dqkv.shape[0]
    per = shard_cols // CH
    return _matmul(
        name, [(ht_all, pl.BlockSpec((None, DM, SEQ), lambda qq, k: (qq // 12, 0, 0)),
                dqkv, pl.BlockSpec((None, None, SEQ, CH), lambda qq, k: (qq // 12, (qq % 12) // 4, 0, qq % 4)), _NN)],
        (g_n * 12, 1), jax.ShapeDtypeStruct((NCHIP, DM, shard_cols), BF16),
        pl.BlockSpec((None, DM, CH), lambda qq, k: (qq // per, 0, qq % per)), None, carrying=True, carry=carry)


def _proj_fwd(name, o, wo, x, g):
    tm = 512

    def body(o_ref, w_ref, x_ref, g_ref, xn_ref, u_ref):
        u = jnp.dot(o_ref[...], w_ref[...], preferred_element_type=F32)
        u_ref[...] = u
        r = lax.rsqrt(jnp.mean(u * u, axis=-1, keepdims=True) + RMS_EPS)
        xn_ref[...] = x_ref[...] + u * r * g_ref[...]

    rows = pl.BlockSpec((tm, DM), lambda i: (i, 0))
    sh = jax.ShapeDtypeStruct((SEQ, DM), F32)
    return pl.pallas_call(
        body, grid=(SEQ // tm,), in_specs=[rows, pl.BlockSpec((DM, DM), lambda i: (0, 0)), rows, pl.BlockSpec((1, DM), lambda i: (0, 0))],
        out_specs=[rows, rows], out_shape=[sh, sh], compiler_params=_params(("parallel",)), name=name)(o, wo, x, g)


def _proj_bwd(name, dy, u, g, wo, dtype, carry):
    tm = 512

    def body(dy_ref, u_ref, g_ref, w_ref, do_ref, du_ref, dg_ref):
        dy = dy_ref[...]
        u = u_ref[...]
        r = lax.rsqrt(jnp.mean(u * u, axis=-1, keepdims=True) + RMS_EPS)
        yh = u * r
        t = dy * g_ref[...]
        du = (r * (t - yh * jnp.mean(t * yh, axis=-1, keepdims=True))).astype(BF16)
        du_ref[...] = du
        do_ref[...] = lax.dot_general(du, w_ref[...], _NT, preferred_element_type=F32).astype(do_ref.dtype)

        @pl.when(pl.program_id(0) == 0)
        def _():
            dg_ref[...] = jnp.zeros_like(dg_ref)

        dg_ref[...] += jnp.sum(dy * yh, axis=0, keepdims=True)

    rows = pl.BlockSpec((tm, DM), lambda i: (i, 0))
    vec = pl.BlockSpec((1, DM), lambda i: (0, 0))
    return _carrier_call(
        name, body, (SEQ // tm,), [rows, rows, vec, pl.BlockSpec((DM, DM), lambda i: (0, 0))], [rows, rows, vec],
        [jax.ShapeDtypeStruct((SEQ, DM), dtype), jax.ShapeDtypeStruct((SEQ, DM), BF16), jax.ShapeDtypeStruct((1, DM), F32)],
        [], (dy, u, g, wo), carry)


def _proj_bwd_dw(name, o, du):
    tn = 512
    return _matmul(
        name, [(o, pl.BlockSpec((SEQ, DM), lambda j, k: (0, 0)), du, pl.BlockSpec((SEQ, tn), lambda j, k: (0, j)), _TN)],
        (DM // tn, 1), jax.ShapeDtypeStruct((DM, DM), BF16), pl.BlockSpec((DM, tn), lambda j, k: (0, j)), None)


def _ffn_wspec(index_map):
    return pl.BlockSpec((None, FSH, DM), index_map)


def _ffn_bwd_dw(name, a4, b):
    return _matmul(
        name, [(a4, pl.BlockSpec((None, SEQ, FSH), lambda s, k: (s, 0, 0)), b, pl.BlockSpec((SEQ, DM), lambda s, k: (0, 0)), _TN)],
        (NCHIP, 1), jax.ShapeDtypeStruct((NCHIP, FSH, DM), BF16), _ffn_wspec(lambda s, k: (s, 0, 0)), None)


ROWS = 256


def _row_spec():
    return pl.BlockSpec((ROWS, DM), lambda i: (i, 0))


def _vec_spec():
    return pl.BlockSpec((1, DM), lambda i: (0, 0))


def _rms_fwd(name, x, g, dtype=BF16):
    def body(x_ref, g_ref, o_ref):
        x = x_ref[...]
        r = lax.rsqrt(jnp.mean(x * x, axis=-1, keepdims=True) + RMS_EPS)
        o_ref[...] = (x * r * g_ref[...]).astype(o_ref.dtype)

    return pl.pallas_call(body, grid=(SEQ // ROWS,), in_specs=[_row_spec(), _vec_spec()], out_specs=_row_spec(),
                          out_shape=jax.ShapeDtypeStruct((SEQ, DM), dtype), compiler_params=_params(("parallel",)), name=name)(x, g)


def _rms_fwd_both(name, x, g):
    def body(x_ref, g_ref, o_ref, t_ref):
        x = x_ref[...]
        r = lax.rsqrt(jnp.mean(x * x, axis=-1, keepdims=True) + RMS_EPS)
        h = x * r * g_ref[...]
        o_ref[...] = h.astype(o_ref.dtype)
        t_ref[...] = h.T.astype(t_ref.dtype)

    return pl.pallas_call(
        body, grid=(SEQ // ROWS,), in_specs=[_row_spec(), _vec_spec()], out_specs=[_row_spec(), pl.BlockSpec((DM, ROWS), lambda i: (0, i))],
        out_shape=[jax.ShapeDtypeStruct((SEQ, DM), BF16), jax.ShapeDtypeStruct((DM, SEQ), BF16)],
        compiler_params=_params(("parallel",)), name=name)(x, g)


def _norm_bwd(name, dy, u, g, res, carry):
    def body(dy_ref, u_ref, g_ref, res_ref, du_ref, dg_ref):
        dy = dy_ref[...]
        u = u_ref[...]
        r = lax.rsqrt(jnp.mean(u * u, axis=-1, keepdims=True) + RMS_EPS)
        yh = u * r
        t = dy * g_ref[...]
        du_ref[...] = r * (t - yh * jnp.mean(t * yh, axis=-1, keepdims=True)) + res_ref[...]

        @pl.when(pl.program_id(0) == 0)
        def _():
            dg_ref[...] = jnp.zeros_like(dg_ref)

        dg_ref[...] += jnp.sum(dy * yh, axis=0, keepdims=True)

    return _carrier_call(
        name, body, (SEQ // ROWS,), [_row_spec(), _row_spec(), _vec_spec(), _row_spec()], [_row_spec(), _vec_spec()],
        [jax.ShapeDtypeStruct((SEQ, DM), F32), jax.ShapeDtypeStruct((1, DM), F32)], [], (dy, u, g, res), carry)


def _loss_grad(name, y, t):
    def body(y_ref, t_ref, dy_ref, l_ref):
        e = y_ref[...] - t_ref[...]
        dy_ref[...] = e * (1.0 / DM)

        @pl.when(pl.program_id(0) == 0)
        def _():
            l_ref[...] = jnp.zeros_like(l_ref)

        l_ref[...] += jnp.sum(e * e) * (0.5 / DM)

    return pl.pallas_call(
        body, grid=(SEQ // ROWS,), in_specs=[_row_spec(), _row_spec()],
        out_specs=[_row_spec(), pl.BlockSpec((1, 128), lambda i: (0, 0))],
        out_shape=[jax.ShapeDtypeStruct((SEQ, DM), F32), jax.ShapeDtypeStruct((1, 128), F32)],
        compiler_params=_params(("arbitrary",)), name=name)(y, t)


HBM_SPEC = pl.BlockSpec(memory_space=pltpu.HBM)


class _Carried:
    def __init__(self, ins, out_shapes, n_sems, issue, drain):
        self.ins, self.out_shapes, self.n_sems, self.issue, self.drain = list(ins), list(out_shapes), tuple(n_sems), issue, drain


def _carrier_call(name, body, grid, in_specs, out_specs, out_shape, scratch_shapes, operands, carry):
    n_in, n_out, n_scr = len(in_specs), len(out_specs), len(scratch_shapes)
    if carry is None:
        res = pl.pallas_call(body, grid=grid, in_specs=in_specs, out_specs=out_specs, out_shape=out_shape, scratch_shapes=scratch_shapes,
                             compiler_params=_params(("arbitrary",) * len(grid)), name=name)(*operands)
        return list(res), []
    ci, co = len(carry.ins), len(carry.out_shapes)

    def wrapped(*refs):
        ins, cins = refs[:n_in], refs[n_in:n_in + ci]
        outs, couts = refs[n_in + ci:n_in + ci + n_out], refs[n_in + ci + n_out:n_in + ci + n_out + co]
        scr, sems = refs[n_in + ci + n_out + co:n_in + ci + n_out + co + n_scr], refs[n_in + ci + n_out + co + n_scr:]
        first = functools.reduce(jnp.logical_and, [pl.program_id(a) == 0 for a in range(len(grid))])
        last = functools.reduce(jnp.logical_and, [pl.program_id(a) == grid[a] - 1 for a in range(len(grid))])

        @pl.when(first)
        def _():
            carry.issue(cins, couts, sems)

        body(*ins, *outs, *scr)

        @pl.when(last)
        def _():
            carry.drain(cins, couts, sems)

    res = pl.pallas_call(
        wrapped, grid=grid, in_specs=list(in_specs) + [HBM_SPEC] * ci, out_specs=list(out_specs) + [HBM_SPEC] * co,
        out_shape=list(out_shape) + carry.out_shapes,
        scratch_shapes=list(scratch_shapes) + [pltpu.SemaphoreType.DMA((k,)) for k in carry.n_sems],
        compiler_params=pltpu.CompilerParams(dimension_semantics=("arbitrary",) * len(grid), vmem_limit_bytes=VMEM_LIMIT, has_side_effects=True),
        name=name)(*operands, *carry.ins)
    return list(res[:n_out]), list(res[n_out:])


def _run_carried(name, carry):
    def body(*refs):
        ci, co = len(carry.ins), len(carry.out_shapes)
        carry.issue(refs[:ci], refs[ci:ci + co], refs[ci + co:])
        carry.drain(refs[:ci], refs[ci:ci + co], refs[ci + co:])

    return pl.pallas_call(
        body, in_specs=[HBM_SPEC] * len(carry.ins), out_specs=[HBM_SPEC] * len(carry.out_shapes), out_shape=carry.out_shapes,
        scratch_shapes=[pltpu.SemaphoreType.DMA((k,)) for k in carry.n_sems],
        compiler_params=pltpu.CompilerParams(has_side_effects=True), name=name)(*carry.ins)


NA_BLOCKS = SEQ // NA_QB
NA_ROWS_TOTAL = SEQ // GRID_W
NA_CLASSES = ((0, 0), (8, 4), (NA_ROWS_TOTAL - NA_QROWS, NA_ROWS_TOTAL - NA_WROWS))


def _na_pairs(i0, ws):
    out = []
    for qi in range(NA_QROWS):
        i = i0 + qi
        rs = min(max(i - 4, 0), NA_ROWS_TOTAL - 8)
        for kr in range(NA_WROWS):
            r = ws + kr
            if rs <= r < rs + 8:
                out.append((qi, kr, r - i + 7))
    return out


def _diag_onehot():
    qc, kc = np.meshgrid(np.arange(GRID_W), np.arange(GRID_W), indexing="ij")
    e = np.zeros((GRID_W * GRID_W, 128), np.float32)
    j = (kc - qc + 15).reshape(-1)
    ok = (j >= 0) & (j <= 30)
    e[np.arange(GRID_W * GRID_W)[ok], j[ok]] = 1.0
    return jnp.asarray(e)


def _rpb_expand(rpb):
    r2 = jnp.pad(rpb.reshape(NH * 15, 31), ((0, 0), (0, 128 - 31)))

    def body(r_ref, e_ref, o_ref):
        o_ref[...] = lax.dot_general(r_ref[...], e_ref[...], _NT, preferred_element_type=F32, precision=lax.Precision.HIGHEST)

    out = pl.pallas_call(body, out_shape=jax.ShapeDtypeStruct((NH * 15, GRID_W * GRID_W), F32), name="rpb_expand",
                         compiler_params=pltpu.CompilerParams(vmem_limit_bytes=VMEM_LIMIT))(r2, _diag_onehot())
    return out.reshape(NH, 15, GRID_W, GRID_W)


def _na_bias_tiles(rpb, carry):
    def body(b_ref, o_ref):
        qc = lax.broadcasted_iota(jnp.int32, (GRID_W, GRID_W), 0)
        kc = lax.broadcasted_iota(jnp.int32, (GRID_W, GRID_W), 1)
        first = jnp.clip(qc - 8, 0, GRID_W - 16)
        in_window = (kc >= first) & (kc < first + 16)
        neg = jnp.full((GRID_W, GRID_W), NEG, F32)
        for cls, (i0, ws) in enumerate(NA_CLASSES):
            @pl.when(pl.program_id(0) == cls)
            def _(i0=i0, ws=ws):
                pairs = {(qi, kr): dr for qi, kr, dr in _na_pairs(i0, ws)}
                masked = {dr: jnp.where(in_window, b_ref[dr], NEG) for dr in sorted(set(pairs.values()))}
                for qi in range(NA_QROWS):
                    for k2 in range(NA_WROWS // 2):
                        blocks = [masked[pairs[(qi, kr)]] if (qi, kr) in pairs else neg for kr in (2 * k2, 2 * k2 + 1)]
                        o_ref[qi * GRID_W:(qi + 1) * GRID_W, k2 * 128:(k2 + 1) * 128] = jnp.concatenate(blocks, axis=1)

    (tiles,), sent = _carrier_call(
        "na_bias_tiles", body, (3, NH), [pl.BlockSpec((None, 15, GRID_W, GRID_W), lambda c, h: (h, 0, 0, 0))],
        [pl.BlockSpec((None, None, NA_QB, NA_WIN), lambda c, h: (c, h, 0, 0))], [jax.ShapeDtypeStruct((3, NH, NA_QB, NA_WIN), F32)],
        [], (_rpb_expand(rpb),), carry)
    return tiles, sent


def _na_cls(b):
    return jnp.where(b == 0, 0, jnp.where(b == NA_BLOCKS - 1, 2, 1))


def _na_start(b):
    return pl.multiple_of(jnp.clip(b * NA_QROWS - 4, 0, NA_ROWS_TOTAL - NA_WROWS) * GRID_W, GRID_W)


HPS = 4
LW = HPS * HD
NLW = DM // LW


NA_BWD_HPS = 4


def _na_in_specs(hps=HPS):
    lw = hps * HD
    nlw = DM // lw
    return [pl.BlockSpec((NA_QB, lw), lambda hp, b: (b, hp)),
            pl.BlockSpec((SEQ, lw), lambda hp, b: (0, nlw + hp)),
            pl.BlockSpec((SEQ, lw), lambda hp, b: (0, 2 * nlw + hp)),
            pl.BlockSpec((None, hps, NA_QB, NA_WIN), lambda hp, b: (_na_cls(b), hp, 0, 0))]


def _na_fwd(qkv, bias, carry):
    def body(q_ref, k_ref, v_ref, b_ref, o_ref):
        start = _na_start(pl.program_id(1))
        q = q_ref[...]
        kw = k_ref[pl.ds(start, NA_WIN), :]
        vw = v_ref[pl.ds(start, NA_WIN), :]
        outs = []
        for hh in range(HPS):
            sl = slice(hh * HD, (hh + 1) * HD)
            s = lax.dot_general(q[:, sl] * QSCALE, kw[:, sl], _NT, preferred_element_type=F32) + b_ref[hh]
            p = jnp.exp(s - jnp.max(s, axis=-1, keepdims=True))
            l = jnp.sum(p, axis=-1, keepdims=True)
            outs.append(jnp.dot(p.astype(BF16), vw[:, sl], preferred_element_type=F32) / l)
        o_ref[...] = jnp.concatenate(outs, axis=1).astype(o_ref.dtype)

    (o,), sent = _carrier_call(
        "na_fwd", body, (NLW, NA_BLOCKS), _na_in_specs(), [pl.BlockSpec((NA_QB, LW), lambda hp, b: (b, hp))],
        [jax.ShapeDtypeStruct((SEQ, DM), BF16)], [], (qkv, qkv, qkv, bias), carry)
    return o, sent


def _na_bwd(qkv, bias, do, carry):
    lw = NA_BWD_HPS * HD

    def body(q_ref, k_ref, v_ref, b_ref, do_ref, dqkv_ref, z_ref, dk_acc, dv_acc):
        blk = pl.program_id(1)

        @pl.when(blk == 0)
        def _():
            dk_acc[...] = jnp.zeros_like(dk_acc)
            dv_acc[...] = jnp.zeros_like(dv_acc)
            z_ref[...] = jnp.zeros_like(z_ref)

        start = _na_start(blk)
        q = q_ref[...]
        do = do_ref[...]
        kw = k_ref[pl.ds(start, NA_WIN), :]
        vw = v_ref[pl.ds(start, NA_WIN), :]
        dqs, dks, dvs, dss = [], [], [], []
        for hh in range(NA_BWD_HPS):
            sl = slice(hh * HD, (hh + 1) * HD)
            qh = q[:, sl] * QSCALE
            s = lax.dot_general(qh, kw[:, sl], _NT, preferred_element_type=F32) + b_ref[hh]
            p = jnp.exp(s - jnp.max(s, axis=-1, keepdims=True))
            p = p / jnp.sum(p, axis=-1, keepdims=True)
            dp = lax.dot_general(do[:, sl], vw[:, sl], _NT, preferred_element_type=F32)
            ds = p * (dp - jnp.sum(p * dp, axis=-1, keepdims=True))
            dsb = ds.astype(BF16)
            dqs.append(jnp.dot(dsb, kw[:, sl], preferred_element_type=F32) * QSCALE)
            dks.append(lax.dot_general(qh, dsb, _TN, preferred_element_type=F32).T)
            dvs.append(lax.dot_general(do[:, sl], p.astype(BF16), _TN, preferred_element_type=F32).T)
            dss.append(ds)
        for cls, (i0, ws) in enumerate(NA_CLASSES):
            @pl.when(_na_cls(blk) == cls)
            def _(i0=i0, ws=ws):
                for hh, ds in enumerate(dss):
                    for qi, kr, dr in _na_pairs(i0, ws):
                        z_ref[hh, dr * GRID_W:(dr + 1) * GRID_W, :] += ds[qi * GRID_W:(qi + 1) * GRID_W, kr * GRID_W:(kr + 1) * GRID_W]
        dqkv_ref[0, pl.ds(pl.multiple_of(blk * NA_QB, NA_QB), NA_QB), :] = jnp.concatenate(dqs, axis=1).astype(dqkv_ref.dtype)
        dk_acc[pl.ds(start, NA_WIN), :] += jnp.concatenate(dks, axis=1)
        dv_acc[pl.ds(start, NA_WIN), :] += jnp.concatenate(dvs, axis=1)

        @pl.when(blk == NA_BLOCKS - 1)
        def _():
            dqkv_ref[1] = dk_acc[...].astype(dqkv_ref.dtype)
            dqkv_ref[2] = dv_acc[...].astype(dqkv_ref.dtype)

    (dqkv, z), sent = _carrier_call(
        "na_bwd", body, (NH // NA_BWD_HPS, NA_BLOCKS),
        _na_in_specs(NA_BWD_HPS) + [pl.BlockSpec((NA_QB, lw), lambda hp, b: (b, hp))],
        [pl.BlockSpec((3, SEQ, lw), lambda hp, b: (0, 0, hp)), pl.BlockSpec((NA_BWD_HPS, 15 * GRID_W, GRID_W), lambda hp, b: (hp, 0, 0))],
        [jax.ShapeDtypeStruct((3, SEQ, DM), BF16), jax.ShapeDtypeStruct((NH, 15 * GRID_W, GRID_W), F32)],
        [pltpu.VMEM((SEQ, lw), F32), pltpu.VMEM((SEQ, lw), F32)], (qkv, qkv, qkv, bias, do), carry)
    return dqkv, z, sent


def _rpb_grad(z):
    z2 = z.reshape(NH * 15, GRID_W * GRID_W)

    def body(z_ref, e_ref, o_ref):
        o_ref[...] = jnp.dot(z_ref[...], e_ref[...], preferred_element_type=F32, precision=lax.Precision.HIGHEST)

    out = pl.pallas_call(body, out_shape=jax.ShapeDtypeStruct((NH * 15, 128), F32), name="rpb_grad",
                         compiler_params=pltpu.CompilerParams(vmem_limit_bytes=VMEM_LIMIT))(z2, _diag_onehot())
    return out[:, :31].reshape(NH, 15, 31)


DIL_BLOCKS = SEQ // DIL_QB
DIL_HPS = 8
DIL_LW = DIL_HPS * HD
DIL_NLW = DM // DIL_LW


COLS = 128


def _col_spec():
    return pl.BlockSpec((SEQ, COLS), lambda j: (0, j))


def _grp_spec():
    return pl.BlockSpec((3, SEQ, COLS), lambda j: (0, 0, j))


def _store_group_order(dst_ref, src_ref):
    for g, d in enumerate(DIL):
        n = SEQ // d
        for r in range(d):
            dst_ref[g, r * n:(r + 1) * n, :] = src_ref[pl.ds(r, n, stride=d), :].astype(dst_ref.dtype)


def _store_token_order(dst_ref, src_ref, g):
    d = DIL[g]
    n = SEQ // d
    for r in range(d):
        dst_ref[pl.ds(r, n, stride=d), :] = src_ref[g, r * n:(r + 1) * n, :]


def _to_groups(name, a):
    def body(a_ref, o_ref, t_ref):
        _store_group_order(o_ref, a_ref)
        for g in range(3):
            t_ref[g] = o_ref[g].astype(F32).T.astype(t_ref.dtype)

    return pl.pallas_call(
        body, grid=(DM // COLS,), in_specs=[_col_spec()], out_specs=[_grp_spec(), pl.BlockSpec((3, COLS, SEQ), lambda j: (0, j, 0))],
        out_shape=[jax.ShapeDtypeStruct((3, SEQ, DM), BF16), jax.ShapeDtypeStruct((3, DM, SEQ), BF16)],
        compiler_params=_params(("parallel",)), name=name)(a)


def _from_groups_sum(name, a):
    def body(a_ref, o_ref, t1, t2):
        _store_token_order(t1, a_ref, 1)
        _store_token_order(t2, a_ref, 2)
        o_ref[...] = (a_ref[0] + t1[...]) + t2[...]

    return pl.pallas_call(body, grid=(DM // COLS,), in_specs=[_grp_spec()], out_specs=_col_spec(),
                          out_shape=jax.ShapeDtypeStruct((SEQ, DM), F32), scratch_shapes=[pltpu.VMEM((SEQ, COLS), F32)] * 2,
                          compiler_params=_params(("parallel",)), name=name)(a)


def _dil_start(b):
    return pl.multiple_of(jnp.clip(b * DIL_QB - DIL_RADIUS, 0, SEQ - DIL_WIN), DIL_RADIUS)


def _dil_neg_dist(g, ii, jj):
    shift = 11 - 2 * g
    dist = jnp.abs(ii - jj)
    valid = (dist <= DIL_RADIUS) & (jnp.right_shift(ii, shift) == jnp.right_shift(jj, shift))
    return jnp.where(valid, -dist.astype(F32), NEG)


def _dil_in_specs():
    return [pl.BlockSpec(memory_space=pltpu.SMEM),
            pl.BlockSpec((None, DIL_QB, DIL_LW), lambda g, hp, b: (g, b, hp)),
            pl.BlockSpec((None, SEQ, DIL_LW), lambda g, hp, b: (g, 0, DIL_NLW + hp)),
            pl.BlockSpec((None, SEQ, DIL_LW), lambda g, hp, b: (g, 0, 2 * DIL_NLW + hp))]


def _dil_fwd(qkv, slopes, carry):
    def body(sl_ref, q_ref, k_ref, v_ref, o_ref, lse_ref):
        g, hp, b = pl.program_id(0), pl.program_id(1), pl.program_id(2)
        start = _dil_start(b)
        neg_dist = _dil_neg_dist(g, b * DIL_QB + lax.broadcasted_iota(jnp.int32, (DIL_QB, DIL_WIN), 0),
                                 start + lax.broadcasted_iota(jnp.int32, (DIL_QB, DIL_WIN), 1))
        dil = jnp.left_shift(1, 2 * g).astype(F32)
        q = q_ref[...]
        kw = k_ref[pl.ds(start, DIL_WIN), :]
        vw = v_ref[pl.ds(start, DIL_WIN), :]
        outs, lses = [], []
        for hh in range(DIL_HPS):
            sl = slice(hh * HD, (hh + 1) * HD)
            s = lax.dot_general(q[:, sl] * QSCALE, kw[:, sl], _NT, preferred_element_type=F32)
            s = s + (sl_ref[hp * DIL_HPS + hh] * dil) * neg_dist
            m = jnp.max(s, axis=-1, keepdims=True)
            p = jnp.exp(s - m)
            l = jnp.sum(p, axis=-1, keepdims=True)
            outs.append(jnp.dot(p.astype(BF16), vw[:, sl], preferred_element_type=F32) / l)
            lses.append(jnp.broadcast_to(m + jnp.log(l), (DIL_QB, HD)))
        o_ref[...] = jnp.concatenate(outs, axis=1)
        lse_ref[...] = jnp.concatenate(lses, axis=1)

    ospec = pl.BlockSpec((None, DIL_QB, DIL_LW), lambda g, hp, b: (g, b, hp))
    sh = jax.ShapeDtypeStruct((3, SEQ, DM), F32)
    (o, lse), sent = _carrier_call("dil_fwd", body, (3, DIL_NLW, DIL_BLOCKS), _dil_in_specs(), [ospec, ospec], [sh, sh], [],
                                   (slopes, qkv, qkv, qkv), carry)
    return o, lse, sent


def _dil_merge(o_all, lse_all):
    def body(o_ref, l_ref, out_ref, lse_ref, o1, o2, l1, l2):
        for g, (ot, lt) in ((1, (o1, l1)), (2, (o2, l2))):
            _store_token_order(ot, o_ref, g)
            _store_token_order(lt, l_ref, g)
        la, lb, lc = l_ref[0], l1[...], l2[...]
        m = jnp.maximum(jnp.maximum(la, lb), lc)
        wa, wb, wc = jnp.exp(la - m), jnp.exp(lb - m), jnp.exp(lc - m)
        sw = (wa + wb) + wc
        out_ref[...] = (((wa * o_ref[0] + wb * o1[...]) + wc * o2[...]) / sw).astype(out_ref.dtype)
        lse_ref[...] = m + jnp.log(sw)

    return pl.pallas_call(
        body, grid=(DM // COLS,), in_specs=[_grp_spec(), _grp_spec()], out_specs=[_col_spec(), _col_spec()],
        out_shape=[jax.ShapeDtypeStruct((SEQ, DM), BF16), jax.ShapeDtypeStruct((SEQ, DM), F32)],
        scratch_shapes=[pltpu.VMEM((SEQ, COLS), F32)] * 4, compiler_params=_params(("parallel",)), name="dil_merge")(o_all, lse_all)


def _dil_bwd_prep(do, o, lse):
    heads = COLS // HD

    def body(do_ref, o_ref, lse_ref, dog_ref, ddr_ref, lser_ref, dd, grp):
        prod = do_ref[...] * o_ref[...].astype(F32)
        dd[...] = jnp.concatenate(
            [jnp.broadcast_to(jnp.sum(prod[:, h * HD:(h + 1) * HD], axis=-1, keepdims=True), (SEQ, HD)) for h in range(heads)], axis=1)
        _store_group_order(dog_ref, do_ref)
        for src, dst in ((dd, ddr_ref), (lse_ref, lser_ref)):
            _store_group_order(grp, src)
            for g in range(3):
                t = grp[g].T
                for h in range(heads):
                    dst[g, h] = t[h * HD:h * HD + 8, :]

    rows = jax.ShapeDtypeStruct((3, NH, 8, SEQ), F32)
    rspec = pl.BlockSpec((3, heads, 8, SEQ), lambda j: (0, j, 0, 0))
    return pl.pallas_call(
        body, grid=(DM // COLS,), in_specs=[_col_spec()] * 3, out_specs=[_grp_spec(), rspec, rspec],
        out_shape=[jax.ShapeDtypeStruct((3, SEQ, DM), BF16), rows, rows],
        scratch_shapes=[pltpu.VMEM((SEQ, COLS), F32), pltpu.VMEM((3, SEQ, COLS), F32)],
        compiler_params=_params(("parallel",)), name="dil_bwd_prep")(do, o, lse)


def _dil_bwd(qkv, do, dd, lse, slopes, carry):
    def body(sl_ref, q_ref, k_ref, v_ref, do_ref, dd_ref, lse_ref, dqkv_ref, dk_acc, dv_acc):
        g, hp, b = pl.program_id(0), pl.program_id(1), pl.program_id(2)

        @pl.when(b == 0)
        def _():
            dk_acc[...] = jnp.zeros_like(dk_acc)
            dv_acc[...] = jnp.zeros_like(dv_acc)

        start = _dil_start(b)
        neg_dist = _dil_neg_dist(g, b * DIL_QB + lax.broadcasted_iota(jnp.int32, (DIL_WIN, DIL_QB), 1),
                                 start + lax.broadcasted_iota(jnp.int32, (DIL_WIN, DIL_QB), 0))
        dil = jnp.left_shift(1, 2 * g).astype(F32)
        q = q_ref[...]
        do = do_ref[...]
        kw = k_ref[pl.ds(start, DIL_WIN), :]
        vw = v_ref[pl.ds(start, DIL_WIN), :]
        dqs, dks, dvs = [], [], []
        for hh in range(DIL_HPS):
            sl = slice(hh * HD, (hh + 1) * HD)
            qh = q[:, sl] * QSCALE
            st = lax.dot_general(kw[:, sl], qh, _NT, preferred_element_type=F32)
            st = st + (sl_ref[hp * DIL_HPS + hh] * dil) * neg_dist
            pt = jnp.exp(st - lse_ref[hh, 0:1, :])
            dpt = lax.dot_general(vw[:, sl], do[:, sl], _NT, preferred_element_type=F32)
            dst = (pt * (dpt - dd_ref[hh, 0:1, :])).astype(BF16)
            dqs.append(lax.dot_general(kw[:, sl], dst, _TN, preferred_element_type=F32).T * QSCALE)
            dks.append(jnp.dot(dst, qh, preferred_element_type=F32))
            dvs.append(jnp.dot(pt.astype(BF16), do[:, sl], preferred_element_type=F32))
        dqkv_ref[0, pl.ds(pl.multiple_of(b * DIL_QB, DIL_QB), DIL_QB), :] = jnp.concatenate(dqs, axis=1).astype(dqkv_ref.dtype)
        dk_acc[pl.ds(start, DIL_WIN), :] += jnp.concatenate(dks, axis=1)
        dv_acc[pl.ds(start, DIL_WIN), :] += jnp.concatenate(dvs, axis=1)

        @pl.when(b == DIL_BLOCKS - 1)
        def _():
            dqkv_ref[1] = dk_acc[...].astype(dqkv_ref.dtype)
            dqkv_ref[2] = dv_acc[...].astype(dqkv_ref.dtype)

    qspec = pl.BlockSpec((None, DIL_QB, DIL_LW), lambda g, hp, b: (g, b, hp))
    rspec = pl.BlockSpec((None, DIL_HPS, 8, DIL_QB), lambda g, hp, b: (g, hp, 0, b))
    (dqkv,), sent = _carrier_call(
        "dil_bwd", body, (3, DIL_NLW, DIL_BLOCKS), _dil_in_specs() + [qspec, rspec, rspec],
        [pl.BlockSpec((None, 3, SEQ, DIL_LW), lambda g, hp, b: (g, 0, 0, hp))], [jax.ShapeDtypeStruct((3, 3, SEQ, DM), BF16)],
        [pltpu.VMEM((SEQ, DIL_LW), F32), pltpu.VMEM((SEQ, DIL_LW), F32)], (slopes, qkv, qkv, qkv, do, dd, lse), carry)
    return dqkv, sent


def _ffn_fwd(name, x, g_pre, g_post, wgt4, wut4, wd4, carry):
    tm = 512

    def body(x_ref, gpre_ref, gpost_ref, wg_ref, wu_ref, wd_ref, xn_ref, h_ref, gate_ref, up_ref, u_ref, acc):
        s = pl.program_id(1)

        @pl.when(s == 0)
        def _():
            x = x_ref[...]
            r = lax.rsqrt(jnp.mean(x * x, axis=-1, keepdims=True) + RMS_EPS)
            h_ref[...] = (x * r * gpre_ref[...]).astype(h_ref.dtype)

        h = h_ref[...]
        gate = lax.dot_general(h, wg_ref[...], _NT, preferred_element_type=F32).astype(BF16)
        up = lax.dot_general(h, wu_ref[...], _NT, preferred_element_type=F32).astype(BF16)
        gate_ref[...] = gate
        up_ref[...] = up
        gf = gate.astype(F32)
        act = (gf * jax.nn.sigmoid(gf) * up.astype(F32)).astype(BF16)
        part = jnp.dot(act, wd_ref[...], preferred_element_type=F32)

        @pl.when(s == 0)
        def _():
            acc[...] = part

        @pl.when(s > 0)
        def _():
            acc[...] += part

        @pl.when(s == NCHIP - 1)
        def _():
            u = acc[...]
            u_ref[...] = u
            r = lax.rsqrt(jnp.mean(u * u, axis=-1, keepdims=True) + RMS_EPS)
            xn_ref[...] = x_ref[...] + u * r * gpost_ref[...]

    rows = pl.BlockSpec((tm, DM), lambda i, s: (i, 0))
    vec = pl.BlockSpec((1, DM), lambda i, s: (0, 0))
    wspec = _ffn_wspec(lambda i, s: (s, 0, 0))
    mid = pl.BlockSpec((None, tm, FSH), lambda i, s: (s, i, 0))
    outs, sent = _carrier_call(
        name, body, (SEQ // tm, NCHIP), [rows, vec, vec, wspec, wspec, wspec], [rows, rows, mid, mid, rows],
        [jax.ShapeDtypeStruct((SEQ, DM), F32), jax.ShapeDtypeStruct((SEQ, DM), BF16), jax.ShapeDtypeStruct((NCHIP, SEQ, FSH), BF16),
         jax.ShapeDtypeStruct((NCHIP, SEQ, FSH), BF16), jax.ShapeDtypeStruct((SEQ, DM), F32)],
        [pltpu.VMEM((tm, DM), F32)], (x, g_pre, g_post, wgt4, wut4, wd4), carry)
    return outs, sent


def _ffn_block(layer, x, g_pre, g_post, ex):
    tag = f"l{layer}_ffn_fwd"
    (x_new, h, gate, up, u), sent = _ffn_fwd(tag, x, g_pre, g_post, ex.weight(("ffn_w_gate", layer)), ex.weight(("ffn_w_up", layer)),
                                             ex.weight(("ffn_w_down", layer)), ex.carry(tag))
    ex.carried(tag, sent)
    return x_new, (x, h, gate, up, u)


def _ffn_bwd(name, dx, x, gate, up, u, g_pre, g_post, wgt4, wut4, wd4, carry):
    tm = 512

    def body(dx_ref, x_ref, gate_ref, up_ref, u_ref, gpre_ref, gpost_ref, wg_ref, wu_ref, wd_ref,
             dxin_ref, du_ref, dgate_ref, dup_ref, act_ref, dgpre_ref, dgpost_ref, dh_acc):
        i, s = pl.program_id(0), pl.program_id(1)

        @pl.when((i == 0) & (s == 0))
        def _():
            dgpre_ref[...] = jnp.zeros_like(dgpre_ref)
            dgpost_ref[...] = jnp.zeros_like(dgpost_ref)

        @pl.when(s == 0)
        def _():
            dy = dx_ref[...]
            uu = u_ref[...]
            r = lax.rsqrt(jnp.mean(uu * uu, axis=-1, keepdims=True) + RMS_EPS)
            yh = uu * r
            t = dy * gpost_ref[...]
            du_ref[...] = (r * (t - yh * jnp.mean(t * yh, axis=-1, keepdims=True))).astype(du_ref.dtype)
            dgpost_ref[...] += jnp.sum(dy * yh, axis=0, keepdims=True)

        dact = lax.dot_general(du_ref[...], wd_ref[...], _NT, preferred_element_type=F32)
        g = gate_ref[...].astype(F32)
        upv = up_ref[...].astype(F32)
        sg = jax.nn.sigmoid(g)
        dgate = (dact * upv * sg * (1.0 + g * (1.0 - sg))).astype(BF16)
        dup = (dact * g * sg).astype(BF16)
        dgate_ref[...] = dgate
        dup_ref[...] = dup
        act_ref[...] = (g * sg * upv).astype(act_ref.dtype)
        part = jnp.dot(dgate, wg_ref[...], preferred_element_type=F32) + jnp.dot(dup, wu_ref[...], preferred_element_type=F32)

        @pl.when(s == 0)
        def _():
            dh_acc[...] = part

        @pl.when(s > 0)
        def _():
            dh_acc[...] += part

        @pl.when(s == NCHIP - 1)
        def _():
            dh = dh_acc[...]
            xx = x_ref[...]
            r = lax.rsqrt(jnp.mean(xx * xx, axis=-1, keepdims=True) + RMS_EPS)
            yh = xx * r
            t = dh * gpre_ref[...]
            dxin_ref[...] = dx_ref[...] + r * (t - yh * jnp.mean(t * yh, axis=-1, keepdims=True))
            dgpre_ref[...] += jnp.sum(dh * yh, axis=0, keepdims=True)

    rows = pl.BlockSpec((tm, DM), lambda i, s: (i, 0))
    vec = pl.BlockSpec((1, DM), lambda i, s: (0, 0))
    wspec = _ffn_wspec(lambda i, s: (s, 0, 0))
    mid = pl.BlockSpec((None, tm, FSH), lambda i, s: (s, i, 0))
    mid_shape = jax.ShapeDtypeStruct((NCHIP, SEQ, FSH), BF16)
    return _carrier_call(
        name, body, (SEQ // tm, NCHIP), [rows, rows, mid, mid, rows, vec, vec, wspec, wspec, wspec], [rows, rows, mid, mid, mid, vec, vec],
        [jax.ShapeDtypeStruct((SEQ, DM), F32), jax.ShapeDtypeStruct((SEQ, DM), BF16), mid_shape, mid_shape, mid_shape,
         jax.ShapeDtypeStruct((1, DM), F32), jax.ShapeDtypeStruct((1, DM), F32)],
        [pltpu.VMEM((tm, DM), F32)], (dx, x, gate, up, u, g_pre, g_post, wgt4, wut4, wd4), carry)


def _ffn_block_bwd(layer, dx, saved, g_pre, g_post, ex):
    tag = f"l{layer}"
    x, h, gate, up, u = saved
    (dx_in, du, dgate, dup, act, dg_pre, dg_post), sent = _ffn_bwd(
        f"{tag}_ffn_bwd", dx, x, gate, up, u, g_pre, g_post, ex.weight(("ffn_w_gate", layer)), ex.weight(("ffn_w_up", layer)),
        ex.weight(("ffn_w_down", layer)), ex.carry(f"{tag}_ffn_bwd"))
    ex.carried(f"{tag}_ffn_bwd", sent)
    d_wd = _ffn_bwd_dw(f"{tag}_dwd", act, du)
    d_wg = _ffn_bwd_dw(f"{tag}_dwg", dgate, h)
    d_wu = _ffn_bwd_dw(f"{tag}_dwu", dup, h)
    ex.grads(f"{tag}_ffn", {("ffn_w_gate", layer): d_wg, ("ffn_w_up", layer): d_wu, ("ffn_w_down", layer): d_wd})
    return dx_in, dg_pre, dg_post


def _alibi_slopes():
    return 2.0 ** (-8.0 * jnp.arange(1, NH + 1, dtype=F32) / NH)


def _local_step(x, target, norms, rpb, ex):
    g_mix_pre, g_mix_post, g_ffn_pre, g_ffn_post = norms
    row = lambda a, i: a[i:i + 1]

    bias, sent = _na_bias_tiles(rpb, ex.carry("na_bias_tiles"))
    ex.carried("na_bias_tiles", sent)
    h0, h0t = _rms_fwd_both("l0_mix_pre", x, row(g_mix_pre, 0))
    qkv0, sent = _qkv_fwd("l0_qkv", h0[None], ex.weight(("na_w_qkv", 0)), ex.carry("l0_qkv"))
    ex.carried("l0_qkv", sent)
    o0, sent = _na_fwd(qkv0[0], bias, ex.carry("na_fwd"))
    ex.carried("na_fwd", sent)
    na_wo = ex.weight(("na_w_o", 0)).reshape(DM, DM)
    x1, u0 = _proj_fwd("l0_proj", o0, na_wo, x, row(g_mix_post, 0))
    x2, ffn0 = _ffn_block(0, x1, row(g_ffn_pre, 0), row(g_ffn_post, 0), ex)

    slopes = _alibi_slopes()
    h2g, h2gt = _to_groups("l1_h_groups", _rms_fwd("l1_mix_pre", x2, row(g_mix_pre, 1), F32))
    dil_wqkv = ex.weight(("dil_w_qkv", 0))
    qkv1, sent = _qkv_fwd("l1_qkv", h2g, dil_wqkv, ex.carry("l1_qkv"))
    ex.carried("l1_qkv", sent)
    og, lg, sent = _dil_fwd(qkv1, slopes, ex.carry("dil_fwd"))
    ex.carried("dil_fwd", sent)
    o1, lse = _dil_merge(og, lg)
    dil_wo = ex.weight(("dil_w_o", 0)).reshape(DM, DM)
    x3, u1 = _proj_fwd("l1_proj", o1, dil_wo, x2, row(g_mix_post, 1))
    x4, ffn1 = _ffn_block(1, x3, row(g_ffn_pre, 1), row(g_ffn_post, 1), ex)

    dx4, loss_row = _loss_grad("loss", x4, target)

    dx3, dg_fpre1, dg_fpost1 = _ffn_block_bwd(1, dx4, ffn1, row(g_ffn_pre, 1), row(g_ffn_post, 1), ex)
    (do1, du1, dg_mpost1), sent = _proj_bwd("l1_proj_bwd", dx3, u1, row(g_mix_post, 1), dil_wo, F32, ex.carry("l1_proj_bwd"))
    ex.carried("l1_proj_bwd", sent)
    d_dil_wo = _proj_bwd_dw("l1_dwo", o1, du1)
    dog, ddg, lseg = _dil_bwd_prep(do1, o1, lse)
    dqkv1, sent = _dil_bwd(qkv1, dog, ddg, lseg, slopes, ex.carry("dil_bwd"))
    ex.carried("dil_bwd", sent)
    d_dil_wqkv, sent = _qkv_bwd_dw("l1_dwqkv", h2gt, dqkv1, dil_wqkv.shape[2], ex.carry("l1_dwqkv"))
    ex.carried("l1_dwqkv", sent)
    ex.grads("l1_mix", {("dil_w_qkv", 0): d_dil_wqkv, ("dil_w_o", 0): d_dil_wo.reshape(NCHIP, DM // NCHIP, DM)})
    dh2g, sent = _qkv_bwd_dh("l1_dh", dqkv1, dil_wqkv, ex.carry("l1_dh"))
    ex.carried("l1_dh", sent)
    dh2 = _from_groups_sum("l1_dh_tokens", dh2g)
    (dx2, dg_mpre1), sent = _norm_bwd("l1_mix_pre_bwd", dh2, x2, row(g_mix_pre, 1), dx3, ex.carry("l1_mix_pre_bwd"))
    ex.carried("l1_mix_pre_bwd", sent)

    dx1, dg_fpre0, dg_fpost0 = _ffn_block_bwd(0, dx2, ffn0, row(g_ffn_pre, 0), row(g_ffn_post, 0), ex)
    (do0, du0, dg_mpost0), sent = _proj_bwd("l0_proj_bwd", dx1, u0, row(g_mix_post, 0), na_wo, BF16, ex.carry("l0_proj_bwd"))
    ex.carried("l0_proj_bwd", sent)
    d_na_wo = _proj_bwd_dw("l0_dwo", o0, du0)
    dqkv0, z, sent = _na_bwd(qkv0[0], bias, do0, ex.carry("na_bwd"))
    ex.carried("na_bwd", sent)
    d_rpb = _rpb_grad(z)
    na_wqkv = ex.weight(("na_w_qkv", 0))
    d_na_wqkv, sent = _qkv_bwd_dw("l0_dwqkv", h0t[None], dqkv0[None], na_wqkv.shape[2], ex.carry("l0_dwqkv"))
    ex.carried("l0_dwqkv", sent)
    ex.grads("l0_mix", {("na_w_qkv", 0): d_na_wqkv, ("na_w_o", 0): d_na_wo.reshape(NCHIP, DM // NCHIP, DM)})
    dh0, sent = _qkv_bwd_dh("l0_dh", dqkv0[None], na_wqkv, ex.carry("l0_dh"))
    ex.carried("l0_dh", sent)
    (dx0, dg_mpre0), sent = _norm_bwd("l0_mix_pre_bwd", dh0[0], x, row(g_mix_pre, 0), dx1, ex.carry("l0_mix_pre_bwd"))
    ex.carried("l0_mix_pre_bwd", sent)

    dnorms = (jnp.concatenate([dg_mpre0, dg_mpre1]), jnp.concatenate([dg_mpost0, dg_mpost1]),
              jnp.concatenate([dg_fpre0, dg_fpre1]), jnp.concatenate([dg_fpost0, dg_fpost1]))
    return loss_row, dx0, dnorms, d_rpb


def _place():
    x, y, c = lax.axis_index("x"), lax.axis_index("y"), lax.axis_index("c")
    chips = ((1 - x, y), (x, 1 - y), (1 - x, 1 - y))
    return x, y, c, chips


def _chip_id(chip):
    return 2 * chip[0] + chip[1]


def _gather_copies(shards):
    n = len(shards)

    def copies(src, out, sems):
        send_sems, recv_sems = sems
        x, y, c, chips = _place()

        def copy(t, k, chip, half, to, from_src=False):
            blk = out[t].at[_chip_id(chip), half]
            return pltpu.make_async_remote_copy(
                src_ref=src[t].at[half] if from_src else blk, dst_ref=blk,
                send_sem=send_sems.at[6 * t + k], recv_sem=recv_sems.at[6 * t + k], device_id=to, device_id_type=MESH)

        return copy, x, y, c, chips

    def issue(src, out, sems):
        copy, x, y, c, chips = copies(src, out, sems)
        for t in range(n):
            for j, chip in enumerate(chips):
                copy(t, j, (x, y), c, (*chip, c), from_src=True).start()

    def drain(src, out, sems):
        copy, x, y, c, chips = copies(src, out, sems)
        passed = []
        for t in range(n):
            for j, chip in enumerate(chips):
                copy(t, j, chip, c, (x, y, c)).wait_recv()
                fwd = copy(t, 3 + j, chip, c, (x, y, 1 - c))
                fwd.start()
                passed.append(fwd)
        for t in range(n):
            for j, chip in enumerate(chips):
                copy(t, 3 + j, chip, 1 - c, (x, y, c)).wait_recv()
        for t in range(n):
            for j, chip in enumerate(chips):
                copy(t, j, (x, y), c, (*chip, c), from_src=True).wait_send()
        for cp in passed:
            cp.wait_send()

    return _Carried(shards, [jax.ShapeDtypeStruct((NCHIP,) + s.shape, s.dtype) for s in shards], (6 * n, 6 * n), issue, drain)


def _pair_exchange_copies(grads):
    n = len(grads)

    def copies(g, theirs, sems):
        send_sems, recv_sems = sems
        x, y, c, _ = _place()
        return [pltpu.make_async_remote_copy(src_ref=g[t].at[:, 1 - c], dst_ref=theirs[t], send_sem=send_sems.at[t],
                                             recv_sem=recv_sems.at[t], device_id=(x, y, 1 - c), device_id_type=MESH) for t in range(n)]

    def issue(g, theirs, sems):
        for cp in copies(g, theirs, sems):
            cp.start()

    def drain(g, theirs, sems):
        for cp in copies(g, theirs, sems):
            cp.wait()

    return _Carried(grads, [jax.ShapeDtypeStruct((NCHIP,) + g.shape[2:], g.dtype) for g in grads], (n, n), issue, drain)


def _chip_exchange_copies(items):
    flat = [(t, i, j) for t, (_, peers) in enumerate(items) for i, j in enumerate(peers)]

    def copies(p, slots, sems):
        send_sems, recv_sems = sems
        x, y, c, chips = _place()
        return [pltpu.make_async_remote_copy(src_ref=p[t].at[_chip_id(chips[j])], dst_ref=slots[t].at[i], send_sem=send_sems.at[k],
                                             recv_sem=recv_sems.at[k], device_id=(*chips[j], c), device_id_type=MESH)
                for k, (t, i, j) in enumerate(flat)]

    def issue(p, slots, sems):
        for cp in copies(p, slots, sems):
            cp.start()

    def drain(p, slots, sems):
        for cp in copies(p, slots, sems):
            cp.wait()

    return _Carried([p for p, _ in items], [jax.ShapeDtypeStruct((len(peers),) + p.shape[1:], p.dtype) for p, peers in items],
                    (len(flat), len(flat)), issue, drain)


def _pair_share_copies(halves):
    n = len(halves)

    def copies(h, other, sems):
        send_sems, recv_sems = sems
        x, y, c, _ = _place()
        return [pltpu.make_async_remote_copy(src_ref=h[t], dst_ref=other[t], send_sem=send_sems.at[t], recv_sem=recv_sems.at[t],
                                             device_id=(x, y, 1 - c), device_id_type=MESH) for t in range(n)]

    def issue(h, other, sems):
        for cp in copies(h, other, sems):
            cp.start()

    def drain(h, other, sems):
        for cp in copies(h, other, sems):
            cp.wait()

    return _Carried(halves, [jax.ShapeDtypeStruct(h.shape, h.dtype) for h in halves], (n, n), issue, drain)


SMALL_ROWS = 128


def _allreduce_small(v):
    def body(v_ref, o_ref, buf, send_sems, recv_sems):
        x, y, c, _ = _place()
        me = 4 * x + 2 * y + c
        flip = lambda a, f: 1 - a if f else a
        buf[me] = v_ref[...]
        peers = [(flip(x, d >> 2 & 1), flip(y, d >> 1 & 1), flip(c, d & 1)) for d in range(1, 8)]
        sends = [pltpu.make_async_remote_copy(src_ref=v_ref, dst_ref=buf.at[me], send_sem=send_sems.at[i], recv_sem=recv_sems.at[i],
                                              device_id=peer, device_id_type=MESH) for i, peer in enumerate(peers)]
        for cp in sends:
            cp.start()
        for i, (px, py, pc) in enumerate(peers):
            pltpu.make_async_remote_copy(src_ref=v_ref, dst_ref=buf.at[4 * px + 2 * py + pc], send_sem=send_sems.at[i], recv_sem=recv_sems.at[i],
                                         device_id=(px, py, pc), device_id_type=MESH).wait_recv()
        for cp in sends:
            cp.wait_send()
        acc = buf[0]
        for k in range(1, 8):
            acc = acc + buf[k]
        o_ref[...] = acc

    vm = pl.BlockSpec(memory_space=pltpu.VMEM)
    return pl.pallas_call(
        body, in_specs=[vm], out_specs=vm, out_shape=jax.ShapeDtypeStruct((SMALL_ROWS, 128), F32),
        scratch_shapes=[pltpu.VMEM((8, SMALL_ROWS, 128), F32), pltpu.SemaphoreType.DMA((7,)), pltpu.SemaphoreType.DMA((7,))],
        compiler_params=pltpu.CompilerParams(has_side_effects=True), name="allreduce_small")(v)


def _row_block(rows, cols, budget=3 << 19):
    best = 8
    for bm in range(8, rows + 1, 8):
        if rows % bm == 0 and bm * cols * 4 <= budget:
            best = bm
    return best


def _pair_sum(name, place, gs, theirs):
    n = len(gs)
    _, m, c = theirs[0].shape
    bm = _row_block(m, c)

    def body(place_ref, *refs):
        for a_ref, b_ref, o_ref in zip(refs[:n], refs[n:2 * n], refs[2 * n:]):
            o_ref[...] = (a_ref[...].astype(F32) + b_ref[...].astype(F32)).astype(o_ref.dtype)

    spec = pl.BlockSpec((None, bm, c), lambda k, i, pr: (k, i, 0))
    return pl.pallas_call(
        body, out_shape=[jax.ShapeDtypeStruct(theirs[0].shape, BF16)] * n,
        grid_spec=pltpu.PrefetchScalarGridSpec(
            num_scalar_prefetch=1, grid=(NCHIP, m // bm),
            in_specs=[pl.BlockSpec((None, None, bm, c), lambda k, i, pr: (k, pr[0], i, 0))] * n + [spec] * n, out_specs=[spec] * n),
        compiler_params=_params(("parallel", "parallel")), name=name)(place, *gs, *theirs)


def _chip_sum(name, place, parts, slots):
    n, ns = len(parts), len(slots[0])
    _, m, c = parts[0].shape
    bm = _row_block(m, c)

    def body(place_ref, *refs):
        for t in range(n):
            acc = refs[t][...].astype(F32)
            for s_ref in refs[n + t * ns:n + (t + 1) * ns]:
                for i in range(s_ref.shape[0]):
                    acc = acc + s_ref[i].astype(F32)
            refs[n + n * ns + t][...] = acc

    half = pl.BlockSpec((bm, c), lambda i, pr: (i, 0))
    return pl.pallas_call(
        body, out_shape=[jax.ShapeDtypeStruct((m, c), F32)] * n,
        grid_spec=pltpu.PrefetchScalarGridSpec(
            num_scalar_prefetch=1, grid=(m // bm,),
            in_specs=[pl.BlockSpec((None, bm, c), lambda i, pr: (pr[1], i, 0))] * n
            + [pl.BlockSpec((s.shape[0], bm, c), lambda i, pr: (0, i, 0)) for group in slots for s in group],
            out_specs=[half] * n),
        compiler_params=_params(("parallel",)), name=name)(place, *parts, *[s for group in slots for s in group])


def _adamw(name, place, w, g_mine, g_other, m, v, layer=0, into=None):
    lead, rows, cols = w.shape
    bm = _row_block(rows // 2, cols, budget=768 * 1024)
    per_half = rows // 2 // bm
    c1 = 1.0 - ADAM_B1 ** ADAM_STEP
    c2 = 1.0 - ADAM_B2 ** ADAM_STEP

    def body(place_ref, w_ref, ga_ref, gb_ref, m_ref, v_ref, *rest):
        go_ref, d_ref, mo_ref, vo_ref = rest[-4:]
        g = jnp.where(pl.program_id(0) // per_half == place_ref[0], ga_ref[...], gb_ref[...])
        mn = ADAM_B1 * m_ref[...] + (1.0 - ADAM_B1) * g
        vn = ADAM_B2 * v_ref[...] + (1.0 - ADAM_B2) * (g * g)
        go_ref[...] = g
        mo_ref[...] = mn
        vo_ref[...] = vn
        d_ref[...] = -ADAM_LR * ((mn / c1) / (jnp.sqrt(vn / c2) + ADAM_EPS) + ADAM_WD * w_ref[...])

    spec = pl.BlockSpec((None, bm, cols), lambda i, pr: (layer, i, 0))

    def half_spec(mine):
        def index(i, pr):
            first = (pr[0] == 0) == mine
            park = jnp.where(first, per_half - 1, 0)
            return jnp.where((i < per_half) == first, i % per_half, park), 0
        return pl.BlockSpec((bm, cols), index)
    sh = jax.ShapeDtypeStruct((lead, rows, cols), F32)
    prev = [] if into is None else list(into)
    return pl.pallas_call(
        body, out_shape=[sh] * 4, input_output_aliases={6 + k: k for k in range(len(prev))},
        grid_spec=pltpu.PrefetchScalarGridSpec(
            num_scalar_prefetch=1, grid=(rows // bm,),
            in_specs=[spec, half_spec(True), half_spec(False), spec, spec] + [pl.BlockSpec(memory_space=pl.ANY)] * len(prev),
            out_specs=[spec] * 4),
        compiler_params=_params(("parallel",)), name=name)(place, w, g_mine, g_other, m, v, *prev)


def _pack_small(norms, rpb, last=None):
    flat = jnp.concatenate([a.reshape(-1) for a in norms] + [rpb.reshape(-1)])
    flat = jnp.pad(flat, (0, SMALL_ROWS * 128 - flat.shape[0]))
    if last is not None:
        flat = jnp.concatenate([flat[:-1], last.reshape(1)])
    return flat.reshape(SMALL_ROWS, 128)


def _unpack_small(p):
    flat = p.reshape(-1)
    norms = [flat[i * 2 * DM:(i + 1) * 2 * DM].reshape(2, DM) for i in range(4)]
    rpb = flat[8 * DM:8 * DM + NH * 15 * 31].reshape(1, NH, 15, 31)
    return norms, rpb


FFN_NAMES = ("ffn_w_gate", "ffn_w_up", "ffn_w_down")
L0_FFN = tuple((n, 0) for n in FFN_NAMES)
L1_FFN = tuple((n, 1) for n in FFN_NAMES)
NA_KEYS = (("na_w_qkv", 0), ("na_w_o", 0))
DIL_KEYS = (("dil_w_qkv", 0), ("dil_w_o", 0))
ALL_PEERS, NEIGHBOURS, DIAGONAL = (0, 1, 2), (0, 1), (2,)


class _Exchange:
    GATHERS = {"na_bias_tiles": NA_KEYS, "l0_qkv": L0_FFN[:1], "na_fwd": L0_FFN[1:], "l0_ffn_fwd": DIL_KEYS[:1], "dil_fwd": L1_FFN + DIL_KEYS[1:]}
    PAIRS = {"l1_proj_bwd": L1_FFN, "l1_dh": DIL_KEYS, "l0_proj_bwd": L0_FFN}
    EXCHANGES = {"dil_bwd": [(k, ALL_PEERS) for k in L1_FFN],
                 "l0_ffn_bwd": [(DIL_KEYS[0], NEIGHBOURS), (DIL_KEYS[1], ALL_PEERS)],
                 "na_bwd": [(k, ALL_PEERS) for k in L0_FFN] + [(DIL_KEYS[0], DIAGONAL)],
                 "l0_dh": [(k, NEIGHBOURS) for k in NA_KEYS],
                 "l0_mix_pre_bwd": [(k, DIAGONAL) for k in NA_KEYS]}
    SHARES = {"l1_dwqkv": L1_FFN, "l0_dwqkv": L0_FFN + DIL_KEYS}

    def __init__(self, shards):
        self.chip = 2 * lax.axis_index("x") + lax.axis_index("y")
        self.place = jnp.stack([lax.axis_index("c"), self.chip]).astype(jnp.int32)
        self.own = {k: s.reshape(2, s.shape[0] // 2, s.shape[1]).astype(BF16) for k, s in shards.items()}
        self.gathered, self.mine, self.parts, self.slots, self.full, self.other = {}, {}, {}, {}, {}, {}

    def _take(self, keys, landed):
        for k, gw in zip(keys, landed):
            self.gathered[k] = lax.dynamic_update_slice(gw, self.own[k][None], (self.chip, 0, 0, 0))

    def _sum(self, items, landed):
        runs = []
        for (k, peers), s in zip(items, landed):
            got = self.slots.setdefault(k, {})
            got[peers] = s
            if sum(len(p) for p in got) == len(ALL_PEERS):
                like = (self.parts[k].shape, tuple(sorted(got)))
                if runs and runs[-1][0] == like:
                    runs[-1][1].append(k)
                else:
                    runs.append((like, [k]))
        for (_, split), ks in runs:
            sums = _chip_sum(f"chip_sum_{ks[0][0]}_{ks[0][1]}", self.place, [self.parts[k] for k in ks],
                             [[self.slots[k][p] for p in split] for k in ks])
            self.full.update(zip(ks, sums))

    def weight(self, key):
        g = self.gathered[key]
        return g.reshape(NCHIP, 2 * g.shape[2], g.shape[3])

    def _pair_sums(self, keys, theirs):
        runs = []
        for k, t in zip(keys, theirs):
            if runs and runs[-1][0][1].shape == t.shape:
                runs[-1].append((k, t))
            else:
                runs.append([(k, t)])
        for run in runs:
            ks = [k for k, _ in run]
            sums = _pair_sum(f"pair_sum_{ks[0][0]}_{ks[0][1]}", self.place, [self.mine[k] for k in ks], [t for _, t in run])
            self.parts.update(zip(ks, sums))

    def carry(self, tag):
        if tag in self.GATHERS:
            return _gather_copies([self.own[k] for k in self.GATHERS[tag]])
        if tag in self.PAIRS:
            return _pair_exchange_copies([self.mine[k] for k in self.PAIRS[tag]])
        if tag in self.EXCHANGES:
            return _chip_exchange_copies([(self.parts[k], peers) for k, peers in self.EXCHANGES[tag]])
        if tag in self.SHARES:
            return _pair_share_copies([self.full[k] for k in self.SHARES[tag]])
        return None

    def carried(self, tag, landed):
        if tag in self.GATHERS:
            self._take(self.GATHERS[tag], landed)
        elif tag in self.PAIRS:
            self._pair_sums(self.PAIRS[tag], landed)
        elif tag in self.EXCHANGES:
            self._sum(self.EXCHANGES[tag], landed)
        elif tag in self.SHARES:
            self.other.update(zip(self.SHARES[tag], landed))

    def grads(self, tag, dw):
        for k, g in dw.items():
            self.mine[k] = g.reshape(NCHIP, 2, -1, g.shape[-1])
        if tag == "l0_mix":
            keys = tuple(dw)
            self._pair_sums(keys, _run_carried("grad_pair_exchange_last", _pair_exchange_copies([self.mine[k] for k in keys])))

    def finish(self):
        rest = tuple(k for k in self.full if k not in self.other)
        self.other.update(zip(rest, _run_carried("grad_pair_share_last", _pair_share_copies([self.full[k] for k in rest]))))
        return {k: (self.full[k], self.other[k]) for k in self.full}


def kernel(x, norm_mix_pre, norm_mix_post, norm_ffn_pre, norm_ffn_post, na_w_qkv, na_w_o, na_rpb, dil_w_qkv, dil_w_o, ffn_w_gate, ffn_w_up, ffn_w_down, loss_target, m_norm_mix_pre, m_norm_mix_post, m_norm_ffn_pre, m_norm_ffn_post, m_na_w_qkv, m_na_w_o, m_na_rpb, m_dil_w_qkv, m_dil_w_o, m_ffn_w_gate, m_ffn_w_up, m_ffn_w_down, v_norm_mix_pre, v_norm_mix_post, v_norm_ffn_pre, v_norm_ffn_post, v_na_w_qkv, v_na_w_o, v_na_rpb, v_dil_w_qkv, v_dil_w_o, v_ffn_w_gate, v_ffn_w_up, v_ffn_w_down):
    tr = lambda a: jnp.swapaxes(a, 1, 2)
    weights = {"na_w_qkv": na_w_qkv, "na_w_o": na_w_o, "dil_w_qkv": dil_w_qkv, "dil_w_o": dil_w_o,
               "ffn_w_gate": tr(ffn_w_gate), "ffn_w_up": tr(ffn_w_up), "ffn_w_down": ffn_w_down}
    m_in = {"na_w_qkv": m_na_w_qkv, "na_w_o": m_na_w_o, "dil_w_qkv": m_dil_w_qkv, "dil_w_o": m_dil_w_o,
            "ffn_w_gate": tr(m_ffn_w_gate), "ffn_w_up": tr(m_ffn_w_up), "ffn_w_down": m_ffn_w_down}
    v_in = {"na_w_qkv": v_na_w_qkv, "na_w_o": v_na_w_o, "dil_w_qkv": v_dil_w_qkv, "dil_w_o": v_dil_w_o,
            "ffn_w_gate": tr(v_ffn_w_gate), "ffn_w_up": tr(v_ffn_w_up), "ffn_w_down": v_ffn_w_down}

    ex = _Exchange({(n, l): weights[n][l] for n in weights for l in range(weights[n].shape[0])})
    norms = (norm_mix_pre, norm_mix_post, norm_ffn_pre, norm_ffn_post)
    loss_row, dx, dnorms, d_rpb = _local_step(x[0], loss_target[0], norms, na_rpb[0], ex)
    full = ex.finish()
    small = _allreduce_small(_pack_small(dnorms, d_rpb, last=loss_row[0, 0]))
    loss = small[SMALL_ROWS - 1, 127]

    out_g, out_d, out_m, out_v = {}, {}, {}, {}
    for n in weights:
        res = None
        for l in range(weights[n].shape[0]):
            res = _adamw(f"adamw_{n}_{l}", ex.place, weights[n], *full[(n, l)], m_in[n], v_in[n], l, res)
        if n in ("ffn_w_gate", "ffn_w_up"):
            res = [tr(r) for r in res]
        out_g[n], out_d[n], out_m[n], out_v[n] = res
    sm_names = ("norm_mix_pre", "norm_mix_post", "norm_ffn_pre", "norm_ffn_post", "na_rpb")
    sm = _adamw("adamw_small", jnp.zeros((2,), jnp.int32), _pack_small(norms, na_rpb)[None], small[:SMALL_ROWS // 2], small[SMALL_ROWS // 2:],
                _pack_small((m_norm_mix_pre, m_norm_mix_post, m_norm_ffn_pre, m_norm_ffn_post), m_na_rpb)[None],
                _pack_small((v_norm_mix_pre, v_norm_mix_post, v_norm_ffn_pre, v_norm_ffn_post), v_na_rpb)[None])
    for res, dst in zip(sm, (out_g, out_d, out_m, out_v)):
        ns, rp = _unpack_small(res)
        for n, a in zip(sm_names, ns + [rp]):
            dst[n] = a

    order = ("norm_mix_pre", "norm_mix_post", "norm_ffn_pre", "norm_ffn_post", "na_w_qkv", "na_w_o", "na_rpb", "dil_w_qkv", "dil_w_o",
             "ffn_w_gate", "ffn_w_up", "ffn_w_down")
    return (loss, dx[None], *[out_g[n] for n in order], *[out_d[n] for n in order], *[out_m[n] for n in order], *[out_v[n] for n in order])
```

```python
import functools

import numpy as np
import jax
import jax.numpy as jnp
from jax import lax
from jax.experimental import pallas as pl
from jax.experimental.pallas import tpu as pltpu

F32 = jnp.float32
BF16 = jnp.bfloat16

SEQ = 2048
DM = 1024
NH = 16
HD = 64
DFF = 2816
NCHIP = 4
FSH = DFF // NCHIP
GRID_W = 64
NA_QROWS = 4
NA_QB = NA_QROWS * GRID_W
NA_WROWS = 12
NA_WIN = NA_WROWS * GRID_W
DIL = (1, 4, 16)
DIL_QB = 256
DIL_WIN = DIL_QB + 128
DIL_RADIUS = 64
RMS_EPS = 1e-6
NEG = -1e30
QSCALE = HD ** -0.5
CH = 256
MESH = pl.DeviceIdType.MESH

ADAM_LR, ADAM_B1, ADAM_B2, ADAM_EPS, ADAM_WD, ADAM_STEP = 0.001, 0.9, 0.999, 1e-08, 0.01, 10

VMEM_LIMIT = 56 * 1024 * 1024

_NN = (((1,), (0,)), ((), ()))
_NT = (((1,), (1,)), ((), ()))
_TN = (((0,), (0,)), ((), ()))


def _params(sem):
    return pltpu.CompilerParams(dimension_semantics=sem, vmem_limit_bytes=VMEM_LIMIT)


def _matmul(name, pairs, grid, out_shape, out_spec, acc_shape, carrying=False, carry=None):
    nk = grid[-1]
    npair = len(pairs)
    n_in = 2 * npair

    def body(*refs):
        ins, o_ref = refs[:2 * npair], refs[n_in]
        part = None
        for p in range(npair):
            d = lax.dot_general(ins[2 * p][...].astype(BF16), ins[2 * p + 1][...].astype(BF16), pairs[p][4],
                                preferred_element_type=F32)
            part = d if part is None else part + d
        if nk == 1:
            o_ref[...] = part.astype(o_ref.dtype)
        else:
            acc_ref = refs[n_in + 1]
            kk = pl.program_id(len(grid) - 1)

            @pl.when(kk == 0)
            def _():
                acc_ref[...] = part

            @pl.when(kk > 0)
            def _():
                acc_ref[...] += part

            @pl.when(kk == nk - 1)
            def _():
                o_ref[...] = acc_ref[...].astype(o_ref.dtype)

    ops, specs = [], []
    for a, a_spec, b, b_spec, _ in pairs:
        ops += [a, b]
        specs += [a_spec, b_spec]
    (out,), sent = _carrier_call(name, body, grid, specs, [out_spec], [out_shape], [] if nk == 1 else [pltpu.VMEM(acc_shape, F32)], ops, carry)
    return (out, sent) if carrying else out


def _qkv_fwd(name, h_all, w4, carry):
    g_n = h_all.shape[0]
    per = w4.shape[2] // CH
    return _matmul(
        name, [(h_all, pl.BlockSpec((None, SEQ, DM), lambda g, q, k: (g, 0, 0)),
                w4, pl.BlockSpec((None, DM, CH), lambda g, q, k: ((g * 12 + q) // per, 0, (g * 12 + q) % per)), _NN)],
        (g_n, 12, 1), jax.ShapeDtypeStruct((g_n, SEQ, 3 * DM), BF16),
        pl.BlockSpec((None, SEQ, CH), lambda g, q, k: (g, 0, q)), None, carrying=True, carry=carry)


def _qkv_bwd_dh(name, dqkv, w4, carry):
    g_n = dqkv.shape[0]
    per = w4.shape[2] // CH
    tm = SEQ

    def pair(cb):
        chunk = lambda g, t: g * 12 + t * 4 + cb
        return (dqkv, pl.BlockSpec((None, None, tm, CH), lambda g, i, t: (g, t, i, cb)),
                w4, pl.BlockSpec((None, DM, CH), lambda g, i, t: (chunk(g, t) // per, 0, chunk(g, t) % per)), _NT)

    return _matmul(name, [pair(cb) for cb in range(4)], (g_n, SEQ // tm, 3), jax.ShapeDtypeStruct((g_n, SEQ, DM), F32),
                   pl.BlockSpec((None, tm, DM), lambda g, i, t: (g, i, 0)), (tm, DM), carrying=True, carry=carry)


def _qkv_bwd_dw(name, ht_all, dqkv, shard_cols, carry):
    g_n = dqkv.shape[0]
    per = shard_cols // CH
    return _matmul(
        name, [(ht_all, pl.BlockSpec((None, DM, SEQ), lambda qq, k: (qq // 12, 0, 0)),
                dqkv, pl.BlockSpec((None, None, SEQ, CH), lambda qq, k: (qq // 12, (qq % 12) // 4, 0, qq % 4)), _NN)],
        (g_n * 12, 1), jax.ShapeDtypeStruct((NCHIP, DM, shard_cols), BF16),
        pl.BlockSpec((None, DM, CH), lambda qq, k: (qq // per, 0, qq % per)), None, carrying=True, carry=carry)


def _proj_fwd(name, o, wo, x, g):
    tm = 512

    def body(o_ref, w_ref, x_ref, g_ref, xn_ref, u_ref):
        u = jnp.dot(o_ref[...], w_ref[...], preferred_element_type=F32)
        u_ref[...] = u
        r = lax.rsqrt(jnp.mean(u * u, axis=-1, keepdims=True) + RMS_EPS)
        xn_ref[...] = x_ref[...] + u * r * g_ref[...]

    rows = pl.BlockSpec((tm, DM), lambda i: (i, 0))
    sh = jax.ShapeDtypeStruct((SEQ, DM), F32)
    return pl.pallas_call(
        body, grid=(SEQ // tm,), in_specs=[rows, pl.BlockSpec((DM, DM), lambda i: (0, 0)), rows, pl.BlockSpec((1, DM), lambda i: (0, 0))],
        out_specs=[rows, rows], out_shape=[sh, sh], compiler_params=_params(("parallel",)), name=name)(o, wo, x, g)


def _proj_bwd(name, dy, u, g, wo, dtype, carry):
    tm = 512

    def body(dy_ref, u_ref, g_ref, w_ref, do_ref, du_ref, dg_ref):
        dy = dy_ref[...]
        u = u_ref[...]
        r = lax.rsqrt(jnp.mean(u * u, axis=-1, keepdims=True) + RMS_EPS)
        yh = u * r
        t = dy * g_ref[...]
        du = (r * (t - yh * jnp.mean(t * yh, axis=-1, keepdims=True))).astype(BF16)
        du_ref[...] = du
        do_ref[...] = lax.dot_general(du, w_ref[...], _NT, preferred_element_type=F32).astype(do_ref.dtype)

        @pl.when(pl.program_id(0) == 0)
        def _():
            dg_ref[...] = jnp.zeros_like(dg_ref)

        dg_ref[...] += jnp.sum(dy * yh, axis=0, keepdims=True)

    rows = pl.BlockSpec((tm, DM), lambda i: (i, 0))
    vec = pl.BlockSpec((1, DM), lambda i: (0, 0))
    return _carrier_call(
        name, body, (SEQ // tm,), [rows, rows, vec, pl.BlockSpec((DM, DM), lambda i: (0, 0))], [rows, rows, vec],
        [jax.ShapeDtypeStruct((SEQ, DM), dtype), jax.ShapeDtypeStruct((SEQ, DM), BF16), jax.ShapeDtypeStruct((1, DM), F32)],
        [], (dy, u, g, wo), carry)


def _proj_bwd_dw(name, o, du):
    tn = 512
    return _matmul(
        name, [(o, pl.BlockSpec((SEQ, DM), lambda j, k: (0, 0)), du, pl.BlockSpec((SEQ, tn), lambda j, k: (0, j)), _TN)],
        (DM // tn, 1), jax.ShapeDtypeStruct((DM, DM), BF16), pl.BlockSpec((DM, tn), lambda j, k: (0, j)), None)


def _ffn_wspec(index_map):
    return pl.BlockSpec((None, FSH, DM), index_map)


def _ffn_bwd_dw(name, a4, b):
    return _matmul(
        name, [(a4, pl.BlockSpec((None, SEQ, FSH), lambda s, k: (s, 0, 0)), b, pl.BlockSpec((SEQ, DM), lambda s, k: (0, 0)), _TN)],
        (NCHIP, 1), jax.ShapeDtypeStruct((NCHIP, FSH, DM), BF16), _ffn_wspec(lambda s, k: (s, 0, 0)), None)


ROWS = 256


def _row_spec():
    return pl.BlockSpec((ROWS, DM), lambda i: (i, 0))


def _vec_spec():
    return pl.BlockSpec((1, DM), lambda i: (0, 0))


def _rms_fwd(name, x, g, dtype=BF16):
    def body(x_ref, g_ref, o_ref):
        x = x_ref[...]
        r = lax.rsqrt(jnp.mean(x * x, axis=-1, keepdims=True) + RMS_EPS)
        o_ref[...] = (x * r * g_ref[...]).astype(o_ref.dtype)

    return pl.pallas_call(body, grid=(SEQ // ROWS,), in_specs=[_row_spec(), _vec_spec()], out_specs=_row_spec(),
                          out_shape=jax.ShapeDtypeStruct((SEQ, DM), dtype), compiler_params=_params(("parallel",)), name=name)(x, g)


def _rms_fwd_both(name, x, g):
    def body(x_ref, g_ref, o_ref, t_ref):
        x = x_ref[...]
        r = lax.rsqrt(jnp.mean(x * x, axis=-1, keepdims=True) + RMS_EPS)
        h = x * r * g_ref[...]
        o_ref[...] = h.astype(o_ref.dtype)
        t_ref[...] = h.T.astype(t_ref.dtype)

    return pl.pallas_call(
        body, grid=(SEQ // ROWS,), in_specs=[_row_spec(), _vec_spec()], out_specs=[_row_spec(), pl.BlockSpec((DM, ROWS), lambda i: (0, i))],
        out_shape=[jax.ShapeDtypeStruct((SEQ, DM), BF16), jax.ShapeDtypeStruct((DM, SEQ), BF16)],
        compiler_params=_params(("parallel",)), name=name)(x, g)


def _norm_bwd(name, dy, u, g, res, carry):
    def body(dy_ref, u_ref, g_ref, res_ref, du_ref, dg_ref):
        dy = dy_ref[...]
        u = u_ref[...]
        r = lax.rsqrt(jnp.mean(u * u, axis=-1, keepdims=True) + RMS_EPS)
        yh = u * r
        t = dy * g_ref[...]
        du_ref[...] = r * (t - yh * jnp.mean(t * yh, axis=-1, keepdims=True)) + res_ref[...]

        @pl.when(pl.program_id(0) == 0)
        def _():
            dg_ref[...] = jnp.zeros_like(dg_ref)

        dg_ref[...] += jnp.sum(dy * yh, axis=0, keepdims=True)

    return _carrier_call(
        name, body, (SEQ // ROWS,), [_row_spec(), _row_spec(), _vec_spec(), _row_spec()], [_row_spec(), _vec_spec()],
        [jax.ShapeDtypeStruct((SEQ, DM), F32), jax.ShapeDtypeStruct((1, DM), F32)], [], (dy, u, g, res), carry)


def _loss_grad(name, y, t):
    def body(y_ref, t_ref, dy_ref, l_ref):
        e = y_ref[...] - t_ref[...]
        dy_ref[...] = e * (1.0 / DM)

        @pl.when(pl.program_id(0) == 0)
        def _():
            l_ref[...] = jnp.zeros_like(l_ref)

        l_ref[...] += jnp.sum(e * e) * (0.5 / DM)

    return pl.pallas_call(
        body, grid=(SEQ // ROWS,), in_specs=[_row_spec(), _row_spec()],
        out_specs=[_row_spec(), pl.BlockSpec((1, 128), lambda i: (0, 0))],
        out_shape=[jax.ShapeDtypeStruct((SEQ, DM), F32), jax.ShapeDtypeStruct((1, 128), F32)],
        compiler_params=_params(("arbitrary",)), name=name)(y, t)


HBM_SPEC = pl.BlockSpec(memory_space=pltpu.HBM)


class _Carried:
    def __init__(self, ins, out_shapes, n_sems, issue, drain):
        self.ins, self.out_shapes, self.n_sems, self.issue, self.drain = list(ins), list(out_shapes), tuple(n_sems), issue, drain


def _carrier_call(name, body, grid, in_specs, out_specs, out_shape, scratch_shapes, operands, carry):
    n_in, n_out, n_scr = len(in_specs), len(out_specs), len(scratch_shapes)
    if carry is None:
        res = pl.pallas_call(body, grid=grid, in_specs=in_specs, out_specs=out_specs, out_shape=out_shape, scratch_shapes=scratch_shapes,
                             compiler_params=_params(("arbitrary",) * len(grid)), name=name)(*operands)
        return list(res), []
    ci, co = len(carry.ins), len(carry.out_shapes)

    def wrapped(*refs):
        ins, cins = refs[:n_in], refs[n_in:n_in + ci]
        outs, couts = refs[n_in + ci:n_in + ci + n_out], refs[n_in + ci + n_out:n_in + ci + n_out + co]
        scr, sems = refs[n_in + ci + n_out + co:n_in + ci + n_out + co + n_scr], refs[n_in + ci + n_out + co + n_scr:]
        first = functools.reduce(jnp.logical_and, [pl.program_id(a) == 0 for a in range(len(grid))])
        last = functools.reduce(jnp.logical_and, [pl.program_id(a) == grid[a] - 1 for a in range(len(grid))])

        @pl.when(first)
        def _():
            carry.issue(cins, couts, sems)

        body(*ins, *outs, *scr)

        @pl.when(last)
        def _():
            carry.drain(cins, couts, sems)

    res = pl.pallas_call(
        wrapped, grid=grid, in_specs=list(in_specs) + [HBM_SPEC] * ci, out_specs=list(out_specs) + [HBM_SPEC] * co,
        out_shape=list(out_shape) + carry.out_shapes,
        scratch_shapes=list(scratch_shapes) + [pltpu.SemaphoreType.DMA((k,)) for k in carry.n_sems],
        compiler_params=pltpu.CompilerParams(dimension_semantics=("arbitrary",) * len(grid), vmem_limit_bytes=VMEM_LIMIT, has_side_effects=True),
        name=name)(*operands, *carry.ins)
    return list(res[:n_out]), list(res[n_out:])


def _run_carried(name, carry):
    def body(*refs):
        ci, co = len(carry.ins), len(carry.out_shapes)
        carry.issue(refs[:ci], refs[ci:ci + co], refs[ci + co:])
        carry.drain(refs[:ci], refs[ci:ci + co], refs[ci + co:])

    return pl.pallas_call(
        body, in_specs=[HBM_SPEC] * len(carry.ins), out_specs=[HBM_SPEC] * len(carry.out_shapes), out_shape=carry.out_shapes,
        scratch_shapes=[pltpu.SemaphoreType.DMA((k,)) for k in carry.n_sems],
        compiler_params=pltpu.CompilerParams(has_side_effects=True), name=name)(*carry.ins)


NA_BLOCKS = SEQ // NA_QB
NA_ROWS_TOTAL = SEQ // GRID_W
NA_CLASSES = ((0, 0), (8, 4), (NA_ROWS_TOTAL - NA_QROWS, NA_ROWS_TOTAL - NA_WROWS))


def _na_pairs(i0, ws):
    out = []
    for qi in range(NA_QROWS):
        i = i0 + qi
        rs = min(max(i - 4, 0), NA_ROWS_TOTAL - 8)
        for kr in range(NA_WROWS):
            r = ws + kr
            if rs <= r < rs + 8:
                out.append((qi, kr, r - i + 7))
    return out


def _diag_onehot():
    qc, kc = np.meshgrid(np.arange(GRID_W), np.arange(GRID_W), indexing="ij")
    e = np.zeros((GRID_W * GRID_W, 128), np.float32)
    j = (kc - qc + 15).reshape(-1)
    ok = (j >= 0) & (j <= 30)
    e[np.arange(GRID_W * GRID_W)[ok], j[ok]] = 1.0
    return jnp.asarray(e)


def _rpb_expand(rpb):
    r2 = jnp.pad(rpb.reshape(NH * 15, 31), ((0, 0), (0, 128 - 31)))

    def body(r_ref, e_ref, o_ref):
        o_ref[...] = lax.dot_general(r_ref[...], e_ref[...], _NT, preferred_element_type=F32, precision=lax.Precision.HIGHEST)

    out = pl.pallas_call(body, out_shape=jax.ShapeDtypeStruct((NH * 15, GRID_W * GRID_W), F32), name="rpb_expand",
                         compiler_params=pltpu.CompilerParams(vmem_limit_bytes=VMEM_LIMIT))(r2, _diag_onehot())
    return out.reshape(NH, 15, GRID_W, GRID_W)


def _na_bias_tiles(rpb, carry):
    def body(b_ref, o_ref):
        qc = lax.broadcasted_iota(jnp.int32, (GRID_W, GRID_W), 0)
        kc = lax.broadcasted_iota(jnp.int32, (GRID_W, GRID_W), 1)
        first = jnp.clip(qc - 8, 0, GRID_W - 16)
        in_window = (kc >= first) & (kc < first + 16)
        neg = jnp.full((GRID_W, GRID_W), NEG, F32)
        for cls, (i0, ws) in enumerate(NA_CLASSES):
            @pl.when(pl.program_id(0) == cls)
            def _(i0=i0, ws=ws):
                pairs = {(qi, kr): dr for qi, kr, dr in _na_pairs(i0, ws)}
                masked = {dr: jnp.where(in_window, b_ref[dr], NEG) for dr in sorted(set(pairs.values()))}
                for qi in range(NA_QROWS):
                    for k2 in range(NA_WROWS // 2):
                        blocks = [masked[pairs[(qi, kr)]] if (qi, kr) in pairs else neg for kr in (2 * k2, 2 * k2 + 1)]
                        o_ref[qi * GRID_W:(qi + 1) * GRID_W, k2 * 128:(k2 + 1) * 128] = jnp.concatenate(blocks, axis=1)

    (tiles,), sent = _carrier_call(
        "na_bias_tiles", body, (3, NH), [pl.BlockSpec((None, 15, GRID_W, GRID_W), lambda c, h: (h, 0, 0, 0))],
        [pl.BlockSpec((None, None, NA_QB, NA_WIN), lambda c, h: (c, h, 0, 0))], [jax.ShapeDtypeStruct((3, NH, NA_QB, NA_WIN), F32)],
        [], (_rpb_expand(rpb),), carry)
    return tiles, sent


def _na_cls(b):
    return jnp.where(b == 0, 0, jnp.where(b == NA_BLOCKS - 1, 2, 1))


def _na_start(b):
    return pl.multiple_of(jnp.clip(b * NA_QROWS - 4, 0, NA_ROWS_TOTAL - NA_WROWS) * GRID_W, GRID_W)


NA_FWD_HPS = 8
NA_BWD_HPS = 4


def _na_in_specs(hps):
    lw = hps * HD
    nlw = DM // lw
    return [pl.BlockSpec((NA_QB, lw), lambda hp, b: (b, hp)),
            pl.BlockSpec((SEQ, lw), lambda hp, b: (0, nlw + hp)),
            pl.BlockSpec((SEQ, lw), lambda hp, b: (0, 2 * nlw + hp)),
            pl.BlockSpec((None, hps, NA_QB, NA_WIN), lambda hp, b: (_na_cls(b), hp, 0, 0))]


def _na_fwd(qkv, bias, carry):
    lw = NA_FWD_HPS * HD

    def body(q_ref, k_ref, v_ref, b_ref, o_ref):
        start = _na_start(pl.program_id(1))
        q = q_ref[...]
        kw = k_ref[pl.ds(start, NA_WIN), :]
        vw = v_ref[pl.ds(start, NA_WIN), :]
        outs = []
        for hh in range(NA_FWD_HPS):
            sl = slice(hh * HD, (hh + 1) * HD)
            s = lax.dot_general(q[:, sl] * QSCALE, kw[:, sl], _NT, preferred_element_type=F32) + b_ref[hh]
            p = jnp.exp(s - jnp.max(s, axis=-1, keepdims=True))
            l = jnp.sum(p, axis=-1, keepdims=True)
            outs.append(jnp.dot(p.astype(BF16), vw[:, sl], preferred_element_type=F32) / l)
        o_ref[...] = jnp.concatenate(outs, axis=1).astype(o_ref.dtype)

    (o,), sent = _carrier_call(
        "na_fwd", body, (NH // NA_FWD_HPS, NA_BLOCKS), _na_in_specs(NA_FWD_HPS), [pl.BlockSpec((NA_QB, lw), lambda hp, b: (b, hp))],
        [jax.ShapeDtypeStruct((SEQ, DM), BF16)], [], (qkv, qkv, qkv, bias), carry)
    return o, sent


def _na_bwd(qkv, bias, do, carry):
    lw = NA_BWD_HPS * HD

    def body(q_ref, k_ref, v_ref, b_ref, do_ref, dqkv_ref, z_ref, dk_acc, dv_acc):
        blk = pl.program_id(1)

        @pl.when(blk == 0)
        def _():
            dk_acc[...] = jnp.zeros_like(dk_acc)
            dv_acc[...] = jnp.zeros_like(dv_acc)
            z_ref[...] = jnp.zeros_like(z_ref)

        start = _na_start(blk)
        q = q_ref[...]
        do = do_ref[...]
        kw = k_ref[pl.ds(start, NA_WIN), :]
        vw = v_ref[pl.ds(start, NA_WIN), :]
        dqs, dks, dvs, dss = [], [], [], []
        for hh in range(NA_BWD_HPS):
            sl = slice(hh * HD, (hh + 1) * HD)
            qh = q[:, sl] * QSCALE
            s = lax.dot_general(qh, kw[:, sl], _NT, preferred_element_type=F32) + b_ref[hh]
            p = jnp.exp(s - jnp.max(s, axis=-1, keepdims=True))
            p = p / jnp.sum(p, axis=-1, keepdims=True)
            dp = lax.dot_general(do[:, sl], vw[:, sl], _NT, preferred_element_type=F32)
            ds = p * (dp - jnp.sum(p * dp, axis=-1, keepdims=True))
            dsb = ds.astype(BF16)
            dqs.append(jnp.dot(dsb, kw[:, sl], preferred_element_type=F32) * QSCALE)
            dks.append(lax.dot_general(qh, dsb, _TN, preferred_element_type=F32).T)
            dvs.append(lax.dot_general(do[:, sl], p.astype(BF16), _TN, preferred_element_type=F32).T)
            dss.append(ds)
        for cls, (i0, ws) in enumerate(NA_CLASSES):
            @pl.when(_na_cls(blk) == cls)
            def _(i0=i0, ws=ws):
                for hh, ds in enumerate(dss):
                    for qi, kr, dr in _na_pairs(i0, ws):
                        z_ref[hh, dr * GRID_W:(dr + 1) * GRID_W, :] += ds[qi * GRID_W:(qi + 1) * GRID_W, kr * GRID_W:(kr + 1) * GRID_W]
        dqkv_ref[0, pl.ds(pl.multiple_of(blk * NA_QB, NA_QB), NA_QB), :] = jnp.concatenate(dqs, axis=1).astype(dqkv_ref.dtype)
        dk_acc[pl.ds(start, NA_WIN), :] += jnp.concatenate(dks, axis=1)
        dv_acc[pl.ds(start, NA_WIN), :] += jnp.concatenate(dvs, axis=1)

        @pl.when(blk == NA_BLOCKS - 1)
        def _():
            dqkv_ref[1] = dk_acc[...].astype(dqkv_ref.dtype)
            dqkv_ref[2] = dv_acc[...].astype(dqkv_ref.dtype)

    (dqkv, z), sent = _carrier_call(
        "na_bwd", body, (NH // NA_BWD_HPS, NA_BLOCKS),
        _na_in_specs(NA_BWD_HPS) + [pl.BlockSpec((NA_QB, lw), lambda hp, b: (b, hp))],
        [pl.BlockSpec((3, SEQ, lw), lambda hp, b: (0, 0, hp)), pl.BlockSpec((NA_BWD_HPS, 15 * GRID_W, GRID_W), lambda hp, b: (hp, 0, 0))],
        [jax.ShapeDtypeStruct((3, SEQ, DM), BF16), jax.ShapeDtypeStruct((NH, 15 * GRID_W, GRID_W), F32)],
        [pltpu.VMEM((SEQ, lw), F32), pltpu.VMEM((SEQ, lw), F32)], (qkv, qkv, qkv, bias, do), carry)
    return dqkv, z, sent


def _rpb_grad(z):
    z2 = z.reshape(NH * 15, GRID_W * GRID_W)

    def body(z_ref, e_ref, o_ref):
        o_ref[...] = jnp.dot(z_ref[...], e_ref[...], preferred_element_type=F32, precision=lax.Precision.HIGHEST)

    out = pl.pallas_call(body, out_shape=jax.ShapeDtypeStruct((NH * 15, 128), F32), name="rpb_grad",
                         compiler_params=pltpu.CompilerParams(vmem_limit_bytes=VMEM_LIMIT))(z2, _diag_onehot())
    return out[:, :31].reshape(NH, 15, 31)


DIL_BLOCKS = SEQ // DIL_QB
DIL_HPS = 8
DIL_LW = DIL_HPS * HD
DIL_NLW = DM // DIL_LW


COLS = 128


def _col_spec():
    return pl.BlockSpec((SEQ, COLS), lambda j: (0, j))


def _grp_spec():
    return pl.BlockSpec((3, SEQ, COLS), lambda j: (0, 0, j))


def _store_group_order(dst_ref, src_ref):
    for g, d in enumerate(DIL):
        n = SEQ // d
        for r in range(d):
            dst_ref[g, r * n:(r + 1) * n, :] = src_ref[pl.ds(r, n, stride=d), :].astype(dst_ref.dtype)


def _store_token_order(dst_ref, src_ref, g):
    d = DIL[g]
    n = SEQ // d
    for r in range(d):
        dst_ref[pl.ds(r, n, stride=d), :] = src_ref[g, r * n:(r + 1) * n, :]


def _to_groups(name, a):
    def body(a_ref, o_ref, t_ref):
        _store_group_order(o_ref, a_ref)
        for g in range(3):
            t_ref[g] = o_ref[g].astype(F32).T.astype(t_ref.dtype)

    return pl.pallas_call(
        body, grid=(DM // COLS,), in_specs=[_col_spec()], out_specs=[_grp_spec(), pl.BlockSpec((3, COLS, SEQ), lambda j: (0, j, 0))],
        out_shape=[jax.ShapeDtypeStruct((3, SEQ, DM), BF16), jax.ShapeDtypeStruct((3, DM, SEQ), BF16)],
        compiler_params=_params(("parallel",)), name=name)(a)


def _from_groups_sum(name, a):
    def body(a_ref, o_ref, t1, t2):
        _store_token_order(t1, a_ref, 1)
        _store_token_order(t2, a_ref, 2)
        o_ref[...] = (a_ref[0] + t1[...]) + t2[...]

    return pl.pallas_call(body, grid=(DM // COLS,), in_specs=[_grp_spec()], out_specs=_col_spec(),
                          out_shape=jax.ShapeDtypeStruct((SEQ, DM), F32), scratch_shapes=[pltpu.VMEM((SEQ, COLS), F32)] * 2,
                          compiler_params=_params(("parallel",)), name=name)(a)


def _dil_start(b):
    return pl.multiple_of(jnp.clip(b * DIL_QB - DIL_RADIUS, 0, SEQ - DIL_WIN), DIL_RADIUS)


def _dil_neg_dist(g, ii, jj):
    shift = 11 - 2 * g
    dist = jnp.abs(ii - jj)
    valid = (dist <= DIL_RADIUS) & (jnp.right_shift(ii, shift) == jnp.right_shift(jj, shift))
    return jnp.where(valid, -dist.astype(F32), NEG)


def _dil_in_specs():
    return [pl.BlockSpec(memory_space=pltpu.SMEM),
            pl.BlockSpec((None, DIL_QB, DIL_LW), lambda g, hp, b: (g, b, hp)),
            pl.BlockSpec((None, SEQ, DIL_LW), lambda g, hp, b: (g, 0, DIL_NLW + hp)),
            pl.BlockSpec((None, SEQ, DIL_LW), lambda g, hp, b: (g, 0, 2 * DIL_NLW + hp))]


def _dil_fwd(qkv, slopes, carry):
    def body(sl_ref, q_ref, k_ref, v_ref, o_ref, lse_ref):
        g, hp, b = pl.program_id(0), pl.program_id(1), pl.program_id(2)
        start = _dil_start(b)
        neg_dist = _dil_neg_dist(g, b * DIL_QB + lax.broadcasted_iota(jnp.int32, (DIL_QB, DIL_WIN), 0),
                                 start + lax.broadcasted_iota(jnp.int32, (DIL_QB, DIL_WIN), 1))
        dil = jnp.left_shift(1, 2 * g).astype(F32)
        q = q_ref[...]
        kw = k_ref[pl.ds(start, DIL_WIN), :]
        vw = v_ref[pl.ds(start, DIL_WIN), :]
        outs, lses = [], []
        for hh in range(DIL_HPS):
            sl = slice(hh * HD, (hh + 1) * HD)
            s = lax.dot_general(q[:, sl] * QSCALE, kw[:, sl], _NT, preferred_element_type=F32)
            s = s + (sl_ref[hp * DIL_HPS + hh] * dil) * neg_dist
            m = jnp.max(s, axis=-1, keepdims=True)
            p = jnp.exp(s - m)
            l = jnp.sum(p, axis=-1, keepdims=True)
            outs.append(jnp.dot(p.astype(BF16), vw[:, sl], preferred_element_type=F32) / l)
            lses.append(jnp.broadcast_to(m + jnp.log(l), (DIL_QB, HD)))
        o_ref[...] = jnp.concatenate(outs, axis=1)
        lse_ref[...] = jnp.concatenate(lses, axis=1)

    ospec = pl.BlockSpec((None, DIL_QB, DIL_LW), lambda g, hp, b: (g, b, hp))
    sh = jax.ShapeDtypeStruct((3, SEQ, DM), F32)
    (o, lse), sent = _carrier_call("dil_fwd", body, (3, DIL_NLW, DIL_BLOCKS), _dil_in_specs(), [ospec, ospec], [sh, sh], [],
                                   (slopes, qkv, qkv, qkv), carry)
    return o, lse, sent


def _dil_merge(o_all, lse_all):
    def body(o_ref, l_ref, out_ref, lse_ref, o1, o2, l1, l2):
        for g, (ot, lt) in ((1, (o1, l1)), (2, (o2, l2))):
            _store_token_order(ot, o_ref, g)
            _store_token_order(lt, l_ref, g)
        la, lb, lc = l_ref[0], l1[...], l2[...]
        m = jnp.maximum(jnp.maximum(la, lb), lc)
        wa, wb, wc = jnp.exp(la - m), jnp.exp(lb - m), jnp.exp(lc - m)
        sw = (wa + wb) + wc
        out_ref[...] = (((wa * o_ref[0] + wb * o1[...]) + wc * o2[...]) / sw).astype(out_ref.dtype)
        lse_ref[...] = m + jnp.log(sw)

    return pl.pallas_call(
        body, grid=(DM // COLS,), in_specs=[_grp_spec(), _grp_spec()], out_specs=[_col_spec(), _col_spec()],
        out_shape=[jax.ShapeDtypeStruct((SEQ, DM), BF16), jax.ShapeDtypeStruct((SEQ, DM), F32)],
        scratch_shapes=[pltpu.VMEM((SEQ, COLS), F32)] * 4, compiler_params=_params(("parallel",)), name="dil_merge")(o_all, lse_all)


def _dil_bwd_prep(do, o, lse):
    heads = COLS // HD

    def body(do_ref, o_ref, lse_ref, dog_ref, ddr_ref, lser_ref, dd, grp):
        prod = do_ref[...] * o_ref[...].astype(F32)
        dd[...] = jnp.concatenate(
            [jnp.broadcast_to(jnp.sum(prod[:, h * HD:(h + 1) * HD], axis=-1, keepdims=True), (SEQ, HD)) for h in range(heads)], axis=1)
        _store_group_order(dog_ref, do_ref)
        for src, dst in ((dd, ddr_ref), (lse_ref, lser_ref)):
            _store_group_order(grp, src)
            for g in range(3):
                t = grp[g].T
                for h in range(heads):
                    dst[g, h] = t[h * HD:h * HD + 8, :]

    rows = jax.ShapeDtypeStruct((3, NH, 8, SEQ), F32)
    rspec = pl.BlockSpec((3, heads, 8, SEQ), lambda j: (0, j, 0, 0))
    return pl.pallas_call(
        body, grid=(DM // COLS,), in_specs=[_col_spec()] * 3, out_specs=[_grp_spec(), rspec, rspec],
        out_shape=[jax.ShapeDtypeStruct((3, SEQ, DM), BF16), rows, rows],
        scratch_shapes=[pltpu.VMEM((SEQ, COLS), F32), pltpu.VMEM((3, SEQ, COLS), F32)],
        compiler_params=_params(("parallel",)), name="dil_bwd_prep")(do, o, lse)


def _dil_bwd(qkv, do, dd, lse, slopes, carry):
    def body(sl_ref, q_ref, k_ref, v_ref, do_ref, dd_ref, lse_ref, dqkv_ref, dk_acc, dv_acc):
        g, hp, b = pl.program_id(0), pl.program_id(1), pl.program_id(2)

        @pl.when(b == 0)
        def _():
            dk_acc[...] = jnp.zeros_like(dk_acc)
            dv_acc[...] = jnp.zeros_like(dv_acc)

        start = _dil_start(b)
        neg_dist = _dil_neg_dist(g, b * DIL_QB + lax.broadcasted_iota(jnp.int32, (DIL_WIN, DIL_QB), 1),
                                 start + lax.broadcasted_iota(jnp.int32, (DIL_WIN, DIL_QB), 0))
        dil = jnp.left_shift(1, 2 * g).astype(F32)
        q = q_ref[...]
        do = do_ref[...]
        kw = k_ref[pl.ds(start, DIL_WIN), :]
        vw = v_ref[pl.ds(start, DIL_WIN), :]
        dqs, dks, dvs = [], [], []
        for hh in range(DIL_HPS):
            sl = slice(hh * HD, (hh + 1) * HD)
            qh = q[:, sl] * QSCALE
            st = lax.dot_general(kw[:, sl], qh, _NT, preferred_element_type=F32)
            st = st + (sl_ref[hp * DIL_HPS + hh] * dil) * neg_dist
            pt = jnp.exp(st - lse_ref[hh, 0:1, :])
            dpt = lax.dot_general(vw[:, sl], do[:, sl], _NT, preferred_element_type=F32)
            dst = (pt * (dpt - dd_ref[hh, 0:1, :])).astype(BF16)
            dqs.append(lax.dot_general(kw[:, sl], dst, _TN, preferred_element_type=F32).T * QSCALE)
            dks.append(jnp.dot(dst, qh, preferred_element_type=F32))
            dvs.append(jnp.dot(pt.astype(BF16), do[:, sl], preferred_element_type=F32))
        dqkv_ref[0, pl.ds(pl.multiple_of(b * DIL_QB, DIL_QB), DIL_QB), :] = jnp.concatenate(dqs, axis=1).astype(dqkv_ref.dtype)
        dk_acc[pl.ds(start, DIL_WIN), :] += jnp.concatenate(dks, axis=1)
        dv_acc[pl.ds(start, DIL_WIN), :] += jnp.concatenate(dvs, axis=1)

        @pl.when(b == DIL_BLOCKS - 1)
        def _():
            dqkv_ref[1] = dk_acc[...].astype(dqkv_ref.dtype)
            dqkv_ref[2] = dv_acc[...].astype(dqkv_ref.dtype)

    qspec = pl.BlockSpec((None, DIL_QB, DIL_LW), lambda g, hp, b: (g, b, hp))
    rspec = pl.BlockSpec((None, DIL_HPS, 8, DIL_QB), lambda g, hp, b: (g, hp, 0, b))
    (dqkv,), sent = _carrier_call(
        "dil_bwd", body, (3, DIL_NLW, DIL_BLOCKS), _dil_in_specs() + [qspec, rspec, rspec],
        [pl.BlockSpec((None, 3, SEQ, DIL_LW), lambda g, hp, b: (g, 0, 0, hp))], [jax.ShapeDtypeStruct((3, 3, SEQ, DM), BF16)],
        [pltpu.VMEM((SEQ, DIL_LW), F32), pltpu.VMEM((SEQ, DIL_LW), F32)], (slopes, qkv, qkv, qkv, do, dd, lse), carry)
    return dqkv, sent


def _ffn_fwd(name, x, g_pre, g_post, wgt4, wut4, wd4, carry):
    tm = 512

    def body(x_ref, gpre_ref, gpost_ref, wg_ref, wu_ref, wd_ref, xn_ref, h_ref, gate_ref, up_ref, u_ref, acc):
        s = pl.program_id(1)

        @pl.when(s == 0)
        def _():
            x = x_ref[...]
            r = lax.rsqrt(jnp.mean(x * x, axis=-1, keepdims=True) + RMS_EPS)
            h_ref[...] = (x * r * gpre_ref[...]).astype(h_ref.dtype)

        h = h_ref[...]
        gate = lax.dot_general(h, wg_ref[...], _NT, preferred_element_type=F32).astype(BF16)
        up = lax.dot_general(h, wu_ref[...], _NT, preferred_element_type=F32).astype(BF16)
        gate_ref[...] = gate
        up_ref[...] = up
        gf = gate.astype(F32)
        act = (gf * jax.nn.sigmoid(gf) * up.astype(F32)).astype(BF16)
        part = jnp.dot(act, wd_ref[...], preferred_element_type=F32)

        @pl.when(s == 0)
        def _():
            acc[...] = part

        @pl.when(s > 0)
        def _():
            acc[...] += part

        @pl.when(s == NCHIP - 1)
        def _():
            u = acc[...]
            u_ref[...] = u
            r = lax.rsqrt(jnp.mean(u * u, axis=-1, keepdims=True) + RMS_EPS)
            xn_ref[...] = x_ref[...] + u * r * gpost_ref[...]

    rows = pl.BlockSpec((tm, DM), lambda i, s: (i, 0))
    vec = pl.BlockSpec((1, DM), lambda i, s: (0, 0))
    wspec = _ffn_wspec(lambda i, s: (s, 0, 0))
    mid = pl.BlockSpec((None, tm, FSH), lambda i, s: (s, i, 0))
    outs, sent = _carrier_call(
        name, body, (SEQ // tm, NCHIP), [rows, vec, vec, wspec, wspec, wspec], [rows, rows, mid, mid, rows],
        [jax.ShapeDtypeStruct((SEQ, DM), F32), jax.ShapeDtypeStruct((SEQ, DM), BF16), jax.ShapeDtypeStruct((NCHIP, SEQ, FSH), BF16),
         jax.ShapeDtypeStruct((NCHIP, SEQ, FSH), BF16), jax.ShapeDtypeStruct((SEQ, DM), F32)],
        [pltpu.VMEM((tm, DM), F32)], (x, g_pre, g_post, wgt4, wut4, wd4), carry)
    return outs, sent


def _ffn_block(layer, x, g_pre, g_post, ex):
    tag = f"l{layer}_ffn_fwd"
    (x_new, h, gate, up, u), sent = _ffn_fwd(tag, x, g_pre, g_post, ex.weight(("ffn_w_gate", layer)), ex.weight(("ffn_w_up", layer)),
                                             ex.weight(("ffn_w_down", layer)), ex.carry(tag))
    ex.carried(tag, sent)
    return x_new, (x, h, gate, up, u)


def _ffn_bwd(name, dx, x, gate, up, u, g_pre, g_post, wgt4, wut4, wd4, carry):
    tm = 512

    def body(dx_ref, x_ref, gate_ref, up_ref, u_ref, gpre_ref, gpost_ref, wg_ref, wu_ref, wd_ref,
             dxin_ref, du_ref, dgate_ref, dup_ref, act_ref, dgpre_ref, dgpost_ref, dh_acc):
        i, s = pl.program_id(0), pl.program_id(1)

        @pl.when((i == 0) & (s == 0))
        def _():
            dgpre_ref[...] = jnp.zeros_like(dgpre_ref)
            dgpost_ref[...] = jnp.zeros_like(dgpost_ref)

        @pl.when(s == 0)
        def _():
            dy = dx_ref[...]
            uu = u_ref[...]
            r = lax.rsqrt(jnp.mean(uu * uu, axis=-1, keepdims=True) + RMS_EPS)
            yh = uu * r
            t = dy * gpost_ref[...]
            du_ref[...] = (r * (t - yh * jnp.mean(t * yh, axis=-1, keepdims=True))).astype(du_ref.dtype)
            dgpost_ref[...] += jnp.sum(dy * yh, axis=0, keepdims=True)

        dact = lax.dot_general(du_ref[...], wd_ref[...], _NT, preferred_element_type=F32)
        g = gate_ref[...].astype(F32)
        upv = up_ref[...].astype(F32)
        sg = jax.nn.sigmoid(g)
        dgate = (dact * upv * sg * (1.0 + g * (1.0 - sg))).astype(BF16)
        dup = (dact * g * sg).astype(BF16)
        dgate_ref[...] = dgate
        dup_ref[...] = dup
        act_ref[...] = (g * sg * upv).astype(act_ref.dtype)
        part = jnp.dot(dgate, wg_ref[...], preferred_element_type=F32) + jnp.dot(dup, wu_ref[...], preferred_element_type=F32)

        @pl.when(s == 0)
        def _():
            dh_acc[...] = part

        @pl.when(s > 0)
        def _():
            dh_acc[...] += part

        @pl.when(s == NCHIP - 1)
        def _():
            dh = dh_acc[...]
            xx = x_ref[...]
            r = lax.rsqrt(jnp.mean(xx * xx, axis=-1, keepdims=True) + RMS_EPS)
            yh = xx * r
            t = dh * gpre_ref[...]
            dxin_ref[...] = dx_ref[...] + r * (t - yh * jnp.mean(t * yh, axis=-1, keepdims=True))
            dgpre_ref[...] += jnp.sum(dh * yh, axis=0, keepdims=True)

    rows = pl.BlockSpec((tm, DM), lambda i, s: (i, 0))
    vec = pl.BlockSpec((1, DM), lambda i, s: (0, 0))
    wspec = _ffn_wspec(lambda i, s: (s, 0, 0))
    mid = pl.BlockSpec((None, tm, FSH), lambda i, s: (s, i, 0))
    mid_shape = jax.ShapeDtypeStruct((NCHIP, SEQ, FSH), BF16)
    return _carrier_call(
        name, body, (SEQ // tm, NCHIP), [rows, rows, mid, mid, rows, vec, vec, wspec, wspec, wspec], [rows, rows, mid, mid, mid, vec, vec],
        [jax.ShapeDtypeStruct((SEQ, DM), F32), jax.ShapeDtypeStruct((SEQ, DM), BF16), mid_shape, mid_shape, mid_shape,
         jax.ShapeDtypeStruct((1, DM), F32), jax.ShapeDtypeStruct((1, DM), F32)],
        [pltpu.VMEM((tm, DM), F32)], (dx, x, gate, up, u, g_pre, g_post, wgt4, wut4, wd4), carry)


def _ffn_block_bwd(layer, dx, saved, g_pre, g_post, ex):
    tag = f"l{layer}"
    x, h, gate, up, u = saved
    (dx_in, du, dgate, dup, act, dg_pre, dg_post), sent = _ffn_bwd(
        f"{tag}_ffn_bwd", dx, x, gate, up, u, g_pre, g_post, ex.weight(("ffn_w_gate", layer)), ex.weight(("ffn_w_up", layer)),
        ex.weight(("ffn_w_down", layer)), ex.carry(f"{tag}_ffn_bwd"))
    ex.carried(f"{tag}_ffn_bwd", sent)
    d_wd = _ffn_bwd_dw(f"{tag}_dwd", act, du)
    d_wg = _ffn_bwd_dw(f"{tag}_dwg", dgate, h)
    d_wu = _ffn_bwd_dw(f"{tag}_dwu", dup, h)
    ex.grads(f"{tag}_ffn", {("ffn_w_gate", layer): d_wg, ("ffn_w_up", layer): d_wu, ("ffn_w_down", layer): d_wd})
    return dx_in, dg_pre, dg_post


def _alibi_slopes():
    return 2.0 ** (-8.0 * jnp.arange(1, NH + 1, dtype=F32) / NH)


def _local_step(x, target, norms, rpb, ex):
    g_mix_pre, g_mix_post, g_ffn_pre, g_ffn_post = norms
    row = lambda a, i: a[i:i + 1]

    bias, sent = _na_bias_tiles(rpb, ex.carry("na_bias_tiles"))
    ex.carried("na_bias_tiles", sent)
    h0, h0t = _rms_fwd_both("l0_mix_pre", x, row(g_mix_pre, 0))
    qkv0, sent = _qkv_fwd("l0_qkv", h0[None], ex.weight(("na_w_qkv", 0)), ex.carry("l0_qkv"))
    ex.carried("l0_qkv", sent)
    o0, sent = _na_fwd(qkv0[0], bias, ex.carry("na_fwd"))
    ex.carried("na_fwd", sent)
    na_wo = ex.weight(("na_w_o", 0)).reshape(DM, DM)
    x1, u0 = _proj_fwd("l0_proj", o0, na_wo, x, row(g_mix_post, 0))
    x2, ffn0 = _ffn_block(0, x1, row(g_ffn_pre, 0), row(g_ffn_post, 0), ex)

    slopes = _alibi_slopes()
    h2g, h2gt = _to_groups("l1_h_groups", _rms_fwd("l1_mix_pre", x2, row(g_mix_pre, 1), F32))
    dil_wqkv = ex.weight(("dil_w_qkv", 0))
    qkv1, sent = _qkv_fwd("l1_qkv", h2g, dil_wqkv, ex.carry("l1_qkv"))
    ex.carried("l1_qkv", sent)
    og, lg, sent = _dil_fwd(qkv1, slopes, ex.carry("dil_fwd"))
    ex.carried("dil_fwd", sent)
    o1, lse = _dil_merge(og, lg)
    dil_wo = ex.weight(("dil_w_o", 0)).reshape(DM, DM)
    x3, u1 = _proj_fwd("l1_proj", o1, dil_wo, x2, row(g_mix_post, 1))
    x4, ffn1 = _ffn_block(1, x3, row(g_ffn_pre, 1), row(g_ffn_post, 1), ex)

    dx4, loss_row = _loss_grad("loss", x4, target)

    dx3, dg_fpre1, dg_fpost1 = _ffn_block_bwd(1, dx4, ffn1, row(g_ffn_pre, 1), row(g_ffn_post, 1), ex)
    (do1, du1, dg_mpost1), sent = _proj_bwd("l1_proj_bwd", dx3, u1, row(g_mix_post, 1), dil_wo, F32, ex.carry("l1_proj_bwd"))
    ex.carried("l1_proj_bwd", sent)
    d_dil_wo = _proj_bwd_dw("l1_dwo", o1, du1)
    dog, ddg, lseg = _dil_bwd_prep(do1, o1, lse)
    dqkv1, sent = _dil_bwd(qkv1, dog, ddg, lseg, slopes, ex.carry("dil_bwd"))
    ex.carried("dil_bwd", sent)
    d_dil_wqkv, sent = _qkv_bwd_dw("l1_dwqkv", h2gt, dqkv1, dil_wqkv.shape[2], ex.carry("l1_dwqkv"))
    ex.carried("l1_dwqkv", sent)
    ex.grads("l1_mix", {("dil_w_qkv", 0): d_dil_wqkv, ("dil_w_o", 0): d_dil_wo.reshape(NCHIP, DM // NCHIP, DM)})
    dh2g, sent = _qkv_bwd_dh("l1_dh", dqkv1, dil_wqkv, ex.carry("l1_dh"))
    ex.carried("l1_dh", sent)
    dh2 = _from_groups_sum("l1_dh_tokens", dh2g)
    (dx2, dg_mpre1), sent = _norm_bwd("l1_mix_pre_bwd", dh2, x2, row(g_mix_pre, 1), dx3, ex.carry("l1_mix_pre_bwd"))
    ex.carried("l1_mix_pre_bwd", sent)

    dx1, dg_fpre0, dg_fpost0 = _ffn_block_bwd(0, dx2, ffn0, row(g_ffn_pre, 0), row(g_ffn_post, 0), ex)
    (do0, du0, dg_mpost0), sent = _proj_bwd("l0_proj_bwd", dx1, u0, row(g_mix_post, 0), na_wo, BF16, ex.carry("l0_proj_bwd"))
    ex.carried("l0_proj_bwd", sent)
    d_na_wo = _proj_bwd_dw("l0_dwo", o0, du0)
    dqkv0, z, sent = _na_bwd(qkv0[0], bias, do0, ex.carry("na_bwd"))
    ex.carried("na_bwd", sent)
    d_rpb = _rpb_grad(z)
    na_wqkv = ex.weight(("na_w_qkv", 0))
    d_na_wqkv, sent = _qkv_bwd_dw("l0_dwqkv", h0t[None], dqkv0[None], na_wqkv.shape[2], ex.carry("l0_dwqkv"))
    ex.carried("l0_dwqkv", sent)
    ex.grads("l0_mix", {("na_w_qkv", 0): d_na_wqkv, ("na_w_o", 0): d_na_wo.reshape(NCHIP, DM // NCHIP, DM)})
    dh0, sent = _qkv_bwd_dh("l0_dh", dqkv0[None], na_wqkv, ex.carry("l0_dh"))
    ex.carried("l0_dh", sent)
    (dx0, dg_mpre0), sent = _norm_bwd("l0_mix_pre_bwd", dh0[0], x, row(g_mix_pre, 0), dx1, ex.carry("l0_mix_pre_bwd"))
    ex.carried("l0_mix_pre_bwd", sent)

    dnorms = (jnp.concatenate([dg_mpre0, dg_mpre1]), jnp.concatenate([dg_mpost0, dg_mpost1]),
              jnp.concatenate([dg_fpre0, dg_fpre1]), jnp.concatenate([dg_fpost0, dg_fpost1]))
    return loss_row, dx0, dnorms, d_rpb


def _place():
    x, y, c = lax.axis_index("x"), lax.axis_index("y"), lax.axis_index("c")
    chips = ((1 - x, y), (x, 1 - y), (1 - x, 1 - y))
    return x, y, c, chips


def _chip_id(chip):
    return 2 * chip[0] + chip[1]


def _gather_copies(shards):
    n = len(shards)

    def copies(src, out, sems):
        send_sems, recv_sems = sems
        x, y, c, chips = _place()

        def copy(t, k, chip, half, to, from_src=False):
            blk = out[t].at[_chip_id(chip), half]
            return pltpu.make_async_remote_copy(
                src_ref=src[t].at[half] if from_src else blk, dst_ref=blk,
                send_sem=send_sems.at[6 * t + k], recv_sem=recv_sems.at[6 * t + k], device_id=to, device_id_type=MESH)

        return copy, x, y, c, chips

    def issue(src, out, sems):
        copy, x, y, c, chips = copies(src, out, sems)
        for t in range(n):
            for j, chip in enumerate(chips):
                copy(t, j, (x, y), c, (*chip, c), from_src=True).start()

    def drain(src, out, sems):
        copy, x, y, c, chips = copies(src, out, sems)
        passed = []
        for t in range(n):
            for j, chip in enumerate(chips):
                copy(t, j, chip, c, (x, y, c)).wait_recv()
                fwd = copy(t, 3 + j, chip, c, (x, y, 1 - c))
                fwd.start()
                passed.append(fwd)
        for t in range(n):
            for j, chip in enumerate(chips):
                copy(t, 3 + j, chip, 1 - c, (x, y, c)).wait_recv()
        for t in range(n):
            for j, chip in enumerate(chips):
                copy(t, j, (x, y), c, (*chip, c), from_src=True).wait_send()
        for cp in passed:
            cp.wait_send()

    return _Carried(shards, [jax.ShapeDtypeStruct((NCHIP,) + s.shape, s.dtype) for s in shards], (6 * n, 6 * n), issue, drain)


def _pair_exchange_copies(grads):
    n = len(grads)

    def copies(g, theirs, sems):
        send_sems, recv_sems = sems
        x, y, c, _ = _place()
        return [pltpu.make_async_remote_copy(src_ref=g[t].at[:, 1 - c], dst_ref=theirs[t], send_sem=send_sems.at[t],
                                             recv_sem=recv_sems.at[t], device_id=(x, y, 1 - c), device_id_type=MESH) for t in range(n)]

    def issue(g, theirs, sems):
        for cp in copies(g, theirs, sems):
            cp.start()

    def drain(g, theirs, sems):
        for cp in copies(g, theirs, sems):
            cp.wait()

    return _Carried(grads, [jax.ShapeDtypeStruct((NCHIP,) + g.shape[2:], g.dtype) for g in grads], (n, n), issue, drain)


def _chip_exchange_copies(items):
    flat = [(t, i, j) for t, (_, peers) in enumerate(items) for i, j in enumerate(peers)]

    def copies(p, slots, sems):
        send_sems, recv_sems = sems
        x, y, c, chips = _place()
        return [pltpu.make_async_remote_copy(src_ref=p[t].at[_chip_id(chips[j])], dst_ref=slots[t].at[i], send_sem=send_sems.at[k],
                                             recv_sem=recv_sems.at[k], device_id=(*chips[j], c), device_id_type=MESH)
                for k, (t, i, j) in enumerate(flat)]

    def issue(p, slots, sems):
        for cp in copies(p, slots, sems):
            cp.start()

    def drain(p, slots, sems):
        for cp in copies(p, slots, sems):
            cp.wait()

    return _Carried([p for p, _ in items], [jax.ShapeDtypeStruct((len(peers),) + p.shape[1:], p.dtype) for p, peers in items],
                    (len(flat), len(flat)), issue, drain)


def _pair_share_copies(halves):
    n = len(halves)

    def copies(h, other, sems):
        send_sems, recv_sems = sems
        x, y, c, _ = _place()
        return [pltpu.make_async_remote_copy(src_ref=h[t], dst_ref=other[t], send_sem=send_sems.at[t], recv_sem=recv_sems.at[t],
                                             device_id=(x, y, 1 - c), device_id_type=MESH) for t in range(n)]

    def issue(h, other, sems):
        for cp in copies(h, other, sems):
            cp.start()

    def drain(h, other, sems):
        for cp in copies(h, other, sems):
            cp.wait()

    return _Carried(halves, [jax.ShapeDtypeStruct(h.shape, h.dtype) for h in halves], (n, n), issue, drain)


SMALL_ROWS = 128


def _allreduce_small(v):
    def body(v_ref, o_ref, buf, send_sems, recv_sems):
        x, y, c, _ = _place()
        me = 4 * x + 2 * y + c
        flip = lambda a, f: 1 - a if f else a
        buf[me] = v_ref[...]
        peers = [(flip(x, d >> 2 & 1), flip(y, d >> 1 & 1), flip(c, d & 1)) for d in range(1, 8)]
        sends = [pltpu.make_async_remote_copy(src_ref=v_ref, dst_ref=buf.at[me], send_sem=send_sems.at[i], recv_sem=recv_sems.at[i],
                                              device_id=peer, device_id_type=MESH) for i, peer in enumerate(peers)]
        for cp in sends:
            cp.start()
        for i, (px, py, pc) in enumerate(peers):
            pltpu.make_async_remote_copy(src_ref=v_ref, dst_ref=buf.at[4 * px + 2 * py + pc], send_sem=send_sems.at[i], recv_sem=recv_sems.at[i],
                                         device_id=(px, py, pc), device_id_type=MESH).wait_recv()
        for cp in sends:
            cp.wait_send()
        acc = buf[0]
        for k in range(1, 8):
            acc = acc + buf[k]
        o_ref[...] = acc

    vm = pl.BlockSpec(memory_space=pltpu.VMEM)
    return pl.pallas_call(
        body, in_specs=[vm], out_specs=vm, out_shape=jax.ShapeDtypeStruct((SMALL_ROWS, 128), F32),
        scratch_shapes=[pltpu.VMEM((8, SMALL_ROWS, 128), F32), pltpu.SemaphoreType.DMA((7,)), pltpu.SemaphoreType.DMA((7,))],
        compiler_params=pltpu.CompilerParams(has_side_effects=True), name="allreduce_small")(v)


def _row_block(rows, cols, budget=3 << 19):
    best = 8
    for bm in range(8, rows + 1, 8):
        if rows % bm == 0 and bm * cols * 4 <= budget:
            best = bm
    return best


def _pair_sum(name, place, gs, theirs):
    n = len(gs)
    _, m, c = theirs[0].shape
    bm = _row_block(m, c)

    def body(place_ref, *refs):
        for a_ref, b_ref, o_ref in zip(refs[:n], refs[n:2 * n], refs[2 * n:]):
            o_ref[...] = (a_ref[...].astype(F32) + b_ref[...].astype(F32)).astype(o_ref.dtype)

    spec = pl.BlockSpec((None, bm, c), lambda k, i, pr: (k, i, 0))
    return pl.pallas_call(
        body, out_shape=[jax.ShapeDtypeStruct(theirs[0].shape, BF16)] * n,
        grid_spec=pltpu.PrefetchScalarGridSpec(
            num_scalar_prefetch=1, grid=(NCHIP, m // bm),
            in_specs=[pl.BlockSpec((None, None, bm, c), lambda k, i, pr: (k, pr[0], i, 0))] * n + [spec] * n, out_specs=[spec] * n),
        compiler_params=_params(("parallel", "parallel")), name=name)(place, *gs, *theirs)


def _chip_sum(name, place, parts, slots):
    n, ns = len(parts), len(slots[0])
    _, m, c = parts[0].shape
    bm = _row_block(m, c)

    def body(place_ref, *refs):
        for t in range(n):
            acc = refs[t][...].astype(F32)
            for s_ref in refs[n + t * ns:n + (t + 1) * ns]:
                for i in range(s_ref.shape[0]):
                    acc = acc + s_ref[i].astype(F32)
            refs[n + n * ns + t][...] = acc

    half = pl.BlockSpec((bm, c), lambda i, pr: (i, 0))
    return pl.pallas_call(
        body, out_shape=[jax.ShapeDtypeStruct((m, c), F32)] * n,
        grid_spec=pltpu.PrefetchScalarGridSpec(
            num_scalar_prefetch=1, grid=(m // bm,),
            in_specs=[pl.BlockSpec((None, bm, c), lambda i, pr: (pr[1], i, 0))] * n
            + [pl.BlockSpec((s.shape[0], bm, c), lambda i, pr: (0, i, 0)) for group in slots for s in group],
            out_specs=[half] * n),
        compiler_params=_params(("parallel",)), name=name)(place, *parts, *[s for group in slots for s in group])


def _adamw(name, place, w, g_mine, g_other, m, v, layer=0, into=None):
    lead, rows, cols = w.shape
    bm = _row_block(rows // 2, cols, budget=768 * 1024)
    per_half = rows // 2 // bm
    c1 = 1.0 - ADAM_B1 ** ADAM_STEP
    c2 = 1.0 - ADAM_B2 ** ADAM_STEP

    def body(place_ref, w_ref, ga_ref, gb_ref, m_ref, v_ref, *rest):
        go_ref, d_ref, mo_ref, vo_ref = rest[-4:]
        g = jnp.where(pl.program_id(0) // per_half == place_ref[0], ga_ref[...], gb_ref[...])
        mn = ADAM_B1 * m_ref[...] + (1.0 - ADAM_B1) * g
        vn = ADAM_B2 * v_ref[...] + (1.0 - ADAM_B2) * (g * g)
        go_ref[...] = g
        mo_ref[...] = mn
        vo_ref[...] = vn
        d_ref[...] = -ADAM_LR * ((mn / c1) / (jnp.sqrt(vn / c2) + ADAM_EPS) + ADAM_WD * w_ref[...])

    spec = pl.BlockSpec((None, bm, cols), lambda i, pr: (layer, i, 0))

    def half_spec(mine):
        def index(i, pr):
            first = (pr[0] == 0) == mine
            park = jnp.where(first, per_half - 1, 0)
            return jnp.where((i < per_half) == first, i % per_half, park), 0
        return pl.BlockSpec((bm, cols), index)
    sh = jax.ShapeDtypeStruct((lead, rows, cols), F32)
    prev = [] if into is None else list(into)
    return pl.pallas_call(
        body, out_shape=[sh] * 4, input_output_aliases={6 + k: k for k in range(len(prev))},
        grid_spec=pltpu.PrefetchScalarGridSpec(
            num_scalar_prefetch=1, grid=(rows // bm,),
            in_specs=[spec, half_spec(True), half_spec(False), spec, spec] + [pl.BlockSpec(memory_space=pl.ANY)] * len(prev),
            out_specs=[spec] * 4),
        compiler_params=_params(("parallel",)), name=name)(place, w, g_mine, g_other, m, v, *prev)


def _pack_small(norms, rpb, last=None):
    flat = jnp.concatenate([a.reshape(-1) for a in norms] + [rpb.reshape(-1)])
    flat = jnp.pad(flat, (0, SMALL_ROWS * 128 - flat.shape[0]))
    if last is not None:
        flat = jnp.concatenate([flat[:-1], last.reshape(1)])
    return flat.reshape(SMALL_ROWS, 128)


def _unpack_small(p):
    flat = p.reshape(-1)
    norms = [flat[i * 2 * DM:(i + 1) * 2 * DM].reshape(2, DM) for i in range(4)]
    rpb = flat[8 * DM:8 * DM + NH * 15 * 31].reshape(1, NH, 15, 31)
    return norms, rpb


FFN_NAMES = ("ffn_w_gate", "ffn_w_up", "ffn_w_down")
L0_FFN = tuple((n, 0) for n in FFN_NAMES)
L1_FFN = tuple((n, 1) for n in FFN_NAMES)
NA_KEYS = (("na_w_qkv", 0), ("na_w_o", 0))
DIL_KEYS = (("dil_w_qkv", 0), ("dil_w_o", 0))
ALL_PEERS, NEIGHBOURS, DIAGONAL = (0, 1, 2), (0, 1), (2,)


class _Exchange:
    GATHERS = {"na_bias_tiles": NA_KEYS, "l0_qkv": L0_FFN[:1], "na_fwd": L0_FFN[1:], "l0_ffn_fwd": DIL_KEYS[:1], "dil_fwd": L1_FFN + DIL_KEYS[1:]}
    PAIRS = {"l1_proj_bwd": L1_FFN, "l1_dh": DIL_KEYS, "l0_proj_bwd": L0_FFN}
    EXCHANGES = {"dil_bwd": [(k, ALL_PEERS) for k in L1_FFN],
                 "l0_ffn_bwd": [(DIL_KEYS[0], NEIGHBOURS), (DIL_KEYS[1], ALL_PEERS)],
                 "na_bwd": [(k, ALL_PEERS) for k in L0_FFN] + [(DIL_KEYS[0], DIAGONAL)],
                 "l0_dh": [(k, NEIGHBOURS) for k in NA_KEYS],
                 "l0_mix_pre_bwd": [(k, DIAGONAL) for k in NA_KEYS]}
    SHARES = {"l1_dwqkv": L1_FFN, "l0_dwqkv": L0_FFN + DIL_KEYS}

    def __init__(self, shards):
        self.chip = 2 * lax.axis_index("x") + lax.axis_index("y")
        self.place = jnp.stack([lax.axis_index("c"), self.chip]).astype(jnp.int32)
        self.own = {k: s.reshape(2, s.shape[0] // 2, s.shape[1]).astype(BF16) for k, s in shards.items()}
        self.gathered, self.mine, self.parts, self.slots, self.full, self.other = {}, {}, {}, {}, {}, {}

    def _take(self, keys, landed):
        for k, gw in zip(keys, landed):
            self.gathered[k] = lax.dynamic_update_slice(gw, self.own[k][None], (self.chip, 0, 0, 0))

    def _sum(self, items, landed):
        runs = []
        for (k, peers), s in zip(items, landed):
            got = self.slots.setdefault(k, {})
            got[peers] = s
            if sum(len(p) for p in got) == len(ALL_PEERS):
                like = (self.parts[k].shape, tuple(sorted(got)))
                if runs and runs[-1][0] == like:
                    runs[-1][1].append(k)
                else:
                    runs.append((like, [k]))
        for (_, split), ks in runs:
            sums = _chip_sum(f"chip_sum_{ks[0][0]}_{ks[0][1]}", self.place, [self.parts[k] for k in ks],
                             [[self.slots[k][p] for p in split] for k in ks])
            self.full.update(zip(ks, sums))

    def weight(self, key):
        g = self.gathered[key]
        return g.reshape(NCHIP, 2 * g.shape[2], g.shape[3])

    def _pair_sums(self, keys, theirs):
        runs = []
        for k, t in zip(keys, theirs):
            if runs and runs[-1][0][1].shape == t.shape:
                runs[-1].append((k, t))
            else:
                runs.append([(k, t)])
        for run in runs:
            ks = [k for k, _ in run]
            sums = _pair_sum(f"pair_sum_{ks[0][0]}_{ks[0][1]}", self.place, [self.mine[k] for k in ks], [t for _, t in run])
            self.parts.update(zip(ks, sums))

    def carry(self, tag):
        if tag in self.GATHERS:
            return _gather_copies([self.own[k] for k in self.GATHERS[tag]])
        if tag in self.PAIRS:
            return _pair_exchange_copies([self.mine[k] for k in self.PAIRS[tag]])
        if tag in self.EXCHANGES:
            return _chip_exchange_copies([(self.parts[k], peers) for k, peers in self.EXCHANGES[tag]])
        if tag in self.SHARES:
            return _pair_share_copies([self.full[k] for k in self.SHARES[tag]])
        return None

    def carried(self, tag, landed):
        if tag in self.GATHERS:
            self._take(self.GATHERS[tag], landed)
        elif tag in self.PAIRS:
            self._pair_sums(self.PAIRS[tag], landed)
        elif tag in self.EXCHANGES:
            self._sum(self.EXCHANGES[tag], landed)
        elif tag in self.SHARES:
            self.other.update(zip(self.SHARES[tag], landed))

    def grads(self, tag, dw):
        for k, g in dw.items():
            self.mine[k] = g.reshape(NCHIP, 2, -1, g.shape[-1])
        if tag == "l0_mix":
            keys = tuple(dw)
            self._pair_sums(keys, _run_carried("grad_pair_exchange_last", _pair_exchange_copies([self.mine[k] for k in keys])))

    def finish(self):
        rest = tuple(k for k in self.full if k not in self.other)
        self.other.update(zip(rest, _run_carried("grad_pair_share_last", _pair_share_copies([self.full[k] for k in rest]))))
        return {k: (self.full[k], self.other[k]) for k in self.full}


def kernel(x, norm_mix_pre, norm_mix_post, norm_ffn_pre, norm_ffn_post, na_w_qkv, na_w_o, na_rpb, dil_w_qkv, dil_w_o, ffn_w_gate, ffn_w_up, ffn_w_down, loss_target, m_norm_mix_pre, m_norm_mix_post, m_norm_ffn_pre, m_norm_ffn_post, m_na_w_qkv, m_na_w_o, m_na_rpb, m_dil_w_qkv, m_dil_w_o, m_ffn_w_gate, m_ffn_w_up, m_ffn_w_down, v_norm_mix_pre, v_norm_mix_post, v_norm_ffn_pre, v_norm_ffn_post, v_na_w_qkv, v_na_w_o, v_na_rpb, v_dil_w_qkv, v_dil_w_o, v_ffn_w_gate, v_ffn_w_up, v_ffn_w_down):
    tr = lambda a: jnp.swapaxes(a, 1, 2)
    weights = {"na_w_qkv": na_w_qkv, "na_w_o": na_w_o, "dil_w_qkv": dil_w_qkv, "dil_w_o": dil_w_o,
               "ffn_w_gate": tr(ffn_w_gate), "ffn_w_up": tr(ffn_w_up), "ffn_w_down": ffn_w_down}
    m_in = {"na_w_qkv": m_na_w_qkv, "na_w_o": m_na_w_o, "dil_w_qkv": m_dil_w_qkv, "dil_w_o": m_dil_w_o,
            "ffn_w_gate": tr(m_ffn_w_gate), "ffn_w_up": tr(m_ffn_w_up), "ffn_w_down": m_ffn_w_down}
    v_in = {"na_w_qkv": v_na_w_qkv, "na_w_o": v_na_w_o, "dil_w_qkv": v_dil_w_qkv, "dil_w_o": v_dil_w_o,
            "ffn_w_gate": tr(v_ffn_w_gate), "ffn_w_up": tr(v_ffn_w_up), "ffn_w_down": v_ffn_w_down}

    ex = _Exchange({(n, l): weights[n][l] for n in weights for l in range(weights[n].shape[0])})
    norms = (norm_mix_pre, norm_mix_post, norm_ffn_pre, norm_ffn_post)
    loss_row, dx, dnorms, d_rpb = _local_step(x[0], loss_target[0], norms, na_rpb[0], ex)
    full = ex.finish()
    small = _allreduce_small(_pack_small(dnorms, d_rpb, last=loss_row[0, 0]))
    loss = small[SMALL_ROWS - 1, 127]

    out_g, out_d, out_m, out_v = {}, {}, {}, {}
    for n in weights:
        res = None
        for l in range(weights[n].shape[0]):
            res = _adamw(f"adamw_{n}_{l}", ex.place, weights[n], *full[(n, l)], m_in[n], v_in[n], l, res)
        if n in ("ffn_w_gate", "ffn_w_up"):
            res = [tr(r) for r in res]
        out_g[n], out_d[n], out_m[n], out_v[n] = res
    sm_names = ("norm_mix_pre", "norm_mix_post", "norm_ffn_pre", "norm_ffn_post", "na_rpb")
    sm = _adamw("adamw_small", jnp.zeros((2,), jnp.int32), _pack_small(norms, na_rpb)[None], small[:SMALL_ROWS // 2], small[SMALL_ROWS // 2:],
                _pack_small((m_norm_mix_pre, m_norm_mix_post, m_norm_ffn_pre, m_norm_ffn_post), m_na_rpb)[None],
                _pack_small((v_norm_mix_pre, v_norm_mix_post, v_norm_ffn_pre, v_norm_ffn_post), v_na_rpb)[None])
    for res, dst in zip(sm, (out_g, out_d, out_m, out_v)):
        ns, rp = _unpack_small(res)
        for n, a in zip(sm_names, ns + [rp]):
            dst[n] = a

    order = ("norm_mix_pre", "norm_mix_post", "norm_ffn_pre", "norm_ffn_post", "na_w_qkv", "na_w_o", "na_rpb", "dil_w_qkv", "dil_w_o",
             "ffn_w_gate", "ffn_w_up", "ffn_w_down")
    return (loss, dx[None], *[out_g[n] for n in order], *[out_d[n] for n in order], *[out_m[n] for n in order], *[out_v[n] for n in order])
```

```python
import functools

import numpy as np
import jax
import jax.numpy as jnp
from jax import lax
from jax.experimental import pallas as pl
from jax.experimental.pallas import tpu as pltpu

F32 = jnp.float32
BF16 = jnp.bfloat16

SEQ = 2048
DM = 1024
NH = 16
HD = 64
DFF = 2816
NCHIP = 4
FSH = DFF // NCHIP
GRID_W = 64
NA_QROWS = 4
NA_QB = NA_QROWS * GRID_W
NA_WROWS = 12
NA_WIN = NA_WROWS * GRID_W
DIL = (1, 4, 16)
DIL_QB = 256
DIL_WIN = DIL_QB + 128
DIL_RADIUS = 64
RMS_EPS = 1e-6
NEG = -1e30
QSCALE = HD ** -0.5
CH = 256
MESH = pl.DeviceIdType.MESH

ADAM_LR, ADAM_B1, ADAM_B2, ADAM_EPS, ADAM_WD, ADAM_STEP = 0.001, 0.9, 0.999, 1e-08, 0.01, 10

VMEM_LIMIT = 56 * 1024 * 1024

_NN = (((1,), (0,)), ((), ()))
_NT = (((1,), (1,)), ((), ()))
_TN = (((0,), (0,)), ((), ()))


def _params(sem):
    return pltpu.CompilerParams(dimension_semantics=sem, vmem_limit_bytes=VMEM_LIMIT)


def _matmul(name, pairs, grid, out_shape, out_spec, acc_shape, carrying=False, carry=None):
    nk = grid[-1]
    npair = len(pairs)
    n_in = 2 * npair

    def body(*refs):
        ins, o_ref = refs[:2 * npair], refs[n_in]
        part = None
        for p in range(npair):
            d = lax.dot_general(ins[2 * p][...].astype(BF16), ins[2 * p + 1][...].astype(BF16), pairs[p][4],
                                preferred_element_type=F32)
            part = d if part is None else part + d
        if nk == 1:
            o_ref[...] = part.astype(o_ref.dtype)
        else:
            acc_ref = refs[n_in + 1]
            kk = pl.program_id(len(grid) - 1)

            @pl.when(kk == 0)
            def _():
                acc_ref[...] = part

            @pl.when(kk > 0)
            def _():
                acc_ref[...] += part

            @pl.when(kk == nk - 1)
            def _():
                o_ref[...] = acc_ref[...].astype(o_ref.dtype)

    ops, specs = [], []
    for a, a_spec, b, b_spec, _ in pairs:
        ops += [a, b]
        specs += [a_spec, b_spec]
    (out,), sent = _carrier_call(name, body, grid, specs, [out_spec], [out_shape], [] if nk == 1 else [pltpu.VMEM(acc_shape, F32)], ops, carry)
    return (out, sent) if carrying else out


def _qkv_fwd(name, h_all, w4, carry):
    g_n = h_all.shape[0]
    per = w4.shape[2] // CH
    return _matmul(
        name, [(h_all, pl.BlockSpec((None, SEQ, DM), lambda g, q, k: (g, 0, 0)),
                w4, pl.BlockSpec((None, DM, CH), lambda g, q, k: ((g * 12 + q) // per, 0, (g * 12 + q) % per)), _NN)],
        (g_n, 12, 1), jax.ShapeDtypeStruct((g_n, SEQ, 3 * DM), BF16),
        pl.BlockSpec((None, SEQ, CH), lambda g, q, k: (g, 0, q)), None, carrying=True, carry=carry)


def _qkv_bwd_dh(name, dqkv, w4, carry):
    g_n = dqkv.shape[0]
    per = w4.shape[2] // CH
    tm = SEQ

    def pair(cb):
        chunk = lambda g, t: g * 12 + t * 4 + cb
        return (dqkv, pl.BlockSpec((None, None, tm, CH), lambda g, i, t: (g, t, i, cb)),
                w4, pl.BlockSpec((None, DM, CH), lambda g, i, t: (chunk(g, t) // per, 0, chunk(g, t) % per)), _NT)

    return _matmul(name, [pair(cb) for cb in range(4)], (g_n, SEQ // tm, 3), jax.ShapeDtypeStruct((g_n, SEQ, DM), F32),
                   pl.BlockSpec((None, tm, DM), lambda g, i, t: (g, i, 0)), (tm, DM), carrying=True, carry=carry)


def _qkv_bwd_dw(name, ht_all, dqkv, shard_cols, carry):
    g_n = dqkv.shape[0]
    per = shard_cols // CH
    return _matmul(
        name, [(ht_all, pl.BlockSpec((None, DM, SEQ), lambda qq, k: (qq // 12, 0, 0)),
                dqkv, pl.BlockSpec((None, None, SEQ, CH), lambda qq, k: (qq // 12, (qq % 12) // 4, 0, qq % 4)), _NN)],
        (g_n * 12, 1), jax.ShapeDtypeStruct((NCHIP, DM, shard_cols), BF16),
        pl.BlockSpec((None, DM, CH), lambda qq, k: (qq // per, 0, qq % per)), None, carrying=True, carry=carry)


def _proj_fwd(name, o, wo, x, g):
    tm = 512

    def body(o_ref, w_ref, x_ref, g_ref, xn_ref, u_ref):
        u = jnp.dot(o_ref[...], w_ref[...], preferred_element_type=F32)
        u_ref[...] = u
        r = lax.rsqrt(jnp.mean(u * u, axis=-1, keepdims=True) + RMS_EPS)
        xn_ref[...] = x_ref[...] + u * r * g_ref[...]

    rows = pl.BlockSpec((tm, DM), lambda i: (i, 0))
    sh = jax.ShapeDtypeStruct((SEQ, DM), F32)
    return pl.pallas_call(
        body, grid=(SEQ // tm,), in_specs=[rows, pl.BlockSpec((DM, DM), lambda i: (0, 0)), rows, pl.BlockSpec((1, DM), lambda i: (0, 0))],
        out_specs=[rows, rows], out_shape=[sh, sh], compiler_params=_params(("parallel",)), name=name)(o, wo, x, g)


def _proj_bwd(name, dy, u, g, wo, dtype, carry):
    tm = 512

    def body(dy_ref, u_ref, g_ref, w_ref, do_ref, du_ref, dg_ref):
        dy = dy_ref[...]
        u = u_ref[...]
        r = lax.rsqrt(jnp.mean(u * u, axis=-1, keepdims=True) + RMS_EPS)
        yh = u * r
        t = dy * g_ref[...]
        du = (r * (t - yh * jnp.mean(t * yh, axis=-1, keepdims=True))).astype(BF16)
        du_ref[...] = du
        do_ref[...] = lax.dot_general(du, w_ref[...], _NT, preferred_element_type=F32).astype(do_ref.dtype)

        @pl.when(pl.program_id(0) == 0)
        def _():
            dg_ref[...] = jnp.zeros_like(dg_ref)

        dg_ref[...] += jnp.sum(dy * yh, axis=0, keepdims=True)

    rows = pl.BlockSpec((tm, DM), lambda i: (i, 0))
    vec = pl.BlockSpec((1, DM), lambda i: (0, 0))
    return _carrier_call(
        name, body, (SEQ // tm,), [rows, rows, vec, pl.BlockSpec((DM, DM), lambda i: (0, 0))], [rows, rows, vec],
        [jax.ShapeDtypeStruct((SEQ, DM), dtype), jax.ShapeDtypeStruct((SEQ, DM), BF16), jax.ShapeDtypeStruct((1, DM), F32)],
        [], (dy, u, g, wo), carry)


def _proj_bwd_dw(name, o, du):
    tn = 512
    return _matmul(
        name, [(o, pl.BlockSpec((SEQ, DM), lambda j, k: (0, 0)), du, pl.BlockSpec((SEQ, tn), lambda j, k: (0, j)), _TN)],
        (DM // tn, 1), jax.ShapeDtypeStruct((DM, DM), BF16), pl.BlockSpec((DM, tn), lambda j, k: (0, j)), None)


def _ffn_wspec(index_map):
    return pl.BlockSpec((None, FSH, DM), index_map)


def _ffn_bwd_dw(name, a4, b):
    return _matmul(
        name, [(a4, pl.BlockSpec((None, SEQ, FSH), lambda s, k: (s, 0, 0)), b, pl.BlockSpec((SEQ, DM), lambda s, k: (0, 0)), _TN)],
        (NCHIP, 1), jax.ShapeDtypeStruct((NCHIP, FSH, DM), BF16), _ffn_wspec(lambda s, k: (s, 0, 0)), None)


ROWS = 256


def _row_spec():
    return pl.BlockSpec((ROWS, DM), lambda i: (i, 0))


def _vec_spec():
    return pl.BlockSpec((1, DM), lambda i: (0, 0))


def _rms_fwd(name, x, g, dtype=BF16):
    def body(x_ref, g_ref, o_ref):
        x = x_ref[...]
        r = lax.rsqrt(jnp.mean(x * x, axis=-1, keepdims=True) + RMS_EPS)
        o_ref[...] = (x * r * g_ref[...]).astype(o_ref.dtype)

    return pl.pallas_call(body, grid=(SEQ // ROWS,), in_specs=[_row_spec(), _vec_spec()], out_specs=_row_spec(),
                          out_shape=jax.ShapeDtypeStruct((SEQ, DM), dtype), compiler_params=_params(("parallel",)), name=name)(x, g)


def _rms_fwd_both(name, x, g):
    def body(x_ref, g_ref, o_ref, t_ref):
        x = x_ref[...]
        r = lax.rsqrt(jnp.mean(x * x, axis=-1, keepdims=True) + RMS_EPS)
        h = x * r * g_ref[...]
        o_ref[...] = h.astype(o_ref.dtype)
        t_ref[...] = h.T.astype(t_ref.dtype)

    return pl.pallas_call(
        body, grid=(SEQ // ROWS,), in_specs=[_row_spec(), _vec_spec()], out_specs=[_row_spec(), pl.BlockSpec((DM, ROWS), lambda i: (0, i))],
        out_shape=[jax.ShapeDtypeStruct((SEQ, DM), BF16), jax.ShapeDtypeStruct((DM, SEQ), BF16)],
        compiler_params=_params(("parallel",)), name=name)(x, g)


def _norm_bwd(name, dy, u, g, res, carry):
    def body(dy_ref, u_ref, g_ref, res_ref, du_ref, dg_ref):
        dy = dy_ref[...]
        u = u_ref[...]
        r = lax.rsqrt(jnp.mean(u * u, axis=-1, keepdims=True) + RMS_EPS)
        yh = u * r
        t = dy * g_ref[...]
        du_ref[...] = r * (t - yh * jnp.mean(t * yh, axis=-1, keepdims=True)) + res_ref[...]

        @pl.when(pl.program_id(0) == 0)
        def _():
            dg_ref[...] = jnp.zeros_like(dg_ref)

        dg_ref[...] += jnp.sum(dy * yh, axis=0, keepdims=True)

    return _carrier_call(
        name, body, (SEQ // ROWS,), [_row_spec(), _row_spec(), _vec_spec(), _row_spec()], [_row_spec(), _vec_spec()],
        [jax.ShapeDtypeStruct((SEQ, DM), F32), jax.ShapeDtypeStruct((1, DM), F32)], [], (dy, u, g, res), carry)


def _loss_grad(name, y, t):
    def body(y_ref, t_ref, dy_ref, l_ref):
        e = y_ref[...] - t_ref[...]
        dy_ref[...] = e * (1.0 / DM)

        @pl.when(pl.program_id(0) == 0)
        def _():
            l_ref[...] = jnp.zeros_like(l_ref)

        l_ref[...] += jnp.sum(e * e) * (0.5 / DM)

    return pl.pallas_call(
        body, grid=(SEQ // ROWS,), in_specs=[_row_spec(), _row_spec()],
        out_specs=[_row_spec(), pl.BlockSpec((1, 128), lambda i: (0, 0))],
        out_shape=[jax.ShapeDtypeStruct((SEQ, DM), F32), jax.ShapeDtypeStruct((1, 128), F32)],
        compiler_params=_params(("arbitrary",)), name=name)(y, t)


HBM_SPEC = pl.BlockSpec(memory_space=pltpu.HBM)


class _Carried:
    def __init__(self, ins, out_shapes, n_sems, issue, drain):
        self.ins, self.out_shapes, self.n_sems, self.issue, self.drain = list(ins), list(out_shapes), tuple(n_sems), issue, drain


def _carrier_call(name, body, grid, in_specs, out_specs, out_shape, scratch_shapes, operands, carry):
    n_in, n_out, n_scr = len(in_specs), len(out_specs), len(scratch_shapes)
    if carry is None:
        res = pl.pallas_call(body, grid=grid, in_specs=in_specs, out_specs=out_specs, out_shape=out_shape, scratch_shapes=scratch_shapes,
                             compiler_params=_params(("arbitrary",) * len(grid)), name=name)(*operands)
        return list(res), []
    ci, co = len(carry.ins), len(carry.out_shapes)

    def wrapped(*refs):
        ins, cins = refs[:n_in], refs[n_in:n_in + ci]
        outs, couts = refs[n_in + ci:n_in + ci + n_out], refs[n_in + ci + n_out:n_in + ci + n_out + co]
        scr, sems = refs[n_in + ci + n_out + co:n_in + ci + n_out + co + n_scr], refs[n_in + ci + n_out + co + n_scr:]
        first = functools.reduce(jnp.logical_and, [pl.program_id(a) == 0 for a in range(len(grid))])
        last = functools.reduce(jnp.logical_and, [pl.program_id(a) == grid[a] - 1 for a in range(len(grid))])

        @pl.when(first)
        def _():
            carry.issue(cins, couts, sems)

        body(*ins, *outs, *scr)

        @pl.when(last)
        def _():
            carry.drain(cins, couts, sems)

    res = pl.pallas_call(
        wrapped, grid=grid, in_specs=list(in_specs) + [HBM_SPEC] * ci, out_specs=list(out_specs) + [HBM_SPEC] * co,
        out_shape=list(out_shape) + carry.out_shapes,
        scratch_shapes=list(scratch_shapes) + [pltpu.SemaphoreType.DMA((k,)) for k in carry.n_sems],
        compiler_params=pltpu.CompilerParams(dimension_semantics=("arbitrary",) * len(grid), vmem_limit_bytes=VMEM_LIMIT, has_side_effects=True),
        name=name)(*operands, *carry.ins)
    return list(res[:n_out]), list(res[n_out:])


def _run_carried(name, carry):
    def body(*refs):
        ci, co = len(carry.ins), len(carry.out_shapes)
        carry.issue(refs[:ci], refs[ci:ci + co], refs[ci + co:])
        carry.drain(refs[:ci], refs[ci:ci + co], refs[ci + co:])

    return pl.pallas_call(
        body, in_specs=[HBM_SPEC] * len(carry.ins), out_specs=[HBM_SPEC] * len(carry.out_shapes), out_shape=carry.out_shapes,
        scratch_shapes=[pltpu.SemaphoreType.DMA((k,)) for k in carry.n_sems],
        compiler_params=pltpu.CompilerParams(has_side_effects=True), name=name)(*carry.ins)


NA_BLOCKS = SEQ // NA_QB
NA_ROWS_TOTAL = SEQ // GRID_W
NA_CLASSES = ((0, 0), (8, 4), (NA_ROWS_TOTAL - NA_QROWS, NA_ROWS_TOTAL - NA_WROWS))


def _na_pairs(i0, ws):
    out = []
    for qi in range(NA_QROWS):
        i = i0 + qi
        rs = min(max(i - 4, 0), NA_ROWS_TOTAL - 8)
        for kr in range(NA_WROWS):
            r = ws + kr
            if rs <= r < rs + 8:
                out.append((qi, kr, r - i + 7))
    return out


def _diag_onehot():
    qc, kc = np.meshgrid(np.arange(GRID_W), np.arange(GRID_W), indexing="ij")
    e = np.zeros((GRID_W * GRID_W, 128), np.float32)
    j = (kc - qc + 15).reshape(-1)
    ok = (j >= 0) & (j <= 30)
    e[np.arange(GRID_W * GRID_W)[ok], j[ok]] = 1.0
    return jnp.asarray(e)


def _rpb_expand(rpb):
    r2 = jnp.pad(rpb.reshape(NH * 15, 31), ((0, 0), (0, 128 - 31)))

    def body(r_ref, e_ref, o_ref):
        o_ref[...] = lax.dot_general(r_ref[...], e_ref[...], _NT, preferred_element_type=F32, precision=lax.Precision.HIGHEST)

    out = pl.pallas_call(body, out_shape=jax.ShapeDtypeStruct((NH * 15, GRID_W * GRID_W), F32), name="rpb_expand",
                         compiler_params=pltpu.CompilerParams(vmem_limit_bytes=VMEM_LIMIT))(r2, _diag_onehot())
    return out.reshape(NH, 15, GRID_W, GRID_W)


def _na_bias_tiles(rpb, carry):
    def body(b_ref, o_ref):
        qc = lax.broadcasted_iota(jnp.int32, (GRID_W, GRID_W), 0)
        kc = lax.broadcasted_iota(jnp.int32, (GRID_W, GRID_W), 1)
        first = jnp.clip(qc - 8, 0, GRID_W - 16)
        in_window = (kc >= first) & (kc < first + 16)
        neg = jnp.full((GRID_W, GRID_W), NEG, F32)
        for cls, (i0, ws) in enumerate(NA_CLASSES):
            @pl.when(pl.program_id(0) == cls)
            def _(i0=i0, ws=ws):
                pairs = {(qi, kr): dr for qi, kr, dr in _na_pairs(i0, ws)}
                masked = {dr: jnp.where(in_window, b_ref[dr], NEG) for dr in sorted(set(pairs.values()))}
                for qi in range(NA_QROWS):
                    for k2 in range(NA_WROWS // 2):
                        blocks = [masked[pairs[(qi, kr)]] if (qi, kr) in pairs else neg for kr in (2 * k2, 2 * k2 + 1)]
                        o_ref[qi * GRID_W:(qi + 1) * GRID_W, k2 * 128:(k2 + 1) * 128] = jnp.concatenate(blocks, axis=1)

    (tiles,), sent = _carrier_call(
        "na_bias_tiles", body, (3, NH), [pl.BlockSpec((None, 15, GRID_W, GRID_W), lambda c, h: (h, 0, 0, 0))],
        [pl.BlockSpec((None, None, NA_QB, NA_WIN), lambda c, h: (c, h, 0, 0))], [jax.ShapeDtypeStruct((3, NH, NA_QB, NA_WIN), F32)],
        [], (_rpb_expand(rpb),), carry)
    return tiles, sent


def _na_cls(b):
    return jnp.where(b == 0, 0, jnp.where(b == NA_BLOCKS - 1, 2, 1))


def _na_start(b):
    return pl.multiple_of(jnp.clip(b * NA_QROWS - 4, 0, NA_ROWS_TOTAL - NA_WROWS) * GRID_W, GRID_W)


NA_FWD_HPS = 8
NA_BWD_HPS = 4


def _na_in_specs(hps):
    lw = hps * HD
    nlw = DM // lw
    return [pl.BlockSpec((NA_QB, lw), lambda hp, b: (b, hp)),
            pl.BlockSpec((SEQ, lw), lambda hp, b: (0, nlw + hp)),
            pl.BlockSpec((SEQ, lw), lambda hp, b: (0, 2 * nlw + hp)),
            pl.BlockSpec((None, hps, NA_QB, NA_WIN), lambda hp, b: (_na_cls(b), hp, 0, 0))]


def _na_fwd(qkv, bias, carry):
    lw = NA_FWD_HPS * HD

    def body(q_ref, k_ref, v_ref, b_ref, o_ref):
        start = _na_start(pl.program_id(1))
        q = q_ref[...]
        kw = k_ref[pl.ds(start, NA_WIN), :]
        vw = v_ref[pl.ds(start, NA_WIN), :]
        outs = []
        for hh in range(NA_FWD_HPS):
            sl = slice(hh * HD, (hh + 1) * HD)
            s = lax.dot_general(q[:, sl] * QSCALE, kw[:, sl], _NT, preferred_element_type=F32) + b_ref[hh]
            p = jnp.exp(s - jnp.max(s, axis=-1, keepdims=True))
            l = jnp.sum(p, axis=-1, keepdims=True)
            outs.append(jnp.dot(p.astype(BF16), vw[:, sl], preferred_element_type=F32) / l)
        o_ref[...] = jnp.concatenate(outs, axis=1).astype(o_ref.dtype)

    (o,), sent = _carrier_call(
        "na_fwd", body, (NH // NA_FWD_HPS, NA_BLOCKS), _na_in_specs(NA_FWD_HPS), [pl.BlockSpec((NA_QB, lw), lambda hp, b: (b, hp))],
        [jax.ShapeDtypeStruct((SEQ, DM), BF16)], [], (qkv, qkv, qkv, bias), carry)
    return o, sent


def _na_bwd(qkv, bias, do, carry):
    lw = NA_BWD_HPS * HD

    def body(q_ref, k_ref, v_ref, b_ref, do_ref, dqkv_ref, z_ref, dk_acc, dv_acc):
        blk = pl.program_id(1)

        @pl.when(blk == 0)
        def _():
            dk_acc[...] = jnp.zeros_like(dk_acc)
            dv_acc[...] = jnp.zeros_like(dv_acc)
            z_ref[...] = jnp.zeros_like(z_ref)

        start = _na_start(blk)
        q = q_ref[...]
        do = do_ref[...]
        kw = k_ref[pl.ds(start, NA_WIN), :]
        vw = v_ref[pl.ds(start, NA_WIN), :]
        dqs, dks, dvs, dss = [], [], [], []
        for hh in range(NA_BWD_HPS):
            sl = slice(hh * HD, (hh + 1) * HD)
            qh = q[:, sl] * QSCALE
            s = lax.dot_general(qh, kw[:, sl], _NT, preferred_element_type=F32) + b_ref[hh]
            p = jnp.exp(s - jnp.max(s, axis=-1, keepdims=True))
            p = p / jnp.sum(p, axis=-1, keepdims=True)
            dp = lax.dot_general(do[:, sl], vw[:, sl], _NT, preferred_element_type=F32)
            ds = p * (dp - jnp.sum(p * dp, axis=-1, keepdims=True))
            dsb = ds.astype(BF16)
            dqs.append(jnp.dot(dsb, kw[:, sl], preferred_element_type=F32) * QSCALE)
            dks.append(lax.dot_general(qh, dsb, _TN, preferred_element_type=F32).T)
            dvs.append(lax.dot_general(do[:, sl], p.astype(BF16), _TN, preferred_element_type=F32).T)
            dss.append(ds)
        for cls, (i0, ws) in enumerate(NA_CLASSES):
            @pl.when(_na_cls(blk) == cls)
            def _(i0=i0, ws=ws):
                for hh, ds in enumerate(dss):
                    for qi, kr, dr in _na_pairs(i0, ws):
                        z_ref[hh, dr * GRID_W:(dr + 1) * GRID_W, :] += ds[qi * GRID_W:(qi + 1) * GRID_W, kr * GRID_W:(kr + 1) * GRID_W]
        dqkv_ref[0, pl.ds(pl.multiple_of(blk * NA_QB, NA_QB), NA_QB), :] = jnp.concatenate(dqs, axis=1).astype(dqkv_ref.dtype)
        dk_acc[pl.ds(start, NA_WIN), :] += jnp.concatenate(dks, axis=1)
        dv_acc[pl.ds(start, NA_WIN), :] += jnp.concatenate(dvs, axis=1)

        @pl.when(blk == NA_BLOCKS - 1)
        def _():
            dqkv_ref[1] = dk_acc[...].astype(dqkv_ref.dtype)
            dqkv_ref[2] = dv_acc[...].astype(dqkv_ref.dtype)

    (dqkv, z), sent = _carrier_call(
        "na_bwd", body, (NH // NA_BWD_HPS, NA_BLOCKS),
        _na_in_specs(NA_BWD_HPS) + [pl.BlockSpec((NA_QB, lw), lambda hp, b: (b, hp))],
        [pl.BlockSpec((3, SEQ, lw), lambda hp, b: (0, 0, hp)), pl.BlockSpec((NA_BWD_HPS, 15 * GRID_W, GRID_W), lambda hp, b: (hp, 0, 0))],
        [jax.ShapeDtypeStruct((3, SEQ, DM), BF16), jax.ShapeDtypeStruct((NH, 15 * GRID_W, GRID_W), F32)],
        [pltpu.VMEM((SEQ, lw), F32), pltpu.VMEM((SEQ, lw), F32)], (qkv, qkv, qkv, bias, do), carry)
    return dqkv, z, sent


def _rpb_grad(z):
    z2 = z.reshape(NH * 15, GRID_W * GRID_W)

    def body(z_ref, e_ref, o_ref):
        o_ref[...] = jnp.dot(z_ref[...], e_ref[...], preferred_element_type=F32, precision=lax.Precision.HIGHEST)

    out = pl.pallas_call(body, out_shape=jax.ShapeDtypeStruct((NH * 15, 128), F32), name="rpb_grad",
                         compiler_params=pltpu.CompilerParams(vmem_limit_bytes=VMEM_LIMIT))(z2, _diag_onehot())
    return out[:, :31].reshape(NH, 15, 31)


DIL_BLOCKS = SEQ // DIL_QB
DIL_HPS = 8
DIL_LW = DIL_HPS * HD
DIL_NLW = DM // DIL_LW


COLS = 128


def _col_spec():
    return pl.BlockSpec((SEQ, COLS), lambda j: (0, j))


def _grp_spec():
    return pl.BlockSpec((3, SEQ, COLS), lambda j: (0, 0, j))


def _store_group_order(dst_ref, src_ref):
    for g, d in enumerate(DIL):
        n = SEQ // d
        for r in range(d):
            dst_ref[g, r * n:(r + 1) * n, :] = src_ref[pl.ds(r, n, stride=d), :].astype(dst_ref.dtype)


def _store_token_order(dst_ref, src_ref, g):
    d = DIL[g]
    n = SEQ // d
    for r in range(d):
        dst_ref[pl.ds(r, n, stride=d), :] = src_ref[g, r * n:(r + 1) * n, :]


def _to_groups(name, a):
    def body(a_ref, o_ref, t_ref):
        _store_group_order(o_ref, a_ref)
        for g in range(3):
            t_ref[g] = o_ref[g].astype(F32).T.astype(t_ref.dtype)

    return pl.pallas_call(
        body, grid=(DM // COLS,), in_specs=[_col_spec()], out_specs=[_grp_spec(), pl.BlockSpec((3, COLS, SEQ), lambda j: (0, j, 0))],
        out_shape=[jax.ShapeDtypeStruct((3, SEQ, DM), BF16), jax.ShapeDtypeStruct((3, DM, SEQ), BF16)],
        compiler_params=_params(("parallel",)), name=name)(a)


def _from_groups_sum(name, a):
    def body(a_ref, o_ref, t1, t2):
        _store_token_order(t1, a_ref, 1)
        _store_token_order(t2, a_ref, 2)
        o_ref[...] = (a_ref[0] + t1[...]) + t2[...]

    return pl.pallas_call(body, grid=(DM // COLS,), in_specs=[_grp_spec()], out_specs=_col_spec(),
                          out_shape=jax.ShapeDtypeStruct((SEQ, DM), F32), scratch_shapes=[pltpu.VMEM((SEQ, COLS), F32)] * 2,
                          compiler_params=_params(("parallel",)), name=name)(a)


def _dil_start(b):
    return pl.multiple_of(jnp.clip(b * DIL_QB - DIL_RADIUS, 0, SEQ - DIL_WIN), DIL_RADIUS)


def _dil_neg_dist(g, ii, jj):
    shift = 11 - 2 * g
    dist = jnp.abs(ii - jj)
    valid = (dist <= DIL_RADIUS) & (jnp.right_shift(ii, shift) == jnp.right_shift(jj, shift))
    return jnp.where(valid, -dist.astype(F32), NEG)


def _dil_in_specs():
    return [pl.BlockSpec(memory_space=pltpu.SMEM),
            pl.BlockSpec((None, DIL_QB, DIL_LW), lambda g, hp, b: (g, b, hp)),
            pl.BlockSpec((None, SEQ, DIL_LW), lambda g, hp, b: (g, 0, DIL_NLW + hp)),
            pl.BlockSpec((None, SEQ, DIL_LW), lambda g, hp, b: (g, 0, 2 * DIL_NLW + hp))]


def _dil_fwd(qkv, slopes, carry):
    def body(sl_ref, q_ref, k_ref, v_ref, o_ref, lse_ref):
        g, hp, b = pl.program_id(0), pl.program_id(1), pl.program_id(2)
        start = _dil_start(b)
        neg_dist = _dil_neg_dist(g, b * DIL_QB + lax.broadcasted_iota(jnp.int32, (DIL_QB, DIL_WIN), 0),
                                 start + lax.broadcasted_iota(jnp.int32, (DIL_QB, DIL_WIN), 1))
        dil = jnp.left_shift(1, 2 * g).astype(F32)
        q = q_ref[...]
        kw = k_ref[pl.ds(start, DIL_WIN), :]
        vw = v_ref[pl.ds(start, DIL_WIN), :]
        outs, lses = [], []
        for hh in range(DIL_HPS):
            sl = slice(hh * HD, (hh + 1) * HD)
            s = lax.dot_general(q[:, sl] * QSCALE, kw[:, sl], _NT, preferred_element_type=F32)
            s = s + (sl_ref[hp * DIL_HPS + hh] * dil) * neg_dist
            m = jnp.max(s, axis=-1, keepdims=True)
            p = jnp.exp(s - m)
            l = jnp.sum(p, axis=-1, keepdims=True)
            outs.append(jnp.dot(p.astype(BF16), vw[:, sl], preferred_element_type=F32) / l)
            lses.append(jnp.broadcast_to(m + jnp.log(l), (DIL_QB, HD)))
        o_ref[...] = jnp.concatenate(outs, axis=1)
        lse_ref[...] = jnp.concatenate(lses, axis=1)

    ospec = pl.BlockSpec((None, DIL_QB, DIL_LW), lambda g, hp, b: (g, b, hp))
    sh = jax.ShapeDtypeStruct((3, SEQ, DM), F32)
    (o, lse), sent = _carrier_call("dil_fwd", body, (3, DIL_NLW, DIL_BLOCKS), _dil_in_specs(), [ospec, ospec], [sh, sh], [],
                                   (slopes, qkv, qkv, qkv), carry)
    return o, lse, sent


def _dil_merge(o_all, lse_all):
    def body(o_ref, l_ref, out_ref, lse_ref, o1, o2, l1, l2):
        for g, (ot, lt) in ((1, (o1, l1)), (2, (o2, l2))):
            _store_token_order(ot, o_ref, g)
            _store_token_order(lt, l_ref, g)
        la, lb, lc = l_ref[0], l1[...], l2[...]
        m = jnp.maximum(jnp.maximum(la, lb), lc)
        wa, wb, wc = jnp.exp(la - m), jnp.exp(lb - m), jnp.exp(lc - m)
        sw = (wa + wb) + wc
        out_ref[...] = (((wa * o_ref[0] + wb * o1[...]) + wc * o2[...]) / sw).astype(out_ref.dtype)
        lse_ref[...] = m + jnp.log(sw)

    return pl.pallas_call(
        body, grid=(DM // COLS,), in_specs=[_grp_spec(), _grp_spec()], out_specs=[_col_spec(), _col_spec()],
        out_shape=[jax.ShapeDtypeStruct((SEQ, DM), BF16), jax.ShapeDtypeStruct((SEQ, DM), F32)],
        scratch_shapes=[pltpu.VMEM((SEQ, COLS), F32)] * 4, compiler_params=_params(("parallel",)), name="dil_merge")(o_all, lse_all)


def _dil_bwd_prep(do, o, lse):
    heads = COLS // HD

    def body(do_ref, o_ref, lse_ref, dog_ref, ddr_ref, lser_ref, dd, grp):
        prod = do_ref[...] * o_ref[...].astype(F32)
        dd[...] = jnp.concatenate(
            [jnp.broadcast_to(jnp.sum(prod[:, h * HD:(h + 1) * HD], axis=-1, keepdims=True), (SEQ, HD)) for h in range(heads)], axis=1)
        _store_group_order(dog_ref, do_ref)
        for src, dst in ((dd, ddr_ref), (lse_ref, lser_ref)):
            _store_group_order(grp, src)
            for g in range(3):
                t = grp[g].T
                for h in range(heads):
                    dst[g, h] = t[h * HD:h * HD + 8, :]

    rows = jax.ShapeDtypeStruct((3, NH, 8, SEQ), F32)
    rspec = pl.BlockSpec((3, heads, 8, SEQ), lambda j: (0, j, 0, 0))
    return pl.pallas_call(
        body, grid=(DM // COLS,), in_specs=[_col_spec()] * 3, out_specs=[_grp_spec(), rspec, rspec],
        out_shape=[jax.ShapeDtypeStruct((3, SEQ, DM), BF16), rows, rows],
        scratch_shapes=[pltpu.VMEM((SEQ, COLS), F32), pltpu.VMEM((3, SEQ, COLS), F32)],
        compiler_params=_params(("parallel",)), name="dil_bwd_prep")(do, o, lse)


def _dil_bwd(qkv, do, dd, lse, slopes, carry):
    def body(sl_ref, q_ref, k_ref, v_ref, do_ref, dd_ref, lse_ref, dqkv_ref, dk_acc, dv_acc):
        g, hp, b = pl.program_id(0), pl.program_id(1), pl.program_id(2)

        @pl.when(b == 0)
        def _():
            dk_acc[...] = jnp.zeros_like(dk_acc)
            dv_acc[...] = jnp.zeros_like(dv_acc)

        start = _dil_start(b)
        neg_dist = _dil_neg_dist(g, b * DIL_QB + lax.broadcasted_iota(jnp.int32, (DIL_WIN, DIL_QB), 1),
                                 start + lax.broadcasted_iota(jnp.int32, (DIL_WIN, DIL_QB), 0))
        dil = jnp.left_shift(1, 2 * g).astype(F32)
        q = q_ref[...]
        do = do_ref[...]
        kw = k_ref[pl.ds(start, DIL_WIN), :]
        vw = v_ref[pl.ds(start, DIL_WIN), :]
        dqs, dks, dvs = [], [], []
        for hh in range(DIL_HPS):
            sl = slice(hh * HD, (hh + 1) * HD)
            qh = q[:, sl] * QSCALE
            st = lax.dot_general(kw[:, sl], qh, _NT, preferred_element_type=F32)
            st = st + (sl_ref[hp * DIL_HPS + hh] * dil) * neg_dist
            pt = jnp.exp(st - lse_ref[hh, 0:1, :])
            dpt = lax.dot_general(vw[:, sl], do[:, sl], _NT, preferred_element_type=F32)
            dst = (pt * (dpt - dd_ref[hh, 0:1, :])).astype(BF16)
            dqs.append(lax.dot_general(kw[:, sl], dst, _TN, preferred_element_type=F32).T * QSCALE)
            dks.append(jnp.dot(dst, qh, preferred_element_type=F32))
            dvs.append(jnp.dot(pt.astype(BF16), do[:, sl], preferred_element_type=F32))
        dqkv_ref[0, pl.ds(pl.multiple_of(b * DIL_QB, DIL_QB), DIL_QB), :] = jnp.concatenate(dqs, axis=1).astype(dqkv_ref.dtype)
        dk_acc[pl.ds(start, DIL_WIN), :] += jnp.concatenate(dks, axis=1)
        dv_acc[pl.ds(start, DIL_WIN), :] += jnp.concatenate(dvs, axis=1)

        @pl.when(b == DIL_BLOCKS - 1)
        def _():
            dqkv_ref[1] = dk_acc[...].astype(dqkv_ref.dtype)
            dqkv_ref[2] = dv_acc[...].astype(dqkv_ref.dtype)

    qspec = pl.BlockSpec((None, DIL_QB, DIL_LW), lambda g, hp, b: (g, b, hp))
    rspec = pl.BlockSpec((None, DIL_HPS, 8, DIL_QB), lambda g, hp, b: (g, hp, 0, b))
    (dqkv,), sent = _carrier_call(
        "dil_bwd", body, (3, DIL_NLW, DIL_BLOCKS), _dil_in_specs() + [qspec, rspec, rspec],
        [pl.BlockSpec((None, 3, SEQ, DIL_LW), lambda g, hp, b: (g, 0, 0, hp))], [jax.ShapeDtypeStruct((3, 3, SEQ, DM), BF16)],
        [pltpu.VMEM((SEQ, DIL_LW), F32), pltpu.VMEM((SEQ, DIL_LW), F32)], (slopes, qkv, qkv, qkv, do, dd, lse), carry)
    return dqkv, sent


def _ffn_fwd(name, x, g_pre, g_post, wgt4, wut4, wd4, carry):
    tm = 512

    def body(x_ref, gpre_ref, gpost_ref, wg_ref, wu_ref, wd_ref, xn_ref, h_ref, gate_ref, up_ref, u_ref, acc):
        s = pl.program_id(1)

        @pl.when(s == 0)
        def _():
            x = x_ref[...]
            r = lax.rsqrt(jnp.mean(x * x, axis=-1, keepdims=True) + RMS_EPS)
            h_ref[...] = (x * r * gpre_ref[...]).astype(h_ref.dtype)

        h = h_ref[...]
        gate = lax.dot_general(h, wg_ref[...], _NT, preferred_element_type=F32).astype(BF16)
        up = lax.dot_general(h, wu_ref[...], _NT, preferred_element_type=F32).astype(BF16)
        gate_ref[...] = gate
        up_ref[...] = up
        gf = gate.astype(F32)
        act = (gf * jax.nn.sigmoid(gf) * up.astype(F32)).astype(BF16)
        part = jnp.dot(act, wd_ref[...], preferred_element_type=F32)

        @pl.when(s == 0)
        def _():
            acc[...] = part

        @pl.when(s > 0)
        def _():
            acc[...] += part

        @pl.when(s == NCHIP - 1)
        def _():
            u = acc[...]
            u_ref[...] = u
            r = lax.rsqrt(jnp.mean(u * u, axis=-1, keepdims=True) + RMS_EPS)
            xn_ref[...] = x_ref[...] + u * r * gpost_ref[...]

    rows = pl.BlockSpec((tm, DM), lambda i, s: (i, 0))
    vec = pl.BlockSpec((1, DM), lambda i, s: (0, 0))
    wspec = _ffn_wspec(lambda i, s: (s, 0, 0))
    mid = pl.BlockSpec((None, tm, FSH), lambda i, s: (s, i, 0))
    outs, sent = _carrier_call(
        name, body, (SEQ // tm, NCHIP), [rows, vec, vec, wspec, wspec, wspec], [rows, rows, mid, mid, rows],
        [jax.ShapeDtypeStruct((SEQ, DM), F32), jax.ShapeDtypeStruct((SEQ, DM), BF16), jax.ShapeDtypeStruct((NCHIP, SEQ, FSH), BF16),
         jax.ShapeDtypeStruct((NCHIP, SEQ, FSH), BF16), jax.ShapeDtypeStruct((SEQ, DM), F32)],
        [pltpu.VMEM((tm, DM), F32)], (x, g_pre, g_post, wgt4, wut4, wd4), carry)
    return outs, sent


def _ffn_block(layer, x, g_pre, g_post, ex):
    tag = f"l{layer}_ffn_fwd"
    (x_new, h, gate, up, u), sent = _ffn_fwd(tag, x, g_pre, g_post, ex.weight(("ffn_w_gate", layer)), ex.weight(("ffn_w_up", layer)),
                                             ex.weight(("ffn_w_down", layer)), ex.carry(tag))
    ex.carried(tag, sent)
    return x_new, (x, h, gate, up, u)


def _ffn_bwd(name, dx, x, gate, up, u, g_pre, g_post, wgt4, wut4, wd4, carry):
    tm = 512

    def body(dx_ref, x_ref, gate_ref, up_ref, u_ref, gpre_ref, gpost_ref, wg_ref, wu_ref, wd_ref,
             dxin_ref, du_ref, dgate_ref, dup_ref, act_ref, dgpre_ref, dgpost_ref, dh_acc):
        i, s = pl.program_id(0), pl.program_id(1)

        @pl.when((i == 0) & (s == 0))
        def _():
            dgpre_ref[...] = jnp.zeros_like(dgpre_ref)
            dgpost_ref[...] = jnp.zeros_like(dgpost_ref)

        @pl.when(s == 0)
        def _():
            dy = dx_ref[...]
            uu = u_ref[...]
            r = lax.rsqrt(jnp.mean(uu * uu, axis=-1, keepdims=True) + RMS_EPS)
            yh = uu * r
            t = dy * gpost_ref[...]
            du_ref[...] = (r * (t - yh * jnp.mean(t * yh, axis=-1, keepdims=True))).astype(du_ref.dtype)
            dgpost_ref[...] += jnp.sum(dy * yh, axis=0, keepdims=True)

        dact = lax.dot_general(du_ref[...], wd_ref[...], _NT, preferred_element_type=F32)
        g = gate_ref[...].astype(F32)
        upv = up_ref[...].astype(F32)
        sg = jax.nn.sigmoid(g)
        dgate = (dact * upv * sg * (1.0 + g * (1.0 - sg))).astype(BF16)
        dup = (dact * g * sg).astype(BF16)
        dgate_ref[...] = dgate
        dup_ref[...] = dup
        act_ref[...] = (g * sg * upv).astype(act_ref.dtype)
        part = jnp.dot(dgate, wg_ref[...], preferred_element_type=F32) + jnp.dot(dup, wu_ref[...], preferred_element_type=F32)

        @pl.when(s == 0)
        def _():
            dh_acc[...] = part

        @pl.when(s > 0)
        def _():
            dh_acc[...] += part

        @pl.when(s == NCHIP - 1)
        def _():
            dh = dh_acc[...]
            xx = x_ref[...]
            r = lax.rsqrt(jnp.mean(xx * xx, axis=-1, keepdims=True) + RMS_EPS)
            yh = xx * r
            t = dh * gpre_ref[...]
            dxin_ref[...] = dx_ref[...] + r * (t - yh * jnp.mean(t * yh, axis=-1, keepdims=True))
            dgpre_ref[...] += jnp.sum(dh * yh, axis=0, keepdims=True)

    rows = pl.BlockSpec((tm, DM), lambda i, s: (i, 0))
    vec = pl.BlockSpec((1, DM), lambda i, s: (0, 0))
    wspec = _ffn_wspec(lambda i, s: (s, 0, 0))
    mid = pl.BlockSpec((None, tm, FSH), lambda i, s: (s, i, 0))
    mid_shape = jax.ShapeDtypeStruct((NCHIP, SEQ, FSH), BF16)
    return _carrier_call(
        name, body, (SEQ // tm, NCHIP), [rows, rows, mid, mid, rows, vec, vec, wspec, wspec, wspec], [rows, rows, mid, mid, mid, vec, vec],
        [jax.ShapeDtypeStruct((SEQ, DM), F32), jax.ShapeDtypeStruct((SEQ, DM), BF16), mid_shape, mid_shape, mid_shape,
         jax.ShapeDtypeStruct((1, DM), F32), jax.ShapeDtypeStruct((1, DM), F32)],
        [pltpu.VMEM((tm, DM), F32)], (dx, x, gate, up, u, g_pre, g_post, wgt4, wut4, wd4), carry)


def _ffn_block_bwd(layer, dx, saved, g_pre, g_post, ex):
    tag = f"l{layer}"
    x, h, gate, up, u = saved
    (dx_in, du, dgate, dup, act, dg_pre, dg_post), sent = _ffn_bwd(
        f"{tag}_ffn_bwd", dx, x, gate, up, u, g_pre, g_post, ex.weight(("ffn_w_gate", layer)), ex.weight(("ffn_w_up", layer)),
        ex.weight(("ffn_w_down", layer)), ex.carry(f"{tag}_ffn_bwd"))
    ex.carried(f"{tag}_ffn_bwd", sent)
    d_wd = _ffn_bwd_dw(f"{tag}_dwd", act, du)
    d_wg = _ffn_bwd_dw(f"{tag}_dwg", dgate, h)
    d_wu = _ffn_bwd_dw(f"{tag}_dwu", dup, h)
    ex.grads(f"{tag}_ffn", {("ffn_w_gate", layer): d_wg, ("ffn_w_up", layer): d_wu, ("ffn_w_down", layer): d_wd})
    return dx_in, dg_pre, dg_post


def _alibi_slopes():
    return 2.0 ** (-8.0 * jnp.arange(1, NH + 1, dtype=F32) / NH)


def _local_step(x, target, norms, rpb, ex):
    g_mix_pre, g_mix_post, g_ffn_pre, g_ffn_post = norms
    row = lambda a, i: a[i:i + 1]

    bias, sent = _na_bias_tiles(rpb, ex.carry("na_bias_tiles"))
    ex.carried("na_bias_tiles", sent)
    h0, h0t = _rms_fwd_both("l0_mix_pre", x, row(g_mix_pre, 0))
    qkv0, sent = _qkv_fwd("l0_qkv", h0[None], ex.weight(("na_w_qkv", 0)), ex.carry("l0_qkv"))
    ex.carried("l0_qkv", sent)
    o0, sent = _na_fwd(qkv0[0], bias, ex.carry("na_fwd"))
    ex.carried("na_fwd", sent)
    na_wo = ex.weight(("na_w_o", 0)).reshape(DM, DM)
    x1, u0 = _proj_fwd("l0_proj", o0, na_wo, x, row(g_mix_post, 0))
    x2, ffn0 = _ffn_block(0, x1, row(g_ffn_pre, 0), row(g_ffn_post, 0), ex)

    slopes = _alibi_slopes()
    h2g, h2gt = _to_groups("l1_h_groups", _rms_fwd("l1_mix_pre", x2, row(g_mix_pre, 1), F32))
    dil_wqkv = ex.weight(("dil_w_qkv", 0))
    qkv1, sent = _qkv_fwd("l1_qkv", h2g, dil_wqkv, ex.carry("l1_qkv"))
    ex.carried("l1_qkv", sent)
    og, lg, sent = _dil_fwd(qkv1, slopes, ex.carry("dil_fwd"))
    ex.carried("dil_fwd", sent)
    o1, lse = _dil_merge(og, lg)
    dil_wo = ex.weight(("dil_w_o", 0)).reshape(DM, DM)
    x3, u1 = _proj_fwd("l1_proj", o1, dil_wo, x2, row(g_mix_post, 1))
    x4, ffn1 = _ffn_block(1, x3, row(g_ffn_pre, 1), row(g_ffn_post, 1), ex)

    dx4, loss_row = _loss_grad("loss", x4, target)

    dx3, dg_fpre1, dg_fpost1 = _ffn_block_bwd(1, dx4, ffn1, row(g_ffn_pre, 1), row(g_ffn_post, 1), ex)
    (do1, du1, dg_mpost1), sent = _proj_bwd("l1_proj_bwd", dx3, u1, row(g_mix_post, 1), dil_wo, F32, ex.carry("l1_proj_bwd"))
    ex.carried("l1_proj_bwd", sent)
    d_dil_wo = _proj_bwd_dw("l1_dwo", o1, du1)
    dog, ddg, lseg = _dil_bwd_prep(do1, o1, lse)
    dqkv1, sent = _dil_bwd(qkv1, dog, ddg, lseg, slopes, ex.carry("dil_bwd"))
    ex.carried("dil_bwd", sent)
    d_dil_wqkv, sent = _qkv_bwd_dw("l1_dwqkv", h2gt, dqkv1, dil_wqkv.shape[2], ex.carry("l1_dwqkv"))
    ex.carried("l1_dwqkv", sent)
    ex.grads("l1_mix", {("dil_w_qkv", 0): d_dil_wqkv, ("dil_w_o", 0): d_dil_wo.reshape(NCHIP, DM // NCHIP, DM)})
    dh2g, sent = _qkv_bwd_dh("l1_dh", dqkv1, dil_wqkv, ex.carry("l1_dh"))
    ex.carried("l1_dh", sent)
    dh2 = _from_groups_sum("l1_dh_tokens", dh2g)
    (dx2, dg_mpre1), sent = _norm_bwd("l1_mix_pre_bwd", dh2, x2, row(g_mix_pre, 1), dx3, ex.carry("l1_mix_pre_bwd"))
    ex.carried("l1_mix_pre_bwd", sent)

    dx1, dg_fpre0, dg_fpost0 = _ffn_block_bwd(0, dx2, ffn0, row(g_ffn_pre, 0), row(g_ffn_post, 0), ex)
    (do0, du0, dg_mpost0), sent = _proj_bwd("l0_proj_bwd", dx1, u0, row(g_mix_post, 0), na_wo, BF16, ex.carry("l0_proj_bwd"))
    ex.carried("l0_proj_bwd", sent)
    d_na_wo = _proj_bwd_dw("l0_dwo", o0, du0)
    dqkv0, z, sent = _na_bwd(qkv0[0], bias, do0, ex.carry("na_bwd"))
    ex.carried("na_bwd", sent)
    d_rpb = _rpb_grad(z)
    na_wqkv = ex.weight(("na_w_qkv", 0))
    d_na_wqkv, sent = _qkv_bwd_dw("l0_dwqkv", h0t[None], dqkv0[None], na_wqkv.shape[2], ex.carry("l0_dwqkv"))
    ex.carried("l0_dwqkv", sent)
    ex.grads("l0_mix", {("na_w_qkv", 0): d_na_wqkv, ("na_w_o", 0): d_na_wo.reshape(NCHIP, DM // NCHIP, DM)})
    dh0, sent = _qkv_bwd_dh("l0_dh", dqkv0[None], na_wqkv, ex.carry("l0_dh"))
    ex.carried("l0_dh", sent)
    (dx0, dg_mpre0), sent = _norm_bwd("l0_mix_pre_bwd", dh0[0], x, row(g_mix_pre, 0), dx1, ex.carry("l0_mix_pre_bwd"))
    ex.carried("l0_mix_pre_bwd", sent)

    dnorms = (jnp.concatenate([dg_mpre0, dg_mpre1]), jnp.concatenate([dg_mpost0, dg_mpost1]),
              jnp.concatenate([dg_fpre0, dg_fpre1]), jnp.concatenate([dg_fpost0, dg_fpost1]))
    return loss_row, dx0, dnorms, d_rpb


def _place():
    x, y, c = lax.axis_index("x"), lax.axis_index("y"), lax.axis_index("c")
    chips = ((1 - x, y), (x, 1 - y), (1 - x, 1 - y))
    return x, y, c, chips


def _chip_id(chip):
    return 2 * chip[0] + chip[1]


def _gather_copies(shards):
    n = len(shards)

    def copies(src, out, sems):
        send_sems, recv_sems = sems
        x, y, c, chips = _place()

        def copy(t, k, chip, half, to, from_src=False):
            blk = out[t].at[_chip_id(chip), half]
            return pltpu.make_async_remote_copy(
                src_ref=src[t].at[half] if from_src else blk, dst_ref=blk,
                send_sem=send_sems.at[6 * t + k], recv_sem=recv_sems.at[6 * t + k], device_id=to, device_id_type=MESH)

        return copy, x, y, c, chips

    def issue(src, out, sems):
        copy, x, y, c, chips = copies(src, out, sems)
        for t in range(n):
            for j, chip in enumerate(chips):
                copy(t, j, (x, y), c, (*chip, c), from_src=True).start()

    def drain(src, out, sems):
        copy, x, y, c, chips = copies(src, out, sems)
        passed = []
        for t in range(n):
            for j, chip in enumerate(chips):
                copy(t, j, chip, c, (x, y, c)).wait_recv()
                fwd = copy(t, 3 + j, chip, c, (x, y, 1 - c))
                fwd.start()
                passed.append(fwd)
        for t in range(n):
            for j, chip in enumerate(chips):
                copy(t, 3 + j, chip, 1 - c, (x, y, c)).wait_recv()
        for t in range(n):
            for j, chip in enumerate(chips):
                copy(t, j, (x, y), c, (*chip, c), from_src=True).wait_send()
        for cp in passed:
            cp.wait_send()

    return _Carried(shards, [jax.ShapeDtypeStruct((NCHIP,) + s.shape, s.dtype) for s in shards], (6 * n, 6 * n), issue, drain)


def _pair_exchange_copies(grads):
    n = len(grads)

    def copies(g, theirs, sems):
        send_sems, recv_sems = sems
        x, y, c, _ = _place()
        return [pltpu.make_async_remote_copy(src_ref=g[t].at[:, 1 - c], dst_ref=theirs[t], send_sem=send_sems.at[t],
                                             recv_sem=recv_sems.at[t], device_id=(x, y, 1 - c), device_id_type=MESH) for t in range(n)]

    def issue(g, theirs, sems):
        for cp in copies(g, theirs, sems):
            cp.start()

    def drain(g, theirs, sems):
        for cp in copies(g, theirs, sems):
            cp.wait()

    return _Carried(grads, [jax.ShapeDtypeStruct((NCHIP,) + g.shape[2:], g.dtype) for g in grads], (n, n), issue, drain)


def _chip_exchange_copies(items):
    flat = [(t, i, j) for t, (_, peers) in enumerate(items) for i, j in enumerate(peers)]

    def copies(p, slots, sems):
        send_sems, recv_sems = sems
        x, y, c, chips = _place()
        return [pltpu.make_async_remote_copy(src_ref=p[t].at[_chip_id(chips[j])], dst_ref=slots[t].at[i], send_sem=send_sems.at[k],
                                             recv_sem=recv_sems.at[k], device_id=(*chips[j], c), device_id_type=MESH)
                for k, (t, i, j) in enumerate(flat)]

    def issue(p, slots, sems):
        for cp in copies(p, slots, sems):
            cp.start()

    def drain(p, slots, sems):
        for cp in copies(p, slots, sems):
            cp.wait()

    return _Carried([p for p, _ in items], [jax.ShapeDtypeStruct((len(peers),) + p.shape[1:], p.dtype) for p, peers in items],
                    (len(flat), len(flat)), issue, drain)


def _pair_share_copies(halves):
    n = len(halves)

    def copies(h, other, sems):
        send_sems, recv_sems = sems
        x, y, c, _ = _place()
        return [pltpu.make_async_remote_copy(src_ref=h[t], dst_ref=other[t], send_sem=send_sems.at[t], recv_sem=recv_sems.at[t],
                                             device_id=(x, y, 1 - c), device_id_type=MESH) for t in range(n)]

    def issue(h, other, sems):
        for cp in copies(h, other, sems):
            cp.start()

    def drain(h, other, sems):
        for cp in copies(h, other, sems):
            cp.wait()

    return _Carried(halves, [jax.ShapeDtypeStruct(h.shape, h.dtype) for h in halves], (n, n), issue, drain)


SMALL_ROWS = 128


def _allreduce_small(v):
    def body(v_ref, o_ref, buf, send_sems, recv_sems):
        x, y, c, _ = _place()
        me = 4 * x + 2 * y + c
        flip = lambda a, f: 1 - a if f else a
        buf[me] = v_ref[...]
        peers = [(flip(x, d >> 2 & 1), flip(y, d >> 1 & 1), flip(c, d & 1)) for d in range(1, 8)]
        sends = [pltpu.make_async_remote_copy(src_ref=v_ref, dst_ref=buf.at[me], send_sem=send_sems.at[i], recv_sem=recv_sems.at[i],
                                              device_id=peer, device_id_type=MESH) for i, peer in enumerate(peers)]
        for cp in sends:
            cp.start()
        for i, (px, py, pc) in enumerate(peers):
            pltpu.make_async_remote_copy(src_ref=v_ref, dst_ref=buf.at[4 * px + 2 * py + pc], send_sem=send_sems.at[i], recv_sem=recv_sems.at[i],
                                         device_id=(px, py, pc), device_id_type=MESH).wait_recv()
        for cp in sends:
            cp.wait_send()
        acc = buf[0]
        for k in range(1, 8):
            acc = acc + buf[k]
        o_ref[...] = acc

    vm = pl.BlockSpec(memory_space=pltpu.VMEM)
    return pl.pallas_call(
        body, in_specs=[vm], out_specs=vm, out_shape=jax.ShapeDtypeStruct((SMALL_ROWS, 128), F32),
        scratch_shapes=[pltpu.VMEM((8, SMALL_ROWS, 128), F32), pltpu.SemaphoreType.DMA((7,)), pltpu.SemaphoreType.DMA((7,))],
        compiler_params=pltpu.CompilerParams(has_side_effects=True), name="allreduce_small")(v)


def _row_block(rows, cols, budget=3 << 19):
    best = 8
    for bm in range(8, rows + 1, 8):
        if rows % bm == 0 and bm * cols * 4 <= budget:
            best = bm
    return best


def _pair_sum(name, place, gs, theirs):
    n = len(gs)
    _, m, c = theirs[0].shape
    bm = _row_block(m, c)

    def body(place_ref, *refs):
        for a_ref, b_ref, o_ref in zip(refs[:n], refs[n:2 * n], refs[2 * n:]):
            o_ref[...] = (a_ref[...].astype(F32) + b_ref[...].astype(F32)).astype(o_ref.dtype)

    spec = pl.BlockSpec((None, bm, c), lambda k, i, pr: (k, i, 0))
    return pl.pallas_call(
        body, out_shape=[jax.ShapeDtypeStruct(theirs[0].shape, BF16)] * n,
        grid_spec=pltpu.PrefetchScalarGridSpec(
            num_scalar_prefetch=1, grid=(NCHIP, m // bm),
            in_specs=[pl.BlockSpec((None, None, bm, c), lambda k, i, pr: (k, pr[0], i, 0))] * n + [spec] * n, out_specs=[spec] * n),
        compiler_params=_params(("parallel", "parallel")), name=name)(place, *gs, *theirs)


def _chip_sum(name, place, parts, slots):
    n, ns = len(parts), len(slots[0])
    _, m, c = parts[0].shape
    bm = _row_block(m, c)

    def body(place_ref, *refs):
        for t in range(n):
            acc = refs[t][...].astype(F32)
            for s_ref in refs[n + t * ns:n + (t + 1) * ns]:
                for i in range(s_ref.shape[0]):
                    acc = acc + s_ref[i].astype(F32)
            refs[n + n * ns + t][...] = acc

    half = pl.BlockSpec((bm, c), lambda i, pr: (i, 0))
    return pl.pallas_call(
        body, out_shape=[jax.ShapeDtypeStruct((m, c), F32)] * n,
        grid_spec=pltpu.PrefetchScalarGridSpec(
            num_scalar_prefetch=1, grid=(m // bm,),
            in_specs=[pl.BlockSpec((None, bm, c), lambda i, pr: (pr[1], i, 0))] * n
            + [pl.BlockSpec((s.shape[0], bm, c), lambda i, pr: (0, i, 0)) for group in slots for s in group],
            out_specs=[half] * n),
        compiler_params=_params(("parallel",)), name=name)(place, *parts, *[s for group in slots for s in group])


def _adamw(name, place, tensors, layer=0, into=None):
    n = len(tensors)
    lead, rows, cols = tensors[0][0].shape
    bm = _row_block(rows // 2, cols, budget=768 * 1024 // n)
    per_half = rows // 2 // bm
    c1 = 1.0 - ADAM_B1 ** ADAM_STEP
    c2 = 1.0 - ADAM_B2 ** ADAM_STEP

    def body(place_ref, *refs):
        outs = refs[len(refs) - 4 * n:]
        for t in range(n):
            w_ref, ga_ref, gb_ref, m_ref, v_ref = refs[5 * t:5 * t + 5]
            go_ref, d_ref, mo_ref, vo_ref = outs[4 * t:4 * t + 4]
            g = jnp.where(pl.program_id(0) // per_half == place_ref[0], ga_ref[...], gb_ref[...])
            mn = ADAM_B1 * m_ref[...] + (1.0 - ADAM_B1) * g
            vn = ADAM_B2 * v_ref[...] + (1.0 - ADAM_B2) * (g * g)
            go_ref[...] = g
            mo_ref[...] = mn
            vo_ref[...] = vn
            d_ref[...] = -ADAM_LR * ((mn / c1) / (jnp.sqrt(vn / c2) + ADAM_EPS) + ADAM_WD * w_ref[...])

    spec = pl.BlockSpec((None, bm, cols), lambda i, pr: (layer, i, 0))

    def half_spec(mine):
        def index(i, pr):
            first = (pr[0] == 0) == mine
            park = jnp.where(first, per_half - 1, 0)
            return jnp.where((i < per_half) == first, i % per_half, park), 0
        return pl.BlockSpec((bm, cols), index)
    sh = jax.ShapeDtypeStruct((lead, rows, cols), F32)
    prev = [] if into is None else [a for res in into for a in res]
    res = pl.pallas_call(
        body, out_shape=[sh] * (4 * n), input_output_aliases={1 + 5 * n + k: k for k in range(len(prev))},
        grid_spec=pltpu.PrefetchScalarGridSpec(
            num_scalar_prefetch=1, grid=(rows // bm,),
            in_specs=[spec, half_spec(True), half_spec(False), spec, spec] * n + [pl.BlockSpec(memory_space=pl.ANY)] * len(prev),
            out_specs=[spec] * (4 * n)),
        compiler_params=_params(("parallel",)), name=name)(place, *[a for t in tensors for a in t], *prev)
    return [res[4 * t:4 * t + 4] for t in range(n)]


def _pack_small(norms, rpb, last=None):
    flat = jnp.concatenate([a.reshape(-1) for a in norms] + [rpb.reshape(-1)])
    flat = jnp.pad(flat, (0, SMALL_ROWS * 128 - flat.shape[0]))
    if last is not None:
        flat = lax.dynamic_update_slice(flat, last.reshape(1), (flat.shape[0] - 1,))
    return flat.reshape(SMALL_ROWS, 128)


def _unpack_small(p):
    flat = p.reshape(-1)
    norms = [flat[i * 2 * DM:(i + 1) * 2 * DM].reshape(2, DM) for i in range(4)]
    rpb = flat[8 * DM:8 * DM + NH * 15 * 31].reshape(1, NH, 15, 31)
    return norms, rpb


FFN_NAMES = ("ffn_w_gate", "ffn_w_up", "ffn_w_down")
L0_FFN = tuple((n, 0) for n in FFN_NAMES)
L1_FFN = tuple((n, 1) for n in FFN_NAMES)
NA_KEYS = (("na_w_qkv", 0), ("na_w_o", 0))
DIL_KEYS = (("dil_w_qkv", 0), ("dil_w_o", 0))
ALL_PEERS, NEIGHBOURS, DIAGONAL = (0, 1, 2), (0, 1), (2,)


class _Exchange:
    GATHERS = {"na_bias_tiles": NA_KEYS, "l0_qkv": L0_FFN[:1], "na_fwd": L0_FFN[1:], "l0_ffn_fwd": DIL_KEYS[:1], "dil_fwd": L1_FFN + DIL_KEYS[1:]}
    PAIRS = {"l1_proj_bwd": L1_FFN, "l1_dh": DIL_KEYS, "l0_proj_bwd": L0_FFN}
    EXCHANGES = {"dil_bwd": [(k, ALL_PEERS) for k in L1_FFN],
                 "l0_ffn_bwd": [(DIL_KEYS[0], NEIGHBOURS), (DIL_KEYS[1], ALL_PEERS)],
                 "na_bwd": [(k, ALL_PEERS) for k in L0_FFN] + [(DIL_KEYS[0], DIAGONAL)],
                 "l0_dh": [(k, NEIGHBOURS) for k in NA_KEYS],
                 "l0_mix_pre_bwd": [(k, DIAGONAL) for k in NA_KEYS]}
    SHARES = {"l1_dwqkv": L1_FFN, "l0_dwqkv": L0_FFN + DIL_KEYS}

    def __init__(self, shards):
        self.chip = 2 * lax.axis_index("x") + lax.axis_index("y")
        self.place = jnp.stack([lax.axis_index("c"), self.chip]).astype(jnp.int32)
        self.own = {k: s.reshape(2, s.shape[0] // 2, s.shape[1]).astype(BF16) for k, s in shards.items()}
        self.gathered, self.mine, self.parts, self.slots, self.full, self.other = {}, {}, {}, {}, {}, {}

    def _take(self, keys, landed):
        for k, gw in zip(keys, landed):
            self.gathered[k] = lax.dynamic_update_slice(gw, self.own[k][None], (self.chip, 0, 0, 0))

    def _sum(self, items, landed):
        runs = []
        for (k, peers), s in zip(items, landed):
            got = self.slots.setdefault(k, {})
            got[peers] = s
            if sum(len(p) for p in got) == len(ALL_PEERS):
                like = (self.parts[k].shape, tuple(sorted(got)))
                if runs and runs[-1][0] == like:
                    runs[-1][1].append(k)
                else:
                    runs.append((like, [k]))
        for (_, split), ks in runs:
            sums = _chip_sum(f"chip_sum_{ks[0][0]}_{ks[0][1]}", self.place, [self.parts[k] for k in ks],
                             [[self.slots[k][p] for p in split] for k in ks])
            self.full.update(zip(ks, sums))

    def weight(self, key):
        g = self.gathered[key]
        return g.reshape(NCHIP, 2 * g.shape[2], g.shape[3])

    def _pair_sums(self, keys, theirs):
        runs = []
        for k, t in zip(keys, theirs):
            if runs and runs[-1][0][1].shape == t.shape:
                runs[-1].append((k, t))
            else:
                runs.append([(k, t)])
        for run in runs:
            ks = [k for k, _ in run]
            sums = _pair_sum(f"pair_sum_{ks[0][0]}_{ks[0][1]}", self.place, [self.mine[k] for k in ks], [t for _, t in run])
            self.parts.update(zip(ks, sums))

    def carry(self, tag):
        if tag in self.GATHERS:
            return _gather_copies([self.own[k] for k in self.GATHERS[tag]])
        if tag in self.PAIRS:
            return _pair_exchange_copies([self.mine[k] for k in self.PAIRS[tag]])
        if tag in self.EXCHANGES:
            return _chip_exchange_copies([(self.parts[k], peers) for k, peers in self.EXCHANGES[tag]])
        if tag in self.SHARES:
            return _pair_share_copies([self.full[k] for k in self.SHARES[tag]])
        return None

    def carried(self, tag, landed):
        if tag in self.GATHERS:
            self._take(self.GATHERS[tag], landed)
        elif tag in self.PAIRS:
            self._pair_sums(self.PAIRS[tag], landed)
        elif tag in self.EXCHANGES:
            self._sum(self.EXCHANGES[tag], landed)
        elif tag in self.SHARES:
            self.other.update(zip(self.SHARES[tag], landed))

    def grads(self, tag, dw):
        for k, g in dw.items():
            self.mine[k] = g.reshape(NCHIP, 2, -1, g.shape[-1])
        if tag == "l0_mix":
            keys = tuple(dw)
            self._pair_sums(keys, _run_carried("grad_pair_exchange_last", _pair_exchange_copies([self.mine[k] for k in keys])))

    def finish(self):
        rest = tuple(k for k in self.full if k not in self.other)
        self.other.update(zip(rest, _run_carried("grad_pair_share_last", _pair_share_copies([self.full[k] for k in rest]))))
        return {k: (self.full[k], self.other[k]) for k in self.full}


def kernel(x, norm_mix_pre, norm_mix_post, norm_ffn_pre, norm_ffn_post, na_w_qkv, na_w_o, na_rpb, dil_w_qkv, dil_w_o, ffn_w_gate, ffn_w_up, ffn_w_down, loss_target, m_norm_mix_pre, m_norm_mix_post, m_norm_ffn_pre, m_norm_ffn_post, m_na_w_qkv, m_na_w_o, m_na_rpb, m_dil_w_qkv, m_dil_w_o, m_ffn_w_gate, m_ffn_w_up, m_ffn_w_down, v_norm_mix_pre, v_norm_mix_post, v_norm_ffn_pre, v_norm_ffn_post, v_na_w_qkv, v_na_w_o, v_na_rpb, v_dil_w_qkv, v_dil_w_o, v_ffn_w_gate, v_ffn_w_up, v_ffn_w_down):
    tr = lambda a: jnp.swapaxes(a, 1, 2)
    weights = {"na_w_qkv": na_w_qkv, "na_w_o": na_w_o, "dil_w_qkv": dil_w_qkv, "dil_w_o": dil_w_o,
               "ffn_w_gate": tr(ffn_w_gate), "ffn_w_up": tr(ffn_w_up), "ffn_w_down": ffn_w_down}
    m_in = {"na_w_qkv": m_na_w_qkv, "na_w_o": m_na_w_o, "dil_w_qkv": m_dil_w_qkv, "dil_w_o": m_dil_w_o,
            "ffn_w_gate": tr(m_ffn_w_gate), "ffn_w_up": tr(m_ffn_w_up), "ffn_w_down": m_ffn_w_down}
    v_in = {"na_w_qkv": v_na_w_qkv, "na_w_o": v_na_w_o, "dil_w_qkv": v_dil_w_qkv, "dil_w_o": v_dil_w_o,
            "ffn_w_gate": tr(v_ffn_w_gate), "ffn_w_up": tr(v_ffn_w_up), "ffn_w_down": v_ffn_w_down}

    ex = _Exchange({(n, l): weights[n][l] for n in weights for l in range(weights[n].shape[0])})
    norms = (norm_mix_pre, norm_mix_post, norm_ffn_pre, norm_ffn_post)
    loss_row, dx, dnorms, d_rpb = _local_step(x[0], loss_target[0], norms, na_rpb[0], ex)
    full = ex.finish()
    small = _allreduce_small(_pack_small(dnorms, d_rpb, last=loss_row[0, 0]))
    loss = small[SMALL_ROWS - 1, 127]

    out_g, out_d, out_m, out_v = {}, {}, {}, {}
    operands = lambda n, l: (weights[n], *full[(n, l)], m_in[n], v_in[n])
    results = {n: _adamw(f"adamw_{n}", ex.place, [operands(n, 0)])[0] for n in weights if n not in FFN_NAMES}
    ffn = None
    for l in range(2):
        ffn = _adamw(f"adamw_ffn_{l}", ex.place, [operands(n, l) for n in FFN_NAMES], l, ffn)
    results.update(zip(FFN_NAMES, ffn))
    for n, res in results.items():
        if n in ("ffn_w_gate", "ffn_w_up"):
            res = [tr(r) for r in res]
        out_g[n], out_d[n], out_m[n], out_v[n] = res
    sm_names = ("norm_mix_pre", "norm_mix_post", "norm_ffn_pre", "norm_ffn_post", "na_rpb")
    sm = _adamw("adamw_small", jnp.zeros((2,), jnp.int32),
                [(_pack_small(norms, na_rpb)[None], small[:SMALL_ROWS // 2], small[SMALL_ROWS // 2:],
                  _pack_small((m_norm_mix_pre, m_norm_mix_post, m_norm_ffn_pre, m_norm_ffn_post), m_na_rpb)[None],
                  _pack_small((v_norm_mix_pre, v_norm_mix_post, v_norm_ffn_pre, v_norm_ffn_post), v_na_rpb)[None])])[0]
    for res, dst in zip(sm, (out_g, out_d, out_m, out_v)):
        ns, rp = _unpack_small(res)
        for n, a in zip(sm_names, ns + [rp]):
            dst[n] = a

    order = ("norm_mix_pre", "norm_mix_post", "norm_ffn_pre", "norm_ffn_post", "na_w_qkv", "na_w_o", "na_rpb", "dil_w_qkv", "dil_w_o",
             "ffn_w_gate", "ffn_w_up", "ffn_w_down")
    return (loss, dx[None], *[out_g[n] for n in order], *[out_d[n] for n in order], *[out_m[n] for n in order], *[out_v[n] for n in order])
```

```python
import functools

import numpy as np
import jax
import jax.numpy as jnp
from jax import lax
from jax.experimental import pallas as pl
from jax.experimental.pallas import tpu as pltpu

F32 = jnp.float32
BF16 = jnp.bfloat16

SEQ = 2048
DM = 1024
NH = 16
HD = 64
DFF = 2816
NCHIP = 4
FSH = DFF // NCHIP
GRID_W = 64
NA_QROWS = 4
NA_QB = NA_QROWS * GRID_W
NA_WROWS = 12
NA_WIN = NA_WROWS * GRID_W
DIL = (1, 4, 16)
DIL_QB = 256
DIL_WIN = DIL_QB + 128
DIL_RADIUS = 64
RMS_EPS = 1e-6
NEG = -1e30
QSCALE = HD ** -0.5
CH = 256
MESH = pl.DeviceIdType.MESH

ADAM_LR, ADAM_B1, ADAM_B2, ADAM_EPS, ADAM_WD, ADAM_STEP = 0.001, 0.9, 0.999, 1e-08, 0.01, 10

VMEM_LIMIT = 56 * 1024 * 1024

_NN = (((1,), (0,)), ((), ()))
_NT = (((1,), (1,)), ((), ()))
_TN = (((0,), (0,)), ((), ()))


def _params(sem):
    return pltpu.CompilerParams(dimension_semantics=sem, vmem_limit_bytes=VMEM_LIMIT)


def _matmul(name, pairs, grid, out_shape, out_spec, acc_shape, carrying=False, carry=None):
    nk = grid[-1]
    npair = len(pairs)
    n_in = 2 * npair

    def body(*refs):
        ins, o_ref = refs[:2 * npair], refs[n_in]
        part = None
        for p in range(npair):
            d = lax.dot_general(ins[2 * p][...].astype(BF16), ins[2 * p + 1][...].astype(BF16), pairs[p][4],
                                preferred_element_type=F32)
            part = d if part is None else part + d
        if nk == 1:
            o_ref[...] = part.astype(o_ref.dtype)
        else:
            acc_ref = refs[n_in + 1]
            kk = pl.program_id(len(grid) - 1)

            @pl.when(kk == 0)
            def _():
                acc_ref[...] = part

            @pl.when(kk > 0)
            def _():
                acc_ref[...] += part

            @pl.when(kk == nk - 1)
            def _():
                o_ref[...] = acc_ref[...].astype(o_ref.dtype)

    ops, specs = [], []
    for a, a_spec, b, b_spec, _ in pairs:
        ops += [a, b]
        specs += [a_spec, b_spec]
    (out,), sent = _carrier_call(name, body, grid, specs, [out_spec], [out_shape], [] if nk == 1 else [pltpu.VMEM(acc_shape, F32)], ops, carry)
    return (out, sent) if carrying else out


def _qkv_fwd(name, h_all, w4, carry):
    g_n = h_all.shape[0]
    per = w4.shape[2] // CH
    return _matmul(
        name, [(h_all, pl.BlockSpec((None, SEQ, DM), lambda g, q, k: (g, 0, 0)),
                w4, pl.BlockSpec((None, DM, CH), lambda g, q, k: ((g * 12 + q) // per, 0, (g * 12 + q) % per)), _NN)],
        (g_n, 12, 1), jax.ShapeDtypeStruct((g_n, SEQ, 3 * DM), BF16),
        pl.BlockSpec((None, SEQ, CH), lambda g, q, k: (g, 0, q)), None, carrying=True, carry=carry)


def _qkv_bwd_dh(name, dqkv, w4, carry):
    g_n = dqkv.shape[0]
    per = w4.shape[2] // CH
    tm = SEQ

    def pair(cb):
        chunk = lambda g, t: g * 12 + t * 4 + cb
        return (dqkv, pl.BlockSpec((None, None, tm, CH), lambda g, i, t: (g, t, i, cb)),
                w4, pl.BlockSpec((None, DM, CH), lambda g, i, t: (chunk(g, t) // per, 0, chunk(g, t) % per)), _NT)

    return _matmul(name, [pair(cb) for cb in range(4)], (g_n, SEQ // tm, 3), jax.ShapeDtypeStruct((g_n, SEQ, DM), F32),
                   pl.BlockSpec((None, tm, DM), lambda g, i, t: (g, i, 0)), (tm, DM), carrying=True, carry=carry)


def _qkv_bwd_dw(name, ht_all, dqkv, shard_cols, carry):
    g_n = dqkv.shape[0]
    per = shard_cols // CH
    return _matmul(
        name, [(ht_all, pl.BlockSpec((None, DM, SEQ), lambda qq, k: (qq // 12, 0, 0)),
                dqkv, pl.BlockSpec((None, None, SEQ, CH), lambda qq, k: (qq // 12, (qq % 12) // 4, 0, qq % 4)), _NN)],
        (g_n * 12, 1), jax.ShapeDtypeStruct((NCHIP, DM, shard_cols), BF16),
        pl.BlockSpec((None, DM, CH), lambda qq, k: (qq // per, 0, qq % per)), None, carrying=True, carry=carry)


def _proj_fwd(name, o, wo, x, g):
    tm = 512

    def body(o_ref, w_ref, x_ref, g_ref, xn_ref, u_ref):
        u = jnp.dot(o_ref[...], w_ref[...], preferred_element_type=F32)
        u_ref[...] = u
        r = lax.rsqrt(jnp.mean(u * u, axis=-1, keepdims=True) + RMS_EPS)
        xn_ref[...] = x_ref[...] + u * r * g_ref[...]

    rows = pl.BlockSpec((tm, DM), lambda i: (i, 0))
    sh = jax.ShapeDtypeStruct((SEQ, DM), F32)
    return pl.pallas_call(
        body, grid=(SEQ // tm,), in_specs=[rows, pl.BlockSpec((DM, DM), lambda i: (0, 0)), rows, pl.BlockSpec((1, DM), lambda i: (0, 0))],
        out_specs=[rows, rows], out_shape=[sh, sh], compiler_params=_params(("parallel",)), name=name)(o, wo, x, g)


def _proj_bwd(name, dy, u, g, wo, dtype, carry):
    tm = 512

    def body(dy_ref, u_ref, g_ref, w_ref, do_ref, du_ref, dg_ref):
        dy = dy_ref[...]
        u = u_ref[...]
        r = lax.rsqrt(jnp.mean(u * u, axis=-1, keepdims=True) + RMS_EPS)
        yh = u * r
        t = dy * g_ref[...]
        du = (r * (t - yh * jnp.mean(t * yh, axis=-1, keepdims=True))).astype(BF16)
        du_ref[...] = du
        do_ref[...] = lax.dot_general(du, w_ref[...], _NT, preferred_element_type=F32).astype(do_ref.dtype)

        @pl.when(pl.program_id(0) == 0)
        def _():
            dg_ref[...] = jnp.zeros_like(dg_ref)

        dg_ref[...] += jnp.sum(dy * yh, axis=0, keepdims=True)

    rows = pl.BlockSpec((tm, DM), lambda i: (i, 0))
    vec = pl.BlockSpec((1, DM), lambda i: (0, 0))
    return _carrier_call(
        name, body, (SEQ // tm,), [rows, rows, vec, pl.BlockSpec((DM, DM), lambda i: (0, 0))], [rows, rows, vec],
        [jax.ShapeDtypeStruct((SEQ, DM), dtype), jax.ShapeDtypeStruct((SEQ, DM), BF16), jax.ShapeDtypeStruct((1, DM), F32)],
        [], (dy, u, g, wo), carry)


def _proj_bwd_dw(name, o, du):
    tn = 512
    return _matmul(
        name, [(o, pl.BlockSpec((SEQ, DM), lambda j, k: (0, 0)), du, pl.BlockSpec((SEQ, tn), lambda j, k: (0, j)), _TN)],
        (DM // tn, 1), jax.ShapeDtypeStruct((DM, DM), BF16), pl.BlockSpec((DM, tn), lambda j, k: (0, j)), None)


def _ffn_wspec(index_map):
    return pl.BlockSpec((None, FSH, DM), index_map)


def _ffn_bwd_dw(name, a4, b):
    return _matmul(
        name, [(a4, pl.BlockSpec((None, SEQ, FSH), lambda s, k: (s, 0, 0)), b, pl.BlockSpec((SEQ, DM), lambda s, k: (0, 0)), _TN)],
        (NCHIP, 1), jax.ShapeDtypeStruct((NCHIP, FSH, DM), BF16), _ffn_wspec(lambda s, k: (s, 0, 0)), None)


ROWS = 256


def _row_spec():
    return pl.BlockSpec((ROWS, DM), lambda i: (i, 0))


def _vec_spec():
    return pl.BlockSpec((1, DM), lambda i: (0, 0))


def _rms_fwd(name, x, g, dtype=BF16):
    def body(x_ref, g_ref, o_ref):
        x = x_ref[...]
        r = lax.rsqrt(jnp.mean(x * x, axis=-1, keepdims=True) + RMS_EPS)
        o_ref[...] = (x * r * g_ref[...]).astype(o_ref.dtype)

    return pl.pallas_call(body, grid=(SEQ // ROWS,), in_specs=[_row_spec(), _vec_spec()], out_specs=_row_spec(),
                          out_shape=jax.ShapeDtypeStruct((SEQ, DM), dtype), compiler_params=_params(("parallel",)), name=name)(x, g)


def _rms_fwd_both(name, x, g):
    def body(x_ref, g_ref, o_ref, t_ref):
        x = x_ref[...]
        r = lax.rsqrt(jnp.mean(x * x, axis=-1, keepdims=True) + RMS_EPS)
        h = x * r * g_ref[...]
        o_ref[...] = h.astype(o_ref.dtype)
        t_ref[...] = h.T.astype(t_ref.dtype)

    return pl.pallas_call(
        body, grid=(SEQ // ROWS,), in_specs=[_row_spec(), _vec_spec()], out_specs=[_row_spec(), pl.BlockSpec((DM, ROWS), lambda i: (0, i))],
        out_shape=[jax.ShapeDtypeStruct((SEQ, DM), BF16), jax.ShapeDtypeStruct((DM, SEQ), BF16)],
        compiler_params=_params(("parallel",)), name=name)(x, g)


def _norm_bwd(name, dy, u, g, res, carry):
    def body(dy_ref, u_ref, g_ref, res_ref, du_ref, dg_ref):
        dy = dy_ref[...]
        u = u_ref[...]
        r = lax.rsqrt(jnp.mean(u * u, axis=-1, keepdims=True) + RMS_EPS)
        yh = u * r
        t = dy * g_ref[...]
        du_ref[...] = r * (t - yh * jnp.mean(t * yh, axis=-1, keepdims=True)) + res_ref[...]

        @pl.when(pl.program_id(0) == 0)
        def _():
            dg_ref[...] = jnp.zeros_like(dg_ref)

        dg_ref[...] += jnp.sum(dy * yh, axis=0, keepdims=True)

    return _carrier_call(
        name, body, (SEQ // ROWS,), [_row_spec(), _row_spec(), _vec_spec(), _row_spec()], [_row_spec(), _vec_spec()],
        [jax.ShapeDtypeStruct((SEQ, DM), F32), jax.ShapeDtypeStruct((1, DM), F32)], [], (dy, u, g, res), carry)


def _loss_grad(name, y, t):
    def body(y_ref, t_ref, dy_ref, l_ref):
        e = y_ref[...] - t_ref[...]
        dy_ref[...] = e * (1.0 / DM)

        @pl.when(pl.program_id(0) == 0)
        def _():
            l_ref[...] = jnp.zeros_like(l_ref)

        l_ref[...] += jnp.sum(e * e) * (0.5 / DM)

    return pl.pallas_call(
        body, grid=(SEQ // ROWS,), in_specs=[_row_spec(), _row_spec()],
        out_specs=[_row_spec(), pl.BlockSpec((1, 128), lambda i: (0, 0))],
        out_shape=[jax.ShapeDtypeStruct((SEQ, DM), F32), jax.ShapeDtypeStruct((1, 128), F32)],
        compiler_params=_params(("arbitrary",)), name=name)(y, t)


HBM_SPEC = pl.BlockSpec(memory_space=pltpu.HBM)


class _Carried:
    def __init__(self, ins, out_shapes, n_sems, issue, drain):
        self.ins, self.out_shapes, self.n_sems, self.issue, self.drain = list(ins), list(out_shapes), tuple(n_sems), issue, drain


def _carrier_call(name, body, grid, in_specs, out_specs, out_shape, scratch_shapes, operands, carry):
    n_in, n_out, n_scr = len(in_specs), len(out_specs), len(scratch_shapes)
    if carry is None:
        res = pl.pallas_call(body, grid=grid, in_specs=in_specs, out_specs=out_specs, out_shape=out_shape, scratch_shapes=scratch_shapes,
                             compiler_params=_params(("arbitrary",) * len(grid)), name=name)(*operands)
        return list(res), []
    ci, co = len(carry.ins), len(carry.out_shapes)

    def wrapped(*refs):
        ins, cins = refs[:n_in], refs[n_in:n_in + ci]
        outs, couts = refs[n_in + ci:n_in + ci + n_out], refs[n_in + ci + n_out:n_in + ci + n_out + co]
        scr, sems = refs[n_in + ci + n_out + co:n_in + ci + n_out + co + n_scr], refs[n_in + ci + n_out + co + n_scr:]
        first = functools.reduce(jnp.logical_and, [pl.program_id(a) == 0 for a in range(len(grid))])
        last = functools.reduce(jnp.logical_and, [pl.program_id(a) == grid[a] - 1 for a in range(len(grid))])

        @pl.when(first)
        def _():
            carry.issue(cins, couts, sems)

        body(*ins, *outs, *scr)

        @pl.when(last)
        def _():
            carry.drain(cins, couts, sems)

    res = pl.pallas_call(
        wrapped, grid=grid, in_specs=list(in_specs) + [HBM_SPEC] * ci, out_specs=list(out_specs) + [HBM_SPEC] * co,
        out_shape=list(out_shape) + carry.out_shapes,
        scratch_shapes=list(scratch_shapes) + [pltpu.SemaphoreType.DMA((k,)) for k in carry.n_sems],
        compiler_params=pltpu.CompilerParams(dimension_semantics=("arbitrary",) * len(grid), vmem_limit_bytes=VMEM_LIMIT, has_side_effects=True),
        name=name)(*operands, *carry.ins)
    return list(res[:n_out]), list(res[n_out:])


def _run_carried(name, carry):
    def body(*refs):
        ci, co = len(carry.ins), len(carry.out_shapes)
        carry.issue(refs[:ci], refs[ci:ci + co], refs[ci + co:])
        carry.drain(refs[:ci], refs[ci:ci + co], refs[ci + co:])

    return pl.pallas_call(
        body, in_specs=[HBM_SPEC] * len(carry.ins), out_specs=[HBM_SPEC] * len(carry.out_shapes), out_shape=carry.out_shapes,
        scratch_shapes=[pltpu.SemaphoreType.DMA((k,)) for k in carry.n_sems],
        compiler_params=pltpu.CompilerParams(has_side_effects=True), name=name)(*carry.ins)


NA_BLOCKS = SEQ // NA_QB
NA_ROWS_TOTAL = SEQ // GRID_W
NA_CLASSES = ((0, 0), (8, 4), (NA_ROWS_TOTAL - NA_QROWS, NA_ROWS_TOTAL - NA_WROWS))


def _na_pairs(i0, ws):
    out = []
    for qi in range(NA_QROWS):
        i = i0 + qi
        rs = min(max(i - 4, 0), NA_ROWS_TOTAL - 8)
        for kr in range(NA_WROWS):
            r = ws + kr
            if rs <= r < rs + 8:
                out.append((qi, kr, r - i + 7))
    return out


def _diag_onehot():
    qc, kc = np.meshgrid(np.arange(GRID_W), np.arange(GRID_W), indexing="ij")
    e = np.zeros((GRID_W * GRID_W, 128), np.float32)
    j = (kc - qc + 15).reshape(-1)
    ok = (j >= 0) & (j <= 30)
    e[np.arange(GRID_W * GRID_W)[ok], j[ok]] = 1.0
    return jnp.asarray(e)


def _rpb_expand(rpb):
    r2 = jnp.pad(rpb.reshape(NH * 15, 31), ((0, 0), (0, 128 - 31)))

    def body(r_ref, e_ref, o_ref):
        o_ref[...] = lax.dot_general(r_ref[...], e_ref[...], _NT, preferred_element_type=F32, precision=lax.Precision.HIGHEST)

    out = pl.pallas_call(body, out_shape=jax.ShapeDtypeStruct((NH * 15, GRID_W * GRID_W), F32), name="rpb_expand",
                         compiler_params=pltpu.CompilerParams(vmem_limit_bytes=VMEM_LIMIT))(r2, _diag_onehot())
    return out.reshape(NH, 15, GRID_W, GRID_W)


def _na_bias_tiles(rpb, carry):
    def body(b_ref, o_ref):
        qc = lax.broadcasted_iota(jnp.int32, (GRID_W, GRID_W), 0)
        kc = lax.broadcasted_iota(jnp.int32, (GRID_W, GRID_W), 1)
        first = jnp.clip(qc - 8, 0, GRID_W - 16)
        in_window = (kc >= first) & (kc < first + 16)
        neg = jnp.full((GRID_W, GRID_W), NEG, F32)
        for cls, (i0, ws) in enumerate(NA_CLASSES):
            @pl.when(pl.program_id(0) == cls)
            def _(i0=i0, ws=ws):
                pairs = {(qi, kr): dr for qi, kr, dr in _na_pairs(i0, ws)}
                masked = {dr: jnp.where(in_window, b_ref[dr], NEG) for dr in sorted(set(pairs.values()))}
                for qi in range(NA_QROWS):
                    for k2 in range(NA_WROWS // 2):
                        blocks = [masked[pairs[(qi, kr)]] if (qi, kr) in pairs else neg for kr in (2 * k2, 2 * k2 + 1)]
                        o_ref[qi * GRID_W:(qi + 1) * GRID_W, k2 * 128:(k2 + 1) * 128] = jnp.concatenate(blocks, axis=1)

    (tiles,), sent = _carrier_call(
        "na_bias_tiles", body, (3, NH), [pl.BlockSpec((None, 15, GRID_W, GRID_W), lambda c, h: (h, 0, 0, 0))],
        [pl.BlockSpec((None, None, NA_QB, NA_WIN), lambda c, h: (c, h, 0, 0))], [jax.ShapeDtypeStruct((3, NH, NA_QB, NA_WIN), F32)],
        [], (_rpb_expand(rpb),), carry)
    return tiles, sent


def _na_cls(b):
    return jnp.where(b == 0, 0, jnp.where(b == NA_BLOCKS - 1, 2, 1))


def _na_start(b):
    return pl.multiple_of(jnp.clip(b * NA_QROWS - 4, 0, NA_ROWS_TOTAL - NA_WROWS) * GRID_W, GRID_W)


NA_FWD_HPS = 8
NA_BWD_HPS = 4


def _na_in_specs(hps):
    lw = hps * HD
    nlw = DM // lw
    return [pl.BlockSpec((NA_QB, lw), lambda hp, b: (b, hp)),
            pl.BlockSpec((SEQ, lw), lambda hp, b: (0, nlw + hp)),
            pl.BlockSpec((SEQ, lw), lambda hp, b: (0, 2 * nlw + hp)),
            pl.BlockSpec((None, hps, NA_QB, NA_WIN), lambda hp, b: (_na_cls(b), hp, 0, 0))]


def _na_fwd(qkv, bias, carry):
    lw = NA_FWD_HPS * HD

    def body(q_ref, k_ref, v_ref, b_ref, o_ref):
        start = _na_start(pl.program_id(1))
        q = q_ref[...]
        kw = k_ref[pl.ds(start, NA_WIN), :]
        vw = v_ref[pl.ds(start, NA_WIN), :]
        outs = []
        for hh in range(NA_FWD_HPS):
            sl = slice(hh * HD, (hh + 1) * HD)
            s = lax.dot_general(q[:, sl] * QSCALE, kw[:, sl], _NT, preferred_element_type=F32) + b_ref[hh]
            p = jnp.exp(s - jnp.max(s, axis=-1, keepdims=True))
            l = jnp.sum(p, axis=-1, keepdims=True)
            outs.append(jnp.dot(p.astype(BF16), vw[:, sl], preferred_element_type=F32) / l)
        o_ref[...] = jnp.concatenate(outs, axis=1).astype(o_ref.dtype)

    (o,), sent = _carrier_call(
        "na_fwd", body, (NH // NA_FWD_HPS, NA_BLOCKS), _na_in_specs(NA_FWD_HPS), [pl.BlockSpec((NA_QB, lw), lambda hp, b: (b, hp))],
        [jax.ShapeDtypeStruct((SEQ, DM), BF16)], [], (qkv, qkv, qkv, bias), carry)
    return o, sent


def _na_bwd(qkv, bias, do, carry):
    lw = NA_BWD_HPS * HD

    def body(q_ref, k_ref, v_ref, b_ref, do_ref, dqkv_ref, z_ref, dk_acc, dv_acc):
        blk = pl.program_id(1)

        @pl.when(blk == 0)
        def _():
            dk_acc[...] = jnp.zeros_like(dk_acc)
            dv_acc[...] = jnp.zeros_like(dv_acc)
            z_ref[...] = jnp.zeros_like(z_ref)

        start = _na_start(blk)
        q = q_ref[...]
        do = do_ref[...]
        kw = k_ref[pl.ds(start, NA_WIN), :]
        vw = v_ref[pl.ds(start, NA_WIN), :]
        dqs, dks, dvs, dss = [], [], [], []
        for hh in range(NA_BWD_HPS):
            sl = slice(hh * HD, (hh + 1) * HD)
            qh = q[:, sl] * QSCALE
            s = lax.dot_general(qh, kw[:, sl], _NT, preferred_element_type=F32) + b_ref[hh]
            p = jnp.exp(s - jnp.max(s, axis=-1, keepdims=True))
            p = p / jnp.sum(p, axis=-1, keepdims=True)
            dp = lax.dot_general(do[:, sl], vw[:, sl], _NT, preferred_element_type=F32)
            ds = p * (dp - jnp.sum(p * dp, axis=-1, keepdims=True))
            dsb = ds.astype(BF16)
            dqs.append(jnp.dot(dsb, kw[:, sl], preferred_element_type=F32) * QSCALE)
            dks.append(lax.dot_general(qh, dsb, _TN, preferred_element_type=F32).T)
            dvs.append(lax.dot_general(do[:, sl], p.astype(BF16), _TN, preferred_element_type=F32).T)
            dss.append(ds)
        for cls, (i0, ws) in enumerate(NA_CLASSES):
            @pl.when(_na_cls(blk) == cls)
            def _(i0=i0, ws=ws):
                for hh, ds in enumerate(dss):
                    for qi, kr, dr in _na_pairs(i0, ws):
                        z_ref[hh, dr * GRID_W:(dr + 1) * GRID_W, :] += ds[qi * GRID_W:(qi + 1) * GRID_W, kr * GRID_W:(kr + 1) * GRID_W]
        dqkv_ref[0, pl.ds(pl.multiple_of(blk * NA_QB, NA_QB), NA_QB), :] = jnp.concatenate(dqs, axis=1).astype(dqkv_ref.dtype)
        dk_acc[pl.ds(start, NA_WIN), :] += jnp.concatenate(dks, axis=1)
        dv_acc[pl.ds(start, NA_WIN), :] += jnp.concatenate(dvs, axis=1)

        @pl.when(blk == NA_BLOCKS - 1)
        def _():
            dqkv_ref[1] = dk_acc[...].astype(dqkv_ref.dtype)
            dqkv_ref[2] = dv_acc[...].astype(dqkv_ref.dtype)

    (dqkv, z), sent = _carrier_call(
        "na_bwd", body, (NH // NA_BWD_HPS, NA_BLOCKS),
        _na_in_specs(NA_BWD_HPS) + [pl.BlockSpec((NA_QB, lw), lambda hp, b: (b, hp))],
        [pl.BlockSpec((3, SEQ, lw), lambda hp, b: (0, 0, hp)), pl.BlockSpec((NA_BWD_HPS, 15 * GRID_W, GRID_W), lambda hp, b: (hp, 0, 0))],
        [jax.ShapeDtypeStruct((3, SEQ, DM), BF16), jax.ShapeDtypeStruct((NH, 15 * GRID_W, GRID_W), F32)],
        [pltpu.VMEM((SEQ, lw), F32), pltpu.VMEM((SEQ, lw), F32)], (qkv, qkv, qkv, bias, do), carry)
    return dqkv, z, sent


def _rpb_grad(z):
    z2 = z.reshape(NH * 15, GRID_W * GRID_W)

    def body(z_ref, e_ref, o_ref):
        o_ref[...] = jnp.dot(z_ref[...], e_ref[...], preferred_element_type=F32, precision=lax.Precision.HIGHEST)

    out = pl.pallas_call(body, out_shape=jax.ShapeDtypeStruct((NH * 15, 128), F32), name="rpb_grad",
                         compiler_params=pltpu.CompilerParams(vmem_limit_bytes=VMEM_LIMIT))(z2, _diag_onehot())
    return out[:, :31].reshape(NH, 15, 31)


DIL_BLOCKS = SEQ // DIL_QB
DIL_HPS = 8
DIL_LW = DIL_HPS * HD
DIL_NLW = DM // DIL_LW


COLS = 128


def _col_spec():
    return pl.BlockSpec((SEQ, COLS), lambda j: (0, j))


def _grp_spec():
    return pl.BlockSpec((3, SEQ, COLS), lambda j: (0, 0, j))


def _store_group_order(dst_ref, src_ref):
    for g, d in enumerate(DIL):
        n = SEQ // d
        for r in range(d):
            dst_ref[g, r * n:(r + 1) * n, :] = src_ref[pl.ds(r, n, stride=d), :].astype(dst_ref.dtype)


def _store_token_order(dst_ref, src_ref, g):
    d = DIL[g]
    n = SEQ // d
    for r in range(d):
        dst_ref[pl.ds(r, n, stride=d), :] = src_ref[g, r * n:(r + 1) * n, :]


def _to_groups(name, a):
    def body(a_ref, o_ref, t_ref):
        _store_group_order(o_ref, a_ref)
        for g in range(3):
            t_ref[g] = o_ref[g].astype(F32).T.astype(t_ref.dtype)

    return pl.pallas_call(
        body, grid=(DM // COLS,), in_specs=[_col_spec()], out_specs=[_grp_spec(), pl.BlockSpec((3, COLS, SEQ), lambda j: (0, j, 0))],
        out_shape=[jax.ShapeDtypeStruct((3, SEQ, DM), BF16), jax.ShapeDtypeStruct((3, DM, SEQ), BF16)],
        compiler_params=_params(("parallel",)), name=name)(a)


def _from_groups_sum(name, a):
    def body(a_ref, o_ref, t1, t2):
        _store_token_order(t1, a_ref, 1)
        _store_token_order(t2, a_ref, 2)
        o_ref[...] = (a_ref[0] + t1[...]) + t2[...]

    return pl.pallas_call(body, grid=(DM // COLS,), in_specs=[_grp_spec()], out_specs=_col_spec(),
                          out_shape=jax.ShapeDtypeStruct((SEQ, DM), F32), scratch_shapes=[pltpu.VMEM((SEQ, COLS), F32)] * 2,
                          compiler_params=_params(("parallel",)), name=name)(a)


def _dil_start(b):
    return pl.multiple_of(jnp.clip(b * DIL_QB - DIL_RADIUS, 0, SEQ - DIL_WIN), DIL_RADIUS)


def _dil_neg_dist(g, ii, jj):
    shift = 11 - 2 * g
    dist = jnp.abs(ii - jj)
    valid = (dist <= DIL_RADIUS) & (jnp.right_shift(ii, shift) == jnp.right_shift(jj, shift))
    return jnp.where(valid, -dist.astype(F32), NEG)


def _dil_in_specs():
    return [pl.BlockSpec(memory_space=pltpu.SMEM),
            pl.BlockSpec((None, DIL_QB, DIL_LW), lambda g, hp, b: (g, b, hp)),
            pl.BlockSpec((None, SEQ, DIL_LW), lambda g, hp, b: (g, 0, DIL_NLW + hp)),
            pl.BlockSpec((None, SEQ, DIL_LW), lambda g, hp, b: (g, 0, 2 * DIL_NLW + hp))]


def _dil_fwd(qkv, slopes, carry):
    def body(sl_ref, q_ref, k_ref, v_ref, o_ref, lse_ref):
        g, hp, b = pl.program_id(0), pl.program_id(1), pl.program_id(2)
        start = _dil_start(b)
        neg_dist = _dil_neg_dist(g, b * DIL_QB + lax.broadcasted_iota(jnp.int32, (DIL_QB, DIL_WIN), 0),
                                 start + lax.broadcasted_iota(jnp.int32, (DIL_QB, DIL_WIN), 1))
        dil = jnp.left_shift(1, 2 * g).astype(F32)
        q = q_ref[...]
        kw = k_ref[pl.ds(start, DIL_WIN), :]
        vw = v_ref[pl.ds(start, DIL_WIN), :]
        outs, lses = [], []
        for hh in range(DIL_HPS):
            sl = slice(hh * HD, (hh + 1) * HD)
            s = lax.dot_general(q[:, sl] * QSCALE, kw[:, sl], _NT, preferred_element_type=F32)
            s = s + (sl_ref[hp * DIL_HPS + hh] * dil) * neg_dist
            m = jnp.max(s, axis=-1, keepdims=True)
            p = jnp.exp(s - m)
            l = jnp.sum(p, axis=-1, keepdims=True)
            outs.append(jnp.dot(p.astype(BF16), vw[:, sl], preferred_element_type=F32) / l)
            lses.append(jnp.broadcast_to(m + jnp.log(l), (DIL_QB, HD)))
        o_ref[...] = jnp.concatenate(outs, axis=1)
        lse_ref[...] = jnp.concatenate(lses, axis=1)

    ospec = pl.BlockSpec((None, DIL_QB, DIL_LW), lambda g, hp, b: (g, b, hp))
    sh = jax.ShapeDtypeStruct((3, SEQ, DM), F32)
    (o, lse), sent = _carrier_call("dil_fwd", body, (3, DIL_NLW, DIL_BLOCKS), _dil_in_specs(), [ospec, ospec], [sh, sh], [],
                                   (slopes, qkv, qkv, qkv), carry)
    return o, lse, sent


def _dil_merge(o_all, lse_all):
    def body(o_ref, l_ref, out_ref, lse_ref, o1, o2, l1, l2):
        for g, (ot, lt) in ((1, (o1, l1)), (2, (o2, l2))):
            _store_token_order(ot, o_ref, g)
            _store_token_order(lt, l_ref, g)
        la, lb, lc = l_ref[0], l1[...], l2[...]
        m = jnp.maximum(jnp.maximum(la, lb), lc)
        wa, wb, wc = jnp.exp(la - m), jnp.exp(lb - m), jnp.exp(lc - m)
        sw = (wa + wb) + wc
        out_ref[...] = (((wa * o_ref[0] + wb * o1[...]) + wc * o2[...]) / sw).astype(out_ref.dtype)
        lse_ref[...] = m + jnp.log(sw)

    return pl.pallas_call(
        body, grid=(DM // COLS,), in_specs=[_grp_spec(), _grp_spec()], out_specs=[_col_spec(), _col_spec()],
        out_shape=[jax.ShapeDtypeStruct((SEQ, DM), BF16), jax.ShapeDtypeStruct((SEQ, DM), F32)],
        scratch_shapes=[pltpu.VMEM((SEQ, COLS), F32)] * 4, compiler_params=_params(("parallel",)), name="dil_merge")(o_all, lse_all)


def _dil_bwd_prep(do, o, lse):
    heads = COLS // HD

    def body(do_ref, o_ref, lse_ref, dog_ref, ddr_ref, lser_ref, dd, grp):
        prod = do_ref[...] * o_ref[...].astype(F32)
        dd[...] = jnp.concatenate(
            [jnp.broadcast_to(jnp.sum(prod[:, h * HD:(h + 1) * HD], axis=-1, keepdims=True), (SEQ, HD)) for h in range(heads)], axis=1)
        _store_group_order(dog_ref, do_ref)
        for src, dst in ((dd, ddr_ref), (lse_ref, lser_ref)):
            _store_group_order(grp, src)
            for g in range(3):
                t = grp[g].T
                for h in range(heads):
                    dst[g, h] = t[h * HD:h * HD + 8, :]

    rows = jax.ShapeDtypeStruct((3, NH, 8, SEQ), F32)
    rspec = pl.BlockSpec((3, heads, 8, SEQ), lambda j: (0, j, 0, 0))
    return pl.pallas_call(
        body, grid=(DM // COLS,), in_specs=[_col_spec()] * 3, out_specs=[_grp_spec(), rspec, rspec],
        out_shape=[jax.ShapeDtypeStruct((3, SEQ, DM), BF16), rows, rows],
        scratch_shapes=[pltpu.VMEM((SEQ, COLS), F32), pltpu.VMEM((3, SEQ, COLS), F32)],
        compiler_params=_params(("parallel",)), name="dil_bwd_prep")(do, o, lse)


def _dil_bwd(qkv, do, dd, lse, slopes, carry):
    def body(sl_ref, q_ref, k_ref, v_ref, do_ref, dd_ref, lse_ref, dqkv_ref, dk_acc, dv_acc):
        g, hp, b = pl.program_id(0), pl.program_id(1), pl.program_id(2)

        @pl.when(b == 0)
        def _():
            dk_acc[...] = jnp.zeros_like(dk_acc)
            dv_acc[...] = jnp.zeros_like(dv_acc)

        start = _dil_start(b)
        neg_dist = _dil_neg_dist(g, b * DIL_QB + lax.broadcasted_iota(jnp.int32, (DIL_WIN, DIL_QB), 1),
                                 start + lax.broadcasted_iota(jnp.int32, (DIL_WIN, DIL_QB), 0))
        dil = jnp.left_shift(1, 2 * g).astype(F32)
        q = q_ref[...]
        do = do_ref[...]
        kw = k_ref[pl.ds(start, DIL_WIN), :]
        vw = v_ref[pl.ds(start, DIL_WIN), :]
        dqs, dks, dvs = [], [], []
        for hh in range(DIL_HPS):
            sl = slice(hh * HD, (hh + 1) * HD)
            qh = q[:, sl] * QSCALE
            st = lax.dot_general(kw[:, sl], qh, _NT, preferred_element_type=F32)
            st = st + (sl_ref[hp * DIL_HPS + hh] * dil) * neg_dist
            pt = jnp.exp(st - lse_ref[hh, 0:1, :])
            dpt = lax.dot_general(vw[:, sl], do[:, sl], _NT, preferred_element_type=F32)
            dst = (pt * (dpt - dd_ref[hh, 0:1, :])).astype(BF16)
            dqs.append(lax.dot_general(kw[:, sl], dst, _TN, preferred_element_type=F32).T * QSCALE)
            dks.append(jnp.dot(dst, qh, preferred_element_type=F32))
            dvs.append(jnp.dot(pt.astype(BF16), do[:, sl], preferred_element_type=F32))
        dqkv_ref[0, pl.ds(pl.multiple_of(b * DIL_QB, DIL_QB), DIL_QB), :] = jnp.concatenate(dqs, axis=1).astype(dqkv_ref.dtype)
        dk_acc[pl.ds(start, DIL_WIN), :] += jnp.concatenate(dks, axis=1)
        dv_acc[pl.ds(start, DIL_WIN), :] += jnp.concatenate(dvs, axis=1)

        @pl.when(b == DIL_BLOCKS - 1)
        def _():
            dqkv_ref[1] = dk_acc[...].astype(dqkv_ref.dtype)
            dqkv_ref[2] = dv_acc[...].astype(dqkv_ref.dtype)

    qspec = pl.BlockSpec((None, DIL_QB, DIL_LW), lambda g, hp, b: (g, b, hp))
    rspec = pl.BlockSpec((None, DIL_HPS, 8, DIL_QB), lambda g, hp, b: (g, hp, 0, b))
    (dqkv,), sent = _carrier_call(
        "dil_bwd", body, (3, DIL_NLW, DIL_BLOCKS), _dil_in_specs() + [qspec, rspec, rspec],
        [pl.BlockSpec((None, 3, SEQ, DIL_LW), lambda g, hp, b: (g, 0, 0, hp))], [jax.ShapeDtypeStruct((3, 3, SEQ, DM), BF16)],
        [pltpu.VMEM((SEQ, DIL_LW), F32), pltpu.VMEM((SEQ, DIL_LW), F32)], (slopes, qkv, qkv, qkv, do, dd, lse), carry)
    return dqkv, sent


def _ffn_fwd(name, x, g_pre, g_post, wgt4, wut4, wd4, carry):
    tm = 512

    def body(x_ref, gpre_ref, gpost_ref, wg_ref, wu_ref, wd_ref, xn_ref, h_ref, gate_ref, up_ref, u_ref, acc):
        s = pl.program_id(1)

        @pl.when(s == 0)
        def _():
            x = x_ref[...]
            r = lax.rsqrt(jnp.mean(x * x, axis=-1, keepdims=True) + RMS_EPS)
            h_ref[...] = (x * r * gpre_ref[...]).astype(h_ref.dtype)

        h = h_ref[...]
        gate = lax.dot_general(h, wg_ref[...], _NT, preferred_element_type=F32).astype(BF16)
        up = lax.dot_general(h, wu_ref[...], _NT, preferred_element_type=F32).astype(BF16)
        gate_ref[...] = gate
        up_ref[...] = up
        gf = gate.astype(F32)
        act = (gf * jax.nn.sigmoid(gf) * up.astype(F32)).astype(BF16)
        part = jnp.dot(act, wd_ref[...], preferred_element_type=F32)

        @pl.when(s == 0)
        def _():
            acc[...] = part

        @pl.when(s > 0)
        def _():
            acc[...] += part

        @pl.when(s == NCHIP - 1)
        def _():
            u = acc[...]
            u_ref[...] = u
            r = lax.rsqrt(jnp.mean(u * u, axis=-1, keepdims=True) + RMS_EPS)
            xn_ref[...] = x_ref[...] + u * r * gpost_ref[...]

    rows = pl.BlockSpec((tm, DM), lambda i, s: (i, 0))
    vec = pl.BlockSpec((1, DM), lambda i, s: (0, 0))
    wspec = _ffn_wspec(lambda i, s: (s, 0, 0))
    mid = pl.BlockSpec((None, tm, FSH), lambda i, s: (s, i, 0))
    outs, sent = _carrier_call(
        name, body, (SEQ // tm, NCHIP), [rows, vec, vec, wspec, wspec, wspec], [rows, rows, mid, mid, rows],
        [jax.ShapeDtypeStruct((SEQ, DM), F32), jax.ShapeDtypeStruct((SEQ, DM), BF16), jax.ShapeDtypeStruct((NCHIP, SEQ, FSH), BF16),
         jax.ShapeDtypeStruct((NCHIP, SEQ, FSH), BF16), jax.ShapeDtypeStruct((SEQ, DM), F32)],
        [pltpu.VMEM((tm, DM), F32)], (x, g_pre, g_post, wgt4, wut4, wd4), carry)
    return outs, sent


def _ffn_block(layer, x, g_pre, g_post, ex):
    tag = f"l{layer}_ffn_fwd"
    (x_new, h, gate, up, u), sent = _ffn_fwd(tag, x, g_pre, g_post, ex.weight(("ffn_w_gate", layer)), ex.weight(("ffn_w_up", layer)),
                                             ex.weight(("ffn_w_down", layer)), ex.carry(tag))
    ex.carried(tag, sent)
    return x_new, (x, h, gate, up, u)


def _ffn_bwd(name, dx, x, gate, up, u, g_pre, g_post, wgt4, wut4, wd4, carry):
    tm = 512

    def body(dx_ref, x_ref, gate_ref, up_ref, u_ref, gpre_ref, gpost_ref, wg_ref, wu_ref, wd_ref,
             dxin_ref, du_ref, dgate_ref, dup_ref, act_ref, dgpre_ref, dgpost_ref, dh_acc):
        i, s = pl.program_id(0), pl.program_id(1)

        @pl.when((i == 0) & (s == 0))
        def _():
            dgpre_ref[...] = jnp.zeros_like(dgpre_ref)
            dgpost_ref[...] = jnp.zeros_like(dgpost_ref)

        @pl.when(s == 0)
        def _():
            dy = dx_ref[...]
            uu = u_ref[...]
            r = lax.rsqrt(jnp.mean(uu * uu, axis=-1, keepdims=True) + RMS_EPS)
            yh = uu * r
            t = dy * gpost_ref[...]
            du_ref[...] = (r * (t - yh * jnp.mean(t * yh, axis=-1, keepdims=True))).astype(du_ref.dtype)
            dgpost_ref[...] += jnp.sum(dy * yh, axis=0, keepdims=True)

        dact = lax.dot_general(du_ref[...], wd_ref[...], _NT, preferred_element_type=F32)
        g = gate_ref[...].astype(F32)
        upv = up_ref[...].astype(F32)
        sg = jax.nn.sigmoid(g)
        dgate = (dact * upv * sg * (1.0 + g * (1.0 - sg))).astype(BF16)
        dup = (dact * g * sg).astype(BF16)
        dgate_ref[...] = dgate
        dup_ref[...] = dup
        act_ref[...] = (g * sg * upv).astype(act_ref.dtype)
        part = jnp.dot(dgate, wg_ref[...], preferred_element_type=F32) + jnp.dot(dup, wu_ref[...], preferred_element_type=F32)

        @pl.when(s == 0)
        def _():
            dh_acc[...] = part

        @pl.when(s > 0)
        def _():
            dh_acc[...] += part

        @pl.when(s == NCHIP - 1)
        def _():
            dh = dh_acc[...]
            xx = x_ref[...]
            r = lax.rsqrt(jnp.mean(xx * xx, axis=-1, keepdims=True) + RMS_EPS)
            yh = xx * r
            t = dh * gpre_ref[...]
            dxin_ref[...] = dx_ref[...] + r * (t - yh * jnp.mean(t * yh, axis=-1, keepdims=True))
            dgpre_ref[...] += jnp.sum(dh * yh, axis=0, keepdims=True)

    rows = pl.BlockSpec((tm, DM), lambda i, s: (i, 0))
    vec = pl.BlockSpec((1, DM), lambda i, s: (0, 0))
    wspec = _ffn_wspec(lambda i, s: (s, 0, 0))
    mid = pl.BlockSpec((None, tm, FSH), lambda i, s: (s, i, 0))
    mid_shape = jax.ShapeDtypeStruct((NCHIP, SEQ, FSH), BF16)
    return _carrier_call(
        name, body, (SEQ // tm, NCHIP), [rows, rows, mid, mid, rows, vec, vec, wspec, wspec, wspec], [rows, rows, mid, mid, mid, vec, vec],
        [jax.ShapeDtypeStruct((SEQ, DM), F32), jax.ShapeDtypeStruct((SEQ, DM), BF16), mid_shape, mid_shape, mid_shape,
         jax.ShapeDtypeStruct((1, DM), F32), jax.ShapeDtypeStruct((1, DM), F32)],
        [pltpu.VMEM((tm, DM), F32)], (dx, x, gate, up, u, g_pre, g_post, wgt4, wut4, wd4), carry)


def _ffn_block_bwd(layer, dx, saved, g_pre, g_post, ex):
    tag = f"l{layer}"
    x, h, gate, up, u = saved
    (dx_in, du, dgate, dup, act, dg_pre, dg_post), sent = _ffn_bwd(
        f"{tag}_ffn_bwd", dx, x, gate, up, u, g_pre, g_post, ex.weight(("ffn_w_gate", layer)), ex.weight(("ffn_w_up", layer)),
        ex.weight(("ffn_w_down", layer)), ex.carry(f"{tag}_ffn_bwd"))
    ex.carried(f"{tag}_ffn_bwd", sent)
    d_wd = _ffn_bwd_dw(f"{tag}_dwd", act, du)
    d_wg = _ffn_bwd_dw(f"{tag}_dwg", dgate, h)
    d_wu = _ffn_bwd_dw(f"{tag}_dwu", dup, h)
    ex.grads(f"{tag}_ffn", {("ffn_w_gate", layer): d_wg, ("ffn_w_up", layer): d_wu, ("ffn_w_down", layer): d_wd})
    return dx_in, dg_pre, dg_post


def _alibi_slopes():
    return 2.0 ** (-8.0 * jnp.arange(1, NH + 1, dtype=F32) / NH)


def _local_step(x, target, norms, rpb, ex):
    g_mix_pre, g_mix_post, g_ffn_pre, g_ffn_post = norms
    row = lambda a, i: a[i:i + 1]

    bias, sent = _na_bias_tiles(rpb, ex.carry("na_bias_tiles"))
    ex.carried("na_bias_tiles", sent)
    h0, h0t = _rms_fwd_both("l0_mix_pre", x, row(g_mix_pre, 0))
    qkv0, sent = _qkv_fwd("l0_qkv", h0[None], ex.weight(("na_w_qkv", 0)), ex.carry("l0_qkv"))
    ex.carried("l0_qkv", sent)
    o0, sent = _na_fwd(qkv0[0], bias, ex.carry("na_fwd"))
    ex.carried("na_fwd", sent)
    na_wo = ex.weight(("na_w_o", 0)).reshape(DM, DM)
    x1, u0 = _proj_fwd("l0_proj", o0, na_wo, x, row(g_mix_post, 0))
    x2, ffn0 = _ffn_block(0, x1, row(g_ffn_pre, 0), row(g_ffn_post, 0), ex)

    slopes = _alibi_slopes()
    h2g, h2gt = _to_groups("l1_h_groups", _rms_fwd("l1_mix_pre", x2, row(g_mix_pre, 1), F32))
    dil_wqkv = ex.weight(("dil_w_qkv", 0))
    qkv1, sent = _qkv_fwd("l1_qkv", h2g, dil_wqkv, ex.carry("l1_qkv"))
    ex.carried("l1_qkv", sent)
    og, lg, sent = _dil_fwd(qkv1, slopes, ex.carry("dil_fwd"))
    ex.carried("dil_fwd", sent)
    o1, lse = _dil_merge(og, lg)
    dil_wo = ex.weight(("dil_w_o", 0)).reshape(DM, DM)
    x3, u1 = _proj_fwd("l1_proj", o1, dil_wo, x2, row(g_mix_post, 1))
    x4, ffn1 = _ffn_block(1, x3, row(g_ffn_pre, 1), row(g_ffn_post, 1), ex)

    dx4, loss_row = _loss_grad("loss", x4, target)

    dx3, dg_fpre1, dg_fpost1 = _ffn_block_bwd(1, dx4, ffn1, row(g_ffn_pre, 1), row(g_ffn_post, 1), ex)
    (do1, du1, dg_mpost1), sent = _proj_bwd("l1_proj_bwd", dx3, u1, row(g_mix_post, 1), dil_wo, F32, ex.carry("l1_proj_bwd"))
    ex.carried("l1_proj_bwd", sent)
    d_dil_wo = _proj_bwd_dw("l1_dwo", o1, du1)
    dog, ddg, lseg = _dil_bwd_prep(do1, o1, lse)
    dqkv1, sent = _dil_bwd(qkv1, dog, ddg, lseg, slopes, ex.carry("dil_bwd"))
    ex.carried("dil_bwd", sent)
    d_dil_wqkv, sent = _qkv_bwd_dw("l1_dwqkv", h2gt, dqkv1, dil_wqkv.shape[2], ex.carry("l1_dwqkv"))
    ex.carried("l1_dwqkv", sent)
    ex.grads("l1_mix", {("dil_w_qkv", 0): d_dil_wqkv, ("dil_w_o", 0): d_dil_wo.reshape(NCHIP, DM // NCHIP, DM)})
    dh2g, sent = _qkv_bwd_dh("l1_dh", dqkv1, dil_wqkv, ex.carry("l1_dh"))
    ex.carried("l1_dh", sent)
    dh2 = _from_groups_sum("l1_dh_tokens", dh2g)
    (dx2, dg_mpre1), sent = _norm_bwd("l1_mix_pre_bwd", dh2, x2, row(g_mix_pre, 1), dx3, ex.carry("l1_mix_pre_bwd"))
    ex.carried("l1_mix_pre_bwd", sent)

    dx1, dg_fpre0, dg_fpost0 = _ffn_block_bwd(0, dx2, ffn0, row(g_ffn_pre, 0), row(g_ffn_post, 0), ex)
    (do0, du0, dg_mpost0), sent = _proj_bwd("l0_proj_bwd", dx1, u0, row(g_mix_post, 0), na_wo, BF16, ex.carry("l0_proj_bwd"))
    ex.carried("l0_proj_bwd", sent)
    d_na_wo = _proj_bwd_dw("l0_dwo", o0, du0)
    dqkv0, z, sent = _na_bwd(qkv0[0], bias, do0, ex.carry("na_bwd"))
    ex.carried("na_bwd", sent)
    d_rpb = _rpb_grad(z)
    na_wqkv = ex.weight(("na_w_qkv", 0))
    d_na_wqkv, sent = _qkv_bwd_dw("l0_dwqkv", h0t[None], dqkv0[None], na_wqkv.shape[2], ex.carry("l0_dwqkv"))
    ex.carried("l0_dwqkv", sent)
    ex.grads("l0_mix", {("na_w_qkv", 0): d_na_wqkv, ("na_w_o", 0): d_na_wo.reshape(NCHIP, DM // NCHIP, DM)})
    dh0, sent = _qkv_bwd_dh("l0_dh", dqkv0[None], na_wqkv, ex.carry("l0_dh"))
    ex.carried("l0_dh", sent)
    (dx0, dg_mpre0), sent = _norm_bwd("l0_mix_pre_bwd", dh0[0], x, row(g_mix_pre, 0), dx1, ex.carry("l0_mix_pre_bwd"))
    ex.carried("l0_mix_pre_bwd", sent)

    dnorms = (jnp.concatenate([dg_mpre0, dg_mpre1]), jnp.concatenate([dg_mpost0, dg_mpost1]),
              jnp.concatenate([dg_fpre0, dg_fpre1]), jnp.concatenate([dg_fpost0, dg_fpost1]))
    return loss_row, dx0, dnorms, d_rpb


def _place():
    x, y, c = lax.axis_index("x"), lax.axis_index("y"), lax.axis_index("c")
    chips = ((1 - x, y), (x, 1 - y), (1 - x, 1 - y))
    return x, y, c, chips


def _chip_id(chip):
    return 2 * chip[0] + chip[1]


def _gather_copies(shards):
    n = len(shards)

    def copies(src, out, sems):
        send_sems, recv_sems = sems
        x, y, c, chips = _place()

        def copy(t, k, chip, half, to, from_src=False):
            blk = out[t].at[_chip_id(chip), half]
            return pltpu.make_async_remote_copy(
                src_ref=src[t].at[half] if from_src else blk, dst_ref=blk,
                send_sem=send_sems.at[6 * t + k], recv_sem=recv_sems.at[6 * t + k], device_id=to, device_id_type=MESH)

        return copy, x, y, c, chips

    def issue(src, out, sems):
        copy, x, y, c, chips = copies(src, out, sems)
        for t in range(n):
            for j, chip in enumerate(chips):
                copy(t, j, (x, y), c, (*chip, c), from_src=True).start()

    def drain(src, out, sems):
        copy, x, y, c, chips = copies(src, out, sems)
        passed = []
        for t in range(n):
            for j, chip in enumerate(chips):
                copy(t, j, chip, c, (x, y, c)).wait_recv()
                fwd = copy(t, 3 + j, chip, c, (x, y, 1 - c))
                fwd.start()
                passed.append(fwd)
        for t in range(n):
            for j, chip in enumerate(chips):
                copy(t, 3 + j, chip, 1 - c, (x, y, c)).wait_recv()
        for t in range(n):
            for j, chip in enumerate(chips):
                copy(t, j, (x, y), c, (*chip, c), from_src=True).wait_send()
        for cp in passed:
            cp.wait_send()

    return _Carried(shards, [jax.ShapeDtypeStruct((NCHIP,) + s.shape, s.dtype) for s in shards], (6 * n, 6 * n), issue, drain)


def _pair_exchange_copies(grads):
    n = len(grads)

    def copies(g, theirs, sems):
        send_sems, recv_sems = sems
        x, y, c, _ = _place()
        return [pltpu.make_async_remote_copy(src_ref=g[t].at[:, 1 - c], dst_ref=theirs[t], send_sem=send_sems.at[t],
                                             recv_sem=recv_sems.at[t], device_id=(x, y, 1 - c), device_id_type=MESH) for t in range(n)]

    def issue(g, theirs, sems):
        for cp in copies(g, theirs, sems):
            cp.start()

    def drain(g, theirs, sems):
        for cp in copies(g, theirs, sems):
            cp.wait()

    return _Carried(grads, [jax.ShapeDtypeStruct((NCHIP,) + g.shape[2:], g.dtype) for g in grads], (n, n), issue, drain)


def _chip_exchange_copies(items):
    flat = [(t, i, j) for t, (_, peers) in enumerate(items) for i, j in enumerate(peers)]

    def copies(p, slots, sems):
        send_sems, recv_sems = sems
        x, y, c, chips = _place()
        return [pltpu.make_async_remote_copy(src_ref=p[t].at[_chip_id(chips[j])], dst_ref=slots[t].at[i], send_sem=send_sems.at[k],
                                             recv_sem=recv_sems.at[k], device_id=(*chips[j], c), device_id_type=MESH)
                for k, (t, i, j) in enumerate(flat)]

    def issue(p, slots, sems):
        for cp in copies(p, slots, sems):
            cp.start()

    def drain(p, slots, sems):
        for cp in copies(p, slots, sems):
            cp.wait()

    return _Carried([p for p, _ in items], [jax.ShapeDtypeStruct((len(peers),) + p.shape[1:], p.dtype) for p, peers in items],
                    (len(flat), len(flat)), issue, drain)


def _pair_share_copies(halves):
    n = len(halves)

    def copies(h, other, sems):
        send_sems, recv_sems = sems
        x, y, c, _ = _place()
        return [pltpu.make_async_remote_copy(src_ref=h[t], dst_ref=other[t], send_sem=send_sems.at[t], recv_sem=recv_sems.at[t],
                                             device_id=(x, y, 1 - c), device_id_type=MESH) for t in range(n)]

    def issue(h, other, sems):
        for cp in copies(h, other, sems):
            cp.start()

    def drain(h, other, sems):
        for cp in copies(h, other, sems):
            cp.wait()

    return _Carried(halves, [jax.ShapeDtypeStruct(h.shape, h.dtype) for h in halves], (n, n), issue, drain)


SMALL_ROWS = 128


def _allreduce_small(v, carry):
    ci, co = len(carry.ins), len(carry.out_shapes)

    def body(*refs):
        v_ref, cins, o_ref, couts = refs[0], refs[1:1 + ci], refs[1 + ci], refs[2 + ci:2 + ci + co]
        buf, send_sems, recv_sems = refs[2 + ci + co:5 + ci + co]
        csems = refs[5 + ci + co:]
        carry.issue(cins, couts, csems)
        x, y, c, _ = _place()
        me = 4 * x + 2 * y + c
        flip = lambda a, f: 1 - a if f else a
        buf[me] = v_ref[...]
        peers = [(flip(x, d >> 2 & 1), flip(y, d >> 1 & 1), flip(c, d & 1)) for d in range(1, 8)]
        sends = [pltpu.make_async_remote_copy(src_ref=v_ref, dst_ref=buf.at[me], send_sem=send_sems.at[i], recv_sem=recv_sems.at[i],
                                              device_id=peer, device_id_type=MESH) for i, peer in enumerate(peers)]
        for cp in sends:
            cp.start()
        for i, (px, py, pc) in enumerate(peers):
            pltpu.make_async_remote_copy(src_ref=v_ref, dst_ref=buf.at[4 * px + 2 * py + pc], send_sem=send_sems.at[i], recv_sem=recv_sems.at[i],
                                         device_id=(px, py, pc), device_id_type=MESH).wait_recv()
        for cp in sends:
            cp.wait_send()
        acc = buf[0]
        for k in range(1, 8):
            acc = acc + buf[k]
        o_ref[...] = acc
        carry.drain(cins, couts, csems)

    vm = pl.BlockSpec(memory_space=pltpu.VMEM)
    res = pl.pallas_call(
        body, in_specs=[vm] + [HBM_SPEC] * ci, out_specs=[vm] + [HBM_SPEC] * co,
        out_shape=[jax.ShapeDtypeStruct((SMALL_ROWS, 128), F32)] + carry.out_shapes,
        scratch_shapes=[pltpu.VMEM((8, SMALL_ROWS, 128), F32), pltpu.SemaphoreType.DMA((7,)), pltpu.SemaphoreType.DMA((7,))]
        + [pltpu.SemaphoreType.DMA((k,)) for k in carry.n_sems],
        compiler_params=pltpu.CompilerParams(has_side_effects=True), name="allreduce_small")(v, *carry.ins)
    return res[0], list(res[1:])


def _row_block(rows, cols, budget=3 << 19):
    best = 8
    for bm in range(8, rows + 1, 8):
        if rows % bm == 0 and bm * cols * 4 <= budget:
            best = bm
    return best


def _pair_sum(name, place, gs, theirs):
    n = len(gs)
    _, m, c = theirs[0].shape
    bm = _row_block(m, c)

    def body(place_ref, *refs):
        for a_ref, b_ref, o_ref in zip(refs[:n], refs[n:2 * n], refs[2 * n:]):
            o_ref[...] = (a_ref[...].astype(F32) + b_ref[...].astype(F32)).astype(o_ref.dtype)

    spec = pl.BlockSpec((None, bm, c), lambda k, i, pr: (k, i, 0))
    return pl.pallas_call(
        body, out_shape=[jax.ShapeDtypeStruct(theirs[0].shape, BF16)] * n,
        grid_spec=pltpu.PrefetchScalarGridSpec(
            num_scalar_prefetch=1, grid=(NCHIP, m // bm),
            in_specs=[pl.BlockSpec((None, None, bm, c), lambda k, i, pr: (k, pr[0], i, 0))] * n + [spec] * n, out_specs=[spec] * n),
        compiler_params=_params(("parallel", "parallel")), name=name)(place, *gs, *theirs)


def _chip_sum(name, place, parts, slots):
    n, ns = len(parts), len(slots[0])
    _, m, c = parts[0].shape
    bm = _row_block(m, c)

    def body(place_ref, *refs):
        for t in range(n):
            acc = refs[t][...].astype(F32)
            for s_ref in refs[n + t * ns:n + (t + 1) * ns]:
                for i in range(s_ref.shape[0]):
                    acc = acc + s_ref[i].astype(F32)
            refs[n + n * ns + t][...] = acc

    half = pl.BlockSpec((bm, c), lambda i, pr: (i, 0))
    return pl.pallas_call(
        body, out_shape=[jax.ShapeDtypeStruct((m, c), F32)] * n,
        grid_spec=pltpu.PrefetchScalarGridSpec(
            num_scalar_prefetch=1, grid=(m // bm,),
            in_specs=[pl.BlockSpec((None, bm, c), lambda i, pr: (pr[1], i, 0))] * n
            + [pl.BlockSpec((s.shape[0], bm, c), lambda i, pr: (0, i, 0)) for group in slots for s in group],
            out_specs=[half] * n),
        compiler_params=_params(("parallel",)), name=name)(place, *parts, *[s for group in slots for s in group])


def _adamw(name, place, tensors, layer=0, into=None):
    n = len(tensors)
    lead, rows, cols = tensors[0][0].shape
    bm = _row_block(rows // 2, cols, budget=768 * 1024 // n)
    per_half = rows // 2 // bm
    c1 = 1.0 - ADAM_B1 ** ADAM_STEP
    c2 = 1.0 - ADAM_B2 ** ADAM_STEP

    def body(place_ref, *refs):
        outs = refs[len(refs) - 4 * n:]
        for t in range(n):
            w_ref, ga_ref, gb_ref, m_ref, v_ref = refs[5 * t:5 * t + 5]
            go_ref, d_ref, mo_ref, vo_ref = outs[4 * t:4 * t + 4]
            g = jnp.where(pl.program_id(0) // per_half == place_ref[0], ga_ref[...], gb_ref[...])
            mn = ADAM_B1 * m_ref[...] + (1.0 - ADAM_B1) * g
            vn = ADAM_B2 * v_ref[...] + (1.0 - ADAM_B2) * (g * g)
            go_ref[...] = g
            mo_ref[...] = mn
            vo_ref[...] = vn
            d_ref[...] = -ADAM_LR * ((mn / c1) / (jnp.sqrt(vn / c2) + ADAM_EPS) + ADAM_WD * w_ref[...])

    spec = pl.BlockSpec((None, bm, cols), lambda i, pr: (layer, i, 0))

    def half_spec(mine):
        def index(i, pr):
            first = (pr[0] == 0) == mine
            park = jnp.where(first, per_half - 1, 0)
            return jnp.where((i < per_half) == first, i % per_half, park), 0
        return pl.BlockSpec((bm, cols), index)
    sh = jax.ShapeDtypeStruct((lead, rows, cols), F32)
    prev = [] if into is None else [a for res in into for a in res]
    res = pl.pallas_call(
        body, out_shape=[sh] * (4 * n), input_output_aliases={1 + 5 * n + k: k for k in range(len(prev))},
        grid_spec=pltpu.PrefetchScalarGridSpec(
            num_scalar_prefetch=1, grid=(rows // bm,),
            in_specs=[spec, half_spec(True), half_spec(False), spec, spec] * n + [pl.BlockSpec(memory_space=pl.ANY)] * len(prev),
            out_specs=[spec] * (4 * n)),
        compiler_params=_params(("parallel",)), name=name)(place, *[a for t in tensors for a in t], *prev)
    return [res[4 * t:4 * t + 4] for t in range(n)]


def _pack_small(norms, rpb, last=None):
    flat = jnp.concatenate([a.reshape(-1) for a in norms] + [rpb.reshape(-1)])
    flat = jnp.pad(flat, (0, SMALL_ROWS * 128 - flat.shape[0]))
    if last is not None:
        flat = lax.dynamic_update_slice(flat, last.reshape(1), (flat.shape[0] - 1,))
    return flat.reshape(SMALL_ROWS, 128)


def _unpack_small(p):
    flat = p.reshape(-1)
    norms = [flat[i * 2 * DM:(i + 1) * 2 * DM].reshape(2, DM) for i in range(4)]
    rpb = flat[8 * DM:8 * DM + NH * 15 * 31].reshape(1, NH, 15, 31)
    return norms, rpb


FFN_NAMES = ("ffn_w_gate", "ffn_w_up", "ffn_w_down")
L0_FFN = tuple((n, 0) for n in FFN_NAMES)
L1_FFN = tuple((n, 1) for n in FFN_NAMES)
NA_KEYS = (("na_w_qkv", 0), ("na_w_o", 0))
DIL_KEYS = (("dil_w_qkv", 0), ("dil_w_o", 0))
ALL_PEERS, NEIGHBOURS, DIAGONAL = (0, 1, 2), (0, 1), (2,)


class _Exchange:
    GATHERS = {"na_bias_tiles": NA_KEYS, "l0_qkv": L0_FFN[:1], "na_fwd": L0_FFN[1:], "l0_ffn_fwd": DIL_KEYS[:1], "dil_fwd": L1_FFN + DIL_KEYS[1:]}
    PAIRS = {"l1_proj_bwd": L1_FFN, "l1_dh": DIL_KEYS, "l0_proj_bwd": L0_FFN}
    EXCHANGES = {"dil_bwd": [(k, ALL_PEERS) for k in L1_FFN],
                 "l0_ffn_bwd": [(DIL_KEYS[0], NEIGHBOURS), (DIL_KEYS[1], ALL_PEERS)],
                 "na_bwd": [(k, ALL_PEERS) for k in L0_FFN] + [(DIL_KEYS[0], DIAGONAL)],
                 "l0_dh": [(k, NEIGHBOURS) for k in NA_KEYS],
                 "allreduce_small": [(k, DIAGONAL) for k in NA_KEYS]}
    SHARES = {"l1_dwqkv": L1_FFN, "l0_dwqkv": L0_FFN + DIL_KEYS}

    def __init__(self, shards):
        self.chip = 2 * lax.axis_index("x") + lax.axis_index("y")
        self.place = jnp.stack([lax.axis_index("c"), self.chip]).astype(jnp.int32)
        self.own = {k: s.reshape(2, s.shape[0] // 2, s.shape[1]).astype(BF16) for k, s in shards.items()}
        self.gathered, self.mine, self.parts, self.slots, self.full, self.other = {}, {}, {}, {}, {}, {}

    def _take(self, keys, landed):
        for k, gw in zip(keys, landed):
            self.gathered[k] = lax.dynamic_update_slice(gw, self.own[k][None], (self.chip, 0, 0, 0))

    def _sum(self, items, landed):
        runs = []
        for (k, peers), s in zip(items, landed):
            got = self.slots.setdefault(k, {})
            got[peers] = s
            if sum(len(p) for p in got) == len(ALL_PEERS):
                like = (self.parts[k].shape, tuple(sorted(got)))
                if runs and runs[-1][0] == like:
                    runs[-1][1].append(k)
                else:
                    runs.append((like, [k]))
        for (_, split), ks in runs:
            sums = _chip_sum(f"chip_sum_{ks[0][0]}_{ks[0][1]}", self.place, [self.parts[k] for k in ks],
                             [[self.slots[k][p] for p in split] for k in ks])
            self.full.update(zip(ks, sums))

    def weight(self, key):
        g = self.gathered[key]
        return g.reshape(NCHIP, 2 * g.shape[2], g.shape[3])

    def _pair_sums(self, keys, theirs):
        runs = []
        for k, t in zip(keys, theirs):
            if runs and runs[-1][0][1].shape == t.shape:
                runs[-1].append((k, t))
            else:
                runs.append([(k, t)])
        for run in runs:
            ks = [k for k, _ in run]
            sums = _pair_sum(f"pair_sum_{ks[0][0]}_{ks[0][1]}", self.place, [self.mine[k] for k in ks], [t for _, t in run])
            self.parts.update(zip(ks, sums))

    def carry(self, tag):
        if tag in self.GATHERS:
            return _gather_copies([self.own[k] for k in self.GATHERS[tag]])
        if tag in self.PAIRS:
            return _pair_exchange_copies([self.mine[k] for k in self.PAIRS[tag]])
        if tag in self.EXCHANGES:
            return _chip_exchange_copies([(self.parts[k], peers) for k, peers in self.EXCHANGES[tag]])
        if tag in self.SHARES:
            return _pair_share_copies([self.full[k] for k in self.SHARES[tag]])
        return None

    def carried(self, tag, landed):
        if tag in self.GATHERS:
            self._take(self.GATHERS[tag], landed)
        elif tag in self.PAIRS:
            self._pair_sums(self.PAIRS[tag], landed)
        elif tag in self.EXCHANGES:
            self._sum(self.EXCHANGES[tag], landed)
        elif tag in self.SHARES:
            self.other.update(zip(self.SHARES[tag], landed))

    def grads(self, tag, dw):
        for k, g in dw.items():
            self.mine[k] = g.reshape(NCHIP, 2, -1, g.shape[-1])
        if tag == "l0_mix":
            keys = tuple(dw)
            self._pair_sums(keys, _run_carried("grad_pair_exchange_last", _pair_exchange_copies([self.mine[k] for k in keys])))

    def finish(self):
        rest = tuple(k for k in self.full if k not in self.other)
        self.other.update(zip(rest, _run_carried("grad_pair_share_last", _pair_share_copies([self.full[k] for k in rest]))))
        return {k: (self.full[k], self.other[k]) for k in self.full}


def kernel(x, norm_mix_pre, norm_mix_post, norm_ffn_pre, norm_ffn_post, na_w_qkv, na_w_o, na_rpb, dil_w_qkv, dil_w_o, ffn_w_gate, ffn_w_up, ffn_w_down, loss_target, m_norm_mix_pre, m_norm_mix_post, m_norm_ffn_pre, m_norm_ffn_post, m_na_w_qkv, m_na_w_o, m_na_rpb, m_dil_w_qkv, m_dil_w_o, m_ffn_w_gate, m_ffn_w_up, m_ffn_w_down, v_norm_mix_pre, v_norm_mix_post, v_norm_ffn_pre, v_norm_ffn_post, v_na_w_qkv, v_na_w_o, v_na_rpb, v_dil_w_qkv, v_dil_w_o, v_ffn_w_gate, v_ffn_w_up, v_ffn_w_down):
    tr = lambda a: jnp.swapaxes(a, 1, 2)
    weights = {"na_w_qkv": na_w_qkv, "na_w_o": na_w_o, "dil_w_qkv": dil_w_qkv, "dil_w_o": dil_w_o,
               "ffn_w_gate": tr(ffn_w_gate), "ffn_w_up": tr(ffn_w_up), "ffn_w_down": ffn_w_down}
    m_in = {"na_w_qkv": m_na_w_qkv, "na_w_o": m_na_w_o, "dil_w_qkv": m_dil_w_qkv, "dil_w_o": m_dil_w_o,
            "ffn_w_gate": tr(m_ffn_w_gate), "ffn_w_up": tr(m_ffn_w_up), "ffn_w_down": m_ffn_w_down}
    v_in = {"na_w_qkv": v_na_w_qkv, "na_w_o": v_na_w_o, "dil_w_qkv": v_dil_w_qkv, "dil_w_o": v_dil_w_o,
            "ffn_w_gate": tr(v_ffn_w_gate), "ffn_w_up": tr(v_ffn_w_up), "ffn_w_down": v_ffn_w_down}

    ex = _Exchange({(n, l): weights[n][l] for n in weights for l in range(weights[n].shape[0])})
    norms = (norm_mix_pre, norm_mix_post, norm_ffn_pre, norm_ffn_post)
    loss_row, dx, dnorms, d_rpb = _local_step(x[0], loss_target[0], norms, na_rpb[0], ex)
    small, sent = _allreduce_small(_pack_small(dnorms, d_rpb, last=loss_row[0, 0]), ex.carry("allreduce_small"))
    ex.carried("allreduce_small", sent)
    full = ex.finish()
    loss = small[SMALL_ROWS - 1, 127]

    out_g, out_d, out_m, out_v = {}, {}, {}, {}
    operands = lambda n, l: (weights[n], *full[(n, l)], m_in[n], v_in[n])
    results = {n: _adamw(f"adamw_{n}", ex.place, [operands(n, 0)])[0] for n in weights if n not in FFN_NAMES}
    ffn = None
    for l in range(2):
        ffn = _adamw(f"adamw_ffn_{l}", ex.place, [operands(n, l) for n in FFN_NAMES], l, ffn)
    results.update(zip(FFN_NAMES, ffn))
    for n, res in results.items():
        if n in ("ffn_w_gate", "ffn_w_up"):
            res = [tr(r) for r in res]
        out_g[n], out_d[n], out_m[n], out_v[n] = res
    sm_names = ("norm_mix_pre", "norm_mix_post", "norm_ffn_pre", "norm_ffn_post", "na_rpb")
    sm = _adamw("adamw_small", jnp.zeros((2,), jnp.int32),
                [(_pack_small(norms, na_rpb)[None], small[:SMALL_ROWS // 2], small[SMALL_ROWS // 2:],
                  _pack_small((m_norm_mix_pre, m_norm_mix_post, m_norm_ffn_pre, m_norm_ffn_post), m_na_rpb)[None],
                  _pack_small((v_norm_mix_pre, v_norm_mix_post, v_norm_ffn_pre, v_norm_ffn_post), v_na_rpb)[None])])[0]
    for res, dst in zip(sm, (out_g, out_d, out_m, out_v)):
        ns, rp = _unpack_small(res)
        for n, a in zip(sm_names, ns + [rp]):
            dst[n] = a

    order = ("norm_mix_pre", "norm_mix_post", "norm_ffn_pre", "norm_ffn_post", "na_w_qkv", "na_w_o", "na_rpb", "dil_w_qkv", "dil_w_o",
             "ffn_w_gate", "ffn_w_up", "ffn_w_down")
    return (loss, dx[None], *[out_g[n] for n in order], *[out_d[n] for n in order], *[out_m[n] for n in order], *[out_v[n] for n in order])
```

```python
import functools

import numpy as np
import jax
import jax.numpy as jnp
from jax import lax
from jax.experimental import pallas as pl
from jax.experimental.pallas import tpu as pltpu

F32 = jnp.float32
BF16 = jnp.bfloat16

SEQ = 2048
DM = 1024
NH = 16
HD = 64
DFF = 2816
NCHIP = 4
FSH = DFF // NCHIP
GRID_W = 64
NA_QROWS = 4
NA_QB = NA_QROWS * GRID_W
NA_WROWS = 12
NA_WIN = NA_WROWS * GRID_W
DIL = (1, 4, 16)
DIL_QB = 256
DIL_WIN = DIL_QB + 128
DIL_RADIUS = 64
RMS_EPS = 1e-6
NEG = -1e30
QSCALE = HD ** -0.5
CH = 256
MESH = pl.DeviceIdType.MESH

ADAM_LR, ADAM_B1, ADAM_B2, ADAM_EPS, ADAM_WD, ADAM_STEP = 0.001, 0.9, 0.999, 1e-08, 0.01, 10

VMEM_LIMIT = 56 * 1024 * 1024

_NN = (((1,), (0,)), ((), ()))
_NT = (((1,), (1,)), ((), ()))
_TN = (((0,), (0,)), ((), ()))


def _params(sem):
    return pltpu.CompilerParams(dimension_semantics=sem, vmem_limit_bytes=VMEM_LIMIT)


def _matmul(name, pairs, grid, out_shape, out_spec, acc_shape, carrying=False, carry=None):
    nk = grid[-1]
    npair = len(pairs)
    n_in = 2 * npair

    def body(*refs):
        ins, o_ref = refs[:2 * npair], refs[n_in]
        part = None
        for p in range(npair):
            d = lax.dot_general(ins[2 * p][...].astype(BF16), ins[2 * p + 1][...].astype(BF16), pairs[p][4],
                                preferred_element_type=F32)
            part = d if part is None else part + d
        if nk == 1:
            o_ref[...] = part.astype(o_ref.dtype)
        else:
            acc_ref = refs[n_in + 1]
            kk = pl.program_id(len(grid) - 1)

            @pl.when(kk == 0)
            def _():
                acc_ref[...] = part

            @pl.when(kk > 0)
            def _():
                acc_ref[...] += part

            @pl.when(kk == nk - 1)
            def _():
                o_ref[...] = acc_ref[...].astype(o_ref.dtype)

    ops, specs = [], []
    for a, a_spec, b, b_spec, _ in pairs:
        ops += [a, b]
        specs += [a_spec, b_spec]
    (out,), sent = _carrier_call(name, body, grid, specs, [out_spec], [out_shape], [] if nk == 1 else [pltpu.VMEM(acc_shape, F32)], ops, carry)
    return (out, sent) if carrying else out


def _qkv_fwd(name, h_all, w4, carry):
    g_n = h_all.shape[0]
    per = w4.shape[2] // CH
    return _matmul(
        name, [(h_all, pl.BlockSpec((None, SEQ, DM), lambda g, q, k: (g, 0, 0)),
                w4, pl.BlockSpec((None, DM, CH), lambda g, q, k: ((g * 12 + q) // per, 0, (g * 12 + q) % per)), _NN)],
        (g_n, 12, 1), jax.ShapeDtypeStruct((g_n, SEQ, 3 * DM), BF16),
        pl.BlockSpec((None, SEQ, CH), lambda g, q, k: (g, 0, q)), None, carrying=True, carry=carry)


def _qkv_bwd_dh(name, dqkv, w4, carry):
    g_n = dqkv.shape[0]
    per = w4.shape[2] // CH
    tm = SEQ

    def pair(cb):
        chunk = lambda g, t: g * 12 + t * 4 + cb
        return (dqkv, pl.BlockSpec((None, None, tm, CH), lambda g, i, t: (g, t, i, cb)),
                w4, pl.BlockSpec((None, DM, CH), lambda g, i, t: (chunk(g, t) // per, 0, chunk(g, t) % per)), _NT)

    return _matmul(name, [pair(cb) for cb in range(4)], (g_n, SEQ // tm, 3), jax.ShapeDtypeStruct((g_n, SEQ, DM), F32),
                   pl.BlockSpec((None, tm, DM), lambda g, i, t: (g, i, 0)), (tm, DM), carrying=True, carry=carry)


def _qkv_bwd_dw(name, ht_all, dqkv, shard_cols, carry):
    g_n = dqkv.shape[0]
    per = shard_cols // CH
    return _matmul(
        name, [(ht_all, pl.BlockSpec((None, DM, SEQ), lambda qq, k: (qq // 12, 0, 0)),
                dqkv, pl.BlockSpec((None, None, SEQ, CH), lambda qq, k: (qq // 12, (qq % 12) // 4, 0, qq % 4)), _NN)],
        (g_n * 12, 1), jax.ShapeDtypeStruct((NCHIP, DM, shard_cols), BF16),
        pl.BlockSpec((None, DM, CH), lambda qq, k: (qq // per, 0, qq % per)), None, carrying=True, carry=carry)


def _proj_fwd(name, o, wo, x, g):
    tm = 512

    def body(o_ref, w_ref, x_ref, g_ref, xn_ref, u_ref):
        u = jnp.dot(o_ref[...], w_ref[...], preferred_element_type=F32)
        u_ref[...] = u
        r = lax.rsqrt(jnp.mean(u * u, axis=-1, keepdims=True) + RMS_EPS)
        xn_ref[...] = x_ref[...] + u * r * g_ref[...]

    rows = pl.BlockSpec((tm, DM), lambda i: (i, 0))
    sh = jax.ShapeDtypeStruct((SEQ, DM), F32)
    return pl.pallas_call(
        body, grid=(SEQ // tm,), in_specs=[rows, pl.BlockSpec((DM, DM), lambda i: (0, 0)), rows, pl.BlockSpec((1, DM), lambda i: (0, 0))],
        out_specs=[rows, rows], out_shape=[sh, sh], compiler_params=_params(("parallel",)), name=name)(o, wo, x, g)


def _proj_bwd(name, dy, u, g, wo, dtype, carry):
    tm = 512

    def body(dy_ref, u_ref, g_ref, w_ref, do_ref, du_ref, dg_ref):
        dy = dy_ref[...]
        u = u_ref[...]
        r = lax.rsqrt(jnp.mean(u * u, axis=-1, keepdims=True) + RMS_EPS)
        yh = u * r
        t = dy * g_ref[...]
        du = (r * (t - yh * jnp.mean(t * yh, axis=-1, keepdims=True))).astype(BF16)
        du_ref[...] = du
        do_ref[...] = lax.dot_general(du, w_ref[...], _NT, preferred_element_type=F32).astype(do_ref.dtype)

        @pl.when(pl.program_id(0) == 0)
        def _():
            dg_ref[...] = jnp.zeros_like(dg_ref)

        dg_ref[...] += jnp.sum(dy * yh, axis=0, keepdims=True)

    rows = pl.BlockSpec((tm, DM), lambda i: (i, 0))
    vec = pl.BlockSpec((1, DM), lambda i: (0, 0))
    return _carrier_call(
        name, body, (SEQ // tm,), [rows, rows, vec, pl.BlockSpec((DM, DM), lambda i: (0, 0))], [rows, rows, vec],
        [jax.ShapeDtypeStruct((SEQ, DM), dtype), jax.ShapeDtypeStruct((SEQ, DM), BF16), jax.ShapeDtypeStruct((1, DM), F32)],
        [], (dy, u, g, wo), carry)


def _proj_bwd_dw(name, o, du):
    tn = 512
    return _matmul(
        name, [(o, pl.BlockSpec((SEQ, DM), lambda j, k: (0, 0)), du, pl.BlockSpec((SEQ, tn), lambda j, k: (0, j)), _TN)],
        (DM // tn, 1), jax.ShapeDtypeStruct((DM, DM), BF16), pl.BlockSpec((DM, tn), lambda j, k: (0, j)), None)


def _ffn_wspec(index_map):
    return pl.BlockSpec((None, FSH, DM), index_map)


def _ffn_bwd_dw(name, a4, b):
    return _matmul(
        name, [(a4, pl.BlockSpec((None, SEQ, FSH), lambda s, k: (s, 0, 0)), b, pl.BlockSpec((SEQ, DM), lambda s, k: (0, 0)), _TN)],
        (NCHIP, 1), jax.ShapeDtypeStruct((NCHIP, FSH, DM), BF16), _ffn_wspec(lambda s, k: (s, 0, 0)), None)


ROWS = 256


def _row_spec():
    return pl.BlockSpec((ROWS, DM), lambda i: (i, 0))


def _vec_spec():
    return pl.BlockSpec((1, DM), lambda i: (0, 0))


def _rms_fwd(name, x, g, dtype=BF16):
    def body(x_ref, g_ref, o_ref):
        x = x_ref[...]
        r = lax.rsqrt(jnp.mean(x * x, axis=-1, keepdims=True) + RMS_EPS)
        o_ref[...] = (x * r * g_ref[...]).astype(o_ref.dtype)

    return pl.pallas_call(body, grid=(SEQ // ROWS,), in_specs=[_row_spec(), _vec_spec()], out_specs=_row_spec(),
                          out_shape=jax.ShapeDtypeStruct((SEQ, DM), dtype), compiler_params=_params(("parallel",)), name=name)(x, g)


def _rms_fwd_both(name, x, g):
    def body(x_ref, g_ref, o_ref, t_ref):
        x = x_ref[...]
        r = lax.rsqrt(jnp.mean(x * x, axis=-1, keepdims=True) + RMS_EPS)
        h = x * r * g_ref[...]
        o_ref[...] = h.astype(o_ref.dtype)
        t_ref[...] = h.T.astype(t_ref.dtype)

    return pl.pallas_call(
        body, grid=(SEQ // ROWS,), in_specs=[_row_spec(), _vec_spec()], out_specs=[_row_spec(), pl.BlockSpec((DM, ROWS), lambda i: (0, i))],
        out_shape=[jax.ShapeDtypeStruct((SEQ, DM), BF16), jax.ShapeDtypeStruct((DM, SEQ), BF16)],
        compiler_params=_params(("parallel",)), name=name)(x, g)


def _norm_bwd(name, dy, u, g, res, carry):
    def body(dy_ref, u_ref, g_ref, res_ref, du_ref, dg_ref):
        dy = dy_ref[...]
        u = u_ref[...]
        r = lax.rsqrt(jnp.mean(u * u, axis=-1, keepdims=True) + RMS_EPS)
        yh = u * r
        t = dy * g_ref[...]
        du_ref[...] = r * (t - yh * jnp.mean(t * yh, axis=-1, keepdims=True)) + res_ref[...]

        @pl.when(pl.program_id(0) == 0)
        def _():
            dg_ref[...] = jnp.zeros_like(dg_ref)

        dg_ref[...] += jnp.sum(dy * yh, axis=0, keepdims=True)

    return _carrier_call(
        name, body, (SEQ // ROWS,), [_row_spec(), _row_spec(), _vec_spec(), _row_spec()], [_row_spec(), _vec_spec()],
        [jax.ShapeDtypeStruct((SEQ, DM), F32), jax.ShapeDtypeStruct((1, DM), F32)], [], (dy, u, g, res), carry)


def _loss_grad(name, y, t):
    def body(y_ref, t_ref, dy_ref, l_ref):
        e = y_ref[...] - t_ref[...]
        dy_ref[...] = e * (1.0 / DM)

        @pl.when(pl.program_id(0) == 0)
        def _():
            l_ref[...] = jnp.zeros_like(l_ref)

        l_ref[...] += jnp.sum(e * e) * (0.5 / DM)

    return pl.pallas_call(
        body, grid=(SEQ // ROWS,), in_specs=[_row_spec(), _row_spec()],
        out_specs=[_row_spec(), pl.BlockSpec((1, 128), lambda i: (0, 0))],
        out_shape=[jax.ShapeDtypeStruct((SEQ, DM), F32), jax.ShapeDtypeStruct((1, 128), F32)],
        compiler_params=_params(("arbitrary",)), name=name)(y, t)


HBM_SPEC = pl.BlockSpec(memory_space=pltpu.HBM)


class _Carried:
    def __init__(self, ins, out_shapes, n_sems, issue, drain):
        self.ins, self.out_shapes, self.n_sems, self.issue, self.drain = list(ins), list(out_shapes), tuple(n_sems), issue, drain


def _carrier_call(name, body, grid, in_specs, out_specs, out_shape, scratch_shapes, operands, carry):
    n_in, n_out, n_scr = len(in_specs), len(out_specs), len(scratch_shapes)
    if carry is None:
        res = pl.pallas_call(body, grid=grid, in_specs=in_specs, out_specs=out_specs, out_shape=out_shape, scratch_shapes=scratch_shapes,
                             compiler_params=_params(("arbitrary",) * len(grid)), name=name)(*operands)
        return list(res), []
    ci, co = len(carry.ins), len(carry.out_shapes)

    def wrapped(*refs):
        ins, cins = refs[:n_in], refs[n_in:n_in + ci]
        outs, couts = refs[n_in + ci:n_in + ci + n_out], refs[n_in + ci + n_out:n_in + ci + n_out + co]
        scr, sems = refs[n_in + ci + n_out + co:n_in + ci + n_out + co + n_scr], refs[n_in + ci + n_out + co + n_scr:]
        first = functools.reduce(jnp.logical_and, [pl.program_id(a) == 0 for a in range(len(grid))])
        last = functools.reduce(jnp.logical_and, [pl.program_id(a) == grid[a] - 1 for a in range(len(grid))])

        @pl.when(first)
        def _():
            carry.issue(cins, couts, sems)

        body(*ins, *outs, *scr)

        @pl.when(last)
        def _():
            carry.drain(cins, couts, sems)

    res = pl.pallas_call(
        wrapped, grid=grid, in_specs=list(in_specs) + [HBM_SPEC] * ci, out_specs=list(out_specs) + [HBM_SPEC] * co,
        out_shape=list(out_shape) + carry.out_shapes,
        scratch_shapes=list(scratch_shapes) + [pltpu.SemaphoreType.DMA((k,)) for k in carry.n_sems],
        compiler_params=pltpu.CompilerParams(dimension_semantics=("arbitrary",) * len(grid), vmem_limit_bytes=VMEM_LIMIT, has_side_effects=True),
        name=name)(*operands, *carry.ins)
    return list(res[:n_out]), list(res[n_out:])


def _run_carried(name, carry):
    def body(*refs):
        ci, co = len(carry.ins), len(carry.out_shapes)
        carry.issue(refs[:ci], refs[ci:ci + co], refs[ci + co:])
        carry.drain(refs[:ci], refs[ci:ci + co], refs[ci + co:])

    return pl.pallas_call(
        body, in_specs=[HBM_SPEC] * len(carry.ins), out_specs=[HBM_SPEC] * len(carry.out_shapes), out_shape=carry.out_shapes,
        scratch_shapes=[pltpu.SemaphoreType.DMA((k,)) for k in carry.n_sems],
        compiler_params=pltpu.CompilerParams(has_side_effects=True), name=name)(*carry.ins)


NA_BLOCKS = SEQ // NA_QB
NA_ROWS_TOTAL = SEQ // GRID_W
NA_CLASSES = ((0, 0), (8, 4), (NA_ROWS_TOTAL - NA_QROWS, NA_ROWS_TOTAL - NA_WROWS))


def _na_pairs(i0, ws):
    out = []
    for qi in range(NA_QROWS):
        i = i0 + qi
        rs = min(max(i - 4, 0), NA_ROWS_TOTAL - 8)
        for kr in range(NA_WROWS):
            r = ws + kr
            if rs <= r < rs + 8:
                out.append((qi, kr, r - i + 7))
    return out


def _diag_onehot():
    qc, kc = np.meshgrid(np.arange(GRID_W), np.arange(GRID_W), indexing="ij")
    e = np.zeros((GRID_W * GRID_W, 128), np.float32)
    j = (kc - qc + 15).reshape(-1)
    ok = (j >= 0) & (j <= 30)
    e[np.arange(GRID_W * GRID_W)[ok], j[ok]] = 1.0
    return jnp.asarray(e)


def _rpb_expand(rpb):
    r2 = jnp.pad(rpb.reshape(NH * 15, 31), ((0, 0), (0, 128 - 31)))

    def body(r_ref, e_ref, o_ref):
        o_ref[...] = lax.dot_general(r_ref[...], e_ref[...], _NT, preferred_element_type=F32, precision=lax.Precision.HIGHEST)

    out = pl.pallas_call(body, out_shape=jax.ShapeDtypeStruct((NH * 15, GRID_W * GRID_W), F32), name="rpb_expand",
                         compiler_params=pltpu.CompilerParams(vmem_limit_bytes=VMEM_LIMIT))(r2, _diag_onehot())
    return out.reshape(NH, 15, GRID_W, GRID_W)


def _na_bias_tiles(rpb, carry):
    def body(b_ref, o_ref):
        qc = lax.broadcasted_iota(jnp.int32, (GRID_W, GRID_W), 0)
        kc = lax.broadcasted_iota(jnp.int32, (GRID_W, GRID_W), 1)
        first = jnp.clip(qc - 8, 0, GRID_W - 16)
        in_window = (kc >= first) & (kc < first + 16)
        neg = jnp.full((GRID_W, GRID_W), NEG, F32)
        for cls, (i0, ws) in enumerate(NA_CLASSES):
            @pl.when(pl.program_id(0) == cls)
            def _(i0=i0, ws=ws):
                pairs = {(qi, kr): dr for qi, kr, dr in _na_pairs(i0, ws)}
                masked = {dr: jnp.where(in_window, b_ref[dr], NEG) for dr in sorted(set(pairs.values()))}
                for qi in range(NA_QROWS):
                    for k2 in range(NA_WROWS // 2):
                        blocks = [masked[pairs[(qi, kr)]] if (qi, kr) in pairs else neg for kr in (2 * k2, 2 * k2 + 1)]
                        o_ref[qi * GRID_W:(qi + 1) * GRID_W, k2 * 128:(k2 + 1) * 128] = jnp.concatenate(blocks, axis=1)

    (tiles,), sent = _carrier_call(
        "na_bias_tiles", body, (3, NH), [pl.BlockSpec((None, 15, GRID_W, GRID_W), lambda c, h: (h, 0, 0, 0))],
        [pl.BlockSpec((None, None, NA_QB, NA_WIN), lambda c, h: (c, h, 0, 0))], [jax.ShapeDtypeStruct((3, NH, NA_QB, NA_WIN), F32)],
        [], (_rpb_expand(rpb),), carry)
    return tiles, sent


def _na_cls(b):
    return jnp.where(b == 0, 0, jnp.where(b == NA_BLOCKS - 1, 2, 1))


def _na_start(b):
    return pl.multiple_of(jnp.clip(b * NA_QROWS - 4, 0, NA_ROWS_TOTAL - NA_WROWS) * GRID_W, GRID_W)


NA_FWD_HPS = 8
NA_BWD_HPS = 4


def _na_in_specs(hps):
    lw = hps * HD
    nlw = DM // lw
    return [pl.BlockSpec((NA_QB, lw), lambda hp, b: (b, hp)),
            pl.BlockSpec((SEQ, lw), lambda hp, b: (0, nlw + hp)),
            pl.BlockSpec((SEQ, lw), lambda hp, b: (0, 2 * nlw + hp)),
            pl.BlockSpec((None, hps, NA_QB, NA_WIN), lambda hp, b: (_na_cls(b), hp, 0, 0))]


def _na_fwd(qkv, bias, carry):
    lw = NA_FWD_HPS * HD

    def body(q_ref, k_ref, v_ref, b_ref, o_ref):
        start = _na_start(pl.program_id(1))
        q = q_ref[...]
        kw = k_ref[pl.ds(start, NA_WIN), :]
        vw = v_ref[pl.ds(start, NA_WIN), :]
        outs = []
        for hh in range(NA_FWD_HPS):
            sl = slice(hh * HD, (hh + 1) * HD)
            s = lax.dot_general(q[:, sl] * QSCALE, kw[:, sl], _NT, preferred_element_type=F32) + b_ref[hh]
            p = jnp.exp(s - jnp.max(s, axis=-1, keepdims=True))
            l = jnp.sum(p, axis=-1, keepdims=True)
            outs.append(jnp.dot(p.astype(BF16), vw[:, sl], preferred_element_type=F32) / l)
        o_ref[...] = jnp.concatenate(outs, axis=1).astype(o_ref.dtype)

    (o,), sent = _carrier_call(
        "na_fwd", body, (NH // NA_FWD_HPS, NA_BLOCKS), _na_in_specs(NA_FWD_HPS), [pl.BlockSpec((NA_QB, lw), lambda hp, b: (b, hp))],
        [jax.ShapeDtypeStruct((SEQ, DM), BF16)], [], (qkv, qkv, qkv, bias), carry)
    return o, sent


def _na_bwd(qkv, bias, do, carry):
    lw = NA_BWD_HPS * HD

    def body(q_ref, k_ref, v_ref, b_ref, do_ref, dqkv_ref, z_ref, dk_acc, dv_acc):
        blk = pl.program_id(1)

        @pl.when(blk == 0)
        def _():
            dk_acc[...] = jnp.zeros_like(dk_acc)
            dv_acc[...] = jnp.zeros_like(dv_acc)
            z_ref[...] = jnp.zeros_like(z_ref)

        start = _na_start(blk)
        q = q_ref[...]
        do = do_ref[...]
        kw = k_ref[pl.ds(start, NA_WIN), :]
        vw = v_ref[pl.ds(start, NA_WIN), :]
        dqs, dks, dvs, dss = [], [], [], []
        for hh in range(NA_BWD_HPS):
            sl = slice(hh * HD, (hh + 1) * HD)
            qh = q[:, sl] * QSCALE
            s = lax.dot_general(qh, kw[:, sl], _NT, preferred_element_type=F32) + b_ref[hh]
            p = jnp.exp(s - jnp.max(s, axis=-1, keepdims=True))
            p = p / jnp.sum(p, axis=-1, keepdims=True)
            dp = lax.dot_general(do[:, sl], vw[:, sl], _NT, preferred_element_type=F32)
            ds = p * (dp - jnp.sum(p * dp, axis=-1, keepdims=True))
            dsb = ds.astype(BF16)
            dqs.append(jnp.dot(dsb, kw[:, sl], preferred_element_type=F32) * QSCALE)
            dks.append(lax.dot_general(qh, dsb, _TN, preferred_element_type=F32).T)
            dvs.append(lax.dot_general(do[:, sl], p.astype(BF16), _TN, preferred_element_type=F32).T)
            dss.append(ds)
        for cls, (i0, ws) in enumerate(NA_CLASSES):
            @pl.when(_na_cls(blk) == cls)
            def _(i0=i0, ws=ws):
                for hh, ds in enumerate(dss):
                    for qi, kr, dr in _na_pairs(i0, ws):
                        z_ref[hh, dr * GRID_W:(dr + 1) * GRID_W, :] += ds[qi * GRID_W:(qi + 1) * GRID_W, kr * GRID_W:(kr + 1) * GRID_W]
        dqkv_ref[0, pl.ds(pl.multiple_of(blk * NA_QB, NA_QB), NA_QB), :] = jnp.concatenate(dqs, axis=1).astype(dqkv_ref.dtype)
        dk_acc[pl.ds(start, NA_WIN), :] += jnp.concatenate(dks, axis=1)
        dv_acc[pl.ds(start, NA_WIN), :] += jnp.concatenate(dvs, axis=1)

        @pl.when(blk == NA_BLOCKS - 1)
        def _():
            dqkv_ref[1] = dk_acc[...].astype(dqkv_ref.dtype)
            dqkv_ref[2] = dv_acc[...].astype(dqkv_ref.dtype)

    (dqkv, z), sent = _carrier_call(
        "na_bwd", body, (NH // NA_BWD_HPS, NA_BLOCKS),
        _na_in_specs(NA_BWD_HPS) + [pl.BlockSpec((NA_QB, lw), lambda hp, b: (b, hp))],
        [pl.BlockSpec((3, SEQ, lw), lambda hp, b: (0, 0, hp)), pl.BlockSpec((NA_BWD_HPS, 15 * GRID_W, GRID_W), lambda hp, b: (hp, 0, 0))],
        [jax.ShapeDtypeStruct((3, SEQ, DM), BF16), jax.ShapeDtypeStruct((NH, 15 * GRID_W, GRID_W), F32)],
        [pltpu.VMEM((SEQ, lw), F32), pltpu.VMEM((SEQ, lw), F32)], (qkv, qkv, qkv, bias, do), carry)
    return dqkv, z, sent


def _rpb_grad(z):
    z2 = z.reshape(NH * 15, GRID_W * GRID_W)

    def body(z_ref, e_ref, o_ref):
        o_ref[...] = jnp.dot(z_ref[...], e_ref[...], preferred_element_type=F32, precision=lax.Precision.HIGHEST)

    out = pl.pallas_call(body, out_shape=jax.ShapeDtypeStruct((NH * 15, 128), F32), name="rpb_grad",
                         compiler_params=pltpu.CompilerParams(vmem_limit_bytes=VMEM_LIMIT))(z2, _diag_onehot())
    return out[:, :31].reshape(NH, 15, 31)


DIL_BLOCKS = SEQ // DIL_QB
DIL_HPS = 8
DIL_LW = DIL_HPS * HD
DIL_NLW = DM // DIL_LW


COLS = 128


def _col_spec():
    return pl.BlockSpec((SEQ, COLS), lambda j: (0, j))


def _grp_spec():
    return pl.BlockSpec((3, SEQ, COLS), lambda j: (0, 0, j))


def _store_group_order(dst_ref, src_ref):
    for g, d in enumerate(DIL):
        n = SEQ // d
        for r in range(d):
            dst_ref[g, r * n:(r + 1) * n, :] = src_ref[pl.ds(r, n, stride=d), :].astype(dst_ref.dtype)


def _store_token_order(dst_ref, src_ref, g):
    d = DIL[g]
    n = SEQ // d
    for r in range(d):
        dst_ref[pl.ds(r, n, stride=d), :] = src_ref[g, r * n:(r + 1) * n, :].astype(dst_ref.dtype)


def _to_groups(name, a):
    def body(a_ref, o_ref, t_ref):
        _store_group_order(o_ref, a_ref)
        for g in range(3):
            t_ref[g] = o_ref[g].astype(F32).T.astype(t_ref.dtype)

    return pl.pallas_call(
        body, grid=(DM // COLS,), in_specs=[_col_spec()], out_specs=[_grp_spec(), pl.BlockSpec((3, COLS, SEQ), lambda j: (0, j, 0))],
        out_shape=[jax.ShapeDtypeStruct((3, SEQ, DM), BF16), jax.ShapeDtypeStruct((3, DM, SEQ), BF16)],
        compiler_params=_params(("parallel",)), name=name)(a)


def _from_groups_sum(name, a):
    def body(a_ref, o_ref, t1, t2):
        _store_token_order(t1, a_ref, 1)
        _store_token_order(t2, a_ref, 2)
        o_ref[...] = (a_ref[0] + t1[...]) + t2[...]

    return pl.pallas_call(body, grid=(DM // COLS,), in_specs=[_grp_spec()], out_specs=_col_spec(),
                          out_shape=jax.ShapeDtypeStruct((SEQ, DM), F32), scratch_shapes=[pltpu.VMEM((SEQ, COLS), F32)] * 2,
                          compiler_params=_params(("parallel",)), name=name)(a)


def _dil_start(b):
    return pl.multiple_of(jnp.clip(b * DIL_QB - DIL_RADIUS, 0, SEQ - DIL_WIN), DIL_RADIUS)


def _dil_neg_dist(g, ii, jj):
    shift = 11 - 2 * g
    dist = jnp.abs(ii - jj)
    valid = (dist <= DIL_RADIUS) & (jnp.right_shift(ii, shift) == jnp.right_shift(jj, shift))
    return jnp.where(valid, -dist.astype(F32), NEG)


def _dil_in_specs():
    return [pl.BlockSpec(memory_space=pltpu.SMEM),
            pl.BlockSpec((None, DIL_QB, DIL_LW), lambda g, hp, b: (g, b, hp)),
            pl.BlockSpec((None, SEQ, DIL_LW), lambda g, hp, b: (g, 0, DIL_NLW + hp)),
            pl.BlockSpec((None, SEQ, DIL_LW), lambda g, hp, b: (g, 0, 2 * DIL_NLW + hp))]


def _dil_fwd(qkv, slopes, carry):
    def body(sl_ref, q_ref, k_ref, v_ref, o_ref, lse_ref):
        g, hp, b = pl.program_id(0), pl.program_id(1), pl.program_id(2)
        start = _dil_start(b)
        neg_dist = _dil_neg_dist(g, b * DIL_QB + lax.broadcasted_iota(jnp.int32, (DIL_QB, DIL_WIN), 0),
                                 start + lax.broadcasted_iota(jnp.int32, (DIL_QB, DIL_WIN), 1))
        dil = jnp.left_shift(1, 2 * g).astype(F32)
        q = q_ref[...]
        kw = k_ref[pl.ds(start, DIL_WIN), :]
        vw = v_ref[pl.ds(start, DIL_WIN), :]
        outs, lses = [], []
        for hh in range(DIL_HPS):
            sl = slice(hh * HD, (hh + 1) * HD)
            s = lax.dot_general(q[:, sl] * QSCALE, kw[:, sl], _NT, preferred_element_type=F32)
            s = s + (sl_ref[hp * DIL_HPS + hh] * dil) * neg_dist
            m = jnp.max(s, axis=-1, keepdims=True)
            p = jnp.exp(s - m)
            l = jnp.sum(p, axis=-1, keepdims=True)
            outs.append(jnp.dot(p.astype(BF16), vw[:, sl], preferred_element_type=F32) / l)
            lses.append(jnp.broadcast_to(m + jnp.log(l), (DIL_QB, HD)))
        o_ref[...] = jnp.concatenate(outs, axis=1).astype(o_ref.dtype)
        lse_ref[...] = jnp.concatenate(lses, axis=1)

    ospec = pl.BlockSpec((None, DIL_QB, DIL_LW), lambda g, hp, b: (g, b, hp))
    (o, lse), sent = _carrier_call(
        "dil_fwd", body, (3, DIL_NLW, DIL_BLOCKS), _dil_in_specs(), [ospec, ospec],
        [jax.ShapeDtypeStruct((3, SEQ, DM), BF16), jax.ShapeDtypeStruct((3, SEQ, DM), F32)], [], (slopes, qkv, qkv, qkv), carry)
    return o, lse, sent


def _dil_merge(o_all, lse_all):
    def body(o_ref, l_ref, out_ref, lse_ref, o1, o2, l1, l2):
        for g, (ot, lt) in ((1, (o1, l1)), (2, (o2, l2))):
            _store_token_order(ot, o_ref, g)
            _store_token_order(lt, l_ref, g)
        la, lb, lc = l_ref[0], l1[...], l2[...]
        m = jnp.maximum(jnp.maximum(la, lb), lc)
        wa, wb, wc = jnp.exp(la - m), jnp.exp(lb - m), jnp.exp(lc - m)
        sw = (wa + wb) + wc
        out_ref[...] = (((wa * o_ref[0].astype(F32) + wb * o1[...]) + wc * o2[...]) / sw).astype(out_ref.dtype)
        lse_ref[...] = m + jnp.log(sw)

    return pl.pallas_call(
        body, grid=(DM // COLS,), in_specs=[_grp_spec(), _grp_spec()], out_specs=[_col_spec(), _col_spec()],
        out_shape=[jax.ShapeDtypeStruct((SEQ, DM), BF16), jax.ShapeDtypeStruct((SEQ, DM), F32)],
        scratch_shapes=[pltpu.VMEM((SEQ, COLS), F32)] * 4, compiler_params=_params(("parallel",)), name="dil_merge")(o_all, lse_all)


def _dil_bwd_prep(do, o, lse):
    heads = COLS // HD

    def body(do_ref, o_ref, lse_ref, dog_ref, ddr_ref, lser_ref, dd, grp):
        prod = do_ref[...] * o_ref[...].astype(F32)
        dd[...] = jnp.concatenate(
            [jnp.broadcast_to(jnp.sum(prod[:, h * HD:(h + 1) * HD], axis=-1, keepdims=True), (SEQ, HD)) for h in range(heads)], axis=1)
        _store_group_order(dog_ref, do_ref)
        for src, dst in ((dd, ddr_ref), (lse_ref, lser_ref)):
            _store_group_order(grp, src)
            for g in range(3):
                t = grp[g].T
                for h in range(heads):
                    dst[g, h] = t[h * HD:h * HD + 8, :]

    rows = jax.ShapeDtypeStruct((3, NH, 8, SEQ), F32)
    rspec = pl.BlockSpec((3, heads, 8, SEQ), lambda j: (0, j, 0, 0))
    return pl.pallas_call(
        body, grid=(DM // COLS,), in_specs=[_col_spec()] * 3, out_specs=[_grp_spec(), rspec, rspec],
        out_shape=[jax.ShapeDtypeStruct((3, SEQ, DM), BF16), rows, rows],
        scratch_shapes=[pltpu.VMEM((SEQ, COLS), F32), pltpu.VMEM((3, SEQ, COLS), F32)],
        compiler_params=_params(("parallel",)), name="dil_bwd_prep")(do, o, lse)


def _dil_bwd(qkv, do, dd, lse, slopes, carry):
    def body(sl_ref, q_ref, k_ref, v_ref, do_ref, dd_ref, lse_ref, dqkv_ref, dk_acc, dv_acc):
        g, hp, b = pl.program_id(0), pl.program_id(1), pl.program_id(2)

        @pl.when(b == 0)
        def _():
            dk_acc[...] = jnp.zeros_like(dk_acc)
            dv_acc[...] = jnp.zeros_like(dv_acc)

        start = _dil_start(b)
        neg_dist = _dil_neg_dist(g, b * DIL_QB + lax.broadcasted_iota(jnp.int32, (DIL_WIN, DIL_QB), 1),
                                 start + lax.broadcasted_iota(jnp.int32, (DIL_WIN, DIL_QB), 0))
        dil = jnp.left_shift(1, 2 * g).astype(F32)
        q = q_ref[...]
        do = do_ref[...]
        kw = k_ref[pl.ds(start, DIL_WIN), :]
        vw = v_ref[pl.ds(start, DIL_WIN), :]
        dqs, dks, dvs = [], [], []
        for hh in range(DIL_HPS):
            sl = slice(hh * HD, (hh + 1) * HD)
            qh = q[:, sl] * QSCALE
            st = lax.dot_general(kw[:, sl], qh, _NT, preferred_element_type=F32)
            st = st + (sl_ref[hp * DIL_HPS + hh] * dil) * neg_dist
            pt = jnp.exp(st - lse_ref[hh, 0:1, :])
            dpt = lax.dot_general(vw[:, sl], do[:, sl], _NT, preferred_element_type=F32)
            dst = (pt * (dpt - dd_ref[hh, 0:1, :])).astype(BF16)
            dqs.append(lax.dot_general(kw[:, sl], dst, _TN, preferred_element_type=F32).T * QSCALE)
            dks.append(jnp.dot(dst, qh, preferred_element_type=F32))
            dvs.append(jnp.dot(pt.astype(BF16), do[:, sl], preferred_element_type=F32))
        dqkv_ref[0, pl.ds(pl.multiple_of(b * DIL_QB, DIL_QB), DIL_QB), :] = jnp.concatenate(dqs, axis=1).astype(dqkv_ref.dtype)
        dk_acc[pl.ds(start, DIL_WIN), :] += jnp.concatenate(dks, axis=1)
        dv_acc[pl.ds(start, DIL_WIN), :] += jnp.concatenate(dvs, axis=1)

        @pl.when(b == DIL_BLOCKS - 1)
        def _():
            dqkv_ref[1] = dk_acc[...].astype(dqkv_ref.dtype)
            dqkv_ref[2] = dv_acc[...].astype(dqkv_ref.dtype)

    qspec = pl.BlockSpec((None, DIL_QB, DIL_LW), lambda g, hp, b: (g, b, hp))
    rspec = pl.BlockSpec((None, DIL_HPS, 8, DIL_QB), lambda g, hp, b: (g, hp, 0, b))
    (dqkv,), sent = _carrier_call(
        "dil_bwd", body, (3, DIL_NLW, DIL_BLOCKS), _dil_in_specs() + [qspec, rspec, rspec],
        [pl.BlockSpec((None, 3, SEQ, DIL_LW), lambda g, hp, b: (g, 0, 0, hp))], [jax.ShapeDtypeStruct((3, 3, SEQ, DM), BF16)],
        [pltpu.VMEM((SEQ, DIL_LW), F32), pltpu.VMEM((SEQ, DIL_LW), F32)], (slopes, qkv, qkv, qkv, do, dd, lse), carry)
    return dqkv, sent


def _ffn_fwd(name, x, g_pre, g_post, wgt4, wut4, wd4, carry):
    tm = 512

    def body(x_ref, gpre_ref, gpost_ref, wg_ref, wu_ref, wd_ref, xn_ref, h_ref, gate_ref, up_ref, u_ref, acc):
        s = pl.program_id(1)

        @pl.when(s == 0)
        def _():
            x = x_ref[...]
            r = lax.rsqrt(jnp.mean(x * x, axis=-1, keepdims=True) + RMS_EPS)
            h_ref[...] = (x * r * gpre_ref[...]).astype(h_ref.dtype)

        h = h_ref[...]
        gate = lax.dot_general(h, wg_ref[...], _NT, preferred_element_type=F32).astype(BF16)
        up = lax.dot_general(h, wu_ref[...], _NT, preferred_element_type=F32).astype(BF16)
        gate_ref[...] = gate
        up_ref[...] = up
        gf = gate.astype(F32)
        act = (gf * jax.nn.sigmoid(gf) * up.astype(F32)).astype(BF16)
        part = jnp.dot(act, wd_ref[...], preferred_element_type=F32)

        @pl.when(s == 0)
        def _():
            acc[...] = part

        @pl.when(s > 0)
        def _():
            acc[...] += part

        @pl.when(s == NCHIP - 1)
        def _():
            u = acc[...]
            u_ref[...] = u
            r = lax.rsqrt(jnp.mean(u * u, axis=-1, keepdims=True) + RMS_EPS)
            xn_ref[...] = x_ref[...] + u * r * gpost_ref[...]

    rows = pl.BlockSpec((tm, DM), lambda i, s: (i, 0))
    vec = pl.BlockSpec((1, DM), lambda i, s: (0, 0))
    wspec = _ffn_wspec(lambda i, s: (s, 0, 0))
    mid = pl.BlockSpec((None, tm, FSH), lambda i, s: (s, i, 0))
    outs, sent = _carrier_call(
        name, body, (SEQ // tm, NCHIP), [rows, vec, vec, wspec, wspec, wspec], [rows, rows, mid, mid, rows],
        [jax.ShapeDtypeStruct((SEQ, DM), F32), jax.ShapeDtypeStruct((SEQ, DM), BF16), jax.ShapeDtypeStruct((NCHIP, SEQ, FSH), BF16),
         jax.ShapeDtypeStruct((NCHIP, SEQ, FSH), BF16), jax.ShapeDtypeStruct((SEQ, DM), F32)],
        [pltpu.VMEM((tm, DM), F32)], (x, g_pre, g_post, wgt4, wut4, wd4), carry)
    return outs, sent


def _ffn_block(layer, x, g_pre, g_post, ex):
    tag = f"l{layer}_ffn_fwd"
    (x_new, h, gate, up, u), sent = _ffn_fwd(tag, x, g_pre, g_post, ex.weight(("ffn_w_gate", layer)), ex.weight(("ffn_w_up", layer)),
                                             ex.weight(("ffn_w_down", layer)), ex.carry(tag))
    ex.carried(tag, sent)
    return x_new, (x, h, gate, up, u)


def _ffn_bwd(name, dx, x, gate, up, u, g_pre, g_post, wgt4, wut4, wd4, carry):
    tm = 512

    def body(dx_ref, x_ref, gate_ref, up_ref, u_ref, gpre_ref, gpost_ref, wg_ref, wu_ref, wd_ref,
             dxin_ref, du_ref, dgate_ref, dup_ref, act_ref, dgpre_ref, dgpost_ref, dh_acc):
        i, s = pl.program_id(0), pl.program_id(1)

        @pl.when((i == 0) & (s == 0))
        def _():
            dgpre_ref[...] = jnp.zeros_like(dgpre_ref)
            dgpost_ref[...] = jnp.zeros_like(dgpost_ref)

        @pl.when(s == 0)
        def _():
            dy = dx_ref[...]
            uu = u_ref[...]
            r = lax.rsqrt(jnp.mean(uu * uu, axis=-1, keepdims=True) + RMS_EPS)
            yh = uu * r
            t = dy * gpost_ref[...]
            du_ref[...] = (r * (t - yh * jnp.mean(t * yh, axis=-1, keepdims=True))).astype(du_ref.dtype)
            dgpost_ref[...] += jnp.sum(dy * yh, axis=0, keepdims=True)

        dact = lax.dot_general(du_ref[...], wd_ref[...], _NT, preferred_element_type=F32)
        g = gate_ref[...].astype(F32)
        upv = up_ref[...].astype(F32)
        sg = jax.nn.sigmoid(g)
        dgate = (dact * upv * sg * (1.0 + g * (1.0 - sg))).astype(BF16)
        dup = (dact * g * sg).astype(BF16)
        dgate_ref[...] = dgate
        dup_ref[...] = dup
        act_ref[...] = (g * sg * upv).astype(act_ref.dtype)
        part = jnp.dot(dgate, wg_ref[...], preferred_element_type=F32) + jnp.dot(dup, wu_ref[...], preferred_element_type=F32)

        @pl.when(s == 0)
        def _():
            dh_acc[...] = part

        @pl.when(s > 0)
        def _():
            dh_acc[...] += part

        @pl.when(s == NCHIP - 1)
        def _():
            dh = dh_acc[...]
            xx = x_ref[...]
            r = lax.rsqrt(jnp.mean(xx * xx, axis=-1, keepdims=True) + RMS_EPS)
            yh = xx * r
            t = dh * gpre_ref[...]
            dxin_ref[...] = dx_ref[...] + r * (t - yh * jnp.mean(t * yh, axis=-1, keepdims=True))
            dgpre_ref[...] += jnp.sum(dh * yh, axis=0, keepdims=True)

    rows = pl.BlockSpec((tm, DM), lambda i, s: (i, 0))
    vec = pl.BlockSpec((1, DM), lambda i, s: (0, 0))
    wspec = _ffn_wspec(lambda i, s: (s, 0, 0))
    mid = pl.BlockSpec((None, tm, FSH), lambda i, s: (s, i, 0))
    mid_shape = jax.ShapeDtypeStruct((NCHIP, SEQ, FSH), BF16)
    return _carrier_call(
        name, body, (SEQ // tm, NCHIP), [rows, rows, mid, mid, rows, vec, vec, wspec, wspec, wspec], [rows, rows, mid, mid, mid, vec, vec],
        [jax.ShapeDtypeStruct((SEQ, DM), F32), jax.ShapeDtypeStruct((SEQ, DM), BF16), mid_shape, mid_shape, mid_shape,
         jax.ShapeDtypeStruct((1, DM), F32), jax.ShapeDtypeStruct((1, DM), F32)],
        [pltpu.VMEM((tm, DM), F32)], (dx, x, gate, up, u, g_pre, g_post, wgt4, wut4, wd4), carry)


def _ffn_block_bwd(layer, dx, saved, g_pre, g_post, ex):
    tag = f"l{layer}"
    x, h, gate, up, u = saved
    (dx_in, du, dgate, dup, act, dg_pre, dg_post), sent = _ffn_bwd(
        f"{tag}_ffn_bwd", dx, x, gate, up, u, g_pre, g_post, ex.weight(("ffn_w_gate", layer)), ex.weight(("ffn_w_up", layer)),
        ex.weight(("ffn_w_down", layer)), ex.carry(f"{tag}_ffn_bwd"))
    ex.carried(f"{tag}_ffn_bwd", sent)
    d_wd = _ffn_bwd_dw(f"{tag}_dwd", act, du)
    d_wg = _ffn_bwd_dw(f"{tag}_dwg", dgate, h)
    d_wu = _ffn_bwd_dw(f"{tag}_dwu", dup, h)
    ex.grads(f"{tag}_ffn", {("ffn_w_gate", layer): d_wg, ("ffn_w_up", layer): d_wu, ("ffn_w_down", layer): d_wd})
    return dx_in, dg_pre, dg_post


def _alibi_slopes():
    return 2.0 ** (-8.0 * jnp.arange(1, NH + 1, dtype=F32) / NH)


def _local_step(x, target, norms, rpb, ex):
    g_mix_pre, g_mix_post, g_ffn_pre, g_ffn_post = norms
    row = lambda a, i: a[i:i + 1]

    bias, sent = _na_bias_tiles(rpb, ex.carry("na_bias_tiles"))
    ex.carried("na_bias_tiles", sent)
    h0, h0t = _rms_fwd_both("l0_mix_pre", x, row(g_mix_pre, 0))
    qkv0, sent = _qkv_fwd("l0_qkv", h0[None], ex.weight(("na_w_qkv", 0)), ex.carry("l0_qkv"))
    ex.carried("l0_qkv", sent)
    o0, sent = _na_fwd(qkv0[0], bias, ex.carry("na_fwd"))
    ex.carried("na_fwd", sent)
    na_wo = ex.weight(("na_w_o", 0)).reshape(DM, DM)
    x1, u0 = _proj_fwd("l0_proj", o0, na_wo, x, row(g_mix_post, 0))
    x2, ffn0 = _ffn_block(0, x1, row(g_ffn_pre, 0), row(g_ffn_post, 0), ex)

    slopes = _alibi_slopes()
    h2g, h2gt = _to_groups("l1_h_groups", _rms_fwd("l1_mix_pre", x2, row(g_mix_pre, 1), F32))
    dil_wqkv = ex.weight(("dil_w_qkv", 0))
    qkv1, sent = _qkv_fwd("l1_qkv", h2g, dil_wqkv, ex.carry("l1_qkv"))
    ex.carried("l1_qkv", sent)
    og, lg, sent = _dil_fwd(qkv1, slopes, ex.carry("dil_fwd"))
    ex.carried("dil_fwd", sent)
    o1, lse = _dil_merge(og, lg)
    dil_wo = ex.weight(("dil_w_o", 0)).reshape(DM, DM)
    x3, u1 = _proj_fwd("l1_proj", o1, dil_wo, x2, row(g_mix_post, 1))
    x4, ffn1 = _ffn_block(1, x3, row(g_ffn_pre, 1), row(g_ffn_post, 1), ex)

    dx4, loss_row = _loss_grad("loss", x4, target)

    dx3, dg_fpre1, dg_fpost1 = _ffn_block_bwd(1, dx4, ffn1, row(g_ffn_pre, 1), row(g_ffn_post, 1), ex)
    (do1, du1, dg_mpost1), sent = _proj_bwd("l1_proj_bwd", dx3, u1, row(g_mix_post, 1), dil_wo, F32, ex.carry("l1_proj_bwd"))
    ex.carried("l1_proj_bwd", sent)
    d_dil_wo = _proj_bwd_dw("l1_dwo", o1, du1)
    dog, ddg, lseg = _dil_bwd_prep(do1, o1, lse)
    dqkv1, sent = _dil_bwd(qkv1, dog, ddg, lseg, slopes, ex.carry("dil_bwd"))
    ex.carried("dil_bwd", sent)
    d_dil_wqkv, sent = _qkv_bwd_dw("l1_dwqkv", h2gt, dqkv1, dil_wqkv.shape[2], ex.carry("l1_dwqkv"))
    ex.carried("l1_dwqkv", sent)
    ex.grads("l1_mix", {("dil_w_qkv", 0): d_dil_wqkv, ("dil_w_o", 0): d_dil_wo.reshape(NCHIP, DM // NCHIP, DM)})
    dh2g, sent = _qkv_bwd_dh("l1_dh", dqkv1, dil_wqkv, ex.carry("l1_dh"))
    ex.carried("l1_dh", sent)
    dh2 = _from_groups_sum("l1_dh_tokens", dh2g)
    (dx2, dg_mpre1), sent = _norm_bwd("l1_mix_pre_bwd", dh2, x2, row(g_mix_pre, 1), dx3, ex.carry("l1_mix_pre_bwd"))
    ex.carried("l1_mix_pre_bwd", sent)

    dx1, dg_fpre0, dg_fpost0 = _ffn_block_bwd(0, dx2, ffn0, row(g_ffn_pre, 0), row(g_ffn_post, 0), ex)
    (do0, du0, dg_mpost0), sent = _proj_bwd("l0_proj_bwd", dx1, u0, row(g_mix_post, 0), na_wo, BF16, ex.carry("l0_proj_bwd"))
    ex.carried("l0_proj_bwd", sent)
    d_na_wo = _proj_bwd_dw("l0_dwo", o0, du0)
    dqkv0, z, sent = _na_bwd(qkv0[0], bias, do0, ex.carry("na_bwd"))
    ex.carried("na_bwd", sent)
    d_rpb = _rpb_grad(z)
    na_wqkv = ex.weight(("na_w_qkv", 0))
    d_na_wqkv, sent = _qkv_bwd_dw("l0_dwqkv", h0t[None], dqkv0[None], na_wqkv.shape[2], ex.carry("l0_dwqkv"))
    ex.carried("l0_dwqkv", sent)
    ex.grads("l0_mix", {("na_w_qkv", 0): d_na_wqkv, ("na_w_o", 0): d_na_wo.reshape(NCHIP, DM // NCHIP, DM)})
    dh0, sent = _qkv_bwd_dh("l0_dh", dqkv0[None], na_wqkv, ex.carry("l0_dh"))
    ex.carried("l0_dh", sent)
    (dx0, dg_mpre0), sent = _norm_bwd("l0_mix_pre_bwd", dh0[0], x, row(g_mix_pre, 0), dx1, ex.carry("l0_mix_pre_bwd"))
    ex.carried("l0_mix_pre_bwd", sent)

    dnorms = (jnp.concatenate([dg_mpre0, dg_mpre1]), jnp.concatenate([dg_mpost0, dg_mpost1]),
              jnp.concatenate([dg_fpre0, dg_fpre1]), jnp.concatenate([dg_fpost0, dg_fpost1]))
    return loss_row, dx0, dnorms, d_rpb


def _place():
    x, y, c = lax.axis_index("x"), lax.axis_index("y"), lax.axis_index("c")
    chips = ((1 - x, y), (x, 1 - y), (1 - x, 1 - y))
    return x, y, c, chips


def _chip_id(chip):
    return 2 * chip[0] + chip[1]


def _gather_copies(shards):
    n = len(shards)

    def copies(src, out, sems):
        send_sems, recv_sems = sems
        x, y, c, chips = _place()

        def copy(t, k, chip, half, to, from_src=False):
            blk = out[t].at[_chip_id(chip), half]
            return pltpu.make_async_remote_copy(
                src_ref=src[t].at[half] if from_src else blk, dst_ref=blk,
                send_sem=send_sems.at[6 * t + k], recv_sem=recv_sems.at[6 * t + k], device_id=to, device_id_type=MESH)

        return copy, x, y, c, chips

    def issue(src, out, sems):
        copy, x, y, c, chips = copies(src, out, sems)
        for t in range(n):
            for j, chip in enumerate(chips):
                copy(t, j, (x, y), c, (*chip, c), from_src=True).start()

    def drain(src, out, sems):
        copy, x, y, c, chips = copies(src, out, sems)
        passed = []
        for t in range(n):
            for j, chip in enumerate(chips):
                copy(t, j, chip, c, (x, y, c)).wait_recv()
                fwd = copy(t, 3 + j, chip, c, (x, y, 1 - c))
                fwd.start()
                passed.append(fwd)
        for t in range(n):
            for j, chip in enumerate(chips):
                copy(t, 3 + j, chip, 1 - c, (x, y, c)).wait_recv()
        for t in range(n):
            for j, chip in enumerate(chips):
                copy(t, j, (x, y), c, (*chip, c), from_src=True).wait_send()
        for cp in passed:
            cp.wait_send()

    return _Carried(shards, [jax.ShapeDtypeStruct((NCHIP,) + s.shape, s.dtype) for s in shards], (6 * n, 6 * n), issue, drain)


def _pair_exchange_copies(grads):
    n = len(grads)

    def copies(g, theirs, sems):
        send_sems, recv_sems = sems
        x, y, c, _ = _place()
        return [pltpu.make_async_remote_copy(src_ref=g[t].at[:, 1 - c], dst_ref=theirs[t], send_sem=send_sems.at[t],
                                             recv_sem=recv_sems.at[t], device_id=(x, y, 1 - c), device_id_type=MESH) for t in range(n)]

    def issue(g, theirs, sems):
        for cp in copies(g, theirs, sems):
            cp.start()

    def drain(g, theirs, sems):
        for cp in copies(g, theirs, sems):
            cp.wait()

    return _Carried(grads, [jax.ShapeDtypeStruct((NCHIP,) + g.shape[2:], g.dtype) for g in grads], (n, n), issue, drain)


def _chip_exchange_copies(items):
    flat = [(t, i, j) for t, (_, peers) in enumerate(items) for i, j in enumerate(peers)]

    def copies(p, slots, sems):
        send_sems, recv_sems = sems
        x, y, c, chips = _place()
        return [pltpu.make_async_remote_copy(src_ref=p[t].at[_chip_id(chips[j])], dst_ref=slots[t].at[i], send_sem=send_sems.at[k],
                                             recv_sem=recv_sems.at[k], device_id=(*chips[j], c), device_id_type=MESH)
                for k, (t, i, j) in enumerate(flat)]

    def issue(p, slots, sems):
        for cp in copies(p, slots, sems):
            cp.start()

    def drain(p, slots, sems):
        for cp in copies(p, slots, sems):
            cp.wait()

    return _Carried([p for p, _ in items], [jax.ShapeDtypeStruct((len(peers),) + p.shape[1:], p.dtype) for p, peers in items],
                    (len(flat), len(flat)), issue, drain)


def _pair_share_copies(halves):
    n = len(halves)

    def copies(h, other, sems):
        send_sems, recv_sems = sems
        x, y, c, _ = _place()
        return [pltpu.make_async_remote_copy(src_ref=h[t], dst_ref=other[t], send_sem=send_sems.at[t], recv_sem=recv_sems.at[t],
                                             device_id=(x, y, 1 - c), device_id_type=MESH) for t in range(n)]

    def issue(h, other, sems):
        for cp in copies(h, other, sems):
            cp.start()

    def drain(h, other, sems):
        for cp in copies(h, other, sems):
            cp.wait()

    return _Carried(halves, [jax.ShapeDtypeStruct(h.shape, h.dtype) for h in halves], (n, n), issue, drain)


SMALL_ROWS = 128


def _allreduce_small(v, carry):
    ci, co = len(carry.ins), len(carry.out_shapes)

    def body(*refs):
        v_ref, cins, o_ref, couts = refs[0], refs[1:1 + ci], refs[1 + ci], refs[2 + ci:2 + ci + co]
        buf, send_sems, recv_sems = refs[2 + ci + co:5 + ci + co]
        csems = refs[5 + ci + co:]
        carry.issue(cins, couts, csems)
        x, y, c, _ = _place()
        me = 4 * x + 2 * y + c
        flip = lambda a, f: 1 - a if f else a
        buf[me] = v_ref[...]
        peers = [(flip(x, d >> 2 & 1), flip(y, d >> 1 & 1), flip(c, d & 1)) for d in range(1, 8)]
        sends = [pltpu.make_async_remote_copy(src_ref=v_ref, dst_ref=buf.at[me], send_sem=send_sems.at[i], recv_sem=recv_sems.at[i],
                                              device_id=peer, device_id_type=MESH) for i, peer in enumerate(peers)]
        for cp in sends:
            cp.start()
        for i, (px, py, pc) in enumerate(peers):
            pltpu.make_async_remote_copy(src_ref=v_ref, dst_ref=buf.at[4 * px + 2 * py + pc], send_sem=send_sems.at[i], recv_sem=recv_sems.at[i],
                                         device_id=(px, py, pc), device_id_type=MESH).wait_recv()
        for cp in sends:
            cp.wait_send()
        acc = buf[0]
        for k in range(1, 8):
            acc = acc + buf[k]
        o_ref[...] = acc
        carry.drain(cins, couts, csems)

    vm = pl.BlockSpec(memory_space=pltpu.VMEM)
    res = pl.pallas_call(
        body, in_specs=[vm] + [HBM_SPEC] * ci, out_specs=[vm] + [HBM_SPEC] * co,
        out_shape=[jax.ShapeDtypeStruct((SMALL_ROWS, 128), F32)] + carry.out_shapes,
        scratch_shapes=[pltpu.VMEM((8, SMALL_ROWS, 128), F32), pltpu.SemaphoreType.DMA((7,)), pltpu.SemaphoreType.DMA((7,))]
        + [pltpu.SemaphoreType.DMA((k,)) for k in carry.n_sems],
        compiler_params=pltpu.CompilerParams(has_side_effects=True), name="allreduce_small")(v, *carry.ins)
    return res[0], list(res[1:])


def _row_block(rows, cols, budget=3 << 19):
    best = 8
    for bm in range(8, rows + 1, 8):
        if rows % bm == 0 and bm * cols * 4 <= budget:
            best = bm
    return best


def _pair_sum(name, place, gs, theirs):
    n = len(gs)
    _, m, c = theirs[0].shape
    bm = _row_block(m, c)

    def body(place_ref, *refs):
        for a_ref, b_ref, o_ref in zip(refs[:n], refs[n:2 * n], refs[2 * n:]):
            o_ref[...] = (a_ref[...].astype(F32) + b_ref[...].astype(F32)).astype(o_ref.dtype)

    spec = pl.BlockSpec((None, bm, c), lambda k, i, pr: (k, i, 0))
    return pl.pallas_call(
        body, out_shape=[jax.ShapeDtypeStruct(theirs[0].shape, BF16)] * n,
        grid_spec=pltpu.PrefetchScalarGridSpec(
            num_scalar_prefetch=1, grid=(NCHIP, m // bm),
            in_specs=[pl.BlockSpec((None, None, bm, c), lambda k, i, pr: (k, pr[0], i, 0))] * n + [spec] * n, out_specs=[spec] * n),
        compiler_params=_params(("parallel", "parallel")), name=name)(place, *gs, *theirs)


def _chip_sum(name, place, parts, slots):
    n, ns = len(parts), len(slots[0])
    _, m, c = parts[0].shape
    bm = _row_block(m, c)

    def body(place_ref, *refs):
        for t in range(n):
            acc = refs[t][...].astype(F32)
            for s_ref in refs[n + t * ns:n + (t + 1) * ns]:
                for i in range(s_ref.shape[0]):
                    acc = acc + s_ref[i].astype(F32)
            refs[n + n * ns + t][...] = acc

    half = pl.BlockSpec((bm, c), lambda i, pr: (i, 0))
    return pl.pallas_call(
        body, out_shape=[jax.ShapeDtypeStruct((m, c), F32)] * n,
        grid_spec=pltpu.PrefetchScalarGridSpec(
            num_scalar_prefetch=1, grid=(m // bm,),
            in_specs=[pl.BlockSpec((None, bm, c), lambda i, pr: (pr[1], i, 0))] * n
            + [pl.BlockSpec((s.shape[0], bm, c), lambda i, pr: (0, i, 0)) for group in slots for s in group],
            out_specs=[half] * n),
        compiler_params=_params(("parallel",)), name=name)(place, *parts, *[s for group in slots for s in group])


def _adamw(name, place, tensors, layer=0, into=None):
    n = len(tensors)
    lead, rows, cols = tensors[0][0].shape
    bm = _row_block(rows // 2, cols, budget=768 * 1024 // n)
    per_half = rows // 2 // bm
    c1 = 1.0 - ADAM_B1 ** ADAM_STEP
    c2 = 1.0 - ADAM_B2 ** ADAM_STEP

    def body(place_ref, *refs):
        outs = refs[len(refs) - 4 * n:]
        for t in range(n):
            w_ref, ga_ref, gb_ref, m_ref, v_ref = refs[5 * t:5 * t + 5]
            go_ref, d_ref, mo_ref, vo_ref = outs[4 * t:4 * t + 4]
            g = jnp.where(pl.program_id(0) // per_half == place_ref[0], ga_ref[...], gb_ref[...])
            mn = ADAM_B1 * m_ref[...] + (1.0 - ADAM_B1) * g
            vn = ADAM_B2 * v_ref[...] + (1.0 - ADAM_B2) * (g * g)
            go_ref[...] = g
            mo_ref[...] = mn
            vo_ref[...] = vn
            d_ref[...] = -ADAM_LR * ((mn / c1) / (jnp.sqrt(vn / c2) + ADAM_EPS) + ADAM_WD * w_ref[...])

    spec = pl.BlockSpec((None, bm, cols), lambda i, pr: (layer, i, 0))

    def half_spec(mine):
        def index(i, pr):
            first = (pr[0] == 0) == mine
            park = jnp.where(first, per_half - 1, 0)
            return jnp.where((i < per_half) == first, i % per_half, park), 0
        return pl.BlockSpec((bm, cols), index)
    sh = jax.ShapeDtypeStruct((lead, rows, cols), F32)
    prev = [] if into is None else [a for res in into for a in res]
    res = pl.pallas_call(
        body, out_shape=[sh] * (4 * n), input_output_aliases={1 + 5 * n + k: k for k in range(len(prev))},
        grid_spec=pltpu.PrefetchScalarGridSpec(
            num_scalar_prefetch=1, grid=(rows // bm,),
            in_specs=[spec, half_spec(True), half_spec(False), spec, spec] * n + [pl.BlockSpec(memory_space=pl.ANY)] * len(prev),
            out_specs=[spec] * (4 * n)),
        compiler_params=_params(("parallel",)), name=name)(place, *[a for t in tensors for a in t], *prev)
    return [res[4 * t:4 * t + 4] for t in range(n)]


def _pack_small(norms, rpb, last=None):
    flat = jnp.concatenate([a.reshape(-1) for a in norms] + [rpb.reshape(-1)])
    flat = jnp.pad(flat, (0, SMALL_ROWS * 128 - flat.shape[0]))
    if last is not None:
        flat = lax.dynamic_update_slice(flat, last.reshape(1), (flat.shape[0] - 1,))
    return flat.reshape(SMALL_ROWS, 128)


def _unpack_small(p):
    flat = p.reshape(-1)
    norms = [flat[i * 2 * DM:(i + 1) * 2 * DM].reshape(2, DM) for i in range(4)]
    rpb = flat[8 * DM:8 * DM + NH * 15 * 31].reshape(1, NH, 15, 31)
    return norms, rpb


FFN_NAMES = ("ffn_w_gate", "ffn_w_up", "ffn_w_down")
L0_FFN = tuple((n, 0) for n in FFN_NAMES)
L1_FFN = tuple((n, 1) for n in FFN_NAMES)
NA_KEYS = (("na_w_qkv", 0), ("na_w_o", 0))
DIL_KEYS = (("dil_w_qkv", 0), ("dil_w_o", 0))
ALL_PEERS, NEIGHBOURS, DIAGONAL = (0, 1, 2), (0, 1), (2,)


class _Exchange:
    GATHERS = {"na_bias_tiles": NA_KEYS, "l0_qkv": L0_FFN[:1], "na_fwd": L0_FFN[1:], "l0_ffn_fwd": DIL_KEYS[:1], "dil_fwd": L1_FFN + DIL_KEYS[1:]}
    PAIRS = {"l1_proj_bwd": L1_FFN, "l1_dh": DIL_KEYS, "l0_proj_bwd": L0_FFN}
    EXCHANGES = {"dil_bwd": [(k, ALL_PEERS) for k in L1_FFN],
                 "l0_ffn_bwd": [(DIL_KEYS[0], NEIGHBOURS), (DIL_KEYS[1], ALL_PEERS)],
                 "na_bwd": [(k, ALL_PEERS) for k in L0_FFN] + [(DIL_KEYS[0], DIAGONAL)],
                 "l0_dh": [(k, NEIGHBOURS) for k in NA_KEYS],
                 "allreduce_small": [(k, DIAGONAL) for k in NA_KEYS]}
    SHARES = {"l1_dwqkv": L1_FFN, "l0_dwqkv": L0_FFN + DIL_KEYS}

    def __init__(self, shards):
        self.chip = 2 * lax.axis_index("x") + lax.axis_index("y")
        self.place = jnp.stack([lax.axis_index("c"), self.chip]).astype(jnp.int32)
        self.own = {k: s.reshape(2, s.shape[0] // 2, s.shape[1]).astype(BF16) for k, s in shards.items()}
        self.gathered, self.mine, self.parts, self.slots, self.full, self.other = {}, {}, {}, {}, {}, {}

    def _take(self, keys, landed):
        for k, gw in zip(keys, landed):
            self.gathered[k] = lax.dynamic_update_slice(gw, self.own[k][None], (self.chip, 0, 0, 0))

    def _sum(self, items, landed):
        runs = []
        for (k, peers), s in zip(items, landed):
            got = self.slots.setdefault(k, {})
            got[peers] = s
            if sum(len(p) for p in got) == len(ALL_PEERS):
                like = (self.parts[k].shape, tuple(sorted(got)))
                if runs and runs[-1][0] == like:
                    runs[-1][1].append(k)
                else:
                    runs.append((like, [k]))
        for (_, split), ks in runs:
            sums = _chip_sum(f"chip_sum_{ks[0][0]}_{ks[0][1]}", self.place, [self.parts[k] for k in ks],
                             [[self.slots[k][p] for p in split] for k in ks])
            self.full.update(zip(ks, sums))

    def weight(self, key):
        g = self.gathered[key]
        return g.reshape(NCHIP, 2 * g.shape[2], g.shape[3])

    def _pair_sums(self, keys, theirs):
        runs = []
        for k, t in zip(keys, theirs):
            if runs and runs[-1][0][1].shape == t.shape:
                runs[-1].append((k, t))
            else:
                runs.append([(k, t)])
        for run in runs:
            ks = [k for k, _ in run]
            sums = _pair_sum(f"pair_sum_{ks[0][0]}_{ks[0][1]}", self.place, [self.mine[k] for k in ks], [t for _, t in run])
            self.parts.update(zip(ks, sums))

    def carry(self, tag):
        if tag in self.GATHERS:
            return _gather_copies([self.own[k] for k in self.GATHERS[tag]])
        if tag in self.PAIRS:
            return _pair_exchange_copies([self.mine[k] for k in self.PAIRS[tag]])
        if tag in self.EXCHANGES:
            return _chip_exchange_copies([(self.parts[k], peers) for k, peers in self.EXCHANGES[tag]])
        if tag in self.SHARES:
            return _pair_share_copies([self.full[k] for k in self.SHARES[tag]])
        return None

    def carried(self, tag, landed):
        if tag in self.GATHERS:
            self._take(self.GATHERS[tag], landed)
        elif tag in self.PAIRS:
            self._pair_sums(self.PAIRS[tag], landed)
        elif tag in self.EXCHANGES:
            self._sum(self.EXCHANGES[tag], landed)
        elif tag in self.SHARES:
            self.other.update(zip(self.SHARES[tag], landed))

    def grads(self, tag, dw):
        for k, g in dw.items():
            self.mine[k] = g.reshape(NCHIP, 2, -1, g.shape[-1])
        if tag == "l0_mix":
            keys = tuple(dw)
            self._pair_sums(keys, _run_carried("grad_pair_exchange_last", _pair_exchange_copies([self.mine[k] for k in keys])))

    def finish(self):
        rest = tuple(k for k in self.full if k not in self.other)
        self.other.update(zip(rest, _run_carried("grad_pair_share_last", _pair_share_copies([self.full[k] for k in rest]))))
        return {k: (self.full[k], self.other[k]) for k in self.full}


def kernel(x, norm_mix_pre, norm_mix_post, norm_ffn_pre, norm_ffn_post, na_w_qkv, na_w_o, na_rpb, dil_w_qkv, dil_w_o, ffn_w_gate, ffn_w_up, ffn_w_down, loss_target, m_norm_mix_pre, m_norm_mix_post, m_norm_ffn_pre, m_norm_ffn_post, m_na_w_qkv, m_na_w_o, m_na_rpb, m_dil_w_qkv, m_dil_w_o, m_ffn_w_gate, m_ffn_w_up, m_ffn_w_down, v_norm_mix_pre, v_norm_mix_post, v_norm_ffn_pre, v_norm_ffn_post, v_na_w_qkv, v_na_w_o, v_na_rpb, v_dil_w_qkv, v_dil_w_o, v_ffn_w_gate, v_ffn_w_up, v_ffn_w_down):
    tr = lambda a: jnp.swapaxes(a, 1, 2)
    weights = {"na_w_qkv": na_w_qkv, "na_w_o": na_w_o, "dil_w_qkv": dil_w_qkv, "dil_w_o": dil_w_o,
               "ffn_w_gate": tr(ffn_w_gate), "ffn_w_up": tr(ffn_w_up), "ffn_w_down": ffn_w_down}
    m_in = {"na_w_qkv": m_na_w_qkv, "na_w_o": m_na_w_o, "dil_w_qkv": m_dil_w_qkv, "dil_w_o": m_dil_w_o,
            "ffn_w_gate": tr(m_ffn_w_gate), "ffn_w_up": tr(m_ffn_w_up), "ffn_w_down": m_ffn_w_down}
    v_in = {"na_w_qkv": v_na_w_qkv, "na_w_o": v_na_w_o, "dil_w_qkv": v_dil_w_qkv, "dil_w_o": v_dil_w_o,
            "ffn_w_gate": tr(v_ffn_w_gate), "ffn_w_up": tr(v_ffn_w_up), "ffn_w_down": v_ffn_w_down}

    ex = _Exchange({(n, l): weights[n][l] for n in weights for l in range(weights[n].shape[0])})
    norms = (norm_mix_pre, norm_mix_post, norm_ffn_pre, norm_ffn_post)
    loss_row, dx, dnorms, d_rpb = _local_step(x[0], loss_target[0], norms, na_rpb[0], ex)
    small, sent = _allreduce_small(_pack_small(dnorms, d_rpb, last=loss_row[0, 0]), ex.carry("allreduce_small"))
    ex.carried("allreduce_small", sent)
    full = ex.finish()
    loss = small[SMALL_ROWS - 1, 127]

    out_g, out_d, out_m, out_v = {}, {}, {}, {}
    operands = lambda n, l: (weights[n], *full[(n, l)], m_in[n], v_in[n])
    results = {n: _adamw(f"adamw_{n}", ex.place, [operands(n, 0)])[0] for n in weights if n not in FFN_NAMES}
    ffn = None
    for l in range(2):
        ffn = _adamw(f"adamw_ffn_{l}", ex.place, [operands(n, l) for n in FFN_NAMES], l, ffn)
    results.update(zip(FFN_NAMES, ffn))
    for n, res in results.items():
        if n in ("ffn_w_gate", "ffn_w_up"):
            res = [tr(r) for r in res]
        out_g[n], out_d[n], out_m[n], out_v[n] = res
    sm_names = ("norm_mix_pre", "norm_mix_post", "norm_ffn_pre", "norm_ffn_post", "na_rpb")
    sm = _adamw("adamw_small", jnp.zeros((2,), jnp.int32),
                [(_pack_small(norms, na_rpb)[None], small[:SMALL_ROWS // 2], small[SMALL_ROWS // 2:],
                  _pack_small((m_norm_mix_pre, m_norm_mix_post, m_norm_ffn_pre, m_norm_ffn_post), m_na_rpb)[None],
                  _pack_small((v_norm_mix_pre, v_norm_mix_post, v_norm_ffn_pre, v_norm_ffn_post), v_na_rpb)[None])])[0]
    for res, dst in zip(sm, (out_g, out_d, out_m, out_v)):
        ns, rp = _unpack_small(res)
        for n, a in zip(sm_names, ns + [rp]):
            dst[n] = a

    order = ("norm_mix_pre", "norm_mix_post", "norm_ffn_pre", "norm_ffn_post", "na_w_qkv", "na_w_o", "na_rpb", "dil_w_qkv", "dil_w_o",
             "ffn_w_gate", "ffn_w_up", "ffn_w_down")
    return (loss, dx[None], *[out_g[n] for n in order], *[out_d[n] for n in order], *[out_m[n] for n in order], *[out_v[n] for n in order])
```

```python
import functools

import numpy as np
import jax
import jax.numpy as jnp
from jax import lax
from jax.experimental import pallas as pl
from jax.experimental.pallas import tpu as pltpu

F32 = jnp.float32
BF16 = jnp.bfloat16

SEQ = 2048
DM = 1024
NH = 16
HD = 64
DFF = 2816
NCHIP = 4
FSH = DFF // NCHIP
GRID_W = 64
NA_QROWS = 4
NA_QB = NA_QROWS * GRID_W
NA_WROWS = 12
NA_WIN = NA_WROWS * GRID_W
DIL = (1, 4, 16)
DIL_QB = 256
DIL_WIN = DIL_QB + 128
DIL_RADIUS = 64
RMS_EPS = 1e-6
NEG = -1e30
QSCALE = HD ** -0.5
CH = 256
MESH = pl.DeviceIdType.MESH

ADAM_LR, ADAM_B1, ADAM_B2, ADAM_EPS, ADAM_WD, ADAM_STEP = 0.001, 0.9, 0.999, 1e-08, 0.01, 10

VMEM_LIMIT = 56 * 1024 * 1024

_NN = (((1,), (0,)), ((), ()))
_NT = (((1,), (1,)), ((), ()))
_TN = (((0,), (0,)), ((), ()))


def _params(sem):
    return pltpu.CompilerParams(dimension_semantics=sem, vmem_limit_bytes=VMEM_LIMIT)


def _matmul(name, pairs, grid, out_shape, out_spec, acc_shape, carrying=False, carry=None):
    nk = grid[-1]
    npair = len(pairs)
    n_in = 2 * npair

    def body(*refs):
        ins, o_ref = refs[:2 * npair], refs[n_in]
        part = None
        for p in range(npair):
            d = lax.dot_general(ins[2 * p][...].astype(BF16), ins[2 * p + 1][...].astype(BF16), pairs[p][4],
                                preferred_element_type=F32)
            part = d if part is None else part + d
        if nk == 1:
            o_ref[...] = part.astype(o_ref.dtype)
        else:
            acc_ref = refs[n_in + 1]
            kk = pl.program_id(len(grid) - 1)

            @pl.when(kk == 0)
            def _():
                acc_ref[...] = part

            @pl.when(kk > 0)
            def _():
                acc_ref[...] += part

            @pl.when(kk == nk - 1)
            def _():
                o_ref[...] = acc_ref[...].astype(o_ref.dtype)

    ops, specs = [], []
    for a, a_spec, b, b_spec, _ in pairs:
        ops += [a, b]
        specs += [a_spec, b_spec]
    (out,), sent = _carrier_call(name, body, grid, specs, [out_spec], [out_shape], [] if nk == 1 else [pltpu.VMEM(acc_shape, F32)], ops, carry)
    return (out, sent) if carrying else out


def _qkv_fwd(name, h_all, w4, carry):
    g_n = h_all.shape[0]
    per = w4.shape[2] // CH
    return _matmul(
        name, [(h_all, pl.BlockSpec((None, SEQ, DM), lambda g, q, k: (g, 0, 0)),
                w4, pl.BlockSpec((None, DM, CH), lambda g, q, k: ((g * 12 + q) // per, 0, (g * 12 + q) % per)), _NN)],
        (g_n, 12, 1), jax.ShapeDtypeStruct((g_n, SEQ, 3 * DM), BF16),
        pl.BlockSpec((None, SEQ, CH), lambda g, q, k: (g, 0, q)), None, carrying=True, carry=carry)


def _qkv_bwd_dh(name, dqkv, w4, carry):
    g_n = dqkv.shape[0]
    per = w4.shape[2] // CH
    tm = SEQ

    def pair(cb):
        chunk = lambda g, t: g * 12 + t * 4 + cb
        return (dqkv, pl.BlockSpec((None, None, tm, CH), lambda g, i, t: (g, t, i, cb)),
                w4, pl.BlockSpec((None, DM, CH), lambda g, i, t: (chunk(g, t) // per, 0, chunk(g, t) % per)), _NT)

    return _matmul(name, [pair(cb) for cb in range(4)], (g_n, SEQ // tm, 3), jax.ShapeDtypeStruct((g_n, SEQ, DM), F32),
                   pl.BlockSpec((None, tm, DM), lambda g, i, t: (g, i, 0)), (tm, DM), carrying=True, carry=carry)


def _qkv_bwd_dw(name, ht_all, dqkv, shard_cols, carry):
    g_n = dqkv.shape[0]
    per = shard_cols // CH
    return _matmul(
        name, [(ht_all, pl.BlockSpec((None, DM, SEQ), lambda qq, k: (qq // 12, 0, 0)),
                dqkv, pl.BlockSpec((None, None, SEQ, CH), lambda qq, k: (qq // 12, (qq % 12) // 4, 0, qq % 4)), _NN)],
        (g_n * 12, 1), jax.ShapeDtypeStruct((NCHIP, DM, shard_cols), BF16),
        pl.BlockSpec((None, DM, CH), lambda qq, k: (qq // per, 0, qq % per)), None, carrying=True, carry=carry)


def _proj_fwd(name, o, wo, x, g):
    tm = 512

    def body(o_ref, w_ref, x_ref, g_ref, xn_ref, u_ref):
        u = jnp.dot(o_ref[...], w_ref[...], preferred_element_type=F32)
        u_ref[...] = u
        r = lax.rsqrt(jnp.mean(u * u, axis=-1, keepdims=True) + RMS_EPS)
        xn_ref[...] = x_ref[...] + u * r * g_ref[...]

    rows = pl.BlockSpec((tm, DM), lambda i: (i, 0))
    sh = jax.ShapeDtypeStruct((SEQ, DM), F32)
    return pl.pallas_call(
        body, grid=(SEQ // tm,), in_specs=[rows, pl.BlockSpec((DM, DM), lambda i: (0, 0)), rows, pl.BlockSpec((1, DM), lambda i: (0, 0))],
        out_specs=[rows, rows], out_shape=[sh, sh], compiler_params=_params(("parallel",)), name=name)(o, wo, x, g)


def _proj_bwd(name, dy, u, g, wo, dtype, carry):
    tm = 512

    def body(dy_ref, u_ref, g_ref, w_ref, do_ref, du_ref, dg_ref):
        dy = dy_ref[...]
        u = u_ref[...]
        r = lax.rsqrt(jnp.mean(u * u, axis=-1, keepdims=True) + RMS_EPS)
        yh = u * r
        t = dy * g_ref[...]
        du = (r * (t - yh * jnp.mean(t * yh, axis=-1, keepdims=True))).astype(BF16)
        du_ref[...] = du
        do_ref[...] = lax.dot_general(du, w_ref[...], _NT, preferred_element_type=F32).astype(do_ref.dtype)

        @pl.when(pl.program_id(0) == 0)
        def _():
            dg_ref[...] = jnp.zeros_like(dg_ref)

        dg_ref[...] += jnp.sum(dy * yh, axis=0, keepdims=True)

    rows = pl.BlockSpec((tm, DM), lambda i: (i, 0))
    vec = pl.BlockSpec((1, DM), lambda i: (0, 0))
    return _carrier_call(
        name, body, (SEQ // tm,), [rows, rows, vec, pl.BlockSpec((DM, DM), lambda i: (0, 0))], [rows, rows, vec],
        [jax.ShapeDtypeStruct((SEQ, DM), dtype), jax.ShapeDtypeStruct((SEQ, DM), BF16), jax.ShapeDtypeStruct((1, DM), F32)],
        [], (dy, u, g, wo), carry)


def _proj_bwd_dw(name, o, du):
    tn = 512
    return _matmul(
        name, [(o, pl.BlockSpec((SEQ, DM), lambda j, k: (0, 0)), du, pl.BlockSpec((SEQ, tn), lambda j, k: (0, j)), _TN)],
        (DM // tn, 1), jax.ShapeDtypeStruct((DM, DM), BF16), pl.BlockSpec((DM, tn), lambda j, k: (0, j)), None)


def _ffn_wspec(index_map):
    return pl.BlockSpec((None, FSH, DM), index_map)


def _ffn_bwd_dw(name, a4, b):
    return _matmul(
        name, [(a4, pl.BlockSpec((None, SEQ, FSH), lambda s, k: (s, 0, 0)), b, pl.BlockSpec((SEQ, DM), lambda s, k: (0, 0)), _TN)],
        (NCHIP, 1), jax.ShapeDtypeStruct((NCHIP, FSH, DM), BF16), _ffn_wspec(lambda s, k: (s, 0, 0)), None)


ROWS = 256


def _row_spec():
    return pl.BlockSpec((ROWS, DM), lambda i: (i, 0))


def _vec_spec():
    return pl.BlockSpec((1, DM), lambda i: (0, 0))


def _rms_fwd(name, x, g, dtype=BF16):
    def body(x_ref, g_ref, o_ref):
        x = x_ref[...]
        r = lax.rsqrt(jnp.mean(x * x, axis=-1, keepdims=True) + RMS_EPS)
        o_ref[...] = (x * r * g_ref[...]).astype(o_ref.dtype)

    return pl.pallas_call(body, grid=(SEQ // ROWS,), in_specs=[_row_spec(), _vec_spec()], out_specs=_row_spec(),
                          out_shape=jax.ShapeDtypeStruct((SEQ, DM), dtype), compiler_params=_params(("parallel",)), name=name)(x, g)


def _rms_fwd_both(name, x, g):
    def body(x_ref, g_ref, o_ref, t_ref):
        x = x_ref[...]
        r = lax.rsqrt(jnp.mean(x * x, axis=-1, keepdims=True) + RMS_EPS)
        h = x * r * g_ref[...]
        o_ref[...] = h.astype(o_ref.dtype)
        t_ref[...] = h.T.astype(t_ref.dtype)

    return pl.pallas_call(
        body, grid=(SEQ // ROWS,), in_specs=[_row_spec(), _vec_spec()], out_specs=[_row_spec(), pl.BlockSpec((DM, ROWS), lambda i: (0, i))],
        out_shape=[jax.ShapeDtypeStruct((SEQ, DM), BF16), jax.ShapeDtypeStruct((DM, SEQ), BF16)],
        compiler_params=_params(("parallel",)), name=name)(x, g)


def _norm_bwd(name, dy, u, g, res, carry):
    def body(dy_ref, u_ref, g_ref, res_ref, du_ref, dg_ref):
        dy = dy_ref[...]
        u = u_ref[...]
        r = lax.rsqrt(jnp.mean(u * u, axis=-1, keepdims=True) + RMS_EPS)
        yh = u * r
        t = dy * g_ref[...]
        du_ref[...] = r * (t - yh * jnp.mean(t * yh, axis=-1, keepdims=True)) + res_ref[...]

        @pl.when(pl.program_id(0) == 0)
        def _():
            dg_ref[...] = jnp.zeros_like(dg_ref)

        dg_ref[...] += jnp.sum(dy * yh, axis=0, keepdims=True)

    return _carrier_call(
        name, body, (SEQ // ROWS,), [_row_spec(), _row_spec(), _vec_spec(), _row_spec()], [_row_spec(), _vec_spec()],
        [jax.ShapeDtypeStruct((SEQ, DM), F32), jax.ShapeDtypeStruct((1, DM), F32)], [], (dy, u, g, res), carry)


def _loss_grad(name, y, t):
    def body(y_ref, t_ref, dy_ref, l_ref):
        e = y_ref[...] - t_ref[...]
        dy_ref[...] = e * (1.0 / DM)

        @pl.when(pl.program_id(0) == 0)
        def _():
            l_ref[...] = jnp.zeros_like(l_ref)

        l_ref[...] += jnp.sum(e * e) * (0.5 / DM)

    return pl.pallas_call(
        body, grid=(SEQ // ROWS,), in_specs=[_row_spec(), _row_spec()],
        out_specs=[_row_spec(), pl.BlockSpec((1, 128), lambda i: (0, 0))],
        out_shape=[jax.ShapeDtypeStruct((SEQ, DM), F32), jax.ShapeDtypeStruct((1, 128), F32)],
        compiler_params=_params(("arbitrary",)), name=name)(y, t)


HBM_SPEC = pl.BlockSpec(memory_space=pltpu.HBM)


class _Carried:
    def __init__(self, ins, out_shapes, n_sems, issue, drain):
        self.ins, self.out_shapes, self.n_sems, self.issue, self.drain = list(ins), list(out_shapes), tuple(n_sems), issue, drain


def _carrier_call(name, body, grid, in_specs, out_specs, out_shape, scratch_shapes, operands, carry):
    n_in, n_out, n_scr = len(in_specs), len(out_specs), len(scratch_shapes)
    if carry is None:
        res = pl.pallas_call(body, grid=grid, in_specs=in_specs, out_specs=out_specs, out_shape=out_shape, scratch_shapes=scratch_shapes,
                             compiler_params=_params(("arbitrary",) * len(grid)), name=name)(*operands)
        return list(res), []
    ci, co = len(carry.ins), len(carry.out_shapes)

    def wrapped(*refs):
        ins, cins = refs[:n_in], refs[n_in:n_in + ci]
        outs, couts = refs[n_in + ci:n_in + ci + n_out], refs[n_in + ci + n_out:n_in + ci + n_out + co]
        scr, sems = refs[n_in + ci + n_out + co:n_in + ci + n_out + co + n_scr], refs[n_in + ci + n_out + co + n_scr:]
        first = functools.reduce(jnp.logical_and, [pl.program_id(a) == 0 for a in range(len(grid))])
        last = functools.reduce(jnp.logical_and, [pl.program_id(a) == grid[a] - 1 for a in range(len(grid))])

        @pl.when(first)
        def _():
            carry.issue(cins, couts, sems)

        body(*ins, *outs, *scr)

        @pl.when(last)
        def _():
            carry.drain(cins, couts, sems)

    res = pl.pallas_call(
        wrapped, grid=grid, in_specs=list(in_specs) + [HBM_SPEC] * ci, out_specs=list(out_specs) + [HBM_SPEC] * co,
        out_shape=list(out_shape) + carry.out_shapes,
        scratch_shapes=list(scratch_shapes) + [pltpu.SemaphoreType.DMA((k,)) for k in carry.n_sems],
        compiler_params=pltpu.CompilerParams(dimension_semantics=("arbitrary",) * len(grid), vmem_limit_bytes=VMEM_LIMIT, has_side_effects=True),
        name=name)(*operands, *carry.ins)
    return list(res[:n_out]), list(res[n_out:])


def _run_carried(name, carry):
    def body(*refs):
        ci, co = len(carry.ins), len(carry.out_shapes)
        carry.issue(refs[:ci], refs[ci:ci + co], refs[ci + co:])
        carry.drain(refs[:ci], refs[ci:ci + co], refs[ci + co:])

    return pl.pallas_call(
        body, in_specs=[HBM_SPEC] * len(carry.ins), out_specs=[HBM_SPEC] * len(carry.out_shapes), out_shape=carry.out_shapes,
        scratch_shapes=[pltpu.SemaphoreType.DMA((k,)) for k in carry.n_sems],
        compiler_params=pltpu.CompilerParams(has_side_effects=True), name=name)(*carry.ins)


NA_BLOCKS = SEQ // NA_QB
NA_ROWS_TOTAL = SEQ // GRID_W
NA_CLASSES = ((0, 0), (8, 4), (NA_ROWS_TOTAL - NA_QROWS, NA_ROWS_TOTAL - NA_WROWS))


def _na_pairs(i0, ws):
    out = []
    for qi in range(NA_QROWS):
        i = i0 + qi
        rs = min(max(i - 4, 0), NA_ROWS_TOTAL - 8)
        for kr in range(NA_WROWS):
            r = ws + kr
            if rs <= r < rs + 8:
                out.append((qi, kr, r - i + 7))
    return out


def _diag_onehot():
    qc, kc = np.meshgrid(np.arange(GRID_W), np.arange(GRID_W), indexing="ij")
    e = np.zeros((GRID_W * GRID_W, 128), np.float32)
    j = (kc - qc + 15).reshape(-1)
    ok = (j >= 0) & (j <= 30)
    e[np.arange(GRID_W * GRID_W)[ok], j[ok]] = 1.0
    return jnp.asarray(e)


def _rpb_expand(rpb):
    r2 = jnp.pad(rpb.reshape(NH * 15, 31), ((0, 0), (0, 128 - 31)))

    def body(r_ref, e_ref, o_ref):
        o_ref[...] = lax.dot_general(r_ref[...], e_ref[...], _NT, preferred_element_type=F32, precision=lax.Precision.HIGHEST)

    out = pl.pallas_call(body, out_shape=jax.ShapeDtypeStruct((NH * 15, GRID_W * GRID_W), F32), name="rpb_expand",
                         compiler_params=pltpu.CompilerParams(vmem_limit_bytes=VMEM_LIMIT))(r2, _diag_onehot())
    return out.reshape(NH, 15, GRID_W, GRID_W)


def _na_bias_tiles(rpb, carry):
    def body(b_ref, o_ref):
        qc = lax.broadcasted_iota(jnp.int32, (GRID_W, GRID_W), 0)
        kc = lax.broadcasted_iota(jnp.int32, (GRID_W, GRID_W), 1)
        first = jnp.clip(qc - 8, 0, GRID_W - 16)
        in_window = (kc >= first) & (kc < first + 16)
        neg = jnp.full((GRID_W, GRID_W), NEG, F32)
        for cls, (i0, ws) in enumerate(NA_CLASSES):
            @pl.when(pl.program_id(0) == cls)
            def _(i0=i0, ws=ws):
                pairs = {(qi, kr): dr for qi, kr, dr in _na_pairs(i0, ws)}
                masked = {dr: jnp.where(in_window, b_ref[dr], NEG) for dr in sorted(set(pairs.values()))}
                for qi in range(NA_QROWS):
                    for k2 in range(NA_WROWS // 2):
                        blocks = [masked[pairs[(qi, kr)]] if (qi, kr) in pairs else neg for kr in (2 * k2, 2 * k2 + 1)]
                        o_ref[qi * GRID_W:(qi + 1) * GRID_W, k2 * 128:(k2 + 1) * 128] = jnp.concatenate(blocks, axis=1)

    (tiles,), sent = _carrier_call(
        "na_bias_tiles", body, (3, NH), [pl.BlockSpec((None, 15, GRID_W, GRID_W), lambda c, h: (h, 0, 0, 0))],
        [pl.BlockSpec((None, None, NA_QB, NA_WIN), lambda c, h: (c, h, 0, 0))], [jax.ShapeDtypeStruct((3, NH, NA_QB, NA_WIN), F32)],
        [], (_rpb_expand(rpb),), carry)
    return tiles, sent


def _na_cls(b):
    return jnp.where(b == 0, 0, jnp.where(b == NA_BLOCKS - 1, 2, 1))


def _na_start(b):
    return pl.multiple_of(jnp.clip(b * NA_QROWS - 4, 0, NA_ROWS_TOTAL - NA_WROWS) * GRID_W, GRID_W)


NA_FWD_HPS = 8
NA_BWD_HPS = 4


def _na_in_specs(hps):
    lw = hps * HD
    nlw = DM // lw
    return [pl.BlockSpec((NA_QB, lw), lambda hp, b: (b, hp)),
            pl.BlockSpec((SEQ, lw), lambda hp, b: (0, nlw + hp)),
            pl.BlockSpec((SEQ, lw), lambda hp, b: (0, 2 * nlw + hp)),
            pl.BlockSpec((None, hps, NA_QB, NA_WIN), lambda hp, b: (_na_cls(b), hp, 0, 0))]


def _na_fwd(qkv, bias, carry):
    lw = NA_FWD_HPS * HD

    def body(q_ref, k_ref, v_ref, b_ref, o_ref):
        start = _na_start(pl.program_id(1))
        q = q_ref[...]
        kw = k_ref[pl.ds(start, NA_WIN), :]
        vw = v_ref[pl.ds(start, NA_WIN), :]
        outs = []
        for hh in range(NA_FWD_HPS):
            sl = slice(hh * HD, (hh + 1) * HD)
            s = lax.dot_general(q[:, sl] * QSCALE, kw[:, sl], _NT, preferred_element_type=F32) + b_ref[hh]
            p = jnp.exp(s - jnp.max(s, axis=-1, keepdims=True))
            l = jnp.sum(p, axis=-1, keepdims=True)
            outs.append(jnp.dot(p.astype(BF16), vw[:, sl], preferred_element_type=F32) / l)
        o_ref[...] = jnp.concatenate(outs, axis=1).astype(o_ref.dtype)

    (o,), sent = _carrier_call(
        "na_fwd", body, (NH // NA_FWD_HPS, NA_BLOCKS), _na_in_specs(NA_FWD_HPS), [pl.BlockSpec((NA_QB, lw), lambda hp, b: (b, hp))],
        [jax.ShapeDtypeStruct((SEQ, DM), BF16)], [], (qkv, qkv, qkv, bias), carry)
    return o, sent


def _na_bwd(qkv, bias, do, carry):
    lw = NA_BWD_HPS * HD

    def body(q_ref, k_ref, v_ref, b_ref, do_ref, dqkv_ref, z_ref, dk_acc, dv_acc):
        blk = pl.program_id(1)

        @pl.when(blk == 0)
        def _():
            dk_acc[...] = jnp.zeros_like(dk_acc)
            dv_acc[...] = jnp.zeros_like(dv_acc)
            z_ref[...] = jnp.zeros_like(z_ref)

        start = _na_start(blk)
        q = q_ref[...]
        do = do_ref[...]
        kw = k_ref[pl.ds(start, NA_WIN), :]
        vw = v_ref[pl.ds(start, NA_WIN), :]
        dqs, dks, dvs, dss = [], [], [], []
        for hh in range(NA_BWD_HPS):
            sl = slice(hh * HD, (hh + 1) * HD)
            qh = q[:, sl] * QSCALE
            s = lax.dot_general(qh, kw[:, sl], _NT, preferred_element_type=F32) + b_ref[hh]
            p = jnp.exp(s - jnp.max(s, axis=-1, keepdims=True))
            p = p / jnp.sum(p, axis=-1, keepdims=True)
            dp = lax.dot_general(do[:, sl], vw[:, sl], _NT, preferred_element_type=F32)
            ds = p * (dp - jnp.sum(p * dp, axis=-1, keepdims=True))
            dsb = ds.astype(BF16)
            dqs.append(jnp.dot(dsb, kw[:, sl], preferred_element_type=F32) * QSCALE)
            dks.append(lax.dot_general(qh, dsb, _TN, preferred_element_type=F32).T)
            dvs.append(lax.dot_general(do[:, sl], p.astype(BF16), _TN, preferred_element_type=F32).T)
            dss.append(ds)
        for cls, (i0, ws) in enumerate(NA_CLASSES):
            @pl.when(_na_cls(blk) == cls)
            def _(i0=i0, ws=ws):
                for hh, ds in enumerate(dss):
                    for qi, kr, dr in _na_pairs(i0, ws):
                        z_ref[hh, dr * GRID_W:(dr + 1) * GRID_W, :] += ds[qi * GRID_W:(qi + 1) * GRID_W, kr * GRID_W:(kr + 1) * GRID_W]
        dqkv_ref[0, pl.ds(pl.multiple_of(blk * NA_QB, NA_QB), NA_QB), :] = jnp.concatenate(dqs, axis=1).astype(dqkv_ref.dtype)
        dk_acc[pl.ds(start, NA_WIN), :] += jnp.concatenate(dks, axis=1)
        dv_acc[pl.ds(start, NA_WIN), :] += jnp.concatenate(dvs, axis=1)

        @pl.when(blk == NA_BLOCKS - 1)
        def _():
            dqkv_ref[1] = dk_acc[...].astype(dqkv_ref.dtype)
            dqkv_ref[2] = dv_acc[...].astype(dqkv_ref.dtype)

    (dqkv, z), sent = _carrier_call(
        "na_bwd", body, (NH // NA_BWD_HPS, NA_BLOCKS),
        _na_in_specs(NA_BWD_HPS) + [pl.BlockSpec((NA_QB, lw), lambda hp, b: (b, hp))],
        [pl.BlockSpec((3, SEQ, lw), lambda hp, b: (0, 0, hp)), pl.BlockSpec((NA_BWD_HPS, 15 * GRID_W, GRID_W), lambda hp, b: (hp, 0, 0))],
        [jax.ShapeDtypeStruct((3, SEQ, DM), BF16), jax.ShapeDtypeStruct((NH, 15 * GRID_W, GRID_W), F32)],
        [pltpu.VMEM((SEQ, lw), F32), pltpu.VMEM((SEQ, lw), F32)], (qkv, qkv, qkv, bias, do), carry)
    return dqkv, z, sent


def _rpb_grad(z):
    z2 = z.reshape(NH * 15, GRID_W * GRID_W)

    def body(z_ref, e_ref, o_ref):
        o_ref[...] = jnp.dot(z_ref[...], e_ref[...], preferred_element_type=F32, precision=lax.Precision.HIGHEST)

    out = pl.pallas_call(body, out_shape=jax.ShapeDtypeStruct((NH * 15, 128), F32), name="rpb_grad",
                         compiler_params=pltpu.CompilerParams(vmem_limit_bytes=VMEM_LIMIT))(z2, _diag_onehot())
    return out[:, :31].reshape(NH, 15, 31)


DIL_BLOCKS = SEQ // DIL_QB
DIL_HPS = 8
DIL_LW = DIL_HPS * HD
DIL_NLW = DM // DIL_LW


COLS = 128


def _col_spec():
    return pl.BlockSpec((SEQ, COLS), lambda j: (0, j))


def _grp_spec():
    return pl.BlockSpec((3, SEQ, COLS), lambda j: (0, 0, j))


def _store_group_order(dst_ref, src_ref):
    for g, d in enumerate(DIL):
        n = SEQ // d
        for r in range(d):
            dst_ref[g, r * n:(r + 1) * n, :] = src_ref[pl.ds(r, n, stride=d), :].astype(dst_ref.dtype)


def _store_token_order(dst_ref, src_ref, g):
    d = DIL[g]
    n = SEQ // d
    for r in range(d):
        dst_ref[pl.ds(r, n, stride=d), :] = src_ref[g, r * n:(r + 1) * n, :].astype(dst_ref.dtype)


def _to_groups(name, a, carry):
    def body(a_ref, o_ref, t_ref):
        _store_group_order(o_ref, a_ref)
        for g in range(3):
            t_ref[g] = o_ref[g].astype(F32).T.astype(t_ref.dtype)

    return _carrier_call(
        name, body, (DM // COLS,), [_col_spec()], [_grp_spec(), pl.BlockSpec((3, COLS, SEQ), lambda j: (0, j, 0))],
        [jax.ShapeDtypeStruct((3, SEQ, DM), BF16), jax.ShapeDtypeStruct((3, DM, SEQ), BF16)], [], (a,), carry)


def _from_groups_sum(name, a):
    def body(a_ref, o_ref, t1, t2):
        _store_token_order(t1, a_ref, 1)
        _store_token_order(t2, a_ref, 2)
        o_ref[...] = (a_ref[0] + t1[...]) + t2[...]

    return pl.pallas_call(body, grid=(DM // COLS,), in_specs=[_grp_spec()], out_specs=_col_spec(),
                          out_shape=jax.ShapeDtypeStruct((SEQ, DM), F32), scratch_shapes=[pltpu.VMEM((SEQ, COLS), F32)] * 2,
                          compiler_params=_params(("parallel",)), name=name)(a)


def _dil_start(b):
    return pl.multiple_of(jnp.clip(b * DIL_QB - DIL_RADIUS, 0, SEQ - DIL_WIN), DIL_RADIUS)


def _dil_neg_dist(g, ii, jj):
    shift = 11 - 2 * g
    dist = jnp.abs(ii - jj)
    valid = (dist <= DIL_RADIUS) & (jnp.right_shift(ii, shift) == jnp.right_shift(jj, shift))
    return jnp.where(valid, -dist.astype(F32), NEG)


def _dil_in_specs():
    return [pl.BlockSpec(memory_space=pltpu.SMEM),
            pl.BlockSpec((None, DIL_QB, DIL_LW), lambda g, hp, b: (g, b, hp)),
            pl.BlockSpec((None, SEQ, DIL_LW), lambda g, hp, b: (g, 0, DIL_NLW + hp)),
            pl.BlockSpec((None, SEQ, DIL_LW), lambda g, hp, b: (g, 0, 2 * DIL_NLW + hp))]


def _dil_fwd(qkv, slopes, carry):
    def body(sl_ref, q_ref, k_ref, v_ref, o_ref, lse_ref):
        g, hp, b = pl.program_id(0), pl.program_id(1), pl.program_id(2)
        start = _dil_start(b)
        neg_dist = _dil_neg_dist(g, b * DIL_QB + lax.broadcasted_iota(jnp.int32, (DIL_QB, DIL_WIN), 0),
                                 start + lax.broadcasted_iota(jnp.int32, (DIL_QB, DIL_WIN), 1))
        dil = jnp.left_shift(1, 2 * g).astype(F32)
        q = q_ref[...]
        kw = k_ref[pl.ds(start, DIL_WIN), :]
        vw = v_ref[pl.ds(start, DIL_WIN), :]
        outs, lses = [], []
        for hh in range(DIL_HPS):
            sl = slice(hh * HD, (hh + 1) * HD)
            s = lax.dot_general(q[:, sl] * QSCALE, kw[:, sl], _NT, preferred_element_type=F32)
            s = s + (sl_ref[hp * DIL_HPS + hh] * dil) * neg_dist
            m = jnp.max(s, axis=-1, keepdims=True)
            p = jnp.exp(s - m)
            l = jnp.sum(p, axis=-1, keepdims=True)
            outs.append(jnp.dot(p.astype(BF16), vw[:, sl], preferred_element_type=F32) / l)
            lses.append(jnp.broadcast_to(m + jnp.log(l), (DIL_QB, HD)))
        o_ref[...] = jnp.concatenate(outs, axis=1).astype(o_ref.dtype)
        lse_ref[...] = jnp.concatenate(lses, axis=1)

    ospec = pl.BlockSpec((None, DIL_QB, DIL_LW), lambda g, hp, b: (g, b, hp))
    (o, lse), sent = _carrier_call(
        "dil_fwd", body, (3, DIL_NLW, DIL_BLOCKS), _dil_in_specs(), [ospec, ospec],
        [jax.ShapeDtypeStruct((3, SEQ, DM), BF16), jax.ShapeDtypeStruct((3, SEQ, DM), F32)], [], (slopes, qkv, qkv, qkv), carry)
    return o, lse, sent


def _dil_merge(o_all, lse_all):
    def body(o_ref, l_ref, out_ref, lse_ref, o1, o2, l1, l2):
        for g, (ot, lt) in ((1, (o1, l1)), (2, (o2, l2))):
            _store_token_order(ot, o_ref, g)
            _store_token_order(lt, l_ref, g)
        la, lb, lc = l_ref[0], l1[...], l2[...]
        m = jnp.maximum(jnp.maximum(la, lb), lc)
        wa, wb, wc = jnp.exp(la - m), jnp.exp(lb - m), jnp.exp(lc - m)
        sw = (wa + wb) + wc
        out_ref[...] = (((wa * o_ref[0].astype(F32) + wb * o1[...]) + wc * o2[...]) / sw).astype(out_ref.dtype)
        lse_ref[...] = m + jnp.log(sw)

    return pl.pallas_call(
        body, grid=(DM // COLS,), in_specs=[_grp_spec(), _grp_spec()], out_specs=[_col_spec(), _col_spec()],
        out_shape=[jax.ShapeDtypeStruct((SEQ, DM), BF16), jax.ShapeDtypeStruct((SEQ, DM), F32)],
        scratch_shapes=[pltpu.VMEM((SEQ, COLS), F32)] * 4, compiler_params=_params(("parallel",)), name="dil_merge")(o_all, lse_all)


def _dil_bwd_prep(do, o, lse):
    heads = COLS // HD

    def body(do_ref, o_ref, lse_ref, dog_ref, ddr_ref, lser_ref, dd, grp):
        prod = do_ref[...] * o_ref[...].astype(F32)
        dd[...] = jnp.concatenate(
            [jnp.broadcast_to(jnp.sum(prod[:, h * HD:(h + 1) * HD], axis=-1, keepdims=True), (SEQ, HD)) for h in range(heads)], axis=1)
        _store_group_order(dog_ref, do_ref)
        for src, dst in ((dd, ddr_ref), (lse_ref, lser_ref)):
            _store_group_order(grp, src)
            for g in range(3):
                t = grp[g].T
                for h in range(heads):
                    dst[g, h] = t[h * HD:h * HD + 8, :]

    rows = jax.ShapeDtypeStruct((3, NH, 8, SEQ), F32)
    rspec = pl.BlockSpec((3, heads, 8, SEQ), lambda j: (0, j, 0, 0))
    return pl.pallas_call(
        body, grid=(DM // COLS,), in_specs=[_col_spec()] * 3, out_specs=[_grp_spec(), rspec, rspec],
        out_shape=[jax.ShapeDtypeStruct((3, SEQ, DM), BF16), rows, rows],
        scratch_shapes=[pltpu.VMEM((SEQ, COLS), F32), pltpu.VMEM((3, SEQ, COLS), F32)],
        compiler_params=_params(("parallel",)), name="dil_bwd_prep")(do, o, lse)


def _dil_bwd(qkv, do, dd, lse, slopes, carry):
    def body(sl_ref, q_ref, k_ref, v_ref, do_ref, dd_ref, lse_ref, dqkv_ref, dk_acc, dv_acc):
        g, hp, b = pl.program_id(0), pl.program_id(1), pl.program_id(2)

        @pl.when(b == 0)
        def _():
            dk_acc[...] = jnp.zeros_like(dk_acc)
            dv_acc[...] = jnp.zeros_like(dv_acc)

        start = _dil_start(b)
        neg_dist = _dil_neg_dist(g, b * DIL_QB + lax.broadcasted_iota(jnp.int32, (DIL_WIN, DIL_QB), 1),
                                 start + lax.broadcasted_iota(jnp.int32, (DIL_WIN, DIL_QB), 0))
        dil = jnp.left_shift(1, 2 * g).astype(F32)
        q = q_ref[...]
        do = do_ref[...]
        kw = k_ref[pl.ds(start, DIL_WIN), :]
        vw = v_ref[pl.ds(start, DIL_WIN), :]
        dqs, dks, dvs = [], [], []
        for hh in range(DIL_HPS):
            sl = slice(hh * HD, (hh + 1) * HD)
            qh = q[:, sl] * QSCALE
            st = lax.dot_general(kw[:, sl], qh, _NT, preferred_element_type=F32)
            st = st + (sl_ref[hp * DIL_HPS + hh] * dil) * neg_dist
            pt = jnp.exp(st - lse_ref[hh, 0:1, :])
            dpt = lax.dot_general(vw[:, sl], do[:, sl], _NT, preferred_element_type=F32)
            dst = (pt * (dpt - dd_ref[hh, 0:1, :])).astype(BF16)
            dqs.append(lax.dot_general(kw[:, sl], dst, _TN, preferred_element_type=F32).T * QSCALE)
            dks.append(jnp.dot(dst, qh, preferred_element_type=F32))
            dvs.append(jnp.dot(pt.astype(BF16), do[:, sl], preferred_element_type=F32))
        dqkv_ref[0, pl.ds(pl.multiple_of(b * DIL_QB, DIL_QB), DIL_QB), :] = jnp.concatenate(dqs, axis=1).astype(dqkv_ref.dtype)
        dk_acc[pl.ds(start, DIL_WIN), :] += jnp.concatenate(dks, axis=1)
        dv_acc[pl.ds(start, DIL_WIN), :] += jnp.concatenate(dvs, axis=1)

        @pl.when(b == DIL_BLOCKS - 1)
        def _():
            dqkv_ref[1] = dk_acc[...].astype(dqkv_ref.dtype)
            dqkv_ref[2] = dv_acc[...].astype(dqkv_ref.dtype)

    qspec = pl.BlockSpec((None, DIL_QB, DIL_LW), lambda g, hp, b: (g, b, hp))
    rspec = pl.BlockSpec((None, DIL_HPS, 8, DIL_QB), lambda g, hp, b: (g, hp, 0, b))
    (dqkv,), sent = _carrier_call(
        "dil_bwd", body, (3, DIL_NLW, DIL_BLOCKS), _dil_in_specs() + [qspec, rspec, rspec],
        [pl.BlockSpec((None, 3, SEQ, DIL_LW), lambda g, hp, b: (g, 0, 0, hp))], [jax.ShapeDtypeStruct((3, 3, SEQ, DM), BF16)],
        [pltpu.VMEM((SEQ, DIL_LW), F32), pltpu.VMEM((SEQ, DIL_LW), F32)], (slopes, qkv, qkv, qkv, do, dd, lse), carry)
    return dqkv, sent


def _ffn_fwd(name, x, g_pre, g_post, wgt4, wut4, wd4, carry):
    tm = 512

    def body(x_ref, gpre_ref, gpost_ref, wg_ref, wu_ref, wd_ref, xn_ref, h_ref, gate_ref, up_ref, u_ref, acc):
        s = pl.program_id(1)

        @pl.when(s == 0)
        def _():
            x = x_ref[...]
            r = lax.rsqrt(jnp.mean(x * x, axis=-1, keepdims=True) + RMS_EPS)
            h_ref[...] = (x * r * gpre_ref[...]).astype(h_ref.dtype)

        h = h_ref[...]
        gate = lax.dot_general(h, wg_ref[...], _NT, preferred_element_type=F32).astype(BF16)
        up = lax.dot_general(h, wu_ref[...], _NT, preferred_element_type=F32).astype(BF16)
        gate_ref[...] = gate
        up_ref[...] = up
        gf = gate.astype(F32)
        act = (gf * jax.nn.sigmoid(gf) * up.astype(F32)).astype(BF16)
        part = jnp.dot(act, wd_ref[...], preferred_element_type=F32)

        @pl.when(s == 0)
        def _():
            acc[...] = part

        @pl.when(s > 0)
        def _():
            acc[...] += part

        @pl.when(s == NCHIP - 1)
        def _():
            u = acc[...]
            u_ref[...] = u
            r = lax.rsqrt(jnp.mean(u * u, axis=-1, keepdims=True) + RMS_EPS)
            xn_ref[...] = x_ref[...] + u * r * gpost_ref[...]

    rows = pl.BlockSpec((tm, DM), lambda i, s: (i, 0))
    vec = pl.BlockSpec((1, DM), lambda i, s: (0, 0))
    wspec = _ffn_wspec(lambda i, s: (s, 0, 0))
    mid = pl.BlockSpec((None, tm, FSH), lambda i, s: (s, i, 0))
    outs, sent = _carrier_call(
        name, body, (SEQ // tm, NCHIP), [rows, vec, vec, wspec, wspec, wspec], [rows, rows, mid, mid, rows],
        [jax.ShapeDtypeStruct((SEQ, DM), F32), jax.ShapeDtypeStruct((SEQ, DM), BF16), jax.ShapeDtypeStruct((NCHIP, SEQ, FSH), BF16),
         jax.ShapeDtypeStruct((NCHIP, SEQ, FSH), BF16), jax.ShapeDtypeStruct((SEQ, DM), F32)],
        [pltpu.VMEM((tm, DM), F32)], (x, g_pre, g_post, wgt4, wut4, wd4), carry)
    return outs, sent


def _ffn_block(layer, x, g_pre, g_post, ex):
    tag = f"l{layer}_ffn_fwd"
    (x_new, h, gate, up, u), sent = _ffn_fwd(tag, x, g_pre, g_post, ex.weight(("ffn_w_gate", layer)), ex.weight(("ffn_w_up", layer)),
                                             ex.weight(("ffn_w_down", layer)), ex.carry(tag))
    ex.carried(tag, sent)
    return x_new, (x, h, gate, up, u)


def _ffn_bwd(name, dx, x, gate, up, u, g_pre, g_post, wgt4, wut4, wd4, carry):
    tm = 512

    def body(dx_ref, x_ref, gate_ref, up_ref, u_ref, gpre_ref, gpost_ref, wg_ref, wu_ref, wd_ref,
             dxin_ref, du_ref, dgate_ref, dup_ref, act_ref, dgpre_ref, dgpost_ref, dh_acc):
        i, s = pl.program_id(0), pl.program_id(1)

        @pl.when((i == 0) & (s == 0))
        def _():
            dgpre_ref[...] = jnp.zeros_like(dgpre_ref)
            dgpost_ref[...] = jnp.zeros_like(dgpost_ref)

        @pl.when(s == 0)
        def _():
            dy = dx_ref[...]
            uu = u_ref[...]
            r = lax.rsqrt(jnp.mean(uu * uu, axis=-1, keepdims=True) + RMS_EPS)
            yh = uu * r
            t = dy * gpost_ref[...]
            du_ref[...] = (r * (t - yh * jnp.mean(t * yh, axis=-1, keepdims=True))).astype(du_ref.dtype)
            dgpost_ref[...] += jnp.sum(dy * yh, axis=0, keepdims=True)

        dact = lax.dot_general(du_ref[...], wd_ref[...], _NT, preferred_element_type=F32)
        g = gate_ref[...].astype(F32)
        upv = up_ref[...].astype(F32)
        sg = jax.nn.sigmoid(g)
        dgate = (dact * upv * sg * (1.0 + g * (1.0 - sg))).astype(BF16)
        dup = (dact * g * sg).astype(BF16)
        dgate_ref[...] = dgate
        dup_ref[...] = dup
        act_ref[...] = (g * sg * upv).astype(act_ref.dtype)
        part = jnp.dot(dgate, wg_ref[...], preferred_element_type=F32) + jnp.dot(dup, wu_ref[...], preferred_element_type=F32)

        @pl.when(s == 0)
        def _():
            dh_acc[...] = part

        @pl.when(s > 0)
        def _():
            dh_acc[...] += part

        @pl.when(s == NCHIP - 1)
        def _():
            dh = dh_acc[...]
            xx = x_ref[...]
            r = lax.rsqrt(jnp.mean(xx * xx, axis=-1, keepdims=True) + RMS_EPS)
            yh = xx * r
            t = dh * gpre_ref[...]
            dxin_ref[...] = dx_ref[...] + r * (t - yh * jnp.mean(t * yh, axis=-1, keepdims=True))
            dgpre_ref[...] += jnp.sum(dh * yh, axis=0, keepdims=True)

    rows = pl.BlockSpec((tm, DM), lambda i, s: (i, 0))
    vec = pl.BlockSpec((1, DM), lambda i, s: (0, 0))
    wspec = _ffn_wspec(lambda i, s: (s, 0, 0))
    mid = pl.BlockSpec((None, tm, FSH), lambda i, s: (s, i, 0))
    mid_shape = jax.ShapeDtypeStruct((NCHIP, SEQ, FSH), BF16)
    return _carrier_call(
        name, body, (SEQ // tm, NCHIP), [rows, rows, mid, mid, rows, vec, vec, wspec, wspec, wspec], [rows, rows, mid, mid, mid, vec, vec],
        [jax.ShapeDtypeStruct((SEQ, DM), F32), jax.ShapeDtypeStruct((SEQ, DM), BF16), mid_shape, mid_shape, mid_shape,
         jax.ShapeDtypeStruct((1, DM), F32), jax.ShapeDtypeStruct((1, DM), F32)],
        [pltpu.VMEM((tm, DM), F32)], (dx, x, gate, up, u, g_pre, g_post, wgt4, wut4, wd4), carry)


def _ffn_block_bwd(layer, dx, saved, g_pre, g_post, ex):
    tag = f"l{layer}"
    x, h, gate, up, u = saved
    (dx_in, du, dgate, dup, act, dg_pre, dg_post), sent = _ffn_bwd(
        f"{tag}_ffn_bwd", dx, x, gate, up, u, g_pre, g_post, ex.weight(("ffn_w_gate", layer)), ex.weight(("ffn_w_up", layer)),
        ex.weight(("ffn_w_down", layer)), ex.carry(f"{tag}_ffn_bwd"))
    ex.carried(f"{tag}_ffn_bwd", sent)
    d_wd = _ffn_bwd_dw(f"{tag}_dwd", act, du)
    d_wg = _ffn_bwd_dw(f"{tag}_dwg", dgate, h)
    d_wu = _ffn_bwd_dw(f"{tag}_dwu", dup, h)
    ex.grads(f"{tag}_ffn", {("ffn_w_gate", layer): d_wg, ("ffn_w_up", layer): d_wu, ("ffn_w_down", layer): d_wd})
    return dx_in, dg_pre, dg_post


def _alibi_slopes():
    return 2.0 ** (-8.0 * jnp.arange(1, NH + 1, dtype=F32) / NH)


def _local_step(x, target, norms, rpb, ex):
    g_mix_pre, g_mix_post, g_ffn_pre, g_ffn_post = norms
    row = lambda a, i: a[i:i + 1]

    bias, sent = _na_bias_tiles(rpb, ex.carry("na_bias_tiles"))
    ex.carried("na_bias_tiles", sent)
    h0, h0t = _rms_fwd_both("l0_mix_pre", x, row(g_mix_pre, 0))
    qkv0, sent = _qkv_fwd("l0_qkv", h0[None], ex.weight(("na_w_qkv", 0)), ex.carry("l0_qkv"))
    ex.carried("l0_qkv", sent)
    o0, sent = _na_fwd(qkv0[0], bias, ex.carry("na_fwd"))
    ex.carried("na_fwd", sent)
    na_wo = ex.weight(("na_w_o", 0)).reshape(DM, DM)
    x1, u0 = _proj_fwd("l0_proj", o0, na_wo, x, row(g_mix_post, 0))
    x2, ffn0 = _ffn_block(0, x1, row(g_ffn_pre, 0), row(g_ffn_post, 0), ex)

    slopes = _alibi_slopes()
    (h2g, h2gt), sent = _to_groups("l1_h_groups", _rms_fwd("l1_mix_pre", x2, row(g_mix_pre, 1), F32), ex.carry("l1_h_groups"))
    ex.carried("l1_h_groups", sent)
    dil_wqkv = ex.weight(("dil_w_qkv", 0))
    qkv1, sent = _qkv_fwd("l1_qkv", h2g, dil_wqkv, ex.carry("l1_qkv"))
    ex.carried("l1_qkv", sent)
    og, lg, sent = _dil_fwd(qkv1, slopes, ex.carry("dil_fwd"))
    ex.carried("dil_fwd", sent)
    o1, lse = _dil_merge(og, lg)
    dil_wo = ex.weight(("dil_w_o", 0)).reshape(DM, DM)
    x3, u1 = _proj_fwd("l1_proj", o1, dil_wo, x2, row(g_mix_post, 1))
    x4, ffn1 = _ffn_block(1, x3, row(g_ffn_pre, 1), row(g_ffn_post, 1), ex)

    dx4, loss_row = _loss_grad("loss", x4, target)

    dx3, dg_fpre1, dg_fpost1 = _ffn_block_bwd(1, dx4, ffn1, row(g_ffn_pre, 1), row(g_ffn_post, 1), ex)
    (do1, du1, dg_mpost1), sent = _proj_bwd("l1_proj_bwd", dx3, u1, row(g_mix_post, 1), dil_wo, F32, ex.carry("l1_proj_bwd"))
    ex.carried("l1_proj_bwd", sent)
    d_dil_wo = _proj_bwd_dw("l1_dwo", o1, du1)
    dog, ddg, lseg = _dil_bwd_prep(do1, o1, lse)
    dqkv1, sent = _dil_bwd(qkv1, dog, ddg, lseg, slopes, ex.carry("dil_bwd"))
    ex.carried("dil_bwd", sent)
    d_dil_wqkv, sent = _qkv_bwd_dw("l1_dwqkv", h2gt, dqkv1, dil_wqkv.shape[2], ex.carry("l1_dwqkv"))
    ex.carried("l1_dwqkv", sent)
    ex.grads("l1_mix", {("dil_w_qkv", 0): d_dil_wqkv, ("dil_w_o", 0): d_dil_wo.reshape(NCHIP, DM // NCHIP, DM)})
    dh2g, sent = _qkv_bwd_dh("l1_dh", dqkv1, dil_wqkv, ex.carry("l1_dh"))
    ex.carried("l1_dh", sent)
    dh2 = _from_groups_sum("l1_dh_tokens", dh2g)
    (dx2, dg_mpre1), sent = _norm_bwd("l1_mix_pre_bwd", dh2, x2, row(g_mix_pre, 1), dx3, ex.carry("l1_mix_pre_bwd"))
    ex.carried("l1_mix_pre_bwd", sent)

    dx1, dg_fpre0, dg_fpost0 = _ffn_block_bwd(0, dx2, ffn0, row(g_ffn_pre, 0), row(g_ffn_post, 0), ex)
    (do0, du0, dg_mpost0), sent = _proj_bwd("l0_proj_bwd", dx1, u0, row(g_mix_post, 0), na_wo, BF16, ex.carry("l0_proj_bwd"))
    ex.carried("l0_proj_bwd", sent)
    d_na_wo = _proj_bwd_dw("l0_dwo", o0, du0)
    dqkv0, z, sent = _na_bwd(qkv0[0], bias, do0, ex.carry("na_bwd"))
    ex.carried("na_bwd", sent)
    d_rpb = _rpb_grad(z)
    na_wqkv = ex.weight(("na_w_qkv", 0))
    d_na_wqkv, sent = _qkv_bwd_dw("l0_dwqkv", h0t[None], dqkv0[None], na_wqkv.shape[2], ex.carry("l0_dwqkv"))
    ex.carried("l0_dwqkv", sent)
    ex.grads("l0_mix", {("na_w_qkv", 0): d_na_wqkv, ("na_w_o", 0): d_na_wo.reshape(NCHIP, DM // NCHIP, DM)})
    dh0, sent = _qkv_bwd_dh("l0_dh", dqkv0[None], na_wqkv, ex.carry("l0_dh"))
    ex.carried("l0_dh", sent)
    (dx0, dg_mpre0), sent = _norm_bwd("l0_mix_pre_bwd", dh0[0], x, row(g_mix_pre, 0), dx1, ex.carry("l0_mix_pre_bwd"))
    ex.carried("l0_mix_pre_bwd", sent)

    dnorms = (jnp.concatenate([dg_mpre0, dg_mpre1]), jnp.concatenate([dg_mpost0, dg_mpost1]),
              jnp.concatenate([dg_fpre0, dg_fpre1]), jnp.concatenate([dg_fpost0, dg_fpost1]))
    return loss_row, dx0, dnorms, d_rpb


def _place():
    x, y, c = lax.axis_index("x"), lax.axis_index("y"), lax.axis_index("c")
    chips = ((1 - x, y), (x, 1 - y), (1 - x, 1 - y))
    return x, y, c, chips


def _chip_id(chip):
    return 2 * chip[0] + chip[1]


def _gather_copies(items):
    flat = [(t, i, j) for t, (_, peers) in enumerate(items) for i, j in enumerate(peers)]
    n_flat = len(flat)

    def copies(src, out, sems):
        send_sems, recv_sems = sems
        x, y, c, chips = _place()

        def copy(k, t, chip, half, to, from_src=False):
            blk = out[t].at[_chip_id(chip), half]
            return pltpu.make_async_remote_copy(
                src_ref=src[t].at[half] if from_src else blk, dst_ref=blk,
                send_sem=send_sems.at[k], recv_sem=recv_sems.at[k], device_id=to, device_id_type=MESH)

        return copy, x, y, c, chips

    def issue(src, out, sems):
        copy, x, y, c, chips = copies(src, out, sems)
        for k, (t, _, j) in enumerate(flat):
            copy(k, t, (x, y), c, (*chips[j], c), from_src=True).start()

    def drain(src, out, sems):
        copy, x, y, c, chips = copies(src, out, sems)
        passed = []
        for k, (t, _, j) in enumerate(flat):
            copy(k, t, chips[j], c, (x, y, c)).wait_recv()
            fwd = copy(n_flat + k, t, chips[j], c, (x, y, 1 - c))
            fwd.start()
            passed.append(fwd)
        for k, (t, _, j) in enumerate(flat):
            copy(n_flat + k, t, chips[j], 1 - c, (x, y, c)).wait_recv()
        for k, (t, _, j) in enumerate(flat):
            copy(k, t, (x, y), c, (*chips[j], c), from_src=True).wait_send()
        for cp in passed:
            cp.wait_send()

    return _Carried([s for s, _ in items], [jax.ShapeDtypeStruct((NCHIP,) + s.shape, s.dtype) for s, _ in items],
                    (2 * n_flat, 2 * n_flat), issue, drain)


def _pair_exchange_copies(grads):
    n = len(grads)

    def copies(g, theirs, sems):
        send_sems, recv_sems = sems
        x, y, c, _ = _place()
        return [pltpu.make_async_remote_copy(src_ref=g[t].at[:, 1 - c], dst_ref=theirs[t], send_sem=send_sems.at[t],
                                             recv_sem=recv_sems.at[t], device_id=(x, y, 1 - c), device_id_type=MESH) for t in range(n)]

    def issue(g, theirs, sems):
        for cp in copies(g, theirs, sems):
            cp.start()

    def drain(g, theirs, sems):
        for cp in copies(g, theirs, sems):
            cp.wait()

    return _Carried(grads, [jax.ShapeDtypeStruct((NCHIP,) + g.shape[2:], g.dtype) for g in grads], (n, n), issue, drain)


def _chip_exchange_copies(items):
    flat = [(t, i, j) for t, (_, peers) in enumerate(items) for i, j in enumerate(peers)]

    def copies(p, slots, sems):
        send_sems, recv_sems = sems
        x, y, c, chips = _place()
        return [pltpu.make_async_remote_copy(src_ref=p[t].at[_chip_id(chips[j])], dst_ref=slots[t].at[i], send_sem=send_sems.at[k],
                                             recv_sem=recv_sems.at[k], device_id=(*chips[j], c), device_id_type=MESH)
                for k, (t, i, j) in enumerate(flat)]

    def issue(p, slots, sems):
        for cp in copies(p, slots, sems):
            cp.start()

    def drain(p, slots, sems):
        for cp in copies(p, slots, sems):
            cp.wait()

    return _Carried([p for p, _ in items], [jax.ShapeDtypeStruct((len(peers),) + p.shape[1:], p.dtype) for p, peers in items],
                    (len(flat), len(flat)), issue, drain)


def _pair_share_copies(halves):
    n = len(halves)

    def copies(h, other, sems):
        send_sems, recv_sems = sems
        x, y, c, _ = _place()
        return [pltpu.make_async_remote_copy(src_ref=h[t], dst_ref=other[t], send_sem=send_sems.at[t], recv_sem=recv_sems.at[t],
                                             device_id=(x, y, 1 - c), device_id_type=MESH) for t in range(n)]

    def issue(h, other, sems):
        for cp in copies(h, other, sems):
            cp.start()

    def drain(h, other, sems):
        for cp in copies(h, other, sems):
            cp.wait()

    return _Carried(halves, [jax.ShapeDtypeStruct(h.shape, h.dtype) for h in halves], (n, n), issue, drain)


SMALL_ROWS = 128


def _allreduce_small(v, carry):
    ci, co = len(carry.ins), len(carry.out_shapes)

    def body(*refs):
        v_ref, cins, o_ref, couts = refs[0], refs[1:1 + ci], refs[1 + ci], refs[2 + ci:2 + ci + co]
        buf, send_sems, recv_sems = refs[2 + ci + co:5 + ci + co]
        csems = refs[5 + ci + co:]
        carry.issue(cins, couts, csems)
        x, y, c, _ = _place()
        me = 4 * x + 2 * y + c
        flip = lambda a, f: 1 - a if f else a
        buf[me] = v_ref[...]
        peers = [(flip(x, d >> 2 & 1), flip(y, d >> 1 & 1), flip(c, d & 1)) for d in range(1, 8)]
        sends = [pltpu.make_async_remote_copy(src_ref=v_ref, dst_ref=buf.at[me], send_sem=send_sems.at[i], recv_sem=recv_sems.at[i],
                                              device_id=peer, device_id_type=MESH) for i, peer in enumerate(peers)]
        for cp in sends:
            cp.start()
        for i, (px, py, pc) in enumerate(peers):
            pltpu.make_async_remote_copy(src_ref=v_ref, dst_ref=buf.at[4 * px + 2 * py + pc], send_sem=send_sems.at[i], recv_sem=recv_sems.at[i],
                                         device_id=(px, py, pc), device_id_type=MESH).wait_recv()
        for cp in sends:
            cp.wait_send()
        acc = buf[0]
        for k in range(1, 8):
            acc = acc + buf[k]
        o_ref[...] = acc
        carry.drain(cins, couts, csems)

    vm = pl.BlockSpec(memory_space=pltpu.VMEM)
    res = pl.pallas_call(
        body, in_specs=[vm] + [HBM_SPEC] * ci, out_specs=[vm] + [HBM_SPEC] * co,
        out_shape=[jax.ShapeDtypeStruct((SMALL_ROWS, 128), F32)] + carry.out_shapes,
        scratch_shapes=[pltpu.VMEM((8, SMALL_ROWS, 128), F32), pltpu.SemaphoreType.DMA((7,)), pltpu.SemaphoreType.DMA((7,))]
        + [pltpu.SemaphoreType.DMA((k,)) for k in carry.n_sems],
        compiler_params=pltpu.CompilerParams(has_side_effects=True), name="allreduce_small")(v, *carry.ins)
    return res[0], list(res[1:])


def _row_block(rows, cols, budget=3 << 19):
    best = 8
    for bm in range(8, rows + 1, 8):
        if rows % bm == 0 and bm * cols * 4 <= budget:
            best = bm
    return best


def _pair_sum(name, place, gs, theirs):
    n = len(gs)
    _, m, c = theirs[0].shape
    bm = _row_block(m, c)

    def body(place_ref, *refs):
        for a_ref, b_ref, o_ref in zip(refs[:n], refs[n:2 * n], refs[2 * n:]):
            o_ref[...] = (a_ref[...].astype(F32) + b_ref[...].astype(F32)).astype(o_ref.dtype)

    spec = pl.BlockSpec((None, bm, c), lambda k, i, pr: (k, i, 0))
    return pl.pallas_call(
        body, out_shape=[jax.ShapeDtypeStruct(theirs[0].shape, BF16)] * n,
        grid_spec=pltpu.PrefetchScalarGridSpec(
            num_scalar_prefetch=1, grid=(NCHIP, m // bm),
            in_specs=[pl.BlockSpec((None, None, bm, c), lambda k, i, pr: (k, pr[0], i, 0))] * n + [spec] * n, out_specs=[spec] * n),
        compiler_params=_params(("parallel", "parallel")), name=name)(place, *gs, *theirs)


def _chip_sum(name, place, parts, slots):
    n, ns = len(parts), len(slots[0])
    _, m, c = parts[0].shape
    bm = _row_block(m, c)

    def body(place_ref, *refs):
        for t in range(n):
            acc = refs[t][...].astype(F32)
            for s_ref in refs[n + t * ns:n + (t + 1) * ns]:
                for i in range(s_ref.shape[0]):
                    acc = acc + s_ref[i].astype(F32)
            refs[n + n * ns + t][...] = acc

    half = pl.BlockSpec((bm, c), lambda i, pr: (i, 0))
    return pl.pallas_call(
        body, out_shape=[jax.ShapeDtypeStruct((m, c), F32)] * n,
        grid_spec=pltpu.PrefetchScalarGridSpec(
            num_scalar_prefetch=1, grid=(m // bm,),
            in_specs=[pl.BlockSpec((None, bm, c), lambda i, pr: (pr[1], i, 0))] * n
            + [pl.BlockSpec((s.shape[0], bm, c), lambda i, pr: (0, i, 0)) for group in slots for s in group],
            out_specs=[half] * n),
        compiler_params=_params(("parallel",)), name=name)(place, *parts, *[s for group in slots for s in group])


def _adamw(name, place, tensors, layer=0, into=None):
    n = len(tensors)
    lead, rows, cols = tensors[0][0].shape
    bm = _row_block(rows // 2, cols, budget=768 * 1024 // n)
    per_half = rows // 2 // bm
    c1 = 1.0 - ADAM_B1 ** ADAM_STEP
    c2 = 1.0 - ADAM_B2 ** ADAM_STEP

    def body(place_ref, *refs):
        outs = refs[len(refs) - 4 * n:]
        for t in range(n):
            w_ref, ga_ref, gb_ref, m_ref, v_ref = refs[5 * t:5 * t + 5]
            go_ref, d_ref, mo_ref, vo_ref = outs[4 * t:4 * t + 4]
            g = jnp.where(pl.program_id(0) // per_half == place_ref[0], ga_ref[...], gb_ref[...])
            mn = ADAM_B1 * m_ref[...] + (1.0 - ADAM_B1) * g
            vn = ADAM_B2 * v_ref[...] + (1.0 - ADAM_B2) * (g * g)
            go_ref[...] = g
            mo_ref[...] = mn
            vo_ref[...] = vn
            d_ref[...] = -ADAM_LR * ((mn / c1) / (jnp.sqrt(vn / c2) + ADAM_EPS) + ADAM_WD * w_ref[...])

    spec = pl.BlockSpec((None, bm, cols), lambda i, pr: (layer, i, 0))

    def half_spec(mine):
        def index(i, pr):
            first = (pr[0] == 0) == mine
            park = jnp.where(first, per_half - 1, 0)
            return jnp.where((i < per_half) == first, i % per_half, park), 0
        return pl.BlockSpec((bm, cols), index)
    sh = jax.ShapeDtypeStruct((lead, rows, cols), F32)
    prev = [] if into is None else [a for res in into for a in res]
    res = pl.pallas_call(
        body, out_shape=[sh] * (4 * n), input_output_aliases={1 + 5 * n + k: k for k in range(len(prev))},
        grid_spec=pltpu.PrefetchScalarGridSpec(
            num_scalar_prefetch=1, grid=(rows // bm,),
            in_specs=[spec, half_spec(True), half_spec(False), spec, spec] * n + [pl.BlockSpec(memory_space=pl.ANY)] * len(prev),
            out_specs=[spec] * (4 * n)),
        compiler_params=_params(("parallel",)), name=name)(place, *[a for t in tensors for a in t], *prev)
    return [res[4 * t:4 * t + 4] for t in range(n)]


def _pack_small(norms, rpb, last=None):
    flat = jnp.concatenate([a.reshape(-1) for a in norms] + [rpb.reshape(-1)])
    flat = jnp.pad(flat, (0, SMALL_ROWS * 128 - flat.shape[0]))
    if last is not None:
        flat = lax.dynamic_update_slice(flat, last.reshape(1), (flat.shape[0] - 1,))
    return flat.reshape(SMALL_ROWS, 128)


def _unpack_small(p):
    flat = p.reshape(-1)
    norms = [flat[i * 2 * DM:(i + 1) * 2 * DM].reshape(2, DM) for i in range(4)]
    rpb = flat[8 * DM:8 * DM + NH * 15 * 31].reshape(1, NH, 15, 31)
    return norms, rpb


FFN_NAMES = ("ffn_w_gate", "ffn_w_up", "ffn_w_down")
L0_FFN = tuple((n, 0) for n in FFN_NAMES)
L1_FFN = tuple((n, 1) for n in FFN_NAMES)
NA_KEYS = (("na_w_qkv", 0), ("na_w_o", 0))
DIL_KEYS = (("dil_w_qkv", 0), ("dil_w_o", 0))
ALL_PEERS, NEIGHBOURS, DIAGONAL = (0, 1, 2), (0, 1), (2,)


class _Exchange:
    GATHERS = {"na_bias_tiles": [(k, ALL_PEERS) for k in NA_KEYS],
               "l0_qkv": [(L0_FFN[0], ALL_PEERS)],
               "na_fwd": [(k, ALL_PEERS) for k in L0_FFN[1:]],
               "l0_ffn_fwd": [(DIL_KEYS[0], NEIGHBOURS)],
               "l1_h_groups": [(DIL_KEYS[0], DIAGONAL)],
               "dil_fwd": [(k, ALL_PEERS) for k in L1_FFN + DIL_KEYS[1:]]}
    PAIRS = {"l1_proj_bwd": L1_FFN, "l1_dh": DIL_KEYS, "l0_proj_bwd": L0_FFN}
    EXCHANGES = {"dil_bwd": [(k, ALL_PEERS) for k in L1_FFN],
                 "l0_ffn_bwd": [(DIL_KEYS[0], NEIGHBOURS), (DIL_KEYS[1], ALL_PEERS)],
                 "na_bwd": [(k, ALL_PEERS) for k in L0_FFN] + [(DIL_KEYS[0], DIAGONAL)],
                 "l0_dh": [(k, NEIGHBOURS) for k in NA_KEYS],
                 "allreduce_small": [(k, DIAGONAL) for k in NA_KEYS]}
    SHARES = {"l1_dwqkv": L1_FFN, "l0_dwqkv": L0_FFN + DIL_KEYS}

    def __init__(self, shards):
        self.chip = 2 * lax.axis_index("x") + lax.axis_index("y")
        self.place = jnp.stack([lax.axis_index("c"), self.chip]).astype(jnp.int32)
        self.own = {k: s.reshape(2, s.shape[0] // 2, s.shape[1]).astype(BF16) for k, s in shards.items()}
        self.gathered, self.mine, self.parts, self.slots, self.full, self.other = {}, {}, {}, {}, {}, {}

    def _take(self, items, landed):
        x, y = lax.axis_index("x"), lax.axis_index("y")
        chips = (2 * (1 - x) + y, 2 * x + (1 - y), 2 * (1 - x) + (1 - y))
        for (k, peers), gw in zip(items, landed):
            if k not in self.gathered:
                self.gathered[k] = lax.dynamic_update_slice(gw, self.own[k][None], (self.chip, 0, 0, 0))
            else:
                for j in peers:
                    entry = lax.dynamic_slice(gw, (chips[j], 0, 0, 0), (1,) + gw.shape[1:])
                    self.gathered[k] = lax.dynamic_update_slice(self.gathered[k], entry, (chips[j], 0, 0, 0))

    def _sum(self, items, landed):
        runs = []
        for (k, peers), s in zip(items, landed):
            got = self.slots.setdefault(k, {})
            got[peers] = s
            if sum(len(p) for p in got) == len(ALL_PEERS):
                like = (self.parts[k].shape, tuple(sorted(got)))
                if runs and runs[-1][0] == like:
                    runs[-1][1].append(k)
                else:
                    runs.append((like, [k]))
        for (_, split), ks in runs:
            sums = _chip_sum(f"chip_sum_{ks[0][0]}_{ks[0][1]}", self.place, [self.parts[k] for k in ks],
                             [[self.slots[k][p] for p in split] for k in ks])
            self.full.update(zip(ks, sums))

    def weight(self, key):
        g = self.gathered[key]
        return g.reshape(NCHIP, 2 * g.shape[2], g.shape[3])

    def _pair_sums(self, keys, theirs):
        runs = []
        for k, t in zip(keys, theirs):
            if runs and runs[-1][0][1].shape == t.shape:
                runs[-1].append((k, t))
            else:
                runs.append([(k, t)])
        for run in runs:
            ks = [k for k, _ in run]
            sums = _pair_sum(f"pair_sum_{ks[0][0]}_{ks[0][1]}", self.place, [self.mine[k] for k in ks], [t for _, t in run])
            self.parts.update(zip(ks, sums))

    def carry(self, tag):
        if tag in self.GATHERS:
            return _gather_copies([(self.own[k], peers) for k, peers in self.GATHERS[tag]])
        if tag in self.PAIRS:
            return _pair_exchange_copies([self.mine[k] for k in self.PAIRS[tag]])
        if tag in self.EXCHANGES:
            return _chip_exchange_copies([(self.parts[k], peers) for k, peers in self.EXCHANGES[tag]])
        if tag in self.SHARES:
            return _pair_share_copies([self.full[k] for k in self.SHARES[tag]])
        return None

    def carried(self, tag, landed):
        if tag in self.GATHERS:
            self._take(self.GATHERS[tag], landed)
        elif tag in self.PAIRS:
            self._pair_sums(self.PAIRS[tag], landed)
        elif tag in self.EXCHANGES:
            self._sum(self.EXCHANGES[tag], landed)
        elif tag in self.SHARES:
            self.other.update(zip(self.SHARES[tag], landed))

    def grads(self, tag, dw):
        for k, g in dw.items():
            self.mine[k] = g.reshape(NCHIP, 2, -1, g.shape[-1])
        if tag == "l0_mix":
            keys = tuple(dw)
            self._pair_sums(keys, _run_carried("grad_pair_exchange_last", _pair_exchange_copies([self.mine[k] for k in keys])))

    def finish(self):
        rest = tuple(k for k in self.full if k not in self.other)
        self.other.update(zip(rest, _run_carried("grad_pair_share_last", _pair_share_copies([self.full[k] for k in rest]))))
        return {k: (self.full[k], self.other[k]) for k in self.full}


def kernel(x, norm_mix_pre, norm_mix_post, norm_ffn_pre, norm_ffn_post, na_w_qkv, na_w_o, na_rpb, dil_w_qkv, dil_w_o, ffn_w_gate, ffn_w_up, ffn_w_down, loss_target, m_norm_mix_pre, m_norm_mix_post, m_norm_ffn_pre, m_norm_ffn_post, m_na_w_qkv, m_na_w_o, m_na_rpb, m_dil_w_qkv, m_dil_w_o, m_ffn_w_gate, m_ffn_w_up, m_ffn_w_down, v_norm_mix_pre, v_norm_mix_post, v_norm_ffn_pre, v_norm_ffn_post, v_na_w_qkv, v_na_w_o, v_na_rpb, v_dil_w_qkv, v_dil_w_o, v_ffn_w_gate, v_ffn_w_up, v_ffn_w_down):
    tr = lambda a: jnp.swapaxes(a, 1, 2)
    weights = {"na_w_qkv": na_w_qkv, "na_w_o": na_w_o, "dil_w_qkv": dil_w_qkv, "dil_w_o": dil_w_o,
               "ffn_w_gate": tr(ffn_w_gate), "ffn_w_up": tr(ffn_w_up), "ffn_w_down": ffn_w_down}
    m_in = {"na_w_qkv": m_na_w_qkv, "na_w_o": m_na_w_o, "dil_w_qkv": m_dil_w_qkv, "dil_w_o": m_dil_w_o,
            "ffn_w_gate": tr(m_ffn_w_gate), "ffn_w_up": tr(m_ffn_w_up), "ffn_w_down": m_ffn_w_down}
    v_in = {"na_w_qkv": v_na_w_qkv, "na_w_o": v_na_w_o, "dil_w_qkv": v_dil_w_qkv, "dil_w_o": v_dil_w_o,
            "ffn_w_gate": tr(v_ffn_w_gate), "ffn_w_up": tr(v_ffn_w_up), "ffn_w_down": v_ffn_w_down}

    ex = _Exchange({(n, l): weights[n][l] for n in weights for l in range(weights[n].shape[0])})
    norms = (norm_mix_pre, norm_mix_post, norm_ffn_pre, norm_ffn_post)
    loss_row, dx, dnorms, d_rpb = _local_step(x[0], loss_target[0], norms, na_rpb[0], ex)
    small, sent = _allreduce_small(_pack_small(dnorms, d_rpb, last=loss_row[0, 0]), ex.carry("allreduce_small"))
    ex.carried("allreduce_small", sent)
    full = ex.finish()
    loss = small[SMALL_ROWS - 1, 127]

    out_g, out_d, out_m, out_v = {}, {}, {}, {}
    operands = lambda n, l: (weights[n], *full[(n, l)], m_in[n], v_in[n])
    results = {n: _adamw(f"adamw_{n}", ex.place, [operands(n, 0)])[0] for n in weights if n not in FFN_NAMES}
    ffn = None
    for l in range(2):
        ffn = _adamw(f"adamw_ffn_{l}", ex.place, [operands(n, l) for n in FFN_NAMES], l, ffn)
    results.update(zip(FFN_NAMES, ffn))
    for n, res in results.items():
        if n in ("ffn_w_gate", "ffn_w_up"):
            res = [tr(r) for r in res]
        out_g[n], out_d[n], out_m[n], out_v[n] = res
    sm_names = ("norm_mix_pre", "norm_mix_post", "norm_ffn_pre", "norm_ffn_post", "na_rpb")
    sm = _adamw("adamw_small", jnp.zeros((2,), jnp.int32),
                [(_pack_small(norms, na_rpb)[None], small[:SMALL_ROWS // 2], small[SMALL_ROWS // 2:],
                  _pack_small((m_norm_mix_pre, m_norm_mix_post, m_norm_ffn_pre, m_norm_ffn_post), m_na_rpb)[None],
                  _pack_small((v_norm_mix_pre, v_norm_mix_post, v_norm_ffn_pre, v_norm_ffn_post), v_na_rpb)[None])])[0]
    for res, dst in zip(sm, (out_g, out_d, out_m, out_v)):
        ns, rp = _unpack_small(res)
        for n, a in zip(sm_names, ns + [rp]):
            dst[n] = a

    order = ("norm_mix_pre", "norm_mix_post", "norm_ffn_pre", "norm_ffn_post", "na_w_qkv", "na_w_o", "na_rpb", "dil_w_qkv", "dil_w_o",
             "ffn_w_gate", "ffn_w_up", "ffn_w_down")
    return (loss, dx[None], *[out_g[n] for n in order], *[out_d[n] for n in order], *[out_m[n] for n in order], *[out_v[n] for n in order])
```

```python
import functools

import numpy as np
import jax
import jax.numpy as jnp
from jax import lax
from jax.experimental import pallas as pl
from jax.experimental.pallas import tpu as pltpu

F32 = jnp.float32
BF16 = jnp.bfloat16

SEQ = 2048
DM = 1024
NH = 16
HD = 64
DFF = 2816
NCHIP = 4
FSH = DFF // NCHIP
GRID_W = 64
NA_QROWS = 4
NA_QB = NA_QROWS * GRID_W
NA_WROWS = 12
NA_WIN = NA_WROWS * GRID_W
DIL = (1, 4, 16)
DIL_QB = 256
DIL_WIN = DIL_QB + 128
DIL_RADIUS = 64
RMS_EPS = 1e-6
NEG = -1e30
QSCALE = HD ** -0.5
CH = 256
MESH = pl.DeviceIdType.MESH

ADAM_LR, ADAM_B1, ADAM_B2, ADAM_EPS, ADAM_WD, ADAM_STEP = 0.001, 0.9, 0.999, 1e-08, 0.01, 10

VMEM_LIMIT = 56 * 1024 * 1024

_NN = (((1,), (0,)), ((), ()))
_NT = (((1,), (1,)), ((), ()))
_TN = (((0,), (0,)), ((), ()))


def _params(sem):
    return pltpu.CompilerParams(dimension_semantics=sem, vmem_limit_bytes=VMEM_LIMIT)


def _matmul(name, pairs, grid, out_shape, out_spec, acc_shape, carrying=False, carry=None):
    nk = grid[-1]
    npair = len(pairs)
    n_in = 2 * npair

    def body(*refs):
        ins, o_ref = refs[:2 * npair], refs[n_in]
        part = None
        for p in range(npair):
            d = lax.dot_general(ins[2 * p][...].astype(BF16), ins[2 * p + 1][...].astype(BF16), pairs[p][4],
                                preferred_element_type=F32)
            part = d if part is None else part + d
        if nk == 1:
            o_ref[...] = part.astype(o_ref.dtype)
        else:
            acc_ref = refs[n_in + 1]
            kk = pl.program_id(len(grid) - 1)

            @pl.when(kk == 0)
            def _():
                acc_ref[...] = part

            @pl.when(kk > 0)
            def _():
                acc_ref[...] += part

            @pl.when(kk == nk - 1)
            def _():
                o_ref[...] = acc_ref[...].astype(o_ref.dtype)

    ops, specs = [], []
    for a, a_spec, b, b_spec, _ in pairs:
        ops += [a, b]
        specs += [a_spec, b_spec]
    (out,), sent = _carrier_call(name, body, grid, specs, [out_spec], [out_shape], [] if nk == 1 else [pltpu.VMEM(acc_shape, F32)], ops, carry)
    return (out, sent) if carrying else out


def _qkv_fwd(name, h_all, w4, carry):
    g_n = h_all.shape[0]
    per = w4.shape[2] // CH
    return _matmul(
        name, [(h_all, pl.BlockSpec((None, SEQ, DM), lambda g, q, k: (g, 0, 0)),
                w4, pl.BlockSpec((None, DM, CH), lambda g, q, k: ((g * 12 + q) // per, 0, (g * 12 + q) % per)), _NN)],
        (g_n, 12, 1), jax.ShapeDtypeStruct((g_n, SEQ, 3 * DM), BF16),
        pl.BlockSpec((None, SEQ, CH), lambda g, q, k: (g, 0, q)), None, carrying=True, carry=carry)


def _qkv_bwd_dh(name, dqkv, w4, carry):
    g_n = dqkv.shape[0]
    per = w4.shape[2] // CH
    tm = SEQ

    def pair(cb):
        chunk = lambda g, t: g * 12 + t * 4 + cb
        return (dqkv, pl.BlockSpec((None, None, tm, CH), lambda g, i, t: (g, t, i, cb)),
                w4, pl.BlockSpec((None, DM, CH), lambda g, i, t: (chunk(g, t) // per, 0, chunk(g, t) % per)), _NT)

    return _matmul(name, [pair(cb) for cb in range(4)], (g_n, SEQ // tm, 3), jax.ShapeDtypeStruct((g_n, SEQ, DM), F32),
                   pl.BlockSpec((None, tm, DM), lambda g, i, t: (g, i, 0)), (tm, DM), carrying=True, carry=carry)


def _qkv_bwd_dw(name, ht_all, dqkv, shard_cols, carry):
    g_n = dqkv.shape[0]
    per = shard_cols // CH
    return _matmul(
        name, [(ht_all, pl.BlockSpec((None, DM, SEQ), lambda qq, k: (qq // 12, 0, 0)),
                dqkv, pl.BlockSpec((None, None, SEQ, CH), lambda qq, k: (qq // 12, (qq % 12) // 4, 0, qq % 4)), _NN)],
        (g_n * 12, 1), jax.ShapeDtypeStruct((NCHIP, DM, shard_cols), BF16),
        pl.BlockSpec((None, DM, CH), lambda qq, k: (qq // per, 0, qq % per)), None, carrying=True, carry=carry)


def _proj_fwd(name, o, wo, x, g):
    tm = 512

    def body(o_ref, w_ref, x_ref, g_ref, xn_ref, u_ref):
        u = jnp.dot(o_ref[...], w_ref[...], preferred_element_type=F32)
        u_ref[...] = u
        r = lax.rsqrt(jnp.mean(u * u, axis=-1, keepdims=True) + RMS_EPS)
        xn_ref[...] = x_ref[...] + u * r * g_ref[...]

    rows = pl.BlockSpec((tm, DM), lambda i: (i, 0))
    sh = jax.ShapeDtypeStruct((SEQ, DM), F32)
    return pl.pallas_call(
        body, grid=(SEQ // tm,), in_specs=[rows, pl.BlockSpec((DM, DM), lambda i: (0, 0)), rows, pl.BlockSpec((1, DM), lambda i: (0, 0))],
        out_specs=[rows, rows], out_shape=[sh, sh], compiler_params=_params(("parallel",)), name=name)(o, wo, x, g)


def _proj_bwd(name, dy, u, g, wo, dtype, carry):
    tm = 512

    def body(dy_ref, u_ref, g_ref, w_ref, do_ref, du_ref, dg_ref):
        dy = dy_ref[...]
        u = u_ref[...]
        r = lax.rsqrt(jnp.mean(u * u, axis=-1, keepdims=True) + RMS_EPS)
        yh = u * r
        t = dy * g_ref[...]
        du = (r * (t - yh * jnp.mean(t * yh, axis=-1, keepdims=True))).astype(BF16)
        du_ref[...] = du
        do_ref[...] = lax.dot_general(du, w_ref[...], _NT, preferred_element_type=F32).astype(do_ref.dtype)

        @pl.when(pl.program_id(0) == 0)
        def _():
            dg_ref[...] = jnp.zeros_like(dg_ref)

        dg_ref[...] += jnp.sum(dy * yh, axis=0, keepdims=True)

    rows = pl.BlockSpec((tm, DM), lambda i: (i, 0))
    vec = pl.BlockSpec((1, DM), lambda i: (0, 0))
    return _carrier_call(
        name, body, (SEQ // tm,), [rows, rows, vec, pl.BlockSpec((DM, DM), lambda i: (0, 0))], [rows, rows, vec],
        [jax.ShapeDtypeStruct((SEQ, DM), dtype), jax.ShapeDtypeStruct((SEQ, DM), BF16), jax.ShapeDtypeStruct((1, DM), F32)],
        [], (dy, u, g, wo), carry)


def _proj_bwd_dw(name, o, du):
    tn = 512
    return _matmul(
        name, [(o, pl.BlockSpec((SEQ, DM), lambda j, k: (0, 0)), du, pl.BlockSpec((SEQ, tn), lambda j, k: (0, j)), _TN)],
        (DM // tn, 1), jax.ShapeDtypeStruct((DM, DM), BF16), pl.BlockSpec((DM, tn), lambda j, k: (0, j)), None)


def _ffn_wspec(index_map):
    return pl.BlockSpec((None, FSH, DM), index_map)


def _ffn_bwd_dw(name, a4, b):
    return _matmul(
        name, [(a4, pl.BlockSpec((None, SEQ, FSH), lambda s, k: (s, 0, 0)), b, pl.BlockSpec((SEQ, DM), lambda s, k: (0, 0)), _TN)],
        (NCHIP, 1), jax.ShapeDtypeStruct((NCHIP, FSH, DM), BF16), _ffn_wspec(lambda s, k: (s, 0, 0)), None)


ROWS = 256


def _row_spec():
    return pl.BlockSpec((ROWS, DM), lambda i: (i, 0))


def _vec_spec():
    return pl.BlockSpec((1, DM), lambda i: (0, 0))


def _rms_fwd(name, x, g, dtype=BF16):
    def body(x_ref, g_ref, o_ref):
        x = x_ref[...]
        r = lax.rsqrt(jnp.mean(x * x, axis=-1, keepdims=True) + RMS_EPS)
        o_ref[...] = (x * r * g_ref[...]).astype(o_ref.dtype)

    return pl.pallas_call(body, grid=(SEQ // ROWS,), in_specs=[_row_spec(), _vec_spec()], out_specs=_row_spec(),
                          out_shape=jax.ShapeDtypeStruct((SEQ, DM), dtype), compiler_params=_params(("parallel",)), name=name)(x, g)


def _rms_fwd_both(name, x, g, carry):
    def body(x_ref, g_ref, o_ref, t_ref):
        x = x_ref[...]
        r = lax.rsqrt(jnp.mean(x * x, axis=-1, keepdims=True) + RMS_EPS)
        h = x * r * g_ref[...]
        o_ref[...] = h.astype(o_ref.dtype)
        t_ref[...] = h.T.astype(t_ref.dtype)

    return _carrier_call(
        name, body, (SEQ // ROWS,), [_row_spec(), _vec_spec()], [_row_spec(), pl.BlockSpec((DM, ROWS), lambda i: (0, i))],
        [jax.ShapeDtypeStruct((SEQ, DM), BF16), jax.ShapeDtypeStruct((DM, SEQ), BF16)], [], (x, g), carry)


def _norm_bwd(name, dy, u, g, res, carry):
    def body(dy_ref, u_ref, g_ref, res_ref, du_ref, dg_ref):
        dy = dy_ref[...]
        u = u_ref[...]
        r = lax.rsqrt(jnp.mean(u * u, axis=-1, keepdims=True) + RMS_EPS)
        yh = u * r
        t = dy * g_ref[...]
        du_ref[...] = r * (t - yh * jnp.mean(t * yh, axis=-1, keepdims=True)) + res_ref[...]

        @pl.when(pl.program_id(0) == 0)
        def _():
            dg_ref[...] = jnp.zeros_like(dg_ref)

        dg_ref[...] += jnp.sum(dy * yh, axis=0, keepdims=True)

    return _carrier_call(
        name, body, (SEQ // ROWS,), [_row_spec(), _row_spec(), _vec_spec(), _row_spec()], [_row_spec(), _vec_spec()],
        [jax.ShapeDtypeStruct((SEQ, DM), F32), jax.ShapeDtypeStruct((1, DM), F32)], [], (dy, u, g, res), carry)


def _loss_grad(name, y, t):
    def body(y_ref, t_ref, dy_ref, l_ref):
        e = y_ref[...] - t_ref[...]
        dy_ref[...] = e * (1.0 / DM)

        @pl.when(pl.program_id(0) == 0)
        def _():
            l_ref[...] = jnp.zeros_like(l_ref)

        l_ref[...] += jnp.sum(e * e) * (0.5 / DM)

    return pl.pallas_call(
        body, grid=(SEQ // ROWS,), in_specs=[_row_spec(), _row_spec()],
        out_specs=[_row_spec(), pl.BlockSpec((1, 128), lambda i: (0, 0))],
        out_shape=[jax.ShapeDtypeStruct((SEQ, DM), F32), jax.ShapeDtypeStruct((1, 128), F32)],
        compiler_params=_params(("arbitrary",)), name=name)(y, t)


HBM_SPEC = pl.BlockSpec(memory_space=pltpu.HBM)


class _Carried:
    def __init__(self, ins, out_shapes, n_sems, issue, drain):
        self.ins, self.out_shapes, self.n_sems, self.issue, self.drain = list(ins), list(out_shapes), tuple(n_sems), issue, drain


def _carrier_call(name, body, grid, in_specs, out_specs, out_shape, scratch_shapes, operands, carry):
    n_in, n_out, n_scr = len(in_specs), len(out_specs), len(scratch_shapes)
    if carry is None:
        res = pl.pallas_call(body, grid=grid, in_specs=in_specs, out_specs=out_specs, out_shape=out_shape, scratch_shapes=scratch_shapes,
                             compiler_params=_params(("arbitrary",) * len(grid)), name=name)(*operands)
        return list(res), []
    ci, co = len(carry.ins), len(carry.out_shapes)

    def wrapped(*refs):
        ins, cins = refs[:n_in], refs[n_in:n_in + ci]
        outs, couts = refs[n_in + ci:n_in + ci + n_out], refs[n_in + ci + n_out:n_in + ci + n_out + co]
        scr, sems = refs[n_in + ci + n_out + co:n_in + ci + n_out + co + n_scr], refs[n_in + ci + n_out + co + n_scr:]
        first = functools.reduce(jnp.logical_and, [pl.program_id(a) == 0 for a in range(len(grid))])
        last = functools.reduce(jnp.logical_and, [pl.program_id(a) == grid[a] - 1 for a in range(len(grid))])

        @pl.when(first)
        def _():
            carry.issue(cins, couts, sems)

        body(*ins, *outs, *scr)

        @pl.when(last)
        def _():
            carry.drain(cins, couts, sems)

    res = pl.pallas_call(
        wrapped, grid=grid, in_specs=list(in_specs) + [HBM_SPEC] * ci, out_specs=list(out_specs) + [HBM_SPEC] * co,
        out_shape=list(out_shape) + carry.out_shapes,
        scratch_shapes=list(scratch_shapes) + [pltpu.SemaphoreType.DMA((k,)) for k in carry.n_sems],
        compiler_params=pltpu.CompilerParams(dimension_semantics=("arbitrary",) * len(grid), vmem_limit_bytes=VMEM_LIMIT, has_side_effects=True),
        name=name)(*operands, *carry.ins)
    return list(res[:n_out]), list(res[n_out:])


def _run_carried(name, carry):
    def body(*refs):
        ci, co = len(carry.ins), len(carry.out_shapes)
        carry.issue(refs[:ci], refs[ci:ci + co], refs[ci + co:])
        carry.drain(refs[:ci], refs[ci:ci + co], refs[ci + co:])

    return pl.pallas_call(
        body, in_specs=[HBM_SPEC] * len(carry.ins), out_specs=[HBM_SPEC] * len(carry.out_shapes), out_shape=carry.out_shapes,
        scratch_shapes=[pltpu.SemaphoreType.DMA((k,)) for k in carry.n_sems],
        compiler_params=pltpu.CompilerParams(has_side_effects=True), name=name)(*carry.ins)


NA_BLOCKS = SEQ // NA_QB
NA_ROWS_TOTAL = SEQ // GRID_W
NA_CLASSES = ((0, 0), (8, 4), (NA_ROWS_TOTAL - NA_QROWS, NA_ROWS_TOTAL - NA_WROWS))


def _na_pairs(i0, ws):
    out = []
    for qi in range(NA_QROWS):
        i = i0 + qi
        rs = min(max(i - 4, 0), NA_ROWS_TOTAL - 8)
        for kr in range(NA_WROWS):
            r = ws + kr
            if rs <= r < rs + 8:
                out.append((qi, kr, r - i + 7))
    return out


def _diag_onehot():
    qc, kc = np.meshgrid(np.arange(GRID_W), np.arange(GRID_W), indexing="ij")
    e = np.zeros((GRID_W * GRID_W, 128), np.float32)
    j = (kc - qc + 15).reshape(-1)
    ok = (j >= 0) & (j <= 30)
    e[np.arange(GRID_W * GRID_W)[ok], j[ok]] = 1.0
    return jnp.asarray(e)


def _rpb_expand(rpb):
    r2 = jnp.pad(rpb.reshape(NH * 15, 31), ((0, 0), (0, 128 - 31)))

    def body(r_ref, e_ref, o_ref):
        o_ref[...] = lax.dot_general(r_ref[...], e_ref[...], _NT, preferred_element_type=F32, precision=lax.Precision.HIGHEST)

    out = pl.pallas_call(body, out_shape=jax.ShapeDtypeStruct((NH * 15, GRID_W * GRID_W), F32), name="rpb_expand",
                         compiler_params=pltpu.CompilerParams(vmem_limit_bytes=VMEM_LIMIT))(r2, _diag_onehot())
    return out.reshape(NH, 15, GRID_W, GRID_W)


def _na_bias_tiles(rpb, carry):
    def body(b_ref, o_ref):
        qc = lax.broadcasted_iota(jnp.int32, (GRID_W, GRID_W), 0)
        kc = lax.broadcasted_iota(jnp.int32, (GRID_W, GRID_W), 1)
        first = jnp.clip(qc - 8, 0, GRID_W - 16)
        in_window = (kc >= first) & (kc < first + 16)
        neg = jnp.full((GRID_W, GRID_W), NEG, F32)
        for cls, (i0, ws) in enumerate(NA_CLASSES):
            @pl.when(pl.program_id(0) == cls)
            def _(i0=i0, ws=ws):
                pairs = {(qi, kr): dr for qi, kr, dr in _na_pairs(i0, ws)}
                masked = {dr: jnp.where(in_window, b_ref[dr], NEG) for dr in sorted(set(pairs.values()))}
                for qi in range(NA_QROWS):
                    for k2 in range(NA_WROWS // 2):
                        blocks = [masked[pairs[(qi, kr)]] if (qi, kr) in pairs else neg for kr in (2 * k2, 2 * k2 + 1)]
                        o_ref[qi * GRID_W:(qi + 1) * GRID_W, k2 * 128:(k2 + 1) * 128] = jnp.concatenate(blocks, axis=1)

    (tiles,), sent = _carrier_call(
        "na_bias_tiles", body, (3, NH), [pl.BlockSpec((None, 15, GRID_W, GRID_W), lambda c, h: (h, 0, 0, 0))],
        [pl.BlockSpec((None, None, NA_QB, NA_WIN), lambda c, h: (c, h, 0, 0))], [jax.ShapeDtypeStruct((3, NH, NA_QB, NA_WIN), F32)],
        [], (_rpb_expand(rpb),), carry)
    return tiles, sent


def _na_cls(b):
    return jnp.where(b == 0, 0, jnp.where(b == NA_BLOCKS - 1, 2, 1))


def _na_start(b):
    return pl.multiple_of(jnp.clip(b * NA_QROWS - 4, 0, NA_ROWS_TOTAL - NA_WROWS) * GRID_W, GRID_W)


NA_FWD_HPS = 8
NA_BWD_HPS = 4


def _na_in_specs(hps):
    lw = hps * HD
    nlw = DM // lw
    return [pl.BlockSpec((NA_QB, lw), lambda hp, b: (b, hp)),
            pl.BlockSpec((SEQ, lw), lambda hp, b: (0, nlw + hp)),
            pl.BlockSpec((SEQ, lw), lambda hp, b: (0, 2 * nlw + hp)),
            pl.BlockSpec((None, hps, NA_QB, NA_WIN), lambda hp, b: (_na_cls(b), hp, 0, 0))]


def _na_fwd(qkv, bias, carry):
    lw = NA_FWD_HPS * HD

    def body(q_ref, k_ref, v_ref, b_ref, o_ref):
        start = _na_start(pl.program_id(1))
        q = q_ref[...]
        kw = k_ref[pl.ds(start, NA_WIN), :]
        vw = v_ref[pl.ds(start, NA_WIN), :]
        outs = []
        for hh in range(NA_FWD_HPS):
            sl = slice(hh * HD, (hh + 1) * HD)
            s = lax.dot_general(q[:, sl] * QSCALE, kw[:, sl], _NT, preferred_element_type=F32) + b_ref[hh]
            p = jnp.exp(s - jnp.max(s, axis=-1, keepdims=True))
            l = jnp.sum(p, axis=-1, keepdims=True)
            outs.append(jnp.dot(p.astype(BF16), vw[:, sl], preferred_element_type=F32) / l)
        o_ref[...] = jnp.concatenate(outs, axis=1).astype(o_ref.dtype)

    (o,), sent = _carrier_call(
        "na_fwd", body, (NH // NA_FWD_HPS, NA_BLOCKS), _na_in_specs(NA_FWD_HPS), [pl.BlockSpec((NA_QB, lw), lambda hp, b: (b, hp))],
        [jax.ShapeDtypeStruct((SEQ, DM), BF16)], [], (qkv, qkv, qkv, bias), carry)
    return o, sent


def _na_bwd(qkv, bias, do, carry):
    lw = NA_BWD_HPS * HD

    def body(q_ref, k_ref, v_ref, b_ref, do_ref, dqkv_ref, z_ref, dk_acc, dv_acc):
        blk = pl.program_id(1)

        @pl.when(blk == 0)
        def _():
            dk_acc[...] = jnp.zeros_like(dk_acc)
            dv_acc[...] = jnp.zeros_like(dv_acc)
            z_ref[...] = jnp.zeros_like(z_ref)

        start = _na_start(blk)
        q = q_ref[...]
        do = do_ref[...]
        kw = k_ref[pl.ds(start, NA_WIN), :]
        vw = v_ref[pl.ds(start, NA_WIN), :]
        dqs, dks, dvs, dss = [], [], [], []
        for hh in range(NA_BWD_HPS):
            sl = slice(hh * HD, (hh + 1) * HD)
            qh = q[:, sl] * QSCALE
            s = lax.dot_general(qh, kw[:, sl], _NT, preferred_element_type=F32) + b_ref[hh]
            p = jnp.exp(s - jnp.max(s, axis=-1, keepdims=True))
            p = p / jnp.sum(p, axis=-1, keepdims=True)
            dp = lax.dot_general(do[:, sl], vw[:, sl], _NT, preferred_element_type=F32)
            ds = p * (dp - jnp.sum(p * dp, axis=-1, keepdims=True))
            dsb = ds.astype(BF16)
            dqs.append(jnp.dot(dsb, kw[:, sl], preferred_element_type=F32) * QSCALE)
            dks.append(lax.dot_general(qh, dsb, _TN, preferred_element_type=F32).T)
            dvs.append(lax.dot_general(do[:, sl], p.astype(BF16), _TN, preferred_element_type=F32).T)
            dss.append(ds)
        for cls, (i0, ws) in enumerate(NA_CLASSES):
            @pl.when(_na_cls(blk) == cls)
            def _(i0=i0, ws=ws):
                for hh, ds in enumerate(dss):
                    for qi, kr, dr in _na_pairs(i0, ws):
                        z_ref[hh, dr * GRID_W:(dr + 1) * GRID_W, :] += ds[qi * GRID_W:(qi + 1) * GRID_W, kr * GRID_W:(kr + 1) * GRID_W]
        dqkv_ref[0, pl.ds(pl.multiple_of(blk * NA_QB, NA_QB), NA_QB), :] = jnp.concatenate(dqs, axis=1).astype(dqkv_ref.dtype)
        dk_acc[pl.ds(start, NA_WIN), :] += jnp.concatenate(dks, axis=1)
        dv_acc[pl.ds(start, NA_WIN), :] += jnp.concatenate(dvs, axis=1)

        @pl.when(blk == NA_BLOCKS - 1)
        def _():
            dqkv_ref[1] = dk_acc[...].astype(dqkv_ref.dtype)
            dqkv_ref[2] = dv_acc[...].astype(dqkv_ref.dtype)

    (dqkv, z), sent = _carrier_call(
        "na_bwd", body, (NH // NA_BWD_HPS, NA_BLOCKS),
        _na_in_specs(NA_BWD_HPS) + [pl.BlockSpec((NA_QB, lw), lambda hp, b: (b, hp))],
        [pl.BlockSpec((3, SEQ, lw), lambda hp, b: (0, 0, hp)), pl.BlockSpec((NA_BWD_HPS, 15 * GRID_W, GRID_W), lambda hp, b: (hp, 0, 0))],
        [jax.ShapeDtypeStruct((3, SEQ, DM), BF16), jax.ShapeDtypeStruct((NH, 15 * GRID_W, GRID_W), F32)],
        [pltpu.VMEM((SEQ, lw), F32), pltpu.VMEM((SEQ, lw), F32)], (qkv, qkv, qkv, bias, do), carry)
    return dqkv, z, sent


def _rpb_grad(z):
    z2 = z.reshape(NH * 15, GRID_W * GRID_W)

    def body(z_ref, e_ref, o_ref):
        o_ref[...] = jnp.dot(z_ref[...], e_ref[...], preferred_element_type=F32, precision=lax.Precision.HIGHEST)

    out = pl.pallas_call(body, out_shape=jax.ShapeDtypeStruct((NH * 15, 128), F32), name="rpb_grad",
                         compiler_params=pltpu.CompilerParams(vmem_limit_bytes=VMEM_LIMIT))(z2, _diag_onehot())
    return out[:, :31].reshape(NH, 15, 31)


DIL_BLOCKS = SEQ // DIL_QB
DIL_HPS = 8
DIL_LW = DIL_HPS * HD
DIL_NLW = DM // DIL_LW


COLS = 128


def _col_spec():
    return pl.BlockSpec((SEQ, COLS), lambda j: (0, j))


def _grp_spec():
    return pl.BlockSpec((3, SEQ, COLS), lambda j: (0, 0, j))


def _store_group_order(dst_ref, src_ref):
    for g, d in enumerate(DIL):
        n = SEQ // d
        for r in range(d):
            dst_ref[g, r * n:(r + 1) * n, :] = src_ref[pl.ds(r, n, stride=d), :].astype(dst_ref.dtype)


def _store_token_order(dst_ref, src_ref, g):
    d = DIL[g]
    n = SEQ // d
    for r in range(d):
        dst_ref[pl.ds(r, n, stride=d), :] = src_ref[g, r * n:(r + 1) * n, :].astype(dst_ref.dtype)


def _to_groups(name, a):
    def body(a_ref, o_ref, t_ref):
        _store_group_order(o_ref, a_ref)
        for g in range(3):
            t_ref[g] = o_ref[g].astype(F32).T.astype(t_ref.dtype)

    return pl.pallas_call(
        body, grid=(DM // COLS,), in_specs=[_col_spec()], out_specs=[_grp_spec(), pl.BlockSpec((3, COLS, SEQ), lambda j: (0, j, 0))],
        out_shape=[jax.ShapeDtypeStruct((3, SEQ, DM), BF16), jax.ShapeDtypeStruct((3, DM, SEQ), BF16)],
        compiler_params=_params(("parallel",)), name=name)(a)


def _from_groups_sum(name, a):
    def body(a_ref, o_ref, t1, t2):
        _store_token_order(t1, a_ref, 1)
        _store_token_order(t2, a_ref, 2)
        o_ref[...] = (a_ref[0] + t1[...]) + t2[...]

    return pl.pallas_call(body, grid=(DM // COLS,), in_specs=[_grp_spec()], out_specs=_col_spec(),
                          out_shape=jax.ShapeDtypeStruct((SEQ, DM), F32), scratch_shapes=[pltpu.VMEM((SEQ, COLS), F32)] * 2,
                          compiler_params=_params(("parallel",)), name=name)(a)


def _dil_start(b):
    return pl.multiple_of(jnp.clip(b * DIL_QB - DIL_RADIUS, 0, SEQ - DIL_WIN), DIL_RADIUS)


def _dil_neg_dist(g, ii, jj):
    shift = 11 - 2 * g
    dist = jnp.abs(ii - jj)
    valid = (dist <= DIL_RADIUS) & (jnp.right_shift(ii, shift) == jnp.right_shift(jj, shift))
    return jnp.where(valid, -dist.astype(F32), NEG)


def _dil_in_specs():
    return [pl.BlockSpec(memory_space=pltpu.SMEM),
            pl.BlockSpec((None, DIL_QB, DIL_LW), lambda g, hp, b: (g, b, hp)),
            pl.BlockSpec((None, SEQ, DIL_LW), lambda g, hp, b: (g, 0, DIL_NLW + hp)),
            pl.BlockSpec((None, SEQ, DIL_LW), lambda g, hp, b: (g, 0, 2 * DIL_NLW + hp))]


def _dil_fwd(qkv, slopes, carry):
    def body(sl_ref, q_ref, k_ref, v_ref, o_ref, lse_ref):
        g, hp, b = pl.program_id(0), pl.program_id(1), pl.program_id(2)
        start = _dil_start(b)
        neg_dist = _dil_neg_dist(g, b * DIL_QB + lax.broadcasted_iota(jnp.int32, (DIL_QB, DIL_WIN), 0),
                                 start + lax.broadcasted_iota(jnp.int32, (DIL_QB, DIL_WIN), 1))
        dil = jnp.left_shift(1, 2 * g).astype(F32)
        q = q_ref[...]
        kw = k_ref[pl.ds(start, DIL_WIN), :]
        vw = v_ref[pl.ds(start, DIL_WIN), :]
        outs, lses = [], []
        for hh in range(DIL_HPS):
            sl = slice(hh * HD, (hh + 1) * HD)
            s = lax.dot_general(q[:, sl] * QSCALE, kw[:, sl], _NT, preferred_element_type=F32)
            s = s + (sl_ref[hp * DIL_HPS + hh] * dil) * neg_dist
            m = jnp.max(s, axis=-1, keepdims=True)
            p = jnp.exp(s - m)
            l = jnp.sum(p, axis=-1, keepdims=True)
            outs.append(jnp.dot(p.astype(BF16), vw[:, sl], preferred_element_type=F32) / l)
            lses.append(jnp.broadcast_to(m + jnp.log(l), (DIL_QB, HD)))
        o_ref[...] = jnp.concatenate(outs, axis=1).astype(o_ref.dtype)
        lse_ref[...] = jnp.concatenate(lses, axis=1)

    ospec = pl.BlockSpec((None, DIL_QB, DIL_LW), lambda g, hp, b: (g, b, hp))
    (o, lse), sent = _carrier_call(
        "dil_fwd", body, (3, DIL_NLW, DIL_BLOCKS), _dil_in_specs(), [ospec, ospec],
        [jax.ShapeDtypeStruct((3, SEQ, DM), BF16), jax.ShapeDtypeStruct((3, SEQ, DM), F32)], [], (slopes, qkv, qkv, qkv), carry)
    return o, lse, sent


def _dil_merge(o_all, lse_all):
    def body(o_ref, l_ref, out_ref, lse_ref, o1, o2, l1, l2):
        for g, (ot, lt) in ((1, (o1, l1)), (2, (o2, l2))):
            _store_token_order(ot, o_ref, g)
            _store_token_order(lt, l_ref, g)
        la, lb, lc = l_ref[0], l1[...], l2[...]
        m = jnp.maximum(jnp.maximum(la, lb), lc)
        wa, wb, wc = jnp.exp(la - m), jnp.exp(lb - m), jnp.exp(lc - m)
        sw = (wa + wb) + wc
        out_ref[...] = (((wa * o_ref[0].astype(F32) + wb * o1[...]) + wc * o2[...]) / sw).astype(out_ref.dtype)
        lse_ref[...] = m + jnp.log(sw)

    return pl.pallas_call(
        body, grid=(DM // COLS,), in_specs=[_grp_spec(), _grp_spec()], out_specs=[_col_spec(), _col_spec()],
        out_shape=[jax.ShapeDtypeStruct((SEQ, DM), BF16), jax.ShapeDtypeStruct((SEQ, DM), F32)],
        scratch_shapes=[pltpu.VMEM((SEQ, COLS), F32)] * 4, compiler_params=_params(("parallel",)), name="dil_merge")(o_all, lse_all)


def _dil_bwd_prep(do, o, lse):
    heads = COLS // HD

    def body(do_ref, o_ref, lse_ref, dog_ref, ddr_ref, lser_ref, dd, grp):
        prod = do_ref[...] * o_ref[...].astype(F32)
        dd[...] = jnp.concatenate(
            [jnp.broadcast_to(jnp.sum(prod[:, h * HD:(h + 1) * HD], axis=-1, keepdims=True), (SEQ, HD)) for h in range(heads)], axis=1)
        _store_group_order(dog_ref, do_ref)
        for src, dst in ((dd, ddr_ref), (lse_ref, lser_ref)):
            _store_group_order(grp, src)
            for g in range(3):
                t = grp[g].T
                for h in range(heads):
                    dst[g, h] = t[h * HD:h * HD + 8, :]

    rows = jax.ShapeDtypeStruct((3, NH, 8, SEQ), F32)
    rspec = pl.BlockSpec((3, heads, 8, SEQ), lambda j: (0, j, 0, 0))
    return pl.pallas_call(
        body, grid=(DM // COLS,), in_specs=[_col_spec()] * 3, out_specs=[_grp_spec(), rspec, rspec],
        out_shape=[jax.ShapeDtypeStruct((3, SEQ, DM), BF16), rows, rows],
        scratch_shapes=[pltpu.VMEM((SEQ, COLS), F32), pltpu.VMEM((3, SEQ, COLS), F32)],
        compiler_params=_params(("parallel",)), name="dil_bwd_prep")(do, o, lse)


def _dil_bwd(qkv, do, dd, lse, slopes, carry):
    def body(sl_ref, q_ref, k_ref, v_ref, do_ref, dd_ref, lse_ref, dqkv_ref, dk_acc, dv_acc):
        g, hp, b = pl.program_id(0), pl.program_id(1), pl.program_id(2)

        @pl.when(b == 0)
        def _():
            dk_acc[...] = jnp.zeros_like(dk_acc)
            dv_acc[...] = jnp.zeros_like(dv_acc)

        start = _dil_start(b)
        neg_dist = _dil_neg_dist(g, b * DIL_QB + lax.broadcasted_iota(jnp.int32, (DIL_WIN, DIL_QB), 1),
                                 start + lax.broadcasted_iota(jnp.int32, (DIL_WIN, DIL_QB), 0))
        dil = jnp.left_shift(1, 2 * g).astype(F32)
        q = q_ref[...]
        do = do_ref[...]
        kw = k_ref[pl.ds(start, DIL_WIN), :]
        vw = v_ref[pl.ds(start, DIL_WIN), :]
        dqs, dks, dvs = [], [], []
        for hh in range(DIL_HPS):
            sl = slice(hh * HD, (hh + 1) * HD)
            qh = q[:, sl] * QSCALE
            st = lax.dot_general(kw[:, sl], qh, _NT, preferred_element_type=F32)
            st = st + (sl_ref[hp * DIL_HPS + hh] * dil) * neg_dist
            pt = jnp.exp(st - lse_ref[hh, 0:1, :])
            dpt = lax.dot_general(vw[:, sl], do[:, sl], _NT, preferred_element_type=F32)
            dst = (pt * (dpt - dd_ref[hh, 0:1, :])).astype(BF16)
            dqs.append(lax.dot_general(kw[:, sl], dst, _TN, preferred_element_type=F32).T * QSCALE)
            dks.append(jnp.dot(dst, qh, preferred_element_type=F32))
            dvs.append(jnp.dot(pt.astype(BF16), do[:, sl], preferred_element_type=F32))
        dqkv_ref[0, pl.ds(pl.multiple_of(b * DIL_QB, DIL_QB), DIL_QB), :] = jnp.concatenate(dqs, axis=1).astype(dqkv_ref.dtype)
        dk_acc[pl.ds(start, DIL_WIN), :] += jnp.concatenate(dks, axis=1)
        dv_acc[pl.ds(start, DIL_WIN), :] += jnp.concatenate(dvs, axis=1)

        @pl.when(b == DIL_BLOCKS - 1)
        def _():
            dqkv_ref[1] = dk_acc[...].astype(dqkv_ref.dtype)
            dqkv_ref[2] = dv_acc[...].astype(dqkv_ref.dtype)

    qspec = pl.BlockSpec((None, DIL_QB, DIL_LW), lambda g, hp, b: (g, b, hp))
    rspec = pl.BlockSpec((None, DIL_HPS, 8, DIL_QB), lambda g, hp, b: (g, hp, 0, b))
    (dqkv,), sent = _carrier_call(
        "dil_bwd", body, (3, DIL_NLW, DIL_BLOCKS), _dil_in_specs() + [qspec, rspec, rspec],
        [pl.BlockSpec((None, 3, SEQ, DIL_LW), lambda g, hp, b: (g, 0, 0, hp))], [jax.ShapeDtypeStruct((3, 3, SEQ, DM), BF16)],
        [pltpu.VMEM((SEQ, DIL_LW), F32), pltpu.VMEM((SEQ, DIL_LW), F32)], (slopes, qkv, qkv, qkv, do, dd, lse), carry)
    return dqkv, sent


def _ffn_fwd(name, x, g_pre, g_post, wgt4, wut4, wd4, carry):
    tm = 512

    def body(x_ref, gpre_ref, gpost_ref, wg_ref, wu_ref, wd_ref, xn_ref, h_ref, gate_ref, up_ref, u_ref, acc):
        s = pl.program_id(1)

        @pl.when(s == 0)
        def _():
            x = x_ref[...]
            r = lax.rsqrt(jnp.mean(x * x, axis=-1, keepdims=True) + RMS_EPS)
            h_ref[...] = (x * r * gpre_ref[...]).astype(h_ref.dtype)

        h = h_ref[...]
        gate = lax.dot_general(h, wg_ref[...], _NT, preferred_element_type=F32).astype(BF16)
        up = lax.dot_general(h, wu_ref[...], _NT, preferred_element_type=F32).astype(BF16)
        gate_ref[...] = gate
        up_ref[...] = up
        gf = gate.astype(F32)
        act = (gf * jax.nn.sigmoid(gf) * up.astype(F32)).astype(BF16)
        part = jnp.dot(act, wd_ref[...], preferred_element_type=F32)

        @pl.when(s == 0)
        def _():
            acc[...] = part

        @pl.when(s > 0)
        def _():
            acc[...] += part

        @pl.when(s == NCHIP - 1)
        def _():
            u = acc[...]
            u_ref[...] = u
            r = lax.rsqrt(jnp.mean(u * u, axis=-1, keepdims=True) + RMS_EPS)
            xn_ref[...] = x_ref[...] + u * r * gpost_ref[...]

    rows = pl.BlockSpec((tm, DM), lambda i, s: (i, 0))
    vec = pl.BlockSpec((1, DM), lambda i, s: (0, 0))
    wspec = _ffn_wspec(lambda i, s: (s, 0, 0))
    mid = pl.BlockSpec((None, tm, FSH), lambda i, s: (s, i, 0))
    outs, sent = _carrier_call(
        name, body, (SEQ // tm, NCHIP), [rows, vec, vec, wspec, wspec, wspec], [rows, rows, mid, mid, rows],
        [jax.ShapeDtypeStruct((SEQ, DM), F32), jax.ShapeDtypeStruct((SEQ, DM), BF16), jax.ShapeDtypeStruct((NCHIP, SEQ, FSH), BF16),
         jax.ShapeDtypeStruct((NCHIP, SEQ, FSH), BF16), jax.ShapeDtypeStruct((SEQ, DM), F32)],
        [pltpu.VMEM((tm, DM), F32)], (x, g_pre, g_post, wgt4, wut4, wd4), carry)
    return outs, sent


def _ffn_block(layer, x, g_pre, g_post, ex):
    tag = f"l{layer}_ffn_fwd"
    (x_new, h, gate, up, u), sent = _ffn_fwd(tag, x, g_pre, g_post, ex.weight(("ffn_w_gate", layer)), ex.weight(("ffn_w_up", layer)),
                                             ex.weight(("ffn_w_down", layer)), ex.carry(tag))
    ex.carried(tag, sent)
    return x_new, (x, h, gate, up, u)


def _ffn_bwd(name, dx, x, gate, up, u, g_pre, g_post, wgt4, wut4, wd4, carry):
    tm = 512

    def body(dx_ref, x_ref, gate_ref, up_ref, u_ref, gpre_ref, gpost_ref, wg_ref, wu_ref, wd_ref,
             dxin_ref, du_ref, dgate_ref, dup_ref, act_ref, dgpre_ref, dgpost_ref, dh_acc):
        i, s = pl.program_id(0), pl.program_id(1)

        @pl.when((i == 0) & (s == 0))
        def _():
            dgpre_ref[...] = jnp.zeros_like(dgpre_ref)
            dgpost_ref[...] = jnp.zeros_like(dgpost_ref)

        @pl.when(s == 0)
        def _():
            dy = dx_ref[...]
            uu = u_ref[...]
            r = lax.rsqrt(jnp.mean(uu * uu, axis=-1, keepdims=True) + RMS_EPS)
            yh = uu * r
            t = dy * gpost_ref[...]
            du_ref[...] = (r * (t - yh * jnp.mean(t * yh, axis=-1, keepdims=True))).astype(du_ref.dtype)
            dgpost_ref[...] += jnp.sum(dy * yh, axis=0, keepdims=True)

        dact = lax.dot_general(du_ref[...], wd_ref[...], _NT, preferred_element_type=F32)
        g = gate_ref[...].astype(F32)
        upv = up_ref[...].astype(F32)
        sg = jax.nn.sigmoid(g)
        dgate = (dact * upv * sg * (1.0 + g * (1.0 - sg))).astype(BF16)
        dup = (dact * g * sg).astype(BF16)
        dgate_ref[...] = dgate
        dup_ref[...] = dup
        act_ref[...] = (g * sg * upv).astype(act_ref.dtype)
        part = jnp.dot(dgate, wg_ref[...], preferred_element_type=F32) + jnp.dot(dup, wu_ref[...], preferred_element_type=F32)

        @pl.when(s == 0)
        def _():
            dh_acc[...] = part

        @pl.when(s > 0)
        def _():
            dh_acc[...] += part

        @pl.when(s == NCHIP - 1)
        def _():
            dh = dh_acc[...]
            xx = x_ref[...]
            r = lax.rsqrt(jnp.mean(xx * xx, axis=-1, keepdims=True) + RMS_EPS)
            yh = xx * r
            t = dh * gpre_ref[...]
            dxin_ref[...] = dx_ref[...] + r * (t - yh * jnp.mean(t * yh, axis=-1, keepdims=True))
            dgpre_ref[...] += jnp.sum(dh * yh, axis=0, keepdims=True)

    rows = pl.BlockSpec((tm, DM), lambda i, s: (i, 0))
    vec = pl.BlockSpec((1, DM), lambda i, s: (0, 0))
    wspec = _ffn_wspec(lambda i, s: (s, 0, 0))
    mid = pl.BlockSpec((None, tm, FSH), lambda i, s: (s, i, 0))
    mid_shape = jax.ShapeDtypeStruct((NCHIP, SEQ, FSH), BF16)
    return _carrier_call(
        name, body, (SEQ // tm, NCHIP), [rows, rows, mid, mid, rows, vec, vec, wspec, wspec, wspec], [rows, rows, mid, mid, mid, vec, vec],
        [jax.ShapeDtypeStruct((SEQ, DM), F32), jax.ShapeDtypeStruct((SEQ, DM), BF16), mid_shape, mid_shape, mid_shape,
         jax.ShapeDtypeStruct((1, DM), F32), jax.ShapeDtypeStruct((1, DM), F32)],
        [pltpu.VMEM((tm, DM), F32)], (dx, x, gate, up, u, g_pre, g_post, wgt4, wut4, wd4), carry)


def _ffn_block_bwd(layer, dx, saved, g_pre, g_post, ex):
    tag = f"l{layer}"
    x, h, gate, up, u = saved
    (dx_in, du, dgate, dup, act, dg_pre, dg_post), sent = _ffn_bwd(
        f"{tag}_ffn_bwd", dx, x, gate, up, u, g_pre, g_post, ex.weight(("ffn_w_gate", layer)), ex.weight(("ffn_w_up", layer)),
        ex.weight(("ffn_w_down", layer)), ex.carry(f"{tag}_ffn_bwd"))
    ex.carried(f"{tag}_ffn_bwd", sent)
    d_wd = _ffn_bwd_dw(f"{tag}_dwd", act, du)
    d_wg = _ffn_bwd_dw(f"{tag}_dwg", dgate, h)
    d_wu = _ffn_bwd_dw(f"{tag}_dwu", dup, h)
    ex.grads(f"{tag}_ffn", {("ffn_w_gate", layer): d_wg, ("ffn_w_up", layer): d_wu, ("ffn_w_down", layer): d_wd})
    return dx_in, dg_pre, dg_post


def _alibi_slopes():
    return 2.0 ** (-8.0 * jnp.arange(1, NH + 1, dtype=F32) / NH)


def _local_step(x, target, norms, rpb, ex):
    g_mix_pre, g_mix_post, g_ffn_pre, g_ffn_post = norms
    row = lambda a, i: a[i:i + 1]

    bias, sent = _na_bias_tiles(rpb, ex.carry("na_bias_tiles"))
    ex.carried("na_bias_tiles", sent)
    (h0, h0t), sent = _rms_fwd_both("l0_mix_pre", x, row(g_mix_pre, 0), ex.carry("l0_mix_pre"))
    ex.carried("l0_mix_pre", sent)
    qkv0, sent = _qkv_fwd("l0_qkv", h0[None], ex.weight(("na_w_qkv", 0)), ex.carry("l0_qkv"))
    ex.carried("l0_qkv", sent)
    o0, sent = _na_fwd(qkv0[0], bias, ex.carry("na_fwd"))
    ex.carried("na_fwd", sent)
    na_wo = ex.weight(("na_w_o", 0)).reshape(DM, DM)
    x1, u0 = _proj_fwd("l0_proj", o0, na_wo, x, row(g_mix_post, 0))
    x2, ffn0 = _ffn_block(0, x1, row(g_ffn_pre, 0), row(g_ffn_post, 0), ex)

    slopes = _alibi_slopes()
    h2g, h2gt = _to_groups("l1_h_groups", _rms_fwd("l1_mix_pre", x2, row(g_mix_pre, 1), F32))
    dil_wqkv = ex.weight(("dil_w_qkv", 0))
    qkv1, sent = _qkv_fwd("l1_qkv", h2g, dil_wqkv, ex.carry("l1_qkv"))
    ex.carried("l1_qkv", sent)
    og, lg, sent = _dil_fwd(qkv1, slopes, ex.carry("dil_fwd"))
    ex.carried("dil_fwd", sent)
    o1, lse = _dil_merge(og, lg)
    dil_wo = ex.weight(("dil_w_o", 0)).reshape(DM, DM)
    x3, u1 = _proj_fwd("l1_proj", o1, dil_wo, x2, row(g_mix_post, 1))
    x4, ffn1 = _ffn_block(1, x3, row(g_ffn_pre, 1), row(g_ffn_post, 1), ex)

    dx4, loss_row = _loss_grad("loss", x4, target)

    dx3, dg_fpre1, dg_fpost1 = _ffn_block_bwd(1, dx4, ffn1, row(g_ffn_pre, 1), row(g_ffn_post, 1), ex)
    (do1, du1, dg_mpost1), sent = _proj_bwd("l1_proj_bwd", dx3, u1, row(g_mix_post, 1), dil_wo, F32, ex.carry("l1_proj_bwd"))
    ex.carried("l1_proj_bwd", sent)
    d_dil_wo = _proj_bwd_dw("l1_dwo", o1, du1)
    dog, ddg, lseg = _dil_bwd_prep(do1, o1, lse)
    dqkv1, sent = _dil_bwd(qkv1, dog, ddg, lseg, slopes, ex.carry("dil_bwd"))
    ex.carried("dil_bwd", sent)
    d_dil_wqkv, sent = _qkv_bwd_dw("l1_dwqkv", h2gt, dqkv1, dil_wqkv.shape[2], ex.carry("l1_dwqkv"))
    ex.carried("l1_dwqkv", sent)
    ex.grads("l1_mix", {("dil_w_qkv", 0): d_dil_wqkv, ("dil_w_o", 0): d_dil_wo.reshape(NCHIP, DM // NCHIP, DM)})
    dh2g, sent = _qkv_bwd_dh("l1_dh", dqkv1, dil_wqkv, ex.carry("l1_dh"))
    ex.carried("l1_dh", sent)
    dh2 = _from_groups_sum("l1_dh_tokens", dh2g)
    (dx2, dg_mpre1), sent = _norm_bwd("l1_mix_pre_bwd", dh2, x2, row(g_mix_pre, 1), dx3, ex.carry("l1_mix_pre_bwd"))
    ex.carried("l1_mix_pre_bwd", sent)

    dx1, dg_fpre0, dg_fpost0 = _ffn_block_bwd(0, dx2, ffn0, row(g_ffn_pre, 0), row(g_ffn_post, 0), ex)
    (do0, du0, dg_mpost0), sent = _proj_bwd("l0_proj_bwd", dx1, u0, row(g_mix_post, 0), na_wo, BF16, ex.carry("l0_proj_bwd"))
    ex.carried("l0_proj_bwd", sent)
    d_na_wo = _proj_bwd_dw("l0_dwo", o0, du0)
    dqkv0, z, sent = _na_bwd(qkv0[0], bias, do0, ex.carry("na_bwd"))
    ex.carried("na_bwd", sent)
    d_rpb = _rpb_grad(z)
    na_wqkv = ex.weight(("na_w_qkv", 0))
    d_na_wqkv, sent = _qkv_bwd_dw("l0_dwqkv", h0t[None], dqkv0[None], na_wqkv.shape[2], ex.carry("l0_dwqkv"))
    ex.carried("l0_dwqkv", sent)
    ex.grads("l0_mix", {("na_w_qkv", 0): d_na_wqkv, ("na_w_o", 0): d_na_wo.reshape(NCHIP, DM // NCHIP, DM)})
    dh0, sent = _qkv_bwd_dh("l0_dh", dqkv0[None], na_wqkv, ex.carry("l0_dh"))
    ex.carried("l0_dh", sent)
    (dx0, dg_mpre0), sent = _norm_bwd("l0_mix_pre_bwd", dh0[0], x, row(g_mix_pre, 0), dx1, ex.carry("l0_mix_pre_bwd"))
    ex.carried("l0_mix_pre_bwd", sent)

    dnorms = (jnp.concatenate([dg_mpre0, dg_mpre1]), jnp.concatenate([dg_mpost0, dg_mpost1]),
              jnp.concatenate([dg_fpre0, dg_fpre1]), jnp.concatenate([dg_fpost0, dg_fpost1]))
    return loss_row, dx0, dnorms, d_rpb


def _place():
    x, y, c = lax.axis_index("x"), lax.axis_index("y"), lax.axis_index("c")
    chips = ((1 - x, y), (x, 1 - y), (1 - x, 1 - y))
    return x, y, c, chips


def _chip_id(chip):
    return 2 * chip[0] + chip[1]


def _gather_copies(shards):
    n = len(shards)

    def copies(src, out, sems):
        send_sems, recv_sems = sems
        x, y, c, chips = _place()

        def copy(t, k, chip, half, to, from_src=False):
            blk = out[t].at[_chip_id(chip), half]
            return pltpu.make_async_remote_copy(
                src_ref=src[t].at[half] if from_src else blk, dst_ref=blk,
                send_sem=send_sems.at[6 * t + k], recv_sem=recv_sems.at[6 * t + k], device_id=to, device_id_type=MESH)

        return copy, x, y, c, chips

    def issue(src, out, sems):
        copy, x, y, c, chips = copies(src, out, sems)
        for t in range(n):
            for j, chip in enumerate(chips):
                copy(t, j, (x, y), c, (*chip, c), from_src=True).start()

    def drain(src, out, sems):
        copy, x, y, c, chips = copies(src, out, sems)
        passed = []
        for t in range(n):
            for j, chip in enumerate(chips):
                copy(t, j, chip, c, (x, y, c)).wait_recv()
                fwd = copy(t, 3 + j, chip, c, (x, y, 1 - c))
                fwd.start()
                passed.append(fwd)
        for t in range(n):
            for j, chip in enumerate(chips):
                copy(t, 3 + j, chip, 1 - c, (x, y, c)).wait_recv()
        for t in range(n):
            for j, chip in enumerate(chips):
                copy(t, j, (x, y), c, (*chip, c), from_src=True).wait_send()
        for cp in passed:
            cp.wait_send()

    return _Carried(shards, [jax.ShapeDtypeStruct((NCHIP,) + s.shape, s.dtype) for s in shards], (6 * n, 6 * n), issue, drain)


def _pair_exchange_copies(grads):
    n = len(grads)

    def copies(g, theirs, sems):
        send_sems, recv_sems = sems
        x, y, c, _ = _place()
        return [pltpu.make_async_remote_copy(src_ref=g[t].at[:, 1 - c], dst_ref=theirs[t], send_sem=send_sems.at[t],
                                             recv_sem=recv_sems.at[t], device_id=(x, y, 1 - c), device_id_type=MESH) for t in range(n)]

    def issue(g, theirs, sems):
        for cp in copies(g, theirs, sems):
            cp.start()

    def drain(g, theirs, sems):
        for cp in copies(g, theirs, sems):
            cp.wait()

    return _Carried(grads, [jax.ShapeDtypeStruct((NCHIP,) + g.shape[2:], g.dtype) for g in grads], (n, n), issue, drain)


def _chip_exchange_copies(items):
    flat = [(t, i, j) for t, (_, peers) in enumerate(items) for i, j in enumerate(peers)]

    def copies(p, slots, sems):
        send_sems, recv_sems = sems
        x, y, c, chips = _place()
        return [pltpu.make_async_remote_copy(src_ref=p[t].at[_chip_id(chips[j])], dst_ref=slots[t].at[i], send_sem=send_sems.at[k],
                                             recv_sem=recv_sems.at[k], device_id=(*chips[j], c), device_id_type=MESH)
                for k, (t, i, j) in enumerate(flat)]

    def issue(p, slots, sems):
        for cp in copies(p, slots, sems):
            cp.start()

    def drain(p, slots, sems):
        for cp in copies(p, slots, sems):
            cp.wait()

    return _Carried([p for p, _ in items], [jax.ShapeDtypeStruct((len(peers),) + p.shape[1:], p.dtype) for p, peers in items],
                    (len(flat), len(flat)), issue, drain)


def _pair_share_copies(halves):
    n = len(halves)

    def copies(h, other, sems):
        send_sems, recv_sems = sems
        x, y, c, _ = _place()
        return [pltpu.make_async_remote_copy(src_ref=h[t], dst_ref=other[t], send_sem=send_sems.at[t], recv_sem=recv_sems.at[t],
                                             device_id=(x, y, 1 - c), device_id_type=MESH) for t in range(n)]

    def issue(h, other, sems):
        for cp in copies(h, other, sems):
            cp.start()

    def drain(h, other, sems):
        for cp in copies(h, other, sems):
            cp.wait()

    return _Carried(halves, [jax.ShapeDtypeStruct(h.shape, h.dtype) for h in halves], (n, n), issue, drain)


SMALL_ROWS = 128


def _allreduce_small(v, carry):
    ci, co = len(carry.ins), len(carry.out_shapes)

    def body(*refs):
        v_ref, cins, o_ref, couts = refs[0], refs[1:1 + ci], refs[1 + ci], refs[2 + ci:2 + ci + co]
        buf, send_sems, recv_sems = refs[2 + ci + co:5 + ci + co]
        csems = refs[5 + ci + co:]
        carry.issue(cins, couts, csems)
        x, y, c, _ = _place()
        me = 4 * x + 2 * y + c
        flip = lambda a, f: 1 - a if f else a
        buf[me] = v_ref[...]
        peers = [(flip(x, d >> 2 & 1), flip(y, d >> 1 & 1), flip(c, d & 1)) for d in range(1, 8)]
        sends = [pltpu.make_async_remote_copy(src_ref=v_ref, dst_ref=buf.at[me], send_sem=send_sems.at[i], recv_sem=recv_sems.at[i],
                                              device_id=peer, device_id_type=MESH) for i, peer in enumerate(peers)]
        for cp in sends:
            cp.start()
        for i, (px, py, pc) in enumerate(peers):
            pltpu.make_async_remote_copy(src_ref=v_ref, dst_ref=buf.at[4 * px + 2 * py + pc], send_sem=send_sems.at[i], recv_sem=recv_sems.at[i],
                                         device_id=(px, py, pc), device_id_type=MESH).wait_recv()
        for cp in sends:
            cp.wait_send()
        acc = buf[0]
        for k in range(1, 8):
            acc = acc + buf[k]
        o_ref[...] = acc
        carry.drain(cins, couts, csems)

    vm = pl.BlockSpec(memory_space=pltpu.VMEM)
    res = pl.pallas_call(
        body, in_specs=[vm] + [HBM_SPEC] * ci, out_specs=[vm] + [HBM_SPEC] * co,
        out_shape=[jax.ShapeDtypeStruct((SMALL_ROWS, 128), F32)] + carry.out_shapes,
        scratch_shapes=[pltpu.VMEM((8, SMALL_ROWS, 128), F32), pltpu.SemaphoreType.DMA((7,)), pltpu.SemaphoreType.DMA((7,))]
        + [pltpu.SemaphoreType.DMA((k,)) for k in carry.n_sems],
        compiler_params=pltpu.CompilerParams(has_side_effects=True), name="allreduce_small")(v, *carry.ins)
    return res[0], list(res[1:])


def _row_block(rows, cols, budget=3 << 19):
    best = 8
    for bm in range(8, rows + 1, 8):
        if rows % bm == 0 and bm * cols * 4 <= budget:
            best = bm
    return best


def _pair_sum(name, place, gs, theirs):
    n = len(gs)
    _, m, c = theirs[0].shape
    bm = _row_block(m, c)

    def body(place_ref, *refs):
        for a_ref, b_ref, o_ref in zip(refs[:n], refs[n:2 * n], refs[2 * n:]):
            o_ref[...] = (a_ref[...].astype(F32) + b_ref[...].astype(F32)).astype(o_ref.dtype)

    spec = pl.BlockSpec((None, bm, c), lambda k, i, pr: (k, i, 0))
    return pl.pallas_call(
        body, out_shape=[jax.ShapeDtypeStruct(theirs[0].shape, BF16)] * n,
        grid_spec=pltpu.PrefetchScalarGridSpec(
            num_scalar_prefetch=1, grid=(NCHIP, m // bm),
            in_specs=[pl.BlockSpec((None, None, bm, c), lambda k, i, pr: (k, pr[0], i, 0))] * n + [spec] * n, out_specs=[spec] * n),
        compiler_params=_params(("parallel", "parallel")), name=name)(place, *gs, *theirs)


def _chip_sum(name, place, parts, slots):
    n, ns = len(parts), len(slots[0])
    _, m, c = parts[0].shape
    bm = _row_block(m, c)

    def body(place_ref, *refs):
        for t in range(n):
            acc = refs[t][...].astype(F32)
            for s_ref in refs[n + t * ns:n + (t + 1) * ns]:
                for i in range(s_ref.shape[0]):
                    acc = acc + s_ref[i].astype(F32)
            refs[n + n * ns + t][...] = acc

    half = pl.BlockSpec((bm, c), lambda i, pr: (i, 0))
    return pl.pallas_call(
        body, out_shape=[jax.ShapeDtypeStruct((m, c), F32)] * n,
        grid_spec=pltpu.PrefetchScalarGridSpec(
            num_scalar_prefetch=1, grid=(m // bm,),
            in_specs=[pl.BlockSpec((None, bm, c), lambda i, pr: (pr[1], i, 0))] * n
            + [pl.BlockSpec((s.shape[0], bm, c), lambda i, pr: (0, i, 0)) for group in slots for s in group],
            out_specs=[half] * n),
        compiler_params=_params(("parallel",)), name=name)(place, *parts, *[s for group in slots for s in group])


def _adamw(name, place, tensors, layer=0, into=None):
    n = len(tensors)
    lead, rows, cols = tensors[0][0].shape
    bm = _row_block(rows // 2, cols, budget=768 * 1024 // n)
    per_half = rows // 2 // bm
    c1 = 1.0 - ADAM_B1 ** ADAM_STEP
    c2 = 1.0 - ADAM_B2 ** ADAM_STEP

    def body(place_ref, *refs):
        outs = refs[len(refs) - 4 * n:]
        for t in range(n):
            w_ref, ga_ref, gb_ref, m_ref, v_ref = refs[5 * t:5 * t + 5]
            go_ref, d_ref, mo_ref, vo_ref = outs[4 * t:4 * t + 4]
            g = jnp.where(pl.program_id(0) // per_half == place_ref[0], ga_ref[...], gb_ref[...])
            mn = ADAM_B1 * m_ref[...] + (1.0 - ADAM_B1) * g
            vn = ADAM_B2 * v_ref[...] + (1.0 - ADAM_B2) * (g * g)
            go_ref[...] = g
            mo_ref[...] = mn
            vo_ref[...] = vn
            d_ref[...] = -ADAM_LR * ((mn / c1) / (jnp.sqrt(vn / c2) + ADAM_EPS) + ADAM_WD * w_ref[...])

    spec = pl.BlockSpec((None, bm, cols), lambda i, pr: (layer, i, 0))

    def half_spec(mine):
        def index(i, pr):
            first = (pr[0] == 0) == mine
            park = jnp.where(first, per_half - 1, 0)
            return jnp.where((i < per_half) == first, i % per_half, park), 0
        return pl.BlockSpec((bm, cols), index)
    sh = jax.ShapeDtypeStruct((lead, rows, cols), F32)
    prev = [] if into is None else [a for res in into for a in res]
    res = pl.pallas_call(
        body, out_shape=[sh] * (4 * n), input_output_aliases={1 + 5 * n + k: k for k in range(len(prev))},
        grid_spec=pltpu.PrefetchScalarGridSpec(
            num_scalar_prefetch=1, grid=(rows // bm,),
            in_specs=[spec, half_spec(True), half_spec(False), spec, spec] * n + [pl.BlockSpec(memory_space=pl.ANY)] * len(prev),
            out_specs=[spec] * (4 * n)),
        compiler_params=_params(("parallel",)), name=name)(place, *[a for t in tensors for a in t], *prev)
    return [res[4 * t:4 * t + 4] for t in range(n)]


def _pack_small(norms, rpb, last=None):
    flat = jnp.concatenate([a.reshape(-1) for a in norms] + [rpb.reshape(-1)])
    flat = jnp.pad(flat, (0, SMALL_ROWS * 128 - flat.shape[0]))
    if last is not None:
        flat = lax.dynamic_update_slice(flat, last.reshape(1), (flat.shape[0] - 1,))
    return flat.reshape(SMALL_ROWS, 128)


def _unpack_small(p):
    flat = p.reshape(-1)
    norms = [flat[i * 2 * DM:(i + 1) * 2 * DM].reshape(2, DM) for i in range(4)]
    rpb = flat[8 * DM:8 * DM + NH * 15 * 31].reshape(1, NH, 15, 31)
    return norms, rpb


FFN_NAMES = ("ffn_w_gate", "ffn_w_up", "ffn_w_down")
L0_FFN = tuple((n, 0) for n in FFN_NAMES)
L1_FFN = tuple((n, 1) for n in FFN_NAMES)
NA_KEYS = (("na_w_qkv", 0), ("na_w_o", 0))
DIL_KEYS = (("dil_w_qkv", 0), ("dil_w_o", 0))
ALL_PEERS, NEIGHBOURS, DIAGONAL = (0, 1, 2), (0, 1), (2,)


class _Exchange:
    GATHERS = {"na_bias_tiles": NA_KEYS[:1], "l0_mix_pre": NA_KEYS[1:], "l0_qkv": L0_FFN[:1], "na_fwd": L0_FFN[1:], "l0_ffn_fwd": DIL_KEYS[:1], "dil_fwd": L1_FFN + DIL_KEYS[1:]}
    PAIRS = {"l1_proj_bwd": L1_FFN, "l1_dh": DIL_KEYS, "l0_proj_bwd": L0_FFN}
    EXCHANGES = {"dil_bwd": [(k, ALL_PEERS) for k in L1_FFN],
                 "l0_ffn_bwd": [(DIL_KEYS[0], NEIGHBOURS), (DIL_KEYS[1], ALL_PEERS)],
                 "na_bwd": [(k, ALL_PEERS) for k in L0_FFN] + [(DIL_KEYS[0], DIAGONAL)],
                 "l0_dh": [(k, NEIGHBOURS) for k in NA_KEYS],
                 "allreduce_small": [(k, DIAGONAL) for k in NA_KEYS]}
    SHARES = {"l1_dwqkv": L1_FFN, "l0_dwqkv": L0_FFN + DIL_KEYS}

    def __init__(self, shards):
        self.chip = 2 * lax.axis_index("x") + lax.axis_index("y")
        self.place = jnp.stack([lax.axis_index("c"), self.chip]).astype(jnp.int32)
        self.own = {k: s.reshape(2, s.shape[0] // 2, s.shape[1]).astype(BF16) for k, s in shards.items()}
        self.gathered, self.mine, self.parts, self.slots, self.full, self.other = {}, {}, {}, {}, {}, {}

    def _take(self, keys, landed):
        for k, gw in zip(keys, landed):
            self.gathered[k] = lax.dynamic_update_slice(gw, self.own[k][None], (self.chip, 0, 0, 0))

    def _sum(self, items, landed):
        runs = []
        for (k, peers), s in zip(items, landed):
            got = self.slots.setdefault(k, {})
            got[peers] = s
            if sum(len(p) for p in got) == len(ALL_PEERS):
                like = (self.parts[k].shape, tuple(sorted(got)))
                if runs and runs[-1][0] == like:
                    runs[-1][1].append(k)
                else:
                    runs.append((like, [k]))
        for (_, split), ks in runs:
            sums = _chip_sum(f"chip_sum_{ks[0][0]}_{ks[0][1]}", self.place, [self.parts[k] for k in ks],
                             [[self.slots[k][p] for p in split] for k in ks])
            self.full.update(zip(ks, sums))

    def weight(self, key):
        g = self.gathered[key]
        return g.reshape(NCHIP, 2 * g.shape[2], g.shape[3])

    def _pair_sums(self, keys, theirs):
        runs = []
        for k, t in zip(keys, theirs):
            if runs and runs[-1][0][1].shape == t.shape:
                runs[-1].append((k, t))
            else:
                runs.append([(k, t)])
        for run in runs:
            ks = [k for k, _ in run]
            sums = _pair_sum(f"pair_sum_{ks[0][0]}_{ks[0][1]}", self.place, [self.mine[k] for k in ks], [t for _, t in run])
            self.parts.update(zip(ks, sums))

    def carry(self, tag):
        if tag in self.GATHERS:
            return _gather_copies([self.own[k] for k in self.GATHERS[tag]])
        if tag in self.PAIRS:
            return _pair_exchange_copies([self.mine[k] for k in self.PAIRS[tag]])
        if tag in self.EXCHANGES:
            return _chip_exchange_copies([(self.parts[k], peers) for k, peers in self.EXCHANGES[tag]])
        if tag in self.SHARES:
            return _pair_share_copies([self.full[k] for k in self.SHARES[tag]])
        return None

    def carried(self, tag, landed):
        if tag in self.GATHERS:
            self._take(self.GATHERS[tag], landed)
        elif tag in self.PAIRS:
            self._pair_sums(self.PAIRS[tag], landed)
        elif tag in self.EXCHANGES:
            self._sum(self.EXCHANGES[tag], landed)
        elif tag in self.SHARES:
            self.other.update(zip(self.SHARES[tag], landed))

    def grads(self, tag, dw):
        for k, g in dw.items():
            self.mine[k] = g.reshape(NCHIP, 2, -1, g.shape[-1])
        if tag == "l0_mix":
            keys = tuple(dw)
            self._pair_sums(keys, _run_carried("grad_pair_exchange_last", _pair_exchange_copies([self.mine[k] for k in keys])))

    def finish(self):
        rest = tuple(k for k in self.full if k not in self.other)
        self.other.update(zip(rest, _run_carried("grad_pair_share_last", _pair_share_copies([self.full[k] for k in rest]))))
        return {k: (self.full[k], self.other[k]) for k in self.full}


def kernel(x, norm_mix_pre, norm_mix_post, norm_ffn_pre, norm_ffn_post, na_w_qkv, na_w_o, na_rpb, dil_w_qkv, dil_w_o, ffn_w_gate, ffn_w_up, ffn_w_down, loss_target, m_norm_mix_pre, m_norm_mix_post, m_norm_ffn_pre, m_norm_ffn_post, m_na_w_qkv, m_na_w_o, m_na_rpb, m_dil_w_qkv, m_dil_w_o, m_ffn_w_gate, m_ffn_w_up, m_ffn_w_down, v_norm_mix_pre, v_norm_mix_post, v_norm_ffn_pre, v_norm_ffn_post, v_na_w_qkv, v_na_w_o, v_na_rpb, v_dil_w_qkv, v_dil_w_o, v_ffn_w_gate, v_ffn_w_up, v_ffn_w_down):
    tr = lambda a: jnp.swapaxes(a, 1, 2)
    weights = {"na_w_qkv": na_w_qkv, "na_w_o": na_w_o, "dil_w_qkv": dil_w_qkv, "dil_w_o": dil_w_o,
               "ffn_w_gate": tr(ffn_w_gate), "ffn_w_up": tr(ffn_w_up), "ffn_w_down": ffn_w_down}
    m_in = {"na_w_qkv": m_na_w_qkv, "na_w_o": m_na_w_o, "dil_w_qkv": m_dil_w_qkv, "dil_w_o": m_dil_w_o,
            "ffn_w_gate": tr(m_ffn_w_gate), "ffn_w_up": tr(m_ffn_w_up), "ffn_w_down": m_ffn_w_down}
    v_in = {"na_w_qkv": v_na_w_qkv, "na_w_o": v_na_w_o, "dil_w_qkv": v_dil_w_qkv, "dil_w_o": v_dil_w_o,
            "ffn_w_gate": tr(v_ffn_w_gate), "ffn_w_up": tr(v_ffn_w_up), "ffn_w_down": v_ffn_w_down}

    ex = _Exchange({(n, l): weights[n][l] for n in weights for l in range(weights[n].shape[0])})
    norms = (norm_mix_pre, norm_mix_post, norm_ffn_pre, norm_ffn_post)
    loss_row, dx, dnorms, d_rpb = _local_step(x[0], loss_target[0], norms, na_rpb[0], ex)
    small, sent = _allreduce_small(_pack_small(dnorms, d_rpb, last=loss_row[0, 0]), ex.carry("allreduce_small"))
    ex.carried("allreduce_small", sent)
    full = ex.finish()
    loss = small[SMALL_ROWS - 1, 127]

    out_g, out_d, out_m, out_v = {}, {}, {}, {}
    operands = lambda n, l: (weights[n], *full[(n, l)], m_in[n], v_in[n])
    results = {n: _adamw(f"adamw_{n}", ex.place, [operands(n, 0)])[0] for n in weights if n not in FFN_NAMES}
    ffn = None
    for l in range(2):
        ffn = _adamw(f"adamw_ffn_{l}", ex.place, [operands(n, l) for n in FFN_NAMES], l, ffn)
    results.update(zip(FFN_NAMES, ffn))
    for n, res in results.items():
        if n in ("ffn_w_gate", "ffn_w_up"):
            res = [tr(r) for r in res]
        out_g[n], out_d[n], out_m[n], out_v[n] = res
    sm_names = ("norm_mix_pre", "norm_mix_post", "norm_ffn_pre", "norm_ffn_post", "na_rpb")
    sm = _adamw("adamw_small", jnp.zeros((2,), jnp.int32),
                [(_pack_small(norms, na_rpb)[None], small[:SMALL_ROWS // 2], small[SMALL_ROWS // 2:],
                  _pack_small((m_norm_mix_pre, m_norm_mix_post, m_norm_ffn_pre, m_norm_ffn_post), m_na_rpb)[None],
                  _pack_small((v_norm_mix_pre, v_norm_mix_post, v_norm_ffn_pre, v_norm_ffn_post), v_na_rpb)[None])])[0]
    for res, dst in zip(sm, (out_g, out_d, out_m, out_v)):
        ns, rp = _unpack_small(res)
        for n, a in zip(sm_names, ns + [rp]):
            dst[n] = a

    order = ("norm_mix_pre", "norm_mix_post", "norm_ffn_pre", "norm_ffn_post", "na_w_qkv", "na_w_o", "na_rpb", "dil_w_qkv", "dil_w_o",
             "ffn_w_gate", "ffn_w_up", "ffn_w_down")
    return (loss, dx[None], *[out_g[n] for n in order], *[out_d[n] for n in order], *[out_m[n] for n in order], *[out_v[n] for n in order])
```

```python
import functools

import numpy as np
import jax
import jax.numpy as jnp
from jax import lax
from jax.experimental import pallas as pl
from jax.experimental.pallas import tpu as pltpu

F32 = jnp.float32
BF16 = jnp.bfloat16

SEQ = 2048
DM = 1024
NH = 16
HD = 64
DFF = 2816
NCHIP = 4
FSH = DFF // NCHIP
GRID_W = 64
NA_QROWS = 4
NA_QB = NA_QROWS * GRID_W
NA_WROWS = 12
NA_WIN = NA_WROWS * GRID_W
DIL = (1, 4, 16)
DIL_QB = 256
DIL_WIN = DIL_QB + 128
DIL_RADIUS = 64
RMS_EPS = 1e-6
NEG = -1e30
QSCALE = HD ** -0.5
CH = 256
MESH = pl.DeviceIdType.MESH

ADAM_LR, ADAM_B1, ADAM_B2, ADAM_EPS, ADAM_WD, ADAM_STEP = 0.001, 0.9, 0.999, 1e-08, 0.01, 10

VMEM_LIMIT = 56 * 1024 * 1024

_NN = (((1,), (0,)), ((), ()))
_NT = (((1,), (1,)), ((), ()))
_TN = (((0,), (0,)), ((), ()))


def _params(sem):
    return pltpu.CompilerParams(dimension_semantics=sem, vmem_limit_bytes=VMEM_LIMIT)


def _matmul(name, pairs, grid, out_shape, out_spec, acc_shape, carrying=False, carry=None):
    nk = grid[-1]
    npair = len(pairs)
    n_in = 2 * npair

    def body(*refs):
        ins, o_ref = refs[:2 * npair], refs[n_in]
        part = None
        for p in range(npair):
            d = lax.dot_general(ins[2 * p][...].astype(BF16), ins[2 * p + 1][...].astype(BF16), pairs[p][4],
                                preferred_element_type=F32)
            part = d if part is None else part + d
        if nk == 1:
            o_ref[...] = part.astype(o_ref.dtype)
        else:
            acc_ref = refs[n_in + 1]
            kk = pl.program_id(len(grid) - 1)

            @pl.when(kk == 0)
            def _():
                acc_ref[...] = part

            @pl.when(kk > 0)
            def _():
                acc_ref[...] += part

            @pl.when(kk == nk - 1)
            def _():
                o_ref[...] = acc_ref[...].astype(o_ref.dtype)

    ops, specs = [], []
    for a, a_spec, b, b_spec, _ in pairs:
        ops += [a, b]
        specs += [a_spec, b_spec]
    (out,), sent = _carrier_call(name, body, grid, specs, [out_spec], [out_shape], [] if nk == 1 else [pltpu.VMEM(acc_shape, F32)], ops, carry)
    return (out, sent) if carrying else out


def _qkv_fwd(name, h_all, w4, carry):
    g_n = h_all.shape[0]
    per = w4.shape[2] // CH
    return _matmul(
        name, [(h_all, pl.BlockSpec((None, SEQ, DM), lambda g, q, k: (g, 0, 0)),
                w4, pl.BlockSpec((None, DM, CH), lambda g, q, k: ((g * 12 + q) // per, 0, (g * 12 + q) % per)), _NN)],
        (g_n, 12, 1), jax.ShapeDtypeStruct((g_n, SEQ, 3 * DM), BF16),
        pl.BlockSpec((None, SEQ, CH), lambda g, q, k: (g, 0, q)), None, carrying=True, carry=carry)


def _qkv_bwd_dh(name, dqkv, w4, carry):
    g_n = dqkv.shape[0]
    per = w4.shape[2] // CH
    tm = SEQ

    def pair(cb):
        chunk = lambda g, t: g * 12 + t * 4 + cb
        return (dqkv, pl.BlockSpec((None, None, tm, CH), lambda g, i, t: (g, t, i, cb)),
                w4, pl.BlockSpec((None, DM, CH), lambda g, i, t: (chunk(g, t) // per, 0, chunk(g, t) % per)), _NT)

    return _matmul(name, [pair(cb) for cb in range(4)], (g_n, SEQ // tm, 3), jax.ShapeDtypeStruct((g_n, SEQ, DM), F32),
                   pl.BlockSpec((None, tm, DM), lambda g, i, t: (g, i, 0)), (tm, DM), carrying=True, carry=carry)


def _qkv_bwd_dw(name, ht_all, dqkv, shard_cols, carry):
    g_n = dqkv.shape[0]
    per = shard_cols // CH
    return _matmul(
        name, [(ht_all, pl.BlockSpec((None, DM, SEQ), lambda qq, k: (qq // 12, 0, 0)),
                dqkv, pl.BlockSpec((None, None, SEQ, CH), lambda qq, k: (qq // 12, (qq % 12) // 4, 0, qq % 4)), _NN)],
        (g_n * 12, 1), jax.ShapeDtypeStruct((NCHIP, DM, shard_cols), BF16),
        pl.BlockSpec((None, DM, CH), lambda qq, k: (qq // per, 0, qq % per)), None, carrying=True, carry=carry)


def _proj_fwd(name, o, wo, x, g):
    tm = 512

    def body(o_ref, w_ref, x_ref, g_ref, xn_ref, u_ref):
        u = jnp.dot(o_ref[...], w_ref[...], preferred_element_type=F32)
        u_ref[...] = u
        r = lax.rsqrt(jnp.mean(u * u, axis=-1, keepdims=True) + RMS_EPS)
        xn_ref[...] = x_ref[...] + u * r * g_ref[...]

    rows = pl.BlockSpec((tm, DM), lambda i: (i, 0))
    sh = jax.ShapeDtypeStruct((SEQ, DM), F32)
    return pl.pallas_call(
        body, grid=(SEQ // tm,), in_specs=[rows, pl.BlockSpec((DM, DM), lambda i: (0, 0)), rows, pl.BlockSpec((1, DM), lambda i: (0, 0))],
        out_specs=[rows, rows], out_shape=[sh, sh], compiler_params=_params(("parallel",)), name=name)(o, wo, x, g)


def _proj_bwd(name, dy, u, g, wo, dtype, carry):
    tm = 512

    def body(dy_ref, u_ref, g_ref, w_ref, do_ref, du_ref, dg_ref):
        dy = dy_ref[...]
        u = u_ref[...]
        r = lax.rsqrt(jnp.mean(u * u, axis=-1, keepdims=True) + RMS_EPS)
        yh = u * r
        t = dy * g_ref[...]
        du = (r * (t - yh * jnp.mean(t * yh, axis=-1, keepdims=True))).astype(BF16)
        du_ref[...] = du
        do_ref[...] = lax.dot_general(du, w_ref[...], _NT, preferred_element_type=F32).astype(do_ref.dtype)

        @pl.when(pl.program_id(0) == 0)
        def _():
            dg_ref[...] = jnp.zeros_like(dg_ref)

        dg_ref[...] += jnp.sum(dy * yh, axis=0, keepdims=True)

    rows = pl.BlockSpec((tm, DM), lambda i: (i, 0))
    vec = pl.BlockSpec((1, DM), lambda i: (0, 0))
    return _carrier_call(
        name, body, (SEQ // tm,), [rows, rows, vec, pl.BlockSpec((DM, DM), lambda i: (0, 0))], [rows, rows, vec],
        [jax.ShapeDtypeStruct((SEQ, DM), dtype), jax.ShapeDtypeStruct((SEQ, DM), BF16), jax.ShapeDtypeStruct((1, DM), F32)],
        [], (dy, u, g, wo), carry)


def _proj_bwd_dw(name, o, du):
    tn = 512
    return _matmul(
        name, [(o, pl.BlockSpec((SEQ, DM), lambda j, k: (0, 0)), du, pl.BlockSpec((SEQ, tn), lambda j, k: (0, j)), _TN)],
        (DM // tn, 1), jax.ShapeDtypeStruct((DM, DM), BF16), pl.BlockSpec((DM, tn), lambda j, k: (0, j)), None)


def _ffn_wspec(index_map):
    return pl.BlockSpec((None, FSH, DM), index_map)


def _ffn_bwd_dw(name, a4, b):
    return _matmul(
        name, [(a4, pl.BlockSpec((None, SEQ, FSH), lambda s, k: (s, 0, 0)), b, pl.BlockSpec((SEQ, DM), lambda s, k: (0, 0)), _TN)],
        (NCHIP, 1), jax.ShapeDtypeStruct((NCHIP, FSH, DM), BF16), _ffn_wspec(lambda s, k: (s, 0, 0)), None)


ROWS = 256


def _row_spec():
    return pl.BlockSpec((ROWS, DM), lambda i: (i, 0))


def _vec_spec():
    return pl.BlockSpec((1, DM), lambda i: (0, 0))


def _rms_fwd(name, x, g, dtype=BF16):
    def body(x_ref, g_ref, o_ref):
        x = x_ref[...]
        r = lax.rsqrt(jnp.mean(x * x, axis=-1, keepdims=True) + RMS_EPS)
        o_ref[...] = (x * r * g_ref[...]).astype(o_ref.dtype)

    return pl.pallas_call(body, grid=(SEQ // ROWS,), in_specs=[_row_spec(), _vec_spec()], out_specs=_row_spec(),
                          out_shape=jax.ShapeDtypeStruct((SEQ, DM), dtype), compiler_params=_params(("parallel",)), name=name)(x, g)


def _rms_fwd_both(name, x, g):
    def body(x_ref, g_ref, o_ref, t_ref):
        x = x_ref[...]
        r = lax.rsqrt(jnp.mean(x * x, axis=-1, keepdims=True) + RMS_EPS)
        h = x * r * g_ref[...]
        o_ref[...] = h.astype(o_ref.dtype)
        t_ref[...] = h.T.astype(t_ref.dtype)

    return pl.pallas_call(
        body, grid=(SEQ // ROWS,), in_specs=[_row_spec(), _vec_spec()], out_specs=[_row_spec(), pl.BlockSpec((DM, ROWS), lambda i: (0, i))],
        out_shape=[jax.ShapeDtypeStruct((SEQ, DM), BF16), jax.ShapeDtypeStruct((DM, SEQ), BF16)],
        compiler_params=_params(("parallel",)), name=name)(x, g)


def _norm_bwd(name, dy, u, g, res, carry):
    def body(dy_ref, u_ref, g_ref, res_ref, du_ref, dg_ref):
        dy = dy_ref[...]
        u = u_ref[...]
        r = lax.rsqrt(jnp.mean(u * u, axis=-1, keepdims=True) + RMS_EPS)
        yh = u * r
        t = dy * g_ref[...]
        du_ref[...] = r * (t - yh * jnp.mean(t * yh, axis=-1, keepdims=True)) + res_ref[...]

        @pl.when(pl.program_id(0) == 0)
        def _():
            dg_ref[...] = jnp.zeros_like(dg_ref)

        dg_ref[...] += jnp.sum(dy * yh, axis=0, keepdims=True)

    return _carrier_call(
        name, body, (SEQ // ROWS,), [_row_spec(), _row_spec(), _vec_spec(), _row_spec()], [_row_spec(), _vec_spec()],
        [jax.ShapeDtypeStruct((SEQ, DM), F32), jax.ShapeDtypeStruct((1, DM), F32)], [], (dy, u, g, res), carry)


def _loss_grad(name, y, t):
    def body(y_ref, t_ref, dy_ref, l_ref):
        e = y_ref[...] - t_ref[...]
        dy_ref[...] = e * (1.0 / DM)

        @pl.when(pl.program_id(0) == 0)
        def _():
            l_ref[...] = jnp.zeros_like(l_ref)

        l_ref[...] += jnp.sum(e * e) * (0.5 / DM)

    return pl.pallas_call(
        body, grid=(SEQ // ROWS,), in_specs=[_row_spec(), _row_spec()],
        out_specs=[_row_spec(), pl.BlockSpec((1, 128), lambda i: (0, 0))],
        out_shape=[jax.ShapeDtypeStruct((SEQ, DM), F32), jax.ShapeDtypeStruct((1, 128), F32)],
        compiler_params=_params(("arbitrary",)), name=name)(y, t)


HBM_SPEC = pl.BlockSpec(memory_space=pltpu.HBM)


class _Carried:
    def __init__(self, ins, out_shapes, n_sems, issue, drain):
        self.ins, self.out_shapes, self.n_sems, self.issue, self.drain = list(ins), list(out_shapes), tuple(n_sems), issue, drain


def _carrier_call(name, body, grid, in_specs, out_specs, out_shape, scratch_shapes, operands, carry):
    n_in, n_out, n_scr = len(in_specs), len(out_specs), len(scratch_shapes)
    if carry is None:
        res = pl.pallas_call(body, grid=grid, in_specs=in_specs, out_specs=out_specs, out_shape=out_shape, scratch_shapes=scratch_shapes,
                             compiler_params=_params(("arbitrary",) * len(grid)), name=name)(*operands)
        return list(res), []
    ci, co = len(carry.ins), len(carry.out_shapes)

    def wrapped(*refs):
        ins, cins = refs[:n_in], refs[n_in:n_in + ci]
        outs, couts = refs[n_in + ci:n_in + ci + n_out], refs[n_in + ci + n_out:n_in + ci + n_out + co]
        scr, sems = refs[n_in + ci + n_out + co:n_in + ci + n_out + co + n_scr], refs[n_in + ci + n_out + co + n_scr:]
        first = functools.reduce(jnp.logical_and, [pl.program_id(a) == 0 for a in range(len(grid))])
        last = functools.reduce(jnp.logical_and, [pl.program_id(a) == grid[a] - 1 for a in range(len(grid))])

        @pl.when(first)
        def _():
            carry.issue(cins, couts, sems)

        body(*ins, *outs, *scr)

        @pl.when(last)
        def _():
            carry.drain(cins, couts, sems)

    res = pl.pallas_call(
        wrapped, grid=grid, in_specs=list(in_specs) + [HBM_SPEC] * ci, out_specs=list(out_specs) + [HBM_SPEC] * co,
        out_shape=list(out_shape) + carry.out_shapes,
        scratch_shapes=list(scratch_shapes) + [pltpu.SemaphoreType.DMA((k,)) for k in carry.n_sems],
        compiler_params=pltpu.CompilerParams(dimension_semantics=("arbitrary",) * len(grid), vmem_limit_bytes=VMEM_LIMIT, has_side_effects=True),
        name=name)(*operands, *carry.ins)
    return list(res[:n_out]), list(res[n_out:])


def _run_carried(name, carry):
    def body(*refs):
        ci, co = len(carry.ins), len(carry.out_shapes)
        carry.issue(refs[:ci], refs[ci:ci + co], refs[ci + co:])
        carry.drain(refs[:ci], refs[ci:ci + co], refs[ci + co:])

    return pl.pallas_call(
        body, in_specs=[HBM_SPEC] * len(carry.ins), out_specs=[HBM_SPEC] * len(carry.out_shapes), out_shape=carry.out_shapes,
        scratch_shapes=[pltpu.SemaphoreType.DMA((k,)) for k in carry.n_sems],
        compiler_params=pltpu.CompilerParams(has_side_effects=True), name=name)(*carry.ins)


NA_BLOCKS = SEQ // NA_QB
NA_ROWS_TOTAL = SEQ // GRID_W
NA_CLASSES = ((0, 0), (8, 4), (NA_ROWS_TOTAL - NA_QROWS, NA_ROWS_TOTAL - NA_WROWS))


def _na_pairs(i0, ws):
    out = []
    for qi in range(NA_QROWS):
        i = i0 + qi
        rs = min(max(i - 4, 0), NA_ROWS_TOTAL - 8)
        for kr in range(NA_WROWS):
            r = ws + kr
            if rs <= r < rs + 8:
                out.append((qi, kr, r - i + 7))
    return out


def _diag_onehot():
    qc, kc = np.meshgrid(np.arange(GRID_W), np.arange(GRID_W), indexing="ij")
    e = np.zeros((GRID_W * GRID_W, 128), np.float32)
    j = (kc - qc + 15).reshape(-1)
    ok = (j >= 0) & (j <= 30)
    e[np.arange(GRID_W * GRID_W)[ok], j[ok]] = 1.0
    return jnp.asarray(e)


def _rpb_expand(rpb):
    r2 = jnp.pad(rpb.reshape(NH * 15, 31), ((0, 0), (0, 128 - 31)))

    def body(r_ref, e_ref, o_ref):
        o_ref[...] = lax.dot_general(r_ref[...], e_ref[...], _NT, preferred_element_type=F32, precision=lax.Precision.HIGHEST)

    out = pl.pallas_call(body, out_shape=jax.ShapeDtypeStruct((NH * 15, GRID_W * GRID_W), F32), name="rpb_expand",
                         compiler_params=pltpu.CompilerParams(vmem_limit_bytes=VMEM_LIMIT))(r2, _diag_onehot())
    return out.reshape(NH, 15, GRID_W, GRID_W)


def _na_bias_tiles(rpb, carry):
    def body(b_ref, o_ref):
        qc = lax.broadcasted_iota(jnp.int32, (GRID_W, GRID_W), 0)
        kc = lax.broadcasted_iota(jnp.int32, (GRID_W, GRID_W), 1)
        first = jnp.clip(qc - 8, 0, GRID_W - 16)
        in_window = (kc >= first) & (kc < first + 16)
        neg = jnp.full((GRID_W, GRID_W), NEG, F32)
        for cls, (i0, ws) in enumerate(NA_CLASSES):
            @pl.when(pl.program_id(0) == cls)
            def _(i0=i0, ws=ws):
                pairs = {(qi, kr): dr for qi, kr, dr in _na_pairs(i0, ws)}
                masked = {dr: jnp.where(in_window, b_ref[dr], NEG) for dr in sorted(set(pairs.values()))}
                for qi in range(NA_QROWS):
                    for k2 in range(NA_WROWS // 2):
                        blocks = [masked[pairs[(qi, kr)]] if (qi, kr) in pairs else neg for kr in (2 * k2, 2 * k2 + 1)]
                        o_ref[qi * GRID_W:(qi + 1) * GRID_W, k2 * 128:(k2 + 1) * 128] = jnp.concatenate(blocks, axis=1)

    (tiles,), sent = _carrier_call(
        "na_bias_tiles", body, (3, NH), [pl.BlockSpec((None, 15, GRID_W, GRID_W), lambda c, h: (h, 0, 0, 0))],
        [pl.BlockSpec((None, None, NA_QB, NA_WIN), lambda c, h: (c, h, 0, 0))], [jax.ShapeDtypeStruct((3, NH, NA_QB, NA_WIN), F32)],
        [], (_rpb_expand(rpb),), carry)
    return tiles, sent


def _na_cls(b):
    return jnp.where(b == 0, 0, jnp.where(b == NA_BLOCKS - 1, 2, 1))


def _na_start(b):
    return pl.multiple_of(jnp.clip(b * NA_QROWS - 4, 0, NA_ROWS_TOTAL - NA_WROWS) * GRID_W, GRID_W)


NA_FWD_HPS = 8
NA_BWD_HPS = 4


def _na_in_specs(hps):
    lw = hps * HD
    nlw = DM // lw
    return [pl.BlockSpec((NA_QB, lw), lambda hp, b: (b, hp)),
            pl.BlockSpec((SEQ, lw), lambda hp, b: (0, nlw + hp)),
            pl.BlockSpec((SEQ, lw), lambda hp, b: (0, 2 * nlw + hp)),
            pl.BlockSpec((None, hps, NA_QB, NA_WIN), lambda hp, b: (_na_cls(b), hp, 0, 0))]


def _na_fwd(qkv, bias, carry):
    lw = NA_FWD_HPS * HD

    def body(q_ref, k_ref, v_ref, b_ref, o_ref):
        start = _na_start(pl.program_id(1))
        q = q_ref[...]
        kw = k_ref[pl.ds(start, NA_WIN), :]
        vw = v_ref[pl.ds(start, NA_WIN), :]
        outs = []
        for hh in range(NA_FWD_HPS):
            sl = slice(hh * HD, (hh + 1) * HD)
            s = lax.dot_general(q[:, sl] * QSCALE, kw[:, sl], _NT, preferred_element_type=F32) + b_ref[hh]
            p = jnp.exp(s - jnp.max(s, axis=-1, keepdims=True))
            l = jnp.sum(p, axis=-1, keepdims=True)
            outs.append(jnp.dot(p.astype(BF16), vw[:, sl], preferred_element_type=F32) / l)
        o_ref[...] = jnp.concatenate(outs, axis=1).astype(o_ref.dtype)

    (o,), sent = _carrier_call(
        "na_fwd", body, (NH // NA_FWD_HPS, NA_BLOCKS), _na_in_specs(NA_FWD_HPS), [pl.BlockSpec((NA_QB, lw), lambda hp, b: (b, hp))],
        [jax.ShapeDtypeStruct((SEQ, DM), BF16)], [], (qkv, qkv, qkv, bias), carry)
    return o, sent


def _na_bwd(qkv, bias, do, carry):
    lw = NA_BWD_HPS * HD

    def body(q_ref, k_ref, v_ref, b_ref, do_ref, dqkv_ref, z_ref, dk_acc, dv_acc):
        blk = pl.program_id(1)

        @pl.when(blk == 0)
        def _():
            dk_acc[...] = jnp.zeros_like(dk_acc)
            dv_acc[...] = jnp.zeros_like(dv_acc)
            z_ref[...] = jnp.zeros_like(z_ref)

        start = _na_start(blk)
        q = q_ref[...]
        do = do_ref[...]
        kw = k_ref[pl.ds(start, NA_WIN), :]
        vw = v_ref[pl.ds(start, NA_WIN), :]
        dqs, dks, dvs, dss = [], [], [], []
        for hh in range(NA_BWD_HPS):
            sl = slice(hh * HD, (hh + 1) * HD)
            qh = q[:, sl] * QSCALE
            s = lax.dot_general(qh, kw[:, sl], _NT, preferred_element_type=F32) + b_ref[hh]
            p = jnp.exp(s - jnp.max(s, axis=-1, keepdims=True))
            p = p / jnp.sum(p, axis=-1, keepdims=True)
            dp = lax.dot_general(do[:, sl], vw[:, sl], _NT, preferred_element_type=F32)
            ds = p * (dp - jnp.sum(p * dp, axis=-1, keepdims=True))
            dsb = ds.astype(BF16)
            dqs.append(jnp.dot(dsb, kw[:, sl], preferred_element_type=F32) * QSCALE)
            dks.append(lax.dot_general(qh, dsb, _TN, preferred_element_type=F32).T)
            dvs.append(lax.dot_general(do[:, sl], p.astype(BF16), _TN, preferred_element_type=F32).T)
            dss.append(ds)
        for cls, (i0, ws) in enumerate(NA_CLASSES):
            @pl.when(_na_cls(blk) == cls)
            def _(i0=i0, ws=ws):
                for hh, ds in enumerate(dss):
                    for qi, kr, dr in _na_pairs(i0, ws):
                        z_ref[hh, dr * GRID_W:(dr + 1) * GRID_W, :] += ds[qi * GRID_W:(qi + 1) * GRID_W, kr * GRID_W:(kr + 1) * GRID_W]
        dqkv_ref[0, pl.ds(pl.multiple_of(blk * NA_QB, NA_QB), NA_QB), :] = jnp.concatenate(dqs, axis=1).astype(dqkv_ref.dtype)
        dk_acc[pl.ds(start, NA_WIN), :] += jnp.concatenate(dks, axis=1)
        dv_acc[pl.ds(start, NA_WIN), :] += jnp.concatenate(dvs, axis=1)

        @pl.when(blk == NA_BLOCKS - 1)
        def _():
            dqkv_ref[1] = dk_acc[...].astype(dqkv_ref.dtype)
            dqkv_ref[2] = dv_acc[...].astype(dqkv_ref.dtype)

    (dqkv, z), sent = _carrier_call(
        "na_bwd", body, (NH // NA_BWD_HPS, NA_BLOCKS),
        _na_in_specs(NA_BWD_HPS) + [pl.BlockSpec((NA_QB, lw), lambda hp, b: (b, hp))],
        [pl.BlockSpec((3, SEQ, lw), lambda hp, b: (0, 0, hp)), pl.BlockSpec((NA_BWD_HPS, 15 * GRID_W, GRID_W), lambda hp, b: (hp, 0, 0))],
        [jax.ShapeDtypeStruct((3, SEQ, DM), BF16), jax.ShapeDtypeStruct((NH, 15 * GRID_W, GRID_W), F32)],
        [pltpu.VMEM((SEQ, lw), F32), pltpu.VMEM((SEQ, lw), F32)], (qkv, qkv, qkv, bias, do), carry)
    return dqkv, z, sent


def _rpb_grad(z):
    z2 = z.reshape(NH * 15, GRID_W * GRID_W)

    def body(z_ref, e_ref, o_ref):
        o_ref[...] = jnp.dot(z_ref[...], e_ref[...], preferred_element_type=F32, precision=lax.Precision.HIGHEST)

    out = pl.pallas_call(body, out_shape=jax.ShapeDtypeStruct((NH * 15, 128), F32), name="rpb_grad",
                         compiler_params=pltpu.CompilerParams(vmem_limit_bytes=VMEM_LIMIT))(z2, _diag_onehot())
    return out[:, :31].reshape(NH, 15, 31)


DIL_BLOCKS = SEQ // DIL_QB
DIL_FWD_HPS = 16
DIL_HPS = 8
DIL_LW = DIL_HPS * HD
DIL_NLW = DM // DIL_LW


COLS = 128


def _col_spec():
    return pl.BlockSpec((SEQ, COLS), lambda j: (0, j))


def _grp_spec():
    return pl.BlockSpec((3, SEQ, COLS), lambda j: (0, 0, j))


def _store_group_order(dst_ref, src_ref):
    for g, d in enumerate(DIL):
        n = SEQ // d
        for r in range(d):
            dst_ref[g, r * n:(r + 1) * n, :] = src_ref[pl.ds(r, n, stride=d), :].astype(dst_ref.dtype)


def _store_token_order(dst_ref, src_ref, g):
    d = DIL[g]
    n = SEQ // d
    for r in range(d):
        dst_ref[pl.ds(r, n, stride=d), :] = src_ref[g, r * n:(r + 1) * n, :].astype(dst_ref.dtype)


def _to_groups(name, a):
    def body(a_ref, o_ref, t_ref):
        _store_group_order(o_ref, a_ref)
        for g in range(3):
            t_ref[g] = o_ref[g].astype(F32).T.astype(t_ref.dtype)

    return pl.pallas_call(
        body, grid=(DM // COLS,), in_specs=[_col_spec()], out_specs=[_grp_spec(), pl.BlockSpec((3, COLS, SEQ), lambda j: (0, j, 0))],
        out_shape=[jax.ShapeDtypeStruct((3, SEQ, DM), BF16), jax.ShapeDtypeStruct((3, DM, SEQ), BF16)],
        compiler_params=_params(("parallel",)), name=name)(a)


def _from_groups_sum(name, a):
    def body(a_ref, o_ref, t1, t2):
        _store_token_order(t1, a_ref, 1)
        _store_token_order(t2, a_ref, 2)
        o_ref[...] = (a_ref[0] + t1[...]) + t2[...]

    return pl.pallas_call(body, grid=(DM // COLS,), in_specs=[_grp_spec()], out_specs=_col_spec(),
                          out_shape=jax.ShapeDtypeStruct((SEQ, DM), F32), scratch_shapes=[pltpu.VMEM((SEQ, COLS), F32)] * 2,
                          compiler_params=_params(("parallel",)), name=name)(a)


def _dil_start(b):
    return pl.multiple_of(jnp.clip(b * DIL_QB - DIL_RADIUS, 0, SEQ - DIL_WIN), DIL_RADIUS)


def _dil_neg_dist(g, ii, jj):
    shift = 11 - 2 * g
    dist = jnp.abs(ii - jj)
    valid = (dist <= DIL_RADIUS) & (jnp.right_shift(ii, shift) == jnp.right_shift(jj, shift))
    return jnp.where(valid, -dist.astype(F32), NEG)


def _dil_in_specs(hps):
    lw = hps * HD
    nlw = DM // lw
    return [pl.BlockSpec(memory_space=pltpu.SMEM),
            pl.BlockSpec((None, DIL_QB, lw), lambda g, hp, b: (g, b, hp)),
            pl.BlockSpec((None, SEQ, lw), lambda g, hp, b: (g, 0, nlw + hp)),
            pl.BlockSpec((None, SEQ, lw), lambda g, hp, b: (g, 0, 2 * nlw + hp))]


def _dil_fwd(qkv, slopes, carry):
    def body(sl_ref, q_ref, k_ref, v_ref, o_ref, lse_ref):
        g, hp, b = pl.program_id(0), pl.program_id(1), pl.program_id(2)
        start = _dil_start(b)
        neg_dist = _dil_neg_dist(g, b * DIL_QB + lax.broadcasted_iota(jnp.int32, (DIL_QB, DIL_WIN), 0),
                                 start + lax.broadcasted_iota(jnp.int32, (DIL_QB, DIL_WIN), 1))
        dil = jnp.left_shift(1, 2 * g).astype(F32)
        q = q_ref[...]
        kw = k_ref[pl.ds(start, DIL_WIN), :]
        vw = v_ref[pl.ds(start, DIL_WIN), :]
        outs, lses = [], []
        for hh in range(DIL_FWD_HPS):
            sl = slice(hh * HD, (hh + 1) * HD)
            s = lax.dot_general(q[:, sl] * QSCALE, kw[:, sl], _NT, preferred_element_type=F32)
            s = s + (sl_ref[hp * DIL_FWD_HPS + hh] * dil) * neg_dist
            m = jnp.max(s, axis=-1, keepdims=True)
            p = jnp.exp(s - m)
            l = jnp.sum(p, axis=-1, keepdims=True)
            outs.append(jnp.dot(p.astype(BF16), vw[:, sl], preferred_element_type=F32) / l)
            lses.append(jnp.broadcast_to(m + jnp.log(l), (DIL_QB, HD)))
        o_ref[...] = jnp.concatenate(outs, axis=1).astype(o_ref.dtype)
        lse_ref[...] = jnp.concatenate(lses, axis=1)

    ospec = pl.BlockSpec((None, DIL_QB, DIL_FWD_HPS * HD), lambda g, hp, b: (g, b, hp))
    (o, lse), sent = _carrier_call(
        "dil_fwd", body, (3, NH // DIL_FWD_HPS, DIL_BLOCKS), _dil_in_specs(DIL_FWD_HPS), [ospec, ospec],
        [jax.ShapeDtypeStruct((3, SEQ, DM), BF16), jax.ShapeDtypeStruct((3, SEQ, DM), F32)], [], (slopes, qkv, qkv, qkv), carry)
    return o, lse, sent


def _dil_merge(o_all, lse_all):
    def body(o_ref, l_ref, out_ref, lse_ref, o1, o2, l1, l2):
        for g, (ot, lt) in ((1, (o1, l1)), (2, (o2, l2))):
            _store_token_order(ot, o_ref, g)
            _store_token_order(lt, l_ref, g)
        la, lb, lc = l_ref[0], l1[...], l2[...]
        m = jnp.maximum(jnp.maximum(la, lb), lc)
        wa, wb, wc = jnp.exp(la - m), jnp.exp(lb - m), jnp.exp(lc - m)
        sw = (wa + wb) + wc
        out_ref[...] = (((wa * o_ref[0].astype(F32) + wb * o1[...]) + wc * o2[...]) / sw).astype(out_ref.dtype)
        lse_ref[...] = m + jnp.log(sw)

    return pl.pallas_call(
        body, grid=(DM // COLS,), in_specs=[_grp_spec(), _grp_spec()], out_specs=[_col_spec(), _col_spec()],
        out_shape=[jax.ShapeDtypeStruct((SEQ, DM), BF16), jax.ShapeDtypeStruct((SEQ, DM), F32)],
        scratch_shapes=[pltpu.VMEM((SEQ, COLS), F32)] * 4, compiler_params=_params(("parallel",)), name="dil_merge")(o_all, lse_all)


def _dil_bwd_prep(do, o, lse):
    heads = COLS // HD

    def body(do_ref, o_ref, lse_ref, dog_ref, ddr_ref, lser_ref, dd, grp):
        prod = do_ref[...] * o_ref[...].astype(F32)
        dd[...] = jnp.concatenate(
            [jnp.broadcast_to(jnp.sum(prod[:, h * HD:(h + 1) * HD], axis=-1, keepdims=True), (SEQ, HD)) for h in range(heads)], axis=1)
        _store_group_order(dog_ref, do_ref)
        for src, dst in ((dd, ddr_ref), (lse_ref, lser_ref)):
            _store_group_order(grp, src)
            for g in range(3):
                t = grp[g].T
                for h in range(heads):
                    dst[g, h] = t[h * HD:h * HD + 8, :]

    rows = jax.ShapeDtypeStruct((3, NH, 8, SEQ), F32)
    rspec = pl.BlockSpec((3, heads, 8, SEQ), lambda j: (0, j, 0, 0))
    return pl.pallas_call(
        body, grid=(DM // COLS,), in_specs=[_col_spec()] * 3, out_specs=[_grp_spec(), rspec, rspec],
        out_shape=[jax.ShapeDtypeStruct((3, SEQ, DM), BF16), rows, rows],
        scratch_shapes=[pltpu.VMEM((SEQ, COLS), F32), pltpu.VMEM((3, SEQ, COLS), F32)],
        compiler_params=_params(("parallel",)), name="dil_bwd_prep")(do, o, lse)


def _dil_bwd(qkv, do, dd, lse, slopes, carry):
    def body(sl_ref, q_ref, k_ref, v_ref, do_ref, dd_ref, lse_ref, dqkv_ref, dk_acc, dv_acc):
        g, hp, b = pl.program_id(0), pl.program_id(1), pl.program_id(2)

        @pl.when(b == 0)
        def _():
            dk_acc[...] = jnp.zeros_like(dk_acc)
            dv_acc[...] = jnp.zeros_like(dv_acc)

        start = _dil_start(b)
        neg_dist = _dil_neg_dist(g, b * DIL_QB + lax.broadcasted_iota(jnp.int32, (DIL_WIN, DIL_QB), 1),
                                 start + lax.broadcasted_iota(jnp.int32, (DIL_WIN, DIL_QB), 0))
        dil = jnp.left_shift(1, 2 * g).astype(F32)
        q = q_ref[...]
        do = do_ref[...]
        kw = k_ref[pl.ds(start, DIL_WIN), :]
        vw = v_ref[pl.ds(start, DIL_WIN), :]
        dqs, dks, dvs = [], [], []
        for hh in range(DIL_HPS):
            sl = slice(hh * HD, (hh + 1) * HD)
            qh = q[:, sl] * QSCALE
            st = lax.dot_general(kw[:, sl], qh, _NT, preferred_element_type=F32)
            st = st + (sl_ref[hp * DIL_HPS + hh] * dil) * neg_dist
            pt = jnp.exp(st - lse_ref[hh, 0:1, :])
            dpt = lax.dot_general(vw[:, sl], do[:, sl], _NT, preferred_element_type=F32)
            dst = (pt * (dpt - dd_ref[hh, 0:1, :])).astype(BF16)
            dqs.append(lax.dot_general(kw[:, sl], dst, _TN, preferred_element_type=F32).T * QSCALE)
            dks.append(jnp.dot(dst, qh, preferred_element_type=F32))
            dvs.append(jnp.dot(pt.astype(BF16), do[:, sl], preferred_element_type=F32))
        dqkv_ref[0, pl.ds(pl.multiple_of(b * DIL_QB, DIL_QB), DIL_QB), :] = jnp.concatenate(dqs, axis=1).astype(dqkv_ref.dtype)
        dk_acc[pl.ds(start, DIL_WIN), :] += jnp.concatenate(dks, axis=1)
        dv_acc[pl.ds(start, DIL_WIN), :] += jnp.concatenate(dvs, axis=1)

        @pl.when(b == DIL_BLOCKS - 1)
        def _():
            dqkv_ref[1] = dk_acc[...].astype(dqkv_ref.dtype)
            dqkv_ref[2] = dv_acc[...].astype(dqkv_ref.dtype)

    qspec = pl.BlockSpec((None, DIL_QB, DIL_LW), lambda g, hp, b: (g, b, hp))
    rspec = pl.BlockSpec((None, DIL_HPS, 8, DIL_QB), lambda g, hp, b: (g, hp, 0, b))
    (dqkv,), sent = _carrier_call(
        "dil_bwd", body, (3, DIL_NLW, DIL_BLOCKS), _dil_in_specs(DIL_HPS) + [qspec, rspec, rspec],
        [pl.BlockSpec((None, 3, SEQ, DIL_LW), lambda g, hp, b: (g, 0, 0, hp))], [jax.ShapeDtypeStruct((3, 3, SEQ, DM), BF16)],
        [pltpu.VMEM((SEQ, DIL_LW), F32), pltpu.VMEM((SEQ, DIL_LW), F32)], (slopes, qkv, qkv, qkv, do, dd, lse), carry)
    return dqkv, sent


def _ffn_fwd(name, x, g_pre, g_post, wgt4, wut4, wd4, carry):
    tm = 512

    def body(x_ref, gpre_ref, gpost_ref, wg_ref, wu_ref, wd_ref, xn_ref, h_ref, gate_ref, up_ref, u_ref, acc):
        s = pl.program_id(1)

        @pl.when(s == 0)
        def _():
            x = x_ref[...]
            r = lax.rsqrt(jnp.mean(x * x, axis=-1, keepdims=True) + RMS_EPS)
            h_ref[...] = (x * r * gpre_ref[...]).astype(h_ref.dtype)

        h = h_ref[...]
        gate = lax.dot_general(h, wg_ref[...], _NT, preferred_element_type=F32).astype(BF16)
        up = lax.dot_general(h, wu_ref[...], _NT, preferred_element_type=F32).astype(BF16)
        gate_ref[...] = gate
        up_ref[...] = up
        gf = gate.astype(F32)
        act = (gf * jax.nn.sigmoid(gf) * up.astype(F32)).astype(BF16)
        part = jnp.dot(act, wd_ref[...], preferred_element_type=F32)

        @pl.when(s == 0)
        def _():
            acc[...] = part

        @pl.when(s > 0)
        def _():
            acc[...] += part

        @pl.when(s == NCHIP - 1)
        def _():
            u = acc[...]
            u_ref[...] = u
            r = lax.rsqrt(jnp.mean(u * u, axis=-1, keepdims=True) + RMS_EPS)
            xn_ref[...] = x_ref[...] + u * r * gpost_ref[...]

    rows = pl.BlockSpec((tm, DM), lambda i, s: (i, 0))
    vec = pl.BlockSpec((1, DM), lambda i, s: (0, 0))
    wspec = _ffn_wspec(lambda i, s: (s, 0, 0))
    mid = pl.BlockSpec((None, tm, FSH), lambda i, s: (s, i, 0))
    outs, sent = _carrier_call(
        name, body, (SEQ // tm, NCHIP), [rows, vec, vec, wspec, wspec, wspec], [rows, rows, mid, mid, rows],
        [jax.ShapeDtypeStruct((SEQ, DM), F32), jax.ShapeDtypeStruct((SEQ, DM), BF16), jax.ShapeDtypeStruct((NCHIP, SEQ, FSH), BF16),
         jax.ShapeDtypeStruct((NCHIP, SEQ, FSH), BF16), jax.ShapeDtypeStruct((SEQ, DM), F32)],
        [pltpu.VMEM((tm, DM), F32)], (x, g_pre, g_post, wgt4, wut4, wd4), carry)
    return outs, sent


def _ffn_block(layer, x, g_pre, g_post, ex):
    tag = f"l{layer}_ffn_fwd"
    (x_new, h, gate, up, u), sent = _ffn_fwd(tag, x, g_pre, g_post, ex.weight(("ffn_w_gate", layer)), ex.weight(("ffn_w_up", layer)),
                                             ex.weight(("ffn_w_down", layer)), ex.carry(tag))
    ex.carried(tag, sent)
    return x_new, (x, h, gate, up, u)


def _ffn_bwd(name, dx, x, gate, up, u, g_pre, g_post, wgt4, wut4, wd4, carry):
    tm = 512

    def body(dx_ref, x_ref, gate_ref, up_ref, u_ref, gpre_ref, gpost_ref, wg_ref, wu_ref, wd_ref,
             dxin_ref, du_ref, dgate_ref, dup_ref, act_ref, dgpre_ref, dgpost_ref, dh_acc):
        i, s = pl.program_id(0), pl.program_id(1)

        @pl.when((i == 0) & (s == 0))
        def _():
            dgpre_ref[...] = jnp.zeros_like(dgpre_ref)
            dgpost_ref[...] = jnp.zeros_like(dgpost_ref)

        @pl.when(s == 0)
        def _():
            dy = dx_ref[...]
            uu = u_ref[...]
            r = lax.rsqrt(jnp.mean(uu * uu, axis=-1, keepdims=True) + RMS_EPS)
            yh = uu * r
            t = dy * gpost_ref[...]
            du_ref[...] = (r * (t - yh * jnp.mean(t * yh, axis=-1, keepdims=True))).astype(du_ref.dtype)
            dgpost_ref[...] += jnp.sum(dy * yh, axis=0, keepdims=True)

        dact = lax.dot_general(du_ref[...], wd_ref[...], _NT, preferred_element_type=F32)
        g = gate_ref[...].astype(F32)
        upv = up_ref[...].astype(F32)
        sg = jax.nn.sigmoid(g)
        dgate = (dact * upv * sg * (1.0 + g * (1.0 - sg))).astype(BF16)
        dup = (dact * g * sg).astype(BF16)
        dgate_ref[...] = dgate
        dup_ref[...] = dup
        act_ref[...] = (g * sg * upv).astype(act_ref.dtype)
        part = jnp.dot(dgate, wg_ref[...], preferred_element_type=F32) + jnp.dot(dup, wu_ref[...], preferred_element_type=F32)

        @pl.when(s == 0)
        def _():
            dh_acc[...] = part

        @pl.when(s > 0)
        def _():
            dh_acc[...] += part

        @pl.when(s == NCHIP - 1)
        def _():
            dh = dh_acc[...]
            xx = x_ref[...]
            r = lax.rsqrt(jnp.mean(xx * xx, axis=-1, keepdims=True) + RMS_EPS)
            yh = xx * r
            t = dh * gpre_ref[...]
            dxin_ref[...] = dx_ref[...] + r * (t - yh * jnp.mean(t * yh, axis=-1, keepdims=True))
            dgpre_ref[...] += jnp.sum(dh * yh, axis=0, keepdims=True)

    rows = pl.BlockSpec((tm, DM), lambda i, s: (i, 0))
    vec = pl.BlockSpec((1, DM), lambda i, s: (0, 0))
    wspec = _ffn_wspec(lambda i, s: (s, 0, 0))
    mid = pl.BlockSpec((None, tm, FSH), lambda i, s: (s, i, 0))
    mid_shape = jax.ShapeDtypeStruct((NCHIP, SEQ, FSH), BF16)
    return _carrier_call(
        name, body, (SEQ // tm, NCHIP), [rows, rows, mid, mid, rows, vec, vec, wspec, wspec, wspec], [rows, rows, mid, mid, mid, vec, vec],
        [jax.ShapeDtypeStruct((SEQ, DM), F32), jax.ShapeDtypeStruct((SEQ, DM), BF16), mid_shape, mid_shape, mid_shape,
         jax.ShapeDtypeStruct((1, DM), F32), jax.ShapeDtypeStruct((1, DM), F32)],
        [pltpu.VMEM((tm, DM), F32)], (dx, x, gate, up, u, g_pre, g_post, wgt4, wut4, wd4), carry)


def _ffn_block_bwd(layer, dx, saved, g_pre, g_post, ex):
    tag = f"l{layer}"
    x, h, gate, up, u = saved
    (dx_in, du, dgate, dup, act, dg_pre, dg_post), sent = _ffn_bwd(
        f"{tag}_ffn_bwd", dx, x, gate, up, u, g_pre, g_post, ex.weight(("ffn_w_gate", layer)), ex.weight(("ffn_w_up", layer)),
        ex.weight(("ffn_w_down", layer)), ex.carry(f"{tag}_ffn_bwd"))
    ex.carried(f"{tag}_ffn_bwd", sent)
    d_wd = _ffn_bwd_dw(f"{tag}_dwd", act, du)
    d_wg = _ffn_bwd_dw(f"{tag}_dwg", dgate, h)
    d_wu = _ffn_bwd_dw(f"{tag}_dwu", dup, h)
    ex.grads(f"{tag}_ffn", {("ffn_w_gate", layer): d_wg, ("ffn_w_up", layer): d_wu, ("ffn_w_down", layer): d_wd})
    return dx_in, dg_pre, dg_post


def _alibi_slopes():
    return 2.0 ** (-8.0 * jnp.arange(1, NH + 1, dtype=F32) / NH)


def _local_step(x, target, norms, rpb, ex):
    g_mix_pre, g_mix_post, g_ffn_pre, g_ffn_post = norms
    row = lambda a, i: a[i:i + 1]

    bias, sent = _na_bias_tiles(rpb, ex.carry("na_bias_tiles"))
    ex.carried("na_bias_tiles", sent)
    h0, h0t = _rms_fwd_both("l0_mix_pre", x, row(g_mix_pre, 0))
    qkv0, sent = _qkv_fwd("l0_qkv", h0[None], ex.weight(("na_w_qkv", 0)), ex.carry("l0_qkv"))
    ex.carried("l0_qkv", sent)
    o0, sent = _na_fwd(qkv0[0], bias, ex.carry("na_fwd"))
    ex.carried("na_fwd", sent)
    na_wo = ex.weight(("na_w_o", 0)).reshape(DM, DM)
    x1, u0 = _proj_fwd("l0_proj", o0, na_wo, x, row(g_mix_post, 0))
    x2, ffn0 = _ffn_block(0, x1, row(g_ffn_pre, 0), row(g_ffn_post, 0), ex)

    slopes = _alibi_slopes()
    h2g, h2gt = _to_groups("l1_h_groups", _rms_fwd("l1_mix_pre", x2, row(g_mix_pre, 1), F32))
    dil_wqkv = ex.weight(("dil_w_qkv", 0))
    qkv1, sent = _qkv_fwd("l1_qkv", h2g, dil_wqkv, ex.carry("l1_qkv"))
    ex.carried("l1_qkv", sent)
    og, lg, sent = _dil_fwd(qkv1, slopes, ex.carry("dil_fwd"))
    ex.carried("dil_fwd", sent)
    o1, lse = _dil_merge(og, lg)
    dil_wo = ex.weight(("dil_w_o", 0)).reshape(DM, DM)
    x3, u1 = _proj_fwd("l1_proj", o1, dil_wo, x2, row(g_mix_post, 1))
    x4, ffn1 = _ffn_block(1, x3, row(g_ffn_pre, 1), row(g_ffn_post, 1), ex)

    dx4, loss_row = _loss_grad("loss", x4, target)

    dx3, dg_fpre1, dg_fpost1 = _ffn_block_bwd(1, dx4, ffn1, row(g_ffn_pre, 1), row(g_ffn_post, 1), ex)
    (do1, du1, dg_mpost1), sent = _proj_bwd("l1_proj_bwd", dx3, u1, row(g_mix_post, 1), dil_wo, F32, ex.carry("l1_proj_bwd"))
    ex.carried("l1_proj_bwd", sent)
    d_dil_wo = _proj_bwd_dw("l1_dwo", o1, du1)
    dog, ddg, lseg = _dil_bwd_prep(do1, o1, lse)
    dqkv1, sent = _dil_bwd(qkv1, dog, ddg, lseg, slopes, ex.carry("dil_bwd"))
    ex.carried("dil_bwd", sent)
    d_dil_wqkv, sent = _qkv_bwd_dw("l1_dwqkv", h2gt, dqkv1, dil_wqkv.shape[2], ex.carry("l1_dwqkv"))
    ex.carried("l1_dwqkv", sent)
    ex.grads("l1_mix", {("dil_w_qkv", 0): d_dil_wqkv, ("dil_w_o", 0): d_dil_wo.reshape(NCHIP, DM // NCHIP, DM)})
    dh2g, sent = _qkv_bwd_dh("l1_dh", dqkv1, dil_wqkv, ex.carry("l1_dh"))
    ex.carried("l1_dh", sent)
    dh2 = _from_groups_sum("l1_dh_tokens", dh2g)
    (dx2, dg_mpre1), sent = _norm_bwd("l1_mix_pre_bwd", dh2, x2, row(g_mix_pre, 1), dx3, ex.carry("l1_mix_pre_bwd"))
    ex.carried("l1_mix_pre_bwd", sent)

    dx1, dg_fpre0, dg_fpost0 = _ffn_block_bwd(0, dx2, ffn0, row(g_ffn_pre, 0), row(g_ffn_post, 0), ex)
    (do0, du0, dg_mpost0), sent = _proj_bwd("l0_proj_bwd", dx1, u0, row(g_mix_post, 0), na_wo, BF16, ex.carry("l0_proj_bwd"))
    ex.carried("l0_proj_bwd", sent)
    d_na_wo = _proj_bwd_dw("l0_dwo", o0, du0)
    dqkv0, z, sent = _na_bwd(qkv0[0], bias, do0, ex.carry("na_bwd"))
    ex.carried("na_bwd", sent)
    d_rpb = _rpb_grad(z)
    na_wqkv = ex.weight(("na_w_qkv", 0))
    d_na_wqkv, sent = _qkv_bwd_dw("l0_dwqkv", h0t[None], dqkv0[None], na_wqkv.shape[2], ex.carry("l0_dwqkv"))
    ex.carried("l0_dwqkv", sent)
    ex.grads("l0_mix", {("na_w_qkv", 0): d_na_wqkv, ("na_w_o", 0): d_na_wo.reshape(NCHIP, DM // NCHIP, DM)})
    dh0, sent = _qkv_bwd_dh("l0_dh", dqkv0[None], na_wqkv, ex.carry("l0_dh"))
    ex.carried("l0_dh", sent)
    (dx0, dg_mpre0), sent = _norm_bwd("l0_mix_pre_bwd", dh0[0], x, row(g_mix_pre, 0), dx1, ex.carry("l0_mix_pre_bwd"))
    ex.carried("l0_mix_pre_bwd", sent)

    dnorms = (jnp.concatenate([dg_mpre0, dg_mpre1]), jnp.concatenate([dg_mpost0, dg_mpost1]),
              jnp.concatenate([dg_fpre0, dg_fpre1]), jnp.concatenate([dg_fpost0, dg_fpost1]))
    return loss_row, dx0, dnorms, d_rpb


def _place():
    x, y, c = lax.axis_index("x"), lax.axis_index("y"), lax.axis_index("c")
    chips = ((1 - x, y), (x, 1 - y), (1 - x, 1 - y))
    return x, y, c, chips


def _chip_id(chip):
    return 2 * chip[0] + chip[1]


def _gather_copies(shards):
    n = len(shards)

    def copies(src, out, sems):
        send_sems, recv_sems = sems
        x, y, c, chips = _place()

        def copy(t, k, chip, half, to, from_src=False):
            blk = out[t].at[_chip_id(chip), half]
            return pltpu.make_async_remote_copy(
                src_ref=src[t].at[half] if from_src else blk, dst_ref=blk,
                send_sem=send_sems.at[6 * t + k], recv_sem=recv_sems.at[6 * t + k], device_id=to, device_id_type=MESH)

        return copy, x, y, c, chips

    def issue(src, out, sems):
        copy, x, y, c, chips = copies(src, out, sems)
        for t in range(n):
            for j, chip in enumerate(chips):
                copy(t, j, (x, y), c, (*chip, c), from_src=True).start()

    def drain(src, out, sems):
        copy, x, y, c, chips = copies(src, out, sems)
        passed = []
        for t in range(n):
            for j, chip in enumerate(chips):
                copy(t, j, chip, c, (x, y, c)).wait_recv()
                fwd = copy(t, 3 + j, chip, c, (x, y, 1 - c))
                fwd.start()
                passed.append(fwd)
        for t in range(n):
            for j, chip in enumerate(chips):
                copy(t, 3 + j, chip, 1 - c, (x, y, c)).wait_recv()
        for t in range(n):
            for j, chip in enumerate(chips):
                copy(t, j, (x, y), c, (*chip, c), from_src=True).wait_send()
        for cp in passed:
            cp.wait_send()

    return _Carried(shards, [jax.ShapeDtypeStruct((NCHIP,) + s.shape, s.dtype) for s in shards], (6 * n, 6 * n), issue, drain)


def _pair_exchange_copies(grads):
    n = len(grads)

    def copies(g, theirs, sems):
        send_sems, recv_sems = sems
        x, y, c, _ = _place()
        return [pltpu.make_async_remote_copy(src_ref=g[t].at[:, 1 - c], dst_ref=theirs[t], send_sem=send_sems.at[t],
                                             recv_sem=recv_sems.at[t], device_id=(x, y, 1 - c), device_id_type=MESH) for t in range(n)]

    def issue(g, theirs, sems):
        for cp in copies(g, theirs, sems):
            cp.start()

    def drain(g, theirs, sems):
        for cp in copies(g, theirs, sems):
            cp.wait()

    return _Carried(grads, [jax.ShapeDtypeStruct((NCHIP,) + g.shape[2:], g.dtype) for g in grads], (n, n), issue, drain)


def _chip_exchange_copies(items):
    flat = [(t, i, j) for t, (_, peers) in enumerate(items) for i, j in enumerate(peers)]

    def copies(p, slots, sems):
        send_sems, recv_sems = sems
        x, y, c, chips = _place()
        return [pltpu.make_async_remote_copy(src_ref=p[t].at[_chip_id(chips[j])], dst_ref=slots[t].at[i], send_sem=send_sems.at[k],
                                             recv_sem=recv_sems.at[k], device_id=(*chips[j], c), device_id_type=MESH)
                for k, (t, i, j) in enumerate(flat)]

    def issue(p, slots, sems):
        for cp in copies(p, slots, sems):
            cp.start()

    def drain(p, slots, sems):
        for cp in copies(p, slots, sems):
            cp.wait()

    return _Carried([p for p, _ in items], [jax.ShapeDtypeStruct((len(peers),) + p.shape[1:], p.dtype) for p, peers in items],
                    (len(flat), len(flat)), issue, drain)


def _pair_share_copies(halves):
    n = len(halves)

    def copies(h, other, sems):
        send_sems, recv_sems = sems
        x, y, c, _ = _place()
        return [pltpu.make_async_remote_copy(src_ref=h[t], dst_ref=other[t], send_sem=send_sems.at[t], recv_sem=recv_sems.at[t],
                                             device_id=(x, y, 1 - c), device_id_type=MESH) for t in range(n)]

    def issue(h, other, sems):
        for cp in copies(h, other, sems):
            cp.start()

    def drain(h, other, sems):
        for cp in copies(h, other, sems):
            cp.wait()

    return _Carried(halves, [jax.ShapeDtypeStruct(h.shape, h.dtype) for h in halves], (n, n), issue, drain)


SMALL_ROWS = 128


def _allreduce_small(v, carry):
    ci, co = len(carry.ins), len(carry.out_shapes)

    def body(*refs):
        v_ref, cins, o_ref, couts = refs[0], refs[1:1 + ci], refs[1 + ci], refs[2 + ci:2 + ci + co]
        buf, send_sems, recv_sems = refs[2 + ci + co:5 + ci + co]
        csems = refs[5 + ci + co:]
        carry.issue(cins, couts, csems)
        x, y, c, _ = _place()
        me = 4 * x + 2 * y + c
        flip = lambda a, f: 1 - a if f else a
        buf[me] = v_ref[...]
        peers = [(flip(x, d >> 2 & 1), flip(y, d >> 1 & 1), flip(c, d & 1)) for d in range(1, 8)]
        sends = [pltpu.make_async_remote_copy(src_ref=v_ref, dst_ref=buf.at[me], send_sem=send_sems.at[i], recv_sem=recv_sems.at[i],
                                              device_id=peer, device_id_type=MESH) for i, peer in enumerate(peers)]
        for cp in sends:
            cp.start()
        for i, (px, py, pc) in enumerate(peers):
            pltpu.make_async_remote_copy(src_ref=v_ref, dst_ref=buf.at[4 * px + 2 * py + pc], send_sem=send_sems.at[i], recv_sem=recv_sems.at[i],
                                         device_id=(px, py, pc), device_id_type=MESH).wait_recv()
        for cp in sends:
            cp.wait_send()
        acc = buf[0]
        for k in range(1, 8):
            acc = acc + buf[k]
        o_ref[...] = acc
        carry.drain(cins, couts, csems)

    vm = pl.BlockSpec(memory_space=pltpu.VMEM)
    res = pl.pallas_call(
        body, in_specs=[vm] + [HBM_SPEC] * ci, out_specs=[vm] + [HBM_SPEC] * co,
        out_shape=[jax.ShapeDtypeStruct((SMALL_ROWS, 128), F32)] + carry.out_shapes,
        scratch_shapes=[pltpu.VMEM((8, SMALL_ROWS, 128), F32), pltpu.SemaphoreType.DMA((7,)), pltpu.SemaphoreType.DMA((7,))]
        + [pltpu.SemaphoreType.DMA((k,)) for k in carry.n_sems],
        compiler_params=pltpu.CompilerParams(has_side_effects=True), name="allreduce_small")(v, *carry.ins)
    return res[0], list(res[1:])


def _row_block(rows, cols, budget=3 << 19):
    best = 8
    for bm in range(8, rows + 1, 8):
        if rows % bm == 0 and bm * cols * 4 <= budget:
            best = bm
    return best


def _pair_sum(name, place, gs, theirs):
    n = len(gs)
    _, m, c = theirs[0].shape
    bm = _row_block(m, c)

    def body(place_ref, *refs):
        for a_ref, b_ref, o_ref in zip(refs[:n], refs[n:2 * n], refs[2 * n:]):
            o_ref[...] = (a_ref[...].astype(F32) + b_ref[...].astype(F32)).astype(o_ref.dtype)

    spec = pl.BlockSpec((None, bm, c), lambda k, i, pr: (k, i, 0))
    return pl.pallas_call(
        body, out_shape=[jax.ShapeDtypeStruct(theirs[0].shape, BF16)] * n,
        grid_spec=pltpu.PrefetchScalarGridSpec(
            num_scalar_prefetch=1, grid=(NCHIP, m // bm),
            in_specs=[pl.BlockSpec((None, None, bm, c), lambda k, i, pr: (k, pr[0], i, 0))] * n + [spec] * n, out_specs=[spec] * n),
        compiler_params=_params(("parallel", "parallel")), name=name)(place, *gs, *theirs)


def _chip_sum(name, place, parts, slots):
    n, ns = len(parts), len(slots[0])
    _, m, c = parts[0].shape
    bm = _row_block(m, c)

    def body(place_ref, *refs):
        for t in range(n):
            acc = refs[t][...].astype(F32)
            for s_ref in refs[n + t * ns:n + (t + 1) * ns]:
                for i in range(s_ref.shape[0]):
                    acc = acc + s_ref[i].astype(F32)
            refs[n + n * ns + t][...] = acc

    half = pl.BlockSpec((bm, c), lambda i, pr: (i, 0))
    return pl.pallas_call(
        body, out_shape=[jax.ShapeDtypeStruct((m, c), F32)] * n,
        grid_spec=pltpu.PrefetchScalarGridSpec(
            num_scalar_prefetch=1, grid=(m // bm,),
            in_specs=[pl.BlockSpec((None, bm, c), lambda i, pr: (pr[1], i, 0))] * n
            + [pl.BlockSpec((s.shape[0], bm, c), lambda i, pr: (0, i, 0)) for group in slots for s in group],
            out_specs=[half] * n),
        compiler_params=_params(("parallel",)), name=name)(place, *parts, *[s for group in slots for s in group])


def _adamw(name, place, tensors, layer=0, into=None):
    n = len(tensors)
    lead, rows, cols = tensors[0][0].shape
    bm = _row_block(rows // 2, cols, budget=768 * 1024 // n)
    per_half = rows // 2 // bm
    c1 = 1.0 - ADAM_B1 ** ADAM_STEP
    c2 = 1.0 - ADAM_B2 ** ADAM_STEP

    def body(place_ref, *refs):
        outs = refs[len(refs) - 4 * n:]
        for t in range(n):
            w_ref, ga_ref, gb_ref, m_ref, v_ref = refs[5 * t:5 * t + 5]
            go_ref, d_ref, mo_ref, vo_ref = outs[4 * t:4 * t + 4]
            g = jnp.where(pl.program_id(0) // per_half == place_ref[0], ga_ref[...], gb_ref[...])
            mn = ADAM_B1 * m_ref[...] + (1.0 - ADAM_B1) * g
            vn = ADAM_B2 * v_ref[...] + (1.0 - ADAM_B2) * (g * g)
            go_ref[...] = g
            mo_ref[...] = mn
            vo_ref[...] = vn
            d_ref[...] = -ADAM_LR * ((mn / c1) / (jnp.sqrt(vn / c2) + ADAM_EPS) + ADAM_WD * w_ref[...])

    spec = pl.BlockSpec((None, bm, cols), lambda i, pr: (layer, i, 0))

    def half_spec(mine):
        def index(i, pr):
            first = (pr[0] == 0) == mine
            park = jnp.where(first, per_half - 1, 0)
            return jnp.where((i < per_half) == first, i % per_half, park), 0
        return pl.BlockSpec((bm, cols), index)
    sh = jax.ShapeDtypeStruct((lead, rows, cols), F32)
    prev = [] if into is None else [a for res in into for a in res]
    res = pl.pallas_call(
        body, out_shape=[sh] * (4 * n), input_output_aliases={1 + 5 * n + k: k for k in range(len(prev))},
        grid_spec=pltpu.PrefetchScalarGridSpec(
            num_scalar_prefetch=1, grid=(rows // bm,),
            in_specs=[spec, half_spec(True), half_spec(False), spec, spec] * n + [pl.BlockSpec(memory_space=pl.ANY)] * len(prev),
            out_specs=[spec] * (4 * n)),
        compiler_params=_params(("parallel",)), name=name)(place, *[a for t in tensors for a in t], *prev)
    return [res[4 * t:4 * t + 4] for t in range(n)]


def _pack_small(norms, rpb, last=None):
    flat = jnp.concatenate([a.reshape(-1) for a in norms] + [rpb.reshape(-1)])
    flat = jnp.pad(flat, (0, SMALL_ROWS * 128 - flat.shape[0]))
    if last is not None:
        flat = lax.dynamic_update_slice(flat, last.reshape(1), (flat.shape[0] - 1,))
    return flat.reshape(SMALL_ROWS, 128)


def _unpack_small(p):
    flat = p.reshape(-1)
    norms = [flat[i * 2 * DM:(i + 1) * 2 * DM].reshape(2, DM) for i in range(4)]
    rpb = flat[8 * DM:8 * DM + NH * 15 * 31].reshape(1, NH, 15, 31)
    return norms, rpb


FFN_NAMES = ("ffn_w_gate", "ffn_w_up", "ffn_w_down")
L0_FFN = tuple((n, 0) for n in FFN_NAMES)
L1_FFN = tuple((n, 1) for n in FFN_NAMES)
NA_KEYS = (("na_w_qkv", 0), ("na_w_o", 0))
DIL_KEYS = (("dil_w_qkv", 0), ("dil_w_o", 0))
ALL_PEERS, NEIGHBOURS, DIAGONAL = (0, 1, 2), (0, 1), (2,)


class _Exchange:
    GATHERS = {"na_bias_tiles": NA_KEYS, "l0_qkv": L0_FFN[:1], "na_fwd": L0_FFN[1:], "l0_ffn_fwd": DIL_KEYS[:1], "dil_fwd": L1_FFN + DIL_KEYS[1:]}
    PAIRS = {"l1_proj_bwd": L1_FFN, "l1_dh": DIL_KEYS, "l0_proj_bwd": L0_FFN}
    EXCHANGES = {"dil_bwd": [(k, ALL_PEERS) for k in L1_FFN],
                 "l0_ffn_bwd": [(DIL_KEYS[0], NEIGHBOURS), (DIL_KEYS[1], ALL_PEERS)],
                 "na_bwd": [(k, ALL_PEERS) for k in L0_FFN] + [(DIL_KEYS[0], DIAGONAL)],
                 "l0_dh": [(k, NEIGHBOURS) for k in NA_KEYS],
                 "allreduce_small": [(k, DIAGONAL) for k in NA_KEYS]}
    SHARES = {"l1_dwqkv": L1_FFN, "l0_dwqkv": L0_FFN + DIL_KEYS}

    def __init__(self, shards):
        self.chip = 2 * lax.axis_index("x") + lax.axis_index("y")
        self.place = jnp.stack([lax.axis_index("c"), self.chip]).astype(jnp.int32)
        self.own = {k: s.reshape(2, s.shape[0] // 2, s.shape[1]).astype(BF16) for k, s in shards.items()}
        self.gathered, self.mine, self.parts, self.slots, self.full, self.other = {}, {}, {}, {}, {}, {}

    def _take(self, keys, landed):
        for k, gw in zip(keys, landed):
            self.gathered[k] = lax.dynamic_update_slice(gw, self.own[k][None], (self.chip, 0, 0, 0))

    def _sum(self, items, landed):
        runs = []
        for (k, peers), s in zip(items, landed):
            got = self.slots.setdefault(k, {})
            got[peers] = s
            if sum(len(p) for p in got) == len(ALL_PEERS):
                like = (self.parts[k].shape, tuple(sorted(got)))
                if runs and runs[-1][0] == like:
                    runs[-1][1].append(k)
                else:
                    runs.append((like, [k]))
        for (_, split), ks in runs:
            sums = _chip_sum(f"chip_sum_{ks[0][0]}_{ks[0][1]}", self.place, [self.parts[k] for k in ks],
                             [[self.slots[k][p] for p in split] for k in ks])
            self.full.update(zip(ks, sums))

    def weight(self, key):
        g = self.gathered[key]
        return g.reshape(NCHIP, 2 * g.shape[2], g.shape[3])

    def _pair_sums(self, keys, theirs):
        runs = []
        for k, t in zip(keys, theirs):
            if runs and runs[-1][0][1].shape == t.shape:
                runs[-1].append((k, t))
            else:
                runs.append([(k, t)])
        for run in runs:
            ks = [k for k, _ in run]
            sums = _pair_sum(f"pair_sum_{ks[0][0]}_{ks[0][1]}", self.place, [self.mine[k] for k in ks], [t for _, t in run])
            self.parts.update(zip(ks, sums))

    def carry(self, tag):
        if tag in self.GATHERS:
            return _gather_copies([self.own[k] for k in self.GATHERS[tag]])
        if tag in self.PAIRS:
            return _pair_exchange_copies([self.mine[k] for k in self.PAIRS[tag]])
        if tag in self.EXCHANGES:
            return _chip_exchange_copies([(self.parts[k], peers) for k, peers in self.EXCHANGES[tag]])
        if tag in self.SHARES:
            return _pair_share_copies([self.full[k] for k in self.SHARES[tag]])
        return None

    def carried(self, tag, landed):
        if tag in self.GATHERS:
            self._take(self.GATHERS[tag], landed)
        elif tag in self.PAIRS:
            self._pair_sums(self.PAIRS[tag], landed)
        elif tag in self.EXCHANGES:
            self._sum(self.EXCHANGES[tag], landed)
        elif tag in self.SHARES:
            self.other.update(zip(self.SHARES[tag], landed))

    def grads(self, tag, dw):
        for k, g in dw.items():
            self.mine[k] = g.reshape(NCHIP, 2, -1, g.shape[-1])
        if tag == "l0_mix":
            keys = tuple(dw)
            self._pair_sums(keys, _run_carried("grad_pair_exchange_last", _pair_exchange_copies([self.mine[k] for k in keys])))

    def finish(self):
        rest = tuple(k for k in self.full if k not in self.other)
        self.other.update(zip(rest, _run_carried("grad_pair_share_last", _pair_share_copies([self.full[k] for k in rest]))))
        return {k: (self.full[k], self.other[k]) for k in self.full}


def kernel(x, norm_mix_pre, norm_mix_post, norm_ffn_pre, norm_ffn_post, na_w_qkv, na_w_o, na_rpb, dil_w_qkv, dil_w_o, ffn_w_gate, ffn_w_up, ffn_w_down, loss_target, m_norm_mix_pre, m_norm_mix_post, m_norm_ffn_pre, m_norm_ffn_post, m_na_w_qkv, m_na_w_o, m_na_rpb, m_dil_w_qkv, m_dil_w_o, m_ffn_w_gate, m_ffn_w_up, m_ffn_w_down, v_norm_mix_pre, v_norm_mix_post, v_norm_ffn_pre, v_norm_ffn_post, v_na_w_qkv, v_na_w_o, v_na_rpb, v_dil_w_qkv, v_dil_w_o, v_ffn_w_gate, v_ffn_w_up, v_ffn_w_down):
    tr = lambda a: jnp.swapaxes(a, 1, 2)
    weights = {"na_w_qkv": na_w_qkv, "na_w_o": na_w_o, "dil_w_qkv": dil_w_qkv, "dil_w_o": dil_w_o,
               "ffn_w_gate": tr(ffn_w_gate), "ffn_w_up": tr(ffn_w_up), "ffn_w_down": ffn_w_down}
    m_in = {"na_w_qkv": m_na_w_qkv, "na_w_o": m_na_w_o, "dil_w_qkv": m_dil_w_qkv, "dil_w_o": m_dil_w_o,
            "ffn_w_gate": tr(m_ffn_w_gate), "ffn_w_up": tr(m_ffn_w_up), "ffn_w_down": m_ffn_w_down}
    v_in = {"na_w_qkv": v_na_w_qkv, "na_w_o": v_na_w_o, "dil_w_qkv": v_dil_w_qkv, "dil_w_o": v_dil_w_o,
            "ffn_w_gate": tr(v_ffn_w_gate), "ffn_w_up": tr(v_ffn_w_up), "ffn_w_down": v_ffn_w_down}

    ex = _Exchange({(n, l): weights[n][l] for n in weights for l in range(weights[n].shape[0])})
    norms = (norm_mix_pre, norm_mix_post, norm_ffn_pre, norm_ffn_post)
    loss_row, dx, dnorms, d_rpb = _local_step(x[0], loss_target[0], norms, na_rpb[0], ex)
    small, sent = _allreduce_small(_pack_small(dnorms, d_rpb, last=loss_row[0, 0]), ex.carry("allreduce_small"))
    ex.carried("allreduce_small", sent)
    full = ex.finish()
    loss = small[SMALL_ROWS - 1, 127]

    out_g, out_d, out_m, out_v = {}, {}, {}, {}
    operands = lambda n, l: (weights[n], *full[(n, l)], m_in[n], v_in[n])
    results = {n: _adamw(f"adamw_{n}", ex.place, [operands(n, 0)])[0] for n in weights if n not in FFN_NAMES}
    ffn = None
    for l in range(2):
        ffn = _adamw(f"adamw_ffn_{l}", ex.place, [operands(n, l) for n in FFN_NAMES], l, ffn)
    results.update(zip(FFN_NAMES, ffn))
    for n, res in results.items():
        if n in ("ffn_w_gate", "ffn_w_up"):
            res = [tr(r) for r in res]
        out_g[n], out_d[n], out_m[n], out_v[n] = res
    sm_names = ("norm_mix_pre", "norm_mix_post", "norm_ffn_pre", "norm_ffn_post", "na_rpb")
    sm = _adamw("adamw_small", jnp.zeros((2,), jnp.int32),
                [(_pack_small(norms, na_rpb)[None], small[:SMALL_ROWS // 2], small[SMALL_ROWS // 2:],
                  _pack_small((m_norm_mix_pre, m_norm_mix_post, m_norm_ffn_pre, m_norm_ffn_post), m_na_rpb)[None],
                  _pack_small((v_norm_mix_pre, v_norm_mix_post, v_norm_ffn_pre, v_norm_ffn_post), v_na_rpb)[None])])[0]
    for res, dst in zip(sm, (out_g, out_d, out_m, out_v)):
        ns, rp = _unpack_small(res)
        for n, a in zip(sm_names, ns + [rp]):
            dst[n] = a

    order = ("norm_mix_pre", "norm_mix_post", "norm_ffn_pre", "norm_ffn_post", "na_w_qkv", "na_w_o", "na_rpb", "dil_w_qkv", "dil_w_o",
             "ffn_w_gate", "ffn_w_up", "ffn_w_down")
    return (loss, dx[None], *[out_g[n] for n in order], *[out_d[n] for n in order], *[out_m[n] for n in order], *[out_v[n] for n in order])
```

```python
import functools

import numpy as np
import jax
import jax.numpy as jnp
from jax import lax
from jax.experimental import pallas as pl
from jax.experimental.pallas import tpu as pltpu

F32 = jnp.float32
BF16 = jnp.bfloat16

SEQ = 2048
DM = 1024
NH = 16
HD = 64
DFF = 2816
NCHIP = 4
FSH = DFF // NCHIP
GRID_W = 64
NA_QROWS = 4
NA_QB = NA_QROWS * GRID_W
NA_WROWS = 12
NA_WIN = NA_WROWS * GRID_W
DIL = (1, 4, 16)
DIL_QB = 256
DIL_WIN = DIL_QB + 128
DIL_RADIUS = 64
RMS_EPS = 1e-6
NEG = -1e30
QSCALE = HD ** -0.5
CH = 256
MESH = pl.DeviceIdType.MESH

ADAM_LR, ADAM_B1, ADAM_B2, ADAM_EPS, ADAM_WD, ADAM_STEP = 0.001, 0.9, 0.999, 1e-08, 0.01, 10

VMEM_LIMIT = 56 * 1024 * 1024

_NN = (((1,), (0,)), ((), ()))
_NT = (((1,), (1,)), ((), ()))
_TN = (((0,), (0,)), ((), ()))


def _params(sem):
    return pltpu.CompilerParams(dimension_semantics=sem, vmem_limit_bytes=VMEM_LIMIT)


def _matmul(name, pairs, grid, out_shape, out_spec, acc_shape, carrying=False, carry=None):
    nk = grid[-1]
    npair = len(pairs)
    n_in = 2 * npair

    def body(*refs):
        ins, o_ref = refs[:2 * npair], refs[n_in]
        part = None
        for p in range(npair):
            d = lax.dot_general(ins[2 * p][...].astype(BF16), ins[2 * p + 1][...].astype(BF16), pairs[p][4],
                                preferred_element_type=F32)
            part = d if part is None else part + d
        if nk == 1:
            o_ref[...] = part.astype(o_ref.dtype)
        else:
            acc_ref = refs[n_in + 1]
            kk = pl.program_id(len(grid) - 1)

            @pl.when(kk == 0)
            def _():
                acc_ref[...] = part

            @pl.when(kk > 0)
            def _():
                acc_ref[...] += part

            @pl.when(kk == nk - 1)
            def _():
                o_ref[...] = acc_ref[...].astype(o_ref.dtype)

    ops, specs = [], []
    for a, a_spec, b, b_spec, _ in pairs:
        ops += [a, b]
        specs += [a_spec, b_spec]
    (out,), sent = _carrier_call(name, body, grid, specs, [out_spec], [out_shape], [] if nk == 1 else [pltpu.VMEM(acc_shape, F32)], ops, carry)
    return (out, sent) if carrying else out


def _qkv_fwd(name, h_all, w4, carry):
    g_n = h_all.shape[0]
    per = w4.shape[2] // CH
    return _matmul(
        name, [(h_all, pl.BlockSpec((None, SEQ, DM), lambda g, q, k: (g, 0, 0)),
                w4, pl.BlockSpec((None, DM, CH), lambda g, q, k: ((g * 12 + q) // per, 0, (g * 12 + q) % per)), _NN)],
        (g_n, 12, 1), jax.ShapeDtypeStruct((g_n, SEQ, 3 * DM), BF16),
        pl.BlockSpec((None, SEQ, CH), lambda g, q, k: (g, 0, q)), None, carrying=True, carry=carry)


def _qkv_bwd_dh(name, dqkv, w4, carry):
    g_n = dqkv.shape[0]
    per = w4.shape[2] // CH
    tm = SEQ

    def pair(cb):
        chunk = lambda g, t: g * 12 + t * 4 + cb
        return (dqkv, pl.BlockSpec((None, None, tm, CH), lambda g, i, t: (g, t, i, cb)),
                w4, pl.BlockSpec((None, DM, CH), lambda g, i, t: (chunk(g, t) // per, 0, chunk(g, t) % per)), _NT)

    return _matmul(name, [pair(cb) for cb in range(4)], (g_n, SEQ // tm, 3), jax.ShapeDtypeStruct((g_n, SEQ, DM), F32),
                   pl.BlockSpec((None, tm, DM), lambda g, i, t: (g, i, 0)), (tm, DM), carrying=True, carry=carry)


def _qkv_bwd_dw(name, ht_all, dqkv, shard_cols, carry):
    g_n = dqkv.shape[0]
    per = shard_cols // CH
    return _matmul(
        name, [(ht_all, pl.BlockSpec((None, DM, SEQ), lambda qq, k: (qq // 12, 0, 0)),
                dqkv, pl.BlockSpec((None, None, SEQ, CH), lambda qq, k: (qq // 12, (qq % 12) // 4, 0, qq % 4)), _NN)],
        (g_n * 12, 1), jax.ShapeDtypeStruct((NCHIP, DM, shard_cols), BF16),
        pl.BlockSpec((None, DM, CH), lambda qq, k: (qq // per, 0, qq % per)), None, carrying=True, carry=carry)


def _proj_fwd(name, o, wo, x, g):
    tm = 512

    def body(o_ref, w_ref, x_ref, g_ref, xn_ref, u_ref):
        u = jnp.dot(o_ref[...], w_ref[...], preferred_element_type=F32)
        u_ref[...] = u
        r = lax.rsqrt(jnp.mean(u * u, axis=-1, keepdims=True) + RMS_EPS)
        xn_ref[...] = x_ref[...] + u * r * g_ref[...]

    rows = pl.BlockSpec((tm, DM), lambda i: (i, 0))
    sh = jax.ShapeDtypeStruct((SEQ, DM), F32)
    return pl.pallas_call(
        body, grid=(SEQ // tm,), in_specs=[rows, pl.BlockSpec((DM, DM), lambda i: (0, 0)), rows, pl.BlockSpec((1, DM), lambda i: (0, 0))],
        out_specs=[rows, rows], out_shape=[sh, sh], compiler_params=_params(("parallel",)), name=name)(o, wo, x, g)


def _proj_bwd(name, dy, u, g, wo, dtype, carry):
    tm = 512

    def body(dy_ref, u_ref, g_ref, w_ref, do_ref, du_ref, dg_ref):
        dy = dy_ref[...]
        u = u_ref[...]
        r = lax.rsqrt(jnp.mean(u * u, axis=-1, keepdims=True) + RMS_EPS)
        yh = u * r
        t = dy * g_ref[...]
        du = (r * (t - yh * jnp.mean(t * yh, axis=-1, keepdims=True))).astype(BF16)
        du_ref[...] = du
        do_ref[...] = lax.dot_general(du, w_ref[...], _NT, preferred_element_type=F32).astype(do_ref.dtype)

        @pl.when(pl.program_id(0) == 0)
        def _():
            dg_ref[...] = jnp.zeros_like(dg_ref)

        dg_ref[...] += jnp.sum(dy * yh, axis=0, keepdims=True)

    rows = pl.BlockSpec((tm, DM), lambda i: (i, 0))
    vec = pl.BlockSpec((1, DM), lambda i: (0, 0))
    return _carrier_call(
        name, body, (SEQ // tm,), [rows, rows, vec, pl.BlockSpec((DM, DM), lambda i: (0, 0))], [rows, rows, vec],
        [jax.ShapeDtypeStruct((SEQ, DM), dtype), jax.ShapeDtypeStruct((SEQ, DM), BF16), jax.ShapeDtypeStruct((1, DM), F32)],
        [], (dy, u, g, wo), carry)


def _proj_bwd_dw(name, o, du):
    tn = 512
    return _matmul(
        name, [(o, pl.BlockSpec((SEQ, DM), lambda j, k: (0, 0)), du, pl.BlockSpec((SEQ, tn), lambda j, k: (0, j)), _TN)],
        (DM // tn, 1), jax.ShapeDtypeStruct((DM, DM), BF16), pl.BlockSpec((DM, tn), lambda j, k: (0, j)), None)


def _ffn_wspec(index_map):
    return pl.BlockSpec((None, FSH, DM), index_map)


def _ffn_bwd_dw(name, a4, b):
    return _matmul(
        name, [(a4, pl.BlockSpec((None, SEQ, FSH), lambda s, k: (s, 0, 0)), b, pl.BlockSpec((SEQ, DM), lambda s, k: (0, 0)), _TN)],
        (NCHIP, 1), jax.ShapeDtypeStruct((NCHIP, FSH, DM), BF16), _ffn_wspec(lambda s, k: (s, 0, 0)), None)


ROWS = 256


def _row_spec():
    return pl.BlockSpec((ROWS, DM), lambda i: (i, 0))


def _vec_spec():
    return pl.BlockSpec((1, DM), lambda i: (0, 0))


def _rms_fwd(name, x, g, dtype=BF16):
    def body(x_ref, g_ref, o_ref):
        x = x_ref[...]
        r = lax.rsqrt(jnp.mean(x * x, axis=-1, keepdims=True) + RMS_EPS)
        o_ref[...] = (x * r * g_ref[...]).astype(o_ref.dtype)

    return pl.pallas_call(body, grid=(SEQ // ROWS,), in_specs=[_row_spec(), _vec_spec()], out_specs=_row_spec(),
                          out_shape=jax.ShapeDtypeStruct((SEQ, DM), dtype), compiler_params=_params(("parallel",)), name=name)(x, g)


def _rms_fwd_both(name, x, g):
    def body(x_ref, g_ref, o_ref, t_ref):
        x = x_ref[...]
        r = lax.rsqrt(jnp.mean(x * x, axis=-1, keepdims=True) + RMS_EPS)
        h = x * r * g_ref[...]
        o_ref[...] = h.astype(o_ref.dtype)
        t_ref[...] = h.T.astype(t_ref.dtype)

    return pl.pallas_call(
        body, grid=(SEQ // ROWS,), in_specs=[_row_spec(), _vec_spec()], out_specs=[_row_spec(), pl.BlockSpec((DM, ROWS), lambda i: (0, i))],
        out_shape=[jax.ShapeDtypeStruct((SEQ, DM), BF16), jax.ShapeDtypeStruct((DM, SEQ), BF16)],
        compiler_params=_params(("parallel",)), name=name)(x, g)


def _norm_bwd(name, dy, u, g, res, carry):
    def body(dy_ref, u_ref, g_ref, res_ref, du_ref, dg_ref):
        dy = dy_ref[...]
        u = u_ref[...]
        r = lax.rsqrt(jnp.mean(u * u, axis=-1, keepdims=True) + RMS_EPS)
        yh = u * r
        t = dy * g_ref[...]
        du_ref[...] = r * (t - yh * jnp.mean(t * yh, axis=-1, keepdims=True)) + res_ref[...]

        @pl.when(pl.program_id(0) == 0)
        def _():
            dg_ref[...] = jnp.zeros_like(dg_ref)

        dg_ref[...] += jnp.sum(dy * yh, axis=0, keepdims=True)

    return _carrier_call(
        name, body, (SEQ // ROWS,), [_row_spec(), _row_spec(), _vec_spec(), _row_spec()], [_row_spec(), _vec_spec()],
        [jax.ShapeDtypeStruct((SEQ, DM), F32), jax.ShapeDtypeStruct((1, DM), F32)], [], (dy, u, g, res), carry)


def _loss_grad(name, y, t):
    def body(y_ref, t_ref, dy_ref, l_ref):
        e = y_ref[...] - t_ref[...]
        dy_ref[...] = e * (1.0 / DM)

        @pl.when(pl.program_id(0) == 0)
        def _():
            l_ref[...] = jnp.zeros_like(l_ref)

        l_ref[...] += jnp.sum(e * e) * (0.5 / DM)

    return pl.pallas_call(
        body, grid=(SEQ // ROWS,), in_specs=[_row_spec(), _row_spec()],
        out_specs=[_row_spec(), pl.BlockSpec((1, 128), lambda i: (0, 0))],
        out_shape=[jax.ShapeDtypeStruct((SEQ, DM), F32), jax.ShapeDtypeStruct((1, 128), F32)],
        compiler_params=_params(("arbitrary",)), name=name)(y, t)


HBM_SPEC = pl.BlockSpec(memory_space=pltpu.HBM)


class _Carried:
    def __init__(self, ins, out_shapes, n_sems, issue, drain):
        self.ins, self.out_shapes, self.n_sems, self.issue, self.drain = list(ins), list(out_shapes), tuple(n_sems), issue, drain


def _carrier_call(name, body, grid, in_specs, out_specs, out_shape, scratch_shapes, operands, carry):
    n_in, n_out, n_scr = len(in_specs), len(out_specs), len(scratch_shapes)
    if carry is None:
        res = pl.pallas_call(body, grid=grid, in_specs=in_specs, out_specs=out_specs, out_shape=out_shape, scratch_shapes=scratch_shapes,
                             compiler_params=_params(("arbitrary",) * len(grid)), name=name)(*operands)
        return list(res), []
    ci, co = len(carry.ins), len(carry.out_shapes)

    def wrapped(*refs):
        ins, cins = refs[:n_in], refs[n_in:n_in + ci]
        outs, couts = refs[n_in + ci:n_in + ci + n_out], refs[n_in + ci + n_out:n_in + ci + n_out + co]
        scr, sems = refs[n_in + ci + n_out + co:n_in + ci + n_out + co + n_scr], refs[n_in + ci + n_out + co + n_scr:]
        first = functools.reduce(jnp.logical_and, [pl.program_id(a) == 0 for a in range(len(grid))])
        last = functools.reduce(jnp.logical_and, [pl.program_id(a) == grid[a] - 1 for a in range(len(grid))])

        @pl.when(first)
        def _():
            carry.issue(cins, couts, sems)

        body(*ins, *outs, *scr)

        @pl.when(last)
        def _():
            carry.drain(cins, couts, sems)

    res = pl.pallas_call(
        wrapped, grid=grid, in_specs=list(in_specs) + [HBM_SPEC] * ci, out_specs=list(out_specs) + [HBM_SPEC] * co,
        out_shape=list(out_shape) + carry.out_shapes,
        scratch_shapes=list(scratch_shapes) + [pltpu.SemaphoreType.DMA((k,)) for k in carry.n_sems],
        compiler_params=pltpu.CompilerParams(dimension_semantics=("arbitrary",) * len(grid), vmem_limit_bytes=VMEM_LIMIT, has_side_effects=True),
        name=name)(*operands, *carry.ins)
    return list(res[:n_out]), list(res[n_out:])


def _run_carried(name, carry):
    def body(*refs):
        ci, co = len(carry.ins), len(carry.out_shapes)
        carry.issue(refs[:ci], refs[ci:ci + co], refs[ci + co:])
        carry.drain(refs[:ci], refs[ci:ci + co], refs[ci + co:])

    return pl.pallas_call(
        body, in_specs=[HBM_SPEC] * len(carry.ins), out_specs=[HBM_SPEC] * len(carry.out_shapes), out_shape=carry.out_shapes,
        scratch_shapes=[pltpu.SemaphoreType.DMA((k,)) for k in carry.n_sems],
        compiler_params=pltpu.CompilerParams(has_side_effects=True), name=name)(*carry.ins)


NA_BLOCKS = SEQ // NA_QB
NA_ROWS_TOTAL = SEQ // GRID_W
NA_CLASSES = ((0, 0), (8, 4), (NA_ROWS_TOTAL - NA_QROWS, NA_ROWS_TOTAL - NA_WROWS))


def _na_pairs(i0, ws):
    out = []
    for qi in range(NA_QROWS):
        i = i0 + qi
        rs = min(max(i - 4, 0), NA_ROWS_TOTAL - 8)
        for kr in range(NA_WROWS):
            r = ws + kr
            if rs <= r < rs + 8:
                out.append((qi, kr, r - i + 7))
    return out


def _diag_onehot():
    qc, kc = np.meshgrid(np.arange(GRID_W), np.arange(GRID_W), indexing="ij")
    e = np.zeros((GRID_W * GRID_W, 128), np.float32)
    j = (kc - qc + 15).reshape(-1)
    ok = (j >= 0) & (j <= 30)
    e[np.arange(GRID_W * GRID_W)[ok], j[ok]] = 1.0
    return jnp.asarray(e)


def _rpb_expand(rpb):
    r2 = jnp.pad(rpb.reshape(NH * 15, 31), ((0, 0), (0, 128 - 31)))

    def body(r_ref, e_ref, o_ref):
        o_ref[...] = lax.dot_general(r_ref[...], e_ref[...], _NT, preferred_element_type=F32, precision=lax.Precision.HIGHEST)

    out = pl.pallas_call(body, out_shape=jax.ShapeDtypeStruct((NH * 15, GRID_W * GRID_W), F32), name="rpb_expand",
                         compiler_params=pltpu.CompilerParams(vmem_limit_bytes=VMEM_LIMIT))(r2, _diag_onehot())
    return out.reshape(NH, 15, GRID_W, GRID_W)


def _na_bias_tiles(rpb, carry):
    def body(b_ref, o_ref):
        qc = lax.broadcasted_iota(jnp.int32, (GRID_W, GRID_W), 0)
        kc = lax.broadcasted_iota(jnp.int32, (GRID_W, GRID_W), 1)
        first = jnp.clip(qc - 8, 0, GRID_W - 16)
        in_window = (kc >= first) & (kc < first + 16)
        neg = jnp.full((GRID_W, GRID_W), NEG, F32)
        for cls, (i0, ws) in enumerate(NA_CLASSES):
            @pl.when(pl.program_id(0) == cls)
            def _(i0=i0, ws=ws):
                pairs = {(qi, kr): dr for qi, kr, dr in _na_pairs(i0, ws)}
                masked = {dr: jnp.where(in_window, b_ref[dr], NEG) for dr in sorted(set(pairs.values()))}
                for qi in range(NA_QROWS):
                    for k2 in range(NA_WROWS // 2):
                        blocks = [masked[pairs[(qi, kr)]] if (qi, kr) in pairs else neg for kr in (2 * k2, 2 * k2 + 1)]
                        o_ref[qi * GRID_W:(qi + 1) * GRID_W, k2 * 128:(k2 + 1) * 128] = jnp.concatenate(blocks, axis=1)

    (tiles,), sent = _carrier_call(
        "na_bias_tiles", body, (3, NH), [pl.BlockSpec((None, 15, GRID_W, GRID_W), lambda c, h: (h, 0, 0, 0))],
        [pl.BlockSpec((None, None, NA_QB, NA_WIN), lambda c, h: (c, h, 0, 0))], [jax.ShapeDtypeStruct((3, NH, NA_QB, NA_WIN), F32)],
        [], (_rpb_expand(rpb),), carry)
    return tiles, sent


def _na_cls(b):
    return jnp.where(b == 0, 0, jnp.where(b == NA_BLOCKS - 1, 2, 1))


def _na_start(b):
    return pl.multiple_of(jnp.clip(b * NA_QROWS - 4, 0, NA_ROWS_TOTAL - NA_WROWS) * GRID_W, GRID_W)


NA_FWD_HPS = 16
NA_BWD_HPS = 4


def _na_in_specs(hps):
    lw = hps * HD
    nlw = DM // lw
    return [pl.BlockSpec((NA_QB, lw), lambda hp, b: (b, hp)),
            pl.BlockSpec((SEQ, lw), lambda hp, b: (0, nlw + hp)),
            pl.BlockSpec((SEQ, lw), lambda hp, b: (0, 2 * nlw + hp)),
            pl.BlockSpec((None, hps, NA_QB, NA_WIN), lambda hp, b: (_na_cls(b), hp, 0, 0))]


def _na_fwd(qkv, bias, carry):
    lw = NA_FWD_HPS * HD

    def body(q_ref, k_ref, v_ref, b_ref, o_ref):
        start = _na_start(pl.program_id(1))
        q = q_ref[...]
        kw = k_ref[pl.ds(start, NA_WIN), :]
        vw = v_ref[pl.ds(start, NA_WIN), :]
        outs = []
        for hh in range(NA_FWD_HPS):
            sl = slice(hh * HD, (hh + 1) * HD)
            s = lax.dot_general(q[:, sl] * QSCALE, kw[:, sl], _NT, preferred_element_type=F32) + b_ref[hh]
            p = jnp.exp(s - jnp.max(s, axis=-1, keepdims=True))
            l = jnp.sum(p, axis=-1, keepdims=True)
            outs.append(jnp.dot(p.astype(BF16), vw[:, sl], preferred_element_type=F32) / l)
        o_ref[...] = jnp.concatenate(outs, axis=1).astype(o_ref.dtype)

    (o,), sent = _carrier_call(
        "na_fwd", body, (NH // NA_FWD_HPS, NA_BLOCKS), _na_in_specs(NA_FWD_HPS), [pl.BlockSpec((NA_QB, lw), lambda hp, b: (b, hp))],
        [jax.ShapeDtypeStruct((SEQ, DM), BF16)], [], (qkv, qkv, qkv, bias), carry)
    return o, sent


def _na_bwd(qkv, bias, do, carry):
    lw = NA_BWD_HPS * HD

    def body(q_ref, k_ref, v_ref, b_ref, do_ref, dqkv_ref, z_ref, dk_acc, dv_acc):
        blk = pl.program_id(1)

        @pl.when(blk == 0)
        def _():
            dk_acc[...] = jnp.zeros_like(dk_acc)
            dv_acc[...] = jnp.zeros_like(dv_acc)
            z_ref[...] = jnp.zeros_like(z_ref)

        start = _na_start(blk)
        q = q_ref[...]
        do = do_ref[...]
        kw = k_ref[pl.ds(start, NA_WIN), :]
        vw = v_ref[pl.ds(start, NA_WIN), :]
        dqs, dks, dvs, dss = [], [], [], []
        for hh in range(NA_BWD_HPS):
            sl = slice(hh * HD, (hh + 1) * HD)
            qh = q[:, sl] * QSCALE
            s = lax.dot_general(qh, kw[:, sl], _NT, preferred_element_type=F32) + b_ref[hh]
            p = jnp.exp(s - jnp.max(s, axis=-1, keepdims=True))
            p = p / jnp.sum(p, axis=-1, keepdims=True)
            dp = lax.dot_general(do[:, sl], vw[:, sl], _NT, preferred_element_type=F32)
            ds = p * (dp - jnp.sum(p * dp, axis=-1, keepdims=True))
            dsb = ds.astype(BF16)
            dqs.append(jnp.dot(dsb, kw[:, sl], preferred_element_type=F32) * QSCALE)
            dks.append(lax.dot_general(qh, dsb, _TN, preferred_element_type=F32).T)
            dvs.append(lax.dot_general(do[:, sl], p.astype(BF16), _TN, preferred_element_type=F32).T)
            dss.append(ds)
        for cls, (i0, ws) in enumerate(NA_CLASSES):
            @pl.when(_na_cls(blk) == cls)
            def _(i0=i0, ws=ws):
                for hh, ds in enumerate(dss):
                    for qi, kr, dr in _na_pairs(i0, ws):
                        z_ref[hh, dr * GRID_W:(dr + 1) * GRID_W, :] += ds[qi * GRID_W:(qi + 1) * GRID_W, kr * GRID_W:(kr + 1) * GRID_W]
        dqkv_ref[0, pl.ds(pl.multiple_of(blk * NA_QB, NA_QB), NA_QB), :] = jnp.concatenate(dqs, axis=1).astype(dqkv_ref.dtype)
        dk_acc[pl.ds(start, NA_WIN), :] += jnp.concatenate(dks, axis=1)
        dv_acc[pl.ds(start, NA_WIN), :] += jnp.concatenate(dvs, axis=1)

        @pl.when(blk == NA_BLOCKS - 1)
        def _():
            dqkv_ref[1] = dk_acc[...].astype(dqkv_ref.dtype)
            dqkv_ref[2] = dv_acc[...].astype(dqkv_ref.dtype)

    (dqkv, z), sent = _carrier_call(
        "na_bwd", body, (NH // NA_BWD_HPS, NA_BLOCKS),
        _na_in_specs(NA_BWD_HPS) + [pl.BlockSpec((NA_QB, lw), lambda hp, b: (b, hp))],
        [pl.BlockSpec((3, SEQ, lw), lambda hp, b: (0, 0, hp)), pl.BlockSpec((NA_BWD_HPS, 15 * GRID_W, GRID_W), lambda hp, b: (hp, 0, 0))],
        [jax.ShapeDtypeStruct((3, SEQ, DM), BF16), jax.ShapeDtypeStruct((NH, 15 * GRID_W, GRID_W), F32)],
        [pltpu.VMEM((SEQ, lw), F32), pltpu.VMEM((SEQ, lw), F32)], (qkv, qkv, qkv, bias, do), carry)
    return dqkv, z, sent


def _rpb_grad(z):
    z2 = z.reshape(NH * 15, GRID_W * GRID_W)

    def body(z_ref, e_ref, o_ref):
        o_ref[...] = jnp.dot(z_ref[...], e_ref[...], preferred_element_type=F32, precision=lax.Precision.HIGHEST)

    out = pl.pallas_call(body, out_shape=jax.ShapeDtypeStruct((NH * 15, 128), F32), name="rpb_grad",
                         compiler_params=pltpu.CompilerParams(vmem_limit_bytes=VMEM_LIMIT))(z2, _diag_onehot())
    return out[:, :31].reshape(NH, 15, 31)


DIL_BLOCKS = SEQ // DIL_QB
DIL_FWD_HPS = 16
DIL_HPS = 8
DIL_LW = DIL_HPS * HD
DIL_NLW = DM // DIL_LW


COLS = 128


def _col_spec():
    return pl.BlockSpec((SEQ, COLS), lambda j: (0, j))


def _grp_spec():
    return pl.BlockSpec((3, SEQ, COLS), lambda j: (0, 0, j))


def _store_group_order(dst_ref, src_ref):
    for g, d in enumerate(DIL):
        n = SEQ // d
        for r in range(d):
            dst_ref[g, r * n:(r + 1) * n, :] = src_ref[pl.ds(r, n, stride=d), :].astype(dst_ref.dtype)


def _store_token_order(dst_ref, src_ref, g):
    d = DIL[g]
    n = SEQ // d
    for r in range(d):
        dst_ref[pl.ds(r, n, stride=d), :] = src_ref[g, r * n:(r + 1) * n, :].astype(dst_ref.dtype)


def _to_groups(name, a):
    def body(a_ref, o_ref, t_ref):
        _store_group_order(o_ref, a_ref)
        for g in range(3):
            t_ref[g] = o_ref[g].astype(F32).T.astype(t_ref.dtype)

    return pl.pallas_call(
        body, grid=(DM // COLS,), in_specs=[_col_spec()], out_specs=[_grp_spec(), pl.BlockSpec((3, COLS, SEQ), lambda j: (0, j, 0))],
        out_shape=[jax.ShapeDtypeStruct((3, SEQ, DM), BF16), jax.ShapeDtypeStruct((3, DM, SEQ), BF16)],
        compiler_params=_params(("parallel",)), name=name)(a)


def _from_groups_sum(name, a):
    def body(a_ref, o_ref, t1, t2):
        _store_token_order(t1, a_ref, 1)
        _store_token_order(t2, a_ref, 2)
        o_ref[...] = (a_ref[0] + t1[...]) + t2[...]

    return pl.pallas_call(body, grid=(DM // COLS,), in_specs=[_grp_spec()], out_specs=_col_spec(),
                          out_shape=jax.ShapeDtypeStruct((SEQ, DM), F32), scratch_shapes=[pltpu.VMEM((SEQ, COLS), F32)] * 2,
                          compiler_params=_params(("parallel",)), name=name)(a)


def _dil_start(b):
    return pl.multiple_of(jnp.clip(b * DIL_QB - DIL_RADIUS, 0, SEQ - DIL_WIN), DIL_RADIUS)


def _dil_neg_dist(g, ii, jj):
    shift = 11 - 2 * g
    dist = jnp.abs(ii - jj)
    valid = (dist <= DIL_RADIUS) & (jnp.right_shift(ii, shift) == jnp.right_shift(jj, shift))
    return jnp.where(valid, -dist.astype(F32), NEG)


def _dil_in_specs(hps):
    lw = hps * HD
    nlw = DM // lw
    return [pl.BlockSpec(memory_space=pltpu.SMEM),
            pl.BlockSpec((None, DIL_QB, lw), lambda g, hp, b: (g, b, hp)),
            pl.BlockSpec((None, SEQ, lw), lambda g, hp, b: (g, 0, nlw + hp)),
            pl.BlockSpec((None, SEQ, lw), lambda g, hp, b: (g, 0, 2 * nlw + hp))]


def _dil_fwd(qkv, slopes, carry):
    def body(sl_ref, q_ref, k_ref, v_ref, o_ref, lse_ref):
        g, hp, b = pl.program_id(0), pl.program_id(1), pl.program_id(2)
        start = _dil_start(b)
        neg_dist = _dil_neg_dist(g, b * DIL_QB + lax.broadcasted_iota(jnp.int32, (DIL_QB, DIL_WIN), 0),
                                 start + lax.broadcasted_iota(jnp.int32, (DIL_QB, DIL_WIN), 1))
        dil = jnp.left_shift(1, 2 * g).astype(F32)
        q = q_ref[...]
        kw = k_ref[pl.ds(start, DIL_WIN), :]
        vw = v_ref[pl.ds(start, DIL_WIN), :]
        outs, lses = [], []
        for hh in range(DIL_FWD_HPS):
            sl = slice(hh * HD, (hh + 1) * HD)
            s = lax.dot_general(q[:, sl] * QSCALE, kw[:, sl], _NT, preferred_element_type=F32)
            s = s + (sl_ref[hp * DIL_FWD_HPS + hh] * dil) * neg_dist
            m = jnp.max(s, axis=-1, keepdims=True)
            p = jnp.exp(s - m)
            l = jnp.sum(p, axis=-1, keepdims=True)
            outs.append(jnp.dot(p.astype(BF16), vw[:, sl], preferred_element_type=F32) / l)
            lses.append(jnp.broadcast_to(m + jnp.log(l), (DIL_QB, HD)))
        o_ref[...] = jnp.concatenate(outs, axis=1).astype(o_ref.dtype)
        lse_ref[...] = jnp.concatenate(lses, axis=1)

    ospec = pl.BlockSpec((None, DIL_QB, DIL_FWD_HPS * HD), lambda g, hp, b: (g, b, hp))
    (o, lse), sent = _carrier_call(
        "dil_fwd", body, (3, NH // DIL_FWD_HPS, DIL_BLOCKS), _dil_in_specs(DIL_FWD_HPS), [ospec, ospec],
        [jax.ShapeDtypeStruct((3, SEQ, DM), BF16), jax.ShapeDtypeStruct((3, SEQ, DM), F32)], [], (slopes, qkv, qkv, qkv), carry)
    return o, lse, sent


def _dil_merge(o_all, lse_all):
    def body(o_ref, l_ref, out_ref, lse_ref, o1, o2, l1, l2):
        for g, (ot, lt) in ((1, (o1, l1)), (2, (o2, l2))):
            _store_token_order(ot, o_ref, g)
            _store_token_order(lt, l_ref, g)
        la, lb, lc = l_ref[0], l1[...], l2[...]
        m = jnp.maximum(jnp.maximum(la, lb), lc)
        wa, wb, wc = jnp.exp(la - m), jnp.exp(lb - m), jnp.exp(lc - m)
        sw = (wa + wb) + wc
        out_ref[...] = (((wa * o_ref[0].astype(F32) + wb * o1[...]) + wc * o2[...]) / sw).astype(out_ref.dtype)
        lse_ref[...] = m + jnp.log(sw)

    return pl.pallas_call(
        body, grid=(DM // COLS,), in_specs=[_grp_spec(), _grp_spec()], out_specs=[_col_spec(), _col_spec()],
        out_shape=[jax.ShapeDtypeStruct((SEQ, DM), BF16), jax.ShapeDtypeStruct((SEQ, DM), F32)],
        scratch_shapes=[pltpu.VMEM((SEQ, COLS), F32)] * 4, compiler_params=_params(("parallel",)), name="dil_merge")(o_all, lse_all)


def _dil_bwd_prep(do, o, lse):
    heads = COLS // HD

    def body(do_ref, o_ref, lse_ref, dog_ref, ddr_ref, lser_ref, dd, grp):
        prod = do_ref[...] * o_ref[...].astype(F32)
        dd[...] = jnp.concatenate(
            [jnp.broadcast_to(jnp.sum(prod[:, h * HD:(h + 1) * HD], axis=-1, keepdims=True), (SEQ, HD)) for h in range(heads)], axis=1)
        _store_group_order(dog_ref, do_ref)
        for src, dst in ((dd, ddr_ref), (lse_ref, lser_ref)):
            _store_group_order(grp, src)
            for g in range(3):
                t = grp[g].T
                for h in range(heads):
                    dst[g, h] = t[h * HD:h * HD + 8, :]

    rows = jax.ShapeDtypeStruct((3, NH, 8, SEQ), F32)
    rspec = pl.BlockSpec((3, heads, 8, SEQ), lambda j: (0, j, 0, 0))
    return pl.pallas_call(
        body, grid=(DM // COLS,), in_specs=[_col_spec()] * 3, out_specs=[_grp_spec(), rspec, rspec],
        out_shape=[jax.ShapeDtypeStruct((3, SEQ, DM), BF16), rows, rows],
        scratch_shapes=[pltpu.VMEM((SEQ, COLS), F32), pltpu.VMEM((3, SEQ, COLS), F32)],
        compiler_params=_params(("parallel",)), name="dil_bwd_prep")(do, o, lse)


def _dil_bwd(qkv, do, dd, lse, slopes, carry):
    def body(sl_ref, q_ref, k_ref, v_ref, do_ref, dd_ref, lse_ref, dqkv_ref, dk_acc, dv_acc):
        g, hp, b = pl.program_id(0), pl.program_id(1), pl.program_id(2)

        @pl.when(b == 0)
        def _():
            dk_acc[...] = jnp.zeros_like(dk_acc)
            dv_acc[...] = jnp.zeros_like(dv_acc)

        start = _dil_start(b)
        neg_dist = _dil_neg_dist(g, b * DIL_QB + lax.broadcasted_iota(jnp.int32, (DIL_WIN, DIL_QB), 1),
                                 start + lax.broadcasted_iota(jnp.int32, (DIL_WIN, DIL_QB), 0))
        dil = jnp.left_shift(1, 2 * g).astype(F32)
        q = q_ref[...]
        do = do_ref[...]
        kw = k_ref[pl.ds(start, DIL_WIN), :]
        vw = v_ref[pl.ds(start, DIL_WIN), :]
        dqs, dks, dvs = [], [], []
        for hh in range(DIL_HPS):
            sl = slice(hh * HD, (hh + 1) * HD)
            qh = q[:, sl] * QSCALE
            st = lax.dot_general(kw[:, sl], qh, _NT, preferred_element_type=F32)
            st = st + (sl_ref[hp * DIL_HPS + hh] * dil) * neg_dist
            pt = jnp.exp(st - lse_ref[hh, 0:1, :])
            dpt = lax.dot_general(vw[:, sl], do[:, sl], _NT, preferred_element_type=F32)
            dst = (pt * (dpt - dd_ref[hh, 0:1, :])).astype(BF16)
            dqs.append(lax.dot_general(kw[:, sl], dst, _TN, preferred_element_type=F32).T * QSCALE)
            dks.append(jnp.dot(dst, qh, preferred_element_type=F32))
            dvs.append(jnp.dot(pt.astype(BF16), do[:, sl], preferred_element_type=F32))
        dqkv_ref[0, pl.ds(pl.multiple_of(b * DIL_QB, DIL_QB), DIL_QB), :] = jnp.concatenate(dqs, axis=1).astype(dqkv_ref.dtype)
        dk_acc[pl.ds(start, DIL_WIN), :] += jnp.concatenate(dks, axis=1)
        dv_acc[pl.ds(start, DIL_WIN), :] += jnp.concatenate(dvs, axis=1)

        @pl.when(b == DIL_BLOCKS - 1)
        def _():
            dqkv_ref[1] = dk_acc[...].astype(dqkv_ref.dtype)
            dqkv_ref[2] = dv_acc[...].astype(dqkv_ref.dtype)

    qspec = pl.BlockSpec((None, DIL_QB, DIL_LW), lambda g, hp, b: (g, b, hp))
    rspec = pl.BlockSpec((None, DIL_HPS, 8, DIL_QB), lambda g, hp, b: (g, hp, 0, b))
    (dqkv,), sent = _carrier_call(
        "dil_bwd", body, (3, DIL_NLW, DIL_BLOCKS), _dil_in_specs(DIL_HPS) + [qspec, rspec, rspec],
        [pl.BlockSpec((None, 3, SEQ, DIL_LW), lambda g, hp, b: (g, 0, 0, hp))], [jax.ShapeDtypeStruct((3, 3, SEQ, DM), BF16)],
        [pltpu.VMEM((SEQ, DIL_LW), F32), pltpu.VMEM((SEQ, DIL_LW), F32)], (slopes, qkv, qkv, qkv, do, dd, lse), carry)
    return dqkv, sent


def _ffn_fwd(name, x, g_pre, g_post, wgt4, wut4, wd4, carry):
    tm = 512

    def body(x_ref, gpre_ref, gpost_ref, wg_ref, wu_ref, wd_ref, xn_ref, h_ref, gate_ref, up_ref, u_ref, acc):
        s = pl.program_id(1)

        @pl.when(s == 0)
        def _():
            x = x_ref[...]
            r = lax.rsqrt(jnp.mean(x * x, axis=-1, keepdims=True) + RMS_EPS)
            h_ref[...] = (x * r * gpre_ref[...]).astype(h_ref.dtype)

        h = h_ref[...]
        gate = lax.dot_general(h, wg_ref[...], _NT, preferred_element_type=F32).astype(BF16)
        up = lax.dot_general(h, wu_ref[...], _NT, preferred_element_type=F32).astype(BF16)
        gate_ref[...] = gate
        up_ref[...] = up
        gf = gate.astype(F32)
        act = (gf * jax.nn.sigmoid(gf) * up.astype(F32)).astype(BF16)
        part = jnp.dot(act, wd_ref[...], preferred_element_type=F32)

        @pl.when(s == 0)
        def _():
            acc[...] = part

        @pl.when(s > 0)
        def _():
            acc[...] += part

        @pl.when(s == NCHIP - 1)
        def _():
            u = acc[...]
            u_ref[...] = u
            r = lax.rsqrt(jnp.mean(u * u, axis=-1, keepdims=True) + RMS_EPS)
            xn_ref[...] = x_ref[...] + u * r * gpost_ref[...]

    rows = pl.BlockSpec((tm, DM), lambda i, s: (i, 0))
    vec = pl.BlockSpec((1, DM), lambda i, s: (0, 0))
    wspec = _ffn_wspec(lambda i, s: (s, 0, 0))
    mid = pl.BlockSpec((None, tm, FSH), lambda i, s: (s, i, 0))
    outs, sent = _carrier_call(
        name, body, (SEQ // tm, NCHIP), [rows, vec, vec, wspec, wspec, wspec], [rows, rows, mid, mid, rows],
        [jax.ShapeDtypeStruct((SEQ, DM), F32), jax.ShapeDtypeStruct((SEQ, DM), BF16), jax.ShapeDtypeStruct((NCHIP, SEQ, FSH), BF16),
         jax.ShapeDtypeStruct((NCHIP, SEQ, FSH), BF16), jax.ShapeDtypeStruct((SEQ, DM), F32)],
        [pltpu.VMEM((tm, DM), F32)], (x, g_pre, g_post, wgt4, wut4, wd4), carry)
    return outs, sent


def _ffn_block(layer, x, g_pre, g_post, ex):
    tag = f"l{layer}_ffn_fwd"
    (x_new, h, gate, up, u), sent = _ffn_fwd(tag, x, g_pre, g_post, ex.weight(("ffn_w_gate", layer)), ex.weight(("ffn_w_up", layer)),
                                             ex.weight(("ffn_w_down", layer)), ex.carry(tag))
    ex.carried(tag, sent)
    return x_new, (x, h, gate, up, u)


def _ffn_bwd(name, dx, x, gate, up, u, g_pre, g_post, wgt4, wut4, wd4, carry):
    tm = 512

    def body(dx_ref, x_ref, gate_ref, up_ref, u_ref, gpre_ref, gpost_ref, wg_ref, wu_ref, wd_ref,
             dxin_ref, du_ref, dgate_ref, dup_ref, act_ref, dgpre_ref, dgpost_ref, dh_acc):
        i, s = pl.program_id(0), pl.program_id(1)

        @pl.when((i == 0) & (s == 0))
        def _():
            dgpre_ref[...] = jnp.zeros_like(dgpre_ref)
            dgpost_ref[...] = jnp.zeros_like(dgpost_ref)

        @pl.when(s == 0)
        def _():
            dy = dx_ref[...]
            uu = u_ref[...]
            r = lax.rsqrt(jnp.mean(uu * uu, axis=-1, keepdims=True) + RMS_EPS)
            yh = uu * r
            t = dy * gpost_ref[...]
            du_ref[...] = (r * (t - yh * jnp.mean(t * yh, axis=-1, keepdims=True))).astype(du_ref.dtype)
            dgpost_ref[...] += jnp.sum(dy * yh, axis=0, keepdims=True)

        dact = lax.dot_general(du_ref[...], wd_ref[...], _NT, preferred_element_type=F32)
        g = gate_ref[...].astype(F32)
        upv = up_ref[...].astype(F32)
        sg = jax.nn.sigmoid(g)
        dgate = (dact * upv * sg * (1.0 + g * (1.0 - sg))).astype(BF16)
        dup = (dact * g * sg).astype(BF16)
        dgate_ref[...] = dgate
        dup_ref[...] = dup
        act_ref[...] = (g * sg * upv).astype(act_ref.dtype)
        part = jnp.dot(dgate, wg_ref[...], preferred_element_type=F32) + jnp.dot(dup, wu_ref[...], preferred_element_type=F32)

        @pl.when(s == 0)
        def _():
            dh_acc[...] = part

        @pl.when(s > 0)
        def _():
            dh_acc[...] += part

        @pl.when(s == NCHIP - 1)
        def _():
            dh = dh_acc[...]
            xx = x_ref[...]
            r = lax.rsqrt(jnp.mean(xx * xx, axis=-1, keepdims=True) + RMS_EPS)
            yh = xx * r
            t = dh * gpre_ref[...]
            dxin_ref[...] = dx_ref[...] + r * (t - yh * jnp.mean(t * yh, axis=-1, keepdims=True))
            dgpre_ref[...] += jnp.sum(dh * yh, axis=0, keepdims=True)

    rows = pl.BlockSpec((tm, DM), lambda i, s: (i, 0))
    vec = pl.BlockSpec((1, DM), lambda i, s: (0, 0))
    wspec = _ffn_wspec(lambda i, s: (s, 0, 0))
    mid = pl.BlockSpec((None, tm, FSH), lambda i, s: (s, i, 0))
    mid_shape = jax.ShapeDtypeStruct((NCHIP, SEQ, FSH), BF16)
    return _carrier_call(
        name, body, (SEQ // tm, NCHIP), [rows, rows, mid, mid, rows, vec, vec, wspec, wspec, wspec], [rows, rows, mid, mid, mid, vec, vec],
        [jax.ShapeDtypeStruct((SEQ, DM), F32), jax.ShapeDtypeStruct((SEQ, DM), BF16), mid_shape, mid_shape, mid_shape,
         jax.ShapeDtypeStruct((1, DM), F32), jax.ShapeDtypeStruct((1, DM), F32)],
        [pltpu.VMEM((tm, DM), F32)], (dx, x, gate, up, u, g_pre, g_post, wgt4, wut4, wd4), carry)


def _ffn_block_bwd(layer, dx, saved, g_pre, g_post, ex):
    tag = f"l{layer}"
    x, h, gate, up, u = saved
    (dx_in, du, dgate, dup, act, dg_pre, dg_post), sent = _ffn_bwd(
        f"{tag}_ffn_bwd", dx, x, gate, up, u, g_pre, g_post, ex.weight(("ffn_w_gate", layer)), ex.weight(("ffn_w_up", layer)),
        ex.weight(("ffn_w_down", layer)), ex.carry(f"{tag}_ffn_bwd"))
    ex.carried(f"{tag}_ffn_bwd", sent)
    d_wd = _ffn_bwd_dw(f"{tag}_dwd", act, du)
    d_wg = _ffn_bwd_dw(f"{tag}_dwg", dgate, h)
    d_wu = _ffn_bwd_dw(f"{tag}_dwu", dup, h)
    ex.grads(f"{tag}_ffn", {("ffn_w_gate", layer): d_wg, ("ffn_w_up", layer): d_wu, ("ffn_w_down", layer): d_wd})
    return dx_in, dg_pre, dg_post


def _alibi_slopes():
    return 2.0 ** (-8.0 * jnp.arange(1, NH + 1, dtype=F32) / NH)


def _local_step(x, target, norms, rpb, ex):
    g_mix_pre, g_mix_post, g_ffn_pre, g_ffn_post = norms
    row = lambda a, i: a[i:i + 1]

    bias, sent = _na_bias_tiles(rpb, ex.carry("na_bias_tiles"))
    ex.carried("na_bias_tiles", sent)
    h0, h0t = _rms_fwd_both("l0_mix_pre", x, row(g_mix_pre, 0))
    qkv0, sent = _qkv_fwd("l0_qkv", h0[None], ex.weight(("na_w_qkv", 0)), ex.carry("l0_qkv"))
    ex.carried("l0_qkv", sent)
    o0, sent = _na_fwd(qkv0[0], bias, ex.carry("na_fwd"))
    ex.carried("na_fwd", sent)
    na_wo = ex.weight(("na_w_o", 0)).reshape(DM, DM)
    x1, u0 = _proj_fwd("l0_proj", o0, na_wo, x, row(g_mix_post, 0))
    x2, ffn0 = _ffn_block(0, x1, row(g_ffn_pre, 0), row(g_ffn_post, 0), ex)

    slopes = _alibi_slopes()
    h2g, h2gt = _to_groups("l1_h_groups", _rms_fwd("l1_mix_pre", x2, row(g_mix_pre, 1), F32))
    dil_wqkv = ex.weight(("dil_w_qkv", 0))
    qkv1, sent = _qkv_fwd("l1_qkv", h2g, dil_wqkv, ex.carry("l1_qkv"))
    ex.carried("l1_qkv", sent)
    og, lg, sent = _dil_fwd(qkv1, slopes, ex.carry("dil_fwd"))
    ex.carried("dil_fwd", sent)
    o1, lse = _dil_merge(og, lg)
    dil_wo = ex.weight(("dil_w_o", 0)).reshape(DM, DM)
    x3, u1 = _proj_fwd("l1_proj", o1, dil_wo, x2, row(g_mix_post, 1))
    x4, ffn1 = _ffn_block(1, x3, row(g_ffn_pre, 1), row(g_ffn_post, 1), ex)

    dx4, loss_row = _loss_grad("loss", x4, target)

    dx3, dg_fpre1, dg_fpost1 = _ffn_block_bwd(1, dx4, ffn1, row(g_ffn_pre, 1), row(g_ffn_post, 1), ex)
    (do1, du1, dg_mpost1), sent = _proj_bwd("l1_proj_bwd", dx3, u1, row(g_mix_post, 1), dil_wo, F32, ex.carry("l1_proj_bwd"))
    ex.carried("l1_proj_bwd", sent)
    d_dil_wo = _proj_bwd_dw("l1_dwo", o1, du1)
    dog, ddg, lseg = _dil_bwd_prep(do1, o1, lse)
    dqkv1, sent = _dil_bwd(qkv1, dog, ddg, lseg, slopes, ex.carry("dil_bwd"))
    ex.carried("dil_bwd", sent)
    d_dil_wqkv, sent = _qkv_bwd_dw("l1_dwqkv", h2gt, dqkv1, dil_wqkv.shape[2], ex.carry("l1_dwqkv"))
    ex.carried("l1_dwqkv", sent)
    ex.grads("l1_mix", {("dil_w_qkv", 0): d_dil_wqkv, ("dil_w_o", 0): d_dil_wo.reshape(NCHIP, DM // NCHIP, DM)})
    dh2g, sent = _qkv_bwd_dh("l1_dh", dqkv1, dil_wqkv, ex.carry("l1_dh"))
    ex.carried("l1_dh", sent)
    dh2 = _from_groups_sum("l1_dh_tokens", dh2g)
    (dx2, dg_mpre1), sent = _norm_bwd("l1_mix_pre_bwd", dh2, x2, row(g_mix_pre, 1), dx3, ex.carry("l1_mix_pre_bwd"))
    ex.carried("l1_mix_pre_bwd", sent)

    dx1, dg_fpre0, dg_fpost0 = _ffn_block_bwd(0, dx2, ffn0, row(g_ffn_pre, 0), row(g_ffn_post, 0), ex)
    (do0, du0, dg_mpost0), sent = _proj_bwd("l0_proj_bwd", dx1, u0, row(g_mix_post, 0), na_wo, BF16, ex.carry("l0_proj_bwd"))
    ex.carried("l0_proj_bwd", sent)
    d_na_wo = _proj_bwd_dw("l0_dwo", o0, du0)
    dqkv0, z, sent = _na_bwd(qkv0[0], bias, do0, ex.carry("na_bwd"))
    ex.carried("na_bwd", sent)
    d_rpb = _rpb_grad(z)
    na_wqkv = ex.weight(("na_w_qkv", 0))
    d_na_wqkv, sent = _qkv_bwd_dw("l0_dwqkv", h0t[None], dqkv0[None], na_wqkv.shape[2], ex.carry("l0_dwqkv"))
    ex.carried("l0_dwqkv", sent)
    ex.grads("l0_mix", {("na_w_qkv", 0): d_na_wqkv, ("na_w_o", 0): d_na_wo.reshape(NCHIP, DM // NCHIP, DM)})
    dh0, sent = _qkv_bwd_dh("l0_dh", dqkv0[None], na_wqkv, ex.carry("l0_dh"))
    ex.carried("l0_dh", sent)
    (dx0, dg_mpre0), sent = _norm_bwd("l0_mix_pre_bwd", dh0[0], x, row(g_mix_pre, 0), dx1, ex.carry("l0_mix_pre_bwd"))
    ex.carried("l0_mix_pre_bwd", sent)

    dnorms = (jnp.concatenate([dg_mpre0, dg_mpre1]), jnp.concatenate([dg_mpost0, dg_mpost1]),
              jnp.concatenate([dg_fpre0, dg_fpre1]), jnp.concatenate([dg_fpost0, dg_fpost1]))
    return loss_row, dx0, dnorms, d_rpb


def _place():
    x, y, c = lax.axis_index("x"), lax.axis_index("y"), lax.axis_index("c")
    chips = ((1 - x, y), (x, 1 - y), (1 - x, 1 - y))
    return x, y, c, chips


def _chip_id(chip):
    return 2 * chip[0] + chip[1]


def _gather_copies(shards):
    n = len(shards)

    def copies(src, out, sems):
        send_sems, recv_sems = sems
        x, y, c, chips = _place()

        def copy(t, k, chip, half, to, from_src=False):
            blk = out[t].at[_chip_id(chip), half]
            return pltpu.make_async_remote_copy(
                src_ref=src[t].at[half] if from_src else blk, dst_ref=blk,
                send_sem=send_sems.at[6 * t + k], recv_sem=recv_sems.at[6 * t + k], device_id=to, device_id_type=MESH)

        return copy, x, y, c, chips

    def issue(src, out, sems):
        copy, x, y, c, chips = copies(src, out, sems)
        for t in range(n):
            for j, chip in enumerate(chips):
                copy(t, j, (x, y), c, (*chip, c), from_src=True).start()

    def drain(src, out, sems):
        copy, x, y, c, chips = copies(src, out, sems)
        passed = []
        for t in range(n):
            for j, chip in enumerate(chips):
                copy(t, j, chip, c, (x, y, c)).wait_recv()
                fwd = copy(t, 3 + j, chip, c, (x, y, 1 - c))
                fwd.start()
                passed.append(fwd)
        for t in range(n):
            for j, chip in enumerate(chips):
                copy(t, 3 + j, chip, 1 - c, (x, y, c)).wait_recv()
        for t in range(n):
            for j, chip in enumerate(chips):
                copy(t, j, (x, y), c, (*chip, c), from_src=True).wait_send()
        for cp in passed:
            cp.wait_send()

    return _Carried(shards, [jax.ShapeDtypeStruct((NCHIP,) + s.shape, s.dtype) for s in shards], (6 * n, 6 * n), issue, drain)


def _pair_exchange_copies(grads):
    n = len(grads)

    def copies(g, theirs, sems):
        send_sems, recv_sems = sems
        x, y, c, _ = _place()
        return [pltpu.make_async_remote_copy(src_ref=g[t].at[:, 1 - c], dst_ref=theirs[t], send_sem=send_sems.at[t],
                                             recv_sem=recv_sems.at[t], device_id=(x, y, 1 - c), device_id_type=MESH) for t in range(n)]

    def issue(g, theirs, sems):
        for cp in copies(g, theirs, sems):
            cp.start()

    def drain(g, theirs, sems):
        for cp in copies(g, theirs, sems):
            cp.wait()

    return _Carried(grads, [jax.ShapeDtypeStruct((NCHIP,) + g.shape[2:], g.dtype) for g in grads], (n, n), issue, drain)


def _chip_exchange_copies(items):
    flat = [(t, i, j) for t, (_, peers) in enumerate(items) for i, j in enumerate(peers)]

    def copies(p, slots, sems):
        send_sems, recv_sems = sems
        x, y, c, chips = _place()
        return [pltpu.make_async_remote_copy(src_ref=p[t].at[_chip_id(chips[j])], dst_ref=slots[t].at[i], send_sem=send_sems.at[k],
                                             recv_sem=recv_sems.at[k], device_id=(*chips[j], c), device_id_type=MESH)
                for k, (t, i, j) in enumerate(flat)]

    def issue(p, slots, sems):
        for cp in copies(p, slots, sems):
            cp.start()

    def drain(p, slots, sems):
        for cp in copies(p, slots, sems):
            cp.wait()

    return _Carried([p for p, _ in items], [jax.ShapeDtypeStruct((len(peers),) + p.shape[1:], p.dtype) for p, peers in items],
                    (len(flat), len(flat)), issue, drain)


def _pair_share_copies(halves):
    n = len(halves)

    def copies(h, other, sems):
        send_sems, recv_sems = sems
        x, y, c, _ = _place()
        return [pltpu.make_async_remote_copy(src_ref=h[t], dst_ref=other[t], send_sem=send_sems.at[t], recv_sem=recv_sems.at[t],
                                             device_id=(x, y, 1 - c), device_id_type=MESH) for t in range(n)]

    def issue(h, other, sems):
        for cp in copies(h, other, sems):
            cp.start()

    def drain(h, other, sems):
        for cp in copies(h, other, sems):
            cp.wait()

    return _Carried(halves, [jax.ShapeDtypeStruct(h.shape, h.dtype) for h in halves], (n, n), issue, drain)


SMALL_ROWS = 128


def _allreduce_small(v, carry):
    ci, co = len(carry.ins), len(carry.out_shapes)

    def body(*refs):
        v_ref, cins, o_ref, couts = refs[0], refs[1:1 + ci], refs[1 + ci], refs[2 + ci:2 + ci + co]
        buf, send_sems, recv_sems = refs[2 + ci + co:5 + ci + co]
        csems = refs[5 + ci + co:]
        carry.issue(cins, couts, csems)
        x, y, c, _ = _place()
        me = 4 * x + 2 * y + c
        flip = lambda a, f: 1 - a if f else a
        buf[me] = v_ref[...]
        peers = [(flip(x, d >> 2 & 1), flip(y, d >> 1 & 1), flip(c, d & 1)) for d in range(1, 8)]
        sends = [pltpu.make_async_remote_copy(src_ref=v_ref, dst_ref=buf.at[me], send_sem=send_sems.at[i], recv_sem=recv_sems.at[i],
                                              device_id=peer, device_id_type=MESH) for i, peer in enumerate(peers)]
        for cp in sends:
            cp.start()
        for i, (px, py, pc) in enumerate(peers):
            pltpu.make_async_remote_copy(src_ref=v_ref, dst_ref=buf.at[4 * px + 2 * py + pc], send_sem=send_sems.at[i], recv_sem=recv_sems.at[i],
                                         device_id=(px, py, pc), device_id_type=MESH).wait_recv()
        for cp in sends:
            cp.wait_send()
        acc = buf[0]
        for k in range(1, 8):
            acc = acc + buf[k]
        o_ref[...] = acc
        carry.drain(cins, couts, csems)

    vm = pl.BlockSpec(memory_space=pltpu.VMEM)
    res = pl.pallas_call(
        body, in_specs=[vm] + [HBM_SPEC] * ci, out_specs=[vm] + [HBM_SPEC] * co,
        out_shape=[jax.ShapeDtypeStruct((SMALL_ROWS, 128), F32)] + carry.out_shapes,
        scratch_shapes=[pltpu.VMEM((8, SMALL_ROWS, 128), F32), pltpu.SemaphoreType.DMA((7,)), pltpu.SemaphoreType.DMA((7,))]
        + [pltpu.SemaphoreType.DMA((k,)) for k in carry.n_sems],
        compiler_params=pltpu.CompilerParams(has_side_effects=True), name="allreduce_small")(v, *carry.ins)
    return res[0], list(res[1:])


def _row_block(rows, cols, budget=3 << 19):
    best = 8
    for bm in range(8, rows + 1, 8):
        if rows % bm == 0 and bm * cols * 4 <= budget:
            best = bm
    return best


def _pair_sum(name, place, gs, theirs):
    n = len(gs)
    _, m, c = theirs[0].shape
    bm = _row_block(m, c)

    def body(place_ref, *refs):
        for a_ref, b_ref, o_ref in zip(refs[:n], refs[n:2 * n], refs[2 * n:]):
            o_ref[...] = (a_ref[...].astype(F32) + b_ref[...].astype(F32)).astype(o_ref.dtype)

    spec = pl.BlockSpec((None, bm, c), lambda k, i, pr: (k, i, 0))
    return pl.pallas_call(
        body, out_shape=[jax.ShapeDtypeStruct(theirs[0].shape, BF16)] * n,
        grid_spec=pltpu.PrefetchScalarGridSpec(
            num_scalar_prefetch=1, grid=(NCHIP, m // bm),
            in_specs=[pl.BlockSpec((None, None, bm, c), lambda k, i, pr: (k, pr[0], i, 0))] * n + [spec] * n, out_specs=[spec] * n),
        compiler_params=_params(("parallel", "parallel")), name=name)(place, *gs, *theirs)


def _chip_sum(name, place, parts, slots):
    n, ns = len(parts), len(slots[0])
    _, m, c = parts[0].shape
    bm = _row_block(m, c)

    def body(place_ref, *refs):
        for t in range(n):
            acc = refs[t][...].astype(F32)
            for s_ref in refs[n + t * ns:n + (t + 1) * ns]:
                for i in range(s_ref.shape[0]):
                    acc = acc + s_ref[i].astype(F32)
            refs[n + n * ns + t][...] = acc

    half = pl.BlockSpec((bm, c), lambda i, pr: (i, 0))
    return pl.pallas_call(
        body, out_shape=[jax.ShapeDtypeStruct((m, c), F32)] * n,
        grid_spec=pltpu.PrefetchScalarGridSpec(
            num_scalar_prefetch=1, grid=(m // bm,),
            in_specs=[pl.BlockSpec((None, bm, c), lambda i, pr: (pr[1], i, 0))] * n
            + [pl.BlockSpec((s.shape[0], bm, c), lambda i, pr: (0, i, 0)) for group in slots for s in group],
            out_specs=[half] * n),
        compiler_params=_params(("parallel",)), name=name)(place, *parts, *[s for group in slots for s in group])


def _adamw(name, place, tensors, layer=0, into=None):
    n = len(tensors)
    lead, rows, cols = tensors[0][0].shape
    bm = _row_block(rows // 2, cols, budget=768 * 1024 // n)
    per_half = rows // 2 // bm
    c1 = 1.0 - ADAM_B1 ** ADAM_STEP
    c2 = 1.0 - ADAM_B2 ** ADAM_STEP

    def body(place_ref, *refs):
        outs = refs[len(refs) - 4 * n:]
        for t in range(n):
            w_ref, ga_ref, gb_ref, m_ref, v_ref = refs[5 * t:5 * t + 5]
            go_ref, d_ref, mo_ref, vo_ref = outs[4 * t:4 * t + 4]
            g = jnp.where(pl.program_id(0) // per_half == place_ref[0], ga_ref[...], gb_ref[...])
            mn = ADAM_B1 * m_ref[...] + (1.0 - ADAM_B1) * g
            vn = ADAM_B2 * v_ref[...] + (1.0 - ADAM_B2) * (g * g)
            go_ref[...] = g
            mo_ref[...] = mn
            vo_ref[...] = vn
            d_ref[...] = -ADAM_LR * ((mn / c1) / (jnp.sqrt(vn / c2) + ADAM_EPS) + ADAM_WD * w_ref[...])

    spec = pl.BlockSpec((None, bm, cols), lambda i, pr: (layer, i, 0))

    def half_spec(mine):
        def index(i, pr):
            first = (pr[0] == 0) == mine
            park = jnp.where(first, per_half - 1, 0)
            return jnp.where((i < per_half) == first, i % per_half, park), 0
        return pl.BlockSpec((bm, cols), index)
    sh = jax.ShapeDtypeStruct((lead, rows, cols), F32)
    prev = [] if into is None else [a for res in into for a in res]
    res = pl.pallas_call(
        body, out_shape=[sh] * (4 * n), input_output_aliases={1 + 5 * n + k: k for k in range(len(prev))},
        grid_spec=pltpu.PrefetchScalarGridSpec(
            num_scalar_prefetch=1, grid=(rows // bm,),
            in_specs=[spec, half_spec(True), half_spec(False), spec, spec] * n + [pl.BlockSpec(memory_space=pl.ANY)] * len(prev),
            out_specs=[spec] * (4 * n)),
        compiler_params=_params(("parallel",)), name=name)(place, *[a for t in tensors for a in t], *prev)
    return [res[4 * t:4 * t + 4] for t in range(n)]


def _pack_small(norms, rpb, last=None):
    flat = jnp.concatenate([a.reshape(-1) for a in norms] + [rpb.reshape(-1)])
    flat = jnp.pad(flat, (0, SMALL_ROWS * 128 - flat.shape[0]))
    if last is not None:
        flat = lax.dynamic_update_slice(flat, last.reshape(1), (flat.shape[0] - 1,))
    return flat.reshape(SMALL_ROWS, 128)


def _unpack_small(p):
    flat = p.reshape(-1)
    norms = [flat[i * 2 * DM:(i + 1) * 2 * DM].reshape(2, DM) for i in range(4)]
    rpb = flat[8 * DM:8 * DM + NH * 15 * 31].reshape(1, NH, 15, 31)
    return norms, rpb


FFN_NAMES = ("ffn_w_gate", "ffn_w_up", "ffn_w_down")
L0_FFN = tuple((n, 0) for n in FFN_NAMES)
L1_FFN = tuple((n, 1) for n in FFN_NAMES)
NA_KEYS = (("na_w_qkv", 0), ("na_w_o", 0))
DIL_KEYS = (("dil_w_qkv", 0), ("dil_w_o", 0))
ALL_PEERS, NEIGHBOURS, DIAGONAL = (0, 1, 2), (0, 1), (2,)


class _Exchange:
    GATHERS = {"na_bias_tiles": NA_KEYS, "l0_qkv": L0_FFN[:1], "na_fwd": L0_FFN[1:], "l0_ffn_fwd": DIL_KEYS[:1], "dil_fwd": L1_FFN + DIL_KEYS[1:]}
    PAIRS = {"l1_proj_bwd": L1_FFN, "l1_dh": DIL_KEYS, "l0_proj_bwd": L0_FFN}
    EXCHANGES = {"dil_bwd": [(k, ALL_PEERS) for k in L1_FFN],
                 "l0_ffn_bwd": [(DIL_KEYS[0], NEIGHBOURS), (DIL_KEYS[1], ALL_PEERS)],
                 "na_bwd": [(k, ALL_PEERS) for k in L0_FFN] + [(DIL_KEYS[0], DIAGONAL)],
                 "l0_dh": [(k, NEIGHBOURS) for k in NA_KEYS],
                 "allreduce_small": [(k, DIAGONAL) for k in NA_KEYS]}
    SHARES = {"l1_dwqkv": L1_FFN, "l0_dwqkv": L0_FFN + DIL_KEYS}

    def __init__(self, shards):
        self.chip = 2 * lax.axis_index("x") + lax.axis_index("y")
        self.place = jnp.stack([lax.axis_index("c"), self.chip]).astype(jnp.int32)
        self.own = {k: s.reshape(2, s.shape[0] // 2, s.shape[1]).astype(BF16) for k, s in shards.items()}
        self.gathered, self.mine, self.parts, self.slots, self.full, self.other = {}, {}, {}, {}, {}, {}

    def _take(self, keys, landed):
        for k, gw in zip(keys, landed):
            self.gathered[k] = lax.dynamic_update_slice(gw, self.own[k][None], (self.chip, 0, 0, 0))

    def _sum(self, items, landed):
        runs = []
        for (k, peers), s in zip(items, landed):
            got = self.slots.setdefault(k, {})
            got[peers] = s
            if sum(len(p) for p in got) == len(ALL_PEERS):
                like = (self.parts[k].shape, tuple(sorted(got)))
                if runs and runs[-1][0] == like:
                    runs[-1][1].append(k)
                else:
                    runs.append((like, [k]))
        for (_, split), ks in runs:
            sums = _chip_sum(f"chip_sum_{ks[0][0]}_{ks[0][1]}", self.place, [self.parts[k] for k in ks],
                             [[self.slots[k][p] for p in split] for k in ks])
            self.full.update(zip(ks, sums))

    def weight(self, key):
        g = self.gathered[key]
        return g.reshape(NCHIP, 2 * g.shape[2], g.shape[3])

    def _pair_sums(self, keys, theirs):
        runs = []
        for k, t in zip(keys, theirs):
            if runs and runs[-1][0][1].shape == t.shape:
                runs[-1].append((k, t))
            else:
                runs.append([(k, t)])
        for run in runs:
            ks = [k for k, _ in run]
            sums = _pair_sum(f"pair_sum_{ks[0][0]}_{ks[0][1]}", self.place, [self.mine[k] for k in ks], [t for _, t in run])
            self.parts.update(zip(ks, sums))

    def carry(self, tag):
        if tag in self.GATHERS:
            return _gather_copies([self.own[k] for k in self.GATHERS[tag]])
        if tag in self.PAIRS:
            return _pair_exchange_copies([self.mine[k] for k in self.PAIRS[tag]])
        if tag in self.EXCHANGES:
            return _chip_exchange_copies([(self.parts[k], peers) for k, peers in self.EXCHANGES[tag]])
        if tag in self.SHARES:
            return _pair_share_copies([self.full[k] for k in self.SHARES[tag]])
        return None

    def carried(self, tag, landed):
        if tag in self.GATHERS:
            self._take(self.GATHERS[tag], landed)
        elif tag in self.PAIRS:
            self._pair_sums(self.PAIRS[tag], landed)
        elif tag in self.EXCHANGES:
            self._sum(self.EXCHANGES[tag], landed)
        elif tag in self.SHARES:
            self.other.update(zip(self.SHARES[tag], landed))

    def grads(self, tag, dw):
        for k, g in dw.items():
            self.mine[k] = g.reshape(NCHIP, 2, -1, g.shape[-1])
        if tag == "l0_mix":
            keys = tuple(dw)
            self._pair_sums(keys, _run_carried("grad_pair_exchange_last", _pair_exchange_copies([self.mine[k] for k in keys])))

    def finish(self):
        rest = tuple(k for k in self.full if k not in self.other)
        self.other.update(zip(rest, _run_carried("grad_pair_share_last", _pair_share_copies([self.full[k] for k in rest]))))
        return {k: (self.full[k], self.other[k]) for k in self.full}


def kernel(x, norm_mix_pre, norm_mix_post, norm_ffn_pre, norm_ffn_post, na_w_qkv, na_w_o, na_rpb, dil_w_qkv, dil_w_o, ffn_w_gate, ffn_w_up, ffn_w_down, loss_target, m_norm_mix_pre, m_norm_mix_post, m_norm_ffn_pre, m_norm_ffn_post, m_na_w_qkv, m_na_w_o, m_na_rpb, m_dil_w_qkv, m_dil_w_o, m_ffn_w_gate, m_ffn_w_up, m_ffn_w_down, v_norm_mix_pre, v_norm_mix_post, v_norm_ffn_pre, v_norm_ffn_post, v_na_w_qkv, v_na_w_o, v_na_rpb, v_dil_w_qkv, v_dil_w_o, v_ffn_w_gate, v_ffn_w_up, v_ffn_w_down):
    tr = lambda a: jnp.swapaxes(a, 1, 2)
    weights = {"na_w_qkv": na_w_qkv, "na_w_o": na_w_o, "dil_w_qkv": dil_w_qkv, "dil_w_o": dil_w_o,
               "ffn_w_gate": tr(ffn_w_gate), "ffn_w_up": tr(ffn_w_up), "ffn_w_down": ffn_w_down}
    m_in = {"na_w_qkv": m_na_w_qkv, "na_w_o": m_na_w_o, "dil_w_qkv": m_dil_w_qkv, "dil_w_o": m_dil_w_o,
            "ffn_w_gate": tr(m_ffn_w_gate), "ffn_w_up": tr(m_ffn_w_up), "ffn_w_down": m_ffn_w_down}
    v_in = {"na_w_qkv": v_na_w_qkv, "na_w_o": v_na_w_o, "dil_w_qkv": v_dil_w_qkv, "dil_w_o": v_dil_w_o,
            "ffn_w_gate": tr(v_ffn_w_gate), "ffn_w_up": tr(v_ffn_w_up), "ffn_w_down": v_ffn_w_down}

    ex = _Exchange({(n, l): weights[n][l] for n in weights for l in range(weights[n].shape[0])})
    norms = (norm_mix_pre, norm_mix_post, norm_ffn_pre, norm_ffn_post)
    loss_row, dx, dnorms, d_rpb = _local_step(x[0], loss_target[0], norms, na_rpb[0], ex)
    small, sent = _allreduce_small(_pack_small(dnorms, d_rpb, last=loss_row[0, 0]), ex.carry("allreduce_small"))
    ex.carried("allreduce_small", sent)
    full = ex.finish()
    loss = small[SMALL_ROWS - 1, 127]

    out_g, out_d, out_m, out_v = {}, {}, {}, {}
    operands = lambda n, l: (weights[n], *full[(n, l)], m_in[n], v_in[n])
    results = {n: _adamw(f"adamw_{n}", ex.place, [operands(n, 0)])[0] for n in weights if n not in FFN_NAMES}
    ffn = None
    for l in range(2):
        ffn = _adamw(f"adamw_ffn_{l}", ex.place, [operands(n, l) for n in FFN_NAMES], l, ffn)
    results.update(zip(FFN_NAMES, ffn))
    for n, res in results.items():
        if n in ("ffn_w_gate", "ffn_w_up"):
            res = [tr(r) for r in res]
        out_g[n], out_d[n], out_m[n], out_v[n] = res
    sm_names = ("norm_mix_pre", "norm_mix_post", "norm_ffn_pre", "norm_ffn_post", "na_rpb")
    sm = _adamw("adamw_small", jnp.zeros((2,), jnp.int32),
                [(_pack_small(norms, na_rpb)[None], small[:SMALL_ROWS // 2], small[SMALL_ROWS // 2:],
                  _pack_small((m_norm_mix_pre, m_norm_mix_post, m_norm_ffn_pre, m_norm_ffn_post), m_na_rpb)[None],
                  _pack_small((v_norm_mix_pre, v_norm_mix_post, v_norm_ffn_pre, v_norm_ffn_post), v_na_rpb)[None])])[0]
    for res, dst in zip(sm, (out_g, out_d, out_m, out_v)):
        ns, rp = _unpack_small(res)
        for n, a in zip(sm_names, ns + [rp]):
            dst[n] = a

    order = ("norm_mix_pre", "norm_mix_post", "norm_ffn_pre", "norm_ffn_post", "na_w_qkv", "na_w_o", "na_rpb", "dil_w_qkv", "dil_w_o",
             "ffn_w_gate", "ffn_w_up", "ffn_w_down")
    return (loss, dx[None], *[out_g[n] for n in order], *[out_d[n] for n in order], *[out_m[n] for n in order], *[out_v[n] for n in order])
```

```python
import functools

import numpy as np
import jax
import jax.numpy as jnp
from jax import lax
from jax.experimental import pallas as pl
from jax.experimental.pallas import tpu as pltpu

F32 = jnp.float32
BF16 = jnp.bfloat16

SEQ = 2048
DM = 1024
NH = 16
HD = 64
DFF = 2816
NCHIP = 4
FSH = DFF // NCHIP
GRID_W = 64
NA_QROWS = 4
NA_QB = NA_QROWS * GRID_W
NA_WROWS = 12
NA_WIN = NA_WROWS * GRID_W
DIL = (1, 4, 16)
DIL_QB = 256
DIL_WIN = DIL_QB + 128
DIL_RADIUS = 64
RMS_EPS = 1e-6
NEG = -1e30
QSCALE = HD ** -0.5
CH = 256
MESH = pl.DeviceIdType.MESH

ADAM_LR, ADAM_B1, ADAM_B2, ADAM_EPS, ADAM_WD, ADAM_STEP = 0.001, 0.9, 0.999, 1e-08, 0.01, 10

VMEM_LIMIT = 56 * 1024 * 1024

_NN = (((1,), (0,)), ((), ()))
_NT = (((1,), (1,)), ((), ()))
_TN = (((0,), (0,)), ((), ()))


def _params(sem):
    return pltpu.CompilerParams(dimension_semantics=sem, vmem_limit_bytes=VMEM_LIMIT)


def _matmul(name, pairs, grid, out_shape, out_spec, acc_shape, carrying=False, carry=None):
    nk = grid[-1]
    npair = len(pairs)
    n_in = 2 * npair

    def body(*refs):
        ins, o_ref = refs[:2 * npair], refs[n_in]
        part = None
        for p in range(npair):
            d = lax.dot_general(ins[2 * p][...].astype(BF16), ins[2 * p + 1][...].astype(BF16), pairs[p][4],
                                preferred_element_type=F32)
            part = d if part is None else part + d
        if nk == 1:
            o_ref[...] = part.astype(o_ref.dtype)
        else:
            acc_ref = refs[n_in + 1]
            kk = pl.program_id(len(grid) - 1)

            @pl.when(kk == 0)
            def _():
                acc_ref[...] = part

            @pl.when(kk > 0)
            def _():
                acc_ref[...] += part

            @pl.when(kk == nk - 1)
            def _():
                o_ref[...] = acc_ref[...].astype(o_ref.dtype)

    ops, specs = [], []
    for a, a_spec, b, b_spec, _ in pairs:
        ops += [a, b]
        specs += [a_spec, b_spec]
    (out,), sent = _carrier_call(name, body, grid, specs, [out_spec], [out_shape], [] if nk == 1 else [pltpu.VMEM(acc_shape, F32)], ops, carry)
    return (out, sent) if carrying else out


def _qkv_fwd(name, h_all, w4, carry):
    g_n = h_all.shape[0]
    per = w4.shape[2] // CH
    return _matmul(
        name, [(h_all, pl.BlockSpec((None, SEQ, DM), lambda g, q, k: (g, 0, 0)),
                w4, pl.BlockSpec((None, DM, CH), lambda g, q, k: ((g * 12 + q) // per, 0, (g * 12 + q) % per)), _NN)],
        (g_n, 12, 1), jax.ShapeDtypeStruct((g_n, SEQ, 3 * DM), BF16),
        pl.BlockSpec((None, SEQ, CH), lambda g, q, k: (g, 0, q)), None, carrying=True, carry=carry)


def _qkv_bwd_dh(name, dqkv, w4, carry):
    g_n = dqkv.shape[0]
    per = w4.shape[2] // CH
    tm = SEQ

    def pair(cb):
        chunk = lambda g, t: g * 12 + t * 4 + cb
        return (dqkv, pl.BlockSpec((None, None, tm, CH), lambda g, i, t: (g, t, i, cb)),
                w4, pl.BlockSpec((None, DM, CH), lambda g, i, t: (chunk(g, t) // per, 0, chunk(g, t) % per)), _NT)

    return _matmul(name, [pair(cb) for cb in range(4)], (g_n, SEQ // tm, 3), jax.ShapeDtypeStruct((g_n, SEQ, DM), F32),
                   pl.BlockSpec((None, tm, DM), lambda g, i, t: (g, i, 0)), (tm, DM), carrying=True, carry=carry)


def _qkv_bwd_dw(name, ht_all, dqkv, shard_cols, carry):
    g_n = dqkv.shape[0]
    per = shard_cols // CH
    return _matmul(
        name, [(ht_all, pl.BlockSpec((None, DM, SEQ), lambda qq, k: (qq // 12, 0, 0)),
                dqkv, pl.BlockSpec((None, None, SEQ, CH), lambda qq, k: (qq // 12, (qq % 12) // 4, 0, qq % 4)), _NN)],
        (g_n * 12, 1), jax.ShapeDtypeStruct((NCHIP, DM, shard_cols), BF16),
        pl.BlockSpec((None, DM, CH), lambda qq, k: (qq // per, 0, qq % per)), None, carrying=True, carry=carry)


def _proj_fwd(name, o, wo, x, g):
    tm = 512

    def body(o_ref, w_ref, x_ref, g_ref, xn_ref, u_ref):
        u = jnp.dot(o_ref[...], w_ref[...], preferred_element_type=F32)
        u_ref[...] = u
        r = lax.rsqrt(jnp.mean(u * u, axis=-1, keepdims=True) + RMS_EPS)
        xn_ref[...] = x_ref[...] + u * r * g_ref[...]

    rows = pl.BlockSpec((tm, DM), lambda i: (i, 0))
    sh = jax.ShapeDtypeStruct((SEQ, DM), F32)
    return pl.pallas_call(
        body, grid=(SEQ // tm,), in_specs=[rows, pl.BlockSpec((DM, DM), lambda i: (0, 0)), rows, pl.BlockSpec((1, DM), lambda i: (0, 0))],
        out_specs=[rows, rows], out_shape=[sh, sh], compiler_params=_params(("parallel",)), name=name)(o, wo, x, g)


def _proj_bwd(name, dy, u, g, wo, dtype, carry):
    tm = 512

    def body(dy_ref, u_ref, g_ref, w_ref, do_ref, du_ref, dg_ref):
        dy = dy_ref[...]
        u = u_ref[...]
        r = lax.rsqrt(jnp.mean(u * u, axis=-1, keepdims=True) + RMS_EPS)
        yh = u * r
        t = dy * g_ref[...]
        du = (r * (t - yh * jnp.mean(t * yh, axis=-1, keepdims=True))).astype(BF16)
        du_ref[...] = du
        do_ref[...] = lax.dot_general(du, w_ref[...], _NT, preferred_element_type=F32).astype(do_ref.dtype)

        @pl.when(pl.program_id(0) == 0)
        def _():
            dg_ref[...] = jnp.zeros_like(dg_ref)

        dg_ref[...] += jnp.sum(dy * yh, axis=0, keepdims=True)

    rows = pl.BlockSpec((tm, DM), lambda i: (i, 0))
    vec = pl.BlockSpec((1, DM), lambda i: (0, 0))
    return _carrier_call(
        name, body, (SEQ // tm,), [rows, rows, vec, pl.BlockSpec((DM, DM), lambda i: (0, 0))], [rows, rows, vec],
        [jax.ShapeDtypeStruct((SEQ, DM), dtype), jax.ShapeDtypeStruct((SEQ, DM), BF16), jax.ShapeDtypeStruct((1, DM), F32)],
        [], (dy, u, g, wo), carry)


def _proj_bwd_dw(name, o, du):
    tn = 512
    return _matmul(
        name, [(o, pl.BlockSpec((SEQ, DM), lambda j, k: (0, 0)), du, pl.BlockSpec((SEQ, tn), lambda j, k: (0, j)), _TN)],
        (DM // tn, 1), jax.ShapeDtypeStruct((DM, DM), BF16), pl.BlockSpec((DM, tn), lambda j, k: (0, j)), None)


def _ffn_wspec(index_map):
    return pl.BlockSpec((None, FSH, DM), index_map)


def _ffn_bwd_dw(name, a4, b):
    return _matmul(
        name, [(a4, pl.BlockSpec((None, SEQ, FSH), lambda s, k: (s, 0, 0)), b, pl.BlockSpec((SEQ, DM), lambda s, k: (0, 0)), _TN)],
        (NCHIP, 1), jax.ShapeDtypeStruct((NCHIP, FSH, DM), BF16), _ffn_wspec(lambda s, k: (s, 0, 0)), None)


ROWS = 256


def _row_spec():
    return pl.BlockSpec((ROWS, DM), lambda i: (i, 0))


def _vec_spec():
    return pl.BlockSpec((1, DM), lambda i: (0, 0))


def _rms_fwd(name, x, g, dtype=BF16):
    def body(x_ref, g_ref, o_ref):
        x = x_ref[...]
        r = lax.rsqrt(jnp.mean(x * x, axis=-1, keepdims=True) + RMS_EPS)
        o_ref[...] = (x * r * g_ref[...]).astype(o_ref.dtype)

    return pl.pallas_call(body, grid=(SEQ // ROWS,), in_specs=[_row_spec(), _vec_spec()], out_specs=_row_spec(),
                          out_shape=jax.ShapeDtypeStruct((SEQ, DM), dtype), compiler_params=_params(("parallel",)), name=name)(x, g)


def _rms_fwd_both(name, x, g):
    def body(x_ref, g_ref, o_ref, t_ref):
        x = x_ref[...]
        r = lax.rsqrt(jnp.mean(x * x, axis=-1, keepdims=True) + RMS_EPS)
        h = x * r * g_ref[...]
        o_ref[...] = h.astype(o_ref.dtype)
        t_ref[...] = h.T.astype(t_ref.dtype)

    return pl.pallas_call(
        body, grid=(SEQ // ROWS,), in_specs=[_row_spec(), _vec_spec()], out_specs=[_row_spec(), pl.BlockSpec((DM, ROWS), lambda i: (0, i))],
        out_shape=[jax.ShapeDtypeStruct((SEQ, DM), BF16), jax.ShapeDtypeStruct((DM, SEQ), BF16)],
        compiler_params=_params(("parallel",)), name=name)(x, g)


def _norm_bwd(name, dy, u, g, res, carry):
    def body(dy_ref, u_ref, g_ref, res_ref, du_ref, dg_ref):
        dy = dy_ref[...]
        u = u_ref[...]
        r = lax.rsqrt(jnp.mean(u * u, axis=-1, keepdims=True) + RMS_EPS)
        yh = u * r
        t = dy * g_ref[...]
        du_ref[...] = r * (t - yh * jnp.mean(t * yh, axis=-1, keepdims=True)) + res_ref[...]

        @pl.when(pl.program_id(0) == 0)
        def _():
            dg_ref[...] = jnp.zeros_like(dg_ref)

        dg_ref[...] += jnp.sum(dy * yh, axis=0, keepdims=True)

    return _carrier_call(
        name, body, (SEQ // ROWS,), [_row_spec(), _row_spec(), _vec_spec(), _row_spec()], [_row_spec(), _vec_spec()],
        [jax.ShapeDtypeStruct((SEQ, DM), F32), jax.ShapeDtypeStruct((1, DM), F32)], [], (dy, u, g, res), carry)


def _loss_grad(name, y, t):
    def body(y_ref, t_ref, dy_ref, l_ref):
        e = y_ref[...] - t_ref[...]
        dy_ref[...] = e * (1.0 / DM)

        @pl.when(pl.program_id(0) == 0)
        def _():
            l_ref[...] = jnp.zeros_like(l_ref)

        l_ref[...] += jnp.sum(e * e) * (0.5 / DM)

    return pl.pallas_call(
        body, grid=(SEQ // ROWS,), in_specs=[_row_spec(), _row_spec()],
        out_specs=[_row_spec(), pl.BlockSpec((1, 128), lambda i: (0, 0))],
        out_shape=[jax.ShapeDtypeStruct((SEQ, DM), F32), jax.ShapeDtypeStruct((1, 128), F32)],
        compiler_params=_params(("arbitrary",)), name=name)(y, t)


HBM_SPEC = pl.BlockSpec(memory_space=pltpu.HBM)


class _Carried:
    def __init__(self, ins, out_shapes, n_sems, issue, drain):
        self.ins, self.out_shapes, self.n_sems, self.issue, self.drain = list(ins), list(out_shapes), tuple(n_sems), issue, drain


def _carrier_call(name, body, grid, in_specs, out_specs, out_shape, scratch_shapes, operands, carry):
    n_in, n_out, n_scr = len(in_specs), len(out_specs), len(scratch_shapes)
    if carry is None:
        res = pl.pallas_call(body, grid=grid, in_specs=in_specs, out_specs=out_specs, out_shape=out_shape, scratch_shapes=scratch_shapes,
                             compiler_params=_params(("arbitrary",) * len(grid)), name=name)(*operands)
        return list(res), []
    ci, co = len(carry.ins), len(carry.out_shapes)

    def wrapped(*refs):
        ins, cins = refs[:n_in], refs[n_in:n_in + ci]
        outs, couts = refs[n_in + ci:n_in + ci + n_out], refs[n_in + ci + n_out:n_in + ci + n_out + co]
        scr, sems = refs[n_in + ci + n_out + co:n_in + ci + n_out + co + n_scr], refs[n_in + ci + n_out + co + n_scr:]
        first = functools.reduce(jnp.logical_and, [pl.program_id(a) == 0 for a in range(len(grid))])
        last = functools.reduce(jnp.logical_and, [pl.program_id(a) == grid[a] - 1 for a in range(len(grid))])

        @pl.when(first)
        def _():
            carry.issue(cins, couts, sems)

        body(*ins, *outs, *scr)

        @pl.when(last)
        def _():
            carry.drain(cins, couts, sems)

    res = pl.pallas_call(
        wrapped, grid=grid, in_specs=list(in_specs) + [HBM_SPEC] * ci, out_specs=list(out_specs) + [HBM_SPEC] * co,
        out_shape=list(out_shape) + carry.out_shapes,
        scratch_shapes=list(scratch_shapes) + [pltpu.SemaphoreType.DMA((k,)) for k in carry.n_sems],
        compiler_params=pltpu.CompilerParams(dimension_semantics=("arbitrary",) * len(grid), vmem_limit_bytes=VMEM_LIMIT, has_side_effects=True),
        name=name)(*operands, *carry.ins)
    return list(res[:n_out]), list(res[n_out:])


def _run_carried(name, carry):
    def body(*refs):
        ci, co = len(carry.ins), len(carry.out_shapes)
        carry.issue(refs[:ci], refs[ci:ci + co], refs[ci + co:])
        carry.drain(refs[:ci], refs[ci:ci + co], refs[ci + co:])

    return pl.pallas_call(
        body, in_specs=[HBM_SPEC] * len(carry.ins), out_specs=[HBM_SPEC] * len(carry.out_shapes), out_shape=carry.out_shapes,
        scratch_shapes=[pltpu.SemaphoreType.DMA((k,)) for k in carry.n_sems],
        compiler_params=pltpu.CompilerParams(has_side_effects=True), name=name)(*carry.ins)


NA_BLOCKS = SEQ // NA_QB
NA_ROWS_TOTAL = SEQ // GRID_W
NA_CLASSES = ((0, 0), (8, 4), (NA_ROWS_TOTAL - NA_QROWS, NA_ROWS_TOTAL - NA_WROWS))


def _na_pairs(i0, ws):
    out = []
    for qi in range(NA_QROWS):
        i = i0 + qi
        rs = min(max(i - 4, 0), NA_ROWS_TOTAL - 8)
        for kr in range(NA_WROWS):
            r = ws + kr
            if rs <= r < rs + 8:
                out.append((qi, kr, r - i + 7))
    return out


def _diag_onehot():
    qc, kc = np.meshgrid(np.arange(GRID_W), np.arange(GRID_W), indexing="ij")
    e = np.zeros((GRID_W * GRID_W, 128), np.float32)
    j = (kc - qc + 15).reshape(-1)
    ok = (j >= 0) & (j <= 30)
    e[np.arange(GRID_W * GRID_W)[ok], j[ok]] = 1.0
    return jnp.asarray(e)


def _rpb_expand(rpb):
    r2 = jnp.pad(rpb.reshape(NH * 15, 31), ((0, 0), (0, 128 - 31)))

    def body(r_ref, e_ref, o_ref):
        o_ref[...] = lax.dot_general(r_ref[...], e_ref[...], _NT, preferred_element_type=F32, precision=lax.Precision.HIGHEST)

    out = pl.pallas_call(body, out_shape=jax.ShapeDtypeStruct((NH * 15, GRID_W * GRID_W), F32), name="rpb_expand",
                         compiler_params=pltpu.CompilerParams(vmem_limit_bytes=VMEM_LIMIT))(r2, _diag_onehot())
    return out.reshape(NH, 15, GRID_W, GRID_W)


def _na_bias_tiles(rpb, carry):
    def body(b_ref, o_ref):
        qc = lax.broadcasted_iota(jnp.int32, (GRID_W, GRID_W), 0)
        kc = lax.broadcasted_iota(jnp.int32, (GRID_W, GRID_W), 1)
        first = jnp.clip(qc - 8, 0, GRID_W - 16)
        in_window = (kc >= first) & (kc < first + 16)
        neg = jnp.full((GRID_W, GRID_W), NEG, F32)
        for cls, (i0, ws) in enumerate(NA_CLASSES):
            @pl.when(pl.program_id(0) == cls)
            def _(i0=i0, ws=ws):
                pairs = {(qi, kr): dr for qi, kr, dr in _na_pairs(i0, ws)}
                masked = {dr: jnp.where(in_window, b_ref[dr], NEG) for dr in sorted(set(pairs.values()))}
                for qi in range(NA_QROWS):
                    for k2 in range(NA_WROWS // 2):
                        blocks = [masked[pairs[(qi, kr)]] if (qi, kr) in pairs else neg for kr in (2 * k2, 2 * k2 + 1)]
                        o_ref[qi * GRID_W:(qi + 1) * GRID_W, k2 * 128:(k2 + 1) * 128] = jnp.concatenate(blocks, axis=1)

    (tiles,), sent = _carrier_call(
        "na_bias_tiles", body, (3, NH), [pl.BlockSpec((None, 15, GRID_W, GRID_W), lambda c, h: (h, 0, 0, 0))],
        [pl.BlockSpec((None, None, NA_QB, NA_WIN), lambda c, h: (c, h, 0, 0))], [jax.ShapeDtypeStruct((3, NH, NA_QB, NA_WIN), F32)],
        [], (_rpb_expand(rpb),), carry)
    return tiles, sent


def _na_cls(b):
    return jnp.where(b == 0, 0, jnp.where(b == NA_BLOCKS - 1, 2, 1))


def _na_start(b):
    return pl.multiple_of(jnp.clip(b * NA_QROWS - 4, 0, NA_ROWS_TOTAL - NA_WROWS) * GRID_W, GRID_W)


NA_FWD_HPS = 16
NA_BWD_HPS = 4


def _na_in_specs(hps):
    lw = hps * HD
    nlw = DM // lw
    return [pl.BlockSpec((NA_QB, lw), lambda hp, b: (b, hp)),
            pl.BlockSpec((SEQ, lw), lambda hp, b: (0, nlw + hp)),
            pl.BlockSpec((SEQ, lw), lambda hp, b: (0, 2 * nlw + hp)),
            pl.BlockSpec((None, hps, NA_QB, NA_WIN), lambda hp, b: (_na_cls(b), hp, 0, 0))]


def _na_fwd(qkv, bias, carry):
    lw = NA_FWD_HPS * HD

    def body(q_ref, k_ref, v_ref, b_ref, o_ref):
        start = _na_start(pl.program_id(1))
        q = q_ref[...]
        kw = k_ref[pl.ds(start, NA_WIN), :]
        vw = v_ref[pl.ds(start, NA_WIN), :]
        outs = []
        for hh in range(NA_FWD_HPS):
            sl = slice(hh * HD, (hh + 1) * HD)
            s = lax.dot_general(q[:, sl] * QSCALE, kw[:, sl], _NT, preferred_element_type=F32) + b_ref[hh]
            p = jnp.exp(s - jnp.max(s, axis=-1, keepdims=True))
            l = jnp.sum(p, axis=-1, keepdims=True)
            outs.append(jnp.dot(p.astype(BF16), vw[:, sl], preferred_element_type=F32) / l)
        o_ref[...] = jnp.concatenate(outs, axis=1).astype(o_ref.dtype)

    (o,), sent = _carrier_call(
        "na_fwd", body, (NH // NA_FWD_HPS, NA_BLOCKS), _na_in_specs(NA_FWD_HPS), [pl.BlockSpec((NA_QB, lw), lambda hp, b: (b, hp))],
        [jax.ShapeDtypeStruct((SEQ, DM), BF16)], [], (qkv, qkv, qkv, bias), carry)
    return o, sent


def _na_bwd(qkv, bias, do, carry):
    lw = NA_BWD_HPS * HD

    def body(q_ref, k_ref, v_ref, b_ref, do_ref, dqkv_ref, z_ref, dk_acc, dv_acc):
        blk = pl.program_id(1)

        @pl.when(blk == 0)
        def _():
            dk_acc[...] = jnp.zeros_like(dk_acc)
            dv_acc[...] = jnp.zeros_like(dv_acc)
            z_ref[...] = jnp.zeros_like(z_ref)

        start = _na_start(blk)
        q = q_ref[...]
        do = do_ref[...]
        kw = k_ref[pl.ds(start, NA_WIN), :]
        vw = v_ref[pl.ds(start, NA_WIN), :]
        dqs, dks, dvs, dss = [], [], [], []
        for hh in range(NA_BWD_HPS):
            sl = slice(hh * HD, (hh + 1) * HD)
            qh = q[:, sl] * QSCALE
            s = lax.dot_general(qh, kw[:, sl], _NT, preferred_element_type=F32) + b_ref[hh]
            p = jnp.exp(s - jnp.max(s, axis=-1, keepdims=True))
            p = p / jnp.sum(p, axis=-1, keepdims=True)
            dp = lax.dot_general(do[:, sl], vw[:, sl], _NT, preferred_element_type=F32)
            ds = p * (dp - jnp.sum(p * dp, axis=-1, keepdims=True))
            dsb = ds.astype(BF16)
            dqs.append(jnp.dot(dsb, kw[:, sl], preferred_element_type=F32) * QSCALE)
            dks.append(lax.dot_general(qh, dsb, _TN, preferred_element_type=F32).T)
            dvs.append(lax.dot_general(do[:, sl], p.astype(BF16), _TN, preferred_element_type=F32).T)
            dss.append(ds)
        for cls, (i0, ws) in enumerate(NA_CLASSES):
            @pl.when(_na_cls(blk) == cls)
            def _(i0=i0, ws=ws):
                for hh, ds in enumerate(dss):
                    for qi, kr, dr in _na_pairs(i0, ws):
                        z_ref[hh, dr * GRID_W:(dr + 1) * GRID_W, :] += ds[qi * GRID_W:(qi + 1) * GRID_W, kr * GRID_W:(kr + 1) * GRID_W]
        dqkv_ref[0, pl.ds(pl.multiple_of(blk * NA_QB, NA_QB), NA_QB), :] = jnp.concatenate(dqs, axis=1).astype(dqkv_ref.dtype)
        dk_acc[pl.ds(start, NA_WIN), :] += jnp.concatenate(dks, axis=1)
        dv_acc[pl.ds(start, NA_WIN), :] += jnp.concatenate(dvs, axis=1)

        @pl.when(blk == NA_BLOCKS - 1)
        def _():
            dqkv_ref[1] = dk_acc[...].astype(dqkv_ref.dtype)
            dqkv_ref[2] = dv_acc[...].astype(dqkv_ref.dtype)

    (dqkv, z), sent = _carrier_call(
        "na_bwd", body, (NH // NA_BWD_HPS, NA_BLOCKS),
        _na_in_specs(NA_BWD_HPS) + [pl.BlockSpec((NA_QB, lw), lambda hp, b: (b, hp))],
        [pl.BlockSpec((3, SEQ, lw), lambda hp, b: (0, 0, hp)), pl.BlockSpec((NA_BWD_HPS, 15 * GRID_W, GRID_W), lambda hp, b: (hp, 0, 0))],
        [jax.ShapeDtypeStruct((3, SEQ, DM), BF16), jax.ShapeDtypeStruct((NH, 15 * GRID_W, GRID_W), F32)],
        [pltpu.VMEM((SEQ, lw), F32), pltpu.VMEM((SEQ, lw), F32)], (qkv, qkv, qkv, bias, do), carry)
    return dqkv, z, sent


def _rpb_grad(z):
    z2 = z.reshape(NH * 15, GRID_W * GRID_W)

    def body(z_ref, e_ref, o_ref):
        o_ref[...] = jnp.dot(z_ref[...], e_ref[...], preferred_element_type=F32, precision=lax.Precision.HIGHEST)

    out = pl.pallas_call(body, out_shape=jax.ShapeDtypeStruct((NH * 15, 128), F32), name="rpb_grad",
                         compiler_params=pltpu.CompilerParams(vmem_limit_bytes=VMEM_LIMIT))(z2, _diag_onehot())
    return out[:, :31].reshape(NH, 15, 31)


DIL_BLOCKS = SEQ // DIL_QB
DIL_FWD_HPS = 16
DIL_HPS = 8
DIL_LW = DIL_HPS * HD
DIL_NLW = DM // DIL_LW


COLS = 128


def _col_spec():
    return pl.BlockSpec((SEQ, COLS), lambda j: (0, j))


def _grp_spec():
    return pl.BlockSpec((3, SEQ, COLS), lambda j: (0, 0, j))


def _store_group_order(dst_ref, src_ref):
    for g, d in enumerate(DIL):
        n = SEQ // d
        for r in range(d):
            dst_ref[g, r * n:(r + 1) * n, :] = src_ref[pl.ds(r, n, stride=d), :].astype(dst_ref.dtype)


def _store_token_order(dst_ref, src_ref, g):
    d = DIL[g]
    n = SEQ // d
    for r in range(d):
        dst_ref[pl.ds(r, n, stride=d), :] = src_ref[g, r * n:(r + 1) * n, :].astype(dst_ref.dtype)


def _to_groups(name, a):
    def body(a_ref, o_ref, t_ref):
        _store_group_order(o_ref, a_ref)
        for g in range(3):
            t_ref[g] = o_ref[g].astype(F32).T.astype(t_ref.dtype)

    return pl.pallas_call(
        body, grid=(DM // COLS,), in_specs=[_col_spec()], out_specs=[_grp_spec(), pl.BlockSpec((3, COLS, SEQ), lambda j: (0, j, 0))],
        out_shape=[jax.ShapeDtypeStruct((3, SEQ, DM), BF16), jax.ShapeDtypeStruct((3, DM, SEQ), BF16)],
        compiler_params=_params(("parallel",)), name=name)(a)


def _from_groups_sum(name, a):
    def body(a_ref, o_ref, t1, t2):
        _store_token_order(t1, a_ref, 1)
        _store_token_order(t2, a_ref, 2)
        o_ref[...] = (a_ref[0] + t1[...]) + t2[...]

    return pl.pallas_call(body, grid=(DM // COLS,), in_specs=[_grp_spec()], out_specs=_col_spec(),
                          out_shape=jax.ShapeDtypeStruct((SEQ, DM), F32), scratch_shapes=[pltpu.VMEM((SEQ, COLS), F32)] * 2,
                          compiler_params=_params(("parallel",)), name=name)(a)


def _dil_start(b):
    return pl.multiple_of(jnp.clip(b * DIL_QB - DIL_RADIUS, 0, SEQ - DIL_WIN), DIL_RADIUS)


def _dil_neg_dist(g, ii, jj):
    shift = 11 - 2 * g
    dist = jnp.abs(ii - jj)
    valid = (dist <= DIL_RADIUS) & (jnp.right_shift(ii, shift) == jnp.right_shift(jj, shift))
    return jnp.where(valid, -dist.astype(F32), NEG)


def _dil_in_specs(hps):
    lw = hps * HD
    nlw = DM // lw
    return [pl.BlockSpec(memory_space=pltpu.SMEM),
            pl.BlockSpec((None, DIL_QB, lw), lambda g, hp, b: (g, b, hp)),
            pl.BlockSpec((None, SEQ, lw), lambda g, hp, b: (g, 0, nlw + hp)),
            pl.BlockSpec((None, SEQ, lw), lambda g, hp, b: (g, 0, 2 * nlw + hp))]


def _dil_fwd(qkv, slopes, carry):
    def body(sl_ref, q_ref, k_ref, v_ref, o_ref, lse_ref):
        g, hp, b = pl.program_id(0), pl.program_id(1), pl.program_id(2)
        start = _dil_start(b)
        neg_dist = _dil_neg_dist(g, b * DIL_QB + lax.broadcasted_iota(jnp.int32, (DIL_QB, DIL_WIN), 0),
                                 start + lax.broadcasted_iota(jnp.int32, (DIL_QB, DIL_WIN), 1))
        dil = jnp.left_shift(1, 2 * g).astype(F32)
        q = q_ref[...]
        kw = k_ref[pl.ds(start, DIL_WIN), :]
        vw = v_ref[pl.ds(start, DIL_WIN), :]
        outs, lses = [], []
        for hh in range(DIL_FWD_HPS):
            sl = slice(hh * HD, (hh + 1) * HD)
            s = lax.dot_general(q[:, sl] * QSCALE, kw[:, sl], _NT, preferred_element_type=F32)
            s = s + (sl_ref[hp * DIL_FWD_HPS + hh] * dil) * neg_dist
            m = jnp.max(s, axis=-1, keepdims=True)
            p = jnp.exp(s - m)
            l = jnp.sum(p, axis=-1, keepdims=True)
            outs.append(jnp.dot(p.astype(BF16), vw[:, sl], preferred_element_type=F32) / l)
            lses.append(jnp.broadcast_to(m + jnp.log(l), (DIL_QB, HD)))
        o_ref[...] = jnp.concatenate(outs, axis=1).astype(o_ref.dtype)
        lse_ref[...] = jnp.concatenate(lses, axis=1)

    ospec = pl.BlockSpec((None, DIL_QB, DIL_FWD_HPS * HD), lambda g, hp, b: (g, b, hp))
    (o, lse), sent = _carrier_call(
        "dil_fwd", body, (3, NH // DIL_FWD_HPS, DIL_BLOCKS), _dil_in_specs(DIL_FWD_HPS), [ospec, ospec],
        [jax.ShapeDtypeStruct((3, SEQ, DM), BF16), jax.ShapeDtypeStruct((3, SEQ, DM), F32)], [], (slopes, qkv, qkv, qkv), carry)
    return o, lse, sent


def _dil_merge(o_all, lse_all):
    def body(o_ref, l_ref, out_ref, lse_ref, o1, o2, l1, l2):
        for g, (ot, lt) in ((1, (o1, l1)), (2, (o2, l2))):
            _store_token_order(ot, o_ref, g)
            _store_token_order(lt, l_ref, g)
        la, lb, lc = l_ref[0], l1[...], l2[...]
        m = jnp.maximum(jnp.maximum(la, lb), lc)
        wa, wb, wc = jnp.exp(la - m), jnp.exp(lb - m), jnp.exp(lc - m)
        sw = (wa + wb) + wc
        out_ref[...] = (((wa * o_ref[0].astype(F32) + wb * o1[...]) + wc * o2[...]) / sw).astype(out_ref.dtype)
        lse_ref[...] = m + jnp.log(sw)

    return pl.pallas_call(
        body, grid=(DM // COLS,), in_specs=[_grp_spec(), _grp_spec()], out_specs=[_col_spec(), _col_spec()],
        out_shape=[jax.ShapeDtypeStruct((SEQ, DM), BF16), jax.ShapeDtypeStruct((SEQ, DM), F32)],
        scratch_shapes=[pltpu.VMEM((SEQ, COLS), F32)] * 4, compiler_params=_params(("parallel",)), name="dil_merge")(o_all, lse_all)


def _dil_bwd_prep(do, o, lse):
    heads = COLS // HD

    def body(do_ref, o_ref, lse_ref, dog_ref, ddr_ref, lser_ref, dd, grp):
        prod = do_ref[...] * o_ref[...].astype(F32)
        dd[...] = jnp.concatenate(
            [jnp.broadcast_to(jnp.sum(prod[:, h * HD:(h + 1) * HD], axis=-1, keepdims=True), (SEQ, HD)) for h in range(heads)], axis=1)
        _store_group_order(dog_ref, do_ref)
        for src, dst in ((dd, ddr_ref), (lse_ref, lser_ref)):
            _store_group_order(grp, src)
            for g in range(3):
                t = grp[g].T
                for h in range(heads):
                    dst[g, h] = t[h * HD:h * HD + 8, :]

    rows = jax.ShapeDtypeStruct((3, NH, 8, SEQ), F32)
    rspec = pl.BlockSpec((3, heads, 8, SEQ), lambda j: (0, j, 0, 0))
    return pl.pallas_call(
        body, grid=(DM // COLS,), in_specs=[_col_spec()] * 3, out_specs=[_grp_spec(), rspec, rspec],
        out_shape=[jax.ShapeDtypeStruct((3, SEQ, DM), BF16), rows, rows],
        scratch_shapes=[pltpu.VMEM((SEQ, COLS), F32), pltpu.VMEM((3, SEQ, COLS), F32)],
        compiler_params=_params(("parallel",)), name="dil_bwd_prep")(do, o, lse)


def _dil_bwd(qkv, do, dd, lse, slopes, carry):
    def body(sl_ref, q_ref, k_ref, v_ref, do_ref, dd_ref, lse_ref, dqkv_ref, dk_acc, dv_acc):
        g, hp, b = pl.program_id(0), pl.program_id(1), pl.program_id(2)

        @pl.when(b == 0)
        def _():
            dk_acc[...] = jnp.zeros_like(dk_acc)
            dv_acc[...] = jnp.zeros_like(dv_acc)

        start = _dil_start(b)
        neg_dist = _dil_neg_dist(g, b * DIL_QB + lax.broadcasted_iota(jnp.int32, (DIL_WIN, DIL_QB), 1),
                                 start + lax.broadcasted_iota(jnp.int32, (DIL_WIN, DIL_QB), 0))
        dil = jnp.left_shift(1, 2 * g).astype(F32)
        q = q_ref[...]
        do = do_ref[...]
        kw = k_ref[pl.ds(start, DIL_WIN), :]
        vw = v_ref[pl.ds(start, DIL_WIN), :]
        dqs, dks, dvs = [], [], []
        for hh in range(DIL_HPS):
            sl = slice(hh * HD, (hh + 1) * HD)
            qh = q[:, sl] * QSCALE
            st = lax.dot_general(kw[:, sl], qh, _NT, preferred_element_type=F32)
            st = st + (sl_ref[hp * DIL_HPS + hh] * dil) * neg_dist
            pt = jnp.exp(st - lse_ref[hh, 0:1, :])
            dpt = lax.dot_general(vw[:, sl], do[:, sl], _NT, preferred_element_type=F32)
            dst = (pt * (dpt - dd_ref[hh, 0:1, :])).astype(BF16)
            dqs.append(lax.dot_general(kw[:, sl], dst, _TN, preferred_element_type=F32).T * QSCALE)
            dks.append(jnp.dot(dst, qh, preferred_element_type=F32))
            dvs.append(jnp.dot(pt.astype(BF16), do[:, sl], preferred_element_type=F32))
        dqkv_ref[0, pl.ds(pl.multiple_of(b * DIL_QB, DIL_QB), DIL_QB), :] = jnp.concatenate(dqs, axis=1).astype(dqkv_ref.dtype)
        dk_acc[pl.ds(start, DIL_WIN), :] += jnp.concatenate(dks, axis=1)
        dv_acc[pl.ds(start, DIL_WIN), :] += jnp.concatenate(dvs, axis=1)

        @pl.when(b == DIL_BLOCKS - 1)
        def _():
            dqkv_ref[1] = dk_acc[...].astype(dqkv_ref.dtype)
            dqkv_ref[2] = dv_acc[...].astype(dqkv_ref.dtype)

    qspec = pl.BlockSpec((None, DIL_QB, DIL_LW), lambda g, hp, b: (g, b, hp))
    rspec = pl.BlockSpec((None, DIL_HPS, 8, DIL_QB), lambda g, hp, b: (g, hp, 0, b))
    (dqkv,), sent = _carrier_call(
        "dil_bwd", body, (3, DIL_NLW, DIL_BLOCKS), _dil_in_specs(DIL_HPS) + [qspec, rspec, rspec],
        [pl.BlockSpec((None, 3, SEQ, DIL_LW), lambda g, hp, b: (g, 0, 0, hp))], [jax.ShapeDtypeStruct((3, 3, SEQ, DM), BF16)],
        [pltpu.VMEM((SEQ, DIL_LW), F32), pltpu.VMEM((SEQ, DIL_LW), F32)], (slopes, qkv, qkv, qkv, do, dd, lse), carry)
    return dqkv, sent


def _ffn_fwd(name, x, g_pre, g_post, wgt4, wut4, wd4, carry):
    tm = 1024

    def body(x_ref, gpre_ref, gpost_ref, wg_ref, wu_ref, wd_ref, xn_ref, h_ref, gate_ref, up_ref, u_ref, acc):
        s = pl.program_id(1)

        @pl.when(s == 0)
        def _():
            x = x_ref[...]
            r = lax.rsqrt(jnp.mean(x * x, axis=-1, keepdims=True) + RMS_EPS)
            h_ref[...] = (x * r * gpre_ref[...]).astype(h_ref.dtype)

        h = h_ref[...]
        gate = lax.dot_general(h, wg_ref[...], _NT, preferred_element_type=F32).astype(BF16)
        up = lax.dot_general(h, wu_ref[...], _NT, preferred_element_type=F32).astype(BF16)
        gate_ref[...] = gate
        up_ref[...] = up
        gf = gate.astype(F32)
        act = (gf * jax.nn.sigmoid(gf) * up.astype(F32)).astype(BF16)
        part = jnp.dot(act, wd_ref[...], preferred_element_type=F32)

        @pl.when(s == 0)
        def _():
            acc[...] = part

        @pl.when(s > 0)
        def _():
            acc[...] += part

        @pl.when(s == NCHIP - 1)
        def _():
            u = acc[...]
            u_ref[...] = u
            r = lax.rsqrt(jnp.mean(u * u, axis=-1, keepdims=True) + RMS_EPS)
            xn_ref[...] = x_ref[...] + u * r * gpost_ref[...]

    rows = pl.BlockSpec((tm, DM), lambda i, s: (i, 0))
    vec = pl.BlockSpec((1, DM), lambda i, s: (0, 0))
    wspec = _ffn_wspec(lambda i, s: (s, 0, 0))
    mid = pl.BlockSpec((None, tm, FSH), lambda i, s: (s, i, 0))
    outs, sent = _carrier_call(
        name, body, (SEQ // tm, NCHIP), [rows, vec, vec, wspec, wspec, wspec], [rows, rows, mid, mid, rows],
        [jax.ShapeDtypeStruct((SEQ, DM), F32), jax.ShapeDtypeStruct((SEQ, DM), BF16), jax.ShapeDtypeStruct((NCHIP, SEQ, FSH), BF16),
         jax.ShapeDtypeStruct((NCHIP, SEQ, FSH), BF16), jax.ShapeDtypeStruct((SEQ, DM), F32)],
        [pltpu.VMEM((tm, DM), F32)], (x, g_pre, g_post, wgt4, wut4, wd4), carry)
    return outs, sent


def _ffn_block(layer, x, g_pre, g_post, ex):
    tag = f"l{layer}_ffn_fwd"
    (x_new, h, gate, up, u), sent = _ffn_fwd(tag, x, g_pre, g_post, ex.weight(("ffn_w_gate", layer)), ex.weight(("ffn_w_up", layer)),
                                             ex.weight(("ffn_w_down", layer)), ex.carry(tag))
    ex.carried(tag, sent)
    return x_new, (x, h, gate, up, u)


def _ffn_bwd(name, dx, x, gate, up, u, g_pre, g_post, wgt4, wut4, wd4, carry):
    tm = 512

    def body(dx_ref, x_ref, gate_ref, up_ref, u_ref, gpre_ref, gpost_ref, wg_ref, wu_ref, wd_ref,
             dxin_ref, du_ref, dgate_ref, dup_ref, act_ref, dgpre_ref, dgpost_ref, dh_acc):
        i, s = pl.program_id(0), pl.program_id(1)

        @pl.when((i == 0) & (s == 0))
        def _():
            dgpre_ref[...] = jnp.zeros_like(dgpre_ref)
            dgpost_ref[...] = jnp.zeros_like(dgpost_ref)

        @pl.when(s == 0)
        def _():
            dy = dx_ref[...]
            uu = u_ref[...]
            r = lax.rsqrt(jnp.mean(uu * uu, axis=-1, keepdims=True) + RMS_EPS)
            yh = uu * r
            t = dy * gpost_ref[...]
            du_ref[...] = (r * (t - yh * jnp.mean(t * yh, axis=-1, keepdims=True))).astype(du_ref.dtype)
            dgpost_ref[...] += jnp.sum(dy * yh, axis=0, keepdims=True)

        dact = lax.dot_general(du_ref[...], wd_ref[...], _NT, preferred_element_type=F32)
        g = gate_ref[...].astype(F32)
        upv = up_ref[...].astype(F32)
        sg = jax.nn.sigmoid(g)
        dgate = (dact * upv * sg * (1.0 + g * (1.0 - sg))).astype(BF16)
        dup = (dact * g * sg).astype(BF16)
        dgate_ref[...] = dgate
        dup_ref[...] = dup
        act_ref[...] = (g * sg * upv).astype(act_ref.dtype)
        part = jnp.dot(dgate, wg_ref[...], preferred_element_type=F32) + jnp.dot(dup, wu_ref[...], preferred_element_type=F32)

        @pl.when(s == 0)
        def _():
            dh_acc[...] = part

        @pl.when(s > 0)
        def _():
            dh_acc[...] += part

        @pl.when(s == NCHIP - 1)
        def _():
            dh = dh_acc[...]
            xx = x_ref[...]
            r = lax.rsqrt(jnp.mean(xx * xx, axis=-1, keepdims=True) + RMS_EPS)
            yh = xx * r
            t = dh * gpre_ref[...]
            dxin_ref[...] = dx_ref[...] + r * (t - yh * jnp.mean(t * yh, axis=-1, keepdims=True))
            dgpre_ref[...] += jnp.sum(dh * yh, axis=0, keepdims=True)

    rows = pl.BlockSpec((tm, DM), lambda i, s: (i, 0))
    vec = pl.BlockSpec((1, DM), lambda i, s: (0, 0))
    wspec = _ffn_wspec(lambda i, s: (s, 0, 0))
    mid = pl.BlockSpec((None, tm, FSH), lambda i, s: (s, i, 0))
    mid_shape = jax.ShapeDtypeStruct((NCHIP, SEQ, FSH), BF16)
    return _carrier_call(
        name, body, (SEQ // tm, NCHIP), [rows, rows, mid, mid, rows, vec, vec, wspec, wspec, wspec], [rows, rows, mid, mid, mid, vec, vec],
        [jax.ShapeDtypeStruct((SEQ, DM), F32), jax.ShapeDtypeStruct((SEQ, DM), BF16), mid_shape, mid_shape, mid_shape,
         jax.ShapeDtypeStruct((1, DM), F32), jax.ShapeDtypeStruct((1, DM), F32)],
        [pltpu.VMEM((tm, DM), F32)], (dx, x, gate, up, u, g_pre, g_post, wgt4, wut4, wd4), carry)


def _ffn_block_bwd(layer, dx, saved, g_pre, g_post, ex):
    tag = f"l{layer}"
    x, h, gate, up, u = saved
    (dx_in, du, dgate, dup, act, dg_pre, dg_post), sent = _ffn_bwd(
        f"{tag}_ffn_bwd", dx, x, gate, up, u, g_pre, g_post, ex.weight(("ffn_w_gate", layer)), ex.weight(("ffn_w_up", layer)),
        ex.weight(("ffn_w_down", layer)), ex.carry(f"{tag}_ffn_bwd"))
    ex.carried(f"{tag}_ffn_bwd", sent)
    d_wd = _ffn_bwd_dw(f"{tag}_dwd", act, du)
    d_wg = _ffn_bwd_dw(f"{tag}_dwg", dgate, h)
    d_wu = _ffn_bwd_dw(f"{tag}_dwu", dup, h)
    ex.grads(f"{tag}_ffn", {("ffn_w_gate", layer): d_wg, ("ffn_w_up", layer): d_wu, ("ffn_w_down", layer): d_wd})
    return dx_in, dg_pre, dg_post


def _alibi_slopes():
    return 2.0 ** (-8.0 * jnp.arange(1, NH + 1, dtype=F32) / NH)


def _local_step(x, target, norms, rpb, ex):
    g_mix_pre, g_mix_post, g_ffn_pre, g_ffn_post = norms
    row = lambda a, i: a[i:i + 1]

    bias, sent = _na_bias_tiles(rpb, ex.carry("na_bias_tiles"))
    ex.carried("na_bias_tiles", sent)
    h0, h0t = _rms_fwd_both("l0_mix_pre", x, row(g_mix_pre, 0))
    qkv0, sent = _qkv_fwd("l0_qkv", h0[None], ex.weight(("na_w_qkv", 0)), ex.carry("l0_qkv"))
    ex.carried("l0_qkv", sent)
    o0, sent = _na_fwd(qkv0[0], bias, ex.carry("na_fwd"))
    ex.carried("na_fwd", sent)
    na_wo = ex.weight(("na_w_o", 0)).reshape(DM, DM)
    x1, u0 = _proj_fwd("l0_proj", o0, na_wo, x, row(g_mix_post, 0))
    x2, ffn0 = _ffn_block(0, x1, row(g_ffn_pre, 0), row(g_ffn_post, 0), ex)

    slopes = _alibi_slopes()
    h2g, h2gt = _to_groups("l1_h_groups", _rms_fwd("l1_mix_pre", x2, row(g_mix_pre, 1), F32))
    dil_wqkv = ex.weight(("dil_w_qkv", 0))
    qkv1, sent = _qkv_fwd("l1_qkv", h2g, dil_wqkv, ex.carry("l1_qkv"))
    ex.carried("l1_qkv", sent)
    og, lg, sent = _dil_fwd(qkv1, slopes, ex.carry("dil_fwd"))
    ex.carried("dil_fwd", sent)
    o1, lse = _dil_merge(og, lg)
    dil_wo = ex.weight(("dil_w_o", 0)).reshape(DM, DM)
    x3, u1 = _proj_fwd("l1_proj", o1, dil_wo, x2, row(g_mix_post, 1))
    x4, ffn1 = _ffn_block(1, x3, row(g_ffn_pre, 1), row(g_ffn_post, 1), ex)

    dx4, loss_row = _loss_grad("loss", x4, target)

    dx3, dg_fpre1, dg_fpost1 = _ffn_block_bwd(1, dx4, ffn1, row(g_ffn_pre, 1), row(g_ffn_post, 1), ex)
    (do1, du1, dg_mpost1), sent = _proj_bwd("l1_proj_bwd", dx3, u1, row(g_mix_post, 1), dil_wo, F32, ex.carry("l1_proj_bwd"))
    ex.carried("l1_proj_bwd", sent)
    d_dil_wo = _proj_bwd_dw("l1_dwo", o1, du1)
    dog, ddg, lseg = _dil_bwd_prep(do1, o1, lse)
    dqkv1, sent = _dil_bwd(qkv1, dog, ddg, lseg, slopes, ex.carry("dil_bwd"))
    ex.carried("dil_bwd", sent)
    d_dil_wqkv, sent = _qkv_bwd_dw("l1_dwqkv", h2gt, dqkv1, dil_wqkv.shape[2], ex.carry("l1_dwqkv"))
    ex.carried("l1_dwqkv", sent)
    ex.grads("l1_mix", {("dil_w_qkv", 0): d_dil_wqkv, ("dil_w_o", 0): d_dil_wo.reshape(NCHIP, DM // NCHIP, DM)})
    dh2g, sent = _qkv_bwd_dh("l1_dh", dqkv1, dil_wqkv, ex.carry("l1_dh"))
    ex.carried("l1_dh", sent)
    dh2 = _from_groups_sum("l1_dh_tokens", dh2g)
    (dx2, dg_mpre1), sent = _norm_bwd("l1_mix_pre_bwd", dh2, x2, row(g_mix_pre, 1), dx3, ex.carry("l1_mix_pre_bwd"))
    ex.carried("l1_mix_pre_bwd", sent)

    dx1, dg_fpre0, dg_fpost0 = _ffn_block_bwd(0, dx2, ffn0, row(g_ffn_pre, 0), row(g_ffn_post, 0), ex)
    (do0, du0, dg_mpost0), sent = _proj_bwd("l0_proj_bwd", dx1, u0, row(g_mix_post, 0), na_wo, BF16, ex.carry("l0_proj_bwd"))
    ex.carried("l0_proj_bwd", sent)
    d_na_wo = _proj_bwd_dw("l0_dwo", o0, du0)
    dqkv0, z, sent = _na_bwd(qkv0[0], bias, do0, ex.carry("na_bwd"))
    ex.carried("na_bwd", sent)
    d_rpb = _rpb_grad(z)
    na_wqkv = ex.weight(("na_w_qkv", 0))
    d_na_wqkv, sent = _qkv_bwd_dw("l0_dwqkv", h0t[None], dqkv0[None], na_wqkv.shape[2], ex.carry("l0_dwqkv"))
    ex.carried("l0_dwqkv", sent)
    ex.grads("l0_mix", {("na_w_qkv", 0): d_na_wqkv, ("na_w_o", 0): d_na_wo.reshape(NCHIP, DM // NCHIP, DM)})
    dh0, sent = _qkv_bwd_dh("l0_dh", dqkv0[None], na_wqkv, ex.carry("l0_dh"))
    ex.carried("l0_dh", sent)
    (dx0, dg_mpre0), sent = _norm_bwd("l0_mix_pre_bwd", dh0[0], x, row(g_mix_pre, 0), dx1, ex.carry("l0_mix_pre_bwd"))
    ex.carried("l0_mix_pre_bwd", sent)

    dnorms = (jnp.concatenate([dg_mpre0, dg_mpre1]), jnp.concatenate([dg_mpost0, dg_mpost1]),
              jnp.concatenate([dg_fpre0, dg_fpre1]), jnp.concatenate([dg_fpost0, dg_fpost1]))
    return loss_row, dx0, dnorms, d_rpb


def _place():
    x, y, c = lax.axis_index("x"), lax.axis_index("y"), lax.axis_index("c")
    chips = ((1 - x, y), (x, 1 - y), (1 - x, 1 - y))
    return x, y, c, chips


def _chip_id(chip):
    return 2 * chip[0] + chip[1]


def _gather_copies(shards):
    n = len(shards)

    def copies(src, out, sems):
        send_sems, recv_sems = sems
        x, y, c, chips = _place()

        def copy(t, k, chip, half, to, from_src=False):
            blk = out[t].at[_chip_id(chip), half]
            return pltpu.make_async_remote_copy(
                src_ref=src[t].at[half] if from_src else blk, dst_ref=blk,
                send_sem=send_sems.at[6 * t + k], recv_sem=recv_sems.at[6 * t + k], device_id=to, device_id_type=MESH)

        return copy, x, y, c, chips

    def issue(src, out, sems):
        copy, x, y, c, chips = copies(src, out, sems)
        for t in range(n):
            for j, chip in enumerate(chips):
                copy(t, j, (x, y), c, (*chip, c), from_src=True).start()

    def drain(src, out, sems):
        copy, x, y, c, chips = copies(src, out, sems)
        passed = []
        for t in range(n):
            for j, chip in enumerate(chips):
                copy(t, j, chip, c, (x, y, c)).wait_recv()
                fwd = copy(t, 3 + j, chip, c, (x, y, 1 - c))
                fwd.start()
                passed.append(fwd)
        for t in range(n):
            for j, chip in enumerate(chips):
                copy(t, 3 + j, chip, 1 - c, (x, y, c)).wait_recv()
        for t in range(n):
            for j, chip in enumerate(chips):
                copy(t, j, (x, y), c, (*chip, c), from_src=True).wait_send()
        for cp in passed:
            cp.wait_send()

    return _Carried(shards, [jax.ShapeDtypeStruct((NCHIP,) + s.shape, s.dtype) for s in shards], (6 * n, 6 * n), issue, drain)


def _pair_exchange_copies(grads):
    n = len(grads)

    def copies(g, theirs, sems):
        send_sems, recv_sems = sems
        x, y, c, _ = _place()
        return [pltpu.make_async_remote_copy(src_ref=g[t].at[:, 1 - c], dst_ref=theirs[t], send_sem=send_sems.at[t],
                                             recv_sem=recv_sems.at[t], device_id=(x, y, 1 - c), device_id_type=MESH) for t in range(n)]

    def issue(g, theirs, sems):
        for cp in copies(g, theirs, sems):
            cp.start()

    def drain(g, theirs, sems):
        for cp in copies(g, theirs, sems):
            cp.wait()

    return _Carried(grads, [jax.ShapeDtypeStruct((NCHIP,) + g.shape[2:], g.dtype) for g in grads], (n, n), issue, drain)


def _chip_exchange_copies(items):
    flat = [(t, i, j) for t, (_, peers) in enumerate(items) for i, j in enumerate(peers)]

    def copies(p, slots, sems):
        send_sems, recv_sems = sems
        x, y, c, chips = _place()
        return [pltpu.make_async_remote_copy(src_ref=p[t].at[_chip_id(chips[j])], dst_ref=slots[t].at[i], send_sem=send_sems.at[k],
                                             recv_sem=recv_sems.at[k], device_id=(*chips[j], c), device_id_type=MESH)
                for k, (t, i, j) in enumerate(flat)]

    def issue(p, slots, sems):
        for cp in copies(p, slots, sems):
            cp.start()

    def drain(p, slots, sems):
        for cp in copies(p, slots, sems):
            cp.wait()

    return _Carried([p for p, _ in items], [jax.ShapeDtypeStruct((len(peers),) + p.shape[1:], p.dtype) for p, peers in items],
                    (len(flat), len(flat)), issue, drain)


def _pair_share_copies(halves):
    n = len(halves)

    def copies(h, other, sems):
        send_sems, recv_sems = sems
        x, y, c, _ = _place()
        return [pltpu.make_async_remote_copy(src_ref=h[t], dst_ref=other[t], send_sem=send_sems.at[t], recv_sem=recv_sems.at[t],
                                             device_id=(x, y, 1 - c), device_id_type=MESH) for t in range(n)]

    def issue(h, other, sems):
        for cp in copies(h, other, sems):
            cp.start()

    def drain(h, other, sems):
        for cp in copies(h, other, sems):
            cp.wait()

    return _Carried(halves, [jax.ShapeDtypeStruct(h.shape, h.dtype) for h in halves], (n, n), issue, drain)


SMALL_ROWS = 128


def _allreduce_small(v, carry):
    ci, co = len(carry.ins), len(carry.out_shapes)

    def body(*refs):
        v_ref, cins, o_ref, couts = refs[0], refs[1:1 + ci], refs[1 + ci], refs[2 + ci:2 + ci + co]
        buf, send_sems, recv_sems = refs[2 + ci + co:5 + ci + co]
        csems = refs[5 + ci + co:]
        carry.issue(cins, couts, csems)
        x, y, c, _ = _place()
        me = 4 * x + 2 * y + c
        flip = lambda a, f: 1 - a if f else a
        buf[me] = v_ref[...]
        peers = [(flip(x, d >> 2 & 1), flip(y, d >> 1 & 1), flip(c, d & 1)) for d in range(1, 8)]
        sends = [pltpu.make_async_remote_copy(src_ref=v_ref, dst_ref=buf.at[me], send_sem=send_sems.at[i], recv_sem=recv_sems.at[i],
                                              device_id=peer, device_id_type=MESH) for i, peer in enumerate(peers)]
        for cp in sends:
            cp.start()
        for i, (px, py, pc) in enumerate(peers):
            pltpu.make_async_remote_copy(src_ref=v_ref, dst_ref=buf.at[4 * px + 2 * py + pc], send_sem=send_sems.at[i], recv_sem=recv_sems.at[i],
                                         device_id=(px, py, pc), device_id_type=MESH).wait_recv()
        for cp in sends:
            cp.wait_send()
        acc = buf[0]
        for k in range(1, 8):
            acc = acc + buf[k]
        o_ref[...] = acc
        carry.drain(cins, couts, csems)

    vm = pl.BlockSpec(memory_space=pltpu.VMEM)
    res = pl.pallas_call(
        body, in_specs=[vm] + [HBM_SPEC] * ci, out_specs=[vm] + [HBM_SPEC] * co,
        out_shape=[jax.ShapeDtypeStruct((SMALL_ROWS, 128), F32)] + carry.out_shapes,
        scratch_shapes=[pltpu.VMEM((8, SMALL_ROWS, 128), F32), pltpu.SemaphoreType.DMA((7,)), pltpu.SemaphoreType.DMA((7,))]
        + [pltpu.SemaphoreType.DMA((k,)) for k in carry.n_sems],
        compiler_params=pltpu.CompilerParams(has_side_effects=True), name="allreduce_small")(v, *carry.ins)
    return res[0], list(res[1:])


def _row_block(rows, cols, budget=3 << 19):
    best = 8
    for bm in range(8, rows + 1, 8):
        if rows % bm == 0 and bm * cols * 4 <= budget:
            best = bm
    return best


def _pair_sum(name, place, gs, theirs):
    n = len(gs)
    _, m, c = theirs[0].shape
    bm = _row_block(m, c)

    def body(place_ref, *refs):
        for a_ref, b_ref, o_ref in zip(refs[:n], refs[n:2 * n], refs[2 * n:]):
            o_ref[...] = (a_ref[...].astype(F32) + b_ref[...].astype(F32)).astype(o_ref.dtype)

    spec = pl.BlockSpec((None, bm, c), lambda k, i, pr: (k, i, 0))
    return pl.pallas_call(
        body, out_shape=[jax.ShapeDtypeStruct(theirs[0].shape, BF16)] * n,
        grid_spec=pltpu.PrefetchScalarGridSpec(
            num_scalar_prefetch=1, grid=(NCHIP, m // bm),
            in_specs=[pl.BlockSpec((None, None, bm, c), lambda k, i, pr: (k, pr[0], i, 0))] * n + [spec] * n, out_specs=[spec] * n),
        compiler_params=_params(("parallel", "parallel")), name=name)(place, *gs, *theirs)


def _chip_sum(name, place, parts, slots):
    n, ns = len(parts), len(slots[0])
    _, m, c = parts[0].shape
    bm = _row_block(m, c)

    def body(place_ref, *refs):
        for t in range(n):
            acc = refs[t][...].astype(F32)
            for s_ref in refs[n + t * ns:n + (t + 1) * ns]:
                for i in range(s_ref.shape[0]):
                    acc = acc + s_ref[i].astype(F32)
            refs[n + n * ns + t][...] = acc

    half = pl.BlockSpec((bm, c), lambda i, pr: (i, 0))
    return pl.pallas_call(
        body, out_shape=[jax.ShapeDtypeStruct((m, c), F32)] * n,
        grid_spec=pltpu.PrefetchScalarGridSpec(
            num_scalar_prefetch=1, grid=(m // bm,),
            in_specs=[pl.BlockSpec((None, bm, c), lambda i, pr: (pr[1], i, 0))] * n
            + [pl.BlockSpec((s.shape[0], bm, c), lambda i, pr: (0, i, 0)) for group in slots for s in group],
            out_specs=[half] * n),
        compiler_params=_params(("parallel",)), name=name)(place, *parts, *[s for group in slots for s in group])


def _adamw(name, place, tensors, layer=0, into=None):
    n = len(tensors)
    lead, rows, cols = tensors[0][0].shape
    bm = _row_block(rows // 2, cols, budget=768 * 1024 // n)
    per_half = rows // 2 // bm
    c1 = 1.0 - ADAM_B1 ** ADAM_STEP
    c2 = 1.0 - ADAM_B2 ** ADAM_STEP

    def body(place_ref, *refs):
        outs = refs[len(refs) - 4 * n:]
        for t in range(n):
            w_ref, ga_ref, gb_ref, m_ref, v_ref = refs[5 * t:5 * t + 5]
            go_ref, d_ref, mo_ref, vo_ref = outs[4 * t:4 * t + 4]
            g = jnp.where(pl.program_id(0) // per_half == place_ref[0], ga_ref[...], gb_ref[...])
            mn = ADAM_B1 * m_ref[...] + (1.0 - ADAM_B1) * g
            vn = ADAM_B2 * v_ref[...] + (1.0 - ADAM_B2) * (g * g)
            go_ref[...] = g
            mo_ref[...] = mn
            vo_ref[...] = vn
            d_ref[...] = -ADAM_LR * ((mn / c1) / (jnp.sqrt(vn / c2) + ADAM_EPS) + ADAM_WD * w_ref[...])

    spec = pl.BlockSpec((None, bm, cols), lambda i, pr: (layer, i, 0))

    def half_spec(mine):
        def index(i, pr):
            first = (pr[0] == 0) == mine
            park = jnp.where(first, per_half - 1, 0)
            return jnp.where((i < per_half) == first, i % per_half, park), 0
        return pl.BlockSpec((bm, cols), index)
    sh = jax.ShapeDtypeStruct((lead, rows, cols), F32)
    prev = [] if into is None else [a for res in into for a in res]
    res = pl.pallas_call(
        body, out_shape=[sh] * (4 * n), input_output_aliases={1 + 5 * n + k: k for k in range(len(prev))},
        grid_spec=pltpu.PrefetchScalarGridSpec(
            num_scalar_prefetch=1, grid=(rows // bm,),
            in_specs=[spec, half_spec(True), half_spec(False), spec, spec] * n + [pl.BlockSpec(memory_space=pl.ANY)] * len(prev),
            out_specs=[spec] * (4 * n)),
        compiler_params=_params(("parallel",)), name=name)(place, *[a for t in tensors for a in t], *prev)
    return [res[4 * t:4 * t + 4] for t in range(n)]


def _pack_small(norms, rpb, last=None):
    flat = jnp.concatenate([a.reshape(-1) for a in norms] + [rpb.reshape(-1)])
    flat = jnp.pad(flat, (0, SMALL_ROWS * 128 - flat.shape[0]))
    if last is not None:
        flat = lax.dynamic_update_slice(flat, last.reshape(1), (flat.shape[0] - 1,))
    return flat.reshape(SMALL_ROWS, 128)


def _unpack_small(p):
    flat = p.reshape(-1)
    norms = [flat[i * 2 * DM:(i + 1) * 2 * DM].reshape(2, DM) for i in range(4)]
    rpb = flat[8 * DM:8 * DM + NH * 15 * 31].reshape(1, NH, 15, 31)
    return norms, rpb


FFN_NAMES = ("ffn_w_gate", "ffn_w_up", "ffn_w_down")
L0_FFN = tuple((n, 0) for n in FFN_NAMES)
L1_FFN = tuple((n, 1) for n in FFN_NAMES)
NA_KEYS = (("na_w_qkv", 0), ("na_w_o", 0))
DIL_KEYS = (("dil_w_qkv", 0), ("dil_w_o", 0))
ALL_PEERS, NEIGHBOURS, DIAGONAL = (0, 1, 2), (0, 1), (2,)


class _Exchange:
    GATHERS = {"na_bias_tiles": NA_KEYS, "l0_qkv": L0_FFN[:1], "na_fwd": L0_FFN[1:], "l0_ffn_fwd": DIL_KEYS[:1], "dil_fwd": L1_FFN + DIL_KEYS[1:]}
    PAIRS = {"l1_proj_bwd": L1_FFN, "l1_dh": DIL_KEYS, "l0_proj_bwd": L0_FFN}
    EXCHANGES = {"dil_bwd": [(k, ALL_PEERS) for k in L1_FFN],
                 "l0_ffn_bwd": [(DIL_KEYS[0], NEIGHBOURS), (DIL_KEYS[1], ALL_PEERS)],
                 "na_bwd": [(k, ALL_PEERS) for k in L0_FFN] + [(DIL_KEYS[0], DIAGONAL)],
                 "l0_dh": [(k, NEIGHBOURS) for k in NA_KEYS],
                 "allreduce_small": [(k, DIAGONAL) for k in NA_KEYS]}
    SHARES = {"l1_dwqkv": L1_FFN, "l0_dwqkv": L0_FFN + DIL_KEYS}

    def __init__(self, shards):
        self.chip = 2 * lax.axis_index("x") + lax.axis_index("y")
        self.place = jnp.stack([lax.axis_index("c"), self.chip]).astype(jnp.int32)
        self.own = {k: s.reshape(2, s.shape[0] // 2, s.shape[1]).astype(BF16) for k, s in shards.items()}
        self.gathered, self.mine, self.parts, self.slots, self.full, self.other = {}, {}, {}, {}, {}, {}

    def _take(self, keys, landed):
        for k, gw in zip(keys, landed):
            self.gathered[k] = lax.dynamic_update_slice(gw, self.own[k][None], (self.chip, 0, 0, 0))

    def _sum(self, items, landed):
        runs = []
        for (k, peers), s in zip(items, landed):
            got = self.slots.setdefault(k, {})
            got[peers] = s
            if sum(len(p) for p in got) == len(ALL_PEERS):
                like = (self.parts[k].shape, tuple(sorted(got)))
                if runs and runs[-1][0] == like:
                    runs[-1][1].append(k)
                else:
                    runs.append((like, [k]))
        for (_, split), ks in runs:
            sums = _chip_sum(f"chip_sum_{ks[0][0]}_{ks[0][1]}", self.place, [self.parts[k] for k in ks],
                             [[self.slots[k][p] for p in split] for k in ks])
            self.full.update(zip(ks, sums))

    def weight(self, key):
        g = self.gathered[key]
        return g.reshape(NCHIP, 2 * g.shape[2], g.shape[3])

    def _pair_sums(self, keys, theirs):
        runs = []
        for k, t in zip(keys, theirs):
            if runs and runs[-1][0][1].shape == t.shape:
                runs[-1].append((k, t))
            else:
                runs.append([(k, t)])
        for run in runs:
            ks = [k for k, _ in run]
            sums = _pair_sum(f"pair_sum_{ks[0][0]}_{ks[0][1]}", self.place, [self.mine[k] for k in ks], [t for _, t in run])
            self.parts.update(zip(ks, sums))

    def carry(self, tag):
        if tag in self.GATHERS:
            return _gather_copies([self.own[k] for k in self.GATHERS[tag]])
        if tag in self.PAIRS:
            return _pair_exchange_copies([self.mine[k] for k in self.PAIRS[tag]])
        if tag in self.EXCHANGES:
            return _chip_exchange_copies([(self.parts[k], peers) for k, peers in self.EXCHANGES[tag]])
        if tag in self.SHARES:
            return _pair_share_copies([self.full[k] for k in self.SHARES[tag]])
        return None

    def carried(self, tag, landed):
        if tag in self.GATHERS:
            self._take(self.GATHERS[tag], landed)
        elif tag in self.PAIRS:
            self._pair_sums(self.PAIRS[tag], landed)
        elif tag in self.EXCHANGES:
            self._sum(self.EXCHANGES[tag], landed)
        elif tag in self.SHARES:
            self.other.update(zip(self.SHARES[tag], landed))

    def grads(self, tag, dw):
        for k, g in dw.items():
            self.mine[k] = g.reshape(NCHIP, 2, -1, g.shape[-1])
        if tag == "l0_mix":
            keys = tuple(dw)
            self._pair_sums(keys, _run_carried("grad_pair_exchange_last", _pair_exchange_copies([self.mine[k] for k in keys])))

    def finish(self):
        rest = tuple(k for k in self.full if k not in self.other)
        self.other.update(zip(rest, _run_carried("grad_pair_share_last", _pair_share_copies([self.full[k] for k in rest]))))
        return {k: (self.full[k], self.other[k]) for k in self.full}


def kernel(x, norm_mix_pre, norm_mix_post, norm_ffn_pre, norm_ffn_post, na_w_qkv, na_w_o, na_rpb, dil_w_qkv, dil_w_o, ffn_w_gate, ffn_w_up, ffn_w_down, loss_target, m_norm_mix_pre, m_norm_mix_post, m_norm_ffn_pre, m_norm_ffn_post, m_na_w_qkv, m_na_w_o, m_na_rpb, m_dil_w_qkv, m_dil_w_o, m_ffn_w_gate, m_ffn_w_up, m_ffn_w_down, v_norm_mix_pre, v_norm_mix_post, v_norm_ffn_pre, v_norm_ffn_post, v_na_w_qkv, v_na_w_o, v_na_rpb, v_dil_w_qkv, v_dil_w_o, v_ffn_w_gate, v_ffn_w_up, v_ffn_w_down):
    tr = lambda a: jnp.swapaxes(a, 1, 2)
    weights = {"na_w_qkv": na_w_qkv, "na_w_o": na_w_o, "dil_w_qkv": dil_w_qkv, "dil_w_o": dil_w_o,
               "ffn_w_gate": tr(ffn_w_gate), "ffn_w_up": tr(ffn_w_up), "ffn_w_down": ffn_w_down}
    m_in = {"na_w_qkv": m_na_w_qkv, "na_w_o": m_na_w_o, "dil_w_qkv": m_dil_w_qkv, "dil_w_o": m_dil_w_o,
            "ffn_w_gate": tr(m_ffn_w_gate), "ffn_w_up": tr(m_ffn_w_up), "ffn_w_down": m_ffn_w_down}
    v_in = {"na_w_qkv": v_na_w_qkv, "na_w_o": v_na_w_o, "dil_w_qkv": v_dil_w_qkv, "dil_w_o": v_dil_w_o,
            "ffn_w_gate": tr(v_ffn_w_gate), "ffn_w_up": tr(v_ffn_w_up), "ffn_w_down": v_ffn_w_down}

    ex = _Exchange({(n, l): weights[n][l] for n in weights for l in range(weights[n].shape[0])})
    norms = (norm_mix_pre, norm_mix_post, norm_ffn_pre, norm_ffn_post)
    loss_row, dx, dnorms, d_rpb = _local_step(x[0], loss_target[0], norms, na_rpb[0], ex)
    small, sent = _allreduce_small(_pack_small(dnorms, d_rpb, last=loss_row[0, 0]), ex.carry("allreduce_small"))
    ex.carried("allreduce_small", sent)
    full = ex.finish()
    loss = small[SMALL_ROWS - 1, 127]

    out_g, out_d, out_m, out_v = {}, {}, {}, {}
    operands = lambda n, l: (weights[n], *full[(n, l)], m_in[n], v_in[n])
    results = {n: _adamw(f"adamw_{n}", ex.place, [operands(n, 0)])[0] for n in weights if n not in FFN_NAMES}
    ffn = None
    for l in range(2):
        ffn = _adamw(f"adamw_ffn_{l}", ex.place, [operands(n, l) for n in FFN_NAMES], l, ffn)
    results.update(zip(FFN_NAMES, ffn))
    for n, res in results.items():
        if n in ("ffn_w_gate", "ffn_w_up"):
            res = [tr(r) for r in res]
        out_g[n], out_d[n], out_m[n], out_v[n] = res
    sm_names = ("norm_mix_pre", "norm_mix_post", "norm_ffn_pre", "norm_ffn_post", "na_rpb")
    sm = _adamw("adamw_small", jnp.zeros((2,), jnp.int32),
                [(_pack_small(norms, na_rpb)[None], small[:SMALL_ROWS // 2], small[SMALL_ROWS // 2:],
                  _pack_small((m_norm_mix_pre, m_norm_mix_post, m_norm_ffn_pre, m_norm_ffn_post), m_na_rpb)[None],
                  _pack_small((v_norm_mix_pre, v_norm_mix_post, v_norm_ffn_pre, v_norm_ffn_post), v_na_rpb)[None])])[0]
    for res, dst in zip(sm, (out_g, out_d, out_m, out_v)):
        ns, rp = _unpack_small(res)
        for n, a in zip(sm_names, ns + [rp]):
            dst[n] = a

    order = ("norm_mix_pre", "norm_mix_post", "norm_ffn_pre", "norm_ffn_post", "na_w_qkv", "na_w_o", "na_rpb", "dil_w_qkv", "dil_w_o",
             "ffn_w_gate", "ffn_w_up", "ffn_w_down")
    return (loss, dx[None], *[out_g[n] for n in order], *[out_d[n] for n in order], *[out_m[n] for n in order], *[out_v[n] for n in order])
```
